```python
import math
import jax, jax.numpy as jnp
from jax import lax
import numpy as np

D_MODEL = 1024
BATCH = 8
SEQ = 8192
DEPTH = 2

N_A_LAYERS = DEPTH // 2
N_B_LAYERS = DEPTH - N_A_LAYERS

EXPAND = 2
A_WIDTH = EXPAND * D_MODEL
A_GROUPS = 8
A_GROUP_DIM = A_WIDTH // A_GROUPS
CHUNK = 128

HEAD_DIM = 64
N_Q_HEADS = D_MODEL // HEAD_DIM
N_KV_HEADS = max(1, N_Q_HEADS // 8)
Q_PER_KV = N_Q_HEADS // N_KV_HEADS
B_WIDTH = N_Q_HEADS * HEAD_DIM
KV_WIDTH = N_KV_HEADS * HEAD_DIM
WINDOW = 128

REL_BUCKETS = 32
REL_MAX_DIST = 128

ALPHA = (2.0 * DEPTH) ** 0.25
BETA = (8.0 * DEPTH) ** -0.25
LN_EPS = 1e-5
NEG_INF = -1e30

kernel_name = "yoco_gmlp_swa_sink_hybrid"


def layer_norm(x, g, b):
    xf = x.astype(jnp.float32)
    mu = jnp.mean(xf, axis=-1, keepdims=True)
    var = jnp.mean(jnp.square(xf - mu), axis=-1, keepdims=True)
    y = (xf - mu) * lax.rsqrt(var + LN_EPS)
    return (y * g.astype(jnp.float32) + b.astype(jnp.float32)).astype(x.dtype)


def rel_bucket(d):
    max_exact = REL_BUCKETS // 2
    df = jnp.maximum(d, 1).astype(jnp.float32)
    large = max_exact + (jnp.log(df / max_exact) / math.log(REL_MAX_DIST / max_exact)
                         * (REL_BUCKETS - max_exact)).astype(jnp.int32)
    large = jnp.minimum(large, REL_BUCKETS - 1)
    return jnp.where(d < max_exact, d, large)


def sgu_branch(h, w_in, ln_g, ln_b, w_spatial, b_spatial, w_out):
    bsz, seq, _ = h.shape
    nc = seq // CHUNK
    u, v, z = jnp.split(h @ w_in, 3, axis=-1)
    v = layer_norm(v, ln_g, ln_b).reshape(bsz, nc, CHUNK, A_GROUPS, A_GROUP_DIM)
    tri = jnp.tril(jnp.ones((CHUNK, CHUNK), dtype=bool))
    ws = jnp.where(tri, w_spatial, jnp.zeros((), w_spatial.dtype))
    s = jnp.einsum("gts,bcsgd->bctgd", ws, v) + b_spatial.T[:, :, None]
    y = u * s.reshape(bsz, seq, A_WIDTH) * jax.nn.silu(z)
    return y @ w_out


def shared_kv_bands(h, w_kv):
    bsz, seq, _ = h.shape
    nc = seq // CHUNK
    k, v = jnp.split(h @ w_kv, 2, axis=-1)

    def band(t):
        t = t.reshape(bsz, seq, N_KV_HEADS, HEAD_DIM)
        prev = jnp.pad(t, ((0, 0), (CHUNK, 0), (0, 0), (0, 0)))[:, :seq]
        prev = prev.reshape(bsz, nc, CHUNK, N_KV_HEADS, HEAD_DIM)
        cur = t.reshape(bsz, nc, CHUNK, N_KV_HEADS, HEAD_DIM)
        return jnp.concatenate([prev, cur], axis=2)

    return band(k), band(v)


def band_bias_and_mask(rel_bias, nc):
    t = jnp.arange(CHUNK, dtype=jnp.int32)[:, None]
    j = jnp.arange(2 * CHUNK, dtype=jnp.int32)[None, :]
    d = t + CHUNK - j
    in_window = (d >= 0) & (d < WINDOW)
    bias = rel_bias[rel_bucket(jnp.clip(d, 0, REL_MAX_DIST - 1))]
    bias = jnp.transpose(bias, (2, 0, 1)).astype(jnp.float32)
    bias = bias.reshape(N_KV_HEADS, Q_PER_KV, 1, CHUNK, 2 * CHUNK)
    has_prev = (jnp.arange(nc)[:, None, None] > 0) | (j[None] >= CHUNK)
    mask = in_window[None] & has_prev
    return bias, mask


def swa_branch(h, k_band, v_band, bias, mask, w_in, sinks, w_out):
    bsz, seq, _ = h.shape
    nc = seq // CHUNK
    q, z = jnp.split(h @ w_in, 2, axis=-1)
    q = q.reshape(bsz, nc, CHUNK, N_KV_HEADS, Q_PER_KV, HEAD_DIM)
    scores = jnp.einsum("bcqkgd,bcjkd->bkgcqj", q, k_band)
    logits = jnp.where(mask, scores.astype(jnp.float32) * (HEAD_DIM ** -0.5) + bias, NEG_INF)
    sink = sinks.astype(jnp.float32).reshape(N_KV_HEADS, Q_PER_KV, 1, 1, 1)
    m = jnp.maximum(jnp.max(logits, axis=-1, keepdims=True), sink)
    e = jnp.exp(logits - m)
    p = e / (jnp.sum(e, axis=-1, keepdims=True) + jnp.exp(sink - m))
    o = jnp.einsum("bkgcqj,bcjkd->bcqkgd", p.astype(v_band.dtype), v_band)
    y = o.reshape(bsz, seq, B_WIDTH) * jax.nn.silu(z)
    return y @ w_out


def _fwd_setup_inputs(seed: int = 0) -> dict:
    key = jax.random.key(seed)
    ks = jax.random.split(key, 16)
    f32 = jnp.float32
    nrm = lambda k, shape, s: jax.random.normal(k, shape, f32) * s
    return {
        "x": nrm(ks[0], (BATCH, SEQ, D_MODEL), 1.0),
        "w_in_a": nrm(ks[1], (N_A_LAYERS, D_MODEL, 3 * A_WIDTH), D_MODEL ** -0.5),
        "sgu_ln_g": 1.0 + nrm(ks[2], (N_A_LAYERS, A_WIDTH), 0.1),
        "sgu_ln_b": nrm(ks[3], (N_A_LAYERS, A_WIDTH), 0.1),
        "w_spatial": nrm(ks[4], (N_A_LAYERS, A_GROUPS, CHUNK, CHUNK), 0.1),
        "b_spatial": 1.0 + nrm(ks[5], (N_A_LAYERS, A_GROUPS, CHUNK), 0.1),
        "w_out_a": nrm(ks[6], (N_A_LAYERS, A_WIDTH, D_MODEL), BETA * A_WIDTH ** -0.5),
        "w_kv": nrm(ks[7], (D_MODEL, 2 * KV_WIDTH), D_MODEL ** -0.5),
        "w_in_b": nrm(ks[8], (N_B_LAYERS, D_MODEL, 2 * B_WIDTH), D_MODEL ** -0.5),
        "attn_sinks": nrm(ks[9], (N_B_LAYERS, N_Q_HEADS), 0.5),
        "rel_bias": nrm(ks[10], (REL_BUCKETS, N_Q_HEADS), 0.5),
        "w_out_b": nrm(ks[11], (N_B_LAYERS, B_WIDTH, D_MODEL), BETA * B_WIDTH ** -0.5),
        "post_ln_g": 1.0 + nrm(ks[12], (DEPTH, D_MODEL), 0.1),
        "post_ln_b": nrm(ks[13], (DEPTH, D_MODEL), 0.1),
    }


def _fwd_reference(x, w_in_a, sgu_ln_g, sgu_ln_b, w_spatial, b_spatial, w_out_a, w_kv,
              w_in_b, attn_sinks, rel_bias, w_out_b, post_ln_g, post_ln_b):
    nc = x.shape[1] // CHUNK
    bias, mask = band_bias_and_mask(rel_bias, nc)
    h = x
    k_band = None
    v_band = None
    for layer in range(DEPTH):
        if layer < N_A_LAYERS:
            i = layer
            sub = sgu_branch(h, w_in_a[i], sgu_ln_g[i], sgu_ln_b[i], w_spatial[i],
                             b_spatial[i], w_out_a[i])
        else:
            i = layer - N_A_LAYERS
            if i == 0:
                k_band, v_band = shared_kv_bands(h, w_kv)
            sub = swa_branch(h, k_band, v_band, bias, mask, w_in_b[i], attn_sinks[i], w_out_b[i])
        h = layer_norm(ALPHA * h + sub, post_ln_g[layer], post_ln_b[layer])
    return h


import jax as _jax
import jax.numpy as _jnp

TWIN_FORMAT = 'train_step'
FWD_PARAMS = ['x', 'w_in_a', 'sgu_ln_g', 'sgu_ln_b', 'w_spatial', 'b_spatial', 'w_out_a', 'w_kv', 'w_in_b', 'attn_sinks', 'rel_bias', 'w_out_b', 'post_ln_g', 'post_ln_b']
TWIN_WEIGHTS = ['w_in_a', 'sgu_ln_g', 'sgu_ln_b', 'w_spatial', 'b_spatial', 'w_out_a', 'w_kv', 'w_in_b', 'attn_sinks', 'rel_bias', 'w_out_b', 'post_ln_g', 'post_ln_b']
TWIN_DIFF_INPUT = 'x'
TWIN_INPUTS = ['x', 'w_in_a', 'sgu_ln_g', 'sgu_ln_b', 'w_spatial', 'b_spatial', 'w_out_a', 'w_kv', 'w_in_b', 'attn_sinks', 'rel_bias', 'w_out_b', 'post_ln_g', 'post_ln_b', 'loss_target', 'm_w_in_a', 'm_sgu_ln_g', 'm_sgu_ln_b', 'm_w_spatial', 'm_b_spatial', 'm_w_out_a', 'm_w_kv', 'm_w_in_b', 'm_attn_sinks', 'm_rel_bias', 'm_w_out_b', 'm_post_ln_g', 'm_post_ln_b', 'v_w_in_a', 'v_sgu_ln_g', 'v_sgu_ln_b', 'v_w_spatial', 'v_b_spatial', 'v_w_out_a', 'v_w_kv', 'v_w_in_b', 'v_attn_sinks', 'v_rel_bias', 'v_w_out_b', 'v_post_ln_g', 'v_post_ln_b']
TWIN_OUTPUTS = ['loss', 'grad_x', 'grad_w_in_a', 'grad_sgu_ln_g', 'grad_sgu_ln_b', 'grad_w_spatial', 'grad_b_spatial', 'grad_w_out_a', 'grad_w_kv', 'grad_w_in_b', 'grad_attn_sinks', 'grad_rel_bias', 'grad_w_out_b', 'grad_post_ln_g', 'grad_post_ln_b', 'delta_w_in_a', 'delta_sgu_ln_g', 'delta_sgu_ln_b', 'delta_w_spatial', 'delta_b_spatial', 'delta_w_out_a', 'delta_w_kv', 'delta_w_in_b', 'delta_attn_sinks', 'delta_rel_bias', 'delta_w_out_b', 'delta_post_ln_g', 'delta_post_ln_b', 'new_m_w_in_a', 'new_m_sgu_ln_g', 'new_m_sgu_ln_b', 'new_m_w_spatial', 'new_m_b_spatial', 'new_m_w_out_a', 'new_m_w_kv', 'new_m_w_in_b', 'new_m_attn_sinks', 'new_m_rel_bias', 'new_m_w_out_b', 'new_m_post_ln_g', 'new_m_post_ln_b', 'new_v_w_in_a', 'new_v_sgu_ln_g', 'new_v_sgu_ln_b', 'new_v_w_spatial', 'new_v_b_spatial', 'new_v_w_out_a', 'new_v_w_kv', 'new_v_w_in_b', 'new_v_attn_sinks', 'new_v_rel_bias', 'new_v_w_out_b', 'new_v_post_ln_g', 'new_v_post_ln_b']
TWIN_LEAF_KINDS = {'loss': 'loss', 'grad_x': 'grad_x', 'grad_w_in_a': 'grad_w', 'grad_sgu_ln_g': 'grad_w', 'grad_sgu_ln_b': 'grad_w', 'grad_w_spatial': 'grad_w', 'grad_b_spatial': 'grad_w', 'grad_w_out_a': 'grad_w', 'grad_w_kv': 'grad_w', 'grad_w_in_b': 'grad_w', 'grad_attn_sinks': 'grad_w', 'grad_rel_bias': 'grad_w', 'grad_w_out_b': 'grad_w', 'grad_post_ln_g': 'grad_w', 'grad_post_ln_b': 'grad_w', 'delta_w_in_a': 'delta_w', 'delta_sgu_ln_g': 'delta_w', 'delta_sgu_ln_b': 'delta_w', 'delta_w_spatial': 'delta_w', 'delta_b_spatial': 'delta_w', 'delta_w_out_a': 'delta_w', 'delta_w_kv': 'delta_w', 'delta_w_in_b': 'delta_w', 'delta_attn_sinks': 'delta_w', 'delta_rel_bias': 'delta_w', 'delta_w_out_b': 'delta_w', 'delta_post_ln_g': 'delta_w', 'delta_post_ln_b': 'delta_w', 'new_m_w_in_a': 'new_m', 'new_m_sgu_ln_g': 'new_m', 'new_m_sgu_ln_b': 'new_m', 'new_m_w_spatial': 'new_m', 'new_m_b_spatial': 'new_m', 'new_m_w_out_a': 'new_m', 'new_m_w_kv': 'new_m', 'new_m_w_in_b': 'new_m', 'new_m_attn_sinks': 'new_m', 'new_m_rel_bias': 'new_m', 'new_m_w_out_b': 'new_m', 'new_m_post_ln_g': 'new_m', 'new_m_post_ln_b': 'new_m', 'new_v_w_in_a': 'new_v', 'new_v_sgu_ln_g': 'new_v', 'new_v_sgu_ln_b': 'new_v', 'new_v_w_spatial': 'new_v', 'new_v_b_spatial': 'new_v', 'new_v_w_out_a': 'new_v', 'new_v_w_kv': 'new_v', 'new_v_w_in_b': 'new_v', 'new_v_attn_sinks': 'new_v', 'new_v_rel_bias': 'new_v', 'new_v_w_out_b': 'new_v', 'new_v_post_ln_g': 'new_v', 'new_v_post_ln_b': 'new_v'}


def _forward(args):
    return _fwd_reference(*[args[k] for k in FWD_PARAMS])


def _output_shape():
    def fwd():
        inp = _fwd_setup_inputs(0)
        return _fwd_reference(*[inp[k] for k in FWD_PARAMS])
    out = _jax.eval_shape(fwd)
    return out.shape, out.dtype

N_MICROBATCH = 1
ADAM_LR = 0.001
ADAM_B1 = 0.9
ADAM_B2 = 0.999
ADAM_EPS = 1e-08
ADAM_WD = 0.01
ADAM_STEP = 10
PER_EXAMPLE_BATCH_AXIS = {'x': 0, 'loss_target': 0}
SHARED_INPUTS = []
_WEIGHT_DTYPES = {'w_in_a': _jnp.float32, 'sgu_ln_g': _jnp.float32, 'sgu_ln_b': _jnp.float32, 'w_spatial': _jnp.float32, 'b_spatial': _jnp.float32, 'w_out_a': _jnp.float32, 'w_kv': _jnp.float32, 'w_in_b': _jnp.float32, 'attn_sinks': _jnp.float32, 'rel_bias': _jnp.float32, 'w_out_b': _jnp.float32, 'post_ln_g': _jnp.float32, 'post_ln_b': _jnp.float32}
MOMENT_SCALE = {'w_in_a': 4.474777e-02, 'sgu_ln_g': 2.962687e-02, 'sgu_ln_b': 2.975867e-02, 'w_spatial': 3.743288e-02, 'b_spatial': 5.265106e-02, 'w_out_a': 1.437036e-01, 'w_kv': 6.204473e-02, 'w_in_b': 1.271944e-02, 'attn_sinks': 8.968468e-03, 'rel_bias': 1.391726e-02, 'w_out_b': 6.938915e-02, 'post_ln_g': 4.713207e+01, 'post_ln_b': 7.830972e+00}


def _to_microbatches(a, axis):
    t = _jnp.moveaxis(a, axis, 0)
    t = t.reshape((N_MICROBATCH, t.shape[0] // N_MICROBATCH) + t.shape[1:])
    return _jnp.moveaxis(t, 1, axis + 1)


def setup_inputs(seed: int = 0) -> dict:
    inp = _fwd_setup_inputs(seed)
    key = _jax.random.fold_in(_jax.random.key(seed), 7919)
    shape, _ = _output_shape()
    out = dict(inp)
    out["loss_target"] = _jax.random.normal(_jax.random.fold_in(key, 0), shape, _jnp.float32)
    for i, name in enumerate(TWIN_WEIGHTS):
        w = inp[name].astype(_jnp.float32)
        if MOMENT_SCALE is None:
            s = _jnp.sqrt(_jnp.mean(_jnp.square(w)) + 1e-30)
        else:
            s = MOMENT_SCALE[name]
        km, kv = _jax.random.split(_jax.random.fold_in(key, i + 1))
        out[name] = w
        out["m_" + name] = s * _jax.random.normal(km, w.shape, _jnp.float32)
        out["v_" + name] = (s * s) * _jax.random.uniform(kv, w.shape, _jnp.float32, 0.5, 1.5)
    if N_MICROBATCH > 1:
        for name, axis in PER_EXAMPLE_BATCH_AXIS.items():
            out[name] = _to_microbatches(out[name], axis)
    return {'x': out['x'], 'w_in_a': out['w_in_a'], 'sgu_ln_g': out['sgu_ln_g'], 'sgu_ln_b': out['sgu_ln_b'], 'w_spatial': out['w_spatial'], 'b_spatial': out['b_spatial'], 'w_out_a': out['w_out_a'], 'w_kv': out['w_kv'], 'w_in_b': out['w_in_b'], 'attn_sinks': out['attn_sinks'], 'rel_bias': out['rel_bias'], 'w_out_b': out['w_out_b'], 'post_ln_g': out['post_ln_g'], 'post_ln_b': out['post_ln_b'], 'loss_target': out['loss_target'], 'm_w_in_a': out['m_w_in_a'], 'm_sgu_ln_g': out['m_sgu_ln_g'], 'm_sgu_ln_b': out['m_sgu_ln_b'], 'm_w_spatial': out['m_w_spatial'], 'm_b_spatial': out['m_b_spatial'], 'm_w_out_a': out['m_w_out_a'], 'm_w_kv': out['m_w_kv'], 'm_w_in_b': out['m_w_in_b'], 'm_attn_sinks': out['m_attn_sinks'], 'm_rel_bias': out['m_rel_bias'], 'm_w_out_b': out['m_w_out_b'], 'm_post_ln_g': out['m_post_ln_g'], 'm_post_ln_b': out['m_post_ln_b'], 'v_w_in_a': out['v_w_in_a'], 'v_sgu_ln_g': out['v_sgu_ln_g'], 'v_sgu_ln_b': out['v_sgu_ln_b'], 'v_w_spatial': out['v_w_spatial'], 'v_b_spatial': out['v_b_spatial'], 'v_w_out_a': out['v_w_out_a'], 'v_w_kv': out['v_w_kv'], 'v_w_in_b': out['v_w_in_b'], 'v_attn_sinks': out['v_attn_sinks'], 'v_rel_bias': out['v_rel_bias'], 'v_w_out_b': out['v_w_out_b'], 'v_post_ln_g': out['v_post_ln_g'], 'v_post_ln_b': out['v_post_ln_b']}


def _loss(weights, diff, rest, loss_target):
    with _jax.named_scope("forward"):
        args = {**rest, TWIN_DIFF_INPUT: diff, **{k: w.astype(_WEIGHT_DTYPES[k]) for k, w in weights.items()}}
        y = _forward(args)
    with _jax.named_scope("loss_head"):
        err = _jnp.square(y.astype(_jnp.float32) - loss_target)
        return 0.5 * _jnp.sum(_jnp.mean(err, axis=-1)) if err.ndim else 0.5 * err


def _adamw(w, g, m, v):
    m = ADAM_B1 * m + (1.0 - ADAM_B1) * g
    v = ADAM_B2 * v + (1.0 - ADAM_B2) * _jnp.square(g)
    m_hat = m / (1.0 - ADAM_B1 ** ADAM_STEP)
    v_hat = v / (1.0 - ADAM_B2 ** ADAM_STEP)
    delta = -ADAM_LR * (m_hat / (_jnp.sqrt(v_hat) + ADAM_EPS) + ADAM_WD * w)
    return delta, m, v


def reference(x, w_in_a, sgu_ln_g, sgu_ln_b, w_spatial, b_spatial, w_out_a, w_kv, w_in_b, attn_sinks, rel_bias, w_out_b, post_ln_g, post_ln_b, loss_target, m_w_in_a, m_sgu_ln_g, m_sgu_ln_b, m_w_spatial, m_b_spatial, m_w_out_a, m_w_kv, m_w_in_b, m_attn_sinks, m_rel_bias, m_w_out_b, m_post_ln_g, m_post_ln_b, v_w_in_a, v_sgu_ln_g, v_sgu_ln_b, v_w_spatial, v_b_spatial, v_w_out_a, v_w_kv, v_w_in_b, v_attn_sinks, v_rel_bias, v_w_out_b, v_post_ln_g, v_post_ln_b):
    given = dict(x=x, w_in_a=w_in_a, sgu_ln_g=sgu_ln_g, sgu_ln_b=sgu_ln_b, w_spatial=w_spatial, b_spatial=b_spatial, w_out_a=w_out_a, w_kv=w_kv, w_in_b=w_in_b, attn_sinks=attn_sinks, rel_bias=rel_bias, w_out_b=w_out_b, post_ln_g=post_ln_g, post_ln_b=post_ln_b, loss_target=loss_target, m_w_in_a=m_w_in_a, m_sgu_ln_g=m_sgu_ln_g, m_sgu_ln_b=m_sgu_ln_b, m_w_spatial=m_w_spatial, m_b_spatial=m_b_spatial, m_w_out_a=m_w_out_a, m_w_kv=m_w_kv, m_w_in_b=m_w_in_b, m_attn_sinks=m_attn_sinks, m_rel_bias=m_rel_bias, m_w_out_b=m_w_out_b, m_post_ln_g=m_post_ln_g, m_post_ln_b=m_post_ln_b, v_w_in_a=v_w_in_a, v_sgu_ln_g=v_sgu_ln_g, v_sgu_ln_b=v_sgu_ln_b, v_w_spatial=v_w_spatial, v_b_spatial=v_b_spatial, v_w_out_a=v_w_out_a, v_w_kv=v_w_kv, v_w_in_b=v_w_in_b, v_attn_sinks=v_attn_sinks, v_rel_bias=v_rel_bias, v_w_out_b=v_w_out_b, v_post_ln_g=v_post_ln_g, v_post_ln_b=v_post_ln_b)
    weights = {n: given[n] for n in TWIN_WEIGHTS}
    shared = {n: given[n] for n in SHARED_INPUTS}
    per_example = {n: given[n] for n in ['x']}
    grad_fn = _jax.value_and_grad(_loss, argnums=(0, 1))

    def one_microbatch(ex, loss_target):
        ex = dict(ex)
        diff = ex.pop(TWIN_DIFF_INPUT)
        return grad_fn(weights, diff, {**shared, **ex}, loss_target)

    if N_MICROBATCH == 1:
        loss, (grad_w, grad_x) = one_microbatch(per_example, given["loss_target"])
    else:
        def body(carry, xs):
            loss_sum, grad_sum = carry
            l_k, (gw_k, gx_k) = one_microbatch(xs[0], xs[1])
            with _jax.named_scope("update"):
                return (loss_sum + l_k, _jax.tree.map(_jnp.add, grad_sum, gw_k)), gx_k

        init = (_jnp.zeros((), _jnp.float32), _jax.tree.map(_jnp.zeros_like, weights))
        (loss, grad_w), grad_x = _jax.lax.scan(body, init, (per_example, given["loss_target"]))
    with _jax.named_scope("update"):
        delta_w, new_m, new_v = {}, {}, {}
        for n in TWIN_WEIGHTS:
            delta_w[n], new_m[n], new_v[n] = _adamw(weights[n], grad_w[n], given["m_" + n], given["v_" + n])
    return (loss, grad_x, *[grad_w[n] for n in TWIN_WEIGHTS], *[delta_w[n] for n in TWIN_WEIGHTS],
            *[new_m[n] for n in TWIN_WEIGHTS], *[new_v[n] for n in TWIN_WEIGHTS])
```

```python
import math

import jax
import jax.numpy as jnp
from jax import lax
from jax.experimental import pallas as pl
from jax.experimental.pallas import tpu as pltpu

F32 = jnp.float32
BF16 = jnp.bfloat16

D_MODEL = 1024
A_WIDTH = 2048
A_GROUPS = 8
A_GROUP_DIM = 256
CHUNK = 128
N_HEADS = 16
N_KV = 2
HEAD_DIM = 64
PAIR = 2 * HEAD_DIM
B_WIDTH = 1024
REL_BUCKETS = 32
ALPHA = 4.0 ** 0.25
LN_EPS = 1e-5
NEG_INF = -1e30
SCALE = HEAD_DIM ** -0.5

ADAM_LR = 0.001
ADAM_B1 = 0.9
ADAM_B2 = 0.999
ADAM_EPS = 1e-08
ADAM_WD = 0.01
ADAM_STEP = 10

N_DEV = 8
N_CHIPS = 4
MESH = pl.DeviceIdType.MESH
VMEM_LIMIT = 56 * 1024 * 1024

TM_ATTN = 256
TM_MM = 512


def _dot(a, b):
    return jnp.dot(a, b, preferred_element_type=F32)


def _dot_nt(a, b):
    return lax.dot_general(a, b, (((1,), (1,)), ((), ())), preferred_element_type=F32)


def _dot_tn(a, b):
    return lax.dot_general(a, b, (((0,), (0,)), ((), ())), preferred_element_type=F32)


def _ln_fwd(r):
    mu = jnp.mean(r, axis=-1, keepdims=True)
    rc = r - mu
    var = jnp.mean(rc * rc, axis=-1, keepdims=True)
    rstd = lax.rsqrt(var + LN_EPS)
    return rc * rstd, rstd


def _ln_bwd(dxh, xh, rstd):
    m1 = jnp.mean(dxh, axis=-1, keepdims=True)
    m2 = jnp.mean(dxh * xh, axis=-1, keepdims=True)
    return rstd * (dxh - m1 - xh * m2)


def _silu_parts(z):
    sg = jax.nn.sigmoid(z)
    return z * sg, sg * (1.0 + z * (1.0 - sg))


def _dup_halves(blk):
    sw = pltpu.roll(blk, HEAD_DIM, 1)
    lo = lax.broadcasted_iota(jnp.int32, blk.shape, 1) < HEAD_DIM
    return jnp.where(lo, blk, sw), jnp.where(lo, sw, blk)


def _fold_halves(blk):
    return blk + pltpu.roll(blk, HEAD_DIM, 1)


def _resident(shape):
    nd = len(shape)
    return pl.BlockSpec(shape, lambda *_: (0,) * nd, pipeline_mode=pl.Buffered(1))


def _const(shape):
    nd = len(shape)
    return pl.BlockSpec(shape, lambda *_: (0,) * nd)


def _rows(tm, cols):
    return pl.BlockSpec((tm, cols), lambda i: (i, 0))


def _params(sem=("arbitrary",)):
    return pltpu.CompilerParams(dimension_semantics=sem, vmem_limit_bytes=VMEM_LIMIT)


def _spatial_mix(ws_ref, bsp_ref, vn, s_scr, n_chunks):
    tri = (lax.broadcasted_iota(jnp.int32, (CHUNK, CHUNK), 0)
           >= lax.broadcasted_iota(jnp.int32, (CHUNK, CHUNK), 1))
    for g in range(A_GROUPS):
        wsg = jnp.where(tri, ws_ref[g], 0.0).astype(BF16)
        cols = slice(g * A_GROUP_DIM, (g + 1) * A_GROUP_DIM)
        for ci in range(n_chunks):
            rows = slice(ci * CHUNK, (ci + 1) * CHUNK)
            s_scr[rows, cols] = _dot(wsg, vn[rows, cols]) + bsp_ref[:, g:g + 1]


def _layer_a_fwd(x, w_in, w_out, lng, lnb, ws, bsp_t):
    t_len = x.shape[0]
    tm = TM_ATTN

    def body(x_ref, win_ref, wout_ref, lng_ref, lnb_ref, ws_ref, bsp_ref,
             xb_ref, u_ref, vh_ref, z_ref, rv_ref, xh_ref, r1_ref, s_scr):
        x_t = x_ref[...]
        xb = x_t.astype(BF16)
        xb_ref[...] = xb
        u = _dot(xb, win_ref[:, 0:A_WIDTH])
        v = _dot(xb, win_ref[:, A_WIDTH:2 * A_WIDTH])
        z = _dot(xb, win_ref[:, 2 * A_WIDTH:3 * A_WIDTH])
        vh, rv = _ln_fwd(v)
        vn = (vh * lng_ref[...] + lnb_ref[...]).astype(BF16)
        _spatial_mix(ws_ref, bsp_ref, vn, s_scr, tm // CHUNK)
        sz, _ = _silu_parts(z)
        y = (u * s_scr[...] * sz).astype(BF16)
        r = ALPHA * x_t + _dot(y, wout_ref[...])
        xh, r1 = _ln_fwd(r)
        u_ref[...] = u.astype(BF16)
        vh_ref[...] = vh.astype(BF16)
        z_ref[...] = z.astype(BF16)
        rv_ref[...] = rv
        xh_ref[...] = xh
        r1_ref[...] = r1

    return pl.pallas_call(
        body, name="layer_a_fwd", grid=(t_len // tm,),
        in_specs=[_rows(tm, D_MODEL), _resident(w_in.shape), _resident(w_out.shape), _const(lng.shape),
                  _const(lnb.shape), _const(ws.shape), _const(bsp_t.shape)],
        out_specs=[_rows(tm, D_MODEL), _rows(tm, A_WIDTH), _rows(tm, A_WIDTH), _rows(tm, A_WIDTH),
                   _rows(tm, 1), _rows(tm, D_MODEL), _rows(tm, 1)],
        out_shape=[jax.ShapeDtypeStruct((t_len, D_MODEL), BF16), jax.ShapeDtypeStruct((t_len, A_WIDTH), BF16),
                   jax.ShapeDtypeStruct((t_len, A_WIDTH), BF16), jax.ShapeDtypeStruct((t_len, A_WIDTH), BF16),
                   jax.ShapeDtypeStruct((t_len, 1), F32), jax.ShapeDtypeStruct((t_len, D_MODEL), F32),
                   jax.ShapeDtypeStruct((t_len, 1), F32)],
        scratch_shapes=[pltpu.VMEM((tm, A_WIDTH), F32)],
        compiler_params=_params(),
    )(x, w_in, w_out, lng, lnb, ws, bsp_t)


def _layer_b_proj(xh1, g1, b1, w_in, w_kv):
    t_len = xh1.shape[0]
    tm = TM_MM

    def body(xh_ref, g_ref, b_ref, win_ref, wkv_ref, q_ref, z_ref, kd_ref, vd_ref):
        h1 = (xh_ref[...] * g_ref[...] + b_ref[...]).astype(BF16)
        q_ref[...] = _dot(h1, win_ref[:, 0:B_WIDTH]).astype(BF16)
        z_ref[...] = _dot(h1, win_ref[:, B_WIDTH:2 * B_WIDTH]).astype(BF16)
        kv = _dot(h1, wkv_ref[...])
        k0, k1 = _dup_halves(kv[:, 0:PAIR])
        v0, v1 = _dup_halves(kv[:, PAIR:2 * PAIR])
        kd_ref[:, 0:PAIR] = k0.astype(BF16)
        kd_ref[:, PAIR:2 * PAIR] = k1.astype(BF16)
        vd_ref[:, 0:PAIR] = v0.astype(BF16)
        vd_ref[:, PAIR:2 * PAIR] = v1.astype(BF16)

    return pl.pallas_call(
        body, name="layer_b_proj", grid=(t_len // tm,),
        in_specs=[_rows(tm, D_MODEL), _const(g1.shape), _const(b1.shape), _resident(w_in.shape),
                  _resident(w_kv.shape)],
        out_specs=[_rows(tm, B_WIDTH), _rows(tm, B_WIDTH), _rows(tm, 2 * PAIR), _rows(tm, 2 * PAIR)],
        out_shape=[jax.ShapeDtypeStruct((t_len, B_WIDTH), BF16), jax.ShapeDtypeStruct((t_len, B_WIDTH), BF16),
                   jax.ShapeDtypeStruct((t_len, 2 * PAIR), BF16), jax.ShapeDtypeStruct((t_len, 2 * PAIR), BF16)],
        compiler_params=_params(),
    )(xh1, g1, b1, w_in, w_kv)


def _band_mask(chunk_index):
    t = lax.broadcasted_iota(jnp.int32, (CHUNK, 2 * CHUNK), 0)
    j = lax.broadcasted_iota(jnp.int32, (CHUNK, 2 * CHUNK), 1)
    dist = t + CHUNK - j
    in_window = (dist >= 0) & (dist < CHUNK)
    return in_window & ((j >= CHUNK) | (chunk_index > 0))


def _band(ref, chunk_index, kvh):
    prev0 = pl.multiple_of(jnp.maximum(chunk_index - 1, 0) * CHUNK, CHUNK)
    cur0 = pl.multiple_of(chunk_index * CHUNK, CHUNK)
    cols = slice(kvh * PAIR, (kvh + 1) * PAIR)
    return jnp.concatenate([ref[pl.ds(prev0, CHUNK), cols], ref[pl.ds(cur0, CHUNK), cols]], axis=0), prev0, cur0


def _attn_probs(qm, kband, bias_h, mask, sink):
    logits = jnp.where(mask, _dot_nt(qm, kband) * SCALE + bias_h, NEG_INF)
    m = jnp.maximum(jnp.max(logits, axis=-1, keepdims=True), sink)
    e = jnp.exp(logits - m)
    es = jnp.exp(sink - m)
    inv = 1.0 / (jnp.sum(e, axis=-1, keepdims=True) + es)
    return e * inv, es * inv


def _half_mask():
    return lax.broadcasted_iota(jnp.int32, (CHUNK, PAIR), 1) < HEAD_DIM


def _keep_half(pair_block, lo, which):
    keep = lo if which == 0 else ~lo
    return jnp.where(keep, pair_block.astype(F32), 0.0).astype(BF16)


def _layer_b_fwd(q, zb, kd, vd, bias, sinks, xh1, g1, b1, w_out, g2, b2, tgt):
    t_len = q.shape[0]
    tm = TM_ATTN

    def body(q_ref, z_ref, kd_ref, vd_ref, bias_ref, sink_ref, xh_ref, g1_ref, b1_ref, wout_ref, g2_ref, b2_ref,
             tgt_ref, o_ref, dr_ref, loss_ref, dg_ref, db_ref, o_scr):
        i = pl.program_id(0)

        @pl.when(i == 0)
        def _():
            loss_ref[...] = jnp.zeros_like(loss_ref)
            dg_ref[...] = jnp.zeros_like(dg_ref)
            db_ref[...] = jnp.zeros_like(db_ref)

        lo = _half_mask()
        for ci in range(tm // CHUNK):
            cg = i * (tm // CHUNK) + ci
            rows = slice(ci * CHUNK, (ci + 1) * CHUNK)
            mask = _band_mask(cg)
            for kvh in range(N_KV):
                kband, _, _ = _band(kd_ref, cg, kvh)
                vband, _, _ = _band(vd_ref, cg, kvh)
                for pp in range(N_HEADS // N_KV // 2):
                    pair = kvh * (N_HEADS // N_KV // 2) + pp
                    cols = slice(pair * PAIR, (pair + 1) * PAIR)
                    qp = q_ref[rows, cols]
                    outs = []
                    for hh in range(2):
                        h = 2 * pair + hh
                        qm = _keep_half(qp, lo, hh)
                        p, _ = _attn_probs(qm, kband, bias_ref[h], mask, sink_ref[0, h])
                        outs.append(_dot(p.astype(BF16), vband))
                    o_scr[rows, cols] = jnp.where(lo, outs[0], outs[1])
        o = o_scr[...]
        o_ref[...] = o.astype(BF16)
        sz, _ = _silu_parts(z_ref[...].astype(F32))
        y = (o * sz).astype(BF16)
        h1 = xh_ref[...] * g1_ref[...] + b1_ref[...]
        r = ALPHA * h1 + _dot(y, wout_ref[...])
        xh2, rstd2 = _ln_fwd(r)
        diff = xh2 * g2_ref[...] + b2_ref[...] - tgt_ref[...]
        loss_ref[...] += jnp.sum(diff * diff, axis=0, keepdims=True)
        dh2 = diff * (1.0 / D_MODEL)
        dg_ref[...] += jnp.sum(dh2 * xh2, axis=0, keepdims=True)
        db_ref[...] += jnp.sum(dh2, axis=0, keepdims=True)
        dr_ref[...] = _ln_bwd(dh2 * g2_ref[...], xh2, rstd2)

    vec = jax.ShapeDtypeStruct((1, D_MODEL), F32)
    return pl.pallas_call(
        body, name="layer_b_fwd", grid=(t_len // tm,),
        in_specs=[_rows(tm, B_WIDTH), _rows(tm, B_WIDTH), _resident(kd.shape), _resident(vd.shape),
                  _resident(bias.shape), pl.BlockSpec(memory_space=pltpu.SMEM), _rows(tm, D_MODEL),
                  _const(g1.shape), _const(b1.shape), _resident(w_out.shape), _const(g2.shape), _const(b2.shape),
                  _rows(tm, D_MODEL)],
        out_specs=[_rows(tm, B_WIDTH), _rows(tm, D_MODEL), _const((1, D_MODEL)), _const((1, D_MODEL)),
                   _const((1, D_MODEL))],
        out_shape=[jax.ShapeDtypeStruct((t_len, B_WIDTH), BF16), jax.ShapeDtypeStruct((t_len, D_MODEL), F32),
                   vec, vec, vec],
        scratch_shapes=[pltpu.VMEM((tm, B_WIDTH), F32)],
        compiler_params=_params(),
    )(q, zb, kd, vd, bias, sinks, xh1, g1, b1, w_out, g2, b2, tgt)


def _layer_b_bwd_attn(dr2, zb, o, q, kd, vd, bias, sinks, w_out):
    t_len = q.shape[0]
    tm = TM_ATTN
    n_steps = t_len // tm

    def body(dr_ref, z_ref, o_ref, q_ref, kd_ref, vd_ref, bias_ref, sink_ref, wout_ref,
             dq_ref, dz_ref, dkd_ref, dvd_ref, gw_ref, dsink_ref, dbias_ref,
             do_scr, dkd_acc, dvd_acc, gw_acc, copy_sem):
        i = pl.program_id(0)

        @pl.when(i == 0)
        def _():
            dkd_acc[...] = jnp.zeros_like(dkd_acc)
            dvd_acc[...] = jnp.zeros_like(dvd_acc)
            gw_acc[...] = jnp.zeros_like(gw_acc)
            dsink_ref[...] = jnp.zeros_like(dsink_ref)
            dbias_ref[...] = jnp.zeros_like(dbias_ref)

        drb = dr_ref[...].astype(BF16)
        dy = _dot_nt(drb, wout_ref[...])
        z = z_ref[...].astype(F32)
        sz, dsz = _silu_parts(z)
        o_t = o_ref[...].astype(F32)
        dz_ref[...] = (dy * o_t * dsz).astype(BF16)
        do_scr[...] = (dy * sz).astype(BF16)
        gw_acc[...] += _dot_tn((o_t * sz).astype(BF16), drb)

        lo = _half_mask()
        for ci in range(tm // CHUNK):
            cg = i * (tm // CHUNK) + ci
            rows = slice(ci * CHUNK, (ci + 1) * CHUNK)
            mask = _band_mask(cg)
            for kvh in range(N_KV):
                kband, prev0, cur0 = _band(kd_ref, cg, kvh)
                vband, _, _ = _band(vd_ref, cg, kvh)
                kcols = slice(kvh * PAIR, (kvh + 1) * PAIR)
                dk_band = jnp.zeros((2 * CHUNK, PAIR), F32)
                dv_band = jnp.zeros((2 * CHUNK, PAIR), F32)
                for pp in range(N_HEADS // N_KV // 2):
                    pair = kvh * (N_HEADS // N_KV // 2) + pp
                    cols = slice(pair * PAIR, (pair + 1) * PAIR)
                    qp = q_ref[rows, cols]
                    dop = do_scr[rows, cols]
                    dqs = []
                    for hh in range(2):
                        h = 2 * pair + hh
                        qm = _keep_half(qp, lo, hh)
                        dom = _keep_half(dop, lo, hh)
                        p, p_sink = _attn_probs(qm, kband, bias_ref[h], mask, sink_ref[0, h])
                        dp = _dot_nt(dom, vband)
                        delta = jnp.sum(p * dp, axis=-1, keepdims=True)
                        dlog = p * (dp - delta)
                        dbias_ref[h] += dlog
                        dsink_ref[h:h + 1, :] += jnp.broadcast_to(
                            -jnp.sum(p_sink * delta, axis=0, keepdims=True), (1, CHUNK))
                        ds = (dlog * SCALE).astype(BF16)
                        dqs.append(_dot(ds, kband))
                        dk_band += _dot_tn(ds, qm)
                        dv_band += _dot_tn(p.astype(BF16), dom)
                    dq_ref[rows, cols] = jnp.where(lo, dqs[0], dqs[1]).astype(BF16)
                dkd_acc[pl.ds(prev0, CHUNK), kcols] += dk_band[0:CHUNK]
                dkd_acc[pl.ds(cur0, CHUNK), kcols] += dk_band[CHUNK:2 * CHUNK]
                dvd_acc[pl.ds(prev0, CHUNK), kcols] += dv_band[0:CHUNK]
                dvd_acc[pl.ds(cur0, CHUNK), kcols] += dv_band[CHUNK:2 * CHUNK]

        @pl.when(i == n_steps - 1)
        def _():
            gw_ref[...] = gw_acc[...].astype(BF16)
            c1 = pltpu.make_async_copy(dkd_acc, dkd_ref, copy_sem.at[0])
            c2 = pltpu.make_async_copy(dvd_acc, dvd_ref, copy_sem.at[1])
            c1.start()
            c2.start()
            c1.wait()
            c2.wait()

    any_spec = pl.BlockSpec(memory_space=pl.ANY)
    return pl.pallas_call(
        body, name="layer_b_bwd_attn", grid=(n_steps,),
        in_specs=[_rows(tm, D_MODEL), _rows(tm, B_WIDTH), _rows(tm, B_WIDTH), _rows(tm, B_WIDTH),
                  _resident(kd.shape), _resident(vd.shape), _resident(bias.shape),
                  pl.BlockSpec(memory_space=pltpu.SMEM), _resident(w_out.shape)],
        out_specs=[_rows(tm, B_WIDTH), _rows(tm, B_WIDTH), any_spec, any_spec, _const(w_out.shape),
                   _const((N_HEADS, CHUNK)), _const(bias.shape)],
        out_shape=[jax.ShapeDtypeStruct((t_len, B_WIDTH), BF16), jax.ShapeDtypeStruct((t_len, B_WIDTH), BF16),
                   jax.ShapeDtypeStruct((t_len, 2 * PAIR), F32), jax.ShapeDtypeStruct((t_len, 2 * PAIR), F32),
                   jax.ShapeDtypeStruct(w_out.shape, BF16), jax.ShapeDtypeStruct((N_HEADS, CHUNK), F32),
                   jax.ShapeDtypeStruct(bias.shape, F32)],
        scratch_shapes=[pltpu.VMEM((tm, B_WIDTH), BF16), pltpu.VMEM((t_len, 2 * PAIR), F32),
                        pltpu.VMEM((t_len, 2 * PAIR), F32), pltpu.VMEM(w_out.shape, F32),
                        pltpu.SemaphoreType.DMA((2,))],
        compiler_params=_params(),
    )(dr2, zb, o, q, kd, vd, bias, sinks, w_out)


def _layer_b_bwd_proj(xh1, rstd1, g1, b1, dr2, dq, dzb, dkd, dvd, w_in, w_kv):
    t_len = xh1.shape[0]
    tm = TM_MM
    n_steps = t_len // tm

    def body(xh_ref, rstd_ref, g_ref, b_ref, dr2_ref, dq_ref, dz_ref, dkd_ref, dvd_ref, win_ref, wkv_ref,
             dr1_ref, dg_ref, db_ref, gwin_ref, gwkv_ref, acc_in, acc_kv):
        i = pl.program_id(0)

        @pl.when(i == 0)
        def _():
            acc_in[...] = jnp.zeros_like(acc_in)
            acc_kv[...] = jnp.zeros_like(acc_kv)
            dg_ref[...] = jnp.zeros_like(dg_ref)
            db_ref[...] = jnp.zeros_like(db_ref)

        xh = xh_ref[...]
        h1 = (xh * g_ref[...] + b_ref[...]).astype(BF16)
        dq_t = dq_ref[...]
        dz_t = dz_ref[...]
        dkd_t = dkd_ref[...].astype(BF16)
        dvd_t = dvd_ref[...].astype(BF16)
        wkv = wkv_ref[...].astype(F32)
        wk0, wk1 = _dup_halves(wkv[:, 0:PAIR])
        wv0, wv1 = _dup_halves(wkv[:, PAIR:2 * PAIR])
        dh1 = ALPHA * dr2_ref[...]
        dh1 += _dot_nt(dq_t, win_ref[:, 0:B_WIDTH])
        dh1 += _dot_nt(dz_t, win_ref[:, B_WIDTH:2 * B_WIDTH])
        for blk, w in ((dkd_t[:, 0:PAIR], wk0), (dkd_t[:, PAIR:2 * PAIR], wk1),
                       (dvd_t[:, 0:PAIR], wv0), (dvd_t[:, PAIR:2 * PAIR], wv1)):
            dh1 += _dot_nt(blk, w.astype(BF16))
        acc_in[:, 0:B_WIDTH] += _dot_tn(h1, dq_t)
        acc_in[:, B_WIDTH:2 * B_WIDTH] += _dot_tn(h1, dz_t)
        acc_kv[:, 0:2 * PAIR] += _dot_tn(h1, dkd_t)
        acc_kv[:, 2 * PAIR:4 * PAIR] += _dot_tn(h1, dvd_t)
        dg_ref[...] += jnp.sum(dh1 * xh, axis=0, keepdims=True)
        db_ref[...] += jnp.sum(dh1, axis=0, keepdims=True)
        dr1_ref[...] = _ln_bwd(dh1 * g_ref[...], xh, rstd_ref[...])

        @pl.when(i == n_steps - 1)
        def _():
            half_rows = D_MODEL // 2
            shard_cols = 2 * B_WIDTH // N_CHIPS
            for s in range(N_CHIPS):
                for c in range(2):
                    gwin_ref[2 * s + c] = acc_in[c * half_rows:(c + 1) * half_rows,
                                                 s * shard_cols:(s + 1) * shard_cols].astype(BF16)
            lo = lax.broadcasted_iota(jnp.int32, (D_MODEL, PAIR), 1) < HEAD_DIM
            for n in range(2):
                f0 = _fold_halves(acc_kv[:, (2 * n) * PAIR:(2 * n + 1) * PAIR])
                f1 = _fold_halves(acc_kv[:, (2 * n + 1) * PAIR:(2 * n + 2) * PAIR])
                gwkv_ref[:, n * PAIR:(n + 1) * PAIR] = jnp.where(lo, f0, f1).astype(BF16)

    vec = jax.ShapeDtypeStruct((1, D_MODEL), F32)
    gwin_shape = (N_DEV, D_MODEL // 2, 2 * B_WIDTH // N_CHIPS)
    return pl.pallas_call(
        body, name="layer_b_bwd_proj", grid=(n_steps,),
        in_specs=[_rows(tm, D_MODEL), _rows(tm, 1), _const(g1.shape), _const(b1.shape), _rows(tm, D_MODEL),
                  _rows(tm, B_WIDTH), _rows(tm, B_WIDTH), _rows(tm, 2 * PAIR), _rows(tm, 2 * PAIR),
                  _resident(w_in.shape), _resident(w_kv.shape)],
        out_specs=[_rows(tm, D_MODEL), _const((1, D_MODEL)), _const((1, D_MODEL)), _const(gwin_shape),
                   _const(w_kv.shape)],
        out_shape=[jax.ShapeDtypeStruct((t_len, D_MODEL), F32), vec, vec,
                   jax.ShapeDtypeStruct(gwin_shape, BF16), jax.ShapeDtypeStruct(w_kv.shape, BF16)],
        scratch_shapes=[pltpu.VMEM(w_in.shape, F32), pltpu.VMEM((D_MODEL, 4 * PAIR), F32)],
        compiler_params=_params(),
    )(xh1, rstd1, g1, b1, dr2, dq, dzb, dkd, dvd, w_in, w_kv)


def _layer_a_bwd_mix(dr1, u, vh, z, rv, w_out, lng, lnb, ws, bsp_t):
    t_len = u.shape[0]
    tm = TM_ATTN
    n_steps = t_len // tm

    def body(dr_ref, u_ref, vh_ref, z_ref, rv_ref, wout_ref, lng_ref, lnb_ref, ws_ref, bsp_ref,
             dp_ref, gw_ref, dws_ref, dbsp_ref, dgs_ref, dbs_ref, s_scr, dvn_scr, gw_acc):
        i = pl.program_id(0)

        @pl.when(i == 0)
        def _():
            gw_acc[...] = jnp.zeros_like(gw_acc)
            dws_ref[...] = jnp.zeros_like(dws_ref)
            dbsp_ref[...] = jnp.zeros_like(dbsp_ref)
            dgs_ref[...] = jnp.zeros_like(dgs_ref)
            dbs_ref[...] = jnp.zeros_like(dbs_ref)

        drb = dr_ref[...].astype(BF16)
        dy = _dot_nt(drb, wout_ref[...])
        u_t = u_ref[...].astype(F32)
        vh_t = vh_ref[...].astype(F32)
        z_t = z_ref[...].astype(F32)
        vn = (vh_t * lng_ref[...] + lnb_ref[...]).astype(BF16)
        _spatial_mix(ws_ref, bsp_ref, vn, s_scr, tm // CHUNK)
        s = s_scr[...]
        sz, dsz = _silu_parts(z_t)
        gw_acc[...] += _dot_tn((u_t * s * sz).astype(BF16), drb)
        dp_ref[:, 0:A_WIDTH] = (dy * s * sz).astype(BF16)
        dp_ref[:, 2 * A_WIDTH:3 * A_WIDTH] = (dy * u_t * s * dsz).astype(BF16)
        ds = dy * u_t * sz

        tri = (lax.broadcasted_iota(jnp.int32, (CHUNK, CHUNK), 0)
               >= lax.broadcasted_iota(jnp.int32, (CHUNK, CHUNK), 1))
        for g in range(A_GROUPS):
            wsg = jnp.where(tri, ws_ref[g], 0.0).astype(BF16)
            cols = slice(g * A_GROUP_DIM, (g + 1) * A_GROUP_DIM)
            dws_g = jnp.zeros((CHUNK, CHUNK), F32)
            dbsp_g = jnp.zeros((CHUNK, 1), F32)
            for ci in range(tm // CHUNK):
                rows = slice(ci * CHUNK, (ci + 1) * CHUNK)
                ds_g = ds[rows, cols]
                ds_b = ds_g.astype(BF16)
                dws_g += _dot_nt(ds_b, vn[rows, cols])
                dbsp_g += jnp.sum(ds_g, axis=-1, keepdims=True)
                dvn_scr[rows, cols] = _dot_tn(wsg, ds_b)
            dws_ref[g] += jnp.where(tri, dws_g, 0.0)
            dbsp_ref[g] += dbsp_g
        dvn = dvn_scr[...]
        dgs_ref[...] += jnp.sum(dvn * vh_t, axis=0, keepdims=True)
        dbs_ref[...] += jnp.sum(dvn, axis=0, keepdims=True)
        dp_ref[:, A_WIDTH:2 * A_WIDTH] = _ln_bwd(dvn * lng_ref[...], vh_t, rv_ref[...]).astype(BF16)

        @pl.when(i == n_steps - 1)
        def _():
            gw_ref[...] = gw_acc[...].astype(BF16)

    wide = jax.ShapeDtypeStruct((1, A_WIDTH), F32)
    return pl.pallas_call(
        body, name="layer_a_bwd_mix", grid=(n_steps,),
        in_specs=[_rows(tm, D_MODEL), _rows(tm, A_WIDTH), _rows(tm, A_WIDTH), _rows(tm, A_WIDTH), _rows(tm, 1),
                  _resident(w_out.shape), _const(lng.shape), _const(lnb.shape), _const(ws.shape),
                  _const(bsp_t.shape)],
        out_specs=[_rows(tm, 3 * A_WIDTH), _const(w_out.shape), _const(ws.shape), _const((A_GROUPS, CHUNK, 1)),
                   _const((1, A_WIDTH)), _const((1, A_WIDTH))],
        out_shape=[jax.ShapeDtypeStruct((t_len, 3 * A_WIDTH), BF16), jax.ShapeDtypeStruct(w_out.shape, BF16),
                   jax.ShapeDtypeStruct(ws.shape, F32), jax.ShapeDtypeStruct((A_GROUPS, CHUNK, 1), F32),
                   wide, wide],
        scratch_shapes=[pltpu.VMEM((tm, A_WIDTH), F32), pltpu.VMEM((tm, A_WIDTH), F32),
                        pltpu.VMEM(w_out.shape, F32)],
        compiler_params=_params(),
    )(dr1, u, vh, z, rv, w_out, lng, lnb, ws, bsp_t)


def _layer_a_bwd_dx(dr1, dp, w_in):
    t_len = dr1.shape[0]
    tm = TM_MM

    def body(dr_ref, dp_ref, win_ref, dx_ref):
        dx_ref[...] = ALPHA * dr_ref[...] + _dot_nt(dp_ref[...], win_ref[...])

    return pl.pallas_call(
        body, name="layer_a_bwd_dx", grid=(t_len // tm,),
        in_specs=[_rows(tm, D_MODEL), _rows(tm, 3 * A_WIDTH), _resident(w_in.shape)],
        out_specs=_rows(tm, D_MODEL),
        out_shape=jax.ShapeDtypeStruct((t_len, D_MODEL), F32),
        compiler_params=_params(),
    )(dr1, dp, w_in)


def _layer_a_bwd_win(xb, dp):
    t_len = xb.shape[0]
    tm = TM_MM
    n_steps = t_len // tm
    shard_cols = 3 * A_WIDTH // N_CHIPS
    half_rows = D_MODEL // 2

    def body(xb_ref, dp_ref, gw_ref, acc):
        i = pl.program_id(1)

        @pl.when(i == 0)
        def _():
            acc[...] = jnp.zeros_like(acc)

        acc[...] += _dot_tn(xb_ref[...], dp_ref[...])

        @pl.when(i == n_steps - 1)
        def _():
            for c in range(2):
                gw_ref[0, c] = acc[c * half_rows:(c + 1) * half_rows, :].astype(BF16)

    return pl.pallas_call(
        body, name="layer_a_bwd_win", grid=(N_CHIPS, n_steps),
        in_specs=[pl.BlockSpec((tm, D_MODEL), lambda j, i: (i, 0)),
                  pl.BlockSpec((tm, shard_cols), lambda j, i: (i, j))],
        out_specs=pl.BlockSpec((1, 2, half_rows, shard_cols), lambda j, i: (j, 0, 0, 0)),
        out_shape=jax.ShapeDtypeStruct((N_CHIPS, 2, half_rows, shard_cols), BF16),
        scratch_shapes=[pltpu.VMEM((D_MODEL, shard_cols), F32)],
        compiler_params=_params(("arbitrary", "arbitrary")),
    )(xb, dp)


def _bucket_onehot():
    t = jnp.arange(CHUNK, dtype=jnp.int32)[:, None]
    j = jnp.arange(2 * CHUNK, dtype=jnp.int32)[None, :]
    dist = jnp.clip(t + CHUNK - j, 0, CHUNK - 1)
    max_exact = REL_BUCKETS // 2
    df = jnp.maximum(dist, 1).astype(F32)
    large = max_exact + (jnp.log(df / max_exact) / math.log(CHUNK / max_exact)
                         * (REL_BUCKETS - max_exact)).astype(jnp.int32)
    bucket = jnp.where(dist < max_exact, dist, jnp.minimum(large, REL_BUCKETS - 1))
    onehot = bucket.reshape(1, -1) == jnp.arange(REL_BUCKETS, dtype=jnp.int32)[:, None]
    return onehot.astype(F32)


def _bias_expand(rel_t, onehot):
    def body(rel_ref, oh_ref, out_ref):
        out_ref[...] = jnp.dot(rel_ref[...], oh_ref[...], preferred_element_type=F32,
                               precision=lax.Precision.HIGHEST)

    return pl.pallas_call(
        body, name="bias_expand",
        out_shape=jax.ShapeDtypeStruct((N_HEADS, onehot.shape[1]), F32),
    )(rel_t, onehot)


def _bias_reduce(onehot, dbias):
    def body(oh_ref, db_ref, out_ref):
        out_ref[...] = lax.dot_general(oh_ref[...], db_ref[...], (((1,), (1,)), ((), ())),
                                       preferred_element_type=F32, precision=lax.Precision.HIGHEST)

    return pl.pallas_call(
        body, name="bias_reduce",
        out_shape=jax.ShapeDtypeStruct((REL_BUCKETS, N_HEADS), F32),
    )(onehot, dbias)


def _place():
    return lax.axis_index("x"), lax.axis_index("y"), lax.axis_index("c")


def _gather_weights(shards, col_sharded, ln_shard):
    n_w = len(shards)
    full_shapes = []
    for w, cs in zip(shards, col_sharded):
        r, c = w.shape
        full_shapes.append((r, c * N_CHIPS) if cs else (r * N_CHIPS, c))

    def body(*refs):
        in_refs = refs[:n_w]
        ln_ref = refs[n_w]
        full_refs = refs[n_w + 1:2 * n_w + 1]
        ln_full = refs[2 * n_w + 1]
        stage = refs[2 * n_w + 2:3 * n_w + 2]
        send_sems, recv_sems, local_sems, ln_send, ln_recv = refs[3 * n_w + 2:]
        x, y, c = _place()
        s_me = 2 * x + y
        chips = [(1 - x, y), (x, 1 - y), (1 - x, 1 - y)]

        def shard_window(w, s, half):
            rows, cols = shards[w].shape
            if col_sharded[w]:
                rsel = pl.ds(0, rows) if half is None else pl.ds(half * (rows // 2), rows // 2)
                return full_refs[w].at[rsel, pl.ds(s * cols, cols)]
            if half is None:
                return full_refs[w].at[pl.ds(s * rows, rows), :]
            return full_refs[w].at[pl.ds(s * rows + half * (rows // 2), rows // 2), :]

        def stage_half(w, half):
            rows = shards[w].shape[0]
            return stage[w].at[pl.ds(half * (rows // 2), rows // 2), :]

        def ici_copy(w, k, sender_shard, src):
            return pltpu.make_async_remote_copy(
                src_ref=src, dst_ref=shard_window(w, sender_shard, c),
                send_sem=send_sems.at[w * 3 + k], recv_sem=recv_sems.at[w * 3 + k],
                device_id=(*chips[k], c), device_id_type=MESH)

        def d2d_copy(w, k, half):
            s_k = 2 * chips[k][0] + chips[k][1]
            win = shard_window(w, s_k, half)
            return pltpu.make_async_remote_copy(
                src_ref=win, dst_ref=win,
                send_sem=send_sems.at[3 * n_w + w * 3 + k], recv_sem=recv_sems.at[3 * n_w + w * 3 + k],
                device_id=(x, y, 1 - c), device_id_type=MESH)

        def ln_copy(k, slot):
            return pltpu.make_async_remote_copy(
                src_ref=ln_ref, dst_ref=ln_full.at[slot], send_sem=ln_send.at[k], recv_sem=ln_recv.at[k],
                device_id=(*chips[k], c), device_id_type=MESH)

        for w in range(n_w):
            stage[w][...] = in_refs[w][...].astype(BF16)
        own = [pltpu.make_async_copy(stage[w], shard_window(w, s_me, None), local_sems.at[w]) for w in range(n_w)]
        for cp in own:
            cp.start()
        ln_full[s_me] = ln_ref[...]
        first = [ici_copy(w, k, s_me, stage_half(w, c)) for w in range(n_w) for k in range(3)]
        first += [ln_copy(k, s_me) for k in range(3)]
        for cp in first:
            cp.start()
        passed = []
        for w in range(n_w):
            for k in range(3):
                s_k = 2 * chips[k][0] + chips[k][1]
                ici_copy(w, k, s_k, stage_half(w, c)).wait_recv()
                fwd = d2d_copy(w, k, c)
                fwd.start()
                passed.append(fwd)
        for w in range(n_w):
            for k in range(3):
                d2d_copy(w, k, 1 - c).wait_recv()
        for k in range(3):
            ln_copy(k, 2 * chips[k][0] + chips[k][1]).wait_recv()
        for cp in first + passed:
            cp.wait_send()
        for cp in own:
            cp.wait()

    vmem = pl.BlockSpec(memory_space=pltpu.VMEM)
    hbm = pl.BlockSpec(memory_space=pl.ANY)
    return pl.pallas_call(
        body, name="gather_weights",
        in_specs=[vmem] * (n_w + 1),
        out_specs=[hbm] * n_w + [vmem],
        out_shape=[jax.ShapeDtypeStruct(s, BF16) for s in full_shapes]
        + [jax.ShapeDtypeStruct((N_CHIPS,) + ln_shard.shape, F32)],
        scratch_shapes=[pltpu.VMEM(w.shape, BF16) for w in shards]
        + [pltpu.SemaphoreType.DMA((6 * n_w,)), pltpu.SemaphoreType.DMA((6 * n_w,)),
           pltpu.SemaphoreType.DMA((n_w,)), pltpu.SemaphoreType.DMA((3,)), pltpu.SemaphoreType.DMA((3,))],
        compiler_params=pltpu.CompilerParams(vmem_limit_bytes=VMEM_LIMIT),
    )(*shards, ln_shard)


def _reduce_grads(pieces, small):
    n_w = len(pieces)
    n_peer = N_DEV - 1

    def body(*refs):
        g_refs = refs[:n_w]
        small_ref = refs[n_w]
        out_refs = refs[n_w + 1:2 * n_w + 1]
        small_out = refs[2 * n_w + 1]
        land = refs[2 * n_w + 2:3 * n_w + 2]
        small_land = refs[3 * n_w + 2]
        send_sems, recv_sems, local_sems, swap_send, swap_recv = refs[3 * n_w + 3:]
        x, y, c = _place()
        me = 4 * x + 2 * y + c

        def peer(k):
            return (x + (k >> 2)) % 2, (y + ((k >> 1) & 1)) % 2, (c + (k & 1)) % 2

        def piece_copy(w, k):
            px, py, pc = peer(k)
            idx = w * n_peer + k - 1
            return pltpu.make_async_remote_copy(
                src_ref=g_refs[w].at[4 * px + 2 * py + pc], dst_ref=land[w].at[me],
                send_sem=send_sems.at[idx], recv_sem=recv_sems.at[idx],
                device_id=(px, py, pc), device_id_type=MESH)

        def small_copy(k):
            idx = n_w * n_peer + k - 1
            return pltpu.make_async_remote_copy(
                src_ref=small_ref, dst_ref=small_land.at[me],
                send_sem=send_sems.at[idx], recv_sem=recv_sems.at[idx],
                device_id=peer(k), device_id_type=MESH)

        def swap_copy(w, half):
            rows = pieces[w].shape[1]
            win = out_refs[w].at[pl.ds(pl.multiple_of(half * rows, rows), rows), :]
            return pltpu.make_async_remote_copy(
                src_ref=win, dst_ref=win, send_sem=swap_send.at[w], recv_sem=swap_recv.at[w],
                device_id=(x, y, 1 - c), device_id_type=MESH)

        own = [pltpu.make_async_copy(g_refs[w].at[me], land[w].at[me], local_sems.at[w]) for w in range(n_w)]
        for cp in own:
            cp.start()
        small_land[me] = small_ref[...]
        sent = [piece_copy(w, k) for w in range(n_w) for k in range(1, N_DEV)]
        sent += [small_copy(k) for k in range(1, N_DEV)]
        for cp in sent:
            cp.start()
        swaps = []
        for w in range(n_w):
            own[w].wait()
            for k in range(1, N_DEV):
                piece_copy(w, k).wait_recv()
            rows = pieces[w].shape[1]
            total = land[w][0].astype(F32)
            for p in range(1, N_DEV):
                total += land[w][p].astype(F32)
            out_refs[w][pl.ds(pl.multiple_of(c * rows, rows), rows), :] = total
            sw = swap_copy(w, c)
            sw.start()
            swaps.append(sw)
        for k in range(1, N_DEV):
            small_copy(k).wait_recv()
        total = small_land[0]
        for p in range(1, N_DEV):
            total += small_land[p]
        small_out[...] = total
        for w in range(n_w):
            swap_copy(w, 1 - c).wait_recv()
        for cp in sent + swaps:
            cp.wait_send()

    vmem = pl.BlockSpec(memory_space=pltpu.VMEM)
    hbm = pl.BlockSpec(memory_space=pl.ANY)
    n_rdma = (n_w + 1) * n_peer
    return pl.pallas_call(
        body, name="reduce_grads",
        in_specs=[hbm] * n_w + [vmem],
        out_specs=[vmem] * (n_w + 1),
        out_shape=[jax.ShapeDtypeStruct((2 * p.shape[1], p.shape[2]), F32) for p in pieces]
        + [jax.ShapeDtypeStruct(small.shape, F32)],
        scratch_shapes=[pltpu.VMEM(p.shape, BF16) for p in pieces]
        + [pltpu.VMEM((N_DEV,) + small.shape, F32),
           pltpu.SemaphoreType.DMA((n_rdma,)), pltpu.SemaphoreType.DMA((n_rdma,)),
           pltpu.SemaphoreType.DMA((n_w,)), pltpu.SemaphoreType.DMA((n_w,)), pltpu.SemaphoreType.DMA((n_w,))],
        compiler_params=pltpu.CompilerParams(vmem_limit_bytes=VMEM_LIMIT),
    )(*pieces, small)


def _adamw(label, w, g, m, v):
    shape = w.shape
    cols = shape[-1]
    rows = w.size // cols
    args = [a.reshape(rows, cols) for a in (w, g, m, v)]
    c1 = 1.0 - ADAM_B1 ** ADAM_STEP
    c2 = 1.0 - ADAM_B2 ** ADAM_STEP

    def body(w_ref, g_ref, m_ref, v_ref, d_ref, nm_ref, nv_ref):
        g_t = g_ref[...]
        nm = ADAM_B1 * m_ref[...] + (1.0 - ADAM_B1) * g_t
        nv = ADAM_B2 * v_ref[...] + (1.0 - ADAM_B2) * (g_t * g_t)
        d_ref[...] = -ADAM_LR * ((nm / c1) / (jnp.sqrt(nv / c2) + ADAM_EPS) + ADAM_WD * w_ref[...])
        nm_ref[...] = nm
        nv_ref[...] = nv

    block_rows = 256 if rows % 256 == 0 and rows > 256 else rows
    spec = pl.BlockSpec((block_rows, cols), lambda i: (i, 0))
    outs = pl.pallas_call(
        body, name="adamw_" + label, grid=(rows // block_rows,),
        in_specs=[spec] * 4, out_specs=[spec] * 3,
        out_shape=[jax.ShapeDtypeStruct((rows, cols), F32)] * 3,
        compiler_params=_params(),
    )(*args)
    return [o.reshape(shape) for o in outs]


def _local_step(x, tgt, w_in_a, w_out_a, w_kv, w_in_b, w_out_b, sgu_ln_g, sgu_ln_b, w_spatial, b_spatial,
                attn_sinks, rel_bias, post_ln_g, post_ln_b):
    bsp_t = b_spatial.T
    g1, b1 = post_ln_g[0:1], post_ln_b[0:1]
    g2, b2 = post_ln_g[1:2], post_ln_b[1:2]
    onehot = _bucket_onehot()
    bias = _bias_expand(rel_bias.T, onehot).reshape(N_HEADS, CHUNK, 2 * CHUNK)

    xb, u, vh, z, rv, xh1, rstd1 = _layer_a_fwd(x, w_in_a, w_out_a, sgu_ln_g, sgu_ln_b, w_spatial, bsp_t)
    q, zb, kd, vd = _layer_b_proj(xh1, g1, b1, w_in_b, w_kv)
    o, dr2, loss_vec, dg2, db2 = _layer_b_fwd(q, zb, kd, vd, bias, attn_sinks, xh1, g1, b1, w_out_b, g2, b2, tgt)
    dq, dzb, dkd, dvd, gw_out_b, dsink, dbias = _layer_b_bwd_attn(dr2, zb, o, q, kd, vd, bias, attn_sinks, w_out_b)
    dr1, dg1, db1, gw_in_b, gw_kv = _layer_b_bwd_proj(xh1, rstd1, g1, b1, dr2, dq, dzb, dkd, dvd, w_in_b, w_kv)
    dp, gw_out_a, dws, dbsp, dgs, dbs = _layer_a_bwd_mix(dr1, u, vh, z, rv, w_out_a, sgu_ln_g, sgu_ln_b,
                                                         w_spatial, bsp_t)
    grad_x = _layer_a_bwd_dx(dr1, dp, w_in_a)
    gw_in_a = _layer_a_bwd_win(xb, dp)
    drel = _bias_reduce(onehot, dbias.reshape(N_HEADS, -1))

    loss = (0.5 / D_MODEL) * jnp.sum(loss_vec)
    pieces = [gw_in_a.reshape(N_DEV, D_MODEL // 2, -1), gw_out_a.reshape(N_DEV, -1, D_MODEL),
              gw_kv.reshape(N_DEV, -1, 2 * PAIR), gw_in_b, gw_out_b.reshape(N_DEV, -1, D_MODEL)]
    small = dict(w_spatial=dws, b_spatial=dbsp.reshape(A_GROUPS, CHUNK), attn_sinks=dsink[:, 0].reshape(1, N_HEADS),
                 rel_bias=drel, post_ln_g=jnp.concatenate([dg1, dg2], axis=0),
                 post_ln_b=jnp.concatenate([db1, db2], axis=0), sgu_ln_g=dgs, sgu_ln_b=dbs)
    return loss, grad_x, pieces, small


_SMALL_ORDER = ("w_spatial", "b_spatial", "attn_sinks", "rel_bias", "post_ln_g", "post_ln_b", "sgu_ln_g", "sgu_ln_b")
_LANES = 128
_SUBLANES = 8


def _pack_small(small):
    parts, layout = [], {}
    row = 0
    for name in _SMALL_ORDER:
        flat = small[name].reshape(-1)
        n_rows = -(-flat.size // _LANES)
        n_rows = -(-n_rows // _SUBLANES) * _SUBLANES
        flat = jnp.pad(flat, (0, n_rows * _LANES - flat.size))
        parts.append(flat.reshape(n_rows, _LANES))
        layout[name] = (row, flat.size, small[name].shape)
        row += n_rows
    return jnp.concatenate(parts, axis=0), layout


def _unpack_small(packed, layout, sizes):
    out = {}
    for name in _SMALL_ORDER:
        row, padded, shape = layout[name]
        n = sizes[name]
        out[name] = packed[row:row + padded // _LANES].reshape(-1)[:n].reshape(shape)
    return out


def kernel(x, w_in_a, sgu_ln_g, sgu_ln_b, w_spatial, b_spatial, w_out_a, w_kv, w_in_b, attn_sinks, rel_bias, w_out_b, post_ln_g, post_ln_b, loss_target, m_w_in_a, m_sgu_ln_g, m_sgu_ln_b, m_w_spatial, m_b_spatial, m_w_out_a, m_w_kv, m_w_in_b, m_attn_sinks, m_rel_bias, m_w_out_b, m_post_ln_g, m_post_ln_b, v_w_in_a, v_sgu_ln_g, v_sgu_ln_b, v_w_spatial, v_b_spatial, v_w_out_a, v_w_kv, v_w_in_b, v_attn_sinks, v_rel_bias, v_w_out_b, v_post_ln_g, v_post_ln_b):
    weights = dict(w_in_a=w_in_a, sgu_ln_g=sgu_ln_g, sgu_ln_b=sgu_ln_b, w_spatial=w_spatial, b_spatial=b_spatial,
                   w_out_a=w_out_a, w_kv=w_kv, w_in_b=w_in_b, attn_sinks=attn_sinks, rel_bias=rel_bias,
                   w_out_b=w_out_b, post_ln_g=post_ln_g, post_ln_b=post_ln_b)
    moments_m = dict(w_in_a=m_w_in_a, sgu_ln_g=m_sgu_ln_g, sgu_ln_b=m_sgu_ln_b, w_spatial=m_w_spatial,
                     b_spatial=m_b_spatial, w_out_a=m_w_out_a, w_kv=m_w_kv, w_in_b=m_w_in_b,
                     attn_sinks=m_attn_sinks, rel_bias=m_rel_bias, w_out_b=m_w_out_b, post_ln_g=m_post_ln_g,
                     post_ln_b=m_post_ln_b)
    moments_v = dict(w_in_a=v_w_in_a, sgu_ln_g=v_sgu_ln_g, sgu_ln_b=v_sgu_ln_b, w_spatial=v_w_spatial,
                     b_spatial=v_b_spatial, w_out_a=v_w_out_a, w_kv=v_w_kv, w_in_b=v_w_in_b,
                     attn_sinks=v_attn_sinks, rel_bias=v_rel_bias, w_out_b=v_w_out_b, post_ln_g=v_post_ln_g,
                     post_ln_b=v_post_ln_b)
    order = ("w_in_a", "sgu_ln_g", "sgu_ln_b", "w_spatial", "b_spatial", "w_out_a", "w_kv", "w_in_b", "attn_sinks",
             "rel_bias", "w_out_b", "post_ln_g", "post_ln_b")
    large = ("w_in_a", "w_out_a", "w_kv", "w_in_b", "w_out_b")

    shard_index = 2 * lax.axis_index("x") + lax.axis_index("y")
    ln_shard = jnp.concatenate([sgu_ln_g, sgu_ln_b], axis=0)
    *full, ln_full = _gather_weights([w_in_a[0], w_out_a[0], w_kv, w_in_b[0], w_out_b[0]],
                                     [True, False, False, True, False], ln_shard)
    ln_full = jnp.transpose(ln_full, (1, 0, 2)).reshape(2, A_WIDTH)

    loss_part, grad_x, pieces, small = _local_step(
        x[0], loss_target[0], *full, ln_full[0:1], ln_full[1:2], w_spatial[0], b_spatial[0], attn_sinks,
        rel_bias, post_ln_g, post_ln_b)
    loss = lax.psum(loss_part, ("x", "y", "c"))

    sizes = {name: small[name].size for name in _SMALL_ORDER}
    packed, layout = _pack_small(small)
    *grads_large, packed_sum = _reduce_grads(pieces, packed)
    small_sum = _unpack_small(packed_sum, layout, sizes)

    grads = {}
    for name, g in zip(large, grads_large):
        grads[name] = g.reshape(weights[name].shape)
    shard_cols = sgu_ln_g.shape[1]
    for name in ("sgu_ln_g", "sgu_ln_b"):
        grads[name] = lax.dynamic_slice(small_sum[name], (0, shard_index * shard_cols), (1, shard_cols))
    for name in ("w_spatial", "b_spatial", "attn_sinks", "rel_bias", "post_ln_g", "post_ln_b"):
        grads[name] = small_sum[name].reshape(weights[name].shape)

    deltas, new_m, new_v = {}, {}, {}
    for name in order:
        deltas[name], new_m[name], new_v[name] = _adamw(name, weights[name], grads[name], moments_m[name], moments_v[name])
    return (loss, grad_x[None], *[grads[n] for n in order], *[deltas[n] for n in order],
            *[new_m[n] for n in order], *[new_v[n] for n in order])
```

```python
import math

import jax
import jax.numpy as jnp
from jax import lax
from jax.experimental import pallas as pl
from jax.experimental.pallas import tpu as pltpu

F32 = jnp.float32
BF16 = jnp.bfloat16

D_MODEL = 1024
A_WIDTH = 2048
A_GROUPS = 8
A_GROUP_DIM = 256
CHUNK = 128
N_HEADS = 16
N_KV = 2
HEAD_DIM = 64
PAIR = 2 * HEAD_DIM
B_WIDTH = 1024
REL_BUCKETS = 32
ALPHA = 4.0 ** 0.25
LN_EPS = 1e-5
NEG_INF = -1e30
SCALE = HEAD_DIM ** -0.5

ADAM_LR = 0.001
ADAM_B1 = 0.9
ADAM_B2 = 0.999
ADAM_EPS = 1e-08
ADAM_WD = 0.01
ADAM_STEP = 10

N_DEV = 8
N_CHIPS = 4
MESH = pl.DeviceIdType.MESH
VMEM_LIMIT = 56 * 1024 * 1024

TM_ATTN = 256
TM_MM = 512


def _dot(a, b):
    return jnp.dot(a, b, preferred_element_type=F32)


def _dot_nt(a, b):
    return lax.dot_general(a, b, (((1,), (1,)), ((), ())), preferred_element_type=F32)


def _dot_tn(a, b):
    return lax.dot_general(a, b, (((0,), (0,)), ((), ())), preferred_element_type=F32)


def _ln_fwd(r):
    mu = jnp.mean(r, axis=-1, keepdims=True)
    rc = r - mu
    var = jnp.mean(rc * rc, axis=-1, keepdims=True)
    rstd = lax.rsqrt(var + LN_EPS)
    return rc * rstd, rstd


def _ln_bwd(dxh, xh, rstd):
    m1 = jnp.mean(dxh, axis=-1, keepdims=True)
    m2 = jnp.mean(dxh * xh, axis=-1, keepdims=True)
    return rstd * (dxh - m1 - xh * m2)


def _silu_parts(z):
    sg = jax.nn.sigmoid(z)
    return z * sg, sg * (1.0 + z * (1.0 - sg))


def _dup_halves(blk):
    sw = pltpu.roll(blk, HEAD_DIM, 1)
    lo = lax.broadcasted_iota(jnp.int32, blk.shape, 1) < HEAD_DIM
    return jnp.where(lo, blk, sw), jnp.where(lo, sw, blk)


def _fold_halves(blk):
    return blk + pltpu.roll(blk, HEAD_DIM, 1)


def _resident(shape):
    nd = len(shape)
    return pl.BlockSpec(shape, lambda *_: (0,) * nd, pipeline_mode=pl.Buffered(1))


def _const(shape):
    nd = len(shape)
    return pl.BlockSpec(shape, lambda *_: (0,) * nd)


def _rows(tm, cols):
    return pl.BlockSpec((tm, cols), lambda i: (i, 0))


def _params(sem=("arbitrary",)):
    return pltpu.CompilerParams(dimension_semantics=sem, vmem_limit_bytes=VMEM_LIMIT)


def _spatial_mix(ws_ref, bsp_ref, vn, s_scr, n_chunks):
    tri = (lax.broadcasted_iota(jnp.int32, (CHUNK, CHUNK), 0)
           >= lax.broadcasted_iota(jnp.int32, (CHUNK, CHUNK), 1))
    for g in range(A_GROUPS):
        wsg = jnp.where(tri, ws_ref[g], 0.0).astype(BF16)
        cols = slice(g * A_GROUP_DIM, (g + 1) * A_GROUP_DIM)
        for ci in range(n_chunks):
            rows = slice(ci * CHUNK, (ci + 1) * CHUNK)
            s_scr[rows, cols] = _dot(wsg, vn[rows, cols]) + bsp_ref[:, g:g + 1]


def _layer_a_fwd(x, w_in, w_out, lng, lnb, ws, bsp_t):
    t_len = x.shape[0]
    tm = TM_ATTN

    def body(x_ref, win_ref, wout_ref, lng_ref, lnb_ref, ws_ref, bsp_ref,
             xb_ref, u_ref, vh_ref, z_ref, rv_ref, xh_ref, r1_ref, s_scr):
        x_t = x_ref[...]
        xb = x_t.astype(BF16)
        xb_ref[...] = xb
        u = _dot(xb, win_ref[:, 0:A_WIDTH])
        v = _dot(xb, win_ref[:, A_WIDTH:2 * A_WIDTH])
        z = _dot(xb, win_ref[:, 2 * A_WIDTH:3 * A_WIDTH])
        vh, rv = _ln_fwd(v)
        vn = (vh * lng_ref[...] + lnb_ref[...]).astype(BF16)
        _spatial_mix(ws_ref, bsp_ref, vn, s_scr, tm // CHUNK)
        sz, _ = _silu_parts(z)
        y = (u * s_scr[...] * sz).astype(BF16)
        r = ALPHA * x_t + _dot(y, wout_ref[...])
        xh, r1 = _ln_fwd(r)
        u_ref[...] = u.astype(BF16)
        vh_ref[...] = vh.astype(BF16)
        z_ref[...] = z.astype(BF16)
        rv_ref[...] = rv
        xh_ref[...] = xh
        r1_ref[...] = r1

    return pl.pallas_call(
        body, name="layer_a_fwd", grid=(t_len // tm,),
        in_specs=[_rows(tm, D_MODEL), _resident(w_in.shape), _resident(w_out.shape), _const(lng.shape),
                  _const(lnb.shape), _const(ws.shape), _const(bsp_t.shape)],
        out_specs=[_rows(tm, D_MODEL), _rows(tm, A_WIDTH), _rows(tm, A_WIDTH), _rows(tm, A_WIDTH),
                   _rows(tm, 1), _rows(tm, D_MODEL), _rows(tm, 1)],
        out_shape=[jax.ShapeDtypeStruct((t_len, D_MODEL), BF16), jax.ShapeDtypeStruct((t_len, A_WIDTH), BF16),
                   jax.ShapeDtypeStruct((t_len, A_WIDTH), BF16), jax.ShapeDtypeStruct((t_len, A_WIDTH), BF16),
                   jax.ShapeDtypeStruct((t_len, 1), F32), jax.ShapeDtypeStruct((t_len, D_MODEL), F32),
                   jax.ShapeDtypeStruct((t_len, 1), F32)],
        scratch_shapes=[pltpu.VMEM((tm, A_WIDTH), F32)],
        compiler_params=_params(),
    )(x, w_in, w_out, lng, lnb, ws, bsp_t)


def _layer_b_proj(xh1, g1, b1, w_in, w_kv):
    t_len = xh1.shape[0]
    tm = TM_MM

    def body(xh_ref, g_ref, b_ref, win_ref, wkv_ref, q_ref, z_ref, kd_ref, vd_ref):
        h1 = (xh_ref[...] * g_ref[...] + b_ref[...]).astype(BF16)
        q_ref[...] = _dot(h1, win_ref[:, 0:B_WIDTH]).astype(BF16)
        z_ref[...] = _dot(h1, win_ref[:, B_WIDTH:2 * B_WIDTH]).astype(BF16)
        kv = _dot(h1, wkv_ref[...])
        k0, k1 = _dup_halves(kv[:, 0:PAIR])
        v0, v1 = _dup_halves(kv[:, PAIR:2 * PAIR])
        kd_ref[:, 0:PAIR] = k0.astype(BF16)
        kd_ref[:, PAIR:2 * PAIR] = k1.astype(BF16)
        vd_ref[:, 0:PAIR] = v0.astype(BF16)
        vd_ref[:, PAIR:2 * PAIR] = v1.astype(BF16)

    return pl.pallas_call(
        body, name="layer_b_proj", grid=(t_len // tm,),
        in_specs=[_rows(tm, D_MODEL), _const(g1.shape), _const(b1.shape), _resident(w_in.shape),
                  _resident(w_kv.shape)],
        out_specs=[_rows(tm, B_WIDTH), _rows(tm, B_WIDTH), _rows(tm, 2 * PAIR), _rows(tm, 2 * PAIR)],
        out_shape=[jax.ShapeDtypeStruct((t_len, B_WIDTH), BF16), jax.ShapeDtypeStruct((t_len, B_WIDTH), BF16),
                   jax.ShapeDtypeStruct((t_len, 2 * PAIR), BF16), jax.ShapeDtypeStruct((t_len, 2 * PAIR), BF16)],
        compiler_params=_params(),
    )(xh1, g1, b1, w_in, w_kv)


GROUP = N_HEADS // N_KV
GROUP_Q = GROUP * CHUNK


def _window_tables():
    j = jnp.arange(2 * CHUNK, dtype=jnp.int32)[:, None]
    t = jnp.arange(CHUNK, dtype=jnp.int32)[None, :]
    dist = t + CHUNK - j
    inside = (dist >= 0) & (dist < CHUNK)
    return jnp.stack([inside & (j >= CHUNK), inside]).astype(F32)


def _band(ref, chunk_index, kvh):
    prev0 = pl.multiple_of(jnp.maximum(chunk_index - 1, 0) * CHUNK, CHUNK)
    cur0 = pl.multiple_of(chunk_index * CHUNK, CHUNK)
    cols = slice(kvh * PAIR, (kvh + 1) * PAIR)
    return jnp.concatenate([ref[pl.ds(prev0, CHUNK), cols], ref[pl.ds(cur0, CHUNK), cols]], axis=0)


def _group_tables(bias_ref, win_ref, sink_ref, chunk_index, kvh):
    bias = jnp.concatenate([bias_ref[kvh * GROUP + j] for j in range(GROUP)], axis=1)
    win = win_ref[jnp.minimum(chunk_index, 1)]
    mask = jnp.concatenate([win] * GROUP, axis=1) > 0.5
    sink = jnp.concatenate([jnp.full((1, CHUNK), sink_ref[0, kvh * GROUP + j], F32) for j in range(GROUP)], axis=1)
    return bias, mask, sink


def _attn_probs(qs, kband, bias, mask, sink):
    logits = jnp.where(mask, _dot_nt(kband, qs) * SCALE + bias, NEG_INF)
    m = jnp.maximum(jnp.max(logits, axis=0, keepdims=True), sink)
    e = jnp.exp(logits - m)
    es = jnp.exp(sink - m)
    inv = 1.0 / (jnp.sum(e, axis=0, keepdims=True) + es)
    return e * inv, es * inv


def _half_mask():
    return lax.broadcasted_iota(jnp.int32, (CHUNK, PAIR), 1) < HEAD_DIM


def _stack_heads(src_ref, rows, kvh, dst_scr, lo):
    for j in range(GROUP):
        h = kvh * GROUP + j
        blk = src_ref[rows, (h // 2) * PAIR:(h // 2 + 1) * PAIR].astype(F32)
        keep = lo if h % 2 == 0 else ~lo
        dst_scr[j * CHUNK:(j + 1) * CHUNK, :] = jnp.where(keep, blk, 0.0).astype(BF16)


def _unstack_pairs(stacked, pp, lo):
    return jnp.where(lo, stacked[(2 * pp) * CHUNK:(2 * pp + 1) * CHUNK], stacked[(2 * pp + 1) * CHUNK:(2 * pp + 2) * CHUNK])


def _layer_b_fwd(q, zb, kd, vd, bias, win, sinks, xh1, g1, b1, w_out, g2, b2, tgt):
    t_len = q.shape[0]
    tm = TM_ATTN

    def body(q_ref, z_ref, kd_ref, vd_ref, bias_ref, win_ref, sink_ref, xh_ref, g1_ref, b1_ref, wout_ref, g2_ref,
             b2_ref, tgt_ref, o_ref, dr_ref, loss_ref, dg_ref, db_ref, o_scr, qs_scr):
        i = pl.program_id(0)

        @pl.when(i == 0)
        def _():
            loss_ref[...] = jnp.zeros_like(loss_ref)
            dg_ref[...] = jnp.zeros_like(dg_ref)
            db_ref[...] = jnp.zeros_like(db_ref)

        lo = _half_mask()
        for ci in range(tm // CHUNK):
            cg = i * (tm // CHUNK) + ci
            rows = slice(ci * CHUNK, (ci + 1) * CHUNK)
            for kvh in range(N_KV):
                kband = _band(kd_ref, cg, kvh)
                vband = _band(vd_ref, cg, kvh)
                bias_g, mask, sink = _group_tables(bias_ref, win_ref, sink_ref, cg, kvh)
                _stack_heads(q_ref, rows, kvh, qs_scr, lo)
                p, _ = _attn_probs(qs_scr[...], kband, bias_g, mask, sink)
                o_stack = _dot_tn(p.astype(BF16), vband)
                for pp in range(GROUP // 2):
                    pair = kvh * (GROUP // 2) + pp
                    o_scr[rows, pair * PAIR:(pair + 1) * PAIR] = _unstack_pairs(o_stack, pp, lo)
        o = o_scr[...]
        o_ref[...] = o.astype(BF16)
        sz, _ = _silu_parts(z_ref[...].astype(F32))
        y = (o * sz).astype(BF16)
        h1 = xh_ref[...] * g1_ref[...] + b1_ref[...]
        r = ALPHA * h1 + _dot(y, wout_ref[...])
        xh2, rstd2 = _ln_fwd(r)
        diff = xh2 * g2_ref[...] + b2_ref[...] - tgt_ref[...]
        loss_ref[...] += jnp.sum(diff * diff, axis=0, keepdims=True)
        dh2 = diff * (1.0 / D_MODEL)
        dg_ref[...] += jnp.sum(dh2 * xh2, axis=0, keepdims=True)
        db_ref[...] += jnp.sum(dh2, axis=0, keepdims=True)
        dr_ref[...] = _ln_bwd(dh2 * g2_ref[...], xh2, rstd2)

    vec = jax.ShapeDtypeStruct((1, D_MODEL), F32)
    return pl.pallas_call(
        body, name="layer_b_fwd", grid=(t_len // tm,),
        in_specs=[_rows(tm, B_WIDTH), _rows(tm, B_WIDTH), _resident(kd.shape), _resident(vd.shape),
                  _resident(bias.shape), _resident(win.shape), pl.BlockSpec(memory_space=pltpu.SMEM),
                  _rows(tm, D_MODEL), _const(g1.shape), _const(b1.shape), _resident(w_out.shape), _const(g2.shape),
                  _const(b2.shape), _rows(tm, D_MODEL)],
        out_specs=[_rows(tm, B_WIDTH), _rows(tm, D_MODEL), _const((1, D_MODEL)), _const((1, D_MODEL)),
                   _const((1, D_MODEL))],
        out_shape=[jax.ShapeDtypeStruct((t_len, B_WIDTH), BF16), jax.ShapeDtypeStruct((t_len, D_MODEL), F32),
                   vec, vec, vec],
        scratch_shapes=[pltpu.VMEM((tm, B_WIDTH), F32), pltpu.VMEM((GROUP_Q, PAIR), BF16)],
        compiler_params=_params(),
    )(q, zb, kd, vd, bias, win, sinks, xh1, g1, b1, w_out, g2, b2, tgt)


def _layer_b_bwd_attn(dr2, zb, o, q, kd, vd, bias, win, sinks, w_out):
    t_len = q.shape[0]
    tm = TM_ATTN
    n_steps = t_len // tm
    n_chunks = tm // CHUNK

    def body(dr_ref, z_ref, o_ref, q_ref, kd_ref, vd_ref, bias_ref, win_ref, sink_ref, wout_ref,
             dq_ref, dz_ref, dkd_ref, dvd_ref, ck_ref, cv_ref, gw_ref, dsink_ref, dbias_ref,
             do_scr, qs_scr, dos_scr, gw_acc):
        i = pl.program_id(0)

        @pl.when(i == 0)
        def _():
            gw_acc[...] = jnp.zeros_like(gw_acc)
            dsink_ref[...] = jnp.zeros_like(dsink_ref)
            dbias_ref[...] = jnp.zeros_like(dbias_ref)

        drb = dr_ref[...].astype(BF16)
        dy = _dot_nt(drb, wout_ref[...])
        z = z_ref[...].astype(F32)
        sz, dsz = _silu_parts(z)
        o_t = o_ref[...].astype(F32)
        dz_ref[...] = (dy * o_t * dsz).astype(BF16)
        do_scr[...] = (dy * sz).astype(BF16)
        gw_acc[...] += _dot_tn((o_t * sz).astype(BF16), drb)

        lo = _half_mask()
        for kvh in range(N_KV):
            kcols = slice(kvh * PAIR, (kvh + 1) * PAIR)
            dk_bands, dv_bands = [], []
            for ci in range(n_chunks):
                cg = i * n_chunks + ci
                rows = slice(ci * CHUNK, (ci + 1) * CHUNK)
                kband = _band(kd_ref, cg, kvh)
                vband = _band(vd_ref, cg, kvh)
                bias_g, mask, sink = _group_tables(bias_ref, win_ref, sink_ref, cg, kvh)
                _stack_heads(q_ref, rows, kvh, qs_scr, lo)
                _stack_heads(do_scr, rows, kvh, dos_scr, lo)
                qs = qs_scr[...]
                dos = dos_scr[...]
                p, p_sink = _attn_probs(qs, kband, bias_g, mask, sink)
                dp = _dot_nt(vband, dos)
                delta = jnp.sum(p * dp, axis=0, keepdims=True)
                dlog = p * (dp - delta)
                for j in range(GROUP):
                    dbias_ref[kvh * GROUP + j] += dlog[:, j * CHUNK:(j + 1) * CHUNK]
                dsink_ref[kvh:kvh + 1, :] += -(p_sink * delta)
                ds = (dlog * SCALE).astype(BF16)
                dq_stack = _dot_tn(ds, kband)
                for pp in range(GROUP // 2):
                    pair = kvh * (GROUP // 2) + pp
                    dq_ref[rows, pair * PAIR:(pair + 1) * PAIR] = _unstack_pairs(dq_stack, pp, lo).astype(BF16)
                dk_bands.append(_dot(ds, qs))
                dv_bands.append(_dot(p.astype(BF16), dos))
            for bands, out_ref, carry_ref in ((dk_bands, dkd_ref, ck_ref), (dv_bands, dvd_ref, cv_ref)):
                carry_ref[0, :, kcols] = bands[0][0:CHUNK]
                for ci in range(n_chunks):
                    own = bands[ci][CHUNK:2 * CHUNK]
                    if ci + 1 < n_chunks:
                        own = own + bands[ci + 1][0:CHUNK]
                    out_ref[ci * CHUNK:(ci + 1) * CHUNK, kcols] = own

        @pl.when(i == n_steps - 1)
        def _():
            gw_ref[...] = gw_acc[...].astype(BF16)

    carry_spec = pl.BlockSpec((1, CHUNK, 2 * PAIR), lambda i: (i, 0, 0))
    carry_shape = jax.ShapeDtypeStruct((n_steps, CHUNK, 2 * PAIR), F32)
    return pl.pallas_call(
        body, name="layer_b_bwd_attn", grid=(n_steps,),
        in_specs=[_rows(tm, D_MODEL), _rows(tm, B_WIDTH), _rows(tm, B_WIDTH), _rows(tm, B_WIDTH),
                  _resident(kd.shape), _resident(vd.shape), _resident(bias.shape), _resident(win.shape),
                  pl.BlockSpec(memory_space=pltpu.SMEM), _resident(w_out.shape)],
        out_specs=[_rows(tm, B_WIDTH), _rows(tm, B_WIDTH), _rows(tm, 2 * PAIR), _rows(tm, 2 * PAIR),
                   carry_spec, carry_spec, _const(w_out.shape), _const((N_KV, GROUP_Q)), _const(bias.shape)],
        out_shape=[jax.ShapeDtypeStruct((t_len, B_WIDTH), BF16), jax.ShapeDtypeStruct((t_len, B_WIDTH), BF16),
                   jax.ShapeDtypeStruct((t_len, 2 * PAIR), F32), jax.ShapeDtypeStruct((t_len, 2 * PAIR), F32),
                   carry_shape, carry_shape, jax.ShapeDtypeStruct(w_out.shape, BF16),
                   jax.ShapeDtypeStruct((N_KV, GROUP_Q), F32), jax.ShapeDtypeStruct(bias.shape, F32)],
        scratch_shapes=[pltpu.VMEM((tm, B_WIDTH), BF16), pltpu.VMEM((GROUP_Q, PAIR), BF16),
                        pltpu.VMEM((GROUP_Q, PAIR), BF16), pltpu.VMEM(w_out.shape, F32)],
        compiler_params=_params(),
    )(dr2, zb, o, q, kd, vd, bias, win, sinks, w_out)


def _layer_b_bwd_proj(xh1, rstd1, g1, b1, dr2, dq, dzb, dkd, dvd, carry_k, carry_v, w_in, w_kv):
    t_len = xh1.shape[0]
    tm = TM_MM
    n_steps = t_len // tm
    per_tile = tm // TM_ATTN
    n_carry = carry_k.shape[0]

    def body(xh_ref, rstd_ref, g_ref, b_ref, dr2_ref, dq_ref, dz_ref, dkd_ref, dvd_ref, *rest):
        carry_refs = rest[:2 * per_tile]
        win_ref, wkv_ref, dr1_ref, dg_ref, db_ref, gwin_ref, gwkv_ref, acc_in, acc_kv = rest[2 * per_tile:]
        i = pl.program_id(0)

        @pl.when(i == 0)
        def _():
            acc_in[...] = jnp.zeros_like(acc_in)
            acc_kv[...] = jnp.zeros_like(acc_kv)
            dg_ref[...] = jnp.zeros_like(dg_ref)
            db_ref[...] = jnp.zeros_like(db_ref)

        def with_carries(tile_ref, refs):
            parts = []
            for a in range(per_tile):
                parts.append(tile_ref[a * TM_ATTN:(a + 1) * TM_ATTN - CHUNK, :])
                carry = refs[a][0]
                if a == per_tile - 1:
                    carry = jnp.where(i < n_steps - 1, carry, 0.0)
                parts.append(tile_ref[(a + 1) * TM_ATTN - CHUNK:(a + 1) * TM_ATTN, :] + carry)
            return jnp.concatenate(parts, axis=0).astype(BF16)

        xh = xh_ref[...]
        h1 = (xh * g_ref[...] + b_ref[...]).astype(BF16)
        dq_t = dq_ref[...]
        dz_t = dz_ref[...]
        dkd_t = with_carries(dkd_ref, carry_refs[:per_tile])
        dvd_t = with_carries(dvd_ref, carry_refs[per_tile:])
        wkv = wkv_ref[...].astype(F32)
        wk0, wk1 = _dup_halves(wkv[:, 0:PAIR])
        wv0, wv1 = _dup_halves(wkv[:, PAIR:2 * PAIR])
        dh1 = ALPHA * dr2_ref[...]
        dh1 += _dot_nt(dq_t, win_ref[:, 0:B_WIDTH])
        dh1 += _dot_nt(dz_t, win_ref[:, B_WIDTH:2 * B_WIDTH])
        for blk, w in ((dkd_t[:, 0:PAIR], wk0), (dkd_t[:, PAIR:2 * PAIR], wk1),
                       (dvd_t[:, 0:PAIR], wv0), (dvd_t[:, PAIR:2 * PAIR], wv1)):
            dh1 += _dot_nt(blk, w.astype(BF16))
        acc_in[:, 0:B_WIDTH] += _dot_tn(h1, dq_t)
        acc_in[:, B_WIDTH:2 * B_WIDTH] += _dot_tn(h1, dz_t)
        acc_kv[:, 0:2 * PAIR] += _dot_tn(h1, dkd_t)
        acc_kv[:, 2 * PAIR:4 * PAIR] += _dot_tn(h1, dvd_t)
        dg_ref[...] += jnp.sum(dh1 * xh, axis=0, keepdims=True)
        db_ref[...] += jnp.sum(dh1, axis=0, keepdims=True)
        dr1_ref[...] = _ln_bwd(dh1 * g_ref[...], xh, rstd_ref[...])

        @pl.when(i == n_steps - 1)
        def _():
            half_rows = D_MODEL // 2
            shard_cols = 2 * B_WIDTH // N_CHIPS
            for s in range(N_CHIPS):
                for c in range(2):
                    gwin_ref[2 * s + c] = acc_in[c * half_rows:(c + 1) * half_rows,
                                                 s * shard_cols:(s + 1) * shard_cols].astype(BF16)
            lo = lax.broadcasted_iota(jnp.int32, (D_MODEL, PAIR), 1) < HEAD_DIM
            for n in range(2):
                f0 = _fold_halves(acc_kv[:, (2 * n) * PAIR:(2 * n + 1) * PAIR])
                f1 = _fold_halves(acc_kv[:, (2 * n + 1) * PAIR:(2 * n + 2) * PAIR])
                gwkv_ref[:, n * PAIR:(n + 1) * PAIR] = jnp.where(lo, f0, f1).astype(BF16)

    vec = jax.ShapeDtypeStruct((1, D_MODEL), F32)
    gwin_shape = (N_DEV, D_MODEL // 2, 2 * B_WIDTH // N_CHIPS)

    def carry_spec(a):
        return pl.BlockSpec((1, CHUNK, 2 * PAIR), lambda i: (jnp.minimum(per_tile * i + a + 1, n_carry - 1), 0, 0))

    carry_specs = [carry_spec(a) for a in range(per_tile)]
    return pl.pallas_call(
        body, name="layer_b_bwd_proj", grid=(n_steps,),
        in_specs=[_rows(tm, D_MODEL), _rows(tm, 1), _const(g1.shape), _const(b1.shape), _rows(tm, D_MODEL),
                  _rows(tm, B_WIDTH), _rows(tm, B_WIDTH), _rows(tm, 2 * PAIR), _rows(tm, 2 * PAIR)]
        + carry_specs + carry_specs + [_resident(w_in.shape), _resident(w_kv.shape)],
        out_specs=[_rows(tm, D_MODEL), _const((1, D_MODEL)), _const((1, D_MODEL)), _const(gwin_shape),
                   _const(w_kv.shape)],
        out_shape=[jax.ShapeDtypeStruct((t_len, D_MODEL), F32), vec, vec,
                   jax.ShapeDtypeStruct(gwin_shape, BF16), jax.ShapeDtypeStruct(w_kv.shape, BF16)],
        scratch_shapes=[pltpu.VMEM(w_in.shape, F32), pltpu.VMEM((D_MODEL, 4 * PAIR), F32)],
        compiler_params=_params(),
    )(xh1, rstd1, g1, b1, dr2, dq, dzb, dkd, dvd, *([carry_k] * per_tile), *([carry_v] * per_tile), w_in, w_kv)


def _layer_a_bwd_mix(dr1, u, vh, z, rv, w_out, lng, lnb, ws, bsp_t):
    t_len = u.shape[0]
    tm = TM_ATTN
    n_steps = t_len // tm

    def body(dr_ref, u_ref, vh_ref, z_ref, rv_ref, wout_ref, lng_ref, lnb_ref, ws_ref, bsp_ref,
             dp_ref, gw_ref, dws_ref, dbsp_ref, dgs_ref, dbs_ref, s_scr, dvn_scr, gw_acc):
        i = pl.program_id(0)

        @pl.when(i == 0)
        def _():
            gw_acc[...] = jnp.zeros_like(gw_acc)
            dws_ref[...] = jnp.zeros_like(dws_ref)
            dbsp_ref[...] = jnp.zeros_like(dbsp_ref)
            dgs_ref[...] = jnp.zeros_like(dgs_ref)
            dbs_ref[...] = jnp.zeros_like(dbs_ref)

        drb = dr_ref[...].astype(BF16)
        dy = _dot_nt(drb, wout_ref[...])
        u_t = u_ref[...].astype(F32)
        vh_t = vh_ref[...].astype(F32)
        z_t = z_ref[...].astype(F32)
        vn = (vh_t * lng_ref[...] + lnb_ref[...]).astype(BF16)
        _spatial_mix(ws_ref, bsp_ref, vn, s_scr, tm // CHUNK)
        s = s_scr[...]
        sz, dsz = _silu_parts(z_t)
        gw_acc[...] += _dot_tn((u_t * s * sz).astype(BF16), drb)
        dp_ref[:, 0:A_WIDTH] = (dy * s * sz).astype(BF16)
        dp_ref[:, 2 * A_WIDTH:3 * A_WIDTH] = (dy * u_t * s * dsz).astype(BF16)
        ds = dy * u_t * sz

        tri = (lax.broadcasted_iota(jnp.int32, (CHUNK, CHUNK), 0)
               >= lax.broadcasted_iota(jnp.int32, (CHUNK, CHUNK), 1))
        for g in range(A_GROUPS):
            wsg = jnp.where(tri, ws_ref[g], 0.0).astype(BF16)
            cols = slice(g * A_GROUP_DIM, (g + 1) * A_GROUP_DIM)
            dws_g = jnp.zeros((CHUNK, CHUNK), F32)
            dbsp_g = jnp.zeros((CHUNK, 1), F32)
            for ci in range(tm // CHUNK):
                rows = slice(ci * CHUNK, (ci + 1) * CHUNK)
                ds_g = ds[rows, cols]
                ds_b = ds_g.astype(BF16)
                dws_g += _dot_nt(ds_b, vn[rows, cols])
                dbsp_g += jnp.sum(ds_g, axis=-1, keepdims=True)
                dvn_scr[rows, cols] = _dot_tn(wsg, ds_b)
            dws_ref[g] += jnp.where(tri, dws_g, 0.0)
            dbsp_ref[g] += dbsp_g
        dvn = dvn_scr[...]
        dgs_ref[...] += jnp.sum(dvn * vh_t, axis=0, keepdims=True)
        dbs_ref[...] += jnp.sum(dvn, axis=0, keepdims=True)
        dp_ref[:, A_WIDTH:2 * A_WIDTH] = _ln_bwd(dvn * lng_ref[...], vh_t, rv_ref[...]).astype(BF16)

        @pl.when(i == n_steps - 1)
        def _():
            gw_ref[...] = gw_acc[...].astype(BF16)

    wide = jax.ShapeDtypeStruct((1, A_WIDTH), F32)
    return pl.pallas_call(
        body, name="layer_a_bwd_mix", grid=(n_steps,),
        in_specs=[_rows(tm, D_MODEL), _rows(tm, A_WIDTH), _rows(tm, A_WIDTH), _rows(tm, A_WIDTH), _rows(tm, 1),
                  _resident(w_out.shape), _const(lng.shape), _const(lnb.shape), _const(ws.shape),
                  _const(bsp_t.shape)],
        out_specs=[_rows(tm, 3 * A_WIDTH), _const(w_out.shape), _const(ws.shape), _const((A_GROUPS, CHUNK, 1)),
                   _const((1, A_WIDTH)), _const((1, A_WIDTH))],
        out_shape=[jax.ShapeDtypeStruct((t_len, 3 * A_WIDTH), BF16), jax.ShapeDtypeStruct(w_out.shape, BF16),
                   jax.ShapeDtypeStruct(ws.shape, F32), jax.ShapeDtypeStruct((A_GROUPS, CHUNK, 1), F32),
                   wide, wide],
        scratch_shapes=[pltpu.VMEM((tm, A_WIDTH), F32), pltpu.VMEM((tm, A_WIDTH), F32),
                        pltpu.VMEM(w_out.shape, F32)],
        compiler_params=_params(),
    )(dr1, u, vh, z, rv, w_out, lng, lnb, ws, bsp_t)


def _layer_a_bwd_dx(dr1, dp, w_in):
    t_len = dr1.shape[0]
    tm = TM_MM

    def body(dr_ref, dp_ref, win_ref, dx_ref):
        dx_ref[...] = ALPHA * dr_ref[...] + _dot_nt(dp_ref[...], win_ref[...])

    return pl.pallas_call(
        body, name="layer_a_bwd_dx", grid=(t_len // tm,),
        in_specs=[_rows(tm, D_MODEL), _rows(tm, 3 * A_WIDTH), _resident(w_in.shape)],
        out_specs=_rows(tm, D_MODEL),
        out_shape=jax.ShapeDtypeStruct((t_len, D_MODEL), F32),
        compiler_params=_params(),
    )(dr1, dp, w_in)


def _layer_a_bwd_win(xb, dp):
    t_len = xb.shape[0]
    tm = TM_MM
    n_steps = t_len // tm
    shard_cols = 3 * A_WIDTH // N_CHIPS
    half_rows = D_MODEL // 2

    def body(xb_ref, dp_ref, gw_ref, acc):
        i = pl.program_id(1)

        @pl.when(i == 0)
        def _():
            acc[...] = jnp.zeros_like(acc)

        acc[...] += _dot_tn(xb_ref[...], dp_ref[...])

        @pl.when(i == n_steps - 1)
        def _():
            for c in range(2):
                gw_ref[0, c] = acc[c * half_rows:(c + 1) * half_rows, :].astype(BF16)

    return pl.pallas_call(
        body, name="layer_a_bwd_win", grid=(N_CHIPS, n_steps),
        in_specs=[pl.BlockSpec((tm, D_MODEL), lambda j, i: (i, 0)),
                  pl.BlockSpec((tm, shard_cols), lambda j, i: (i, j))],
        out_specs=pl.BlockSpec((1, 2, half_rows, shard_cols), lambda j, i: (j, 0, 0, 0)),
        out_shape=jax.ShapeDtypeStruct((N_CHIPS, 2, half_rows, shard_cols), BF16),
        scratch_shapes=[pltpu.VMEM((D_MODEL, shard_cols), F32)],
        compiler_params=_params(("arbitrary", "arbitrary")),
    )(xb, dp)


def _bucket_onehot():
    t = jnp.arange(CHUNK, dtype=jnp.int32)[None, :]
    j = jnp.arange(2 * CHUNK, dtype=jnp.int32)[:, None]
    dist = jnp.clip(t + CHUNK - j, 0, CHUNK - 1)
    max_exact = REL_BUCKETS // 2
    df = jnp.maximum(dist, 1).astype(F32)
    large = max_exact + (jnp.log(df / max_exact) / math.log(CHUNK / max_exact)
                         * (REL_BUCKETS - max_exact)).astype(jnp.int32)
    bucket = jnp.where(dist < max_exact, dist, jnp.minimum(large, REL_BUCKETS - 1))
    onehot = bucket.reshape(1, -1) == jnp.arange(REL_BUCKETS, dtype=jnp.int32)[:, None]
    return onehot.astype(F32)


def _bias_expand(rel_t, onehot):
    def body(rel_ref, oh_ref, out_ref):
        out_ref[...] = jnp.dot(rel_ref[...], oh_ref[...], preferred_element_type=F32,
                               precision=lax.Precision.HIGHEST)

    return pl.pallas_call(
        body, name="bias_expand",
        out_shape=jax.ShapeDtypeStruct((N_HEADS, onehot.shape[1]), F32),
    )(rel_t, onehot)


def _bias_reduce(onehot, dbias):
    def body(oh_ref, db_ref, out_ref):
        out_ref[...] = lax.dot_general(oh_ref[...], db_ref[...], (((1,), (1,)), ((), ())),
                                       preferred_element_type=F32, precision=lax.Precision.HIGHEST)

    return pl.pallas_call(
        body, name="bias_reduce",
        out_shape=jax.ShapeDtypeStruct((REL_BUCKETS, N_HEADS), F32),
    )(onehot, dbias)


def _place():
    return lax.axis_index("x"), lax.axis_index("y"), lax.axis_index("c")


def _gather_weights(shards, col_sharded, ln_shard):
    n_w = len(shards)
    full_shapes = []
    for w, cs in zip(shards, col_sharded):
        r, c = w.shape
        full_shapes.append((r, c * N_CHIPS) if cs else (r * N_CHIPS, c))

    def body(*refs):
        in_refs = refs[:n_w]
        ln_ref = refs[n_w]
        full_refs = refs[n_w + 1:2 * n_w + 1]
        ln_full = refs[2 * n_w + 1]
        stage = refs[2 * n_w + 2:3 * n_w + 2]
        send_sems, recv_sems, local_sems, ln_send, ln_recv = refs[3 * n_w + 2:]
        x, y, c = _place()
        s_me = 2 * x + y
        chips = [(1 - x, y), (x, 1 - y), (1 - x, 1 - y)]

        def shard_window(w, s, half):
            rows, cols = shards[w].shape
            if col_sharded[w]:
                rsel = pl.ds(0, rows) if half is None else pl.ds(half * (rows // 2), rows // 2)
                return full_refs[w].at[rsel, pl.ds(s * cols, cols)]
            if half is None:
                return full_refs[w].at[pl.ds(s * rows, rows), :]
            return full_refs[w].at[pl.ds(s * rows + half * (rows // 2), rows // 2), :]

        def stage_half(w, half):
            rows = shards[w].shape[0]
            return stage[w].at[pl.ds(half * (rows // 2), rows // 2), :]

        def ici_copy(w, k, sender_shard, src):
            return pltpu.make_async_remote_copy(
                src_ref=src, dst_ref=shard_window(w, sender_shard, c),
                send_sem=send_sems.at[w * 3 + k], recv_sem=recv_sems.at[w * 3 + k],
                device_id=(*chips[k], c), device_id_type=MESH)

        def d2d_copy(w, k, half):
            s_k = 2 * chips[k][0] + chips[k][1]
            win = shard_window(w, s_k, half)
            return pltpu.make_async_remote_copy(
                src_ref=win, dst_ref=win,
                send_sem=send_sems.at[3 * n_w + w * 3 + k], recv_sem=recv_sems.at[3 * n_w + w * 3 + k],
                device_id=(x, y, 1 - c), device_id_type=MESH)

        def ln_copy(k, slot):
            return pltpu.make_async_remote_copy(
                src_ref=ln_ref, dst_ref=ln_full.at[slot], send_sem=ln_send.at[k], recv_sem=ln_recv.at[k],
                device_id=(*chips[k], c), device_id_type=MESH)

        for w in range(n_w):
            stage[w][...] = in_refs[w][...].astype(BF16)
        own = [pltpu.make_async_copy(stage[w], shard_window(w, s_me, None), local_sems.at[w]) for w in range(n_w)]
        for cp in own:
            cp.start()
        ln_full[s_me] = ln_ref[...]
        first = [ici_copy(w, k, s_me, stage_half(w, c)) for w in range(n_w) for k in range(3)]
        first += [ln_copy(k, s_me) for k in range(3)]
        for cp in first:
            cp.start()
        passed = []
        for w in range(n_w):
            for k in range(3):
                s_k = 2 * chips[k][0] + chips[k][1]
                ici_copy(w, k, s_k, stage_half(w, c)).wait_recv()
                fwd = d2d_copy(w, k, c)
                fwd.start()
                passed.append(fwd)
        for w in range(n_w):
            for k in range(3):
                d2d_copy(w, k, 1 - c).wait_recv()
        for k in range(3):
            ln_copy(k, 2 * chips[k][0] + chips[k][1]).wait_recv()
        for cp in first + passed:
            cp.wait_send()
        for cp in own:
            cp.wait()

    vmem = pl.BlockSpec(memory_space=pltpu.VMEM)
    hbm = pl.BlockSpec(memory_space=pl.ANY)
    return pl.pallas_call(
        body, name="gather_weights",
        in_specs=[vmem] * (n_w + 1),
        out_specs=[hbm] * n_w + [vmem],
        out_shape=[jax.ShapeDtypeStruct(s, BF16) for s in full_shapes]
        + [jax.ShapeDtypeStruct((N_CHIPS,) + ln_shard.shape, F32)],
        scratch_shapes=[pltpu.VMEM(w.shape, BF16) for w in shards]
        + [pltpu.SemaphoreType.DMA((6 * n_w,)), pltpu.SemaphoreType.DMA((6 * n_w,)),
           pltpu.SemaphoreType.DMA((n_w,)), pltpu.SemaphoreType.DMA((3,)), pltpu.SemaphoreType.DMA((3,))],
        compiler_params=pltpu.CompilerParams(vmem_limit_bytes=VMEM_LIMIT),
    )(*shards, ln_shard)


def _reduce_grads(pieces, small):
    n_w = len(pieces)
    n_peer = N_DEV - 1

    def body(*refs):
        g_refs = refs[:n_w]
        small_ref = refs[n_w]
        out_refs = refs[n_w + 1:2 * n_w + 1]
        small_out = refs[2 * n_w + 1]
        land = refs[2 * n_w + 2:3 * n_w + 2]
        small_land = refs[3 * n_w + 2]
        send_sems, recv_sems, local_sems, swap_send, swap_recv = refs[3 * n_w + 3:]
        x, y, c = _place()
        me = 4 * x + 2 * y + c

        def peer(k):
            return (x + (k >> 2)) % 2, (y + ((k >> 1) & 1)) % 2, (c + (k & 1)) % 2

        def piece_copy(w, k):
            px, py, pc = peer(k)
            idx = w * n_peer + k - 1
            return pltpu.make_async_remote_copy(
                src_ref=g_refs[w].at[4 * px + 2 * py + pc], dst_ref=land[w].at[me],
                send_sem=send_sems.at[idx], recv_sem=recv_sems.at[idx],
                device_id=(px, py, pc), device_id_type=MESH)

        def small_copy(k):
            idx = n_w * n_peer + k - 1
            return pltpu.make_async_remote_copy(
                src_ref=small_ref, dst_ref=small_land.at[me],
                send_sem=send_sems.at[idx], recv_sem=recv_sems.at[idx],
                device_id=peer(k), device_id_type=MESH)

        def swap_copy(w, half):
            rows = pieces[w].shape[1]
            win = out_refs[w].at[pl.ds(pl.multiple_of(half * rows, rows), rows), :]
            return pltpu.make_async_remote_copy(
                src_ref=win, dst_ref=win, send_sem=swap_send.at[w], recv_sem=swap_recv.at[w],
                device_id=(x, y, 1 - c), device_id_type=MESH)

        own = [pltpu.make_async_copy(g_refs[w].at[me], land[w].at[me], local_sems.at[w]) for w in range(n_w)]
        for cp in own:
            cp.start()
        small_land[me] = small_ref[...]
        sent = [piece_copy(w, k) for w in range(n_w) for k in range(1, N_DEV)]
        sent += [small_copy(k) for k in range(1, N_DEV)]
        for cp in sent:
            cp.start()
        swaps = []
        for w in range(n_w):
            own[w].wait()
            for k in range(1, N_DEV):
                piece_copy(w, k).wait_recv()
            rows = pieces[w].shape[1]
            total = land[w][0].astype(F32)
            for p in range(1, N_DEV):
                total += land[w][p].astype(F32)
            out_refs[w][pl.ds(pl.multiple_of(c * rows, rows), rows), :] = total
            sw = swap_copy(w, c)
            sw.start()
            swaps.append(sw)
        for k in range(1, N_DEV):
            small_copy(k).wait_recv()
        total = small_land[0]
        for p in range(1, N_DEV):
            total += small_land[p]
        small_out[...] = total
        for w in range(n_w):
            swap_copy(w, 1 - c).wait_recv()
        for cp in sent + swaps:
            cp.wait_send()

    vmem = pl.BlockSpec(memory_space=pltpu.VMEM)
    hbm = pl.BlockSpec(memory_space=pl.ANY)
    n_rdma = (n_w + 1) * n_peer
    return pl.pallas_call(
        body, name="reduce_grads",
        in_specs=[hbm] * n_w + [vmem],
        out_specs=[vmem] * (n_w + 1),
        out_shape=[jax.ShapeDtypeStruct((2 * p.shape[1], p.shape[2]), F32) for p in pieces]
        + [jax.ShapeDtypeStruct(small.shape, F32)],
        scratch_shapes=[pltpu.VMEM(p.shape, BF16) for p in pieces]
        + [pltpu.VMEM((N_DEV,) + small.shape, F32),
           pltpu.SemaphoreType.DMA((n_rdma,)), pltpu.SemaphoreType.DMA((n_rdma,)),
           pltpu.SemaphoreType.DMA((n_w,)), pltpu.SemaphoreType.DMA((n_w,)), pltpu.SemaphoreType.DMA((n_w,))],
        compiler_params=pltpu.CompilerParams(vmem_limit_bytes=VMEM_LIMIT),
    )(*pieces, small)


def _adamw(label, w, g, m, v):
    shape = w.shape
    cols = shape[-1]
    rows = w.size // cols
    args = [a.reshape(rows, cols) for a in (w, g, m, v)]
    c1 = 1.0 - ADAM_B1 ** ADAM_STEP
    c2 = 1.0 - ADAM_B2 ** ADAM_STEP

    def body(w_ref, g_ref, m_ref, v_ref, d_ref, nm_ref, nv_ref):
        g_t = g_ref[...]
        nm = ADAM_B1 * m_ref[...] + (1.0 - ADAM_B1) * g_t
        nv = ADAM_B2 * v_ref[...] + (1.0 - ADAM_B2) * (g_t * g_t)
        d_ref[...] = -ADAM_LR * ((nm / c1) / (jnp.sqrt(nv / c2) + ADAM_EPS) + ADAM_WD * w_ref[...])
        nm_ref[...] = nm
        nv_ref[...] = nv

    block_rows = 256 if rows % 256 == 0 and rows > 256 else rows
    spec = pl.BlockSpec((block_rows, cols), lambda i: (i, 0))
    outs = pl.pallas_call(
        body, name="adamw_" + label, grid=(rows // block_rows,),
        in_specs=[spec] * 4, out_specs=[spec] * 3,
        out_shape=[jax.ShapeDtypeStruct((rows, cols), F32)] * 3,
        compiler_params=_params(),
    )(*args)
    return [o.reshape(shape) for o in outs]


def _local_step(x, tgt, w_in_a, w_out_a, w_kv, w_in_b, w_out_b, sgu_ln_g, sgu_ln_b, w_spatial, b_spatial,
                attn_sinks, rel_bias, post_ln_g, post_ln_b):
    bsp_t = b_spatial.T
    g1, b1 = post_ln_g[0:1], post_ln_b[0:1]
    g2, b2 = post_ln_g[1:2], post_ln_b[1:2]
    onehot = _bucket_onehot()
    bias = _bias_expand(rel_bias.T, onehot).reshape(N_HEADS, 2 * CHUNK, CHUNK)
    win = _window_tables()

    xb, u, vh, z, rv, xh1, rstd1 = _layer_a_fwd(x, w_in_a, w_out_a, sgu_ln_g, sgu_ln_b, w_spatial, bsp_t)
    q, zb, kd, vd = _layer_b_proj(xh1, g1, b1, w_in_b, w_kv)
    o, dr2, loss_vec, dg2, db2 = _layer_b_fwd(q, zb, kd, vd, bias, win, attn_sinks, xh1, g1, b1, w_out_b, g2, b2,
                                              tgt)
    dq, dzb, dkd, dvd, carry_k, carry_v, gw_out_b, dsink, dbias = _layer_b_bwd_attn(
        dr2, zb, o, q, kd, vd, bias, win, attn_sinks, w_out_b)
    dr1, dg1, db1, gw_in_b, gw_kv = _layer_b_bwd_proj(xh1, rstd1, g1, b1, dr2, dq, dzb, dkd, dvd, carry_k, carry_v,
                                                      w_in_b, w_kv)
    dp, gw_out_a, dws, dbsp, dgs, dbs = _layer_a_bwd_mix(dr1, u, vh, z, rv, w_out_a, sgu_ln_g, sgu_ln_b,
                                                         w_spatial, bsp_t)
    grad_x = _layer_a_bwd_dx(dr1, dp, w_in_a)
    gw_in_a = _layer_a_bwd_win(xb, dp)
    drel = _bias_reduce(onehot, dbias.reshape(N_HEADS, -1))

    loss = (0.5 / D_MODEL) * jnp.sum(loss_vec)
    pieces = [gw_in_a.reshape(N_DEV, D_MODEL // 2, -1), gw_out_a.reshape(N_DEV, -1, D_MODEL),
              gw_kv.reshape(N_DEV, -1, 2 * PAIR), gw_in_b, gw_out_b.reshape(N_DEV, -1, D_MODEL)]
    dsink = jnp.sum(dsink.reshape(N_HEADS, CHUNK), axis=1).reshape(1, N_HEADS)
    small = dict(w_spatial=dws, b_spatial=dbsp.reshape(A_GROUPS, CHUNK), attn_sinks=dsink,
                 rel_bias=drel, post_ln_g=jnp.concatenate([dg1, dg2], axis=0),
                 post_ln_b=jnp.concatenate([db1, db2], axis=0), sgu_ln_g=dgs, sgu_ln_b=dbs)
    return loss, grad_x, pieces, small


_SMALL_ORDER = ("w_spatial", "b_spatial", "attn_sinks", "rel_bias", "post_ln_g", "post_ln_b", "sgu_ln_g", "sgu_ln_b")
_LANES = 128
_SUBLANES = 8


def _pack_small(small):
    parts, layout = [], {}
    row = 0
    for name in _SMALL_ORDER:
        flat = small[name].reshape(-1)
        n_rows = -(-flat.size // _LANES)
        n_rows = -(-n_rows // _SUBLANES) * _SUBLANES
        flat = jnp.pad(flat, (0, n_rows * _LANES - flat.size))
        parts.append(flat.reshape(n_rows, _LANES))
        layout[name] = (row, flat.size, small[name].shape)
        row += n_rows
    return jnp.concatenate(parts, axis=0), layout


def _unpack_small(packed, layout, sizes):
    out = {}
    for name in _SMALL_ORDER:
        row, padded, shape = layout[name]
        n = sizes[name]
        out[name] = packed[row:row + padded // _LANES].reshape(-1)[:n].reshape(shape)
    return out


def kernel(x, w_in_a, sgu_ln_g, sgu_ln_b, w_spatial, b_spatial, w_out_a, w_kv, w_in_b, attn_sinks, rel_bias, w_out_b, post_ln_g, post_ln_b, loss_target, m_w_in_a, m_sgu_ln_g, m_sgu_ln_b, m_w_spatial, m_b_spatial, m_w_out_a, m_w_kv, m_w_in_b, m_attn_sinks, m_rel_bias, m_w_out_b, m_post_ln_g, m_post_ln_b, v_w_in_a, v_sgu_ln_g, v_sgu_ln_b, v_w_spatial, v_b_spatial, v_w_out_a, v_w_kv, v_w_in_b, v_attn_sinks, v_rel_bias, v_w_out_b, v_post_ln_g, v_post_ln_b):
    weights = dict(w_in_a=w_in_a, sgu_ln_g=sgu_ln_g, sgu_ln_b=sgu_ln_b, w_spatial=w_spatial, b_spatial=b_spatial,
                   w_out_a=w_out_a, w_kv=w_kv, w_in_b=w_in_b, attn_sinks=attn_sinks, rel_bias=rel_bias,
                   w_out_b=w_out_b, post_ln_g=post_ln_g, post_ln_b=post_ln_b)
    moments_m = dict(w_in_a=m_w_in_a, sgu_ln_g=m_sgu_ln_g, sgu_ln_b=m_sgu_ln_b, w_spatial=m_w_spatial,
                     b_spatial=m_b_spatial, w_out_a=m_w_out_a, w_kv=m_w_kv, w_in_b=m_w_in_b,
                     attn_sinks=m_attn_sinks, rel_bias=m_rel_bias, w_out_b=m_w_out_b, post_ln_g=m_post_ln_g,
                     post_ln_b=m_post_ln_b)
    moments_v = dict(w_in_a=v_w_in_a, sgu_ln_g=v_sgu_ln_g, sgu_ln_b=v_sgu_ln_b, w_spatial=v_w_spatial,
                     b_spatial=v_b_spatial, w_out_a=v_w_out_a, w_kv=v_w_kv, w_in_b=v_w_in_b,
                     attn_sinks=v_attn_sinks, rel_bias=v_rel_bias, w_out_b=v_w_out_b, post_ln_g=v_post_ln_g,
                     post_ln_b=v_post_ln_b)
    order = ("w_in_a", "sgu_ln_g", "sgu_ln_b", "w_spatial", "b_spatial", "w_out_a", "w_kv", "w_in_b", "attn_sinks",
             "rel_bias", "w_out_b", "post_ln_g", "post_ln_b")
    large = ("w_in_a", "w_out_a", "w_kv", "w_in_b", "w_out_b")

    shard_index = 2 * lax.axis_index("x") + lax.axis_index("y")
    ln_shard = jnp.concatenate([sgu_ln_g, sgu_ln_b], axis=0)
    *full, ln_full = _gather_weights([w_in_a[0], w_out_a[0], w_kv, w_in_b[0], w_out_b[0]],
                                     [True, False, False, True, False], ln_shard)
    ln_full = jnp.transpose(ln_full, (1, 0, 2)).reshape(2, A_WIDTH)

    loss_part, grad_x, pieces, small = _local_step(
        x[0], loss_target[0], *full, ln_full[0:1], ln_full[1:2], w_spatial[0], b_spatial[0], attn_sinks,
        rel_bias, post_ln_g, post_ln_b)
    loss = lax.psum(loss_part, ("x", "y", "c"))

    sizes = {name: small[name].size for name in _SMALL_ORDER}
    packed, layout = _pack_small(small)
    *grads_large, packed_sum = _reduce_grads(pieces, packed)
    small_sum = _unpack_small(packed_sum, layout, sizes)

    grads = {}
    for name, g in zip(large, grads_large):
        grads[name] = g.reshape(weights[name].shape)
    shard_cols = sgu_ln_g.shape[1]
    for name in ("sgu_ln_g", "sgu_ln_b"):
        grads[name] = lax.dynamic_slice(small_sum[name], (0, shard_index * shard_cols), (1, shard_cols))
    for name in ("w_spatial", "b_spatial", "attn_sinks", "rel_bias", "post_ln_g", "post_ln_b"):
        grads[name] = small_sum[name].reshape(weights[name].shape)

    deltas, new_m, new_v = {}, {}, {}
    for name in order:
        deltas[name], new_m[name], new_v[name] = _adamw(name, weights[name], grads[name], moments_m[name], moments_v[name])
    return (loss, grad_x[None], *[grads[n] for n in order], *[deltas[n] for n in order],
            *[new_m[n] for n in order], *[new_v[n] for n in order])
```

```python
import math

import jax
import jax.numpy as jnp
from jax import lax
from jax.experimental import pallas as pl
from jax.experimental.pallas import tpu as pltpu

F32 = jnp.float32
BF16 = jnp.bfloat16

D_MODEL = 1024
A_WIDTH = 2048
A_GROUPS = 8
A_GROUP_DIM = 256
CHUNK = 128
N_HEADS = 16
N_KV = 2
HEAD_DIM = 64
PAIR = 2 * HEAD_DIM
B_WIDTH = 1024
REL_BUCKETS = 32
ALPHA = 4.0 ** 0.25
LN_EPS = 1e-5
NEG_INF = -1e30
SCALE = HEAD_DIM ** -0.5

ADAM_LR = 0.001
ADAM_B1 = 0.9
ADAM_B2 = 0.999
ADAM_EPS = 1e-08
ADAM_WD = 0.01
ADAM_STEP = 10

N_DEV = 8
N_CHIPS = 4
MESH = pl.DeviceIdType.MESH
VMEM_LIMIT = 56 * 1024 * 1024

TM_ATTN = 256
TM_MM = 512


def _dot(a, b):
    return jnp.dot(a, b, preferred_element_type=F32)


def _dot_nt(a, b):
    return lax.dot_general(a, b, (((1,), (1,)), ((), ())), preferred_element_type=F32)


def _dot_tn(a, b):
    return lax.dot_general(a, b, (((0,), (0,)), ((), ())), preferred_element_type=F32)


def _ln_fwd(r):
    mu = jnp.mean(r, axis=-1, keepdims=True)
    rc = r - mu
    var = jnp.mean(rc * rc, axis=-1, keepdims=True)
    rstd = lax.rsqrt(var + LN_EPS)
    return rc * rstd, rstd


def _ln_bwd(dxh, xh, rstd):
    m1 = jnp.mean(dxh, axis=-1, keepdims=True)
    m2 = jnp.mean(dxh * xh, axis=-1, keepdims=True)
    return rstd * (dxh - m1 - xh * m2)


def _silu_parts(z):
    sg = jax.nn.sigmoid(z)
    return z * sg, sg * (1.0 + z * (1.0 - sg))


def _dup_halves(blk):
    sw = pltpu.roll(blk, HEAD_DIM, 1)
    lo = lax.broadcasted_iota(jnp.int32, blk.shape, 1) < HEAD_DIM
    return jnp.where(lo, blk, sw), jnp.where(lo, sw, blk)


def _fold_halves(blk):
    return blk + pltpu.roll(blk, HEAD_DIM, 1)


def _resident(shape):
    nd = len(shape)
    return pl.BlockSpec(shape, lambda *_: (0,) * nd, pipeline_mode=pl.Buffered(1))


def _const(shape):
    nd = len(shape)
    return pl.BlockSpec(shape, lambda *_: (0,) * nd)


def _rows(tm, cols):
    return pl.BlockSpec((tm, cols), lambda i: (i, 0))


def _params(sem=("arbitrary",)):
    return pltpu.CompilerParams(dimension_semantics=sem, vmem_limit_bytes=VMEM_LIMIT)


def _spatial_mix(ws_ref, bsp_ref, vn, s_scr, n_chunks):
    tri = (lax.broadcasted_iota(jnp.int32, (CHUNK, CHUNK), 0)
           >= lax.broadcasted_iota(jnp.int32, (CHUNK, CHUNK), 1))
    for g in range(A_GROUPS):
        wsg = jnp.where(tri, ws_ref[g], 0.0).astype(BF16)
        cols = slice(g * A_GROUP_DIM, (g + 1) * A_GROUP_DIM)
        for ci in range(n_chunks):
            rows = slice(ci * CHUNK, (ci + 1) * CHUNK)
            s_scr[rows, cols] = _dot(wsg, vn[rows, cols]) + bsp_ref[:, g:g + 1]


def _layer_a_fwd(x, w_in, w_out, lng, lnb, ws, bsp_t):
    t_len = x.shape[0]
    tm = TM_ATTN

    def body(x_ref, win_ref, wout_ref, lng_ref, lnb_ref, ws_ref, bsp_ref,
             xb_ref, u_ref, vh_ref, z_ref, rv_ref, xh_ref, r1_ref, s_scr):
        x_t = x_ref[...]
        xb = x_t.astype(BF16)
        xb_ref[...] = xb
        u = _dot(xb, win_ref[:, 0:A_WIDTH])
        v = _dot(xb, win_ref[:, A_WIDTH:2 * A_WIDTH])
        z = _dot(xb, win_ref[:, 2 * A_WIDTH:3 * A_WIDTH])
        vh, rv = _ln_fwd(v)
        vn = (vh * lng_ref[...] + lnb_ref[...]).astype(BF16)
        _spatial_mix(ws_ref, bsp_ref, vn, s_scr, tm // CHUNK)
        sz, _ = _silu_parts(z)
        y = (u * s_scr[...] * sz).astype(BF16)
        r = ALPHA * x_t + _dot(y, wout_ref[...])
        xh, r1 = _ln_fwd(r)
        u_ref[...] = u.astype(BF16)
        vh_ref[...] = vh.astype(BF16)
        z_ref[...] = z.astype(BF16)
        rv_ref[...] = rv
        xh_ref[...] = xh
        r1_ref[...] = r1

    return pl.pallas_call(
        body, name="layer_a_fwd", grid=(t_len // tm,),
        in_specs=[_rows(tm, D_MODEL), _resident(w_in.shape), _resident(w_out.shape), _const(lng.shape),
                  _const(lnb.shape), _const(ws.shape), _const(bsp_t.shape)],
        out_specs=[_rows(tm, D_MODEL), _rows(tm, A_WIDTH), _rows(tm, A_WIDTH), _rows(tm, A_WIDTH),
                   _rows(tm, 1), _rows(tm, D_MODEL), _rows(tm, 1)],
        out_shape=[jax.ShapeDtypeStruct((t_len, D_MODEL), BF16), jax.ShapeDtypeStruct((t_len, A_WIDTH), BF16),
                   jax.ShapeDtypeStruct((t_len, A_WIDTH), BF16), jax.ShapeDtypeStruct((t_len, A_WIDTH), BF16),
                   jax.ShapeDtypeStruct((t_len, 1), F32), jax.ShapeDtypeStruct((t_len, D_MODEL), F32),
                   jax.ShapeDtypeStruct((t_len, 1), F32)],
        scratch_shapes=[pltpu.VMEM((tm, A_WIDTH), F32)],
        compiler_params=_params(),
    )(x, w_in, w_out, lng, lnb, ws, bsp_t)


def _layer_b_proj(xh1, g1, b1, w_in, w_kv):
    t_len = xh1.shape[0]
    tm = TM_MM

    def body(xh_ref, g_ref, b_ref, win_ref, wkv_ref, q_ref, z_ref, kd_ref, vd_ref):
        h1 = (xh_ref[...] * g_ref[...] + b_ref[...]).astype(BF16)
        q_ref[...] = _dot(h1, win_ref[:, 0:B_WIDTH]).astype(BF16)
        z_ref[...] = _dot(h1, win_ref[:, B_WIDTH:2 * B_WIDTH]).astype(BF16)
        kv = _dot(h1, wkv_ref[...])
        k0, k1 = _dup_halves(kv[:, 0:PAIR])
        v0, v1 = _dup_halves(kv[:, PAIR:2 * PAIR])
        kd_ref[:, 0:PAIR] = k0.astype(BF16)
        kd_ref[:, PAIR:2 * PAIR] = k1.astype(BF16)
        vd_ref[:, 0:PAIR] = v0.astype(BF16)
        vd_ref[:, PAIR:2 * PAIR] = v1.astype(BF16)

    return pl.pallas_call(
        body, name="layer_b_proj", grid=(t_len // tm,),
        in_specs=[_rows(tm, D_MODEL), _const(g1.shape), _const(b1.shape), _resident(w_in.shape),
                  _resident(w_kv.shape)],
        out_specs=[_rows(tm, B_WIDTH), _rows(tm, B_WIDTH), _rows(tm, 2 * PAIR), _rows(tm, 2 * PAIR)],
        out_shape=[jax.ShapeDtypeStruct((t_len, B_WIDTH), BF16), jax.ShapeDtypeStruct((t_len, B_WIDTH), BF16),
                   jax.ShapeDtypeStruct((t_len, 2 * PAIR), BF16), jax.ShapeDtypeStruct((t_len, 2 * PAIR), BF16)],
        compiler_params=_params(),
    )(xh1, g1, b1, w_in, w_kv)


GROUP = N_HEADS // N_KV
GROUP_Q = GROUP * CHUNK


def _window_tables():
    j = jnp.arange(2 * CHUNK, dtype=jnp.int32)[:, None]
    t = jnp.arange(CHUNK, dtype=jnp.int32)[None, :]
    dist = t + CHUNK - j
    inside = (dist >= 0) & (dist < CHUNK)
    return jnp.stack([inside & (j >= CHUNK), inside]).astype(F32)


def _band(ref, chunk_index, kvh):
    prev0 = pl.multiple_of(jnp.maximum(chunk_index - 1, 0) * CHUNK, CHUNK)
    cur0 = pl.multiple_of(chunk_index * CHUNK, CHUNK)
    cols = slice(kvh * PAIR, (kvh + 1) * PAIR)
    return jnp.concatenate([ref[pl.ds(prev0, CHUNK), cols], ref[pl.ds(cur0, CHUNK), cols]], axis=0)


def _group_tables(bias_ref, win_ref, sink_ref, chunk_index, kvh):
    bias = jnp.concatenate([bias_ref[kvh * GROUP + j] for j in range(GROUP)], axis=1)
    win = win_ref[jnp.minimum(chunk_index, 1)]
    mask = jnp.concatenate([win] * GROUP, axis=1) > 0.5
    sink = jnp.concatenate([jnp.full((1, CHUNK), sink_ref[0, kvh * GROUP + j], F32) for j in range(GROUP)], axis=1)
    return bias, mask, sink


def _attn_probs(qs, kband, bias, mask, sink):
    logits = jnp.where(mask, _dot_nt(kband, qs) * SCALE + bias, NEG_INF)
    m = jnp.maximum(jnp.max(logits, axis=0, keepdims=True), sink)
    e = jnp.exp(logits - m)
    es = jnp.exp(sink - m)
    inv = 1.0 / (jnp.sum(e, axis=0, keepdims=True) + es)
    return e * inv, es * inv


def _half_mask():
    return lax.broadcasted_iota(jnp.int32, (CHUNK, PAIR), 1) < HEAD_DIM


def _stack_heads(src_ref, rows, kvh, dst_scr, lo):
    for j in range(GROUP):
        h = kvh * GROUP + j
        blk = src_ref[rows, (h // 2) * PAIR:(h // 2 + 1) * PAIR].astype(F32)
        keep = lo if h % 2 == 0 else ~lo
        dst_scr[j * CHUNK:(j + 1) * CHUNK, :] = jnp.where(keep, blk, 0.0).astype(BF16)


def _unstack_pairs(stacked, pp, lo):
    return jnp.where(lo, stacked[(2 * pp) * CHUNK:(2 * pp + 1) * CHUNK], stacked[(2 * pp + 1) * CHUNK:(2 * pp + 2) * CHUNK])


def _layer_b_fwd(q, zb, kd, vd, bias, win, sinks, xh1, g1, b1, w_out, g2, b2, tgt):
    t_len = q.shape[0]
    tm = TM_ATTN

    def body(q_ref, z_ref, kd_ref, vd_ref, bias_ref, win_ref, sink_ref, xh_ref, g1_ref, b1_ref, wout_ref, g2_ref,
             b2_ref, tgt_ref, o_ref, dr_ref, loss_ref, dg_ref, db_ref, o_scr, qs_scr):
        i = pl.program_id(0)

        @pl.when(i == 0)
        def _():
            loss_ref[...] = jnp.zeros_like(loss_ref)
            dg_ref[...] = jnp.zeros_like(dg_ref)
            db_ref[...] = jnp.zeros_like(db_ref)

        lo = _half_mask()
        for ci in range(tm // CHUNK):
            cg = i * (tm // CHUNK) + ci
            rows = slice(ci * CHUNK, (ci + 1) * CHUNK)
            for kvh in range(N_KV):
                kband = _band(kd_ref, cg, kvh)
                vband = _band(vd_ref, cg, kvh)
                bias_g, mask, sink = _group_tables(bias_ref, win_ref, sink_ref, cg, kvh)
                _stack_heads(q_ref, rows, kvh, qs_scr, lo)
                p, _ = _attn_probs(qs_scr[...], kband, bias_g, mask, sink)
                o_stack = _dot_tn(p.astype(BF16), vband)
                for pp in range(GROUP // 2):
                    pair = kvh * (GROUP // 2) + pp
                    o_scr[rows, pair * PAIR:(pair + 1) * PAIR] = _unstack_pairs(o_stack, pp, lo)
        o = o_scr[...]
        o_ref[...] = o.astype(BF16)
        sz, _ = _silu_parts(z_ref[...].astype(F32))
        y = (o * sz).astype(BF16)
        h1 = xh_ref[...] * g1_ref[...] + b1_ref[...]
        r = ALPHA * h1 + _dot(y, wout_ref[...])
        xh2, rstd2 = _ln_fwd(r)
        diff = xh2 * g2_ref[...] + b2_ref[...] - tgt_ref[...]
        loss_ref[...] += jnp.sum(diff * diff, axis=0, keepdims=True)
        dh2 = diff * (1.0 / D_MODEL)
        dg_ref[...] += jnp.sum(dh2 * xh2, axis=0, keepdims=True)
        db_ref[...] += jnp.sum(dh2, axis=0, keepdims=True)
        dr_ref[...] = _ln_bwd(dh2 * g2_ref[...], xh2, rstd2)

    vec = jax.ShapeDtypeStruct((1, D_MODEL), F32)
    return pl.pallas_call(
        body, name="layer_b_fwd", grid=(t_len // tm,),
        in_specs=[_rows(tm, B_WIDTH), _rows(tm, B_WIDTH), _resident(kd.shape), _resident(vd.shape),
                  _resident(bias.shape), _resident(win.shape), pl.BlockSpec(memory_space=pltpu.SMEM),
                  _rows(tm, D_MODEL), _const(g1.shape), _const(b1.shape), _resident(w_out.shape), _const(g2.shape),
                  _const(b2.shape), _rows(tm, D_MODEL)],
        out_specs=[_rows(tm, B_WIDTH), _rows(tm, D_MODEL), _const((1, D_MODEL)), _const((1, D_MODEL)),
                   _const((1, D_MODEL))],
        out_shape=[jax.ShapeDtypeStruct((t_len, B_WIDTH), BF16), jax.ShapeDtypeStruct((t_len, D_MODEL), F32),
                   vec, vec, vec],
        scratch_shapes=[pltpu.VMEM((tm, B_WIDTH), F32), pltpu.VMEM((GROUP_Q, PAIR), BF16)],
        compiler_params=_params(),
    )(q, zb, kd, vd, bias, win, sinks, xh1, g1, b1, w_out, g2, b2, tgt)


def _layer_b_bwd_attn(dr2, zb, o, q, kd, vd, bias, win, sinks, w_out):
    t_len = q.shape[0]
    tm = TM_ATTN
    n_steps = t_len // tm
    n_chunks = tm // CHUNK

    def body(dr_ref, z_ref, o_ref, q_ref, kd_ref, vd_ref, bias_ref, win_ref, sink_ref, wout_ref,
             dq_ref, dz_ref, dkd_ref, dvd_ref, ck_ref, cv_ref, gw_ref, dsink_ref, dbias_ref,
             do_scr, qs_scr, dos_scr, gw_acc):
        i = pl.program_id(0)

        @pl.when(i == 0)
        def _():
            gw_acc[...] = jnp.zeros_like(gw_acc)
            dsink_ref[...] = jnp.zeros_like(dsink_ref)
            dbias_ref[...] = jnp.zeros_like(dbias_ref)

        drb = dr_ref[...].astype(BF16)
        dy = _dot_nt(drb, wout_ref[...])
        z = z_ref[...].astype(F32)
        sz, dsz = _silu_parts(z)
        o_t = o_ref[...].astype(F32)
        dz_ref[...] = (dy * o_t * dsz).astype(BF16)
        do_scr[...] = (dy * sz).astype(BF16)
        gw_acc[...] += _dot_tn((o_t * sz).astype(BF16), drb)

        lo = _half_mask()
        for kvh in range(N_KV):
            kcols = slice(kvh * PAIR, (kvh + 1) * PAIR)
            dk_bands, dv_bands = [], []
            for ci in range(n_chunks):
                cg = i * n_chunks + ci
                rows = slice(ci * CHUNK, (ci + 1) * CHUNK)
                kband = _band(kd_ref, cg, kvh)
                vband = _band(vd_ref, cg, kvh)
                bias_g, mask, sink = _group_tables(bias_ref, win_ref, sink_ref, cg, kvh)
                _stack_heads(q_ref, rows, kvh, qs_scr, lo)
                _stack_heads(do_scr, rows, kvh, dos_scr, lo)
                qs = qs_scr[...]
                dos = dos_scr[...]
                p, p_sink = _attn_probs(qs, kband, bias_g, mask, sink)
                dp = _dot_nt(vband, dos)
                delta = jnp.sum(p * dp, axis=0, keepdims=True)
                dlog = p * (dp - delta)
                for j in range(GROUP):
                    dbias_ref[kvh * GROUP + j] += dlog[:, j * CHUNK:(j + 1) * CHUNK]
                dsink_ref[kvh:kvh + 1, :] += -(p_sink * delta)
                ds = (dlog * SCALE).astype(BF16)
                dq_stack = _dot_tn(ds, kband)
                for pp in range(GROUP // 2):
                    pair = kvh * (GROUP // 2) + pp
                    dq_ref[rows, pair * PAIR:(pair + 1) * PAIR] = _unstack_pairs(dq_stack, pp, lo).astype(BF16)
                dk_bands.append(_dot(ds, qs))
                dv_bands.append(_dot(p.astype(BF16), dos))
            for bands, out_ref, carry_ref in ((dk_bands, dkd_ref, ck_ref), (dv_bands, dvd_ref, cv_ref)):
                carry_ref[0, :, kcols] = bands[0][0:CHUNK]
                for ci in range(n_chunks):
                    own = bands[ci][CHUNK:2 * CHUNK]
                    if ci + 1 < n_chunks:
                        own = own + bands[ci + 1][0:CHUNK]
                    out_ref[ci * CHUNK:(ci + 1) * CHUNK, kcols] = own

        @pl.when(i == n_steps - 1)
        def _():
            gw_ref[...] = gw_acc[...].astype(BF16)

    carry_spec = pl.BlockSpec((1, CHUNK, 2 * PAIR), lambda i: (i, 0, 0))
    carry_shape = jax.ShapeDtypeStruct((n_steps, CHUNK, 2 * PAIR), F32)
    return pl.pallas_call(
        body, name="layer_b_bwd_attn", grid=(n_steps,),
        in_specs=[_rows(tm, D_MODEL), _rows(tm, B_WIDTH), _rows(tm, B_WIDTH), _rows(tm, B_WIDTH),
                  _resident(kd.shape), _resident(vd.shape), _resident(bias.shape), _resident(win.shape),
                  pl.BlockSpec(memory_space=pltpu.SMEM), _resident(w_out.shape)],
        out_specs=[_rows(tm, B_WIDTH), _rows(tm, B_WIDTH), _rows(tm, 2 * PAIR), _rows(tm, 2 * PAIR),
                   carry_spec, carry_spec, _const(w_out.shape), _const((N_KV, GROUP_Q)), _const(bias.shape)],
        out_shape=[jax.ShapeDtypeStruct((t_len, B_WIDTH), BF16), jax.ShapeDtypeStruct((t_len, B_WIDTH), BF16),
                   jax.ShapeDtypeStruct((t_len, 2 * PAIR), F32), jax.ShapeDtypeStruct((t_len, 2 * PAIR), F32),
                   carry_shape, carry_shape, jax.ShapeDtypeStruct(w_out.shape, BF16),
                   jax.ShapeDtypeStruct((N_KV, GROUP_Q), F32), jax.ShapeDtypeStruct(bias.shape, F32)],
        scratch_shapes=[pltpu.VMEM((tm, B_WIDTH), BF16), pltpu.VMEM((GROUP_Q, PAIR), BF16),
                        pltpu.VMEM((GROUP_Q, PAIR), BF16), pltpu.VMEM(w_out.shape, F32)],
        compiler_params=_params(),
    )(dr2, zb, o, q, kd, vd, bias, win, sinks, w_out)


def _layer_b_bwd_proj(xh1, rstd1, g1, b1, dr2, dq, dzb, dkd, dvd, carry_k, carry_v, w_in, w_kv):
    t_len = xh1.shape[0]
    tm = TM_MM
    n_steps = t_len // tm
    per_tile = tm // TM_ATTN
    n_carry = carry_k.shape[0]

    def body(xh_ref, rstd_ref, g_ref, b_ref, dr2_ref, dq_ref, dz_ref, dkd_ref, dvd_ref, *rest):
        carry_refs = rest[:2 * per_tile]
        win_ref, wkv_ref, dr1_ref, dg_ref, db_ref, gwin_ref, gwkv_ref, acc_in, acc_kv = rest[2 * per_tile:]
        i = pl.program_id(0)

        @pl.when(i == 0)
        def _():
            acc_in[...] = jnp.zeros_like(acc_in)
            acc_kv[...] = jnp.zeros_like(acc_kv)
            dg_ref[...] = jnp.zeros_like(dg_ref)
            db_ref[...] = jnp.zeros_like(db_ref)

        def with_carries(tile_ref, refs):
            parts = []
            for a in range(per_tile):
                parts.append(tile_ref[a * TM_ATTN:(a + 1) * TM_ATTN - CHUNK, :])
                carry = refs[a][0]
                if a == per_tile - 1:
                    carry = jnp.where(i < n_steps - 1, carry, 0.0)
                parts.append(tile_ref[(a + 1) * TM_ATTN - CHUNK:(a + 1) * TM_ATTN, :] + carry)
            return jnp.concatenate(parts, axis=0).astype(BF16)

        xh = xh_ref[...]
        h1 = (xh * g_ref[...] + b_ref[...]).astype(BF16)
        dq_t = dq_ref[...]
        dz_t = dz_ref[...]
        dkd_t = with_carries(dkd_ref, carry_refs[:per_tile])
        dvd_t = with_carries(dvd_ref, carry_refs[per_tile:])
        wkv = wkv_ref[...].astype(F32)
        wk0, wk1 = _dup_halves(wkv[:, 0:PAIR])
        wv0, wv1 = _dup_halves(wkv[:, PAIR:2 * PAIR])
        dh1 = ALPHA * dr2_ref[...]
        dh1 += _dot_nt(dq_t, win_ref[:, 0:B_WIDTH])
        dh1 += _dot_nt(dz_t, win_ref[:, B_WIDTH:2 * B_WIDTH])
        for blk, w in ((dkd_t[:, 0:PAIR], wk0), (dkd_t[:, PAIR:2 * PAIR], wk1),
                       (dvd_t[:, 0:PAIR], wv0), (dvd_t[:, PAIR:2 * PAIR], wv1)):
            dh1 += _dot_nt(blk, w.astype(BF16))
        acc_in[:, 0:B_WIDTH] += _dot_tn(h1, dq_t)
        acc_in[:, B_WIDTH:2 * B_WIDTH] += _dot_tn(h1, dz_t)
        acc_kv[:, 0:2 * PAIR] += _dot_tn(h1, dkd_t)
        acc_kv[:, 2 * PAIR:4 * PAIR] += _dot_tn(h1, dvd_t)
        dg_ref[...] += jnp.sum(dh1 * xh, axis=0, keepdims=True)
        db_ref[...] += jnp.sum(dh1, axis=0, keepdims=True)
        dr1_ref[...] = _ln_bwd(dh1 * g_ref[...], xh, rstd_ref[...])

        @pl.when(i == n_steps - 1)
        def _():
            half_rows = D_MODEL // 2
            shard_cols = 2 * B_WIDTH // N_CHIPS
            for s in range(N_CHIPS):
                for c in range(2):
                    gwin_ref[2 * s + c] = acc_in[c * half_rows:(c + 1) * half_rows,
                                                 s * shard_cols:(s + 1) * shard_cols].astype(BF16)
            lo = lax.broadcasted_iota(jnp.int32, (D_MODEL, PAIR), 1) < HEAD_DIM
            for n in range(2):
                f0 = _fold_halves(acc_kv[:, (2 * n) * PAIR:(2 * n + 1) * PAIR])
                f1 = _fold_halves(acc_kv[:, (2 * n + 1) * PAIR:(2 * n + 2) * PAIR])
                gwkv_ref[:, n * PAIR:(n + 1) * PAIR] = jnp.where(lo, f0, f1).astype(BF16)

    vec = jax.ShapeDtypeStruct((1, D_MODEL), F32)
    gwin_shape = (N_DEV, D_MODEL // 2, 2 * B_WIDTH // N_CHIPS)

    def carry_spec(a):
        return pl.BlockSpec((1, CHUNK, 2 * PAIR), lambda i: (jnp.minimum(per_tile * i + a + 1, n_carry - 1), 0, 0))

    carry_specs = [carry_spec(a) for a in range(per_tile)]
    return pl.pallas_call(
        body, name="layer_b_bwd_proj", grid=(n_steps,),
        in_specs=[_rows(tm, D_MODEL), _rows(tm, 1), _const(g1.shape), _const(b1.shape), _rows(tm, D_MODEL),
                  _rows(tm, B_WIDTH), _rows(tm, B_WIDTH), _rows(tm, 2 * PAIR), _rows(tm, 2 * PAIR)]
        + carry_specs + carry_specs + [_resident(w_in.shape), _resident(w_kv.shape)],
        out_specs=[_rows(tm, D_MODEL), _const((1, D_MODEL)), _const((1, D_MODEL)), _const(gwin_shape),
                   _const(w_kv.shape)],
        out_shape=[jax.ShapeDtypeStruct((t_len, D_MODEL), F32), vec, vec,
                   jax.ShapeDtypeStruct(gwin_shape, BF16), jax.ShapeDtypeStruct(w_kv.shape, BF16)],
        scratch_shapes=[pltpu.VMEM(w_in.shape, F32), pltpu.VMEM((D_MODEL, 4 * PAIR), F32)],
        compiler_params=_params(),
    )(xh1, rstd1, g1, b1, dr2, dq, dzb, dkd, dvd, *([carry_k] * per_tile), *([carry_v] * per_tile), w_in, w_kv)


def _layer_a_bwd_mix(dr1, u, vh, z, rv, w_out, lng, lnb, ws, bsp_t, after):
    t_len = u.shape[0]
    tm = TM_ATTN
    n_steps = t_len // tm

    def body(dr_ref, u_ref, vh_ref, z_ref, rv_ref, wout_ref, lng_ref, lnb_ref, ws_ref, bsp_ref, after_ref,
             dp_ref, gw_ref, dws_ref, dbsp_ref, dgs_ref, dbs_ref, s_scr, dvn_scr, gw_acc):
        i = pl.program_id(0)

        @pl.when(i == 0)
        def _():
            gw_acc[...] = jnp.zeros_like(gw_acc)
            dws_ref[...] = jnp.zeros_like(dws_ref)
            dbsp_ref[...] = jnp.zeros_like(dbsp_ref)
            dgs_ref[...] = jnp.zeros_like(dgs_ref)
            dbs_ref[...] = jnp.zeros_like(dbs_ref)

        drb = dr_ref[...].astype(BF16)
        dy = _dot_nt(drb, wout_ref[...])
        u_t = u_ref[...].astype(F32)
        vh_t = vh_ref[...].astype(F32)
        z_t = z_ref[...].astype(F32)
        vn = (vh_t * lng_ref[...] + lnb_ref[...]).astype(BF16)
        _spatial_mix(ws_ref, bsp_ref, vn, s_scr, tm // CHUNK)
        s = s_scr[...]
        sz, dsz = _silu_parts(z_t)
        gw_acc[...] += _dot_tn((u_t * s * sz).astype(BF16), drb)
        dp_ref[:, 0:A_WIDTH] = (dy * s * sz).astype(BF16)
        dp_ref[:, 2 * A_WIDTH:3 * A_WIDTH] = (dy * u_t * s * dsz).astype(BF16)
        ds = dy * u_t * sz

        tri = (lax.broadcasted_iota(jnp.int32, (CHUNK, CHUNK), 0)
               >= lax.broadcasted_iota(jnp.int32, (CHUNK, CHUNK), 1))
        for g in range(A_GROUPS):
            wsg = jnp.where(tri, ws_ref[g], 0.0).astype(BF16)
            cols = slice(g * A_GROUP_DIM, (g + 1) * A_GROUP_DIM)
            dws_g = jnp.zeros((CHUNK, CHUNK), F32)
            dbsp_g = jnp.zeros((CHUNK, 1), F32)
            for ci in range(tm // CHUNK):
                rows = slice(ci * CHUNK, (ci + 1) * CHUNK)
                ds_g = ds[rows, cols]
                ds_b = ds_g.astype(BF16)
                dws_g += _dot_nt(ds_b, vn[rows, cols])
                dbsp_g += jnp.sum(ds_g, axis=-1, keepdims=True)
                dvn_scr[rows, cols] = _dot_tn(wsg, ds_b)
            dws_ref[g] += jnp.where(tri, dws_g, 0.0)
            dbsp_ref[g] += dbsp_g
        dvn = dvn_scr[...]
        dgs_ref[...] += jnp.sum(dvn * vh_t, axis=0, keepdims=True)
        dbs_ref[...] += jnp.sum(dvn, axis=0, keepdims=True)
        dp_ref[:, A_WIDTH:2 * A_WIDTH] = _ln_bwd(dvn * lng_ref[...], vh_t, rv_ref[...]).astype(BF16)

        @pl.when(i == n_steps - 1)
        def _():
            gw_ref[...] = gw_acc[...].astype(BF16)

    wide = jax.ShapeDtypeStruct((1, A_WIDTH), F32)
    return pl.pallas_call(
        body, name="layer_a_bwd_mix", grid=(n_steps,),
        in_specs=[_rows(tm, D_MODEL), _rows(tm, A_WIDTH), _rows(tm, A_WIDTH), _rows(tm, A_WIDTH), _rows(tm, 1),
                  _resident(w_out.shape), _const(lng.shape), _const(lnb.shape), _const(ws.shape),
                  _const(bsp_t.shape), _const(after.shape)],
        out_specs=[_rows(tm, 3 * A_WIDTH), _const(w_out.shape), _const(ws.shape), _const((A_GROUPS, CHUNK, 1)),
                   _const((1, A_WIDTH)), _const((1, A_WIDTH))],
        out_shape=[jax.ShapeDtypeStruct((t_len, 3 * A_WIDTH), BF16), jax.ShapeDtypeStruct(w_out.shape, BF16),
                   jax.ShapeDtypeStruct(ws.shape, F32), jax.ShapeDtypeStruct((A_GROUPS, CHUNK, 1), F32),
                   wide, wide],
        scratch_shapes=[pltpu.VMEM((tm, A_WIDTH), F32), pltpu.VMEM((tm, A_WIDTH), F32),
                        pltpu.VMEM(w_out.shape, F32)],
        compiler_params=_params(),
    )(dr1, u, vh, z, rv, w_out, lng, lnb, ws, bsp_t, after)


def _layer_a_bwd_dx(dr1, dp, w_in, after):
    t_len = dr1.shape[0]
    tm = TM_MM

    def body(dr_ref, dp_ref, win_ref, after_ref, dx_ref):
        dx_ref[...] = ALPHA * dr_ref[...] + _dot_nt(dp_ref[...], win_ref[...])

    return pl.pallas_call(
        body, name="layer_a_bwd_dx", grid=(t_len // tm,),
        in_specs=[_rows(tm, D_MODEL), _rows(tm, 3 * A_WIDTH), _resident(w_in.shape), _const(after.shape)],
        out_specs=_rows(tm, D_MODEL),
        out_shape=jax.ShapeDtypeStruct((t_len, D_MODEL), F32),
        compiler_params=_params(),
    )(dr1, dp, w_in, after)


def _layer_a_bwd_win(xb, dp, after):
    t_len = xb.shape[0]
    tm = TM_MM
    n_steps = t_len // tm
    shard_cols = 3 * A_WIDTH // N_CHIPS
    half_rows = D_MODEL // 2

    def body(xb_ref, dp_ref, after_ref, gw_ref, acc):
        i = pl.program_id(1)

        @pl.when(i == 0)
        def _():
            acc[...] = jnp.zeros_like(acc)

        acc[...] += _dot_tn(xb_ref[...], dp_ref[...])

        @pl.when(i == n_steps - 1)
        def _():
            for c in range(2):
                gw_ref[0, c] = acc[c * half_rows:(c + 1) * half_rows, :].astype(BF16)

    return pl.pallas_call(
        body, name="layer_a_bwd_win", grid=(N_CHIPS, n_steps),
        in_specs=[pl.BlockSpec((tm, D_MODEL), lambda j, i: (i, 0)),
                  pl.BlockSpec((tm, shard_cols), lambda j, i: (i, j)), _const(after.shape)],
        out_specs=pl.BlockSpec((1, 2, half_rows, shard_cols), lambda j, i: (j, 0, 0, 0)),
        out_shape=jax.ShapeDtypeStruct((N_CHIPS, 2, half_rows, shard_cols), BF16),
        scratch_shapes=[pltpu.VMEM((D_MODEL, shard_cols), F32)],
        compiler_params=_params(("arbitrary", "arbitrary")),
    )(xb, dp, after)


def _bucket_onehot():
    t = jnp.arange(CHUNK, dtype=jnp.int32)[None, :]
    j = jnp.arange(2 * CHUNK, dtype=jnp.int32)[:, None]
    dist = jnp.clip(t + CHUNK - j, 0, CHUNK - 1)
    max_exact = REL_BUCKETS // 2
    df = jnp.maximum(dist, 1).astype(F32)
    large = max_exact + (jnp.log(df / max_exact) / math.log(CHUNK / max_exact)
                         * (REL_BUCKETS - max_exact)).astype(jnp.int32)
    bucket = jnp.where(dist < max_exact, dist, jnp.minimum(large, REL_BUCKETS - 1))
    onehot = bucket.reshape(1, -1) == jnp.arange(REL_BUCKETS, dtype=jnp.int32)[:, None]
    return onehot.astype(F32)


def _bias_expand(rel_t, onehot):
    def body(rel_ref, oh_ref, out_ref):
        out_ref[...] = jnp.dot(rel_ref[...], oh_ref[...], preferred_element_type=F32,
                               precision=lax.Precision.HIGHEST)

    return pl.pallas_call(
        body, name="bias_expand",
        out_shape=jax.ShapeDtypeStruct((N_HEADS, onehot.shape[1]), F32),
    )(rel_t, onehot)


def _bias_reduce(onehot, dbias):
    def body(oh_ref, db_ref, out_ref):
        out_ref[...] = lax.dot_general(oh_ref[...], db_ref[...], (((1,), (1,)), ((), ())),
                                       preferred_element_type=F32, precision=lax.Precision.HIGHEST)

    return pl.pallas_call(
        body, name="bias_reduce",
        out_shape=jax.ShapeDtypeStruct((REL_BUCKETS, N_HEADS), F32),
    )(onehot, dbias)


def _place():
    return lax.axis_index("x"), lax.axis_index("y"), lax.axis_index("c")


def _gather_weights(shards, col_sharded, ln_shard):
    n_w = len(shards)
    full_shapes = []
    for w, cs in zip(shards, col_sharded):
        r, c = w.shape
        full_shapes.append((r, c * N_CHIPS) if cs else (r * N_CHIPS, c))

    def body(*refs):
        in_refs = refs[:n_w]
        ln_ref = refs[n_w]
        full_refs = refs[n_w + 1:2 * n_w + 1]
        ln_full = refs[2 * n_w + 1]
        stage = refs[2 * n_w + 2:3 * n_w + 2]
        send_sems, recv_sems, local_sems, ln_send, ln_recv = refs[3 * n_w + 2:]
        x, y, c = _place()
        s_me = 2 * x + y
        chips = [(1 - x, y), (x, 1 - y), (1 - x, 1 - y)]

        def shard_window(w, s, half):
            rows, cols = shards[w].shape
            if col_sharded[w]:
                rsel = pl.ds(0, rows) if half is None else pl.ds(half * (rows // 2), rows // 2)
                return full_refs[w].at[rsel, pl.ds(s * cols, cols)]
            if half is None:
                return full_refs[w].at[pl.ds(s * rows, rows), :]
            return full_refs[w].at[pl.ds(s * rows + half * (rows // 2), rows // 2), :]

        def stage_half(w, half):
            rows = shards[w].shape[0]
            return stage[w].at[pl.ds(half * (rows // 2), rows // 2), :]

        def ici_copy(w, k, sender_shard, src):
            return pltpu.make_async_remote_copy(
                src_ref=src, dst_ref=shard_window(w, sender_shard, c),
                send_sem=send_sems.at[w * 3 + k], recv_sem=recv_sems.at[w * 3 + k],
                device_id=(*chips[k], c), device_id_type=MESH)

        def d2d_copy(w, k, half):
            s_k = 2 * chips[k][0] + chips[k][1]
            win = shard_window(w, s_k, half)
            return pltpu.make_async_remote_copy(
                src_ref=win, dst_ref=win,
                send_sem=send_sems.at[3 * n_w + w * 3 + k], recv_sem=recv_sems.at[3 * n_w + w * 3 + k],
                device_id=(x, y, 1 - c), device_id_type=MESH)

        def ln_copy(k, slot):
            return pltpu.make_async_remote_copy(
                src_ref=ln_ref, dst_ref=ln_full.at[slot], send_sem=ln_send.at[k], recv_sem=ln_recv.at[k],
                device_id=(*chips[k], c), device_id_type=MESH)

        for w in range(n_w):
            stage[w][...] = in_refs[w][...].astype(BF16)
        own = [pltpu.make_async_copy(stage[w], shard_window(w, s_me, None), local_sems.at[w]) for w in range(n_w)]
        for cp in own:
            cp.start()
        ln_full[s_me] = ln_ref[...]
        first = [ici_copy(w, k, s_me, stage_half(w, c)) for w in range(n_w) for k in range(3)]
        first += [ln_copy(k, s_me) for k in range(3)]
        for cp in first:
            cp.start()
        passed = []
        for w in range(n_w):
            for k in range(3):
                s_k = 2 * chips[k][0] + chips[k][1]
                ici_copy(w, k, s_k, stage_half(w, c)).wait_recv()
                fwd = d2d_copy(w, k, c)
                fwd.start()
                passed.append(fwd)
        for w in range(n_w):
            for k in range(3):
                d2d_copy(w, k, 1 - c).wait_recv()
        for k in range(3):
            ln_copy(k, 2 * chips[k][0] + chips[k][1]).wait_recv()
        for cp in first + passed:
            cp.wait_send()
        for cp in own:
            cp.wait()

    vmem = pl.BlockSpec(memory_space=pltpu.VMEM)
    hbm = pl.BlockSpec(memory_space=pl.ANY)
    return pl.pallas_call(
        body, name="gather_weights",
        in_specs=[vmem] * (n_w + 1),
        out_specs=[hbm] * n_w + [vmem],
        out_shape=[jax.ShapeDtypeStruct(s, BF16) for s in full_shapes]
        + [jax.ShapeDtypeStruct((N_CHIPS,) + ln_shard.shape, F32)],
        scratch_shapes=[pltpu.VMEM(w.shape, BF16) for w in shards]
        + [pltpu.SemaphoreType.DMA((6 * n_w,)), pltpu.SemaphoreType.DMA((6 * n_w,)),
           pltpu.SemaphoreType.DMA((n_w,)), pltpu.SemaphoreType.DMA((3,)), pltpu.SemaphoreType.DMA((3,))],
        compiler_params=pltpu.CompilerParams(vmem_limit_bytes=VMEM_LIMIT),
    )(*shards, ln_shard)


_HBM = pl.BlockSpec(memory_space=pltpu.HBM)
_SEM = pl.BlockSpec(memory_space=pltpu.SEMAPHORE)
_N_PEER = N_DEV - 1


def _peer(x, y, c, k):
    return (x + (k >> 2)) % 2, (y + ((k >> 1) & 1)) % 2, (c + (k & 1)) % 2


def _exchange_copy(src_ref, land_ref, sliced, send_sems, recv_sems, idx, x, y, c, k):
    px, py, pc = _peer(x, y, c, k)
    src = src_ref.at[4 * px + 2 * py + pc] if sliced else src_ref
    return pltpu.make_async_remote_copy(
        src_ref=src, dst_ref=land_ref.at[4 * x + 2 * y + c],
        send_sem=send_sems.at[idx], recv_sem=recv_sems.at[idx], device_id=(px, py, pc), device_id_type=MESH)


def _exchange_start(tag, arrays, sliced):
    n = len(arrays)
    lands = [lax.empty(a.shape if s else (N_DEV,) + a.shape, a.dtype) for a, s in zip(arrays, sliced)]

    def body(*refs):
        src, land = refs[:n], refs[n:2 * n]
        send_sems, recv_sems = refs[2 * n], refs[2 * n + 1]
        token = refs[-1]
        x, y, c = _place()
        for w in range(n):
            for k in range(1, N_DEV):
                _exchange_copy(src[w], land[w], sliced[w], send_sems, recv_sems, w * _N_PEER + k - 1, x, y, c, k).start()
        token[...] = jnp.zeros_like(token)

    outs = pl.pallas_call(
        body, name="exchange_start_" + tag,
        out_shape=(pltpu.SemaphoreType.DMA((n * _N_PEER,)), pltpu.SemaphoreType.DMA((n * _N_PEER,)),
                   *[pltpu.HBM(a.shape, a.dtype) for a in arrays], *[pltpu.HBM(l.shape, l.dtype) for l in lands],
                   jax.ShapeDtypeStruct((8, 128), F32)),
        in_specs=[_HBM] * (2 * n),
        out_specs=(_SEM, _SEM, *([_HBM] * (2 * n)), pl.BlockSpec(memory_space=pltpu.VMEM)),
        input_output_aliases={i: 2 + i for i in range(2 * n)},
        compiler_params=pltpu.CompilerParams(has_side_effects=pltpu.SideEffectType.DATAFLOW_SIDE_EFFECTING),
    )(*[pltpu.with_memory_space_constraint(a, pltpu.HBM) for a in arrays],
      *[pltpu.with_memory_space_constraint(l, pltpu.HBM) for l in lands])
    return dict(send=outs[0], recv=outs[1], src=list(outs[2:2 + n]), land=list(outs[2 + n:2 + 2 * n]),
                sliced=list(sliced)), outs[-1]


def _exchange_wait(groups, after):
    counts = [len(g["src"]) for g in groups]
    total = sum(counts)

    def body(*refs):
        pos = 0
        x, y, c = _place()
        for g, n in zip(groups, counts):
            src, land = refs[pos:pos + n], refs[pos + n:pos + 2 * n]
            send_sems, recv_sems = refs[pos + 2 * n], refs[pos + 2 * n + 1]
            pos += 2 * n + 2
            for w in range(n):
                for k in range(1, N_DEV):
                    cp = _exchange_copy(src[w], land[w], g["sliced"][w], send_sems, recv_sems,
                                        w * _N_PEER + k - 1, x, y, c, k)
                    cp.wait_send()
                    cp.wait_recv()

    operands, in_specs, aliases, out_shape = [], [], {}, []
    for g in groups:
        for a in g["src"] + g["land"]:
            aliases[len(operands)] = len(out_shape)
            out_shape.append(pltpu.HBM(a.shape, a.dtype))
            operands.append(a)
            in_specs.append(_HBM)
        operands += [g["send"], g["recv"]]
        in_specs += [_SEM, _SEM]
    operands.append(after)
    in_specs.append(pl.BlockSpec(memory_space=pl.ANY))
    outs = pl.pallas_call(
        body, name="exchange_wait", out_shape=tuple(out_shape), in_specs=in_specs,
        out_specs=tuple([_HBM] * (2 * total)), input_output_aliases=aliases,
        compiler_params=pltpu.CompilerParams(has_side_effects=pltpu.SideEffectType.DATAFLOW_SIDE_EFFECTING),
    )(*operands)
    srcs, lands, pos = [], [], 0
    for n in counts:
        srcs += list(outs[pos:pos + n])
        lands += list(outs[pos + n:pos + 2 * n])
        pos += 2 * n
    return srcs, lands


def _sum_and_swap(pieces, lands, small, small_land):
    n_w = len(pieces)

    def body(*refs):
        g_refs, land_refs = refs[:n_w], refs[n_w:2 * n_w]
        small_ref, small_land_ref = refs[2 * n_w], refs[2 * n_w + 1]
        out_refs = refs[2 * n_w + 2:3 * n_w + 2]
        small_out = refs[3 * n_w + 2]
        bufs = refs[3 * n_w + 3:4 * n_w + 3]
        small_buf, load_sems, small_sems, swap_send, swap_recv = refs[4 * n_w + 3:]
        x, y, c = _place()
        me = 4 * x + 2 * y + c

        def slot(k):
            px, py, pc = _peer(x, y, c, k)
            return 4 * px + 2 * py + pc

        def swap_copy(w, half):
            rows = pieces[w].shape[1]
            win = out_refs[w].at[pl.ds(pl.multiple_of(half * rows, rows), rows), :]
            return pltpu.make_async_remote_copy(
                src_ref=win, dst_ref=win, send_sem=swap_send.at[w], recv_sem=swap_recv.at[w],
                device_id=(x, y, 1 - c), device_id_type=MESH)

        loads = []
        for w in range(n_w):
            per_w = [pltpu.make_async_copy(g_refs[w].at[me], bufs[w].at[me], load_sems.at[w * N_DEV])]
            per_w += [pltpu.make_async_copy(land_refs[w].at[slot(k)], bufs[w].at[slot(k)], load_sems.at[w * N_DEV + k])
                      for k in range(1, N_DEV)]
            loads.append(per_w)
        small_loads = [pltpu.make_async_copy(small_land_ref.at[slot(k)], small_buf.at[slot(k)], small_sems.at[k - 1])
                       for k in range(1, N_DEV)]
        for cp in [cp for per_w in loads for cp in per_w] + small_loads:
            cp.start()
        small_buf[me] = small_ref[...]
        swaps = []
        for w in range(n_w):
            for cp in loads[w]:
                cp.wait()
            rows = pieces[w].shape[1]
            total = bufs[w][0].astype(F32)
            for p in range(1, N_DEV):
                total += bufs[w][p].astype(F32)
            out_refs[w][pl.ds(pl.multiple_of(c * rows, rows), rows), :] = total
            sw = swap_copy(w, c)
            sw.start()
            swaps.append(sw)
        for cp in small_loads:
            cp.wait()
        total = small_buf[0]
        for p in range(1, N_DEV):
            total += small_buf[p]
        small_out[...] = total
        for w in range(n_w):
            swap_copy(w, 1 - c).wait_recv()
        for sw in swaps:
            sw.wait_send()

    vmem = pl.BlockSpec(memory_space=pltpu.VMEM)
    hbm = pl.BlockSpec(memory_space=pl.ANY)
    return pl.pallas_call(
        body, name="sum_and_swap",
        in_specs=[hbm] * (2 * n_w) + [vmem, hbm],
        out_specs=[vmem] * (n_w + 1),
        out_shape=[jax.ShapeDtypeStruct((2 * p.shape[1], p.shape[2]), F32) for p in pieces]
        + [jax.ShapeDtypeStruct(small.shape, F32)],
        scratch_shapes=[pltpu.VMEM(p.shape, BF16) for p in pieces]
        + [pltpu.VMEM((N_DEV,) + small.shape, F32),
           pltpu.SemaphoreType.DMA((n_w * N_DEV,)), pltpu.SemaphoreType.DMA((_N_PEER,)),
           pltpu.SemaphoreType.DMA((n_w,)), pltpu.SemaphoreType.DMA((n_w,))],
        compiler_params=pltpu.CompilerParams(vmem_limit_bytes=VMEM_LIMIT),
    )(*pieces, *lands, small, small_land)


def _adamw(label, w, g, m, v):
    shape = w.shape
    cols = shape[-1]
    rows = w.size // cols
    args = [a.reshape(rows, cols) for a in (w, g, m, v)]
    c1 = 1.0 - ADAM_B1 ** ADAM_STEP
    c2 = 1.0 - ADAM_B2 ** ADAM_STEP

    def body(w_ref, g_ref, m_ref, v_ref, d_ref, nm_ref, nv_ref):
        g_t = g_ref[...]
        nm = ADAM_B1 * m_ref[...] + (1.0 - ADAM_B1) * g_t
        nv = ADAM_B2 * v_ref[...] + (1.0 - ADAM_B2) * (g_t * g_t)
        d_ref[...] = -ADAM_LR * ((nm / c1) / (jnp.sqrt(nv / c2) + ADAM_EPS) + ADAM_WD * w_ref[...])
        nm_ref[...] = nm
        nv_ref[...] = nv

    block_rows = 256 if rows % 256 == 0 and rows > 256 else rows
    spec = pl.BlockSpec((block_rows, cols), lambda i: (i, 0))
    outs = pl.pallas_call(
        body, name="adamw_" + label, grid=(rows // block_rows,),
        in_specs=[spec] * 4, out_specs=[spec] * 3,
        out_shape=[jax.ShapeDtypeStruct((rows, cols), F32)] * 3,
        compiler_params=_params(),
    )(*args)
    return [o.reshape(shape) for o in outs]


def _no_send(tag, arrays, sliced):
    return jnp.zeros((8, 128), F32)


def _local_step(x, tgt, w_in_a, w_out_a, w_kv, w_in_b, w_out_b, sgu_ln_g, sgu_ln_b, w_spatial, b_spatial,
                attn_sinks, rel_bias, post_ln_g, post_ln_b, send=_no_send):
    bsp_t = b_spatial.T
    g1, b1 = post_ln_g[0:1], post_ln_b[0:1]
    g2, b2 = post_ln_g[1:2], post_ln_b[1:2]
    onehot = _bucket_onehot()
    bias = _bias_expand(rel_bias.T, onehot).reshape(N_HEADS, 2 * CHUNK, CHUNK)
    win = _window_tables()

    xb, u, vh, z, rv, xh1, rstd1 = _layer_a_fwd(x, w_in_a, w_out_a, sgu_ln_g, sgu_ln_b, w_spatial, bsp_t)
    q, zb, kd, vd = _layer_b_proj(xh1, g1, b1, w_in_b, w_kv)
    o, dr2, loss_vec, dg2, db2 = _layer_b_fwd(q, zb, kd, vd, bias, win, attn_sinks, xh1, g1, b1, w_out_b, g2, b2,
                                              tgt)
    dq, dzb, dkd, dvd, carry_k, carry_v, gw_out_b, dsink, dbias = _layer_b_bwd_attn(
        dr2, zb, o, q, kd, vd, bias, win, attn_sinks, w_out_b)
    dr1, dg1, db1, gw_in_b, gw_kv = _layer_b_bwd_proj(xh1, rstd1, g1, b1, dr2, dq, dzb, dkd, dvd, carry_k, carry_v,
                                                      w_in_b, w_kv)
    gw_out_b = gw_out_b.reshape(N_DEV, -1, D_MODEL)
    gw_kv = gw_kv.reshape(N_DEV, -1, 2 * PAIR)
    after = send("b", [gw_out_b, gw_in_b, gw_kv], [True, True, True])
    dp, gw_out_a, dws, dbsp, dgs, dbs = _layer_a_bwd_mix(dr1, u, vh, z, rv, w_out_a, sgu_ln_g, sgu_ln_b,
                                                         w_spatial, bsp_t, after)
    gw_out_a = gw_out_a.reshape(N_DEV, -1, D_MODEL)
    drel = _bias_reduce(onehot, dbias.reshape(N_HEADS, -1))
    dsink = jnp.sum(dsink.reshape(N_HEADS, CHUNK), axis=1).reshape(1, N_HEADS)
    small = dict(w_spatial=dws, b_spatial=dbsp.reshape(A_GROUPS, CHUNK), attn_sinks=dsink,
                 rel_bias=drel, post_ln_g=jnp.concatenate([dg1, dg2], axis=0),
                 post_ln_b=jnp.concatenate([db1, db2], axis=0), sgu_ln_g=dgs, sgu_ln_b=dbs)
    packed, layout = _pack_small(small)
    after = send("a_out", [gw_out_a, packed], [True, False])
    gw_in_a = _layer_a_bwd_win(xb, dp, after).reshape(N_DEV, D_MODEL // 2, -1)
    after = send("a_in", [gw_in_a], [True])
    grad_x = _layer_a_bwd_dx(dr1, dp, w_in_a, after)

    loss = (0.5 / D_MODEL) * jnp.sum(loss_vec)
    pieces = [gw_in_a, gw_out_a, gw_kv, gw_in_b, gw_out_b]
    return loss, grad_x, pieces, small, packed, layout


_SMALL_ORDER = ("w_spatial", "b_spatial", "attn_sinks", "rel_bias", "post_ln_g", "post_ln_b", "sgu_ln_g", "sgu_ln_b")
_LANES = 128
_SUBLANES = 8


def _pack_small(small):
    parts, layout = [], {}
    row = 0
    for name in _SMALL_ORDER:
        flat = small[name].reshape(-1)
        n_rows = -(-flat.size // _LANES)
        n_rows = -(-n_rows // _SUBLANES) * _SUBLANES
        flat = jnp.pad(flat, (0, n_rows * _LANES - flat.size))
        parts.append(flat.reshape(n_rows, _LANES))
        layout[name] = (row, flat.size, small[name].shape)
        row += n_rows
    return jnp.concatenate(parts, axis=0), layout


def _unpack_small(packed, layout, sizes):
    out = {}
    for name in _SMALL_ORDER:
        row, padded, shape = layout[name]
        n = sizes[name]
        out[name] = packed[row:row + padded // _LANES].reshape(-1)[:n].reshape(shape)
    return out


def kernel(x, w_in_a, sgu_ln_g, sgu_ln_b, w_spatial, b_spatial, w_out_a, w_kv, w_in_b, attn_sinks, rel_bias, w_out_b, post_ln_g, post_ln_b, loss_target, m_w_in_a, m_sgu_ln_g, m_sgu_ln_b, m_w_spatial, m_b_spatial, m_w_out_a, m_w_kv, m_w_in_b, m_attn_sinks, m_rel_bias, m_w_out_b, m_post_ln_g, m_post_ln_b, v_w_in_a, v_sgu_ln_g, v_sgu_ln_b, v_w_spatial, v_b_spatial, v_w_out_a, v_w_kv, v_w_in_b, v_attn_sinks, v_rel_bias, v_w_out_b, v_post_ln_g, v_post_ln_b):
    weights = dict(w_in_a=w_in_a, sgu_ln_g=sgu_ln_g, sgu_ln_b=sgu_ln_b, w_spatial=w_spatial, b_spatial=b_spatial,
                   w_out_a=w_out_a, w_kv=w_kv, w_in_b=w_in_b, attn_sinks=attn_sinks, rel_bias=rel_bias,
                   w_out_b=w_out_b, post_ln_g=post_ln_g, post_ln_b=post_ln_b)
    moments_m = dict(w_in_a=m_w_in_a, sgu_ln_g=m_sgu_ln_g, sgu_ln_b=m_sgu_ln_b, w_spatial=m_w_spatial,
                     b_spatial=m_b_spatial, w_out_a=m_w_out_a, w_kv=m_w_kv, w_in_b=m_w_in_b,
                     attn_sinks=m_attn_sinks, rel_bias=m_rel_bias, w_out_b=m_w_out_b, post_ln_g=m_post_ln_g,
                     post_ln_b=m_post_ln_b)
    moments_v = dict(w_in_a=v_w_in_a, sgu_ln_g=v_sgu_ln_g, sgu_ln_b=v_sgu_ln_b, w_spatial=v_w_spatial,
                     b_spatial=v_b_spatial, w_out_a=v_w_out_a, w_kv=v_w_kv, w_in_b=v_w_in_b,
                     attn_sinks=v_attn_sinks, rel_bias=v_rel_bias, w_out_b=v_w_out_b, post_ln_g=v_post_ln_g,
                     post_ln_b=v_post_ln_b)
    order = ("w_in_a", "sgu_ln_g", "sgu_ln_b", "w_spatial", "b_spatial", "w_out_a", "w_kv", "w_in_b", "attn_sinks",
             "rel_bias", "w_out_b", "post_ln_g", "post_ln_b")
    large = ("w_in_a", "w_out_a", "w_kv", "w_in_b", "w_out_b")

    shard_index = 2 * lax.axis_index("x") + lax.axis_index("y")
    ln_shard = jnp.concatenate([sgu_ln_g, sgu_ln_b], axis=0)
    *full, ln_full = _gather_weights([w_in_a[0], w_out_a[0], w_kv, w_in_b[0], w_out_b[0]],
                                     [True, False, False, True, False], ln_shard)
    ln_full = jnp.transpose(ln_full, (1, 0, 2)).reshape(2, A_WIDTH)

    groups = []

    def send(tag, arrays, sliced):
        group, token = _exchange_start(tag, arrays, sliced)
        groups.append(group)
        return token

    loss_part, grad_x, pieces, small, packed, layout = _local_step(
        x[0], loss_target[0], *full, ln_full[0:1], ln_full[1:2], w_spatial[0], b_spatial[0], attn_sinks,
        rel_bias, post_ln_g, post_ln_b, send=send)
    loss = lax.psum(loss_part, ("x", "y", "c"))

    srcs, lands = _exchange_wait(groups, grad_x)
    pieces = [srcs[5], srcs[3], srcs[2], srcs[1], srcs[0]]
    landed = [lands[5], lands[3], lands[2], lands[1], lands[0]]
    sizes = {name: small[name].size for name in _SMALL_ORDER}
    *grads_large, packed_sum = _sum_and_swap(pieces, landed, srcs[4], lands[4])
    small_sum = _unpack_small(packed_sum, layout, sizes)

    grads = {}
    for name, g in zip(large, grads_large):
        grads[name] = g.reshape(weights[name].shape)
    shard_cols = sgu_ln_g.shape[1]
    for name in ("sgu_ln_g", "sgu_ln_b"):
        grads[name] = lax.dynamic_slice(small_sum[name], (0, shard_index * shard_cols), (1, shard_cols))
    for name in ("w_spatial", "b_spatial", "attn_sinks", "rel_bias", "post_ln_g", "post_ln_b"):
        grads[name] = small_sum[name].reshape(weights[name].shape)

    deltas, new_m, new_v = {}, {}, {}
    for name in order:
        deltas[name], new_m[name], new_v[name] = _adamw(name, weights[name], grads[name], moments_m[name], moments_v[name])
    return (loss, grad_x[None], *[grads[n] for n in order], *[deltas[n] for n in order],
            *[new_m[n] for n in order], *[new_v[n] for n in order])
```

```python
import math

import jax
import jax.numpy as jnp
from jax import lax
from jax.experimental import pallas as pl
from jax.experimental.pallas import tpu as pltpu

F32 = jnp.float32
BF16 = jnp.bfloat16

D_MODEL = 1024
A_WIDTH = 2048
A_GROUPS = 8
A_GROUP_DIM = 256
CHUNK = 128
N_HEADS = 16
N_KV = 2
HEAD_DIM = 64
PAIR = 2 * HEAD_DIM
B_WIDTH = 1024
REL_BUCKETS = 32
ALPHA = 4.0 ** 0.25
LN_EPS = 1e-5
NEG_INF = -1e30
SCALE = HEAD_DIM ** -0.5

ADAM_LR = 0.001
ADAM_B1 = 0.9
ADAM_B2 = 0.999
ADAM_EPS = 1e-08
ADAM_WD = 0.01
ADAM_STEP = 10

N_DEV = 8
N_CHIPS = 4
MESH = pl.DeviceIdType.MESH
VMEM_LIMIT = 56 * 1024 * 1024

TM_ATTN = 256
TM_MM = 512
TM_WIN = 1024


def _dot(a, b):
    return jnp.dot(a, b, preferred_element_type=F32)


def _dot_nt(a, b):
    return lax.dot_general(a, b, (((1,), (1,)), ((), ())), preferred_element_type=F32)


def _dot_tn(a, b):
    return lax.dot_general(a, b, (((0,), (0,)), ((), ())), preferred_element_type=F32)


def _ln_fwd(r):
    mu = jnp.mean(r, axis=-1, keepdims=True)
    rc = r - mu
    var = jnp.mean(rc * rc, axis=-1, keepdims=True)
    rstd = lax.rsqrt(var + LN_EPS)
    return rc * rstd, rstd


def _ln_bwd(dxh, xh, rstd):
    m1 = jnp.mean(dxh, axis=-1, keepdims=True)
    m2 = jnp.mean(dxh * xh, axis=-1, keepdims=True)
    return rstd * (dxh - m1 - xh * m2)


def _silu_parts(z):
    sg = jax.nn.sigmoid(z)
    return z * sg, sg * (1.0 + z * (1.0 - sg))


def _dup_halves(blk):
    sw = pltpu.roll(blk, HEAD_DIM, 1)
    lo = lax.broadcasted_iota(jnp.int32, blk.shape, 1) < HEAD_DIM
    return jnp.where(lo, blk, sw), jnp.where(lo, sw, blk)


def _fold_halves(blk):
    return blk + pltpu.roll(blk, HEAD_DIM, 1)


def _resident(shape):
    nd = len(shape)
    return pl.BlockSpec(shape, lambda *_: (0,) * nd, pipeline_mode=pl.Buffered(1))


def _const(shape):
    nd = len(shape)
    return pl.BlockSpec(shape, lambda *_: (0,) * nd)


def _rows(tm, cols):
    return pl.BlockSpec((tm, cols), lambda i: (i, 0))


def _params(sem=("arbitrary",)):
    return pltpu.CompilerParams(dimension_semantics=sem, vmem_limit_bytes=VMEM_LIMIT)


def _spatial_mix(ws_ref, bsp_ref, vn, s_scr, n_chunks):
    tri = (lax.broadcasted_iota(jnp.int32, (CHUNK, CHUNK), 0)
           >= lax.broadcasted_iota(jnp.int32, (CHUNK, CHUNK), 1))
    for g in range(A_GROUPS):
        wsg = jnp.where(tri, ws_ref[g], 0.0).astype(BF16)
        cols = slice(g * A_GROUP_DIM, (g + 1) * A_GROUP_DIM)
        for ci in range(n_chunks):
            rows = slice(ci * CHUNK, (ci + 1) * CHUNK)
            s_scr[rows, cols] = _dot(wsg, vn[rows, cols]) + bsp_ref[:, g:g + 1]


def _layer_a_fwd(x, w_in, w_out, lng, lnb, ws, bsp_t):
    t_len = x.shape[0]
    tm = TM_ATTN

    def body(x_ref, win_ref, wout_ref, lng_ref, lnb_ref, ws_ref, bsp_ref,
             xb_ref, u_ref, vh_ref, z_ref, rv_ref, xh_ref, r1_ref, s_scr):
        x_t = x_ref[...]
        xb = x_t.astype(BF16)
        xb_ref[...] = xb
        u = _dot(xb, win_ref[:, 0:A_WIDTH])
        v = _dot(xb, win_ref[:, A_WIDTH:2 * A_WIDTH])
        z = _dot(xb, win_ref[:, 2 * A_WIDTH:3 * A_WIDTH])
        vh, rv = _ln_fwd(v)
        vn = (vh * lng_ref[...] + lnb_ref[...]).astype(BF16)
        _spatial_mix(ws_ref, bsp_ref, vn, s_scr, tm // CHUNK)
        sz, _ = _silu_parts(z)
        y = (u * s_scr[...] * sz).astype(BF16)
        r = ALPHA * x_t + _dot(y, wout_ref[...])
        xh, r1 = _ln_fwd(r)
        u_ref[...] = u.astype(BF16)
        vh_ref[...] = vh.astype(BF16)
        z_ref[...] = z.astype(BF16)
        rv_ref[...] = rv
        xh_ref[...] = xh
        r1_ref[...] = r1

    return pl.pallas_call(
        body, name="layer_a_fwd", grid=(t_len // tm,),
        in_specs=[_rows(tm, D_MODEL), _resident(w_in.shape), _resident(w_out.shape), _const(lng.shape),
                  _const(lnb.shape), _const(ws.shape), _const(bsp_t.shape)],
        out_specs=[_rows(tm, D_MODEL), _rows(tm, A_WIDTH), _rows(tm, A_WIDTH), _rows(tm, A_WIDTH),
                   _rows(tm, 1), _rows(tm, D_MODEL), _rows(tm, 1)],
        out_shape=[jax.ShapeDtypeStruct((t_len, D_MODEL), BF16), jax.ShapeDtypeStruct((t_len, A_WIDTH), BF16),
                   jax.ShapeDtypeStruct((t_len, A_WIDTH), BF16), jax.ShapeDtypeStruct((t_len, A_WIDTH), BF16),
                   jax.ShapeDtypeStruct((t_len, 1), F32), jax.ShapeDtypeStruct((t_len, D_MODEL), F32),
                   jax.ShapeDtypeStruct((t_len, 1), F32)],
        scratch_shapes=[pltpu.VMEM((tm, A_WIDTH), F32)],
        compiler_params=_params(),
    )(x, w_in, w_out, lng, lnb, ws, bsp_t)


def _layer_b_proj(xh1, g1, b1, w_in, w_kv):
    t_len = xh1.shape[0]
    tm = TM_MM

    def body(xh_ref, g_ref, b_ref, win_ref, wkv_ref, q_ref, z_ref, kd_ref, vd_ref):
        h1 = (xh_ref[...] * g_ref[...] + b_ref[...]).astype(BF16)
        q_ref[...] = (_dot(h1, win_ref[:, 0:B_WIDTH]) * SCALE).astype(BF16)
        z_ref[...] = _dot(h1, win_ref[:, B_WIDTH:2 * B_WIDTH]).astype(BF16)
        kv = _dot(h1, wkv_ref[...])
        k0, k1 = _dup_halves(kv[:, 0:PAIR])
        v0, v1 = _dup_halves(kv[:, PAIR:2 * PAIR])
        kd_ref[:, 0:PAIR] = k0.astype(BF16)
        kd_ref[:, PAIR:2 * PAIR] = k1.astype(BF16)
        vd_ref[:, 0:PAIR] = v0.astype(BF16)
        vd_ref[:, PAIR:2 * PAIR] = v1.astype(BF16)

    return pl.pallas_call(
        body, name="layer_b_proj", grid=(t_len // tm,),
        in_specs=[_rows(tm, D_MODEL), _const(g1.shape), _const(b1.shape), _resident(w_in.shape),
                  _resident(w_kv.shape)],
        out_specs=[_rows(tm, B_WIDTH), _rows(tm, B_WIDTH), _rows(tm, 2 * PAIR), _rows(tm, 2 * PAIR)],
        out_shape=[jax.ShapeDtypeStruct((t_len, B_WIDTH), BF16), jax.ShapeDtypeStruct((t_len, B_WIDTH), BF16),
                   jax.ShapeDtypeStruct((t_len, 2 * PAIR), BF16), jax.ShapeDtypeStruct((t_len, 2 * PAIR), BF16)],
        compiler_params=_params(),
    )(xh1, g1, b1, w_in, w_kv)


GROUP = N_HEADS // N_KV
GROUP_Q = GROUP * CHUNK


def _window_tables():
    j = jnp.arange(2 * CHUNK, dtype=jnp.int32)[:, None]
    t = jnp.arange(CHUNK, dtype=jnp.int32)[None, :]
    dist = t + CHUNK - j
    inside = (dist >= 0) & (dist < CHUNK)
    return jnp.stack([inside & (j >= CHUNK), inside]).astype(F32)


def _band(ref, chunk_index, kvh):
    prev0 = pl.multiple_of(jnp.maximum(chunk_index - 1, 0) * CHUNK, CHUNK)
    cur0 = pl.multiple_of(chunk_index * CHUNK, CHUNK)
    cols = slice(kvh * PAIR, (kvh + 1) * PAIR)
    return jnp.concatenate([ref[pl.ds(prev0, CHUNK), cols], ref[pl.ds(cur0, CHUNK), cols]], axis=0)


def _group_tables(bias_ref, win_ref, sink_ref, chunk_index, kvh):
    bias = jnp.concatenate([bias_ref[kvh * GROUP + j] for j in range(GROUP)], axis=1)
    win = win_ref[jnp.minimum(chunk_index, 1)]
    mask = jnp.concatenate([win] * GROUP, axis=1) > 0.5
    sink = jnp.concatenate([jnp.full((1, CHUNK), sink_ref[0, kvh * GROUP + j], F32) for j in range(GROUP)], axis=1)
    return bias, mask, sink


def _attn_probs(qs, kband, bias, mask, sink):
    logits = jnp.where(mask, _dot_nt(kband, qs) + bias, NEG_INF)
    m = jnp.maximum(jnp.max(logits, axis=0, keepdims=True), sink)
    e = jnp.exp(logits - m)
    es = jnp.exp(sink - m)
    inv = 1.0 / (jnp.sum(e, axis=0, keepdims=True) + es)
    return e * inv, es * inv


def _half_mask():
    return lax.broadcasted_iota(jnp.int32, (CHUNK, PAIR), 1) < HEAD_DIM


def _stack_heads(src_ref, rows, kvh, dst_scr, lo):
    for j in range(GROUP):
        h = kvh * GROUP + j
        blk = src_ref[rows, (h // 2) * PAIR:(h // 2 + 1) * PAIR].astype(F32)
        keep = lo if h % 2 == 0 else ~lo
        dst_scr[j * CHUNK:(j + 1) * CHUNK, :] = jnp.where(keep, blk, 0.0).astype(BF16)


def _probs_spec(tm):
    return pl.BlockSpec((tm // CHUNK, N_KV, 2 * CHUNK, GROUP_Q), lambda i: (i, 0, 0, 0))


def _sink_probs_spec():
    return pl.BlockSpec((1, 8, GROUP_Q), lambda i: (i, 0, 0))


def _unstack_pairs(stacked, pp, lo):
    return jnp.where(lo, stacked[(2 * pp) * CHUNK:(2 * pp + 1) * CHUNK], stacked[(2 * pp + 1) * CHUNK:(2 * pp + 2) * CHUNK])


def _layer_b_fwd(q, zb, kd, vd, bias, win, sinks, xh1, g1, b1, w_out, g2, b2, tgt):
    t_len = q.shape[0]
    tm = TM_ATTN

    def body(q_ref, z_ref, kd_ref, vd_ref, bias_ref, win_ref, sink_ref, xh_ref, g1_ref, b1_ref, wout_ref, g2_ref,
             b2_ref, tgt_ref, o_ref, p_ref, ps_ref, dr_ref, loss_ref, dg_ref, db_ref, o_scr, qs_scr):
        i = pl.program_id(0)

        @pl.when(i == 0)
        def _():
            loss_ref[...] = jnp.zeros_like(loss_ref)
            dg_ref[...] = jnp.zeros_like(dg_ref)
            db_ref[...] = jnp.zeros_like(db_ref)

        lo = _half_mask()
        ps_ref[...] = jnp.zeros_like(ps_ref)
        for ci in range(tm // CHUNK):
            cg = i * (tm // CHUNK) + ci
            rows = slice(ci * CHUNK, (ci + 1) * CHUNK)
            for kvh in range(N_KV):
                kband = _band(kd_ref, cg, kvh)
                vband = _band(vd_ref, cg, kvh)
                bias_g, mask, sink = _group_tables(bias_ref, win_ref, sink_ref, cg, kvh)
                _stack_heads(q_ref, rows, kvh, qs_scr, lo)
                p, p_sink = _attn_probs(qs_scr[...], kband, bias_g, mask, sink)
                p = p.astype(BF16)
                p_ref[ci, kvh] = p
                ps_ref[0, ci * N_KV + kvh:ci * N_KV + kvh + 1, :] = p_sink
                o_stack = _dot_tn(p, vband)
                for pp in range(GROUP // 2):
                    pair = kvh * (GROUP // 2) + pp
                    o_scr[rows, pair * PAIR:(pair + 1) * PAIR] = _unstack_pairs(o_stack, pp, lo)
        o = o_scr[...]
        o_ref[...] = o.astype(BF16)
        sz, _ = _silu_parts(z_ref[...].astype(F32))
        y = (o * sz).astype(BF16)
        h1 = xh_ref[...] * g1_ref[...] + b1_ref[...]
        r = ALPHA * h1 + _dot(y, wout_ref[...])
        xh2, rstd2 = _ln_fwd(r)
        diff = xh2 * g2_ref[...] + b2_ref[...] - tgt_ref[...]
        loss_ref[...] += jnp.sum(diff * diff, axis=0, keepdims=True)
        dh2 = diff * (1.0 / D_MODEL)
        dg_ref[...] += jnp.sum(dh2 * xh2, axis=0, keepdims=True)
        db_ref[...] += jnp.sum(dh2, axis=0, keepdims=True)
        dr_ref[...] = _ln_bwd(dh2 * g2_ref[...], xh2, rstd2)

    vec = jax.ShapeDtypeStruct((1, D_MODEL), F32)
    return pl.pallas_call(
        body, name="layer_b_fwd", grid=(t_len // tm,),
        in_specs=[_rows(tm, B_WIDTH), _rows(tm, B_WIDTH), _resident(kd.shape), _resident(vd.shape),
                  _resident(bias.shape), _resident(win.shape), pl.BlockSpec(memory_space=pltpu.SMEM),
                  _rows(tm, D_MODEL), _const(g1.shape), _const(b1.shape), _resident(w_out.shape), _const(g2.shape),
                  _const(b2.shape), _rows(tm, D_MODEL)],
        out_specs=[_rows(tm, B_WIDTH), _probs_spec(tm), _sink_probs_spec(), _rows(tm, D_MODEL), _const((1, D_MODEL)),
                   _const((1, D_MODEL)), _const((1, D_MODEL))],
        out_shape=[jax.ShapeDtypeStruct((t_len, B_WIDTH), BF16),
                   jax.ShapeDtypeStruct((t_len // CHUNK, N_KV, 2 * CHUNK, GROUP_Q), BF16),
                   jax.ShapeDtypeStruct((t_len // tm, 8, GROUP_Q), F32),
                   jax.ShapeDtypeStruct((t_len, D_MODEL), F32), vec, vec, vec],
        scratch_shapes=[pltpu.VMEM((tm, B_WIDTH), F32), pltpu.VMEM((GROUP_Q, PAIR), BF16)],
        compiler_params=_params(),
    )(q, zb, kd, vd, bias, win, sinks, xh1, g1, b1, w_out, g2, b2, tgt)


def _layer_b_bwd_attn(dr2, zb, o, q, kd, vd, probs, sink_probs, w_out):
    t_len = q.shape[0]
    tm = TM_ATTN
    n_steps = t_len // tm
    n_chunks = tm // CHUNK

    def body(dr_ref, z_ref, o_ref, q_ref, kd_ref, vd_ref, p_ref, ps_ref, wout_ref,
             dq_ref, dz_ref, dkd_ref, dvd_ref, ck_ref, cv_ref, gw_ref, dsink_ref, dbias_ref,
             do_scr, qs_scr, dos_scr, gw_acc):
        i = pl.program_id(0)

        @pl.when(i == 0)
        def _():
            gw_acc[...] = jnp.zeros_like(gw_acc)
            dsink_ref[...] = jnp.zeros_like(dsink_ref)
            dbias_ref[...] = jnp.zeros_like(dbias_ref)

        drb = dr_ref[...].astype(BF16)
        dy = _dot_nt(drb, wout_ref[...])
        z = z_ref[...].astype(F32)
        sz, dsz = _silu_parts(z)
        o_t = o_ref[...].astype(F32)
        dz_ref[...] = (dy * o_t * dsz).astype(BF16)
        do_scr[...] = (dy * sz).astype(BF16)
        gw_acc[...] += _dot_tn((o_t * sz).astype(BF16), drb)

        lo = _half_mask()
        for kvh in range(N_KV):
            kcols = slice(kvh * PAIR, (kvh + 1) * PAIR)
            dk_bands, dv_bands = [], []
            for ci in range(n_chunks):
                cg = i * n_chunks + ci
                rows = slice(ci * CHUNK, (ci + 1) * CHUNK)
                kband = _band(kd_ref, cg, kvh)
                vband = _band(vd_ref, cg, kvh)
                _stack_heads(q_ref, rows, kvh, qs_scr, lo)
                _stack_heads(do_scr, rows, kvh, dos_scr, lo)
                qs = qs_scr[...]
                dos = dos_scr[...]
                pb = p_ref[ci, kvh]
                p = pb.astype(F32)
                p_sink = ps_ref[0, ci * N_KV + kvh:ci * N_KV + kvh + 1, :]
                dp = _dot_nt(vband, dos)
                delta = jnp.sum(p * dp, axis=0, keepdims=True)
                dlog = p * (dp - delta)
                for j in range(GROUP):
                    dbias_ref[kvh * GROUP + j] += dlog[:, j * CHUNK:(j + 1) * CHUNK]
                dsink_ref[kvh:kvh + 1, :] += -(p_sink * delta)
                ds = dlog.astype(BF16)
                dq_stack = _dot_tn(ds, kband) * SCALE
                for pp in range(GROUP // 2):
                    pair = kvh * (GROUP // 2) + pp
                    dq_ref[rows, pair * PAIR:(pair + 1) * PAIR] = _unstack_pairs(dq_stack, pp, lo).astype(BF16)
                dk_bands.append(_dot(ds, qs))
                dv_bands.append(_dot(pb, dos))
            for bands, out_ref, carry_ref in ((dk_bands, dkd_ref, ck_ref), (dv_bands, dvd_ref, cv_ref)):
                carry_ref[0, :, kcols] = bands[0][0:CHUNK]
                for ci in range(n_chunks):
                    own = bands[ci][CHUNK:2 * CHUNK]
                    if ci + 1 < n_chunks:
                        own = own + bands[ci + 1][0:CHUNK]
                    out_ref[ci * CHUNK:(ci + 1) * CHUNK, kcols] = own

        @pl.when(i == n_steps - 1)
        def _():
            gw_ref[...] = gw_acc[...].astype(BF16)

    carry_spec = pl.BlockSpec((1, CHUNK, 2 * PAIR), lambda i: (i, 0, 0))
    carry_shape = jax.ShapeDtypeStruct((n_steps, CHUNK, 2 * PAIR), F32)
    bias_shape = (N_HEADS, 2 * CHUNK, CHUNK)
    return pl.pallas_call(
        body, name="layer_b_bwd_attn", grid=(n_steps,),
        in_specs=[_rows(tm, D_MODEL), _rows(tm, B_WIDTH), _rows(tm, B_WIDTH), _rows(tm, B_WIDTH),
                  _resident(kd.shape), _resident(vd.shape), _probs_spec(tm), _sink_probs_spec(),
                  _resident(w_out.shape)],
        out_specs=[_rows(tm, B_WIDTH), _rows(tm, B_WIDTH), _rows(tm, 2 * PAIR), _rows(tm, 2 * PAIR),
                   carry_spec, carry_spec, _const(w_out.shape), _const((N_KV, GROUP_Q)), _const(bias_shape)],
        out_shape=[jax.ShapeDtypeStruct((t_len, B_WIDTH), BF16), jax.ShapeDtypeStruct((t_len, B_WIDTH), BF16),
                   jax.ShapeDtypeStruct((t_len, 2 * PAIR), F32), jax.ShapeDtypeStruct((t_len, 2 * PAIR), F32),
                   carry_shape, carry_shape, jax.ShapeDtypeStruct(w_out.shape, BF16),
                   jax.ShapeDtypeStruct((N_KV, GROUP_Q), F32), jax.ShapeDtypeStruct(bias_shape, F32)],
        scratch_shapes=[pltpu.VMEM((tm, B_WIDTH), BF16), pltpu.VMEM((GROUP_Q, PAIR), BF16),
                        pltpu.VMEM((GROUP_Q, PAIR), BF16), pltpu.VMEM(w_out.shape, F32)],
        compiler_params=_params(),
    )(dr2, zb, o, q, kd, vd, probs, sink_probs, w_out)


def _layer_b_bwd_proj(xh1, rstd1, g1, b1, dr2, dq, dzb, dkd, dvd, carry_k, carry_v, w_in, w_kv):
    t_len = xh1.shape[0]
    tm = TM_MM
    n_steps = t_len // tm
    per_tile = tm // TM_ATTN
    n_carry = carry_k.shape[0]

    def body(xh_ref, rstd_ref, g_ref, b_ref, dr2_ref, dq_ref, dz_ref, dkd_ref, dvd_ref, *rest):
        carry_refs = rest[:2 * per_tile]
        win_ref, wkv_ref, dr1_ref, dg_ref, db_ref, gwin_ref, gwkv_ref, acc_in, acc_kv = rest[2 * per_tile:]
        i = pl.program_id(0)

        @pl.when(i == 0)
        def _():
            acc_in[...] = jnp.zeros_like(acc_in)
            acc_kv[...] = jnp.zeros_like(acc_kv)
            dg_ref[...] = jnp.zeros_like(dg_ref)
            db_ref[...] = jnp.zeros_like(db_ref)

        def with_carries(tile_ref, refs):
            parts = []
            for a in range(per_tile):
                parts.append(tile_ref[a * TM_ATTN:(a + 1) * TM_ATTN - CHUNK, :])
                carry = refs[a][0]
                if a == per_tile - 1:
                    carry = jnp.where(i < n_steps - 1, carry, 0.0)
                parts.append(tile_ref[(a + 1) * TM_ATTN - CHUNK:(a + 1) * TM_ATTN, :] + carry)
            return jnp.concatenate(parts, axis=0).astype(BF16)

        xh = xh_ref[...]
        h1 = (xh * g_ref[...] + b_ref[...]).astype(BF16)
        dq_t = dq_ref[...]
        dz_t = dz_ref[...]
        dkd_t = with_carries(dkd_ref, carry_refs[:per_tile])
        dvd_t = with_carries(dvd_ref, carry_refs[per_tile:])
        wkv = wkv_ref[...].astype(F32)
        wk0, wk1 = _dup_halves(wkv[:, 0:PAIR])
        wv0, wv1 = _dup_halves(wkv[:, PAIR:2 * PAIR])
        dh1 = ALPHA * dr2_ref[...]
        dh1 += _dot_nt(dq_t, win_ref[:, 0:B_WIDTH])
        dh1 += _dot_nt(dz_t, win_ref[:, B_WIDTH:2 * B_WIDTH])
        for blk, w in ((dkd_t[:, 0:PAIR], wk0), (dkd_t[:, PAIR:2 * PAIR], wk1),
                       (dvd_t[:, 0:PAIR], wv0), (dvd_t[:, PAIR:2 * PAIR], wv1)):
            dh1 += _dot_nt(blk, w.astype(BF16))
        acc_in[:, 0:B_WIDTH] += _dot_tn(h1, dq_t)
        acc_in[:, B_WIDTH:2 * B_WIDTH] += _dot_tn(h1, dz_t)
        acc_kv[:, 0:2 * PAIR] += _dot_tn(h1, dkd_t)
        acc_kv[:, 2 * PAIR:4 * PAIR] += _dot_tn(h1, dvd_t)
        dg_ref[...] += jnp.sum(dh1 * xh, axis=0, keepdims=True)
        db_ref[...] += jnp.sum(dh1, axis=0, keepdims=True)
        dr1_ref[...] = _ln_bwd(dh1 * g_ref[...], xh, rstd_ref[...])

        @pl.when(i == n_steps - 1)
        def _():
            half_rows = D_MODEL // 2
            shard_cols = 2 * B_WIDTH // N_CHIPS
            for s in range(N_CHIPS):
                for c in range(2):
                    gwin_ref[2 * s + c] = acc_in[c * half_rows:(c + 1) * half_rows,
                                                 s * shard_cols:(s + 1) * shard_cols].astype(BF16)
            lo = lax.broadcasted_iota(jnp.int32, (D_MODEL, PAIR), 1) < HEAD_DIM
            for n in range(2):
                f0 = _fold_halves(acc_kv[:, (2 * n) * PAIR:(2 * n + 1) * PAIR])
                f1 = _fold_halves(acc_kv[:, (2 * n + 1) * PAIR:(2 * n + 2) * PAIR])
                gwkv_ref[:, n * PAIR:(n + 1) * PAIR] = jnp.where(lo, f0, f1).astype(BF16)

    vec = jax.ShapeDtypeStruct((1, D_MODEL), F32)
    gwin_shape = (N_DEV, D_MODEL // 2, 2 * B_WIDTH // N_CHIPS)

    def carry_spec(a):
        return pl.BlockSpec((1, CHUNK, 2 * PAIR), lambda i: (jnp.minimum(per_tile * i + a + 1, n_carry - 1), 0, 0))

    carry_specs = [carry_spec(a) for a in range(per_tile)]
    return pl.pallas_call(
        body, name="layer_b_bwd_proj", grid=(n_steps,),
        in_specs=[_rows(tm, D_MODEL), _rows(tm, 1), _const(g1.shape), _const(b1.shape), _rows(tm, D_MODEL),
                  _rows(tm, B_WIDTH), _rows(tm, B_WIDTH), _rows(tm, 2 * PAIR), _rows(tm, 2 * PAIR)]
        + carry_specs + carry_specs + [_resident(w_in.shape), _resident(w_kv.shape)],
        out_specs=[_rows(tm, D_MODEL), _const((1, D_MODEL)), _const((1, D_MODEL)), _const(gwin_shape),
                   _const(w_kv.shape)],
        out_shape=[jax.ShapeDtypeStruct((t_len, D_MODEL), F32), vec, vec,
                   jax.ShapeDtypeStruct(gwin_shape, BF16), jax.ShapeDtypeStruct(w_kv.shape, BF16)],
        scratch_shapes=[pltpu.VMEM(w_in.shape, F32), pltpu.VMEM((D_MODEL, 4 * PAIR), F32)],
        compiler_params=_params(),
    )(xh1, rstd1, g1, b1, dr2, dq, dzb, dkd, dvd, *([carry_k] * per_tile), *([carry_v] * per_tile), w_in, w_kv)


def _layer_a_bwd_mix(dr1, u, vh, z, rv, w_out, lng, lnb, ws, bsp_t, after):
    t_len = u.shape[0]
    tm = TM_ATTN
    n_steps = t_len // tm

    def body(dr_ref, u_ref, vh_ref, z_ref, rv_ref, wout_ref, lng_ref, lnb_ref, ws_ref, bsp_ref, after_ref,
             dp_ref, gw_ref, dws_ref, dbsp_ref, dgs_ref, dbs_ref, s_scr, dvn_scr, gw_acc):
        i = pl.program_id(0)

        @pl.when(i == 0)
        def _():
            gw_acc[...] = jnp.zeros_like(gw_acc)
            dws_ref[...] = jnp.zeros_like(dws_ref)
            dbsp_ref[...] = jnp.zeros_like(dbsp_ref)
            dgs_ref[...] = jnp.zeros_like(dgs_ref)
            dbs_ref[...] = jnp.zeros_like(dbs_ref)

        drb = dr_ref[...].astype(BF16)
        dy = _dot_nt(drb, wout_ref[...])
        u_t = u_ref[...].astype(F32)
        vh_t = vh_ref[...].astype(F32)
        z_t = z_ref[...].astype(F32)
        vn = (vh_t * lng_ref[...] + lnb_ref[...]).astype(BF16)
        _spatial_mix(ws_ref, bsp_ref, vn, s_scr, tm // CHUNK)
        s = s_scr[...]
        sz, dsz = _silu_parts(z_t)
        gw_acc[...] += _dot_tn((u_t * s * sz).astype(BF16), drb)
        dp_ref[:, 0:A_WIDTH] = (dy * s * sz).astype(BF16)
        dp_ref[:, 2 * A_WIDTH:3 * A_WIDTH] = (dy * u_t * s * dsz).astype(BF16)
        ds = dy * u_t * sz

        tri = (lax.broadcasted_iota(jnp.int32, (CHUNK, CHUNK), 0)
               >= lax.broadcasted_iota(jnp.int32, (CHUNK, CHUNK), 1))
        for g in range(A_GROUPS):
            wsg = jnp.where(tri, ws_ref[g], 0.0).astype(BF16)
            cols = slice(g * A_GROUP_DIM, (g + 1) * A_GROUP_DIM)
            dws_g = jnp.zeros((CHUNK, CHUNK), F32)
            dbsp_g = jnp.zeros((CHUNK, 1), F32)
            for ci in range(tm // CHUNK):
                rows = slice(ci * CHUNK, (ci + 1) * CHUNK)
                ds_g = ds[rows, cols]
                ds_b = ds_g.astype(BF16)
                dws_g += _dot_nt(ds_b, vn[rows, cols])
                dbsp_g += jnp.sum(ds_g, axis=-1, keepdims=True)
                dvn_scr[rows, cols] = _dot_tn(wsg, ds_b)
            dws_ref[g] += jnp.where(tri, dws_g, 0.0)
            dbsp_ref[g] += dbsp_g
        dvn = dvn_scr[...]
        dgs_ref[...] += jnp.sum(dvn * vh_t, axis=0, keepdims=True)
        dbs_ref[...] += jnp.sum(dvn, axis=0, keepdims=True)
        dp_ref[:, A_WIDTH:2 * A_WIDTH] = _ln_bwd(dvn * lng_ref[...], vh_t, rv_ref[...]).astype(BF16)

        @pl.when(i == n_steps - 1)
        def _():
            gw_ref[...] = gw_acc[...].astype(BF16)

    wide = jax.ShapeDtypeStruct((1, A_WIDTH), F32)
    return pl.pallas_call(
        body, name="layer_a_bwd_mix", grid=(n_steps,),
        in_specs=[_rows(tm, D_MODEL), _rows(tm, A_WIDTH), _rows(tm, A_WIDTH), _rows(tm, A_WIDTH), _rows(tm, 1),
                  _resident(w_out.shape), _const(lng.shape), _const(lnb.shape), _const(ws.shape),
                  _const(bsp_t.shape), _const(after.shape)],
        out_specs=[_rows(tm, 3 * A_WIDTH), _const(w_out.shape), _const(ws.shape), _const((A_GROUPS, CHUNK, 1)),
                   _const((1, A_WIDTH)), _const((1, A_WIDTH))],
        out_shape=[jax.ShapeDtypeStruct((t_len, 3 * A_WIDTH), BF16), jax.ShapeDtypeStruct(w_out.shape, BF16),
                   jax.ShapeDtypeStruct(ws.shape, F32), jax.ShapeDtypeStruct((A_GROUPS, CHUNK, 1), F32),
                   wide, wide],
        scratch_shapes=[pltpu.VMEM((tm, A_WIDTH), F32), pltpu.VMEM((tm, A_WIDTH), F32),
                        pltpu.VMEM(w_out.shape, F32)],
        compiler_params=_params(),
    )(dr1, u, vh, z, rv, w_out, lng, lnb, ws, bsp_t, after)


def _layer_a_bwd_dx(dr1, dp, w_in, after):
    t_len = dr1.shape[0]
    tm = TM_MM

    def body(dr_ref, dp_ref, win_ref, after_ref, dx_ref):
        dx_ref[...] = ALPHA * dr_ref[...] + _dot_nt(dp_ref[...], win_ref[...])

    return pl.pallas_call(
        body, name="layer_a_bwd_dx", grid=(t_len // tm,),
        in_specs=[_rows(tm, D_MODEL), _rows(tm, 3 * A_WIDTH), _resident(w_in.shape), _const(after.shape)],
        out_specs=_rows(tm, D_MODEL),
        out_shape=jax.ShapeDtypeStruct((t_len, D_MODEL), F32),
        compiler_params=_params(),
    )(dr1, dp, w_in, after)


def _layer_a_bwd_win(xb, dp, after):
    t_len = xb.shape[0]
    tm = TM_WIN
    n_steps = t_len // tm
    shard_cols = 3 * A_WIDTH // N_CHIPS
    half_rows = D_MODEL // 2

    def body(xb_ref, dp_ref, after_ref, gw_ref, acc):
        i = pl.program_id(1)

        @pl.when(i == 0)
        def _():
            acc[...] = jnp.zeros_like(acc)

        acc[...] += _dot_tn(xb_ref[...], dp_ref[...])

        @pl.when(i == n_steps - 1)
        def _():
            for c in range(2):
                gw_ref[0, c] = acc[c * half_rows:(c + 1) * half_rows, :].astype(BF16)

    return pl.pallas_call(
        body, name="layer_a_bwd_win", grid=(N_CHIPS, n_steps),
        in_specs=[pl.BlockSpec((tm, D_MODEL), lambda j, i: (i, 0)),
                  pl.BlockSpec((tm, shard_cols), lambda j, i: (i, j)), _const(after.shape)],
        out_specs=pl.BlockSpec((1, 2, half_rows, shard_cols), lambda j, i: (j, 0, 0, 0)),
        out_shape=jax.ShapeDtypeStruct((N_CHIPS, 2, half_rows, shard_cols), BF16),
        scratch_shapes=[pltpu.VMEM((D_MODEL, shard_cols), F32)],
        compiler_params=_params(("arbitrary", "arbitrary")),
    )(xb, dp, after)


def _bucket_onehot():
    t = jnp.arange(CHUNK, dtype=jnp.int32)[None, :]
    j = jnp.arange(2 * CHUNK, dtype=jnp.int32)[:, None]
    dist = jnp.clip(t + CHUNK - j, 0, CHUNK - 1)
    max_exact = REL_BUCKETS // 2
    df = jnp.maximum(dist, 1).astype(F32)
    large = max_exact + (jnp.log(df / max_exact) / math.log(CHUNK / max_exact)
                         * (REL_BUCKETS - max_exact)).astype(jnp.int32)
    bucket = jnp.where(dist < max_exact, dist, jnp.minimum(large, REL_BUCKETS - 1))
    onehot = bucket.reshape(1, -1) == jnp.arange(REL_BUCKETS, dtype=jnp.int32)[:, None]
    return onehot.astype(F32)


def _bias_expand(rel_t, onehot):
    def body(rel_ref, oh_ref, out_ref):
        out_ref[...] = jnp.dot(rel_ref[...], oh_ref[...], preferred_element_type=F32,
                               precision=lax.Precision.HIGHEST)

    return pl.pallas_call(
        body, name="bias_expand",
        out_shape=jax.ShapeDtypeStruct((N_HEADS, onehot.shape[1]), F32),
    )(rel_t, onehot)


def _bias_reduce(onehot, dbias):
    def body(oh_ref, db_ref, out_ref):
        out_ref[...] = lax.dot_general(oh_ref[...], db_ref[...], (((1,), (1,)), ((), ())),
                                       preferred_element_type=F32, precision=lax.Precision.HIGHEST)

    return pl.pallas_call(
        body, name="bias_reduce",
        out_shape=jax.ShapeDtypeStruct((REL_BUCKETS, N_HEADS), F32),
    )(onehot, dbias)


def _place():
    return lax.axis_index("x"), lax.axis_index("y"), lax.axis_index("c")


def _gather_weights(shards, col_sharded, ln_shard):
    n_w = len(shards)
    full_shapes = []
    for w, cs in zip(shards, col_sharded):
        r, c = w.shape
        full_shapes.append((r, c * N_CHIPS) if cs else (r * N_CHIPS, c))

    def body(*refs):
        in_refs = refs[:n_w]
        ln_ref = refs[n_w]
        full_refs = refs[n_w + 1:2 * n_w + 1]
        ln_full = refs[2 * n_w + 1]
        stage = refs[2 * n_w + 2:3 * n_w + 2]
        send_sems, recv_sems, local_sems, ln_send, ln_recv = refs[3 * n_w + 2:]
        x, y, c = _place()
        s_me = 2 * x + y
        chips = [(1 - x, y), (x, 1 - y), (1 - x, 1 - y)]

        def shard_window(w, s, half):
            rows, cols = shards[w].shape
            if col_sharded[w]:
                rsel = pl.ds(0, rows) if half is None else pl.ds(half * (rows // 2), rows // 2)
                return full_refs[w].at[rsel, pl.ds(s * cols, cols)]
            if half is None:
                return full_refs[w].at[pl.ds(s * rows, rows), :]
            return full_refs[w].at[pl.ds(s * rows + half * (rows // 2), rows // 2), :]

        def stage_half(w, half):
            rows = shards[w].shape[0]
            return stage[w].at[pl.ds(half * (rows // 2), rows // 2), :]

        def ici_copy(w, k, sender_shard, src):
            return pltpu.make_async_remote_copy(
                src_ref=src, dst_ref=shard_window(w, sender_shard, c),
                send_sem=send_sems.at[w * 3 + k], recv_sem=recv_sems.at[w * 3 + k],
                device_id=(*chips[k], c), device_id_type=MESH)

        def d2d_copy(w, k, half):
            s_k = 2 * chips[k][0] + chips[k][1]
            win = shard_window(w, s_k, half)
            return pltpu.make_async_remote_copy(
                src_ref=win, dst_ref=win,
                send_sem=send_sems.at[3 * n_w + w * 3 + k], recv_sem=recv_sems.at[3 * n_w + w * 3 + k],
                device_id=(x, y, 1 - c), device_id_type=MESH)

        def ln_copy(k, slot):
            return pltpu.make_async_remote_copy(
                src_ref=ln_ref, dst_ref=ln_full.at[slot], send_sem=ln_send.at[k], recv_sem=ln_recv.at[k],
                device_id=(*chips[k], c), device_id_type=MESH)

        for w in range(n_w):
            stage[w][...] = in_refs[w][...].astype(BF16)
        own = [pltpu.make_async_copy(stage[w], shard_window(w, s_me, None), local_sems.at[w]) for w in range(n_w)]
        for cp in own:
            cp.start()
        ln_full[s_me] = ln_ref[...]
        first = [ici_copy(w, k, s_me, stage_half(w, c)) for w in range(n_w) for k in range(3)]
        first += [ln_copy(k, s_me) for k in range(3)]
        for cp in first:
            cp.start()
        passed = []
        for w in range(n_w):
            for k in range(3):
                s_k = 2 * chips[k][0] + chips[k][1]
                ici_copy(w, k, s_k, stage_half(w, c)).wait_recv()
                fwd = d2d_copy(w, k, c)
                fwd.start()
                passed.append(fwd)
        for w in range(n_w):
            for k in range(3):
                d2d_copy(w, k, 1 - c).wait_recv()
        for k in range(3):
            ln_copy(k, 2 * chips[k][0] + chips[k][1]).wait_recv()
        for cp in first + passed:
            cp.wait_send()
        for cp in own:
            cp.wait()

    vmem = pl.BlockSpec(memory_space=pltpu.VMEM)
    hbm = pl.BlockSpec(memory_space=pl.ANY)
    return pl.pallas_call(
        body, name="gather_weights",
        in_specs=[vmem] * (n_w + 1),
        out_specs=[hbm] * n_w + [vmem],
        out_shape=[jax.ShapeDtypeStruct(s, BF16) for s in full_shapes]
        + [jax.ShapeDtypeStruct((N_CHIPS,) + ln_shard.shape, F32)],
        scratch_shapes=[pltpu.VMEM(w.shape, BF16) for w in shards]
        + [pltpu.SemaphoreType.DMA((6 * n_w,)), pltpu.SemaphoreType.DMA((6 * n_w,)),
           pltpu.SemaphoreType.DMA((n_w,)), pltpu.SemaphoreType.DMA((3,)), pltpu.SemaphoreType.DMA((3,))],
        compiler_params=pltpu.CompilerParams(vmem_limit_bytes=VMEM_LIMIT),
    )(*shards, ln_shard)


_HBM = pl.BlockSpec(memory_space=pltpu.HBM)
_SEM = pl.BlockSpec(memory_space=pltpu.SEMAPHORE)
_N_PEER = N_DEV - 1


def _peer(x, y, c, k):
    return (x + (k >> 2)) % 2, (y + ((k >> 1) & 1)) % 2, (c + (k & 1)) % 2


def _exchange_copy(src_ref, land_ref, sliced, send_sems, recv_sems, idx, x, y, c, k):
    px, py, pc = _peer(x, y, c, k)
    src = src_ref.at[4 * px + 2 * py + pc] if sliced else src_ref
    return pltpu.make_async_remote_copy(
        src_ref=src, dst_ref=land_ref.at[4 * x + 2 * y + c],
        send_sem=send_sems.at[idx], recv_sem=recv_sems.at[idx], device_id=(px, py, pc), device_id_type=MESH)


def _exchange_start(tag, arrays, sliced):
    n = len(arrays)
    lands = [lax.empty(a.shape if s else (N_DEV,) + a.shape, a.dtype) for a, s in zip(arrays, sliced)]

    def body(*refs):
        src, land = refs[:n], refs[n:2 * n]
        send_sems, recv_sems = refs[2 * n], refs[2 * n + 1]
        token = refs[-1]
        x, y, c = _place()
        for w in range(n):
            for k in range(1, N_DEV):
                _exchange_copy(src[w], land[w], sliced[w], send_sems, recv_sems, w * _N_PEER + k - 1, x, y, c, k).start()
        token[...] = jnp.zeros_like(token)

    outs = pl.pallas_call(
        body, name="exchange_start_" + tag,
        out_shape=(pltpu.SemaphoreType.DMA((n * _N_PEER,)), pltpu.SemaphoreType.DMA((n * _N_PEER,)),
                   *[pltpu.HBM(a.shape, a.dtype) for a in arrays], *[pltpu.HBM(l.shape, l.dtype) for l in lands],
                   jax.ShapeDtypeStruct((8, 128), F32)),
        in_specs=[_HBM] * (2 * n),
        out_specs=(_SEM, _SEM, *([_HBM] * (2 * n)), pl.BlockSpec(memory_space=pltpu.VMEM)),
        input_output_aliases={i: 2 + i for i in range(2 * n)},
        compiler_params=pltpu.CompilerParams(has_side_effects=pltpu.SideEffectType.DATAFLOW_SIDE_EFFECTING),
    )(*[pltpu.with_memory_space_constraint(a, pltpu.HBM) for a in arrays],
      *[pltpu.with_memory_space_constraint(l, pltpu.HBM) for l in lands])
    return dict(send=outs[0], recv=outs[1], src=list(outs[2:2 + n]), land=list(outs[2 + n:2 + 2 * n]),
                sliced=list(sliced)), outs[-1]


def _exchange_wait(groups, after):
    counts = [len(g["src"]) for g in groups]
    total = sum(counts)

    def body(*refs):
        pos = 0
        x, y, c = _place()
        for g, n in zip(groups, counts):
            src, land = refs[pos:pos + n], refs[pos + n:pos + 2 * n]
            send_sems, recv_sems = refs[pos + 2 * n], refs[pos + 2 * n + 1]
            pos += 2 * n + 2
            for w in range(n):
                for k in range(1, N_DEV):
                    cp = _exchange_copy(src[w], land[w], g["sliced"][w], send_sems, recv_sems,
                                        w * _N_PEER + k - 1, x, y, c, k)
                    cp.wait_send()
                    cp.wait_recv()

    operands, in_specs, aliases, out_shape = [], [], {}, []
    for g in groups:
        for a in g["src"] + g["land"]:
            aliases[len(operands)] = len(out_shape)
            out_shape.append(pltpu.HBM(a.shape, a.dtype))
            operands.append(a)
            in_specs.append(_HBM)
        operands += [g["send"], g["recv"]]
        in_specs += [_SEM, _SEM]
    operands.append(after)
    in_specs.append(pl.BlockSpec(memory_space=pl.ANY))
    outs = pl.pallas_call(
        body, name="exchange_wait", out_shape=tuple(out_shape), in_specs=in_specs,
        out_specs=tuple([_HBM] * (2 * total)), input_output_aliases=aliases,
        compiler_params=pltpu.CompilerParams(has_side_effects=pltpu.SideEffectType.DATAFLOW_SIDE_EFFECTING),
    )(*operands)
    srcs, lands, pos = [], [], 0
    for n in counts:
        srcs += list(outs[pos:pos + n])
        lands += list(outs[pos + n:pos + 2 * n])
        pos += 2 * n
    return srcs, lands


def _sum_and_swap(pieces, lands, small, small_land):
    n_w = len(pieces)

    def body(*refs):
        g_refs, land_refs = refs[:n_w], refs[n_w:2 * n_w]
        small_ref, small_land_ref = refs[2 * n_w], refs[2 * n_w + 1]
        out_refs = refs[2 * n_w + 2:3 * n_w + 2]
        small_out = refs[3 * n_w + 2]
        bufs = refs[3 * n_w + 3:4 * n_w + 3]
        small_buf, load_sems, small_sems, swap_send, swap_recv = refs[4 * n_w + 3:]
        x, y, c = _place()
        me = 4 * x + 2 * y + c

        def slot(k):
            px, py, pc = _peer(x, y, c, k)
            return 4 * px + 2 * py + pc

        def swap_copy(w, half):
            rows = pieces[w].shape[1]
            win = out_refs[w].at[pl.ds(pl.multiple_of(half * rows, rows), rows), :]
            return pltpu.make_async_remote_copy(
                src_ref=win, dst_ref=win, send_sem=swap_send.at[w], recv_sem=swap_recv.at[w],
                device_id=(x, y, 1 - c), device_id_type=MESH)

        loads = []
        for w in range(n_w):
            per_w = [pltpu.make_async_copy(g_refs[w].at[me], bufs[w].at[me], load_sems.at[w * N_DEV])]
            per_w += [pltpu.make_async_copy(land_refs[w].at[slot(k)], bufs[w].at[slot(k)], load_sems.at[w * N_DEV + k])
                      for k in range(1, N_DEV)]
            loads.append(per_w)
        small_loads = [pltpu.make_async_copy(small_land_ref.at[slot(k)], small_buf.at[slot(k)], small_sems.at[k - 1])
                       for k in range(1, N_DEV)]
        for cp in [cp for per_w in loads for cp in per_w] + small_loads:
            cp.start()
        small_buf[me] = small_ref[...]
        swaps = []
        for w in range(n_w):
            for cp in loads[w]:
                cp.wait()
            rows = pieces[w].shape[1]
            total = bufs[w][0].astype(F32)
            for p in range(1, N_DEV):
                total += bufs[w][p].astype(F32)
            out_refs[w][pl.ds(pl.multiple_of(c * rows, rows), rows), :] = total
            sw = swap_copy(w, c)
            sw.start()
            swaps.append(sw)
        for cp in small_loads:
            cp.wait()
        total = small_buf[0]
        for p in range(1, N_DEV):
            total += small_buf[p]
        small_out[...] = total
        for w in range(n_w):
            swap_copy(w, 1 - c).wait_recv()
        for sw in swaps:
            sw.wait_send()

    vmem = pl.BlockSpec(memory_space=pltpu.VMEM)
    hbm = pl.BlockSpec(memory_space=pl.ANY)
    return pl.pallas_call(
        body, name="sum_and_swap",
        in_specs=[hbm] * (2 * n_w) + [vmem, hbm],
        out_specs=[vmem] * (n_w + 1),
        out_shape=[jax.ShapeDtypeStruct((2 * p.shape[1], p.shape[2]), F32) for p in pieces]
        + [jax.ShapeDtypeStruct(small.shape, F32)],
        scratch_shapes=[pltpu.VMEM(p.shape, BF16) for p in pieces]
        + [pltpu.VMEM((N_DEV,) + small.shape, F32),
           pltpu.SemaphoreType.DMA((n_w * N_DEV,)), pltpu.SemaphoreType.DMA((_N_PEER,)),
           pltpu.SemaphoreType.DMA((n_w,)), pltpu.SemaphoreType.DMA((n_w,))],
        compiler_params=pltpu.CompilerParams(vmem_limit_bytes=VMEM_LIMIT),
    )(*pieces, *lands, small, small_land)


def _adamw(label, w, g, m, v):
    shape = w.shape
    cols = shape[-1]
    rows = w.size // cols
    args = [a.reshape(rows, cols) for a in (w, g, m, v)]
    c1 = 1.0 - ADAM_B1 ** ADAM_STEP
    c2 = 1.0 - ADAM_B2 ** ADAM_STEP

    def body(w_ref, g_ref, m_ref, v_ref, d_ref, nm_ref, nv_ref):
        g_t = g_ref[...]
        nm = ADAM_B1 * m_ref[...] + (1.0 - ADAM_B1) * g_t
        nv = ADAM_B2 * v_ref[...] + (1.0 - ADAM_B2) * (g_t * g_t)
        d_ref[...] = -ADAM_LR * ((nm / c1) / (jnp.sqrt(nv / c2) + ADAM_EPS) + ADAM_WD * w_ref[...])
        nm_ref[...] = nm
        nv_ref[...] = nv

    block_rows = 256 if rows % 256 == 0 and rows > 256 else rows
    spec = pl.BlockSpec((block_rows, cols), lambda i: (i, 0))
    outs = pl.pallas_call(
        body, name="adamw_" + label, grid=(rows // block_rows,),
        in_specs=[spec] * 4, out_specs=[spec] * 3,
        out_shape=[jax.ShapeDtypeStruct((rows, cols), F32)] * 3,
        compiler_params=_params(),
    )(*args)
    return [o.reshape(shape) for o in outs]


def _no_send(tag, arrays, sliced):
    return jnp.zeros((8, 128), F32)


def _local_step(x, tgt, w_in_a, w_out_a, w_kv, w_in_b, w_out_b, sgu_ln_g, sgu_ln_b, w_spatial, b_spatial,
                attn_sinks, rel_bias, post_ln_g, post_ln_b, send=_no_send):
    bsp_t = b_spatial.T
    g1, b1 = post_ln_g[0:1], post_ln_b[0:1]
    g2, b2 = post_ln_g[1:2], post_ln_b[1:2]
    onehot = _bucket_onehot()
    bias = _bias_expand(rel_bias.T, onehot).reshape(N_HEADS, 2 * CHUNK, CHUNK)
    win = _window_tables()

    xb, u, vh, z, rv, xh1, rstd1 = _layer_a_fwd(x, w_in_a, w_out_a, sgu_ln_g, sgu_ln_b, w_spatial, bsp_t)
    q, zb, kd, vd = _layer_b_proj(xh1, g1, b1, w_in_b, w_kv)
    o, probs, sink_probs, dr2, loss_vec, dg2, db2 = _layer_b_fwd(q, zb, kd, vd, bias, win, attn_sinks, xh1, g1, b1,
                                                                 w_out_b, g2, b2, tgt)
    dq, dzb, dkd, dvd, carry_k, carry_v, gw_out_b, dsink, dbias = _layer_b_bwd_attn(
        dr2, zb, o, q, kd, vd, probs, sink_probs, w_out_b)
    dr1, dg1, db1, gw_in_b, gw_kv = _layer_b_bwd_proj(xh1, rstd1, g1, b1, dr2, dq, dzb, dkd, dvd, carry_k, carry_v,
                                                      w_in_b, w_kv)
    gw_out_b = gw_out_b.reshape(N_DEV, -1, D_MODEL)
    gw_kv = gw_kv.reshape(N_DEV, -1, 2 * PAIR)
    after = send("b", [gw_out_b, gw_in_b, gw_kv], [True, True, True])
    dp, gw_out_a, dws, dbsp, dgs, dbs = _layer_a_bwd_mix(dr1, u, vh, z, rv, w_out_a, sgu_ln_g, sgu_ln_b,
                                                         w_spatial, bsp_t, after)
    gw_out_a = gw_out_a.reshape(N_DEV, -1, D_MODEL)
    drel = _bias_reduce(onehot, dbias.reshape(N_HEADS, -1))
    dsink = jnp.sum(dsink.reshape(N_HEADS, CHUNK), axis=1).reshape(1, N_HEADS)
    small = dict(w_spatial=dws, b_spatial=dbsp.reshape(A_GROUPS, CHUNK), attn_sinks=dsink,
                 rel_bias=drel, post_ln_g=jnp.concatenate([dg1, dg2], axis=0),
                 post_ln_b=jnp.concatenate([db1, db2], axis=0), sgu_ln_g=dgs, sgu_ln_b=dbs)
    packed, layout = _pack_small(small)
    after = send("a_out", [gw_out_a, packed], [True, False])
    gw_in_a = _layer_a_bwd_win(xb, dp, after).reshape(N_DEV, D_MODEL // 2, -1)
    after = send("a_in", [gw_in_a], [True])
    grad_x = _layer_a_bwd_dx(dr1, dp, w_in_a, after)

    loss = (0.5 / D_MODEL) * jnp.sum(loss_vec)
    pieces = [gw_in_a, gw_out_a, gw_kv, gw_in_b, gw_out_b]
    return loss, grad_x, pieces, small, packed, layout


_SMALL_ORDER = ("w_spatial", "b_spatial", "attn_sinks", "rel_bias", "post_ln_g", "post_ln_b", "sgu_ln_g", "sgu_ln_b")
_LANES = 128
_SUBLANES = 8


def _pack_small(small):
    parts, layout = [], {}
    row = 0
    for name in _SMALL_ORDER:
        flat = small[name].reshape(-1)
        n_rows = -(-flat.size // _LANES)
        n_rows = -(-n_rows // _SUBLANES) * _SUBLANES
        flat = jnp.pad(flat, (0, n_rows * _LANES - flat.size))
        parts.append(flat.reshape(n_rows, _LANES))
        layout[name] = (row, flat.size, small[name].shape)
        row += n_rows
    return jnp.concatenate(parts, axis=0), layout


def _unpack_small(packed, layout, sizes):
    out = {}
    for name in _SMALL_ORDER:
        row, padded, shape = layout[name]
        n = sizes[name]
        out[name] = packed[row:row + padded // _LANES].reshape(-1)[:n].reshape(shape)
    return out


def kernel(x, w_in_a, sgu_ln_g, sgu_ln_b, w_spatial, b_spatial, w_out_a, w_kv, w_in_b, attn_sinks, rel_bias, w_out_b, post_ln_g, post_ln_b, loss_target, m_w_in_a, m_sgu_ln_g, m_sgu_ln_b, m_w_spatial, m_b_spatial, m_w_out_a, m_w_kv, m_w_in_b, m_attn_sinks, m_rel_bias, m_w_out_b, m_post_ln_g, m_post_ln_b, v_w_in_a, v_sgu_ln_g, v_sgu_ln_b, v_w_spatial, v_b_spatial, v_w_out_a, v_w_kv, v_w_in_b, v_attn_sinks, v_rel_bias, v_w_out_b, v_post_ln_g, v_post_ln_b):
    weights = dict(w_in_a=w_in_a, sgu_ln_g=sgu_ln_g, sgu_ln_b=sgu_ln_b, w_spatial=w_spatial, b_spatial=b_spatial,
                   w_out_a=w_out_a, w_kv=w_kv, w_in_b=w_in_b, attn_sinks=attn_sinks, rel_bias=rel_bias,
                   w_out_b=w_out_b, post_ln_g=post_ln_g, post_ln_b=post_ln_b)
    moments_m = dict(w_in_a=m_w_in_a, sgu_ln_g=m_sgu_ln_g, sgu_ln_b=m_sgu_ln_b, w_spatial=m_w_spatial,
                     b_spatial=m_b_spatial, w_out_a=m_w_out_a, w_kv=m_w_kv, w_in_b=m_w_in_b,
                     attn_sinks=m_attn_sinks, rel_bias=m_rel_bias, w_out_b=m_w_out_b, post_ln_g=m_post_ln_g,
                     post_ln_b=m_post_ln_b)
    moments_v = dict(w_in_a=v_w_in_a, sgu_ln_g=v_sgu_ln_g, sgu_ln_b=v_sgu_ln_b, w_spatial=v_w_spatial,
                     b_spatial=v_b_spatial, w_out_a=v_w_out_a, w_kv=v_w_kv, w_in_b=v_w_in_b,
                     attn_sinks=v_attn_sinks, rel_bias=v_rel_bias, w_out_b=v_w_out_b, post_ln_g=v_post_ln_g,
                     post_ln_b=v_post_ln_b)
    order = ("w_in_a", "sgu_ln_g", "sgu_ln_b", "w_spatial", "b_spatial", "w_out_a", "w_kv", "w_in_b", "attn_sinks",
             "rel_bias", "w_out_b", "post_ln_g", "post_ln_b")
    large = ("w_in_a", "w_out_a", "w_kv", "w_in_b", "w_out_b")

    shard_index = 2 * lax.axis_index("x") + lax.axis_index("y")
    ln_shard = jnp.concatenate([sgu_ln_g, sgu_ln_b], axis=0)
    *full, ln_full = _gather_weights([w_in_a[0], w_out_a[0], w_kv, w_in_b[0], w_out_b[0]],
                                     [True, False, False, True, False], ln_shard)
    ln_full = jnp.transpose(ln_full, (1, 0, 2)).reshape(2, A_WIDTH)

    groups = []

    def send(tag, arrays, sliced):
        group, token = _exchange_start(tag, arrays, sliced)
        groups.append(group)
        return token

    loss_part, grad_x, pieces, small, packed, layout = _local_step(
        x[0], loss_target[0], *full, ln_full[0:1], ln_full[1:2], w_spatial[0], b_spatial[0], attn_sinks,
        rel_bias, post_ln_g, post_ln_b, send=send)
    loss = lax.psum(loss_part, ("x", "y", "c"))

    srcs, lands = _exchange_wait(groups, grad_x)
    pieces = [srcs[5], srcs[3], srcs[2], srcs[1], srcs[0]]
    landed = [lands[5], lands[3], lands[2], lands[1], lands[0]]
    sizes = {name: small[name].size for name in _SMALL_ORDER}
    *grads_large, packed_sum = _sum_and_swap(pieces, landed, srcs[4], lands[4])
    small_sum = _unpack_small(packed_sum, layout, sizes)

    grads = {}
    for name, g in zip(large, grads_large):
        grads[name] = g.reshape(weights[name].shape)
    shard_cols = sgu_ln_g.shape[1]
    for name in ("sgu_ln_g", "sgu_ln_b"):
        grads[name] = lax.dynamic_slice(small_sum[name], (0, shard_index * shard_cols), (1, shard_cols))
    for name in ("w_spatial", "b_spatial", "attn_sinks", "rel_bias", "post_ln_g", "post_ln_b"):
        grads[name] = small_sum[name].reshape(weights[name].shape)

    deltas, new_m, new_v = {}, {}, {}
    for name in order:
        deltas[name], new_m[name], new_v[name] = _adamw(name, weights[name], grads[name], moments_m[name], moments_v[name])
    return (loss, grad_x[None], *[grads[n] for n in order], *[deltas[n] for n in order],
            *[new_m[n] for n in order], *[new_v[n] for n in order])
```

```python
import math

import jax
import jax.numpy as jnp
from jax import lax
from jax.experimental import pallas as pl
from jax.experimental.pallas import tpu as pltpu

F32 = jnp.float32
BF16 = jnp.bfloat16

D_MODEL = 1024
A_WIDTH = 2048
A_GROUPS = 8
A_GROUP_DIM = 256
CHUNK = 128
N_HEADS = 16
N_KV = 2
HEAD_DIM = 64
PAIR = 2 * HEAD_DIM
B_WIDTH = 1024
REL_BUCKETS = 32
ALPHA = 4.0 ** 0.25
LN_EPS = 1e-5
NEG_INF = -1e30
SCALE = HEAD_DIM ** -0.5

ADAM_LR = 0.001
ADAM_B1 = 0.9
ADAM_B2 = 0.999
ADAM_EPS = 1e-08
ADAM_WD = 0.01
ADAM_STEP = 10

N_DEV = 8
N_CHIPS = 4
MESH = pl.DeviceIdType.MESH
VMEM_LIMIT = 56 * 1024 * 1024

TM_ATTN = 256
TM_MM = 512
TM_WIN = 1024


def _dot(a, b):
    return jnp.dot(a, b, preferred_element_type=F32)


def _dot_nt(a, b):
    return lax.dot_general(a, b, (((1,), (1,)), ((), ())), preferred_element_type=F32)


def _dot_tn(a, b):
    return lax.dot_general(a, b, (((0,), (0,)), ((), ())), preferred_element_type=F32)


def _ln_fwd(r):
    mu = jnp.mean(r, axis=-1, keepdims=True)
    rc = r - mu
    var = jnp.mean(rc * rc, axis=-1, keepdims=True)
    rstd = lax.rsqrt(var + LN_EPS)
    return rc * rstd, rstd


def _ln_bwd(dxh, xh, rstd):
    m1 = jnp.mean(dxh, axis=-1, keepdims=True)
    m2 = jnp.mean(dxh * xh, axis=-1, keepdims=True)
    return rstd * (dxh - m1 - xh * m2)


def _silu_parts(z):
    sg = jax.nn.sigmoid(z)
    return z * sg, sg * (1.0 + z * (1.0 - sg))


def _dup_halves(blk):
    sw = pltpu.roll(blk, HEAD_DIM, 1)
    lo = lax.broadcasted_iota(jnp.int32, blk.shape, 1) < HEAD_DIM
    return jnp.where(lo, blk, sw), jnp.where(lo, sw, blk)


def _fold_halves(blk):
    return blk + pltpu.roll(blk, HEAD_DIM, 1)


def _resident(shape):
    nd = len(shape)
    return pl.BlockSpec(shape, lambda *_: (0,) * nd, pipeline_mode=pl.Buffered(1))


def _const(shape):
    nd = len(shape)
    return pl.BlockSpec(shape, lambda *_: (0,) * nd)


def _rows(tm, cols):
    return pl.BlockSpec((tm, cols), lambda i: (i, 0))


def _params(sem=("arbitrary",)):
    return pltpu.CompilerParams(dimension_semantics=sem, vmem_limit_bytes=VMEM_LIMIT)


def _spatial_mix(ws_ref, bsp_ref, vn, s_scr, n_chunks):
    tri = (lax.broadcasted_iota(jnp.int32, (CHUNK, CHUNK), 0)
           >= lax.broadcasted_iota(jnp.int32, (CHUNK, CHUNK), 1))
    for g in range(A_GROUPS):
        wsg = jnp.where(tri, ws_ref[g], 0.0).astype(BF16)
        cols = slice(g * A_GROUP_DIM, (g + 1) * A_GROUP_DIM)
        for ci in range(n_chunks):
            rows = slice(ci * CHUNK, (ci + 1) * CHUNK)
            s_scr[rows, cols] = _dot(wsg, vn[rows, cols]) + bsp_ref[:, g:g + 1]


def _layer_a_fwd(x, w_in, lng, lnb, ws, bsp_t, after):
    t_len = x.shape[0]
    tm = TM_ATTN

    def body(x_ref, win_ref, lng_ref, lnb_ref, ws_ref, bsp_ref, after_ref,
             xb_ref, u_ref, vh_ref, z_ref, rv_ref, y_ref, s_scr):
        xb = x_ref[...].astype(BF16)
        xb_ref[...] = xb
        u = _dot(xb, win_ref[:, 0:A_WIDTH])
        v = _dot(xb, win_ref[:, A_WIDTH:2 * A_WIDTH])
        z = _dot(xb, win_ref[:, 2 * A_WIDTH:3 * A_WIDTH])
        vh, rv = _ln_fwd(v)
        vn = (vh * lng_ref[...] + lnb_ref[...]).astype(BF16)
        _spatial_mix(ws_ref, bsp_ref, vn, s_scr, tm // CHUNK)
        sz, _ = _silu_parts(z)
        y_ref[...] = (u * s_scr[...] * sz).astype(BF16)
        u_ref[...] = u.astype(BF16)
        vh_ref[...] = vh.astype(BF16)
        z_ref[...] = z.astype(BF16)
        rv_ref[...] = rv

    wide = jax.ShapeDtypeStruct((t_len, A_WIDTH), BF16)
    return pl.pallas_call(
        body, name="layer_a_fwd", grid=(t_len // tm,),
        in_specs=[_rows(tm, D_MODEL), _resident(w_in.shape), _const(lng.shape), _const(lnb.shape), _const(ws.shape),
                  _const(bsp_t.shape), _const(after.shape)],
        out_specs=[_rows(tm, D_MODEL), _rows(tm, A_WIDTH), _rows(tm, A_WIDTH), _rows(tm, A_WIDTH),
                   _rows(tm, 1), _rows(tm, A_WIDTH)],
        out_shape=[jax.ShapeDtypeStruct((t_len, D_MODEL), BF16), wide, wide, wide,
                   jax.ShapeDtypeStruct((t_len, 1), F32), wide],
        scratch_shapes=[pltpu.VMEM((tm, A_WIDTH), F32)],
        compiler_params=_params(),
    )(x, w_in, lng, lnb, ws, bsp_t, after)


def _layer_b_proj(x, y, w_out_a, g1, b1, w_in, w_kv):
    t_len = x.shape[0]
    tm = TM_MM

    def body(x_ref, y_ref, wout_ref, g_ref, b_ref, win_ref, wkv_ref, xh_ref, r1_ref, q_ref, z_ref, kd_ref, vd_ref):
        xh, r1 = _ln_fwd(ALPHA * x_ref[...] + _dot(y_ref[...], wout_ref[...]))
        xh_ref[...] = xh
        r1_ref[...] = r1
        h1 = (xh * g_ref[...] + b_ref[...]).astype(BF16)
        q_ref[...] = (_dot(h1, win_ref[:, 0:B_WIDTH]) * SCALE).astype(BF16)
        z_ref[...] = _dot(h1, win_ref[:, B_WIDTH:2 * B_WIDTH]).astype(BF16)
        kv = _dot(h1, wkv_ref[...])
        k0, k1 = _dup_halves(kv[:, 0:PAIR])
        v0, v1 = _dup_halves(kv[:, PAIR:2 * PAIR])
        kd_ref[:, 0:PAIR] = k0.astype(BF16)
        kd_ref[:, PAIR:2 * PAIR] = k1.astype(BF16)
        vd_ref[:, 0:PAIR] = v0.astype(BF16)
        vd_ref[:, PAIR:2 * PAIR] = v1.astype(BF16)

    return pl.pallas_call(
        body, name="layer_b_proj", grid=(t_len // tm,),
        in_specs=[_rows(tm, D_MODEL), _rows(tm, A_WIDTH), _resident(w_out_a.shape), _const(g1.shape),
                  _const(b1.shape), _resident(w_in.shape), _resident(w_kv.shape)],
        out_specs=[_rows(tm, D_MODEL), _rows(tm, 1), _rows(tm, B_WIDTH), _rows(tm, B_WIDTH), _rows(tm, 2 * PAIR),
                   _rows(tm, 2 * PAIR)],
        out_shape=[jax.ShapeDtypeStruct((t_len, D_MODEL), F32), jax.ShapeDtypeStruct((t_len, 1), F32),
                   jax.ShapeDtypeStruct((t_len, B_WIDTH), BF16), jax.ShapeDtypeStruct((t_len, B_WIDTH), BF16),
                   jax.ShapeDtypeStruct((t_len, 2 * PAIR), BF16), jax.ShapeDtypeStruct((t_len, 2 * PAIR), BF16)],
        compiler_params=_params(),
    )(x, y, w_out_a, g1, b1, w_in, w_kv)


GROUP = N_HEADS // N_KV
GROUP_Q = GROUP * CHUNK


def _window_tables():
    j = jnp.arange(2 * CHUNK, dtype=jnp.int32)[:, None]
    t = jnp.arange(CHUNK, dtype=jnp.int32)[None, :]
    dist = t + CHUNK - j
    inside = (dist >= 0) & (dist < CHUNK)
    return jnp.stack([inside & (j >= CHUNK), inside]).astype(F32)


def _band(ref, chunk_index, kvh):
    prev0 = pl.multiple_of(jnp.maximum(chunk_index - 1, 0) * CHUNK, CHUNK)
    cur0 = pl.multiple_of(chunk_index * CHUNK, CHUNK)
    cols = slice(kvh * PAIR, (kvh + 1) * PAIR)
    return jnp.concatenate([ref[pl.ds(prev0, CHUNK), cols], ref[pl.ds(cur0, CHUNK), cols]], axis=0)


def _group_tables(bias_ref, win_ref, sink_ref, chunk_index, kvh):
    bias = jnp.concatenate([bias_ref[kvh * GROUP + j] for j in range(GROUP)], axis=1)
    win = win_ref[jnp.minimum(chunk_index, 1)]
    mask = jnp.concatenate([win] * GROUP, axis=1) > 0.5
    sink = jnp.concatenate([jnp.full((1, CHUNK), sink_ref[0, kvh * GROUP + j], F32) for j in range(GROUP)], axis=1)
    return bias, mask, sink


def _attn_probs(qs, kband, bias, mask, sink):
    logits = jnp.where(mask, _dot_nt(kband, qs) + bias, NEG_INF)
    m = jnp.maximum(jnp.max(logits, axis=0, keepdims=True), sink)
    e = jnp.exp(logits - m)
    es = jnp.exp(sink - m)
    inv = 1.0 / (jnp.sum(e, axis=0, keepdims=True) + es)
    return e * inv, es * inv


def _half_mask():
    return lax.broadcasted_iota(jnp.int32, (CHUNK, PAIR), 1) < HEAD_DIM


def _stack_heads(src_ref, rows, kvh, dst_scr, lo):
    for j in range(GROUP):
        h = kvh * GROUP + j
        blk = src_ref[rows, (h // 2) * PAIR:(h // 2 + 1) * PAIR].astype(F32)
        keep = lo if h % 2 == 0 else ~lo
        dst_scr[j * CHUNK:(j + 1) * CHUNK, :] = jnp.where(keep, blk, 0.0).astype(BF16)


def _probs_spec(tm):
    return pl.BlockSpec((tm // CHUNK, N_KV, 2 * CHUNK, GROUP_Q), lambda i: (i, 0, 0, 0))


def _sink_probs_spec():
    return pl.BlockSpec((1, 8, GROUP_Q), lambda i: (i, 0, 0))


def _unstack_pairs(stacked, pp, lo):
    return jnp.where(lo, stacked[(2 * pp) * CHUNK:(2 * pp + 1) * CHUNK], stacked[(2 * pp + 1) * CHUNK:(2 * pp + 2) * CHUNK])


def _layer_b_fwd(q, zb, kd, vd, bias, win, sinks, xh1, g1, b1, w_out, g2, b2, tgt):
    t_len = q.shape[0]
    tm = TM_ATTN

    def body(q_ref, z_ref, kd_ref, vd_ref, bias_ref, win_ref, sink_ref, xh_ref, g1_ref, b1_ref, wout_ref, g2_ref,
             b2_ref, tgt_ref, o_ref, p_ref, ps_ref, dr_ref, loss_ref, dg_ref, db_ref, o_scr, qs_scr):
        i = pl.program_id(0)

        @pl.when(i == 0)
        def _():
            loss_ref[...] = jnp.zeros_like(loss_ref)
            dg_ref[...] = jnp.zeros_like(dg_ref)
            db_ref[...] = jnp.zeros_like(db_ref)

        lo = _half_mask()
        ps_ref[...] = jnp.zeros_like(ps_ref)
        for ci in range(tm // CHUNK):
            cg = i * (tm // CHUNK) + ci
            rows = slice(ci * CHUNK, (ci + 1) * CHUNK)
            for kvh in range(N_KV):
                kband = _band(kd_ref, cg, kvh)
                vband = _band(vd_ref, cg, kvh)
                bias_g, mask, sink = _group_tables(bias_ref, win_ref, sink_ref, cg, kvh)
                _stack_heads(q_ref, rows, kvh, qs_scr, lo)
                p, p_sink = _attn_probs(qs_scr[...], kband, bias_g, mask, sink)
                p = p.astype(BF16)
                p_ref[ci, kvh] = p
                ps_ref[0, ci * N_KV + kvh:ci * N_KV + kvh + 1, :] = p_sink
                o_stack = _dot_tn(p, vband)
                for pp in range(GROUP // 2):
                    pair = kvh * (GROUP // 2) + pp
                    o_scr[rows, pair * PAIR:(pair + 1) * PAIR] = _unstack_pairs(o_stack, pp, lo)
        o = o_scr[...]
        o_ref[...] = o.astype(BF16)
        sz, _ = _silu_parts(z_ref[...].astype(F32))
        y = (o * sz).astype(BF16)
        h1 = xh_ref[...] * g1_ref[...] + b1_ref[...]
        r = ALPHA * h1 + _dot(y, wout_ref[...])
        xh2, rstd2 = _ln_fwd(r)
        diff = xh2 * g2_ref[...] + b2_ref[...] - tgt_ref[...]
        loss_ref[...] += jnp.sum(diff * diff, axis=0, keepdims=True)
        dh2 = diff * (1.0 / D_MODEL)
        dg_ref[...] += jnp.sum(dh2 * xh2, axis=0, keepdims=True)
        db_ref[...] += jnp.sum(dh2, axis=0, keepdims=True)
        dr_ref[...] = _ln_bwd(dh2 * g2_ref[...], xh2, rstd2)

    vec = jax.ShapeDtypeStruct((1, D_MODEL), F32)
    return pl.pallas_call(
        body, name="layer_b_fwd", grid=(t_len // tm,),
        in_specs=[_rows(tm, B_WIDTH), _rows(tm, B_WIDTH), _resident(kd.shape), _resident(vd.shape),
                  _resident(bias.shape), _resident(win.shape), pl.BlockSpec(memory_space=pltpu.SMEM),
                  _rows(tm, D_MODEL), _const(g1.shape), _const(b1.shape), _resident(w_out.shape), _const(g2.shape),
                  _const(b2.shape), _rows(tm, D_MODEL)],
        out_specs=[_rows(tm, B_WIDTH), _probs_spec(tm), _sink_probs_spec(), _rows(tm, D_MODEL), _const((1, D_MODEL)),
                   _const((1, D_MODEL)), _const((1, D_MODEL))],
        out_shape=[jax.ShapeDtypeStruct((t_len, B_WIDTH), BF16),
                   jax.ShapeDtypeStruct((t_len // CHUNK, N_KV, 2 * CHUNK, GROUP_Q), BF16),
                   jax.ShapeDtypeStruct((t_len // tm, 8, GROUP_Q), F32),
                   jax.ShapeDtypeStruct((t_len, D_MODEL), F32), vec, vec, vec],
        scratch_shapes=[pltpu.VMEM((tm, B_WIDTH), F32), pltpu.VMEM((GROUP_Q, PAIR), BF16)],
        compiler_params=_params(),
    )(q, zb, kd, vd, bias, win, sinks, xh1, g1, b1, w_out, g2, b2, tgt)


def _layer_b_bwd_attn(dr2, zb, o, q, kd, vd, probs, sink_probs, w_out):
    t_len = q.shape[0]
    tm = TM_ATTN
    n_steps = t_len // tm
    n_chunks = tm // CHUNK

    def body(dr_ref, z_ref, o_ref, q_ref, kd_ref, vd_ref, p_ref, ps_ref, wout_ref,
             dq_ref, dz_ref, dkd_ref, dvd_ref, ck_ref, cv_ref, gw_ref, dsink_ref, dbias_ref,
             do_scr, qs_scr, dos_scr, gw_acc):
        i = pl.program_id(0)

        @pl.when(i == 0)
        def _():
            gw_acc[...] = jnp.zeros_like(gw_acc)
            dsink_ref[...] = jnp.zeros_like(dsink_ref)
            dbias_ref[...] = jnp.zeros_like(dbias_ref)

        drb = dr_ref[...].astype(BF16)
        dy = _dot_nt(drb, wout_ref[...])
        z = z_ref[...].astype(F32)
        sz, dsz = _silu_parts(z)
        o_t = o_ref[...].astype(F32)
        dz_ref[...] = (dy * o_t * dsz).astype(BF16)
        do_scr[...] = (dy * sz).astype(BF16)
        gw_acc[...] += _dot_tn((o_t * sz).astype(BF16), drb)

        lo = _half_mask()
        for kvh in range(N_KV):
            kcols = slice(kvh * PAIR, (kvh + 1) * PAIR)
            dk_bands, dv_bands = [], []
            for ci in range(n_chunks):
                cg = i * n_chunks + ci
                rows = slice(ci * CHUNK, (ci + 1) * CHUNK)
                kband = _band(kd_ref, cg, kvh)
                vband = _band(vd_ref, cg, kvh)
                _stack_heads(q_ref, rows, kvh, qs_scr, lo)
                _stack_heads(do_scr, rows, kvh, dos_scr, lo)
                qs = qs_scr[...]
                dos = dos_scr[...]
                pb = p_ref[ci, kvh]
                p = pb.astype(F32)
                p_sink = ps_ref[0, ci * N_KV + kvh:ci * N_KV + kvh + 1, :]
                dp = _dot_nt(vband, dos)
                delta = jnp.sum(p * dp, axis=0, keepdims=True)
                dlog = p * (dp - delta)
                for j in range(GROUP):
                    dbias_ref[kvh * GROUP + j] += dlog[:, j * CHUNK:(j + 1) * CHUNK]
                dsink_ref[kvh:kvh + 1, :] += -(p_sink * delta)
                ds = dlog.astype(BF16)
                dq_stack = _dot_tn(ds, kband) * SCALE
                for pp in range(GROUP // 2):
                    pair = kvh * (GROUP // 2) + pp
                    dq_ref[rows, pair * PAIR:(pair + 1) * PAIR] = _unstack_pairs(dq_stack, pp, lo).astype(BF16)
                dk_bands.append(_dot(ds, qs))
                dv_bands.append(_dot(pb, dos))
            for bands, out_ref, carry_ref in ((dk_bands, dkd_ref, ck_ref), (dv_bands, dvd_ref, cv_ref)):
                carry_ref[0, :, kcols] = bands[0][0:CHUNK]
                for ci in range(n_chunks):
                    own = bands[ci][CHUNK:2 * CHUNK]
                    if ci + 1 < n_chunks:
                        own = own + bands[ci + 1][0:CHUNK]
                    out_ref[ci * CHUNK:(ci + 1) * CHUNK, kcols] = own

        @pl.when(i == n_steps - 1)
        def _():
            gw_ref[...] = gw_acc[...].astype(BF16)

    carry_spec = pl.BlockSpec((1, CHUNK, 2 * PAIR), lambda i: (i, 0, 0))
    carry_shape = jax.ShapeDtypeStruct((n_steps, CHUNK, 2 * PAIR), F32)
    bias_shape = (N_HEADS, 2 * CHUNK, CHUNK)
    return pl.pallas_call(
        body, name="layer_b_bwd_attn", grid=(n_steps,),
        in_specs=[_rows(tm, D_MODEL), _rows(tm, B_WIDTH), _rows(tm, B_WIDTH), _rows(tm, B_WIDTH),
                  _resident(kd.shape), _resident(vd.shape), _probs_spec(tm), _sink_probs_spec(),
                  _resident(w_out.shape)],
        out_specs=[_rows(tm, B_WIDTH), _rows(tm, B_WIDTH), _rows(tm, 2 * PAIR), _rows(tm, 2 * PAIR),
                   carry_spec, carry_spec, _const(w_out.shape), _const((N_KV, GROUP_Q)), _const(bias_shape)],
        out_shape=[jax.ShapeDtypeStruct((t_len, B_WIDTH), BF16), jax.ShapeDtypeStruct((t_len, B_WIDTH), BF16),
                   jax.ShapeDtypeStruct((t_len, 2 * PAIR), F32), jax.ShapeDtypeStruct((t_len, 2 * PAIR), F32),
                   carry_shape, carry_shape, jax.ShapeDtypeStruct(w_out.shape, BF16),
                   jax.ShapeDtypeStruct((N_KV, GROUP_Q), F32), jax.ShapeDtypeStruct(bias_shape, F32)],
        scratch_shapes=[pltpu.VMEM((tm, B_WIDTH), BF16), pltpu.VMEM((GROUP_Q, PAIR), BF16),
                        pltpu.VMEM((GROUP_Q, PAIR), BF16), pltpu.VMEM(w_out.shape, F32)],
        compiler_params=_params(),
    )(dr2, zb, o, q, kd, vd, probs, sink_probs, w_out)


def _layer_b_bwd_proj(xh1, rstd1, g1, b1, dr2, dq, dzb, dkd, dvd, carry_k, carry_v, w_in, w_kv):
    t_len = xh1.shape[0]
    tm = TM_MM
    n_steps = t_len // tm
    per_tile = tm // TM_ATTN
    n_carry = carry_k.shape[0]

    def body(xh_ref, rstd_ref, g_ref, b_ref, dr2_ref, dq_ref, dz_ref, dkd_ref, dvd_ref, *rest):
        carry_refs = rest[:2 * per_tile]
        win_ref, wkv_ref, dr1_ref, dg_ref, db_ref, gwin_ref, gwkv_ref, acc_in, acc_kv = rest[2 * per_tile:]
        i = pl.program_id(0)

        @pl.when(i == 0)
        def _():
            acc_in[...] = jnp.zeros_like(acc_in)
            acc_kv[...] = jnp.zeros_like(acc_kv)
            dg_ref[...] = jnp.zeros_like(dg_ref)
            db_ref[...] = jnp.zeros_like(db_ref)

        def with_carries(tile_ref, refs):
            parts = []
            for a in range(per_tile):
                parts.append(tile_ref[a * TM_ATTN:(a + 1) * TM_ATTN - CHUNK, :])
                carry = refs[a][0]
                if a == per_tile - 1:
                    carry = jnp.where(i < n_steps - 1, carry, 0.0)
                parts.append(tile_ref[(a + 1) * TM_ATTN - CHUNK:(a + 1) * TM_ATTN, :] + carry)
            return jnp.concatenate(parts, axis=0).astype(BF16)

        xh = xh_ref[...]
        h1 = (xh * g_ref[...] + b_ref[...]).astype(BF16)
        dq_t = dq_ref[...]
        dz_t = dz_ref[...]
        dkd_t = with_carries(dkd_ref, carry_refs[:per_tile])
        dvd_t = with_carries(dvd_ref, carry_refs[per_tile:])
        wkv = wkv_ref[...].astype(F32)
        wk0, wk1 = _dup_halves(wkv[:, 0:PAIR])
        wv0, wv1 = _dup_halves(wkv[:, PAIR:2 * PAIR])
        dh1 = ALPHA * dr2_ref[...]
        dh1 += _dot_nt(dq_t, win_ref[:, 0:B_WIDTH])
        dh1 += _dot_nt(dz_t, win_ref[:, B_WIDTH:2 * B_WIDTH])
        for blk, w in ((dkd_t[:, 0:PAIR], wk0), (dkd_t[:, PAIR:2 * PAIR], wk1),
                       (dvd_t[:, 0:PAIR], wv0), (dvd_t[:, PAIR:2 * PAIR], wv1)):
            dh1 += _dot_nt(blk, w.astype(BF16))
        acc_in[:, 0:B_WIDTH] += _dot_tn(h1, dq_t)
        acc_in[:, B_WIDTH:2 * B_WIDTH] += _dot_tn(h1, dz_t)
        acc_kv[:, 0:2 * PAIR] += _dot_tn(h1, dkd_t)
        acc_kv[:, 2 * PAIR:4 * PAIR] += _dot_tn(h1, dvd_t)
        dg_ref[...] += jnp.sum(dh1 * xh, axis=0, keepdims=True)
        db_ref[...] += jnp.sum(dh1, axis=0, keepdims=True)
        dr1_ref[...] = _ln_bwd(dh1 * g_ref[...], xh, rstd_ref[...])

        @pl.when(i == n_steps - 1)
        def _():
            half_rows = D_MODEL // 2
            shard_cols = 2 * B_WIDTH // N_CHIPS
            for s in range(N_CHIPS):
                for c in range(2):
                    gwin_ref[2 * s + c] = acc_in[c * half_rows:(c + 1) * half_rows,
                                                 s * shard_cols:(s + 1) * shard_cols].astype(BF16)
            lo = lax.broadcasted_iota(jnp.int32, (D_MODEL, PAIR), 1) < HEAD_DIM
            for n in range(2):
                f0 = _fold_halves(acc_kv[:, (2 * n) * PAIR:(2 * n + 1) * PAIR])
                f1 = _fold_halves(acc_kv[:, (2 * n + 1) * PAIR:(2 * n + 2) * PAIR])
                gwkv_ref[:, n * PAIR:(n + 1) * PAIR] = jnp.where(lo, f0, f1).astype(BF16)

    vec = jax.ShapeDtypeStruct((1, D_MODEL), F32)
    gwin_shape = (N_DEV, D_MODEL // 2, 2 * B_WIDTH // N_CHIPS)

    def carry_spec(a):
        return pl.BlockSpec((1, CHUNK, 2 * PAIR), lambda i: (jnp.minimum(per_tile * i + a + 1, n_carry - 1), 0, 0))

    carry_specs = [carry_spec(a) for a in range(per_tile)]
    return pl.pallas_call(
        body, name="layer_b_bwd_proj", grid=(n_steps,),
        in_specs=[_rows(tm, D_MODEL), _rows(tm, 1), _const(g1.shape), _const(b1.shape), _rows(tm, D_MODEL),
                  _rows(tm, B_WIDTH), _rows(tm, B_WIDTH), _rows(tm, 2 * PAIR), _rows(tm, 2 * PAIR)]
        + carry_specs + carry_specs + [_resident(w_in.shape), _resident(w_kv.shape)],
        out_specs=[_rows(tm, D_MODEL), _const((1, D_MODEL)), _const((1, D_MODEL)), _const(gwin_shape),
                   _const(w_kv.shape)],
        out_shape=[jax.ShapeDtypeStruct((t_len, D_MODEL), F32), vec, vec,
                   jax.ShapeDtypeStruct(gwin_shape, BF16), jax.ShapeDtypeStruct(w_kv.shape, BF16)],
        scratch_shapes=[pltpu.VMEM(w_in.shape, F32), pltpu.VMEM((D_MODEL, 4 * PAIR), F32)],
        compiler_params=_params(),
    )(xh1, rstd1, g1, b1, dr2, dq, dzb, dkd, dvd, *([carry_k] * per_tile), *([carry_v] * per_tile), w_in, w_kv)


def _layer_a_bwd_mix(dr1, u, vh, z, rv, w_out, lng, lnb, ws, bsp_t, after):
    t_len = u.shape[0]
    tm = TM_ATTN
    n_steps = t_len // tm

    def body(dr_ref, u_ref, vh_ref, z_ref, rv_ref, wout_ref, lng_ref, lnb_ref, ws_ref, bsp_ref, after_ref,
             dp_ref, gw_ref, dws_ref, dbsp_ref, dgs_ref, dbs_ref, s_scr, dvn_scr, gw_acc):
        i = pl.program_id(0)

        @pl.when(i == 0)
        def _():
            gw_acc[...] = jnp.zeros_like(gw_acc)
            dws_ref[...] = jnp.zeros_like(dws_ref)
            dbsp_ref[...] = jnp.zeros_like(dbsp_ref)
            dgs_ref[...] = jnp.zeros_like(dgs_ref)
            dbs_ref[...] = jnp.zeros_like(dbs_ref)

        drb = dr_ref[...].astype(BF16)
        dy = _dot_nt(drb, wout_ref[...])
        u_t = u_ref[...].astype(F32)
        vh_t = vh_ref[...].astype(F32)
        vn = (vh_t * lng_ref[...] + lnb_ref[...]).astype(BF16)
        _spatial_mix(ws_ref, bsp_ref, vn, s_scr, tm // CHUNK)
        s = s_scr[...]
        sz, dsz = _silu_parts(z_ref[...].astype(F32))
        a = s * sz
        t = dy * u_t
        gw_acc[...] += _dot_tn((u_t * a).astype(BF16), drb)
        dp_ref[:, 0:A_WIDTH] = (dy * a).astype(BF16)
        dp_ref[:, 2 * A_WIDTH:3 * A_WIDTH] = (t * s * dsz).astype(BF16)
        ds = (t * sz).astype(BF16)

        group_of = lax.broadcasted_iota(jnp.int32, (A_WIDTH, CHUNK), 0) // A_GROUP_DIM
        indicator = (group_of == lax.broadcasted_iota(jnp.int32, (A_WIDTH, CHUNK), 1)).astype(BF16)
        group_sums = _dot(ds, indicator)
        total = group_sums[0:CHUNK]
        for ci in range(1, tm // CHUNK):
            total += group_sums[ci * CHUNK:(ci + 1) * CHUNK]
        dbsp_ref[...] += total

        tri = (lax.broadcasted_iota(jnp.int32, (CHUNK, CHUNK), 0)
               >= lax.broadcasted_iota(jnp.int32, (CHUNK, CHUNK), 1))
        for g in range(A_GROUPS):
            wsg = jnp.where(tri, ws_ref[g], 0.0).astype(BF16)
            cols = slice(g * A_GROUP_DIM, (g + 1) * A_GROUP_DIM)
            dws_g = jnp.zeros((CHUNK, CHUNK), F32)
            for ci in range(tm // CHUNK):
                rows = slice(ci * CHUNK, (ci + 1) * CHUNK)
                ds_b = ds[rows, cols]
                dws_g += _dot_nt(ds_b, vn[rows, cols])
                dvn_scr[rows, cols] = _dot_tn(wsg, ds_b)
            dws_ref[g] += jnp.where(tri, dws_g, 0.0)
        dvn = dvn_scr[...]
        dgs_ref[...] += jnp.sum(dvn * vh_t, axis=0, keepdims=True)
        dbs_ref[...] += jnp.sum(dvn, axis=0, keepdims=True)
        dp_ref[:, A_WIDTH:2 * A_WIDTH] = _ln_bwd(dvn * lng_ref[...], vh_t, rv_ref[...]).astype(BF16)

        @pl.when(i == n_steps - 1)
        def _():
            gw_ref[...] = gw_acc[...].astype(BF16)

    wide = jax.ShapeDtypeStruct((1, A_WIDTH), F32)
    return pl.pallas_call(
        body, name="layer_a_bwd_mix", grid=(n_steps,),
        in_specs=[_rows(tm, D_MODEL), _rows(tm, A_WIDTH), _rows(tm, A_WIDTH), _rows(tm, A_WIDTH), _rows(tm, 1),
                  _resident(w_out.shape), _const(lng.shape), _const(lnb.shape), _const(ws.shape),
                  _const(bsp_t.shape), _const(after.shape)],
        out_specs=[_rows(tm, 3 * A_WIDTH), _const(w_out.shape), _const(ws.shape), _const((CHUNK, CHUNK)),
                   _const((1, A_WIDTH)), _const((1, A_WIDTH))],
        out_shape=[jax.ShapeDtypeStruct((t_len, 3 * A_WIDTH), BF16), jax.ShapeDtypeStruct(w_out.shape, BF16),
                   jax.ShapeDtypeStruct(ws.shape, F32), jax.ShapeDtypeStruct((CHUNK, CHUNK), F32),
                   wide, wide],
        scratch_shapes=[pltpu.VMEM((tm, A_WIDTH), F32), pltpu.VMEM((tm, A_WIDTH), F32),
                        pltpu.VMEM(w_out.shape, F32)],
        compiler_params=_params(),
    )(dr1, u, vh, z, rv, w_out, lng, lnb, ws, bsp_t, after)


def _layer_a_bwd_dx(dr1, dp, w_in, after):
    t_len = dr1.shape[0]
    tm = TM_MM

    def body(dr_ref, dp_ref, win_ref, after_ref, dx_ref):
        dx_ref[...] = ALPHA * dr_ref[...] + _dot_nt(dp_ref[...], win_ref[...])

    return pl.pallas_call(
        body, name="layer_a_bwd_dx", grid=(t_len // tm,),
        in_specs=[_rows(tm, D_MODEL), _rows(tm, 3 * A_WIDTH), _resident(w_in.shape), _const(after.shape)],
        out_specs=_rows(tm, D_MODEL),
        out_shape=jax.ShapeDtypeStruct((t_len, D_MODEL), F32),
        compiler_params=_params(),
    )(dr1, dp, w_in, after)


def _layer_a_bwd_win(xb, dp, after):
    t_len = xb.shape[0]
    tm = TM_WIN
    n_steps = t_len // tm
    shard_cols = 3 * A_WIDTH // N_CHIPS
    half_rows = D_MODEL // 2

    def body(xb_ref, dp_ref, after_ref, gw_ref, acc):
        i = pl.program_id(1)

        @pl.when(i == 0)
        def _():
            acc[...] = jnp.zeros_like(acc)

        acc[...] += _dot_tn(xb_ref[...], dp_ref[...])

        @pl.when(i == n_steps - 1)
        def _():
            for c in range(2):
                gw_ref[0, c] = acc[c * half_rows:(c + 1) * half_rows, :].astype(BF16)

    return pl.pallas_call(
        body, name="layer_a_bwd_win", grid=(N_CHIPS, n_steps),
        in_specs=[pl.BlockSpec((tm, D_MODEL), lambda j, i: (i, 0)),
                  pl.BlockSpec((tm, shard_cols), lambda j, i: (i, j)), _const(after.shape)],
        out_specs=pl.BlockSpec((1, 2, half_rows, shard_cols), lambda j, i: (j, 0, 0, 0)),
        out_shape=jax.ShapeDtypeStruct((N_CHIPS, 2, half_rows, shard_cols), BF16),
        scratch_shapes=[pltpu.VMEM((D_MODEL, shard_cols), F32)],
        compiler_params=_params(("arbitrary", "arbitrary")),
    )(xb, dp, after)


def _bucket_onehot():
    t = jnp.arange(CHUNK, dtype=jnp.int32)[None, :]
    j = jnp.arange(2 * CHUNK, dtype=jnp.int32)[:, None]
    dist = jnp.clip(t + CHUNK - j, 0, CHUNK - 1)
    max_exact = REL_BUCKETS // 2
    df = jnp.maximum(dist, 1).astype(F32)
    large = max_exact + (jnp.log(df / max_exact) / math.log(CHUNK / max_exact)
                         * (REL_BUCKETS - max_exact)).astype(jnp.int32)
    bucket = jnp.where(dist < max_exact, dist, jnp.minimum(large, REL_BUCKETS - 1))
    onehot = bucket.reshape(1, -1) == jnp.arange(REL_BUCKETS, dtype=jnp.int32)[:, None]
    return onehot.astype(F32)


def _bias_expand(rel_t, onehot):
    def body(rel_ref, oh_ref, out_ref):
        out_ref[...] = jnp.dot(rel_ref[...], oh_ref[...], preferred_element_type=F32,
                               precision=lax.Precision.HIGHEST)

    return pl.pallas_call(
        body, name="bias_expand",
        out_shape=jax.ShapeDtypeStruct((N_HEADS, onehot.shape[1]), F32),
    )(rel_t, onehot)


def _bias_reduce(onehot, dbias):
    def body(oh_ref, db_ref, out_ref):
        out_ref[...] = lax.dot_general(oh_ref[...], db_ref[...], (((1,), (1,)), ((), ())),
                                       preferred_element_type=F32, precision=lax.Precision.HIGHEST)

    return pl.pallas_call(
        body, name="bias_reduce",
        out_shape=jax.ShapeDtypeStruct((REL_BUCKETS, N_HEADS), F32),
    )(onehot, dbias)


def _place():
    return lax.axis_index("x"), lax.axis_index("y"), lax.axis_index("c")


def _shard_window(full_ref, shard_shape, col_sharded, s, half):
    rows, cols = shard_shape
    if col_sharded:
        rsel = pl.ds(0, rows) if half is None else pl.ds(half * (rows // 2), rows // 2)
        return full_ref.at[rsel, pl.ds(s * cols, cols)]
    if half is None:
        return full_ref.at[pl.ds(s * rows, rows), :]
    return full_ref.at[pl.ds(s * rows + half * (rows // 2), rows // 2), :]


def _other_chips(x, y):
    return [(1 - x, y), (x, 1 - y), (1 - x, 1 - y)]


def _gather_weights(shards, col_sharded, fetch, ln_shard):
    n_w = len(shards)
    fetched = [w for w in range(n_w) if fetch[w]]
    full_shapes = []
    for w, cs in zip(shards, col_sharded):
        r, c = w.shape
        full_shapes.append((r, c * N_CHIPS) if cs else (r * N_CHIPS, c))

    def body(*refs):
        in_refs = refs[:n_w]
        ln_ref = refs[n_w]
        full_refs = refs[n_w + 1:2 * n_w + 1]
        ln_full = refs[2 * n_w + 1]
        stage = refs[2 * n_w + 2:3 * n_w + 2]
        send_sems, recv_sems, local_sems, ln_send, ln_recv = refs[3 * n_w + 2:]
        x, y, c = _place()
        s_me = 2 * x + y
        chips = _other_chips(x, y)

        def shard_window(w, s, half):
            return _shard_window(full_refs[w], shards[w].shape, col_sharded[w], s, half)

        def stage_half(w, half):
            rows = shards[w].shape[0]
            return stage[w].at[pl.ds(half * (rows // 2), rows // 2), :]

        def ici_copy(w, k, sender_shard, src):
            return pltpu.make_async_remote_copy(
                src_ref=src, dst_ref=shard_window(w, sender_shard, c),
                send_sem=send_sems.at[w * 3 + k], recv_sem=recv_sems.at[w * 3 + k],
                device_id=(*chips[k], c), device_id_type=MESH)

        def d2d_copy(w, k, half):
            s_k = 2 * chips[k][0] + chips[k][1]
            win = shard_window(w, s_k, half)
            return pltpu.make_async_remote_copy(
                src_ref=win, dst_ref=win,
                send_sem=send_sems.at[3 * n_w + w * 3 + k], recv_sem=recv_sems.at[3 * n_w + w * 3 + k],
                device_id=(x, y, 1 - c), device_id_type=MESH)

        def ln_copy(k, slot):
            return pltpu.make_async_remote_copy(
                src_ref=ln_ref, dst_ref=ln_full.at[slot], send_sem=ln_send.at[k], recv_sem=ln_recv.at[k],
                device_id=(*chips[k], c), device_id_type=MESH)

        for w in range(n_w):
            stage[w][...] = in_refs[w][...].astype(BF16)
        own = [pltpu.make_async_copy(stage[w], shard_window(w, s_me, None), local_sems.at[w]) for w in range(n_w)]
        for cp in own:
            cp.start()
        ln_full[s_me] = ln_ref[...]
        first = [ici_copy(w, k, s_me, stage_half(w, c)) for w in fetched for k in range(3)]
        first += [ln_copy(k, s_me) for k in range(3)]
        for cp in first:
            cp.start()
        passed = []
        for w in fetched:
            for k in range(3):
                s_k = 2 * chips[k][0] + chips[k][1]
                ici_copy(w, k, s_k, stage_half(w, c)).wait_recv()
                fwd = d2d_copy(w, k, c)
                fwd.start()
                passed.append(fwd)
        for w in fetched:
            for k in range(3):
                d2d_copy(w, k, 1 - c).wait_recv()
        for k in range(3):
            ln_copy(k, 2 * chips[k][0] + chips[k][1]).wait_recv()
        for cp in first + passed:
            cp.wait_send()
        for cp in own:
            cp.wait()

    vmem = pl.BlockSpec(memory_space=pltpu.VMEM)
    hbm = pl.BlockSpec(memory_space=pl.ANY)
    return pl.pallas_call(
        body, name="gather_weights",
        in_specs=[vmem] * (n_w + 1),
        out_specs=[hbm] * n_w + [vmem],
        out_shape=[jax.ShapeDtypeStruct(s, BF16) for s in full_shapes]
        + [jax.ShapeDtypeStruct((N_CHIPS,) + ln_shard.shape, F32)],
        scratch_shapes=[pltpu.VMEM(w.shape, BF16) for w in shards]
        + [pltpu.SemaphoreType.DMA((6 * n_w,)), pltpu.SemaphoreType.DMA((6 * n_w,)),
           pltpu.SemaphoreType.DMA((n_w,)), pltpu.SemaphoreType.DMA((3,)), pltpu.SemaphoreType.DMA((3,))],
        compiler_params=pltpu.CompilerParams(vmem_limit_bytes=VMEM_LIMIT),
    )(*shards, ln_shard)


def _fetch_copy(full_ref, shard_shape, col_sharded, sender_shard, send_sems, recv_sems, idx, chip, c):
    win = _shard_window(full_ref, shard_shape, col_sharded, sender_shard, None)
    return pltpu.make_async_remote_copy(src_ref=win, dst_ref=win, send_sem=send_sems.at[idx],
                                        recv_sem=recv_sems.at[idx], device_id=(*chip, c), device_id_type=MESH)


def _fetch_start(fulls, shard_shapes, col_sharded):
    n = len(fulls)

    def body(*refs):
        full = refs[:n]
        send_sems, recv_sems = refs[n], refs[n + 1]
        token = refs[-1]
        x, y, c = _place()
        for w in range(n):
            for k, chip in enumerate(_other_chips(x, y)):
                _fetch_copy(full[w], shard_shapes[w], col_sharded[w], 2 * x + y, send_sems, recv_sems, w * 3 + k,
                            chip, c).start()
        token[...] = jnp.zeros_like(token)

    outs = pl.pallas_call(
        body, name="fetch_start",
        out_shape=(pltpu.SemaphoreType.DMA((3 * n,)), pltpu.SemaphoreType.DMA((3 * n,)),
                   *[pltpu.HBM(f.shape, f.dtype) for f in fulls], jax.ShapeDtypeStruct((8, 128), F32)),
        in_specs=[_HBM] * n,
        out_specs=(_SEM, _SEM, *([_HBM] * n), pl.BlockSpec(memory_space=pltpu.VMEM)),
        input_output_aliases={i: 2 + i for i in range(n)},
        compiler_params=pltpu.CompilerParams(has_side_effects=pltpu.SideEffectType.DATAFLOW_SIDE_EFFECTING),
    )(*[pltpu.with_memory_space_constraint(f, pltpu.HBM) for f in fulls])
    return dict(send=outs[0], recv=outs[1], full=list(outs[2:2 + n])), outs[-1]


def _fetch_wait(group, shard_shapes, col_sharded, after):
    n = len(group["full"])

    def body(*refs):
        full = refs[:n]
        send_sems, recv_sems = refs[n], refs[n + 1]
        x, y, c = _place()
        for w in range(n):
            for k, chip in enumerate(_other_chips(x, y)):
                _fetch_copy(full[w], shard_shapes[w], col_sharded[w], 2 * x + y, send_sems, recv_sems, w * 3 + k,
                            chip, c).wait_send()
                _fetch_copy(full[w], shard_shapes[w], col_sharded[w], 2 * chip[0] + chip[1], send_sems, recv_sems,
                            w * 3 + k, chip, c).wait_recv()

    outs = pl.pallas_call(
        body, name="fetch_wait", out_shape=tuple(pltpu.HBM(f.shape, f.dtype) for f in group["full"]),
        in_specs=[_HBM] * n + [_SEM, _SEM, pl.BlockSpec(memory_space=pl.ANY)],
        out_specs=tuple([_HBM] * n), input_output_aliases={i: i for i in range(n)},
        compiler_params=pltpu.CompilerParams(has_side_effects=pltpu.SideEffectType.DATAFLOW_SIDE_EFFECTING),
    )(*group["full"], group["send"], group["recv"], after)
    return list(outs)


_HBM = pl.BlockSpec(memory_space=pltpu.HBM)
_SEM = pl.BlockSpec(memory_space=pltpu.SEMAPHORE)
_N_PEER = N_DEV - 1


def _peer(x, y, c, k):
    return (x + (k >> 2)) % 2, (y + ((k >> 1) & 1)) % 2, (c + (k & 1)) % 2


def _exchange_copy(src_ref, land_ref, sliced, send_sems, recv_sems, idx, x, y, c, k):
    px, py, pc = _peer(x, y, c, k)
    src = src_ref.at[4 * px + 2 * py + pc] if sliced else src_ref
    return pltpu.make_async_remote_copy(
        src_ref=src, dst_ref=land_ref.at[4 * x + 2 * y + c],
        send_sem=send_sems.at[idx], recv_sem=recv_sems.at[idx], device_id=(px, py, pc), device_id_type=MESH)


def _exchange_start(tag, arrays, sliced):
    n = len(arrays)
    lands = [lax.empty(a.shape if s else (N_DEV,) + a.shape, a.dtype) for a, s in zip(arrays, sliced)]

    def body(*refs):
        src, land = refs[:n], refs[n:2 * n]
        send_sems, recv_sems = refs[2 * n], refs[2 * n + 1]
        token = refs[-1]
        x, y, c = _place()
        for w in range(n):
            for k in range(1, N_DEV):
                _exchange_copy(src[w], land[w], sliced[w], send_sems, recv_sems, w * _N_PEER + k - 1, x, y, c, k).start()
        token[...] = jnp.zeros_like(token)

    outs = pl.pallas_call(
        body, name="exchange_start_" + tag,
        out_shape=(pltpu.SemaphoreType.DMA((n * _N_PEER,)), pltpu.SemaphoreType.DMA((n * _N_PEER,)),
                   *[pltpu.HBM(a.shape, a.dtype) for a in arrays], *[pltpu.HBM(l.shape, l.dtype) for l in lands],
                   jax.ShapeDtypeStruct((8, 128), F32)),
        in_specs=[_HBM] * (2 * n),
        out_specs=(_SEM, _SEM, *([_HBM] * (2 * n)), pl.BlockSpec(memory_space=pltpu.VMEM)),
        input_output_aliases={i: 2 + i for i in range(2 * n)},
        compiler_params=pltpu.CompilerParams(has_side_effects=pltpu.SideEffectType.DATAFLOW_SIDE_EFFECTING),
    )(*[pltpu.with_memory_space_constraint(a, pltpu.HBM) for a in arrays],
      *[pltpu.with_memory_space_constraint(l, pltpu.HBM) for l in lands])
    return dict(send=outs[0], recv=outs[1], src=list(outs[2:2 + n]), land=list(outs[2 + n:2 + 2 * n]),
                sliced=list(sliced)), outs[-1]


def _exchange_wait(groups, after):
    counts = [len(g["src"]) for g in groups]
    total = sum(counts)

    def body(*refs):
        pos = 0
        x, y, c = _place()
        for g, n in zip(groups, counts):
            src, land = refs[pos:pos + n], refs[pos + n:pos + 2 * n]
            send_sems, recv_sems = refs[pos + 2 * n], refs[pos + 2 * n + 1]
            pos += 2 * n + 2
            for w in range(n):
                for k in range(1, N_DEV):
                    cp = _exchange_copy(src[w], land[w], g["sliced"][w], send_sems, recv_sems,
                                        w * _N_PEER + k - 1, x, y, c, k)
                    cp.wait_send()
                    cp.wait_recv()

    operands, in_specs, aliases, out_shape = [], [], {}, []
    for g in groups:
        for a in g["src"] + g["land"]:
            aliases[len(operands)] = len(out_shape)
            out_shape.append(pltpu.HBM(a.shape, a.dtype))
            operands.append(a)
            in_specs.append(_HBM)
        operands += [g["send"], g["recv"]]
        in_specs += [_SEM, _SEM]
    operands.append(after)
    in_specs.append(pl.BlockSpec(memory_space=pl.ANY))
    outs = pl.pallas_call(
        body, name="exchange_wait", out_shape=tuple(out_shape), in_specs=in_specs,
        out_specs=tuple([_HBM] * (2 * total)), input_output_aliases=aliases,
        compiler_params=pltpu.CompilerParams(has_side_effects=pltpu.SideEffectType.DATAFLOW_SIDE_EFFECTING),
    )(*operands)
    srcs, lands, pos = [], [], 0
    for n in counts:
        srcs += list(outs[pos:pos + n])
        lands += list(outs[pos + n:pos + 2 * n])
        pos += 2 * n
    return srcs, lands


def _sum_and_swap(pieces, lands, small, small_land):
    n_w = len(pieces)

    def body(*refs):
        g_refs, land_refs = refs[:n_w], refs[n_w:2 * n_w]
        small_ref, small_land_ref = refs[2 * n_w], refs[2 * n_w + 1]
        out_refs = refs[2 * n_w + 2:3 * n_w + 2]
        small_out = refs[3 * n_w + 2]
        bufs = refs[3 * n_w + 3:4 * n_w + 3]
        small_buf, load_sems, small_sems, swap_send, swap_recv = refs[4 * n_w + 3:]
        x, y, c = _place()
        me = 4 * x + 2 * y + c

        def slot(k):
            px, py, pc = _peer(x, y, c, k)
            return 4 * px + 2 * py + pc

        def swap_copy(w, half):
            rows = pieces[w].shape[1]
            win = out_refs[w].at[pl.ds(pl.multiple_of(half * rows, rows), rows), :]
            return pltpu.make_async_remote_copy(
                src_ref=win, dst_ref=win, send_sem=swap_send.at[w], recv_sem=swap_recv.at[w],
                device_id=(x, y, 1 - c), device_id_type=MESH)

        loads = []
        for w in range(n_w):
            per_w = [pltpu.make_async_copy(g_refs[w].at[me], bufs[w].at[me], load_sems.at[w * N_DEV])]
            per_w += [pltpu.make_async_copy(land_refs[w].at[slot(k)], bufs[w].at[slot(k)], load_sems.at[w * N_DEV + k])
                      for k in range(1, N_DEV)]
            loads.append(per_w)
        small_loads = [pltpu.make_async_copy(small_land_ref.at[slot(k)], small_buf.at[slot(k)], small_sems.at[k - 1])
                       for k in range(1, N_DEV)]
        for cp in [cp for per_w in loads for cp in per_w] + small_loads:
            cp.start()
        small_buf[me] = small_ref[...]
        swaps = []
        for w in range(n_w):
            for cp in loads[w]:
                cp.wait()
            rows = pieces[w].shape[1]
            total = bufs[w][0].astype(F32)
            for p in range(1, N_DEV):
                total += bufs[w][p].astype(F32)
            out_refs[w][pl.ds(pl.multiple_of(c * rows, rows), rows), :] = total
            sw = swap_copy(w, c)
            sw.start()
            swaps.append(sw)
        for cp in small_loads:
            cp.wait()
        total = small_buf[0]
        for p in range(1, N_DEV):
            total += small_buf[p]
        small_out[...] = total
        for w in range(n_w):
            swap_copy(w, 1 - c).wait_recv()
        for sw in swaps:
            sw.wait_send()

    vmem = pl.BlockSpec(memory_space=pltpu.VMEM)
    hbm = pl.BlockSpec(memory_space=pl.ANY)
    return pl.pallas_call(
        body, name="sum_and_swap",
        in_specs=[hbm] * (2 * n_w) + [vmem, hbm],
        out_specs=[vmem] * (n_w + 1),
        out_shape=[jax.ShapeDtypeStruct((2 * p.shape[1], p.shape[2]), F32) for p in pieces]
        + [jax.ShapeDtypeStruct(small.shape, F32)],
        scratch_shapes=[pltpu.VMEM(p.shape, BF16) for p in pieces]
        + [pltpu.VMEM((N_DEV,) + small.shape, F32),
           pltpu.SemaphoreType.DMA((n_w * N_DEV,)), pltpu.SemaphoreType.DMA((_N_PEER,)),
           pltpu.SemaphoreType.DMA((n_w,)), pltpu.SemaphoreType.DMA((n_w,))],
        compiler_params=pltpu.CompilerParams(vmem_limit_bytes=VMEM_LIMIT),
    )(*pieces, *lands, small, small_land)


def _adamw(label, w, g, m, v):
    shape = w.shape
    cols = shape[-1]
    rows = w.size // cols
    args = [a.reshape(rows, cols) for a in (w, g, m, v)]
    c1 = 1.0 - ADAM_B1 ** ADAM_STEP
    c2 = 1.0 - ADAM_B2 ** ADAM_STEP

    def body(w_ref, g_ref, m_ref, v_ref, d_ref, nm_ref, nv_ref):
        g_t = g_ref[...]
        nm = ADAM_B1 * m_ref[...] + (1.0 - ADAM_B1) * g_t
        nv = ADAM_B2 * v_ref[...] + (1.0 - ADAM_B2) * (g_t * g_t)
        d_ref[...] = -ADAM_LR * ((nm / c1) / (jnp.sqrt(nv / c2) + ADAM_EPS) + ADAM_WD * w_ref[...])
        nm_ref[...] = nm
        nv_ref[...] = nv

    block_rows = 256 if rows % 256 == 0 and rows > 256 else rows
    spec = pl.BlockSpec((block_rows, cols), lambda i: (i, 0))
    outs = pl.pallas_call(
        body, name="adamw_" + label, grid=(rows // block_rows,),
        in_specs=[spec] * 4, out_specs=[spec] * 3,
        out_shape=[jax.ShapeDtypeStruct((rows, cols), F32)] * 3,
        compiler_params=_params(),
    )(*args)
    return [o.reshape(shape) for o in outs]


def _no_send(tag, arrays, sliced):
    return jnp.zeros((8, 128), F32)


def _local_step(x, tgt, w_in_a, later_weights, first_after, sgu_ln_g, sgu_ln_b, w_spatial, b_spatial,
                attn_sinks, rel_bias, post_ln_g, post_ln_b, send=_no_send):
    bsp_t = b_spatial.T
    g1, b1 = post_ln_g[0:1], post_ln_b[0:1]
    g2, b2 = post_ln_g[1:2], post_ln_b[1:2]
    onehot = _bucket_onehot()
    bias = _bias_expand(rel_bias.T, onehot).reshape(N_HEADS, 2 * CHUNK, CHUNK)
    win = _window_tables()

    xb, u, vh, z, rv, y = _layer_a_fwd(x, w_in_a, sgu_ln_g, sgu_ln_b, w_spatial, bsp_t, first_after)
    w_out_a, w_kv, w_in_b, w_out_b = later_weights(y)
    xh1, rstd1, q, zb, kd, vd = _layer_b_proj(x, y, w_out_a, g1, b1, w_in_b, w_kv)
    o, probs, sink_probs, dr2, loss_vec, dg2, db2 = _layer_b_fwd(q, zb, kd, vd, bias, win, attn_sinks, xh1, g1, b1,
                                                                 w_out_b, g2, b2, tgt)
    dq, dzb, dkd, dvd, carry_k, carry_v, gw_out_b, dsink, dbias = _layer_b_bwd_attn(
        dr2, zb, o, q, kd, vd, probs, sink_probs, w_out_b)
    dr1, dg1, db1, gw_in_b, gw_kv = _layer_b_bwd_proj(xh1, rstd1, g1, b1, dr2, dq, dzb, dkd, dvd, carry_k, carry_v,
                                                      w_in_b, w_kv)
    gw_out_b = gw_out_b.reshape(N_DEV, -1, D_MODEL)
    gw_kv = gw_kv.reshape(N_DEV, -1, 2 * PAIR)
    after = send("b", [gw_out_b, gw_in_b, gw_kv], [True, True, True])
    dp, gw_out_a, dws, dbsp, dgs, dbs = _layer_a_bwd_mix(dr1, u, vh, z, rv, w_out_a, sgu_ln_g, sgu_ln_b,
                                                         w_spatial, bsp_t, after)
    gw_out_a = gw_out_a.reshape(N_DEV, -1, D_MODEL)
    drel = _bias_reduce(onehot, dbias.reshape(N_HEADS, -1))
    dsink = jnp.sum(dsink.reshape(N_HEADS, CHUNK), axis=1).reshape(1, N_HEADS)
    loss = ((0.5 / D_MODEL) * jnp.sum(loss_vec)).reshape(1, 1)
    small = dict(w_spatial=dws, b_spatial=dbsp[:, 0:A_GROUPS].T, attn_sinks=dsink,
                 rel_bias=drel, post_ln_g=jnp.concatenate([dg1, dg2], axis=0),
                 post_ln_b=jnp.concatenate([db1, db2], axis=0), sgu_ln_g=dgs, sgu_ln_b=dbs, loss=loss)
    packed, layout = _pack_small(small)
    after = send("a_out", [gw_out_a, packed], [True, False])
    gw_in_a = _layer_a_bwd_win(xb, dp, after).reshape(N_DEV, D_MODEL // 2, -1)
    after = send("a_in", [gw_in_a], [True])
    grad_x = _layer_a_bwd_dx(dr1, dp, w_in_a, after)

    pieces = [gw_in_a, gw_out_a, gw_kv, gw_in_b, gw_out_b]
    return grad_x, pieces, small, packed, layout


_SMALL_ORDER = ("w_spatial", "b_spatial", "attn_sinks", "rel_bias", "post_ln_g", "post_ln_b", "sgu_ln_g", "sgu_ln_b",
                "loss")
_LANES = 128
_SUBLANES = 8


def _pack_small(small):
    parts, layout = [], {}
    row = 0
    for name in _SMALL_ORDER:
        flat = small[name].reshape(-1)
        n_rows = -(-flat.size // _LANES)
        n_rows = -(-n_rows // _SUBLANES) * _SUBLANES
        flat = jnp.pad(flat, (0, n_rows * _LANES - flat.size))
        parts.append(flat.reshape(n_rows, _LANES))
        layout[name] = (row, flat.size, small[name].shape)
        row += n_rows
    return jnp.concatenate(parts, axis=0), layout


def _unpack_small(packed, layout, sizes):
    out = {}
    for name in _SMALL_ORDER:
        row, padded, shape = layout[name]
        n = sizes[name]
        out[name] = packed[row:row + padded // _LANES].reshape(-1)[:n].reshape(shape)
    return out


def kernel(x, w_in_a, sgu_ln_g, sgu_ln_b, w_spatial, b_spatial, w_out_a, w_kv, w_in_b, attn_sinks, rel_bias, w_out_b, post_ln_g, post_ln_b, loss_target, m_w_in_a, m_sgu_ln_g, m_sgu_ln_b, m_w_spatial, m_b_spatial, m_w_out_a, m_w_kv, m_w_in_b, m_attn_sinks, m_rel_bias, m_w_out_b, m_post_ln_g, m_post_ln_b, v_w_in_a, v_sgu_ln_g, v_sgu_ln_b, v_w_spatial, v_b_spatial, v_w_out_a, v_w_kv, v_w_in_b, v_attn_sinks, v_rel_bias, v_w_out_b, v_post_ln_g, v_post_ln_b):
    weights = dict(w_in_a=w_in_a, sgu_ln_g=sgu_ln_g, sgu_ln_b=sgu_ln_b, w_spatial=w_spatial, b_spatial=b_spatial,
                   w_out_a=w_out_a, w_kv=w_kv, w_in_b=w_in_b, attn_sinks=attn_sinks, rel_bias=rel_bias,
                   w_out_b=w_out_b, post_ln_g=post_ln_g, post_ln_b=post_ln_b)
    moments_m = dict(w_in_a=m_w_in_a, sgu_ln_g=m_sgu_ln_g, sgu_ln_b=m_sgu_ln_b, w_spatial=m_w_spatial,
                     b_spatial=m_b_spatial, w_out_a=m_w_out_a, w_kv=m_w_kv, w_in_b=m_w_in_b,
                     attn_sinks=m_attn_sinks, rel_bias=m_rel_bias, w_out_b=m_w_out_b, post_ln_g=m_post_ln_g,
                     post_ln_b=m_post_ln_b)
    moments_v = dict(w_in_a=v_w_in_a, sgu_ln_g=v_sgu_ln_g, sgu_ln_b=v_sgu_ln_b, w_spatial=v_w_spatial,
                     b_spatial=v_b_spatial, w_out_a=v_w_out_a, w_kv=v_w_kv, w_in_b=v_w_in_b,
                     attn_sinks=v_attn_sinks, rel_bias=v_rel_bias, w_out_b=v_w_out_b, post_ln_g=v_post_ln_g,
                     post_ln_b=v_post_ln_b)
    order = ("w_in_a", "sgu_ln_g", "sgu_ln_b", "w_spatial", "b_spatial", "w_out_a", "w_kv", "w_in_b", "attn_sinks",
             "rel_bias", "w_out_b", "post_ln_g", "post_ln_b")
    large = ("w_in_a", "w_out_a", "w_kv", "w_in_b", "w_out_b")

    shard_index = 2 * lax.axis_index("x") + lax.axis_index("y")
    ln_shard = jnp.concatenate([sgu_ln_g, sgu_ln_b], axis=0)
    shards = [w_in_a[0], w_out_a[0], w_kv, w_in_b[0], w_out_b[0]]
    col_sharded = [True, False, False, True, False]
    full_in_a, *later, ln_full = _gather_weights(shards, col_sharded, [True, False, False, False, False], ln_shard)
    ln_full = jnp.transpose(ln_full, (1, 0, 2)).reshape(2, A_WIDTH)
    later_shapes = [s.shape for s in shards[1:]]
    fetch_group, fetch_token = _fetch_start(later, later_shapes, col_sharded[1:])

    def later_weights(y):
        return _fetch_wait(fetch_group, later_shapes, col_sharded[1:], y)

    groups = []

    def send(tag, arrays, sliced):
        group, token = _exchange_start(tag, arrays, sliced)
        groups.append(group)
        return token

    grad_x, pieces, small, packed, layout = _local_step(
        x[0], loss_target[0], full_in_a, later_weights, fetch_token, ln_full[0:1], ln_full[1:2], w_spatial[0],
        b_spatial[0], attn_sinks, rel_bias, post_ln_g, post_ln_b, send=send)

    srcs, lands = _exchange_wait(groups, grad_x)
    pieces = [srcs[5], srcs[3], srcs[2], srcs[1], srcs[0]]
    landed = [lands[5], lands[3], lands[2], lands[1], lands[0]]
    sizes = {name: small[name].size for name in _SMALL_ORDER}
    *grads_large, packed_sum = _sum_and_swap(pieces, landed, srcs[4], lands[4])
    small_sum = _unpack_small(packed_sum, layout, sizes)
    loss = small_sum["loss"].reshape(())

    grads = {}
    for name, g in zip(large, grads_large):
        grads[name] = g.reshape(weights[name].shape)
    shard_cols = sgu_ln_g.shape[1]
    for name in ("sgu_ln_g", "sgu_ln_b"):
        grads[name] = lax.dynamic_slice(small_sum[name], (0, shard_index * shard_cols), (1, shard_cols))
    for name in ("w_spatial", "b_spatial", "attn_sinks", "rel_bias", "post_ln_g", "post_ln_b"):
        grads[name] = small_sum[name].reshape(weights[name].shape)

    deltas, new_m, new_v = {}, {}, {}
    for name in order:
        deltas[name], new_m[name], new_v[name] = _adamw(name, weights[name], grads[name], moments_m[name], moments_v[name])
    return (loss, grad_x[None], *[grads[n] for n in order], *[deltas[n] for n in order],
            *[new_m[n] for n in order], *[new_v[n] for n in order])
```

```python
import math

import jax
import jax.numpy as jnp
from jax import lax
from jax.experimental import pallas as pl
from jax.experimental.pallas import tpu as pltpu

F32 = jnp.float32
BF16 = jnp.bfloat16

D_MODEL = 1024
A_WIDTH = 2048
A_GROUPS = 8
A_GROUP_DIM = 256
CHUNK = 128
N_HEADS = 16
N_KV = 2
HEAD_DIM = 64
PAIR = 2 * HEAD_DIM
B_WIDTH = 1024
REL_BUCKETS = 32
ALPHA = 4.0 ** 0.25
LN_EPS = 1e-5
NEG_INF = -1e30
SCALE = HEAD_DIM ** -0.5

ADAM_LR = 0.001
ADAM_B1 = 0.9
ADAM_B2 = 0.999
ADAM_EPS = 1e-08
ADAM_WD = 0.01
ADAM_STEP = 10

N_DEV = 8
N_CHIPS = 4
MESH = pl.DeviceIdType.MESH
VMEM_LIMIT = 56 * 1024 * 1024

TM_ATTN = 256
TM_MM = 512
TM_WIN = 1024


def _dot(a, b):
    return jnp.dot(a, b, preferred_element_type=F32)


def _dot_nt(a, b):
    return lax.dot_general(a, b, (((1,), (1,)), ((), ())), preferred_element_type=F32)


def _dot_tn(a, b):
    return lax.dot_general(a, b, (((0,), (0,)), ((), ())), preferred_element_type=F32)


def _ln_fwd(r):
    mu = jnp.mean(r, axis=-1, keepdims=True)
    rc = r - mu
    var = jnp.mean(rc * rc, axis=-1, keepdims=True)
    rstd = lax.rsqrt(var + LN_EPS)
    return rc * rstd, rstd


def _ln_bwd(dxh, xh, rstd):
    m1 = jnp.mean(dxh, axis=-1, keepdims=True)
    m2 = jnp.mean(dxh * xh, axis=-1, keepdims=True)
    return rstd * (dxh - m1 - xh * m2)


def _silu_parts(z):
    sg = jax.nn.sigmoid(z)
    return z * sg, sg * (1.0 + z * (1.0 - sg))


def _dup_halves(blk):
    sw = pltpu.roll(blk, HEAD_DIM, 1)
    lo = lax.broadcasted_iota(jnp.int32, blk.shape, 1) < HEAD_DIM
    return jnp.where(lo, blk, sw), jnp.where(lo, sw, blk)


def _fold_halves(blk):
    return blk + pltpu.roll(blk, HEAD_DIM, 1)


def _resident(shape):
    nd = len(shape)
    return pl.BlockSpec(shape, lambda *_: (0,) * nd, pipeline_mode=pl.Buffered(1))


def _const(shape):
    nd = len(shape)
    return pl.BlockSpec(shape, lambda *_: (0,) * nd)


def _rows(tm, cols):
    return pl.BlockSpec((tm, cols), lambda i: (i, 0))


def _params(sem=("arbitrary",)):
    return pltpu.CompilerParams(dimension_semantics=sem, vmem_limit_bytes=VMEM_LIMIT)


def _spatial_mix(ws_ref, bsp_ref, vn, s_scr, n_chunks):
    tri = (lax.broadcasted_iota(jnp.int32, (CHUNK, CHUNK), 0)
           >= lax.broadcasted_iota(jnp.int32, (CHUNK, CHUNK), 1))
    for g in range(A_GROUPS):
        wsg = jnp.where(tri, ws_ref[g], 0.0).astype(BF16)
        cols = slice(g * A_GROUP_DIM, (g + 1) * A_GROUP_DIM)
        for ci in range(n_chunks):
            rows = slice(ci * CHUNK, (ci + 1) * CHUNK)
            s_scr[rows, cols] = _dot(wsg, vn[rows, cols]) + bsp_ref[:, g:g + 1]


def _layer_a_fwd(x, w_in, lng, lnb, ws, bsp_t, after):
    t_len = x.shape[0]
    tm = TM_ATTN

    def body(x_ref, win_ref, lng_ref, lnb_ref, ws_ref, bsp_ref, after_ref,
             xt_ref, u_ref, vh_ref, z_ref, rv_ref, y_ref, s_scr):
        x_t = x_ref[...]
        xb = x_t.astype(BF16)
        xt_ref[...] = x_t.T.astype(BF16)
        u = _dot(xb, win_ref[:, 0:A_WIDTH])
        v = _dot(xb, win_ref[:, A_WIDTH:2 * A_WIDTH])
        z = _dot(xb, win_ref[:, 2 * A_WIDTH:3 * A_WIDTH])
        vh, rv = _ln_fwd(v)
        vn = (vh * lng_ref[...] + lnb_ref[...]).astype(BF16)
        _spatial_mix(ws_ref, bsp_ref, vn, s_scr, tm // CHUNK)
        sz, _ = _silu_parts(z)
        y_ref[...] = (u * s_scr[...] * sz).astype(BF16)
        u_ref[...] = u.astype(BF16)
        vh_ref[...] = vh.astype(BF16)
        z_ref[...] = z.astype(BF16)
        rv_ref[...] = rv

    wide = jax.ShapeDtypeStruct((t_len, A_WIDTH), BF16)
    return pl.pallas_call(
        body, name="layer_a_fwd", grid=(t_len // tm,),
        in_specs=[_rows(tm, D_MODEL), _resident(w_in.shape), _const(lng.shape), _const(lnb.shape), _const(ws.shape),
                  _const(bsp_t.shape), _const(after.shape)],
        out_specs=[pl.BlockSpec((D_MODEL, tm), lambda i: (0, i)), _rows(tm, A_WIDTH), _rows(tm, A_WIDTH),
                   _rows(tm, A_WIDTH), _rows(tm, 1), _rows(tm, A_WIDTH)],
        out_shape=[jax.ShapeDtypeStruct((D_MODEL, t_len), BF16), wide, wide, wide,
                   jax.ShapeDtypeStruct((t_len, 1), F32), wide],
        scratch_shapes=[pltpu.VMEM((tm, A_WIDTH), F32)],
        compiler_params=_params(),
    )(x, w_in, lng, lnb, ws, bsp_t, after)


def _layer_b_proj(x, y, w_out_a, g1, b1, w_in, w_kv):
    t_len = x.shape[0]
    tm = TM_MM

    def body(x_ref, y_ref, wout_ref, g_ref, b_ref, win_ref, wkv_ref, xh_ref, r1_ref, q_ref, z_ref, kd_ref, vd_ref):
        xh, r1 = _ln_fwd(ALPHA * x_ref[...] + _dot(y_ref[...], wout_ref[...]))
        xh_ref[...] = xh
        r1_ref[...] = r1
        h1 = (xh * g_ref[...] + b_ref[...]).astype(BF16)
        q_ref[...] = (_dot(h1, win_ref[:, 0:B_WIDTH]) * SCALE).astype(BF16)
        z_ref[...] = _dot(h1, win_ref[:, B_WIDTH:2 * B_WIDTH]).astype(BF16)
        kv = _dot(h1, wkv_ref[...])
        k0, k1 = _dup_halves(kv[:, 0:PAIR])
        v0, v1 = _dup_halves(kv[:, PAIR:2 * PAIR])
        kd_ref[:, 0:PAIR] = k0.astype(BF16)
        kd_ref[:, PAIR:2 * PAIR] = k1.astype(BF16)
        vd_ref[:, 0:PAIR] = v0.astype(BF16)
        vd_ref[:, PAIR:2 * PAIR] = v1.astype(BF16)

    return pl.pallas_call(
        body, name="layer_b_proj", grid=(t_len // tm,),
        in_specs=[_rows(tm, D_MODEL), _rows(tm, A_WIDTH), _resident(w_out_a.shape), _const(g1.shape),
                  _const(b1.shape), _resident(w_in.shape), _resident(w_kv.shape)],
        out_specs=[_rows(tm, D_MODEL), _rows(tm, 1), _rows(tm, B_WIDTH), _rows(tm, B_WIDTH), _rows(tm, 2 * PAIR),
                   _rows(tm, 2 * PAIR)],
        out_shape=[jax.ShapeDtypeStruct((t_len, D_MODEL), F32), jax.ShapeDtypeStruct((t_len, 1), F32),
                   jax.ShapeDtypeStruct((t_len, B_WIDTH), BF16), jax.ShapeDtypeStruct((t_len, B_WIDTH), BF16),
                   jax.ShapeDtypeStruct((t_len, 2 * PAIR), BF16), jax.ShapeDtypeStruct((t_len, 2 * PAIR), BF16)],
        compiler_params=_params(),
    )(x, y, w_out_a, g1, b1, w_in, w_kv)


GROUP = N_HEADS // N_KV
GROUP_Q = GROUP * CHUNK


def _window_tables():
    j = jnp.arange(2 * CHUNK, dtype=jnp.int32)[:, None]
    t = jnp.arange(CHUNK, dtype=jnp.int32)[None, :]
    dist = t + CHUNK - j
    inside = (dist >= 0) & (dist < CHUNK)
    return jnp.stack([inside & (j >= CHUNK), inside]).astype(F32)


def _band(ref, chunk_index, kvh):
    prev0 = pl.multiple_of(jnp.maximum(chunk_index - 1, 0) * CHUNK, CHUNK)
    cur0 = pl.multiple_of(chunk_index * CHUNK, CHUNK)
    cols = slice(kvh * PAIR, (kvh + 1) * PAIR)
    return jnp.concatenate([ref[pl.ds(prev0, CHUNK), cols], ref[pl.ds(cur0, CHUNK), cols]], axis=0)


def _group_tables(bias_ref, win_ref, sink_ref, chunk_index, kvh):
    bias = jnp.concatenate([bias_ref[kvh * GROUP + j] for j in range(GROUP)], axis=1)
    win = win_ref[jnp.minimum(chunk_index, 1)]
    mask = jnp.concatenate([win] * GROUP, axis=1) > 0.5
    sink = jnp.concatenate([jnp.full((1, CHUNK), sink_ref[0, kvh * GROUP + j], F32) for j in range(GROUP)], axis=1)
    return bias, mask, sink


def _attn_probs(qs, kband, bias, mask, sink):
    logits = jnp.where(mask, _dot_nt(kband, qs) + bias, NEG_INF)
    m = jnp.maximum(jnp.max(logits, axis=0, keepdims=True), sink)
    e = jnp.exp(logits - m)
    es = jnp.exp(sink - m)
    inv = 1.0 / (jnp.sum(e, axis=0, keepdims=True) + es)
    return e * inv, es * inv


def _half_mask():
    return lax.broadcasted_iota(jnp.int32, (CHUNK, PAIR), 1) < HEAD_DIM


def _stack_heads(src_ref, rows, kvh, dst_scr, lo):
    for j in range(GROUP):
        h = kvh * GROUP + j
        blk = src_ref[rows, (h // 2) * PAIR:(h // 2 + 1) * PAIR].astype(F32)
        keep = lo if h % 2 == 0 else ~lo
        dst_scr[j * CHUNK:(j + 1) * CHUNK, :] = jnp.where(keep, blk, 0.0).astype(BF16)


def _probs_spec(tm):
    return pl.BlockSpec((tm // CHUNK, N_KV, 2 * CHUNK, GROUP_Q), lambda i: (i, 0, 0, 0))


def _sink_probs_spec():
    return pl.BlockSpec((1, 8, GROUP_Q), lambda i: (i, 0, 0))


def _unstack_pairs(stacked, pp, lo):
    return jnp.where(lo, stacked[(2 * pp) * CHUNK:(2 * pp + 1) * CHUNK], stacked[(2 * pp + 1) * CHUNK:(2 * pp + 2) * CHUNK])


def _layer_b_fwd(q, zb, kd, vd, bias, win, sinks, xh1, g1, b1, w_out, g2, b2, tgt):
    t_len = q.shape[0]
    tm = TM_ATTN

    def body(q_ref, z_ref, kd_ref, vd_ref, bias_ref, win_ref, sink_ref, xh_ref, g1_ref, b1_ref, wout_ref, g2_ref,
             b2_ref, tgt_ref, o_ref, p_ref, ps_ref, dr_ref, loss_ref, dg_ref, db_ref, o_scr, qs_scr):
        i = pl.program_id(0)

        @pl.when(i == 0)
        def _():
            loss_ref[...] = jnp.zeros_like(loss_ref)
            dg_ref[...] = jnp.zeros_like(dg_ref)
            db_ref[...] = jnp.zeros_like(db_ref)

        lo = _half_mask()
        ps_ref[...] = jnp.zeros_like(ps_ref)
        for ci in range(tm // CHUNK):
            cg = i * (tm // CHUNK) + ci
            rows = slice(ci * CHUNK, (ci + 1) * CHUNK)
            for kvh in range(N_KV):
                kband = _band(kd_ref, cg, kvh)
                vband = _band(vd_ref, cg, kvh)
                bias_g, mask, sink = _group_tables(bias_ref, win_ref, sink_ref, cg, kvh)
                _stack_heads(q_ref, rows, kvh, qs_scr, lo)
                p, p_sink = _attn_probs(qs_scr[...], kband, bias_g, mask, sink)
                p = p.astype(BF16)
                p_ref[ci, kvh] = p
                ps_ref[0, ci * N_KV + kvh:ci * N_KV + kvh + 1, :] = p_sink
                o_stack = _dot_tn(p, vband)
                for pp in range(GROUP // 2):
                    pair = kvh * (GROUP // 2) + pp
                    o_scr[rows, pair * PAIR:(pair + 1) * PAIR] = _unstack_pairs(o_stack, pp, lo)
        o = o_scr[...]
        o_ref[...] = o.astype(BF16)
        sz, _ = _silu_parts(z_ref[...].astype(F32))
        y = (o * sz).astype(BF16)
        h1 = xh_ref[...] * g1_ref[...] + b1_ref[...]
        r = ALPHA * h1 + _dot(y, wout_ref[...])
        xh2, rstd2 = _ln_fwd(r)
        diff = xh2 * g2_ref[...] + b2_ref[...] - tgt_ref[...]
        loss_ref[...] += jnp.sum(diff * diff, axis=0, keepdims=True)
        dh2 = diff * (1.0 / D_MODEL)
        dg_ref[...] += jnp.sum(dh2 * xh2, axis=0, keepdims=True)
        db_ref[...] += jnp.sum(dh2, axis=0, keepdims=True)
        dr_ref[...] = _ln_bwd(dh2 * g2_ref[...], xh2, rstd2)

    vec = jax.ShapeDtypeStruct((1, D_MODEL), F32)
    return pl.pallas_call(
        body, name="layer_b_fwd", grid=(t_len // tm,),
        in_specs=[_rows(tm, B_WIDTH), _rows(tm, B_WIDTH), _resident(kd.shape), _resident(vd.shape),
                  _resident(bias.shape), _resident(win.shape), pl.BlockSpec(memory_space=pltpu.SMEM),
                  _rows(tm, D_MODEL), _const(g1.shape), _const(b1.shape), _resident(w_out.shape), _const(g2.shape),
                  _const(b2.shape), _rows(tm, D_MODEL)],
        out_specs=[_rows(tm, B_WIDTH), _probs_spec(tm), _sink_probs_spec(), _rows(tm, D_MODEL), _const((1, D_MODEL)),
                   _const((1, D_MODEL)), _const((1, D_MODEL))],
        out_shape=[jax.ShapeDtypeStruct((t_len, B_WIDTH), BF16),
                   jax.ShapeDtypeStruct((t_len // CHUNK, N_KV, 2 * CHUNK, GROUP_Q), BF16),
                   jax.ShapeDtypeStruct((t_len // tm, 8, GROUP_Q), F32),
                   jax.ShapeDtypeStruct((t_len, D_MODEL), F32), vec, vec, vec],
        scratch_shapes=[pltpu.VMEM((tm, B_WIDTH), F32), pltpu.VMEM((GROUP_Q, PAIR), BF16)],
        compiler_params=_params(),
    )(q, zb, kd, vd, bias, win, sinks, xh1, g1, b1, w_out, g2, b2, tgt)


def _layer_b_bwd_attn(dr2, zb, o, q, kd, vd, probs, sink_probs, w_out):
    t_len = q.shape[0]
    tm = TM_ATTN
    n_steps = t_len // tm
    n_chunks = tm // CHUNK

    def body(dr_ref, z_ref, o_ref, q_ref, kd_ref, vd_ref, p_ref, ps_ref, wout_ref,
             dq_ref, dz_ref, dkd_ref, dvd_ref, ck_ref, cv_ref, gw_ref, dsink_ref, dbias_ref,
             do_scr, qs_scr, dos_scr, gw_acc):
        i = pl.program_id(0)

        @pl.when(i == 0)
        def _():
            gw_acc[...] = jnp.zeros_like(gw_acc)
            dsink_ref[...] = jnp.zeros_like(dsink_ref)
            dbias_ref[...] = jnp.zeros_like(dbias_ref)

        drb = dr_ref[...].astype(BF16)
        dy = _dot_nt(drb, wout_ref[...])
        z = z_ref[...].astype(F32)
        sz, dsz = _silu_parts(z)
        o_t = o_ref[...].astype(F32)
        dz_ref[...] = (dy * o_t * dsz).astype(BF16)
        do_scr[...] = (dy * sz).astype(BF16)
        gw_acc[...] += _dot_tn((o_t * sz).astype(BF16), drb)

        lo = _half_mask()
        for kvh in range(N_KV):
            kcols = slice(kvh * PAIR, (kvh + 1) * PAIR)
            dk_bands, dv_bands = [], []
            for ci in range(n_chunks):
                cg = i * n_chunks + ci
                rows = slice(ci * CHUNK, (ci + 1) * CHUNK)
                kband = _band(kd_ref, cg, kvh)
                vband = _band(vd_ref, cg, kvh)
                _stack_heads(q_ref, rows, kvh, qs_scr, lo)
                _stack_heads(do_scr, rows, kvh, dos_scr, lo)
                qs = qs_scr[...]
                dos = dos_scr[...]
                pb = p_ref[ci, kvh]
                p = pb.astype(F32)
                p_sink = ps_ref[0, ci * N_KV + kvh:ci * N_KV + kvh + 1, :]
                dp = _dot_nt(vband, dos)
                delta = jnp.sum(p * dp, axis=0, keepdims=True)
                dlog = p * (dp - delta)
                for j in range(GROUP):
                    dbias_ref[kvh * GROUP + j] += dlog[:, j * CHUNK:(j + 1) * CHUNK]
                dsink_ref[kvh:kvh + 1, :] += -(p_sink * delta)
                ds = dlog.astype(BF16)
                dq_stack = _dot_tn(ds, kband) * SCALE
                for pp in range(GROUP // 2):
                    pair = kvh * (GROUP // 2) + pp
                    dq_ref[rows, pair * PAIR:(pair + 1) * PAIR] = _unstack_pairs(dq_stack, pp, lo).astype(BF16)
                dk_bands.append(_dot(ds, qs))
                dv_bands.append(_dot(pb, dos))
            for bands, out_ref, carry_ref in ((dk_bands, dkd_ref, ck_ref), (dv_bands, dvd_ref, cv_ref)):
                carry_ref[0, :, kcols] = bands[0][0:CHUNK]
                for ci in range(n_chunks):
                    own = bands[ci][CHUNK:2 * CHUNK]
                    if ci + 1 < n_chunks:
                        own = own + bands[ci + 1][0:CHUNK]
                    out_ref[ci * CHUNK:(ci + 1) * CHUNK, kcols] = own

        @pl.when(i == n_steps - 1)
        def _():
            gw_ref[...] = gw_acc[...].astype(BF16)

    carry_spec = pl.BlockSpec((1, CHUNK, 2 * PAIR), lambda i: (i, 0, 0))
    carry_shape = jax.ShapeDtypeStruct((n_steps, CHUNK, 2 * PAIR), F32)
    bias_shape = (N_HEADS, 2 * CHUNK, CHUNK)
    return pl.pallas_call(
        body, name="layer_b_bwd_attn", grid=(n_steps,),
        in_specs=[_rows(tm, D_MODEL), _rows(tm, B_WIDTH), _rows(tm, B_WIDTH), _rows(tm, B_WIDTH),
                  _resident(kd.shape), _resident(vd.shape), _probs_spec(tm), _sink_probs_spec(),
                  _resident(w_out.shape)],
        out_specs=[_rows(tm, B_WIDTH), _rows(tm, B_WIDTH), _rows(tm, 2 * PAIR), _rows(tm, 2 * PAIR),
                   carry_spec, carry_spec, _const(w_out.shape), _const((N_KV, GROUP_Q)), _const(bias_shape)],
        out_shape=[jax.ShapeDtypeStruct((t_len, B_WIDTH), BF16), jax.ShapeDtypeStruct((t_len, B_WIDTH), BF16),
                   jax.ShapeDtypeStruct((t_len, 2 * PAIR), F32), jax.ShapeDtypeStruct((t_len, 2 * PAIR), F32),
                   carry_shape, carry_shape, jax.ShapeDtypeStruct(w_out.shape, BF16),
                   jax.ShapeDtypeStruct((N_KV, GROUP_Q), F32), jax.ShapeDtypeStruct(bias_shape, F32)],
        scratch_shapes=[pltpu.VMEM((tm, B_WIDTH), BF16), pltpu.VMEM((GROUP_Q, PAIR), BF16),
                        pltpu.VMEM((GROUP_Q, PAIR), BF16), pltpu.VMEM(w_out.shape, F32)],
        compiler_params=_params(),
    )(dr2, zb, o, q, kd, vd, probs, sink_probs, w_out)


def _layer_b_bwd_proj(xh1, rstd1, g1, b1, dr2, dq, dzb, dkd, dvd, carry_k, carry_v, w_in, w_kv):
    t_len = xh1.shape[0]
    tm = TM_MM
    n_steps = t_len // tm
    per_tile = tm // TM_ATTN
    n_carry = carry_k.shape[0]

    def body(xh_ref, rstd_ref, g_ref, b_ref, dr2_ref, dq_ref, dz_ref, dkd_ref, dvd_ref, *rest):
        carry_refs = rest[:2 * per_tile]
        win_ref, wkv_ref, dr1_ref, dg_ref, db_ref, gwin_ref, gwkv_ref, acc_in, acc_kv = rest[2 * per_tile:]
        i = pl.program_id(0)

        @pl.when(i == 0)
        def _():
            acc_in[...] = jnp.zeros_like(acc_in)
            acc_kv[...] = jnp.zeros_like(acc_kv)
            dg_ref[...] = jnp.zeros_like(dg_ref)
            db_ref[...] = jnp.zeros_like(db_ref)

        lo = lax.broadcasted_iota(jnp.int32, (tm, PAIR), 1) < HEAD_DIM

        def heads_gradient(tile_ref, refs):
            parts = []
            for a in range(per_tile):
                parts.append(tile_ref[a * TM_ATTN:(a + 1) * TM_ATTN - CHUNK, :])
                carry = refs[a][0]
                if a == per_tile - 1:
                    carry = jnp.where(i < n_steps - 1, carry, 0.0)
                parts.append(tile_ref[(a + 1) * TM_ATTN - CHUNK:(a + 1) * TM_ATTN, :] + carry)
            dup = jnp.concatenate(parts, axis=0)
            return jnp.where(lo, _fold_halves(dup[:, 0:PAIR]), _fold_halves(dup[:, PAIR:2 * PAIR]))

        xh = xh_ref[...]
        h1 = (xh * g_ref[...] + b_ref[...]).astype(BF16)
        dq_t = dq_ref[...]
        dz_t = dz_ref[...]
        dkv = jnp.concatenate([heads_gradient(dkd_ref, carry_refs[:per_tile]),
                               heads_gradient(dvd_ref, carry_refs[per_tile:])], axis=1).astype(BF16)
        dh1 = ALPHA * dr2_ref[...]
        dh1 += _dot_nt(dq_t, win_ref[:, 0:B_WIDTH])
        dh1 += _dot_nt(dz_t, win_ref[:, B_WIDTH:2 * B_WIDTH])
        dh1 += _dot_nt(dkv, wkv_ref[...])
        acc_in[:, 0:B_WIDTH] += _dot_tn(h1, dq_t)
        acc_in[:, B_WIDTH:2 * B_WIDTH] += _dot_tn(h1, dz_t)
        acc_kv[...] += _dot_tn(h1, dkv)
        dg_ref[...] += jnp.sum(dh1 * xh, axis=0, keepdims=True)
        db_ref[...] += jnp.sum(dh1, axis=0, keepdims=True)
        dr1_ref[...] = _ln_bwd(dh1 * g_ref[...], xh, rstd_ref[...])

        @pl.when(i == n_steps - 1)
        def _():
            half_rows = D_MODEL // 2
            shard_cols = 2 * B_WIDTH // N_CHIPS
            for s in range(N_CHIPS):
                for c in range(2):
                    gwin_ref[2 * s + c] = acc_in[c * half_rows:(c + 1) * half_rows,
                                                 s * shard_cols:(s + 1) * shard_cols].astype(BF16)
            gwkv_ref[...] = acc_kv[...].astype(BF16)

    vec = jax.ShapeDtypeStruct((1, D_MODEL), F32)
    gwin_shape = (N_DEV, D_MODEL // 2, 2 * B_WIDTH // N_CHIPS)

    def carry_spec(a):
        return pl.BlockSpec((1, CHUNK, 2 * PAIR), lambda i: (jnp.minimum(per_tile * i + a + 1, n_carry - 1), 0, 0))

    carry_specs = [carry_spec(a) for a in range(per_tile)]
    return pl.pallas_call(
        body, name="layer_b_bwd_proj", grid=(n_steps,),
        in_specs=[_rows(tm, D_MODEL), _rows(tm, 1), _const(g1.shape), _const(b1.shape), _rows(tm, D_MODEL),
                  _rows(tm, B_WIDTH), _rows(tm, B_WIDTH), _rows(tm, 2 * PAIR), _rows(tm, 2 * PAIR)]
        + carry_specs + carry_specs + [_resident(w_in.shape), _resident(w_kv.shape)],
        out_specs=[_rows(tm, D_MODEL), _const((1, D_MODEL)), _const((1, D_MODEL)), _const(gwin_shape),
                   _const(w_kv.shape)],
        out_shape=[jax.ShapeDtypeStruct((t_len, D_MODEL), F32), vec, vec,
                   jax.ShapeDtypeStruct(gwin_shape, BF16), jax.ShapeDtypeStruct(w_kv.shape, BF16)],
        scratch_shapes=[pltpu.VMEM(w_in.shape, F32), pltpu.VMEM(w_kv.shape, F32)],
        compiler_params=_params(),
    )(xh1, rstd1, g1, b1, dr2, dq, dzb, dkd, dvd, *([carry_k] * per_tile), *([carry_v] * per_tile), w_in, w_kv)


def _layer_a_bwd_mix(dr1, u, vh, z, y, rv, w_out, lng, lnb, ws, bsp_t, after):
    t_len = u.shape[0]
    tm = TM_ATTN
    n_steps = t_len // tm

    def body(dr_ref, u_ref, vh_ref, z_ref, y_ref, rv_ref, wout_ref, lng_ref, lnb_ref, ws_ref, bsp_ref, after_ref,
             dp_ref, gw_ref, dws_ref, dbsp_ref, dgs_ref, dbs_ref, s_scr, dvn_scr, gw_acc):
        i = pl.program_id(0)

        @pl.when(i == 0)
        def _():
            gw_acc[...] = jnp.zeros_like(gw_acc)
            dws_ref[...] = jnp.zeros_like(dws_ref)
            dbsp_ref[...] = jnp.zeros_like(dbsp_ref)
            dgs_ref[...] = jnp.zeros_like(dgs_ref)
            dbs_ref[...] = jnp.zeros_like(dbs_ref)

        drb = dr_ref[...].astype(BF16)
        dy = _dot_nt(drb, wout_ref[...])
        vh_t = vh_ref[...].astype(F32)
        vn = (vh_t * lng_ref[...] + lnb_ref[...]).astype(BF16)
        _spatial_mix(ws_ref, bsp_ref, vn, s_scr, tm // CHUNK)
        gw_acc[...] += _dot_tn(y_ref[...], drb)
        s = s_scr[...]
        sz, dsz = _silu_parts(z_ref[...].astype(F32))
        t = dy * u_ref[...].astype(F32)
        dp_ref[:, 0:A_WIDTH] = (dy * (s * sz)).astype(BF16)
        dp_ref[:, 2 * A_WIDTH:3 * A_WIDTH] = (t * s * dsz).astype(BF16)
        ds = (t * sz).astype(BF16)

        group_of = lax.broadcasted_iota(jnp.int32, (A_WIDTH, CHUNK), 0) // A_GROUP_DIM
        indicator = (group_of == lax.broadcasted_iota(jnp.int32, (A_WIDTH, CHUNK), 1)).astype(BF16)
        group_sums = _dot(ds, indicator)
        total = group_sums[0:CHUNK]
        for ci in range(1, tm // CHUNK):
            total += group_sums[ci * CHUNK:(ci + 1) * CHUNK]
        dbsp_ref[...] += total

        tri = (lax.broadcasted_iota(jnp.int32, (CHUNK, CHUNK), 0)
               >= lax.broadcasted_iota(jnp.int32, (CHUNK, CHUNK), 1))
        for g in range(A_GROUPS):
            wsg = jnp.where(tri, ws_ref[g], 0.0).astype(BF16)
            cols = slice(g * A_GROUP_DIM, (g + 1) * A_GROUP_DIM)
            dws_g = jnp.zeros((CHUNK, CHUNK), F32)
            for ci in range(tm // CHUNK):
                rows = slice(ci * CHUNK, (ci + 1) * CHUNK)
                ds_b = ds[rows, cols]
                dws_g += _dot_nt(ds_b, vn[rows, cols])
                dvn_scr[rows, cols] = _dot_tn(wsg, ds_b)
            dws_ref[g] += jnp.where(tri, dws_g, 0.0)
        dvn = dvn_scr[...]
        dgs_ref[...] += jnp.sum(dvn * vh_t, axis=0, keepdims=True)
        dbs_ref[...] += jnp.sum(dvn, axis=0, keepdims=True)
        dp_ref[:, A_WIDTH:2 * A_WIDTH] = _ln_bwd(dvn * lng_ref[...], vh_t, rv_ref[...]).astype(BF16)

        @pl.when(i == n_steps - 1)
        def _():
            gw_ref[...] = gw_acc[...].astype(BF16)

    wide = jax.ShapeDtypeStruct((1, A_WIDTH), F32)
    return pl.pallas_call(
        body, name="layer_a_bwd_mix", grid=(n_steps,),
        in_specs=[_rows(tm, D_MODEL), _rows(tm, A_WIDTH), _rows(tm, A_WIDTH), _rows(tm, A_WIDTH), _rows(tm, A_WIDTH),
                  _rows(tm, 1), _resident(w_out.shape), _const(lng.shape), _const(lnb.shape), _const(ws.shape),
                  _const(bsp_t.shape), _const(after.shape)],
        out_specs=[_rows(tm, 3 * A_WIDTH), _const(w_out.shape), _const(ws.shape), _const((CHUNK, CHUNK)),
                   _const((1, A_WIDTH)), _const((1, A_WIDTH))],
        out_shape=[jax.ShapeDtypeStruct((t_len, 3 * A_WIDTH), BF16), jax.ShapeDtypeStruct(w_out.shape, BF16),
                   jax.ShapeDtypeStruct(ws.shape, F32), jax.ShapeDtypeStruct((CHUNK, CHUNK), F32),
                   wide, wide],
        scratch_shapes=[pltpu.VMEM((tm, A_WIDTH), F32), pltpu.VMEM((tm, A_WIDTH), F32),
                        pltpu.VMEM(w_out.shape, F32)],
        compiler_params=_params(),
    )(dr1, u, vh, z, y, rv, w_out, lng, lnb, ws, bsp_t, after)


def _layer_a_bwd_dx(dr1, dp, w_in, after):
    t_len = dr1.shape[0]
    tm = TM_MM

    def body(dr_ref, dp_ref, win_ref, after_ref, dx_ref):
        dx_ref[...] = ALPHA * dr_ref[...] + _dot_nt(dp_ref[...], win_ref[...])

    return pl.pallas_call(
        body, name="layer_a_bwd_dx", grid=(t_len // tm,),
        in_specs=[_rows(tm, D_MODEL), _rows(tm, 3 * A_WIDTH), _resident(w_in.shape), _const(after.shape)],
        out_specs=_rows(tm, D_MODEL),
        out_shape=jax.ShapeDtypeStruct((t_len, D_MODEL), F32),
        compiler_params=_params(),
    )(dr1, dp, w_in, after)


def _layer_a_bwd_win(xt, dp, after):
    t_len = xt.shape[1]
    tm = TM_WIN
    n_steps = t_len // tm
    shard_cols = 3 * A_WIDTH // N_CHIPS
    half_rows = D_MODEL // 2

    def body(xt_ref, dp_ref, after_ref, gw_ref, acc):
        i = pl.program_id(1)

        @pl.when(i == 0)
        def _():
            acc[...] = jnp.zeros_like(acc)

        acc[...] += _dot(xt_ref[...], dp_ref[...])

        @pl.when(i == n_steps - 1)
        def _():
            for c in range(2):
                gw_ref[0, c] = acc[c * half_rows:(c + 1) * half_rows, :].astype(BF16)

    return pl.pallas_call(
        body, name="layer_a_bwd_win", grid=(N_CHIPS, n_steps),
        in_specs=[pl.BlockSpec((D_MODEL, tm), lambda j, i: (0, i)),
                  pl.BlockSpec((tm, shard_cols), lambda j, i: (i, j)), _const(after.shape)],
        out_specs=pl.BlockSpec((1, 2, half_rows, shard_cols), lambda j, i: (j, 0, 0, 0)),
        out_shape=jax.ShapeDtypeStruct((N_CHIPS, 2, half_rows, shard_cols), BF16),
        scratch_shapes=[pltpu.VMEM((D_MODEL, shard_cols), F32)],
        compiler_params=_params(("arbitrary", "arbitrary")),
    )(xt, dp, after)


def _bucket_onehot():
    t = jnp.arange(CHUNK, dtype=jnp.int32)[None, :]
    j = jnp.arange(2 * CHUNK, dtype=jnp.int32)[:, None]
    dist = jnp.clip(t + CHUNK - j, 0, CHUNK - 1)
    max_exact = REL_BUCKETS // 2
    df = jnp.maximum(dist, 1).astype(F32)
    large = max_exact + (jnp.log(df / max_exact) / math.log(CHUNK / max_exact)
                         * (REL_BUCKETS - max_exact)).astype(jnp.int32)
    bucket = jnp.where(dist < max_exact, dist, jnp.minimum(large, REL_BUCKETS - 1))
    onehot = bucket.reshape(1, -1) == jnp.arange(REL_BUCKETS, dtype=jnp.int32)[:, None]
    return onehot.astype(F32)


def _bias_expand(rel_t, onehot):
    def body(rel_ref, oh_ref, out_ref):
        out_ref[...] = jnp.dot(rel_ref[...], oh_ref[...], preferred_element_type=F32,
                               precision=lax.Precision.HIGHEST)

    return pl.pallas_call(
        body, name="bias_expand",
        out_shape=jax.ShapeDtypeStruct((N_HEADS, onehot.shape[1]), F32),
    )(rel_t, onehot)


def _bias_reduce(onehot, dbias):
    def body(oh_ref, db_ref, out_ref):
        out_ref[...] = lax.dot_general(oh_ref[...], db_ref[...], (((1,), (1,)), ((), ())),
                                       preferred_element_type=F32, precision=lax.Precision.HIGHEST)

    return pl.pallas_call(
        body, name="bias_reduce",
        out_shape=jax.ShapeDtypeStruct((REL_BUCKETS, N_HEADS), F32),
    )(onehot, dbias)


def _place():
    return lax.axis_index("x"), lax.axis_index("y"), lax.axis_index("c")


def _shard_window(full_ref, shard_shape, col_sharded, s, half):
    rows, cols = shard_shape
    if col_sharded:
        rsel = pl.ds(0, rows) if half is None else pl.ds(half * (rows // 2), rows // 2)
        return full_ref.at[rsel, pl.ds(s * cols, cols)]
    if half is None:
        return full_ref.at[pl.ds(s * rows, rows), :]
    return full_ref.at[pl.ds(s * rows + half * (rows // 2), rows // 2), :]


def _other_chips(x, y):
    return [(1 - x, y), (x, 1 - y), (1 - x, 1 - y)]


def _gather_weights(shards, col_sharded, fetch, ln_shard):
    n_w = len(shards)
    fetched = [w for w in range(n_w) if fetch[w]]
    full_shapes = []
    for w, cs in zip(shards, col_sharded):
        r, c = w.shape
        full_shapes.append((r, c * N_CHIPS) if cs else (r * N_CHIPS, c))

    def body(*refs):
        in_refs = refs[:n_w]
        ln_ref = refs[n_w]
        full_refs = refs[n_w + 1:2 * n_w + 1]
        ln_full = refs[2 * n_w + 1]
        stage = refs[2 * n_w + 2:3 * n_w + 2]
        send_sems, recv_sems, local_sems, ln_send, ln_recv = refs[3 * n_w + 2:]
        x, y, c = _place()
        s_me = 2 * x + y
        chips = _other_chips(x, y)

        def shard_window(w, s, half):
            return _shard_window(full_refs[w], shards[w].shape, col_sharded[w], s, half)

        def stage_half(w, half):
            rows = shards[w].shape[0]
            return stage[w].at[pl.ds(half * (rows // 2), rows // 2), :]

        def ici_copy(w, k, sender_shard, src):
            return pltpu.make_async_remote_copy(
                src_ref=src, dst_ref=shard_window(w, sender_shard, c),
                send_sem=send_sems.at[w * 3 + k], recv_sem=recv_sems.at[w * 3 + k],
                device_id=(*chips[k], c), device_id_type=MESH)

        def d2d_copy(w, k, half):
            s_k = 2 * chips[k][0] + chips[k][1]
            win = shard_window(w, s_k, half)
            return pltpu.make_async_remote_copy(
                src_ref=win, dst_ref=win,
                send_sem=send_sems.at[3 * n_w + w * 3 + k], recv_sem=recv_sems.at[3 * n_w + w * 3 + k],
                device_id=(x, y, 1 - c), device_id_type=MESH)

        def ln_copy(k, slot):
            return pltpu.make_async_remote_copy(
                src_ref=ln_ref, dst_ref=ln_full.at[slot], send_sem=ln_send.at[k], recv_sem=ln_recv.at[k],
                device_id=(*chips[k], c), device_id_type=MESH)

        for w in range(n_w):
            stage[w][...] = in_refs[w][...].astype(BF16)
        own = [pltpu.make_async_copy(stage[w], shard_window(w, s_me, None), local_sems.at[w]) for w in range(n_w)]
        for cp in own:
            cp.start()
        ln_full[s_me] = ln_ref[...]
        first = [ici_copy(w, k, s_me, stage_half(w, c)) for w in fetched for k in range(3)]
        first += [ln_copy(k, s_me) for k in range(3)]
        for cp in first:
            cp.start()
        passed = []
        for w in fetched:
            for k in range(3):
                s_k = 2 * chips[k][0] + chips[k][1]
                ici_copy(w, k, s_k, stage_half(w, c)).wait_recv()
                fwd = d2d_copy(w, k, c)
                fwd.start()
                passed.append(fwd)
        for w in fetched:
            for k in range(3):
                d2d_copy(w, k, 1 - c).wait_recv()
        for k in range(3):
            ln_copy(k, 2 * chips[k][0] + chips[k][1]).wait_recv()
        for cp in first + passed:
            cp.wait_send()
        for cp in own:
            cp.wait()

    vmem = pl.BlockSpec(memory_space=pltpu.VMEM)
    hbm = pl.BlockSpec(memory_space=pl.ANY)
    return pl.pallas_call(
        body, name="gather_weights",
        in_specs=[vmem] * (n_w + 1),
        out_specs=[hbm] * n_w + [vmem],
        out_shape=[jax.ShapeDtypeStruct(s, BF16) for s in full_shapes]
        + [jax.ShapeDtypeStruct((N_CHIPS,) + ln_shard.shape, F32)],
        scratch_shapes=[pltpu.VMEM(w.shape, BF16) for w in shards]
        + [pltpu.SemaphoreType.DMA((6 * n_w,)), pltpu.SemaphoreType.DMA((6 * n_w,)),
           pltpu.SemaphoreType.DMA((n_w,)), pltpu.SemaphoreType.DMA((3,)), pltpu.SemaphoreType.DMA((3,))],
        compiler_params=pltpu.CompilerParams(vmem_limit_bytes=VMEM_LIMIT),
    )(*shards, ln_shard)


def _fetch_copy(full_ref, shard_shape, col_sharded, sender_shard, send_sems, recv_sems, idx, chip, c):
    win = _shard_window(full_ref, shard_shape, col_sharded, sender_shard, None)
    return pltpu.make_async_remote_copy(src_ref=win, dst_ref=win, send_sem=send_sems.at[idx],
                                        recv_sem=recv_sems.at[idx], device_id=(*chip, c), device_id_type=MESH)


def _fetch_start(fulls, shard_shapes, col_sharded):
    n = len(fulls)

    def body(*refs):
        full = refs[:n]
        send_sems, recv_sems = refs[n], refs[n + 1]
        token = refs[-1]
        x, y, c = _place()
        for w in range(n):
            for k, chip in enumerate(_other_chips(x, y)):
                _fetch_copy(full[w], shard_shapes[w], col_sharded[w], 2 * x + y, send_sems, recv_sems, w * 3 + k,
                            chip, c).start()
        token[...] = jnp.zeros_like(token)

    outs = pl.pallas_call(
        body, name="fetch_start",
        out_shape=(pltpu.SemaphoreType.DMA((3 * n,)), pltpu.SemaphoreType.DMA((3 * n,)),
                   *[pltpu.HBM(f.shape, f.dtype) for f in fulls], jax.ShapeDtypeStruct((8, 128), F32)),
        in_specs=[_HBM] * n,
        out_specs=(_SEM, _SEM, *([_HBM] * n), pl.BlockSpec(memory_space=pltpu.VMEM)),
        input_output_aliases={i: 2 + i for i in range(n)},
        compiler_params=pltpu.CompilerParams(has_side_effects=pltpu.SideEffectType.DATAFLOW_SIDE_EFFECTING),
    )(*[pltpu.with_memory_space_constraint(f, pltpu.HBM) for f in fulls])
    return dict(send=outs[0], recv=outs[1], full=list(outs[2:2 + n])), outs[-1]


def _fetch_wait(group, shard_shapes, col_sharded, after):
    n = len(group["full"])

    def body(*refs):
        full = refs[:n]
        send_sems, recv_sems = refs[n], refs[n + 1]
        x, y, c = _place()
        for w in range(n):
            for k, chip in enumerate(_other_chips(x, y)):
                _fetch_copy(full[w], shard_shapes[w], col_sharded[w], 2 * x + y, send_sems, recv_sems, w * 3 + k,
                            chip, c).wait_send()
                _fetch_copy(full[w], shard_shapes[w], col_sharded[w], 2 * chip[0] + chip[1], send_sems, recv_sems,
                            w * 3 + k, chip, c).wait_recv()

    outs = pl.pallas_call(
        body, name="fetch_wait", out_shape=tuple(pltpu.HBM(f.shape, f.dtype) for f in group["full"]),
        in_specs=[_HBM] * n + [_SEM, _SEM, pl.BlockSpec(memory_space=pl.ANY)],
        out_specs=tuple([_HBM] * n), input_output_aliases={i: i for i in range(n)},
        compiler_params=pltpu.CompilerParams(has_side_effects=pltpu.SideEffectType.DATAFLOW_SIDE_EFFECTING),
    )(*group["full"], group["send"], group["recv"], after)
    return list(outs)


_HBM = pl.BlockSpec(memory_space=pltpu.HBM)
_SEM = pl.BlockSpec(memory_space=pltpu.SEMAPHORE)
_N_PEER = N_DEV - 1


def _peer(x, y, c, k):
    return (x + (k >> 2)) % 2, (y + ((k >> 1) & 1)) % 2, (c + (k & 1)) % 2


def _exchange_copy(src_ref, land_ref, sliced, send_sems, recv_sems, idx, x, y, c, k):
    px, py, pc = _peer(x, y, c, k)
    src = src_ref.at[4 * px + 2 * py + pc] if sliced else src_ref
    return pltpu.make_async_remote_copy(
        src_ref=src, dst_ref=land_ref.at[4 * x + 2 * y + c],
        send_sem=send_sems.at[idx], recv_sem=recv_sems.at[idx], device_id=(px, py, pc), device_id_type=MESH)


def _exchange_start(tag, arrays, sliced):
    n = len(arrays)
    lands = [lax.empty(a.shape if s else (N_DEV,) + a.shape, a.dtype) for a, s in zip(arrays, sliced)]

    def body(*refs):
        src, land = refs[:n], refs[n:2 * n]
        send_sems, recv_sems = refs[2 * n], refs[2 * n + 1]
        token = refs[-1]
        x, y, c = _place()
        for w in range(n):
            for k in range(1, N_DEV):
                _exchange_copy(src[w], land[w], sliced[w], send_sems, recv_sems, w * _N_PEER + k - 1, x, y, c, k).start()
        token[...] = jnp.zeros_like(token)

    outs = pl.pallas_call(
        body, name="exchange_start_" + tag,
        out_shape=(pltpu.SemaphoreType.DMA((n * _N_PEER,)), pltpu.SemaphoreType.DMA((n * _N_PEER,)),
                   *[pltpu.HBM(a.shape, a.dtype) for a in arrays], *[pltpu.HBM(l.shape, l.dtype) for l in lands],
                   jax.ShapeDtypeStruct((8, 128), F32)),
        in_specs=[_HBM] * (2 * n),
        out_specs=(_SEM, _SEM, *([_HBM] * (2 * n)), pl.BlockSpec(memory_space=pltpu.VMEM)),
        input_output_aliases={i: 2 + i for i in range(2 * n)},
        compiler_params=pltpu.CompilerParams(has_side_effects=pltpu.SideEffectType.DATAFLOW_SIDE_EFFECTING),
    )(*[pltpu.with_memory_space_constraint(a, pltpu.HBM) for a in arrays],
      *[pltpu.with_memory_space_constraint(l, pltpu.HBM) for l in lands])
    return dict(send=outs[0], recv=outs[1], src=list(outs[2:2 + n]), land=list(outs[2 + n:2 + 2 * n]),
                sliced=list(sliced)), outs[-1]


def _exchange_wait(tag, groups, after):
    counts = [len(g["src"]) for g in groups]
    total = sum(counts)

    def body(*refs):
        pos = 0
        x, y, c = _place()
        for g, n in zip(groups, counts):
            src, land = refs[pos:pos + n], refs[pos + n:pos + 2 * n]
            send_sems, recv_sems = refs[pos + 2 * n], refs[pos + 2 * n + 1]
            pos += 2 * n + 2
            for w in range(n):
                for k in range(1, N_DEV):
                    cp = _exchange_copy(src[w], land[w], g["sliced"][w], send_sems, recv_sems,
                                        w * _N_PEER + k - 1, x, y, c, k)
                    cp.wait_send()
                    cp.wait_recv()

    operands, in_specs, aliases, out_shape = [], [], {}, []
    for g in groups:
        for a in g["src"] + g["land"]:
            aliases[len(operands)] = len(out_shape)
            out_shape.append(pltpu.HBM(a.shape, a.dtype))
            operands.append(a)
            in_specs.append(_HBM)
        operands += [g["send"], g["recv"]]
        in_specs += [_SEM, _SEM]
    operands.append(after)
    in_specs.append(pl.BlockSpec(memory_space=pl.ANY))
    outs = pl.pallas_call(
        body, name="exchange_wait_" + tag, out_shape=tuple(out_shape), in_specs=in_specs,
        out_specs=tuple([_HBM] * (2 * total)), input_output_aliases=aliases,
        compiler_params=pltpu.CompilerParams(has_side_effects=pltpu.SideEffectType.DATAFLOW_SIDE_EFFECTING),
    )(*operands)
    srcs, lands, pos = [], [], 0
    for n in counts:
        srcs += list(outs[pos:pos + n])
        lands += list(outs[pos + n:pos + 2 * n])
        pos += 2 * n
    return srcs, lands


def _sum_and_swap(tag, pieces, lands, small=None, small_land=None):
    n_w = len(pieces)
    n_small = 0 if small is None else 1

    def body(*refs):
        g_refs, land_refs = refs[:n_w], refs[n_w:2 * n_w]
        pos = 2 * n_w + 2 * n_small
        out_refs = refs[pos:pos + n_w]
        pos += n_w + n_small
        bufs = refs[pos:pos + n_w]
        load_sems, swap_send, swap_recv = refs[pos + n_w + 2 * n_small:]
        x, y, c = _place()
        me = 4 * x + 2 * y + c

        def slot(k):
            px, py, pc = _peer(x, y, c, k)
            return 4 * px + 2 * py + pc

        def swap_copy(w, half):
            rows = pieces[w].shape[1]
            win = out_refs[w].at[pl.ds(pl.multiple_of(half * rows, rows), rows), :]
            return pltpu.make_async_remote_copy(
                src_ref=win, dst_ref=win, send_sem=swap_send.at[w], recv_sem=swap_recv.at[w],
                device_id=(x, y, 1 - c), device_id_type=MESH)

        loads = []
        for w in range(n_w):
            per_w = [pltpu.make_async_copy(g_refs[w].at[me], bufs[w].at[me], load_sems.at[w * N_DEV])]
            per_w += [pltpu.make_async_copy(land_refs[w].at[slot(k)], bufs[w].at[slot(k)], load_sems.at[w * N_DEV + k])
                      for k in range(1, N_DEV)]
            loads.append(per_w)
        small_loads = []
        if n_small:
            small_ref, small_land_ref = refs[2 * n_w], refs[2 * n_w + 1]
            small_out = refs[2 * n_w + 2 + n_w]
            small_buf, small_sems = refs[pos + n_w], refs[pos + n_w + 1]
            small_loads = [pltpu.make_async_copy(small_land_ref.at[slot(k)], small_buf.at[slot(k)],
                                                 small_sems.at[k - 1]) for k in range(1, N_DEV)]
        for cp in [cp for per_w in loads for cp in per_w] + small_loads:
            cp.start()
        if n_small:
            small_buf[me] = small_ref[...]
        swaps = []
        for w in range(n_w):
            for cp in loads[w]:
                cp.wait()
            rows = pieces[w].shape[1]
            total = bufs[w][0].astype(F32)
            for p in range(1, N_DEV):
                total += bufs[w][p].astype(F32)
            out_refs[w][pl.ds(pl.multiple_of(c * rows, rows), rows), :] = total
            sw = swap_copy(w, c)
            sw.start()
            swaps.append(sw)
        if n_small:
            for cp in small_loads:
                cp.wait()
            total = small_buf[0]
            for p in range(1, N_DEV):
                total += small_buf[p]
            small_out[...] = total
        for w in range(n_w):
            swap_copy(w, 1 - c).wait_recv()
        for sw in swaps:
            sw.wait_send()

    vmem = pl.BlockSpec(memory_space=pltpu.VMEM)
    hbm = pl.BlockSpec(memory_space=pl.ANY)
    small_args = [small, small_land] if n_small else []
    small_shapes = [jax.ShapeDtypeStruct(small.shape, F32)] if n_small else []
    small_scratch = ([pltpu.VMEM((N_DEV,) + small.shape, F32), pltpu.SemaphoreType.DMA((_N_PEER,))]
                     if n_small else [])
    return pl.pallas_call(
        body, name="sum_and_swap_" + tag,
        in_specs=[hbm] * (2 * n_w) + [vmem, hbm] * n_small,
        out_specs=[vmem] * (n_w + n_small),
        out_shape=[jax.ShapeDtypeStruct((2 * p.shape[1], p.shape[2]), F32) for p in pieces] + small_shapes,
        scratch_shapes=[pltpu.VMEM(p.shape, BF16) for p in pieces] + small_scratch
        + [pltpu.SemaphoreType.DMA((n_w * N_DEV,)), pltpu.SemaphoreType.DMA((n_w,)),
           pltpu.SemaphoreType.DMA((n_w,))],
        compiler_params=pltpu.CompilerParams(vmem_limit_bytes=VMEM_LIMIT),
    )(*pieces, *lands, *small_args)


def _adamw_update(w_ref, g_ref, m_ref, v_ref, d_ref, nm_ref, nv_ref):
    c1 = 1.0 - ADAM_B1 ** ADAM_STEP
    c2 = 1.0 - ADAM_B2 ** ADAM_STEP
    g_t = g_ref[...]
    nm = ADAM_B1 * m_ref[...] + (1.0 - ADAM_B1) * g_t
    nv = ADAM_B2 * v_ref[...] + (1.0 - ADAM_B2) * (g_t * g_t)
    d_ref[...] = -ADAM_LR * ((nm / c1) / (jnp.sqrt(nv / c2) + ADAM_EPS) + ADAM_WD * w_ref[...])
    nm_ref[...] = nm
    nv_ref[...] = nv


def _adamw_small(items):
    n = len(items)
    flat, shapes = [], []
    for w, g, m, v in items:
        cols = w.shape[-1]
        shapes.append(w.shape)
        flat += [a.reshape(w.size // cols, cols) for a in (w, g, m, v)]

    def body(*refs):
        for k in range(n):
            _adamw_update(*refs[4 * k:4 * k + 4], *refs[4 * n + 3 * k:4 * n + 3 * k + 3])

    outs = pl.pallas_call(
        body, name="adamw_small",
        out_shape=[jax.ShapeDtypeStruct(flat[4 * k].shape, F32) for k in range(n) for _ in range(3)],
    )(*flat)
    return [tuple(o.reshape(shapes[k]) for o in outs[3 * k:3 * k + 3]) for k in range(n)]


def _adamw(label, w, g, m, v):
    shape = w.shape
    cols = shape[-1]
    rows = w.size // cols
    args = [a.reshape(rows, cols) for a in (w, g, m, v)]

    def body(*refs):
        _adamw_update(*refs)

    block_rows = 256 if rows % 256 == 0 and rows > 256 else rows
    spec = pl.BlockSpec((block_rows, cols), lambda i: (i, 0))
    outs = pl.pallas_call(
        body, name="adamw_" + label, grid=(rows // block_rows,),
        in_specs=[spec] * 4, out_specs=[spec] * 3,
        out_shape=[jax.ShapeDtypeStruct((rows, cols), F32)] * 3,
        compiler_params=_params(),
    )(*args)
    return [o.reshape(shape) for o in outs]


def _no_send(tag, arrays, sliced):
    return jnp.zeros((8, 128), F32)


def _local_step(x, tgt, w_in_a, later_weights, first_after, sgu_ln_g, sgu_ln_b, w_spatial, b_spatial,
                attn_sinks, rel_bias, post_ln_g, post_ln_b, send=_no_send):
    bsp_t = b_spatial.T
    g1, b1 = post_ln_g[0:1], post_ln_b[0:1]
    g2, b2 = post_ln_g[1:2], post_ln_b[1:2]
    onehot = _bucket_onehot()
    bias = _bias_expand(rel_bias.T, onehot).reshape(N_HEADS, 2 * CHUNK, CHUNK)
    win = _window_tables()

    xt, u, vh, z, rv, y = _layer_a_fwd(x, w_in_a, sgu_ln_g, sgu_ln_b, w_spatial, bsp_t, first_after)
    w_out_a, w_kv, w_in_b, w_out_b = later_weights(y)
    xh1, rstd1, q, zb, kd, vd = _layer_b_proj(x, y, w_out_a, g1, b1, w_in_b, w_kv)
    o, probs, sink_probs, dr2, loss_vec, dg2, db2 = _layer_b_fwd(q, zb, kd, vd, bias, win, attn_sinks, xh1, g1, b1,
                                                                 w_out_b, g2, b2, tgt)
    dq, dzb, dkd, dvd, carry_k, carry_v, gw_out_b, dsink, dbias = _layer_b_bwd_attn(
        dr2, zb, o, q, kd, vd, probs, sink_probs, w_out_b)
    dr1, dg1, db1, gw_in_b, gw_kv = _layer_b_bwd_proj(xh1, rstd1, g1, b1, dr2, dq, dzb, dkd, dvd, carry_k, carry_v,
                                                      w_in_b, w_kv)
    gw_out_b = gw_out_b.reshape(N_DEV, -1, D_MODEL)
    gw_kv = gw_kv.reshape(N_DEV, -1, 2 * PAIR)
    after = send("b", [gw_out_b, gw_in_b, gw_kv], [True, True, True])
    dp, gw_out_a, dws, dbsp, dgs, dbs = _layer_a_bwd_mix(dr1, u, vh, z, y, rv, w_out_a, sgu_ln_g, sgu_ln_b,
                                                         w_spatial, bsp_t, after)
    gw_out_a = gw_out_a.reshape(N_DEV, -1, D_MODEL)
    drel = _bias_reduce(onehot, dbias.reshape(N_HEADS, -1))
    dsink = jnp.sum(dsink.reshape(N_HEADS, CHUNK), axis=1).reshape(1, N_HEADS)
    loss = ((0.5 / D_MODEL) * jnp.sum(loss_vec)).reshape(1, 1)
    small = dict(w_spatial=dws, b_spatial=dbsp[:, 0:A_GROUPS].T, attn_sinks=dsink,
                 rel_bias=drel, post_ln_g=jnp.concatenate([dg1, dg2], axis=0),
                 post_ln_b=jnp.concatenate([db1, db2], axis=0), sgu_ln_g=dgs, sgu_ln_b=dbs, loss=loss)
    after = send("a_out", [gw_out_a, _pack_small(small)], [True, False])
    gw_in_a = _layer_a_bwd_win(xt, dp, after).reshape(N_DEV, D_MODEL // 2, -1)
    after = send("a_in", [gw_in_a], [True])
    grad_x = _layer_a_bwd_dx(dr1, dp, w_in_a, after)

    pieces = [gw_in_a, gw_out_a, gw_kv, gw_in_b, gw_out_b]
    return grad_x, pieces, small


_SMALL_SHAPES = dict(w_spatial=(A_GROUPS, CHUNK, CHUNK), b_spatial=(A_GROUPS, CHUNK), attn_sinks=(1, N_HEADS),
                     rel_bias=(REL_BUCKETS, N_HEADS), post_ln_g=(2, D_MODEL), post_ln_b=(2, D_MODEL),
                     sgu_ln_g=(1, A_WIDTH), sgu_ln_b=(1, A_WIDTH), loss=(1, 1))
_SMALL_ORDER = tuple(_SMALL_SHAPES)
_LANES = 128
_SUBLANES = 8


def _small_rows(name):
    size = math.prod(_SMALL_SHAPES[name])
    return -(-size // (_LANES * _SUBLANES)) * _SUBLANES


def _pack_small(small):
    parts = []
    for name in _SMALL_ORDER:
        assert small[name].shape == _SMALL_SHAPES[name], (name, small[name].shape)
        flat = small[name].reshape(-1)
        flat = jnp.pad(flat, (0, _small_rows(name) * _LANES - flat.size))
        parts.append(flat.reshape(-1, _LANES))
    return jnp.concatenate(parts, axis=0)


def _unpack_small(packed):
    out, row = {}, 0
    for name in _SMALL_ORDER:
        shape = _SMALL_SHAPES[name]
        out[name] = packed[row:row + _small_rows(name)].reshape(-1)[:math.prod(shape)].reshape(shape)
        row += _small_rows(name)
    return out


def kernel(x, w_in_a, sgu_ln_g, sgu_ln_b, w_spatial, b_spatial, w_out_a, w_kv, w_in_b, attn_sinks, rel_bias, w_out_b, post_ln_g, post_ln_b, loss_target, m_w_in_a, m_sgu_ln_g, m_sgu_ln_b, m_w_spatial, m_b_spatial, m_w_out_a, m_w_kv, m_w_in_b, m_attn_sinks, m_rel_bias, m_w_out_b, m_post_ln_g, m_post_ln_b, v_w_in_a, v_sgu_ln_g, v_sgu_ln_b, v_w_spatial, v_b_spatial, v_w_out_a, v_w_kv, v_w_in_b, v_attn_sinks, v_rel_bias, v_w_out_b, v_post_ln_g, v_post_ln_b):
    weights = dict(w_in_a=w_in_a, sgu_ln_g=sgu_ln_g, sgu_ln_b=sgu_ln_b, w_spatial=w_spatial, b_spatial=b_spatial,
                   w_out_a=w_out_a, w_kv=w_kv, w_in_b=w_in_b, attn_sinks=attn_sinks, rel_bias=rel_bias,
                   w_out_b=w_out_b, post_ln_g=post_ln_g, post_ln_b=post_ln_b)
    moments_m = dict(w_in_a=m_w_in_a, sgu_ln_g=m_sgu_ln_g, sgu_ln_b=m_sgu_ln_b, w_spatial=m_w_spatial,
                     b_spatial=m_b_spatial, w_out_a=m_w_out_a, w_kv=m_w_kv, w_in_b=m_w_in_b,
                     attn_sinks=m_attn_sinks, rel_bias=m_rel_bias, w_out_b=m_w_out_b, post_ln_g=m_post_ln_g,
                     post_ln_b=m_post_ln_b)
    moments_v = dict(w_in_a=v_w_in_a, sgu_ln_g=v_sgu_ln_g, sgu_ln_b=v_sgu_ln_b, w_spatial=v_w_spatial,
                     b_spatial=v_b_spatial, w_out_a=v_w_out_a, w_kv=v_w_kv, w_in_b=v_w_in_b,
                     attn_sinks=v_attn_sinks, rel_bias=v_rel_bias, w_out_b=v_w_out_b, post_ln_g=v_post_ln_g,
                     post_ln_b=v_post_ln_b)
    order = ("w_in_a", "sgu_ln_g", "sgu_ln_b", "w_spatial", "b_spatial", "w_out_a", "w_kv", "w_in_b", "attn_sinks",
             "rel_bias", "w_out_b", "post_ln_g", "post_ln_b")

    shard_index = 2 * lax.axis_index("x") + lax.axis_index("y")
    ln_shard = jnp.concatenate([sgu_ln_g, sgu_ln_b], axis=0)
    shards = [w_in_a[0], w_out_a[0], w_kv, w_in_b[0], w_out_b[0]]
    col_sharded = [True, False, False, True, False]
    full_in_a, *later, ln_full = _gather_weights(shards, col_sharded, [True, False, False, False, False], ln_shard)
    ln_full = jnp.transpose(ln_full, (1, 0, 2)).reshape(2, A_WIDTH)
    later_shapes = [s.shape for s in shards[1:]]
    fetch_group, fetch_token = _fetch_start(later, later_shapes, col_sharded[1:])

    def later_weights(y):
        return _fetch_wait(fetch_group, later_shapes, col_sharded[1:], y)

    groups, grads, deltas, new_m, new_v, scalars = {}, {}, {}, {}, {}, {}

    def apply_large(names, reduced):
        for name, g in zip(names, reduced):
            grads[name] = g.reshape(weights[name].shape)
            deltas[name], new_m[name], new_v[name] = _adamw(name, weights[name], grads[name], moments_m[name],
                                                            moments_v[name])

    def send(tag, arrays, sliced):
        groups[tag], token = _exchange_start(tag, arrays, sliced)
        if tag != "a_in":
            return token
        srcs, lands = _exchange_wait("early", [groups["b"], groups["a_out"]], token)
        *reduced, packed_sum = _sum_and_swap("early", srcs[:4], lands[:4], srcs[4], lands[4])
        apply_large(("w_out_b", "w_in_b", "w_kv", "w_out_a"), reduced)
        small_sum = _unpack_small(packed_sum)
        scalars["loss"] = small_sum["loss"].reshape(())
        shard_cols = sgu_ln_g.shape[1]
        for name in ("sgu_ln_g", "sgu_ln_b"):
            grads[name] = lax.dynamic_slice(small_sum[name], (0, shard_index * shard_cols), (1, shard_cols))
        small_names = ("sgu_ln_g", "sgu_ln_b", "w_spatial", "b_spatial", "attn_sinks", "rel_bias", "post_ln_g",
                       "post_ln_b")
        for name in small_names[2:]:
            grads[name] = small_sum[name].reshape(weights[name].shape)
        updates = _adamw_small([(weights[n], grads[n], moments_m[n], moments_v[n]) for n in small_names])
        for name, (d, nm, nv) in zip(small_names, updates):
            deltas[name], new_m[name], new_v[name] = d, nm, nv
        return packed_sum[0:_SUBLANES]

    grad_x, _, _ = _local_step(
        x[0], loss_target[0], full_in_a, later_weights, fetch_token, ln_full[0:1], ln_full[1:2], w_spatial[0],
        b_spatial[0], attn_sinks, rel_bias, post_ln_g, post_ln_b, send=send)

    srcs, lands = _exchange_wait("late", [groups["a_in"]], grad_x)
    apply_large(("w_in_a",), _sum_and_swap("late", srcs, lands))
    return (scalars["loss"], grad_x[None], *[grads[n] for n in order], *[deltas[n] for n in order],
            *[new_m[n] for n in order], *[new_v[n] for n in order])
```

```python
import math

import jax
import jax.numpy as jnp
from jax import lax
from jax.experimental import pallas as pl
from jax.experimental.pallas import tpu as pltpu

F32 = jnp.float32
BF16 = jnp.bfloat16

D_MODEL = 1024
A_WIDTH = 2048
A_GROUPS = 8
A_GROUP_DIM = 256
CHUNK = 128
N_HEADS = 16
N_KV = 2
HEAD_DIM = 64
PAIR = 2 * HEAD_DIM
B_WIDTH = 1024
REL_BUCKETS = 32
ALPHA = 4.0 ** 0.25
LN_EPS = 1e-5
NEG_INF = -1e30
SCALE = HEAD_DIM ** -0.5

ADAM_LR = 0.001
ADAM_B1 = 0.9
ADAM_B2 = 0.999
ADAM_EPS = 1e-08
ADAM_WD = 0.01
ADAM_STEP = 10

N_DEV = 8
N_CHIPS = 4
MESH = pl.DeviceIdType.MESH
VMEM_LIMIT = 56 * 1024 * 1024

TM_ATTN = 256
TM_MM = 512
TM_WIN = 1024
_LANES = 128
_SUBLANES = 8


def _dot(a, b):
    return jnp.dot(a, b, preferred_element_type=F32)


def _dot_nt(a, b):
    return lax.dot_general(a, b, (((1,), (1,)), ((), ())), preferred_element_type=F32)


def _dot_tn(a, b):
    return lax.dot_general(a, b, (((0,), (0,)), ((), ())), preferred_element_type=F32)


def _ln_fwd(r):
    mu = jnp.mean(r, axis=-1, keepdims=True)
    rc = r - mu
    var = jnp.mean(rc * rc, axis=-1, keepdims=True)
    rstd = lax.rsqrt(var + LN_EPS)
    return rc * rstd, rstd


def _ln_bwd(dxh, xh, rstd):
    m1 = jnp.mean(dxh, axis=-1, keepdims=True)
    m2 = jnp.mean(dxh * xh, axis=-1, keepdims=True)
    return rstd * (dxh - m1 - xh * m2)


def _silu_parts(z):
    sg = jax.nn.sigmoid(z)
    return z * sg, sg * (1.0 + z * (1.0 - sg))


def _dup_halves(blk):
    sw = pltpu.roll(blk, HEAD_DIM, 1)
    lo = lax.broadcasted_iota(jnp.int32, blk.shape, 1) < HEAD_DIM
    return jnp.where(lo, blk, sw), jnp.where(lo, sw, blk)


def _fold_halves(blk):
    return blk + pltpu.roll(blk, HEAD_DIM, 1)


def _resident(shape):
    nd = len(shape)
    return pl.BlockSpec(shape, lambda *_: (0,) * nd, pipeline_mode=pl.Buffered(1))


def _const(shape):
    nd = len(shape)
    return pl.BlockSpec(shape, lambda *_: (0,) * nd)


def _rows(tm, cols):
    return pl.BlockSpec((tm, cols), lambda i: (i, 0))


def _params(sem=("arbitrary",)):
    return pltpu.CompilerParams(dimension_semantics=sem, vmem_limit_bytes=VMEM_LIMIT)


def _spatial_mix(ws_ref, bsp_ref, vn, s_scr, n_chunks):
    tri = (lax.broadcasted_iota(jnp.int32, (CHUNK, CHUNK), 0)
           >= lax.broadcasted_iota(jnp.int32, (CHUNK, CHUNK), 1))
    for g in range(A_GROUPS):
        wsg = jnp.where(tri, ws_ref[g], 0.0).astype(BF16)
        cols = slice(g * A_GROUP_DIM, (g + 1) * A_GROUP_DIM)
        for ci in range(n_chunks):
            rows = slice(ci * CHUNK, (ci + 1) * CHUNK)
            s_scr[rows, cols] = _dot(wsg, vn[rows, cols]) + bsp_ref[:, g:g + 1]


def _layer_a_in_proj(x, w_shard, after):
    t_len = x.shape[0]
    tm = TM_MM
    n_steps = t_len // tm
    rows, cols = w_shard.shape
    half = rows // 2
    chip_of_pass = {1: 1, 2: 0, 3: 2}

    def body(shard_ref, x_ref, w_ref, after_ref, p_ref, full_ref, land, send_sems, recv_sems, out_sems):
        j, i = pl.program_id(0), pl.program_id(1)
        cx, cy, c = _place()
        s_me = shard_ref[0]
        chips = _other_chips(cx, cy)

        def half_rows(s, h):
            return land.at[s, pl.ds(h * half, half), :]

        def ici_copy(k, sender_shard):
            win = half_rows(sender_shard, c)
            return pltpu.make_async_remote_copy(src_ref=win, dst_ref=win, send_sem=send_sems.at[k],
                                                recv_sem=recv_sems.at[k], device_id=(*chips[k], c),
                                                device_id_type=MESH)

        def d2d_copy(k, h):
            win = half_rows(2 * chips[k][0] + chips[k][1], h)
            return pltpu.make_async_remote_copy(src_ref=win, dst_ref=win, send_sem=send_sems.at[3 + k],
                                                recv_sem=recv_sems.at[3 + k], device_id=(cx, cy, 1 - c),
                                                device_id_type=MESH)

        @pl.when((j == 0) & (i == 0))
        def _():
            land[s_me] = w_ref[...].astype(BF16)
            for k in range(3):
                ici_copy(k, s_me).start()

        for jj, k in chip_of_pass.items():
            @pl.when((j == jj) & (i == 0))
            def _(k=k):
                ici_copy(k, 2 * chips[k][0] + chips[k][1]).wait_recv()
                d2d_copy(k, c).start()
                d2d_copy(k, 1 - c).wait_recv()

        shard = jnp.bitwise_xor(s_me, j)
        p_ref[...] = _dot(x_ref[...].astype(BF16), land[shard])

        @pl.when((j == N_CHIPS - 1) & (i == n_steps - 1))
        def _():
            for k in range(3):
                ici_copy(k, s_me).wait_send()
                d2d_copy(k, c).wait_send()
            outs = [pltpu.make_async_copy(land.at[s], full_ref.at[:, pl.ds(s * cols, cols)], out_sems.at[s])
                    for s in range(N_CHIPS)]
            for cp in outs:
                cp.start()
            for cp in outs:
                cp.wait()

    own_shard = (2 * lax.axis_index("x") + lax.axis_index("y")).astype(jnp.int32).reshape(1)
    grid_spec = pltpu.PrefetchScalarGridSpec(
        num_scalar_prefetch=1, grid=(N_CHIPS, n_steps),
        in_specs=[pl.BlockSpec((tm, D_MODEL), lambda j, i, s: (i, 0)), _resident(w_shard.shape),
                  _const(after.shape)],
        out_specs=[pl.BlockSpec((tm, cols), lambda j, i, s: (i, jnp.bitwise_xor(s[0], j))),
                   pl.BlockSpec(memory_space=pl.ANY)],
        scratch_shapes=[pltpu.VMEM((N_CHIPS, rows, cols), BF16), pltpu.SemaphoreType.DMA((6,)),
                        pltpu.SemaphoreType.DMA((6,)), pltpu.SemaphoreType.DMA((N_CHIPS,))])
    return pl.pallas_call(
        body, name="layer_a_in_proj", grid_spec=grid_spec,
        out_shape=[jax.ShapeDtypeStruct((t_len, N_CHIPS * cols), F32),
                   jax.ShapeDtypeStruct((rows, N_CHIPS * cols), BF16)],
        compiler_params=_params(("arbitrary", "arbitrary")),
    )(own_shard, x, w_shard, after)


def _layer_a_fwd(p, lng, lnb, ws, bsp_t):
    t_len = p.shape[0]
    tm = TM_ATTN

    def body(p_ref, lng_ref, lnb_ref, ws_ref, bsp_ref, u_ref, vh_ref, z_ref, rv_ref, y_ref, s_scr):
        u = p_ref[:, 0:A_WIDTH]
        z = p_ref[:, 2 * A_WIDTH:3 * A_WIDTH]
        vh, rv = _ln_fwd(p_ref[:, A_WIDTH:2 * A_WIDTH])
        vn = (vh * lng_ref[...] + lnb_ref[...]).astype(BF16)
        _spatial_mix(ws_ref, bsp_ref, vn, s_scr, tm // CHUNK)
        sz, _ = _silu_parts(z)
        y_ref[...] = (u * s_scr[...] * sz).astype(BF16)
        u_ref[...] = u.astype(BF16)
        vh_ref[...] = vh.astype(BF16)
        z_ref[...] = z.astype(BF16)
        rv_ref[...] = rv

    wide = jax.ShapeDtypeStruct((t_len, A_WIDTH), BF16)
    return pl.pallas_call(
        body, name="layer_a_fwd", grid=(t_len // tm,),
        in_specs=[_rows(tm, 3 * A_WIDTH), _const(lng.shape), _const(lnb.shape), _const(ws.shape),
                  _const(bsp_t.shape)],
        out_specs=[_rows(tm, A_WIDTH), _rows(tm, A_WIDTH), _rows(tm, A_WIDTH), _rows(tm, 1), _rows(tm, A_WIDTH)],
        out_shape=[wide, wide, wide, jax.ShapeDtypeStruct((t_len, 1), F32), wide],
        scratch_shapes=[pltpu.VMEM((tm, A_WIDTH), F32)],
        compiler_params=_params(),
    )(p, lng, lnb, ws, bsp_t)


def _layer_b_proj(x, y, w_out_a, g1, b1, w_in, w_kv):
    t_len = x.shape[0]
    tm = TM_MM

    def body(x_ref, y_ref, wout_ref, g_ref, b_ref, win_ref, wkv_ref, xh_ref, r1_ref, q_ref, z_ref, kd_ref, vd_ref):
        xh, r1 = _ln_fwd(ALPHA * x_ref[...] + _dot(y_ref[...], wout_ref[...]))
        xh_ref[...] = xh
        r1_ref[...] = r1
        h1 = (xh * g_ref[...] + b_ref[...]).astype(BF16)
        q_ref[...] = (_dot(h1, win_ref[:, 0:B_WIDTH]) * SCALE).astype(BF16)
        z_ref[...] = _dot(h1, win_ref[:, B_WIDTH:2 * B_WIDTH]).astype(BF16)
        kv = _dot(h1, wkv_ref[...])
        k0, k1 = _dup_halves(kv[:, 0:PAIR])
        v0, v1 = _dup_halves(kv[:, PAIR:2 * PAIR])
        kd_ref[:, 0:PAIR] = k0.astype(BF16)
        kd_ref[:, PAIR:2 * PAIR] = k1.astype(BF16)
        vd_ref[:, 0:PAIR] = v0.astype(BF16)
        vd_ref[:, PAIR:2 * PAIR] = v1.astype(BF16)

    return pl.pallas_call(
        body, name="layer_b_proj", grid=(t_len // tm,),
        in_specs=[_rows(tm, D_MODEL), _rows(tm, A_WIDTH), _resident(w_out_a.shape), _const(g1.shape),
                  _const(b1.shape), _resident(w_in.shape), _resident(w_kv.shape)],
        out_specs=[_rows(tm, D_MODEL), _rows(tm, 1), _rows(tm, B_WIDTH), _rows(tm, B_WIDTH), _rows(tm, 2 * PAIR),
                   _rows(tm, 2 * PAIR)],
        out_shape=[jax.ShapeDtypeStruct((t_len, D_MODEL), F32), jax.ShapeDtypeStruct((t_len, 1), F32),
                   jax.ShapeDtypeStruct((t_len, B_WIDTH), BF16), jax.ShapeDtypeStruct((t_len, B_WIDTH), BF16),
                   jax.ShapeDtypeStruct((t_len, 2 * PAIR), BF16), jax.ShapeDtypeStruct((t_len, 2 * PAIR), BF16)],
        compiler_params=_params(),
    )(x, y, w_out_a, g1, b1, w_in, w_kv)


GROUP = N_HEADS // N_KV
GROUP_Q = GROUP * CHUNK


def _window_tables():
    j = jnp.arange(2 * CHUNK, dtype=jnp.int32)[:, None]
    t = jnp.arange(CHUNK, dtype=jnp.int32)[None, :]
    dist = t + CHUNK - j
    inside = (dist >= 0) & (dist < CHUNK)
    return jnp.stack([inside & (j >= CHUNK), inside]).astype(F32)


def _band(ref, chunk_index, kvh):
    prev0 = pl.multiple_of(jnp.maximum(chunk_index - 1, 0) * CHUNK, CHUNK)
    cur0 = pl.multiple_of(chunk_index * CHUNK, CHUNK)
    cols = slice(kvh * PAIR, (kvh + 1) * PAIR)
    return jnp.concatenate([ref[pl.ds(prev0, CHUNK), cols], ref[pl.ds(cur0, CHUNK), cols]], axis=0)


def _group_tables(bias_ref, win_ref, sink_ref, chunk_index, kvh):
    bias = jnp.concatenate([bias_ref[kvh * GROUP + j] for j in range(GROUP)], axis=1)
    win = win_ref[jnp.minimum(chunk_index, 1)]
    mask = jnp.concatenate([win] * GROUP, axis=1) > 0.5
    sink = jnp.concatenate([jnp.full((1, CHUNK), sink_ref[0, kvh * GROUP + j], F32) for j in range(GROUP)], axis=1)
    return bias, mask, sink


def _attn_probs(qs, kband, bias, mask, sink):
    logits = jnp.where(mask, _dot_nt(kband, qs) + bias, NEG_INF)
    m = jnp.maximum(jnp.max(logits, axis=0, keepdims=True), sink)
    e = jnp.exp(logits - m)
    es = jnp.exp(sink - m)
    inv = 1.0 / (jnp.sum(e, axis=0, keepdims=True) + es)
    return e * inv, es * inv


def _half_mask():
    return lax.broadcasted_iota(jnp.int32, (CHUNK, PAIR), 1) < HEAD_DIM


def _stack_heads(src_ref, rows, kvh, dst_scr, lo):
    for j in range(GROUP):
        h = kvh * GROUP + j
        blk = src_ref[rows, (h // 2) * PAIR:(h // 2 + 1) * PAIR].astype(F32)
        keep = lo if h % 2 == 0 else ~lo
        dst_scr[j * CHUNK:(j + 1) * CHUNK, :] = jnp.where(keep, blk, 0.0).astype(BF16)


def _probs_spec(tm):
    return pl.BlockSpec((tm // CHUNK, N_KV, 2 * CHUNK, GROUP_Q), lambda i: (i, 0, 0, 0))


def _sink_probs_spec():
    return pl.BlockSpec((1, 8, GROUP_Q), lambda i: (i, 0, 0))


def _unstack_pairs(stacked, pp, lo):
    return jnp.where(lo, stacked[(2 * pp) * CHUNK:(2 * pp + 1) * CHUNK], stacked[(2 * pp + 1) * CHUNK:(2 * pp + 2) * CHUNK])


def _layer_b_fwd(q, zb, kd, vd, bias, win, sinks, xh1, g1, b1, w_out, g2, b2, tgt):
    t_len = q.shape[0]
    tm = TM_ATTN

    def body(q_ref, z_ref, kd_ref, vd_ref, bias_ref, win_ref, sink_ref, xh_ref, g1_ref, b1_ref, wout_ref, g2_ref,
             b2_ref, tgt_ref, o_ref, p_ref, ps_ref, dr_ref, loss_ref, dg_ref, db_ref, o_scr, qs_scr):
        i = pl.program_id(0)

        @pl.when(i == 0)
        def _():
            loss_ref[...] = jnp.zeros_like(loss_ref)
            dg_ref[...] = jnp.zeros_like(dg_ref)
            db_ref[...] = jnp.zeros_like(db_ref)

        lo = _half_mask()
        ps_ref[...] = jnp.zeros_like(ps_ref)
        for ci in range(tm // CHUNK):
            cg = i * (tm // CHUNK) + ci
            rows = slice(ci * CHUNK, (ci + 1) * CHUNK)
            for kvh in range(N_KV):
                kband = _band(kd_ref, cg, kvh)
                vband = _band(vd_ref, cg, kvh)
                bias_g, mask, sink = _group_tables(bias_ref, win_ref, sink_ref, cg, kvh)
                _stack_heads(q_ref, rows, kvh, qs_scr, lo)
                p, p_sink = _attn_probs(qs_scr[...], kband, bias_g, mask, sink)
                p = p.astype(BF16)
                p_ref[ci, kvh] = p
                ps_ref[0, ci * N_KV + kvh:ci * N_KV + kvh + 1, :] = p_sink
                o_stack = _dot_tn(p, vband)
                for pp in range(GROUP // 2):
                    pair = kvh * (GROUP // 2) + pp
                    o_scr[rows, pair * PAIR:(pair + 1) * PAIR] = _unstack_pairs(o_stack, pp, lo)
        o = o_scr[...]
        o_ref[...] = o.astype(BF16)
        sz, _ = _silu_parts(z_ref[...].astype(F32))
        y = (o * sz).astype(BF16)
        h1 = xh_ref[...] * g1_ref[...] + b1_ref[...]
        r = ALPHA * h1 + _dot(y, wout_ref[...])
        xh2, rstd2 = _ln_fwd(r)
        diff = xh2 * g2_ref[...] + b2_ref[...] - tgt_ref[...]
        loss_ref[...] += jnp.sum(diff * diff, axis=0, keepdims=True)
        dh2 = diff * (1.0 / D_MODEL)
        dg_ref[...] += jnp.sum(dh2 * xh2, axis=0, keepdims=True)
        db_ref[...] += jnp.sum(dh2, axis=0, keepdims=True)
        dr_ref[...] = _ln_bwd(dh2 * g2_ref[...], xh2, rstd2)

    vec = jax.ShapeDtypeStruct((1, D_MODEL), F32)
    return pl.pallas_call(
        body, name="layer_b_fwd", grid=(t_len // tm,),
        in_specs=[_rows(tm, B_WIDTH), _rows(tm, B_WIDTH), _resident(kd.shape), _resident(vd.shape),
                  _resident(bias.shape), _resident(win.shape), pl.BlockSpec(memory_space=pltpu.SMEM),
                  _rows(tm, D_MODEL), _const(g1.shape), _const(b1.shape), _resident(w_out.shape), _const(g2.shape),
                  _const(b2.shape), _rows(tm, D_MODEL)],
        out_specs=[_rows(tm, B_WIDTH), _probs_spec(tm), _sink_probs_spec(), _rows(tm, D_MODEL)]
        + [_const((1, D_MODEL))] * 3,
        out_shape=[jax.ShapeDtypeStruct((t_len, B_WIDTH), BF16),
                   jax.ShapeDtypeStruct((t_len // CHUNK, N_KV, 2 * CHUNK, GROUP_Q), BF16),
                   jax.ShapeDtypeStruct((t_len // tm, 8, GROUP_Q), F32),
                   jax.ShapeDtypeStruct((t_len, D_MODEL), F32), vec, vec, vec],
        scratch_shapes=[pltpu.VMEM((tm, B_WIDTH), F32), pltpu.VMEM((GROUP_Q, PAIR), BF16)],
        compiler_params=_params(),
    )(q, zb, kd, vd, bias, win, sinks, xh1, g1, b1, w_out, g2, b2, tgt)


def _layer_b_bwd_attn(dr2, zb, o, q, kd, vd, probs, sink_probs, w_out):
    t_len = q.shape[0]
    tm = TM_ATTN
    n_steps = t_len // tm
    n_chunks = tm // CHUNK

    def body(dr_ref, z_ref, o_ref, q_ref, kd_ref, vd_ref, p_ref, ps_ref, wout_ref,
             dq_ref, dz_ref, dkd_ref, dvd_ref, ck_ref, cv_ref, gw_ref, dsink_ref, dbias_ref,
             do_scr, qs_scr, dos_scr, gw_acc):
        i = pl.program_id(0)

        @pl.when(i == 0)
        def _():
            gw_acc[...] = jnp.zeros_like(gw_acc)
            dsink_ref[...] = jnp.zeros_like(dsink_ref)
            dbias_ref[...] = jnp.zeros_like(dbias_ref)

        drb = dr_ref[...].astype(BF16)
        dy = _dot_nt(drb, wout_ref[...])
        z = z_ref[...].astype(F32)
        sz, dsz = _silu_parts(z)
        o_t = o_ref[...].astype(F32)
        dz_ref[...] = (dy * o_t * dsz).astype(BF16)
        do_scr[...] = (dy * sz).astype(BF16)
        gw_acc[...] += _dot_tn((o_t * sz).astype(BF16), drb)

        lo = _half_mask()
        for kvh in range(N_KV):
            kcols = slice(kvh * PAIR, (kvh + 1) * PAIR)
            dk_bands, dv_bands = [], []
            for ci in range(n_chunks):
                cg = i * n_chunks + ci
                rows = slice(ci * CHUNK, (ci + 1) * CHUNK)
                kband = _band(kd_ref, cg, kvh)
                vband = _band(vd_ref, cg, kvh)
                _stack_heads(q_ref, rows, kvh, qs_scr, lo)
                _stack_heads(do_scr, rows, kvh, dos_scr, lo)
                qs = qs_scr[...]
                dos = dos_scr[...]
                pb = p_ref[ci, kvh]
                p = pb.astype(F32)
                p_sink = ps_ref[0, ci * N_KV + kvh:ci * N_KV + kvh + 1, :]
                dp = _dot_nt(vband, dos)
                delta = jnp.sum(p * dp, axis=0, keepdims=True)
                dlog = p * (dp - delta)
                for j in range(GROUP):
                    dbias_ref[kvh * GROUP + j] += dlog[:, j * CHUNK:(j + 1) * CHUNK]
                dsink_ref[kvh:kvh + 1, :] += -(p_sink * delta)
                ds = dlog.astype(BF16)
                dq_stack = _dot_tn(ds, kband) * SCALE
                for pp in range(GROUP // 2):
                    pair = kvh * (GROUP // 2) + pp
                    dq_ref[rows, pair * PAIR:(pair + 1) * PAIR] = _unstack_pairs(dq_stack, pp, lo).astype(BF16)
                dk_bands.append(_dot(ds, qs))
                dv_bands.append(_dot(pb, dos))
            for bands, out_ref, carry_ref in ((dk_bands, dkd_ref, ck_ref), (dv_bands, dvd_ref, cv_ref)):
                carry_ref[0, :, kcols] = bands[0][0:CHUNK]
                for ci in range(n_chunks):
                    own = bands[ci][CHUNK:2 * CHUNK]
                    if ci + 1 < n_chunks:
                        own = own + bands[ci + 1][0:CHUNK]
                    out_ref[ci * CHUNK:(ci + 1) * CHUNK, kcols] = own

        @pl.when(i == n_steps - 1)
        def _():
            gw_ref[...] = gw_acc[...].astype(BF16)

    carry_spec = pl.BlockSpec((1, CHUNK, 2 * PAIR), lambda i: (i, 0, 0))
    carry_shape = jax.ShapeDtypeStruct((n_steps, CHUNK, 2 * PAIR), F32)
    bias_shape = (N_HEADS, 2 * CHUNK, CHUNK)
    return pl.pallas_call(
        body, name="layer_b_bwd_attn", grid=(n_steps,),
        in_specs=[_rows(tm, D_MODEL), _rows(tm, B_WIDTH), _rows(tm, B_WIDTH), _rows(tm, B_WIDTH),
                  _resident(kd.shape), _resident(vd.shape), _probs_spec(tm), _sink_probs_spec(),
                  _resident(w_out.shape)],
        out_specs=[_rows(tm, B_WIDTH), _rows(tm, B_WIDTH), _rows(tm, 2 * PAIR), _rows(tm, 2 * PAIR),
                   carry_spec, carry_spec, _const(w_out.shape), _const((N_KV, GROUP_Q)), _const(bias_shape)],
        out_shape=[jax.ShapeDtypeStruct((t_len, B_WIDTH), BF16), jax.ShapeDtypeStruct((t_len, B_WIDTH), BF16),
                   jax.ShapeDtypeStruct((t_len, 2 * PAIR), F32), jax.ShapeDtypeStruct((t_len, 2 * PAIR), F32),
                   carry_shape, carry_shape, jax.ShapeDtypeStruct(w_out.shape, BF16),
                   jax.ShapeDtypeStruct((N_KV, GROUP_Q), F32), jax.ShapeDtypeStruct(bias_shape, F32)],
        scratch_shapes=[pltpu.VMEM((tm, B_WIDTH), BF16), pltpu.VMEM((GROUP_Q, PAIR), BF16),
                        pltpu.VMEM((GROUP_Q, PAIR), BF16), pltpu.VMEM(w_out.shape, F32)],
        compiler_params=_params(),
    )(dr2, zb, o, q, kd, vd, probs, sink_probs, w_out)


def _layer_b_bwd_proj(xh1, rstd1, g1, b1, dr2, dq, dzb, dkd, dvd, carry_k, carry_v, w_in, w_kv):
    t_len = xh1.shape[0]
    tm = TM_MM
    n_steps = t_len // tm
    per_tile = tm // TM_ATTN
    n_carry = carry_k.shape[0]

    def body(xh_ref, rstd_ref, g_ref, b_ref, dr2_ref, dq_ref, dz_ref, dkd_ref, dvd_ref, *rest):
        carry_refs = rest[:2 * per_tile]
        win_ref, wkv_ref, dr1_ref, dg_ref, db_ref, gwin_ref, gwkv_ref, acc_in, acc_kv = rest[2 * per_tile:]
        i = pl.program_id(0)

        @pl.when(i == 0)
        def _():
            acc_in[...] = jnp.zeros_like(acc_in)
            acc_kv[...] = jnp.zeros_like(acc_kv)
            dg_ref[...] = jnp.zeros_like(dg_ref)
            db_ref[...] = jnp.zeros_like(db_ref)

        lo = lax.broadcasted_iota(jnp.int32, (tm, PAIR), 1) < HEAD_DIM

        def heads_gradient(tile_ref, refs):
            parts = []
            for a in range(per_tile):
                parts.append(tile_ref[a * TM_ATTN:(a + 1) * TM_ATTN - CHUNK, :])
                carry = refs[a][0]
                if a == per_tile - 1:
                    carry = jnp.where(i < n_steps - 1, carry, 0.0)
                parts.append(tile_ref[(a + 1) * TM_ATTN - CHUNK:(a + 1) * TM_ATTN, :] + carry)
            dup = jnp.concatenate(parts, axis=0)
            return jnp.where(lo, _fold_halves(dup[:, 0:PAIR]), _fold_halves(dup[:, PAIR:2 * PAIR]))

        xh = xh_ref[...]
        h1 = (xh * g_ref[...] + b_ref[...]).astype(BF16)
        dq_t = dq_ref[...]
        dz_t = dz_ref[...]
        dkv = jnp.concatenate([heads_gradient(dkd_ref, carry_refs[:per_tile]),
                               heads_gradient(dvd_ref, carry_refs[per_tile:])], axis=1).astype(BF16)
        dh1 = ALPHA * dr2_ref[...]
        dh1 += _dot_nt(dq_t, win_ref[:, 0:B_WIDTH])
        dh1 += _dot_nt(dz_t, win_ref[:, B_WIDTH:2 * B_WIDTH])
        dh1 += _dot_nt(dkv, wkv_ref[...])
        acc_in[:, 0:B_WIDTH] += _dot_tn(h1, dq_t)
        acc_in[:, B_WIDTH:2 * B_WIDTH] += _dot_tn(h1, dz_t)
        acc_kv[...] += _dot_tn(h1, dkv)
        dg_ref[...] += jnp.sum(dh1 * xh, axis=0, keepdims=True)
        db_ref[...] += jnp.sum(dh1, axis=0, keepdims=True)
        dr1_ref[...] = _ln_bwd(dh1 * g_ref[...], xh, rstd_ref[...])

        @pl.when(i == n_steps - 1)
        def _():
            half_rows = D_MODEL // 2
            shard_cols = 2 * B_WIDTH // N_CHIPS
            for s in range(N_CHIPS):
                for c in range(2):
                    gwin_ref[2 * s + c] = acc_in[c * half_rows:(c + 1) * half_rows,
                                                 s * shard_cols:(s + 1) * shard_cols].astype(BF16)
            gwkv_ref[...] = acc_kv[...].astype(BF16)

    vec = jax.ShapeDtypeStruct((1, D_MODEL), F32)
    gwin_shape = (N_DEV, D_MODEL // 2, 2 * B_WIDTH // N_CHIPS)

    def carry_spec(a):
        return pl.BlockSpec((1, CHUNK, 2 * PAIR), lambda i: (jnp.minimum(per_tile * i + a + 1, n_carry - 1), 0, 0))

    carry_specs = [carry_spec(a) for a in range(per_tile)]
    return pl.pallas_call(
        body, name="layer_b_bwd_proj", grid=(n_steps,),
        in_specs=[_rows(tm, D_MODEL), _rows(tm, 1), _const(g1.shape), _const(b1.shape), _rows(tm, D_MODEL),
                  _rows(tm, B_WIDTH), _rows(tm, B_WIDTH), _rows(tm, 2 * PAIR), _rows(tm, 2 * PAIR)]
        + carry_specs + carry_specs + [_resident(w_in.shape), _resident(w_kv.shape)],
        out_specs=[_rows(tm, D_MODEL), _const((1, D_MODEL)), _const((1, D_MODEL)), _const(gwin_shape),
                   _const(w_kv.shape)],
        out_shape=[jax.ShapeDtypeStruct((t_len, D_MODEL), F32), vec, vec,
                   jax.ShapeDtypeStruct(gwin_shape, BF16), jax.ShapeDtypeStruct(w_kv.shape, BF16)],
        scratch_shapes=[pltpu.VMEM(w_in.shape, F32), pltpu.VMEM(w_kv.shape, F32)],
        compiler_params=_params(),
    )(xh1, rstd1, g1, b1, dr2, dq, dzb, dkd, dvd, *([carry_k] * per_tile), *([carry_v] * per_tile), w_in, w_kv)


def _layer_a_bwd_mix(dr1, u, vh, z, y, rv, w_out, lng, lnb, ws, bsp_t, after):
    t_len = u.shape[0]
    tm = TM_ATTN
    n_steps = t_len // tm

    def body(dr_ref, u_ref, vh_ref, z_ref, y_ref, rv_ref, wout_ref, lng_ref, lnb_ref, ws_ref, bsp_ref, after_ref,
             dp_ref, gw_ref, dws_ref, dbsp_ref, dgs_ref, dbs_ref, s_scr, dvn_scr, gw_acc):
        i = pl.program_id(0)

        @pl.when(i == 0)
        def _():
            gw_acc[...] = jnp.zeros_like(gw_acc)
            dws_ref[...] = jnp.zeros_like(dws_ref)
            dbsp_ref[...] = jnp.zeros_like(dbsp_ref)
            dgs_ref[...] = jnp.zeros_like(dgs_ref)
            dbs_ref[...] = jnp.zeros_like(dbs_ref)

        drb = dr_ref[...].astype(BF16)
        dy = _dot_nt(drb, wout_ref[...])
        vh_t = vh_ref[...].astype(F32)
        vn = (vh_t * lng_ref[...] + lnb_ref[...]).astype(BF16)
        _spatial_mix(ws_ref, bsp_ref, vn, s_scr, tm // CHUNK)
        gw_acc[...] += _dot_tn(y_ref[...], drb)
        s = s_scr[...]
        sz, dsz = _silu_parts(z_ref[...].astype(F32))
        t = dy * u_ref[...].astype(F32)
        dp_ref[:, 0:A_WIDTH] = (dy * (s * sz)).astype(BF16)
        dp_ref[:, 2 * A_WIDTH:3 * A_WIDTH] = (t * s * dsz).astype(BF16)
        ds = (t * sz).astype(BF16)

        group_of = lax.broadcasted_iota(jnp.int32, (A_WIDTH, CHUNK), 0) // A_GROUP_DIM
        indicator = (group_of == lax.broadcasted_iota(jnp.int32, (A_WIDTH, CHUNK), 1)).astype(BF16)
        group_sums = _dot(ds, indicator)
        total = group_sums[0:CHUNK]
        for ci in range(1, tm // CHUNK):
            total += group_sums[ci * CHUNK:(ci + 1) * CHUNK]
        dbsp_ref[...] += total

        tri = (lax.broadcasted_iota(jnp.int32, (CHUNK, CHUNK), 0)
               >= lax.broadcasted_iota(jnp.int32, (CHUNK, CHUNK), 1))
        for g in range(A_GROUPS):
            wsg = jnp.where(tri, ws_ref[g], 0.0).astype(BF16)
            cols = slice(g * A_GROUP_DIM, (g + 1) * A_GROUP_DIM)
            dws_g = jnp.zeros((CHUNK, CHUNK), F32)
            for ci in range(tm // CHUNK):
                rows = slice(ci * CHUNK, (ci + 1) * CHUNK)
                ds_b = ds[rows, cols]
                dws_g += _dot_nt(ds_b, vn[rows, cols])
                dvn_scr[rows, cols] = _dot_tn(wsg, ds_b)
            dws_ref[g] += jnp.where(tri, dws_g, 0.0)
        dvn = dvn_scr[...]
        dgs_ref[...] += jnp.sum(dvn * vh_t, axis=0, keepdims=True)
        dbs_ref[...] += jnp.sum(dvn, axis=0, keepdims=True)
        dp_ref[:, A_WIDTH:2 * A_WIDTH] = _ln_bwd(dvn * lng_ref[...], vh_t, rv_ref[...]).astype(BF16)

        @pl.when(i == n_steps - 1)
        def _():
            gw_ref[...] = gw_acc[...].astype(BF16)

    wide = jax.ShapeDtypeStruct((1, A_WIDTH), F32)
    return pl.pallas_call(
        body, name="layer_a_bwd_mix", grid=(n_steps,),
        in_specs=[_rows(tm, D_MODEL), _rows(tm, A_WIDTH), _rows(tm, A_WIDTH), _rows(tm, A_WIDTH), _rows(tm, A_WIDTH),
                  _rows(tm, 1), _resident(w_out.shape), _const(lng.shape), _const(lnb.shape), _const(ws.shape),
                  _const(bsp_t.shape), _const(after.shape)],
        out_specs=[_rows(tm, 3 * A_WIDTH), _const(w_out.shape), _const(ws.shape), _const((CHUNK, CHUNK)),
                   _const((1, A_WIDTH)), _const((1, A_WIDTH))],
        out_shape=[jax.ShapeDtypeStruct((t_len, 3 * A_WIDTH), BF16), jax.ShapeDtypeStruct(w_out.shape, BF16),
                   jax.ShapeDtypeStruct(ws.shape, F32), jax.ShapeDtypeStruct((CHUNK, CHUNK), F32),
                   wide, wide],
        scratch_shapes=[pltpu.VMEM((tm, A_WIDTH), F32), pltpu.VMEM((tm, A_WIDTH), F32),
                        pltpu.VMEM(w_out.shape, F32)],
        compiler_params=_params(),
    )(dr1, u, vh, z, y, rv, w_out, lng, lnb, ws, bsp_t, after)


def _layer_a_bwd_dx(dr1, dp, w_in, after):
    t_len = dr1.shape[0]
    tm = TM_MM

    def body(dr_ref, dp_ref, win_ref, after_ref, dx_ref):
        dx_ref[...] = ALPHA * dr_ref[...] + _dot_nt(dp_ref[...], win_ref[...])

    return pl.pallas_call(
        body, name="layer_a_bwd_dx", grid=(t_len // tm,),
        in_specs=[_rows(tm, D_MODEL), _rows(tm, 3 * A_WIDTH), _resident(w_in.shape), _const(after.shape)],
        out_specs=_rows(tm, D_MODEL),
        out_shape=jax.ShapeDtypeStruct((t_len, D_MODEL), F32),
        compiler_params=_params(),
    )(dr1, dp, w_in, after)


def _layer_a_bwd_win(x, dp, after):
    t_len = x.shape[0]
    tm = TM_WIN
    n_steps = t_len // tm
    shard_cols = 3 * A_WIDTH // N_CHIPS
    half_rows = D_MODEL // 2

    def body(x_ref, dp_ref, after_ref, gw_ref, acc):
        i = pl.program_id(1)

        @pl.when(i == 0)
        def _():
            acc[...] = jnp.zeros_like(acc)

        acc[...] += _dot_tn(x_ref[...].astype(BF16), dp_ref[...])

        @pl.when(i == n_steps - 1)
        def _():
            for c in range(2):
                gw_ref[0, c] = acc[c * half_rows:(c + 1) * half_rows, :].astype(BF16)

    return pl.pallas_call(
        body, name="layer_a_bwd_win", grid=(N_CHIPS, n_steps),
        in_specs=[pl.BlockSpec((tm, D_MODEL), lambda j, i: (i, 0)),
                  pl.BlockSpec((tm, shard_cols), lambda j, i: (i, j)), _const(after.shape)],
        out_specs=pl.BlockSpec((1, 2, half_rows, shard_cols), lambda j, i: (j, 0, 0, 0)),
        out_shape=jax.ShapeDtypeStruct((N_CHIPS, 2, half_rows, shard_cols), BF16),
        scratch_shapes=[pltpu.VMEM((D_MODEL, shard_cols), F32)],
        compiler_params=_params(("arbitrary", "arbitrary")),
    )(x, dp, after)


def _bucket_onehot():
    t = jnp.arange(CHUNK, dtype=jnp.int32)[None, :]
    j = jnp.arange(2 * CHUNK, dtype=jnp.int32)[:, None]
    dist = jnp.clip(t + CHUNK - j, 0, CHUNK - 1)
    max_exact = REL_BUCKETS // 2
    df = jnp.maximum(dist, 1).astype(F32)
    large = max_exact + (jnp.log(df / max_exact) / math.log(CHUNK / max_exact)
                         * (REL_BUCKETS - max_exact)).astype(jnp.int32)
    bucket = jnp.where(dist < max_exact, dist, jnp.minimum(large, REL_BUCKETS - 1))
    onehot = bucket.reshape(1, -1) == jnp.arange(REL_BUCKETS, dtype=jnp.int32)[:, None]
    return onehot.astype(F32)


def _bias_expand(rel_t, onehot):
    def body(rel_ref, oh_ref, out_ref):
        out_ref[...] = jnp.dot(rel_ref[...], oh_ref[...], preferred_element_type=F32,
                               precision=lax.Precision.HIGHEST)

    return pl.pallas_call(
        body, name="bias_expand",
        out_shape=jax.ShapeDtypeStruct((N_HEADS, onehot.shape[1]), F32),
    )(rel_t, onehot)


def _bias_reduce(onehot, dbias):
    def body(oh_ref, db_ref, out_ref):
        out_ref[...] = lax.dot_general(oh_ref[...], db_ref[...], (((1,), (1,)), ((), ())),
                                       preferred_element_type=F32, precision=lax.Precision.HIGHEST)

    return pl.pallas_call(
        body, name="bias_reduce",
        out_shape=jax.ShapeDtypeStruct((REL_BUCKETS, N_HEADS), F32),
    )(onehot, dbias)


def _place():
    return lax.axis_index("x"), lax.axis_index("y"), lax.axis_index("c")


def _shard_window(full_ref, shard_shape, col_sharded, s, half):
    rows, cols = shard_shape
    if col_sharded:
        rsel = pl.ds(0, rows) if half is None else pl.ds(half * (rows // 2), rows // 2)
        return full_ref.at[rsel, pl.ds(s * cols, cols)]
    if half is None:
        return full_ref.at[pl.ds(s * rows, rows), :]
    return full_ref.at[pl.ds(s * rows + half * (rows // 2), rows // 2), :]


def _other_chips(x, y):
    return [(1 - x, y), (x, 1 - y), (1 - x, 1 - y)]


def _gather_weights(shards, col_sharded, fetch, ln_shard):
    n_w = len(shards)
    fetched = [w for w in range(n_w) if fetch[w]]
    full_shapes = []
    for w, cs in zip(shards, col_sharded):
        r, c = w.shape
        full_shapes.append((r, c * N_CHIPS) if cs else (r * N_CHIPS, c))

    def body(*refs):
        in_refs = refs[:n_w]
        ln_ref = refs[n_w]
        full_refs = refs[n_w + 1:2 * n_w + 1]
        ln_full = refs[2 * n_w + 1]
        stage = refs[2 * n_w + 2:3 * n_w + 2]
        send_sems, recv_sems, local_sems, ln_send, ln_recv = refs[3 * n_w + 2:]
        x, y, c = _place()
        s_me = 2 * x + y
        chips = _other_chips(x, y)

        def shard_window(w, s, half):
            return _shard_window(full_refs[w], shards[w].shape, col_sharded[w], s, half)

        def stage_half(w, half):
            rows = shards[w].shape[0]
            return stage[w].at[pl.ds(half * (rows // 2), rows // 2), :]

        def ici_copy(w, k, sender_shard, src):
            return pltpu.make_async_remote_copy(
                src_ref=src, dst_ref=shard_window(w, sender_shard, c),
                send_sem=send_sems.at[w * 3 + k], recv_sem=recv_sems.at[w * 3 + k],
                device_id=(*chips[k], c), device_id_type=MESH)

        def d2d_copy(w, k, half):
            s_k = 2 * chips[k][0] + chips[k][1]
            win = shard_window(w, s_k, half)
            return pltpu.make_async_remote_copy(
                src_ref=win, dst_ref=win,
                send_sem=send_sems.at[3 * n_w + w * 3 + k], recv_sem=recv_sems.at[3 * n_w + w * 3 + k],
                device_id=(x, y, 1 - c), device_id_type=MESH)

        def ln_copy(k, slot):
            return pltpu.make_async_remote_copy(
                src_ref=ln_ref, dst_ref=ln_full.at[slot], send_sem=ln_send.at[k], recv_sem=ln_recv.at[k],
                device_id=(*chips[k], c), device_id_type=MESH)

        for w in range(n_w):
            stage[w][...] = in_refs[w][...].astype(BF16)
        own = [pltpu.make_async_copy(stage[w], shard_window(w, s_me, None), local_sems.at[w]) for w in range(n_w)]
        for cp in own:
            cp.start()
        ln_full[s_me] = ln_ref[...]
        first = [ici_copy(w, k, s_me, stage_half(w, c)) for w in fetched for k in range(3)]
        first += [ln_copy(k, s_me) for k in range(3)]
        for cp in first:
            cp.start()
        passed = []
        for w in fetched:
            for k in range(3):
                s_k = 2 * chips[k][0] + chips[k][1]
                ici_copy(w, k, s_k, stage_half(w, c)).wait_recv()
                fwd = d2d_copy(w, k, c)
                fwd.start()
                passed.append(fwd)
        for w in fetched:
            for k in range(3):
                d2d_copy(w, k, 1 - c).wait_recv()
        for k in range(3):
            ln_copy(k, 2 * chips[k][0] + chips[k][1]).wait_recv()
        for cp in first + passed:
            cp.wait_send()
        for cp in own:
            cp.wait()

    vmem = pl.BlockSpec(memory_space=pltpu.VMEM)
    hbm = pl.BlockSpec(memory_space=pl.ANY)
    return pl.pallas_call(
        body, name="gather_weights",
        in_specs=[vmem] * (n_w + 1),
        out_specs=[hbm] * n_w + [vmem],
        out_shape=[jax.ShapeDtypeStruct(s, BF16) for s in full_shapes]
        + [jax.ShapeDtypeStruct((N_CHIPS,) + ln_shard.shape, F32)],
        scratch_shapes=[pltpu.VMEM(w.shape, BF16) for w in shards]
        + [pltpu.SemaphoreType.DMA((6 * n_w,)), pltpu.SemaphoreType.DMA((6 * n_w,)),
           pltpu.SemaphoreType.DMA((n_w,)), pltpu.SemaphoreType.DMA((3,)), pltpu.SemaphoreType.DMA((3,))],
        compiler_params=pltpu.CompilerParams(vmem_limit_bytes=VMEM_LIMIT),
    )(*shards, ln_shard)


def _fetch_copy(full_ref, shard_shape, col_sharded, sender_shard, send_sems, recv_sems, idx, chip, c):
    win = _shard_window(full_ref, shard_shape, col_sharded, sender_shard, None)
    return pltpu.make_async_remote_copy(src_ref=win, dst_ref=win, send_sem=send_sems.at[idx],
                                        recv_sem=recv_sems.at[idx], device_id=(*chip, c), device_id_type=MESH)


def _fetch_start(fulls, shard_shapes, col_sharded):
    n = len(fulls)

    def body(*refs):
        full = refs[:n]
        send_sems, recv_sems = refs[n], refs[n + 1]
        token = refs[-1]
        x, y, c = _place()
        for w in range(n):
            for k, chip in enumerate(_other_chips(x, y)):
                _fetch_copy(full[w], shard_shapes[w], col_sharded[w], 2 * x + y, send_sems, recv_sems, w * 3 + k,
                            chip, c).start()
        token[...] = jnp.zeros_like(token)

    outs = pl.pallas_call(
        body, name="fetch_start",
        out_shape=(pltpu.SemaphoreType.DMA((3 * n,)), pltpu.SemaphoreType.DMA((3 * n,)),
                   *[pltpu.HBM(f.shape, f.dtype) for f in fulls], jax.ShapeDtypeStruct((8, 128), F32)),
        in_specs=[_HBM] * n,
        out_specs=(_SEM, _SEM, *([_HBM] * n), pl.BlockSpec(memory_space=pltpu.VMEM)),
        input_output_aliases={i: 2 + i for i in range(n)},
        compiler_params=pltpu.CompilerParams(has_side_effects=pltpu.SideEffectType.DATAFLOW_SIDE_EFFECTING),
    )(*[pltpu.with_memory_space_constraint(f, pltpu.HBM) for f in fulls])
    return dict(send=outs[0], recv=outs[1], full=list(outs[2:2 + n])), outs[-1]


def _fetch_wait(group, shard_shapes, col_sharded, after):
    n = len(group["full"])

    def body(*refs):
        full = refs[:n]
        send_sems, recv_sems = refs[n], refs[n + 1]
        x, y, c = _place()
        for w in range(n):
            for k, chip in enumerate(_other_chips(x, y)):
                _fetch_copy(full[w], shard_shapes[w], col_sharded[w], 2 * x + y, send_sems, recv_sems, w * 3 + k,
                            chip, c).wait_send()
                _fetch_copy(full[w], shard_shapes[w], col_sharded[w], 2 * chip[0] + chip[1], send_sems, recv_sems,
                            w * 3 + k, chip, c).wait_recv()

    outs = pl.pallas_call(
        body, name="fetch_wait", out_shape=tuple(pltpu.HBM(f.shape, f.dtype) for f in group["full"]),
        in_specs=[_HBM] * n + [_SEM, _SEM, pl.BlockSpec(memory_space=pl.ANY)],
        out_specs=tuple([_HBM] * n), input_output_aliases={i: i for i in range(n)},
        compiler_params=pltpu.CompilerParams(has_side_effects=pltpu.SideEffectType.DATAFLOW_SIDE_EFFECTING),
    )(*group["full"], group["send"], group["recv"], after)
    return list(outs)


_HBM = pl.BlockSpec(memory_space=pltpu.HBM)
_SEM = pl.BlockSpec(memory_space=pltpu.SEMAPHORE)
_N_PEER = N_DEV - 1


def _peer(x, y, c, k):
    return (x + (k >> 2)) % 2, (y + ((k >> 1) & 1)) % 2, (c + (k & 1)) % 2


def _exchange_copy(src_ref, land_ref, sliced, send_sems, recv_sems, idx, x, y, c, k):
    px, py, pc = _peer(x, y, c, k)
    src = src_ref.at[4 * px + 2 * py + pc] if sliced else src_ref
    return pltpu.make_async_remote_copy(
        src_ref=src, dst_ref=land_ref.at[4 * x + 2 * y + c],
        send_sem=send_sems.at[idx], recv_sem=recv_sems.at[idx], device_id=(px, py, pc), device_id_type=MESH)


def _exchange_start(tag, arrays, sliced):
    n = len(arrays)
    lands = [lax.empty(a.shape if s else (N_DEV,) + a.shape, a.dtype) for a, s in zip(arrays, sliced)]

    def body(*refs):
        src, land = refs[:n], refs[n:2 * n]
        send_sems, recv_sems = refs[2 * n], refs[2 * n + 1]
        token = refs[-1]
        x, y, c = _place()
        for w in range(n):
            for k in range(1, N_DEV):
                _exchange_copy(src[w], land[w], sliced[w], send_sems, recv_sems, w * _N_PEER + k - 1, x, y, c, k).start()
        token[...] = jnp.zeros_like(token)

    outs = pl.pallas_call(
        body, name="exchange_start_" + tag,
        out_shape=(pltpu.SemaphoreType.DMA((n * _N_PEER,)), pltpu.SemaphoreType.DMA((n * _N_PEER,)),
                   *[pltpu.HBM(a.shape, a.dtype) for a in arrays], *[pltpu.HBM(l.shape, l.dtype) for l in lands],
                   jax.ShapeDtypeStruct((8, 128), F32)),
        in_specs=[_HBM] * (2 * n),
        out_specs=(_SEM, _SEM, *([_HBM] * (2 * n)), pl.BlockSpec(memory_space=pltpu.VMEM)),
        input_output_aliases={i: 2 + i for i in range(2 * n)},
        compiler_params=pltpu.CompilerParams(has_side_effects=pltpu.SideEffectType.DATAFLOW_SIDE_EFFECTING),
    )(*[pltpu.with_memory_space_constraint(a, pltpu.HBM) for a in arrays],
      *[pltpu.with_memory_space_constraint(l, pltpu.HBM) for l in lands])
    return dict(send=outs[0], recv=outs[1], src=list(outs[2:2 + n]), land=list(outs[2 + n:2 + 2 * n]),
                sliced=list(sliced)), outs[-1]


def _exchange_wait(tag, groups, after):
    counts = [len(g["src"]) for g in groups]
    total = sum(counts)

    def body(*refs):
        pos = 0
        x, y, c = _place()
        for g, n in zip(groups, counts):
            src, land = refs[pos:pos + n], refs[pos + n:pos + 2 * n]
            send_sems, recv_sems = refs[pos + 2 * n], refs[pos + 2 * n + 1]
            pos += 2 * n + 2
            for w in range(n):
                for k in range(1, N_DEV):
                    cp = _exchange_copy(src[w], land[w], g["sliced"][w], send_sems, recv_sems,
                                        w * _N_PEER + k - 1, x, y, c, k)
                    cp.wait_send()
                    cp.wait_recv()

    operands, in_specs, aliases, out_shape = [], [], {}, []
    for g in groups:
        for a in g["src"] + g["land"]:
            aliases[len(operands)] = len(out_shape)
            out_shape.append(pltpu.HBM(a.shape, a.dtype))
            operands.append(a)
            in_specs.append(_HBM)
        operands += [g["send"], g["recv"]]
        in_specs += [_SEM, _SEM]
    operands.append(after)
    in_specs.append(pl.BlockSpec(memory_space=pl.ANY))
    outs = pl.pallas_call(
        body, name="exchange_wait_" + tag, out_shape=tuple(out_shape), in_specs=in_specs,
        out_specs=tuple([_HBM] * (2 * total)), input_output_aliases=aliases,
        compiler_params=pltpu.CompilerParams(has_side_effects=pltpu.SideEffectType.DATAFLOW_SIDE_EFFECTING),
    )(*operands)
    srcs, lands, pos = [], [], 0
    for n in counts:
        srcs += list(outs[pos:pos + n])
        lands += list(outs[pos + n:pos + 2 * n])
        pos += 2 * n
    return srcs, lands


def _sum_and_swap(tag, pieces, lands, small=None, small_land=None):
    n_w = len(pieces)
    n_small = 0 if small is None else 1

    def body(*refs):
        g_refs, land_refs = refs[:n_w], refs[n_w:2 * n_w]
        pos = 2 * n_w + 2 * n_small
        out_refs = refs[pos:pos + n_w]
        pos += n_w + n_small
        bufs = refs[pos:pos + n_w]
        load_sems, swap_send, swap_recv = refs[pos + n_w + 2 * n_small:]
        x, y, c = _place()
        me = 4 * x + 2 * y + c

        def slot(k):
            px, py, pc = _peer(x, y, c, k)
            return 4 * px + 2 * py + pc

        def swap_copy(w, half):
            rows = pieces[w].shape[1]
            win = out_refs[w].at[pl.ds(pl.multiple_of(half * rows, rows), rows), :]
            return pltpu.make_async_remote_copy(
                src_ref=win, dst_ref=win, send_sem=swap_send.at[w], recv_sem=swap_recv.at[w],
                device_id=(x, y, 1 - c), device_id_type=MESH)

        loads = []
        for w in range(n_w):
            per_w = [pltpu.make_async_copy(g_refs[w].at[me], bufs[w].at[me], load_sems.at[w * N_DEV])]
            per_w += [pltpu.make_async_copy(land_refs[w].at[slot(k)], bufs[w].at[slot(k)], load_sems.at[w * N_DEV + k])
                      for k in range(1, N_DEV)]
            loads.append(per_w)
        small_loads = []
        if n_small:
            small_ref, small_land_ref = refs[2 * n_w], refs[2 * n_w + 1]
            small_out = refs[2 * n_w + 2 + n_w]
            small_buf, small_sems = refs[pos + n_w], refs[pos + n_w + 1]
            small_loads = [pltpu.make_async_copy(small_land_ref.at[slot(k)], small_buf.at[slot(k)],
                                                 small_sems.at[k - 1]) for k in range(1, N_DEV)]
        for cp in [cp for per_w in loads for cp in per_w] + small_loads:
            cp.start()
        if n_small:
            small_buf[me] = small_ref[...]
        swaps = []
        for w in range(n_w):
            for cp in loads[w]:
                cp.wait()
            rows = pieces[w].shape[1]
            total = bufs[w][0].astype(F32)
            for p in range(1, N_DEV):
                total += bufs[w][p].astype(F32)
            out_refs[w][pl.ds(pl.multiple_of(c * rows, rows), rows), :] = total
            sw = swap_copy(w, c)
            sw.start()
            swaps.append(sw)
        if n_small:
            for cp in small_loads:
                cp.wait()
            total = small_buf[0]
            for p in range(1, N_DEV):
                total += small_buf[p]
            small_out[...] = total
        for w in range(n_w):
            swap_copy(w, 1 - c).wait_recv()
        for sw in swaps:
            sw.wait_send()

    vmem = pl.BlockSpec(memory_space=pltpu.VMEM)
    hbm = pl.BlockSpec(memory_space=pl.ANY)
    small_args = [small, small_land] if n_small else []
    small_shapes = [jax.ShapeDtypeStruct(small.shape, F32)] if n_small else []
    small_scratch = ([pltpu.VMEM((N_DEV,) + small.shape, F32), pltpu.SemaphoreType.DMA((_N_PEER,))]
                     if n_small else [])
    return pl.pallas_call(
        body, name="sum_and_swap_" + tag,
        in_specs=[hbm] * (2 * n_w) + [vmem, hbm] * n_small,
        out_specs=[vmem] * (n_w + n_small),
        out_shape=[jax.ShapeDtypeStruct((2 * p.shape[1], p.shape[2]), F32) for p in pieces] + small_shapes,
        scratch_shapes=[pltpu.VMEM(p.shape, BF16) for p in pieces] + small_scratch
        + [pltpu.SemaphoreType.DMA((n_w * N_DEV,)), pltpu.SemaphoreType.DMA((n_w,)),
           pltpu.SemaphoreType.DMA((n_w,))],
        compiler_params=pltpu.CompilerParams(vmem_limit_bytes=VMEM_LIMIT),
    )(*pieces, *lands, *small_args)


def _adamw_update(w_ref, g_ref, m_ref, v_ref, d_ref, nm_ref, nv_ref):
    c1 = 1.0 - ADAM_B1 ** ADAM_STEP
    c2 = 1.0 - ADAM_B2 ** ADAM_STEP
    g_t = g_ref[...]
    nm = ADAM_B1 * m_ref[...] + (1.0 - ADAM_B1) * g_t
    nv = ADAM_B2 * v_ref[...] + (1.0 - ADAM_B2) * (g_t * g_t)
    d_ref[...] = -ADAM_LR * ((nm / c1) / (jnp.sqrt(nv / c2) + ADAM_EPS) + ADAM_WD * w_ref[...])
    nm_ref[...] = nm
    nv_ref[...] = nv


def _adamw_small(items):
    n = len(items)
    flat, shapes = [], []
    for w, g, m, v in items:
        cols = w.shape[-1]
        shapes.append(w.shape)
        flat += [a.reshape(w.size // cols, cols) for a in (w, g, m, v)]

    def body(*refs):
        for k in range(n):
            _adamw_update(*refs[4 * k:4 * k + 4], *refs[4 * n + 3 * k:4 * n + 3 * k + 3])

    outs = pl.pallas_call(
        body, name="adamw_small",
        out_shape=[jax.ShapeDtypeStruct(flat[4 * k].shape, F32) for k in range(n) for _ in range(3)],
    )(*flat)
    return [tuple(o.reshape(shapes[k]) for o in outs[3 * k:3 * k + 3]) for k in range(n)]


def _adamw(label, w, g, m, v):
    shape = w.shape
    cols = shape[-1]
    rows = w.size // cols
    args = [a.reshape(rows, cols) for a in (w, g, m, v)]

    def body(*refs):
        _adamw_update(*refs)

    block_rows = 256 if rows % 256 == 0 and rows > 256 else rows
    spec = pl.BlockSpec((block_rows, cols), lambda i: (i, 0))
    outs = pl.pallas_call(
        body, name="adamw_" + label, grid=(rows // block_rows,),
        in_specs=[spec] * 4, out_specs=[spec] * 3,
        out_shape=[jax.ShapeDtypeStruct((rows, cols), F32)] * 3,
        compiler_params=_params(),
    )(*args)
    return [o.reshape(shape) for o in outs]


def _no_send(tag, arrays, sliced):
    return jnp.zeros((8, 128), F32)


def _local_step(x, tgt, in_proj, later_weights, sgu_ln_g, sgu_ln_b, w_spatial, b_spatial,
                attn_sinks, rel_bias, post_ln_g, post_ln_b, send=_no_send):
    bsp_t = b_spatial.T
    g1, b1 = post_ln_g[0:1], post_ln_b[0:1]
    g2, b2 = post_ln_g[1:2], post_ln_b[1:2]
    onehot = _bucket_onehot()
    bias = _bias_expand(rel_bias.T, onehot).reshape(N_HEADS, 2 * CHUNK, CHUNK)
    win = _window_tables()

    p, w_in_a = in_proj(x)
    u, vh, z, rv, y = _layer_a_fwd(p, sgu_ln_g, sgu_ln_b, w_spatial, bsp_t)
    w_out_a, w_kv, w_in_b, w_out_b = later_weights(y)
    xh1, rstd1, q, zb, kd, vd = _layer_b_proj(x, y, w_out_a, g1, b1, w_in_b, w_kv)
    o, probs, sink_probs, dr2, loss_vec, dg2, db2 = _layer_b_fwd(q, zb, kd, vd, bias, win, attn_sinks, xh1, g1, b1,
                                                                 w_out_b, g2, b2, tgt)
    dq, dzb, dkd, dvd, carry_k, carry_v, gw_out_b, dsink, dbias = _layer_b_bwd_attn(
        dr2, zb, o, q, kd, vd, probs, sink_probs, w_out_b)
    dr1, dg1, db1, gw_in_b, gw_kv = _layer_b_bwd_proj(xh1, rstd1, g1, b1, dr2, dq, dzb, dkd, dvd, carry_k, carry_v,
                                                      w_in_b, w_kv)
    gw_out_b = gw_out_b.reshape(N_DEV, -1, D_MODEL)
    gw_kv = gw_kv.reshape(N_DEV, -1, 2 * PAIR)
    after = send("b", [gw_out_b, gw_in_b, gw_kv], [True, True, True])
    dp, gw_out_a, dws, dbsp, dgs, dbs = _layer_a_bwd_mix(dr1, u, vh, z, y, rv, w_out_a, sgu_ln_g, sgu_ln_b,
                                                         w_spatial, bsp_t, after)
    gw_out_a = gw_out_a.reshape(N_DEV, -1, D_MODEL)
    drel = _bias_reduce(onehot, dbias.reshape(N_HEADS, -1))
    dsink = jnp.sum(dsink.reshape(N_HEADS, CHUNK), axis=1).reshape(1, N_HEADS)
    loss = ((0.5 / D_MODEL) * jnp.sum(loss_vec)).reshape(1, 1)
    small = dict(w_spatial=dws, b_spatial=dbsp[:, 0:A_GROUPS].T, attn_sinks=dsink,
                 rel_bias=drel, post_ln_g=jnp.concatenate([dg1, dg2], axis=0),
                 post_ln_b=jnp.concatenate([db1, db2], axis=0), sgu_ln_g=dgs, sgu_ln_b=dbs, loss=loss)
    after = send("a_out", [gw_out_a, _pack_small(small)], [True, False])
    gw_in_a = _layer_a_bwd_win(x, dp, after).reshape(N_DEV, D_MODEL // 2, -1)
    after = send("a_in", [gw_in_a], [True])
    grad_x = _layer_a_bwd_dx(dr1, dp, w_in_a, after)

    pieces = [gw_in_a, gw_out_a, gw_kv, gw_in_b, gw_out_b]
    return grad_x, pieces, small


_SMALL_SHAPES = dict(w_spatial=(A_GROUPS, CHUNK, CHUNK), b_spatial=(A_GROUPS, CHUNK), attn_sinks=(1, N_HEADS),
                     rel_bias=(REL_BUCKETS, N_HEADS), post_ln_g=(2, D_MODEL), post_ln_b=(2, D_MODEL),
                     sgu_ln_g=(1, A_WIDTH), sgu_ln_b=(1, A_WIDTH), loss=(1, 1))
_SMALL_ORDER = tuple(_SMALL_SHAPES)


def _small_rows(name):
    size = math.prod(_SMALL_SHAPES[name])
    return -(-size // (_LANES * _SUBLANES)) * _SUBLANES


def _pack_small(small):
    parts = []
    for name in _SMALL_ORDER:
        assert small[name].shape == _SMALL_SHAPES[name], (name, small[name].shape)
        flat = small[name].reshape(-1)
        flat = jnp.pad(flat, (0, _small_rows(name) * _LANES - flat.size))
        parts.append(flat.reshape(-1, _LANES))
    return jnp.concatenate(parts, axis=0)


def _unpack_small(packed):
    out, row = {}, 0
    for name in _SMALL_ORDER:
        shape = _SMALL_SHAPES[name]
        out[name] = packed[row:row + _small_rows(name)].reshape(-1)[:math.prod(shape)].reshape(shape)
        row += _small_rows(name)
    return out


def kernel(x, w_in_a, sgu_ln_g, sgu_ln_b, w_spatial, b_spatial, w_out_a, w_kv, w_in_b, attn_sinks, rel_bias, w_out_b, post_ln_g, post_ln_b, loss_target, m_w_in_a, m_sgu_ln_g, m_sgu_ln_b, m_w_spatial, m_b_spatial, m_w_out_a, m_w_kv, m_w_in_b, m_attn_sinks, m_rel_bias, m_w_out_b, m_post_ln_g, m_post_ln_b, v_w_in_a, v_sgu_ln_g, v_sgu_ln_b, v_w_spatial, v_b_spatial, v_w_out_a, v_w_kv, v_w_in_b, v_attn_sinks, v_rel_bias, v_w_out_b, v_post_ln_g, v_post_ln_b):
    weights = dict(w_in_a=w_in_a, sgu_ln_g=sgu_ln_g, sgu_ln_b=sgu_ln_b, w_spatial=w_spatial, b_spatial=b_spatial,
                   w_out_a=w_out_a, w_kv=w_kv, w_in_b=w_in_b, attn_sinks=attn_sinks, rel_bias=rel_bias,
                   w_out_b=w_out_b, post_ln_g=post_ln_g, post_ln_b=post_ln_b)
    moments_m = dict(w_in_a=m_w_in_a, sgu_ln_g=m_sgu_ln_g, sgu_ln_b=m_sgu_ln_b, w_spatial=m_w_spatial,
                     b_spatial=m_b_spatial, w_out_a=m_w_out_a, w_kv=m_w_kv, w_in_b=m_w_in_b,
                     attn_sinks=m_attn_sinks, rel_bias=m_rel_bias, w_out_b=m_w_out_b, post_ln_g=m_post_ln_g,
                     post_ln_b=m_post_ln_b)
    moments_v = dict(w_in_a=v_w_in_a, sgu_ln_g=v_sgu_ln_g, sgu_ln_b=v_sgu_ln_b, w_spatial=v_w_spatial,
                     b_spatial=v_b_spatial, w_out_a=v_w_out_a, w_kv=v_w_kv, w_in_b=v_w_in_b,
                     attn_sinks=v_attn_sinks, rel_bias=v_rel_bias, w_out_b=v_w_out_b, post_ln_g=v_post_ln_g,
                     post_ln_b=v_post_ln_b)
    order = ("w_in_a", "sgu_ln_g", "sgu_ln_b", "w_spatial", "b_spatial", "w_out_a", "w_kv", "w_in_b", "attn_sinks",
             "rel_bias", "w_out_b", "post_ln_g", "post_ln_b")

    shard_index = 2 * lax.axis_index("x") + lax.axis_index("y")
    ln_shard = jnp.concatenate([sgu_ln_g, sgu_ln_b], axis=0)
    shards = [w_out_a[0], w_kv, w_in_b[0], w_out_b[0]]
    col_sharded = [False, False, True, False]
    *later, ln_full = _gather_weights(shards, col_sharded, [False] * len(shards), ln_shard)
    ln_full = jnp.transpose(ln_full, (1, 0, 2)).reshape(2, A_WIDTH)
    later_shapes = [s.shape for s in shards]
    fetch_group, fetch_token = _fetch_start(later, later_shapes, col_sharded)

    def in_proj(tokens):
        return _layer_a_in_proj(tokens, w_in_a[0], fetch_token)

    def later_weights(y):
        return _fetch_wait(fetch_group, later_shapes, col_sharded, y)

    groups, grads, deltas, new_m, new_v, scalars = {}, {}, {}, {}, {}, {}

    def apply_large(names, reduced):
        for name, g in zip(names, reduced):
            grads[name] = g.reshape(weights[name].shape)
            deltas[name], new_m[name], new_v[name] = _adamw(name, weights[name], grads[name], moments_m[name],
                                                            moments_v[name])

    def send(tag, arrays, sliced):
        groups[tag], token = _exchange_start(tag, arrays, sliced)
        if tag != "a_in":
            return token
        srcs, lands = _exchange_wait("early", [groups["b"], groups["a_out"]], token)
        *reduced, packed_sum = _sum_and_swap("early", srcs[:4], lands[:4], srcs[4], lands[4])
        apply_large(("w_out_b", "w_in_b", "w_kv", "w_out_a"), reduced)
        small_sum = _unpack_small(packed_sum)
        scalars["loss"] = small_sum["loss"].reshape(())
        shard_cols = sgu_ln_g.shape[1]
        for name in ("sgu_ln_g", "sgu_ln_b"):
            grads[name] = lax.dynamic_slice(small_sum[name], (0, shard_index * shard_cols), (1, shard_cols))
        small_names = ("sgu_ln_g", "sgu_ln_b", "w_spatial", "b_spatial", "attn_sinks", "rel_bias", "post_ln_g",
                       "post_ln_b")
        for name in small_names[2:]:
            grads[name] = small_sum[name].reshape(weights[name].shape)
        updates = _adamw_small([(weights[n], grads[n], moments_m[n], moments_v[n]) for n in small_names])
        for name, (d, nm, nv) in zip(small_names, updates):
            deltas[name], new_m[name], new_v[name] = d, nm, nv
        return packed_sum[0:_SUBLANES]

    grad_x, _, _ = _local_step(
        x[0], loss_target[0], in_proj, later_weights, ln_full[0:1], ln_full[1:2], w_spatial[0],
        b_spatial[0], attn_sinks, rel_bias, post_ln_g, post_ln_b, send=send)

    srcs, lands = _exchange_wait("late", [groups["a_in"]], grad_x)
    apply_large(("w_in_a",), _sum_and_swap("late", srcs, lands))
    return (scalars["loss"], grad_x[None], *[grads[n] for n in order], *[deltas[n] for n in order],
            *[new_m[n] for n in order], *[new_v[n] for n in order])
```

```python
import math

import jax
import jax.numpy as jnp
from jax import lax
from jax.experimental import pallas as pl
from jax.experimental.pallas import tpu as pltpu

F32 = jnp.float32
BF16 = jnp.bfloat16

D_MODEL = 1024
A_WIDTH = 2048
A_GROUPS = 8
A_GROUP_DIM = 256
CHUNK = 128
N_HEADS = 16
N_KV = 2
HEAD_DIM = 64
PAIR = 2 * HEAD_DIM
B_WIDTH = 1024
REL_BUCKETS = 32
ALPHA = 4.0 ** 0.25
LN_EPS = 1e-5
NEG_INF = -1e30
SCALE = HEAD_DIM ** -0.5

ADAM_LR = 0.001
ADAM_B1 = 0.9
ADAM_B2 = 0.999
ADAM_EPS = 1e-08
ADAM_WD = 0.01
ADAM_STEP = 10

N_DEV = 8
N_CHIPS = 4
MESH = pl.DeviceIdType.MESH
VMEM_LIMIT = 56 * 1024 * 1024

TM_ATTN = 256
TM_MM = 512
TM_WIN = 1024
_LANES = 128
_SUBLANES = 8


def _dot(a, b):
    return jnp.dot(a, b, preferred_element_type=F32)


def _dot_nt(a, b):
    return lax.dot_general(a, b, (((1,), (1,)), ((), ())), preferred_element_type=F32)


def _dot_tn(a, b):
    return lax.dot_general(a, b, (((0,), (0,)), ((), ())), preferred_element_type=F32)


def _ln_fwd(r):
    mu = jnp.mean(r, axis=-1, keepdims=True)
    rc = r - mu
    var = jnp.mean(rc * rc, axis=-1, keepdims=True)
    rstd = lax.rsqrt(var + LN_EPS)
    return rc * rstd, rstd


def _ln_bwd(dxh, xh, rstd):
    m1 = jnp.mean(dxh, axis=-1, keepdims=True)
    m2 = jnp.mean(dxh * xh, axis=-1, keepdims=True)
    return rstd * (dxh - m1 - xh * m2)


def _silu_parts(z):
    sg = jax.nn.sigmoid(z)
    return z * sg, sg * (1.0 + z * (1.0 - sg))


def _dup_halves(blk):
    sw = pltpu.roll(blk, HEAD_DIM, 1)
    lo = lax.broadcasted_iota(jnp.int32, blk.shape, 1) < HEAD_DIM
    return jnp.where(lo, blk, sw), jnp.where(lo, sw, blk)


def _fold_halves(blk):
    return blk + pltpu.roll(blk, HEAD_DIM, 1)


def _resident(shape):
    nd = len(shape)
    return pl.BlockSpec(shape, lambda *_: (0,) * nd, pipeline_mode=pl.Buffered(1))


def _const(shape):
    nd = len(shape)
    return pl.BlockSpec(shape, lambda *_: (0,) * nd)


def _rows(tm, cols):
    return pl.BlockSpec((tm, cols), lambda i: (i, 0))


def _params(sem=("arbitrary",)):
    return pltpu.CompilerParams(dimension_semantics=sem, vmem_limit_bytes=VMEM_LIMIT)


def _spatial_mix(ws_ref, bsp_ref, vn, s_scr, n_chunks):
    tri = (lax.broadcasted_iota(jnp.int32, (CHUNK, CHUNK), 0)
           >= lax.broadcasted_iota(jnp.int32, (CHUNK, CHUNK), 1))
    for g in range(A_GROUPS):
        wsg = jnp.where(tri, ws_ref[g], 0.0).astype(BF16)
        cols = slice(g * A_GROUP_DIM, (g + 1) * A_GROUP_DIM)
        for ci in range(n_chunks):
            rows = slice(ci * CHUNK, (ci + 1) * CHUNK)
            s_scr[rows, cols] = _dot(wsg, vn[rows, cols]) + bsp_ref[:, g:g + 1]


def _layer_a_fwd(x, w_in, lng, lnb, ws, bsp_t, after):
    t_len = x.shape[0]
    tm = TM_ATTN

    def body(x_ref, win_ref, lng_ref, lnb_ref, ws_ref, bsp_ref, after_ref,
             xt_ref, u_ref, vh_ref, z_ref, rv_ref, y_ref, s_scr):
        x_t = x_ref[...]
        xb = x_t.astype(BF16)
        xt_ref[...] = x_t.T.astype(BF16)
        u = _dot(xb, win_ref[:, 0:A_WIDTH])
        v = _dot(xb, win_ref[:, A_WIDTH:2 * A_WIDTH])
        z = _dot(xb, win_ref[:, 2 * A_WIDTH:3 * A_WIDTH])
        vh, rv = _ln_fwd(v)
        vn = (vh * lng_ref[...] + lnb_ref[...]).astype(BF16)
        _spatial_mix(ws_ref, bsp_ref, vn, s_scr, tm // CHUNK)
        sz, _ = _silu_parts(z)
        y_ref[...] = (u * s_scr[...] * sz).astype(BF16)
        u_ref[...] = u.astype(BF16)
        vh_ref[...] = vh.astype(BF16)
        z_ref[...] = z.astype(BF16)
        rv_ref[...] = rv

    wide = jax.ShapeDtypeStruct((t_len, A_WIDTH), BF16)
    return pl.pallas_call(
        body, name="layer_a_fwd", grid=(t_len // tm,),
        in_specs=[_rows(tm, D_MODEL), _resident(w_in.shape), _const(lng.shape), _const(lnb.shape), _const(ws.shape),
                  _const(bsp_t.shape), _const(after.shape)],
        out_specs=[pl.BlockSpec((D_MODEL, tm), lambda i: (0, i)), _rows(tm, A_WIDTH), _rows(tm, A_WIDTH),
                   _rows(tm, A_WIDTH), _rows(tm, 1), _rows(tm, A_WIDTH)],
        out_shape=[jax.ShapeDtypeStruct((D_MODEL, t_len), BF16), wide, wide, wide,
                   jax.ShapeDtypeStruct((t_len, 1), F32), wide],
        scratch_shapes=[pltpu.VMEM((tm, A_WIDTH), F32)],
        compiler_params=_params(),
    )(x, w_in, lng, lnb, ws, bsp_t, after)


def _layer_b_proj(x, y, w_out_a, g1, b1, w_in, w_kv):
    t_len = x.shape[0]
    tm = TM_MM

    def body(x_ref, y_ref, wout_ref, g_ref, b_ref, win_ref, wkv_ref, xh_ref, r1_ref, q_ref, z_ref, kd_ref, vd_ref):
        xh, r1 = _ln_fwd(ALPHA * x_ref[...] + _dot(y_ref[...], wout_ref[...]))
        xh_ref[...] = xh
        r1_ref[...] = r1
        h1 = (xh * g_ref[...] + b_ref[...]).astype(BF16)
        q_ref[...] = (_dot(h1, win_ref[:, 0:B_WIDTH]) * SCALE).astype(BF16)
        z_ref[...] = _dot(h1, win_ref[:, B_WIDTH:2 * B_WIDTH]).astype(BF16)
        kv = _dot(h1, wkv_ref[...])
        k0, k1 = _dup_halves(kv[:, 0:PAIR])
        v0, v1 = _dup_halves(kv[:, PAIR:2 * PAIR])
        kd_ref[:, 0:PAIR] = k0.astype(BF16)
        kd_ref[:, PAIR:2 * PAIR] = k1.astype(BF16)
        vd_ref[:, 0:PAIR] = v0.astype(BF16)
        vd_ref[:, PAIR:2 * PAIR] = v1.astype(BF16)

    return pl.pallas_call(
        body, name="layer_b_proj", grid=(t_len // tm,),
        in_specs=[_rows(tm, D_MODEL), _rows(tm, A_WIDTH), _resident(w_out_a.shape), _const(g1.shape),
                  _const(b1.shape), _resident(w_in.shape), _resident(w_kv.shape)],
        out_specs=[_rows(tm, D_MODEL), _rows(tm, 1), _rows(tm, B_WIDTH), _rows(tm, B_WIDTH), _rows(tm, 2 * PAIR),
                   _rows(tm, 2 * PAIR)],
        out_shape=[jax.ShapeDtypeStruct((t_len, D_MODEL), F32), jax.ShapeDtypeStruct((t_len, 1), F32),
                   jax.ShapeDtypeStruct((t_len, B_WIDTH), BF16), jax.ShapeDtypeStruct((t_len, B_WIDTH), BF16),
                   jax.ShapeDtypeStruct((t_len, 2 * PAIR), BF16), jax.ShapeDtypeStruct((t_len, 2 * PAIR), BF16)],
        compiler_params=_params(),
    )(x, y, w_out_a, g1, b1, w_in, w_kv)


GROUP = N_HEADS // N_KV
GROUP_Q = GROUP * CHUNK


def _window_tables():
    j = jnp.arange(2 * CHUNK, dtype=jnp.int32)[:, None]
    t = jnp.arange(CHUNK, dtype=jnp.int32)[None, :]
    dist = t + CHUNK - j
    inside = (dist >= 0) & (dist < CHUNK)
    return jnp.stack([inside & (j >= CHUNK), inside]).astype(F32)


def _band(ref, chunk_index, kvh):
    prev0 = pl.multiple_of(jnp.maximum(chunk_index - 1, 0) * CHUNK, CHUNK)
    cur0 = pl.multiple_of(chunk_index * CHUNK, CHUNK)
    cols = slice(kvh * PAIR, (kvh + 1) * PAIR)
    return jnp.concatenate([ref[pl.ds(prev0, CHUNK), cols], ref[pl.ds(cur0, CHUNK), cols]], axis=0)


def _group_tables(bias_ref, win_ref, sink_ref, chunk_index, kvh):
    bias = jnp.concatenate([bias_ref[kvh * GROUP + j] for j in range(GROUP)], axis=1)
    win = win_ref[jnp.minimum(chunk_index, 1)]
    mask = jnp.concatenate([win] * GROUP, axis=1) > 0.5
    sink = jnp.concatenate([jnp.full((1, CHUNK), sink_ref[0, kvh * GROUP + j], F32) for j in range(GROUP)], axis=1)
    return bias, mask, sink


def _attn_probs(qs, kband, bias, mask, sink):
    logits = jnp.where(mask, _dot_nt(kband, qs) + bias, NEG_INF)
    m = jnp.maximum(jnp.max(logits, axis=0, keepdims=True), sink)
    e = jnp.exp(logits - m)
    es = jnp.exp(sink - m)
    inv = 1.0 / (jnp.sum(e, axis=0, keepdims=True) + es)
    return e * inv, es * inv


def _half_mask():
    return lax.broadcasted_iota(jnp.int32, (CHUNK, PAIR), 1) < HEAD_DIM


def _stack_heads(src_ref, rows, kvh, dst_scr, lo):
    for j in range(GROUP):
        h = kvh * GROUP + j
        blk = src_ref[rows, (h // 2) * PAIR:(h // 2 + 1) * PAIR].astype(F32)
        keep = lo if h % 2 == 0 else ~lo
        dst_scr[j * CHUNK:(j + 1) * CHUNK, :] = jnp.where(keep, blk, 0.0).astype(BF16)


def _probs_spec(tm):
    return pl.BlockSpec((tm // CHUNK, N_KV, 2 * CHUNK, GROUP_Q), lambda i: (i, 0, 0, 0))


def _sink_probs_spec():
    return pl.BlockSpec((1, 8, GROUP_Q), lambda i: (i, 0, 0))


def _unstack_pairs(stacked, pp, lo):
    return jnp.where(lo, stacked[(2 * pp) * CHUNK:(2 * pp + 1) * CHUNK], stacked[(2 * pp + 1) * CHUNK:(2 * pp + 2) * CHUNK])


def _layer_b_fwd(q, zb, kd, vd, bias, win, sinks, xh1, g1, b1, w_out, g2, b2, tgt):
    t_len = q.shape[0]
    tm = TM_ATTN

    def body(q_ref, z_ref, kd_ref, vd_ref, bias_ref, win_ref, sink_ref, xh_ref, g1_ref, b1_ref, wout_ref, g2_ref,
             b2_ref, tgt_ref, o_ref, p_ref, ps_ref, dr_ref, loss_ref, dg_ref, db_ref, o_scr, qs_scr):
        i = pl.program_id(0)

        @pl.when(i == 0)
        def _():
            loss_ref[...] = jnp.zeros_like(loss_ref)
            dg_ref[...] = jnp.zeros_like(dg_ref)
            db_ref[...] = jnp.zeros_like(db_ref)

        lo = _half_mask()
        ps_ref[...] = jnp.zeros_like(ps_ref)
        for ci in range(tm // CHUNK):
            cg = i * (tm // CHUNK) + ci
            rows = slice(ci * CHUNK, (ci + 1) * CHUNK)
            for kvh in range(N_KV):
                kband = _band(kd_ref, cg, kvh)
                vband = _band(vd_ref, cg, kvh)
                bias_g, mask, sink = _group_tables(bias_ref, win_ref, sink_ref, cg, kvh)
                _stack_heads(q_ref, rows, kvh, qs_scr, lo)
                p, p_sink = _attn_probs(qs_scr[...], kband, bias_g, mask, sink)
                p = p.astype(BF16)
                p_ref[ci, kvh] = p
                ps_ref[0, ci * N_KV + kvh:ci * N_KV + kvh + 1, :] = p_sink
                o_stack = _dot_tn(p, vband)
                for pp in range(GROUP // 2):
                    pair = kvh * (GROUP // 2) + pp
                    o_scr[rows, pair * PAIR:(pair + 1) * PAIR] = _unstack_pairs(o_stack, pp, lo)
        o = o_scr[...]
        o_ref[...] = o.astype(BF16)
        sz, _ = _silu_parts(z_ref[...].astype(F32))
        y = (o * sz).astype(BF16)
        h1 = xh_ref[...] * g1_ref[...] + b1_ref[...]
        r = ALPHA * h1 + _dot(y, wout_ref[...])
        xh2, rstd2 = _ln_fwd(r)
        diff = xh2 * g2_ref[...] + b2_ref[...] - tgt_ref[...]
        loss_ref[...] += jnp.sum(diff * diff, axis=0, keepdims=True)
        dh2 = diff * (1.0 / D_MODEL)
        dg_ref[...] += jnp.sum(dh2 * xh2, axis=0, keepdims=True)
        db_ref[...] += jnp.sum(dh2, axis=0, keepdims=True)
        dr_ref[...] = _ln_bwd(dh2 * g2_ref[...], xh2, rstd2)

    vec = jax.ShapeDtypeStruct((1, D_MODEL), F32)
    return pl.pallas_call(
        body, name="layer_b_fwd", grid=(t_len // tm,),
        in_specs=[_rows(tm, B_WIDTH), _rows(tm, B_WIDTH), _resident(kd.shape), _resident(vd.shape),
                  _resident(bias.shape), _resident(win.shape), pl.BlockSpec(memory_space=pltpu.SMEM),
                  _rows(tm, D_MODEL), _const(g1.shape), _const(b1.shape), _resident(w_out.shape), _const(g2.shape),
                  _const(b2.shape), _rows(tm, D_MODEL)],
        out_specs=[_rows(tm, B_WIDTH), _probs_spec(tm), _sink_probs_spec(), _rows(tm, D_MODEL)]
        + [_const((1, D_MODEL))] * 3,
        out_shape=[jax.ShapeDtypeStruct((t_len, B_WIDTH), BF16),
                   jax.ShapeDtypeStruct((t_len // CHUNK, N_KV, 2 * CHUNK, GROUP_Q), BF16),
                   jax.ShapeDtypeStruct((t_len // tm, 8, GROUP_Q), F32),
                   jax.ShapeDtypeStruct((t_len, D_MODEL), F32), vec, vec, vec],
        scratch_shapes=[pltpu.VMEM((tm, B_WIDTH), F32), pltpu.VMEM((GROUP_Q, PAIR), BF16)],
        compiler_params=_params(),
    )(q, zb, kd, vd, bias, win, sinks, xh1, g1, b1, w_out, g2, b2, tgt)


def _layer_b_bwd_attn(dr2, zb, o, q, kd, vd, probs, sink_probs, w_out):
    t_len = q.shape[0]
    tm = TM_ATTN
    n_steps = t_len // tm
    n_chunks = tm // CHUNK

    def body(dr_ref, z_ref, o_ref, q_ref, kd_ref, vd_ref, p_ref, ps_ref, wout_ref,
             dq_ref, dz_ref, dkd_ref, dvd_ref, ck_ref, cv_ref, gw_ref, dsink_ref, dbias_ref,
             do_scr, qs_scr, dos_scr, gw_acc):
        i = pl.program_id(0)

        @pl.when(i == 0)
        def _():
            gw_acc[...] = jnp.zeros_like(gw_acc)
            dsink_ref[...] = jnp.zeros_like(dsink_ref)
            dbias_ref[...] = jnp.zeros_like(dbias_ref)

        drb = dr_ref[...].astype(BF16)
        dy = _dot_nt(drb, wout_ref[...])
        z = z_ref[...].astype(F32)
        sz, dsz = _silu_parts(z)
        o_t = o_ref[...].astype(F32)
        dz_ref[...] = (dy * o_t * dsz).astype(BF16)
        do_scr[...] = (dy * sz).astype(BF16)
        gw_acc[...] += _dot_tn((o_t * sz).astype(BF16), drb)

        lo = _half_mask()
        for kvh in range(N_KV):
            kcols = slice(kvh * PAIR, (kvh + 1) * PAIR)
            dk_bands, dv_bands = [], []
            for ci in range(n_chunks):
                cg = i * n_chunks + ci
                rows = slice(ci * CHUNK, (ci + 1) * CHUNK)
                kband = _band(kd_ref, cg, kvh)
                vband = _band(vd_ref, cg, kvh)
                _stack_heads(q_ref, rows, kvh, qs_scr, lo)
                _stack_heads(do_scr, rows, kvh, dos_scr, lo)
                qs = qs_scr[...]
                dos = dos_scr[...]
                pb = p_ref[ci, kvh]
                p = pb.astype(F32)
                p_sink = ps_ref[0, ci * N_KV + kvh:ci * N_KV + kvh + 1, :]
                dp = _dot_nt(vband, dos)
                delta = jnp.sum(p * dp, axis=0, keepdims=True)
                dlog = p * (dp - delta)
                for j in range(GROUP):
                    dbias_ref[kvh * GROUP + j] += dlog[:, j * CHUNK:(j + 1) * CHUNK]
                dsink_ref[kvh:kvh + 1, :] += -(p_sink * delta)
                ds = dlog.astype(BF16)
                dq_stack = _dot_tn(ds, kband) * SCALE
                for pp in range(GROUP // 2):
                    pair = kvh * (GROUP // 2) + pp
                    dq_ref[rows, pair * PAIR:(pair + 1) * PAIR] = _unstack_pairs(dq_stack, pp, lo).astype(BF16)
                dk_bands.append(_dot(ds, qs))
                dv_bands.append(_dot(pb, dos))
            for bands, out_ref, carry_ref in ((dk_bands, dkd_ref, ck_ref), (dv_bands, dvd_ref, cv_ref)):
                carry_ref[0, :, kcols] = bands[0][0:CHUNK]
                for ci in range(n_chunks):
                    own = bands[ci][CHUNK:2 * CHUNK]
                    if ci + 1 < n_chunks:
                        own = own + bands[ci + 1][0:CHUNK]
                    out_ref[ci * CHUNK:(ci + 1) * CHUNK, kcols] = own

        @pl.when(i == n_steps - 1)
        def _():
            gw_ref[...] = gw_acc[...].astype(BF16)

    carry_spec = pl.BlockSpec((1, CHUNK, 2 * PAIR), lambda i: (i, 0, 0))
    carry_shape = jax.ShapeDtypeStruct((n_steps, CHUNK, 2 * PAIR), F32)
    bias_shape = (N_HEADS, 2 * CHUNK, CHUNK)
    return pl.pallas_call(
        body, name="layer_b_bwd_attn", grid=(n_steps,),
        in_specs=[_rows(tm, D_MODEL), _rows(tm, B_WIDTH), _rows(tm, B_WIDTH), _rows(tm, B_WIDTH),
                  _resident(kd.shape), _resident(vd.shape), _probs_spec(tm), _sink_probs_spec(),
                  _resident(w_out.shape)],
        out_specs=[_rows(tm, B_WIDTH), _rows(tm, B_WIDTH), _rows(tm, 2 * PAIR), _rows(tm, 2 * PAIR),
                   carry_spec, carry_spec, _const(w_out.shape), _const((N_KV, GROUP_Q)), _const(bias_shape)],
        out_shape=[jax.ShapeDtypeStruct((t_len, B_WIDTH), BF16), jax.ShapeDtypeStruct((t_len, B_WIDTH), BF16),
                   jax.ShapeDtypeStruct((t_len, 2 * PAIR), F32), jax.ShapeDtypeStruct((t_len, 2 * PAIR), F32),
                   carry_shape, carry_shape, jax.ShapeDtypeStruct(w_out.shape, BF16),
                   jax.ShapeDtypeStruct((N_KV, GROUP_Q), F32), jax.ShapeDtypeStruct(bias_shape, F32)],
        scratch_shapes=[pltpu.VMEM((tm, B_WIDTH), BF16), pltpu.VMEM((GROUP_Q, PAIR), BF16),
                        pltpu.VMEM((GROUP_Q, PAIR), BF16), pltpu.VMEM(w_out.shape, F32)],
        compiler_params=_params(),
    )(dr2, zb, o, q, kd, vd, probs, sink_probs, w_out)


def _layer_b_bwd_proj(xh1, rstd1, g1, b1, dr2, dq, dzb, dkd, dvd, carry_k, carry_v, w_in, w_kv):
    t_len = xh1.shape[0]
    tm = TM_MM
    n_steps = t_len // tm
    per_tile = tm // TM_ATTN
    n_carry = carry_k.shape[0]

    def body(xh_ref, rstd_ref, g_ref, b_ref, dr2_ref, dq_ref, dz_ref, dkd_ref, dvd_ref, *rest):
        carry_refs = rest[:2 * per_tile]
        win_ref, wkv_ref, dr1_ref, dg_ref, db_ref, gwin_ref, gwkv_ref, acc_in, acc_kv = rest[2 * per_tile:]
        i = pl.program_id(0)

        @pl.when(i == 0)
        def _():
            acc_in[...] = jnp.zeros_like(acc_in)
            acc_kv[...] = jnp.zeros_like(acc_kv)
            dg_ref[...] = jnp.zeros_like(dg_ref)
            db_ref[...] = jnp.zeros_like(db_ref)

        lo = lax.broadcasted_iota(jnp.int32, (tm, PAIR), 1) < HEAD_DIM

        def heads_gradient(tile_ref, refs):
            parts = []
            for a in range(per_tile):
                parts.append(tile_ref[a * TM_ATTN:(a + 1) * TM_ATTN - CHUNK, :])
                carry = refs[a][0]
                if a == per_tile - 1:
                    carry = jnp.where(i < n_steps - 1, carry, 0.0)
                parts.append(tile_ref[(a + 1) * TM_ATTN - CHUNK:(a + 1) * TM_ATTN, :] + carry)
            dup = jnp.concatenate(parts, axis=0)
            return jnp.where(lo, _fold_halves(dup[:, 0:PAIR]), _fold_halves(dup[:, PAIR:2 * PAIR]))

        xh = xh_ref[...]
        h1 = (xh * g_ref[...] + b_ref[...]).astype(BF16)
        dq_t = dq_ref[...]
        dz_t = dz_ref[...]
        dkv = jnp.concatenate([heads_gradient(dkd_ref, carry_refs[:per_tile]),
                               heads_gradient(dvd_ref, carry_refs[per_tile:])], axis=1).astype(BF16)
        dh1 = ALPHA * dr2_ref[...]
        dh1 += _dot_nt(dq_t, win_ref[:, 0:B_WIDTH])
        dh1 += _dot_nt(dz_t, win_ref[:, B_WIDTH:2 * B_WIDTH])
        dh1 += _dot_nt(dkv, wkv_ref[...])
        acc_in[:, 0:B_WIDTH] += _dot_tn(h1, dq_t)
        acc_in[:, B_WIDTH:2 * B_WIDTH] += _dot_tn(h1, dz_t)
        acc_kv[...] += _dot_tn(h1, dkv)
        dg_ref[...] += jnp.sum(dh1 * xh, axis=0, keepdims=True)
        db_ref[...] += jnp.sum(dh1, axis=0, keepdims=True)
        dr1_ref[...] = _ln_bwd(dh1 * g_ref[...], xh, rstd_ref[...])

        @pl.when(i == n_steps - 1)
        def _():
            half_rows = D_MODEL // 2
            shard_cols = 2 * B_WIDTH // N_CHIPS
            for s in range(N_CHIPS):
                for c in range(2):
                    gwin_ref[2 * s + c] = acc_in[c * half_rows:(c + 1) * half_rows,
                                                 s * shard_cols:(s + 1) * shard_cols].astype(BF16)
            gwkv_ref[...] = acc_kv[...].astype(BF16)

    vec = jax.ShapeDtypeStruct((1, D_MODEL), F32)
    gwin_shape = (N_DEV, D_MODEL // 2, 2 * B_WIDTH // N_CHIPS)

    def carry_spec(a):
        return pl.BlockSpec((1, CHUNK, 2 * PAIR), lambda i: (jnp.minimum(per_tile * i + a + 1, n_carry - 1), 0, 0))

    carry_specs = [carry_spec(a) for a in range(per_tile)]
    return pl.pallas_call(
        body, name="layer_b_bwd_proj", grid=(n_steps,),
        in_specs=[_rows(tm, D_MODEL), _rows(tm, 1), _const(g1.shape), _const(b1.shape), _rows(tm, D_MODEL),
                  _rows(tm, B_WIDTH), _rows(tm, B_WIDTH), _rows(tm, 2 * PAIR), _rows(tm, 2 * PAIR)]
        + carry_specs + carry_specs + [_resident(w_in.shape), _resident(w_kv.shape)],
        out_specs=[_rows(tm, D_MODEL), _const((1, D_MODEL)), _const((1, D_MODEL)), _const(gwin_shape),
                   _const(w_kv.shape)],
        out_shape=[jax.ShapeDtypeStruct((t_len, D_MODEL), F32), vec, vec,
                   jax.ShapeDtypeStruct(gwin_shape, BF16), jax.ShapeDtypeStruct(w_kv.shape, BF16)],
        scratch_shapes=[pltpu.VMEM(w_in.shape, F32), pltpu.VMEM(w_kv.shape, F32)],
        compiler_params=_params(),
    )(xh1, rstd1, g1, b1, dr2, dq, dzb, dkd, dvd, *([carry_k] * per_tile), *([carry_v] * per_tile), w_in, w_kv)


def _layer_a_bwd_mix(dr1, u, vh, z, y, rv, w_out, lng, lnb, ws, bsp_t, after):
    t_len = u.shape[0]
    tm = TM_ATTN
    n_steps = t_len // tm

    def body(dr_ref, u_ref, vh_ref, z_ref, y_ref, rv_ref, wout_ref, lng_ref, lnb_ref, ws_ref, bsp_ref, after_ref,
             dp_ref, gw_ref, dws_ref, dbsp_ref, dgs_ref, dbs_ref, s_scr, dvn_scr, gw_acc):
        i = pl.program_id(0)

        @pl.when(i == 0)
        def _():
            gw_acc[...] = jnp.zeros_like(gw_acc)
            dws_ref[...] = jnp.zeros_like(dws_ref)
            dbsp_ref[...] = jnp.zeros_like(dbsp_ref)
            dgs_ref[...] = jnp.zeros_like(dgs_ref)
            dbs_ref[...] = jnp.zeros_like(dbs_ref)

        drb = dr_ref[...].astype(BF16)
        dy = _dot_nt(drb, wout_ref[...])
        vh_t = vh_ref[...].astype(F32)
        vn = (vh_t * lng_ref[...] + lnb_ref[...]).astype(BF16)
        _spatial_mix(ws_ref, bsp_ref, vn, s_scr, tm // CHUNK)
        gw_acc[...] += _dot_tn(y_ref[...], drb)
        s = s_scr[...]
        sz, dsz = _silu_parts(z_ref[...].astype(F32))
        t = dy * u_ref[...].astype(F32)
        dp_ref[:, 0:A_WIDTH] = (dy * (s * sz)).astype(BF16)
        dp_ref[:, 2 * A_WIDTH:3 * A_WIDTH] = (t * s * dsz).astype(BF16)
        ds = (t * sz).astype(BF16)

        group_of = lax.broadcasted_iota(jnp.int32, (A_WIDTH, CHUNK), 0) // A_GROUP_DIM
        indicator = (group_of == lax.broadcasted_iota(jnp.int32, (A_WIDTH, CHUNK), 1)).astype(BF16)
        group_sums = _dot(ds, indicator)
        total = group_sums[0:CHUNK]
        for ci in range(1, tm // CHUNK):
            total += group_sums[ci * CHUNK:(ci + 1) * CHUNK]
        dbsp_ref[...] += total

        tri = (lax.broadcasted_iota(jnp.int32, (CHUNK, CHUNK), 0)
               >= lax.broadcasted_iota(jnp.int32, (CHUNK, CHUNK), 1))
        for g in range(A_GROUPS):
            wsg = jnp.where(tri, ws_ref[g], 0.0).astype(BF16)
            cols = slice(g * A_GROUP_DIM, (g + 1) * A_GROUP_DIM)
            dws_g = jnp.zeros((CHUNK, CHUNK), F32)
            for ci in range(tm // CHUNK):
                rows = slice(ci * CHUNK, (ci + 1) * CHUNK)
                ds_b = ds[rows, cols]
                dws_g += _dot_nt(ds_b, vn[rows, cols])
                dvn_scr[rows, cols] = _dot_tn(wsg, ds_b)
            dws_ref[g] += jnp.where(tri, dws_g, 0.0)
        dvn = dvn_scr[...]
        dgs_ref[...] += jnp.sum(dvn * vh_t, axis=0, keepdims=True)
        dbs_ref[...] += jnp.sum(dvn, axis=0, keepdims=True)
        dp_ref[:, A_WIDTH:2 * A_WIDTH] = _ln_bwd(dvn * lng_ref[...], vh_t, rv_ref[...]).astype(BF16)

        @pl.when(i == n_steps - 1)
        def _():
            gw_ref[...] = gw_acc[...].astype(BF16)

    wide = jax.ShapeDtypeStruct((1, A_WIDTH), F32)
    return pl.pallas_call(
        body, name="layer_a_bwd_mix", grid=(n_steps,),
        in_specs=[_rows(tm, D_MODEL), _rows(tm, A_WIDTH), _rows(tm, A_WIDTH), _rows(tm, A_WIDTH), _rows(tm, A_WIDTH),
                  _rows(tm, 1), _resident(w_out.shape), _const(lng.shape), _const(lnb.shape), _const(ws.shape),
                  _const(bsp_t.shape), _const(after.shape)],
        out_specs=[_rows(tm, 3 * A_WIDTH), _const(w_out.shape), _const(ws.shape), _const((CHUNK, CHUNK)),
                   _const((1, A_WIDTH)), _const((1, A_WIDTH))],
        out_shape=[jax.ShapeDtypeStruct((t_len, 3 * A_WIDTH), BF16), jax.ShapeDtypeStruct(w_out.shape, BF16),
                   jax.ShapeDtypeStruct(ws.shape, F32), jax.ShapeDtypeStruct((CHUNK, CHUNK), F32),
                   wide, wide],
        scratch_shapes=[pltpu.VMEM((tm, A_WIDTH), F32), pltpu.VMEM((tm, A_WIDTH), F32),
                        pltpu.VMEM(w_out.shape, F32)],
        compiler_params=_params(),
    )(dr1, u, vh, z, y, rv, w_out, lng, lnb, ws, bsp_t, after)


def _layer_a_bwd_dx(dr1, dp, w_in, after, updates=()):
    t_len = dr1.shape[0]
    tm = TM_MM
    n_steps = t_len // tm
    n_upd = len(updates)

    def body(dr_ref, dp_ref, win_ref, after_ref, *refs):
        upd_in, dx_ref, upd_out = refs[:4 * n_upd], refs[4 * n_upd], refs[4 * n_upd + 1:]
        dx_ref[...] = ALPHA * dr_ref[...] + _dot_nt(dp_ref[...], win_ref[...])
        for k in range(n_upd):
            w_ref, g_ref, m_ref, v_ref = upd_in[4 * k:4 * k + 4]
            g_out, d_ref, nm_ref, nv_ref = upd_out[4 * k:4 * k + 4]
            g_out[...] = g_ref[...]
            _adamw_update(w_ref, g_ref, m_ref, v_ref, d_ref, nm_ref, nv_ref)

    upd_specs, upd_shapes, upd_args = [], [], []
    for w, g, m, v in updates:
        rows, cols = w.shape
        upd_specs.append(pl.BlockSpec((rows // n_steps, cols), lambda i: (i, 0)))
        upd_shapes.append(jax.ShapeDtypeStruct((rows, cols), F32))
        upd_args += [w, g, m, v]
    return pl.pallas_call(
        body, name="layer_a_bwd_dx", grid=(n_steps,),
        in_specs=[_rows(tm, D_MODEL), _rows(tm, 3 * A_WIDTH), _resident(w_in.shape), _const(after.shape)]
        + [s for s in upd_specs for _ in range(4)],
        out_specs=[_rows(tm, D_MODEL)] + [s for s in upd_specs for _ in range(4)],
        out_shape=[jax.ShapeDtypeStruct((t_len, D_MODEL), F32)] + [s for s in upd_shapes for _ in range(4)],
        compiler_params=_params(),
    )(dr1, dp, w_in, after, *upd_args)


def _layer_a_bwd_win(xt, dp, after):
    t_len = xt.shape[1]
    tm = TM_WIN
    n_steps = t_len // tm
    shard_cols = 3 * A_WIDTH // N_CHIPS
    half_rows = D_MODEL // 2

    def body(xt_ref, dp_ref, after_ref, gw_ref, acc):
        i = pl.program_id(1)

        @pl.when(i == 0)
        def _():
            acc[...] = jnp.zeros_like(acc)

        acc[...] += _dot(xt_ref[...], dp_ref[...])

        @pl.when(i == n_steps - 1)
        def _():
            for c in range(2):
                gw_ref[0, c] = acc[c * half_rows:(c + 1) * half_rows, :].astype(BF16)

    return pl.pallas_call(
        body, name="layer_a_bwd_win", grid=(N_CHIPS, n_steps),
        in_specs=[pl.BlockSpec((D_MODEL, tm), lambda j, i: (0, i)),
                  pl.BlockSpec((tm, shard_cols), lambda j, i: (i, j)), _const(after.shape)],
        out_specs=pl.BlockSpec((1, 2, half_rows, shard_cols), lambda j, i: (j, 0, 0, 0)),
        out_shape=jax.ShapeDtypeStruct((N_CHIPS, 2, half_rows, shard_cols), BF16),
        scratch_shapes=[pltpu.VMEM((D_MODEL, shard_cols), F32)],
        compiler_params=_params(("arbitrary", "arbitrary")),
    )(xt, dp, after)


def _bucket_onehot():
    t = jnp.arange(CHUNK, dtype=jnp.int32)[None, :]
    j = jnp.arange(2 * CHUNK, dtype=jnp.int32)[:, None]
    dist = jnp.clip(t + CHUNK - j, 0, CHUNK - 1)
    max_exact = REL_BUCKETS // 2
    df = jnp.maximum(dist, 1).astype(F32)
    large = max_exact + (jnp.log(df / max_exact) / math.log(CHUNK / max_exact)
                         * (REL_BUCKETS - max_exact)).astype(jnp.int32)
    bucket = jnp.where(dist < max_exact, dist, jnp.minimum(large, REL_BUCKETS - 1))
    onehot = bucket.reshape(1, -1) == jnp.arange(REL_BUCKETS, dtype=jnp.int32)[:, None]
    return onehot.astype(F32)


def _bias_expand(rel_t, onehot):
    def body(rel_ref, oh_ref, out_ref):
        out_ref[...] = jnp.dot(rel_ref[...], oh_ref[...], preferred_element_type=F32,
                               precision=lax.Precision.HIGHEST)

    return pl.pallas_call(
        body, name="bias_expand",
        out_shape=jax.ShapeDtypeStruct((N_HEADS, onehot.shape[1]), F32),
    )(rel_t, onehot)


def _bias_reduce(onehot, dbias):
    def body(oh_ref, db_ref, out_ref):
        out_ref[...] = lax.dot_general(oh_ref[...], db_ref[...], (((1,), (1,)), ((), ())),
                                       preferred_element_type=F32, precision=lax.Precision.HIGHEST)

    return pl.pallas_call(
        body, name="bias_reduce",
        out_shape=jax.ShapeDtypeStruct((REL_BUCKETS, N_HEADS), F32),
    )(onehot, dbias)


def _place():
    return lax.axis_index("x"), lax.axis_index("y"), lax.axis_index("c")


def _shard_window(full_ref, shard_shape, col_sharded, s, half):
    rows, cols = shard_shape
    if col_sharded:
        rsel = pl.ds(0, rows) if half is None else pl.ds(half * (rows // 2), rows // 2)
        return full_ref.at[rsel, pl.ds(s * cols, cols)]
    if half is None:
        return full_ref.at[pl.ds(s * rows, rows), :]
    return full_ref.at[pl.ds(s * rows + half * (rows // 2), rows // 2), :]


def _other_chips(x, y):
    return [(1 - x, y), (x, 1 - y), (1 - x, 1 - y)]


def _gather_weights(shards, col_sharded, fetch, ln_shard):
    n_w = len(shards)
    fetched = [w for w in range(n_w) if fetch[w]]
    full_shapes = []
    for w, cs in zip(shards, col_sharded):
        r, c = w.shape
        full_shapes.append((r, c * N_CHIPS) if cs else (r * N_CHIPS, c))

    def body(*refs):
        in_refs = refs[:n_w]
        ln_ref = refs[n_w]
        full_refs = refs[n_w + 1:2 * n_w + 1]
        ln_full = refs[2 * n_w + 1]
        stage = refs[2 * n_w + 2:3 * n_w + 2]
        send_sems, recv_sems, local_sems, ln_send, ln_recv = refs[3 * n_w + 2:]
        x, y, c = _place()
        s_me = 2 * x + y
        chips = _other_chips(x, y)

        def shard_window(w, s, half):
            return _shard_window(full_refs[w], shards[w].shape, col_sharded[w], s, half)

        def stage_half(w, half):
            rows = shards[w].shape[0]
            return stage[w].at[pl.ds(half * (rows // 2), rows // 2), :]

        def ici_copy(w, k, sender_shard, src):
            return pltpu.make_async_remote_copy(
                src_ref=src, dst_ref=shard_window(w, sender_shard, c),
                send_sem=send_sems.at[w * 3 + k], recv_sem=recv_sems.at[w * 3 + k],
                device_id=(*chips[k], c), device_id_type=MESH)

        def d2d_copy(w, k, half):
            s_k = 2 * chips[k][0] + chips[k][1]
            win = shard_window(w, s_k, half)
            return pltpu.make_async_remote_copy(
                src_ref=win, dst_ref=win,
                send_sem=send_sems.at[3 * n_w + w * 3 + k], recv_sem=recv_sems.at[3 * n_w + w * 3 + k],
                device_id=(x, y, 1 - c), device_id_type=MESH)

        def ln_copy(k, slot):
            return pltpu.make_async_remote_copy(
                src_ref=ln_ref, dst_ref=ln_full.at[slot], send_sem=ln_send.at[k], recv_sem=ln_recv.at[k],
                device_id=(*chips[k], c), device_id_type=MESH)

        for w in range(n_w):
            stage[w][...] = in_refs[w][...].astype(BF16)
        own = [pltpu.make_async_copy(stage[w], shard_window(w, s_me, None), local_sems.at[w]) for w in range(n_w)]
        for cp in own:
            cp.start()
        ln_full[s_me] = ln_ref[...]
        first = [ici_copy(w, k, s_me, stage_half(w, c)) for w in fetched for k in range(3)]
        first += [ln_copy(k, s_me) for k in range(3)]
        for cp in first:
            cp.start()
        passed = []
        for w in fetched:
            for k in range(3):
                s_k = 2 * chips[k][0] + chips[k][1]
                ici_copy(w, k, s_k, stage_half(w, c)).wait_recv()
                fwd = d2d_copy(w, k, c)
                fwd.start()
                passed.append(fwd)
        for w in fetched:
            for k in range(3):
                d2d_copy(w, k, 1 - c).wait_recv()
        for k in range(3):
            ln_copy(k, 2 * chips[k][0] + chips[k][1]).wait_recv()
        for cp in first + passed:
            cp.wait_send()
        for cp in own:
            cp.wait()

    vmem = pl.BlockSpec(memory_space=pltpu.VMEM)
    hbm = pl.BlockSpec(memory_space=pl.ANY)
    return pl.pallas_call(
        body, name="gather_weights",
        in_specs=[vmem] * (n_w + 1),
        out_specs=[hbm] * n_w + [vmem],
        out_shape=[jax.ShapeDtypeStruct(s, BF16) for s in full_shapes]
        + [jax.ShapeDtypeStruct((N_CHIPS,) + ln_shard.shape, F32)],
        scratch_shapes=[pltpu.VMEM(w.shape, BF16) for w in shards]
        + [pltpu.SemaphoreType.DMA((6 * n_w,)), pltpu.SemaphoreType.DMA((6 * n_w,)),
           pltpu.SemaphoreType.DMA((n_w,)), pltpu.SemaphoreType.DMA((3,)), pltpu.SemaphoreType.DMA((3,))],
        compiler_params=pltpu.CompilerParams(vmem_limit_bytes=VMEM_LIMIT),
    )(*shards, ln_shard)


def _fetch_copy(full_ref, shard_shape, col_sharded, sender_shard, send_sems, recv_sems, idx, chip, c):
    win = _shard_window(full_ref, shard_shape, col_sharded, sender_shard, None)
    return pltpu.make_async_remote_copy(src_ref=win, dst_ref=win, send_sem=send_sems.at[idx],
                                        recv_sem=recv_sems.at[idx], device_id=(*chip, c), device_id_type=MESH)


def _fetch_start(fulls, shard_shapes, col_sharded):
    n = len(fulls)

    def body(*refs):
        full = refs[:n]
        send_sems, recv_sems = refs[n], refs[n + 1]
        token = refs[-1]
        x, y, c = _place()
        for w in range(n):
            for k, chip in enumerate(_other_chips(x, y)):
                _fetch_copy(full[w], shard_shapes[w], col_sharded[w], 2 * x + y, send_sems, recv_sems, w * 3 + k,
                            chip, c).start()
        token[...] = jnp.zeros_like(token)

    outs = pl.pallas_call(
        body, name="fetch_start",
        out_shape=(pltpu.SemaphoreType.DMA((3 * n,)), pltpu.SemaphoreType.DMA((3 * n,)),
                   *[pltpu.HBM(f.shape, f.dtype) for f in fulls], jax.ShapeDtypeStruct((8, 128), F32)),
        in_specs=[_HBM] * n,
        out_specs=(_SEM, _SEM, *([_HBM] * n), pl.BlockSpec(memory_space=pltpu.VMEM)),
        input_output_aliases={i: 2 + i for i in range(n)},
        compiler_params=pltpu.CompilerParams(has_side_effects=pltpu.SideEffectType.DATAFLOW_SIDE_EFFECTING),
    )(*[pltpu.with_memory_space_constraint(f, pltpu.HBM) for f in fulls])
    return dict(send=outs[0], recv=outs[1], full=list(outs[2:2 + n])), outs[-1]


def _fetch_wait(group, shard_shapes, col_sharded, after):
    n = len(group["full"])

    def body(*refs):
        full = refs[:n]
        send_sems, recv_sems = refs[n], refs[n + 1]
        x, y, c = _place()
        for w in range(n):
            for k, chip in enumerate(_other_chips(x, y)):
                _fetch_copy(full[w], shard_shapes[w], col_sharded[w], 2 * x + y, send_sems, recv_sems, w * 3 + k,
                            chip, c).wait_send()
                _fetch_copy(full[w], shard_shapes[w], col_sharded[w], 2 * chip[0] + chip[1], send_sems, recv_sems,
                            w * 3 + k, chip, c).wait_recv()

    outs = pl.pallas_call(
        body, name="fetch_wait", out_shape=tuple(pltpu.HBM(f.shape, f.dtype) for f in group["full"]),
        in_specs=[_HBM] * n + [_SEM, _SEM, pl.BlockSpec(memory_space=pl.ANY)],
        out_specs=tuple([_HBM] * n), input_output_aliases={i: i for i in range(n)},
        compiler_params=pltpu.CompilerParams(has_side_effects=pltpu.SideEffectType.DATAFLOW_SIDE_EFFECTING),
    )(*group["full"], group["send"], group["recv"], after)
    return list(outs)


_HBM = pl.BlockSpec(memory_space=pltpu.HBM)
_SEM = pl.BlockSpec(memory_space=pltpu.SEMAPHORE)
_N_PEER = N_DEV - 1


def _peer(x, y, c, k):
    return (x + (k >> 2)) % 2, (y + ((k >> 1) & 1)) % 2, (c + (k & 1)) % 2


def _exchange_copy(src_ref, land_ref, sliced, send_sems, recv_sems, idx, x, y, c, k):
    px, py, pc = _peer(x, y, c, k)
    src = src_ref.at[4 * px + 2 * py + pc] if sliced else src_ref
    return pltpu.make_async_remote_copy(
        src_ref=src, dst_ref=land_ref.at[4 * x + 2 * y + c],
        send_sem=send_sems.at[idx], recv_sem=recv_sems.at[idx], device_id=(px, py, pc), device_id_type=MESH)


def _exchange_start(tag, arrays, sliced):
    n = len(arrays)
    lands = [lax.empty(a.shape if s else (N_DEV,) + a.shape, a.dtype) for a, s in zip(arrays, sliced)]

    def body(*refs):
        src, land = refs[:n], refs[n:2 * n]
        send_sems, recv_sems = refs[2 * n], refs[2 * n + 1]
        token = refs[-1]
        x, y, c = _place()
        for w in range(n):
            for k in range(1, N_DEV):
                _exchange_copy(src[w], land[w], sliced[w], send_sems, recv_sems, w * _N_PEER + k - 1, x, y, c, k).start()
        token[...] = jnp.zeros_like(token)

    outs = pl.pallas_call(
        body, name="exchange_start_" + tag,
        out_shape=(pltpu.SemaphoreType.DMA((n * _N_PEER,)), pltpu.SemaphoreType.DMA((n * _N_PEER,)),
                   *[pltpu.HBM(a.shape, a.dtype) for a in arrays], *[pltpu.HBM(l.shape, l.dtype) for l in lands],
                   jax.ShapeDtypeStruct((8, 128), F32)),
        in_specs=[_HBM] * (2 * n),
        out_specs=(_SEM, _SEM, *([_HBM] * (2 * n)), pl.BlockSpec(memory_space=pltpu.VMEM)),
        input_output_aliases={i: 2 + i for i in range(2 * n)},
        compiler_params=pltpu.CompilerParams(has_side_effects=pltpu.SideEffectType.DATAFLOW_SIDE_EFFECTING),
    )(*[pltpu.with_memory_space_constraint(a, pltpu.HBM) for a in arrays],
      *[pltpu.with_memory_space_constraint(l, pltpu.HBM) for l in lands])
    return dict(send=outs[0], recv=outs[1], src=list(outs[2:2 + n]), land=list(outs[2 + n:2 + 2 * n]),
                sliced=list(sliced)), outs[-1]


def _exchange_wait(tag, groups, after):
    counts = [len(g["src"]) for g in groups]
    total = sum(counts)

    def body(*refs):
        pos = 0
        x, y, c = _place()
        for g, n in zip(groups, counts):
            src, land = refs[pos:pos + n], refs[pos + n:pos + 2 * n]
            send_sems, recv_sems = refs[pos + 2 * n], refs[pos + 2 * n + 1]
            pos += 2 * n + 2
            for w in range(n):
                for k in range(1, N_DEV):
                    cp = _exchange_copy(src[w], land[w], g["sliced"][w], send_sems, recv_sems,
                                        w * _N_PEER + k - 1, x, y, c, k)
                    cp.wait_send()
                    cp.wait_recv()

    operands, in_specs, aliases, out_shape = [], [], {}, []
    for g in groups:
        for a in g["src"] + g["land"]:
            aliases[len(operands)] = len(out_shape)
            out_shape.append(pltpu.HBM(a.shape, a.dtype))
            operands.append(a)
            in_specs.append(_HBM)
        operands += [g["send"], g["recv"]]
        in_specs += [_SEM, _SEM]
    operands.append(after)
    in_specs.append(pl.BlockSpec(memory_space=pl.ANY))
    outs = pl.pallas_call(
        body, name="exchange_wait_" + tag, out_shape=tuple(out_shape), in_specs=in_specs,
        out_specs=tuple([_HBM] * (2 * total)), input_output_aliases=aliases,
        compiler_params=pltpu.CompilerParams(has_side_effects=pltpu.SideEffectType.DATAFLOW_SIDE_EFFECTING),
    )(*operands)
    srcs, lands, pos = [], [], 0
    for n in counts:
        srcs += list(outs[pos:pos + n])
        lands += list(outs[pos + n:pos + 2 * n])
        pos += 2 * n
    return srcs, lands


def _sum_and_swap(tag, pieces, lands, small=None, small_land=None):
    n_w = len(pieces)
    n_small = 0 if small is None else 1

    def body(*refs):
        g_refs, land_refs = refs[:n_w], refs[n_w:2 * n_w]
        pos = 2 * n_w + 2 * n_small
        out_refs = refs[pos:pos + n_w]
        pos += n_w + n_small
        bufs = refs[pos:pos + n_w]
        load_sems, swap_send, swap_recv = refs[pos + n_w + 2 * n_small:]
        x, y, c = _place()
        me = 4 * x + 2 * y + c

        def slot(k):
            px, py, pc = _peer(x, y, c, k)
            return 4 * px + 2 * py + pc

        def swap_copy(w, half):
            rows = pieces[w].shape[1]
            win = out_refs[w].at[pl.ds(pl.multiple_of(half * rows, rows), rows), :]
            return pltpu.make_async_remote_copy(
                src_ref=win, dst_ref=win, send_sem=swap_send.at[w], recv_sem=swap_recv.at[w],
                device_id=(x, y, 1 - c), device_id_type=MESH)

        loads = []
        for w in range(n_w):
            per_w = [pltpu.make_async_copy(g_refs[w].at[me], bufs[w].at[me], load_sems.at[w * N_DEV])]
            per_w += [pltpu.make_async_copy(land_refs[w].at[slot(k)], bufs[w].at[slot(k)], load_sems.at[w * N_DEV + k])
                      for k in range(1, N_DEV)]
            loads.append(per_w)
        small_loads = []
        if n_small:
            small_ref, small_land_ref = refs[2 * n_w], refs[2 * n_w + 1]
            small_out = refs[2 * n_w + 2 + n_w]
            small_buf, small_sems = refs[pos + n_w], refs[pos + n_w + 1]
            small_loads = [pltpu.make_async_copy(small_land_ref.at[slot(k)], small_buf.at[slot(k)],
                                                 small_sems.at[k - 1]) for k in range(1, N_DEV)]
        for cp in [cp for per_w in loads for cp in per_w] + small_loads:
            cp.start()
        if n_small:
            small_buf[me] = small_ref[...]
        swaps = []
        for w in range(n_w):
            for cp in loads[w]:
                cp.wait()
            rows = pieces[w].shape[1]
            total = bufs[w][0].astype(F32)
            for p in range(1, N_DEV):
                total += bufs[w][p].astype(F32)
            out_refs[w][pl.ds(pl.multiple_of(c * rows, rows), rows), :] = total
            sw = swap_copy(w, c)
            sw.start()
            swaps.append(sw)
        if n_small:
            for cp in small_loads:
                cp.wait()
            total = small_buf[0]
            for p in range(1, N_DEV):
                total += small_buf[p]
            small_out[...] = total
        for w in range(n_w):
            swap_copy(w, 1 - c).wait_recv()
        for sw in swaps:
            sw.wait_send()

    vmem = pl.BlockSpec(memory_space=pltpu.VMEM)
    hbm = pl.BlockSpec(memory_space=pl.ANY)
    small_args = [small, small_land] if n_small else []
    small_shapes = [jax.ShapeDtypeStruct(small.shape, F32)] if n_small else []
    small_scratch = ([pltpu.VMEM((N_DEV,) + small.shape, F32), pltpu.SemaphoreType.DMA((_N_PEER,))]
                     if n_small else [])
    return pl.pallas_call(
        body, name="sum_and_swap_" + tag,
        in_specs=[hbm] * (2 * n_w) + [vmem, hbm] * n_small,
        out_specs=[vmem] * (n_w + n_small),
        out_shape=[jax.ShapeDtypeStruct((2 * p.shape[1], p.shape[2]), F32) for p in pieces] + small_shapes,
        scratch_shapes=[pltpu.VMEM(p.shape, BF16) for p in pieces] + small_scratch
        + [pltpu.SemaphoreType.DMA((n_w * N_DEV,)), pltpu.SemaphoreType.DMA((n_w,)),
           pltpu.SemaphoreType.DMA((n_w,))],
        compiler_params=pltpu.CompilerParams(vmem_limit_bytes=VMEM_LIMIT),
    )(*pieces, *lands, *small_args)


def _adamw_update(w_ref, g_ref, m_ref, v_ref, d_ref, nm_ref, nv_ref):
    c1 = 1.0 - ADAM_B1 ** ADAM_STEP
    c2 = 1.0 - ADAM_B2 ** ADAM_STEP
    g_t = g_ref[...]
    nm = ADAM_B1 * m_ref[...] + (1.0 - ADAM_B1) * g_t
    nv = ADAM_B2 * v_ref[...] + (1.0 - ADAM_B2) * (g_t * g_t)
    d_ref[...] = -ADAM_LR * ((nm / c1) / (jnp.sqrt(nv / c2) + ADAM_EPS) + ADAM_WD * w_ref[...])
    nm_ref[...] = nm
    nv_ref[...] = nv


def _adamw_small(items):
    n = len(items)
    flat, shapes = [], []
    for w, g, m, v in items:
        cols = w.shape[-1]
        shapes.append(w.shape)
        flat += [a.reshape(w.size // cols, cols) for a in (w, g, m, v)]

    def body(*refs):
        for k in range(n):
            _adamw_update(*refs[4 * k:4 * k + 4], *refs[4 * n + 3 * k:4 * n + 3 * k + 3])

    outs = pl.pallas_call(
        body, name="adamw_small",
        out_shape=[jax.ShapeDtypeStruct(flat[4 * k].shape, F32) for k in range(n) for _ in range(3)],
    )(*flat)
    return [tuple(o.reshape(shapes[k]) for o in outs[3 * k:3 * k + 3]) for k in range(n)]


def _adamw(label, w, g, m, v):
    shape = w.shape
    cols = shape[-1]
    rows = w.size // cols
    args = [a.reshape(rows, cols) for a in (w, g, m, v)]

    def body(w_ref, g_ref, m_ref, v_ref, g_out, d_ref, nm_ref, nv_ref):
        g_out[...] = g_ref[...]
        _adamw_update(w_ref, g_ref, m_ref, v_ref, d_ref, nm_ref, nv_ref)

    block_rows = 256 if rows % 256 == 0 and rows > 256 else rows
    spec = pl.BlockSpec((block_rows, cols), lambda i: (i, 0))
    outs = pl.pallas_call(
        body, name="adamw_" + label, grid=(rows // block_rows,),
        in_specs=[spec] * 4, out_specs=[spec] * 4,
        out_shape=[jax.ShapeDtypeStruct((rows, cols), F32)] * 4,
        compiler_params=_params(),
    )(*args)
    return [o.reshape(shape) for o in outs]


def _no_send(tag, arrays, sliced):
    return jnp.zeros((8, 128), F32)


def _local_step(x, tgt, w_in_a, later_weights, first_after, sgu_ln_g, sgu_ln_b, w_spatial, b_spatial,
                attn_sinks, rel_bias, post_ln_g, post_ln_b, send=_no_send):
    bsp_t = b_spatial.T
    g1, b1 = post_ln_g[0:1], post_ln_b[0:1]
    g2, b2 = post_ln_g[1:2], post_ln_b[1:2]
    onehot = _bucket_onehot()
    bias = _bias_expand(rel_bias.T, onehot).reshape(N_HEADS, 2 * CHUNK, CHUNK)
    win = _window_tables()

    xt, u, vh, z, rv, y = _layer_a_fwd(x, w_in_a, sgu_ln_g, sgu_ln_b, w_spatial, bsp_t, first_after)
    w_out_a, w_kv, w_in_b, w_out_b = later_weights(y)
    xh1, rstd1, q, zb, kd, vd = _layer_b_proj(x, y, w_out_a, g1, b1, w_in_b, w_kv)
    o, probs, sink_probs, dr2, loss_vec, dg2, db2 = _layer_b_fwd(q, zb, kd, vd, bias, win, attn_sinks, xh1, g1, b1,
                                                                 w_out_b, g2, b2, tgt)
    dq, dzb, dkd, dvd, carry_k, carry_v, gw_out_b, dsink, dbias = _layer_b_bwd_attn(
        dr2, zb, o, q, kd, vd, probs, sink_probs, w_out_b)
    dr1, dg1, db1, gw_in_b, gw_kv = _layer_b_bwd_proj(xh1, rstd1, g1, b1, dr2, dq, dzb, dkd, dvd, carry_k, carry_v,
                                                      w_in_b, w_kv)
    gw_out_b = gw_out_b.reshape(N_DEV, -1, D_MODEL)
    gw_kv = gw_kv.reshape(N_DEV, -1, 2 * PAIR)
    after = send("b", [gw_out_b, gw_in_b, gw_kv], [True, True, True])
    dp, gw_out_a, dws, dbsp, dgs, dbs = _layer_a_bwd_mix(dr1, u, vh, z, y, rv, w_out_a, sgu_ln_g, sgu_ln_b,
                                                         w_spatial, bsp_t, after)
    gw_out_a = gw_out_a.reshape(N_DEV, -1, D_MODEL)
    drel = _bias_reduce(onehot, dbias.reshape(N_HEADS, -1))
    dsink = jnp.sum(dsink.reshape(N_HEADS, CHUNK), axis=1).reshape(1, N_HEADS)
    loss = ((0.5 / D_MODEL) * jnp.sum(loss_vec)).reshape(1, 1)
    small = dict(w_spatial=dws, b_spatial=dbsp[:, 0:A_GROUPS].T, attn_sinks=dsink,
                 rel_bias=drel, post_ln_g=jnp.concatenate([dg1, dg2], axis=0),
                 post_ln_b=jnp.concatenate([db1, db2], axis=0), sgu_ln_g=dgs, sgu_ln_b=dbs, loss=loss)
    after = send("a_out", [gw_out_a, _pack_small(small)], [True, False])
    gw_in_a = _layer_a_bwd_win(xt, dp, after).reshape(N_DEV, D_MODEL // 2, -1)
    after = send("a_in", [gw_in_a], [True])
    after, updates = after if isinstance(after, tuple) else (after, ())
    grad_x, *updated = _layer_a_bwd_dx(dr1, dp, w_in_a, after, updates)

    pieces = [gw_in_a, gw_out_a, gw_kv, gw_in_b, gw_out_b]
    return grad_x, pieces, small, updated


_SMALL_SHAPES = dict(w_spatial=(A_GROUPS, CHUNK, CHUNK), b_spatial=(A_GROUPS, CHUNK), attn_sinks=(1, N_HEADS),
                     rel_bias=(REL_BUCKETS, N_HEADS), post_ln_g=(2, D_MODEL), post_ln_b=(2, D_MODEL),
                     sgu_ln_g=(1, A_WIDTH), sgu_ln_b=(1, A_WIDTH), loss=(1, 1))
_SMALL_ORDER = tuple(_SMALL_SHAPES)


def _small_rows(name):
    size = math.prod(_SMALL_SHAPES[name])
    return -(-size // (_LANES * _SUBLANES)) * _SUBLANES


def _pack_small(small):
    parts = []
    for name in _SMALL_ORDER:
        assert small[name].shape == _SMALL_SHAPES[name], (name, small[name].shape)
        flat = small[name].reshape(-1)
        flat = jnp.pad(flat, (0, _small_rows(name) * _LANES - flat.size))
        parts.append(flat.reshape(-1, _LANES))
    return jnp.concatenate(parts, axis=0)


def _unpack_small(packed):
    out, row = {}, 0
    for name in _SMALL_ORDER:
        shape = _SMALL_SHAPES[name]
        out[name] = packed[row:row + _small_rows(name)].reshape(-1)[:math.prod(shape)].reshape(shape)
        row += _small_rows(name)
    return out


def kernel(x, w_in_a, sgu_ln_g, sgu_ln_b, w_spatial, b_spatial, w_out_a, w_kv, w_in_b, attn_sinks, rel_bias, w_out_b, post_ln_g, post_ln_b, loss_target, m_w_in_a, m_sgu_ln_g, m_sgu_ln_b, m_w_spatial, m_b_spatial, m_w_out_a, m_w_kv, m_w_in_b, m_attn_sinks, m_rel_bias, m_w_out_b, m_post_ln_g, m_post_ln_b, v_w_in_a, v_sgu_ln_g, v_sgu_ln_b, v_w_spatial, v_b_spatial, v_w_out_a, v_w_kv, v_w_in_b, v_attn_sinks, v_rel_bias, v_w_out_b, v_post_ln_g, v_post_ln_b):
    weights = dict(w_in_a=w_in_a, sgu_ln_g=sgu_ln_g, sgu_ln_b=sgu_ln_b, w_spatial=w_spatial, b_spatial=b_spatial,
                   w_out_a=w_out_a, w_kv=w_kv, w_in_b=w_in_b, attn_sinks=attn_sinks, rel_bias=rel_bias,
                   w_out_b=w_out_b, post_ln_g=post_ln_g, post_ln_b=post_ln_b)
    moments_m = dict(w_in_a=m_w_in_a, sgu_ln_g=m_sgu_ln_g, sgu_ln_b=m_sgu_ln_b, w_spatial=m_w_spatial,
                     b_spatial=m_b_spatial, w_out_a=m_w_out_a, w_kv=m_w_kv, w_in_b=m_w_in_b,
                     attn_sinks=m_attn_sinks, rel_bias=m_rel_bias, w_out_b=m_w_out_b, post_ln_g=m_post_ln_g,
                     post_ln_b=m_post_ln_b)
    moments_v = dict(w_in_a=v_w_in_a, sgu_ln_g=v_sgu_ln_g, sgu_ln_b=v_sgu_ln_b, w_spatial=v_w_spatial,
                     b_spatial=v_b_spatial, w_out_a=v_w_out_a, w_kv=v_w_kv, w_in_b=v_w_in_b,
                     attn_sinks=v_attn_sinks, rel_bias=v_rel_bias, w_out_b=v_w_out_b, post_ln_g=v_post_ln_g,
                     post_ln_b=v_post_ln_b)
    order = ("w_in_a", "sgu_ln_g", "sgu_ln_b", "w_spatial", "b_spatial", "w_out_a", "w_kv", "w_in_b", "attn_sinks",
             "rel_bias", "w_out_b", "post_ln_g", "post_ln_b")

    shard_index = 2 * lax.axis_index("x") + lax.axis_index("y")
    ln_shard = jnp.concatenate([sgu_ln_g, sgu_ln_b], axis=0)
    shards = [w_in_a[0], w_out_a[0], w_kv, w_in_b[0], w_out_b[0]]
    col_sharded = [True, False, False, True, False]
    full_in_a, *later, ln_full = _gather_weights(shards, col_sharded, [True, False, False, False, False], ln_shard)
    ln_full = jnp.transpose(ln_full, (1, 0, 2)).reshape(2, A_WIDTH)
    later_shapes = [s.shape for s in shards[1:]]
    fetch_group, fetch_token = _fetch_start(later, later_shapes, col_sharded[1:])

    def later_weights(y):
        return _fetch_wait(fetch_group, later_shapes, col_sharded[1:], y)

    groups, grads, deltas, new_m, new_v, scalars = {}, {}, {}, {}, {}, {}
    early = ("w_out_b", "w_in_b", "w_kv", "w_out_a")

    def two_dim(a):
        return a.reshape(-1, a.shape[-1])

    def send(tag, arrays, sliced):
        groups[tag], token = _exchange_start(tag, arrays, sliced)
        if tag != "a_in":
            return token
        srcs, lands = _exchange_wait("early", [groups["b"], groups["a_out"]], token)
        *reduced, packed_sum = _sum_and_swap("early", srcs[:4], lands[:4], srcs[4], lands[4])
        updates = [(two_dim(weights[n]), g, two_dim(moments_m[n]), two_dim(moments_v[n]))
                   for n, g in zip(early, reduced)]
        small_sum = _unpack_small(packed_sum)
        scalars["loss"] = small_sum["loss"].reshape(())
        shard_cols = sgu_ln_g.shape[1]
        for name in ("sgu_ln_g", "sgu_ln_b"):
            grads[name] = lax.dynamic_slice(small_sum[name], (0, shard_index * shard_cols), (1, shard_cols))
        small_names = ("sgu_ln_g", "sgu_ln_b", "w_spatial", "b_spatial", "attn_sinks", "rel_bias", "post_ln_g",
                       "post_ln_b")
        for name in small_names[2:]:
            grads[name] = small_sum[name].reshape(weights[name].shape)
        small_updates = _adamw_small([(weights[n], grads[n], moments_m[n], moments_v[n]) for n in small_names])
        for name, (d, nm, nv) in zip(small_names, small_updates):
            deltas[name], new_m[name], new_v[name] = d, nm, nv
        return packed_sum[0:_SUBLANES], updates

    grad_x, _, _, updated = _local_step(
        x[0], loss_target[0], full_in_a, later_weights, fetch_token, ln_full[0:1], ln_full[1:2], w_spatial[0],
        b_spatial[0], attn_sinks, rel_bias, post_ln_g, post_ln_b, send=send)
    for k, name in enumerate(early):
        grads[name], deltas[name], new_m[name], new_v[name] = [
            a.reshape(weights[name].shape) for a in updated[4 * k:4 * k + 4]]

    srcs, lands = _exchange_wait("late", [groups["a_in"]], grad_x)
    (g_in_a,) = _sum_and_swap("late", srcs, lands)
    grads["w_in_a"], deltas["w_in_a"], new_m["w_in_a"], new_v["w_in_a"] = _adamw(
        "w_in_a", w_in_a, g_in_a.reshape(w_in_a.shape), m_w_in_a, v_w_in_a)
    return (scalars["loss"], grad_x[None], *[grads[n] for n in order], *[deltas[n] for n in order],
            *[new_m[n] for n in order], *[new_v[n] for n in order])
```

```python
import math

import jax
import jax.numpy as jnp
from jax import lax
from jax.experimental import pallas as pl
from jax.experimental.pallas import tpu as pltpu

F32 = jnp.float32
BF16 = jnp.bfloat16

D_MODEL = 1024
A_WIDTH = 2048
A_GROUPS = 8
A_GROUP_DIM = 256
CHUNK = 128
N_HEADS = 16
N_KV = 2
HEAD_DIM = 64
PAIR = 2 * HEAD_DIM
B_WIDTH = 1024
REL_BUCKETS = 32
ALPHA = 4.0 ** 0.25
LN_EPS = 1e-5
NEG_INF = -1e30
SCALE = HEAD_DIM ** -0.5

ADAM_LR = 0.001
ADAM_B1 = 0.9
ADAM_B2 = 0.999
ADAM_EPS = 1e-08
ADAM_WD = 0.01
ADAM_STEP = 10

N_DEV = 8
N_CHIPS = 4
MESH = pl.DeviceIdType.MESH
VMEM_LIMIT = 56 * 1024 * 1024

TM_ATTN = 256
TM_MM = 512
TM_WIN = 1024
_LANES = 128
_SUBLANES = 8


def _dot(a, b):
    return jnp.dot(a, b, preferred_element_type=F32)


def _dot_nt(a, b):
    return lax.dot_general(a, b, (((1,), (1,)), ((), ())), preferred_element_type=F32)


def _dot_tn(a, b):
    return lax.dot_general(a, b, (((0,), (0,)), ((), ())), preferred_element_type=F32)


def _ln_fwd(r):
    mu = jnp.mean(r, axis=-1, keepdims=True)
    rc = r - mu
    var = jnp.mean(rc * rc, axis=-1, keepdims=True)
    rstd = lax.rsqrt(var + LN_EPS)
    return rc * rstd, rstd


def _ln_bwd(dxh, xh, rstd):
    m1 = jnp.mean(dxh, axis=-1, keepdims=True)
    m2 = jnp.mean(dxh * xh, axis=-1, keepdims=True)
    return rstd * (dxh - m1 - xh * m2)


def _silu_parts(z):
    sg = jax.nn.sigmoid(z)
    return z * sg, sg * (1.0 + z * (1.0 - sg))


def _dup_halves(blk):
    sw = pltpu.roll(blk, HEAD_DIM, 1)
    lo = lax.broadcasted_iota(jnp.int32, blk.shape, 1) < HEAD_DIM
    return jnp.where(lo, blk, sw), jnp.where(lo, sw, blk)


def _fold_halves(blk):
    return blk + pltpu.roll(blk, HEAD_DIM, 1)


def _resident(shape):
    nd = len(shape)
    return pl.BlockSpec(shape, lambda *_: (0,) * nd, pipeline_mode=pl.Buffered(1))


def _const(shape):
    nd = len(shape)
    return pl.BlockSpec(shape, lambda *_: (0,) * nd)


def _rows(tm, cols):
    return pl.BlockSpec((tm, cols), lambda i: (i, 0))


def _params(sem=("arbitrary",)):
    return pltpu.CompilerParams(dimension_semantics=sem, vmem_limit_bytes=VMEM_LIMIT)


def _spatial_mix(ws_ref, bsp_ref, vn, s_scr, n_chunks):
    tri = (lax.broadcasted_iota(jnp.int32, (CHUNK, CHUNK), 0)
           >= lax.broadcasted_iota(jnp.int32, (CHUNK, CHUNK), 1))
    for g in range(A_GROUPS):
        wsg = jnp.where(tri, ws_ref[g], 0.0).astype(BF16)
        cols = slice(g * A_GROUP_DIM, (g + 1) * A_GROUP_DIM)
        for ci in range(n_chunks):
            rows = slice(ci * CHUNK, (ci + 1) * CHUNK)
            s_scr[rows, cols] = _dot(wsg, vn[rows, cols]) + bsp_ref[:, g:g + 1]


def _layer_a_fwd(x, w_in, lng, lnb, ws, bsp_t, after):
    t_len = x.shape[0]
    tm = TM_ATTN

    def body(x_ref, win_ref, lng_ref, lnb_ref, ws_ref, bsp_ref, after_ref,
             xt_ref, u_ref, vh_ref, z_ref, rv_ref, y_ref, s_scr):
        x_t = x_ref[...]
        xb = x_t.astype(BF16)
        xt_ref[...] = x_t.T.astype(BF16)
        u = _dot(xb, win_ref[:, 0:A_WIDTH])
        v = _dot(xb, win_ref[:, A_WIDTH:2 * A_WIDTH])
        z = _dot(xb, win_ref[:, 2 * A_WIDTH:3 * A_WIDTH])
        vh, rv = _ln_fwd(v)
        vn = (vh * lng_ref[...] + lnb_ref[...]).astype(BF16)
        _spatial_mix(ws_ref, bsp_ref, vn, s_scr, tm // CHUNK)
        sz, _ = _silu_parts(z)
        y_ref[...] = (u * s_scr[...] * sz).astype(BF16)
        u_ref[...] = u.astype(BF16)
        vh_ref[...] = vh.astype(BF16)
        z_ref[...] = z.astype(BF16)
        rv_ref[...] = rv

    wide = jax.ShapeDtypeStruct((t_len, A_WIDTH), BF16)
    return pl.pallas_call(
        body, name="layer_a_fwd", grid=(t_len // tm,),
        in_specs=[_rows(tm, D_MODEL), _resident(w_in.shape), _const(lng.shape), _const(lnb.shape), _const(ws.shape),
                  _const(bsp_t.shape), _const(after.shape)],
        out_specs=[pl.BlockSpec((D_MODEL, tm), lambda i: (0, i)), _rows(tm, A_WIDTH), _rows(tm, A_WIDTH),
                   _rows(tm, A_WIDTH), _rows(tm, 1), _rows(tm, A_WIDTH)],
        out_shape=[jax.ShapeDtypeStruct((D_MODEL, t_len), BF16), wide, wide, wide,
                   jax.ShapeDtypeStruct((t_len, 1), F32), wide],
        scratch_shapes=[pltpu.VMEM((tm, A_WIDTH), F32)],
        compiler_params=_params(),
    )(x, w_in, lng, lnb, ws, bsp_t, after)


def _layer_b_proj(x, y, w_out_a, g1, b1, w_in, w_kv):
    t_len = x.shape[0]
    tm = TM_MM

    def body(x_ref, y_ref, wout_ref, g_ref, b_ref, win_ref, wkv_ref, xh_ref, r1_ref, q_ref, z_ref, kd_ref, vd_ref):
        xh, r1 = _ln_fwd(ALPHA * x_ref[...] + _dot(y_ref[...], wout_ref[...]))
        xh_ref[...] = xh
        r1_ref[...] = r1
        h1 = (xh * g_ref[...] + b_ref[...]).astype(BF16)
        q_ref[...] = (_dot(h1, win_ref[:, 0:B_WIDTH]) * SCALE).astype(BF16)
        z_ref[...] = _dot(h1, win_ref[:, B_WIDTH:2 * B_WIDTH]).astype(BF16)
        kv = _dot(h1, wkv_ref[...])
        k0, k1 = _dup_halves(kv[:, 0:PAIR])
        v0, v1 = _dup_halves(kv[:, PAIR:2 * PAIR])
        kd_ref[:, 0:PAIR] = k0.astype(BF16)
        kd_ref[:, PAIR:2 * PAIR] = k1.astype(BF16)
        vd_ref[:, 0:PAIR] = v0.astype(BF16)
        vd_ref[:, PAIR:2 * PAIR] = v1.astype(BF16)

    return pl.pallas_call(
        body, name="layer_b_proj", grid=(t_len // tm,),
        in_specs=[_rows(tm, D_MODEL), _rows(tm, A_WIDTH), _resident(w_out_a.shape), _const(g1.shape),
                  _const(b1.shape), _resident(w_in.shape), _resident(w_kv.shape)],
        out_specs=[_rows(tm, D_MODEL), _rows(tm, 1), _rows(tm, B_WIDTH), _rows(tm, B_WIDTH), _rows(tm, 2 * PAIR),
                   _rows(tm, 2 * PAIR)],
        out_shape=[jax.ShapeDtypeStruct((t_len, D_MODEL), F32), jax.ShapeDtypeStruct((t_len, 1), F32),
                   jax.ShapeDtypeStruct((t_len, B_WIDTH), BF16), jax.ShapeDtypeStruct((t_len, B_WIDTH), BF16),
                   jax.ShapeDtypeStruct((t_len, 2 * PAIR), BF16), jax.ShapeDtypeStruct((t_len, 2 * PAIR), BF16)],
        compiler_params=_params(),
    )(x, y, w_out_a, g1, b1, w_in, w_kv)


GROUP = N_HEADS // N_KV
GROUP_Q = GROUP * CHUNK


def _window_tables():
    j = jnp.arange(2 * CHUNK, dtype=jnp.int32)[:, None]
    t = jnp.arange(CHUNK, dtype=jnp.int32)[None, :]
    dist = t + CHUNK - j
    inside = (dist >= 0) & (dist < CHUNK)
    return jnp.stack([inside & (j >= CHUNK), inside]).astype(F32)


def _band(ref, chunk_index, kvh):
    prev0 = pl.multiple_of(jnp.maximum(chunk_index - 1, 0) * CHUNK, CHUNK)
    cur0 = pl.multiple_of(chunk_index * CHUNK, CHUNK)
    cols = slice(kvh * PAIR, (kvh + 1) * PAIR)
    return jnp.concatenate([ref[pl.ds(prev0, CHUNK), cols], ref[pl.ds(cur0, CHUNK), cols]], axis=0)


def _group_tables(bias_ref, win_ref, sink_ref, chunk_index, kvh):
    bias = jnp.concatenate([bias_ref[kvh * GROUP + j] for j in range(GROUP)], axis=1)
    win = win_ref[jnp.minimum(chunk_index, 1)]
    mask = jnp.concatenate([win] * GROUP, axis=1) > 0.5
    sink = jnp.concatenate([jnp.full((1, CHUNK), sink_ref[0, kvh * GROUP + j], F32) for j in range(GROUP)], axis=1)
    return bias, mask, sink


def _attn_probs(qs, kband, bias, mask, sink):
    logits = jnp.where(mask, _dot_nt(kband, qs) + bias, NEG_INF)
    m = jnp.maximum(jnp.max(logits, axis=0, keepdims=True), sink)
    e = jnp.exp(logits - m)
    es = jnp.exp(sink - m)
    inv = 1.0 / (jnp.sum(e, axis=0, keepdims=True) + es)
    return e * inv, es * inv


def _half_mask():
    return lax.broadcasted_iota(jnp.int32, (CHUNK, PAIR), 1) < HEAD_DIM


def _stack_heads(src_ref, rows, kvh, dst_scr, lo):
    for j in range(GROUP):
        h = kvh * GROUP + j
        blk = src_ref[rows, (h // 2) * PAIR:(h // 2 + 1) * PAIR].astype(F32)
        keep = lo if h % 2 == 0 else ~lo
        dst_scr[j * CHUNK:(j + 1) * CHUNK, :] = jnp.where(keep, blk, 0.0).astype(BF16)


def _probs_spec(tm):
    return pl.BlockSpec((tm // CHUNK, N_KV, 2 * CHUNK, GROUP_Q), lambda i: (i, 0, 0, 0))


def _sink_probs_spec():
    return pl.BlockSpec((1, 8, GROUP_Q), lambda i: (i, 0, 0))


def _unstack_pairs(stacked, pp, lo):
    return jnp.where(lo, stacked[(2 * pp) * CHUNK:(2 * pp + 1) * CHUNK], stacked[(2 * pp + 1) * CHUNK:(2 * pp + 2) * CHUNK])


def _layer_b_fwd(q, zb, kd, vd, bias, win, sinks, xh1, g1, b1, w_out, g2, b2, tgt):
    t_len = q.shape[0]
    tm = TM_ATTN

    def body(q_ref, z_ref, kd_ref, vd_ref, bias_ref, win_ref, sink_ref, xh_ref, g1_ref, b1_ref, wout_ref, g2_ref,
             b2_ref, tgt_ref, o_ref, p_ref, ps_ref, dr_ref, loss_ref, dg_ref, db_ref, o_scr, qs_scr):
        i = pl.program_id(0)

        @pl.when(i == 0)
        def _():
            loss_ref[...] = jnp.zeros_like(loss_ref)
            dg_ref[...] = jnp.zeros_like(dg_ref)
            db_ref[...] = jnp.zeros_like(db_ref)

        lo = _half_mask()
        ps_ref[...] = jnp.zeros_like(ps_ref)
        for ci in range(tm // CHUNK):
            cg = i * (tm // CHUNK) + ci
            rows = slice(ci * CHUNK, (ci + 1) * CHUNK)
            for kvh in range(N_KV):
                kband = _band(kd_ref, cg, kvh)
                vband = _band(vd_ref, cg, kvh)
                bias_g, mask, sink = _group_tables(bias_ref, win_ref, sink_ref, cg, kvh)
                _stack_heads(q_ref, rows, kvh, qs_scr, lo)
                p, p_sink = _attn_probs(qs_scr[...], kband, bias_g, mask, sink)
                p = p.astype(BF16)
                p_ref[ci, kvh] = p
                ps_ref[0, ci * N_KV + kvh:ci * N_KV + kvh + 1, :] = p_sink
                o_stack = _dot_tn(p, vband)
                for pp in range(GROUP // 2):
                    pair = kvh * (GROUP // 2) + pp
                    o_scr[rows, pair * PAIR:(pair + 1) * PAIR] = _unstack_pairs(o_stack, pp, lo)
        o = o_scr[...]
        o_ref[...] = o.astype(BF16)
        sz, _ = _silu_parts(z_ref[...].astype(F32))
        y = (o * sz).astype(BF16)
        h1 = xh_ref[...] * g1_ref[...] + b1_ref[...]
        r = ALPHA * h1 + _dot(y, wout_ref[...])
        xh2, rstd2 = _ln_fwd(r)
        diff = xh2 * g2_ref[...] + b2_ref[...] - tgt_ref[...]
        loss_ref[...] += jnp.sum(diff * diff, axis=0, keepdims=True)
        dh2 = diff * (1.0 / D_MODEL)
        dg_ref[...] += jnp.sum(dh2 * xh2, axis=0, keepdims=True)
        db_ref[...] += jnp.sum(dh2, axis=0, keepdims=True)
        dr_ref[...] = _ln_bwd(dh2 * g2_ref[...], xh2, rstd2)

    vec = jax.ShapeDtypeStruct((1, D_MODEL), F32)
    return pl.pallas_call(
        body, name="layer_b_fwd", grid=(t_len // tm,),
        in_specs=[_rows(tm, B_WIDTH), _rows(tm, B_WIDTH), _resident(kd.shape), _resident(vd.shape),
                  _resident(bias.shape), _resident(win.shape), pl.BlockSpec(memory_space=pltpu.SMEM),
                  _rows(tm, D_MODEL), _const(g1.shape), _const(b1.shape), _resident(w_out.shape), _const(g2.shape),
                  _const(b2.shape), _rows(tm, D_MODEL)],
        out_specs=[_rows(tm, B_WIDTH), _probs_spec(tm), _sink_probs_spec(), _rows(tm, D_MODEL)]
        + [_const((1, D_MODEL))] * 3,
        out_shape=[jax.ShapeDtypeStruct((t_len, B_WIDTH), BF16),
                   jax.ShapeDtypeStruct((t_len // CHUNK, N_KV, 2 * CHUNK, GROUP_Q), BF16),
                   jax.ShapeDtypeStruct((t_len // tm, 8, GROUP_Q), F32),
                   jax.ShapeDtypeStruct((t_len, D_MODEL), F32), vec, vec, vec],
        scratch_shapes=[pltpu.VMEM((tm, B_WIDTH), F32), pltpu.VMEM((GROUP_Q, PAIR), BF16)],
        compiler_params=_params(),
    )(q, zb, kd, vd, bias, win, sinks, xh1, g1, b1, w_out, g2, b2, tgt)


def _layer_b_bwd_attn(dr2, zb, o, q, kd, vd, probs, sink_probs, w_out):
    t_len = q.shape[0]
    tm = TM_ATTN
    n_steps = t_len // tm
    n_chunks = tm // CHUNK

    def body(dr_ref, z_ref, o_ref, q_ref, kd_ref, vd_ref, p_ref, ps_ref, wout_ref,
             dq_ref, dz_ref, dkd_ref, dvd_ref, ck_ref, cv_ref, gw_ref, dsink_ref, dbias_ref,
             do_scr, qs_scr, dos_scr, gw_acc):
        i = pl.program_id(0)

        @pl.when(i == 0)
        def _():
            gw_acc[...] = jnp.zeros_like(gw_acc)
            dsink_ref[...] = jnp.zeros_like(dsink_ref)
            dbias_ref[...] = jnp.zeros_like(dbias_ref)

        drb = dr_ref[...].astype(BF16)
        dy = _dot_nt(drb, wout_ref[...])
        z = z_ref[...].astype(F32)
        sz, dsz = _silu_parts(z)
        o_t = o_ref[...].astype(F32)
        dz_ref[...] = (dy * o_t * dsz).astype(BF16)
        do_scr[...] = (dy * sz).astype(BF16)
        gw_acc[...] += _dot_tn((o_t * sz).astype(BF16), drb)

        lo = _half_mask()
        for kvh in range(N_KV):
            kcols = slice(kvh * PAIR, (kvh + 1) * PAIR)
            dk_bands, dv_bands = [], []
            for ci in range(n_chunks):
                cg = i * n_chunks + ci
                rows = slice(ci * CHUNK, (ci + 1) * CHUNK)
                kband = _band(kd_ref, cg, kvh)
                vband = _band(vd_ref, cg, kvh)
                _stack_heads(q_ref, rows, kvh, qs_scr, lo)
                _stack_heads(do_scr, rows, kvh, dos_scr, lo)
                qs = qs_scr[...]
                dos = dos_scr[...]
                pb = p_ref[ci, kvh]
                p = pb.astype(F32)
                p_sink = ps_ref[0, ci * N_KV + kvh:ci * N_KV + kvh + 1, :]
                dp = _dot_nt(vband, dos)
                delta = jnp.sum(p * dp, axis=0, keepdims=True)
                dlog = p * (dp - delta)
                for j in range(GROUP):
                    dbias_ref[kvh * GROUP + j] += dlog[:, j * CHUNK:(j + 1) * CHUNK]
                dsink_ref[kvh:kvh + 1, :] += -(p_sink * delta)
                ds = dlog.astype(BF16)
                dq_stack = _dot_tn(ds, kband) * SCALE
                for pp in range(GROUP // 2):
                    pair = kvh * (GROUP // 2) + pp
                    dq_ref[rows, pair * PAIR:(pair + 1) * PAIR] = _unstack_pairs(dq_stack, pp, lo).astype(BF16)
                dk_bands.append(_dot(ds, qs))
                dv_bands.append(_dot(pb, dos))
            for bands, out_ref, carry_ref in ((dk_bands, dkd_ref, ck_ref), (dv_bands, dvd_ref, cv_ref)):
                carry_ref[0, :, kcols] = bands[0][0:CHUNK]
                for ci in range(n_chunks):
                    own = bands[ci][CHUNK:2 * CHUNK]
                    if ci + 1 < n_chunks:
                        own = own + bands[ci + 1][0:CHUNK]
                    out_ref[ci * CHUNK:(ci + 1) * CHUNK, kcols] = own

        @pl.when(i == n_steps - 1)
        def _():
            gw_ref[...] = gw_acc[...].astype(BF16)

    carry_spec = pl.BlockSpec((1, CHUNK, 2 * PAIR), lambda i: (i, 0, 0))
    carry_shape = jax.ShapeDtypeStruct((n_steps, CHUNK, 2 * PAIR), F32)
    bias_shape = (N_HEADS, 2 * CHUNK, CHUNK)
    return pl.pallas_call(
        body, name="layer_b_bwd_attn", grid=(n_steps,),
        in_specs=[_rows(tm, D_MODEL), _rows(tm, B_WIDTH), _rows(tm, B_WIDTH), _rows(tm, B_WIDTH),
                  _resident(kd.shape), _resident(vd.shape), _probs_spec(tm), _sink_probs_spec(),
                  _resident(w_out.shape)],
        out_specs=[_rows(tm, B_WIDTH), _rows(tm, B_WIDTH), _rows(tm, 2 * PAIR), _rows(tm, 2 * PAIR),
                   carry_spec, carry_spec, _const(w_out.shape), _const((N_KV, GROUP_Q)), _const(bias_shape)],
        out_shape=[jax.ShapeDtypeStruct((t_len, B_WIDTH), BF16), jax.ShapeDtypeStruct((t_len, B_WIDTH), BF16),
                   jax.ShapeDtypeStruct((t_len, 2 * PAIR), F32), jax.ShapeDtypeStruct((t_len, 2 * PAIR), F32),
                   carry_shape, carry_shape, jax.ShapeDtypeStruct(w_out.shape, BF16),
                   jax.ShapeDtypeStruct((N_KV, GROUP_Q), F32), jax.ShapeDtypeStruct(bias_shape, F32)],
        scratch_shapes=[pltpu.VMEM((tm, B_WIDTH), BF16), pltpu.VMEM((GROUP_Q, PAIR), BF16),
                        pltpu.VMEM((GROUP_Q, PAIR), BF16), pltpu.VMEM(w_out.shape, F32)],
        compiler_params=_params(),
    )(dr2, zb, o, q, kd, vd, probs, sink_probs, w_out)


def _layer_b_bwd_proj(xh1, rstd1, g1, b1, dr2, dq, dzb, dkd, dvd, carry_k, carry_v, w_in, w_kv):
    t_len = xh1.shape[0]
    tm = TM_MM
    n_steps = t_len // tm
    per_tile = tm // TM_ATTN
    n_carry = carry_k.shape[0]

    def body(xh_ref, rstd_ref, g_ref, b_ref, dr2_ref, dq_ref, dz_ref, dkd_ref, dvd_ref, *rest):
        carry_refs = rest[:2 * per_tile]
        win_ref, wkv_ref, dr1_ref, dg_ref, db_ref, gwin_ref, gwkv_ref, acc_in, acc_kv = rest[2 * per_tile:]
        i = pl.program_id(0)

        @pl.when(i == 0)
        def _():
            acc_in[...] = jnp.zeros_like(acc_in)
            acc_kv[...] = jnp.zeros_like(acc_kv)
            dg_ref[...] = jnp.zeros_like(dg_ref)
            db_ref[...] = jnp.zeros_like(db_ref)

        lo = lax.broadcasted_iota(jnp.int32, (tm, PAIR), 1) < HEAD_DIM

        def heads_gradient(tile_ref, refs):
            parts = []
            for a in range(per_tile):
                parts.append(tile_ref[a * TM_ATTN:(a + 1) * TM_ATTN - CHUNK, :])
                carry = refs[a][0]
                if a == per_tile - 1:
                    carry = jnp.where(i < n_steps - 1, carry, 0.0)
                parts.append(tile_ref[(a + 1) * TM_ATTN - CHUNK:(a + 1) * TM_ATTN, :] + carry)
            dup = jnp.concatenate(parts, axis=0)
            return jnp.where(lo, _fold_halves(dup[:, 0:PAIR]), _fold_halves(dup[:, PAIR:2 * PAIR]))

        xh = xh_ref[...]
        h1 = (xh * g_ref[...] + b_ref[...]).astype(BF16)
        dq_t = dq_ref[...]
        dz_t = dz_ref[...]
        dkv = jnp.concatenate([heads_gradient(dkd_ref, carry_refs[:per_tile]),
                               heads_gradient(dvd_ref, carry_refs[per_tile:])], axis=1).astype(BF16)
        dh1 = ALPHA * dr2_ref[...]
        dh1 += _dot_nt(dq_t, win_ref[:, 0:B_WIDTH])
        dh1 += _dot_nt(dz_t, win_ref[:, B_WIDTH:2 * B_WIDTH])
        dh1 += _dot_nt(dkv, wkv_ref[...])
        acc_in[:, 0:B_WIDTH] += _dot_tn(h1, dq_t)
        acc_in[:, B_WIDTH:2 * B_WIDTH] += _dot_tn(h1, dz_t)
        acc_kv[...] += _dot_tn(h1, dkv)
        dg_ref[...] += jnp.sum(dh1 * xh, axis=0, keepdims=True)
        db_ref[...] += jnp.sum(dh1, axis=0, keepdims=True)
        dr1_ref[...] = _ln_bwd(dh1 * g_ref[...], xh, rstd_ref[...])

        @pl.when(i == n_steps - 1)
        def _():
            half_rows = D_MODEL // 2
            shard_cols = 2 * B_WIDTH // N_CHIPS
            for s in range(N_CHIPS):
                for c in range(2):
                    gwin_ref[2 * s + c] = acc_in[c * half_rows:(c + 1) * half_rows,
                                                 s * shard_cols:(s + 1) * shard_cols].astype(BF16)
            gwkv_ref[...] = acc_kv[...].astype(BF16)

    vec = jax.ShapeDtypeStruct((1, D_MODEL), F32)
    gwin_shape = (N_DEV, D_MODEL // 2, 2 * B_WIDTH // N_CHIPS)

    def carry_spec(a):
        return pl.BlockSpec((1, CHUNK, 2 * PAIR), lambda i: (jnp.minimum(per_tile * i + a + 1, n_carry - 1), 0, 0))

    carry_specs = [carry_spec(a) for a in range(per_tile)]
    return pl.pallas_call(
        body, name="layer_b_bwd_proj", grid=(n_steps,),
        in_specs=[_rows(tm, D_MODEL), _rows(tm, 1), _const(g1.shape), _const(b1.shape), _rows(tm, D_MODEL),
                  _rows(tm, B_WIDTH), _rows(tm, B_WIDTH), _rows(tm, 2 * PAIR), _rows(tm, 2 * PAIR)]
        + carry_specs + carry_specs + [_resident(w_in.shape), _resident(w_kv.shape)],
        out_specs=[_rows(tm, D_MODEL), _const((1, D_MODEL)), _const((1, D_MODEL)), _const(gwin_shape),
                   _const(w_kv.shape)],
        out_shape=[jax.ShapeDtypeStruct((t_len, D_MODEL), F32), vec, vec,
                   jax.ShapeDtypeStruct(gwin_shape, BF16), jax.ShapeDtypeStruct(w_kv.shape, BF16)],
        scratch_shapes=[pltpu.VMEM(w_in.shape, F32), pltpu.VMEM(w_kv.shape, F32)],
        compiler_params=_params(),
    )(xh1, rstd1, g1, b1, dr2, dq, dzb, dkd, dvd, *([carry_k] * per_tile), *([carry_v] * per_tile), w_in, w_kv)


def _layer_a_bwd_mix(dr1, u, vh, z, y, rv, w_out, lng, lnb, ws, bsp_t, after):
    t_len = u.shape[0]
    tm = TM_ATTN
    n_steps = t_len // tm

    def body(dr_ref, u_ref, vh_ref, z_ref, y_ref, rv_ref, wout_ref, lng_ref, lnb_ref, ws_ref, bsp_ref, after_ref,
             dp_ref, gw_ref, dws_ref, dbsp_ref, dgs_ref, dbs_ref, s_scr, dvn_scr, gw_acc):
        i = pl.program_id(0)

        @pl.when(i == 0)
        def _():
            gw_acc[...] = jnp.zeros_like(gw_acc)
            dws_ref[...] = jnp.zeros_like(dws_ref)
            dbsp_ref[...] = jnp.zeros_like(dbsp_ref)
            dgs_ref[...] = jnp.zeros_like(dgs_ref)
            dbs_ref[...] = jnp.zeros_like(dbs_ref)

        drb = dr_ref[...].astype(BF16)
        dy = _dot_nt(drb, wout_ref[...])
        vh_t = vh_ref[...].astype(F32)
        vn = (vh_t * lng_ref[...] + lnb_ref[...]).astype(BF16)
        _spatial_mix(ws_ref, bsp_ref, vn, s_scr, tm // CHUNK)
        gw_acc[...] += _dot_tn(y_ref[...], drb)
        s = s_scr[...]
        sz, dsz = _silu_parts(z_ref[...].astype(F32))
        t = dy * u_ref[...].astype(F32)
        dp_ref[:, 0:A_WIDTH] = (dy * (s * sz)).astype(BF16)
        dp_ref[:, 2 * A_WIDTH:3 * A_WIDTH] = (t * s * dsz).astype(BF16)
        ds = (t * sz).astype(BF16)

        group_of = lax.broadcasted_iota(jnp.int32, (A_WIDTH, CHUNK), 0) // A_GROUP_DIM
        indicator = (group_of == lax.broadcasted_iota(jnp.int32, (A_WIDTH, CHUNK), 1)).astype(BF16)
        group_sums = _dot(ds, indicator)
        total = group_sums[0:CHUNK]
        for ci in range(1, tm // CHUNK):
            total += group_sums[ci * CHUNK:(ci + 1) * CHUNK]
        dbsp_ref[...] += total

        tri = (lax.broadcasted_iota(jnp.int32, (CHUNK, CHUNK), 0)
               >= lax.broadcasted_iota(jnp.int32, (CHUNK, CHUNK), 1))
        for g in range(A_GROUPS):
            wsg = jnp.where(tri, ws_ref[g], 0.0).astype(BF16)
            cols = slice(g * A_GROUP_DIM, (g + 1) * A_GROUP_DIM)
            dws_g = jnp.zeros((CHUNK, CHUNK), F32)
            for ci in range(tm // CHUNK):
                rows = slice(ci * CHUNK, (ci + 1) * CHUNK)
                ds_b = ds[rows, cols]
                dws_g += _dot_nt(ds_b, vn[rows, cols])
                dvn_scr[rows, cols] = _dot_tn(wsg, ds_b)
            dws_ref[g] += jnp.where(tri, dws_g, 0.0)
        dvn = dvn_scr[...]
        dgs_ref[...] += jnp.sum(dvn * vh_t, axis=0, keepdims=True)
        dbs_ref[...] += jnp.sum(dvn, axis=0, keepdims=True)
        dp_ref[:, A_WIDTH:2 * A_WIDTH] = _ln_bwd(dvn * lng_ref[...], vh_t, rv_ref[...]).astype(BF16)

        @pl.when(i == n_steps - 1)
        def _():
            gw_ref[...] = gw_acc[...].astype(BF16)

    wide = jax.ShapeDtypeStruct((1, A_WIDTH), F32)
    return pl.pallas_call(
        body, name="layer_a_bwd_mix", grid=(n_steps,),
        in_specs=[_rows(tm, D_MODEL), _rows(tm, A_WIDTH), _rows(tm, A_WIDTH), _rows(tm, A_WIDTH), _rows(tm, A_WIDTH),
                  _rows(tm, 1), _resident(w_out.shape), _const(lng.shape), _const(lnb.shape), _const(ws.shape),
                  _const(bsp_t.shape), _const(after.shape)],
        out_specs=[_rows(tm, 3 * A_WIDTH), _const(w_out.shape), _const(ws.shape), _const((CHUNK, CHUNK)),
                   _const((1, A_WIDTH)), _const((1, A_WIDTH))],
        out_shape=[jax.ShapeDtypeStruct((t_len, 3 * A_WIDTH), BF16), jax.ShapeDtypeStruct(w_out.shape, BF16),
                   jax.ShapeDtypeStruct(ws.shape, F32), jax.ShapeDtypeStruct((CHUNK, CHUNK), F32),
                   wide, wide],
        scratch_shapes=[pltpu.VMEM((tm, A_WIDTH), F32), pltpu.VMEM((tm, A_WIDTH), F32),
                        pltpu.VMEM(w_out.shape, F32)],
        compiler_params=_params(),
    )(dr1, u, vh, z, y, rv, w_out, lng, lnb, ws, bsp_t, after)


def _layer_a_bwd_dx(dr1, dp, w_in, after, updates=()):
    t_len = dr1.shape[0]
    tm = TM_MM
    n_steps = t_len // tm
    n_upd = len(updates)

    def body(dr_ref, dp_ref, win_ref, after_ref, *refs):
        upd_in, dx_ref, upd_out = refs[:4 * n_upd], refs[4 * n_upd], refs[4 * n_upd + 1:]
        dx_ref[...] = ALPHA * dr_ref[...] + _dot_nt(dp_ref[...], win_ref[...])
        for k in range(n_upd):
            w_ref, g_ref, m_ref, v_ref = upd_in[4 * k:4 * k + 4]
            g_out, d_ref, nm_ref, nv_ref = upd_out[4 * k:4 * k + 4]
            g_out[...] = g_ref[...]
            _adamw_update(w_ref, g_ref, m_ref, v_ref, d_ref, nm_ref, nv_ref)

    upd_specs, upd_shapes, upd_args = [], [], []
    for w, g, m, v in updates:
        rows, cols = w.shape
        upd_specs.append(pl.BlockSpec((rows // n_steps, cols), lambda i: (i, 0)))
        upd_shapes.append(jax.ShapeDtypeStruct((rows, cols), F32))
        upd_args += [w, g, m, v]
    return pl.pallas_call(
        body, name="layer_a_bwd_dx", grid=(n_steps,),
        in_specs=[_rows(tm, D_MODEL), _rows(tm, 3 * A_WIDTH), _resident(w_in.shape), _const(after.shape)]
        + [s for s in upd_specs for _ in range(4)],
        out_specs=[_rows(tm, D_MODEL)] + [s for s in upd_specs for _ in range(4)],
        out_shape=[jax.ShapeDtypeStruct((t_len, D_MODEL), F32)] + [s for s in upd_shapes for _ in range(4)],
        compiler_params=_params(),
    )(dr1, dp, w_in, after, *upd_args)


def _layer_a_bwd_win(xt, dp, after):
    t_len = xt.shape[1]
    tm = TM_WIN
    n_steps = t_len // tm
    shard_cols = 3 * A_WIDTH // N_CHIPS
    half_rows = D_MODEL // 2

    def body(xt_ref, dp_ref, after_ref, gw_ref, acc):
        i = pl.program_id(1)

        @pl.when(i == 0)
        def _():
            acc[...] = jnp.zeros_like(acc)

        acc[...] += _dot(xt_ref[...], dp_ref[...])

        @pl.when(i == n_steps - 1)
        def _():
            for c in range(2):
                gw_ref[0, c] = acc[c * half_rows:(c + 1) * half_rows, :].astype(BF16)

    return pl.pallas_call(
        body, name="layer_a_bwd_win", grid=(N_CHIPS, n_steps),
        in_specs=[pl.BlockSpec((D_MODEL, tm), lambda j, i: (0, i)),
                  pl.BlockSpec((tm, shard_cols), lambda j, i: (i, j)), _const(after.shape)],
        out_specs=pl.BlockSpec((1, 2, half_rows, shard_cols), lambda j, i: (j, 0, 0, 0)),
        out_shape=jax.ShapeDtypeStruct((N_CHIPS, 2, half_rows, shard_cols), BF16),
        scratch_shapes=[pltpu.VMEM((D_MODEL, shard_cols), F32)],
        compiler_params=_params(("arbitrary", "arbitrary")),
    )(xt, dp, after)


def _bucket_onehot():
    t = jnp.arange(CHUNK, dtype=jnp.int32)[None, :]
    j = jnp.arange(2 * CHUNK, dtype=jnp.int32)[:, None]
    dist = jnp.clip(t + CHUNK - j, 0, CHUNK - 1)
    max_exact = REL_BUCKETS // 2
    df = jnp.maximum(dist, 1).astype(F32)
    large = max_exact + (jnp.log(df / max_exact) / math.log(CHUNK / max_exact)
                         * (REL_BUCKETS - max_exact)).astype(jnp.int32)
    bucket = jnp.where(dist < max_exact, dist, jnp.minimum(large, REL_BUCKETS - 1))
    onehot = bucket.reshape(1, -1) == jnp.arange(REL_BUCKETS, dtype=jnp.int32)[:, None]
    return onehot.astype(F32)


def _bias_expand(rel_t, onehot):
    def body(rel_ref, oh_ref, out_ref):
        out_ref[...] = jnp.dot(rel_ref[...], oh_ref[...], preferred_element_type=F32,
                               precision=lax.Precision.HIGHEST)

    return pl.pallas_call(
        body, name="bias_expand",
        out_shape=jax.ShapeDtypeStruct((N_HEADS, onehot.shape[1]), F32),
    )(rel_t, onehot)


def _bias_reduce(onehot, dbias):
    def body(oh_ref, db_ref, out_ref):
        out_ref[...] = lax.dot_general(oh_ref[...], db_ref[...], (((1,), (1,)), ((), ())),
                                       preferred_element_type=F32, precision=lax.Precision.HIGHEST)

    return pl.pallas_call(
        body, name="bias_reduce",
        out_shape=jax.ShapeDtypeStruct((REL_BUCKETS, N_HEADS), F32),
    )(onehot, dbias)


def _place():
    return lax.axis_index("x"), lax.axis_index("y"), lax.axis_index("c")


def _shard_window(full_ref, shard_shape, col_sharded, s, half):
    rows, cols = shard_shape
    if col_sharded:
        rsel = pl.ds(0, rows) if half is None else pl.ds(half * (rows // 2), rows // 2)
        return full_ref.at[rsel, pl.ds(s * cols, cols)]
    if half is None:
        return full_ref.at[pl.ds(s * rows, rows), :]
    return full_ref.at[pl.ds(s * rows + half * (rows // 2), rows // 2), :]


def _other_chips(x, y):
    return [(1 - x, y), (x, 1 - y), (1 - x, 1 - y)]


def _gather_weights(shards, col_sharded, fetch, ln_shard):
    n_w = len(shards)
    fetched = [w for w in range(n_w) if fetch[w]]
    full_shapes = []
    for w, cs in zip(shards, col_sharded):
        r, c = w.shape
        full_shapes.append((r, c * N_CHIPS) if cs else (r * N_CHIPS, c))

    def body(*refs):
        in_refs = refs[:n_w]
        ln_ref = refs[n_w]
        full_refs = refs[n_w + 1:2 * n_w + 1]
        ln_full = refs[2 * n_w + 1]
        stage = refs[2 * n_w + 2:3 * n_w + 2]
        send_sems, recv_sems, local_sems, ln_send, ln_recv = refs[3 * n_w + 2:]
        x, y, c = _place()
        s_me = 2 * x + y
        chips = _other_chips(x, y)

        def shard_window(w, s, half):
            return _shard_window(full_refs[w], shards[w].shape, col_sharded[w], s, half)

        def stage_half(w, half):
            rows = shards[w].shape[0]
            return stage[w].at[pl.ds(half * (rows // 2), rows // 2), :]

        def ici_copy(w, k, sender_shard, src):
            return pltpu.make_async_remote_copy(
                src_ref=src, dst_ref=shard_window(w, sender_shard, c),
                send_sem=send_sems.at[w * 3 + k], recv_sem=recv_sems.at[w * 3 + k],
                device_id=(*chips[k], c), device_id_type=MESH)

        def d2d_copy(w, k, half):
            s_k = 2 * chips[k][0] + chips[k][1]
            win = shard_window(w, s_k, half)
            return pltpu.make_async_remote_copy(
                src_ref=win, dst_ref=win,
                send_sem=send_sems.at[3 * n_w + w * 3 + k], recv_sem=recv_sems.at[3 * n_w + w * 3 + k],
                device_id=(x, y, 1 - c), device_id_type=MESH)

        def ln_copy(k, slot):
            return pltpu.make_async_remote_copy(
                src_ref=ln_ref, dst_ref=ln_full.at[slot], send_sem=ln_send.at[k], recv_sem=ln_recv.at[k],
                device_id=(*chips[k], c), device_id_type=MESH)

        for w in range(n_w):
            stage[w][...] = in_refs[w][...].astype(BF16)
        own = [pltpu.make_async_copy(stage[w], shard_window(w, s_me, None), local_sems.at[w]) for w in range(n_w)]
        for cp in own:
            cp.start()
        ln_full[s_me] = ln_ref[...]
        def shard_of(k):
            return 2 * chips[k][0] + chips[k][1]

        relay_from = jnp.where(c == 0, shard_of(0), shard_of(1))
        relay_to = (jnp.where(c == 0, x, 1 - x), jnp.where(c == 0, 1 - y, y), c)

        def relay_copy(w, sender_shard):
            win = shard_window(w, sender_shard, c)
            return pltpu.make_async_remote_copy(
                src_ref=win, dst_ref=win, send_sem=send_sems.at[w * 3 + 2], recv_sem=recv_sems.at[w * 3 + 2],
                device_id=relay_to, device_id_type=MESH)

        first = [ici_copy(w, k, s_me, stage_half(w, c)) for w in fetched for k in range(2)]
        first += [ln_copy(k, s_me) for k in range(3)]
        for cp in first:
            cp.start()
        passed = []
        for w in fetched:
            for k in range(2):
                ici_copy(w, k, shard_of(k), stage_half(w, c)).wait_recv()
            relay = relay_copy(w, relay_from)
            relay.start()
            passed.append(relay)
            for k in range(2):
                fwd = d2d_copy(w, k, c)
                fwd.start()
                passed.append(fwd)
        for w in fetched:
            relay_copy(w, shard_of(2)).wait_recv()
            fwd = d2d_copy(w, 2, c)
            fwd.start()
            passed.append(fwd)
        for w in fetched:
            for k in range(3):
                d2d_copy(w, k, 1 - c).wait_recv()
        for k in range(3):
            ln_copy(k, 2 * chips[k][0] + chips[k][1]).wait_recv()
        for cp in first + passed:
            cp.wait_send()
        for cp in own:
            cp.wait()

    vmem = pl.BlockSpec(memory_space=pltpu.VMEM)
    hbm = pl.BlockSpec(memory_space=pl.ANY)
    return pl.pallas_call(
        body, name="gather_weights",
        in_specs=[vmem] * (n_w + 1),
        out_specs=[hbm] * n_w + [vmem],
        out_shape=[jax.ShapeDtypeStruct(s, BF16) for s in full_shapes]
        + [jax.ShapeDtypeStruct((N_CHIPS,) + ln_shard.shape, F32)],
        scratch_shapes=[pltpu.VMEM(w.shape, BF16) for w in shards]
        + [pltpu.SemaphoreType.DMA((6 * n_w,)), pltpu.SemaphoreType.DMA((6 * n_w,)),
           pltpu.SemaphoreType.DMA((n_w,)), pltpu.SemaphoreType.DMA((3,)), pltpu.SemaphoreType.DMA((3,))],
        compiler_params=pltpu.CompilerParams(vmem_limit_bytes=VMEM_LIMIT),
    )(*shards, ln_shard)


def _fetch_copy(full_ref, shard_shape, col_sharded, sender_shard, send_sems, recv_sems, idx, chip, c):
    win = _shard_window(full_ref, shard_shape, col_sharded, sender_shard, None)
    return pltpu.make_async_remote_copy(src_ref=win, dst_ref=win, send_sem=send_sems.at[idx],
                                        recv_sem=recv_sems.at[idx], device_id=(*chip, c), device_id_type=MESH)


def _fetch_start(fulls, shard_shapes, col_sharded):
    n = len(fulls)

    def body(*refs):
        full = refs[:n]
        send_sems, recv_sems = refs[n], refs[n + 1]
        token = refs[-1]
        x, y, c = _place()
        for w in range(n):
            for k, chip in enumerate(_other_chips(x, y)):
                _fetch_copy(full[w], shard_shapes[w], col_sharded[w], 2 * x + y, send_sems, recv_sems, w * 3 + k,
                            chip, c).start()
        token[...] = jnp.zeros_like(token)

    outs = pl.pallas_call(
        body, name="fetch_start",
        out_shape=(pltpu.SemaphoreType.DMA((3 * n,)), pltpu.SemaphoreType.DMA((3 * n,)),
                   *[pltpu.HBM(f.shape, f.dtype) for f in fulls], jax.ShapeDtypeStruct((8, 128), F32)),
        in_specs=[_HBM] * n,
        out_specs=(_SEM, _SEM, *([_HBM] * n), pl.BlockSpec(memory_space=pltpu.VMEM)),
        input_output_aliases={i: 2 + i for i in range(n)},
        compiler_params=pltpu.CompilerParams(has_side_effects=pltpu.SideEffectType.DATAFLOW_SIDE_EFFECTING),
    )(*[pltpu.with_memory_space_constraint(f, pltpu.HBM) for f in fulls])
    return dict(send=outs[0], recv=outs[1], full=list(outs[2:2 + n])), outs[-1]


def _fetch_wait(group, shard_shapes, col_sharded, after):
    n = len(group["full"])

    def body(*refs):
        full = refs[:n]
        send_sems, recv_sems = refs[n], refs[n + 1]
        x, y, c = _place()
        for w in range(n):
            for k, chip in enumerate(_other_chips(x, y)):
                _fetch_copy(full[w], shard_shapes[w], col_sharded[w], 2 * x + y, send_sems, recv_sems, w * 3 + k,
                            chip, c).wait_send()
                _fetch_copy(full[w], shard_shapes[w], col_sharded[w], 2 * chip[0] + chip[1], send_sems, recv_sems,
                            w * 3 + k, chip, c).wait_recv()

    outs = pl.pallas_call(
        body, name="fetch_wait", out_shape=tuple(pltpu.HBM(f.shape, f.dtype) for f in group["full"]),
        in_specs=[_HBM] * n + [_SEM, _SEM, pl.BlockSpec(memory_space=pl.ANY)],
        out_specs=tuple([_HBM] * n), input_output_aliases={i: i for i in range(n)},
        compiler_params=pltpu.CompilerParams(has_side_effects=pltpu.SideEffectType.DATAFLOW_SIDE_EFFECTING),
    )(*group["full"], group["send"], group["recv"], after)
    return list(outs)


_HBM = pl.BlockSpec(memory_space=pltpu.HBM)
_SEM = pl.BlockSpec(memory_space=pltpu.SEMAPHORE)
_N_PEER = N_DEV - 1


def _peer(x, y, c, k):
    return (x + (k >> 2)) % 2, (y + ((k >> 1) & 1)) % 2, (c + (k & 1)) % 2


def _exchange_copy(src_ref, land_ref, sliced, send_sems, recv_sems, idx, x, y, c, k):
    px, py, pc = _peer(x, y, c, k)
    src = src_ref.at[4 * px + 2 * py + pc] if sliced else src_ref
    return pltpu.make_async_remote_copy(
        src_ref=src, dst_ref=land_ref.at[4 * x + 2 * y + c],
        send_sem=send_sems.at[idx], recv_sem=recv_sems.at[idx], device_id=(px, py, pc), device_id_type=MESH)


def _exchange_start(tag, arrays, sliced):
    n = len(arrays)
    lands = [lax.empty(a.shape if s else (N_DEV,) + a.shape, a.dtype) for a, s in zip(arrays, sliced)]

    def body(*refs):
        src, land = refs[:n], refs[n:2 * n]
        send_sems, recv_sems = refs[2 * n], refs[2 * n + 1]
        token = refs[-1]
        x, y, c = _place()
        for w in range(n):
            for k in range(1, N_DEV):
                _exchange_copy(src[w], land[w], sliced[w], send_sems, recv_sems, w * _N_PEER + k - 1, x, y, c, k).start()
        token[...] = jnp.zeros_like(token)

    outs = pl.pallas_call(
        body, name="exchange_start_" + tag,
        out_shape=(pltpu.SemaphoreType.DMA((n * _N_PEER,)), pltpu.SemaphoreType.DMA((n * _N_PEER,)),
                   *[pltpu.HBM(a.shape, a.dtype) for a in arrays], *[pltpu.HBM(l.shape, l.dtype) for l in lands],
                   jax.ShapeDtypeStruct((8, 128), F32)),
        in_specs=[_HBM] * (2 * n),
        out_specs=(_SEM, _SEM, *([_HBM] * (2 * n)), pl.BlockSpec(memory_space=pltpu.VMEM)),
        input_output_aliases={i: 2 + i for i in range(2 * n)},
        compiler_params=pltpu.CompilerParams(has_side_effects=pltpu.SideEffectType.DATAFLOW_SIDE_EFFECTING),
    )(*[pltpu.with_memory_space_constraint(a, pltpu.HBM) for a in arrays],
      *[pltpu.with_memory_space_constraint(l, pltpu.HBM) for l in lands])
    return dict(send=outs[0], recv=outs[1], src=list(outs[2:2 + n]), land=list(outs[2 + n:2 + 2 * n]),
                sliced=list(sliced)), outs[-1]


def _exchange_wait(tag, groups, after):
    counts = [len(g["src"]) for g in groups]
    total = sum(counts)

    def body(*refs):
        pos = 0
        x, y, c = _place()
        for g, n in zip(groups, counts):
            src, land = refs[pos:pos + n], refs[pos + n:pos + 2 * n]
            send_sems, recv_sems = refs[pos + 2 * n], refs[pos + 2 * n + 1]
            pos += 2 * n + 2
            for w in range(n):
                for k in range(1, N_DEV):
                    cp = _exchange_copy(src[w], land[w], g["sliced"][w], send_sems, recv_sems,
                                        w * _N_PEER + k - 1, x, y, c, k)
                    cp.wait_send()
                    cp.wait_recv()

    operands, in_specs, aliases, out_shape = [], [], {}, []
    for g in groups:
        for a in g["src"] + g["land"]:
            aliases[len(operands)] = len(out_shape)
            out_shape.append(pltpu.HBM(a.shape, a.dtype))
            operands.append(a)
            in_specs.append(_HBM)
        operands += [g["send"], g["recv"]]
        in_specs += [_SEM, _SEM]
    operands.append(after)
    in_specs.append(pl.BlockSpec(memory_space=pl.ANY))
    outs = pl.pallas_call(
        body, name="exchange_wait_" + tag, out_shape=tuple(out_shape), in_specs=in_specs,
        out_specs=tuple([_HBM] * (2 * total)), input_output_aliases=aliases,
        compiler_params=pltpu.CompilerParams(has_side_effects=pltpu.SideEffectType.DATAFLOW_SIDE_EFFECTING),
    )(*operands)
    srcs, lands, pos = [], [], 0
    for n in counts:
        srcs += list(outs[pos:pos + n])
        lands += list(outs[pos + n:pos + 2 * n])
        pos += 2 * n
    return srcs, lands


def _sum_and_swap(tag, pieces, lands, small=None, small_land=None):
    n_w = len(pieces)
    n_small = 0 if small is None else 1

    def body(*refs):
        g_refs, land_refs = refs[:n_w], refs[n_w:2 * n_w]
        pos = 2 * n_w + 2 * n_small
        out_refs = refs[pos:pos + n_w]
        pos += n_w + n_small
        bufs = refs[pos:pos + n_w]
        load_sems, swap_send, swap_recv = refs[pos + n_w + 2 * n_small:]
        x, y, c = _place()
        me = 4 * x + 2 * y + c

        def slot(k):
            px, py, pc = _peer(x, y, c, k)
            return 4 * px + 2 * py + pc

        def swap_copy(w, half):
            rows = pieces[w].shape[1]
            win = out_refs[w].at[pl.ds(pl.multiple_of(half * rows, rows), rows), :]
            return pltpu.make_async_remote_copy(
                src_ref=win, dst_ref=win, send_sem=swap_send.at[w], recv_sem=swap_recv.at[w],
                device_id=(x, y, 1 - c), device_id_type=MESH)

        loads = []
        for w in range(n_w):
            per_w = [pltpu.make_async_copy(g_refs[w].at[me], bufs[w].at[me], load_sems.at[w * N_DEV])]
            per_w += [pltpu.make_async_copy(land_refs[w].at[slot(k)], bufs[w].at[slot(k)], load_sems.at[w * N_DEV + k])
                      for k in range(1, N_DEV)]
            loads.append(per_w)
        small_loads = []
        if n_small:
            small_ref, small_land_ref = refs[2 * n_w], refs[2 * n_w + 1]
            small_out = refs[2 * n_w + 2 + n_w]
            small_buf, small_sems = refs[pos + n_w], refs[pos + n_w + 1]
            small_loads = [pltpu.make_async_copy(small_land_ref.at[slot(k)], small_buf.at[slot(k)],
                                                 small_sems.at[k - 1]) for k in range(1, N_DEV)]
        for cp in [cp for per_w in loads for cp in per_w] + small_loads:
            cp.start()
        if n_small:
            small_buf[me] = small_ref[...]
        swaps = []
        for w in range(n_w):
            for cp in loads[w]:
                cp.wait()
            rows = pieces[w].shape[1]
            total = bufs[w][0].astype(F32)
            for p in range(1, N_DEV):
                total += bufs[w][p].astype(F32)
            out_refs[w][pl.ds(pl.multiple_of(c * rows, rows), rows), :] = total
            sw = swap_copy(w, c)
            sw.start()
            swaps.append(sw)
        if n_small:
            for cp in small_loads:
                cp.wait()
            total = small_buf[0]
            for p in range(1, N_DEV):
                total += small_buf[p]
            small_out[...] = total
        for w in range(n_w):
            swap_copy(w, 1 - c).wait_recv()
        for sw in swaps:
            sw.wait_send()

    vmem = pl.BlockSpec(memory_space=pltpu.VMEM)
    hbm = pl.BlockSpec(memory_space=pl.ANY)
    small_args = [small, small_land] if n_small else []
    small_shapes = [jax.ShapeDtypeStruct(small.shape, F32)] if n_small else []
    small_scratch = ([pltpu.VMEM((N_DEV,) + small.shape, F32), pltpu.SemaphoreType.DMA((_N_PEER,))]
                     if n_small else [])
    return pl.pallas_call(
        body, name="sum_and_swap_" + tag,
        in_specs=[hbm] * (2 * n_w) + [vmem, hbm] * n_small,
        out_specs=[vmem] * (n_w + n_small),
        out_shape=[jax.ShapeDtypeStruct((2 * p.shape[1], p.shape[2]), F32) for p in pieces] + small_shapes,
        scratch_shapes=[pltpu.VMEM(p.shape, BF16) for p in pieces] + small_scratch
        + [pltpu.SemaphoreType.DMA((n_w * N_DEV,)), pltpu.SemaphoreType.DMA((n_w,)),
           pltpu.SemaphoreType.DMA((n_w,))],
        compiler_params=pltpu.CompilerParams(vmem_limit_bytes=VMEM_LIMIT),
    )(*pieces, *lands, *small_args)


def _adamw_update(w_ref, g_ref, m_ref, v_ref, d_ref, nm_ref, nv_ref):
    c1 = 1.0 - ADAM_B1 ** ADAM_STEP
    c2 = 1.0 - ADAM_B2 ** ADAM_STEP
    g_t = g_ref[...]
    nm = ADAM_B1 * m_ref[...] + (1.0 - ADAM_B1) * g_t
    nv = ADAM_B2 * v_ref[...] + (1.0 - ADAM_B2) * (g_t * g_t)
    d_ref[...] = -ADAM_LR * ((nm / c1) / (jnp.sqrt(nv / c2) + ADAM_EPS) + ADAM_WD * w_ref[...])
    nm_ref[...] = nm
    nv_ref[...] = nv


def _adamw_small(items):
    n = len(items)
    flat, shapes = [], []
    for w, g, m, v in items:
        cols = w.shape[-1]
        shapes.append(w.shape)
        flat += [a.reshape(w.size // cols, cols) for a in (w, g, m, v)]

    def body(*refs):
        for k in range(n):
            _adamw_update(*refs[4 * k:4 * k + 4], *refs[4 * n + 3 * k:4 * n + 3 * k + 3])

    outs = pl.pallas_call(
        body, name="adamw_small",
        out_shape=[jax.ShapeDtypeStruct(flat[4 * k].shape, F32) for k in range(n) for _ in range(3)],
    )(*flat)
    return [tuple(o.reshape(shapes[k]) for o in outs[3 * k:3 * k + 3]) for k in range(n)]


def _adamw(label, w, g, m, v):
    shape = w.shape
    cols = shape[-1]
    rows = w.size // cols
    args = [a.reshape(rows, cols) for a in (w, g, m, v)]

    def body(w_ref, g_ref, m_ref, v_ref, g_out, d_ref, nm_ref, nv_ref):
        g_out[...] = g_ref[...]
        _adamw_update(w_ref, g_ref, m_ref, v_ref, d_ref, nm_ref, nv_ref)

    block_rows = 256 if rows % 256 == 0 and rows > 256 else rows
    spec = pl.BlockSpec((block_rows, cols), lambda i: (i, 0))
    outs = pl.pallas_call(
        body, name="adamw_" + label, grid=(rows // block_rows,),
        in_specs=[spec] * 4, out_specs=[spec] * 4,
        out_shape=[jax.ShapeDtypeStruct((rows, cols), F32)] * 4,
        compiler_params=_params(),
    )(*args)
    return [o.reshape(shape) for o in outs]


def _no_send(tag, arrays, sliced):
    return jnp.zeros((8, 128), F32)


def _local_step(x, tgt, w_in_a, later_weights, first_after, sgu_ln_g, sgu_ln_b, w_spatial, b_spatial,
                attn_sinks, rel_bias, post_ln_g, post_ln_b, send=_no_send):
    bsp_t = b_spatial.T
    g1, b1 = post_ln_g[0:1], post_ln_b[0:1]
    g2, b2 = post_ln_g[1:2], post_ln_b[1:2]
    onehot = _bucket_onehot()
    bias = _bias_expand(rel_bias.T, onehot).reshape(N_HEADS, 2 * CHUNK, CHUNK)
    win = _window_tables()

    xt, u, vh, z, rv, y = _layer_a_fwd(x, w_in_a, sgu_ln_g, sgu_ln_b, w_spatial, bsp_t, first_after)
    w_out_a, w_kv, w_in_b, w_out_b = later_weights(y)
    xh1, rstd1, q, zb, kd, vd = _layer_b_proj(x, y, w_out_a, g1, b1, w_in_b, w_kv)
    o, probs, sink_probs, dr2, loss_vec, dg2, db2 = _layer_b_fwd(q, zb, kd, vd, bias, win, attn_sinks, xh1, g1, b1,
                                                                 w_out_b, g2, b2, tgt)
    dq, dzb, dkd, dvd, carry_k, carry_v, gw_out_b, dsink, dbias = _layer_b_bwd_attn(
        dr2, zb, o, q, kd, vd, probs, sink_probs, w_out_b)
    dr1, dg1, db1, gw_in_b, gw_kv = _layer_b_bwd_proj(xh1, rstd1, g1, b1, dr2, dq, dzb, dkd, dvd, carry_k, carry_v,
                                                      w_in_b, w_kv)
    gw_out_b = gw_out_b.reshape(N_DEV, -1, D_MODEL)
    gw_kv = gw_kv.reshape(N_DEV, -1, 2 * PAIR)
    after = send("b", [gw_out_b, gw_in_b, gw_kv], [True, True, True])
    dp, gw_out_a, dws, dbsp, dgs, dbs = _layer_a_bwd_mix(dr1, u, vh, z, y, rv, w_out_a, sgu_ln_g, sgu_ln_b,
                                                         w_spatial, bsp_t, after)
    gw_out_a = gw_out_a.reshape(N_DEV, -1, D_MODEL)
    drel = _bias_reduce(onehot, dbias.reshape(N_HEADS, -1))
    dsink = jnp.sum(dsink.reshape(N_HEADS, CHUNK), axis=1).reshape(1, N_HEADS)
    loss = ((0.5 / D_MODEL) * jnp.sum(loss_vec)).reshape(1, 1)
    small = dict(w_spatial=dws, b_spatial=dbsp[:, 0:A_GROUPS].T, attn_sinks=dsink,
                 rel_bias=drel, post_ln_g=jnp.concatenate([dg1, dg2], axis=0),
                 post_ln_b=jnp.concatenate([db1, db2], axis=0), sgu_ln_g=dgs, sgu_ln_b=dbs, loss=loss)
    after = send("a_out", [gw_out_a, _pack_small(small)], [True, False])
    gw_in_a = _layer_a_bwd_win(xt, dp, after).reshape(N_DEV, D_MODEL // 2, -1)
    after = send("a_in", [gw_in_a], [True])
    after, updates = after if isinstance(after, tuple) else (after, ())
    grad_x, *updated = _layer_a_bwd_dx(dr1, dp, w_in_a, after, updates)

    pieces = [gw_in_a, gw_out_a, gw_kv, gw_in_b, gw_out_b]
    return grad_x, pieces, small, updated


_SMALL_SHAPES = dict(w_spatial=(A_GROUPS, CHUNK, CHUNK), b_spatial=(A_GROUPS, CHUNK), attn_sinks=(1, N_HEADS),
                     rel_bias=(REL_BUCKETS, N_HEADS), post_ln_g=(2, D_MODEL), post_ln_b=(2, D_MODEL),
                     sgu_ln_g=(1, A_WIDTH), sgu_ln_b=(1, A_WIDTH), loss=(1, 1))
_SMALL_ORDER = tuple(_SMALL_SHAPES)


def _small_rows(name):
    size = math.prod(_SMALL_SHAPES[name])
    return -(-size // (_LANES * _SUBLANES)) * _SUBLANES


def _pack_small(small):
    parts = []
    for name in _SMALL_ORDER:
        assert small[name].shape == _SMALL_SHAPES[name], (name, small[name].shape)
        flat = small[name].reshape(-1)
        flat = jnp.pad(flat, (0, _small_rows(name) * _LANES - flat.size))
        parts.append(flat.reshape(-1, _LANES))
    return jnp.concatenate(parts, axis=0)


def _unpack_small(packed):
    out, row = {}, 0
    for name in _SMALL_ORDER:
        shape = _SMALL_SHAPES[name]
        out[name] = packed[row:row + _small_rows(name)].reshape(-1)[:math.prod(shape)].reshape(shape)
        row += _small_rows(name)
    return out


def kernel(x, w_in_a, sgu_ln_g, sgu_ln_b, w_spatial, b_spatial, w_out_a, w_kv, w_in_b, attn_sinks, rel_bias, w_out_b, post_ln_g, post_ln_b, loss_target, m_w_in_a, m_sgu_ln_g, m_sgu_ln_b, m_w_spatial, m_b_spatial, m_w_out_a, m_w_kv, m_w_in_b, m_attn_sinks, m_rel_bias, m_w_out_b, m_post_ln_g, m_post_ln_b, v_w_in_a, v_sgu_ln_g, v_sgu_ln_b, v_w_spatial, v_b_spatial, v_w_out_a, v_w_kv, v_w_in_b, v_attn_sinks, v_rel_bias, v_w_out_b, v_post_ln_g, v_post_ln_b):
    weights = dict(w_in_a=w_in_a, sgu_ln_g=sgu_ln_g, sgu_ln_b=sgu_ln_b, w_spatial=w_spatial, b_spatial=b_spatial,
                   w_out_a=w_out_a, w_kv=w_kv, w_in_b=w_in_b, attn_sinks=attn_sinks, rel_bias=rel_bias,
                   w_out_b=w_out_b, post_ln_g=post_ln_g, post_ln_b=post_ln_b)
    moments_m = dict(w_in_a=m_w_in_a, sgu_ln_g=m_sgu_ln_g, sgu_ln_b=m_sgu_ln_b, w_spatial=m_w_spatial,
                     b_spatial=m_b_spatial, w_out_a=m_w_out_a, w_kv=m_w_kv, w_in_b=m_w_in_b,
                     attn_sinks=m_attn_sinks, rel_bias=m_rel_bias, w_out_b=m_w_out_b, post_ln_g=m_post_ln_g,
                     post_ln_b=m_post_ln_b)
    moments_v = dict(w_in_a=v_w_in_a, sgu_ln_g=v_sgu_ln_g, sgu_ln_b=v_sgu_ln_b, w_spatial=v_w_spatial,
                     b_spatial=v_b_spatial, w_out_a=v_w_out_a, w_kv=v_w_kv, w_in_b=v_w_in_b,
                     attn_sinks=v_attn_sinks, rel_bias=v_rel_bias, w_out_b=v_w_out_b, post_ln_g=v_post_ln_g,
                     post_ln_b=v_post_ln_b)
    order = ("w_in_a", "sgu_ln_g", "sgu_ln_b", "w_spatial", "b_spatial", "w_out_a", "w_kv", "w_in_b", "attn_sinks",
             "rel_bias", "w_out_b", "post_ln_g", "post_ln_b")

    shard_index = 2 * lax.axis_index("x") + lax.axis_index("y")
    ln_shard = jnp.concatenate([sgu_ln_g, sgu_ln_b], axis=0)
    shards = [w_in_a[0], w_out_a[0], w_kv, w_in_b[0], w_out_b[0]]
    col_sharded = [True, False, False, True, False]
    full_in_a, *later, ln_full = _gather_weights(shards, col_sharded, [True, False, False, False, False], ln_shard)
    ln_full = jnp.transpose(ln_full, (1, 0, 2)).reshape(2, A_WIDTH)
    later_shapes = [s.shape for s in shards[1:]]
    fetch_group, fetch_token = _fetch_start(later, later_shapes, col_sharded[1:])

    def later_weights(y):
        return _fetch_wait(fetch_group, later_shapes, col_sharded[1:], y)

    groups, grads, deltas, new_m, new_v, scalars = {}, {}, {}, {}, {}, {}
    early = ("w_out_b", "w_in_b", "w_kv", "w_out_a")

    def two_dim(a):
        return a.reshape(-1, a.shape[-1])

    def send(tag, arrays, sliced):
        groups[tag], token = _exchange_start(tag, arrays, sliced)
        if tag != "a_in":
            return token
        srcs, lands = _exchange_wait("early", [groups["b"], groups["a_out"]], token)
        *reduced, packed_sum = _sum_and_swap("early", srcs[:4], lands[:4], srcs[4], lands[4])
        updates = [(two_dim(weights[n]), g, two_dim(moments_m[n]), two_dim(moments_v[n]))
                   for n, g in zip(early, reduced)]
        small_sum = _unpack_small(packed_sum)
        scalars["loss"] = small_sum["loss"].reshape(())
        shard_cols = sgu_ln_g.shape[1]
        for name in ("sgu_ln_g", "sgu_ln_b"):
            grads[name] = lax.dynamic_slice(small_sum[name], (0, shard_index * shard_cols), (1, shard_cols))
        small_names = ("sgu_ln_g", "sgu_ln_b", "w_spatial", "b_spatial", "attn_sinks", "rel_bias", "post_ln_g",
                       "post_ln_b")
        for name in small_names[2:]:
            grads[name] = small_sum[name].reshape(weights[name].shape)
        small_updates = _adamw_small([(weights[n], grads[n], moments_m[n], moments_v[n]) for n in small_names])
        for name, (d, nm, nv) in zip(small_names, small_updates):
            deltas[name], new_m[name], new_v[name] = d, nm, nv
        return packed_sum[0:_SUBLANES], updates

    grad_x, _, _, updated = _local_step(
        x[0], loss_target[0], full_in_a, later_weights, fetch_token, ln_full[0:1], ln_full[1:2], w_spatial[0],
        b_spatial[0], attn_sinks, rel_bias, post_ln_g, post_ln_b, send=send)
    for k, name in enumerate(early):
        grads[name], deltas[name], new_m[name], new_v[name] = [
            a.reshape(weights[name].shape) for a in updated[4 * k:4 * k + 4]]

    srcs, lands = _exchange_wait("late", [groups["a_in"]], grad_x)
    (g_in_a,) = _sum_and_swap("late", srcs, lands)
    grads["w_in_a"], deltas["w_in_a"], new_m["w_in_a"], new_v["w_in_a"] = _adamw(
        "w_in_a", w_in_a, g_in_a.reshape(w_in_a.shape), m_w_in_a, v_w_in_a)
    return (scalars["loss"], grad_x[None], *[grads[n] for n in order], *[deltas[n] for n in order],
            *[new_m[n] for n in order], *[new_v[n] for n in order])
```

```python
import math

import jax
import jax.numpy as jnp
from jax import lax
from jax.experimental import pallas as pl
from jax.experimental.pallas import tpu as pltpu

F32 = jnp.float32
BF16 = jnp.bfloat16

D_MODEL = 1024
A_WIDTH = 2048
A_GROUPS = 8
A_GROUP_DIM = 256
CHUNK = 128
N_HEADS = 16
N_KV = 2
HEAD_DIM = 64
PAIR = 2 * HEAD_DIM
B_WIDTH = 1024
REL_BUCKETS = 32
ALPHA = 4.0 ** 0.25
LN_EPS = 1e-5
NEG_INF = -1e30
SCALE = HEAD_DIM ** -0.5

ADAM_LR = 0.001
ADAM_B1 = 0.9
ADAM_B2 = 0.999
ADAM_EPS = 1e-08
ADAM_WD = 0.01
ADAM_STEP = 10

N_DEV = 8
N_CHIPS = 4
MESH = pl.DeviceIdType.MESH
VMEM_LIMIT = 56 * 1024 * 1024

TM_ATTN = 256
TM_MM = 512
TM_WIN = 1024
_LANES = 128
_SUBLANES = 8


def _dot(a, b):
    return jnp.dot(a, b, preferred_element_type=F32)


def _dot_nt(a, b):
    return lax.dot_general(a, b, (((1,), (1,)), ((), ())), preferred_element_type=F32)


def _dot_tn(a, b):
    return lax.dot_general(a, b, (((0,), (0,)), ((), ())), preferred_element_type=F32)


def _ln_fwd(r):
    mu = jnp.mean(r, axis=-1, keepdims=True)
    rc = r - mu
    var = jnp.mean(rc * rc, axis=-1, keepdims=True)
    rstd = lax.rsqrt(var + LN_EPS)
    return rc * rstd, rstd


def _ln_bwd(dxh, xh, rstd):
    m1 = jnp.mean(dxh, axis=-1, keepdims=True)
    m2 = jnp.mean(dxh * xh, axis=-1, keepdims=True)
    return rstd * (dxh - m1 - xh * m2)


def _silu_parts(z):
    sg = jax.nn.sigmoid(z)
    return z * sg, sg * (1.0 + z * (1.0 - sg))


def _dup_halves(blk):
    sw = pltpu.roll(blk, HEAD_DIM, 1)
    lo = lax.broadcasted_iota(jnp.int32, blk.shape, 1) < HEAD_DIM
    return jnp.where(lo, blk, sw), jnp.where(lo, sw, blk)


def _fold_halves(blk):
    return blk + pltpu.roll(blk, HEAD_DIM, 1)


def _resident(shape):
    nd = len(shape)
    return pl.BlockSpec(shape, lambda *_: (0,) * nd, pipeline_mode=pl.Buffered(1))


def _const(shape):
    nd = len(shape)
    return pl.BlockSpec(shape, lambda *_: (0,) * nd)


def _rows(tm, cols):
    return pl.BlockSpec((tm, cols), lambda i: (i, 0))


def _params(sem=("arbitrary",)):
    return pltpu.CompilerParams(dimension_semantics=sem, vmem_limit_bytes=VMEM_LIMIT)


def _spatial_mix(ws_ref, bsp_ref, vn, s_scr, n_chunks):
    tri = (lax.broadcasted_iota(jnp.int32, (CHUNK, CHUNK), 0)
           >= lax.broadcasted_iota(jnp.int32, (CHUNK, CHUNK), 1))
    for g in range(A_GROUPS):
        wsg = jnp.where(tri, ws_ref[g], 0.0).astype(BF16)
        cols = slice(g * A_GROUP_DIM, (g + 1) * A_GROUP_DIM)
        for ci in range(n_chunks):
            rows = slice(ci * CHUNK, (ci + 1) * CHUNK)
            s_scr[rows, cols] = _dot(wsg, vn[rows, cols]) + bsp_ref[:, g:g + 1]


def _layer_a_fwd(x, w_in, lng, lnb, ws, bsp_t, after):
    t_len = x.shape[0]
    tm = TM_ATTN

    def body(x_ref, win_ref, lng_ref, lnb_ref, ws_ref, bsp_ref, after_ref,
             xt_ref, u_ref, vh_ref, z_ref, rv_ref, y_ref, s_scr):
        x_t = x_ref[...]
        xb = x_t.astype(BF16)
        xt_ref[...] = x_t.T.astype(BF16)
        u = _dot(xb, win_ref[:, 0:A_WIDTH])
        v = _dot(xb, win_ref[:, A_WIDTH:2 * A_WIDTH])
        z = _dot(xb, win_ref[:, 2 * A_WIDTH:3 * A_WIDTH])
        vh, rv = _ln_fwd(v)
        vn = (vh * lng_ref[...] + lnb_ref[...]).astype(BF16)
        _spatial_mix(ws_ref, bsp_ref, vn, s_scr, tm // CHUNK)
        sz, _ = _silu_parts(z)
        y_ref[...] = (u * s_scr[...] * sz).astype(BF16)
        u_ref[...] = u.astype(BF16)
        vh_ref[...] = vh.astype(BF16)
        z_ref[...] = z.astype(BF16)
        rv_ref[...] = rv

    wide = jax.ShapeDtypeStruct((t_len, A_WIDTH), BF16)
    return pl.pallas_call(
        body, name="layer_a_fwd", grid=(t_len // tm,),
        in_specs=[_rows(tm, D_MODEL), _resident(w_in.shape), _const(lng.shape), _const(lnb.shape), _const(ws.shape),
                  _const(bsp_t.shape), _const(after.shape)],
        out_specs=[pl.BlockSpec((D_MODEL, tm), lambda i: (0, i)), _rows(tm, A_WIDTH), _rows(tm, A_WIDTH),
                   _rows(tm, A_WIDTH), _rows(tm, 1), _rows(tm, A_WIDTH)],
        out_shape=[jax.ShapeDtypeStruct((D_MODEL, t_len), BF16), wide, wide, wide,
                   jax.ShapeDtypeStruct((t_len, 1), F32), wide],
        scratch_shapes=[pltpu.VMEM((tm, A_WIDTH), F32)],
        compiler_params=_params(),
    )(x, w_in, lng, lnb, ws, bsp_t, after)


def _layer_b_proj(x, y, w_out_a, g1, b1, w_in, w_kv):
    t_len = x.shape[0]
    tm = TM_MM

    def body(x_ref, y_ref, wout_ref, g_ref, b_ref, win_ref, wkv_ref, xh_ref, r1_ref, q_ref, z_ref, kd_ref, vd_ref):
        xh, r1 = _ln_fwd(ALPHA * x_ref[...] + _dot(y_ref[...], wout_ref[...]))
        xh_ref[...] = xh
        r1_ref[...] = r1
        h1 = (xh * g_ref[...] + b_ref[...]).astype(BF16)
        q_ref[...] = (_dot(h1, win_ref[:, 0:B_WIDTH]) * SCALE).astype(BF16)
        z_ref[...] = _dot(h1, win_ref[:, B_WIDTH:2 * B_WIDTH]).astype(BF16)
        kv = _dot(h1, wkv_ref[...])
        k0, k1 = _dup_halves(kv[:, 0:PAIR])
        v0, v1 = _dup_halves(kv[:, PAIR:2 * PAIR])
        kd_ref[:, 0:PAIR] = k0.astype(BF16)
        kd_ref[:, PAIR:2 * PAIR] = k1.astype(BF16)
        vd_ref[:, 0:PAIR] = v0.astype(BF16)
        vd_ref[:, PAIR:2 * PAIR] = v1.astype(BF16)

    return pl.pallas_call(
        body, name="layer_b_proj", grid=(t_len // tm,),
        in_specs=[_rows(tm, D_MODEL), _rows(tm, A_WIDTH), _resident(w_out_a.shape), _const(g1.shape),
                  _const(b1.shape), _resident(w_in.shape), _resident(w_kv.shape)],
        out_specs=[_rows(tm, D_MODEL), _rows(tm, 1), _rows(tm, B_WIDTH), _rows(tm, B_WIDTH), _rows(tm, 2 * PAIR),
                   _rows(tm, 2 * PAIR)],
        out_shape=[jax.ShapeDtypeStruct((t_len, D_MODEL), F32), jax.ShapeDtypeStruct((t_len, 1), F32),
                   jax.ShapeDtypeStruct((t_len, B_WIDTH), BF16), jax.ShapeDtypeStruct((t_len, B_WIDTH), BF16),
                   jax.ShapeDtypeStruct((t_len, 2 * PAIR), BF16), jax.ShapeDtypeStruct((t_len, 2 * PAIR), BF16)],
        compiler_params=_params(),
    )(x, y, w_out_a, g1, b1, w_in, w_kv)


GROUP = N_HEADS // N_KV
GROUP_Q = GROUP * CHUNK


def _window_tables():
    j = jnp.arange(2 * CHUNK, dtype=jnp.int32)[:, None]
    t = jnp.arange(CHUNK, dtype=jnp.int32)[None, :]
    dist = t + CHUNK - j
    inside = (dist >= 0) & (dist < CHUNK)
    return jnp.stack([inside & (j >= CHUNK), inside]).astype(F32)


def _band(ref, chunk_index, kvh):
    prev0 = pl.multiple_of(jnp.maximum(chunk_index - 1, 0) * CHUNK, CHUNK)
    cur0 = pl.multiple_of(chunk_index * CHUNK, CHUNK)
    cols = slice(kvh * PAIR, (kvh + 1) * PAIR)
    return jnp.concatenate([ref[pl.ds(prev0, CHUNK), cols], ref[pl.ds(cur0, CHUNK), cols]], axis=0)


def _group_tables(bias_ref, win_ref, sink_ref, chunk_index, kvh):
    bias = jnp.concatenate([bias_ref[kvh * GROUP + j] for j in range(GROUP)], axis=1)
    win = win_ref[jnp.minimum(chunk_index, 1)]
    mask = jnp.concatenate([win] * GROUP, axis=1) > 0.5
    sink = jnp.concatenate([jnp.full((1, CHUNK), sink_ref[0, kvh * GROUP + j], F32) for j in range(GROUP)], axis=1)
    return bias, mask, sink


def _attn_probs(qs, kband, bias, mask, sink):
    logits = jnp.where(mask, _dot_nt(kband, qs) + bias, NEG_INF)
    m = jnp.maximum(jnp.max(logits, axis=0, keepdims=True), sink)
    e = jnp.exp(logits - m)
    es = jnp.exp(sink - m)
    inv = 1.0 / (jnp.sum(e, axis=0, keepdims=True) + es)
    return e * inv, es * inv


def _half_mask():
    return lax.broadcasted_iota(jnp.int32, (CHUNK, PAIR), 1) < HEAD_DIM


def _stack_heads(src_ref, rows, kvh, dst_scr, lo):
    for j in range(GROUP):
        h = kvh * GROUP + j
        blk = src_ref[rows, (h // 2) * PAIR:(h // 2 + 1) * PAIR].astype(F32)
        keep = lo if h % 2 == 0 else ~lo
        dst_scr[j * CHUNK:(j + 1) * CHUNK, :] = jnp.where(keep, blk, 0.0).astype(BF16)


def _probs_spec(tm):
    return pl.BlockSpec((tm // CHUNK, N_KV, 2 * CHUNK, GROUP_Q), lambda i: (i, 0, 0, 0))


def _sink_probs_spec():
    return pl.BlockSpec((1, 8, GROUP_Q), lambda i: (i, 0, 0))


def _unstack_pairs(stacked, pp, lo):
    return jnp.where(lo, stacked[(2 * pp) * CHUNK:(2 * pp + 1) * CHUNK], stacked[(2 * pp + 1) * CHUNK:(2 * pp + 2) * CHUNK])


def _layer_b_fwd(q, zb, kd, vd, bias, win, sinks, xh1, g1, b1, w_out, g2, b2, tgt):
    t_len = q.shape[0]
    tm = TM_ATTN

    def body(q_ref, z_ref, kd_ref, vd_ref, bias_ref, win_ref, sink_ref, xh_ref, g1_ref, b1_ref, wout_ref, g2_ref,
             b2_ref, tgt_ref, o_ref, p_ref, ps_ref, dr_ref, loss_ref, dg_ref, db_ref, o_scr, qs_scr):
        i = pl.program_id(0)

        @pl.when(i == 0)
        def _():
            loss_ref[...] = jnp.zeros_like(loss_ref)
            dg_ref[...] = jnp.zeros_like(dg_ref)
            db_ref[...] = jnp.zeros_like(db_ref)

        lo = _half_mask()
        ps_ref[...] = jnp.zeros_like(ps_ref)
        for ci in range(tm // CHUNK):
            cg = i * (tm // CHUNK) + ci
            rows = slice(ci * CHUNK, (ci + 1) * CHUNK)
            for kvh in range(N_KV):
                kband = _band(kd_ref, cg, kvh)
                vband = _band(vd_ref, cg, kvh)
                bias_g, mask, sink = _group_tables(bias_ref, win_ref, sink_ref, cg, kvh)
                _stack_heads(q_ref, rows, kvh, qs_scr, lo)
                p, p_sink = _attn_probs(qs_scr[...], kband, bias_g, mask, sink)
                p = p.astype(BF16)
                p_ref[ci, kvh] = p
                ps_ref[0, ci * N_KV + kvh:ci * N_KV + kvh + 1, :] = p_sink
                o_stack = _dot_tn(p, vband)
                for pp in range(GROUP // 2):
                    pair = kvh * (GROUP // 2) + pp
                    o_scr[rows, pair * PAIR:(pair + 1) * PAIR] = _unstack_pairs(o_stack, pp, lo)
        o = o_scr[...]
        o_ref[...] = o.astype(BF16)
        sz, _ = _silu_parts(z_ref[...].astype(F32))
        y = (o * sz).astype(BF16)
        h1 = xh_ref[...] * g1_ref[...] + b1_ref[...]
        r = ALPHA * h1 + _dot(y, wout_ref[...])
        xh2, rstd2 = _ln_fwd(r)
        diff = xh2 * g2_ref[...] + b2_ref[...] - tgt_ref[...]
        loss_ref[...] += jnp.sum(diff * diff, axis=0, keepdims=True)
        dh2 = diff * (1.0 / D_MODEL)
        dg_ref[...] += jnp.sum(dh2 * xh2, axis=0, keepdims=True)
        db_ref[...] += jnp.sum(dh2, axis=0, keepdims=True)
        dr_ref[...] = _ln_bwd(dh2 * g2_ref[...], xh2, rstd2)

    vec = jax.ShapeDtypeStruct((1, D_MODEL), F32)
    return pl.pallas_call(
        body, name="layer_b_fwd", grid=(t_len // tm,),
        in_specs=[_rows(tm, B_WIDTH), _rows(tm, B_WIDTH), _resident(kd.shape), _resident(vd.shape),
                  _resident(bias.shape), _resident(win.shape), pl.BlockSpec(memory_space=pltpu.SMEM),
                  _rows(tm, D_MODEL), _const(g1.shape), _const(b1.shape), _resident(w_out.shape), _const(g2.shape),
                  _const(b2.shape), _rows(tm, D_MODEL)],
        out_specs=[_rows(tm, B_WIDTH), _probs_spec(tm), _sink_probs_spec(), _rows(tm, D_MODEL)]
        + [_const((1, D_MODEL))] * 3,
        out_shape=[jax.ShapeDtypeStruct((t_len, B_WIDTH), BF16),
                   jax.ShapeDtypeStruct((t_len // CHUNK, N_KV, 2 * CHUNK, GROUP_Q), BF16),
                   jax.ShapeDtypeStruct((t_len // tm, 8, GROUP_Q), F32),
                   jax.ShapeDtypeStruct((t_len, D_MODEL), F32), vec, vec, vec],
        scratch_shapes=[pltpu.VMEM((tm, B_WIDTH), F32), pltpu.VMEM((GROUP_Q, PAIR), BF16)],
        compiler_params=_params(),
    )(q, zb, kd, vd, bias, win, sinks, xh1, g1, b1, w_out, g2, b2, tgt)


def _layer_b_bwd_attn(dr2, zb, o, q, kd, vd, probs, sink_probs, w_out):
    t_len = q.shape[0]
    tm = TM_ATTN
    n_steps = t_len // tm
    n_chunks = tm // CHUNK

    def body(dr_ref, z_ref, o_ref, q_ref, kd_ref, vd_ref, p_ref, ps_ref, wout_ref,
             dq_ref, dz_ref, dkd_ref, dvd_ref, ck_ref, cv_ref, gw_ref, dsink_ref, dbias_ref,
             do_scr, qs_scr, dos_scr, gw_acc):
        i = pl.program_id(0)

        @pl.when(i == 0)
        def _():
            gw_acc[...] = jnp.zeros_like(gw_acc)
            dsink_ref[...] = jnp.zeros_like(dsink_ref)
            dbias_ref[...] = jnp.zeros_like(dbias_ref)

        drb = dr_ref[...].astype(BF16)
        dy = _dot_nt(drb, wout_ref[...])
        z = z_ref[...].astype(F32)
        sz, dsz = _silu_parts(z)
        o_t = o_ref[...].astype(F32)
        dz_ref[...] = (dy * o_t * dsz).astype(BF16)
        do_scr[...] = (dy * sz).astype(BF16)
        gw_acc[...] += _dot_tn((o_t * sz).astype(BF16), drb)

        lo = _half_mask()
        for kvh in range(N_KV):
            kcols = slice(kvh * PAIR, (kvh + 1) * PAIR)
            dk_bands, dv_bands = [], []
            for ci in range(n_chunks):
                cg = i * n_chunks + ci
                rows = slice(ci * CHUNK, (ci + 1) * CHUNK)
                kband = _band(kd_ref, cg, kvh)
                vband = _band(vd_ref, cg, kvh)
                _stack_heads(q_ref, rows, kvh, qs_scr, lo)
                _stack_heads(do_scr, rows, kvh, dos_scr, lo)
                qs = qs_scr[...]
                dos = dos_scr[...]
                pb = p_ref[ci, kvh]
                p = pb.astype(F32)
                p_sink = ps_ref[0, ci * N_KV + kvh:ci * N_KV + kvh + 1, :]
                dp = _dot_nt(vband, dos)
                delta = jnp.sum(p * dp, axis=0, keepdims=True)
                dlog = p * (dp - delta)
                for j in range(GROUP):
                    dbias_ref[kvh * GROUP + j] += dlog[:, j * CHUNK:(j + 1) * CHUNK]
                dsink_ref[kvh:kvh + 1, :] += -(p_sink * delta)
                ds = dlog.astype(BF16)
                dq_stack = _dot_tn(ds, kband) * SCALE
                for pp in range(GROUP // 2):
                    pair = kvh * (GROUP // 2) + pp
                    dq_ref[rows, pair * PAIR:(pair + 1) * PAIR] = _unstack_pairs(dq_stack, pp, lo).astype(BF16)
                dk_bands.append(_dot(ds, qs))
                dv_bands.append(_dot(pb, dos))
            for bands, out_ref, carry_ref in ((dk_bands, dkd_ref, ck_ref), (dv_bands, dvd_ref, cv_ref)):
                carry_ref[0, :, kcols] = bands[0][0:CHUNK]
                for ci in range(n_chunks):
                    own = bands[ci][CHUNK:2 * CHUNK]
                    if ci + 1 < n_chunks:
                        own = own + bands[ci + 1][0:CHUNK]
                    out_ref[ci * CHUNK:(ci + 1) * CHUNK, kcols] = own

        @pl.when(i == n_steps - 1)
        def _():
            gw_ref[...] = gw_acc[...].astype(BF16)

    carry_spec = pl.BlockSpec((1, CHUNK, 2 * PAIR), lambda i: (i, 0, 0))
    carry_shape = jax.ShapeDtypeStruct((n_steps, CHUNK, 2 * PAIR), F32)
    bias_shape = (N_HEADS, 2 * CHUNK, CHUNK)
    return pl.pallas_call(
        body, name="layer_b_bwd_attn", grid=(n_steps,),
        in_specs=[_rows(tm, D_MODEL), _rows(tm, B_WIDTH), _rows(tm, B_WIDTH), _rows(tm, B_WIDTH),
                  _resident(kd.shape), _resident(vd.shape), _probs_spec(tm), _sink_probs_spec(),
                  _resident(w_out.shape)],
        out_specs=[_rows(tm, B_WIDTH), _rows(tm, B_WIDTH), _rows(tm, 2 * PAIR), _rows(tm, 2 * PAIR),
                   carry_spec, carry_spec, _const(w_out.shape), _const((N_KV, GROUP_Q)), _const(bias_shape)],
        out_shape=[jax.ShapeDtypeStruct((t_len, B_WIDTH), BF16), jax.ShapeDtypeStruct((t_len, B_WIDTH), BF16),
                   jax.ShapeDtypeStruct((t_len, 2 * PAIR), F32), jax.ShapeDtypeStruct((t_len, 2 * PAIR), F32),
                   carry_shape, carry_shape, jax.ShapeDtypeStruct(w_out.shape, BF16),
                   jax.ShapeDtypeStruct((N_KV, GROUP_Q), F32), jax.ShapeDtypeStruct(bias_shape, F32)],
        scratch_shapes=[pltpu.VMEM((tm, B_WIDTH), BF16), pltpu.VMEM((GROUP_Q, PAIR), BF16),
                        pltpu.VMEM((GROUP_Q, PAIR), BF16), pltpu.VMEM(w_out.shape, F32)],
        compiler_params=_params(),
    )(dr2, zb, o, q, kd, vd, probs, sink_probs, w_out)


def _layer_b_bwd_proj(xh1, rstd1, g1, b1, dr2, dq, dzb, dkd, dvd, carry_k, carry_v, w_in, w_kv):
    t_len = xh1.shape[0]
    tm = TM_MM
    n_steps = t_len // tm
    per_tile = tm // TM_ATTN
    n_carry = carry_k.shape[0]

    def body(xh_ref, rstd_ref, g_ref, b_ref, dr2_ref, dq_ref, dz_ref, dkd_ref, dvd_ref, *rest):
        carry_refs = rest[:2 * per_tile]
        win_ref, wkv_ref, dr1_ref, dg_ref, db_ref, gwin_ref, gwkv_ref, acc_in, acc_kv = rest[2 * per_tile:]
        i = pl.program_id(0)

        @pl.when(i == 0)
        def _():
            acc_in[...] = jnp.zeros_like(acc_in)
            acc_kv[...] = jnp.zeros_like(acc_kv)
            dg_ref[...] = jnp.zeros_like(dg_ref)
            db_ref[...] = jnp.zeros_like(db_ref)

        lo = lax.broadcasted_iota(jnp.int32, (tm, PAIR), 1) < HEAD_DIM

        def heads_gradient(tile_ref, refs):
            parts = []
            for a in range(per_tile):
                parts.append(tile_ref[a * TM_ATTN:(a + 1) * TM_ATTN - CHUNK, :])
                carry = refs[a][0]
                if a == per_tile - 1:
                    carry = jnp.where(i < n_steps - 1, carry, 0.0)
                parts.append(tile_ref[(a + 1) * TM_ATTN - CHUNK:(a + 1) * TM_ATTN, :] + carry)
            dup = jnp.concatenate(parts, axis=0)
            return jnp.where(lo, _fold_halves(dup[:, 0:PAIR]), _fold_halves(dup[:, PAIR:2 * PAIR]))

        xh = xh_ref[...]
        h1 = (xh * g_ref[...] + b_ref[...]).astype(BF16)
        dq_t = dq_ref[...]
        dz_t = dz_ref[...]
        dkv = jnp.concatenate([heads_gradient(dkd_ref, carry_refs[:per_tile]),
                               heads_gradient(dvd_ref, carry_refs[per_tile:])], axis=1).astype(BF16)
        dh1 = ALPHA * dr2_ref[...]
        dh1 += _dot_nt(dq_t, win_ref[:, 0:B_WIDTH])
        dh1 += _dot_nt(dz_t, win_ref[:, B_WIDTH:2 * B_WIDTH])
        dh1 += _dot_nt(dkv, wkv_ref[...])
        acc_in[:, 0:B_WIDTH] += _dot_tn(h1, dq_t)
        acc_in[:, B_WIDTH:2 * B_WIDTH] += _dot_tn(h1, dz_t)
        acc_kv[...] += _dot_tn(h1, dkv)
        dg_ref[...] += jnp.sum(dh1 * xh, axis=0, keepdims=True)
        db_ref[...] += jnp.sum(dh1, axis=0, keepdims=True)
        dr1_ref[...] = _ln_bwd(dh1 * g_ref[...], xh, rstd_ref[...])

        @pl.when(i == n_steps - 1)
        def _():
            half_rows = D_MODEL // 2
            shard_cols = 2 * B_WIDTH // N_CHIPS
            for s in range(N_CHIPS):
                for c in range(2):
                    gwin_ref[2 * s + c] = acc_in[c * half_rows:(c + 1) * half_rows,
                                                 s * shard_cols:(s + 1) * shard_cols].astype(BF16)
            gwkv_ref[...] = acc_kv[...].astype(BF16)

    vec = jax.ShapeDtypeStruct((1, D_MODEL), F32)
    gwin_shape = (N_DEV, D_MODEL // 2, 2 * B_WIDTH // N_CHIPS)

    def carry_spec(a):
        return pl.BlockSpec((1, CHUNK, 2 * PAIR), lambda i: (jnp.minimum(per_tile * i + a + 1, n_carry - 1), 0, 0))

    carry_specs = [carry_spec(a) for a in range(per_tile)]
    return pl.pallas_call(
        body, name="layer_b_bwd_proj", grid=(n_steps,),
        in_specs=[_rows(tm, D_MODEL), _rows(tm, 1), _const(g1.shape), _const(b1.shape), _rows(tm, D_MODEL),
                  _rows(tm, B_WIDTH), _rows(tm, B_WIDTH), _rows(tm, 2 * PAIR), _rows(tm, 2 * PAIR)]
        + carry_specs + carry_specs + [_resident(w_in.shape), _resident(w_kv.shape)],
        out_specs=[_rows(tm, D_MODEL), _const((1, D_MODEL)), _const((1, D_MODEL)), _const(gwin_shape),
                   _const(w_kv.shape)],
        out_shape=[jax.ShapeDtypeStruct((t_len, D_MODEL), F32), vec, vec,
                   jax.ShapeDtypeStruct(gwin_shape, BF16), jax.ShapeDtypeStruct(w_kv.shape, BF16)],
        scratch_shapes=[pltpu.VMEM(w_in.shape, F32), pltpu.VMEM(w_kv.shape, F32)],
        compiler_params=_params(),
    )(xh1, rstd1, g1, b1, dr2, dq, dzb, dkd, dvd, *([carry_k] * per_tile), *([carry_v] * per_tile), w_in, w_kv)


def _layer_a_bwd_mix(dr1, u, vh, z, y, rv, w_out, lng, lnb, ws, bsp_t, after):
    t_len = u.shape[0]
    tm = TM_ATTN
    n_steps = t_len // tm

    def body(dr_ref, u_ref, vh_ref, z_ref, y_ref, rv_ref, wout_ref, lng_ref, lnb_ref, ws_ref, bsp_ref, after_ref,
             dp_ref, gw_ref, dws_ref, dbsp_ref, dgs_ref, dbs_ref, s_scr, dvn_scr, gw_acc):
        i = pl.program_id(0)

        @pl.when(i == 0)
        def _():
            gw_acc[...] = jnp.zeros_like(gw_acc)
            dws_ref[...] = jnp.zeros_like(dws_ref)
            dbsp_ref[...] = jnp.zeros_like(dbsp_ref)
            dgs_ref[...] = jnp.zeros_like(dgs_ref)
            dbs_ref[...] = jnp.zeros_like(dbs_ref)

        drb = dr_ref[...].astype(BF16)
        dy = _dot_nt(drb, wout_ref[...])
        vh_t = vh_ref[...].astype(F32)
        vn = (vh_t * lng_ref[...] + lnb_ref[...]).astype(BF16)
        _spatial_mix(ws_ref, bsp_ref, vn, s_scr, tm // CHUNK)
        gw_acc[...] += _dot_tn(y_ref[...], drb)
        s = s_scr[...]
        sz, dsz = _silu_parts(z_ref[...].astype(F32))
        t = dy * u_ref[...].astype(F32)
        dp_ref[:, 0:A_WIDTH] = (dy * (s * sz)).astype(BF16)
        dp_ref[:, 2 * A_WIDTH:3 * A_WIDTH] = (t * s * dsz).astype(BF16)
        ds = (t * sz).astype(BF16)

        group_of = lax.broadcasted_iota(jnp.int32, (A_WIDTH, CHUNK), 0) // A_GROUP_DIM
        indicator = (group_of == lax.broadcasted_iota(jnp.int32, (A_WIDTH, CHUNK), 1)).astype(BF16)
        group_sums = _dot(ds, indicator)
        total = group_sums[0:CHUNK]
        for ci in range(1, tm // CHUNK):
            total += group_sums[ci * CHUNK:(ci + 1) * CHUNK]
        dbsp_ref[...] += total

        tri = (lax.broadcasted_iota(jnp.int32, (CHUNK, CHUNK), 0)
               >= lax.broadcasted_iota(jnp.int32, (CHUNK, CHUNK), 1))
        for g in range(A_GROUPS):
            wsg = jnp.where(tri, ws_ref[g], 0.0).astype(BF16)
            cols = slice(g * A_GROUP_DIM, (g + 1) * A_GROUP_DIM)
            dws_g = jnp.zeros((CHUNK, CHUNK), F32)
            for ci in range(tm // CHUNK):
                rows = slice(ci * CHUNK, (ci + 1) * CHUNK)
                ds_b = ds[rows, cols]
                dws_g += _dot_nt(ds_b, vn[rows, cols])
                dvn_scr[rows, cols] = _dot_tn(wsg, ds_b)
            dws_ref[g] += jnp.where(tri, dws_g, 0.0)
        dvn = dvn_scr[...]
        dgs_ref[...] += jnp.sum(dvn * vh_t, axis=0, keepdims=True)
        dbs_ref[...] += jnp.sum(dvn, axis=0, keepdims=True)
        dp_ref[:, A_WIDTH:2 * A_WIDTH] = _ln_bwd(dvn * lng_ref[...], vh_t, rv_ref[...]).astype(BF16)

        @pl.when(i == n_steps - 1)
        def _():
            gw_ref[...] = gw_acc[...].astype(BF16)

    wide = jax.ShapeDtypeStruct((1, A_WIDTH), F32)
    return pl.pallas_call(
        body, name="layer_a_bwd_mix", grid=(n_steps,),
        in_specs=[_rows(tm, D_MODEL), _rows(tm, A_WIDTH), _rows(tm, A_WIDTH), _rows(tm, A_WIDTH), _rows(tm, A_WIDTH),
                  _rows(tm, 1), _resident(w_out.shape), _const(lng.shape), _const(lnb.shape), _const(ws.shape),
                  _const(bsp_t.shape), _const(after.shape)],
        out_specs=[_rows(tm, 3 * A_WIDTH), _const(w_out.shape), _const(ws.shape), _const((CHUNK, CHUNK)),
                   _const((1, A_WIDTH)), _const((1, A_WIDTH))],
        out_shape=[jax.ShapeDtypeStruct((t_len, 3 * A_WIDTH), BF16), jax.ShapeDtypeStruct(w_out.shape, BF16),
                   jax.ShapeDtypeStruct(ws.shape, F32), jax.ShapeDtypeStruct((CHUNK, CHUNK), F32),
                   wide, wide],
        scratch_shapes=[pltpu.VMEM((tm, A_WIDTH), F32), pltpu.VMEM((tm, A_WIDTH), F32),
                        pltpu.VMEM(w_out.shape, F32)],
        compiler_params=_params(),
    )(dr1, u, vh, z, y, rv, w_out, lng, lnb, ws, bsp_t, after)


def _layer_a_bwd_dx(dr1, dp, w_in, after, updates=()):
    t_len = dr1.shape[0]
    tm = TM_MM
    n_steps = t_len // tm
    n_upd = len(updates)

    def body(dr_ref, dp_ref, win_ref, after_ref, *refs):
        upd_in, dx_ref, upd_out = refs[:4 * n_upd], refs[4 * n_upd], refs[4 * n_upd + 1:]
        dx_ref[...] = ALPHA * dr_ref[...] + _dot_nt(dp_ref[...], win_ref[...])
        for k in range(n_upd):
            w_ref, g_ref, m_ref, v_ref = upd_in[4 * k:4 * k + 4]
            g_out, d_ref, nm_ref, nv_ref = upd_out[4 * k:4 * k + 4]
            g_out[...] = g_ref[...]
            _adamw_update(w_ref, g_ref, m_ref, v_ref, d_ref, nm_ref, nv_ref)

    upd_specs, upd_shapes, upd_args = [], [], []
    for w, g, m, v in updates:
        rows, cols = w.shape
        upd_specs.append(pl.BlockSpec((rows // n_steps, cols), lambda i: (i, 0)))
        upd_shapes.append(jax.ShapeDtypeStruct((rows, cols), F32))
        upd_args += [w, g, m, v]
    return pl.pallas_call(
        body, name="layer_a_bwd_dx", grid=(n_steps,),
        in_specs=[_rows(tm, D_MODEL), _rows(tm, 3 * A_WIDTH), _resident(w_in.shape), _const(after.shape)]
        + [s for s in upd_specs for _ in range(4)],
        out_specs=[_rows(tm, D_MODEL)] + [s for s in upd_specs for _ in range(4)],
        out_shape=[jax.ShapeDtypeStruct((t_len, D_MODEL), F32)] + [s for s in upd_shapes for _ in range(4)],
        compiler_params=_params(),
    )(dr1, dp, w_in, after, *upd_args)


def _layer_a_bwd_win(xt, dp, after):
    t_len = xt.shape[1]
    tm = TM_WIN
    n_steps = t_len // tm
    shard_cols = 3 * A_WIDTH // N_CHIPS
    half_rows = D_MODEL // 2

    def body(xt_ref, dp_ref, after_ref, gw_ref, acc):
        i = pl.program_id(1)

        @pl.when(i == 0)
        def _():
            acc[...] = jnp.zeros_like(acc)

        acc[...] += _dot(xt_ref[...], dp_ref[...])

        @pl.when(i == n_steps - 1)
        def _():
            for c in range(2):
                gw_ref[0, c] = acc[c * half_rows:(c + 1) * half_rows, :].astype(BF16)

    return pl.pallas_call(
        body, name="layer_a_bwd_win", grid=(N_CHIPS, n_steps),
        in_specs=[pl.BlockSpec((D_MODEL, tm), lambda j, i: (0, i)),
                  pl.BlockSpec((tm, shard_cols), lambda j, i: (i, j)), _const(after.shape)],
        out_specs=pl.BlockSpec((1, 2, half_rows, shard_cols), lambda j, i: (j, 0, 0, 0)),
        out_shape=jax.ShapeDtypeStruct((N_CHIPS, 2, half_rows, shard_cols), BF16),
        scratch_shapes=[pltpu.VMEM((D_MODEL, shard_cols), F32)],
        compiler_params=_params(("arbitrary", "arbitrary")),
    )(xt, dp, after)


def _bucket_onehot():
    t = jnp.arange(CHUNK, dtype=jnp.int32)[None, :]
    j = jnp.arange(2 * CHUNK, dtype=jnp.int32)[:, None]
    dist = jnp.clip(t + CHUNK - j, 0, CHUNK - 1)
    max_exact = REL_BUCKETS // 2
    df = jnp.maximum(dist, 1).astype(F32)
    large = max_exact + (jnp.log(df / max_exact) / math.log(CHUNK / max_exact)
                         * (REL_BUCKETS - max_exact)).astype(jnp.int32)
    bucket = jnp.where(dist < max_exact, dist, jnp.minimum(large, REL_BUCKETS - 1))
    onehot = bucket.reshape(1, -1) == jnp.arange(REL_BUCKETS, dtype=jnp.int32)[:, None]
    return onehot.astype(F32)


def _bias_expand(rel_t, onehot):
    def body(rel_ref, oh_ref, out_ref):
        out_ref[...] = jnp.dot(rel_ref[...], oh_ref[...], preferred_element_type=F32,
                               precision=lax.Precision.HIGHEST)

    return pl.pallas_call(
        body, name="bias_expand",
        out_shape=jax.ShapeDtypeStruct((N_HEADS, onehot.shape[1]), F32),
    )(rel_t, onehot)


def _bias_reduce(onehot, dbias):
    def body(oh_ref, db_ref, out_ref):
        out_ref[...] = lax.dot_general(oh_ref[...], db_ref[...], (((1,), (1,)), ((), ())),
                                       preferred_element_type=F32, precision=lax.Precision.HIGHEST)

    return pl.pallas_call(
        body, name="bias_reduce",
        out_shape=jax.ShapeDtypeStruct((REL_BUCKETS, N_HEADS), F32),
    )(onehot, dbias)


def _place():
    return lax.axis_index("x"), lax.axis_index("y"), lax.axis_index("c")


RELAY_PIECES = 4


def _shard_window(full_ref, shard_shape, col_sharded, s, half, piece=None):
    rows, cols = shard_shape
    if half is None:
        start, size = 0, rows
    elif piece is None:
        start, size = half * (rows // 2), rows // 2
    else:
        size = rows // 2 // RELAY_PIECES
        start = (half * RELAY_PIECES + piece) * size
    if col_sharded:
        return full_ref.at[pl.ds(start, size), pl.ds(s * cols, cols)]
    return full_ref.at[pl.ds(s * rows + start, size), :]


def _other_chips(x, y):
    return [(1 - x, y), (x, 1 - y), (1 - x, 1 - y)]


def _gather_weights(shards, col_sharded, fetch, ln_shard):
    n_w = len(shards)
    fetched = [w for w in range(n_w) if fetch[w]]
    full_shapes = []
    for w, cs in zip(shards, col_sharded):
        r, c = w.shape
        full_shapes.append((r, c * N_CHIPS) if cs else (r * N_CHIPS, c))

    def body(*refs):
        in_refs = refs[:n_w]
        ln_ref = refs[n_w]
        full_refs = refs[n_w + 1:2 * n_w + 1]
        ln_full = refs[2 * n_w + 1]
        stage = refs[2 * n_w + 2:3 * n_w + 2]
        send_sems, recv_sems, local_sems, ln_send, ln_recv = refs[3 * n_w + 2:]
        x, y, c = _place()
        s_me = 2 * x + y
        chips = _other_chips(x, y)

        def shard_window(w, s, half, piece=None):
            return _shard_window(full_refs[w], shards[w].shape, col_sharded[w], s, half, piece)

        def stage_piece(w, half, piece):
            rows = shards[w].shape[0] // 2 // RELAY_PIECES
            return stage[w].at[pl.ds((half * RELAY_PIECES + piece) * rows, rows), :]

        def ici_copy(w, k, sender_shard, piece):
            idx = (w * 3 + k) * RELAY_PIECES + piece
            return pltpu.make_async_remote_copy(
                src_ref=stage_piece(w, c, piece), dst_ref=shard_window(w, sender_shard, c, piece),
                send_sem=send_sems.at[idx], recv_sem=recv_sems.at[idx],
                device_id=(*chips[k], c), device_id_type=MESH)

        def d2d_copy(w, k, half):
            s_k = 2 * chips[k][0] + chips[k][1]
            win = shard_window(w, s_k, half)
            idx = 3 * RELAY_PIECES * n_w + w * 3 + k
            return pltpu.make_async_remote_copy(
                src_ref=win, dst_ref=win, send_sem=send_sems.at[idx], recv_sem=recv_sems.at[idx],
                device_id=(x, y, 1 - c), device_id_type=MESH)

        def ln_copy(k, slot):
            return pltpu.make_async_remote_copy(
                src_ref=ln_ref, dst_ref=ln_full.at[slot], send_sem=ln_send.at[k], recv_sem=ln_recv.at[k],
                device_id=(*chips[k], c), device_id_type=MESH)

        for w in range(n_w):
            stage[w][...] = in_refs[w][...].astype(BF16)
        own = [pltpu.make_async_copy(stage[w], shard_window(w, s_me, None), local_sems.at[w]) for w in range(n_w)]
        for cp in own:
            cp.start()
        ln_full[s_me] = ln_ref[...]
        def shard_of(k):
            return 2 * chips[k][0] + chips[k][1]

        relay_from = jnp.where(c == 0, shard_of(0), shard_of(1))
        relay_to = (jnp.where(c == 0, x, 1 - x), jnp.where(c == 0, 1 - y, y), c)

        def relay_copy(w, sender_shard, piece):
            win = shard_window(w, sender_shard, c, piece)
            idx = (w * 3 + 2) * RELAY_PIECES + piece
            return pltpu.make_async_remote_copy(
                src_ref=win, dst_ref=win, send_sem=send_sems.at[idx], recv_sem=recv_sems.at[idx],
                device_id=relay_to, device_id_type=MESH)

        pieces = range(RELAY_PIECES)
        first = [ici_copy(w, k, s_me, q) for w in fetched for q in pieces for k in range(2)]
        first += [ln_copy(k, s_me) for k in range(3)]
        for cp in first:
            cp.start()
        passed = []
        for w in fetched:
            for q in pieces:
                for k in range(2):
                    ici_copy(w, k, shard_of(k), q).wait_recv()
                relay = relay_copy(w, relay_from, q)
                relay.start()
                passed.append(relay)
            for k in range(2):
                fwd = d2d_copy(w, k, c)
                fwd.start()
                passed.append(fwd)
        for w in fetched:
            for q in pieces:
                relay_copy(w, shard_of(2), q).wait_recv()
            fwd = d2d_copy(w, 2, c)
            fwd.start()
            passed.append(fwd)
        for w in fetched:
            for k in range(3):
                d2d_copy(w, k, 1 - c).wait_recv()
        for k in range(3):
            ln_copy(k, 2 * chips[k][0] + chips[k][1]).wait_recv()
        for cp in first + passed:
            cp.wait_send()
        for cp in own:
            cp.wait()

    vmem = pl.BlockSpec(memory_space=pltpu.VMEM)
    hbm = pl.BlockSpec(memory_space=pl.ANY)
    return pl.pallas_call(
        body, name="gather_weights",
        in_specs=[vmem] * (n_w + 1),
        out_specs=[hbm] * n_w + [vmem],
        out_shape=[jax.ShapeDtypeStruct(s, BF16) for s in full_shapes]
        + [jax.ShapeDtypeStruct((N_CHIPS,) + ln_shard.shape, F32)],
        scratch_shapes=[pltpu.VMEM(w.shape, BF16) for w in shards]
        + [pltpu.SemaphoreType.DMA((3 * (RELAY_PIECES + 1) * n_w,)),
           pltpu.SemaphoreType.DMA((3 * (RELAY_PIECES + 1) * n_w,)),
           pltpu.SemaphoreType.DMA((n_w,)), pltpu.SemaphoreType.DMA((3,)), pltpu.SemaphoreType.DMA((3,))],
        compiler_params=pltpu.CompilerParams(vmem_limit_bytes=VMEM_LIMIT),
    )(*shards, ln_shard)


def _fetch_copy(full_ref, shard_shape, col_sharded, sender_shard, send_sems, recv_sems, idx, chip, c):
    win = _shard_window(full_ref, shard_shape, col_sharded, sender_shard, None)
    return pltpu.make_async_remote_copy(src_ref=win, dst_ref=win, send_sem=send_sems.at[idx],
                                        recv_sem=recv_sems.at[idx], device_id=(*chip, c), device_id_type=MESH)


def _fetch_start(fulls, shard_shapes, col_sharded):
    n = len(fulls)

    def body(*refs):
        full = refs[:n]
        send_sems, recv_sems = refs[n], refs[n + 1]
        token = refs[-1]
        x, y, c = _place()
        for w in range(n):
            for k, chip in enumerate(_other_chips(x, y)):
                _fetch_copy(full[w], shard_shapes[w], col_sharded[w], 2 * x + y, send_sems, recv_sems, w * 3 + k,
                            chip, c).start()
        token[...] = jnp.zeros_like(token)

    outs = pl.pallas_call(
        body, name="fetch_start",
        out_shape=(pltpu.SemaphoreType.DMA((3 * n,)), pltpu.SemaphoreType.DMA((3 * n,)),
                   *[pltpu.HBM(f.shape, f.dtype) for f in fulls], jax.ShapeDtypeStruct((8, 128), F32)),
        in_specs=[_HBM] * n,
        out_specs=(_SEM, _SEM, *([_HBM] * n), pl.BlockSpec(memory_space=pltpu.VMEM)),
        input_output_aliases={i: 2 + i for i in range(n)},
        compiler_params=pltpu.CompilerParams(has_side_effects=pltpu.SideEffectType.DATAFLOW_SIDE_EFFECTING),
    )(*[pltpu.with_memory_space_constraint(f, pltpu.HBM) for f in fulls])
    return dict(send=outs[0], recv=outs[1], full=list(outs[2:2 + n])), outs[-1]


def _fetch_wait(group, shard_shapes, col_sharded, after):
    n = len(group["full"])

    def body(*refs):
        full = refs[:n]
        send_sems, recv_sems = refs[n], refs[n + 1]
        x, y, c = _place()
        for w in range(n):
            for k, chip in enumerate(_other_chips(x, y)):
                _fetch_copy(full[w], shard_shapes[w], col_sharded[w], 2 * x + y, send_sems, recv_sems, w * 3 + k,
                            chip, c).wait_send()
                _fetch_copy(full[w], shard_shapes[w], col_sharded[w], 2 * chip[0] + chip[1], send_sems, recv_sems,
                            w * 3 + k, chip, c).wait_recv()

    outs = pl.pallas_call(
        body, name="fetch_wait", out_shape=tuple(pltpu.HBM(f.shape, f.dtype) for f in group["full"]),
        in_specs=[_HBM] * n + [_SEM, _SEM, pl.BlockSpec(memory_space=pl.ANY)],
        out_specs=tuple([_HBM] * n), input_output_aliases={i: i for i in range(n)},
        compiler_params=pltpu.CompilerParams(has_side_effects=pltpu.SideEffectType.DATAFLOW_SIDE_EFFECTING),
    )(*group["full"], group["send"], group["recv"], after)
    return list(outs)


_HBM = pl.BlockSpec(memory_space=pltpu.HBM)
_SEM = pl.BlockSpec(memory_space=pltpu.SEMAPHORE)
_N_PEER = N_DEV - 1


def _peer(x, y, c, k):
    return (x + (k >> 2)) % 2, (y + ((k >> 1) & 1)) % 2, (c + (k & 1)) % 2


def _exchange_copy(src_ref, land_ref, sliced, send_sems, recv_sems, idx, x, y, c, k):
    px, py, pc = _peer(x, y, c, k)
    src = src_ref.at[4 * px + 2 * py + pc] if sliced else src_ref
    return pltpu.make_async_remote_copy(
        src_ref=src, dst_ref=land_ref.at[4 * x + 2 * y + c],
        send_sem=send_sems.at[idx], recv_sem=recv_sems.at[idx], device_id=(px, py, pc), device_id_type=MESH)


def _exchange_start(tag, arrays, sliced):
    n = len(arrays)
    lands = [lax.empty(a.shape if s else (N_DEV,) + a.shape, a.dtype) for a, s in zip(arrays, sliced)]

    def body(*refs):
        src, land = refs[:n], refs[n:2 * n]
        send_sems, recv_sems = refs[2 * n], refs[2 * n + 1]
        token = refs[-1]
        x, y, c = _place()
        for w in range(n):
            for k in range(1, N_DEV):
                _exchange_copy(src[w], land[w], sliced[w], send_sems, recv_sems, w * _N_PEER + k - 1, x, y, c, k).start()
        token[...] = jnp.zeros_like(token)

    outs = pl.pallas_call(
        body, name="exchange_start_" + tag,
        out_shape=(pltpu.SemaphoreType.DMA((n * _N_PEER,)), pltpu.SemaphoreType.DMA((n * _N_PEER,)),
                   *[pltpu.HBM(a.shape, a.dtype) for a in arrays], *[pltpu.HBM(l.shape, l.dtype) for l in lands],
                   jax.ShapeDtypeStruct((8, 128), F32)),
        in_specs=[_HBM] * (2 * n),
        out_specs=(_SEM, _SEM, *([_HBM] * (2 * n)), pl.BlockSpec(memory_space=pltpu.VMEM)),
        input_output_aliases={i: 2 + i for i in range(2 * n)},
        compiler_params=pltpu.CompilerParams(has_side_effects=pltpu.SideEffectType.DATAFLOW_SIDE_EFFECTING),
    )(*[pltpu.with_memory_space_constraint(a, pltpu.HBM) for a in arrays],
      *[pltpu.with_memory_space_constraint(l, pltpu.HBM) for l in lands])
    return dict(send=outs[0], recv=outs[1], src=list(outs[2:2 + n]), land=list(outs[2 + n:2 + 2 * n]),
                sliced=list(sliced)), outs[-1]


def _exchange_wait(tag, groups, after):
    counts = [len(g["src"]) for g in groups]
    total = sum(counts)

    def body(*refs):
        pos = 0
        x, y, c = _place()
        for g, n in zip(groups, counts):
            src, land = refs[pos:pos + n], refs[pos + n:pos + 2 * n]
            send_sems, recv_sems = refs[pos + 2 * n], refs[pos + 2 * n + 1]
            pos += 2 * n + 2
            for w in range(n):
                for k in range(1, N_DEV):
                    cp = _exchange_copy(src[w], land[w], g["sliced"][w], send_sems, recv_sems,
                                        w * _N_PEER + k - 1, x, y, c, k)
                    cp.wait_send()
                    cp.wait_recv()

    operands, in_specs, aliases, out_shape = [], [], {}, []
    for g in groups:
        for a in g["src"] + g["land"]:
            aliases[len(operands)] = len(out_shape)
            out_shape.append(pltpu.HBM(a.shape, a.dtype))
            operands.append(a)
            in_specs.append(_HBM)
        operands += [g["send"], g["recv"]]
        in_specs += [_SEM, _SEM]
    operands.append(after)
    in_specs.append(pl.BlockSpec(memory_space=pl.ANY))
    outs = pl.pallas_call(
        body, name="exchange_wait_" + tag, out_shape=tuple(out_shape), in_specs=in_specs,
        out_specs=tuple([_HBM] * (2 * total)), input_output_aliases=aliases,
        compiler_params=pltpu.CompilerParams(has_side_effects=pltpu.SideEffectType.DATAFLOW_SIDE_EFFECTING),
    )(*operands)
    srcs, lands, pos = [], [], 0
    for n in counts:
        srcs += list(outs[pos:pos + n])
        lands += list(outs[pos + n:pos + 2 * n])
        pos += 2 * n
    return srcs, lands


def _sum_and_swap(tag, pieces, lands, small=None, small_land=None):
    n_w = len(pieces)
    n_small = 0 if small is None else 1

    def body(*refs):
        g_refs, land_refs = refs[:n_w], refs[n_w:2 * n_w]
        pos = 2 * n_w + 2 * n_small
        out_refs = refs[pos:pos + n_w]
        pos += n_w + n_small
        bufs = refs[pos:pos + n_w]
        load_sems, swap_send, swap_recv = refs[pos + n_w + 2 * n_small:]
        x, y, c = _place()
        me = 4 * x + 2 * y + c

        def slot(k):
            px, py, pc = _peer(x, y, c, k)
            return 4 * px + 2 * py + pc

        def swap_copy(w, half):
            rows = pieces[w].shape[1]
            win = out_refs[w].at[pl.ds(pl.multiple_of(half * rows, rows), rows), :]
            return pltpu.make_async_remote_copy(
                src_ref=win, dst_ref=win, send_sem=swap_send.at[w], recv_sem=swap_recv.at[w],
                device_id=(x, y, 1 - c), device_id_type=MESH)

        loads = []
        for w in range(n_w):
            per_w = [pltpu.make_async_copy(g_refs[w].at[me], bufs[w].at[me], load_sems.at[w * N_DEV])]
            per_w += [pltpu.make_async_copy(land_refs[w].at[slot(k)], bufs[w].at[slot(k)], load_sems.at[w * N_DEV + k])
                      for k in range(1, N_DEV)]
            loads.append(per_w)
        small_loads = []
        if n_small:
            small_ref, small_land_ref = refs[2 * n_w], refs[2 * n_w + 1]
            small_out = refs[2 * n_w + 2 + n_w]
            small_buf, small_sems = refs[pos + n_w], refs[pos + n_w + 1]
            small_loads = [pltpu.make_async_copy(small_land_ref.at[slot(k)], small_buf.at[slot(k)],
                                                 small_sems.at[k - 1]) for k in range(1, N_DEV)]
        for cp in [cp for per_w in loads for cp in per_w] + small_loads:
            cp.start()
        if n_small:
            small_buf[me] = small_ref[...]
        swaps = []
        for w in range(n_w):
            for cp in loads[w]:
                cp.wait()
            rows = pieces[w].shape[1]
            total = bufs[w][0].astype(F32)
            for p in range(1, N_DEV):
                total += bufs[w][p].astype(F32)
            out_refs[w][pl.ds(pl.multiple_of(c * rows, rows), rows), :] = total
            sw = swap_copy(w, c)
            sw.start()
            swaps.append(sw)
        if n_small:
            for cp in small_loads:
                cp.wait()
            total = small_buf[0]
            for p in range(1, N_DEV):
                total += small_buf[p]
            small_out[...] = total
        for w in range(n_w):
            swap_copy(w, 1 - c).wait_recv()
        for sw in swaps:
            sw.wait_send()

    vmem = pl.BlockSpec(memory_space=pltpu.VMEM)
    hbm = pl.BlockSpec(memory_space=pl.ANY)
    small_args = [small, small_land] if n_small else []
    small_shapes = [jax.ShapeDtypeStruct(small.shape, F32)] if n_small else []
    small_scratch = ([pltpu.VMEM((N_DEV,) + small.shape, F32), pltpu.SemaphoreType.DMA((_N_PEER,))]
                     if n_small else [])
    return pl.pallas_call(
        body, name="sum_and_swap_" + tag,
        in_specs=[hbm] * (2 * n_w) + [vmem, hbm] * n_small,
        out_specs=[vmem] * (n_w + n_small),
        out_shape=[jax.ShapeDtypeStruct((2 * p.shape[1], p.shape[2]), F32) for p in pieces] + small_shapes,
        scratch_shapes=[pltpu.VMEM(p.shape, BF16) for p in pieces] + small_scratch
        + [pltpu.SemaphoreType.DMA((n_w * N_DEV,)), pltpu.SemaphoreType.DMA((n_w,)),
           pltpu.SemaphoreType.DMA((n_w,))],
        compiler_params=pltpu.CompilerParams(vmem_limit_bytes=VMEM_LIMIT),
    )(*pieces, *lands, *small_args)


def _adamw_update(w_ref, g_ref, m_ref, v_ref, d_ref, nm_ref, nv_ref):
    c1 = 1.0 - ADAM_B1 ** ADAM_STEP
    c2 = 1.0 - ADAM_B2 ** ADAM_STEP
    g_t = g_ref[...]
    nm = ADAM_B1 * m_ref[...] + (1.0 - ADAM_B1) * g_t
    nv = ADAM_B2 * v_ref[...] + (1.0 - ADAM_B2) * (g_t * g_t)
    d_ref[...] = -ADAM_LR * ((nm / c1) / (jnp.sqrt(nv / c2) + ADAM_EPS) + ADAM_WD * w_ref[...])
    nm_ref[...] = nm
    nv_ref[...] = nv


def _adamw_small(items):
    n = len(items)
    flat, shapes = [], []
    for w, g, m, v in items:
        cols = w.shape[-1]
        shapes.append(w.shape)
        flat += [a.reshape(w.size // cols, cols) for a in (w, g, m, v)]

    def body(*refs):
        for k in range(n):
            _adamw_update(*refs[4 * k:4 * k + 4], *refs[4 * n + 3 * k:4 * n + 3 * k + 3])

    outs = pl.pallas_call(
        body, name="adamw_small",
        out_shape=[jax.ShapeDtypeStruct(flat[4 * k].shape, F32) for k in range(n) for _ in range(3)],
    )(*flat)
    return [tuple(o.reshape(shapes[k]) for o in outs[3 * k:3 * k + 3]) for k in range(n)]


def _adamw(label, w, g, m, v):
    shape = w.shape
    cols = shape[-1]
    rows = w.size // cols
    args = [a.reshape(rows, cols) for a in (w, g, m, v)]

    def body(w_ref, g_ref, m_ref, v_ref, g_out, d_ref, nm_ref, nv_ref):
        g_out[...] = g_ref[...]
        _adamw_update(w_ref, g_ref, m_ref, v_ref, d_ref, nm_ref, nv_ref)

    block_rows = 256 if rows % 256 == 0 and rows > 256 else rows
    spec = pl.BlockSpec((block_rows, cols), lambda i: (i, 0))
    outs = pl.pallas_call(
        body, name="adamw_" + label, grid=(rows // block_rows,),
        in_specs=[spec] * 4, out_specs=[spec] * 4,
        out_shape=[jax.ShapeDtypeStruct((rows, cols), F32)] * 4,
        compiler_params=_params(),
    )(*args)
    return [o.reshape(shape) for o in outs]


def _no_send(tag, arrays, sliced):
    return jnp.zeros((8, 128), F32)


def _local_step(x, tgt, w_in_a, later_weights, first_after, sgu_ln_g, sgu_ln_b, w_spatial, b_spatial,
                attn_sinks, rel_bias, post_ln_g, post_ln_b, send=_no_send):
    bsp_t = b_spatial.T
    g1, b1 = post_ln_g[0:1], post_ln_b[0:1]
    g2, b2 = post_ln_g[1:2], post_ln_b[1:2]
    onehot = _bucket_onehot()
    bias = _bias_expand(rel_bias.T, onehot).reshape(N_HEADS, 2 * CHUNK, CHUNK)
    win = _window_tables()

    xt, u, vh, z, rv, y = _layer_a_fwd(x, w_in_a, sgu_ln_g, sgu_ln_b, w_spatial, bsp_t, first_after)
    w_out_a, w_kv, w_in_b, w_out_b = later_weights(y)
    xh1, rstd1, q, zb, kd, vd = _layer_b_proj(x, y, w_out_a, g1, b1, w_in_b, w_kv)
    o, probs, sink_probs, dr2, loss_vec, dg2, db2 = _layer_b_fwd(q, zb, kd, vd, bias, win, attn_sinks, xh1, g1, b1,
                                                                 w_out_b, g2, b2, tgt)
    dq, dzb, dkd, dvd, carry_k, carry_v, gw_out_b, dsink, dbias = _layer_b_bwd_attn(
        dr2, zb, o, q, kd, vd, probs, sink_probs, w_out_b)
    dr1, dg1, db1, gw_in_b, gw_kv = _layer_b_bwd_proj(xh1, rstd1, g1, b1, dr2, dq, dzb, dkd, dvd, carry_k, carry_v,
                                                      w_in_b, w_kv)
    gw_out_b = gw_out_b.reshape(N_DEV, -1, D_MODEL)
    gw_kv = gw_kv.reshape(N_DEV, -1, 2 * PAIR)
    after = send("b", [gw_out_b, gw_in_b, gw_kv], [True, True, True])
    dp, gw_out_a, dws, dbsp, dgs, dbs = _layer_a_bwd_mix(dr1, u, vh, z, y, rv, w_out_a, sgu_ln_g, sgu_ln_b,
                                                         w_spatial, bsp_t, after)
    gw_out_a = gw_out_a.reshape(N_DEV, -1, D_MODEL)
    drel = _bias_reduce(onehot, dbias.reshape(N_HEADS, -1))
    dsink = jnp.sum(dsink.reshape(N_HEADS, CHUNK), axis=1).reshape(1, N_HEADS)
    loss = ((0.5 / D_MODEL) * jnp.sum(loss_vec)).reshape(1, 1)
    small = dict(w_spatial=dws, b_spatial=dbsp[:, 0:A_GROUPS].T, attn_sinks=dsink,
                 rel_bias=drel, post_ln_g=jnp.concatenate([dg1, dg2], axis=0),
                 post_ln_b=jnp.concatenate([db1, db2], axis=0), sgu_ln_g=dgs, sgu_ln_b=dbs, loss=loss)
    after = send("a_out", [gw_out_a, _pack_small(small)], [True, False])
    gw_in_a = _layer_a_bwd_win(xt, dp, after).reshape(N_DEV, D_MODEL // 2, -1)
    after = send("a_in", [gw_in_a], [True])
    after, updates = after if isinstance(after, tuple) else (after, ())
    grad_x, *updated = _layer_a_bwd_dx(dr1, dp, w_in_a, after, updates)

    pieces = [gw_in_a, gw_out_a, gw_kv, gw_in_b, gw_out_b]
    return grad_x, pieces, small, updated


_SMALL_SHAPES = dict(w_spatial=(A_GROUPS, CHUNK, CHUNK), b_spatial=(A_GROUPS, CHUNK), attn_sinks=(1, N_HEADS),
                     rel_bias=(REL_BUCKETS, N_HEADS), post_ln_g=(2, D_MODEL), post_ln_b=(2, D_MODEL),
                     sgu_ln_g=(1, A_WIDTH), sgu_ln_b=(1, A_WIDTH), loss=(1, 1))
_SMALL_ORDER = tuple(_SMALL_SHAPES)


def _small_rows(name):
    size = math.prod(_SMALL_SHAPES[name])
    return -(-size // (_LANES * _SUBLANES)) * _SUBLANES


def _pack_small(small):
    parts = []
    for name in _SMALL_ORDER:
        assert small[name].shape == _SMALL_SHAPES[name], (name, small[name].shape)
        flat = small[name].reshape(-1)
        flat = jnp.pad(flat, (0, _small_rows(name) * _LANES - flat.size))
        parts.append(flat.reshape(-1, _LANES))
    return jnp.concatenate(parts, axis=0)


def _unpack_small(packed):
    out, row = {}, 0
    for name in _SMALL_ORDER:
        shape = _SMALL_SHAPES[name]
        out[name] = packed[row:row + _small_rows(name)].reshape(-1)[:math.prod(shape)].reshape(shape)
        row += _small_rows(name)
    return out


def kernel(x, w_in_a, sgu_ln_g, sgu_ln_b, w_spatial, b_spatial, w_out_a, w_kv, w_in_b, attn_sinks, rel_bias, w_out_b, post_ln_g, post_ln_b, loss_target, m_w_in_a, m_sgu_ln_g, m_sgu_ln_b, m_w_spatial, m_b_spatial, m_w_out_a, m_w_kv, m_w_in_b, m_attn_sinks, m_rel_bias, m_w_out_b, m_post_ln_g, m_post_ln_b, v_w_in_a, v_sgu_ln_g, v_sgu_ln_b, v_w_spatial, v_b_spatial, v_w_out_a, v_w_kv, v_w_in_b, v_attn_sinks, v_rel_bias, v_w_out_b, v_post_ln_g, v_post_ln_b):
    weights = dict(w_in_a=w_in_a, sgu_ln_g=sgu_ln_g, sgu_ln_b=sgu_ln_b, w_spatial=w_spatial, b_spatial=b_spatial,
                   w_out_a=w_out_a, w_kv=w_kv, w_in_b=w_in_b, attn_sinks=attn_sinks, rel_bias=rel_bias,
                   w_out_b=w_out_b, post_ln_g=post_ln_g, post_ln_b=post_ln_b)
    moments_m = dict(w_in_a=m_w_in_a, sgu_ln_g=m_sgu_ln_g, sgu_ln_b=m_sgu_ln_b, w_spatial=m_w_spatial,
                     b_spatial=m_b_spatial, w_out_a=m_w_out_a, w_kv=m_w_kv, w_in_b=m_w_in_b,
                     attn_sinks=m_attn_sinks, rel_bias=m_rel_bias, w_out_b=m_w_out_b, post_ln_g=m_post_ln_g,
                     post_ln_b=m_post_ln_b)
    moments_v = dict(w_in_a=v_w_in_a, sgu_ln_g=v_sgu_ln_g, sgu_ln_b=v_sgu_ln_b, w_spatial=v_w_spatial,
                     b_spatial=v_b_spatial, w_out_a=v_w_out_a, w_kv=v_w_kv, w_in_b=v_w_in_b,
                     attn_sinks=v_attn_sinks, rel_bias=v_rel_bias, w_out_b=v_w_out_b, post_ln_g=v_post_ln_g,
                     post_ln_b=v_post_ln_b)
    order = ("w_in_a", "sgu_ln_g", "sgu_ln_b", "w_spatial", "b_spatial", "w_out_a", "w_kv", "w_in_b", "attn_sinks",
             "rel_bias", "w_out_b", "post_ln_g", "post_ln_b")

    shard_index = 2 * lax.axis_index("x") + lax.axis_index("y")
    ln_shard = jnp.concatenate([sgu_ln_g, sgu_ln_b], axis=0)
    shards = [w_in_a[0], w_out_a[0], w_kv, w_in_b[0], w_out_b[0]]
    col_sharded = [True, False, False, True, False]
    full_in_a, *later, ln_full = _gather_weights(shards, col_sharded, [True, False, False, False, False], ln_shard)
    ln_full = jnp.transpose(ln_full, (1, 0, 2)).reshape(2, A_WIDTH)
    later_shapes = [s.shape for s in shards[1:]]
    fetch_group, fetch_token = _fetch_start(later, later_shapes, col_sharded[1:])

    def later_weights(y):
        return _fetch_wait(fetch_group, later_shapes, col_sharded[1:], y)

    groups, grads, deltas, new_m, new_v, scalars = {}, {}, {}, {}, {}, {}
    early = ("w_out_b", "w_in_b", "w_kv", "w_out_a")

    def two_dim(a):
        return a.reshape(-1, a.shape[-1])

    def send(tag, arrays, sliced):
        groups[tag], token = _exchange_start(tag, arrays, sliced)
        if tag != "a_in":
            return token
        srcs, lands = _exchange_wait("early", [groups["b"], groups["a_out"]], token)
        *reduced, packed_sum = _sum_and_swap("early", srcs[:4], lands[:4], srcs[4], lands[4])
        updates = [(two_dim(weights[n]), g, two_dim(moments_m[n]), two_dim(moments_v[n]))
                   for n, g in zip(early, reduced)]
        small_sum = _unpack_small(packed_sum)
        scalars["loss"] = small_sum["loss"].reshape(())
        shard_cols = sgu_ln_g.shape[1]
        for name in ("sgu_ln_g", "sgu_ln_b"):
            grads[name] = lax.dynamic_slice(small_sum[name], (0, shard_index * shard_cols), (1, shard_cols))
        small_names = ("sgu_ln_g", "sgu_ln_b", "w_spatial", "b_spatial", "attn_sinks", "rel_bias", "post_ln_g",
                       "post_ln_b")
        for name in small_names[2:]:
            grads[name] = small_sum[name].reshape(weights[name].shape)
        small_updates = _adamw_small([(weights[n], grads[n], moments_m[n], moments_v[n]) for n in small_names])
        for name, (d, nm, nv) in zip(small_names, small_updates):
            deltas[name], new_m[name], new_v[name] = d, nm, nv
        return packed_sum[0:_SUBLANES], updates

    grad_x, _, _, updated = _local_step(
        x[0], loss_target[0], full_in_a, later_weights, fetch_token, ln_full[0:1], ln_full[1:2], w_spatial[0],
        b_spatial[0], attn_sinks, rel_bias, post_ln_g, post_ln_b, send=send)
    for k, name in enumerate(early):
        grads[name], deltas[name], new_m[name], new_v[name] = [
            a.reshape(weights[name].shape) for a in updated[4 * k:4 * k + 4]]

    srcs, lands = _exchange_wait("late", [groups["a_in"]], grad_x)
    (g_in_a,) = _sum_and_swap("late", srcs, lands)
    grads["w_in_a"], deltas["w_in_a"], new_m["w_in_a"], new_v["w_in_a"] = _adamw(
        "w_in_a", w_in_a, g_in_a.reshape(w_in_a.shape), m_w_in_a, v_w_in_a)
    return (scalars["loss"], grad_x[None], *[grads[n] for n in order], *[deltas[n] for n in order],
            *[new_m[n] for n in order], *[new_v[n] for n in order])
```

```python
import math

import jax
import jax.numpy as jnp
from jax import lax
from jax.experimental import pallas as pl
from jax.experimental.pallas import tpu as pltpu

F32 = jnp.float32
BF16 = jnp.bfloat16

D_MODEL = 1024
A_WIDTH = 2048
A_GROUPS = 8
A_GROUP_DIM = 256
CHUNK = 128
N_HEADS = 16
N_KV = 2
HEAD_DIM = 64
PAIR = 2 * HEAD_DIM
B_WIDTH = 1024
REL_BUCKETS = 32
ALPHA = 4.0 ** 0.25
LN_EPS = 1e-5
NEG_INF = -1e30
SCALE = HEAD_DIM ** -0.5

ADAM_LR = 0.001
ADAM_B1 = 0.9
ADAM_B2 = 0.999
ADAM_EPS = 1e-08
ADAM_WD = 0.01
ADAM_STEP = 10

N_DEV = 8
N_CHIPS = 4
MESH = pl.DeviceIdType.MESH
VMEM_LIMIT = 56 * 1024 * 1024

TM_ATTN = 256
TM_MM = 512
TM_WIN = 1024
_LANES = 128
_SUBLANES = 8


def _dot(a, b):
    return jnp.dot(a, b, preferred_element_type=F32)


def _dot_nt(a, b):
    return lax.dot_general(a, b, (((1,), (1,)), ((), ())), preferred_element_type=F32)


def _dot_tn(a, b):
    return lax.dot_general(a, b, (((0,), (0,)), ((), ())), preferred_element_type=F32)


def _ln_fwd(r):
    mu = jnp.mean(r, axis=-1, keepdims=True)
    rc = r - mu
    var = jnp.mean(rc * rc, axis=-1, keepdims=True)
    rstd = lax.rsqrt(var + LN_EPS)
    return rc * rstd, rstd


def _ln_bwd(dxh, xh, rstd):
    m1 = jnp.mean(dxh, axis=-1, keepdims=True)
    m2 = jnp.mean(dxh * xh, axis=-1, keepdims=True)
    return rstd * (dxh - m1 - xh * m2)


def _silu_parts(z):
    sg = jax.nn.sigmoid(z)
    return z * sg, sg * (1.0 + z * (1.0 - sg))


def _dup_halves(blk):
    sw = pltpu.roll(blk, HEAD_DIM, 1)
    lo = lax.broadcasted_iota(jnp.int32, blk.shape, 1) < HEAD_DIM
    return jnp.where(lo, blk, sw), jnp.where(lo, sw, blk)


def _fold_halves(blk):
    return blk + pltpu.roll(blk, HEAD_DIM, 1)


def _resident(shape):
    nd = len(shape)
    return pl.BlockSpec(shape, lambda *_: (0,) * nd, pipeline_mode=pl.Buffered(1))


def _const(shape):
    nd = len(shape)
    return pl.BlockSpec(shape, lambda *_: (0,) * nd)


def _rows(tm, cols):
    return pl.BlockSpec((tm, cols), lambda i: (i, 0))


def _params(sem=("arbitrary",)):
    return pltpu.CompilerParams(dimension_semantics=sem, vmem_limit_bytes=VMEM_LIMIT)


def _spatial_mix(ws_ref, bsp_ref, vn, s_scr, n_chunks):
    tri = (lax.broadcasted_iota(jnp.int32, (CHUNK, CHUNK), 0)
           >= lax.broadcasted_iota(jnp.int32, (CHUNK, CHUNK), 1))
    for g in range(A_GROUPS):
        wsg = jnp.where(tri, ws_ref[g], 0.0).astype(BF16)
        cols = slice(g * A_GROUP_DIM, (g + 1) * A_GROUP_DIM)
        for ci in range(n_chunks):
            rows = slice(ci * CHUNK, (ci + 1) * CHUNK)
            s_scr[rows, cols] = _dot(wsg, vn[rows, cols]) + bsp_ref[:, g:g + 1]


def _layer_a_fwd(x, w_in, lng, lnb, ws, bsp_t, after):
    t_len = x.shape[0]
    tm = TM_ATTN

    def body(x_ref, win_ref, lng_ref, lnb_ref, ws_ref, bsp_ref, after_ref,
             xt_ref, u_ref, vh_ref, z_ref, rv_ref, y_ref, s_scr):
        x_t = x_ref[...]
        xb = x_t.astype(BF16)
        xt_ref[...] = x_t.T.astype(BF16)
        u = _dot(xb, win_ref[:, 0:A_WIDTH])
        v = _dot(xb, win_ref[:, A_WIDTH:2 * A_WIDTH])
        z = _dot(xb, win_ref[:, 2 * A_WIDTH:3 * A_WIDTH])
        vh, rv = _ln_fwd(v)
        vn = (vh * lng_ref[...] + lnb_ref[...]).astype(BF16)
        _spatial_mix(ws_ref, bsp_ref, vn, s_scr, tm // CHUNK)
        sz, _ = _silu_parts(z)
        y_ref[...] = (u * s_scr[...] * sz).astype(BF16)
        u_ref[...] = u.astype(BF16)
        vh_ref[...] = vh.astype(BF16)
        z_ref[...] = z.astype(BF16)
        rv_ref[...] = rv

    wide = jax.ShapeDtypeStruct((t_len, A_WIDTH), BF16)
    return pl.pallas_call(
        body, name="layer_a_fwd", grid=(t_len // tm,),
        in_specs=[_rows(tm, D_MODEL), _resident(w_in.shape), _const(lng.shape), _const(lnb.shape), _const(ws.shape),
                  _const(bsp_t.shape), _const(after.shape)],
        out_specs=[pl.BlockSpec((D_MODEL, tm), lambda i: (0, i)), _rows(tm, A_WIDTH), _rows(tm, A_WIDTH),
                   _rows(tm, A_WIDTH), _rows(tm, 1), _rows(tm, A_WIDTH)],
        out_shape=[jax.ShapeDtypeStruct((D_MODEL, t_len), BF16), wide, wide, wide,
                   jax.ShapeDtypeStruct((t_len, 1), F32), wide],
        scratch_shapes=[pltpu.VMEM((tm, A_WIDTH), F32)],
        compiler_params=_params(),
    )(x, w_in, lng, lnb, ws, bsp_t, after)


def _layer_b_proj(x, y, w_out_a, g1, b1, w_in, w_kv):
    t_len = x.shape[0]
    tm = TM_MM

    def body(x_ref, y_ref, wout_ref, g_ref, b_ref, win_ref, wkv_ref, xh_ref, r1_ref, q_ref, z_ref, kd_ref, vd_ref):
        xh, r1 = _ln_fwd(ALPHA * x_ref[...] + _dot(y_ref[...], wout_ref[...]))
        xh_ref[...] = xh
        r1_ref[...] = r1
        h1 = (xh * g_ref[...] + b_ref[...]).astype(BF16)
        q_ref[...] = (_dot(h1, win_ref[:, 0:B_WIDTH]) * SCALE).astype(BF16)
        z_ref[...] = _dot(h1, win_ref[:, B_WIDTH:2 * B_WIDTH]).astype(BF16)
        kv = _dot(h1, wkv_ref[...])
        k0, k1 = _dup_halves(kv[:, 0:PAIR])
        v0, v1 = _dup_halves(kv[:, PAIR:2 * PAIR])
        kd_ref[:, 0:PAIR] = k0.astype(BF16)
        kd_ref[:, PAIR:2 * PAIR] = k1.astype(BF16)
        vd_ref[:, 0:PAIR] = v0.astype(BF16)
        vd_ref[:, PAIR:2 * PAIR] = v1.astype(BF16)

    return pl.pallas_call(
        body, name="layer_b_proj", grid=(t_len // tm,),
        in_specs=[_rows(tm, D_MODEL), _rows(tm, A_WIDTH), _resident(w_out_a.shape), _const(g1.shape),
                  _const(b1.shape), _resident(w_in.shape), _resident(w_kv.shape)],
        out_specs=[_rows(tm, D_MODEL), _rows(tm, 1), _rows(tm, B_WIDTH), _rows(tm, B_WIDTH), _rows(tm, 2 * PAIR),
                   _rows(tm, 2 * PAIR)],
        out_shape=[jax.ShapeDtypeStruct((t_len, D_MODEL), F32), jax.ShapeDtypeStruct((t_len, 1), F32),
                   jax.ShapeDtypeStruct((t_len, B_WIDTH), BF16), jax.ShapeDtypeStruct((t_len, B_WIDTH), BF16),
                   jax.ShapeDtypeStruct((t_len, 2 * PAIR), BF16), jax.ShapeDtypeStruct((t_len, 2 * PAIR), BF16)],
        compiler_params=_params(),
    )(x, y, w_out_a, g1, b1, w_in, w_kv)


GROUP = N_HEADS // N_KV
GROUP_Q = GROUP * CHUNK


def _window_tables():
    j = jnp.arange(2 * CHUNK, dtype=jnp.int32)[:, None]
    t = jnp.arange(CHUNK, dtype=jnp.int32)[None, :]
    dist = t + CHUNK - j
    inside = (dist >= 0) & (dist < CHUNK)
    return jnp.stack([inside & (j >= CHUNK), inside]).astype(F32)


def _band(ref, chunk_index, kvh):
    prev0 = pl.multiple_of(jnp.maximum(chunk_index - 1, 0) * CHUNK, CHUNK)
    cur0 = pl.multiple_of(chunk_index * CHUNK, CHUNK)
    cols = slice(kvh * PAIR, (kvh + 1) * PAIR)
    return jnp.concatenate([ref[pl.ds(prev0, CHUNK), cols], ref[pl.ds(cur0, CHUNK), cols]], axis=0)


def _group_tables(bias_ref, win_ref, sink_ref, chunk_index, kvh):
    bias = jnp.concatenate([bias_ref[kvh * GROUP + j] for j in range(GROUP)], axis=1)
    win = win_ref[jnp.minimum(chunk_index, 1)]
    mask = jnp.concatenate([win] * GROUP, axis=1) > 0.5
    sink = jnp.concatenate([jnp.full((1, CHUNK), sink_ref[0, kvh * GROUP + j], F32) for j in range(GROUP)], axis=1)
    return bias, mask, sink


def _attn_probs(qs, kband, bias, mask, sink):
    logits = jnp.where(mask, _dot_nt(kband, qs) + bias, NEG_INF)
    m = jnp.maximum(jnp.max(logits, axis=0, keepdims=True), sink)
    e = jnp.exp(logits - m)
    es = jnp.exp(sink - m)
    inv = 1.0 / (jnp.sum(e, axis=0, keepdims=True) + es)
    return e * inv, es * inv


def _half_mask():
    return lax.broadcasted_iota(jnp.int32, (CHUNK, PAIR), 1) < HEAD_DIM


def _stack_heads(src_ref, rows, kvh, dst_scr, lo):
    for j in range(GROUP):
        h = kvh * GROUP + j
        blk = src_ref[rows, (h // 2) * PAIR:(h // 2 + 1) * PAIR].astype(F32)
        keep = lo if h % 2 == 0 else ~lo
        dst_scr[j * CHUNK:(j + 1) * CHUNK, :] = jnp.where(keep, blk, 0.0).astype(BF16)


def _probs_spec(tm):
    return pl.BlockSpec((tm // CHUNK, N_KV, 2 * CHUNK, GROUP_Q), lambda i: (i, 0, 0, 0))


def _sink_probs_spec():
    return pl.BlockSpec((1, 8, GROUP_Q), lambda i: (i, 0, 0))


def _unstack_pairs(stacked, pp, lo):
    return jnp.where(lo, stacked[(2 * pp) * CHUNK:(2 * pp + 1) * CHUNK], stacked[(2 * pp + 1) * CHUNK:(2 * pp + 2) * CHUNK])


def _layer_b_fwd(q, zb, kd, vd, bias, win, sinks, xh1, g1, b1, w_out, g2, b2, tgt):
    t_len = q.shape[0]
    tm = TM_ATTN

    def body(q_ref, z_ref, kd_ref, vd_ref, bias_ref, win_ref, sink_ref, xh_ref, g1_ref, b1_ref, wout_ref, g2_ref,
             b2_ref, tgt_ref, o_ref, p_ref, ps_ref, dr_ref, loss_ref, dg_ref, db_ref, o_scr, qs_scr):
        i = pl.program_id(0)

        @pl.when(i == 0)
        def _():
            loss_ref[...] = jnp.zeros_like(loss_ref)
            dg_ref[...] = jnp.zeros_like(dg_ref)
            db_ref[...] = jnp.zeros_like(db_ref)

        lo = _half_mask()
        ps_ref[...] = jnp.zeros_like(ps_ref)
        for ci in range(tm // CHUNK):
            cg = i * (tm // CHUNK) + ci
            rows = slice(ci * CHUNK, (ci + 1) * CHUNK)
            for kvh in range(N_KV):
                kband = _band(kd_ref, cg, kvh)
                vband = _band(vd_ref, cg, kvh)
                bias_g, mask, sink = _group_tables(bias_ref, win_ref, sink_ref, cg, kvh)
                _stack_heads(q_ref, rows, kvh, qs_scr, lo)
                p, p_sink = _attn_probs(qs_scr[...], kband, bias_g, mask, sink)
                p = p.astype(BF16)
                p_ref[ci, kvh] = p
                ps_ref[0, ci * N_KV + kvh:ci * N_KV + kvh + 1, :] = p_sink
                o_stack = _dot_tn(p, vband)
                for pp in range(GROUP // 2):
                    pair = kvh * (GROUP // 2) + pp
                    o_scr[rows, pair * PAIR:(pair + 1) * PAIR] = _unstack_pairs(o_stack, pp, lo)
        o = o_scr[...]
        o_ref[...] = o.astype(BF16)
        sz, _ = _silu_parts(z_ref[...].astype(F32))
        y = (o * sz).astype(BF16)
        h1 = xh_ref[...] * g1_ref[...] + b1_ref[...]
        r = ALPHA * h1 + _dot(y, wout_ref[...])
        xh2, rstd2 = _ln_fwd(r)
        diff = xh2 * g2_ref[...] + b2_ref[...] - tgt_ref[...]
        loss_ref[...] += jnp.sum(diff * diff, axis=0, keepdims=True)
        dh2 = diff * (1.0 / D_MODEL)
        dg_ref[...] += jnp.sum(dh2 * xh2, axis=0, keepdims=True)
        db_ref[...] += jnp.sum(dh2, axis=0, keepdims=True)
        dr_ref[...] = _ln_bwd(dh2 * g2_ref[...], xh2, rstd2)

    vec = jax.ShapeDtypeStruct((1, D_MODEL), F32)
    return pl.pallas_call(
        body, name="layer_b_fwd", grid=(t_len // tm,),
        in_specs=[_rows(tm, B_WIDTH), _rows(tm, B_WIDTH), _resident(kd.shape), _resident(vd.shape),
                  _resident(bias.shape), _resident(win.shape), pl.BlockSpec(memory_space=pltpu.SMEM),
                  _rows(tm, D_MODEL), _const(g1.shape), _const(b1.shape), _resident(w_out.shape), _const(g2.shape),
                  _const(b2.shape), _rows(tm, D_MODEL)],
        out_specs=[_rows(tm, B_WIDTH), _probs_spec(tm), _sink_probs_spec(), _rows(tm, D_MODEL)]
        + [_const((1, D_MODEL))] * 3,
        out_shape=[jax.ShapeDtypeStruct((t_len, B_WIDTH), BF16),
                   jax.ShapeDtypeStruct((t_len // CHUNK, N_KV, 2 * CHUNK, GROUP_Q), BF16),
                   jax.ShapeDtypeStruct((t_len // tm, 8, GROUP_Q), F32),
                   jax.ShapeDtypeStruct((t_len, D_MODEL), F32), vec, vec, vec],
        scratch_shapes=[pltpu.VMEM((tm, B_WIDTH), F32), pltpu.VMEM((GROUP_Q, PAIR), BF16)],
        compiler_params=_params(),
    )(q, zb, kd, vd, bias, win, sinks, xh1, g1, b1, w_out, g2, b2, tgt)


def _layer_b_bwd_attn(dr2, zb, o, q, kd, vd, probs, sink_probs, w_out):
    t_len = q.shape[0]
    tm = TM_ATTN
    n_steps = t_len // tm
    n_chunks = tm // CHUNK

    def body(dr_ref, z_ref, o_ref, q_ref, kd_ref, vd_ref, p_ref, ps_ref, wout_ref,
             dq_ref, dz_ref, dkd_ref, dvd_ref, ck_ref, cv_ref, gw_ref, dsink_ref, dbias_ref,
             do_scr, qs_scr, dos_scr, gw_acc):
        i = pl.program_id(0)

        @pl.when(i == 0)
        def _():
            gw_acc[...] = jnp.zeros_like(gw_acc)
            dsink_ref[...] = jnp.zeros_like(dsink_ref)
            dbias_ref[...] = jnp.zeros_like(dbias_ref)

        drb = dr_ref[...].astype(BF16)
        dy = _dot_nt(drb, wout_ref[...])
        z = z_ref[...].astype(F32)
        sz, dsz = _silu_parts(z)
        o_t = o_ref[...].astype(F32)
        dz_ref[...] = (dy * o_t * dsz).astype(BF16)
        do_scr[...] = (dy * sz).astype(BF16)
        gw_acc[...] += _dot_tn((o_t * sz).astype(BF16), drb)

        lo = _half_mask()
        for kvh in range(N_KV):
            kcols = slice(kvh * PAIR, (kvh + 1) * PAIR)
            dk_bands, dv_bands = [], []
            for ci in range(n_chunks):
                cg = i * n_chunks + ci
                rows = slice(ci * CHUNK, (ci + 1) * CHUNK)
                kband = _band(kd_ref, cg, kvh)
                vband = _band(vd_ref, cg, kvh)
                _stack_heads(q_ref, rows, kvh, qs_scr, lo)
                _stack_heads(do_scr, rows, kvh, dos_scr, lo)
                qs = qs_scr[...]
                dos = dos_scr[...]
                pb = p_ref[ci, kvh]
                p = pb.astype(F32)
                p_sink = ps_ref[0, ci * N_KV + kvh:ci * N_KV + kvh + 1, :]
                dp = _dot_nt(vband, dos)
                delta = jnp.sum(p * dp, axis=0, keepdims=True)
                dlog = p * (dp - delta)
                for j in range(GROUP):
                    dbias_ref[kvh * GROUP + j] += dlog[:, j * CHUNK:(j + 1) * CHUNK]
                dsink_ref[kvh:kvh + 1, :] += -(p_sink * delta)
                ds = dlog.astype(BF16)
                dq_stack = _dot_tn(ds, kband) * SCALE
                for pp in range(GROUP // 2):
                    pair = kvh * (GROUP // 2) + pp
                    dq_ref[rows, pair * PAIR:(pair + 1) * PAIR] = _unstack_pairs(dq_stack, pp, lo).astype(BF16)
                dk_bands.append(_dot(ds, qs))
                dv_bands.append(_dot(pb, dos))
            for bands, out_ref, carry_ref in ((dk_bands, dkd_ref, ck_ref), (dv_bands, dvd_ref, cv_ref)):
                carry_ref[0, :, kcols] = bands[0][0:CHUNK]
                for ci in range(n_chunks):
                    own = bands[ci][CHUNK:2 * CHUNK]
                    if ci + 1 < n_chunks:
                        own = own + bands[ci + 1][0:CHUNK]
                    out_ref[ci * CHUNK:(ci + 1) * CHUNK, kcols] = own

        @pl.when(i == n_steps - 1)
        def _():
            gw_ref[...] = gw_acc[...].astype(BF16)

    carry_spec = pl.BlockSpec((1, CHUNK, 2 * PAIR), lambda i: (i, 0, 0))
    carry_shape = jax.ShapeDtypeStruct((n_steps, CHUNK, 2 * PAIR), F32)
    bias_shape = (N_HEADS, 2 * CHUNK, CHUNK)
    return pl.pallas_call(
        body, name="layer_b_bwd_attn", grid=(n_steps,),
        in_specs=[_rows(tm, D_MODEL), _rows(tm, B_WIDTH), _rows(tm, B_WIDTH), _rows(tm, B_WIDTH),
                  _resident(kd.shape), _resident(vd.shape), _probs_spec(tm), _sink_probs_spec(),
                  _resident(w_out.shape)],
        out_specs=[_rows(tm, B_WIDTH), _rows(tm, B_WIDTH), _rows(tm, 2 * PAIR), _rows(tm, 2 * PAIR),
                   carry_spec, carry_spec, _const(w_out.shape), _const((N_KV, GROUP_Q)), _const(bias_shape)],
        out_shape=[jax.ShapeDtypeStruct((t_len, B_WIDTH), BF16), jax.ShapeDtypeStruct((t_len, B_WIDTH), BF16),
                   jax.ShapeDtypeStruct((t_len, 2 * PAIR), F32), jax.ShapeDtypeStruct((t_len, 2 * PAIR), F32),
                   carry_shape, carry_shape, jax.ShapeDtypeStruct(w_out.shape, BF16),
                   jax.ShapeDtypeStruct((N_KV, GROUP_Q), F32), jax.ShapeDtypeStruct(bias_shape, F32)],
        scratch_shapes=[pltpu.VMEM((tm, B_WIDTH), BF16), pltpu.VMEM((GROUP_Q, PAIR), BF16),
                        pltpu.VMEM((GROUP_Q, PAIR), BF16), pltpu.VMEM(w_out.shape, F32)],
        compiler_params=_params(),
    )(dr2, zb, o, q, kd, vd, probs, sink_probs, w_out)


def _layer_b_bwd_proj(xh1, rstd1, g1, b1, dr2, dq, dzb, dkd, dvd, carry_k, carry_v, w_in, w_kv):
    t_len = xh1.shape[0]
    tm = TM_MM
    n_steps = t_len // tm
    per_tile = tm // TM_ATTN
    n_carry = carry_k.shape[0]

    def body(xh_ref, rstd_ref, g_ref, b_ref, dr2_ref, dq_ref, dz_ref, dkd_ref, dvd_ref, *rest):
        carry_refs = rest[:2 * per_tile]
        win_ref, wkv_ref, dr1_ref, dg_ref, db_ref, gwin_ref, gwkv_ref, acc_in, acc_kv = rest[2 * per_tile:]
        i = pl.program_id(0)

        @pl.when(i == 0)
        def _():
            acc_in[...] = jnp.zeros_like(acc_in)
            acc_kv[...] = jnp.zeros_like(acc_kv)
            dg_ref[...] = jnp.zeros_like(dg_ref)
            db_ref[...] = jnp.zeros_like(db_ref)

        lo = lax.broadcasted_iota(jnp.int32, (tm, PAIR), 1) < HEAD_DIM

        def heads_gradient(tile_ref, refs):
            parts = []
            for a in range(per_tile):
                parts.append(tile_ref[a * TM_ATTN:(a + 1) * TM_ATTN - CHUNK, :])
                carry = refs[a][0]
                if a == per_tile - 1:
                    carry = jnp.where(i < n_steps - 1, carry, 0.0)
                parts.append(tile_ref[(a + 1) * TM_ATTN - CHUNK:(a + 1) * TM_ATTN, :] + carry)
            dup = jnp.concatenate(parts, axis=0)
            return jnp.where(lo, _fold_halves(dup[:, 0:PAIR]), _fold_halves(dup[:, PAIR:2 * PAIR]))

        xh = xh_ref[...]
        h1 = (xh * g_ref[...] + b_ref[...]).astype(BF16)
        dq_t = dq_ref[...]
        dz_t = dz_ref[...]
        dkv = jnp.concatenate([heads_gradient(dkd_ref, carry_refs[:per_tile]),
                               heads_gradient(dvd_ref, carry_refs[per_tile:])], axis=1).astype(BF16)
        dh1 = ALPHA * dr2_ref[...]
        dh1 += _dot_nt(dq_t, win_ref[:, 0:B_WIDTH])
        dh1 += _dot_nt(dz_t, win_ref[:, B_WIDTH:2 * B_WIDTH])
        dh1 += _dot_nt(dkv, wkv_ref[...])
        acc_in[:, 0:B_WIDTH] += _dot_tn(h1, dq_t)
        acc_in[:, B_WIDTH:2 * B_WIDTH] += _dot_tn(h1, dz_t)
        acc_kv[...] += _dot_tn(h1, dkv)
        dg_ref[...] += jnp.sum(dh1 * xh, axis=0, keepdims=True)
        db_ref[...] += jnp.sum(dh1, axis=0, keepdims=True)
        dr1_ref[...] = _ln_bwd(dh1 * g_ref[...], xh, rstd_ref[...])

        @pl.when(i == n_steps - 1)
        def _():
            half_rows = D_MODEL // 2
            shard_cols = 2 * B_WIDTH // N_CHIPS
            for s in range(N_CHIPS):
                for c in range(2):
                    gwin_ref[2 * s + c] = acc_in[c * half_rows:(c + 1) * half_rows,
                                                 s * shard_cols:(s + 1) * shard_cols].astype(BF16)
            gwkv_ref[...] = acc_kv[...].astype(BF16)

    vec = jax.ShapeDtypeStruct((1, D_MODEL), F32)
    gwin_shape = (N_DEV, D_MODEL // 2, 2 * B_WIDTH // N_CHIPS)

    def carry_spec(a):
        return pl.BlockSpec((1, CHUNK, 2 * PAIR), lambda i: (jnp.minimum(per_tile * i + a + 1, n_carry - 1), 0, 0))

    carry_specs = [carry_spec(a) for a in range(per_tile)]
    return pl.pallas_call(
        body, name="layer_b_bwd_proj", grid=(n_steps,),
        in_specs=[_rows(tm, D_MODEL), _rows(tm, 1), _const(g1.shape), _const(b1.shape), _rows(tm, D_MODEL),
                  _rows(tm, B_WIDTH), _rows(tm, B_WIDTH), _rows(tm, 2 * PAIR), _rows(tm, 2 * PAIR)]
        + carry_specs + carry_specs + [_resident(w_in.shape), _resident(w_kv.shape)],
        out_specs=[_rows(tm, D_MODEL), _const((1, D_MODEL)), _const((1, D_MODEL)), _const(gwin_shape),
                   _const(w_kv.shape)],
        out_shape=[jax.ShapeDtypeStruct((t_len, D_MODEL), F32), vec, vec,
                   jax.ShapeDtypeStruct(gwin_shape, BF16), jax.ShapeDtypeStruct(w_kv.shape, BF16)],
        scratch_shapes=[pltpu.VMEM(w_in.shape, F32), pltpu.VMEM(w_kv.shape, F32)],
        compiler_params=_params(),
    )(xh1, rstd1, g1, b1, dr2, dq, dzb, dkd, dvd, *([carry_k] * per_tile), *([carry_v] * per_tile), w_in, w_kv)


def _layer_a_bwd_mix(dr1, u, vh, z, y, rv, w_out, lng, lnb, ws, bsp_t, after):
    t_len = u.shape[0]
    tm = TM_ATTN
    n_steps = t_len // tm

    def body(dr_ref, u_ref, vh_ref, z_ref, y_ref, rv_ref, wout_ref, lng_ref, lnb_ref, ws_ref, bsp_ref, after_ref,
             dp_ref, gw_ref, dws_ref, dbsp_ref, dgs_ref, dbs_ref, s_scr, dvn_scr, gw_acc):
        i = pl.program_id(0)

        @pl.when(i == 0)
        def _():
            gw_acc[...] = jnp.zeros_like(gw_acc)
            dws_ref[...] = jnp.zeros_like(dws_ref)
            dbsp_ref[...] = jnp.zeros_like(dbsp_ref)
            dgs_ref[...] = jnp.zeros_like(dgs_ref)
            dbs_ref[...] = jnp.zeros_like(dbs_ref)

        drb = dr_ref[...].astype(BF16)
        dy = _dot_nt(drb, wout_ref[...])
        vh_t = vh_ref[...].astype(F32)
        vn = (vh_t * lng_ref[...] + lnb_ref[...]).astype(BF16)
        _spatial_mix(ws_ref, bsp_ref, vn, s_scr, tm // CHUNK)
        gw_acc[...] += _dot_tn(y_ref[...], drb)
        s = s_scr[...]
        sz, dsz = _silu_parts(z_ref[...].astype(F32))
        t = dy * u_ref[...].astype(F32)
        dp_ref[:, 0:A_WIDTH] = (dy * (s * sz)).astype(BF16)
        dp_ref[:, 2 * A_WIDTH:3 * A_WIDTH] = (t * s * dsz).astype(BF16)
        ds = (t * sz).astype(BF16)

        group_of = lax.broadcasted_iota(jnp.int32, (A_WIDTH, CHUNK), 0) // A_GROUP_DIM
        indicator = (group_of == lax.broadcasted_iota(jnp.int32, (A_WIDTH, CHUNK), 1)).astype(BF16)
        group_sums = _dot(ds, indicator)
        total = group_sums[0:CHUNK]
        for ci in range(1, tm // CHUNK):
            total += group_sums[ci * CHUNK:(ci + 1) * CHUNK]
        dbsp_ref[...] += total

        tri = (lax.broadcasted_iota(jnp.int32, (CHUNK, CHUNK), 0)
               >= lax.broadcasted_iota(jnp.int32, (CHUNK, CHUNK), 1))
        for g in range(A_GROUPS):
            wsg = jnp.where(tri, ws_ref[g], 0.0).astype(BF16)
            cols = slice(g * A_GROUP_DIM, (g + 1) * A_GROUP_DIM)
            dws_g = jnp.zeros((CHUNK, CHUNK), F32)
            for ci in range(tm // CHUNK):
                rows = slice(ci * CHUNK, (ci + 1) * CHUNK)
                ds_b = ds[rows, cols]
                dws_g += _dot_nt(ds_b, vn[rows, cols])
                dvn_scr[rows, cols] = _dot_tn(wsg, ds_b)
            dws_ref[g] += jnp.where(tri, dws_g, 0.0)
        dvn = dvn_scr[...]
        dgs_ref[...] += jnp.sum(dvn * vh_t, axis=0, keepdims=True)
        dbs_ref[...] += jnp.sum(dvn, axis=0, keepdims=True)
        dp_ref[:, A_WIDTH:2 * A_WIDTH] = _ln_bwd(dvn * lng_ref[...], vh_t, rv_ref[...]).astype(BF16)

        @pl.when(i == n_steps - 1)
        def _():
            gw_ref[...] = gw_acc[...].astype(BF16)

    wide = jax.ShapeDtypeStruct((1, A_WIDTH), F32)
    return pl.pallas_call(
        body, name="layer_a_bwd_mix", grid=(n_steps,),
        in_specs=[_rows(tm, D_MODEL), _rows(tm, A_WIDTH), _rows(tm, A_WIDTH), _rows(tm, A_WIDTH), _rows(tm, A_WIDTH),
                  _rows(tm, 1), _resident(w_out.shape), _const(lng.shape), _const(lnb.shape), _const(ws.shape),
                  _const(bsp_t.shape), _const(after.shape)],
        out_specs=[_rows(tm, 3 * A_WIDTH), _const(w_out.shape), _const(ws.shape), _const((CHUNK, CHUNK)),
                   _const((1, A_WIDTH)), _const((1, A_WIDTH))],
        out_shape=[jax.ShapeDtypeStruct((t_len, 3 * A_WIDTH), BF16), jax.ShapeDtypeStruct(w_out.shape, BF16),
                   jax.ShapeDtypeStruct(ws.shape, F32), jax.ShapeDtypeStruct((CHUNK, CHUNK), F32),
                   wide, wide],
        scratch_shapes=[pltpu.VMEM((tm, A_WIDTH), F32), pltpu.VMEM((tm, A_WIDTH), F32),
                        pltpu.VMEM(w_out.shape, F32)],
        compiler_params=_params(),
    )(dr1, u, vh, z, y, rv, w_out, lng, lnb, ws, bsp_t, after)


def _layer_a_bwd_dx(dr1, dp, w_in, after, updates=()):
    t_len = dr1.shape[0]
    tm = TM_MM
    n_steps = t_len // tm
    n_upd = len(updates)

    def body(dr_ref, dp_ref, win_ref, after_ref, *refs):
        upd_in, dx_ref, upd_out = refs[:4 * n_upd], refs[4 * n_upd], refs[4 * n_upd + 1:]
        dx_ref[...] = ALPHA * dr_ref[...] + _dot_nt(dp_ref[...], win_ref[...])
        for k in range(n_upd):
            w_ref, g_ref, m_ref, v_ref = upd_in[4 * k:4 * k + 4]
            g_out, d_ref, nm_ref, nv_ref = upd_out[4 * k:4 * k + 4]
            g_out[...] = g_ref[...]
            _adamw_update(w_ref, g_ref, m_ref, v_ref, d_ref, nm_ref, nv_ref)

    upd_specs, upd_shapes, upd_args = [], [], []
    for w, g, m, v in updates:
        rows, cols = w.shape
        upd_specs.append(pl.BlockSpec((rows // n_steps, cols), lambda i: (i, 0)))
        upd_shapes.append(jax.ShapeDtypeStruct((rows, cols), F32))
        upd_args += [w, g, m, v]
    return pl.pallas_call(
        body, name="layer_a_bwd_dx", grid=(n_steps,),
        in_specs=[_rows(tm, D_MODEL), _rows(tm, 3 * A_WIDTH), _resident(w_in.shape), _const(after.shape)]
        + [s for s in upd_specs for _ in range(4)],
        out_specs=[_rows(tm, D_MODEL)] + [s for s in upd_specs for _ in range(4)],
        out_shape=[jax.ShapeDtypeStruct((t_len, D_MODEL), F32)] + [s for s in upd_shapes for _ in range(4)],
        compiler_params=_params(),
    )(dr1, dp, w_in, after, *upd_args)


def _layer_a_bwd_win(xt, dp, after):
    t_len = xt.shape[1]
    tm = TM_WIN
    n_steps = t_len // tm
    shard_cols = 3 * A_WIDTH // N_CHIPS
    half_rows = D_MODEL // 2

    def body(xt_ref, dp_ref, after_ref, gw_ref, acc):
        i = pl.program_id(1)

        @pl.when(i == 0)
        def _():
            acc[...] = jnp.zeros_like(acc)

        acc[...] += _dot(xt_ref[...], dp_ref[...])

        @pl.when(i == n_steps - 1)
        def _():
            for c in range(2):
                gw_ref[0, c] = acc[c * half_rows:(c + 1) * half_rows, :].astype(BF16)

    return pl.pallas_call(
        body, name="layer_a_bwd_win", grid=(N_CHIPS, n_steps),
        in_specs=[pl.BlockSpec((D_MODEL, tm), lambda j, i: (0, i)),
                  pl.BlockSpec((tm, shard_cols), lambda j, i: (i, j)), _const(after.shape)],
        out_specs=pl.BlockSpec((1, 2, half_rows, shard_cols), lambda j, i: (j, 0, 0, 0)),
        out_shape=jax.ShapeDtypeStruct((N_CHIPS, 2, half_rows, shard_cols), BF16),
        scratch_shapes=[pltpu.VMEM((D_MODEL, shard_cols), F32)],
        compiler_params=_params(("arbitrary", "arbitrary")),
    )(xt, dp, after)


def _bucket_onehot():
    dist = jnp.arange(CHUNK, dtype=jnp.int32)[None, :]
    max_exact = REL_BUCKETS // 2
    df = jnp.maximum(dist, 1).astype(F32)
    large = max_exact + (jnp.log(df / max_exact) / math.log(CHUNK / max_exact)
                         * (REL_BUCKETS - max_exact)).astype(jnp.int32)
    bucket = jnp.where(dist < max_exact, dist, jnp.minimum(large, REL_BUCKETS - 1))
    onehot = bucket == jnp.arange(REL_BUCKETS, dtype=jnp.int32)[:, None]
    return onehot.astype(F32)


def _bias_expand(rel_t, onehot):
    def body(rel_ref, oh_ref, out_ref):
        by_distance = jnp.dot(rel_ref[...], oh_ref[...], preferred_element_type=F32,
                              precision=lax.Precision.HIGHEST)
        for h in range(N_HEADS):
            rows = jnp.broadcast_to(by_distance[h:h + 1, :], (2 * CHUNK, CHUNK))
            out_ref[h] = pltpu.roll(rows, 0, 1, stride=1, stride_axis=0)

    return pl.pallas_call(
        body, name="bias_expand",
        out_shape=jax.ShapeDtypeStruct((N_HEADS, 2 * CHUNK, CHUNK), F32),
    )(rel_t, onehot)


def _bias_reduce(oh_ref, db_ref):
    sublane = lax.broadcasted_iota(jnp.int32, (_SUBLANES, CHUNK), 0)
    rows = []
    for h in range(N_HEADS):
        part = db_ref[h, 0:_SUBLANES, :]
        for a in range(1, 2 * CHUNK // _SUBLANES):
            tile = db_ref[h, a * _SUBLANES:(a + 1) * _SUBLANES, :]
            back = (-a * _SUBLANES) % CHUNK
            part += pltpu.roll(tile, back, 1) if back else tile
        total = jnp.where(sublane == 0, part, 0.0)
        for s in range(1, _SUBLANES):
            total += jnp.where(sublane == s, pltpu.roll(part, CHUNK - s, 1), 0.0)
        rows.append(jnp.sum(total, axis=0, keepdims=True))
    by_distance = jnp.concatenate(rows, axis=0)
    return lax.dot_general(oh_ref[...], by_distance, (((1,), (1,)), ((), ())),
                           preferred_element_type=F32, precision=lax.Precision.HIGHEST)


_SMALL_SHAPES = dict(w_spatial=(A_GROUPS, CHUNK, CHUNK), b_spatial=(A_GROUPS, CHUNK), attn_sinks=(1, N_HEADS),
                     rel_bias=(REL_BUCKETS, N_HEADS), post_ln_g=(2, D_MODEL), post_ln_b=(2, D_MODEL),
                     sgu_ln_g=(1, A_WIDTH), sgu_ln_b=(1, A_WIDTH), loss=(1, 1))
_SMALL_ORDER = tuple(_SMALL_SHAPES)


def _small_rows(name):
    shape = _SMALL_SHAPES[name]
    rows = math.prod(shape[:-1]) if shape[-1] < _LANES else math.prod(shape) // _LANES
    return -(-rows // _SUBLANES) * _SUBLANES


def _small_offset(name):
    return sum(_small_rows(n) for n in _SMALL_ORDER[:_SMALL_ORDER.index(name)])


def _pack_small(dws, dbsp, dsink, dbias, onehot, post_g, post_b, dgs, dbs, loss_vec):
    def body(dws_ref, dbsp_ref, dsink_ref, db_ref, oh_ref, g1_ref, g2_ref, b1_ref, b2_ref, dgs_ref, dbs_ref,
             loss_ref, out_ref):
        out_ref[...] = jnp.zeros_like(out_ref)

        def put_flat(name, refs):
            row = _small_offset(name)
            for ref in refs:
                for k in range(ref.shape[1] // _LANES):
                    out_ref[row:row + 1, :] = ref[:, k * _LANES:(k + 1) * _LANES]
                    row += 1

        row = _small_offset("w_spatial")
        for g in range(A_GROUPS):
            out_ref[row + g * CHUNK:row + (g + 1) * CHUNK, :] = dws_ref[g]
        row = _small_offset("b_spatial")
        out_ref[row:row + A_GROUPS, :] = dbsp_ref[...].T[0:A_GROUPS, :]
        lane = lax.broadcasted_iota(jnp.int32, (1, _LANES), 1)
        sinks = jnp.zeros((1, _LANES), F32)
        for h in range(N_HEADS):
            per_query = dsink_ref[h // GROUP:h // GROUP + 1, (h % GROUP) * CHUNK:(h % GROUP + 1) * CHUNK]
            sinks = jnp.where(lane == h, jnp.sum(per_query, axis=1, keepdims=True), sinks)
        row = _small_offset("attn_sinks")
        out_ref[row:row + 1, :] = sinks
        row = _small_offset("rel_bias")
        out_ref[row:row + REL_BUCKETS, 0:N_HEADS] = _bias_reduce(oh_ref, db_ref)
        put_flat("post_ln_g", [g1_ref, g2_ref])
        put_flat("post_ln_b", [b1_ref, b2_ref])
        put_flat("sgu_ln_g", [dgs_ref])
        put_flat("sgu_ln_b", [dbs_ref])
        row = _small_offset("loss")
        out_ref[row:row + 1, 0:1] = (0.5 / D_MODEL) * jnp.sum(loss_ref[...], axis=1, keepdims=True)

    total_rows = sum(_small_rows(n) for n in _SMALL_ORDER)
    return pl.pallas_call(
        body, name="pack_small",
        out_shape=jax.ShapeDtypeStruct((total_rows, _LANES), F32),
    )(dws, dbsp, dsink, dbias, onehot, *post_g, *post_b, dgs, dbs, loss_vec)


def _place():
    return lax.axis_index("x"), lax.axis_index("y"), lax.axis_index("c")


RELAY_PIECES = 4


def _shard_window(full_ref, shard_shape, col_sharded, s, half, piece=None):
    rows, cols = shard_shape
    if half is None:
        start, size = 0, rows
    elif piece is None:
        start, size = half * (rows // 2), rows // 2
    else:
        size = rows // 2 // RELAY_PIECES
        start = (half * RELAY_PIECES + piece) * size
    if col_sharded:
        return full_ref.at[pl.ds(start, size), pl.ds(s * cols, cols)]
    return full_ref.at[pl.ds(s * rows + start, size), :]


def _other_chips(x, y):
    return [(1 - x, y), (x, 1 - y), (1 - x, 1 - y)]


def _gather_weights(shards, col_sharded, fetch, ln_shard):
    n_w = len(shards)
    fetched = [w for w in range(n_w) if fetch[w]]
    full_shapes = []
    for w, cs in zip(shards, col_sharded):
        r, c = w.shape
        full_shapes.append((r, c * N_CHIPS) if cs else (r * N_CHIPS, c))

    def body(*refs):
        in_refs = refs[:n_w]
        ln_ref = refs[n_w]
        full_refs = refs[n_w + 1:2 * n_w + 1]
        ln_full = refs[2 * n_w + 1]
        stage = refs[2 * n_w + 2:3 * n_w + 2]
        send_sems, recv_sems, local_sems, ln_send, ln_recv = refs[3 * n_w + 2:]
        x, y, c = _place()
        s_me = 2 * x + y
        chips = _other_chips(x, y)

        def shard_window(w, s, half, piece=None):
            return _shard_window(full_refs[w], shards[w].shape, col_sharded[w], s, half, piece)

        def stage_piece(w, half, piece):
            rows = shards[w].shape[0] // 2 // RELAY_PIECES
            return stage[w].at[pl.ds((half * RELAY_PIECES + piece) * rows, rows), :]

        def ici_copy(w, k, sender_shard, piece):
            idx = (w * 3 + k) * RELAY_PIECES + piece
            return pltpu.make_async_remote_copy(
                src_ref=stage_piece(w, c, piece), dst_ref=shard_window(w, sender_shard, c, piece),
                send_sem=send_sems.at[idx], recv_sem=recv_sems.at[idx],
                device_id=(*chips[k], c), device_id_type=MESH)

        def d2d_copy(w, k, half):
            s_k = 2 * chips[k][0] + chips[k][1]
            win = shard_window(w, s_k, half)
            idx = 3 * RELAY_PIECES * n_w + w * 3 + k
            return pltpu.make_async_remote_copy(
                src_ref=win, dst_ref=win, send_sem=send_sems.at[idx], recv_sem=recv_sems.at[idx],
                device_id=(x, y, 1 - c), device_id_type=MESH)

        def ln_copy(k, slot):
            return pltpu.make_async_remote_copy(
                src_ref=ln_ref, dst_ref=ln_full.at[slot], send_sem=ln_send.at[k], recv_sem=ln_recv.at[k],
                device_id=(*chips[k], c), device_id_type=MESH)

        for w in range(n_w):
            stage[w][...] = in_refs[w][...].astype(BF16)
        own = [pltpu.make_async_copy(stage[w], shard_window(w, s_me, None), local_sems.at[w]) for w in range(n_w)]
        for cp in own:
            cp.start()
        ln_full[s_me] = ln_ref[...]
        def shard_of(k):
            return 2 * chips[k][0] + chips[k][1]

        relay_from = jnp.where(c == 0, shard_of(0), shard_of(1))
        relay_to = (jnp.where(c == 0, x, 1 - x), jnp.where(c == 0, 1 - y, y), c)

        def relay_copy(w, sender_shard, piece):
            win = shard_window(w, sender_shard, c, piece)
            idx = (w * 3 + 2) * RELAY_PIECES + piece
            return pltpu.make_async_remote_copy(
                src_ref=win, dst_ref=win, send_sem=send_sems.at[idx], recv_sem=recv_sems.at[idx],
                device_id=relay_to, device_id_type=MESH)

        pieces = range(RELAY_PIECES)
        first = [ici_copy(w, k, s_me, q) for w in fetched for q in pieces for k in range(2)]
        first += [ln_copy(k, s_me) for k in range(3)]
        for cp in first:
            cp.start()
        passed = []
        for w in fetched:
            for q in pieces:
                for k in range(2):
                    ici_copy(w, k, shard_of(k), q).wait_recv()
                relay = relay_copy(w, relay_from, q)
                relay.start()
                passed.append(relay)
            for k in range(2):
                fwd = d2d_copy(w, k, c)
                fwd.start()
                passed.append(fwd)
        for w in fetched:
            for q in pieces:
                relay_copy(w, shard_of(2), q).wait_recv()
            fwd = d2d_copy(w, 2, c)
            fwd.start()
            passed.append(fwd)
        for w in fetched:
            for k in range(3):
                d2d_copy(w, k, 1 - c).wait_recv()
        for k in range(3):
            ln_copy(k, 2 * chips[k][0] + chips[k][1]).wait_recv()
        for cp in first + passed:
            cp.wait_send()
        for cp in own:
            cp.wait()

    vmem = pl.BlockSpec(memory_space=pltpu.VMEM)
    hbm = pl.BlockSpec(memory_space=pl.ANY)
    return pl.pallas_call(
        body, name="gather_weights",
        in_specs=[vmem] * (n_w + 1),
        out_specs=[hbm] * n_w + [vmem],
        out_shape=[jax.ShapeDtypeStruct(s, BF16) for s in full_shapes]
        + [jax.ShapeDtypeStruct((N_CHIPS,) + ln_shard.shape, F32)],
        scratch_shapes=[pltpu.VMEM(w.shape, BF16) for w in shards]
        + [pltpu.SemaphoreType.DMA((3 * (RELAY_PIECES + 1) * n_w,)),
           pltpu.SemaphoreType.DMA((3 * (RELAY_PIECES + 1) * n_w,)),
           pltpu.SemaphoreType.DMA((n_w,)), pltpu.SemaphoreType.DMA((3,)), pltpu.SemaphoreType.DMA((3,))],
        compiler_params=pltpu.CompilerParams(vmem_limit_bytes=VMEM_LIMIT),
    )(*shards, ln_shard)


def _fetch_copy(full_ref, shard_shape, col_sharded, sender_shard, send_sems, recv_sems, idx, chip, c):
    win = _shard_window(full_ref, shard_shape, col_sharded, sender_shard, None)
    return pltpu.make_async_remote_copy(src_ref=win, dst_ref=win, send_sem=send_sems.at[idx],
                                        recv_sem=recv_sems.at[idx], device_id=(*chip, c), device_id_type=MESH)


def _fetch_start(fulls, shard_shapes, col_sharded):
    n = len(fulls)

    def body(*refs):
        full = refs[:n]
        send_sems, recv_sems = refs[n], refs[n + 1]
        token = refs[-1]
        x, y, c = _place()
        for w in range(n):
            for k, chip in enumerate(_other_chips(x, y)):
                _fetch_copy(full[w], shard_shapes[w], col_sharded[w], 2 * x + y, send_sems, recv_sems, w * 3 + k,
                            chip, c).start()
        token[...] = jnp.zeros_like(token)

    outs = pl.pallas_call(
        body, name="fetch_start",
        out_shape=(pltpu.SemaphoreType.DMA((3 * n,)), pltpu.SemaphoreType.DMA((3 * n,)),
                   *[pltpu.HBM(f.shape, f.dtype) for f in fulls], jax.ShapeDtypeStruct((8, 128), F32)),
        in_specs=[_HBM] * n,
        out_specs=(_SEM, _SEM, *([_HBM] * n), pl.BlockSpec(memory_space=pltpu.VMEM)),
        input_output_aliases={i: 2 + i for i in range(n)},
        compiler_params=pltpu.CompilerParams(has_side_effects=pltpu.SideEffectType.DATAFLOW_SIDE_EFFECTING),
    )(*[pltpu.with_memory_space_constraint(f, pltpu.HBM) for f in fulls])
    return dict(send=outs[0], recv=outs[1], full=list(outs[2:2 + n])), outs[-1]


def _fetch_wait(group, shard_shapes, col_sharded, after):
    n = len(group["full"])

    def body(*refs):
        full = refs[:n]
        send_sems, recv_sems = refs[n], refs[n + 1]
        x, y, c = _place()
        for w in range(n):
            for k, chip in enumerate(_other_chips(x, y)):
                _fetch_copy(full[w], shard_shapes[w], col_sharded[w], 2 * x + y, send_sems, recv_sems, w * 3 + k,
                            chip, c).wait_send()
                _fetch_copy(full[w], shard_shapes[w], col_sharded[w], 2 * chip[0] + chip[1], send_sems, recv_sems,
                            w * 3 + k, chip, c).wait_recv()

    outs = pl.pallas_call(
        body, name="fetch_wait", out_shape=tuple(pltpu.HBM(f.shape, f.dtype) for f in group["full"]),
        in_specs=[_HBM] * n + [_SEM, _SEM, pl.BlockSpec(memory_space=pl.ANY)],
        out_specs=tuple([_HBM] * n), input_output_aliases={i: i for i in range(n)},
        compiler_params=pltpu.CompilerParams(has_side_effects=pltpu.SideEffectType.DATAFLOW_SIDE_EFFECTING),
    )(*group["full"], group["send"], group["recv"], after)
    return list(outs)


_HBM = pl.BlockSpec(memory_space=pltpu.HBM)
_SEM = pl.BlockSpec(memory_space=pltpu.SEMAPHORE)
_N_PEER = N_DEV - 1


def _peer(x, y, c, k):
    return (x + (k >> 2)) % 2, (y + ((k >> 1) & 1)) % 2, (c + (k & 1)) % 2


def _exchange_copy(src_ref, land_ref, sliced, send_sems, recv_sems, idx, x, y, c, k):
    px, py, pc = _peer(x, y, c, k)
    src = src_ref.at[4 * px + 2 * py + pc] if sliced else src_ref
    return pltpu.make_async_remote_copy(
        src_ref=src, dst_ref=land_ref.at[4 * x + 2 * y + c],
        send_sem=send_sems.at[idx], recv_sem=recv_sems.at[idx], device_id=(px, py, pc), device_id_type=MESH)


def _exchange_start(tag, arrays, sliced):
    n = len(arrays)
    lands = [lax.empty(a.shape if s else (N_DEV,) + a.shape, a.dtype) for a, s in zip(arrays, sliced)]

    def body(*refs):
        src, land = refs[:n], refs[n:2 * n]
        send_sems, recv_sems = refs[2 * n], refs[2 * n + 1]
        token = refs[-1]
        x, y, c = _place()
        for w in range(n):
            for k in range(1, N_DEV):
                _exchange_copy(src[w], land[w], sliced[w], send_sems, recv_sems, w * _N_PEER + k - 1, x, y, c, k).start()
        token[...] = jnp.zeros_like(token)

    outs = pl.pallas_call(
        body, name="exchange_start_" + tag,
        out_shape=(pltpu.SemaphoreType.DMA((n * _N_PEER,)), pltpu.SemaphoreType.DMA((n * _N_PEER,)),
                   *[pltpu.HBM(a.shape, a.dtype) for a in arrays], *[pltpu.HBM(l.shape, l.dtype) for l in lands],
                   jax.ShapeDtypeStruct((8, 128), F32)),
        in_specs=[_HBM] * (2 * n),
        out_specs=(_SEM, _SEM, *([_HBM] * (2 * n)), pl.BlockSpec(memory_space=pltpu.VMEM)),
        input_output_aliases={i: 2 + i for i in range(2 * n)},
        compiler_params=pltpu.CompilerParams(has_side_effects=pltpu.SideEffectType.DATAFLOW_SIDE_EFFECTING),
    )(*[pltpu.with_memory_space_constraint(a, pltpu.HBM) for a in arrays],
      *[pltpu.with_memory_space_constraint(l, pltpu.HBM) for l in lands])
    return dict(send=outs[0], recv=outs[1], src=list(outs[2:2 + n]), land=list(outs[2 + n:2 + 2 * n]),
                sliced=list(sliced)), outs[-1]


def _exchange_wait(tag, groups, after):
    counts = [len(g["src"]) for g in groups]
    total = sum(counts)

    def body(*refs):
        pos = 0
        x, y, c = _place()
        for g, n in zip(groups, counts):
            src, land = refs[pos:pos + n], refs[pos + n:pos + 2 * n]
            send_sems, recv_sems = refs[pos + 2 * n], refs[pos + 2 * n + 1]
            pos += 2 * n + 2
            for w in range(n):
                for k in range(1, N_DEV):
                    cp = _exchange_copy(src[w], land[w], g["sliced"][w], send_sems, recv_sems,
                                        w * _N_PEER + k - 1, x, y, c, k)
                    cp.wait_send()
                    cp.wait_recv()

    operands, in_specs, aliases, out_shape = [], [], {}, []
    for g in groups:
        for a in g["src"] + g["land"]:
            aliases[len(operands)] = len(out_shape)
            out_shape.append(pltpu.HBM(a.shape, a.dtype))
            operands.append(a)
            in_specs.append(_HBM)
        operands += [g["send"], g["recv"]]
        in_specs += [_SEM, _SEM]
    operands.append(after)
    in_specs.append(pl.BlockSpec(memory_space=pl.ANY))
    outs = pl.pallas_call(
        body, name="exchange_wait_" + tag, out_shape=tuple(out_shape), in_specs=in_specs,
        out_specs=tuple([_HBM] * (2 * total)), input_output_aliases=aliases,
        compiler_params=pltpu.CompilerParams(has_side_effects=pltpu.SideEffectType.DATAFLOW_SIDE_EFFECTING),
    )(*operands)
    srcs, lands, pos = [], [], 0
    for n in counts:
        srcs += list(outs[pos:pos + n])
        lands += list(outs[pos + n:pos + 2 * n])
        pos += 2 * n
    return srcs, lands


def _sum_and_swap(tag, pieces, lands, small=None, small_land=None):
    n_w = len(pieces)
    n_small = 0 if small is None else 1

    def body(*refs):
        g_refs, land_refs = refs[:n_w], refs[n_w:2 * n_w]
        pos = 2 * n_w + 2 * n_small
        out_refs = refs[pos:pos + n_w]
        pos += n_w + n_small
        bufs = refs[pos:pos + n_w]
        load_sems, swap_send, swap_recv = refs[pos + n_w + 2 * n_small:]
        x, y, c = _place()
        me = 4 * x + 2 * y + c

        def slot(k):
            px, py, pc = _peer(x, y, c, k)
            return 4 * px + 2 * py + pc

        def swap_copy(w, half):
            rows = pieces[w].shape[1]
            win = out_refs[w].at[pl.ds(pl.multiple_of(half * rows, rows), rows), :]
            return pltpu.make_async_remote_copy(
                src_ref=win, dst_ref=win, send_sem=swap_send.at[w], recv_sem=swap_recv.at[w],
                device_id=(x, y, 1 - c), device_id_type=MESH)

        loads = []
        for w in range(n_w):
            per_w = [pltpu.make_async_copy(g_refs[w].at[me], bufs[w].at[me], load_sems.at[w * N_DEV])]
            per_w += [pltpu.make_async_copy(land_refs[w].at[slot(k)], bufs[w].at[slot(k)], load_sems.at[w * N_DEV + k])
                      for k in range(1, N_DEV)]
            loads.append(per_w)
        small_loads = []
        if n_small:
            small_ref, small_land_ref = refs[2 * n_w], refs[2 * n_w + 1]
            small_out = refs[2 * n_w + 2 + n_w]
            small_buf, small_sems = refs[pos + n_w], refs[pos + n_w + 1]
            small_loads = [pltpu.make_async_copy(small_land_ref.at[slot(k)], small_buf.at[slot(k)],
                                                 small_sems.at[k - 1]) for k in range(1, N_DEV)]
        for cp in [cp for per_w in loads for cp in per_w] + small_loads:
            cp.start()
        if n_small:
            small_buf[me] = small_ref[...]
        swaps = []
        for w in range(n_w):
            for cp in loads[w]:
                cp.wait()
            rows = pieces[w].shape[1]
            total = bufs[w][0].astype(F32)
            for p in range(1, N_DEV):
                total += bufs[w][p].astype(F32)
            out_refs[w][pl.ds(pl.multiple_of(c * rows, rows), rows), :] = total
            sw = swap_copy(w, c)
            sw.start()
            swaps.append(sw)
        if n_small:
            for cp in small_loads:
                cp.wait()
            total = small_buf[0]
            for p in range(1, N_DEV):
                total += small_buf[p]
            small_out[...] = total
        for w in range(n_w):
            swap_copy(w, 1 - c).wait_recv()
        for sw in swaps:
            sw.wait_send()

    vmem = pl.BlockSpec(memory_space=pltpu.VMEM)
    hbm = pl.BlockSpec(memory_space=pl.ANY)
    small_args = [small, small_land] if n_small else []
    small_shapes = [jax.ShapeDtypeStruct(small.shape, F32)] if n_small else []
    small_scratch = ([pltpu.VMEM((N_DEV,) + small.shape, F32), pltpu.SemaphoreType.DMA((_N_PEER,))]
                     if n_small else [])
    return pl.pallas_call(
        body, name="sum_and_swap_" + tag,
        in_specs=[hbm] * (2 * n_w) + [vmem, hbm] * n_small,
        out_specs=[vmem] * (n_w + n_small),
        out_shape=[jax.ShapeDtypeStruct((2 * p.shape[1], p.shape[2]), F32) for p in pieces] + small_shapes,
        scratch_shapes=[pltpu.VMEM(p.shape, BF16) for p in pieces] + small_scratch
        + [pltpu.SemaphoreType.DMA((n_w * N_DEV,)), pltpu.SemaphoreType.DMA((n_w,)),
           pltpu.SemaphoreType.DMA((n_w,))],
        compiler_params=pltpu.CompilerParams(vmem_limit_bytes=VMEM_LIMIT),
    )(*pieces, *lands, *small_args)


def _adamw_values(w, g_t, m, v):
    c1 = 1.0 - ADAM_B1 ** ADAM_STEP
    c2 = 1.0 - ADAM_B2 ** ADAM_STEP
    nm = ADAM_B1 * m + (1.0 - ADAM_B1) * g_t
    nv = ADAM_B2 * v + (1.0 - ADAM_B2) * (g_t * g_t)
    return -ADAM_LR * ((nm / c1) / (jnp.sqrt(nv / c2) + ADAM_EPS) + ADAM_WD * w), nm, nv


def _adamw_update(w_ref, g_ref, m_ref, v_ref, d_ref, nm_ref, nv_ref):
    d_ref[...], nm_ref[...], nv_ref[...] = _adamw_values(w_ref[...], g_ref[...], m_ref[...], v_ref[...])


def _adamw_small(packed, shard_index, names, weights, moments_m, moments_v):
    n = len(names)
    shapes = [weights[name].shape for name in names]
    flat = [a[name].reshape(-1, a[name].shape[-1]) for name in names for a in (weights, moments_m, moments_v)]

    def body(packed_ref, shard_ref, *refs):
        for k, name in enumerate(names):
            w_ref, m_ref, v_ref = refs[3 * k:3 * k + 3]
            g_ref, d_ref, nm_ref, nv_ref = refs[3 * n + 4 * k:3 * n + 4 * k + 4]
            rows, cols = w_ref.shape
            first = _small_offset(name)
            if cols <= _LANES:
                blocks = [(slice(0, rows), packed_ref[first:first + rows, 0:cols])]
            else:
                per_row = cols // _LANES
                if cols < _SMALL_SHAPES[name][-1]:
                    first = first + shard_ref[0] * per_row
                blocks = [(slice(i, i + 1),
                           jnp.concatenate([packed_ref[pl.ds(first + i * per_row + j, 1), :] for j in range(per_row)],
                                           axis=1)) for i in range(rows)]
            for at, g_t in blocks:
                g_ref[at, :] = g_t
                d_ref[at, :], nm_ref[at, :], nv_ref[at, :] = _adamw_values(w_ref[at, :], g_t, m_ref[at, :],
                                                                           v_ref[at, :])

    vmem = pl.BlockSpec(memory_space=pltpu.VMEM)
    outs = pl.pallas_call(
        body, name="adamw_small",
        in_specs=[vmem, pl.BlockSpec(memory_space=pltpu.SMEM)] + [vmem] * (3 * n),
        out_shape=[jax.ShapeDtypeStruct(flat[3 * k].shape, F32) for k in range(n) for _ in range(4)],
    )(packed, shard_index.reshape(1).astype(jnp.int32), *flat)
    return [tuple(o.reshape(shapes[k]) for o in outs[4 * k:4 * k + 4]) for k in range(n)]


def _adamw(label, w, g, m, v):
    shape = w.shape
    cols = shape[-1]
    rows = w.size // cols
    args = [a.reshape(rows, cols) for a in (w, g, m, v)]

    def body(w_ref, g_ref, m_ref, v_ref, g_out, d_ref, nm_ref, nv_ref):
        g_out[...] = g_ref[...]
        _adamw_update(w_ref, g_ref, m_ref, v_ref, d_ref, nm_ref, nv_ref)

    block_rows = 256 if rows % 256 == 0 and rows > 256 else rows
    spec = pl.BlockSpec((block_rows, cols), lambda i: (i, 0))
    outs = pl.pallas_call(
        body, name="adamw_" + label, grid=(rows // block_rows,),
        in_specs=[spec] * 4, out_specs=[spec] * 4,
        out_shape=[jax.ShapeDtypeStruct((rows, cols), F32)] * 4,
        compiler_params=_params(),
    )(*args)
    return [o.reshape(shape) for o in outs]


def _no_send(tag, arrays, sliced):
    return jnp.zeros((8, 128), F32)


def _local_step(x, tgt, w_in_a, later_weights, first_after, sgu_ln_g, sgu_ln_b, w_spatial, b_spatial,
                attn_sinks, rel_bias, post_ln_g, post_ln_b, send=_no_send):
    bsp_t = b_spatial.T
    g1, b1 = post_ln_g[0:1], post_ln_b[0:1]
    g2, b2 = post_ln_g[1:2], post_ln_b[1:2]
    onehot = _bucket_onehot()
    bias = _bias_expand(rel_bias.T, onehot)
    win = _window_tables()

    xt, u, vh, z, rv, y = _layer_a_fwd(x, w_in_a, sgu_ln_g, sgu_ln_b, w_spatial, bsp_t, first_after)
    w_out_a, w_kv, w_in_b, w_out_b = later_weights(y)
    xh1, rstd1, q, zb, kd, vd = _layer_b_proj(x, y, w_out_a, g1, b1, w_in_b, w_kv)
    o, probs, sink_probs, dr2, loss_vec, dg2, db2 = _layer_b_fwd(q, zb, kd, vd, bias, win, attn_sinks, xh1, g1, b1,
                                                                 w_out_b, g2, b2, tgt)
    dq, dzb, dkd, dvd, carry_k, carry_v, gw_out_b, dsink, dbias = _layer_b_bwd_attn(
        dr2, zb, o, q, kd, vd, probs, sink_probs, w_out_b)
    dr1, dg1, db1, gw_in_b, gw_kv = _layer_b_bwd_proj(xh1, rstd1, g1, b1, dr2, dq, dzb, dkd, dvd, carry_k, carry_v,
                                                      w_in_b, w_kv)
    gw_out_b = gw_out_b.reshape(N_DEV, -1, D_MODEL)
    gw_kv = gw_kv.reshape(N_DEV, -1, 2 * PAIR)
    after = send("b", [gw_out_b, gw_in_b, gw_kv], [True, True, True])
    dp, gw_out_a, dws, dbsp, dgs, dbs = _layer_a_bwd_mix(dr1, u, vh, z, y, rv, w_out_a, sgu_ln_g, sgu_ln_b,
                                                         w_spatial, bsp_t, after)
    gw_out_a = gw_out_a.reshape(N_DEV, -1, D_MODEL)
    small = _pack_small(dws, dbsp, dsink, dbias, onehot, (dg1, dg2), (db1, db2), dgs, dbs, loss_vec)
    after = send("a_out", [gw_out_a, small], [True, False])
    gw_in_a = _layer_a_bwd_win(xt, dp, after).reshape(N_DEV, D_MODEL // 2, -1)
    after = send("a_in", [gw_in_a], [True])
    after, updates = after if isinstance(after, tuple) else (after, ())
    grad_x, *updated = _layer_a_bwd_dx(dr1, dp, w_in_a, after, updates)

    pieces = [gw_in_a, gw_out_a, gw_kv, gw_in_b, gw_out_b]
    return grad_x, pieces, small, updated


def kernel(x, w_in_a, sgu_ln_g, sgu_ln_b, w_spatial, b_spatial, w_out_a, w_kv, w_in_b, attn_sinks, rel_bias, w_out_b, post_ln_g, post_ln_b, loss_target, m_w_in_a, m_sgu_ln_g, m_sgu_ln_b, m_w_spatial, m_b_spatial, m_w_out_a, m_w_kv, m_w_in_b, m_attn_sinks, m_rel_bias, m_w_out_b, m_post_ln_g, m_post_ln_b, v_w_in_a, v_sgu_ln_g, v_sgu_ln_b, v_w_spatial, v_b_spatial, v_w_out_a, v_w_kv, v_w_in_b, v_attn_sinks, v_rel_bias, v_w_out_b, v_post_ln_g, v_post_ln_b):
    weights = dict(w_in_a=w_in_a, sgu_ln_g=sgu_ln_g, sgu_ln_b=sgu_ln_b, w_spatial=w_spatial, b_spatial=b_spatial,
                   w_out_a=w_out_a, w_kv=w_kv, w_in_b=w_in_b, attn_sinks=attn_sinks, rel_bias=rel_bias,
                   w_out_b=w_out_b, post_ln_g=post_ln_g, post_ln_b=post_ln_b)
    moments_m = dict(w_in_a=m_w_in_a, sgu_ln_g=m_sgu_ln_g, sgu_ln_b=m_sgu_ln_b, w_spatial=m_w_spatial,
                     b_spatial=m_b_spatial, w_out_a=m_w_out_a, w_kv=m_w_kv, w_in_b=m_w_in_b,
                     attn_sinks=m_attn_sinks, rel_bias=m_rel_bias, w_out_b=m_w_out_b, post_ln_g=m_post_ln_g,
                     post_ln_b=m_post_ln_b)
    moments_v = dict(w_in_a=v_w_in_a, sgu_ln_g=v_sgu_ln_g, sgu_ln_b=v_sgu_ln_b, w_spatial=v_w_spatial,
                     b_spatial=v_b_spatial, w_out_a=v_w_out_a, w_kv=v_w_kv, w_in_b=v_w_in_b,
                     attn_sinks=v_attn_sinks, rel_bias=v_rel_bias, w_out_b=v_w_out_b, post_ln_g=v_post_ln_g,
                     post_ln_b=v_post_ln_b)
    order = ("w_in_a", "sgu_ln_g", "sgu_ln_b", "w_spatial", "b_spatial", "w_out_a", "w_kv", "w_in_b", "attn_sinks",
             "rel_bias", "w_out_b", "post_ln_g", "post_ln_b")

    shard_index = 2 * lax.axis_index("x") + lax.axis_index("y")
    ln_shard = jnp.concatenate([sgu_ln_g, sgu_ln_b], axis=0)
    shards = [w_in_a[0], w_out_a[0], w_kv, w_in_b[0], w_out_b[0]]
    col_sharded = [True, False, False, True, False]
    full_in_a, *later, ln_full = _gather_weights(shards, col_sharded, [True, False, False, False, False], ln_shard)
    ln_full = jnp.transpose(ln_full, (1, 0, 2)).reshape(2, A_WIDTH)
    later_shapes = [s.shape for s in shards[1:]]
    fetch_group, fetch_token = _fetch_start(later, later_shapes, col_sharded[1:])

    def later_weights(y):
        return _fetch_wait(fetch_group, later_shapes, col_sharded[1:], y)

    groups, grads, deltas, new_m, new_v, scalars = {}, {}, {}, {}, {}, {}
    early = ("w_out_b", "w_in_b", "w_kv", "w_out_a")

    def two_dim(a):
        return a.reshape(-1, a.shape[-1])

    def send(tag, arrays, sliced):
        groups[tag], token = _exchange_start(tag, arrays, sliced)
        if tag != "a_in":
            return token
        srcs, lands = _exchange_wait("early", [groups["b"], groups["a_out"]], token)
        *reduced, packed_sum = _sum_and_swap("early", srcs[:4], lands[:4], srcs[4], lands[4])
        updates = [(two_dim(weights[n]), g, two_dim(moments_m[n]), two_dim(moments_v[n]))
                   for n, g in zip(early, reduced)]
        scalars["loss"] = packed_sum[_small_offset("loss"), 0]
        small_names = ("sgu_ln_g", "sgu_ln_b", "w_spatial", "b_spatial", "attn_sinks", "rel_bias", "post_ln_g",
                       "post_ln_b")
        small_updates = _adamw_small(packed_sum, shard_index, small_names, weights, moments_m, moments_v)
        for name, (g, d, nm, nv) in zip(small_names, small_updates):
            grads[name], deltas[name], new_m[name], new_v[name] = g, d, nm, nv
        return new_m["b_spatial"].reshape(A_GROUPS, CHUNK), updates

    grad_x, _, _, updated = _local_step(
        x[0], loss_target[0], full_in_a, later_weights, fetch_token, ln_full[0:1], ln_full[1:2], w_spatial[0],
        b_spatial[0], attn_sinks, rel_bias, post_ln_g, post_ln_b, send=send)
    for k, name in enumerate(early):
        grads[name], deltas[name], new_m[name], new_v[name] = [
            a.reshape(weights[name].shape) for a in updated[4 * k:4 * k + 4]]

    srcs, lands = _exchange_wait("late", [groups["a_in"]], grad_x)
    (g_in_a,) = _sum_and_swap("late", srcs, lands)
    grads["w_in_a"], deltas["w_in_a"], new_m["w_in_a"], new_v["w_in_a"] = _adamw(
        "w_in_a", w_in_a, g_in_a.reshape(w_in_a.shape), m_w_in_a, v_w_in_a)
    return (scalars["loss"], grad_x[None], *[grads[n] for n in order], *[deltas[n] for n in order],
            *[new_m[n] for n in order], *[new_v[n] for n in order])
```

```python
import math

import jax
import jax.numpy as jnp
from jax import lax
from jax.experimental import pallas as pl
from jax.experimental.pallas import tpu as pltpu

F32 = jnp.float32
BF16 = jnp.bfloat16

D_MODEL = 1024
A_WIDTH = 2048
A_GROUPS = 8
A_GROUP_DIM = 256
CHUNK = 128
N_HEADS = 16
N_KV = 2
HEAD_DIM = 64
PAIR = 2 * HEAD_DIM
B_WIDTH = 1024
REL_BUCKETS = 32
ALPHA = 4.0 ** 0.25
LN_EPS = 1e-5
NEG_INF = -1e30
SCALE = HEAD_DIM ** -0.5

ADAM_LR = 0.001
ADAM_B1 = 0.9
ADAM_B2 = 0.999
ADAM_EPS = 1e-08
ADAM_WD = 0.01
ADAM_STEP = 10

N_DEV = 8
N_CHIPS = 4
MESH = pl.DeviceIdType.MESH
VMEM_LIMIT = 56 * 1024 * 1024

TM_ATTN = 256
TM_MM = 512
TM_WIN = 1024
_LANES = 128
_SUBLANES = 8


def _dot(a, b):
    return jnp.dot(a, b, preferred_element_type=F32)


def _dot_nt(a, b):
    return lax.dot_general(a, b, (((1,), (1,)), ((), ())), preferred_element_type=F32)


def _dot_tn(a, b):
    return lax.dot_general(a, b, (((0,), (0,)), ((), ())), preferred_element_type=F32)


def _ln_fwd(r):
    mu = jnp.mean(r, axis=-1, keepdims=True)
    rc = r - mu
    var = jnp.mean(rc * rc, axis=-1, keepdims=True)
    rstd = lax.rsqrt(var + LN_EPS)
    return rc * rstd, rstd


def _ln_bwd(dxh, xh, rstd):
    m1 = jnp.mean(dxh, axis=-1, keepdims=True)
    m2 = jnp.mean(dxh * xh, axis=-1, keepdims=True)
    return rstd * (dxh - m1 - xh * m2)


def _silu_parts(z):
    sg = jax.nn.sigmoid(z)
    return z * sg, sg * (1.0 + z * (1.0 - sg))


def _dup_halves(blk):
    sw = pltpu.roll(blk, HEAD_DIM, 1)
    lo = lax.broadcasted_iota(jnp.int32, blk.shape, 1) < HEAD_DIM
    return jnp.where(lo, blk, sw), jnp.where(lo, sw, blk)


def _fold_halves(blk):
    return blk + pltpu.roll(blk, HEAD_DIM, 1)


def _resident(shape):
    nd = len(shape)
    return pl.BlockSpec(shape, lambda *_: (0,) * nd, pipeline_mode=pl.Buffered(1))


def _const(shape):
    nd = len(shape)
    return pl.BlockSpec(shape, lambda *_: (0,) * nd)


def _rows(tm, cols):
    return pl.BlockSpec((tm, cols), lambda i: (i, 0))


def _params(sem=("arbitrary",)):
    return pltpu.CompilerParams(dimension_semantics=sem, vmem_limit_bytes=VMEM_LIMIT)


def _spatial_mix(ws_ref, bsp_ref, vn, s_scr, n_chunks):
    tri = (lax.broadcasted_iota(jnp.int32, (CHUNK, CHUNK), 0)
           >= lax.broadcasted_iota(jnp.int32, (CHUNK, CHUNK), 1))
    for g in range(A_GROUPS):
        wsg = jnp.where(tri, ws_ref[g], 0.0).astype(BF16)
        cols = slice(g * A_GROUP_DIM, (g + 1) * A_GROUP_DIM)
        for ci in range(n_chunks):
            rows = slice(ci * CHUNK, (ci + 1) * CHUNK)
            s_scr[rows, cols] = _dot(wsg, vn[rows, cols]) + bsp_ref[:, g:g + 1]


def _layer_a_fwd(x, w_in, lng, lnb, ws, bsp_t, after):
    t_len = x.shape[0]
    tm = TM_ATTN

    def body(x_ref, win_ref, lng_ref, lnb_ref, ws_ref, bsp_ref, after_ref,
             xt_ref, u_ref, vh_ref, z_ref, rv_ref, y_ref, s_scr):
        x_t = x_ref[...]
        xb = x_t.astype(BF16)
        xt_ref[...] = x_t.T.astype(BF16)
        u = _dot(xb, win_ref[:, 0:A_WIDTH])
        v = _dot(xb, win_ref[:, A_WIDTH:2 * A_WIDTH])
        z = _dot(xb, win_ref[:, 2 * A_WIDTH:3 * A_WIDTH])
        vh, rv = _ln_fwd(v)
        vn = (vh * lng_ref[...] + lnb_ref[...]).astype(BF16)
        _spatial_mix(ws_ref, bsp_ref, vn, s_scr, tm // CHUNK)
        sz, _ = _silu_parts(z)
        y_ref[...] = (u * s_scr[...] * sz).astype(BF16)
        u_ref[...] = u.astype(BF16)
        vh_ref[...] = vh.astype(BF16)
        z_ref[...] = z.astype(BF16)
        rv_ref[...] = rv

    wide = jax.ShapeDtypeStruct((t_len, A_WIDTH), BF16)
    return pl.pallas_call(
        body, name="layer_a_fwd", grid=(t_len // tm,),
        in_specs=[_rows(tm, D_MODEL), _resident(w_in.shape), _const(lng.shape), _const(lnb.shape), _const(ws.shape),
                  _const(bsp_t.shape), _const(after.shape)],
        out_specs=[pl.BlockSpec((D_MODEL, tm), lambda i: (0, i)), _rows(tm, A_WIDTH), _rows(tm, A_WIDTH),
                   _rows(tm, A_WIDTH), _rows(tm, 1), _rows(tm, A_WIDTH)],
        out_shape=[jax.ShapeDtypeStruct((D_MODEL, t_len), BF16), wide, wide, wide,
                   jax.ShapeDtypeStruct((t_len, 1), F32), wide],
        scratch_shapes=[pltpu.VMEM((tm, A_WIDTH), F32)],
        compiler_params=_params(),
    )(x, w_in, lng, lnb, ws, bsp_t, after)


def _layer_b_proj(x, y, w_out_a, g1, b1, w_in, w_kv):
    t_len = x.shape[0]
    tm = TM_MM

    def body(x_ref, y_ref, wout_ref, g_ref, b_ref, win_ref, wkv_ref, xh_ref, r1_ref, q_ref, z_ref, kd_ref, vd_ref):
        xh, r1 = _ln_fwd(ALPHA * x_ref[...] + _dot(y_ref[...], wout_ref[...]))
        xh_ref[...] = xh
        r1_ref[...] = r1
        h1 = (xh * g_ref[...] + b_ref[...]).astype(BF16)
        q_ref[...] = (_dot(h1, win_ref[:, 0:B_WIDTH]) * SCALE).astype(BF16)
        z_ref[...] = _dot(h1, win_ref[:, B_WIDTH:2 * B_WIDTH]).astype(BF16)
        kv = _dot(h1, wkv_ref[...])
        k0, k1 = _dup_halves(kv[:, 0:PAIR])
        v0, v1 = _dup_halves(kv[:, PAIR:2 * PAIR])
        kd_ref[:, 0:PAIR] = k0.astype(BF16)
        kd_ref[:, PAIR:2 * PAIR] = k1.astype(BF16)
        vd_ref[:, 0:PAIR] = v0.astype(BF16)
        vd_ref[:, PAIR:2 * PAIR] = v1.astype(BF16)

    return pl.pallas_call(
        body, name="layer_b_proj", grid=(t_len // tm,),
        in_specs=[_rows(tm, D_MODEL), _rows(tm, A_WIDTH), _resident(w_out_a.shape), _const(g1.shape),
                  _const(b1.shape), _resident(w_in.shape), _resident(w_kv.shape)],
        out_specs=[_rows(tm, D_MODEL), _rows(tm, 1), _rows(tm, B_WIDTH), _rows(tm, B_WIDTH), _rows(tm, 2 * PAIR),
                   _rows(tm, 2 * PAIR)],
        out_shape=[jax.ShapeDtypeStruct((t_len, D_MODEL), F32), jax.ShapeDtypeStruct((t_len, 1), F32),
                   jax.ShapeDtypeStruct((t_len, B_WIDTH), BF16), jax.ShapeDtypeStruct((t_len, B_WIDTH), BF16),
                   jax.ShapeDtypeStruct((t_len, 2 * PAIR), BF16), jax.ShapeDtypeStruct((t_len, 2 * PAIR), BF16)],
        compiler_params=_params(),
    )(x, y, w_out_a, g1, b1, w_in, w_kv)


GROUP = N_HEADS // N_KV
GROUP_Q = GROUP * CHUNK


def _window_tables():
    j = jnp.arange(2 * CHUNK, dtype=jnp.int32)[:, None]
    t = jnp.arange(CHUNK, dtype=jnp.int32)[None, :]
    dist = t + CHUNK - j
    inside = (dist >= 0) & (dist < CHUNK)
    return jnp.stack([inside & (j >= CHUNK), inside]).astype(F32)


def _band(ref, chunk_index, kvh):
    prev0 = pl.multiple_of(jnp.maximum(chunk_index - 1, 0) * CHUNK, CHUNK)
    cur0 = pl.multiple_of(chunk_index * CHUNK, CHUNK)
    cols = slice(kvh * PAIR, (kvh + 1) * PAIR)
    return jnp.concatenate([ref[pl.ds(prev0, CHUNK), cols], ref[pl.ds(cur0, CHUNK), cols]], axis=0)


def _group_tables(bias_ref, win_ref, sink_ref, chunk_index, kvh):
    bias = jnp.concatenate([bias_ref[kvh * GROUP + j] for j in range(GROUP)], axis=1)
    win = win_ref[jnp.minimum(chunk_index, 1)]
    mask = jnp.concatenate([win] * GROUP, axis=1) > 0.5
    sink = jnp.concatenate([jnp.full((1, CHUNK), sink_ref[0, kvh * GROUP + j], F32) for j in range(GROUP)], axis=1)
    return bias, mask, sink


def _attn_probs(qs, kband, bias, mask, sink):
    logits = jnp.where(mask, _dot_nt(kband, qs) + bias, NEG_INF)
    m = jnp.maximum(jnp.max(logits, axis=0, keepdims=True), sink)
    e = jnp.exp(logits - m)
    es = jnp.exp(sink - m)
    inv = 1.0 / (jnp.sum(e, axis=0, keepdims=True) + es)
    return e * inv, es * inv


def _half_mask():
    return lax.broadcasted_iota(jnp.int32, (CHUNK, PAIR), 1) < HEAD_DIM


def _stack_heads(src_ref, rows, kvh, dst_scr, lo):
    for j in range(GROUP):
        h = kvh * GROUP + j
        blk = src_ref[rows, (h // 2) * PAIR:(h // 2 + 1) * PAIR].astype(F32)
        keep = lo if h % 2 == 0 else ~lo
        dst_scr[j * CHUNK:(j + 1) * CHUNK, :] = jnp.where(keep, blk, 0.0).astype(BF16)


def _probs_spec(tm):
    return pl.BlockSpec((tm // CHUNK, N_KV, 2 * CHUNK, GROUP_Q), lambda i: (i, 0, 0, 0))


def _sink_probs_spec():
    return pl.BlockSpec((1, 8, GROUP_Q), lambda i: (i, 0, 0))


def _unstack_pairs(stacked, pp, lo):
    return jnp.where(lo, stacked[(2 * pp) * CHUNK:(2 * pp + 1) * CHUNK], stacked[(2 * pp + 1) * CHUNK:(2 * pp + 2) * CHUNK])


def _layer_b_fwd(q, zb, kd, vd, bias, win, sinks, xh1, g1, b1, w_out, g2, b2, tgt):
    t_len = q.shape[0]
    tm = TM_ATTN

    def body(q_ref, z_ref, kd_ref, vd_ref, bias_ref, win_ref, sink_ref, xh_ref, g1_ref, b1_ref, wout_ref, g2_ref,
             b2_ref, tgt_ref, o_ref, p_ref, ps_ref, dr_ref, loss_ref, dg_ref, db_ref, o_scr, qs_scr):
        i = pl.program_id(0)

        @pl.when(i == 0)
        def _():
            loss_ref[...] = jnp.zeros_like(loss_ref)
            dg_ref[...] = jnp.zeros_like(dg_ref)
            db_ref[...] = jnp.zeros_like(db_ref)

        lo = _half_mask()
        ps_ref[...] = jnp.zeros_like(ps_ref)
        for ci in range(tm // CHUNK):
            cg = i * (tm // CHUNK) + ci
            rows = slice(ci * CHUNK, (ci + 1) * CHUNK)
            for kvh in range(N_KV):
                kband = _band(kd_ref, cg, kvh)
                vband = _band(vd_ref, cg, kvh)
                bias_g, mask, sink = _group_tables(bias_ref, win_ref, sink_ref, cg, kvh)
                _stack_heads(q_ref, rows, kvh, qs_scr, lo)
                p, p_sink = _attn_probs(qs_scr[...], kband, bias_g, mask, sink)
                p = p.astype(BF16)
                p_ref[ci, kvh] = p
                ps_ref[0, ci * N_KV + kvh:ci * N_KV + kvh + 1, :] = p_sink
                o_stack = _dot_tn(p, vband)
                for pp in range(GROUP // 2):
                    pair = kvh * (GROUP // 2) + pp
                    o_scr[rows, pair * PAIR:(pair + 1) * PAIR] = _unstack_pairs(o_stack, pp, lo)
        o = o_scr[...]
        o_ref[...] = o.astype(BF16)
        sz, _ = _silu_parts(z_ref[...].astype(F32))
        y = (o * sz).astype(BF16)
        h1 = xh_ref[...] * g1_ref[...] + b1_ref[...]
        r = ALPHA * h1 + _dot(y, wout_ref[...])
        xh2, rstd2 = _ln_fwd(r)
        diff = xh2 * g2_ref[...] + b2_ref[...] - tgt_ref[...]
        loss_ref[...] += jnp.sum(diff * diff, axis=0, keepdims=True)
        dh2 = diff * (1.0 / D_MODEL)
        dg_ref[...] += jnp.sum(dh2 * xh2, axis=0, keepdims=True)
        db_ref[...] += jnp.sum(dh2, axis=0, keepdims=True)
        dr_ref[...] = _ln_bwd(dh2 * g2_ref[...], xh2, rstd2)

    vec = jax.ShapeDtypeStruct((1, D_MODEL), F32)
    return pl.pallas_call(
        body, name="layer_b_fwd", grid=(t_len // tm,),
        in_specs=[_rows(tm, B_WIDTH), _rows(tm, B_WIDTH), _resident(kd.shape), _resident(vd.shape),
                  _resident(bias.shape), _resident(win.shape), pl.BlockSpec(memory_space=pltpu.SMEM),
                  _rows(tm, D_MODEL), _const(g1.shape), _const(b1.shape), _resident(w_out.shape), _const(g2.shape),
                  _const(b2.shape), _rows(tm, D_MODEL)],
        out_specs=[_rows(tm, B_WIDTH), _probs_spec(tm), _sink_probs_spec(), _rows(tm, D_MODEL)]
        + [_const((1, D_MODEL))] * 3,
        out_shape=[jax.ShapeDtypeStruct((t_len, B_WIDTH), BF16),
                   jax.ShapeDtypeStruct((t_len // CHUNK, N_KV, 2 * CHUNK, GROUP_Q), BF16),
                   jax.ShapeDtypeStruct((t_len // tm, 8, GROUP_Q), F32),
                   jax.ShapeDtypeStruct((t_len, D_MODEL), F32), vec, vec, vec],
        scratch_shapes=[pltpu.VMEM((tm, B_WIDTH), F32), pltpu.VMEM((GROUP_Q, PAIR), BF16)],
        compiler_params=_params(),
    )(q, zb, kd, vd, bias, win, sinks, xh1, g1, b1, w_out, g2, b2, tgt)


def _layer_b_bwd_attn(dr2, zb, o, q, kd, vd, probs, sink_probs, w_out):
    t_len = q.shape[0]
    tm = TM_ATTN
    n_steps = t_len // tm
    n_chunks = tm // CHUNK

    def body(dr_ref, z_ref, o_ref, q_ref, kd_ref, vd_ref, p_ref, ps_ref, wout_ref,
             dq_ref, dz_ref, dkd_ref, dvd_ref, ck_ref, cv_ref, gw_ref, dsink_ref, dbias_ref,
             do_scr, qs_scr, dos_scr, gw_acc):
        i = pl.program_id(0)

        @pl.when(i == 0)
        def _():
            gw_acc[...] = jnp.zeros_like(gw_acc)
            dsink_ref[...] = jnp.zeros_like(dsink_ref)
            dbias_ref[...] = jnp.zeros_like(dbias_ref)

        drb = dr_ref[...].astype(BF16)
        dy = _dot_nt(drb, wout_ref[...])
        z = z_ref[...].astype(F32)
        sz, dsz = _silu_parts(z)
        o_t = o_ref[...].astype(F32)
        dz_ref[...] = (dy * o_t * dsz).astype(BF16)
        do_scr[...] = (dy * sz).astype(BF16)
        gw_acc[...] += _dot_tn((o_t * sz).astype(BF16), drb)

        lo = _half_mask()
        for kvh in range(N_KV):
            kcols = slice(kvh * PAIR, (kvh + 1) * PAIR)
            dk_bands, dv_bands = [], []
            for ci in range(n_chunks):
                cg = i * n_chunks + ci
                rows = slice(ci * CHUNK, (ci + 1) * CHUNK)
                kband = _band(kd_ref, cg, kvh)
                vband = _band(vd_ref, cg, kvh)
                _stack_heads(q_ref, rows, kvh, qs_scr, lo)
                _stack_heads(do_scr, rows, kvh, dos_scr, lo)
                qs = qs_scr[...]
                dos = dos_scr[...]
                pb = p_ref[ci, kvh]
                p = pb.astype(F32)
                p_sink = ps_ref[0, ci * N_KV + kvh:ci * N_KV + kvh + 1, :]
                dp = _dot_nt(vband, dos)
                delta = jnp.sum(p * dp, axis=0, keepdims=True)
                dlog = p * (dp - delta)
                for j in range(GROUP):
                    dbias_ref[kvh * GROUP + j] += dlog[:, j * CHUNK:(j + 1) * CHUNK]
                dsink_ref[kvh:kvh + 1, :] += -(p_sink * delta)
                ds = dlog.astype(BF16)
                dq_stack = _dot_tn(ds, kband) * SCALE
                for pp in range(GROUP // 2):
                    pair = kvh * (GROUP // 2) + pp
                    dq_ref[rows, pair * PAIR:(pair + 1) * PAIR] = _unstack_pairs(dq_stack, pp, lo).astype(BF16)
                dk_bands.append(_dot(ds, qs))
                dv_bands.append(_dot(pb, dos))
            for bands, out_ref, carry_ref in ((dk_bands, dkd_ref, ck_ref), (dv_bands, dvd_ref, cv_ref)):
                carry_ref[0, :, kcols] = bands[0][0:CHUNK]
                for ci in range(n_chunks):
                    own = bands[ci][CHUNK:2 * CHUNK]
                    if ci + 1 < n_chunks:
                        own = own + bands[ci + 1][0:CHUNK]
                    out_ref[ci * CHUNK:(ci + 1) * CHUNK, kcols] = own

        @pl.when(i == n_steps - 1)
        def _():
            gw_ref[...] = gw_acc[...].astype(BF16)

    carry_spec = pl.BlockSpec((1, CHUNK, 2 * PAIR), lambda i: (i, 0, 0))
    carry_shape = jax.ShapeDtypeStruct((n_steps, CHUNK, 2 * PAIR), F32)
    bias_shape = (N_HEADS, 2 * CHUNK, CHUNK)
    return pl.pallas_call(
        body, name="layer_b_bwd_attn", grid=(n_steps,),
        in_specs=[_rows(tm, D_MODEL), _rows(tm, B_WIDTH), _rows(tm, B_WIDTH), _rows(tm, B_WIDTH),
                  _resident(kd.shape), _resident(vd.shape), _probs_spec(tm), _sink_probs_spec(),
                  _resident(w_out.shape)],
        out_specs=[_rows(tm, B_WIDTH), _rows(tm, B_WIDTH), _rows(tm, 2 * PAIR), _rows(tm, 2 * PAIR),
                   carry_spec, carry_spec, _const(w_out.shape), _const((N_KV, GROUP_Q)), _const(bias_shape)],
        out_shape=[jax.ShapeDtypeStruct((t_len, B_WIDTH), BF16), jax.ShapeDtypeStruct((t_len, B_WIDTH), BF16),
                   jax.ShapeDtypeStruct((t_len, 2 * PAIR), F32), jax.ShapeDtypeStruct((t_len, 2 * PAIR), F32),
                   carry_shape, carry_shape, jax.ShapeDtypeStruct(w_out.shape, BF16),
                   jax.ShapeDtypeStruct((N_KV, GROUP_Q), F32), jax.ShapeDtypeStruct(bias_shape, F32)],
        scratch_shapes=[pltpu.VMEM((tm, B_WIDTH), BF16), pltpu.VMEM((GROUP_Q, PAIR), BF16),
                        pltpu.VMEM((GROUP_Q, PAIR), BF16), pltpu.VMEM(w_out.shape, F32)],
        compiler_params=_params(),
    )(dr2, zb, o, q, kd, vd, probs, sink_probs, w_out)


def _layer_b_bwd_proj(xh1, rstd1, g1, b1, dr2, dq, dzb, dkd, dvd, carry_k, carry_v, w_in, w_kv):
    t_len = xh1.shape[0]
    tm = TM_MM
    n_steps = t_len // tm
    per_tile = tm // TM_ATTN
    n_carry = carry_k.shape[0]

    def body(xh_ref, rstd_ref, g_ref, b_ref, dr2_ref, dq_ref, dz_ref, dkd_ref, dvd_ref, *rest):
        carry_refs = rest[:2 * per_tile]
        win_ref, wkv_ref, dr1_ref, dg_ref, db_ref, gwin_ref, gwkv_ref, acc_in, acc_kv = rest[2 * per_tile:]
        i = pl.program_id(0)

        @pl.when(i == 0)
        def _():
            acc_in[...] = jnp.zeros_like(acc_in)
            acc_kv[...] = jnp.zeros_like(acc_kv)
            dg_ref[...] = jnp.zeros_like(dg_ref)
            db_ref[...] = jnp.zeros_like(db_ref)

        lo = lax.broadcasted_iota(jnp.int32, (tm, PAIR), 1) < HEAD_DIM

        def heads_gradient(tile_ref, refs):
            parts = []
            for a in range(per_tile):
                parts.append(tile_ref[a * TM_ATTN:(a + 1) * TM_ATTN - CHUNK, :])
                carry = refs[a][0]
                if a == per_tile - 1:
                    carry = jnp.where(i < n_steps - 1, carry, 0.0)
                parts.append(tile_ref[(a + 1) * TM_ATTN - CHUNK:(a + 1) * TM_ATTN, :] + carry)
            dup = jnp.concatenate(parts, axis=0)
            return jnp.where(lo, _fold_halves(dup[:, 0:PAIR]), _fold_halves(dup[:, PAIR:2 * PAIR]))

        xh = xh_ref[...]
        h1 = (xh * g_ref[...] + b_ref[...]).astype(BF16)
        dq_t = dq_ref[...]
        dz_t = dz_ref[...]
        dkv = jnp.concatenate([heads_gradient(dkd_ref, carry_refs[:per_tile]),
                               heads_gradient(dvd_ref, carry_refs[per_tile:])], axis=1).astype(BF16)
        dh1 = ALPHA * dr2_ref[...]
        dh1 += _dot_nt(dq_t, win_ref[:, 0:B_WIDTH])
        dh1 += _dot_nt(dz_t, win_ref[:, B_WIDTH:2 * B_WIDTH])
        dh1 += _dot_nt(dkv, wkv_ref[...])
        acc_in[:, 0:B_WIDTH] += _dot_tn(h1, dq_t)
        acc_in[:, B_WIDTH:2 * B_WIDTH] += _dot_tn(h1, dz_t)
        acc_kv[...] += _dot_tn(h1, dkv)
        dg_ref[...] += jnp.sum(dh1 * xh, axis=0, keepdims=True)
        db_ref[...] += jnp.sum(dh1, axis=0, keepdims=True)
        dr1_ref[...] = _ln_bwd(dh1 * g_ref[...], xh, rstd_ref[...])

        @pl.when(i == n_steps - 1)
        def _():
            half_rows = D_MODEL // 2
            shard_cols = 2 * B_WIDTH // N_CHIPS
            for s in range(N_CHIPS):
                for c in range(2):
                    gwin_ref[2 * s + c] = acc_in[c * half_rows:(c + 1) * half_rows,
                                                 s * shard_cols:(s + 1) * shard_cols].astype(BF16)
            gwkv_ref[...] = acc_kv[...].astype(BF16)

    vec = jax.ShapeDtypeStruct((1, D_MODEL), F32)
    gwin_shape = (N_DEV, D_MODEL // 2, 2 * B_WIDTH // N_CHIPS)

    def carry_spec(a):
        return pl.BlockSpec((1, CHUNK, 2 * PAIR), lambda i: (jnp.minimum(per_tile * i + a + 1, n_carry - 1), 0, 0))

    carry_specs = [carry_spec(a) for a in range(per_tile)]
    return pl.pallas_call(
        body, name="layer_b_bwd_proj", grid=(n_steps,),
        in_specs=[_rows(tm, D_MODEL), _rows(tm, 1), _const(g1.shape), _const(b1.shape), _rows(tm, D_MODEL),
                  _rows(tm, B_WIDTH), _rows(tm, B_WIDTH), _rows(tm, 2 * PAIR), _rows(tm, 2 * PAIR)]
        + carry_specs + carry_specs + [_resident(w_in.shape), _resident(w_kv.shape)],
        out_specs=[_rows(tm, D_MODEL), _const((1, D_MODEL)), _const((1, D_MODEL)), _const(gwin_shape),
                   _const(w_kv.shape)],
        out_shape=[jax.ShapeDtypeStruct((t_len, D_MODEL), F32), vec, vec,
                   jax.ShapeDtypeStruct(gwin_shape, BF16), jax.ShapeDtypeStruct(w_kv.shape, BF16)],
        scratch_shapes=[pltpu.VMEM(w_in.shape, F32), pltpu.VMEM(w_kv.shape, F32)],
        compiler_params=_params(),
    )(xh1, rstd1, g1, b1, dr2, dq, dzb, dkd, dvd, *([carry_k] * per_tile), *([carry_v] * per_tile), w_in, w_kv)


def _layer_a_bwd_mix(dr1, u, vh, z, y, rv, w_out, lng, lnb, ws, bsp_t, after):
    t_len = u.shape[0]
    tm = TM_ATTN
    n_steps = t_len // tm

    def body(dr_ref, u_ref, vh_ref, z_ref, y_ref, rv_ref, wout_ref, lng_ref, lnb_ref, ws_ref, bsp_ref, after_ref,
             dp_ref, gw_ref, dws_ref, dbsp_ref, dgs_ref, dbs_ref, s_scr, dvn_scr, gw_acc):
        i = pl.program_id(0)

        @pl.when(i == 0)
        def _():
            gw_acc[...] = jnp.zeros_like(gw_acc)
            dws_ref[...] = jnp.zeros_like(dws_ref)
            dbsp_ref[...] = jnp.zeros_like(dbsp_ref)
            dgs_ref[...] = jnp.zeros_like(dgs_ref)
            dbs_ref[...] = jnp.zeros_like(dbs_ref)

        drb = dr_ref[...].astype(BF16)
        dy = _dot_nt(drb, wout_ref[...])
        vh_t = vh_ref[...].astype(F32)
        vn = (vh_t * lng_ref[...] + lnb_ref[...]).astype(BF16)
        _spatial_mix(ws_ref, bsp_ref, vn, s_scr, tm // CHUNK)
        gw_acc[...] += _dot_tn(y_ref[...], drb)
        s = s_scr[...]
        sz, dsz = _silu_parts(z_ref[...].astype(F32))
        t = dy * u_ref[...].astype(F32)
        dp_ref[:, 0:A_WIDTH] = (dy * (s * sz)).astype(BF16)
        dp_ref[:, 2 * A_WIDTH:3 * A_WIDTH] = (t * s * dsz).astype(BF16)
        ds = (t * sz).astype(BF16)

        group_of = lax.broadcasted_iota(jnp.int32, (A_WIDTH, CHUNK), 0) // A_GROUP_DIM
        indicator = (group_of == lax.broadcasted_iota(jnp.int32, (A_WIDTH, CHUNK), 1)).astype(BF16)
        group_sums = _dot(ds, indicator)
        total = group_sums[0:CHUNK]
        for ci in range(1, tm // CHUNK):
            total += group_sums[ci * CHUNK:(ci + 1) * CHUNK]
        dbsp_ref[...] += total

        tri = (lax.broadcasted_iota(jnp.int32, (CHUNK, CHUNK), 0)
               >= lax.broadcasted_iota(jnp.int32, (CHUNK, CHUNK), 1))
        for g in range(A_GROUPS):
            wsg = jnp.where(tri, ws_ref[g], 0.0).astype(BF16)
            cols = slice(g * A_GROUP_DIM, (g + 1) * A_GROUP_DIM)
            dws_g = jnp.zeros((CHUNK, CHUNK), F32)
            for ci in range(tm // CHUNK):
                rows = slice(ci * CHUNK, (ci + 1) * CHUNK)
                ds_b = ds[rows, cols]
                dws_g += _dot_nt(ds_b, vn[rows, cols])
                dvn_scr[rows, cols] = _dot_tn(wsg, ds_b)
            dws_ref[g] += jnp.where(tri, dws_g, 0.0)
        dvn = dvn_scr[...]
        dgs_ref[...] += jnp.sum(dvn * vh_t, axis=0, keepdims=True)
        dbs_ref[...] += jnp.sum(dvn, axis=0, keepdims=True)
        dp_ref[:, A_WIDTH:2 * A_WIDTH] = _ln_bwd(dvn * lng_ref[...], vh_t, rv_ref[...]).astype(BF16)

        @pl.when(i == n_steps - 1)
        def _():
            gw_ref[...] = gw_acc[...].astype(BF16)

    wide = jax.ShapeDtypeStruct((1, A_WIDTH), F32)
    return pl.pallas_call(
        body, name="layer_a_bwd_mix", grid=(n_steps,),
        in_specs=[_rows(tm, D_MODEL), _rows(tm, A_WIDTH), _rows(tm, A_WIDTH), _rows(tm, A_WIDTH), _rows(tm, A_WIDTH),
                  _rows(tm, 1), _resident(w_out.shape), _const(lng.shape), _const(lnb.shape), _const(ws.shape),
                  _const(bsp_t.shape), _const(after.shape)],
        out_specs=[_rows(tm, 3 * A_WIDTH), _const(w_out.shape), _const(ws.shape), _const((CHUNK, CHUNK)),
                   _const((1, A_WIDTH)), _const((1, A_WIDTH))],
        out_shape=[jax.ShapeDtypeStruct((t_len, 3 * A_WIDTH), BF16), jax.ShapeDtypeStruct(w_out.shape, BF16),
                   jax.ShapeDtypeStruct(ws.shape, F32), jax.ShapeDtypeStruct((CHUNK, CHUNK), F32),
                   wide, wide],
        scratch_shapes=[pltpu.VMEM((tm, A_WIDTH), F32), pltpu.VMEM((tm, A_WIDTH), F32),
                        pltpu.VMEM(w_out.shape, F32)],
        compiler_params=_params(),
    )(dr1, u, vh, z, y, rv, w_out, lng, lnb, ws, bsp_t, after)


def _layer_a_bwd_dx(dr1, dp, w_in, after, updates=()):
    t_len = dr1.shape[0]
    tm = TM_MM
    n_steps = t_len // tm
    n_upd = len(updates)

    def body(dr_ref, dp_ref, win_ref, after_ref, *refs):
        upd_in, dx_ref, upd_out = refs[:4 * n_upd], refs[4 * n_upd], refs[4 * n_upd + 1:]
        dx_ref[...] = ALPHA * dr_ref[...] + _dot_nt(dp_ref[...], win_ref[...])
        for k in range(n_upd):
            w_ref, g_ref, m_ref, v_ref = upd_in[4 * k:4 * k + 4]
            g_out, d_ref, nm_ref, nv_ref = upd_out[4 * k:4 * k + 4]
            g_out[...] = g_ref[...]
            _adamw_update(w_ref, g_ref, m_ref, v_ref, d_ref, nm_ref, nv_ref)

    upd_specs, upd_shapes, upd_args = [], [], []
    for w, g, m, v in updates:
        rows, cols = w.shape
        upd_specs.append(pl.BlockSpec((rows // n_steps, cols), lambda i: (i, 0)))
        upd_shapes.append(jax.ShapeDtypeStruct((rows, cols), F32))
        upd_args += [w, g, m, v]
    return pl.pallas_call(
        body, name="layer_a_bwd_dx", grid=(n_steps,),
        in_specs=[_rows(tm, D_MODEL), _rows(tm, 3 * A_WIDTH), _resident(w_in.shape), _const(after.shape)]
        + [s for s in upd_specs for _ in range(4)],
        out_specs=[_rows(tm, D_MODEL)] + [s for s in upd_specs for _ in range(4)],
        out_shape=[jax.ShapeDtypeStruct((t_len, D_MODEL), F32)] + [s for s in upd_shapes for _ in range(4)],
        compiler_params=_params(),
    )(dr1, dp, w_in, after, *upd_args)


def _layer_a_bwd_win(xt, dp, after):
    t_len = xt.shape[1]
    tm = TM_WIN
    n_steps = t_len // tm
    shard_cols = 3 * A_WIDTH // N_CHIPS
    half_rows = D_MODEL // 2

    def body(xt_ref, dp_ref, after_ref, gw_ref, acc):
        i = pl.program_id(1)

        @pl.when(i == 0)
        def _():
            acc[...] = jnp.zeros_like(acc)

        acc[...] += _dot(xt_ref[...], dp_ref[...])

        @pl.when(i == n_steps - 1)
        def _():
            for c in range(2):
                gw_ref[0, c] = acc[c * half_rows:(c + 1) * half_rows, :].astype(BF16)

    return pl.pallas_call(
        body, name="layer_a_bwd_win", grid=(N_CHIPS, n_steps),
        in_specs=[pl.BlockSpec((D_MODEL, tm), lambda j, i: (0, i)),
                  pl.BlockSpec((tm, shard_cols), lambda j, i: (i, j)), _const(after.shape)],
        out_specs=pl.BlockSpec((1, 2, half_rows, shard_cols), lambda j, i: (j, 0, 0, 0)),
        out_shape=jax.ShapeDtypeStruct((N_CHIPS, 2, half_rows, shard_cols), BF16),
        scratch_shapes=[pltpu.VMEM((D_MODEL, shard_cols), F32)],
        compiler_params=_params(("arbitrary", "arbitrary")),
    )(xt, dp, after)


def _bucket_onehot():
    dist = jnp.arange(CHUNK, dtype=jnp.int32)[None, :]
    max_exact = REL_BUCKETS // 2
    df = jnp.maximum(dist, 1).astype(F32)
    large = max_exact + (jnp.log(df / max_exact) / math.log(CHUNK / max_exact)
                         * (REL_BUCKETS - max_exact)).astype(jnp.int32)
    bucket = jnp.where(dist < max_exact, dist, jnp.minimum(large, REL_BUCKETS - 1))
    onehot = bucket == jnp.arange(REL_BUCKETS, dtype=jnp.int32)[:, None]
    return onehot.astype(F32)


def _bias_expand(rel_t, onehot):
    def body(rel_ref, oh_ref, out_ref):
        by_distance = jnp.dot(rel_ref[...], oh_ref[...], preferred_element_type=F32,
                              precision=lax.Precision.HIGHEST)
        for h in range(N_HEADS):
            rows = jnp.broadcast_to(by_distance[h:h + 1, :], (2 * CHUNK, CHUNK))
            out_ref[h] = pltpu.roll(rows, 0, 1, stride=1, stride_axis=0)

    return pl.pallas_call(
        body, name="bias_expand",
        out_shape=jax.ShapeDtypeStruct((N_HEADS, 2 * CHUNK, CHUNK), F32),
    )(rel_t, onehot)


def _bias_reduce(oh_ref, db_ref):
    sublane = lax.broadcasted_iota(jnp.int32, (_SUBLANES, CHUNK), 0)
    rows = []
    for h in range(N_HEADS):
        part = db_ref[h, 0:_SUBLANES, :]
        for a in range(1, 2 * CHUNK // _SUBLANES):
            tile = db_ref[h, a * _SUBLANES:(a + 1) * _SUBLANES, :]
            back = (-a * _SUBLANES) % CHUNK
            part += pltpu.roll(tile, back, 1) if back else tile
        total = jnp.where(sublane == 0, part, 0.0)
        for s in range(1, _SUBLANES):
            total += jnp.where(sublane == s, pltpu.roll(part, CHUNK - s, 1), 0.0)
        rows.append(jnp.sum(total, axis=0, keepdims=True))
    by_distance = jnp.concatenate(rows, axis=0)
    return lax.dot_general(oh_ref[...], by_distance, (((1,), (1,)), ((), ())),
                           preferred_element_type=F32, precision=lax.Precision.HIGHEST)


_SMALL_SHAPES = dict(w_spatial=(A_GROUPS, CHUNK, CHUNK), b_spatial=(A_GROUPS, CHUNK), attn_sinks=(1, N_HEADS),
                     rel_bias=(REL_BUCKETS, N_HEADS), post_ln_g=(2, D_MODEL), post_ln_b=(2, D_MODEL),
                     sgu_ln_g=(1, A_WIDTH), sgu_ln_b=(1, A_WIDTH), loss=(1, 1))
_SMALL_ORDER = tuple(_SMALL_SHAPES)


def _small_rows(name):
    shape = _SMALL_SHAPES[name]
    rows = math.prod(shape[:-1]) if shape[-1] < _LANES else math.prod(shape) // _LANES
    return -(-rows // _SUBLANES) * _SUBLANES


def _small_offset(name):
    return sum(_small_rows(n) for n in _SMALL_ORDER[:_SMALL_ORDER.index(name)])


def _pack_small(dws, dbsp, dsink, dbias, onehot, post_g, post_b, dgs, dbs, loss_vec):
    def body(dws_ref, dbsp_ref, dsink_ref, db_ref, oh_ref, g1_ref, g2_ref, b1_ref, b2_ref, dgs_ref, dbs_ref,
             loss_ref, out_ref):
        out_ref[...] = jnp.zeros_like(out_ref)

        def put_flat(name, refs):
            row = _small_offset(name)
            for ref in refs:
                for k in range(ref.shape[1] // _LANES):
                    out_ref[row:row + 1, :] = ref[:, k * _LANES:(k + 1) * _LANES]
                    row += 1

        row = _small_offset("w_spatial")
        for g in range(A_GROUPS):
            out_ref[row + g * CHUNK:row + (g + 1) * CHUNK, :] = dws_ref[g]
        row = _small_offset("b_spatial")
        out_ref[row:row + A_GROUPS, :] = dbsp_ref[...].T[0:A_GROUPS, :]
        lane = lax.broadcasted_iota(jnp.int32, (1, _LANES), 1)
        sinks = jnp.zeros((1, _LANES), F32)
        for h in range(N_HEADS):
            per_query = dsink_ref[h // GROUP:h // GROUP + 1, (h % GROUP) * CHUNK:(h % GROUP + 1) * CHUNK]
            sinks = jnp.where(lane == h, jnp.sum(per_query, axis=1, keepdims=True), sinks)
        row = _small_offset("attn_sinks")
        out_ref[row:row + 1, :] = sinks
        row = _small_offset("rel_bias")
        out_ref[row:row + REL_BUCKETS, 0:N_HEADS] = _bias_reduce(oh_ref, db_ref)
        put_flat("post_ln_g", [g1_ref, g2_ref])
        put_flat("post_ln_b", [b1_ref, b2_ref])
        put_flat("sgu_ln_g", [dgs_ref])
        put_flat("sgu_ln_b", [dbs_ref])
        row = _small_offset("loss")
        out_ref[row:row + 1, 0:1] = (0.5 / D_MODEL) * jnp.sum(loss_ref[...], axis=1, keepdims=True)

    total_rows = sum(_small_rows(n) for n in _SMALL_ORDER)
    return pl.pallas_call(
        body, name="pack_small",
        out_shape=jax.ShapeDtypeStruct((total_rows, _LANES), F32),
    )(dws, dbsp, dsink, dbias, onehot, *post_g, *post_b, dgs, dbs, loss_vec)


def _place():
    return lax.axis_index("x"), lax.axis_index("y"), lax.axis_index("c")


RELAY_PIECES = 4


def _shard_window(full_ref, shard_shape, col_sharded, s, half, piece=None):
    rows, cols = shard_shape
    if half is None:
        start, size = 0, rows
    elif piece is None:
        start, size = half * (rows // 2), rows // 2
    else:
        size = rows // 2 // RELAY_PIECES
        start = (half * RELAY_PIECES + piece) * size
    if col_sharded:
        return full_ref.at[pl.ds(start, size), pl.ds(s * cols, cols)]
    return full_ref.at[pl.ds(s * rows + start, size), :]


def _other_chips(x, y):
    return [(1 - x, y), (x, 1 - y), (1 - x, 1 - y)]


def _gather_weights(shards, col_sharded, fetch, ln_shard):
    n_w = len(shards)
    fetched = [w for w in range(n_w) if fetch[w]]
    full_shapes = []
    for w, cs in zip(shards, col_sharded):
        r, c = w.shape
        full_shapes.append((r, c * N_CHIPS) if cs else (r * N_CHIPS, c))

    def body(*refs):
        in_refs = refs[:n_w]
        ln_ref = refs[n_w]
        full_refs = refs[n_w + 1:2 * n_w + 1]
        ln_full = refs[2 * n_w + 1]
        raw = refs[2 * n_w + 2:3 * n_w + 2]
        stage = refs[3 * n_w + 2:4 * n_w + 2]
        send_sems, recv_sems, load_sems, local_sems, ln_send, ln_recv = refs[4 * n_w + 2:]
        x, y, c = _place()
        s_me = 2 * x + y
        chips = _other_chips(x, y)
        pieces = range(RELAY_PIECES)

        def shard_window(w, s, half, piece=None):
            return _shard_window(full_refs[w], shards[w].shape, col_sharded[w], s, half, piece)

        def piece_rows(w, half, piece):
            rows = shards[w].shape[0] // 2 // RELAY_PIECES
            return pl.ds(pl.multiple_of((half * RELAY_PIECES + piece) * rows, rows), rows)

        def ici_copy(w, k, sender_shard, piece):
            idx = (w * 3 + k) * RELAY_PIECES + piece
            return pltpu.make_async_remote_copy(
                src_ref=stage[w].at[piece_rows(w, c, piece), :], dst_ref=shard_window(w, sender_shard, c, piece),
                send_sem=send_sems.at[idx], recv_sem=recv_sems.at[idx],
                device_id=(*chips[k], c), device_id_type=MESH)

        def d2d_copy(w, k, half, piece):
            s_k = 2 * chips[k][0] + chips[k][1]
            win = shard_window(w, s_k, half, piece)
            idx = (3 * n_w + w * 3 + k) * RELAY_PIECES + piece
            return pltpu.make_async_remote_copy(
                src_ref=win, dst_ref=win, send_sem=send_sems.at[idx], recv_sem=recv_sems.at[idx],
                device_id=(x, y, 1 - c), device_id_type=MESH)

        def ln_copy(k, slot):
            return pltpu.make_async_remote_copy(
                src_ref=ln_ref, dst_ref=ln_full.at[slot], send_sem=ln_send.at[k], recv_sem=ln_recv.at[k],
                device_id=(*chips[k], c), device_id_type=MESH)

        loads = []

        def load(w, rows):
            window = (rows, slice(None)) if rows is not None else (slice(None), slice(None))
            cp = pltpu.make_async_copy(in_refs[w].at[window], raw[w].at[window], load_sems.at[len(loads)])
            cp.start()
            loads.append((cp, w, window))

        for half in (c, 1 - c):
            for w in fetched:
                for q in pieces:
                    load(w, piece_rows(w, half, q))
        for w in range(n_w):
            if not fetch[w]:
                load(w, None)

        def to_bf16(k):
            cp, w, window = loads[k]
            cp.wait()
            stage[w][window] = raw[w][window].astype(BF16)

        ln_full[s_me] = ln_ref[...]
        def shard_of(k):
            return 2 * chips[k][0] + chips[k][1]

        relay_from = jnp.where(c == 0, shard_of(0), shard_of(1))
        relay_to = (jnp.where(c == 0, x, 1 - x), jnp.where(c == 0, 1 - y, y), c)

        def relay_copy(w, sender_shard, piece):
            win = shard_window(w, sender_shard, c, piece)
            idx = (w * 3 + 2) * RELAY_PIECES + piece
            return pltpu.make_async_remote_copy(
                src_ref=win, dst_ref=win, send_sem=send_sems.at[idx], recv_sem=recv_sems.at[idx],
                device_id=relay_to, device_id_type=MESH)

        first = [ln_copy(k, s_me) for k in range(3)]
        for cp in first:
            cp.start()
        n_sent = 0
        for w in fetched:
            for q in pieces:
                to_bf16(n_sent)
                n_sent += 1
                for k in range(2):
                    cp = ici_copy(w, k, s_me, q)
                    cp.start()
                    first.append(cp)
        for k in range(n_sent, len(loads)):
            to_bf16(k)
        own = [pltpu.make_async_copy(stage[w], shard_window(w, s_me, None), local_sems.at[w]) for w in range(n_w)]
        for cp in own:
            cp.start()
        passed = []
        for w in fetched:
            for q in pieces:
                for k in range(2):
                    ici_copy(w, k, shard_of(k), q).wait_recv()
                relay = relay_copy(w, relay_from, q)
                relay.start()
                passed.append(relay)
                for k in range(2):
                    fwd = d2d_copy(w, k, c, q)
                    fwd.start()
                    passed.append(fwd)
        for w in fetched:
            for q in pieces:
                relay_copy(w, shard_of(2), q).wait_recv()
                fwd = d2d_copy(w, 2, c, q)
                fwd.start()
                passed.append(fwd)
        for w in fetched:
            for k in range(3):
                for q in pieces:
                    d2d_copy(w, k, 1 - c, q).wait_recv()
        for k in range(3):
            ln_copy(k, 2 * chips[k][0] + chips[k][1]).wait_recv()
        for cp in first + passed:
            cp.wait_send()
        for cp in own:
            cp.wait()

    vmem = pl.BlockSpec(memory_space=pltpu.VMEM)
    hbm = pl.BlockSpec(memory_space=pl.ANY)
    return pl.pallas_call(
        body, name="gather_weights",
        in_specs=[hbm] * n_w + [vmem],
        out_specs=[hbm] * n_w + [vmem],
        out_shape=[jax.ShapeDtypeStruct(s, BF16) for s in full_shapes]
        + [jax.ShapeDtypeStruct((N_CHIPS,) + ln_shard.shape, F32)],
        scratch_shapes=[pltpu.VMEM(w.shape, F32) for w in shards] + [pltpu.VMEM(w.shape, BF16) for w in shards]
        + [pltpu.SemaphoreType.DMA((6 * RELAY_PIECES * n_w,)), pltpu.SemaphoreType.DMA((6 * RELAY_PIECES * n_w,)),
           pltpu.SemaphoreType.DMA((2 * RELAY_PIECES * len(fetched) + n_w - len(fetched),)),
           pltpu.SemaphoreType.DMA((n_w,)), pltpu.SemaphoreType.DMA((3,)), pltpu.SemaphoreType.DMA((3,))],
        compiler_params=pltpu.CompilerParams(vmem_limit_bytes=VMEM_LIMIT),
    )(*shards, ln_shard)


def _fetch_copy(full_ref, shard_shape, col_sharded, sender_shard, send_sems, recv_sems, idx, chip, c):
    win = _shard_window(full_ref, shard_shape, col_sharded, sender_shard, None)
    return pltpu.make_async_remote_copy(src_ref=win, dst_ref=win, send_sem=send_sems.at[idx],
                                        recv_sem=recv_sems.at[idx], device_id=(*chip, c), device_id_type=MESH)


def _fetch_start(fulls, shard_shapes, col_sharded):
    n = len(fulls)

    def body(*refs):
        full = refs[:n]
        send_sems, recv_sems = refs[n], refs[n + 1]
        token = refs[-1]
        x, y, c = _place()
        for w in range(n):
            for k, chip in enumerate(_other_chips(x, y)):
                _fetch_copy(full[w], shard_shapes[w], col_sharded[w], 2 * x + y, send_sems, recv_sems, w * 3 + k,
                            chip, c).start()
        token[...] = jnp.zeros_like(token)

    outs = pl.pallas_call(
        body, name="fetch_start",
        out_shape=(pltpu.SemaphoreType.DMA((3 * n,)), pltpu.SemaphoreType.DMA((3 * n,)),
                   *[pltpu.HBM(f.shape, f.dtype) for f in fulls], jax.ShapeDtypeStruct((8, 128), F32)),
        in_specs=[_HBM] * n,
        out_specs=(_SEM, _SEM, *([_HBM] * n), pl.BlockSpec(memory_space=pltpu.VMEM)),
        input_output_aliases={i: 2 + i for i in range(n)},
        compiler_params=pltpu.CompilerParams(has_side_effects=pltpu.SideEffectType.DATAFLOW_SIDE_EFFECTING),
    )(*[pltpu.with_memory_space_constraint(f, pltpu.HBM) for f in fulls])
    return dict(send=outs[0], recv=outs[1], full=list(outs[2:2 + n])), outs[-1]


def _fetch_wait(group, shard_shapes, col_sharded, after):
    n = len(group["full"])

    def body(*refs):
        full = refs[:n]
        send_sems, recv_sems = refs[n], refs[n + 1]
        x, y, c = _place()
        for w in range(n):
            for k, chip in enumerate(_other_chips(x, y)):
                _fetch_copy(full[w], shard_shapes[w], col_sharded[w], 2 * x + y, send_sems, recv_sems, w * 3 + k,
                            chip, c).wait_send()
                _fetch_copy(full[w], shard_shapes[w], col_sharded[w], 2 * chip[0] + chip[1], send_sems, recv_sems,
                            w * 3 + k, chip, c).wait_recv()

    outs = pl.pallas_call(
        body, name="fetch_wait", out_shape=tuple(pltpu.HBM(f.shape, f.dtype) for f in group["full"]),
        in_specs=[_HBM] * n + [_SEM, _SEM, pl.BlockSpec(memory_space=pl.ANY)],
        out_specs=tuple([_HBM] * n), input_output_aliases={i: i for i in range(n)},
        compiler_params=pltpu.CompilerParams(has_side_effects=pltpu.SideEffectType.DATAFLOW_SIDE_EFFECTING),
    )(*group["full"], group["send"], group["recv"], after)
    return list(outs)


_HBM = pl.BlockSpec(memory_space=pltpu.HBM)
_SEM = pl.BlockSpec(memory_space=pltpu.SEMAPHORE)
_N_PEER = N_DEV - 1


def _peer(x, y, c, k):
    return (x + (k >> 2)) % 2, (y + ((k >> 1) & 1)) % 2, (c + (k & 1)) % 2


def _exchange_copy(src_ref, land_ref, sliced, send_sems, recv_sems, idx, x, y, c, k):
    px, py, pc = _peer(x, y, c, k)
    src = src_ref.at[4 * px + 2 * py + pc] if sliced else src_ref
    return pltpu.make_async_remote_copy(
        src_ref=src, dst_ref=land_ref.at[4 * x + 2 * y + c],
        send_sem=send_sems.at[idx], recv_sem=recv_sems.at[idx], device_id=(px, py, pc), device_id_type=MESH)


def _exchange_start(tag, arrays, sliced):
    n = len(arrays)
    lands = [lax.empty(a.shape if s else (N_DEV,) + a.shape, a.dtype) for a, s in zip(arrays, sliced)]

    def body(*refs):
        src, land = refs[:n], refs[n:2 * n]
        send_sems, recv_sems = refs[2 * n], refs[2 * n + 1]
        token = refs[-1]
        x, y, c = _place()
        for w in range(n):
            for k in range(1, N_DEV):
                _exchange_copy(src[w], land[w], sliced[w], send_sems, recv_sems, w * _N_PEER + k - 1, x, y, c, k).start()
        token[...] = jnp.zeros_like(token)

    outs = pl.pallas_call(
        body, name="exchange_start_" + tag,
        out_shape=(pltpu.SemaphoreType.DMA((n * _N_PEER,)), pltpu.SemaphoreType.DMA((n * _N_PEER,)),
                   *[pltpu.HBM(a.shape, a.dtype) for a in arrays], *[pltpu.HBM(l.shape, l.dtype) for l in lands],
                   jax.ShapeDtypeStruct((8, 128), F32)),
        in_specs=[_HBM] * (2 * n),
        out_specs=(_SEM, _SEM, *([_HBM] * (2 * n)), pl.BlockSpec(memory_space=pltpu.VMEM)),
        input_output_aliases={i: 2 + i for i in range(2 * n)},
        compiler_params=pltpu.CompilerParams(has_side_effects=pltpu.SideEffectType.DATAFLOW_SIDE_EFFECTING),
    )(*[pltpu.with_memory_space_constraint(a, pltpu.HBM) for a in arrays],
      *[pltpu.with_memory_space_constraint(l, pltpu.HBM) for l in lands])
    return dict(send=outs[0], recv=outs[1], src=list(outs[2:2 + n]), land=list(outs[2 + n:2 + 2 * n]),
                sliced=list(sliced)), outs[-1]


def _exchange_wait(tag, groups, after):
    counts = [len(g["src"]) for g in groups]
    total = sum(counts)

    def body(*refs):
        pos = 0
        x, y, c = _place()
        for g, n in zip(groups, counts):
            src, land = refs[pos:pos + n], refs[pos + n:pos + 2 * n]
            send_sems, recv_sems = refs[pos + 2 * n], refs[pos + 2 * n + 1]
            pos += 2 * n + 2
            for w in range(n):
                for k in range(1, N_DEV):
                    cp = _exchange_copy(src[w], land[w], g["sliced"][w], send_sems, recv_sems,
                                        w * _N_PEER + k - 1, x, y, c, k)
                    cp.wait_send()
                    cp.wait_recv()

    operands, in_specs, aliases, out_shape = [], [], {}, []
    for g in groups:
        for a in g["src"] + g["land"]:
            aliases[len(operands)] = len(out_shape)
            out_shape.append(pltpu.HBM(a.shape, a.dtype))
            operands.append(a)
            in_specs.append(_HBM)
        operands += [g["send"], g["recv"]]
        in_specs += [_SEM, _SEM]
    operands.append(after)
    in_specs.append(pl.BlockSpec(memory_space=pl.ANY))
    outs = pl.pallas_call(
        body, name="exchange_wait_" + tag, out_shape=tuple(out_shape), in_specs=in_specs,
        out_specs=tuple([_HBM] * (2 * total)), input_output_aliases=aliases,
        compiler_params=pltpu.CompilerParams(has_side_effects=pltpu.SideEffectType.DATAFLOW_SIDE_EFFECTING),
    )(*operands)
    srcs, lands, pos = [], [], 0
    for n in counts:
        srcs += list(outs[pos:pos + n])
        lands += list(outs[pos + n:pos + 2 * n])
        pos += 2 * n
    return srcs, lands


def _sum_and_swap(tag, pieces, lands, small=None, small_land=None):
    n_w = len(pieces)
    n_small = 0 if small is None else 1

    def body(*refs):
        g_refs, land_refs = refs[:n_w], refs[n_w:2 * n_w]
        pos = 2 * n_w + 2 * n_small
        out_refs = refs[pos:pos + n_w]
        pos += n_w + n_small
        bufs = refs[pos:pos + n_w]
        load_sems, swap_send, swap_recv = refs[pos + n_w + 2 * n_small:]
        x, y, c = _place()
        me = 4 * x + 2 * y + c

        def slot(k):
            px, py, pc = _peer(x, y, c, k)
            return 4 * px + 2 * py + pc

        def swap_copy(w, half):
            rows = pieces[w].shape[1]
            win = out_refs[w].at[pl.ds(pl.multiple_of(half * rows, rows), rows), :]
            return pltpu.make_async_remote_copy(
                src_ref=win, dst_ref=win, send_sem=swap_send.at[w], recv_sem=swap_recv.at[w],
                device_id=(x, y, 1 - c), device_id_type=MESH)

        loads = []
        for w in range(n_w):
            per_w = [pltpu.make_async_copy(g_refs[w].at[me], bufs[w].at[me], load_sems.at[w * N_DEV])]
            per_w += [pltpu.make_async_copy(land_refs[w].at[slot(k)], bufs[w].at[slot(k)], load_sems.at[w * N_DEV + k])
                      for k in range(1, N_DEV)]
            loads.append(per_w)
        small_loads = []
        if n_small:
            small_ref, small_land_ref = refs[2 * n_w], refs[2 * n_w + 1]
            small_out = refs[2 * n_w + 2 + n_w]
            small_buf, small_sems = refs[pos + n_w], refs[pos + n_w + 1]
            small_loads = [pltpu.make_async_copy(small_land_ref.at[slot(k)], small_buf.at[slot(k)],
                                                 small_sems.at[k - 1]) for k in range(1, N_DEV)]
        for cp in [cp for per_w in loads for cp in per_w] + small_loads:
            cp.start()
        if n_small:
            small_buf[me] = small_ref[...]
        swaps = []
        for w in range(n_w):
            for cp in loads[w]:
                cp.wait()
            rows = pieces[w].shape[1]
            total = bufs[w][0].astype(F32)
            for p in range(1, N_DEV):
                total += bufs[w][p].astype(F32)
            out_refs[w][pl.ds(pl.multiple_of(c * rows, rows), rows), :] = total
            sw = swap_copy(w, c)
            sw.start()
            swaps.append(sw)
        if n_small:
            for cp in small_loads:
                cp.wait()
            total = small_buf[0]
            for p in range(1, N_DEV):
                total += small_buf[p]
            small_out[...] = total
        for w in range(n_w):
            swap_copy(w, 1 - c).wait_recv()
        for sw in swaps:
            sw.wait_send()

    vmem = pl.BlockSpec(memory_space=pltpu.VMEM)
    hbm = pl.BlockSpec(memory_space=pl.ANY)
    small_args = [small, small_land] if n_small else []
    small_shapes = [jax.ShapeDtypeStruct(small.shape, F32)] if n_small else []
    small_scratch = ([pltpu.VMEM((N_DEV,) + small.shape, F32), pltpu.SemaphoreType.DMA((_N_PEER,))]
                     if n_small else [])
    return pl.pallas_call(
        body, name="sum_and_swap_" + tag,
        in_specs=[hbm] * (2 * n_w) + [vmem, hbm] * n_small,
        out_specs=[vmem] * (n_w + n_small),
        out_shape=[jax.ShapeDtypeStruct((2 * p.shape[1], p.shape[2]), F32) for p in pieces] + small_shapes,
        scratch_shapes=[pltpu.VMEM(p.shape, BF16) for p in pieces] + small_scratch
        + [pltpu.SemaphoreType.DMA((n_w * N_DEV,)), pltpu.SemaphoreType.DMA((n_w,)),
           pltpu.SemaphoreType.DMA((n_w,))],
        compiler_params=pltpu.CompilerParams(vmem_limit_bytes=VMEM_LIMIT),
    )(*pieces, *lands, *small_args)


def _adamw_values(w, g_t, m, v):
    c1 = 1.0 - ADAM_B1 ** ADAM_STEP
    c2 = 1.0 - ADAM_B2 ** ADAM_STEP
    nm = ADAM_B1 * m + (1.0 - ADAM_B1) * g_t
    nv = ADAM_B2 * v + (1.0 - ADAM_B2) * (g_t * g_t)
    return -ADAM_LR * ((nm / c1) / (jnp.sqrt(nv / c2) + ADAM_EPS) + ADAM_WD * w), nm, nv


def _adamw_update(w_ref, g_ref, m_ref, v_ref, d_ref, nm_ref, nv_ref):
    d_ref[...], nm_ref[...], nv_ref[...] = _adamw_values(w_ref[...], g_ref[...], m_ref[...], v_ref[...])


def _adamw_small(packed, shard_index, names, weights, moments_m, moments_v):
    n = len(names)
    shapes = [weights[name].shape for name in names]
    flat = [a[name].reshape(-1, a[name].shape[-1]) for name in names for a in (weights, moments_m, moments_v)]

    def body(packed_ref, shard_ref, *refs):
        loss_row = _small_offset("loss")
        refs[-1][...] = packed_ref[loss_row:loss_row + 1, 0:1]
        for k, name in enumerate(names):
            w_ref, m_ref, v_ref = refs[3 * k:3 * k + 3]
            g_ref, d_ref, nm_ref, nv_ref = refs[3 * n + 4 * k:3 * n + 4 * k + 4]
            rows, cols = w_ref.shape
            first = _small_offset(name)
            if cols <= _LANES:
                blocks = [(slice(0, rows), packed_ref[first:first + rows, 0:cols])]
            else:
                per_row = cols // _LANES
                if cols < _SMALL_SHAPES[name][-1]:
                    first = first + shard_ref[0] * per_row
                blocks = [(slice(i, i + 1),
                           jnp.concatenate([packed_ref[pl.ds(first + i * per_row + j, 1), :] for j in range(per_row)],
                                           axis=1)) for i in range(rows)]
            for at, g_t in blocks:
                g_ref[at, :] = g_t
                d_ref[at, :], nm_ref[at, :], nv_ref[at, :] = _adamw_values(w_ref[at, :], g_t, m_ref[at, :],
                                                                           v_ref[at, :])

    vmem = pl.BlockSpec(memory_space=pltpu.VMEM)
    outs = pl.pallas_call(
        body, name="adamw_small",
        in_specs=[vmem, pl.BlockSpec(memory_space=pltpu.SMEM)] + [vmem] * (3 * n),
        out_shape=[jax.ShapeDtypeStruct(flat[3 * k].shape, F32) for k in range(n) for _ in range(4)]
        + [jax.ShapeDtypeStruct((1, 1), F32)],
    )(packed, shard_index.reshape(1).astype(jnp.int32), *flat)
    return [tuple(o.reshape(shapes[k]) for o in outs[4 * k:4 * k + 4]) for k in range(n)], outs[-1].reshape(())


def _adamw(label, w, g, m, v):
    shape = w.shape
    cols = shape[-1]
    rows = w.size // cols
    args = [a.reshape(rows, cols) for a in (w, g, m, v)]

    def body(w_ref, g_ref, m_ref, v_ref, g_out, d_ref, nm_ref, nv_ref):
        g_out[...] = g_ref[...]
        _adamw_update(w_ref, g_ref, m_ref, v_ref, d_ref, nm_ref, nv_ref)

    block_rows = 256 if rows % 256 == 0 and rows > 256 else rows
    spec = pl.BlockSpec((block_rows, cols), lambda i: (i, 0))
    outs = pl.pallas_call(
        body, name="adamw_" + label, grid=(rows // block_rows,),
        in_specs=[spec] * 4, out_specs=[spec] * 4,
        out_shape=[jax.ShapeDtypeStruct((rows, cols), F32)] * 4,
        compiler_params=_params(),
    )(*args)
    return [o.reshape(shape) for o in outs]


def _no_send(tag, arrays, sliced):
    return jnp.zeros((8, 128), F32)


def _local_step(x, tgt, w_in_a, later_weights, first_after, sgu_ln_g, sgu_ln_b, w_spatial, b_spatial,
                attn_sinks, rel_bias, post_ln_g, post_ln_b, send=_no_send):
    bsp_t = b_spatial.T
    g1, b1 = post_ln_g[0:1], post_ln_b[0:1]
    g2, b2 = post_ln_g[1:2], post_ln_b[1:2]
    onehot = _bucket_onehot()
    bias = _bias_expand(rel_bias.T, onehot)
    win = _window_tables()

    xt, u, vh, z, rv, y = _layer_a_fwd(x, w_in_a, sgu_ln_g, sgu_ln_b, w_spatial, bsp_t, first_after)
    w_out_a, w_kv, w_in_b, w_out_b = later_weights(y)
    xh1, rstd1, q, zb, kd, vd = _layer_b_proj(x, y, w_out_a, g1, b1, w_in_b, w_kv)
    o, probs, sink_probs, dr2, loss_vec, dg2, db2 = _layer_b_fwd(q, zb, kd, vd, bias, win, attn_sinks, xh1, g1, b1,
                                                                 w_out_b, g2, b2, tgt)
    dq, dzb, dkd, dvd, carry_k, carry_v, gw_out_b, dsink, dbias = _layer_b_bwd_attn(
        dr2, zb, o, q, kd, vd, probs, sink_probs, w_out_b)
    dr1, dg1, db1, gw_in_b, gw_kv = _layer_b_bwd_proj(xh1, rstd1, g1, b1, dr2, dq, dzb, dkd, dvd, carry_k, carry_v,
                                                      w_in_b, w_kv)
    gw_out_b = gw_out_b.reshape(N_DEV, -1, D_MODEL)
    gw_kv = gw_kv.reshape(N_DEV, -1, 2 * PAIR)
    after = send("b", [gw_out_b, gw_in_b, gw_kv], [True, True, True])
    dp, gw_out_a, dws, dbsp, dgs, dbs = _layer_a_bwd_mix(dr1, u, vh, z, y, rv, w_out_a, sgu_ln_g, sgu_ln_b,
                                                         w_spatial, bsp_t, after)
    gw_out_a = gw_out_a.reshape(N_DEV, -1, D_MODEL)
    small = _pack_small(dws, dbsp, dsink, dbias, onehot, (dg1, dg2), (db1, db2), dgs, dbs, loss_vec)
    after = send("a_out", [gw_out_a, small], [True, False])
    gw_in_a = _layer_a_bwd_win(xt, dp, after).reshape(N_DEV, D_MODEL // 2, -1)
    after = send("a_in", [gw_in_a], [True])
    after, updates = after if isinstance(after, tuple) else (after, ())
    grad_x, *updated = _layer_a_bwd_dx(dr1, dp, w_in_a, after, updates)

    pieces = [gw_in_a, gw_out_a, gw_kv, gw_in_b, gw_out_b]
    return grad_x, pieces, small, updated


def kernel(x, w_in_a, sgu_ln_g, sgu_ln_b, w_spatial, b_spatial, w_out_a, w_kv, w_in_b, attn_sinks, rel_bias, w_out_b, post_ln_g, post_ln_b, loss_target, m_w_in_a, m_sgu_ln_g, m_sgu_ln_b, m_w_spatial, m_b_spatial, m_w_out_a, m_w_kv, m_w_in_b, m_attn_sinks, m_rel_bias, m_w_out_b, m_post_ln_g, m_post_ln_b, v_w_in_a, v_sgu_ln_g, v_sgu_ln_b, v_w_spatial, v_b_spatial, v_w_out_a, v_w_kv, v_w_in_b, v_attn_sinks, v_rel_bias, v_w_out_b, v_post_ln_g, v_post_ln_b):
    weights = dict(w_in_a=w_in_a, sgu_ln_g=sgu_ln_g, sgu_ln_b=sgu_ln_b, w_spatial=w_spatial, b_spatial=b_spatial,
                   w_out_a=w_out_a, w_kv=w_kv, w_in_b=w_in_b, attn_sinks=attn_sinks, rel_bias=rel_bias,
                   w_out_b=w_out_b, post_ln_g=post_ln_g, post_ln_b=post_ln_b)
    moments_m = dict(w_in_a=m_w_in_a, sgu_ln_g=m_sgu_ln_g, sgu_ln_b=m_sgu_ln_b, w_spatial=m_w_spatial,
                     b_spatial=m_b_spatial, w_out_a=m_w_out_a, w_kv=m_w_kv, w_in_b=m_w_in_b,
                     attn_sinks=m_attn_sinks, rel_bias=m_rel_bias, w_out_b=m_w_out_b, post_ln_g=m_post_ln_g,
                     post_ln_b=m_post_ln_b)
    moments_v = dict(w_in_a=v_w_in_a, sgu_ln_g=v_sgu_ln_g, sgu_ln_b=v_sgu_ln_b, w_spatial=v_w_spatial,
                     b_spatial=v_b_spatial, w_out_a=v_w_out_a, w_kv=v_w_kv, w_in_b=v_w_in_b,
                     attn_sinks=v_attn_sinks, rel_bias=v_rel_bias, w_out_b=v_w_out_b, post_ln_g=v_post_ln_g,
                     post_ln_b=v_post_ln_b)
    order = ("w_in_a", "sgu_ln_g", "sgu_ln_b", "w_spatial", "b_spatial", "w_out_a", "w_kv", "w_in_b", "attn_sinks",
             "rel_bias", "w_out_b", "post_ln_g", "post_ln_b")

    shard_index = 2 * lax.axis_index("x") + lax.axis_index("y")
    ln_shard = jnp.concatenate([sgu_ln_g, sgu_ln_b], axis=0)
    shards = [w_in_a[0], w_out_a[0], w_kv, w_in_b[0], w_out_b[0]]
    col_sharded = [True, False, False, True, False]
    full_in_a, *later, ln_full = _gather_weights(shards, col_sharded, [True, False, False, False, False], ln_shard)
    ln_full = jnp.transpose(ln_full, (1, 0, 2)).reshape(2, A_WIDTH)
    later_shapes = [s.shape for s in shards[1:]]
    fetch_group, fetch_token = _fetch_start(later, later_shapes, col_sharded[1:])

    def later_weights(y):
        return _fetch_wait(fetch_group, later_shapes, col_sharded[1:], y)

    groups, grads, deltas, new_m, new_v, scalars = {}, {}, {}, {}, {}, {}
    early = ("w_out_b", "w_in_b", "w_kv", "w_out_a")

    def two_dim(a):
        return a.reshape(-1, a.shape[-1])

    def send(tag, arrays, sliced):
        groups[tag], token = _exchange_start(tag, arrays, sliced)
        if tag != "a_in":
            return token
        srcs, lands = _exchange_wait("early", [groups["b"], groups["a_out"]], token)
        *reduced, packed_sum = _sum_and_swap("early", srcs[:4], lands[:4], srcs[4], lands[4])
        updates = [(two_dim(weights[n]), g, two_dim(moments_m[n]), two_dim(moments_v[n]))
                   for n, g in zip(early, reduced)]
        small_names = ("sgu_ln_g", "sgu_ln_b", "w_spatial", "b_spatial", "attn_sinks", "rel_bias", "post_ln_g",
                       "post_ln_b")
        small_updates, scalars["loss"] = _adamw_small(packed_sum, shard_index, small_names, weights, moments_m,
                                                      moments_v)
        for name, (g, d, nm, nv) in zip(small_names, small_updates):
            grads[name], deltas[name], new_m[name], new_v[name] = g, d, nm, nv
        return new_m["b_spatial"].reshape(A_GROUPS, CHUNK), updates

    grad_x, _, _, updated = _local_step(
        x[0], loss_target[0], full_in_a, later_weights, fetch_token, ln_full[0:1], ln_full[1:2], w_spatial[0],
        b_spatial[0], attn_sinks, rel_bias, post_ln_g, post_ln_b, send=send)
    for k, name in enumerate(early):
        grads[name], deltas[name], new_m[name], new_v[name] = [
            a.reshape(weights[name].shape) for a in updated[4 * k:4 * k + 4]]

    srcs, lands = _exchange_wait("late", [groups["a_in"]], grad_x)
    (g_in_a,) = _sum_and_swap("late", srcs, lands)
    grads["w_in_a"], deltas["w_in_a"], new_m["w_in_a"], new_v["w_in_a"] = _adamw(
        "w_in_a", w_in_a, g_in_a.reshape(w_in_a.shape), m_w_in_a, v_w_in_a)
    return (scalars["loss"], grad_x[None], *[grads[n] for n in order], *[deltas[n] for n in order],
            *[new_m[n] for n in order], *[new_v[n] for n in order])
```

```python
import math

import jax
import jax.numpy as jnp
from jax import lax
from jax.experimental import pallas as pl
from jax.experimental.pallas import tpu as pltpu

F32 = jnp.float32
BF16 = jnp.bfloat16

D_MODEL = 1024
A_WIDTH = 2048
A_GROUPS = 8
A_GROUP_DIM = 256
CHUNK = 128
N_HEADS = 16
N_KV = 2
HEAD_DIM = 64
PAIR = 2 * HEAD_DIM
B_WIDTH = 1024
REL_BUCKETS = 32
ALPHA = 4.0 ** 0.25
LN_EPS = 1e-5
NEG_INF = -1e30
SCALE = HEAD_DIM ** -0.5

ADAM_LR = 0.001
ADAM_B1 = 0.9
ADAM_B2 = 0.999
ADAM_EPS = 1e-08
ADAM_WD = 0.01
ADAM_STEP = 10

N_DEV = 8
N_CHIPS = 4
MESH = pl.DeviceIdType.MESH
VMEM_LIMIT = 56 * 1024 * 1024

TM_ATTN = 256
TM_MM = 512
TM_WIN = 1024
_LANES = 128
_SUBLANES = 8


def _dot(a, b):
    return jnp.dot(a, b, preferred_element_type=F32)


def _dot_nt(a, b):
    return lax.dot_general(a, b, (((1,), (1,)), ((), ())), preferred_element_type=F32)


def _dot_tn(a, b):
    return lax.dot_general(a, b, (((0,), (0,)), ((), ())), preferred_element_type=F32)


def _ln_fwd(r):
    mu = jnp.mean(r, axis=-1, keepdims=True)
    rc = r - mu
    var = jnp.mean(rc * rc, axis=-1, keepdims=True)
    rstd = lax.rsqrt(var + LN_EPS)
    return rc * rstd, rstd


def _ln_bwd(dxh, xh, rstd):
    m1 = jnp.mean(dxh, axis=-1, keepdims=True)
    m2 = jnp.mean(dxh * xh, axis=-1, keepdims=True)
    return rstd * (dxh - m1 - xh * m2)


def _silu_parts(z):
    sg = jax.nn.sigmoid(z)
    return z * sg, sg * (1.0 + z * (1.0 - sg))


def _dup_halves(blk):
    sw = pltpu.roll(blk, HEAD_DIM, 1)
    lo = lax.broadcasted_iota(jnp.int32, blk.shape, 1) < HEAD_DIM
    return jnp.where(lo, blk, sw), jnp.where(lo, sw, blk)


def _fold_halves(blk):
    return blk + pltpu.roll(blk, HEAD_DIM, 1)


def _resident(shape):
    nd = len(shape)
    return pl.BlockSpec(shape, lambda *_: (0,) * nd, pipeline_mode=pl.Buffered(1))


def _const(shape):
    nd = len(shape)
    return pl.BlockSpec(shape, lambda *_: (0,) * nd)


def _rows(tm, cols):
    return pl.BlockSpec((tm, cols), lambda i: (i, 0))


def _params(sem=("arbitrary",)):
    return pltpu.CompilerParams(dimension_semantics=sem, vmem_limit_bytes=VMEM_LIMIT)


def _spatial_mix(ws_ref, bsp_ref, vn, s_scr, n_chunks):
    tri = (lax.broadcasted_iota(jnp.int32, (CHUNK, CHUNK), 0)
           >= lax.broadcasted_iota(jnp.int32, (CHUNK, CHUNK), 1))
    for g in range(A_GROUPS):
        wsg = jnp.where(tri, ws_ref[g], 0.0).astype(BF16)
        cols = slice(g * A_GROUP_DIM, (g + 1) * A_GROUP_DIM)
        for ci in range(n_chunks):
            rows = slice(ci * CHUNK, (ci + 1) * CHUNK)
            s_scr[rows, cols] = _dot(wsg, vn[rows, cols]) + bsp_ref[:, g:g + 1]


def _layer_a_fwd(x, w_in, lng, lnb, ws, bsp_t, after):
    t_len = x.shape[0]
    tm = TM_ATTN

    def body(x_ref, win_ref, lng_ref, lnb_ref, ws_ref, bsp_ref, after_ref,
             xt_ref, u_ref, vh_ref, z_ref, rv_ref, y_ref, s_scr):
        x_t = x_ref[...]
        xb = x_t.astype(BF16)
        xt_ref[...] = x_t.T.astype(BF16)
        u = _dot(xb, win_ref[:, 0:A_WIDTH])
        v = _dot(xb, win_ref[:, A_WIDTH:2 * A_WIDTH])
        z = _dot(xb, win_ref[:, 2 * A_WIDTH:3 * A_WIDTH])
        vh, rv = _ln_fwd(v)
        vn = (vh * lng_ref[...] + lnb_ref[...]).astype(BF16)
        _spatial_mix(ws_ref, bsp_ref, vn, s_scr, tm // CHUNK)
        sz, _ = _silu_parts(z)
        y_ref[...] = (u * s_scr[...] * sz).astype(BF16)
        u_ref[...] = u.astype(BF16)
        vh_ref[...] = vh.astype(BF16)
        z_ref[...] = z.astype(BF16)
        rv_ref[...] = rv

    wide = jax.ShapeDtypeStruct((t_len, A_WIDTH), BF16)
    return pl.pallas_call(
        body, name="layer_a_fwd", grid=(t_len // tm,),
        in_specs=[_rows(tm, D_MODEL), _resident(w_in.shape), _const(lng.shape), _const(lnb.shape), _const(ws.shape),
                  _const(bsp_t.shape), _const(after.shape)],
        out_specs=[pl.BlockSpec((D_MODEL, tm), lambda i: (0, i)), _rows(tm, A_WIDTH), _rows(tm, A_WIDTH),
                   _rows(tm, A_WIDTH), _rows(tm, 1), _rows(tm, A_WIDTH)],
        out_shape=[jax.ShapeDtypeStruct((D_MODEL, t_len), BF16), wide, wide, wide,
                   jax.ShapeDtypeStruct((t_len, 1), F32), wide],
        scratch_shapes=[pltpu.VMEM((tm, A_WIDTH), F32)],
        compiler_params=_params(),
    )(x, w_in, lng, lnb, ws, bsp_t, after)


def _layer_b_proj(x, y, w_out_a, g1, b1, w_in, w_kv):
    t_len = x.shape[0]
    tm = TM_MM

    def body(x_ref, y_ref, wout_ref, g_ref, b_ref, win_ref, wkv_ref, xh_ref, r1_ref, q_ref, z_ref, kd_ref, vd_ref):
        xh, r1 = _ln_fwd(ALPHA * x_ref[...] + _dot(y_ref[...], wout_ref[...]))
        xh_ref[...] = xh
        r1_ref[...] = r1
        h1 = (xh * g_ref[...] + b_ref[...]).astype(BF16)
        q_ref[...] = (_dot(h1, win_ref[:, 0:B_WIDTH]) * SCALE).astype(BF16)
        z_ref[...] = _dot(h1, win_ref[:, B_WIDTH:2 * B_WIDTH]).astype(BF16)
        kv = _dot(h1, wkv_ref[...])
        k0, k1 = _dup_halves(kv[:, 0:PAIR])
        v0, v1 = _dup_halves(kv[:, PAIR:2 * PAIR])
        kd_ref[:, 0:PAIR] = k0.astype(BF16)
        kd_ref[:, PAIR:2 * PAIR] = k1.astype(BF16)
        vd_ref[:, 0:PAIR] = v0.astype(BF16)
        vd_ref[:, PAIR:2 * PAIR] = v1.astype(BF16)

    return pl.pallas_call(
        body, name="layer_b_proj", grid=(t_len // tm,),
        in_specs=[_rows(tm, D_MODEL), _rows(tm, A_WIDTH), _resident(w_out_a.shape), _const(g1.shape),
                  _const(b1.shape), _resident(w_in.shape), _resident(w_kv.shape)],
        out_specs=[_rows(tm, D_MODEL), _rows(tm, 1), _rows(tm, B_WIDTH), _rows(tm, B_WIDTH), _rows(tm, 2 * PAIR),
                   _rows(tm, 2 * PAIR)],
        out_shape=[jax.ShapeDtypeStruct((t_len, D_MODEL), F32), jax.ShapeDtypeStruct((t_len, 1), F32),
                   jax.ShapeDtypeStruct((t_len, B_WIDTH), BF16), jax.ShapeDtypeStruct((t_len, B_WIDTH), BF16),
                   jax.ShapeDtypeStruct((t_len, 2 * PAIR), BF16), jax.ShapeDtypeStruct((t_len, 2 * PAIR), BF16)],
        compiler_params=_params(),
    )(x, y, w_out_a, g1, b1, w_in, w_kv)


GROUP = N_HEADS // N_KV
GROUP_Q = GROUP * CHUNK


def _window_tables():
    j = jnp.arange(2 * CHUNK, dtype=jnp.int32)[:, None]
    t = jnp.arange(CHUNK, dtype=jnp.int32)[None, :]
    dist = t + CHUNK - j
    inside = (dist >= 0) & (dist < CHUNK)
    return jnp.stack([inside & (j >= CHUNK), inside]).astype(F32)


def _band(ref, chunk_index, kvh):
    prev0 = pl.multiple_of(jnp.maximum(chunk_index - 1, 0) * CHUNK, CHUNK)
    cur0 = pl.multiple_of(chunk_index * CHUNK, CHUNK)
    cols = slice(kvh * PAIR, (kvh + 1) * PAIR)
    return jnp.concatenate([ref[pl.ds(prev0, CHUNK), cols], ref[pl.ds(cur0, CHUNK), cols]], axis=0)


def _group_tables(bias_ref, win_ref, sink_ref, chunk_index, kvh):
    bias = jnp.concatenate([bias_ref[kvh * GROUP + j] for j in range(GROUP)], axis=1)
    win = win_ref[jnp.minimum(chunk_index, 1)]
    mask = jnp.concatenate([win] * GROUP, axis=1) > 0.5
    sink = jnp.concatenate([jnp.full((1, CHUNK), sink_ref[0, kvh * GROUP + j], F32) for j in range(GROUP)], axis=1)
    return bias, mask, sink


def _attn_probs(qs, kband, bias, mask, sink):
    logits = jnp.where(mask, _dot_nt(kband, qs) + bias, NEG_INF)
    m = jnp.maximum(jnp.max(logits, axis=0, keepdims=True), sink)
    e = jnp.exp(logits - m)
    es = jnp.exp(sink - m)
    inv = 1.0 / (jnp.sum(e, axis=0, keepdims=True) + es)
    return e * inv, es * inv


def _half_mask():
    return lax.broadcasted_iota(jnp.int32, (CHUNK, PAIR), 1) < HEAD_DIM


def _stack_heads(src_ref, rows, kvh, dst_scr, lo):
    for j in range(GROUP):
        h = kvh * GROUP + j
        blk = src_ref[rows, (h // 2) * PAIR:(h // 2 + 1) * PAIR].astype(F32)
        keep = lo if h % 2 == 0 else ~lo
        dst_scr[j * CHUNK:(j + 1) * CHUNK, :] = jnp.where(keep, blk, 0.0).astype(BF16)


def _probs_spec(tm):
    return pl.BlockSpec((tm // CHUNK, N_KV, 2 * CHUNK, GROUP_Q), lambda i: (i, 0, 0, 0))


def _sink_probs_spec():
    return pl.BlockSpec((1, 8, GROUP_Q), lambda i: (i, 0, 0))


def _unstack_pairs(stacked, pp, lo):
    return jnp.where(lo, stacked[(2 * pp) * CHUNK:(2 * pp + 1) * CHUNK], stacked[(2 * pp + 1) * CHUNK:(2 * pp + 2) * CHUNK])


def _layer_b_fwd(q, zb, kd, vd, bias, win, sinks, xh1, g1, b1, w_out, g2, b2, tgt):
    t_len = q.shape[0]
    tm = TM_ATTN

    def body(q_ref, z_ref, kd_ref, vd_ref, bias_ref, win_ref, sink_ref, xh_ref, g1_ref, b1_ref, wout_ref, g2_ref,
             b2_ref, tgt_ref, o_ref, p_ref, ps_ref, dr_ref, loss_ref, dg_ref, db_ref, o_scr, qs_scr):
        i = pl.program_id(0)

        @pl.when(i == 0)
        def _():
            loss_ref[...] = jnp.zeros_like(loss_ref)
            dg_ref[...] = jnp.zeros_like(dg_ref)
            db_ref[...] = jnp.zeros_like(db_ref)

        lo = _half_mask()
        ps_ref[...] = jnp.zeros_like(ps_ref)
        for ci in range(tm // CHUNK):
            cg = i * (tm // CHUNK) + ci
            rows = slice(ci * CHUNK, (ci + 1) * CHUNK)
            for kvh in range(N_KV):
                kband = _band(kd_ref, cg, kvh)
                vband = _band(vd_ref, cg, kvh)
                bias_g, mask, sink = _group_tables(bias_ref, win_ref, sink_ref, cg, kvh)
                _stack_heads(q_ref, rows, kvh, qs_scr, lo)
                p, p_sink = _attn_probs(qs_scr[...], kband, bias_g, mask, sink)
                p = p.astype(BF16)
                p_ref[ci, kvh] = p
                ps_ref[0, ci * N_KV + kvh:ci * N_KV + kvh + 1, :] = p_sink
                o_stack = _dot_tn(p, vband)
                for pp in range(GROUP // 2):
                    pair = kvh * (GROUP // 2) + pp
                    o_scr[rows, pair * PAIR:(pair + 1) * PAIR] = _unstack_pairs(o_stack, pp, lo)
        o = o_scr[...]
        o_ref[...] = o.astype(BF16)
        sz, _ = _silu_parts(z_ref[...].astype(F32))
        y = (o * sz).astype(BF16)
        h1 = xh_ref[...] * g1_ref[...] + b1_ref[...]
        r = ALPHA * h1 + _dot(y, wout_ref[...])
        xh2, rstd2 = _ln_fwd(r)
        diff = xh2 * g2_ref[...] + b2_ref[...] - tgt_ref[...]
        loss_ref[...] += jnp.sum(diff * diff, axis=0, keepdims=True)
        dh2 = diff * (1.0 / D_MODEL)
        dg_ref[...] += jnp.sum(dh2 * xh2, axis=0, keepdims=True)
        db_ref[...] += jnp.sum(dh2, axis=0, keepdims=True)
        dr_ref[...] = _ln_bwd(dh2 * g2_ref[...], xh2, rstd2)

    vec = jax.ShapeDtypeStruct((1, D_MODEL), F32)
    return pl.pallas_call(
        body, name="layer_b_fwd", grid=(t_len // tm,),
        in_specs=[_rows(tm, B_WIDTH), _rows(tm, B_WIDTH), _resident(kd.shape), _resident(vd.shape),
                  _resident(bias.shape), _resident(win.shape), pl.BlockSpec(memory_space=pltpu.SMEM),
                  _rows(tm, D_MODEL), _const(g1.shape), _const(b1.shape), _resident(w_out.shape), _const(g2.shape),
                  _const(b2.shape), _rows(tm, D_MODEL)],
        out_specs=[_rows(tm, B_WIDTH), _probs_spec(tm), _sink_probs_spec(), _rows(tm, D_MODEL)]
        + [_const((1, D_MODEL))] * 3,
        out_shape=[jax.ShapeDtypeStruct((t_len, B_WIDTH), BF16),
                   jax.ShapeDtypeStruct((t_len // CHUNK, N_KV, 2 * CHUNK, GROUP_Q), BF16),
                   jax.ShapeDtypeStruct((t_len // tm, 8, GROUP_Q), F32),
                   jax.ShapeDtypeStruct((t_len, D_MODEL), F32), vec, vec, vec],
        scratch_shapes=[pltpu.VMEM((tm, B_WIDTH), F32), pltpu.VMEM((GROUP_Q, PAIR), BF16)],
        compiler_params=_params(),
    )(q, zb, kd, vd, bias, win, sinks, xh1, g1, b1, w_out, g2, b2, tgt)


def _layer_b_bwd_attn(dr2, zb, o, q, kd, vd, probs, sink_probs, w_out):
    t_len = q.shape[0]
    tm = TM_ATTN
    n_steps = t_len // tm
    n_chunks = tm // CHUNK

    def body(dr_ref, z_ref, o_ref, q_ref, kd_ref, vd_ref, p_ref, ps_ref, wout_ref,
             dq_ref, dz_ref, dkd_ref, dvd_ref, ck_ref, cv_ref, gw_ref, dsink_ref, dbias_ref,
             do_scr, qs_scr, dos_scr, gw_acc):
        i = pl.program_id(0)

        @pl.when(i == 0)
        def _():
            gw_acc[...] = jnp.zeros_like(gw_acc)
            dsink_ref[...] = jnp.zeros_like(dsink_ref)
            dbias_ref[...] = jnp.zeros_like(dbias_ref)

        drb = dr_ref[...].astype(BF16)
        n_blocks = N_KV * n_chunks
        block_cols = B_WIDTH // n_blocks

        def through_gate(b):
            cols = slice(b * block_cols, (b + 1) * block_cols)
            dy = _dot_nt(drb, wout_ref[cols, :])
            sz, dsz = _silu_parts(z_ref[:, cols].astype(F32))
            o_t = o_ref[:, cols].astype(F32)
            dz_ref[:, cols] = (dy * o_t * dsz).astype(BF16)
            do_scr[:, cols] = (dy * sz).astype(BF16)
            return (o_t * sz).astype(BF16)

        def weight_gradient(b, gated):
            cols = slice(b * block_cols, (b + 1) * block_cols)
            gw_acc[cols, :] += _dot_tn(gated, drb)

        gated = {b: through_gate(b) for b in range(n_chunks)}

        lo = _half_mask()
        for kvh in range(N_KV):
            kcols = slice(kvh * PAIR, (kvh + 1) * PAIR)
            dk_bands, dv_bands = [], []
            for ci in range(n_chunks):
                unit = kvh * n_chunks + ci
                if unit + n_chunks < n_blocks:
                    gated[unit + n_chunks] = through_gate(unit + n_chunks)
                weight_gradient(unit, gated[unit])
                cg = i * n_chunks + ci
                rows = slice(ci * CHUNK, (ci + 1) * CHUNK)
                kband = _band(kd_ref, cg, kvh)
                vband = _band(vd_ref, cg, kvh)
                _stack_heads(q_ref, rows, kvh, qs_scr, lo)
                _stack_heads(do_scr, rows, kvh, dos_scr, lo)
                qs = qs_scr[...]
                dos = dos_scr[...]
                pb = p_ref[ci, kvh]
                p = pb.astype(F32)
                p_sink = ps_ref[0, ci * N_KV + kvh:ci * N_KV + kvh + 1, :]
                dp = _dot_nt(vband, dos)
                delta = jnp.sum(p * dp, axis=0, keepdims=True)
                dlog = p * (dp - delta)
                for j in range(GROUP):
                    dbias_ref[kvh * GROUP + j] += dlog[:, j * CHUNK:(j + 1) * CHUNK]
                dsink_ref[kvh:kvh + 1, :] += -(p_sink * delta)
                ds = dlog.astype(BF16)
                dq_stack = _dot_tn(ds, kband) * SCALE
                for pp in range(GROUP // 2):
                    pair = kvh * (GROUP // 2) + pp
                    dq_ref[rows, pair * PAIR:(pair + 1) * PAIR] = _unstack_pairs(dq_stack, pp, lo).astype(BF16)
                dk_bands.append(_dot(ds, qs))
                dv_bands.append(_dot(pb, dos))
            for bands, out_ref, carry_ref in ((dk_bands, dkd_ref, ck_ref), (dv_bands, dvd_ref, cv_ref)):
                carry_ref[0, :, kcols] = bands[0][0:CHUNK]
                for ci in range(n_chunks):
                    own = bands[ci][CHUNK:2 * CHUNK]
                    if ci + 1 < n_chunks:
                        own = own + bands[ci + 1][0:CHUNK]
                    out_ref[ci * CHUNK:(ci + 1) * CHUNK, kcols] = own

        @pl.when(i == n_steps - 1)
        def _():
            gw_ref[...] = gw_acc[...].astype(BF16)

    carry_spec = pl.BlockSpec((1, CHUNK, 2 * PAIR), lambda i: (i, 0, 0))
    carry_shape = jax.ShapeDtypeStruct((n_steps, CHUNK, 2 * PAIR), F32)
    bias_shape = (N_HEADS, 2 * CHUNK, CHUNK)
    return pl.pallas_call(
        body, name="layer_b_bwd_attn", grid=(n_steps,),
        in_specs=[_rows(tm, D_MODEL), _rows(tm, B_WIDTH), _rows(tm, B_WIDTH), _rows(tm, B_WIDTH),
                  _resident(kd.shape), _resident(vd.shape), _probs_spec(tm), _sink_probs_spec(),
                  _resident(w_out.shape)],
        out_specs=[_rows(tm, B_WIDTH), _rows(tm, B_WIDTH), _rows(tm, 2 * PAIR), _rows(tm, 2 * PAIR),
                   carry_spec, carry_spec, _const(w_out.shape), _const((N_KV, GROUP_Q)), _const(bias_shape)],
        out_shape=[jax.ShapeDtypeStruct((t_len, B_WIDTH), BF16), jax.ShapeDtypeStruct((t_len, B_WIDTH), BF16),
                   jax.ShapeDtypeStruct((t_len, 2 * PAIR), F32), jax.ShapeDtypeStruct((t_len, 2 * PAIR), F32),
                   carry_shape, carry_shape, jax.ShapeDtypeStruct(w_out.shape, BF16),
                   jax.ShapeDtypeStruct((N_KV, GROUP_Q), F32), jax.ShapeDtypeStruct(bias_shape, F32)],
        scratch_shapes=[pltpu.VMEM((tm, B_WIDTH), BF16), pltpu.VMEM((GROUP_Q, PAIR), BF16),
                        pltpu.VMEM((GROUP_Q, PAIR), BF16), pltpu.VMEM(w_out.shape, F32)],
        compiler_params=_params(),
    )(dr2, zb, o, q, kd, vd, probs, sink_probs, w_out)


def _layer_b_bwd_proj(xh1, rstd1, g1, b1, dr2, dq, dzb, dkd, dvd, carry_k, carry_v, w_in, w_kv):
    t_len = xh1.shape[0]
    tm = TM_MM
    n_steps = t_len // tm
    per_tile = tm // TM_ATTN
    n_carry = carry_k.shape[0]

    def body(xh_ref, rstd_ref, g_ref, b_ref, dr2_ref, dq_ref, dz_ref, dkd_ref, dvd_ref, *rest):
        carry_refs = rest[:2 * per_tile]
        win_ref, wkv_ref, dr1_ref, dg_ref, db_ref, gwin_ref, gwkv_ref, acc_in, acc_kv = rest[2 * per_tile:]
        i = pl.program_id(0)

        @pl.when(i == 0)
        def _():
            acc_in[...] = jnp.zeros_like(acc_in)
            acc_kv[...] = jnp.zeros_like(acc_kv)
            dg_ref[...] = jnp.zeros_like(dg_ref)
            db_ref[...] = jnp.zeros_like(db_ref)

        lo = lax.broadcasted_iota(jnp.int32, (tm, PAIR), 1) < HEAD_DIM

        def heads_gradient(tile_ref, refs):
            parts = []
            for a in range(per_tile):
                parts.append(tile_ref[a * TM_ATTN:(a + 1) * TM_ATTN - CHUNK, :])
                carry = refs[a][0]
                if a == per_tile - 1:
                    carry = jnp.where(i < n_steps - 1, carry, 0.0)
                parts.append(tile_ref[(a + 1) * TM_ATTN - CHUNK:(a + 1) * TM_ATTN, :] + carry)
            dup = jnp.concatenate(parts, axis=0)
            return jnp.where(lo, _fold_halves(dup[:, 0:PAIR]), _fold_halves(dup[:, PAIR:2 * PAIR]))

        xh = xh_ref[...]
        h1 = (xh * g_ref[...] + b_ref[...]).astype(BF16)
        dq_t = dq_ref[...]
        dz_t = dz_ref[...]
        dkv = jnp.concatenate([heads_gradient(dkd_ref, carry_refs[:per_tile]),
                               heads_gradient(dvd_ref, carry_refs[per_tile:])], axis=1).astype(BF16)
        dh1 = ALPHA * dr2_ref[...]
        dh1 += _dot_nt(dq_t, win_ref[:, 0:B_WIDTH])
        dh1 += _dot_nt(dz_t, win_ref[:, B_WIDTH:2 * B_WIDTH])
        dh1 += _dot_nt(dkv, wkv_ref[...])
        acc_in[:, 0:B_WIDTH] += _dot_tn(h1, dq_t)
        acc_in[:, B_WIDTH:2 * B_WIDTH] += _dot_tn(h1, dz_t)
        acc_kv[...] += _dot_tn(h1, dkv)
        dg_ref[...] += jnp.sum(dh1 * xh, axis=0, keepdims=True)
        db_ref[...] += jnp.sum(dh1, axis=0, keepdims=True)
        dr1_ref[...] = _ln_bwd(dh1 * g_ref[...], xh, rstd_ref[...])

        @pl.when(i == n_steps - 1)
        def _():
            half_rows = D_MODEL // 2
            shard_cols = 2 * B_WIDTH // N_CHIPS
            for s in range(N_CHIPS):
                for c in range(2):
                    gwin_ref[2 * s + c] = acc_in[c * half_rows:(c + 1) * half_rows,
                                                 s * shard_cols:(s + 1) * shard_cols].astype(BF16)
            gwkv_ref[...] = acc_kv[...].astype(BF16)

    vec = jax.ShapeDtypeStruct((1, D_MODEL), F32)
    gwin_shape = (N_DEV, D_MODEL // 2, 2 * B_WIDTH // N_CHIPS)

    def carry_spec(a):
        return pl.BlockSpec((1, CHUNK, 2 * PAIR), lambda i: (jnp.minimum(per_tile * i + a + 1, n_carry - 1), 0, 0))

    carry_specs = [carry_spec(a) for a in range(per_tile)]
    return pl.pallas_call(
        body, name="layer_b_bwd_proj", grid=(n_steps,),
        in_specs=[_rows(tm, D_MODEL), _rows(tm, 1), _const(g1.shape), _const(b1.shape), _rows(tm, D_MODEL),
                  _rows(tm, B_WIDTH), _rows(tm, B_WIDTH), _rows(tm, 2 * PAIR), _rows(tm, 2 * PAIR)]
        + carry_specs + carry_specs + [_resident(w_in.shape), _resident(w_kv.shape)],
        out_specs=[_rows(tm, D_MODEL), _const((1, D_MODEL)), _const((1, D_MODEL)), _const(gwin_shape),
                   _const(w_kv.shape)],
        out_shape=[jax.ShapeDtypeStruct((t_len, D_MODEL), F32), vec, vec,
                   jax.ShapeDtypeStruct(gwin_shape, BF16), jax.ShapeDtypeStruct(w_kv.shape, BF16)],
        scratch_shapes=[pltpu.VMEM(w_in.shape, F32), pltpu.VMEM(w_kv.shape, F32)],
        compiler_params=_params(),
    )(xh1, rstd1, g1, b1, dr2, dq, dzb, dkd, dvd, *([carry_k] * per_tile), *([carry_v] * per_tile), w_in, w_kv)


def _layer_a_bwd_mix(dr1, u, vh, z, y, rv, w_out, lng, lnb, ws, bsp_t, after):
    t_len = u.shape[0]
    tm = TM_ATTN
    n_steps = t_len // tm

    def body(dr_ref, u_ref, vh_ref, z_ref, y_ref, rv_ref, wout_ref, lng_ref, lnb_ref, ws_ref, bsp_ref, after_ref,
             dp_ref, gw_ref, dws_ref, dbsp_ref, dgs_ref, dbs_ref, dvn_scr, gw_acc):
        i = pl.program_id(0)

        @pl.when(i == 0)
        def _():
            gw_acc[...] = jnp.zeros_like(gw_acc)
            dws_ref[...] = jnp.zeros_like(dws_ref)
            dbsp_ref[...] = jnp.zeros_like(dbsp_ref)
            dgs_ref[...] = jnp.zeros_like(dgs_ref)
            dbs_ref[...] = jnp.zeros_like(dbs_ref)

        drb = dr_ref[...].astype(BF16)

        def group_cols(g):
            return slice(g * A_GROUP_DIM, (g + 1) * A_GROUP_DIM)

        tri = (lax.broadcasted_iota(jnp.int32, (CHUNK, CHUNK), 0)
               >= lax.broadcasted_iota(jnp.int32, (CHUNK, CHUNK), 1))
        lane = lax.broadcasted_iota(jnp.int32, (CHUNK, CHUNK), 1)
        ones = jnp.ones((CHUNK, A_GROUP_DIM), BF16)
        dbsp = jnp.zeros((CHUNK, CHUNK), F32)
        dy_next = _dot_nt(drb, wout_ref[group_cols(0), :])
        for g in range(A_GROUPS):
            wsg = jnp.where(tri, ws_ref[g], 0.0).astype(BF16)
            cols = group_cols(g)
            cols_z = slice(2 * A_WIDTH + g * A_GROUP_DIM, 2 * A_WIDTH + (g + 1) * A_GROUP_DIM)
            dy_g = dy_next
            if g + 1 < A_GROUPS:
                dy_next = _dot_nt(drb, wout_ref[group_cols(g + 1), :])
            gw_acc[cols, :] += _dot_tn(y_ref[:, cols], drb)
            both = jnp.zeros((CHUNK, 2 * CHUNK), F32)
            for ci in range(tm // CHUNK):
                rows = slice(ci * CHUNK, (ci + 1) * CHUNK)
                vn = (vh_ref[rows, cols].astype(F32) * lng_ref[:, cols] + lnb_ref[:, cols]).astype(BF16)
                s = _dot(wsg, vn) + bsp_ref[:, g:g + 1]
                sz, dsz = _silu_parts(z_ref[rows, cols].astype(F32))
                dy = dy_g[rows]
                t = dy * u_ref[rows, cols].astype(F32)
                dp_ref[rows, cols] = (dy * (s * sz)).astype(BF16)
                dp_ref[rows, cols_z] = (t * s * dsz).astype(BF16)
                ds_b = (t * sz).astype(BF16)
                both += _dot_nt(ds_b, jnp.concatenate([vn, ones], axis=0))
                dvn_scr[rows, cols] = _dot_tn(wsg, ds_b)
            dws_ref[g] += jnp.where(tri, both[:, 0:CHUNK], 0.0)
            dbsp = jnp.where(lane == g, both[:, CHUNK:2 * CHUNK], dbsp)
        dbsp_ref[...] += dbsp
        dvn = dvn_scr[...]
        vh_t = vh_ref[...].astype(F32)
        dgs_ref[...] += jnp.sum(dvn * vh_t, axis=0, keepdims=True)
        dbs_ref[...] += jnp.sum(dvn, axis=0, keepdims=True)
        dp_ref[:, A_WIDTH:2 * A_WIDTH] = _ln_bwd(dvn * lng_ref[...], vh_t, rv_ref[...]).astype(BF16)

        @pl.when(i == n_steps - 1)
        def _():
            gw_ref[...] = gw_acc[...].astype(BF16)

    wide = jax.ShapeDtypeStruct((1, A_WIDTH), F32)
    return pl.pallas_call(
        body, name="layer_a_bwd_mix", grid=(n_steps,),
        in_specs=[_rows(tm, D_MODEL), _rows(tm, A_WIDTH), _rows(tm, A_WIDTH), _rows(tm, A_WIDTH), _rows(tm, A_WIDTH),
                  _rows(tm, 1), _resident(w_out.shape), _const(lng.shape), _const(lnb.shape), _const(ws.shape),
                  _const(bsp_t.shape), _const(after.shape)],
        out_specs=[_rows(tm, 3 * A_WIDTH), _const(w_out.shape), _const(ws.shape), _const((CHUNK, CHUNK)),
                   _const((1, A_WIDTH)), _const((1, A_WIDTH))],
        out_shape=[jax.ShapeDtypeStruct((t_len, 3 * A_WIDTH), BF16), jax.ShapeDtypeStruct(w_out.shape, BF16),
                   jax.ShapeDtypeStruct(ws.shape, F32), jax.ShapeDtypeStruct((CHUNK, CHUNK), F32),
                   wide, wide],
        scratch_shapes=[pltpu.VMEM((tm, A_WIDTH), F32), pltpu.VMEM(w_out.shape, F32)],
        compiler_params=_params(),
    )(dr1, u, vh, z, y, rv, w_out, lng, lnb, ws, bsp_t, after)


def _layer_a_bwd_dx(dr1, dp, w_in, after, updates=()):
    t_len = dr1.shape[0]
    tm = TM_MM
    n_steps = t_len // tm
    n_upd = len(updates)

    def body(dr_ref, dp_ref, win_ref, after_ref, *refs):
        upd_in, dx_ref, upd_out = refs[:4 * n_upd], refs[4 * n_upd], refs[4 * n_upd + 1:]
        dx_ref[...] = ALPHA * dr_ref[...] + _dot_nt(dp_ref[...], win_ref[...])
        for k in range(n_upd):
            w_ref, g_ref, m_ref, v_ref = upd_in[4 * k:4 * k + 4]
            g_out, d_ref, nm_ref, nv_ref = upd_out[4 * k:4 * k + 4]
            g_out[...] = g_ref[...]
            _adamw_update(w_ref, g_ref, m_ref, v_ref, d_ref, nm_ref, nv_ref)

    upd_specs, upd_shapes, upd_args = [], [], []
    for w, g, m, v in updates:
        rows, cols = w.shape
        upd_specs.append(pl.BlockSpec((rows // n_steps, cols), lambda i: (i, 0)))
        upd_shapes.append(jax.ShapeDtypeStruct((rows, cols), F32))
        upd_args += [w, g, m, v]
    return pl.pallas_call(
        body, name="layer_a_bwd_dx", grid=(n_steps,),
        in_specs=[_rows(tm, D_MODEL), _rows(tm, 3 * A_WIDTH), _resident(w_in.shape), _const(after.shape)]
        + [s for s in upd_specs for _ in range(4)],
        out_specs=[_rows(tm, D_MODEL)] + [s for s in upd_specs for _ in range(4)],
        out_shape=[jax.ShapeDtypeStruct((t_len, D_MODEL), F32)] + [s for s in upd_shapes for _ in range(4)],
        compiler_params=_params(),
    )(dr1, dp, w_in, after, *upd_args)


def _layer_a_bwd_win(xt, dp, after):
    t_len = xt.shape[1]
    tm = TM_WIN
    n_steps = t_len // tm
    shard_cols = 3 * A_WIDTH // N_CHIPS
    half_rows = D_MODEL // 2

    def body(xt_ref, dp_ref, after_ref, gw_ref, acc):
        i = pl.program_id(1)

        @pl.when(i == 0)
        def _():
            acc[...] = jnp.zeros_like(acc)

        acc[...] += _dot(xt_ref[...], dp_ref[...])

        @pl.when(i == n_steps - 1)
        def _():
            for c in range(2):
                gw_ref[0, c] = acc[c * half_rows:(c + 1) * half_rows, :].astype(BF16)

    return pl.pallas_call(
        body, name="layer_a_bwd_win", grid=(N_CHIPS, n_steps),
        in_specs=[pl.BlockSpec((D_MODEL, tm), lambda j, i: (0, i)),
                  pl.BlockSpec((tm, shard_cols), lambda j, i: (i, j)), _const(after.shape)],
        out_specs=pl.BlockSpec((1, 2, half_rows, shard_cols), lambda j, i: (j, 0, 0, 0)),
        out_shape=jax.ShapeDtypeStruct((N_CHIPS, 2, half_rows, shard_cols), BF16),
        scratch_shapes=[pltpu.VMEM((D_MODEL, shard_cols), F32)],
        compiler_params=_params(("arbitrary", "arbitrary")),
    )(xt, dp, after)


def _bucket_onehot():
    dist = jnp.arange(CHUNK, dtype=jnp.int32)[None, :]
    max_exact = REL_BUCKETS // 2
    df = jnp.maximum(dist, 1).astype(F32)
    large = max_exact + (jnp.log(df / max_exact) / math.log(CHUNK / max_exact)
                         * (REL_BUCKETS - max_exact)).astype(jnp.int32)
    bucket = jnp.where(dist < max_exact, dist, jnp.minimum(large, REL_BUCKETS - 1))
    onehot = bucket == jnp.arange(REL_BUCKETS, dtype=jnp.int32)[:, None]
    return onehot.astype(F32)


def _bias_expand(rel_t, onehot):
    def body(rel_ref, oh_ref, out_ref):
        by_distance = jnp.dot(rel_ref[...], oh_ref[...], preferred_element_type=F32,
                              precision=lax.Precision.HIGHEST)
        for h in range(N_HEADS):
            rows = jnp.broadcast_to(by_distance[h:h + 1, :], (2 * CHUNK, CHUNK))
            out_ref[h] = pltpu.roll(rows, 0, 1, stride=1, stride_axis=0)

    return pl.pallas_call(
        body, name="bias_expand",
        out_shape=jax.ShapeDtypeStruct((N_HEADS, 2 * CHUNK, CHUNK), F32),
    )(rel_t, onehot)


def _bias_reduce(oh_ref, db_ref):
    sublane = lax.broadcasted_iota(jnp.int32, (_SUBLANES, CHUNK), 0)
    rows = []
    for h in range(N_HEADS):
        part = db_ref[h, 0:_SUBLANES, :]
        for a in range(1, 2 * CHUNK // _SUBLANES):
            tile = db_ref[h, a * _SUBLANES:(a + 1) * _SUBLANES, :]
            back = (-a * _SUBLANES) % CHUNK
            part += pltpu.roll(tile, back, 1) if back else tile
        total = jnp.where(sublane == 0, part, 0.0)
        for s in range(1, _SUBLANES):
            total += jnp.where(sublane == s, pltpu.roll(part, CHUNK - s, 1), 0.0)
        rows.append(jnp.sum(total, axis=0, keepdims=True))
    by_distance = jnp.concatenate(rows, axis=0)
    return lax.dot_general(oh_ref[...], by_distance, (((1,), (1,)), ((), ())),
                           preferred_element_type=F32, precision=lax.Precision.HIGHEST)


_SMALL_SHAPES = dict(w_spatial=(A_GROUPS, CHUNK, CHUNK), b_spatial=(A_GROUPS, CHUNK), attn_sinks=(1, N_HEADS),
                     rel_bias=(REL_BUCKETS, N_HEADS), post_ln_g=(2, D_MODEL), post_ln_b=(2, D_MODEL),
                     sgu_ln_g=(1, A_WIDTH), sgu_ln_b=(1, A_WIDTH), loss=(1, 1))
_SMALL_ORDER = tuple(_SMALL_SHAPES)


def _small_rows(name):
    shape = _SMALL_SHAPES[name]
    rows = math.prod(shape[:-1]) if shape[-1] < _LANES else math.prod(shape) // _LANES
    return -(-rows // _SUBLANES) * _SUBLANES


def _small_offset(name):
    return sum(_small_rows(n) for n in _SMALL_ORDER[:_SMALL_ORDER.index(name)])


def _pack_small(dws, dbsp, dsink, dbias, onehot, post_g, post_b, dgs, dbs, loss_vec):
    def body(dws_ref, dbsp_ref, dsink_ref, db_ref, oh_ref, g1_ref, g2_ref, b1_ref, b2_ref, dgs_ref, dbs_ref,
             loss_ref, out_ref):
        out_ref[...] = jnp.zeros_like(out_ref)

        def put_flat(name, refs):
            row = _small_offset(name)
            for ref in refs:
                for k in range(ref.shape[1] // _LANES):
                    out_ref[row:row + 1, :] = ref[:, k * _LANES:(k + 1) * _LANES]
                    row += 1

        row = _small_offset("w_spatial")
        for g in range(A_GROUPS):
            out_ref[row + g * CHUNK:row + (g + 1) * CHUNK, :] = dws_ref[g]
        row = _small_offset("b_spatial")
        out_ref[row:row + A_GROUPS, :] = dbsp_ref[...].T[0:A_GROUPS, :]
        lane = lax.broadcasted_iota(jnp.int32, (1, _LANES), 1)
        sinks = jnp.zeros((1, _LANES), F32)
        for h in range(N_HEADS):
            per_query = dsink_ref[h // GROUP:h // GROUP + 1, (h % GROUP) * CHUNK:(h % GROUP + 1) * CHUNK]
            sinks = jnp.where(lane == h, jnp.sum(per_query, axis=1, keepdims=True), sinks)
        row = _small_offset("attn_sinks")
        out_ref[row:row + 1, :] = sinks
        row = _small_offset("rel_bias")
        out_ref[row:row + REL_BUCKETS, 0:N_HEADS] = _bias_reduce(oh_ref, db_ref)
        put_flat("post_ln_g", [g1_ref, g2_ref])
        put_flat("post_ln_b", [b1_ref, b2_ref])
        put_flat("sgu_ln_g", [dgs_ref])
        put_flat("sgu_ln_b", [dbs_ref])
        row = _small_offset("loss")
        out_ref[row:row + 1, 0:1] = (0.5 / D_MODEL) * jnp.sum(loss_ref[...], axis=1, keepdims=True)

    total_rows = sum(_small_rows(n) for n in _SMALL_ORDER)
    return pl.pallas_call(
        body, name="pack_small",
        out_shape=jax.ShapeDtypeStruct((total_rows, _LANES), F32),
    )(dws, dbsp, dsink, dbias, onehot, *post_g, *post_b, dgs, dbs, loss_vec)


def _place():
    return lax.axis_index("x"), lax.axis_index("y"), lax.axis_index("c")


RELAY_PIECES = 4


def _shard_window(full_ref, shard_shape, col_sharded, s, half, piece=None):
    rows, cols = shard_shape
    if half is None:
        start, size = 0, rows
    elif piece is None:
        start, size = half * (rows // 2), rows // 2
    else:
        size = rows // 2 // RELAY_PIECES
        start = (half * RELAY_PIECES + piece) * size
    if col_sharded:
        return full_ref.at[pl.ds(start, size), pl.ds(s * cols, cols)]
    return full_ref.at[pl.ds(s * rows + start, size), :]


def _other_chips(x, y):
    return [(1 - x, y), (x, 1 - y), (1 - x, 1 - y)]


def _gather_weights(shards, col_sharded, fetch, ln_shard):
    n_w = len(shards)
    fetched = [w for w in range(n_w) if fetch[w]]
    full_shapes = []
    for w, cs in zip(shards, col_sharded):
        r, c = w.shape
        full_shapes.append((r, c * N_CHIPS) if cs else (r * N_CHIPS, c))

    def body(*refs):
        in_refs = refs[:n_w]
        ln_ref = refs[n_w]
        full_refs = refs[n_w + 1:2 * n_w + 1]
        ln_full = refs[2 * n_w + 1]
        raw = refs[2 * n_w + 2:3 * n_w + 2]
        stage = refs[3 * n_w + 2:4 * n_w + 2]
        send_sems, recv_sems, load_sems, local_sems, ln_send, ln_recv = refs[4 * n_w + 2:]
        x, y, c = _place()
        s_me = 2 * x + y
        chips = _other_chips(x, y)
        pieces = range(RELAY_PIECES)

        def shard_window(w, s, half, piece=None):
            return _shard_window(full_refs[w], shards[w].shape, col_sharded[w], s, half, piece)

        def piece_rows(w, half, piece):
            rows = shards[w].shape[0] // 2 // RELAY_PIECES
            return pl.ds(pl.multiple_of((half * RELAY_PIECES + piece) * rows, rows), rows)

        def ici_copy(w, k, sender_shard, piece):
            idx = (w * 3 + k) * RELAY_PIECES + piece
            return pltpu.make_async_remote_copy(
                src_ref=stage[w].at[piece_rows(w, c, piece), :], dst_ref=shard_window(w, sender_shard, c, piece),
                send_sem=send_sems.at[idx], recv_sem=recv_sems.at[idx],
                device_id=(*chips[k], c), device_id_type=MESH)

        def d2d_copy(w, k, half, piece):
            s_k = 2 * chips[k][0] + chips[k][1]
            win = shard_window(w, s_k, half, piece)
            idx = (3 * n_w + w * 3 + k) * RELAY_PIECES + piece
            return pltpu.make_async_remote_copy(
                src_ref=win, dst_ref=win, send_sem=send_sems.at[idx], recv_sem=recv_sems.at[idx],
                device_id=(x, y, 1 - c), device_id_type=MESH)

        def ln_copy(k, slot):
            return pltpu.make_async_remote_copy(
                src_ref=ln_ref, dst_ref=ln_full.at[slot], send_sem=ln_send.at[k], recv_sem=ln_recv.at[k],
                device_id=(*chips[k], c), device_id_type=MESH)

        loads = []

        def load(w, rows):
            window = (rows, slice(None)) if rows is not None else (slice(None), slice(None))
            cp = pltpu.make_async_copy(in_refs[w].at[window], raw[w].at[window], load_sems.at[len(loads)])
            cp.start()
            loads.append((cp, w, window))

        for half in (c, 1 - c):
            for w in fetched:
                for q in pieces:
                    load(w, piece_rows(w, half, q))
        for w in range(n_w):
            if not fetch[w]:
                load(w, None)

        def to_bf16(k):
            cp, w, window = loads[k]
            cp.wait()
            stage[w][window] = raw[w][window].astype(BF16)

        ln_full[s_me] = ln_ref[...]
        def shard_of(k):
            return 2 * chips[k][0] + chips[k][1]

        relay_from = jnp.where(c == 0, shard_of(0), shard_of(1))
        relay_to = (jnp.where(c == 0, x, 1 - x), jnp.where(c == 0, 1 - y, y), c)

        def relay_copy(w, sender_shard, piece):
            win = shard_window(w, sender_shard, c, piece)
            idx = (w * 3 + 2) * RELAY_PIECES + piece
            return pltpu.make_async_remote_copy(
                src_ref=win, dst_ref=win, send_sem=send_sems.at[idx], recv_sem=recv_sems.at[idx],
                device_id=relay_to, device_id_type=MESH)

        first = [ln_copy(k, s_me) for k in range(3)]
        for cp in first:
            cp.start()
        n_sent = 0
        for w in fetched:
            for q in pieces:
                to_bf16(n_sent)
                n_sent += 1
                for k in range(2):
                    cp = ici_copy(w, k, s_me, q)
                    cp.start()
                    first.append(cp)
        for k in range(n_sent, len(loads)):
            to_bf16(k)
        own = [pltpu.make_async_copy(stage[w], shard_window(w, s_me, None), local_sems.at[w]) for w in range(n_w)]
        for cp in own:
            cp.start()
        passed = []
        for w in fetched:
            for q in pieces:
                for k in range(2):
                    ici_copy(w, k, shard_of(k), q).wait_recv()
                relay = relay_copy(w, relay_from, q)
                relay.start()
                passed.append(relay)
                for k in range(2):
                    fwd = d2d_copy(w, k, c, q)
                    fwd.start()
                    passed.append(fwd)
        for w in fetched:
            for q in pieces:
                relay_copy(w, shard_of(2), q).wait_recv()
                fwd = d2d_copy(w, 2, c, q)
                fwd.start()
                passed.append(fwd)
        for w in fetched:
            for k in range(3):
                for q in pieces:
                    d2d_copy(w, k, 1 - c, q).wait_recv()
        for k in range(3):
            ln_copy(k, 2 * chips[k][0] + chips[k][1]).wait_recv()
        for cp in first + passed:
            cp.wait_send()
        for cp in own:
            cp.wait()

    vmem = pl.BlockSpec(memory_space=pltpu.VMEM)
    hbm = pl.BlockSpec(memory_space=pl.ANY)
    return pl.pallas_call(
        body, name="gather_weights",
        in_specs=[hbm] * n_w + [vmem],
        out_specs=[hbm] * n_w + [vmem],
        out_shape=[jax.ShapeDtypeStruct(s, BF16) for s in full_shapes]
        + [jax.ShapeDtypeStruct((N_CHIPS,) + ln_shard.shape, F32)],
        scratch_shapes=[pltpu.VMEM(w.shape, F32) for w in shards] + [pltpu.VMEM(w.shape, BF16) for w in shards]
        + [pltpu.SemaphoreType.DMA((6 * RELAY_PIECES * n_w,)), pltpu.SemaphoreType.DMA((6 * RELAY_PIECES * n_w,)),
           pltpu.SemaphoreType.DMA((2 * RELAY_PIECES * len(fetched) + n_w - len(fetched),)),
           pltpu.SemaphoreType.DMA((n_w,)), pltpu.SemaphoreType.DMA((3,)), pltpu.SemaphoreType.DMA((3,))],
        compiler_params=pltpu.CompilerParams(vmem_limit_bytes=VMEM_LIMIT),
    )(*shards, ln_shard)


def _fetch_copy(full_ref, shard_shape, col_sharded, sender_shard, send_sems, recv_sems, idx, chip, c):
    win = _shard_window(full_ref, shard_shape, col_sharded, sender_shard, None)
    return pltpu.make_async_remote_copy(src_ref=win, dst_ref=win, send_sem=send_sems.at[idx],
                                        recv_sem=recv_sems.at[idx], device_id=(*chip, c), device_id_type=MESH)


def _fetch_start(fulls, shard_shapes, col_sharded):
    n = len(fulls)

    def body(*refs):
        full = refs[:n]
        send_sems, recv_sems = refs[n], refs[n + 1]
        token = refs[-1]
        x, y, c = _place()
        for w in range(n):
            for k, chip in enumerate(_other_chips(x, y)):
                _fetch_copy(full[w], shard_shapes[w], col_sharded[w], 2 * x + y, send_sems, recv_sems, w * 3 + k,
                            chip, c).start()
        token[...] = jnp.zeros_like(token)

    outs = pl.pallas_call(
        body, name="fetch_start",
        out_shape=(pltpu.SemaphoreType.DMA((3 * n,)), pltpu.SemaphoreType.DMA((3 * n,)),
                   *[pltpu.HBM(f.shape, f.dtype) for f in fulls], jax.ShapeDtypeStruct((8, 128), F32)),
        in_specs=[_HBM] * n,
        out_specs=(_SEM, _SEM, *([_HBM] * n), pl.BlockSpec(memory_space=pltpu.VMEM)),
        input_output_aliases={i: 2 + i for i in range(n)},
        compiler_params=pltpu.CompilerParams(has_side_effects=pltpu.SideEffectType.DATAFLOW_SIDE_EFFECTING),
    )(*[pltpu.with_memory_space_constraint(f, pltpu.HBM) for f in fulls])
    return dict(send=outs[0], recv=outs[1], full=list(outs[2:2 + n])), outs[-1]


def _fetch_wait(group, shard_shapes, col_sharded, after):
    n = len(group["full"])

    def body(*refs):
        full = refs[:n]
        send_sems, recv_sems = refs[n], refs[n + 1]
        x, y, c = _place()
        for w in range(n):
            for k, chip in enumerate(_other_chips(x, y)):
                _fetch_copy(full[w], shard_shapes[w], col_sharded[w], 2 * x + y, send_sems, recv_sems, w * 3 + k,
                            chip, c).wait_send()
                _fetch_copy(full[w], shard_shapes[w], col_sharded[w], 2 * chip[0] + chip[1], send_sems, recv_sems,
                            w * 3 + k, chip, c).wait_recv()

    outs = pl.pallas_call(
        body, name="fetch_wait", out_shape=tuple(pltpu.HBM(f.shape, f.dtype) for f in group["full"]),
        in_specs=[_HBM] * n + [_SEM, _SEM, pl.BlockSpec(memory_space=pl.ANY)],
        out_specs=tuple([_HBM] * n), input_output_aliases={i: i for i in range(n)},
        compiler_params=pltpu.CompilerParams(has_side_effects=pltpu.SideEffectType.DATAFLOW_SIDE_EFFECTING),
    )(*group["full"], group["send"], group["recv"], after)
    return list(outs)


_HBM = pl.BlockSpec(memory_space=pltpu.HBM)
_SEM = pl.BlockSpec(memory_space=pltpu.SEMAPHORE)
_N_PEER = N_DEV - 1


def _peer(x, y, c, k):
    return (x + (k >> 2)) % 2, (y + ((k >> 1) & 1)) % 2, (c + (k & 1)) % 2


def _exchange_copy(src_ref, land_ref, sliced, send_sems, recv_sems, idx, x, y, c, k):
    px, py, pc = _peer(x, y, c, k)
    src = src_ref.at[4 * px + 2 * py + pc] if sliced else src_ref
    return pltpu.make_async_remote_copy(
        src_ref=src, dst_ref=land_ref.at[4 * x + 2 * y + c],
        send_sem=send_sems.at[idx], recv_sem=recv_sems.at[idx], device_id=(px, py, pc), device_id_type=MESH)


def _exchange_start(tag, arrays, sliced):
    n = len(arrays)
    lands = [lax.empty(a.shape if s else (N_DEV,) + a.shape, a.dtype) for a, s in zip(arrays, sliced)]

    def body(*refs):
        src, land = refs[:n], refs[n:2 * n]
        send_sems, recv_sems = refs[2 * n], refs[2 * n + 1]
        token = refs[-1]
        x, y, c = _place()
        for w in range(n):
            for k in range(1, N_DEV):
                _exchange_copy(src[w], land[w], sliced[w], send_sems, recv_sems, w * _N_PEER + k - 1, x, y, c, k).start()
        token[...] = jnp.zeros_like(token)

    outs = pl.pallas_call(
        body, name="exchange_start_" + tag,
        out_shape=(pltpu.SemaphoreType.DMA((n * _N_PEER,)), pltpu.SemaphoreType.DMA((n * _N_PEER,)),
                   *[pltpu.HBM(a.shape, a.dtype) for a in arrays], *[pltpu.HBM(l.shape, l.dtype) for l in lands],
                   jax.ShapeDtypeStruct((8, 128), F32)),
        in_specs=[_HBM] * (2 * n),
        out_specs=(_SEM, _SEM, *([_HBM] * (2 * n)), pl.BlockSpec(memory_space=pltpu.VMEM)),
        input_output_aliases={i: 2 + i for i in range(2 * n)},
        compiler_params=pltpu.CompilerParams(has_side_effects=pltpu.SideEffectType.DATAFLOW_SIDE_EFFECTING),
    )(*[pltpu.with_memory_space_constraint(a, pltpu.HBM) for a in arrays],
      *[pltpu.with_memory_space_constraint(l, pltpu.HBM) for l in lands])
    return dict(send=outs[0], recv=outs[1], src=list(outs[2:2 + n]), land=list(outs[2 + n:2 + 2 * n]),
                sliced=list(sliced)), outs[-1]


def _exchange_wait(tag, groups, after):
    counts = [len(g["src"]) for g in groups]
    total = sum(counts)

    def body(*refs):
        pos = 0
        x, y, c = _place()
        for g, n in zip(groups, counts):
            src, land = refs[pos:pos + n], refs[pos + n:pos + 2 * n]
            send_sems, recv_sems = refs[pos + 2 * n], refs[pos + 2 * n + 1]
            pos += 2 * n + 2
            for w in range(n):
                for k in range(1, N_DEV):
                    cp = _exchange_copy(src[w], land[w], g["sliced"][w], send_sems, recv_sems,
                                        w * _N_PEER + k - 1, x, y, c, k)
                    cp.wait_send()
                    cp.wait_recv()

    operands, in_specs, aliases, out_shape = [], [], {}, []
    for g in groups:
        for a in g["src"] + g["land"]:
            aliases[len(operands)] = len(out_shape)
            out_shape.append(pltpu.HBM(a.shape, a.dtype))
            operands.append(a)
            in_specs.append(_HBM)
        operands += [g["send"], g["recv"]]
        in_specs += [_SEM, _SEM]
    operands.append(after)
    in_specs.append(pl.BlockSpec(memory_space=pl.ANY))
    outs = pl.pallas_call(
        body, name="exchange_wait_" + tag, out_shape=tuple(out_shape), in_specs=in_specs,
        out_specs=tuple([_HBM] * (2 * total)), input_output_aliases=aliases,
        compiler_params=pltpu.CompilerParams(has_side_effects=pltpu.SideEffectType.DATAFLOW_SIDE_EFFECTING),
    )(*operands)
    srcs, lands, pos = [], [], 0
    for n in counts:
        srcs += list(outs[pos:pos + n])
        lands += list(outs[pos + n:pos + 2 * n])
        pos += 2 * n
    return srcs, lands


def _sum_and_swap(tag, pieces, lands, small=None, small_land=None):
    n_w = len(pieces)
    n_small = 0 if small is None else 1

    def body(*refs):
        g_refs, land_refs = refs[:n_w], refs[n_w:2 * n_w]
        pos = 2 * n_w + 2 * n_small
        out_refs = refs[pos:pos + n_w]
        pos += n_w + n_small
        bufs = refs[pos:pos + n_w]
        load_sems, swap_send, swap_recv = refs[pos + n_w + 2 * n_small:]
        x, y, c = _place()
        me = 4 * x + 2 * y + c

        def slot(k):
            px, py, pc = _peer(x, y, c, k)
            return 4 * px + 2 * py + pc

        def swap_copy(w, half):
            rows = pieces[w].shape[1]
            win = out_refs[w].at[pl.ds(pl.multiple_of(half * rows, rows), rows), :]
            return pltpu.make_async_remote_copy(
                src_ref=win, dst_ref=win, send_sem=swap_send.at[w], recv_sem=swap_recv.at[w],
                device_id=(x, y, 1 - c), device_id_type=MESH)

        loads = []
        for w in range(n_w):
            per_w = [pltpu.make_async_copy(g_refs[w].at[me], bufs[w].at[me], load_sems.at[w * N_DEV])]
            per_w += [pltpu.make_async_copy(land_refs[w].at[slot(k)], bufs[w].at[slot(k)], load_sems.at[w * N_DEV + k])
                      for k in range(1, N_DEV)]
            loads.append(per_w)
        small_loads = []
        if n_small:
            small_ref, small_land_ref = refs[2 * n_w], refs[2 * n_w + 1]
            small_out = refs[2 * n_w + 2 + n_w]
            small_buf, small_sems = refs[pos + n_w], refs[pos + n_w + 1]
            small_loads = [pltpu.make_async_copy(small_land_ref.at[slot(k)], small_buf.at[slot(k)],
                                                 small_sems.at[k - 1]) for k in range(1, N_DEV)]
        for cp in [cp for per_w in loads for cp in per_w] + small_loads:
            cp.start()
        if n_small:
            small_buf[me] = small_ref[...]
        swaps = []
        for w in range(n_w):
            for cp in loads[w]:
                cp.wait()
            rows = pieces[w].shape[1]
            total = bufs[w][0].astype(F32)
            for p in range(1, N_DEV):
                total += bufs[w][p].astype(F32)
            out_refs[w][pl.ds(pl.multiple_of(c * rows, rows), rows), :] = total
            sw = swap_copy(w, c)
            sw.start()
            swaps.append(sw)
        if n_small:
            for cp in small_loads:
                cp.wait()
            total = small_buf[0]
            for p in range(1, N_DEV):
                total += small_buf[p]
            small_out[...] = total
        for w in range(n_w):
            swap_copy(w, 1 - c).wait_recv()
        for sw in swaps:
            sw.wait_send()

    vmem = pl.BlockSpec(memory_space=pltpu.VMEM)
    hbm = pl.BlockSpec(memory_space=pl.ANY)
    small_args = [small, small_land] if n_small else []
    small_shapes = [jax.ShapeDtypeStruct(small.shape, F32)] if n_small else []
    small_scratch = ([pltpu.VMEM((N_DEV,) + small.shape, F32), pltpu.SemaphoreType.DMA((_N_PEER,))]
                     if n_small else [])
    return pl.pallas_call(
        body, name="sum_and_swap_" + tag,
        in_specs=[hbm] * (2 * n_w) + [vmem, hbm] * n_small,
        out_specs=[vmem] * (n_w + n_small),
        out_shape=[jax.ShapeDtypeStruct((2 * p.shape[1], p.shape[2]), F32) for p in pieces] + small_shapes,
        scratch_shapes=[pltpu.VMEM(p.shape, BF16) for p in pieces] + small_scratch
        + [pltpu.SemaphoreType.DMA((n_w * N_DEV,)), pltpu.SemaphoreType.DMA((n_w,)),
           pltpu.SemaphoreType.DMA((n_w,))],
        compiler_params=pltpu.CompilerParams(vmem_limit_bytes=VMEM_LIMIT),
    )(*pieces, *lands, *small_args)


def _adamw_values(w, g_t, m, v):
    c1 = 1.0 - ADAM_B1 ** ADAM_STEP
    c2 = 1.0 - ADAM_B2 ** ADAM_STEP
    nm = ADAM_B1 * m + (1.0 - ADAM_B1) * g_t
    nv = ADAM_B2 * v + (1.0 - ADAM_B2) * (g_t * g_t)
    return -ADAM_LR * ((nm / c1) / (jnp.sqrt(nv / c2) + ADAM_EPS) + ADAM_WD * w), nm, nv


def _adamw_update(w_ref, g_ref, m_ref, v_ref, d_ref, nm_ref, nv_ref):
    d_ref[...], nm_ref[...], nv_ref[...] = _adamw_values(w_ref[...], g_ref[...], m_ref[...], v_ref[...])


def _adamw_small(packed, shard_index, names, weights, moments_m, moments_v):
    n = len(names)
    shapes = [weights[name].shape for name in names]
    flat = [a[name].reshape(-1, a[name].shape[-1]) for name in names for a in (weights, moments_m, moments_v)]

    def body(packed_ref, shard_ref, *refs):
        loss_row = _small_offset("loss")
        refs[-1][...] = packed_ref[loss_row:loss_row + 1, 0:1]
        for k, name in enumerate(names):
            w_ref, m_ref, v_ref = refs[3 * k:3 * k + 3]
            g_ref, d_ref, nm_ref, nv_ref = refs[3 * n + 4 * k:3 * n + 4 * k + 4]
            rows, cols = w_ref.shape
            first = _small_offset(name)
            if cols <= _LANES:
                blocks = [(slice(0, rows), packed_ref[first:first + rows, 0:cols])]
            else:
                per_row = cols // _LANES
                if cols < _SMALL_SHAPES[name][-1]:
                    first = first + shard_ref[0] * per_row
                blocks = [(slice(i, i + 1),
                           jnp.concatenate([packed_ref[pl.ds(first + i * per_row + j, 1), :] for j in range(per_row)],
                                           axis=1)) for i in range(rows)]
            for at, g_t in blocks:
                g_ref[at, :] = g_t
                d_ref[at, :], nm_ref[at, :], nv_ref[at, :] = _adamw_values(w_ref[at, :], g_t, m_ref[at, :],
                                                                           v_ref[at, :])

    vmem = pl.BlockSpec(memory_space=pltpu.VMEM)
    outs = pl.pallas_call(
        body, name="adamw_small",
        in_specs=[vmem, pl.BlockSpec(memory_space=pltpu.SMEM)] + [vmem] * (3 * n),
        out_shape=[jax.ShapeDtypeStruct(flat[3 * k].shape, F32) for k in range(n) for _ in range(4)]
        + [jax.ShapeDtypeStruct((1, 1), F32)],
    )(packed, shard_index.reshape(1).astype(jnp.int32), *flat)
    return [tuple(o.reshape(shapes[k]) for o in outs[4 * k:4 * k + 4]) for k in range(n)], outs[-1].reshape(())


def _adamw(label, w, g, m, v):
    shape = w.shape
    cols = shape[-1]
    rows = w.size // cols
    args = [a.reshape(rows, cols) for a in (w, g, m, v)]

    def body(w_ref, g_ref, m_ref, v_ref, g_out, d_ref, nm_ref, nv_ref):
        g_out[...] = g_ref[...]
        _adamw_update(w_ref, g_ref, m_ref, v_ref, d_ref, nm_ref, nv_ref)

    block_rows = 256 if rows % 256 == 0 and rows > 256 else rows
    spec = pl.BlockSpec((block_rows, cols), lambda i: (i, 0))
    outs = pl.pallas_call(
        body, name="adamw_" + label, grid=(rows // block_rows,),
        in_specs=[spec] * 4, out_specs=[spec] * 4,
        out_shape=[jax.ShapeDtypeStruct((rows, cols), F32)] * 4,
        compiler_params=_params(),
    )(*args)
    return [o.reshape(shape) for o in outs]


def _no_send(tag, arrays, sliced):
    return jnp.zeros((8, 128), F32)


def _local_step(x, tgt, w_in_a, later_weights, first_after, sgu_ln_g, sgu_ln_b, w_spatial, b_spatial,
                attn_sinks, rel_bias, post_ln_g, post_ln_b, send=_no_send):
    bsp_t = b_spatial.T
    g1, b1 = post_ln_g[0:1], post_ln_b[0:1]
    g2, b2 = post_ln_g[1:2], post_ln_b[1:2]
    onehot = _bucket_onehot()
    bias = _bias_expand(rel_bias.T, onehot)
    win = _window_tables()

    xt, u, vh, z, rv, y = _layer_a_fwd(x, w_in_a, sgu_ln_g, sgu_ln_b, w_spatial, bsp_t, first_after)
    w_out_a, w_kv, w_in_b, w_out_b = later_weights(y)
    xh1, rstd1, q, zb, kd, vd = _layer_b_proj(x, y, w_out_a, g1, b1, w_in_b, w_kv)
    o, probs, sink_probs, dr2, loss_vec, dg2, db2 = _layer_b_fwd(q, zb, kd, vd, bias, win, attn_sinks, xh1, g1, b1,
                                                                 w_out_b, g2, b2, tgt)
    dq, dzb, dkd, dvd, carry_k, carry_v, gw_out_b, dsink, dbias = _layer_b_bwd_attn(
        dr2, zb, o, q, kd, vd, probs, sink_probs, w_out_b)
    dr1, dg1, db1, gw_in_b, gw_kv = _layer_b_bwd_proj(xh1, rstd1, g1, b1, dr2, dq, dzb, dkd, dvd, carry_k, carry_v,
                                                      w_in_b, w_kv)
    gw_out_b = gw_out_b.reshape(N_DEV, -1, D_MODEL)
    gw_kv = gw_kv.reshape(N_DEV, -1, 2 * PAIR)
    after = send("b", [gw_out_b, gw_in_b, gw_kv], [True, True, True])
    dp, gw_out_a, dws, dbsp, dgs, dbs = _layer_a_bwd_mix(dr1, u, vh, z, y, rv, w_out_a, sgu_ln_g, sgu_ln_b,
                                                         w_spatial, bsp_t, after)
    gw_out_a = gw_out_a.reshape(N_DEV, -1, D_MODEL)
    small = _pack_small(dws, dbsp, dsink, dbias, onehot, (dg1, dg2), (db1, db2), dgs, dbs, loss_vec)
    after = send("a_out", [gw_out_a, small], [True, False])
    gw_in_a = _layer_a_bwd_win(xt, dp, after).reshape(N_DEV, D_MODEL // 2, -1)
    after = send("a_in", [gw_in_a], [True])
    after, updates = after if isinstance(after, tuple) else (after, ())
    grad_x, *updated = _layer_a_bwd_dx(dr1, dp, w_in_a, after, updates)

    pieces = [gw_in_a, gw_out_a, gw_kv, gw_in_b, gw_out_b]
    return grad_x, pieces, small, updated


def kernel(x, w_in_a, sgu_ln_g, sgu_ln_b, w_spatial, b_spatial, w_out_a, w_kv, w_in_b, attn_sinks, rel_bias, w_out_b, post_ln_g, post_ln_b, loss_target, m_w_in_a, m_sgu_ln_g, m_sgu_ln_b, m_w_spatial, m_b_spatial, m_w_out_a, m_w_kv, m_w_in_b, m_attn_sinks, m_rel_bias, m_w_out_b, m_post_ln_g, m_post_ln_b, v_w_in_a, v_sgu_ln_g, v_sgu_ln_b, v_w_spatial, v_b_spatial, v_w_out_a, v_w_kv, v_w_in_b, v_attn_sinks, v_rel_bias, v_w_out_b, v_post_ln_g, v_post_ln_b):
    weights = dict(w_in_a=w_in_a, sgu_ln_g=sgu_ln_g, sgu_ln_b=sgu_ln_b, w_spatial=w_spatial, b_spatial=b_spatial,
                   w_out_a=w_out_a, w_kv=w_kv, w_in_b=w_in_b, attn_sinks=attn_sinks, rel_bias=rel_bias,
                   w_out_b=w_out_b, post_ln_g=post_ln_g, post_ln_b=post_ln_b)
    moments_m = dict(w_in_a=m_w_in_a, sgu_ln_g=m_sgu_ln_g, sgu_ln_b=m_sgu_ln_b, w_spatial=m_w_spatial,
                     b_spatial=m_b_spatial, w_out_a=m_w_out_a, w_kv=m_w_kv, w_in_b=m_w_in_b,
                     attn_sinks=m_attn_sinks, rel_bias=m_rel_bias, w_out_b=m_w_out_b, post_ln_g=m_post_ln_g,
                     post_ln_b=m_post_ln_b)
    moments_v = dict(w_in_a=v_w_in_a, sgu_ln_g=v_sgu_ln_g, sgu_ln_b=v_sgu_ln_b, w_spatial=v_w_spatial,
                     b_spatial=v_b_spatial, w_out_a=v_w_out_a, w_kv=v_w_kv, w_in_b=v_w_in_b,
                     attn_sinks=v_attn_sinks, rel_bias=v_rel_bias, w_out_b=v_w_out_b, post_ln_g=v_post_ln_g,
                     post_ln_b=v_post_ln_b)
    order = ("w_in_a", "sgu_ln_g", "sgu_ln_b", "w_spatial", "b_spatial", "w_out_a", "w_kv", "w_in_b", "attn_sinks",
             "rel_bias", "w_out_b", "post_ln_g", "post_ln_b")

    shard_index = 2 * lax.axis_index("x") + lax.axis_index("y")
    ln_shard = jnp.concatenate([sgu_ln_g, sgu_ln_b], axis=0)
    shards = [w_in_a[0], w_out_a[0], w_kv, w_in_b[0], w_out_b[0]]
    col_sharded = [True, False, False, True, False]
    full_in_a, *later, ln_full = _gather_weights(shards, col_sharded, [True, False, False, False, False], ln_shard)
    ln_full = jnp.transpose(ln_full, (1, 0, 2)).reshape(2, A_WIDTH)
    later_shapes = [s.shape for s in shards[1:]]
    fetch_group, fetch_token = _fetch_start(later, later_shapes, col_sharded[1:])

    def later_weights(y):
        return _fetch_wait(fetch_group, later_shapes, col_sharded[1:], y)

    groups, grads, deltas, new_m, new_v, scalars = {}, {}, {}, {}, {}, {}
    early = ("w_out_b", "w_in_b", "w_kv", "w_out_a")

    def two_dim(a):
        return a.reshape(-1, a.shape[-1])

    def send(tag, arrays, sliced):
        groups[tag], token = _exchange_start(tag, arrays, sliced)
        if tag != "a_in":
            return token
        srcs, lands = _exchange_wait("early", [groups["b"], groups["a_out"]], token)
        *reduced, packed_sum = _sum_and_swap("early", srcs[:4], lands[:4], srcs[4], lands[4])
        updates = [(two_dim(weights[n]), g, two_dim(moments_m[n]), two_dim(moments_v[n]))
                   for n, g in zip(early, reduced)]
        small_names = ("sgu_ln_g", "sgu_ln_b", "w_spatial", "b_spatial", "attn_sinks", "rel_bias", "post_ln_g",
                       "post_ln_b")
        small_updates, scalars["loss"] = _adamw_small(packed_sum, shard_index, small_names, weights, moments_m,
                                                      moments_v)
        for name, (g, d, nm, nv) in zip(small_names, small_updates):
            grads[name], deltas[name], new_m[name], new_v[name] = g, d, nm, nv
        return new_m["b_spatial"].reshape(A_GROUPS, CHUNK), updates

    grad_x, _, _, updated = _local_step(
        x[0], loss_target[0], full_in_a, later_weights, fetch_token, ln_full[0:1], ln_full[1:2], w_spatial[0],
        b_spatial[0], attn_sinks, rel_bias, post_ln_g, post_ln_b, send=send)
    for k, name in enumerate(early):
        grads[name], deltas[name], new_m[name], new_v[name] = [
            a.reshape(weights[name].shape) for a in updated[4 * k:4 * k + 4]]

    srcs, lands = _exchange_wait("late", [groups["a_in"]], grad_x)
    (g_in_a,) = _sum_and_swap("late", srcs, lands)
    grads["w_in_a"], deltas["w_in_a"], new_m["w_in_a"], new_v["w_in_a"] = _adamw(
        "w_in_a", w_in_a, g_in_a.reshape(w_in_a.shape), m_w_in_a, v_w_in_a)
    return (scalars["loss"], grad_x[None], *[grads[n] for n in order], *[deltas[n] for n in order],
            *[new_m[n] for n in order], *[new_v[n] for n in order])
```

```python
import math

import jax
import jax.numpy as jnp
from jax import lax
from jax.experimental import pallas as pl
from jax.experimental.pallas import tpu as pltpu

F32 = jnp.float32
BF16 = jnp.bfloat16

D_MODEL = 1024
A_WIDTH = 2048
A_GROUPS = 8
A_GROUP_DIM = 256
CHUNK = 128
N_HEADS = 16
N_KV = 2
HEAD_DIM = 64
PAIR = 2 * HEAD_DIM
B_WIDTH = 1024
REL_BUCKETS = 32
ALPHA = 4.0 ** 0.25
LN_EPS = 1e-5
NEG_INF = -1e30
SCALE = HEAD_DIM ** -0.5

ADAM_LR = 0.001
ADAM_B1 = 0.9
ADAM_B2 = 0.999
ADAM_EPS = 1e-08
ADAM_WD = 0.01
ADAM_STEP = 10

N_DEV = 8
N_CHIPS = 4
MESH = pl.DeviceIdType.MESH
VMEM_LIMIT = 56 * 1024 * 1024

TM_ATTN = 256
TM_MM = 512
TM_WIN = 1024
_LANES = 128
_SUBLANES = 8


def _dot(a, b):
    return jnp.dot(a, b, preferred_element_type=F32)


def _dot_nt(a, b):
    return lax.dot_general(a, b, (((1,), (1,)), ((), ())), preferred_element_type=F32)


def _dot_tn(a, b):
    return lax.dot_general(a, b, (((0,), (0,)), ((), ())), preferred_element_type=F32)


def _ln_fwd(r):
    mu = jnp.mean(r, axis=-1, keepdims=True)
    rc = r - mu
    var = jnp.mean(rc * rc, axis=-1, keepdims=True)
    rstd = lax.rsqrt(var + LN_EPS)
    return rc * rstd, rstd


def _ln_bwd(dxh, xh, rstd):
    m1 = jnp.mean(dxh, axis=-1, keepdims=True)
    m2 = jnp.mean(dxh * xh, axis=-1, keepdims=True)
    return rstd * (dxh - m1 - xh * m2)


def _silu_parts(z):
    sg = jax.nn.sigmoid(z)
    return z * sg, sg * (1.0 + z * (1.0 - sg))


def _dup_halves(blk):
    sw = pltpu.roll(blk, HEAD_DIM, 1)
    lo = lax.broadcasted_iota(jnp.int32, blk.shape, 1) < HEAD_DIM
    return jnp.where(lo, blk, sw), jnp.where(lo, sw, blk)


def _fold_halves(blk):
    return blk + pltpu.roll(blk, HEAD_DIM, 1)


def _resident(shape):
    nd = len(shape)
    return pl.BlockSpec(shape, lambda *_: (0,) * nd, pipeline_mode=pl.Buffered(1))


def _const(shape):
    nd = len(shape)
    return pl.BlockSpec(shape, lambda *_: (0,) * nd)


def _rows(tm, cols):
    return pl.BlockSpec((tm, cols), lambda i: (i, 0))


def _params(sem=("arbitrary",)):
    return pltpu.CompilerParams(dimension_semantics=sem, vmem_limit_bytes=VMEM_LIMIT)


def _spatial_mix(ws_ref, bsp_ref, vn, s_scr, n_chunks):
    tri = (lax.broadcasted_iota(jnp.int32, (CHUNK, CHUNK), 0)
           >= lax.broadcasted_iota(jnp.int32, (CHUNK, CHUNK), 1))
    for g in range(A_GROUPS):
        wsg = jnp.where(tri, ws_ref[g], 0.0).astype(BF16)
        cols = slice(g * A_GROUP_DIM, (g + 1) * A_GROUP_DIM)
        for ci in range(n_chunks):
            rows = slice(ci * CHUNK, (ci + 1) * CHUNK)
            s_scr[rows, cols] = _dot(wsg, vn[rows, cols]) + bsp_ref[:, g:g + 1]


def _layer_a_fwd(x, w_in, lng, lnb, ws, bsp_t, after):
    t_len = x.shape[0]
    tm = TM_ATTN

    def body(x_ref, win_ref, lng_ref, lnb_ref, ws_ref, bsp_ref, after_ref,
             xt_ref, u_ref, vh_ref, z_ref, rv_ref, y_ref, s_scr):
        x_t = x_ref[...]
        xb = x_t.astype(BF16)
        xt_ref[...] = x_t.T.astype(BF16)
        u = _dot(xb, win_ref[:, 0:A_WIDTH])
        v = _dot(xb, win_ref[:, A_WIDTH:2 * A_WIDTH])
        z = _dot(xb, win_ref[:, 2 * A_WIDTH:3 * A_WIDTH])
        vh, rv = _ln_fwd(v)
        vn = (vh * lng_ref[...] + lnb_ref[...]).astype(BF16)
        _spatial_mix(ws_ref, bsp_ref, vn, s_scr, tm // CHUNK)
        sz, _ = _silu_parts(z)
        y_ref[...] = (u * s_scr[...] * sz).astype(BF16)
        u_ref[...] = u.astype(BF16)
        vh_ref[...] = vh.astype(BF16)
        z_ref[...] = z.astype(BF16)
        rv_ref[...] = rv

    wide = jax.ShapeDtypeStruct((t_len, A_WIDTH), BF16)
    return pl.pallas_call(
        body, name="layer_a_fwd", grid=(t_len // tm,),
        in_specs=[_rows(tm, D_MODEL), _resident(w_in.shape), _const(lng.shape), _const(lnb.shape), _const(ws.shape),
                  _const(bsp_t.shape), _const(after.shape)],
        out_specs=[pl.BlockSpec((D_MODEL, tm), lambda i: (0, i)), _rows(tm, A_WIDTH), _rows(tm, A_WIDTH),
                   _rows(tm, A_WIDTH), _rows(tm, 1), _rows(tm, A_WIDTH)],
        out_shape=[jax.ShapeDtypeStruct((D_MODEL, t_len), BF16), wide, wide, wide,
                   jax.ShapeDtypeStruct((t_len, 1), F32), wide],
        scratch_shapes=[pltpu.VMEM((tm, A_WIDTH), F32)],
        compiler_params=_params(),
    )(x, w_in, lng, lnb, ws, bsp_t, after)


def _layer_b_proj(x, y, w_out_a, g1, b1, w_in, w_kv):
    t_len = x.shape[0]
    tm = 2 * TM_MM

    def body(x_ref, y_ref, wout_ref, g_ref, b_ref, win_ref, wkv_ref, xh_ref, r1_ref, q_ref, z_ref, kd_ref, vd_ref):
        halves = [slice(k * TM_MM, (k + 1) * TM_MM) for k in range(2)]
        projected = [_dot(y_ref[rows, :], wout_ref[...]) for rows in halves]
        for rows, out_a in zip(halves, projected):
            xh, r1 = _ln_fwd(ALPHA * x_ref[rows, :] + out_a)
            xh_ref[rows, :] = xh
            r1_ref[rows, :] = r1
            h1 = (xh * g_ref[...] + b_ref[...]).astype(BF16)
            q_ref[rows, :] = (_dot(h1, win_ref[:, 0:B_WIDTH]) * SCALE).astype(BF16)
            z_ref[rows, :] = _dot(h1, win_ref[:, B_WIDTH:2 * B_WIDTH]).astype(BF16)
            kv = _dot(h1, wkv_ref[...])
            k0, k1 = _dup_halves(kv[:, 0:PAIR])
            v0, v1 = _dup_halves(kv[:, PAIR:2 * PAIR])
            kd_ref[rows, 0:PAIR] = k0.astype(BF16)
            kd_ref[rows, PAIR:2 * PAIR] = k1.astype(BF16)
            vd_ref[rows, 0:PAIR] = v0.astype(BF16)
            vd_ref[rows, PAIR:2 * PAIR] = v1.astype(BF16)

    return pl.pallas_call(
        body, name="layer_b_proj", grid=(t_len // tm,),
        in_specs=[_rows(tm, D_MODEL), _rows(tm, A_WIDTH), _resident(w_out_a.shape), _const(g1.shape),
                  _const(b1.shape), _resident(w_in.shape), _resident(w_kv.shape)],
        out_specs=[_rows(tm, D_MODEL), _rows(tm, 1), _rows(tm, B_WIDTH), _rows(tm, B_WIDTH), _rows(tm, 2 * PAIR),
                   _rows(tm, 2 * PAIR)],
        out_shape=[jax.ShapeDtypeStruct((t_len, D_MODEL), F32), jax.ShapeDtypeStruct((t_len, 1), F32),
                   jax.ShapeDtypeStruct((t_len, B_WIDTH), BF16), jax.ShapeDtypeStruct((t_len, B_WIDTH), BF16),
                   jax.ShapeDtypeStruct((t_len, 2 * PAIR), BF16), jax.ShapeDtypeStruct((t_len, 2 * PAIR), BF16)],
        compiler_params=_params(),
    )(x, y, w_out_a, g1, b1, w_in, w_kv)


GROUP = N_HEADS // N_KV
GROUP_Q = GROUP * CHUNK


def _window_tables():
    j = jnp.arange(2 * CHUNK, dtype=jnp.int32)[:, None]
    t = jnp.arange(CHUNK, dtype=jnp.int32)[None, :]
    dist = t + CHUNK - j
    inside = (dist >= 0) & (dist < CHUNK)
    return jnp.stack([inside & (j >= CHUNK), inside]).astype(F32)


def _band(ref, chunk_index, kvh):
    prev0 = pl.multiple_of(jnp.maximum(chunk_index - 1, 0) * CHUNK, CHUNK)
    cur0 = pl.multiple_of(chunk_index * CHUNK, CHUNK)
    cols = slice(kvh * PAIR, (kvh + 1) * PAIR)
    return jnp.concatenate([ref[pl.ds(prev0, CHUNK), cols], ref[pl.ds(cur0, CHUNK), cols]], axis=0)


def _group_tables(bias_ref, win_ref, sink_ref, chunk_index, kvh):
    bias = jnp.concatenate([bias_ref[kvh * GROUP + j] for j in range(GROUP)], axis=1)
    win = win_ref[jnp.minimum(chunk_index, 1)]
    mask = jnp.concatenate([win] * GROUP, axis=1) > 0.5
    sink = jnp.concatenate([jnp.full((1, CHUNK), sink_ref[0, kvh * GROUP + j], F32) for j in range(GROUP)], axis=1)
    return bias, mask, sink


def _attn_probs(qs, kband, bias, mask, sink):
    logits = jnp.where(mask, _dot_nt(kband, qs) + bias, NEG_INF)
    m = jnp.maximum(jnp.max(logits, axis=0, keepdims=True), sink)
    e = jnp.exp(logits - m)
    es = jnp.exp(sink - m)
    inv = 1.0 / (jnp.sum(e, axis=0, keepdims=True) + es)
    return e * inv, es * inv


def _half_mask():
    return lax.broadcasted_iota(jnp.int32, (CHUNK, PAIR), 1) < HEAD_DIM


def _stack_heads(src_ref, rows, kvh, dst_scr, lo):
    for j in range(GROUP):
        h = kvh * GROUP + j
        blk = src_ref[rows, (h // 2) * PAIR:(h // 2 + 1) * PAIR].astype(F32)
        keep = lo if h % 2 == 0 else ~lo
        dst_scr[j * CHUNK:(j + 1) * CHUNK, :] = jnp.where(keep, blk, 0.0).astype(BF16)


def _probs_spec(tm):
    return pl.BlockSpec((tm // CHUNK, N_KV, 2 * CHUNK, GROUP_Q), lambda i: (i, 0, 0, 0))


def _sink_probs_spec(tiles=1):
    return pl.BlockSpec((tiles, 8, GROUP_Q), lambda i: (i, 0, 0))


def _unstack_pairs(stacked, pp, lo):
    return jnp.where(lo, stacked[(2 * pp) * CHUNK:(2 * pp + 1) * CHUNK], stacked[(2 * pp + 1) * CHUNK:(2 * pp + 2) * CHUNK])


def _layer_b_fwd(q, zb, kd, vd, bias, win, sinks, xh1, g1, b1, w_out, g2, b2, tgt):
    t_len = q.shape[0]
    tm = 2 * TM_ATTN

    def body(q_ref, z_ref, kd_ref, vd_ref, bias_ref, win_ref, sink_ref, xh_ref, g1_ref, b1_ref, wout_ref, g2_ref,
             b2_ref, tgt_ref, o_ref, p_ref, ps_ref, dr_ref, loss_ref, dg_ref, db_ref, o_scr, qs_scr):
        i = pl.program_id(0)

        @pl.when(i == 0)
        def _():
            loss_ref[...] = jnp.zeros_like(loss_ref)
            dg_ref[...] = jnp.zeros_like(dg_ref)
            db_ref[...] = jnp.zeros_like(db_ref)

        lo = _half_mask()
        ps_ref[...] = jnp.zeros_like(ps_ref)
        per_part = TM_ATTN // CHUNK
        for part in range(tm // TM_ATTN):
            part_rows = slice(part * TM_ATTN, (part + 1) * TM_ATTN)
            for cp in range(per_part):
                ci = part * per_part + cp
                cg = i * (tm // CHUNK) + ci
                rows = slice(ci * CHUNK, (ci + 1) * CHUNK)
                for kvh in range(N_KV):
                    kband = _band(kd_ref, cg, kvh)
                    vband = _band(vd_ref, cg, kvh)
                    bias_g, mask, sink = _group_tables(bias_ref, win_ref, sink_ref, cg, kvh)
                    _stack_heads(q_ref, rows, kvh, qs_scr, lo)
                    p, p_sink = _attn_probs(qs_scr[...], kband, bias_g, mask, sink)
                    p = p.astype(BF16)
                    p_ref[ci, kvh] = p
                    ps_ref[part, cp * N_KV + kvh:cp * N_KV + kvh + 1, :] = p_sink
                    o_stack = _dot_tn(p, vband)
                    for pp in range(GROUP // 2):
                        pair = kvh * (GROUP // 2) + pp
                        o_scr[rows, pair * PAIR:(pair + 1) * PAIR] = _unstack_pairs(o_stack, pp, lo)
            o = o_scr[part_rows, :]
            o_ref[part_rows, :] = o.astype(BF16)
            sz, _ = _silu_parts(z_ref[part_rows, :].astype(F32))
            y = (o * sz).astype(BF16)
            h1 = xh_ref[part_rows, :] * g1_ref[...] + b1_ref[...]
            r = ALPHA * h1 + _dot(y, wout_ref[...])
            xh2, rstd2 = _ln_fwd(r)
            diff = xh2 * g2_ref[...] + b2_ref[...] - tgt_ref[part_rows, :]
            loss_ref[...] += jnp.sum(diff * diff, axis=0, keepdims=True)
            dh2 = diff * (1.0 / D_MODEL)
            dg_ref[...] += jnp.sum(dh2 * xh2, axis=0, keepdims=True)
            db_ref[...] += jnp.sum(dh2, axis=0, keepdims=True)
            dr_ref[part_rows, :] = _ln_bwd(dh2 * g2_ref[...], xh2, rstd2)

    vec = jax.ShapeDtypeStruct((1, D_MODEL), F32)
    return pl.pallas_call(
        body, name="layer_b_fwd", grid=(t_len // tm,),
        in_specs=[_rows(tm, B_WIDTH), _rows(tm, B_WIDTH), _resident(kd.shape), _resident(vd.shape),
                  _resident(bias.shape), _resident(win.shape), pl.BlockSpec(memory_space=pltpu.SMEM),
                  _rows(tm, D_MODEL), _const(g1.shape), _const(b1.shape), _resident(w_out.shape), _const(g2.shape),
                  _const(b2.shape), _rows(tm, D_MODEL)],
        out_specs=[_rows(tm, B_WIDTH), _probs_spec(tm), _sink_probs_spec(tm // TM_ATTN), _rows(tm, D_MODEL)]
        + [_const((1, D_MODEL))] * 3,
        out_shape=[jax.ShapeDtypeStruct((t_len, B_WIDTH), BF16),
                   jax.ShapeDtypeStruct((t_len // CHUNK, N_KV, 2 * CHUNK, GROUP_Q), BF16),
                   jax.ShapeDtypeStruct((t_len // TM_ATTN, 8, GROUP_Q), F32),
                   jax.ShapeDtypeStruct((t_len, D_MODEL), F32), vec, vec, vec],
        scratch_shapes=[pltpu.VMEM((tm, B_WIDTH), F32), pltpu.VMEM((GROUP_Q, PAIR), BF16)],
        compiler_params=_params(),
    )(q, zb, kd, vd, bias, win, sinks, xh1, g1, b1, w_out, g2, b2, tgt)


def _layer_b_bwd_attn(dr2, zb, o, q, kd, vd, probs, sink_probs, w_out):
    t_len = q.shape[0]
    tm = TM_ATTN
    n_steps = t_len // tm
    n_chunks = tm // CHUNK

    def body(dr_ref, z_ref, o_ref, q_ref, kd_ref, vd_ref, p_ref, ps_ref, wout_ref,
             dq_ref, dz_ref, dkd_ref, dvd_ref, ck_ref, cv_ref, gw_ref, dsink_ref, dbias_ref,
             do_scr, qs_scr, dos_scr, gw_acc):
        i = pl.program_id(0)

        @pl.when(i == 0)
        def _():
            gw_acc[...] = jnp.zeros_like(gw_acc)
            dsink_ref[...] = jnp.zeros_like(dsink_ref)
            dbias_ref[...] = jnp.zeros_like(dbias_ref)

        drb = dr_ref[...].astype(BF16)
        n_blocks = N_KV * n_chunks
        block_cols = B_WIDTH // n_blocks

        def through_gate(b):
            cols = slice(b * block_cols, (b + 1) * block_cols)
            dy = _dot_nt(drb, wout_ref[cols, :])
            sz, dsz = _silu_parts(z_ref[:, cols].astype(F32))
            o_t = o_ref[:, cols].astype(F32)
            dz_ref[:, cols] = (dy * o_t * dsz).astype(BF16)
            do_scr[:, cols] = (dy * sz).astype(BF16)
            return (o_t * sz).astype(BF16)

        def weight_gradient(b, gated):
            cols = slice(b * block_cols, (b + 1) * block_cols)
            gw_acc[cols, :] += _dot_tn(gated, drb)

        gated = {b: through_gate(b) for b in range(n_chunks)}

        lo = _half_mask()
        for kvh in range(N_KV):
            kcols = slice(kvh * PAIR, (kvh + 1) * PAIR)
            dk_bands, dv_bands = [], []
            for ci in range(n_chunks):
                unit = kvh * n_chunks + ci
                if unit + n_chunks < n_blocks:
                    gated[unit + n_chunks] = through_gate(unit + n_chunks)
                weight_gradient(unit, gated[unit])
                cg = i * n_chunks + ci
                rows = slice(ci * CHUNK, (ci + 1) * CHUNK)
                kband = _band(kd_ref, cg, kvh)
                vband = _band(vd_ref, cg, kvh)
                _stack_heads(q_ref, rows, kvh, qs_scr, lo)
                _stack_heads(do_scr, rows, kvh, dos_scr, lo)
                qs = qs_scr[...]
                dos = dos_scr[...]
                pb = p_ref[ci, kvh]
                p = pb.astype(F32)
                p_sink = ps_ref[0, ci * N_KV + kvh:ci * N_KV + kvh + 1, :]
                dp = _dot_nt(vband, dos)
                delta = jnp.sum(p * dp, axis=0, keepdims=True)
                dlog = p * (dp - delta)
                for j in range(GROUP):
                    dbias_ref[kvh * GROUP + j] += dlog[:, j * CHUNK:(j + 1) * CHUNK]
                dsink_ref[kvh:kvh + 1, :] += -(p_sink * delta)
                ds = dlog.astype(BF16)
                dq_stack = _dot_tn(ds, kband) * SCALE
                for pp in range(GROUP // 2):
                    pair = kvh * (GROUP // 2) + pp
                    dq_ref[rows, pair * PAIR:(pair + 1) * PAIR] = _unstack_pairs(dq_stack, pp, lo).astype(BF16)
                dk_bands.append(_dot(ds, qs))
                dv_bands.append(_dot(pb, dos))
            for bands, out_ref, carry_ref in ((dk_bands, dkd_ref, ck_ref), (dv_bands, dvd_ref, cv_ref)):
                carry_ref[0, :, kcols] = bands[0][0:CHUNK]
                for ci in range(n_chunks):
                    own = bands[ci][CHUNK:2 * CHUNK]
                    if ci + 1 < n_chunks:
                        own = own + bands[ci + 1][0:CHUNK]
                    out_ref[ci * CHUNK:(ci + 1) * CHUNK, kcols] = own

        @pl.when(i == n_steps - 1)
        def _():
            gw_ref[...] = gw_acc[...].astype(BF16)

    carry_spec = pl.BlockSpec((1, CHUNK, 2 * PAIR), lambda i: (i, 0, 0))
    carry_shape = jax.ShapeDtypeStruct((n_steps, CHUNK, 2 * PAIR), F32)
    bias_shape = (N_HEADS, 2 * CHUNK, CHUNK)
    return pl.pallas_call(
        body, name="layer_b_bwd_attn", grid=(n_steps,),
        in_specs=[_rows(tm, D_MODEL), _rows(tm, B_WIDTH), _rows(tm, B_WIDTH), _rows(tm, B_WIDTH),
                  _resident(kd.shape), _resident(vd.shape), _probs_spec(tm), _sink_probs_spec(),
                  _resident(w_out.shape)],
        out_specs=[_rows(tm, B_WIDTH), _rows(tm, B_WIDTH), _rows(tm, 2 * PAIR), _rows(tm, 2 * PAIR),
                   carry_spec, carry_spec, _const(w_out.shape), _const((N_KV, GROUP_Q)), _const(bias_shape)],
        out_shape=[jax.ShapeDtypeStruct((t_len, B_WIDTH), BF16), jax.ShapeDtypeStruct((t_len, B_WIDTH), BF16),
                   jax.ShapeDtypeStruct((t_len, 2 * PAIR), F32), jax.ShapeDtypeStruct((t_len, 2 * PAIR), F32),
                   carry_shape, carry_shape, jax.ShapeDtypeStruct(w_out.shape, BF16),
                   jax.ShapeDtypeStruct((N_KV, GROUP_Q), F32), jax.ShapeDtypeStruct(bias_shape, F32)],
        scratch_shapes=[pltpu.VMEM((tm, B_WIDTH), BF16), pltpu.VMEM((GROUP_Q, PAIR), BF16),
                        pltpu.VMEM((GROUP_Q, PAIR), BF16), pltpu.VMEM(w_out.shape, F32)],
        compiler_params=_params(),
    )(dr2, zb, o, q, kd, vd, probs, sink_probs, w_out)


def _layer_b_bwd_proj(xh1, rstd1, g1, b1, dr2, dq, dzb, dkd, dvd, carry_k, carry_v, w_in, w_kv):
    t_len = xh1.shape[0]
    tm = TM_MM
    n_steps = t_len // tm
    per_tile = tm // TM_ATTN
    n_carry = carry_k.shape[0]

    def body(xh_ref, rstd_ref, g_ref, b_ref, dr2_ref, dq_ref, dz_ref, dkd_ref, dvd_ref, *rest):
        carry_refs = rest[:2 * per_tile]
        win_ref, wkv_ref, dr1_ref, dg_ref, db_ref, gwin_ref, gwkv_ref, acc_in, acc_kv = rest[2 * per_tile:]
        i = pl.program_id(0)

        @pl.when(i == 0)
        def _():
            acc_in[...] = jnp.zeros_like(acc_in)
            acc_kv[...] = jnp.zeros_like(acc_kv)
            dg_ref[...] = jnp.zeros_like(dg_ref)
            db_ref[...] = jnp.zeros_like(db_ref)

        lo = lax.broadcasted_iota(jnp.int32, (tm, PAIR), 1) < HEAD_DIM

        def heads_gradient(tile_ref, refs):
            parts = []
            for a in range(per_tile):
                parts.append(tile_ref[a * TM_ATTN:(a + 1) * TM_ATTN - CHUNK, :])
                carry = refs[a][0]
                if a == per_tile - 1:
                    carry = jnp.where(i < n_steps - 1, carry, 0.0)
                parts.append(tile_ref[(a + 1) * TM_ATTN - CHUNK:(a + 1) * TM_ATTN, :] + carry)
            dup = jnp.concatenate(parts, axis=0)
            return jnp.where(lo, _fold_halves(dup[:, 0:PAIR]), _fold_halves(dup[:, PAIR:2 * PAIR]))

        xh = xh_ref[...]
        h1 = (xh * g_ref[...] + b_ref[...]).astype(BF16)
        dq_t = dq_ref[...]
        dz_t = dz_ref[...]
        dkv = jnp.concatenate([heads_gradient(dkd_ref, carry_refs[:per_tile]),
                               heads_gradient(dvd_ref, carry_refs[per_tile:])], axis=1).astype(BF16)
        dh1 = ALPHA * dr2_ref[...]
        dh1 += _dot_nt(dq_t, win_ref[:, 0:B_WIDTH])
        dh1 += _dot_nt(dz_t, win_ref[:, B_WIDTH:2 * B_WIDTH])
        dh1 += _dot_nt(dkv, wkv_ref[...])
        acc_in[:, 0:B_WIDTH] += _dot_tn(h1, dq_t)
        acc_in[:, B_WIDTH:2 * B_WIDTH] += _dot_tn(h1, dz_t)
        acc_kv[...] += _dot_tn(h1, dkv)
        dg_ref[...] += jnp.sum(dh1 * xh, axis=0, keepdims=True)
        db_ref[...] += jnp.sum(dh1, axis=0, keepdims=True)
        dr1_ref[...] = _ln_bwd(dh1 * g_ref[...], xh, rstd_ref[...])

        @pl.when(i == n_steps - 1)
        def _():
            half_rows = D_MODEL // 2
            shard_cols = 2 * B_WIDTH // N_CHIPS
            for s in range(N_CHIPS):
                for c in range(2):
                    gwin_ref[2 * s + c] = acc_in[c * half_rows:(c + 1) * half_rows,
                                                 s * shard_cols:(s + 1) * shard_cols].astype(BF16)
            gwkv_ref[...] = acc_kv[...].astype(BF16)

    vec = jax.ShapeDtypeStruct((1, D_MODEL), F32)
    gwin_shape = (N_DEV, D_MODEL // 2, 2 * B_WIDTH // N_CHIPS)

    def carry_spec(a):
        return pl.BlockSpec((1, CHUNK, 2 * PAIR), lambda i: (jnp.minimum(per_tile * i + a + 1, n_carry - 1), 0, 0))

    carry_specs = [carry_spec(a) for a in range(per_tile)]
    return pl.pallas_call(
        body, name="layer_b_bwd_proj", grid=(n_steps,),
        in_specs=[_rows(tm, D_MODEL), _rows(tm, 1), _const(g1.shape), _const(b1.shape), _rows(tm, D_MODEL),
                  _rows(tm, B_WIDTH), _rows(tm, B_WIDTH), _rows(tm, 2 * PAIR), _rows(tm, 2 * PAIR)]
        + carry_specs + carry_specs + [_resident(w_in.shape), _resident(w_kv.shape)],
        out_specs=[_rows(tm, D_MODEL), _const((1, D_MODEL)), _const((1, D_MODEL)), _const(gwin_shape),
                   _const(w_kv.shape)],
        out_shape=[jax.ShapeDtypeStruct((t_len, D_MODEL), F32), vec, vec,
                   jax.ShapeDtypeStruct(gwin_shape, BF16), jax.ShapeDtypeStruct(w_kv.shape, BF16)],
        scratch_shapes=[pltpu.VMEM(w_in.shape, F32), pltpu.VMEM(w_kv.shape, F32)],
        compiler_params=_params(),
    )(xh1, rstd1, g1, b1, dr2, dq, dzb, dkd, dvd, *([carry_k] * per_tile), *([carry_v] * per_tile), w_in, w_kv)


def _layer_a_bwd_mix(dr1, u, vh, z, y, rv, w_out, lng, lnb, ws, bsp_t, after):
    t_len = u.shape[0]
    tm = TM_ATTN
    n_steps = t_len // tm

    def body(dr_ref, u_ref, vh_ref, z_ref, y_ref, rv_ref, wout_ref, lng_ref, lnb_ref, ws_ref, bsp_ref, after_ref,
             dp_ref, gw_ref, dws_ref, dbsp_ref, dgs_ref, dbs_ref, dvn_scr, gw_acc):
        i = pl.program_id(0)

        @pl.when(i == 0)
        def _():
            gw_acc[...] = jnp.zeros_like(gw_acc)
            dws_ref[...] = jnp.zeros_like(dws_ref)
            dbsp_ref[...] = jnp.zeros_like(dbsp_ref)
            dgs_ref[...] = jnp.zeros_like(dgs_ref)
            dbs_ref[...] = jnp.zeros_like(dbs_ref)

        drb = dr_ref[...].astype(BF16)

        def group_cols(g):
            return slice(g * A_GROUP_DIM, (g + 1) * A_GROUP_DIM)

        tri = (lax.broadcasted_iota(jnp.int32, (CHUNK, CHUNK), 0)
               >= lax.broadcasted_iota(jnp.int32, (CHUNK, CHUNK), 1))
        lane = lax.broadcasted_iota(jnp.int32, (CHUNK, CHUNK), 1)
        ones = jnp.ones((CHUNK, A_GROUP_DIM), BF16)
        dbsp = jnp.zeros((CHUNK, CHUNK), F32)
        dy_next = _dot_nt(drb, wout_ref[group_cols(0), :])
        for g in range(A_GROUPS):
            wsg = jnp.where(tri, ws_ref[g], 0.0).astype(BF16)
            cols = group_cols(g)
            cols_z = slice(2 * A_WIDTH + g * A_GROUP_DIM, 2 * A_WIDTH + (g + 1) * A_GROUP_DIM)
            dy_g = dy_next
            if g + 1 < A_GROUPS:
                dy_next = _dot_nt(drb, wout_ref[group_cols(g + 1), :])
            gw_acc[cols, :] += _dot_tn(y_ref[:, cols], drb)
            both = jnp.zeros((CHUNK, 2 * CHUNK), F32)
            for ci in range(tm // CHUNK):
                rows = slice(ci * CHUNK, (ci + 1) * CHUNK)
                vn = (vh_ref[rows, cols].astype(F32) * lng_ref[:, cols] + lnb_ref[:, cols]).astype(BF16)
                s = _dot(wsg, vn) + bsp_ref[:, g:g + 1]
                sz, dsz = _silu_parts(z_ref[rows, cols].astype(F32))
                dy = dy_g[rows]
                t = dy * u_ref[rows, cols].astype(F32)
                dp_ref[rows, cols] = (dy * (s * sz)).astype(BF16)
                dp_ref[rows, cols_z] = (t * s * dsz).astype(BF16)
                ds_b = (t * sz).astype(BF16)
                both += _dot_nt(ds_b, jnp.concatenate([vn, ones], axis=0))
                dvn_scr[rows, cols] = _dot_tn(wsg, ds_b)
            dws_ref[g] += jnp.where(tri, both[:, 0:CHUNK], 0.0)
            dbsp = jnp.where(lane == g, both[:, CHUNK:2 * CHUNK], dbsp)
        dbsp_ref[...] += dbsp
        dvn = dvn_scr[...]
        vh_t = vh_ref[...].astype(F32)
        dgs_ref[...] += jnp.sum(dvn * vh_t, axis=0, keepdims=True)
        dbs_ref[...] += jnp.sum(dvn, axis=0, keepdims=True)
        dp_ref[:, A_WIDTH:2 * A_WIDTH] = _ln_bwd(dvn * lng_ref[...], vh_t, rv_ref[...]).astype(BF16)

        @pl.when(i == n_steps - 1)
        def _():
            gw_ref[...] = gw_acc[...].astype(BF16)

    wide = jax.ShapeDtypeStruct((1, A_WIDTH), F32)
    return pl.pallas_call(
        body, name="layer_a_bwd_mix", grid=(n_steps,),
        in_specs=[_rows(tm, D_MODEL), _rows(tm, A_WIDTH), _rows(tm, A_WIDTH), _rows(tm, A_WIDTH), _rows(tm, A_WIDTH),
                  _rows(tm, 1), _resident(w_out.shape), _const(lng.shape), _const(lnb.shape), _const(ws.shape),
                  _const(bsp_t.shape), _const(after.shape)],
        out_specs=[_rows(tm, 3 * A_WIDTH), _const(w_out.shape), _const(ws.shape), _const((CHUNK, CHUNK)),
                   _const((1, A_WIDTH)), _const((1, A_WIDTH))],
        out_shape=[jax.ShapeDtypeStruct((t_len, 3 * A_WIDTH), BF16), jax.ShapeDtypeStruct(w_out.shape, BF16),
                   jax.ShapeDtypeStruct(ws.shape, F32), jax.ShapeDtypeStruct((CHUNK, CHUNK), F32),
                   wide, wide],
        scratch_shapes=[pltpu.VMEM((tm, A_WIDTH), F32), pltpu.VMEM(w_out.shape, F32)],
        compiler_params=_params(),
    )(dr1, u, vh, z, y, rv, w_out, lng, lnb, ws, bsp_t, after)


def _layer_a_bwd_dx(dr1, dp, w_in, after, updates=()):
    t_len = dr1.shape[0]
    tm = TM_MM
    n_steps = t_len // tm
    n_upd = len(updates)

    def body(dr_ref, dp_ref, win_ref, after_ref, *refs):
        upd_in, dx_ref, upd_out = refs[:4 * n_upd], refs[4 * n_upd], refs[4 * n_upd + 1:]
        dx_ref[...] = ALPHA * dr_ref[...] + _dot_nt(dp_ref[...], win_ref[...])
        for k in range(n_upd):
            w_ref, g_ref, m_ref, v_ref = upd_in[4 * k:4 * k + 4]
            g_out, d_ref, nm_ref, nv_ref = upd_out[4 * k:4 * k + 4]
            g_out[...] = g_ref[...]
            _adamw_update(w_ref, g_ref, m_ref, v_ref, d_ref, nm_ref, nv_ref)

    upd_specs, upd_shapes, upd_args = [], [], []
    for w, g, m, v in updates:
        rows, cols = w.shape
        upd_specs.append(pl.BlockSpec((rows // n_steps, cols), lambda i: (i, 0)))
        upd_shapes.append(jax.ShapeDtypeStruct((rows, cols), F32))
        upd_args += [w, g, m, v]
    return pl.pallas_call(
        body, name="layer_a_bwd_dx", grid=(n_steps,),
        in_specs=[_rows(tm, D_MODEL), _rows(tm, 3 * A_WIDTH), _resident(w_in.shape), _const(after.shape)]
        + [s for s in upd_specs for _ in range(4)],
        out_specs=[_rows(tm, D_MODEL)] + [s for s in upd_specs for _ in range(4)],
        out_shape=[jax.ShapeDtypeStruct((t_len, D_MODEL), F32)] + [s for s in upd_shapes for _ in range(4)],
        compiler_params=_params(),
    )(dr1, dp, w_in, after, *upd_args)


def _layer_a_bwd_win(xt, dp, after):
    t_len = xt.shape[1]
    tm = TM_WIN
    n_steps = t_len // tm
    shard_cols = 3 * A_WIDTH // N_CHIPS
    half_rows = D_MODEL // 2

    def body(xt_ref, dp_ref, after_ref, gw_ref, acc):
        i = pl.program_id(1)

        @pl.when(i == 0)
        def _():
            acc[...] = jnp.zeros_like(acc)

        acc[...] += _dot(xt_ref[...], dp_ref[...])

        @pl.when(i == n_steps - 1)
        def _():
            for c in range(2):
                gw_ref[0, c] = acc[c * half_rows:(c + 1) * half_rows, :].astype(BF16)

    return pl.pallas_call(
        body, name="layer_a_bwd_win", grid=(N_CHIPS, n_steps),
        in_specs=[pl.BlockSpec((D_MODEL, tm), lambda j, i: (0, i)),
                  pl.BlockSpec((tm, shard_cols), lambda j, i: (i, j)), _const(after.shape)],
        out_specs=pl.BlockSpec((1, 2, half_rows, shard_cols), lambda j, i: (j, 0, 0, 0)),
        out_shape=jax.ShapeDtypeStruct((N_CHIPS, 2, half_rows, shard_cols), BF16),
        scratch_shapes=[pltpu.VMEM((D_MODEL, shard_cols), F32)],
        compiler_params=_params(("arbitrary", "arbitrary")),
    )(xt, dp, after)


def _bucket_onehot():
    dist = jnp.arange(CHUNK, dtype=jnp.int32)[None, :]
    max_exact = REL_BUCKETS // 2
    df = jnp.maximum(dist, 1).astype(F32)
    large = max_exact + (jnp.log(df / max_exact) / math.log(CHUNK / max_exact)
                         * (REL_BUCKETS - max_exact)).astype(jnp.int32)
    bucket = jnp.where(dist < max_exact, dist, jnp.minimum(large, REL_BUCKETS - 1))
    onehot = bucket == jnp.arange(REL_BUCKETS, dtype=jnp.int32)[:, None]
    return onehot.astype(F32)


def _bias_expand(rel_t, onehot):
    def body(rel_ref, oh_ref, out_ref):
        by_distance = jnp.dot(rel_ref[...], oh_ref[...], preferred_element_type=F32,
                              precision=lax.Precision.HIGHEST)
        for h in range(N_HEADS):
            rows = jnp.broadcast_to(by_distance[h:h + 1, :], (2 * CHUNK, CHUNK))
            out_ref[h] = pltpu.roll(rows, 0, 1, stride=1, stride_axis=0)

    return pl.pallas_call(
        body, name="bias_expand",
        out_shape=jax.ShapeDtypeStruct((N_HEADS, 2 * CHUNK, CHUNK), F32),
    )(rel_t, onehot)


def _bias_reduce(oh_ref, db_ref):
    sublane = lax.broadcasted_iota(jnp.int32, (_SUBLANES, CHUNK), 0)
    rows = []
    for h in range(N_HEADS):
        part = db_ref[h, 0:_SUBLANES, :]
        for a in range(1, 2 * CHUNK // _SUBLANES):
            tile = db_ref[h, a * _SUBLANES:(a + 1) * _SUBLANES, :]
            back = (-a * _SUBLANES) % CHUNK
            part += pltpu.roll(tile, back, 1) if back else tile
        total = jnp.where(sublane == 0, part, 0.0)
        for s in range(1, _SUBLANES):
            total += jnp.where(sublane == s, pltpu.roll(part, CHUNK - s, 1), 0.0)
        rows.append(jnp.sum(total, axis=0, keepdims=True))
    by_distance = jnp.concatenate(rows, axis=0)
    return lax.dot_general(oh_ref[...], by_distance, (((1,), (1,)), ((), ())),
                           preferred_element_type=F32, precision=lax.Precision.HIGHEST)


_SMALL_SHAPES = dict(w_spatial=(A_GROUPS, CHUNK, CHUNK), b_spatial=(A_GROUPS, CHUNK), attn_sinks=(1, N_HEADS),
                     rel_bias=(REL_BUCKETS, N_HEADS), post_ln_g=(2, D_MODEL), post_ln_b=(2, D_MODEL),
                     sgu_ln_g=(1, A_WIDTH), sgu_ln_b=(1, A_WIDTH), loss=(1, 1))
_SMALL_ORDER = tuple(_SMALL_SHAPES)


def _small_rows(name):
    shape = _SMALL_SHAPES[name]
    rows = math.prod(shape[:-1]) if shape[-1] < _LANES else math.prod(shape) // _LANES
    return -(-rows // _SUBLANES) * _SUBLANES


def _small_offset(name):
    return sum(_small_rows(n) for n in _SMALL_ORDER[:_SMALL_ORDER.index(name)])


def _pack_small(dws, dbsp, dsink, dbias, onehot, post_g, post_b, dgs, dbs, loss_vec):
    def body(dws_ref, dbsp_ref, dsink_ref, db_ref, oh_ref, g1_ref, g2_ref, b1_ref, b2_ref, dgs_ref, dbs_ref,
             loss_ref, out_ref):
        out_ref[...] = jnp.zeros_like(out_ref)

        def put_flat(name, refs):
            row = _small_offset(name)
            for ref in refs:
                for k in range(ref.shape[1] // _LANES):
                    out_ref[row:row + 1, :] = ref[:, k * _LANES:(k + 1) * _LANES]
                    row += 1

        row = _small_offset("w_spatial")
        for g in range(A_GROUPS):
            out_ref[row + g * CHUNK:row + (g + 1) * CHUNK, :] = dws_ref[g]
        row = _small_offset("b_spatial")
        out_ref[row:row + A_GROUPS, :] = dbsp_ref[...].T[0:A_GROUPS, :]
        lane = lax.broadcasted_iota(jnp.int32, (1, _LANES), 1)
        sinks = jnp.zeros((1, _LANES), F32)
        for h in range(N_HEADS):
            per_query = dsink_ref[h // GROUP:h // GROUP + 1, (h % GROUP) * CHUNK:(h % GROUP + 1) * CHUNK]
            sinks = jnp.where(lane == h, jnp.sum(per_query, axis=1, keepdims=True), sinks)
        row = _small_offset("attn_sinks")
        out_ref[row:row + 1, :] = sinks
        row = _small_offset("rel_bias")
        out_ref[row:row + REL_BUCKETS, 0:N_HEADS] = _bias_reduce(oh_ref, db_ref)
        put_flat("post_ln_g", [g1_ref, g2_ref])
        put_flat("post_ln_b", [b1_ref, b2_ref])
        put_flat("sgu_ln_g", [dgs_ref])
        put_flat("sgu_ln_b", [dbs_ref])
        row = _small_offset("loss")
        out_ref[row:row + 1, 0:1] = (0.5 / D_MODEL) * jnp.sum(loss_ref[...], axis=1, keepdims=True)

    total_rows = sum(_small_rows(n) for n in _SMALL_ORDER)
    return pl.pallas_call(
        body, name="pack_small",
        out_shape=jax.ShapeDtypeStruct((total_rows, _LANES), F32),
    )(dws, dbsp, dsink, dbias, onehot, *post_g, *post_b, dgs, dbs, loss_vec)


def _place():
    return lax.axis_index("x"), lax.axis_index("y"), lax.axis_index("c")


RELAY_PIECES = 4


def _shard_window(full_ref, shard_shape, col_sharded, s, half, piece=None):
    rows, cols = shard_shape
    if half is None:
        start, size = 0, rows
    elif piece is None:
        start, size = half * (rows // 2), rows // 2
    else:
        size = rows // 2 // RELAY_PIECES
        start = (half * RELAY_PIECES + piece) * size
    if col_sharded:
        return full_ref.at[pl.ds(start, size), pl.ds(s * cols, cols)]
    return full_ref.at[pl.ds(s * rows + start, size), :]


def _other_chips(x, y):
    return [(1 - x, y), (x, 1 - y), (1 - x, 1 - y)]


def _gather_weights(shards, col_sharded, fetch, ln_shard):
    n_w = len(shards)
    fetched = [w for w in range(n_w) if fetch[w]]
    full_shapes = []
    for w, cs in zip(shards, col_sharded):
        r, c = w.shape
        full_shapes.append((r, c * N_CHIPS) if cs else (r * N_CHIPS, c))

    def body(*refs):
        in_refs = refs[:n_w]
        ln_ref = refs[n_w]
        full_refs = refs[n_w + 1:2 * n_w + 1]
        ln_full = refs[2 * n_w + 1]
        raw = refs[2 * n_w + 2:3 * n_w + 2]
        stage = refs[3 * n_w + 2:4 * n_w + 2]
        send_sems, recv_sems, load_sems, local_sems, ln_send, ln_recv = refs[4 * n_w + 2:]
        x, y, c = _place()
        s_me = 2 * x + y
        chips = _other_chips(x, y)
        pieces = range(RELAY_PIECES)

        def shard_window(w, s, half, piece=None):
            return _shard_window(full_refs[w], shards[w].shape, col_sharded[w], s, half, piece)

        def piece_rows(w, half, piece):
            rows = shards[w].shape[0] // 2 // RELAY_PIECES
            return pl.ds(pl.multiple_of((half * RELAY_PIECES + piece) * rows, rows), rows)

        def ici_copy(w, k, sender_shard, piece):
            idx = (w * 3 + k) * RELAY_PIECES + piece
            return pltpu.make_async_remote_copy(
                src_ref=stage[w].at[piece_rows(w, c, piece), :], dst_ref=shard_window(w, sender_shard, c, piece),
                send_sem=send_sems.at[idx], recv_sem=recv_sems.at[idx],
                device_id=(*chips[k], c), device_id_type=MESH)

        def d2d_copy(w, k, half, piece):
            s_k = 2 * chips[k][0] + chips[k][1]
            win = shard_window(w, s_k, half, piece)
            idx = (3 * n_w + w * 3 + k) * RELAY_PIECES + piece
            return pltpu.make_async_remote_copy(
                src_ref=win, dst_ref=win, send_sem=send_sems.at[idx], recv_sem=recv_sems.at[idx],
                device_id=(x, y, 1 - c), device_id_type=MESH)

        def ln_copy(k, slot):
            return pltpu.make_async_remote_copy(
                src_ref=ln_ref, dst_ref=ln_full.at[slot], send_sem=ln_send.at[k], recv_sem=ln_recv.at[k],
                device_id=(*chips[k], c), device_id_type=MESH)

        loads = []

        def load(w, rows):
            window = (rows, slice(None)) if rows is not None else (slice(None), slice(None))
            cp = pltpu.make_async_copy(in_refs[w].at[window], raw[w].at[window], load_sems.at[len(loads)])
            cp.start()
            loads.append((cp, w, window))

        for half in (c, 1 - c):
            for w in fetched:
                for q in pieces:
                    load(w, piece_rows(w, half, q))
        for w in range(n_w):
            if not fetch[w]:
                load(w, None)

        def to_bf16(k):
            cp, w, window = loads[k]
            cp.wait()
            stage[w][window] = raw[w][window].astype(BF16)

        ln_full[s_me] = ln_ref[...]
        def shard_of(k):
            return 2 * chips[k][0] + chips[k][1]

        relay_from = jnp.where(c == 0, shard_of(0), shard_of(1))
        relay_to = (jnp.where(c == 0, x, 1 - x), jnp.where(c == 0, 1 - y, y), c)

        def relay_copy(w, sender_shard, piece):
            win = shard_window(w, sender_shard, c, piece)
            idx = (w * 3 + 2) * RELAY_PIECES + piece
            return pltpu.make_async_remote_copy(
                src_ref=win, dst_ref=win, send_sem=send_sems.at[idx], recv_sem=recv_sems.at[idx],
                device_id=relay_to, device_id_type=MESH)

        first = [ln_copy(k, s_me) for k in range(3)]
        for cp in first:
            cp.start()
        n_sent = 0
        for w in fetched:
            for q in pieces:
                to_bf16(n_sent)
                n_sent += 1
                for k in range(2):
                    cp = ici_copy(w, k, s_me, q)
                    cp.start()
                    first.append(cp)
        for k in range(n_sent, len(loads)):
            to_bf16(k)
        own = [pltpu.make_async_copy(stage[w], shard_window(w, s_me, None), local_sems.at[w]) for w in range(n_w)]
        for cp in own:
            cp.start()
        passed = []
        for w in fetched:
            for q in pieces:
                for k in range(2):
                    ici_copy(w, k, shard_of(k), q).wait_recv()
                relay = relay_copy(w, relay_from, q)
                relay.start()
                passed.append(relay)
                for k in range(2):
                    fwd = d2d_copy(w, k, c, q)
                    fwd.start()
                    passed.append(fwd)
        for w in fetched:
            for q in pieces:
                relay_copy(w, shard_of(2), q).wait_recv()
                fwd = d2d_copy(w, 2, c, q)
                fwd.start()
                passed.append(fwd)
        for w in fetched:
            for k in range(3):
                for q in pieces:
                    d2d_copy(w, k, 1 - c, q).wait_recv()
        for k in range(3):
            ln_copy(k, 2 * chips[k][0] + chips[k][1]).wait_recv()
        for cp in first + passed:
            cp.wait_send()
        for cp in own:
            cp.wait()

    vmem = pl.BlockSpec(memory_space=pltpu.VMEM)
    hbm = pl.BlockSpec(memory_space=pl.ANY)
    return pl.pallas_call(
        body, name="gather_weights",
        in_specs=[hbm] * n_w + [vmem],
        out_specs=[hbm] * n_w + [vmem],
        out_shape=[jax.ShapeDtypeStruct(s, BF16) for s in full_shapes]
        + [jax.ShapeDtypeStruct((N_CHIPS,) + ln_shard.shape, F32)],
        scratch_shapes=[pltpu.VMEM(w.shape, F32) for w in shards] + [pltpu.VMEM(w.shape, BF16) for w in shards]
        + [pltpu.SemaphoreType.DMA((6 * RELAY_PIECES * n_w,)), pltpu.SemaphoreType.DMA((6 * RELAY_PIECES * n_w,)),
           pltpu.SemaphoreType.DMA((2 * RELAY_PIECES * len(fetched) + n_w - len(fetched),)),
           pltpu.SemaphoreType.DMA((n_w,)), pltpu.SemaphoreType.DMA((3,)), pltpu.SemaphoreType.DMA((3,))],
        compiler_params=pltpu.CompilerParams(vmem_limit_bytes=VMEM_LIMIT),
    )(*shards, ln_shard)


def _fetch_copy(full_ref, shard_shape, col_sharded, sender_shard, send_sems, recv_sems, idx, chip, c):
    win = _shard_window(full_ref, shard_shape, col_sharded, sender_shard, None)
    return pltpu.make_async_remote_copy(src_ref=win, dst_ref=win, send_sem=send_sems.at[idx],
                                        recv_sem=recv_sems.at[idx], device_id=(*chip, c), device_id_type=MESH)


def _fetch_start(fulls, shard_shapes, col_sharded):
    n = len(fulls)

    def body(*refs):
        full = refs[:n]
        send_sems, recv_sems = refs[n], refs[n + 1]
        token = refs[-1]
        x, y, c = _place()
        for w in range(n):
            for k, chip in enumerate(_other_chips(x, y)):
                _fetch_copy(full[w], shard_shapes[w], col_sharded[w], 2 * x + y, send_sems, recv_sems, w * 3 + k,
                            chip, c).start()
        token[...] = jnp.zeros_like(token)

    outs = pl.pallas_call(
        body, name="fetch_start",
        out_shape=(pltpu.SemaphoreType.DMA((3 * n,)), pltpu.SemaphoreType.DMA((3 * n,)),
                   *[pltpu.HBM(f.shape, f.dtype) for f in fulls], jax.ShapeDtypeStruct((8, 128), F32)),
        in_specs=[_HBM] * n,
        out_specs=(_SEM, _SEM, *([_HBM] * n), pl.BlockSpec(memory_space=pltpu.VMEM)),
        input_output_aliases={i: 2 + i for i in range(n)},
        compiler_params=pltpu.CompilerParams(has_side_effects=pltpu.SideEffectType.DATAFLOW_SIDE_EFFECTING),
    )(*[pltpu.with_memory_space_constraint(f, pltpu.HBM) for f in fulls])
    return dict(send=outs[0], recv=outs[1], full=list(outs[2:2 + n])), outs[-1]


def _fetch_wait(group, shard_shapes, col_sharded, after):
    n = len(group["full"])

    def body(*refs):
        full = refs[:n]
        send_sems, recv_sems = refs[n], refs[n + 1]
        x, y, c = _place()
        for w in range(n):
            for k, chip in enumerate(_other_chips(x, y)):
                _fetch_copy(full[w], shard_shapes[w], col_sharded[w], 2 * x + y, send_sems, recv_sems, w * 3 + k,
                            chip, c).wait_send()
                _fetch_copy(full[w], shard_shapes[w], col_sharded[w], 2 * chip[0] + chip[1], send_sems, recv_sems,
                            w * 3 + k, chip, c).wait_recv()

    outs = pl.pallas_call(
        body, name="fetch_wait", out_shape=tuple(pltpu.HBM(f.shape, f.dtype) for f in group["full"]),
        in_specs=[_HBM] * n + [_SEM, _SEM, pl.BlockSpec(memory_space=pl.ANY)],
        out_specs=tuple([_HBM] * n), input_output_aliases={i: i for i in range(n)},
        compiler_params=pltpu.CompilerParams(has_side_effects=pltpu.SideEffectType.DATAFLOW_SIDE_EFFECTING),
    )(*group["full"], group["send"], group["recv"], after)
    return list(outs)


_HBM = pl.BlockSpec(memory_space=pltpu.HBM)
_SEM = pl.BlockSpec(memory_space=pltpu.SEMAPHORE)
_N_PEER = N_DEV - 1


def _peer(x, y, c, k):
    return (x + (k >> 2)) % 2, (y + ((k >> 1) & 1)) % 2, (c + (k & 1)) % 2


def _exchange_copy(src_ref, land_ref, sliced, send_sems, recv_sems, idx, x, y, c, k):
    px, py, pc = _peer(x, y, c, k)
    src = src_ref.at[4 * px + 2 * py + pc] if sliced else src_ref
    return pltpu.make_async_remote_copy(
        src_ref=src, dst_ref=land_ref.at[4 * x + 2 * y + c],
        send_sem=send_sems.at[idx], recv_sem=recv_sems.at[idx], device_id=(px, py, pc), device_id_type=MESH)


def _exchange_start(tag, arrays, sliced):
    n = len(arrays)
    lands = [lax.empty(a.shape if s else (N_DEV,) + a.shape, a.dtype) for a, s in zip(arrays, sliced)]

    def body(*refs):
        src, land = refs[:n], refs[n:2 * n]
        send_sems, recv_sems = refs[2 * n], refs[2 * n + 1]
        token = refs[-1]
        x, y, c = _place()
        for w in range(n):
            for k in range(1, N_DEV):
                _exchange_copy(src[w], land[w], sliced[w], send_sems, recv_sems, w * _N_PEER + k - 1, x, y, c, k).start()
        token[...] = jnp.zeros_like(token)

    outs = pl.pallas_call(
        body, name="exchange_start_" + tag,
        out_shape=(pltpu.SemaphoreType.DMA((n * _N_PEER,)), pltpu.SemaphoreType.DMA((n * _N_PEER,)),
                   *[pltpu.HBM(a.shape, a.dtype) for a in arrays], *[pltpu.HBM(l.shape, l.dtype) for l in lands],
                   jax.ShapeDtypeStruct((8, 128), F32)),
        in_specs=[_HBM] * (2 * n),
        out_specs=(_SEM, _SEM, *([_HBM] * (2 * n)), pl.BlockSpec(memory_space=pltpu.VMEM)),
        input_output_aliases={i: 2 + i for i in range(2 * n)},
        compiler_params=pltpu.CompilerParams(has_side_effects=pltpu.SideEffectType.DATAFLOW_SIDE_EFFECTING),
    )(*[pltpu.with_memory_space_constraint(a, pltpu.HBM) for a in arrays],
      *[pltpu.with_memory_space_constraint(l, pltpu.HBM) for l in lands])
    return dict(send=outs[0], recv=outs[1], src=list(outs[2:2 + n]), land=list(outs[2 + n:2 + 2 * n]),
                sliced=list(sliced)), outs[-1]


def _exchange_wait(tag, groups, after):
    counts = [len(g["src"]) for g in groups]
    total = sum(counts)

    def body(*refs):
        pos = 0
        x, y, c = _place()
        for g, n in zip(groups, counts):
            src, land = refs[pos:pos + n], refs[pos + n:pos + 2 * n]
            send_sems, recv_sems = refs[pos + 2 * n], refs[pos + 2 * n + 1]
            pos += 2 * n + 2
            for w in range(n):
                for k in range(1, N_DEV):
                    cp = _exchange_copy(src[w], land[w], g["sliced"][w], send_sems, recv_sems,
                                        w * _N_PEER + k - 1, x, y, c, k)
                    cp.wait_send()
                    cp.wait_recv()

    operands, in_specs, aliases, out_shape = [], [], {}, []
    for g in groups:
        for a in g["src"] + g["land"]:
            aliases[len(operands)] = len(out_shape)
            out_shape.append(pltpu.HBM(a.shape, a.dtype))
            operands.append(a)
            in_specs.append(_HBM)
        operands += [g["send"], g["recv"]]
        in_specs += [_SEM, _SEM]
    operands.append(after)
    in_specs.append(pl.BlockSpec(memory_space=pl.ANY))
    outs = pl.pallas_call(
        body, name="exchange_wait_" + tag, out_shape=tuple(out_shape), in_specs=in_specs,
        out_specs=tuple([_HBM] * (2 * total)), input_output_aliases=aliases,
        compiler_params=pltpu.CompilerParams(has_side_effects=pltpu.SideEffectType.DATAFLOW_SIDE_EFFECTING),
    )(*operands)
    srcs, lands, pos = [], [], 0
    for n in counts:
        srcs += list(outs[pos:pos + n])
        lands += list(outs[pos + n:pos + 2 * n])
        pos += 2 * n
    return srcs, lands


def _sum_and_swap(tag, pieces, lands, small=None, small_land=None):
    n_w = len(pieces)
    n_small = 0 if small is None else 1

    def body(*refs):
        g_refs, land_refs = refs[:n_w], refs[n_w:2 * n_w]
        pos = 2 * n_w + 2 * n_small
        out_refs = refs[pos:pos + n_w]
        pos += n_w + n_small
        bufs = refs[pos:pos + n_w]
        load_sems, swap_send, swap_recv = refs[pos + n_w + 2 * n_small:]
        x, y, c = _place()
        me = 4 * x + 2 * y + c

        def slot(k):
            px, py, pc = _peer(x, y, c, k)
            return 4 * px + 2 * py + pc

        def swap_copy(w, half):
            rows = pieces[w].shape[1]
            win = out_refs[w].at[pl.ds(pl.multiple_of(half * rows, rows), rows), :]
            return pltpu.make_async_remote_copy(
                src_ref=win, dst_ref=win, send_sem=swap_send.at[w], recv_sem=swap_recv.at[w],
                device_id=(x, y, 1 - c), device_id_type=MESH)

        loads = []
        for w in range(n_w):
            per_w = [pltpu.make_async_copy(g_refs[w].at[me], bufs[w].at[me], load_sems.at[w * N_DEV])]
            per_w += [pltpu.make_async_copy(land_refs[w].at[slot(k)], bufs[w].at[slot(k)], load_sems.at[w * N_DEV + k])
                      for k in range(1, N_DEV)]
            loads.append(per_w)
        small_loads = []
        if n_small:
            small_ref, small_land_ref = refs[2 * n_w], refs[2 * n_w + 1]
            small_out = refs[2 * n_w + 2 + n_w]
            small_buf, small_sems = refs[pos + n_w], refs[pos + n_w + 1]
            small_loads = [pltpu.make_async_copy(small_land_ref.at[slot(k)], small_buf.at[slot(k)],
                                                 small_sems.at[k - 1]) for k in range(1, N_DEV)]
        for cp in [cp for per_w in loads for cp in per_w] + small_loads:
            cp.start()
        if n_small:
            small_buf[me] = small_ref[...]
        swaps = []
        for w in range(n_w):
            for cp in loads[w]:
                cp.wait()
            rows = pieces[w].shape[1]
            total = bufs[w][0].astype(F32)
            for p in range(1, N_DEV):
                total += bufs[w][p].astype(F32)
            out_refs[w][pl.ds(pl.multiple_of(c * rows, rows), rows), :] = total
            sw = swap_copy(w, c)
            sw.start()
            swaps.append(sw)
        if n_small:
            for cp in small_loads:
                cp.wait()
            total = small_buf[0]
            for p in range(1, N_DEV):
                total += small_buf[p]
            small_out[...] = total
        for w in range(n_w):
            swap_copy(w, 1 - c).wait_recv()
        for sw in swaps:
            sw.wait_send()

    vmem = pl.BlockSpec(memory_space=pltpu.VMEM)
    hbm = pl.BlockSpec(memory_space=pl.ANY)
    small_args = [small, small_land] if n_small else []
    small_shapes = [jax.ShapeDtypeStruct(small.shape, F32)] if n_small else []
    small_scratch = ([pltpu.VMEM((N_DEV,) + small.shape, F32), pltpu.SemaphoreType.DMA((_N_PEER,))]
                     if n_small else [])
    return pl.pallas_call(
        body, name="sum_and_swap_" + tag,
        in_specs=[hbm] * (2 * n_w) + [vmem, hbm] * n_small,
        out_specs=[vmem] * (n_w + n_small),
        out_shape=[jax.ShapeDtypeStruct((2 * p.shape[1], p.shape[2]), F32) for p in pieces] + small_shapes,
        scratch_shapes=[pltpu.VMEM(p.shape, BF16) for p in pieces] + small_scratch
        + [pltpu.SemaphoreType.DMA((n_w * N_DEV,)), pltpu.SemaphoreType.DMA((n_w,)),
           pltpu.SemaphoreType.DMA((n_w,))],
        compiler_params=pltpu.CompilerParams(vmem_limit_bytes=VMEM_LIMIT),
    )(*pieces, *lands, *small_args)


def _adamw_values(w, g_t, m, v):
    c1 = 1.0 - ADAM_B1 ** ADAM_STEP
    c2 = 1.0 - ADAM_B2 ** ADAM_STEP
    nm = ADAM_B1 * m + (1.0 - ADAM_B1) * g_t
    nv = ADAM_B2 * v + (1.0 - ADAM_B2) * (g_t * g_t)
    return -ADAM_LR * ((nm / c1) / (jnp.sqrt(nv / c2) + ADAM_EPS) + ADAM_WD * w), nm, nv


def _adamw_update(w_ref, g_ref, m_ref, v_ref, d_ref, nm_ref, nv_ref):
    d_ref[...], nm_ref[...], nv_ref[...] = _adamw_values(w_ref[...], g_ref[...], m_ref[...], v_ref[...])


def _adamw_small(packed, shard_index, names, weights, moments_m, moments_v):
    n = len(names)
    shapes = [weights[name].shape for name in names]
    flat = [a[name].reshape(-1, a[name].shape[-1]) for name in names for a in (weights, moments_m, moments_v)]

    def body(packed_ref, shard_ref, *refs):
        loss_row = _small_offset("loss")
        refs[-1][...] = packed_ref[loss_row:loss_row + 1, 0:1]
        for k, name in enumerate(names):
            w_ref, m_ref, v_ref = refs[3 * k:3 * k + 3]
            g_ref, d_ref, nm_ref, nv_ref = refs[3 * n + 4 * k:3 * n + 4 * k + 4]
            rows, cols = w_ref.shape
            first = _small_offset(name)
            if cols <= _LANES:
                blocks = [(slice(0, rows), packed_ref[first:first + rows, 0:cols])]
            else:
                per_row = cols // _LANES
                if cols < _SMALL_SHAPES[name][-1]:
                    first = first + shard_ref[0] * per_row
                blocks = [(slice(i, i + 1),
                           jnp.concatenate([packed_ref[pl.ds(first + i * per_row + j, 1), :] for j in range(per_row)],
                                           axis=1)) for i in range(rows)]
            for at, g_t in blocks:
                g_ref[at, :] = g_t
                d_ref[at, :], nm_ref[at, :], nv_ref[at, :] = _adamw_values(w_ref[at, :], g_t, m_ref[at, :],
                                                                           v_ref[at, :])

    vmem = pl.BlockSpec(memory_space=pltpu.VMEM)
    outs = pl.pallas_call(
        body, name="adamw_small",
        in_specs=[vmem, pl.BlockSpec(memory_space=pltpu.SMEM)] + [vmem] * (3 * n),
        out_shape=[jax.ShapeDtypeStruct(flat[3 * k].shape, F32) for k in range(n) for _ in range(4)]
        + [jax.ShapeDtypeStruct((1, 1), F32)],
    )(packed, shard_index.reshape(1).astype(jnp.int32), *flat)
    return [tuple(o.reshape(shapes[k]) for o in outs[4 * k:4 * k + 4]) for k in range(n)], outs[-1].reshape(())


def _adamw(label, w, g, m, v):
    shape = w.shape
    cols = shape[-1]
    rows = w.size // cols
    args = [a.reshape(rows, cols) for a in (w, g, m, v)]

    def body(w_ref, g_ref, m_ref, v_ref, g_out, d_ref, nm_ref, nv_ref):
        g_out[...] = g_ref[...]
        _adamw_update(w_ref, g_ref, m_ref, v_ref, d_ref, nm_ref, nv_ref)

    block_rows = 256 if rows % 256 == 0 and rows > 256 else rows
    spec = pl.BlockSpec((block_rows, cols), lambda i: (i, 0))
    outs = pl.pallas_call(
        body, name="adamw_" + label, grid=(rows // block_rows,),
        in_specs=[spec] * 4, out_specs=[spec] * 4,
        out_shape=[jax.ShapeDtypeStruct((rows, cols), F32)] * 4,
        compiler_params=_params(),
    )(*args)
    return [o.reshape(shape) for o in outs]


def _no_send(tag, arrays, sliced):
    return jnp.zeros((8, 128), F32)


def _local_step(x, tgt, w_in_a, later_weights, first_after, sgu_ln_g, sgu_ln_b, w_spatial, b_spatial,
                attn_sinks, rel_bias, post_ln_g, post_ln_b, send=_no_send):
    bsp_t = b_spatial.T
    g1, b1 = post_ln_g[0:1], post_ln_b[0:1]
    g2, b2 = post_ln_g[1:2], post_ln_b[1:2]
    onehot = _bucket_onehot()
    bias = _bias_expand(rel_bias.T, onehot)
    win = _window_tables()

    xt, u, vh, z, rv, y = _layer_a_fwd(x, w_in_a, sgu_ln_g, sgu_ln_b, w_spatial, bsp_t, first_after)
    w_out_a, w_kv, w_in_b, w_out_b = later_weights(y)
    xh1, rstd1, q, zb, kd, vd = _layer_b_proj(x, y, w_out_a, g1, b1, w_in_b, w_kv)
    o, probs, sink_probs, dr2, loss_vec, dg2, db2 = _layer_b_fwd(q, zb, kd, vd, bias, win, attn_sinks, xh1, g1, b1,
                                                                 w_out_b, g2, b2, tgt)
    dq, dzb, dkd, dvd, carry_k, carry_v, gw_out_b, dsink, dbias = _layer_b_bwd_attn(
        dr2, zb, o, q, kd, vd, probs, sink_probs, w_out_b)
    dr1, dg1, db1, gw_in_b, gw_kv = _layer_b_bwd_proj(xh1, rstd1, g1, b1, dr2, dq, dzb, dkd, dvd, carry_k, carry_v,
                                                      w_in_b, w_kv)
    gw_out_b = gw_out_b.reshape(N_DEV, -1, D_MODEL)
    gw_kv = gw_kv.reshape(N_DEV, -1, 2 * PAIR)
    after = send("b", [gw_out_b, gw_in_b, gw_kv], [True, True, True])
    dp, gw_out_a, dws, dbsp, dgs, dbs = _layer_a_bwd_mix(dr1, u, vh, z, y, rv, w_out_a, sgu_ln_g, sgu_ln_b,
                                                         w_spatial, bsp_t, after)
    gw_out_a = gw_out_a.reshape(N_DEV, -1, D_MODEL)
    small = _pack_small(dws, dbsp, dsink, dbias, onehot, (dg1, dg2), (db1, db2), dgs, dbs, loss_vec)
    after = send("a_out", [gw_out_a, small], [True, False])
    gw_in_a = _layer_a_bwd_win(xt, dp, after).reshape(N_DEV, D_MODEL // 2, -1)
    after = send("a_in", [gw_in_a], [True])
    after, updates = after if isinstance(after, tuple) else (after, ())
    grad_x, *updated = _layer_a_bwd_dx(dr1, dp, w_in_a, after, updates)

    pieces = [gw_in_a, gw_out_a, gw_kv, gw_in_b, gw_out_b]
    return grad_x, pieces, small, updated


def kernel(x, w_in_a, sgu_ln_g, sgu_ln_b, w_spatial, b_spatial, w_out_a, w_kv, w_in_b, attn_sinks, rel_bias, w_out_b, post_ln_g, post_ln_b, loss_target, m_w_in_a, m_sgu_ln_g, m_sgu_ln_b, m_w_spatial, m_b_spatial, m_w_out_a, m_w_kv, m_w_in_b, m_attn_sinks, m_rel_bias, m_w_out_b, m_post_ln_g, m_post_ln_b, v_w_in_a, v_sgu_ln_g, v_sgu_ln_b, v_w_spatial, v_b_spatial, v_w_out_a, v_w_kv, v_w_in_b, v_attn_sinks, v_rel_bias, v_w_out_b, v_post_ln_g, v_post_ln_b):
    weights = dict(w_in_a=w_in_a, sgu_ln_g=sgu_ln_g, sgu_ln_b=sgu_ln_b, w_spatial=w_spatial, b_spatial=b_spatial,
                   w_out_a=w_out_a, w_kv=w_kv, w_in_b=w_in_b, attn_sinks=attn_sinks, rel_bias=rel_bias,
                   w_out_b=w_out_b, post_ln_g=post_ln_g, post_ln_b=post_ln_b)
    moments_m = dict(w_in_a=m_w_in_a, sgu_ln_g=m_sgu_ln_g, sgu_ln_b=m_sgu_ln_b, w_spatial=m_w_spatial,
                     b_spatial=m_b_spatial, w_out_a=m_w_out_a, w_kv=m_w_kv, w_in_b=m_w_in_b,
                     attn_sinks=m_attn_sinks, rel_bias=m_rel_bias, w_out_b=m_w_out_b, post_ln_g=m_post_ln_g,
                     post_ln_b=m_post_ln_b)
    moments_v = dict(w_in_a=v_w_in_a, sgu_ln_g=v_sgu_ln_g, sgu_ln_b=v_sgu_ln_b, w_spatial=v_w_spatial,
                     b_spatial=v_b_spatial, w_out_a=v_w_out_a, w_kv=v_w_kv, w_in_b=v_w_in_b,
                     attn_sinks=v_attn_sinks, rel_bias=v_rel_bias, w_out_b=v_w_out_b, post_ln_g=v_post_ln_g,
                     post_ln_b=v_post_ln_b)
    order = ("w_in_a", "sgu_ln_g", "sgu_ln_b", "w_spatial", "b_spatial", "w_out_a", "w_kv", "w_in_b", "attn_sinks",
             "rel_bias", "w_out_b", "post_ln_g", "post_ln_b")

    shard_index = 2 * lax.axis_index("x") + lax.axis_index("y")
    ln_shard = jnp.concatenate([sgu_ln_g, sgu_ln_b], axis=0)
    shards = [w_in_a[0], w_out_a[0], w_kv, w_in_b[0], w_out_b[0]]
    col_sharded = [True, False, False, True, False]
    full_in_a, *later, ln_full = _gather_weights(shards, col_sharded, [True, False, False, False, False], ln_shard)
    ln_full = jnp.transpose(ln_full, (1, 0, 2)).reshape(2, A_WIDTH)
    later_shapes = [s.shape for s in shards[1:]]
    fetch_group, fetch_token = _fetch_start(later, later_shapes, col_sharded[1:])

    def later_weights(y):
        return _fetch_wait(fetch_group, later_shapes, col_sharded[1:], y)

    groups, grads, deltas, new_m, new_v, scalars = {}, {}, {}, {}, {}, {}
    early = ("w_out_b", "w_in_b", "w_kv", "w_out_a")

    def two_dim(a):
        return a.reshape(-1, a.shape[-1])

    def send(tag, arrays, sliced):
        groups[tag], token = _exchange_start(tag, arrays, sliced)
        if tag != "a_in":
            return token
        srcs, lands = _exchange_wait("early", [groups["b"], groups["a_out"]], token)
        *reduced, packed_sum = _sum_and_swap("early", srcs[:4], lands[:4], srcs[4], lands[4])
        updates = [(two_dim(weights[n]), g, two_dim(moments_m[n]), two_dim(moments_v[n]))
                   for n, g in zip(early, reduced)]
        small_names = ("sgu_ln_g", "sgu_ln_b", "w_spatial", "b_spatial", "attn_sinks", "rel_bias", "post_ln_g",
                       "post_ln_b")
        small_updates, scalars["loss"] = _adamw_small(packed_sum, shard_index, small_names, weights, moments_m,
                                                      moments_v)
        for name, (g, d, nm, nv) in zip(small_names, small_updates):
            grads[name], deltas[name], new_m[name], new_v[name] = g, d, nm, nv
        return new_m["b_spatial"].reshape(A_GROUPS, CHUNK), updates

    grad_x, _, _, updated = _local_step(
        x[0], loss_target[0], full_in_a, later_weights, fetch_token, ln_full[0:1], ln_full[1:2], w_spatial[0],
        b_spatial[0], attn_sinks, rel_bias, post_ln_g, post_ln_b, send=send)
    for k, name in enumerate(early):
        grads[name], deltas[name], new_m[name], new_v[name] = [
            a.reshape(weights[name].shape) for a in updated[4 * k:4 * k + 4]]

    srcs, lands = _exchange_wait("late", [groups["a_in"]], grad_x)
    (g_in_a,) = _sum_and_swap("late", srcs, lands)
    grads["w_in_a"], deltas["w_in_a"], new_m["w_in_a"], new_v["w_in_a"] = _adamw(
        "w_in_a", w_in_a, g_in_a.reshape(w_in_a.shape), m_w_in_a, v_w_in_a)
    return (scalars["loss"], grad_x[None], *[grads[n] for n in order], *[deltas[n] for n in order],
            *[new_m[n] for n in order], *[new_v[n] for n in order])
```

```python
import math

import jax
import jax.numpy as jnp
from jax import lax
from jax.experimental import pallas as pl
from jax.experimental.pallas import tpu as pltpu

F32 = jnp.float32
BF16 = jnp.bfloat16

D_MODEL = 1024
A_WIDTH = 2048
A_GROUPS = 8
A_GROUP_DIM = 256
CHUNK = 128
N_HEADS = 16
N_KV = 2
HEAD_DIM = 64
PAIR = 2 * HEAD_DIM
B_WIDTH = 1024
REL_BUCKETS = 32
ALPHA = 4.0 ** 0.25
LN_EPS = 1e-5
NEG_INF = -1e30
SCALE = HEAD_DIM ** -0.5

ADAM_LR = 0.001
ADAM_B1 = 0.9
ADAM_B2 = 0.999
ADAM_EPS = 1e-08
ADAM_WD = 0.01
ADAM_STEP = 10

N_DEV = 8
N_CHIPS = 4
MESH = pl.DeviceIdType.MESH
VMEM_LIMIT = 56 * 1024 * 1024

TM_ATTN = 256
TM_BWD_ATTN = 512
TM_MM = 512
TM_WIN = 1024
_LANES = 128
_SUBLANES = 8


def _dot(a, b):
    return jnp.dot(a, b, preferred_element_type=F32)


def _dot_nt(a, b):
    return lax.dot_general(a, b, (((1,), (1,)), ((), ())), preferred_element_type=F32)


def _dot_tn(a, b):
    return lax.dot_general(a, b, (((0,), (0,)), ((), ())), preferred_element_type=F32)


def _ln_fwd(r):
    mu = jnp.mean(r, axis=-1, keepdims=True)
    rc = r - mu
    var = jnp.mean(rc * rc, axis=-1, keepdims=True)
    rstd = lax.rsqrt(var + LN_EPS)
    return rc * rstd, rstd


def _ln_bwd(dxh, xh, rstd):
    m1 = jnp.mean(dxh, axis=-1, keepdims=True)
    m2 = jnp.mean(dxh * xh, axis=-1, keepdims=True)
    return rstd * (dxh - m1 - xh * m2)


def _silu_parts(z):
    sg = jax.nn.sigmoid(z)
    return z * sg, sg * (1.0 + z * (1.0 - sg))


def _dup_halves(blk):
    sw = pltpu.roll(blk, HEAD_DIM, 1)
    lo = lax.broadcasted_iota(jnp.int32, blk.shape, 1) < HEAD_DIM
    return jnp.where(lo, blk, sw), jnp.where(lo, sw, blk)


def _fold_halves(blk):
    return blk + pltpu.roll(blk, HEAD_DIM, 1)


def _resident(shape):
    nd = len(shape)
    return pl.BlockSpec(shape, lambda *_: (0,) * nd, pipeline_mode=pl.Buffered(1))


def _const(shape):
    nd = len(shape)
    return pl.BlockSpec(shape, lambda *_: (0,) * nd)


def _rows(tm, cols):
    return pl.BlockSpec((tm, cols), lambda i: (i, 0))


def _params(sem=("arbitrary",)):
    return pltpu.CompilerParams(dimension_semantics=sem, vmem_limit_bytes=VMEM_LIMIT)


def _spatial_mix(ws_ref, bsp_ref, vn, s_scr, n_chunks):
    tri = (lax.broadcasted_iota(jnp.int32, (CHUNK, CHUNK), 0)
           >= lax.broadcasted_iota(jnp.int32, (CHUNK, CHUNK), 1))
    for g in range(A_GROUPS):
        wsg = jnp.where(tri, ws_ref[g], 0.0).astype(BF16)
        cols = slice(g * A_GROUP_DIM, (g + 1) * A_GROUP_DIM)
        for ci in range(n_chunks):
            rows = slice(ci * CHUNK, (ci + 1) * CHUNK)
            s_scr[rows, cols] = _dot(wsg, vn[rows, cols]) + bsp_ref[:, g:g + 1]


def _layer_a_fwd(x, w_in, lng, lnb, ws, bsp_t, after):
    t_len = x.shape[0]
    tm = TM_ATTN

    def body(x_ref, win_ref, lng_ref, lnb_ref, ws_ref, bsp_ref, after_ref,
             xt_ref, u_ref, vh_ref, z_ref, rv_ref, y_ref, s_scr):
        x_t = x_ref[...]
        xb = x_t.astype(BF16)
        xt_ref[...] = x_t.T.astype(BF16)
        u = _dot(xb, win_ref[:, 0:A_WIDTH])
        v = _dot(xb, win_ref[:, A_WIDTH:2 * A_WIDTH])
        z = _dot(xb, win_ref[:, 2 * A_WIDTH:3 * A_WIDTH])
        vh, rv = _ln_fwd(v)
        vn = (vh * lng_ref[...] + lnb_ref[...]).astype(BF16)
        _spatial_mix(ws_ref, bsp_ref, vn, s_scr, tm // CHUNK)
        sz, _ = _silu_parts(z)
        y_ref[...] = (u * s_scr[...] * sz).astype(BF16)
        u_ref[...] = u.astype(BF16)
        vh_ref[...] = vh.astype(BF16)
        z_ref[...] = z.astype(BF16)
        rv_ref[...] = rv

    wide = jax.ShapeDtypeStruct((t_len, A_WIDTH), BF16)
    return pl.pallas_call(
        body, name="layer_a_fwd", grid=(t_len // tm,),
        in_specs=[_rows(tm, D_MODEL), _resident(w_in.shape), _const(lng.shape), _const(lnb.shape), _const(ws.shape),
                  _const(bsp_t.shape), _const(after.shape)],
        out_specs=[pl.BlockSpec((D_MODEL, tm), lambda i: (0, i)), _rows(tm, A_WIDTH), _rows(tm, A_WIDTH),
                   _rows(tm, A_WIDTH), _rows(tm, 1), _rows(tm, A_WIDTH)],
        out_shape=[jax.ShapeDtypeStruct((D_MODEL, t_len), BF16), wide, wide, wide,
                   jax.ShapeDtypeStruct((t_len, 1), F32), wide],
        scratch_shapes=[pltpu.VMEM((tm, A_WIDTH), F32)],
        compiler_params=_params(),
    )(x, w_in, lng, lnb, ws, bsp_t, after)


def _layer_b_proj(x, y, w_out_a, g1, b1, w_in, w_kv):
    t_len = x.shape[0]
    tm = 2 * TM_MM

    def body(x_ref, y_ref, wout_ref, g_ref, b_ref, win_ref, wkv_ref, xh_ref, r1_ref, q_ref, z_ref, kd_ref, vd_ref):
        halves = [slice(k * TM_MM, (k + 1) * TM_MM) for k in range(2)]
        projected = [_dot(y_ref[rows, :], wout_ref[...]) for rows in halves]
        for rows, out_a in zip(halves, projected):
            xh, r1 = _ln_fwd(ALPHA * x_ref[rows, :] + out_a)
            xh_ref[rows, :] = xh
            r1_ref[rows, :] = r1
            h1 = (xh * g_ref[...] + b_ref[...]).astype(BF16)
            q_ref[rows, :] = (_dot(h1, win_ref[:, 0:B_WIDTH]) * SCALE).astype(BF16)
            z_ref[rows, :] = _dot(h1, win_ref[:, B_WIDTH:2 * B_WIDTH]).astype(BF16)
            kv = _dot(h1, wkv_ref[...])
            k0, k1 = _dup_halves(kv[:, 0:PAIR])
            v0, v1 = _dup_halves(kv[:, PAIR:2 * PAIR])
            kd_ref[rows, 0:PAIR] = k0.astype(BF16)
            kd_ref[rows, PAIR:2 * PAIR] = k1.astype(BF16)
            vd_ref[rows, 0:PAIR] = v0.astype(BF16)
            vd_ref[rows, PAIR:2 * PAIR] = v1.astype(BF16)

    return pl.pallas_call(
        body, name="layer_b_proj", grid=(t_len // tm,),
        in_specs=[_rows(tm, D_MODEL), _rows(tm, A_WIDTH), _resident(w_out_a.shape), _const(g1.shape),
                  _const(b1.shape), _resident(w_in.shape), _resident(w_kv.shape)],
        out_specs=[_rows(tm, D_MODEL), _rows(tm, 1), _rows(tm, B_WIDTH), _rows(tm, B_WIDTH), _rows(tm, 2 * PAIR),
                   _rows(tm, 2 * PAIR)],
        out_shape=[jax.ShapeDtypeStruct((t_len, D_MODEL), F32), jax.ShapeDtypeStruct((t_len, 1), F32),
                   jax.ShapeDtypeStruct((t_len, B_WIDTH), BF16), jax.ShapeDtypeStruct((t_len, B_WIDTH), BF16),
                   jax.ShapeDtypeStruct((t_len, 2 * PAIR), BF16), jax.ShapeDtypeStruct((t_len, 2 * PAIR), BF16)],
        compiler_params=_params(),
    )(x, y, w_out_a, g1, b1, w_in, w_kv)


GROUP = N_HEADS // N_KV
GROUP_Q = GROUP * CHUNK


def _window_tables():
    j = jnp.arange(2 * CHUNK, dtype=jnp.int32)[:, None]
    t = jnp.arange(CHUNK, dtype=jnp.int32)[None, :]
    dist = t + CHUNK - j
    inside = (dist >= 0) & (dist < CHUNK)
    return jnp.stack([inside & (j >= CHUNK), inside]).astype(F32)


def _band(ref, chunk_index, kvh):
    prev0 = pl.multiple_of(jnp.maximum(chunk_index - 1, 0) * CHUNK, CHUNK)
    cur0 = pl.multiple_of(chunk_index * CHUNK, CHUNK)
    cols = slice(kvh * PAIR, (kvh + 1) * PAIR)
    return jnp.concatenate([ref[pl.ds(prev0, CHUNK), cols], ref[pl.ds(cur0, CHUNK), cols]], axis=0)


def _group_tables(bias_ref, win_ref, sink_ref, chunk_index, kvh):
    bias = jnp.concatenate([bias_ref[kvh * GROUP + j] for j in range(GROUP)], axis=1)
    win = win_ref[jnp.minimum(chunk_index, 1)]
    mask = jnp.concatenate([win] * GROUP, axis=1) > 0.5
    sink = jnp.concatenate([jnp.full((1, CHUNK), sink_ref[0, kvh * GROUP + j], F32) for j in range(GROUP)], axis=1)
    return bias, mask, sink


def _attn_probs(qs, kband, bias, mask, sink):
    logits = jnp.where(mask, _dot_nt(kband, qs) + bias, NEG_INF)
    m = jnp.maximum(jnp.max(logits, axis=0, keepdims=True), sink)
    e = jnp.exp(logits - m)
    es = jnp.exp(sink - m)
    inv = 1.0 / (jnp.sum(e, axis=0, keepdims=True) + es)
    return e * inv, es * inv


def _half_mask():
    return lax.broadcasted_iota(jnp.int32, (CHUNK, PAIR), 1) < HEAD_DIM


def _stack_heads(src_ref, rows, kvh, dst_scr, lo):
    for j in range(GROUP):
        h = kvh * GROUP + j
        blk = src_ref[rows, (h // 2) * PAIR:(h // 2 + 1) * PAIR].astype(F32)
        keep = lo if h % 2 == 0 else ~lo
        dst_scr[j * CHUNK:(j + 1) * CHUNK, :] = jnp.where(keep, blk, 0.0).astype(BF16)


def _probs_spec(tm):
    return pl.BlockSpec((tm // CHUNK, N_KV, 2 * CHUNK, GROUP_Q), lambda i: (i, 0, 0, 0))


def _sink_probs_spec(tiles=1):
    return pl.BlockSpec((tiles, 8, GROUP_Q), lambda i: (i, 0, 0))


def _unstack_pairs(stacked, pp, lo):
    return jnp.where(lo, stacked[(2 * pp) * CHUNK:(2 * pp + 1) * CHUNK], stacked[(2 * pp + 1) * CHUNK:(2 * pp + 2) * CHUNK])


def _layer_b_fwd(q, zb, kd, vd, bias, win, sinks, xh1, g1, b1, w_out, g2, b2, tgt):
    t_len = q.shape[0]
    tm = 2 * TM_ATTN

    def body(q_ref, z_ref, kd_ref, vd_ref, bias_ref, win_ref, sink_ref, xh_ref, g1_ref, b1_ref, wout_ref, g2_ref,
             b2_ref, tgt_ref, o_ref, p_ref, ps_ref, dr_ref, loss_ref, dg_ref, db_ref, o_scr, qs_scr):
        i = pl.program_id(0)

        @pl.when(i == 0)
        def _():
            loss_ref[...] = jnp.zeros_like(loss_ref)
            dg_ref[...] = jnp.zeros_like(dg_ref)
            db_ref[...] = jnp.zeros_like(db_ref)

        lo = _half_mask()
        ps_ref[...] = jnp.zeros_like(ps_ref)
        per_part = TM_ATTN // CHUNK
        for part in range(tm // TM_ATTN):
            part_rows = slice(part * TM_ATTN, (part + 1) * TM_ATTN)
            for cp in range(per_part):
                ci = part * per_part + cp
                cg = i * (tm // CHUNK) + ci
                rows = slice(ci * CHUNK, (ci + 1) * CHUNK)
                for kvh in range(N_KV):
                    kband = _band(kd_ref, cg, kvh)
                    vband = _band(vd_ref, cg, kvh)
                    bias_g, mask, sink = _group_tables(bias_ref, win_ref, sink_ref, cg, kvh)
                    _stack_heads(q_ref, rows, kvh, qs_scr, lo)
                    p, p_sink = _attn_probs(qs_scr[...], kband, bias_g, mask, sink)
                    p = p.astype(BF16)
                    p_ref[ci, kvh] = p
                    ps_ref[part, cp * N_KV + kvh:cp * N_KV + kvh + 1, :] = p_sink
                    o_stack = _dot_tn(p, vband)
                    for pp in range(GROUP // 2):
                        pair = kvh * (GROUP // 2) + pp
                        o_scr[rows, pair * PAIR:(pair + 1) * PAIR] = _unstack_pairs(o_stack, pp, lo)
            o = o_scr[part_rows, :]
            o_ref[part_rows, :] = o.astype(BF16)
            sz, _ = _silu_parts(z_ref[part_rows, :].astype(F32))
            y = (o * sz).astype(BF16)
            h1 = xh_ref[part_rows, :] * g1_ref[...] + b1_ref[...]
            r = ALPHA * h1 + _dot(y, wout_ref[...])
            xh2, rstd2 = _ln_fwd(r)
            diff = xh2 * g2_ref[...] + b2_ref[...] - tgt_ref[part_rows, :]
            loss_ref[...] += jnp.sum(diff * diff, axis=0, keepdims=True)
            dh2 = diff * (1.0 / D_MODEL)
            dg_ref[...] += jnp.sum(dh2 * xh2, axis=0, keepdims=True)
            db_ref[...] += jnp.sum(dh2, axis=0, keepdims=True)
            dr_ref[part_rows, :] = _ln_bwd(dh2 * g2_ref[...], xh2, rstd2)

    vec = jax.ShapeDtypeStruct((1, D_MODEL), F32)
    return pl.pallas_call(
        body, name="layer_b_fwd", grid=(t_len // tm,),
        in_specs=[_rows(tm, B_WIDTH), _rows(tm, B_WIDTH), _resident(kd.shape), _resident(vd.shape),
                  _resident(bias.shape), _resident(win.shape), pl.BlockSpec(memory_space=pltpu.SMEM),
                  _rows(tm, D_MODEL), _const(g1.shape), _const(b1.shape), _resident(w_out.shape), _const(g2.shape),
                  _const(b2.shape), _rows(tm, D_MODEL)],
        out_specs=[_rows(tm, B_WIDTH), _probs_spec(tm), _sink_probs_spec(tm // TM_ATTN), _rows(tm, D_MODEL)]
        + [_const((1, D_MODEL))] * 3,
        out_shape=[jax.ShapeDtypeStruct((t_len, B_WIDTH), BF16),
                   jax.ShapeDtypeStruct((t_len // CHUNK, N_KV, 2 * CHUNK, GROUP_Q), BF16),
                   jax.ShapeDtypeStruct((t_len // TM_ATTN, 8, GROUP_Q), F32),
                   jax.ShapeDtypeStruct((t_len, D_MODEL), F32), vec, vec, vec],
        scratch_shapes=[pltpu.VMEM((tm, B_WIDTH), F32), pltpu.VMEM((GROUP_Q, PAIR), BF16)],
        compiler_params=_params(),
    )(q, zb, kd, vd, bias, win, sinks, xh1, g1, b1, w_out, g2, b2, tgt)


def _layer_b_bwd_attn(dr2, zb, o, q, kd, vd, probs, sink_probs, w_out):
    t_len = q.shape[0]
    tm = TM_BWD_ATTN
    n_steps = t_len // tm
    n_chunks = tm // CHUNK
    per_part = TM_ATTN // CHUNK

    def body(dr_ref, z_ref, o_ref, q_ref, kd_ref, vd_ref, p_ref, ps_ref, wout_ref,
             dq_ref, dz_ref, dkd_ref, dvd_ref, ck_ref, cv_ref, gw_ref, dsink_ref, dbias_ref,
             do_scr, qs_scr, dos_scr, gw_acc):
        i = pl.program_id(0)

        @pl.when(i == 0)
        def _():
            gw_acc[...] = jnp.zeros_like(gw_acc)
            dsink_ref[...] = jnp.zeros_like(dsink_ref)
            dbias_ref[...] = jnp.zeros_like(dbias_ref)

        drb = dr_ref[...].astype(BF16)
        per_kvh = 2
        n_blocks = N_KV * per_kvh
        block_cols = B_WIDTH // n_blocks

        def through_gate(b):
            cols = slice(b * block_cols, (b + 1) * block_cols)
            dy = _dot_nt(drb, wout_ref[cols, :])
            sz, dsz = _silu_parts(z_ref[:, cols].astype(F32))
            o_t = o_ref[:, cols].astype(F32)
            dz_ref[:, cols] = (dy * o_t * dsz).astype(BF16)
            do_scr[:, cols] = (dy * sz).astype(BF16)
            return (o_t * sz).astype(BF16)

        def weight_gradient(b, gated):
            cols = slice(b * block_cols, (b + 1) * block_cols)
            gw_acc[cols, :] += _dot_tn(gated, drb)

        gated = {b: through_gate(b) for b in range(per_kvh)}

        lo = _half_mask()
        for kvh in range(N_KV):
            kcols = slice(kvh * PAIR, (kvh + 1) * PAIR)
            dk_bands, dv_bands = [], []
            for ci in range(n_chunks):
                unit = kvh * n_chunks + ci
                if ci < per_kvh and kvh + 1 < N_KV:
                    gated[(kvh + 1) * per_kvh + ci] = through_gate((kvh + 1) * per_kvh + ci)
                if unit in gated:
                    weight_gradient(unit, gated.pop(unit))
                cg = i * n_chunks + ci
                rows = slice(ci * CHUNK, (ci + 1) * CHUNK)
                kband = _band(kd_ref, cg, kvh)
                vband = _band(vd_ref, cg, kvh)
                _stack_heads(q_ref, rows, kvh, qs_scr, lo)
                _stack_heads(do_scr, rows, kvh, dos_scr, lo)
                qs = qs_scr[...]
                dos = dos_scr[...]
                pb = p_ref[ci, kvh]
                p = pb.astype(F32)
                sink_row = (ci % per_part) * N_KV + kvh
                p_sink = ps_ref[ci // per_part, sink_row:sink_row + 1, :]
                dp = _dot_nt(vband, dos)
                delta = jnp.sum(p * dp, axis=0, keepdims=True)
                dlog = p * (dp - delta)
                for j in range(GROUP):
                    dbias_ref[kvh * GROUP + j] += dlog[:, j * CHUNK:(j + 1) * CHUNK]
                dsink_ref[kvh:kvh + 1, :] += -(p_sink * delta)
                ds = dlog.astype(BF16)
                dq_stack = _dot_tn(ds, kband) * SCALE
                for pp in range(GROUP // 2):
                    pair = kvh * (GROUP // 2) + pp
                    dq_ref[rows, pair * PAIR:(pair + 1) * PAIR] = _unstack_pairs(dq_stack, pp, lo).astype(BF16)
                dk_bands.append(_dot(ds, qs))
                dv_bands.append(_dot(pb, dos))
            for bands, out_ref, carry_ref in ((dk_bands, dkd_ref, ck_ref), (dv_bands, dvd_ref, cv_ref)):
                carry_ref[0, :, kcols] = bands[0][0:CHUNK]
                for ci in range(n_chunks):
                    own = bands[ci][CHUNK:2 * CHUNK]
                    if ci + 1 < n_chunks:
                        own = own + bands[ci + 1][0:CHUNK]
                    out_ref[ci * CHUNK:(ci + 1) * CHUNK, kcols] = own

        @pl.when(i == n_steps - 1)
        def _():
            gw_ref[...] = gw_acc[...].astype(BF16)

    carry_spec = pl.BlockSpec((1, CHUNK, 2 * PAIR), lambda i: (i, 0, 0))
    carry_shape = jax.ShapeDtypeStruct((n_steps, CHUNK, 2 * PAIR), F32)
    bias_shape = (N_HEADS, 2 * CHUNK, CHUNK)
    return pl.pallas_call(
        body, name="layer_b_bwd_attn", grid=(n_steps,),
        in_specs=[_rows(tm, D_MODEL), _rows(tm, B_WIDTH), _rows(tm, B_WIDTH), _rows(tm, B_WIDTH),
                  _resident(kd.shape), _resident(vd.shape), _probs_spec(tm), _sink_probs_spec(tm // TM_ATTN),
                  _resident(w_out.shape)],
        out_specs=[_rows(tm, B_WIDTH), _rows(tm, B_WIDTH), _rows(tm, 2 * PAIR), _rows(tm, 2 * PAIR),
                   carry_spec, carry_spec, _const(w_out.shape), _const((N_KV, GROUP_Q)), _const(bias_shape)],
        out_shape=[jax.ShapeDtypeStruct((t_len, B_WIDTH), BF16), jax.ShapeDtypeStruct((t_len, B_WIDTH), BF16),
                   jax.ShapeDtypeStruct((t_len, 2 * PAIR), F32), jax.ShapeDtypeStruct((t_len, 2 * PAIR), F32),
                   carry_shape, carry_shape, jax.ShapeDtypeStruct(w_out.shape, BF16),
                   jax.ShapeDtypeStruct((N_KV, GROUP_Q), F32), jax.ShapeDtypeStruct(bias_shape, F32)],
        scratch_shapes=[pltpu.VMEM((tm, B_WIDTH), BF16), pltpu.VMEM((GROUP_Q, PAIR), BF16),
                        pltpu.VMEM((GROUP_Q, PAIR), BF16), pltpu.VMEM(w_out.shape, F32)],
        compiler_params=_params(),
    )(dr2, zb, o, q, kd, vd, probs, sink_probs, w_out)


def _layer_b_bwd_proj(xh1, rstd1, g1, b1, dr2, dq, dzb, dkd, dvd, carry_k, carry_v, w_in, w_kv):
    t_len = xh1.shape[0]
    tm = TM_MM
    n_steps = t_len // tm
    per_tile = tm // TM_BWD_ATTN
    n_carry = carry_k.shape[0]

    def body(xh_ref, rstd_ref, g_ref, b_ref, dr2_ref, dq_ref, dz_ref, dkd_ref, dvd_ref, *rest):
        carry_refs = rest[:2 * per_tile]
        win_ref, wkv_ref, dr1_ref, dg_ref, db_ref, gwin_ref, gwkv_ref, acc_in, acc_kv = rest[2 * per_tile:]
        i = pl.program_id(0)

        @pl.when(i == 0)
        def _():
            acc_in[...] = jnp.zeros_like(acc_in)
            acc_kv[...] = jnp.zeros_like(acc_kv)
            dg_ref[...] = jnp.zeros_like(dg_ref)
            db_ref[...] = jnp.zeros_like(db_ref)

        lo = lax.broadcasted_iota(jnp.int32, (tm, PAIR), 1) < HEAD_DIM

        def heads_gradient(tile_ref, refs):
            parts = []
            for a in range(per_tile):
                parts.append(tile_ref[a * TM_BWD_ATTN:(a + 1) * TM_BWD_ATTN - CHUNK, :])
                carry = refs[a][0]
                if a == per_tile - 1:
                    carry = jnp.where(i < n_steps - 1, carry, 0.0)
                parts.append(tile_ref[(a + 1) * TM_BWD_ATTN - CHUNK:(a + 1) * TM_BWD_ATTN, :] + carry)
            dup = jnp.concatenate(parts, axis=0)
            return jnp.where(lo, _fold_halves(dup[:, 0:PAIR]), _fold_halves(dup[:, PAIR:2 * PAIR]))

        xh = xh_ref[...]
        h1 = (xh * g_ref[...] + b_ref[...]).astype(BF16)
        dq_t = dq_ref[...]
        dz_t = dz_ref[...]
        dkv = jnp.concatenate([heads_gradient(dkd_ref, carry_refs[:per_tile]),
                               heads_gradient(dvd_ref, carry_refs[per_tile:])], axis=1).astype(BF16)
        dh1 = ALPHA * dr2_ref[...]
        dh1 += _dot_nt(dq_t, win_ref[:, 0:B_WIDTH])
        dh1 += _dot_nt(dz_t, win_ref[:, B_WIDTH:2 * B_WIDTH])
        dh1 += _dot_nt(dkv, wkv_ref[...])
        acc_in[:, 0:B_WIDTH] += _dot_tn(h1, dq_t)
        acc_in[:, B_WIDTH:2 * B_WIDTH] += _dot_tn(h1, dz_t)
        acc_kv[...] += _dot_tn(h1, dkv)
        dg_ref[...] += jnp.sum(dh1 * xh, axis=0, keepdims=True)
        db_ref[...] += jnp.sum(dh1, axis=0, keepdims=True)
        dr1_ref[...] = _ln_bwd(dh1 * g_ref[...], xh, rstd_ref[...])

        @pl.when(i == n_steps - 1)
        def _():
            half_rows = D_MODEL // 2
            shard_cols = 2 * B_WIDTH // N_CHIPS
            for s in range(N_CHIPS):
                for c in range(2):
                    gwin_ref[2 * s + c] = acc_in[c * half_rows:(c + 1) * half_rows,
                                                 s * shard_cols:(s + 1) * shard_cols].astype(BF16)
            gwkv_ref[...] = acc_kv[...].astype(BF16)

    vec = jax.ShapeDtypeStruct((1, D_MODEL), F32)
    gwin_shape = (N_DEV, D_MODEL // 2, 2 * B_WIDTH // N_CHIPS)

    def carry_spec(a):
        return pl.BlockSpec((1, CHUNK, 2 * PAIR), lambda i: (jnp.minimum(per_tile * i + a + 1, n_carry - 1), 0, 0))

    carry_specs = [carry_spec(a) for a in range(per_tile)]
    return pl.pallas_call(
        body, name="layer_b_bwd_proj", grid=(n_steps,),
        in_specs=[_rows(tm, D_MODEL), _rows(tm, 1), _const(g1.shape), _const(b1.shape), _rows(tm, D_MODEL),
                  _rows(tm, B_WIDTH), _rows(tm, B_WIDTH), _rows(tm, 2 * PAIR), _rows(tm, 2 * PAIR)]
        + carry_specs + carry_specs + [_resident(w_in.shape), _resident(w_kv.shape)],
        out_specs=[_rows(tm, D_MODEL), _const((1, D_MODEL)), _const((1, D_MODEL)), _const(gwin_shape),
                   _const(w_kv.shape)],
        out_shape=[jax.ShapeDtypeStruct((t_len, D_MODEL), F32), vec, vec,
                   jax.ShapeDtypeStruct(gwin_shape, BF16), jax.ShapeDtypeStruct(w_kv.shape, BF16)],
        scratch_shapes=[pltpu.VMEM(w_in.shape, F32), pltpu.VMEM(w_kv.shape, F32)],
        compiler_params=_params(),
    )(xh1, rstd1, g1, b1, dr2, dq, dzb, dkd, dvd, *([carry_k] * per_tile), *([carry_v] * per_tile), w_in, w_kv)


def _layer_a_bwd_mix(dr1, u, vh, z, y, rv, w_out, lng, lnb, ws, bsp_t, after):
    t_len = u.shape[0]
    tm = TM_ATTN
    n_steps = t_len // tm

    def body(dr_ref, u_ref, vh_ref, z_ref, y_ref, rv_ref, wout_ref, lng_ref, lnb_ref, ws_ref, bsp_ref, after_ref,
             dp_ref, gw_ref, dws_ref, dbsp_ref, dgs_ref, dbs_ref, dvn_scr, gw_acc):
        i = pl.program_id(0)

        @pl.when(i == 0)
        def _():
            gw_acc[...] = jnp.zeros_like(gw_acc)
            dws_ref[...] = jnp.zeros_like(dws_ref)
            dbsp_ref[...] = jnp.zeros_like(dbsp_ref)
            dgs_ref[...] = jnp.zeros_like(dgs_ref)
            dbs_ref[...] = jnp.zeros_like(dbs_ref)

        drb = dr_ref[...].astype(BF16)

        def group_cols(g):
            return slice(g * A_GROUP_DIM, (g + 1) * A_GROUP_DIM)

        tri = (lax.broadcasted_iota(jnp.int32, (CHUNK, CHUNK), 0)
               >= lax.broadcasted_iota(jnp.int32, (CHUNK, CHUNK), 1))
        lane = lax.broadcasted_iota(jnp.int32, (CHUNK, CHUNK), 1)
        ones = jnp.ones((CHUNK, A_GROUP_DIM), BF16)
        dbsp = jnp.zeros((CHUNK, CHUNK), F32)
        dy_next = _dot_nt(drb, wout_ref[group_cols(0), :])
        for g in range(A_GROUPS):
            wsg = jnp.where(tri, ws_ref[g], 0.0).astype(BF16)
            cols = group_cols(g)
            cols_z = slice(2 * A_WIDTH + g * A_GROUP_DIM, 2 * A_WIDTH + (g + 1) * A_GROUP_DIM)
            dy_g = dy_next
            if g + 1 < A_GROUPS:
                dy_next = _dot_nt(drb, wout_ref[group_cols(g + 1), :])
            gw_acc[cols, :] += _dot_tn(y_ref[:, cols], drb)
            both = jnp.zeros((CHUNK, 2 * CHUNK), F32)
            for ci in range(tm // CHUNK):
                rows = slice(ci * CHUNK, (ci + 1) * CHUNK)
                vn = (vh_ref[rows, cols].astype(F32) * lng_ref[:, cols] + lnb_ref[:, cols]).astype(BF16)
                s = _dot(wsg, vn) + bsp_ref[:, g:g + 1]
                sz, dsz = _silu_parts(z_ref[rows, cols].astype(F32))
                dy = dy_g[rows]
                t = dy * u_ref[rows, cols].astype(F32)
                dp_ref[rows, cols] = (dy * (s * sz)).astype(BF16)
                dp_ref[rows, cols_z] = (t * s * dsz).astype(BF16)
                ds_b = (t * sz).astype(BF16)
                both += _dot_nt(ds_b, jnp.concatenate([vn, ones], axis=0))
                dvn_scr[rows, cols] = _dot_tn(wsg, ds_b)
            dws_ref[g] += jnp.where(tri, both[:, 0:CHUNK], 0.0)
            dbsp = jnp.where(lane == g, both[:, CHUNK:2 * CHUNK], dbsp)
        dbsp_ref[...] += dbsp
        dvn = dvn_scr[...]
        vh_t = vh_ref[...].astype(F32)
        dgs_ref[...] += jnp.sum(dvn * vh_t, axis=0, keepdims=True)
        dbs_ref[...] += jnp.sum(dvn, axis=0, keepdims=True)
        dp_ref[:, A_WIDTH:2 * A_WIDTH] = _ln_bwd(dvn * lng_ref[...], vh_t, rv_ref[...]).astype(BF16)

        @pl.when(i == n_steps - 1)
        def _():
            gw_ref[...] = gw_acc[...].astype(BF16)

    wide = jax.ShapeDtypeStruct((1, A_WIDTH), F32)
    return pl.pallas_call(
        body, name="layer_a_bwd_mix", grid=(n_steps,),
        in_specs=[_rows(tm, D_MODEL), _rows(tm, A_WIDTH), _rows(tm, A_WIDTH), _rows(tm, A_WIDTH), _rows(tm, A_WIDTH),
                  _rows(tm, 1), _resident(w_out.shape), _const(lng.shape), _const(lnb.shape), _const(ws.shape),
                  _const(bsp_t.shape), _const(after.shape)],
        out_specs=[_rows(tm, 3 * A_WIDTH), _const(w_out.shape), _const(ws.shape), _const((CHUNK, CHUNK)),
                   _const((1, A_WIDTH)), _const((1, A_WIDTH))],
        out_shape=[jax.ShapeDtypeStruct((t_len, 3 * A_WIDTH), BF16), jax.ShapeDtypeStruct(w_out.shape, BF16),
                   jax.ShapeDtypeStruct(ws.shape, F32), jax.ShapeDtypeStruct((CHUNK, CHUNK), F32),
                   wide, wide],
        scratch_shapes=[pltpu.VMEM((tm, A_WIDTH), F32), pltpu.VMEM(w_out.shape, F32)],
        compiler_params=_params(),
    )(dr1, u, vh, z, y, rv, w_out, lng, lnb, ws, bsp_t, after)


def _layer_a_bwd_dx(dr1, dp, w_in, after, updates=()):
    t_len = dr1.shape[0]
    tm = TM_MM
    n_steps = t_len // tm
    n_upd = len(updates)

    def body(dr_ref, dp_ref, win_ref, after_ref, *refs):
        upd_in, dx_ref, upd_out = refs[:4 * n_upd], refs[4 * n_upd], refs[4 * n_upd + 1:]
        dx_ref[...] = ALPHA * dr_ref[...] + _dot_nt(dp_ref[...], win_ref[...])
        for k in range(n_upd):
            w_ref, g_ref, m_ref, v_ref = upd_in[4 * k:4 * k + 4]
            g_out, d_ref, nm_ref, nv_ref = upd_out[4 * k:4 * k + 4]
            g_out[...] = g_ref[...]
            _adamw_update(w_ref, g_ref, m_ref, v_ref, d_ref, nm_ref, nv_ref)

    upd_specs, upd_shapes, upd_args = [], [], []
    for w, g, m, v in updates:
        rows, cols = w.shape
        upd_specs.append(pl.BlockSpec((rows // n_steps, cols), lambda i: (i, 0)))
        upd_shapes.append(jax.ShapeDtypeStruct((rows, cols), F32))
        upd_args += [w, g, m, v]
    return pl.pallas_call(
        body, name="layer_a_bwd_dx", grid=(n_steps,),
        in_specs=[_rows(tm, D_MODEL), _rows(tm, 3 * A_WIDTH), _resident(w_in.shape), _const(after.shape)]
        + [s for s in upd_specs for _ in range(4)],
        out_specs=[_rows(tm, D_MODEL)] + [s for s in upd_specs for _ in range(4)],
        out_shape=[jax.ShapeDtypeStruct((t_len, D_MODEL), F32)] + [s for s in upd_shapes for _ in range(4)],
        compiler_params=_params(),
    )(dr1, dp, w_in, after, *upd_args)


def _layer_a_bwd_win(xt, dp, after):
    t_len = xt.shape[1]
    tm = TM_WIN
    n_steps = t_len // tm
    shard_cols = 3 * A_WIDTH // N_CHIPS
    half_rows = D_MODEL // 2

    def body(xt_ref, dp_ref, after_ref, gw_ref, acc):
        i = pl.program_id(1)

        @pl.when(i == 0)
        def _():
            acc[...] = jnp.zeros_like(acc)

        acc[...] += _dot(xt_ref[...], dp_ref[...])

        @pl.when(i == n_steps - 1)
        def _():
            for c in range(2):
                gw_ref[0, c] = acc[c * half_rows:(c + 1) * half_rows, :].astype(BF16)

    return pl.pallas_call(
        body, name="layer_a_bwd_win", grid=(N_CHIPS, n_steps),
        in_specs=[pl.BlockSpec((D_MODEL, tm), lambda j, i: (0, i)),
                  pl.BlockSpec((tm, shard_cols), lambda j, i: (i, j)), _const(after.shape)],
        out_specs=pl.BlockSpec((1, 2, half_rows, shard_cols), lambda j, i: (j, 0, 0, 0)),
        out_shape=jax.ShapeDtypeStruct((N_CHIPS, 2, half_rows, shard_cols), BF16),
        scratch_shapes=[pltpu.VMEM((D_MODEL, shard_cols), F32)],
        compiler_params=_params(("arbitrary", "arbitrary")),
    )(xt, dp, after)


def _bucket_onehot():
    dist = jnp.arange(CHUNK, dtype=jnp.int32)[None, :]
    max_exact = REL_BUCKETS // 2
    df = jnp.maximum(dist, 1).astype(F32)
    large = max_exact + (jnp.log(df / max_exact) / math.log(CHUNK / max_exact)
                         * (REL_BUCKETS - max_exact)).astype(jnp.int32)
    bucket = jnp.where(dist < max_exact, dist, jnp.minimum(large, REL_BUCKETS - 1))
    onehot = bucket == jnp.arange(REL_BUCKETS, dtype=jnp.int32)[:, None]
    return onehot.astype(F32)


def _bias_expand(rel_t, onehot):
    def body(rel_ref, oh_ref, out_ref):
        by_distance = jnp.dot(rel_ref[...], oh_ref[...], preferred_element_type=F32,
                              precision=lax.Precision.HIGHEST)
        for h in range(N_HEADS):
            rows = jnp.broadcast_to(by_distance[h:h + 1, :], (2 * CHUNK, CHUNK))
            out_ref[h] = pltpu.roll(rows, 0, 1, stride=1, stride_axis=0)

    return pl.pallas_call(
        body, name="bias_expand",
        out_shape=jax.ShapeDtypeStruct((N_HEADS, 2 * CHUNK, CHUNK), F32),
    )(rel_t, onehot)


def _bias_reduce(oh_ref, db_ref):
    sublane = lax.broadcasted_iota(jnp.int32, (_SUBLANES, CHUNK), 0)
    rows = []
    for h in range(N_HEADS):
        part = db_ref[h, 0:_SUBLANES, :]
        for a in range(1, 2 * CHUNK // _SUBLANES):
            tile = db_ref[h, a * _SUBLANES:(a + 1) * _SUBLANES, :]
            back = (-a * _SUBLANES) % CHUNK
            part += pltpu.roll(tile, back, 1) if back else tile
        total = jnp.where(sublane == 0, part, 0.0)
        for s in range(1, _SUBLANES):
            total += jnp.where(sublane == s, pltpu.roll(part, CHUNK - s, 1), 0.0)
        rows.append(jnp.sum(total, axis=0, keepdims=True))
    by_distance = jnp.concatenate(rows, axis=0)
    return lax.dot_general(oh_ref[...], by_distance, (((1,), (1,)), ((), ())),
                           preferred_element_type=F32, precision=lax.Precision.HIGHEST)


_SMALL_SHAPES = dict(w_spatial=(A_GROUPS, CHUNK, CHUNK), b_spatial=(A_GROUPS, CHUNK), attn_sinks=(1, N_HEADS),
                     rel_bias=(REL_BUCKETS, N_HEADS), post_ln_g=(2, D_MODEL), post_ln_b=(2, D_MODEL),
                     sgu_ln_g=(1, A_WIDTH), sgu_ln_b=(1, A_WIDTH), loss=(1, 1))
_SMALL_ORDER = tuple(_SMALL_SHAPES)


def _small_rows(name):
    shape = _SMALL_SHAPES[name]
    rows = math.prod(shape[:-1]) if shape[-1] < _LANES else math.prod(shape) // _LANES
    return -(-rows // _SUBLANES) * _SUBLANES


def _small_offset(name):
    return sum(_small_rows(n) for n in _SMALL_ORDER[:_SMALL_ORDER.index(name)])


def _pack_small(dws, dbsp, dsink, dbias, onehot, post_g, post_b, dgs, dbs, loss_vec):
    def body(dws_ref, dbsp_ref, dsink_ref, db_ref, oh_ref, g1_ref, g2_ref, b1_ref, b2_ref, dgs_ref, dbs_ref,
             loss_ref, out_ref):
        out_ref[...] = jnp.zeros_like(out_ref)

        def put_flat(name, refs):
            row = _small_offset(name)
            for ref in refs:
                for k in range(ref.shape[1] // _LANES):
                    out_ref[row:row + 1, :] = ref[:, k * _LANES:(k + 1) * _LANES]
                    row += 1

        row = _small_offset("w_spatial")
        for g in range(A_GROUPS):
            out_ref[row + g * CHUNK:row + (g + 1) * CHUNK, :] = dws_ref[g]
        row = _small_offset("b_spatial")
        out_ref[row:row + A_GROUPS, :] = dbsp_ref[...].T[0:A_GROUPS, :]
        lane = lax.broadcasted_iota(jnp.int32, (1, _LANES), 1)
        sinks = jnp.zeros((1, _LANES), F32)
        for h in range(N_HEADS):
            per_query = dsink_ref[h // GROUP:h // GROUP + 1, (h % GROUP) * CHUNK:(h % GROUP + 1) * CHUNK]
            sinks = jnp.where(lane == h, jnp.sum(per_query, axis=1, keepdims=True), sinks)
        row = _small_offset("attn_sinks")
        out_ref[row:row + 1, :] = sinks
        row = _small_offset("rel_bias")
        out_ref[row:row + REL_BUCKETS, 0:N_HEADS] = _bias_reduce(oh_ref, db_ref)
        put_flat("post_ln_g", [g1_ref, g2_ref])
        put_flat("post_ln_b", [b1_ref, b2_ref])
        put_flat("sgu_ln_g", [dgs_ref])
        put_flat("sgu_ln_b", [dbs_ref])
        row = _small_offset("loss")
        out_ref[row:row + 1, 0:1] = (0.5 / D_MODEL) * jnp.sum(loss_ref[...], axis=1, keepdims=True)

    total_rows = sum(_small_rows(n) for n in _SMALL_ORDER)
    return pl.pallas_call(
        body, name="pack_small",
        out_shape=jax.ShapeDtypeStruct((total_rows, _LANES), F32),
    )(dws, dbsp, dsink, dbias, onehot, *post_g, *post_b, dgs, dbs, loss_vec)


def _place():
    return lax.axis_index("x"), lax.axis_index("y"), lax.axis_index("c")


RELAY_PIECES = 4


def _shard_window(full_ref, shard_shape, col_sharded, s, half, piece=None):
    rows, cols = shard_shape
    if half is None:
        start, size = 0, rows
    elif piece is None:
        start, size = half * (rows // 2), rows // 2
    else:
        size = rows // 2 // RELAY_PIECES
        start = (half * RELAY_PIECES + piece) * size
    if col_sharded:
        return full_ref.at[pl.ds(start, size), pl.ds(s * cols, cols)]
    return full_ref.at[pl.ds(s * rows + start, size), :]


def _other_chips(x, y):
    return [(1 - x, y), (x, 1 - y), (1 - x, 1 - y)]


def _gather_weights(shards, col_sharded, fetch, ln_shard):
    n_w = len(shards)
    fetched = [w for w in range(n_w) if fetch[w]]
    full_shapes = []
    for w, cs in zip(shards, col_sharded):
        r, c = w.shape
        full_shapes.append((r, c * N_CHIPS) if cs else (r * N_CHIPS, c))

    def body(*refs):
        in_refs = refs[:n_w]
        ln_ref = refs[n_w]
        full_refs = refs[n_w + 1:2 * n_w + 1]
        ln_full = refs[2 * n_w + 1]
        raw = refs[2 * n_w + 2:3 * n_w + 2]
        stage = refs[3 * n_w + 2:4 * n_w + 2]
        send_sems, recv_sems, load_sems, local_sems, ln_send, ln_recv = refs[4 * n_w + 2:]
        x, y, c = _place()
        s_me = 2 * x + y
        chips = _other_chips(x, y)
        pieces = range(RELAY_PIECES)

        def shard_window(w, s, half, piece=None):
            return _shard_window(full_refs[w], shards[w].shape, col_sharded[w], s, half, piece)

        def piece_rows(w, half, piece):
            rows = shards[w].shape[0] // 2 // RELAY_PIECES
            return pl.ds(pl.multiple_of((half * RELAY_PIECES + piece) * rows, rows), rows)

        def ici_copy(w, k, sender_shard, piece):
            idx = (w * 3 + k) * RELAY_PIECES + piece
            return pltpu.make_async_remote_copy(
                src_ref=stage[w].at[piece_rows(w, c, piece), :], dst_ref=shard_window(w, sender_shard, c, piece),
                send_sem=send_sems.at[idx], recv_sem=recv_sems.at[idx],
                device_id=(*chips[k], c), device_id_type=MESH)

        def d2d_copy(w, k, half, piece):
            s_k = 2 * chips[k][0] + chips[k][1]
            win = shard_window(w, s_k, half, piece)
            idx = (3 * n_w + w * 3 + k) * RELAY_PIECES + piece
            return pltpu.make_async_remote_copy(
                src_ref=win, dst_ref=win, send_sem=send_sems.at[idx], recv_sem=recv_sems.at[idx],
                device_id=(x, y, 1 - c), device_id_type=MESH)

        def ln_copy(k, slot):
            return pltpu.make_async_remote_copy(
                src_ref=ln_ref, dst_ref=ln_full.at[slot], send_sem=ln_send.at[k], recv_sem=ln_recv.at[k],
                device_id=(*chips[k], c), device_id_type=MESH)

        loads = []

        def load(w, rows):
            window = (rows, slice(None)) if rows is not None else (slice(None), slice(None))
            cp = pltpu.make_async_copy(in_refs[w].at[window], raw[w].at[window], load_sems.at[len(loads)])
            cp.start()
            loads.append((cp, w, window))

        for half in (c, 1 - c):
            for w in fetched:
                for q in pieces:
                    load(w, piece_rows(w, half, q))
        for w in range(n_w):
            if not fetch[w]:
                load(w, None)

        def to_bf16(k):
            cp, w, window = loads[k]
            cp.wait()
            stage[w][window] = raw[w][window].astype(BF16)

        ln_full[s_me] = ln_ref[...]
        def shard_of(k):
            return 2 * chips[k][0] + chips[k][1]

        relay_from = jnp.where(c == 0, shard_of(0), shard_of(1))
        relay_to = (jnp.where(c == 0, x, 1 - x), jnp.where(c == 0, 1 - y, y), c)

        def relay_copy(w, sender_shard, piece):
            win = shard_window(w, sender_shard, c, piece)
            idx = (w * 3 + 2) * RELAY_PIECES + piece
            return pltpu.make_async_remote_copy(
                src_ref=win, dst_ref=win, send_sem=send_sems.at[idx], recv_sem=recv_sems.at[idx],
                device_id=relay_to, device_id_type=MESH)

        first = [ln_copy(k, s_me) for k in range(3)]
        for cp in first:
            cp.start()
        n_sent = 0
        for w in fetched:
            for q in pieces:
                to_bf16(n_sent)
                n_sent += 1
                for k in range(2):
                    cp = ici_copy(w, k, s_me, q)
                    cp.start()
                    first.append(cp)
        for k in range(n_sent, len(loads)):
            to_bf16(k)
        own = [pltpu.make_async_copy(stage[w], shard_window(w, s_me, None), local_sems.at[w]) for w in range(n_w)]
        for cp in own:
            cp.start()
        passed = []
        for w in fetched:
            for q in pieces:
                for k in range(2):
                    ici_copy(w, k, shard_of(k), q).wait_recv()
                relay = relay_copy(w, relay_from, q)
                relay.start()
                passed.append(relay)
                for k in range(2):
                    fwd = d2d_copy(w, k, c, q)
                    fwd.start()
                    passed.append(fwd)
        for w in fetched:
            for q in pieces:
                relay_copy(w, shard_of(2), q).wait_recv()
                fwd = d2d_copy(w, 2, c, q)
                fwd.start()
                passed.append(fwd)
        for w in fetched:
            for k in range(3):
                for q in pieces:
                    d2d_copy(w, k, 1 - c, q).wait_recv()
        for k in range(3):
            ln_copy(k, 2 * chips[k][0] + chips[k][1]).wait_recv()
        for cp in first + passed:
            cp.wait_send()
        for cp in own:
            cp.wait()

    vmem = pl.BlockSpec(memory_space=pltpu.VMEM)
    hbm = pl.BlockSpec(memory_space=pl.ANY)
    return pl.pallas_call(
        body, name="gather_weights",
        in_specs=[hbm] * n_w + [vmem],
        out_specs=[hbm] * n_w + [vmem],
        out_shape=[jax.ShapeDtypeStruct(s, BF16) for s in full_shapes]
        + [jax.ShapeDtypeStruct((N_CHIPS,) + ln_shard.shape, F32)],
        scratch_shapes=[pltpu.VMEM(w.shape, F32) for w in shards] + [pltpu.VMEM(w.shape, BF16) for w in shards]
        + [pltpu.SemaphoreType.DMA((6 * RELAY_PIECES * n_w,)), pltpu.SemaphoreType.DMA((6 * RELAY_PIECES * n_w,)),
           pltpu.SemaphoreType.DMA((2 * RELAY_PIECES * len(fetched) + n_w - len(fetched),)),
           pltpu.SemaphoreType.DMA((n_w,)), pltpu.SemaphoreType.DMA((3,)), pltpu.SemaphoreType.DMA((3,))],
        compiler_params=pltpu.CompilerParams(vmem_limit_bytes=VMEM_LIMIT),
    )(*shards, ln_shard)


def _fetch_copy(full_ref, shard_shape, col_sharded, sender_shard, send_sems, recv_sems, idx, chip, c):
    win = _shard_window(full_ref, shard_shape, col_sharded, sender_shard, None)
    return pltpu.make_async_remote_copy(src_ref=win, dst_ref=win, send_sem=send_sems.at[idx],
                                        recv_sem=recv_sems.at[idx], device_id=(*chip, c), device_id_type=MESH)


def _fetch_start(fulls, shard_shapes, col_sharded):
    n = len(fulls)

    def body(*refs):
        full = refs[:n]
        send_sems, recv_sems = refs[n], refs[n + 1]
        token = refs[-1]
        x, y, c = _place()
        for w in range(n):
            for k, chip in enumerate(_other_chips(x, y)):
                _fetch_copy(full[w], shard_shapes[w], col_sharded[w], 2 * x + y, send_sems, recv_sems, w * 3 + k,
                            chip, c).start()
        token[...] = jnp.zeros_like(token)

    outs = pl.pallas_call(
        body, name="fetch_start",
        out_shape=(pltpu.SemaphoreType.DMA((3 * n,)), pltpu.SemaphoreType.DMA((3 * n,)),
                   *[pltpu.HBM(f.shape, f.dtype) for f in fulls], jax.ShapeDtypeStruct((8, 128), F32)),
        in_specs=[_HBM] * n,
        out_specs=(_SEM, _SEM, *([_HBM] * n), pl.BlockSpec(memory_space=pltpu.VMEM)),
        input_output_aliases={i: 2 + i for i in range(n)},
        compiler_params=pltpu.CompilerParams(has_side_effects=pltpu.SideEffectType.DATAFLOW_SIDE_EFFECTING),
    )(*[pltpu.with_memory_space_constraint(f, pltpu.HBM) for f in fulls])
    return dict(send=outs[0], recv=outs[1], full=list(outs[2:2 + n])), outs[-1]


def _fetch_wait(group, shard_shapes, col_sharded, after):
    n = len(group["full"])

    def body(*refs):
        full = refs[:n]
        send_sems, recv_sems = refs[n], refs[n + 1]
        x, y, c = _place()
        for w in range(n):
            for k, chip in enumerate(_other_chips(x, y)):
                _fetch_copy(full[w], shard_shapes[w], col_sharded[w], 2 * x + y, send_sems, recv_sems, w * 3 + k,
                            chip, c).wait_send()
                _fetch_copy(full[w], shard_shapes[w], col_sharded[w], 2 * chip[0] + chip[1], send_sems, recv_sems,
                            w * 3 + k, chip, c).wait_recv()

    outs = pl.pallas_call(
        body, name="fetch_wait", out_shape=tuple(pltpu.HBM(f.shape, f.dtype) for f in group["full"]),
        in_specs=[_HBM] * n + [_SEM, _SEM, pl.BlockSpec(memory_space=pl.ANY)],
        out_specs=tuple([_HBM] * n), input_output_aliases={i: i for i in range(n)},
        compiler_params=pltpu.CompilerParams(has_side_effects=pltpu.SideEffectType.DATAFLOW_SIDE_EFFECTING),
    )(*group["full"], group["send"], group["recv"], after)
    return list(outs)


_HBM = pl.BlockSpec(memory_space=pltpu.HBM)
_SEM = pl.BlockSpec(memory_space=pltpu.SEMAPHORE)
_N_PEER = N_DEV - 1


def _peer(x, y, c, k):
    return (x + (k >> 2)) % 2, (y + ((k >> 1) & 1)) % 2, (c + (k & 1)) % 2


def _exchange_copy(src_ref, land_ref, sliced, send_sems, recv_sems, idx, x, y, c, k):
    px, py, pc = _peer(x, y, c, k)
    src = src_ref.at[4 * px + 2 * py + pc] if sliced else src_ref
    return pltpu.make_async_remote_copy(
        src_ref=src, dst_ref=land_ref.at[4 * x + 2 * y + c],
        send_sem=send_sems.at[idx], recv_sem=recv_sems.at[idx], device_id=(px, py, pc), device_id_type=MESH)


def _exchange_start(tag, arrays, sliced):
    n = len(arrays)
    lands = [lax.empty(a.shape if s else (N_DEV,) + a.shape, a.dtype) for a, s in zip(arrays, sliced)]

    def body(*refs):
        src, land = refs[:n], refs[n:2 * n]
        send_sems, recv_sems = refs[2 * n], refs[2 * n + 1]
        token = refs[-1]
        x, y, c = _place()
        for w in range(n):
            for k in range(1, N_DEV):
                _exchange_copy(src[w], land[w], sliced[w], send_sems, recv_sems, w * _N_PEER + k - 1, x, y, c, k).start()
        token[...] = jnp.zeros_like(token)

    outs = pl.pallas_call(
        body, name="exchange_start_" + tag,
        out_shape=(pltpu.SemaphoreType.DMA((n * _N_PEER,)), pltpu.SemaphoreType.DMA((n * _N_PEER,)),
                   *[pltpu.HBM(a.shape, a.dtype) for a in arrays], *[pltpu.HBM(l.shape, l.dtype) for l in lands],
                   jax.ShapeDtypeStruct((8, 128), F32)),
        in_specs=[_HBM] * (2 * n),
        out_specs=(_SEM, _SEM, *([_HBM] * (2 * n)), pl.BlockSpec(memory_space=pltpu.VMEM)),
        input_output_aliases={i: 2 + i for i in range(2 * n)},
        compiler_params=pltpu.CompilerParams(has_side_effects=pltpu.SideEffectType.DATAFLOW_SIDE_EFFECTING),
    )(*[pltpu.with_memory_space_constraint(a, pltpu.HBM) for a in arrays],
      *[pltpu.with_memory_space_constraint(l, pltpu.HBM) for l in lands])
    return dict(send=outs[0], recv=outs[1], src=list(outs[2:2 + n]), land=list(outs[2 + n:2 + 2 * n]),
                sliced=list(sliced)), outs[-1]


def _exchange_wait(tag, groups, after):
    counts = [len(g["src"]) for g in groups]
    total = sum(counts)

    def body(*refs):
        pos = 0
        x, y, c = _place()
        for g, n in zip(groups, counts):
            src, land = refs[pos:pos + n], refs[pos + n:pos + 2 * n]
            send_sems, recv_sems = refs[pos + 2 * n], refs[pos + 2 * n + 1]
            pos += 2 * n + 2
            for w in range(n):
                for k in range(1, N_DEV):
                    cp = _exchange_copy(src[w], land[w], g["sliced"][w], send_sems, recv_sems,
                                        w * _N_PEER + k - 1, x, y, c, k)
                    cp.wait_send()
                    cp.wait_recv()

    operands, in_specs, aliases, out_shape = [], [], {}, []
    for g in groups:
        for a in g["src"] + g["land"]:
            aliases[len(operands)] = len(out_shape)
            out_shape.append(pltpu.HBM(a.shape, a.dtype))
            operands.append(a)
            in_specs.append(_HBM)
        operands += [g["send"], g["recv"]]
        in_specs += [_SEM, _SEM]
    operands.append(after)
    in_specs.append(pl.BlockSpec(memory_space=pl.ANY))
    outs = pl.pallas_call(
        body, name="exchange_wait_" + tag, out_shape=tuple(out_shape), in_specs=in_specs,
        out_specs=tuple([_HBM] * (2 * total)), input_output_aliases=aliases,
        compiler_params=pltpu.CompilerParams(has_side_effects=pltpu.SideEffectType.DATAFLOW_SIDE_EFFECTING),
    )(*operands)
    srcs, lands, pos = [], [], 0
    for n in counts:
        srcs += list(outs[pos:pos + n])
        lands += list(outs[pos + n:pos + 2 * n])
        pos += 2 * n
    return srcs, lands


def _sum_and_swap(tag, pieces, lands, small=None, small_land=None):
    n_w = len(pieces)
    n_small = 0 if small is None else 1

    def body(*refs):
        g_refs, land_refs = refs[:n_w], refs[n_w:2 * n_w]
        pos = 2 * n_w + 2 * n_small
        out_refs = refs[pos:pos + n_w]
        pos += n_w + n_small
        bufs = refs[pos:pos + n_w]
        load_sems, swap_send, swap_recv = refs[pos + n_w + 2 * n_small:]
        x, y, c = _place()
        me = 4 * x + 2 * y + c

        def slot(k):
            px, py, pc = _peer(x, y, c, k)
            return 4 * px + 2 * py + pc

        def swap_copy(w, half):
            rows = pieces[w].shape[1]
            win = out_refs[w].at[pl.ds(pl.multiple_of(half * rows, rows), rows), :]
            return pltpu.make_async_remote_copy(
                src_ref=win, dst_ref=win, send_sem=swap_send.at[w], recv_sem=swap_recv.at[w],
                device_id=(x, y, 1 - c), device_id_type=MESH)

        loads = []
        for w in range(n_w):
            per_w = [pltpu.make_async_copy(g_refs[w].at[me], bufs[w].at[me], load_sems.at[w * N_DEV])]
            per_w += [pltpu.make_async_copy(land_refs[w].at[slot(k)], bufs[w].at[slot(k)], load_sems.at[w * N_DEV + k])
                      for k in range(1, N_DEV)]
            loads.append(per_w)
        small_loads = []
        if n_small:
            small_ref, small_land_ref = refs[2 * n_w], refs[2 * n_w + 1]
            small_out = refs[2 * n_w + 2 + n_w]
            small_buf, small_sems = refs[pos + n_w], refs[pos + n_w + 1]
            small_loads = [pltpu.make_async_copy(small_land_ref.at[slot(k)], small_buf.at[slot(k)],
                                                 small_sems.at[k - 1]) for k in range(1, N_DEV)]
        for cp in [cp for per_w in loads for cp in per_w] + small_loads:
            cp.start()
        if n_small:
            small_buf[me] = small_ref[...]
        swaps = []
        for w in range(n_w):
            for cp in loads[w]:
                cp.wait()
            rows = pieces[w].shape[1]
            total = bufs[w][0].astype(F32)
            for p in range(1, N_DEV):
                total += bufs[w][p].astype(F32)
            out_refs[w][pl.ds(pl.multiple_of(c * rows, rows), rows), :] = total
            sw = swap_copy(w, c)
            sw.start()
            swaps.append(sw)
        if n_small:
            for cp in small_loads:
                cp.wait()
            total = small_buf[0]
            for p in range(1, N_DEV):
                total += small_buf[p]
            small_out[...] = total
        for w in range(n_w):
            swap_copy(w, 1 - c).wait_recv()
        for sw in swaps:
            sw.wait_send()

    vmem = pl.BlockSpec(memory_space=pltpu.VMEM)
    hbm = pl.BlockSpec(memory_space=pl.ANY)
    small_args = [small, small_land] if n_small else []
    small_shapes = [jax.ShapeDtypeStruct(small.shape, F32)] if n_small else []
    small_scratch = ([pltpu.VMEM((N_DEV,) + small.shape, F32), pltpu.SemaphoreType.DMA((_N_PEER,))]
                     if n_small else [])
    return pl.pallas_call(
        body, name="sum_and_swap_" + tag,
        in_specs=[hbm] * (2 * n_w) + [vmem, hbm] * n_small,
        out_specs=[vmem] * (n_w + n_small),
        out_shape=[jax.ShapeDtypeStruct((2 * p.shape[1], p.shape[2]), F32) for p in pieces] + small_shapes,
        scratch_shapes=[pltpu.VMEM(p.shape, BF16) for p in pieces] + small_scratch
        + [pltpu.SemaphoreType.DMA((n_w * N_DEV,)), pltpu.SemaphoreType.DMA((n_w,)),
           pltpu.SemaphoreType.DMA((n_w,))],
        compiler_params=pltpu.CompilerParams(vmem_limit_bytes=VMEM_LIMIT),
    )(*pieces, *lands, *small_args)


def _adamw_values(w, g_t, m, v):
    c1 = 1.0 - ADAM_B1 ** ADAM_STEP
    c2 = 1.0 - ADAM_B2 ** ADAM_STEP
    nm = ADAM_B1 * m + (1.0 - ADAM_B1) * g_t
    nv = ADAM_B2 * v + (1.0 - ADAM_B2) * (g_t * g_t)
    return -ADAM_LR * ((nm / c1) / (jnp.sqrt(nv / c2) + ADAM_EPS) + ADAM_WD * w), nm, nv


def _adamw_update(w_ref, g_ref, m_ref, v_ref, d_ref, nm_ref, nv_ref):
    d_ref[...], nm_ref[...], nv_ref[...] = _adamw_values(w_ref[...], g_ref[...], m_ref[...], v_ref[...])


def _adamw_small(packed, shard_index, names, weights, moments_m, moments_v):
    n = len(names)
    shapes = [weights[name].shape for name in names]
    flat = [a[name].reshape(-1, a[name].shape[-1]) for name in names for a in (weights, moments_m, moments_v)]

    def body(packed_ref, shard_ref, *refs):
        loss_row = _small_offset("loss")
        refs[-1][...] = packed_ref[loss_row:loss_row + 1, 0:1]
        for k, name in enumerate(names):
            w_ref, m_ref, v_ref = refs[3 * k:3 * k + 3]
            g_ref, d_ref, nm_ref, nv_ref = refs[3 * n + 4 * k:3 * n + 4 * k + 4]
            rows, cols = w_ref.shape
            first = _small_offset(name)
            if cols <= _LANES:
                blocks = [(slice(0, rows), packed_ref[first:first + rows, 0:cols])]
            else:
                per_row = cols // _LANES
                if cols < _SMALL_SHAPES[name][-1]:
                    first = first + shard_ref[0] * per_row
                blocks = [(slice(i, i + 1),
                           jnp.concatenate([packed_ref[pl.ds(first + i * per_row + j, 1), :] for j in range(per_row)],
                                           axis=1)) for i in range(rows)]
            for at, g_t in blocks:
                g_ref[at, :] = g_t
                d_ref[at, :], nm_ref[at, :], nv_ref[at, :] = _adamw_values(w_ref[at, :], g_t, m_ref[at, :],
                                                                           v_ref[at, :])

    vmem = pl.BlockSpec(memory_space=pltpu.VMEM)
    outs = pl.pallas_call(
        body, name="adamw_small",
        in_specs=[vmem, pl.BlockSpec(memory_space=pltpu.SMEM)] + [vmem] * (3 * n),
        out_shape=[jax.ShapeDtypeStruct(flat[3 * k].shape, F32) for k in range(n) for _ in range(4)]
        + [jax.ShapeDtypeStruct((1, 1), F32)],
    )(packed, shard_index.reshape(1).astype(jnp.int32), *flat)
    return [tuple(o.reshape(shapes[k]) for o in outs[4 * k:4 * k + 4]) for k in range(n)], outs[-1].reshape(())


def _adamw(label, w, g, m, v):
    shape = w.shape
    cols = shape[-1]
    rows = w.size // cols
    args = [a.reshape(rows, cols) for a in (w, g, m, v)]

    def body(w_ref, g_ref, m_ref, v_ref, g_out, d_ref, nm_ref, nv_ref):
        g_out[...] = g_ref[...]
        _adamw_update(w_ref, g_ref, m_ref, v_ref, d_ref, nm_ref, nv_ref)

    block_rows = 256 if rows % 256 == 0 and rows > 256 else rows
    spec = pl.BlockSpec((block_rows, cols), lambda i: (i, 0))
    outs = pl.pallas_call(
        body, name="adamw_" + label, grid=(rows // block_rows,),
        in_specs=[spec] * 4, out_specs=[spec] * 4,
        out_shape=[jax.ShapeDtypeStruct((rows, cols), F32)] * 4,
        compiler_params=_params(),
    )(*args)
    return [o.reshape(shape) for o in outs]


def _no_send(tag, arrays, sliced):
    return jnp.zeros((8, 128), F32)


def _local_step(x, tgt, w_in_a, later_weights, first_after, sgu_ln_g, sgu_ln_b, w_spatial, b_spatial,
                attn_sinks, rel_bias, post_ln_g, post_ln_b, send=_no_send):
    bsp_t = b_spatial.T
    g1, b1 = post_ln_g[0:1], post_ln_b[0:1]
    g2, b2 = post_ln_g[1:2], post_ln_b[1:2]
    onehot = _bucket_onehot()
    bias = _bias_expand(rel_bias.T, onehot)
    win = _window_tables()

    xt, u, vh, z, rv, y = _layer_a_fwd(x, w_in_a, sgu_ln_g, sgu_ln_b, w_spatial, bsp_t, first_after)
    w_out_a, w_kv, w_in_b, w_out_b = later_weights(y)
    xh1, rstd1, q, zb, kd, vd = _layer_b_proj(x, y, w_out_a, g1, b1, w_in_b, w_kv)
    o, probs, sink_probs, dr2, loss_vec, dg2, db2 = _layer_b_fwd(q, zb, kd, vd, bias, win, attn_sinks, xh1, g1, b1,
                                                                 w_out_b, g2, b2, tgt)
    dq, dzb, dkd, dvd, carry_k, carry_v, gw_out_b, dsink, dbias = _layer_b_bwd_attn(
        dr2, zb, o, q, kd, vd, probs, sink_probs, w_out_b)
    dr1, dg1, db1, gw_in_b, gw_kv = _layer_b_bwd_proj(xh1, rstd1, g1, b1, dr2, dq, dzb, dkd, dvd, carry_k, carry_v,
                                                      w_in_b, w_kv)
    gw_out_b = gw_out_b.reshape(N_DEV, -1, D_MODEL)
    gw_kv = gw_kv.reshape(N_DEV, -1, 2 * PAIR)
    after = send("b", [gw_out_b, gw_in_b, gw_kv], [True, True, True])
    dp, gw_out_a, dws, dbsp, dgs, dbs = _layer_a_bwd_mix(dr1, u, vh, z, y, rv, w_out_a, sgu_ln_g, sgu_ln_b,
                                                         w_spatial, bsp_t, after)
    gw_out_a = gw_out_a.reshape(N_DEV, -1, D_MODEL)
    small = _pack_small(dws, dbsp, dsink, dbias, onehot, (dg1, dg2), (db1, db2), dgs, dbs, loss_vec)
    after = send("a_out", [gw_out_a, small], [True, False])
    gw_in_a = _layer_a_bwd_win(xt, dp, after).reshape(N_DEV, D_MODEL // 2, -1)
    after = send("a_in", [gw_in_a], [True])
    after, updates = after if isinstance(after, tuple) else (after, ())
    grad_x, *updated = _layer_a_bwd_dx(dr1, dp, w_in_a, after, updates)

    pieces = [gw_in_a, gw_out_a, gw_kv, gw_in_b, gw_out_b]
    return grad_x, pieces, small, updated


def kernel(x, w_in_a, sgu_ln_g, sgu_ln_b, w_spatial, b_spatial, w_out_a, w_kv, w_in_b, attn_sinks, rel_bias, w_out_b, post_ln_g, post_ln_b, loss_target, m_w_in_a, m_sgu_ln_g, m_sgu_ln_b, m_w_spatial, m_b_spatial, m_w_out_a, m_w_kv, m_w_in_b, m_attn_sinks, m_rel_bias, m_w_out_b, m_post_ln_g, m_post_ln_b, v_w_in_a, v_sgu_ln_g, v_sgu_ln_b, v_w_spatial, v_b_spatial, v_w_out_a, v_w_kv, v_w_in_b, v_attn_sinks, v_rel_bias, v_w_out_b, v_post_ln_g, v_post_ln_b):
    weights = dict(w_in_a=w_in_a, sgu_ln_g=sgu_ln_g, sgu_ln_b=sgu_ln_b, w_spatial=w_spatial, b_spatial=b_spatial,
                   w_out_a=w_out_a, w_kv=w_kv, w_in_b=w_in_b, attn_sinks=attn_sinks, rel_bias=rel_bias,
                   w_out_b=w_out_b, post_ln_g=post_ln_g, post_ln_b=post_ln_b)
    moments_m = dict(w_in_a=m_w_in_a, sgu_ln_g=m_sgu_ln_g, sgu_ln_b=m_sgu_ln_b, w_spatial=m_w_spatial,
                     b_spatial=m_b_spatial, w_out_a=m_w_out_a, w_kv=m_w_kv, w_in_b=m_w_in_b,
                     attn_sinks=m_attn_sinks, rel_bias=m_rel_bias, w_out_b=m_w_out_b, post_ln_g=m_post_ln_g,
                     post_ln_b=m_post_ln_b)
    moments_v = dict(w_in_a=v_w_in_a, sgu_ln_g=v_sgu_ln_g, sgu_ln_b=v_sgu_ln_b, w_spatial=v_w_spatial,
                     b_spatial=v_b_spatial, w_out_a=v_w_out_a, w_kv=v_w_kv, w_in_b=v_w_in_b,
                     attn_sinks=v_attn_sinks, rel_bias=v_rel_bias, w_out_b=v_w_out_b, post_ln_g=v_post_ln_g,
                     post_ln_b=v_post_ln_b)
    order = ("w_in_a", "sgu_ln_g", "sgu_ln_b", "w_spatial", "b_spatial", "w_out_a", "w_kv", "w_in_b", "attn_sinks",
             "rel_bias", "w_out_b", "post_ln_g", "post_ln_b")

    shard_index = 2 * lax.axis_index("x") + lax.axis_index("y")
    ln_shard = jnp.concatenate([sgu_ln_g, sgu_ln_b], axis=0)
    shards = [w_in_a[0], w_out_a[0], w_kv, w_in_b[0], w_out_b[0]]
    col_sharded = [True, False, False, True, False]
    full_in_a, *later, ln_full = _gather_weights(shards, col_sharded, [True, False, False, False, False], ln_shard)
    ln_full = jnp.transpose(ln_full, (1, 0, 2)).reshape(2, A_WIDTH)
    later_shapes = [s.shape for s in shards[1:]]
    fetch_group, fetch_token = _fetch_start(later, later_shapes, col_sharded[1:])

    def later_weights(y):
        return _fetch_wait(fetch_group, later_shapes, col_sharded[1:], y)

    groups, grads, deltas, new_m, new_v, scalars = {}, {}, {}, {}, {}, {}
    early = ("w_out_b", "w_in_b", "w_kv", "w_out_a")

    def two_dim(a):
        return a.reshape(-1, a.shape[-1])

    def send(tag, arrays, sliced):
        groups[tag], token = _exchange_start(tag, arrays, sliced)
        if tag != "a_in":
            return token
        srcs, lands = _exchange_wait("early", [groups["b"], groups["a_out"]], token)
        *reduced, packed_sum = _sum_and_swap("early", srcs[:4], lands[:4], srcs[4], lands[4])
        updates = [(two_dim(weights[n]), g, two_dim(moments_m[n]), two_dim(moments_v[n]))
                   for n, g in zip(early, reduced)]
        small_names = ("sgu_ln_g", "sgu_ln_b", "w_spatial", "b_spatial", "attn_sinks", "rel_bias", "post_ln_g",
                       "post_ln_b")
        small_updates, scalars["loss"] = _adamw_small(packed_sum, shard_index, small_names, weights, moments_m,
                                                      moments_v)
        for name, (g, d, nm, nv) in zip(small_names, small_updates):
            grads[name], deltas[name], new_m[name], new_v[name] = g, d, nm, nv
        return new_m["b_spatial"].reshape(A_GROUPS, CHUNK), updates

    grad_x, _, _, updated = _local_step(
        x[0], loss_target[0], full_in_a, later_weights, fetch_token, ln_full[0:1], ln_full[1:2], w_spatial[0],
        b_spatial[0], attn_sinks, rel_bias, post_ln_g, post_ln_b, send=send)
    for k, name in enumerate(early):
        grads[name], deltas[name], new_m[name], new_v[name] = [
            a.reshape(weights[name].shape) for a in updated[4 * k:4 * k + 4]]

    srcs, lands = _exchange_wait("late", [groups["a_in"]], grad_x)
    (g_in_a,) = _sum_and_swap("late", srcs, lands)
    grads["w_in_a"], deltas["w_in_a"], new_m["w_in_a"], new_v["w_in_a"] = _adamw(
        "w_in_a", w_in_a, g_in_a.reshape(w_in_a.shape), m_w_in_a, v_w_in_a)
    return (scalars["loss"], grad_x[None], *[grads[n] for n in order], *[deltas[n] for n in order],
            *[new_m[n] for n in order], *[new_v[n] for n in order])
```

```python
import math

import jax
import jax.numpy as jnp
from jax import lax
from jax.experimental import pallas as pl
from jax.experimental.pallas import tpu as pltpu

F32 = jnp.float32
BF16 = jnp.bfloat16

D_MODEL = 1024
A_WIDTH = 2048
A_GROUPS = 8
A_GROUP_DIM = 256
CHUNK = 128
N_HEADS = 16
N_KV = 2
HEAD_DIM = 64
PAIR = 2 * HEAD_DIM
B_WIDTH = 1024
REL_BUCKETS = 32
ALPHA = 4.0 ** 0.25
LN_EPS = 1e-5
NEG_INF = -1e30
SCALE = HEAD_DIM ** -0.5

ADAM_LR = 0.001
ADAM_B1 = 0.9
ADAM_B2 = 0.999
ADAM_EPS = 1e-08
ADAM_WD = 0.01
ADAM_STEP = 10

N_DEV = 8
N_CHIPS = 4
MESH = pl.DeviceIdType.MESH
VMEM_LIMIT = 56 * 1024 * 1024

TM_ATTN = 256
TM_BWD_ATTN = 512
TM_MM = 512
TM_WIN = 1024
_LANES = 128
_SUBLANES = 8


def _dot(a, b):
    return jnp.dot(a, b, preferred_element_type=F32)


def _dot_nt(a, b):
    return lax.dot_general(a, b, (((1,), (1,)), ((), ())), preferred_element_type=F32)


def _dot_tn(a, b):
    return lax.dot_general(a, b, (((0,), (0,)), ((), ())), preferred_element_type=F32)


def _ln_fwd(r):
    mu = jnp.mean(r, axis=-1, keepdims=True)
    rc = r - mu
    var = jnp.mean(rc * rc, axis=-1, keepdims=True)
    rstd = lax.rsqrt(var + LN_EPS)
    return rc * rstd, rstd


def _ln_bwd(dxh, xh, rstd):
    m1 = jnp.mean(dxh, axis=-1, keepdims=True)
    m2 = jnp.mean(dxh * xh, axis=-1, keepdims=True)
    return rstd * (dxh - m1 - xh * m2)


def _silu_parts(z):
    sg = jax.nn.sigmoid(z)
    return z * sg, sg * (1.0 + z * (1.0 - sg))


def _dup_halves(blk):
    sw = pltpu.roll(blk, HEAD_DIM, 1)
    lo = lax.broadcasted_iota(jnp.int32, blk.shape, 1) < HEAD_DIM
    return jnp.where(lo, blk, sw), jnp.where(lo, sw, blk)


def _fold_halves(blk):
    return blk + pltpu.roll(blk, HEAD_DIM, 1)


def _resident(shape):
    nd = len(shape)
    return pl.BlockSpec(shape, lambda *_: (0,) * nd, pipeline_mode=pl.Buffered(1))


def _const(shape):
    nd = len(shape)
    return pl.BlockSpec(shape, lambda *_: (0,) * nd)


def _rows(tm, cols):
    return pl.BlockSpec((tm, cols), lambda i: (i, 0))


def _params(sem=("arbitrary",)):
    return pltpu.CompilerParams(dimension_semantics=sem, vmem_limit_bytes=VMEM_LIMIT)


def _spatial_mix(ws_ref, bsp_ref, vn, s_scr, n_chunks):
    tri = (lax.broadcasted_iota(jnp.int32, (CHUNK, CHUNK), 0)
           >= lax.broadcasted_iota(jnp.int32, (CHUNK, CHUNK), 1))
    for g in range(A_GROUPS):
        wsg = jnp.where(tri, ws_ref[g], 0.0).astype(BF16)
        cols = slice(g * A_GROUP_DIM, (g + 1) * A_GROUP_DIM)
        for ci in range(n_chunks):
            rows = slice(ci * CHUNK, (ci + 1) * CHUNK)
            s_scr[rows, cols] = _dot(wsg, vn[rows, cols]) + bsp_ref[:, g:g + 1]


def _layer_a_fwd(x, w_in, lng, lnb, ws, bsp_t, after):
    t_len = x.shape[0]
    tm = TM_ATTN

    def body(x_ref, win_ref, lng_ref, lnb_ref, ws_ref, bsp_ref, after_ref,
             xt_ref, u_ref, vh_ref, z_ref, rv_ref, y_ref, s_scr):
        x_t = x_ref[...]
        xb = x_t.astype(BF16)
        xt_ref[...] = x_t.T.astype(BF16)
        u = _dot(xb, win_ref[:, 0:A_WIDTH])
        v = _dot(xb, win_ref[:, A_WIDTH:2 * A_WIDTH])
        z = _dot(xb, win_ref[:, 2 * A_WIDTH:3 * A_WIDTH])
        vh, rv = _ln_fwd(v)
        vn = (vh * lng_ref[...] + lnb_ref[...]).astype(BF16)
        _spatial_mix(ws_ref, bsp_ref, vn, s_scr, tm // CHUNK)
        sz, _ = _silu_parts(z)
        y_ref[...] = (u * s_scr[...] * sz).astype(BF16)
        u_ref[...] = u.astype(BF16)
        vh_ref[...] = vh.astype(BF16)
        z_ref[...] = z.astype(BF16)
        rv_ref[...] = rv

    wide = jax.ShapeDtypeStruct((t_len, A_WIDTH), BF16)
    return pl.pallas_call(
        body, name="layer_a_fwd", grid=(t_len // tm,),
        in_specs=[_rows(tm, D_MODEL), _resident(w_in.shape), _const(lng.shape), _const(lnb.shape), _const(ws.shape),
                  _const(bsp_t.shape), _const(after.shape)],
        out_specs=[pl.BlockSpec((D_MODEL, tm), lambda i: (0, i)), _rows(tm, A_WIDTH), _rows(tm, A_WIDTH),
                   _rows(tm, A_WIDTH), _rows(tm, 1), _rows(tm, A_WIDTH)],
        out_shape=[jax.ShapeDtypeStruct((D_MODEL, t_len), BF16), wide, wide, wide,
                   jax.ShapeDtypeStruct((t_len, 1), F32), wide],
        scratch_shapes=[pltpu.VMEM((tm, A_WIDTH), F32)],
        compiler_params=_params(),
    )(x, w_in, lng, lnb, ws, bsp_t, after)


def _layer_b_proj(x, y, w_out_a, g1, b1, w_in, w_kv):
    t_len = x.shape[0]
    tm = 2 * TM_MM

    def body(x_ref, y_ref, wout_ref, g_ref, b_ref, win_ref, wkv_ref, xh_ref, r1_ref, q_ref, z_ref, kd_ref, vd_ref):
        halves = [slice(k * TM_MM, (k + 1) * TM_MM) for k in range(2)]
        projected = [_dot(y_ref[rows, :], wout_ref[...]) for rows in halves]
        for rows, out_a in zip(halves, projected):
            xh, r1 = _ln_fwd(ALPHA * x_ref[rows, :] + out_a)
            xh_ref[rows, :] = xh
            r1_ref[rows, :] = r1
            h1 = (xh * g_ref[...] + b_ref[...]).astype(BF16)
            q_ref[rows, :] = (_dot(h1, win_ref[:, 0:B_WIDTH]) * SCALE).astype(BF16)
            z_ref[rows, :] = _dot(h1, win_ref[:, B_WIDTH:2 * B_WIDTH]).astype(BF16)
            kv = _dot(h1, wkv_ref[...])
            k0, k1 = _dup_halves(kv[:, 0:PAIR])
            v0, v1 = _dup_halves(kv[:, PAIR:2 * PAIR])
            kd_ref[rows, 0:PAIR] = k0.astype(BF16)
            kd_ref[rows, PAIR:2 * PAIR] = k1.astype(BF16)
            vd_ref[rows, 0:PAIR] = v0.astype(BF16)
            vd_ref[rows, PAIR:2 * PAIR] = v1.astype(BF16)

    return pl.pallas_call(
        body, name="layer_b_proj", grid=(t_len // tm,),
        in_specs=[_rows(tm, D_MODEL), _rows(tm, A_WIDTH), _resident(w_out_a.shape), _const(g1.shape),
                  _const(b1.shape), _resident(w_in.shape), _resident(w_kv.shape)],
        out_specs=[_rows(tm, D_MODEL), _rows(tm, 1), _rows(tm, B_WIDTH), _rows(tm, B_WIDTH), _rows(tm, 2 * PAIR),
                   _rows(tm, 2 * PAIR)],
        out_shape=[jax.ShapeDtypeStruct((t_len, D_MODEL), F32), jax.ShapeDtypeStruct((t_len, 1), F32),
                   jax.ShapeDtypeStruct((t_len, B_WIDTH), BF16), jax.ShapeDtypeStruct((t_len, B_WIDTH), BF16),
                   jax.ShapeDtypeStruct((t_len, 2 * PAIR), BF16), jax.ShapeDtypeStruct((t_len, 2 * PAIR), BF16)],
        compiler_params=_params(),
    )(x, y, w_out_a, g1, b1, w_in, w_kv)


GROUP = N_HEADS // N_KV
GROUP_Q = GROUP * CHUNK


def _window_tables():
    j = jnp.arange(2 * CHUNK, dtype=jnp.int32)[:, None]
    t = jnp.arange(CHUNK, dtype=jnp.int32)[None, :]
    dist = t + CHUNK - j
    inside = (dist >= 0) & (dist < CHUNK)
    return jnp.stack([inside & (j >= CHUNK), inside]).astype(F32)


def _band(ref, chunk_index, kvh):
    prev0 = pl.multiple_of(jnp.maximum(chunk_index - 1, 0) * CHUNK, CHUNK)
    cur0 = pl.multiple_of(chunk_index * CHUNK, CHUNK)
    cols = slice(kvh * PAIR, (kvh + 1) * PAIR)
    return jnp.concatenate([ref[pl.ds(prev0, CHUNK), cols], ref[pl.ds(cur0, CHUNK), cols]], axis=0)


def _group_tables(bias_ref, win_ref, sink_ref, chunk_index, kvh):
    bias = jnp.concatenate([bias_ref[kvh * GROUP + j] for j in range(GROUP)], axis=1)
    win = win_ref[jnp.minimum(chunk_index, 1)]
    mask = jnp.concatenate([win] * GROUP, axis=1) > 0.5
    sink = jnp.concatenate([jnp.full((1, CHUNK), sink_ref[0, kvh * GROUP + j], F32) for j in range(GROUP)], axis=1)
    return bias, mask, sink


def _attn_probs(qs, kband, bias, mask, sink):
    logits = jnp.where(mask, _dot_nt(kband, qs) + bias, NEG_INF)
    m = jnp.maximum(jnp.max(logits, axis=0, keepdims=True), sink)
    e = jnp.exp(logits - m)
    es = jnp.exp(sink - m)
    inv = 1.0 / (jnp.sum(e, axis=0, keepdims=True) + es)
    return e * inv, es * inv


def _half_mask():
    return lax.broadcasted_iota(jnp.int32, (CHUNK, PAIR), 1) < HEAD_DIM


def _stack_heads(src_ref, rows, kvh, dst_scr, lo):
    for j in range(GROUP):
        h = kvh * GROUP + j
        blk = src_ref[rows, (h // 2) * PAIR:(h // 2 + 1) * PAIR].astype(F32)
        keep = lo if h % 2 == 0 else ~lo
        dst_scr[j * CHUNK:(j + 1) * CHUNK, :] = jnp.where(keep, blk, 0.0).astype(BF16)


def _probs_spec(tm):
    return pl.BlockSpec((tm // CHUNK, N_KV, 2 * CHUNK, GROUP_Q), lambda i: (i, 0, 0, 0))


def _sink_probs_spec(tiles=1):
    return pl.BlockSpec((tiles, 8, GROUP_Q), lambda i: (i, 0, 0))


def _unstack_pairs(stacked, pp, lo):
    return jnp.where(lo, stacked[(2 * pp) * CHUNK:(2 * pp + 1) * CHUNK], stacked[(2 * pp + 1) * CHUNK:(2 * pp + 2) * CHUNK])


def _layer_b_fwd(q, zb, kd, vd, bias, win, sinks, xh1, g1, b1, w_out, g2, b2, tgt):
    t_len = q.shape[0]
    tm = 2 * TM_ATTN

    def body(q_ref, z_ref, kd_ref, vd_ref, bias_ref, win_ref, sink_ref, xh_ref, g1_ref, b1_ref, wout_ref, g2_ref,
             b2_ref, tgt_ref, o_ref, p_ref, ps_ref, dr_ref, loss_ref, dg_ref, db_ref, o_scr, qs_scr):
        i = pl.program_id(0)

        @pl.when(i == 0)
        def _():
            loss_ref[...] = jnp.zeros_like(loss_ref)
            dg_ref[...] = jnp.zeros_like(dg_ref)
            db_ref[...] = jnp.zeros_like(db_ref)

        lo = _half_mask()
        ps_ref[...] = jnp.zeros_like(ps_ref)
        per_part = TM_ATTN // CHUNK
        for part in range(tm // TM_ATTN):
            part_rows = slice(part * TM_ATTN, (part + 1) * TM_ATTN)
            for cp in range(per_part):
                ci = part * per_part + cp
                cg = i * (tm // CHUNK) + ci
                rows = slice(ci * CHUNK, (ci + 1) * CHUNK)
                for kvh in range(N_KV):
                    kband = _band(kd_ref, cg, kvh)
                    vband = _band(vd_ref, cg, kvh)
                    bias_g, mask, sink = _group_tables(bias_ref, win_ref, sink_ref, cg, kvh)
                    _stack_heads(q_ref, rows, kvh, qs_scr, lo)
                    p, p_sink = _attn_probs(qs_scr[...], kband, bias_g, mask, sink)
                    p = p.astype(BF16)
                    p_ref[ci, kvh] = p
                    ps_ref[part, cp * N_KV + kvh:cp * N_KV + kvh + 1, :] = p_sink
                    o_stack = _dot_tn(p, vband)
                    for pp in range(GROUP // 2):
                        pair = kvh * (GROUP // 2) + pp
                        o_scr[rows, pair * PAIR:(pair + 1) * PAIR] = _unstack_pairs(o_stack, pp, lo)
            o = o_scr[part_rows, :]
            o_ref[part_rows, :] = o.astype(BF16)
            sz, _ = _silu_parts(z_ref[part_rows, :].astype(F32))
            y = (o * sz).astype(BF16)
            h1 = xh_ref[part_rows, :] * g1_ref[...] + b1_ref[...]
            r = ALPHA * h1 + _dot(y, wout_ref[...])
            xh2, rstd2 = _ln_fwd(r)
            diff = xh2 * g2_ref[...] + b2_ref[...] - tgt_ref[part_rows, :]
            loss_ref[...] += jnp.sum(diff * diff, axis=0, keepdims=True)
            dh2 = diff * (1.0 / D_MODEL)
            dg_ref[...] += jnp.sum(dh2 * xh2, axis=0, keepdims=True)
            db_ref[...] += jnp.sum(dh2, axis=0, keepdims=True)
            dr_ref[part_rows, :] = _ln_bwd(dh2 * g2_ref[...], xh2, rstd2)

    vec = jax.ShapeDtypeStruct((1, D_MODEL), F32)
    return pl.pallas_call(
        body, name="layer_b_fwd", grid=(t_len // tm,),
        in_specs=[_rows(tm, B_WIDTH), _rows(tm, B_WIDTH), _resident(kd.shape), _resident(vd.shape),
                  _resident(bias.shape), _resident(win.shape), pl.BlockSpec(memory_space=pltpu.SMEM),
                  _rows(tm, D_MODEL), _const(g1.shape), _const(b1.shape), _resident(w_out.shape), _const(g2.shape),
                  _const(b2.shape), _rows(tm, D_MODEL)],
        out_specs=[_rows(tm, B_WIDTH), _probs_spec(tm), _sink_probs_spec(tm // TM_ATTN), _rows(tm, D_MODEL)]
        + [_const((1, D_MODEL))] * 3,
        out_shape=[jax.ShapeDtypeStruct((t_len, B_WIDTH), BF16),
                   jax.ShapeDtypeStruct((t_len // CHUNK, N_KV, 2 * CHUNK, GROUP_Q), BF16),
                   jax.ShapeDtypeStruct((t_len // TM_ATTN, 8, GROUP_Q), F32),
                   jax.ShapeDtypeStruct((t_len, D_MODEL), F32), vec, vec, vec],
        scratch_shapes=[pltpu.VMEM((tm, B_WIDTH), F32), pltpu.VMEM((GROUP_Q, PAIR), BF16)],
        compiler_params=_params(),
    )(q, zb, kd, vd, bias, win, sinks, xh1, g1, b1, w_out, g2, b2, tgt)


def _layer_b_bwd_attn(dr2, zb, o, q, kd, vd, probs, sink_probs, w_out):
    t_len = q.shape[0]
    tm = TM_BWD_ATTN
    n_steps = t_len // tm
    n_chunks = tm // CHUNK
    per_part = TM_ATTN // CHUNK

    def body(dr_ref, z_ref, o_ref, q_ref, kd_ref, vd_ref, p_ref, ps_ref, wout_ref,
             dq_ref, dz_ref, dkd_ref, dvd_ref, ck_ref, cv_ref, gw_ref, dsink_ref, dbias_ref,
             do_scr, qs_scr, dos_scr, gw_acc):
        i = pl.program_id(0)

        @pl.when(i == 0)
        def _():
            gw_acc[...] = jnp.zeros_like(gw_acc)
            dsink_ref[...] = jnp.zeros_like(dsink_ref)
            dbias_ref[...] = jnp.zeros_like(dbias_ref)

        drb = dr_ref[...].astype(BF16)
        per_kvh = 2
        n_blocks = N_KV * per_kvh
        block_cols = B_WIDTH // n_blocks

        def through_gate(b):
            cols = slice(b * block_cols, (b + 1) * block_cols)
            dy = _dot_nt(drb, wout_ref[cols, :])
            sz, dsz = _silu_parts(z_ref[:, cols].astype(F32))
            o_t = o_ref[:, cols].astype(F32)
            dz_ref[:, cols] = (dy * o_t * dsz).astype(BF16)
            do_scr[:, cols] = (dy * sz).astype(BF16)
            return (o_t * sz).astype(BF16)

        def weight_gradient(b, gated):
            cols = slice(b * block_cols, (b + 1) * block_cols)
            gw_acc[cols, :] += _dot_tn(gated, drb)

        gated = {b: through_gate(b) for b in range(per_kvh)}

        lo = _half_mask()
        for kvh in range(N_KV):
            kcols = slice(kvh * PAIR, (kvh + 1) * PAIR)
            dk_bands, dv_bands = [], []
            for ci in range(n_chunks):
                unit = kvh * n_chunks + ci
                if ci < per_kvh and kvh + 1 < N_KV:
                    gated[(kvh + 1) * per_kvh + ci] = through_gate((kvh + 1) * per_kvh + ci)
                if unit in gated:
                    weight_gradient(unit, gated.pop(unit))
                cg = i * n_chunks + ci
                rows = slice(ci * CHUNK, (ci + 1) * CHUNK)
                kband = _band(kd_ref, cg, kvh)
                vband = _band(vd_ref, cg, kvh)
                _stack_heads(q_ref, rows, kvh, qs_scr, lo)
                _stack_heads(do_scr, rows, kvh, dos_scr, lo)
                qs = qs_scr[...]
                dos = dos_scr[...]
                pb = p_ref[ci, kvh]
                p = pb.astype(F32)
                sink_row = (ci % per_part) * N_KV + kvh
                p_sink = ps_ref[ci // per_part, sink_row:sink_row + 1, :]
                dp = _dot_nt(vband, dos)
                delta = jnp.sum(p * dp, axis=0, keepdims=True)
                dlog = p * (dp - delta)
                for j in range(GROUP):
                    dbias_ref[kvh * GROUP + j] += dlog[:, j * CHUNK:(j + 1) * CHUNK]
                dsink_ref[kvh:kvh + 1, :] += -(p_sink * delta)
                ds = dlog.astype(BF16)
                dq_stack = _dot_tn(ds, kband) * SCALE
                for pp in range(GROUP // 2):
                    pair = kvh * (GROUP // 2) + pp
                    dq_ref[rows, pair * PAIR:(pair + 1) * PAIR] = _unstack_pairs(dq_stack, pp, lo).astype(BF16)
                dk_bands.append(_dot(ds, qs))
                dv_bands.append(_dot(pb, dos))
            for bands, out_ref, carry_ref in ((dk_bands, dkd_ref, ck_ref), (dv_bands, dvd_ref, cv_ref)):
                carry_ref[0, :, kcols] = bands[0][0:CHUNK]
                for ci in range(n_chunks):
                    own = bands[ci][CHUNK:2 * CHUNK]
                    if ci + 1 < n_chunks:
                        own = own + bands[ci + 1][0:CHUNK]
                    out_ref[ci * CHUNK:(ci + 1) * CHUNK, kcols] = own

        @pl.when(i == n_steps - 1)
        def _():
            gw_ref[...] = gw_acc[...].astype(BF16)

    carry_spec = pl.BlockSpec((1, CHUNK, 2 * PAIR), lambda i: (i, 0, 0))
    carry_shape = jax.ShapeDtypeStruct((n_steps, CHUNK, 2 * PAIR), F32)
    bias_shape = (N_HEADS, 2 * CHUNK, CHUNK)
    return pl.pallas_call(
        body, name="layer_b_bwd_attn", grid=(n_steps,),
        in_specs=[_rows(tm, D_MODEL), _rows(tm, B_WIDTH), _rows(tm, B_WIDTH), _rows(tm, B_WIDTH),
                  _resident(kd.shape), _resident(vd.shape), _probs_spec(tm), _sink_probs_spec(tm // TM_ATTN),
                  _resident(w_out.shape)],
        out_specs=[_rows(tm, B_WIDTH), _rows(tm, B_WIDTH), _rows(tm, 2 * PAIR), _rows(tm, 2 * PAIR),
                   carry_spec, carry_spec, _const(w_out.shape), _const((N_KV, GROUP_Q)), _const(bias_shape)],
        out_shape=[jax.ShapeDtypeStruct((t_len, B_WIDTH), BF16), jax.ShapeDtypeStruct((t_len, B_WIDTH), BF16),
                   jax.ShapeDtypeStruct((t_len, 2 * PAIR), F32), jax.ShapeDtypeStruct((t_len, 2 * PAIR), F32),
                   carry_shape, carry_shape, jax.ShapeDtypeStruct(w_out.shape, BF16),
                   jax.ShapeDtypeStruct((N_KV, GROUP_Q), F32), jax.ShapeDtypeStruct(bias_shape, F32)],
        scratch_shapes=[pltpu.VMEM((tm, B_WIDTH), BF16), pltpu.VMEM((GROUP_Q, PAIR), BF16),
                        pltpu.VMEM((GROUP_Q, PAIR), BF16), pltpu.VMEM(w_out.shape, F32)],
        compiler_params=_params(),
    )(dr2, zb, o, q, kd, vd, probs, sink_probs, w_out)


def _layer_b_bwd_proj(xh1, rstd1, g1, b1, dr2, dq, dzb, dkd, dvd, carry_k, carry_v, w_in, w_kv):
    t_len = xh1.shape[0]
    tm = TM_MM
    n_steps = t_len // tm
    per_tile = tm // TM_BWD_ATTN
    n_carry = carry_k.shape[0]

    def body(xh_ref, rstd_ref, g_ref, b_ref, dr2_ref, dq_ref, dz_ref, dkd_ref, dvd_ref, *rest):
        carry_refs = rest[:2 * per_tile]
        win_ref, wkv_ref, dr1_ref, dg_ref, db_ref, gwin_ref, gwkv_ref, acc_in, acc_kv = rest[2 * per_tile:]
        i = pl.program_id(0)

        @pl.when(i == 0)
        def _():
            acc_in[...] = jnp.zeros_like(acc_in)
            acc_kv[...] = jnp.zeros_like(acc_kv)
            dg_ref[...] = jnp.zeros_like(dg_ref)
            db_ref[...] = jnp.zeros_like(db_ref)

        lo = lax.broadcasted_iota(jnp.int32, (tm, PAIR), 1) < HEAD_DIM

        def heads_gradient(tile_ref, refs):
            parts = []
            for a in range(per_tile):
                parts.append(tile_ref[a * TM_BWD_ATTN:(a + 1) * TM_BWD_ATTN - CHUNK, :])
                carry = refs[a][0]
                if a == per_tile - 1:
                    carry = jnp.where(i < n_steps - 1, carry, 0.0)
                parts.append(tile_ref[(a + 1) * TM_BWD_ATTN - CHUNK:(a + 1) * TM_BWD_ATTN, :] + carry)
            dup = jnp.concatenate(parts, axis=0)
            return jnp.where(lo, _fold_halves(dup[:, 0:PAIR]), _fold_halves(dup[:, PAIR:2 * PAIR]))

        xh = xh_ref[...]
        h1 = (xh * g_ref[...] + b_ref[...]).astype(BF16)
        dq_t = dq_ref[...]
        dz_t = dz_ref[...]
        dkv = jnp.concatenate([heads_gradient(dkd_ref, carry_refs[:per_tile]),
                               heads_gradient(dvd_ref, carry_refs[per_tile:])], axis=1).astype(BF16)
        dh1 = ALPHA * dr2_ref[...]
        dh1 += _dot_nt(dq_t, win_ref[:, 0:B_WIDTH])
        dh1 += _dot_nt(dz_t, win_ref[:, B_WIDTH:2 * B_WIDTH])
        dh1 += _dot_nt(dkv, wkv_ref[...])
        acc_in[:, 0:B_WIDTH] += _dot_tn(h1, dq_t)
        acc_in[:, B_WIDTH:2 * B_WIDTH] += _dot_tn(h1, dz_t)
        acc_kv[...] += _dot_tn(h1, dkv)
        dg_ref[...] += jnp.sum(dh1 * xh, axis=0, keepdims=True)
        db_ref[...] += jnp.sum(dh1, axis=0, keepdims=True)
        dr1_ref[...] = _ln_bwd(dh1 * g_ref[...], xh, rstd_ref[...])

        @pl.when(i == n_steps - 1)
        def _():
            half_rows = D_MODEL // 2
            shard_cols = 2 * B_WIDTH // N_CHIPS
            for s in range(N_CHIPS):
                for c in range(2):
                    gwin_ref[2 * s + c] = acc_in[c * half_rows:(c + 1) * half_rows,
                                                 s * shard_cols:(s + 1) * shard_cols].astype(BF16)
            gwkv_ref[...] = acc_kv[...].astype(BF16)

    vec = jax.ShapeDtypeStruct((1, D_MODEL), F32)
    gwin_shape = (N_DEV, D_MODEL // 2, 2 * B_WIDTH // N_CHIPS)

    def carry_spec(a):
        return pl.BlockSpec((1, CHUNK, 2 * PAIR), lambda i: (jnp.minimum(per_tile * i + a + 1, n_carry - 1), 0, 0))

    carry_specs = [carry_spec(a) for a in range(per_tile)]
    return pl.pallas_call(
        body, name="layer_b_bwd_proj", grid=(n_steps,),
        in_specs=[_rows(tm, D_MODEL), _rows(tm, 1), _const(g1.shape), _const(b1.shape), _rows(tm, D_MODEL),
                  _rows(tm, B_WIDTH), _rows(tm, B_WIDTH), _rows(tm, 2 * PAIR), _rows(tm, 2 * PAIR)]
        + carry_specs + carry_specs + [_resident(w_in.shape), _resident(w_kv.shape)],
        out_specs=[_rows(tm, D_MODEL), _const((1, D_MODEL)), _const((1, D_MODEL)), _const(gwin_shape),
                   _const(w_kv.shape)],
        out_shape=[jax.ShapeDtypeStruct((t_len, D_MODEL), F32), vec, vec,
                   jax.ShapeDtypeStruct(gwin_shape, BF16), jax.ShapeDtypeStruct(w_kv.shape, BF16)],
        scratch_shapes=[pltpu.VMEM(w_in.shape, F32), pltpu.VMEM(w_kv.shape, F32)],
        compiler_params=_params(),
    )(xh1, rstd1, g1, b1, dr2, dq, dzb, dkd, dvd, *([carry_k] * per_tile), *([carry_v] * per_tile), w_in, w_kv)


def _layer_a_bwd_mix(dr1, u, vh, z, y, rv, w_out, lng, lnb, ws, bsp_t, after):
    t_len = u.shape[0]
    tm = TM_ATTN
    n_steps = t_len // tm

    def body(dr_ref, u_ref, vh_ref, z_ref, y_ref, rv_ref, wout_ref, lng_ref, lnb_ref, ws_ref, bsp_ref, after_ref,
             dp_ref, gw_ref, dws_ref, dbsp_ref, dgs_ref, dbs_ref, dvn_scr, gw_acc):
        i = pl.program_id(0)

        @pl.when(i == 0)
        def _():
            gw_acc[...] = jnp.zeros_like(gw_acc)
            dws_ref[...] = jnp.zeros_like(dws_ref)
            dbsp_ref[...] = jnp.zeros_like(dbsp_ref)
            dgs_ref[...] = jnp.zeros_like(dgs_ref)
            dbs_ref[...] = jnp.zeros_like(dbs_ref)

        drb = dr_ref[...].astype(BF16)

        def group_cols(g):
            return slice(g * A_GROUP_DIM, (g + 1) * A_GROUP_DIM)

        tri = (lax.broadcasted_iota(jnp.int32, (CHUNK, CHUNK), 0)
               >= lax.broadcasted_iota(jnp.int32, (CHUNK, CHUNK), 1))
        lane = lax.broadcasted_iota(jnp.int32, (CHUNK, CHUNK), 1)
        ones = jnp.ones((CHUNK, A_GROUP_DIM), BF16)
        dbsp = jnp.zeros((CHUNK, CHUNK), F32)
        dy_next = _dot_nt(drb, wout_ref[group_cols(0), :])
        for g in range(A_GROUPS):
            wsg = jnp.where(tri, ws_ref[g], 0.0).astype(BF16)
            cols = group_cols(g)
            cols_z = slice(2 * A_WIDTH + g * A_GROUP_DIM, 2 * A_WIDTH + (g + 1) * A_GROUP_DIM)
            dy_g = dy_next
            if g + 1 < A_GROUPS:
                dy_next = _dot_nt(drb, wout_ref[group_cols(g + 1), :])
            gw_acc[cols, :] += _dot_tn(y_ref[:, cols], drb)
            both = jnp.zeros((CHUNK, 2 * CHUNK), F32)
            for ci in range(tm // CHUNK):
                rows = slice(ci * CHUNK, (ci + 1) * CHUNK)
                vn = (vh_ref[rows, cols].astype(F32) * lng_ref[:, cols] + lnb_ref[:, cols]).astype(BF16)
                s = _dot(wsg, vn) + bsp_ref[:, g:g + 1]
                sz, dsz = _silu_parts(z_ref[rows, cols].astype(F32))
                dy = dy_g[rows]
                t = dy * u_ref[rows, cols].astype(F32)
                dp_ref[rows, cols] = (dy * (s * sz)).astype(BF16)
                dp_ref[rows, cols_z] = (t * s * dsz).astype(BF16)
                ds_b = (t * sz).astype(BF16)
                both += _dot_nt(ds_b, jnp.concatenate([vn, ones], axis=0))
                dvn_scr[rows, cols] = _dot_tn(wsg, ds_b)
            dws_ref[g] += jnp.where(tri, both[:, 0:CHUNK], 0.0)
            dbsp = jnp.where(lane == g, both[:, CHUNK:2 * CHUNK], dbsp)
        dbsp_ref[...] += dbsp
        dvn = dvn_scr[...]
        vh_t = vh_ref[...].astype(F32)
        dgs_ref[...] += jnp.sum(dvn * vh_t, axis=0, keepdims=True)
        dbs_ref[...] += jnp.sum(dvn, axis=0, keepdims=True)
        dp_ref[:, A_WIDTH:2 * A_WIDTH] = _ln_bwd(dvn * lng_ref[...], vh_t, rv_ref[...]).astype(BF16)

        @pl.when(i == n_steps - 1)
        def _():
            gw_ref[...] = gw_acc[...].astype(BF16)

    wide = jax.ShapeDtypeStruct((1, A_WIDTH), F32)
    return pl.pallas_call(
        body, name="layer_a_bwd_mix", grid=(n_steps,),
        in_specs=[_rows(tm, D_MODEL), _rows(tm, A_WIDTH), _rows(tm, A_WIDTH), _rows(tm, A_WIDTH), _rows(tm, A_WIDTH),
                  _rows(tm, 1), _resident(w_out.shape), _const(lng.shape), _const(lnb.shape), _const(ws.shape),
                  _const(bsp_t.shape), _const(after.shape)],
        out_specs=[_rows(tm, 3 * A_WIDTH), _const(w_out.shape), _const(ws.shape), _const((CHUNK, CHUNK)),
                   _const((1, A_WIDTH)), _const((1, A_WIDTH))],
        out_shape=[jax.ShapeDtypeStruct((t_len, 3 * A_WIDTH), BF16), jax.ShapeDtypeStruct(w_out.shape, BF16),
                   jax.ShapeDtypeStruct(ws.shape, F32), jax.ShapeDtypeStruct((CHUNK, CHUNK), F32),
                   wide, wide],
        scratch_shapes=[pltpu.VMEM((tm, A_WIDTH), F32), pltpu.VMEM(w_out.shape, F32)],
        compiler_params=_params(),
    )(dr1, u, vh, z, y, rv, w_out, lng, lnb, ws, bsp_t, after)


def _layer_a_bwd_dx(dr1, dp, w_in, after, updates=()):
    t_len = dr1.shape[0]
    tm = TM_MM
    n_steps = t_len // tm
    n_upd = len(updates)

    def body(dr_ref, dp_ref, win_ref, after_ref, *refs):
        upd_in, dx_ref, upd_out = refs[:4 * n_upd], refs[4 * n_upd], refs[4 * n_upd + 1:]
        dx_ref[...] = ALPHA * dr_ref[...] + _dot_nt(dp_ref[...], win_ref[...])
        for k in range(n_upd):
            w_ref, g_ref, m_ref, v_ref = upd_in[4 * k:4 * k + 4]
            g_out, d_ref, nm_ref, nv_ref = upd_out[4 * k:4 * k + 4]
            g_out[...] = g_ref[...]
            _adamw_update(w_ref, g_ref, m_ref, v_ref, d_ref, nm_ref, nv_ref)

    upd_specs, upd_shapes, upd_args = [], [], []
    for w, g, m, v in updates:
        rows, cols = w.shape
        upd_specs.append(pl.BlockSpec((rows // n_steps, cols), lambda i: (i, 0)))
        upd_shapes.append(jax.ShapeDtypeStruct((rows, cols), F32))
        upd_args += [w, g, m, v]
    return pl.pallas_call(
        body, name="layer_a_bwd_dx", grid=(n_steps,),
        in_specs=[_rows(tm, D_MODEL), _rows(tm, 3 * A_WIDTH), _resident(w_in.shape), _const(after.shape)]
        + [s for s in upd_specs for _ in range(4)],
        out_specs=[_rows(tm, D_MODEL)] + [s for s in upd_specs for _ in range(4)],
        out_shape=[jax.ShapeDtypeStruct((t_len, D_MODEL), F32)] + [s for s in upd_shapes for _ in range(4)],
        compiler_params=_params(),
    )(dr1, dp, w_in, after, *upd_args)


def _layer_a_bwd_win(xt, dp, after):
    t_len = xt.shape[1]
    tm = TM_WIN
    n_steps = t_len // tm
    shard_cols = 3 * A_WIDTH // N_CHIPS
    half_rows = D_MODEL // 2

    def body(xt_ref, dp_ref, after_ref, gw_ref, acc):
        i = pl.program_id(1)

        @pl.when(i == 0)
        def _():
            acc[...] = jnp.zeros_like(acc)

        acc[...] += _dot(xt_ref[...], dp_ref[...])

        @pl.when(i == n_steps - 1)
        def _():
            for c in range(2):
                gw_ref[0, c] = acc[c * half_rows:(c + 1) * half_rows, :].astype(BF16)

    return pl.pallas_call(
        body, name="layer_a_bwd_win", grid=(N_CHIPS, n_steps),
        in_specs=[pl.BlockSpec((D_MODEL, tm), lambda j, i: (0, i)),
                  pl.BlockSpec((tm, shard_cols), lambda j, i: (i, j)), _const(after.shape)],
        out_specs=pl.BlockSpec((1, 2, half_rows, shard_cols), lambda j, i: (j, 0, 0, 0)),
        out_shape=jax.ShapeDtypeStruct((N_CHIPS, 2, half_rows, shard_cols), BF16),
        scratch_shapes=[pltpu.VMEM((D_MODEL, shard_cols), F32)],
        compiler_params=_params(("arbitrary", "arbitrary")),
    )(xt, dp, after)


def _bucket_onehot():
    dist = jnp.arange(CHUNK, dtype=jnp.int32)[None, :]
    max_exact = REL_BUCKETS // 2
    df = jnp.maximum(dist, 1).astype(F32)
    large = max_exact + (jnp.log(df / max_exact) / math.log(CHUNK / max_exact)
                         * (REL_BUCKETS - max_exact)).astype(jnp.int32)
    bucket = jnp.where(dist < max_exact, dist, jnp.minimum(large, REL_BUCKETS - 1))
    onehot = bucket == jnp.arange(REL_BUCKETS, dtype=jnp.int32)[:, None]
    return onehot.astype(F32)


def _bias_expand(rel_t, onehot):
    def body(rel_ref, oh_ref, out_ref):
        by_distance = jnp.dot(rel_ref[...], oh_ref[...], preferred_element_type=F32,
                              precision=lax.Precision.HIGHEST)
        for h in range(N_HEADS):
            rows = jnp.broadcast_to(by_distance[h:h + 1, :], (2 * CHUNK, CHUNK))
            out_ref[h] = pltpu.roll(rows, 0, 1, stride=1, stride_axis=0)

    return pl.pallas_call(
        body, name="bias_expand",
        out_shape=jax.ShapeDtypeStruct((N_HEADS, 2 * CHUNK, CHUNK), F32),
    )(rel_t, onehot)


def _bias_reduce(oh_ref, db_ref):
    sublane = lax.broadcasted_iota(jnp.int32, (_SUBLANES, CHUNK), 0)
    rows = []
    for h in range(N_HEADS):
        part = db_ref[h, 0:_SUBLANES, :]
        for a in range(1, 2 * CHUNK // _SUBLANES):
            tile = db_ref[h, a * _SUBLANES:(a + 1) * _SUBLANES, :]
            back = (-a * _SUBLANES) % CHUNK
            part += pltpu.roll(tile, back, 1) if back else tile
        total = jnp.where(sublane == 0, part, 0.0)
        for s in range(1, _SUBLANES):
            total += jnp.where(sublane == s, pltpu.roll(part, CHUNK - s, 1), 0.0)
        rows.append(jnp.sum(total, axis=0, keepdims=True))
    by_distance = jnp.concatenate(rows, axis=0)
    return lax.dot_general(oh_ref[...], by_distance, (((1,), (1,)), ((), ())),
                           preferred_element_type=F32, precision=lax.Precision.HIGHEST)


_SMALL_SHAPES = dict(w_spatial=(A_GROUPS, CHUNK, CHUNK), b_spatial=(A_GROUPS, CHUNK), attn_sinks=(1, N_HEADS),
                     rel_bias=(REL_BUCKETS, N_HEADS), post_ln_g=(2, D_MODEL), post_ln_b=(2, D_MODEL),
                     sgu_ln_g=(1, A_WIDTH), sgu_ln_b=(1, A_WIDTH), loss=(1, 1))
_SMALL_ORDER = tuple(_SMALL_SHAPES)


def _small_rows(name):
    shape = _SMALL_SHAPES[name]
    rows = math.prod(shape[:-1]) if shape[-1] < _LANES else math.prod(shape) // _LANES
    return -(-rows // _SUBLANES) * _SUBLANES


def _small_offset(name):
    return sum(_small_rows(n) for n in _SMALL_ORDER[:_SMALL_ORDER.index(name)])


def _pack_small(dws, dbsp, dsink, dbias, onehot, post_g, post_b, dgs, dbs, loss_vec):
    def body(dws_ref, dbsp_ref, dsink_ref, db_ref, oh_ref, g1_ref, g2_ref, b1_ref, b2_ref, dgs_ref, dbs_ref,
             loss_ref, out_ref):
        out_ref[...] = jnp.zeros_like(out_ref)

        def put_flat(name, refs):
            row = _small_offset(name)
            for ref in refs:
                for k in range(ref.shape[1] // _LANES):
                    out_ref[row:row + 1, :] = ref[:, k * _LANES:(k + 1) * _LANES]
                    row += 1

        row = _small_offset("w_spatial")
        for g in range(A_GROUPS):
            out_ref[row + g * CHUNK:row + (g + 1) * CHUNK, :] = dws_ref[g]
        row = _small_offset("b_spatial")
        out_ref[row:row + A_GROUPS, :] = dbsp_ref[...].T[0:A_GROUPS, :]
        lane = lax.broadcasted_iota(jnp.int32, (1, _LANES), 1)
        sinks = jnp.zeros((1, _LANES), F32)
        for h in range(N_HEADS):
            per_query = dsink_ref[h // GROUP:h // GROUP + 1, (h % GROUP) * CHUNK:(h % GROUP + 1) * CHUNK]
            sinks = jnp.where(lane == h, jnp.sum(per_query, axis=1, keepdims=True), sinks)
        row = _small_offset("attn_sinks")
        out_ref[row:row + 1, :] = sinks
        row = _small_offset("rel_bias")
        out_ref[row:row + REL_BUCKETS, 0:N_HEADS] = _bias_reduce(oh_ref, db_ref)
        put_flat("post_ln_g", [g1_ref, g2_ref])
        put_flat("post_ln_b", [b1_ref, b2_ref])
        put_flat("sgu_ln_g", [dgs_ref])
        put_flat("sgu_ln_b", [dbs_ref])
        row = _small_offset("loss")
        out_ref[row:row + 1, 0:1] = (0.5 / D_MODEL) * jnp.sum(loss_ref[...], axis=1, keepdims=True)

    total_rows = sum(_small_rows(n) for n in _SMALL_ORDER)
    return pl.pallas_call(
        body, name="pack_small",
        out_shape=jax.ShapeDtypeStruct((total_rows, _LANES), F32),
    )(dws, dbsp, dsink, dbias, onehot, *post_g, *post_b, dgs, dbs, loss_vec)


def _place():
    return lax.axis_index("x"), lax.axis_index("y"), lax.axis_index("c")


RELAY_PIECES = 4


def _shard_window(full_ref, shard_shape, col_sharded, s, half, piece=None):
    rows, cols = shard_shape
    if half is None:
        start, size = 0, rows
    elif piece is None:
        start, size = half * (rows // 2), rows // 2
    else:
        size = rows // 2 // RELAY_PIECES
        start = (half * RELAY_PIECES + piece) * size
    if col_sharded:
        return full_ref.at[pl.ds(start, size), pl.ds(s * cols, cols)]
    return full_ref.at[pl.ds(s * rows + start, size), :]


def _other_chips(x, y):
    return [(1 - x, y), (x, 1 - y), (1 - x, 1 - y)]


def _gather_weights(shards, col_sharded, fetch, ln_shard):
    n_w = len(shards)
    fetched = [w for w in range(n_w) if fetch[w]]
    full_shapes = []
    for w, cs in zip(shards, col_sharded):
        r, c = w.shape
        full_shapes.append((r, c * N_CHIPS) if cs else (r * N_CHIPS, c))

    def body(*refs):
        in_refs = refs[:n_w]
        ln_ref = refs[n_w]
        full_refs = refs[n_w + 1:2 * n_w + 1]
        ln_full = refs[2 * n_w + 1]
        raw = refs[2 * n_w + 2:3 * n_w + 2]
        stage = refs[3 * n_w + 2:4 * n_w + 2]
        send_sems, recv_sems, load_sems, local_sems, ln_send, ln_recv = refs[4 * n_w + 2:]
        x, y, c = _place()
        s_me = 2 * x + y
        chips = _other_chips(x, y)
        pieces = range(RELAY_PIECES)

        def shard_window(w, s, half, piece=None):
            return _shard_window(full_refs[w], shards[w].shape, col_sharded[w], s, half, piece)

        def piece_rows(w, half, piece):
            rows = shards[w].shape[0] // 2 // RELAY_PIECES
            return pl.ds(pl.multiple_of((half * RELAY_PIECES + piece) * rows, rows), rows)

        def ici_copy(w, k, sender_shard, piece):
            idx = (w * 3 + k) * RELAY_PIECES + piece
            return pltpu.make_async_remote_copy(
                src_ref=stage[w].at[piece_rows(w, c, piece), :], dst_ref=shard_window(w, sender_shard, c, piece),
                send_sem=send_sems.at[idx], recv_sem=recv_sems.at[idx],
                device_id=(*chips[k], c), device_id_type=MESH)

        def d2d_copy(w, k, half, piece):
            s_k = 2 * chips[k][0] + chips[k][1]
            win = shard_window(w, s_k, half, piece)
            idx = (3 * n_w + w * 3 + k) * RELAY_PIECES + piece
            return pltpu.make_async_remote_copy(
                src_ref=win, dst_ref=win, send_sem=send_sems.at[idx], recv_sem=recv_sems.at[idx],
                device_id=(x, y, 1 - c), device_id_type=MESH)

        def ln_copy(k, slot):
            return pltpu.make_async_remote_copy(
                src_ref=ln_ref, dst_ref=ln_full.at[slot], send_sem=ln_send.at[k], recv_sem=ln_recv.at[k],
                device_id=(*chips[k], c), device_id_type=MESH)

        loads = []

        def load(w, rows):
            window = (rows, slice(None)) if rows is not None else (slice(None), slice(None))
            cp = pltpu.make_async_copy(in_refs[w].at[window], raw[w].at[window], load_sems.at[len(loads)])
            cp.start()
            loads.append((cp, w, window))

        for half in (c, 1 - c):
            for w in fetched:
                for q in pieces:
                    load(w, piece_rows(w, half, q))
        for w in range(n_w):
            if not fetch[w]:
                load(w, None)

        def to_bf16(k):
            cp, w, window = loads[k]
            cp.wait()
            stage[w][window] = raw[w][window].astype(BF16)

        ln_full[s_me] = ln_ref[...]
        def shard_of(k):
            return 2 * chips[k][0] + chips[k][1]

        relay_from = jnp.where(c == 0, shard_of(0), shard_of(1))
        relay_to = (jnp.where(c == 0, x, 1 - x), jnp.where(c == 0, 1 - y, y), c)

        def relay_copy(w, sender_shard, piece):
            win = shard_window(w, sender_shard, c, piece)
            idx = (w * 3 + 2) * RELAY_PIECES + piece
            return pltpu.make_async_remote_copy(
                src_ref=win, dst_ref=win, send_sem=send_sems.at[idx], recv_sem=recv_sems.at[idx],
                device_id=relay_to, device_id_type=MESH)

        first = [ln_copy(k, s_me) for k in range(3)]
        for cp in first:
            cp.start()
        n_sent = 0
        for w in fetched:
            for q in pieces:
                to_bf16(n_sent)
                n_sent += 1
                for k in range(2):
                    cp = ici_copy(w, k, s_me, q)
                    cp.start()
                    first.append(cp)
        for k in range(n_sent, len(loads)):
            to_bf16(k)
        own = [pltpu.make_async_copy(stage[w], shard_window(w, s_me, None), local_sems.at[w]) for w in range(n_w)]
        for cp in own:
            cp.start()
        passed = []
        for w in fetched:
            for q in pieces:
                for k in range(2):
                    ici_copy(w, k, shard_of(k), q).wait_recv()
                relay = relay_copy(w, relay_from, q)
                relay.start()
                passed.append(relay)
                for k in range(2):
                    fwd = d2d_copy(w, k, c, q)
                    fwd.start()
                    passed.append(fwd)
        for w in fetched:
            for q in pieces:
                relay_copy(w, shard_of(2), q).wait_recv()
                fwd = d2d_copy(w, 2, c, q)
                fwd.start()
                passed.append(fwd)
        for w in fetched:
            for k in range(3):
                for q in pieces:
                    d2d_copy(w, k, 1 - c, q).wait_recv()
        for k in range(3):
            ln_copy(k, 2 * chips[k][0] + chips[k][1]).wait_recv()
        for cp in first + passed:
            cp.wait_send()
        for cp in own:
            cp.wait()

    vmem = pl.BlockSpec(memory_space=pltpu.VMEM)
    hbm = pl.BlockSpec(memory_space=pl.ANY)
    return pl.pallas_call(
        body, name="gather_weights",
        in_specs=[hbm] * n_w + [vmem],
        out_specs=[hbm] * n_w + [vmem],
        out_shape=[jax.ShapeDtypeStruct(s, BF16) for s in full_shapes]
        + [jax.ShapeDtypeStruct((N_CHIPS,) + ln_shard.shape, F32)],
        scratch_shapes=[pltpu.VMEM(w.shape, F32) for w in shards] + [pltpu.VMEM(w.shape, BF16) for w in shards]
        + [pltpu.SemaphoreType.DMA((6 * RELAY_PIECES * n_w,)), pltpu.SemaphoreType.DMA((6 * RELAY_PIECES * n_w,)),
           pltpu.SemaphoreType.DMA((2 * RELAY_PIECES * len(fetched) + n_w - len(fetched),)),
           pltpu.SemaphoreType.DMA((n_w,)), pltpu.SemaphoreType.DMA((3,)), pltpu.SemaphoreType.DMA((3,))],
        compiler_params=pltpu.CompilerParams(vmem_limit_bytes=VMEM_LIMIT),
    )(*shards, ln_shard)


def _fetch_copy(full_ref, shard_shape, col_sharded, sender_shard, send_sems, recv_sems, idx, chip, c):
    win = _shard_window(full_ref, shard_shape, col_sharded, sender_shard, None)
    return pltpu.make_async_remote_copy(src_ref=win, dst_ref=win, send_sem=send_sems.at[idx],
                                        recv_sem=recv_sems.at[idx], device_id=(*chip, c), device_id_type=MESH)


def _fetch_start(fulls, shard_shapes, col_sharded):
    n = len(fulls)

    def body(*refs):
        full = refs[:n]
        send_sems, recv_sems = refs[n], refs[n + 1]
        token = refs[-1]
        x, y, c = _place()
        for w in range(n):
            for k, chip in enumerate(_other_chips(x, y)):
                _fetch_copy(full[w], shard_shapes[w], col_sharded[w], 2 * x + y, send_sems, recv_sems, w * 3 + k,
                            chip, c).start()
        token[...] = jnp.zeros_like(token)

    outs = pl.pallas_call(
        body, name="fetch_start",
        out_shape=(pltpu.SemaphoreType.DMA((3 * n,)), pltpu.SemaphoreType.DMA((3 * n,)),
                   *[pltpu.HBM(f.shape, f.dtype) for f in fulls], jax.ShapeDtypeStruct((8, 128), F32)),
        in_specs=[_HBM] * n,
        out_specs=(_SEM, _SEM, *([_HBM] * n), pl.BlockSpec(memory_space=pltpu.VMEM)),
        input_output_aliases={i: 2 + i for i in range(n)},
        compiler_params=pltpu.CompilerParams(has_side_effects=pltpu.SideEffectType.DATAFLOW_SIDE_EFFECTING),
    )(*[pltpu.with_memory_space_constraint(f, pltpu.HBM) for f in fulls])
    return dict(send=outs[0], recv=outs[1], full=list(outs[2:2 + n])), outs[-1]


def _fetch_wait(group, shard_shapes, col_sharded, after):
    n = len(group["full"])

    def body(*refs):
        full = refs[:n]
        send_sems, recv_sems = refs[n], refs[n + 1]
        x, y, c = _place()
        for w in range(n):
            for k, chip in enumerate(_other_chips(x, y)):
                _fetch_copy(full[w], shard_shapes[w], col_sharded[w], 2 * x + y, send_sems, recv_sems, w * 3 + k,
                            chip, c).wait_send()
                _fetch_copy(full[w], shard_shapes[w], col_sharded[w], 2 * chip[0] + chip[1], send_sems, recv_sems,
                            w * 3 + k, chip, c).wait_recv()

    outs = pl.pallas_call(
        body, name="fetch_wait", out_shape=tuple(pltpu.HBM(f.shape, f.dtype) for f in group["full"]),
        in_specs=[_HBM] * n + [_SEM, _SEM, pl.BlockSpec(memory_space=pl.ANY)],
        out_specs=tuple([_HBM] * n), input_output_aliases={i: i for i in range(n)},
        compiler_params=pltpu.CompilerParams(has_side_effects=pltpu.SideEffectType.DATAFLOW_SIDE_EFFECTING),
    )(*group["full"], group["send"], group["recv"], after)
    return list(outs)


_HBM = pl.BlockSpec(memory_space=pltpu.HBM)
_SEM = pl.BlockSpec(memory_space=pltpu.SEMAPHORE)
_N_PEER = N_DEV - 1


def _peer(x, y, c, k):
    return (x + (k >> 2)) % 2, (y + ((k >> 1) & 1)) % 2, (c + (k & 1)) % 2


def _exchange_copy(src_ref, land_ref, sliced, send_sems, recv_sems, idx, x, y, c, k):
    px, py, pc = _peer(x, y, c, k)
    src = src_ref.at[4 * px + 2 * py + pc] if sliced else src_ref
    return pltpu.make_async_remote_copy(
        src_ref=src, dst_ref=land_ref.at[4 * x + 2 * y + c],
        send_sem=send_sems.at[idx], recv_sem=recv_sems.at[idx], device_id=(px, py, pc), device_id_type=MESH)


def _exchange_start(tag, arrays, sliced):
    n = len(arrays)
    lands = [lax.empty(a.shape if s else (N_DEV,) + a.shape, a.dtype) for a, s in zip(arrays, sliced)]

    def body(*refs):
        src, land = refs[:n], refs[n:2 * n]
        send_sems, recv_sems = refs[2 * n], refs[2 * n + 1]
        token = refs[-1]
        x, y, c = _place()
        for w in range(n):
            for k in range(1, N_DEV):
                _exchange_copy(src[w], land[w], sliced[w], send_sems, recv_sems, w * _N_PEER + k - 1, x, y, c, k).start()
        token[...] = jnp.zeros_like(token)

    outs = pl.pallas_call(
        body, name="exchange_start_" + tag,
        out_shape=(pltpu.SemaphoreType.DMA((n * _N_PEER,)), pltpu.SemaphoreType.DMA((n * _N_PEER,)),
                   *[pltpu.HBM(a.shape, a.dtype) for a in arrays], *[pltpu.HBM(l.shape, l.dtype) for l in lands],
                   jax.ShapeDtypeStruct((8, 128), F32)),
        in_specs=[_HBM] * (2 * n),
        out_specs=(_SEM, _SEM, *([_HBM] * (2 * n)), pl.BlockSpec(memory_space=pltpu.VMEM)),
        input_output_aliases={i: 2 + i for i in range(2 * n)},
        compiler_params=pltpu.CompilerParams(has_side_effects=pltpu.SideEffectType.DATAFLOW_SIDE_EFFECTING),
    )(*[pltpu.with_memory_space_constraint(a, pltpu.HBM) for a in arrays],
      *[pltpu.with_memory_space_constraint(l, pltpu.HBM) for l in lands])
    return dict(send=outs[0], recv=outs[1], src=list(outs[2:2 + n]), land=list(outs[2 + n:2 + 2 * n]),
                sliced=list(sliced)), outs[-1]


def _exchange_wait(tag, groups, after):
    counts = [len(g["src"]) for g in groups]
    total = sum(counts)

    def body(*refs):
        pos = 0
        x, y, c = _place()
        for g, n in zip(groups, counts):
            src, land = refs[pos:pos + n], refs[pos + n:pos + 2 * n]
            send_sems, recv_sems = refs[pos + 2 * n], refs[pos + 2 * n + 1]
            pos += 2 * n + 2
            for w in range(n):
                for k in range(1, N_DEV):
                    cp = _exchange_copy(src[w], land[w], g["sliced"][w], send_sems, recv_sems,
                                        w * _N_PEER + k - 1, x, y, c, k)
                    cp.wait_send()
                    cp.wait_recv()

    operands, in_specs, aliases, out_shape = [], [], {}, []
    for g in groups:
        for a in g["src"] + g["land"]:
            aliases[len(operands)] = len(out_shape)
            out_shape.append(pltpu.HBM(a.shape, a.dtype))
            operands.append(a)
            in_specs.append(_HBM)
        operands += [g["send"], g["recv"]]
        in_specs += [_SEM, _SEM]
    operands.append(after)
    in_specs.append(pl.BlockSpec(memory_space=pl.ANY))
    outs = pl.pallas_call(
        body, name="exchange_wait_" + tag, out_shape=tuple(out_shape), in_specs=in_specs,
        out_specs=tuple([_HBM] * (2 * total)), input_output_aliases=aliases,
        compiler_params=pltpu.CompilerParams(has_side_effects=pltpu.SideEffectType.DATAFLOW_SIDE_EFFECTING),
    )(*operands)
    srcs, lands, pos = [], [], 0
    for n in counts:
        srcs += list(outs[pos:pos + n])
        lands += list(outs[pos + n:pos + 2 * n])
        pos += 2 * n
    return srcs, lands


def _sum_and_swap(tag, pieces, lands, small=None, small_land=None):
    n_w = len(pieces)
    n_small = 0 if small is None else 1

    def body(*refs):
        g_refs, land_refs = refs[:n_w], refs[n_w:2 * n_w]
        pos = 2 * n_w + 2 * n_small
        out_refs = refs[pos:pos + n_w]
        pos += n_w + n_small
        bufs = refs[pos:pos + n_w]
        load_sems, swap_send, swap_recv = refs[pos + n_w + 2 * n_small:]
        x, y, c = _place()
        me = 4 * x + 2 * y + c

        def slot(k):
            px, py, pc = _peer(x, y, c, k)
            return 4 * px + 2 * py + pc

        def swap_copy(w, half):
            rows = pieces[w].shape[1]
            win = out_refs[w].at[pl.ds(pl.multiple_of(half * rows, rows), rows), :]
            return pltpu.make_async_remote_copy(
                src_ref=win, dst_ref=win, send_sem=swap_send.at[w], recv_sem=swap_recv.at[w],
                device_id=(x, y, 1 - c), device_id_type=MESH)

        loads = []
        for w in range(n_w):
            per_w = [pltpu.make_async_copy(g_refs[w].at[me], bufs[w].at[me], load_sems.at[w * N_DEV])]
            per_w += [pltpu.make_async_copy(land_refs[w].at[slot(k)], bufs[w].at[slot(k)], load_sems.at[w * N_DEV + k])
                      for k in range(1, N_DEV)]
            loads.append(per_w)
        small_loads = []
        if n_small:
            small_ref, small_land_ref = refs[2 * n_w], refs[2 * n_w + 1]
            small_out = refs[2 * n_w + 2 + n_w]
            small_buf, small_sems = refs[pos + n_w], refs[pos + n_w + 1]
            small_loads = [pltpu.make_async_copy(small_land_ref.at[slot(k)], small_buf.at[slot(k)],
                                                 small_sems.at[k - 1]) for k in range(1, N_DEV)]
        for cp in [cp for per_w in loads for cp in per_w] + small_loads:
            cp.start()
        if n_small:
            small_buf[me] = small_ref[...]
        swaps = []
        for w in range(n_w):
            for cp in loads[w]:
                cp.wait()
            rows = pieces[w].shape[1]
            total = bufs[w][0].astype(F32)
            for p in range(1, N_DEV):
                total += bufs[w][p].astype(F32)
            out_refs[w][pl.ds(pl.multiple_of(c * rows, rows), rows), :] = total
            sw = swap_copy(w, c)
            sw.start()
            swaps.append(sw)
        if n_small:
            for cp in small_loads:
                cp.wait()
            total = small_buf[0]
            for p in range(1, N_DEV):
                total += small_buf[p]
            small_out[...] = total
        for w in range(n_w):
            swap_copy(w, 1 - c).wait_recv()
        for sw in swaps:
            sw.wait_send()

    vmem = pl.BlockSpec(memory_space=pltpu.VMEM)
    hbm = pl.BlockSpec(memory_space=pl.ANY)
    small_args = [small, small_land] if n_small else []
    small_shapes = [jax.ShapeDtypeStruct(small.shape, F32)] if n_small else []
    small_scratch = ([pltpu.VMEM((N_DEV,) + small.shape, F32), pltpu.SemaphoreType.DMA((_N_PEER,))]
                     if n_small else [])
    return pl.pallas_call(
        body, name="sum_and_swap_" + tag,
        in_specs=[hbm] * (2 * n_w) + [vmem, hbm] * n_small,
        out_specs=[vmem] * (n_w + n_small),
        out_shape=[jax.ShapeDtypeStruct((2 * p.shape[1], p.shape[2]), F32) for p in pieces] + small_shapes,
        scratch_shapes=[pltpu.VMEM(p.shape, BF16) for p in pieces] + small_scratch
        + [pltpu.SemaphoreType.DMA((n_w * N_DEV,)), pltpu.SemaphoreType.DMA((n_w,)),
           pltpu.SemaphoreType.DMA((n_w,))],
        compiler_params=pltpu.CompilerParams(vmem_limit_bytes=VMEM_LIMIT),
    )(*pieces, *lands, *small_args)


UPDATE_ROWS = 128


def _sum_swap_update(tag, piece, land, w, m, v):
    _, rows, cols = piece.shape
    n_half = rows // UPDATE_ROWS
    n_chunks = 2 * n_half

    def body(g_ref, land_ref, w_ref, m_ref, v_ref, g_out, d_out, nm_out, nv_out,
             buf, g_scr, stage_in, stage_out, load_sems, in_sems, out_sems, g_sems, swap_send, swap_recv):
        x, y, c = _place()
        me = 4 * x + 2 * y + c

        def slot(k):
            px, py, pc = _peer(x, y, c, k)
            return 4 * px + 2 * py + pc

        def half_rows(half):
            return pl.ds(pl.multiple_of(half * rows, rows), rows)

        def chunk_rows(k):
            half = c if k < n_half else 1 - c
            return pl.ds(pl.multiple_of(half * rows + (k % n_half) * UPDATE_ROWS, UPDATE_ROWS), UPDATE_ROWS)

        def swap_copy(half):
            win = g_scr.at[half_rows(half), :]
            return pltpu.make_async_remote_copy(src_ref=win, dst_ref=win, send_sem=swap_send.at[0],
                                                recv_sem=swap_recv.at[0], device_id=(x, y, 1 - c),
                                                device_id_type=MESH)

        def in_copies(k):
            return [pltpu.make_async_copy(src.at[chunk_rows(k), :], stage_in.at[k % 2, a], in_sems.at[k % 2, a])
                    for a, src in enumerate((w_ref, m_ref, v_ref))]

        def out_copies(k):
            return [pltpu.make_async_copy(stage_out.at[k % 2, a], dst.at[chunk_rows(k), :], out_sems.at[k % 2, a])
                    for a, dst in enumerate((d_out, nm_out, nv_out))]

        def g_copy(half):
            return pltpu.make_async_copy(g_scr.at[half_rows(half), :], g_out.at[half_rows(half), :], g_sems.at[half])

        loads = [pltpu.make_async_copy(g_ref.at[me], buf.at[me], load_sems.at[0])]
        loads += [pltpu.make_async_copy(land_ref.at[slot(k)], buf.at[slot(k)], load_sems.at[k])
                  for k in range(1, N_DEV)]
        for cp in loads:
            cp.start()
        for cp in in_copies(0):
            cp.start()
        for cp in loads:
            cp.wait()
        total = buf[0].astype(F32)
        for p in range(1, N_DEV):
            total += buf[p].astype(F32)
        g_scr[half_rows(c), :] = total
        swap_copy(c).start()
        g_copy(c).start()

        for k in range(n_chunks):
            if k + 1 < n_chunks:
                for cp in in_copies(k + 1):
                    cp.start()
            if k == n_half:
                swap_copy(1 - c).wait_recv()
                g_copy(1 - c).start()
            for cp in in_copies(k):
                cp.wait()
            if k >= 2:
                for cp in out_copies(k - 2):
                    cp.wait()
            w_t, m_t, v_t = (stage_in[k % 2, a] for a in range(3))
            results = _adamw_values(w_t, g_scr[chunk_rows(k), :], m_t, v_t)
            for a in range(3):
                stage_out[k % 2, a] = results[a]
            for cp in out_copies(k):
                cp.start()
        for k in range(n_chunks - 2, n_chunks):
            for cp in out_copies(k):
                cp.wait()
        g_copy(c).wait()
        g_copy(1 - c).wait()
        swap_copy(c).wait_send()

    hbm = pl.BlockSpec(memory_space=pl.ANY)
    whole = jax.ShapeDtypeStruct((2 * rows, cols), F32)
    return pl.pallas_call(
        body, name="sum_swap_update_" + tag,
        in_specs=[hbm] * 5, out_specs=[hbm] * 4, out_shape=[whole] * 4,
        scratch_shapes=[pltpu.VMEM(piece.shape, BF16), pltpu.VMEM((2 * rows, cols), F32),
                        pltpu.VMEM((2, 3, UPDATE_ROWS, cols), F32), pltpu.VMEM((2, 3, UPDATE_ROWS, cols), F32),
                        pltpu.SemaphoreType.DMA((N_DEV,)), pltpu.SemaphoreType.DMA((2, 3)),
                        pltpu.SemaphoreType.DMA((2, 3)), pltpu.SemaphoreType.DMA((2,)),
                        pltpu.SemaphoreType.DMA((1,)), pltpu.SemaphoreType.DMA((1,))],
        compiler_params=pltpu.CompilerParams(vmem_limit_bytes=VMEM_LIMIT),
    )(piece, land, w, m, v)


def _adamw_values(w, g_t, m, v):
    c1 = 1.0 - ADAM_B1 ** ADAM_STEP
    c2 = 1.0 - ADAM_B2 ** ADAM_STEP
    nm = ADAM_B1 * m + (1.0 - ADAM_B1) * g_t
    nv = ADAM_B2 * v + (1.0 - ADAM_B2) * (g_t * g_t)
    return -ADAM_LR * ((nm / c1) / (jnp.sqrt(nv / c2) + ADAM_EPS) + ADAM_WD * w), nm, nv


def _adamw_update(w_ref, g_ref, m_ref, v_ref, d_ref, nm_ref, nv_ref):
    d_ref[...], nm_ref[...], nv_ref[...] = _adamw_values(w_ref[...], g_ref[...], m_ref[...], v_ref[...])


def _adamw_small(packed, shard_index, names, weights, moments_m, moments_v):
    n = len(names)
    shapes = [weights[name].shape for name in names]
    flat = [a[name].reshape(-1, a[name].shape[-1]) for name in names for a in (weights, moments_m, moments_v)]

    def body(packed_ref, shard_ref, *refs):
        loss_row = _small_offset("loss")
        refs[-1][...] = packed_ref[loss_row:loss_row + 1, 0:1]
        for k, name in enumerate(names):
            w_ref, m_ref, v_ref = refs[3 * k:3 * k + 3]
            g_ref, d_ref, nm_ref, nv_ref = refs[3 * n + 4 * k:3 * n + 4 * k + 4]
            rows, cols = w_ref.shape
            first = _small_offset(name)
            if cols <= _LANES:
                blocks = [(slice(0, rows), packed_ref[first:first + rows, 0:cols])]
            else:
                per_row = cols // _LANES
                if cols < _SMALL_SHAPES[name][-1]:
                    first = first + shard_ref[0] * per_row
                blocks = [(slice(i, i + 1),
                           jnp.concatenate([packed_ref[pl.ds(first + i * per_row + j, 1), :] for j in range(per_row)],
                                           axis=1)) for i in range(rows)]
            for at, g_t in blocks:
                g_ref[at, :] = g_t
                d_ref[at, :], nm_ref[at, :], nv_ref[at, :] = _adamw_values(w_ref[at, :], g_t, m_ref[at, :],
                                                                           v_ref[at, :])

    vmem = pl.BlockSpec(memory_space=pltpu.VMEM)
    outs = pl.pallas_call(
        body, name="adamw_small",
        in_specs=[vmem, pl.BlockSpec(memory_space=pltpu.SMEM)] + [vmem] * (3 * n),
        out_shape=[jax.ShapeDtypeStruct(flat[3 * k].shape, F32) for k in range(n) for _ in range(4)]
        + [jax.ShapeDtypeStruct((1, 1), F32)],
    )(packed, shard_index.reshape(1).astype(jnp.int32), *flat)
    return [tuple(o.reshape(shapes[k]) for o in outs[4 * k:4 * k + 4]) for k in range(n)], outs[-1].reshape(())


def _no_send(tag, arrays, sliced):
    return jnp.zeros((8, 128), F32)


def _local_step(x, tgt, w_in_a, later_weights, first_after, sgu_ln_g, sgu_ln_b, w_spatial, b_spatial,
                attn_sinks, rel_bias, post_ln_g, post_ln_b, send=_no_send):
    bsp_t = b_spatial.T
    g1, b1 = post_ln_g[0:1], post_ln_b[0:1]
    g2, b2 = post_ln_g[1:2], post_ln_b[1:2]
    onehot = _bucket_onehot()
    bias = _bias_expand(rel_bias.T, onehot)
    win = _window_tables()

    xt, u, vh, z, rv, y = _layer_a_fwd(x, w_in_a, sgu_ln_g, sgu_ln_b, w_spatial, bsp_t, first_after)
    w_out_a, w_kv, w_in_b, w_out_b = later_weights(y)
    xh1, rstd1, q, zb, kd, vd = _layer_b_proj(x, y, w_out_a, g1, b1, w_in_b, w_kv)
    o, probs, sink_probs, dr2, loss_vec, dg2, db2 = _layer_b_fwd(q, zb, kd, vd, bias, win, attn_sinks, xh1, g1, b1,
                                                                 w_out_b, g2, b2, tgt)
    dq, dzb, dkd, dvd, carry_k, carry_v, gw_out_b, dsink, dbias = _layer_b_bwd_attn(
        dr2, zb, o, q, kd, vd, probs, sink_probs, w_out_b)
    dr1, dg1, db1, gw_in_b, gw_kv = _layer_b_bwd_proj(xh1, rstd1, g1, b1, dr2, dq, dzb, dkd, dvd, carry_k, carry_v,
                                                      w_in_b, w_kv)
    gw_out_b = gw_out_b.reshape(N_DEV, -1, D_MODEL)
    gw_kv = gw_kv.reshape(N_DEV, -1, 2 * PAIR)
    after = send("b", [gw_out_b, gw_in_b, gw_kv], [True, True, True])
    dp, gw_out_a, dws, dbsp, dgs, dbs = _layer_a_bwd_mix(dr1, u, vh, z, y, rv, w_out_a, sgu_ln_g, sgu_ln_b,
                                                         w_spatial, bsp_t, after)
    gw_out_a = gw_out_a.reshape(N_DEV, -1, D_MODEL)
    small = _pack_small(dws, dbsp, dsink, dbias, onehot, (dg1, dg2), (db1, db2), dgs, dbs, loss_vec)
    after = send("a_out", [gw_out_a, small], [True, False])
    gw_in_a = _layer_a_bwd_win(xt, dp, after).reshape(N_DEV, D_MODEL // 2, -1)
    after = send("a_in", [gw_in_a], [True])
    after, updates = after if isinstance(after, tuple) else (after, ())
    grad_x, *updated = _layer_a_bwd_dx(dr1, dp, w_in_a, after, updates)

    pieces = [gw_in_a, gw_out_a, gw_kv, gw_in_b, gw_out_b]
    return grad_x, pieces, small, updated


def kernel(x, w_in_a, sgu_ln_g, sgu_ln_b, w_spatial, b_spatial, w_out_a, w_kv, w_in_b, attn_sinks, rel_bias, w_out_b, post_ln_g, post_ln_b, loss_target, m_w_in_a, m_sgu_ln_g, m_sgu_ln_b, m_w_spatial, m_b_spatial, m_w_out_a, m_w_kv, m_w_in_b, m_attn_sinks, m_rel_bias, m_w_out_b, m_post_ln_g, m_post_ln_b, v_w_in_a, v_sgu_ln_g, v_sgu_ln_b, v_w_spatial, v_b_spatial, v_w_out_a, v_w_kv, v_w_in_b, v_attn_sinks, v_rel_bias, v_w_out_b, v_post_ln_g, v_post_ln_b):
    weights = dict(w_in_a=w_in_a, sgu_ln_g=sgu_ln_g, sgu_ln_b=sgu_ln_b, w_spatial=w_spatial, b_spatial=b_spatial,
                   w_out_a=w_out_a, w_kv=w_kv, w_in_b=w_in_b, attn_sinks=attn_sinks, rel_bias=rel_bias,
                   w_out_b=w_out_b, post_ln_g=post_ln_g, post_ln_b=post_ln_b)
    moments_m = dict(w_in_a=m_w_in_a, sgu_ln_g=m_sgu_ln_g, sgu_ln_b=m_sgu_ln_b, w_spatial=m_w_spatial,
                     b_spatial=m_b_spatial, w_out_a=m_w_out_a, w_kv=m_w_kv, w_in_b=m_w_in_b,
                     attn_sinks=m_attn_sinks, rel_bias=m_rel_bias, w_out_b=m_w_out_b, post_ln_g=m_post_ln_g,
                     post_ln_b=m_post_ln_b)
    moments_v = dict(w_in_a=v_w_in_a, sgu_ln_g=v_sgu_ln_g, sgu_ln_b=v_sgu_ln_b, w_spatial=v_w_spatial,
                     b_spatial=v_b_spatial, w_out_a=v_w_out_a, w_kv=v_w_kv, w_in_b=v_w_in_b,
                     attn_sinks=v_attn_sinks, rel_bias=v_rel_bias, w_out_b=v_w_out_b, post_ln_g=v_post_ln_g,
                     post_ln_b=v_post_ln_b)
    order = ("w_in_a", "sgu_ln_g", "sgu_ln_b", "w_spatial", "b_spatial", "w_out_a", "w_kv", "w_in_b", "attn_sinks",
             "rel_bias", "w_out_b", "post_ln_g", "post_ln_b")

    shard_index = 2 * lax.axis_index("x") + lax.axis_index("y")
    ln_shard = jnp.concatenate([sgu_ln_g, sgu_ln_b], axis=0)
    shards = [w_in_a[0], w_out_a[0], w_kv, w_in_b[0], w_out_b[0]]
    col_sharded = [True, False, False, True, False]
    full_in_a, *later, ln_full = _gather_weights(shards, col_sharded, [True, False, False, False, False], ln_shard)
    ln_full = jnp.transpose(ln_full, (1, 0, 2)).reshape(2, A_WIDTH)
    later_shapes = [s.shape for s in shards[1:]]
    fetch_group, fetch_token = _fetch_start(later, later_shapes, col_sharded[1:])

    def later_weights(y):
        return _fetch_wait(fetch_group, later_shapes, col_sharded[1:], y)

    groups, grads, deltas, new_m, new_v, scalars = {}, {}, {}, {}, {}, {}
    early = ("w_out_b", "w_in_b", "w_kv", "w_out_a")

    def two_dim(a):
        return a.reshape(-1, a.shape[-1])

    def send(tag, arrays, sliced):
        groups[tag], token = _exchange_start(tag, arrays, sliced)
        if tag != "a_in":
            return token
        srcs, lands = _exchange_wait("early", [groups["b"], groups["a_out"]], token)
        *reduced, packed_sum = _sum_and_swap("early", srcs[:4], lands[:4], srcs[4], lands[4])
        updates = [(two_dim(weights[n]), g, two_dim(moments_m[n]), two_dim(moments_v[n]))
                   for n, g in zip(early, reduced)]
        small_names = ("sgu_ln_g", "sgu_ln_b", "w_spatial", "b_spatial", "attn_sinks", "rel_bias", "post_ln_g",
                       "post_ln_b")
        small_updates, scalars["loss"] = _adamw_small(packed_sum, shard_index, small_names, weights, moments_m,
                                                      moments_v)
        for name, (g, d, nm, nv) in zip(small_names, small_updates):
            grads[name], deltas[name], new_m[name], new_v[name] = g, d, nm, nv
        return new_m["b_spatial"].reshape(A_GROUPS, CHUNK), updates

    grad_x, _, _, updated = _local_step(
        x[0], loss_target[0], full_in_a, later_weights, fetch_token, ln_full[0:1], ln_full[1:2], w_spatial[0],
        b_spatial[0], attn_sinks, rel_bias, post_ln_g, post_ln_b, send=send)
    for k, name in enumerate(early):
        grads[name], deltas[name], new_m[name], new_v[name] = [
            a.reshape(weights[name].shape) for a in updated[4 * k:4 * k + 4]]

    srcs, lands = _exchange_wait("late", [groups["a_in"]], grad_x)
    grads["w_in_a"], deltas["w_in_a"], new_m["w_in_a"], new_v["w_in_a"] = [
        a.reshape(w_in_a.shape) for a in _sum_swap_update("late", srcs[0], lands[0], w_in_a[0], m_w_in_a[0],
                                                          v_w_in_a[0])]
    return (scalars["loss"], grad_x[None], *[grads[n] for n in order], *[deltas[n] for n in order],
            *[new_m[n] for n in order], *[new_v[n] for n in order])
```

```python
import functools
import math

import jax
import jax.numpy as jnp
from jax import lax
from jax.experimental import pallas as pl
from jax.experimental.pallas import tpu as pltpu

F32 = jnp.float32
BF16 = jnp.bfloat16

D_MODEL = 1024
A_WIDTH = 2048
A_GROUPS = 8
A_GROUP_DIM = 256
CHUNK = 128
N_HEADS = 16
N_KV = 2
HEAD_DIM = 64
PAIR = 2 * HEAD_DIM
B_WIDTH = 1024
REL_BUCKETS = 32
ALPHA = 4.0 ** 0.25
LN_EPS = 1e-5
NEG_INF = -1e30
SCALE = HEAD_DIM ** -0.5

ADAM_LR = 0.001
ADAM_B1 = 0.9
ADAM_B2 = 0.999
ADAM_EPS = 1e-08
ADAM_WD = 0.01
ADAM_STEP = 10

N_DEV = 8
N_CHIPS = 4
MESH = pl.DeviceIdType.MESH
VMEM_LIMIT = 56 * 1024 * 1024

TM_ATTN = 256
TM_BWD_ATTN = 512
TM_MM = 512
TM_WIN = 1024
_LANES = 128
_SUBLANES = 8


def _dot(a, b):
    return jnp.dot(a, b, preferred_element_type=F32)


def _dot_nt(a, b):
    return lax.dot_general(a, b, (((1,), (1,)), ((), ())), preferred_element_type=F32)


def _dot_tn(a, b):
    return lax.dot_general(a, b, (((0,), (0,)), ((), ())), preferred_element_type=F32)


def _ln_fwd(r):
    mu = jnp.mean(r, axis=-1, keepdims=True)
    rc = r - mu
    var = jnp.mean(rc * rc, axis=-1, keepdims=True)
    rstd = lax.rsqrt(var + LN_EPS)
    return rc * rstd, rstd


def _ln_bwd(dxh, xh, rstd):
    m1 = jnp.mean(dxh, axis=-1, keepdims=True)
    m2 = jnp.mean(dxh * xh, axis=-1, keepdims=True)
    return rstd * (dxh - m1 - xh * m2)


def _silu_parts(z):
    sg = jax.nn.sigmoid(z)
    return z * sg, sg * (1.0 + z * (1.0 - sg))


def _dup_halves(blk):
    sw = pltpu.roll(blk, HEAD_DIM, 1)
    lo = lax.broadcasted_iota(jnp.int32, blk.shape, 1) < HEAD_DIM
    return jnp.where(lo, blk, sw), jnp.where(lo, sw, blk)


def _fold_halves(blk):
    return blk + pltpu.roll(blk, HEAD_DIM, 1)


def _resident(shape):
    nd = len(shape)
    return pl.BlockSpec(shape, lambda *_: (0,) * nd, pipeline_mode=pl.Buffered(1))


def _const(shape):
    nd = len(shape)
    return pl.BlockSpec(shape, lambda *_: (0,) * nd)


def _rows(tm, cols):
    return pl.BlockSpec((tm, cols), lambda i: (i, 0))


def _params(sem=("arbitrary",)):
    return pltpu.CompilerParams(dimension_semantics=sem, vmem_limit_bytes=VMEM_LIMIT)


def _spatial_mix(ws_ref, bsp_ref, vn, s_scr, n_chunks):
    tri = (lax.broadcasted_iota(jnp.int32, (CHUNK, CHUNK), 0)
           >= lax.broadcasted_iota(jnp.int32, (CHUNK, CHUNK), 1))
    for g in range(A_GROUPS):
        wsg = jnp.where(tri, ws_ref[g], 0.0).astype(BF16)
        cols = slice(g * A_GROUP_DIM, (g + 1) * A_GROUP_DIM)
        for ci in range(n_chunks):
            rows = slice(ci * CHUNK, (ci + 1) * CHUNK)
            s_scr[rows, cols] = _dot(wsg, vn[rows, cols]) + bsp_ref[:, g:g + 1]


def _layer_a_fwd(x, w_in, p_own, lng, lnb, ws, bsp_t, after, own=None):
    t_len = x.shape[0]
    tm = TM_ATTN
    shard_cols = 3 * A_WIDTH // N_CHIPS

    def body(x_ref, win_ref, *rest):
        pown_ref = rest[0] if own is not None else None
        (lng_ref, lnb_ref, ws_ref, bsp_ref, after_ref,
         xt_ref, u_ref, vh_ref, z_ref, rv_ref, y_ref, s_scr) = rest[0 if own is None else 1:]
        x_t = x_ref[...]
        xb = x_t.astype(BF16)
        xt_ref[...] = x_t.T.astype(BF16)
        if own is None:
            u = _dot(xb, win_ref[:, 0:A_WIDTH])
            v = _dot(xb, win_ref[:, A_WIDTH:2 * A_WIDTH])
            z = _dot(xb, win_ref[:, 2 * A_WIDTH:3 * A_WIDTH])
        else:
            p = jnp.concatenate(
                [pown_ref[...] if s == own else _dot(xb, win_ref[:, s * shard_cols:(s + 1) * shard_cols])
                 for s in range(N_CHIPS)], axis=1)
            u, v, z = p[:, 0:A_WIDTH], p[:, A_WIDTH:2 * A_WIDTH], p[:, 2 * A_WIDTH:3 * A_WIDTH]
        vh, rv = _ln_fwd(v)
        vn = (vh * lng_ref[...] + lnb_ref[...]).astype(BF16)
        _spatial_mix(ws_ref, bsp_ref, vn, s_scr, tm // CHUNK)
        sz, _ = _silu_parts(z)
        y_ref[...] = (u * s_scr[...] * sz).astype(BF16)
        u_ref[...] = u.astype(BF16)
        vh_ref[...] = vh.astype(BF16)
        z_ref[...] = z.astype(BF16)
        rv_ref[...] = rv

    wide = jax.ShapeDtypeStruct((t_len, A_WIDTH), BF16)
    product = [] if own is None else [p_own]
    return pl.pallas_call(
        body, name="layer_a_fwd" if own is None else "layer_a_fwd_own%d" % own, grid=(t_len // tm,),
        in_specs=[_rows(tm, D_MODEL), _resident(w_in.shape)] + [_rows(tm, shard_cols)] * len(product)
        + [_const(lng.shape), _const(lnb.shape), _const(ws.shape), _const(bsp_t.shape), _const(after.shape)],
        out_specs=[pl.BlockSpec((D_MODEL, tm), lambda i: (0, i)), _rows(tm, A_WIDTH), _rows(tm, A_WIDTH),
                   _rows(tm, A_WIDTH), _rows(tm, 1), _rows(tm, A_WIDTH)],
        out_shape=[jax.ShapeDtypeStruct((D_MODEL, t_len), BF16), wide, wide, wide,
                   jax.ShapeDtypeStruct((t_len, 1), F32), wide],
        scratch_shapes=[pltpu.VMEM((tm, A_WIDTH), F32)],
        compiler_params=_params(),
    )(x, w_in, *product, lng, lnb, ws, bsp_t, after)


def _layer_b_proj(x, y, w_out_a, g1, b1, w_in, w_kv):
    t_len = x.shape[0]
    tm = 2 * TM_MM

    def body(x_ref, y_ref, wout_ref, g_ref, b_ref, win_ref, wkv_ref, xh_ref, r1_ref, q_ref, z_ref, kd_ref, vd_ref):
        halves = [slice(k * TM_MM, (k + 1) * TM_MM) for k in range(2)]
        projected = [_dot(y_ref[rows, :], wout_ref[...]) for rows in halves]
        for rows, out_a in zip(halves, projected):
            xh, r1 = _ln_fwd(ALPHA * x_ref[rows, :] + out_a)
            xh_ref[rows, :] = xh
            r1_ref[rows, :] = r1
            h1 = (xh * g_ref[...] + b_ref[...]).astype(BF16)
            q_ref[rows, :] = (_dot(h1, win_ref[:, 0:B_WIDTH]) * SCALE).astype(BF16)
            z_ref[rows, :] = _dot(h1, win_ref[:, B_WIDTH:2 * B_WIDTH]).astype(BF16)
            kv = _dot(h1, wkv_ref[...])
            k0, k1 = _dup_halves(kv[:, 0:PAIR])
            v0, v1 = _dup_halves(kv[:, PAIR:2 * PAIR])
            kd_ref[rows, 0:PAIR] = k0.astype(BF16)
            kd_ref[rows, PAIR:2 * PAIR] = k1.astype(BF16)
            vd_ref[rows, 0:PAIR] = v0.astype(BF16)
            vd_ref[rows, PAIR:2 * PAIR] = v1.astype(BF16)

    return pl.pallas_call(
        body, name="layer_b_proj", grid=(t_len // tm,),
        in_specs=[_rows(tm, D_MODEL), _rows(tm, A_WIDTH), _resident(w_out_a.shape), _const(g1.shape),
                  _const(b1.shape), _resident(w_in.shape), _resident(w_kv.shape)],
        out_specs=[_rows(tm, D_MODEL), _rows(tm, 1), _rows(tm, B_WIDTH), _rows(tm, B_WIDTH), _rows(tm, 2 * PAIR),
                   _rows(tm, 2 * PAIR)],
        out_shape=[jax.ShapeDtypeStruct((t_len, D_MODEL), F32), jax.ShapeDtypeStruct((t_len, 1), F32),
                   jax.ShapeDtypeStruct((t_len, B_WIDTH), BF16), jax.ShapeDtypeStruct((t_len, B_WIDTH), BF16),
                   jax.ShapeDtypeStruct((t_len, 2 * PAIR), BF16), jax.ShapeDtypeStruct((t_len, 2 * PAIR), BF16)],
        compiler_params=_params(),
    )(x, y, w_out_a, g1, b1, w_in, w_kv)


GROUP = N_HEADS // N_KV
GROUP_Q = GROUP * CHUNK


def _window_tables():
    j = jnp.arange(2 * CHUNK, dtype=jnp.int32)[:, None]
    t = jnp.arange(CHUNK, dtype=jnp.int32)[None, :]
    dist = t + CHUNK - j
    inside = (dist >= 0) & (dist < CHUNK)
    return jnp.stack([inside & (j >= CHUNK), inside]).astype(F32)


def _band(ref, chunk_index, kvh):
    prev0 = pl.multiple_of(jnp.maximum(chunk_index - 1, 0) * CHUNK, CHUNK)
    cur0 = pl.multiple_of(chunk_index * CHUNK, CHUNK)
    cols = slice(kvh * PAIR, (kvh + 1) * PAIR)
    return jnp.concatenate([ref[pl.ds(prev0, CHUNK), cols], ref[pl.ds(cur0, CHUNK), cols]], axis=0)


def _group_tables(bias_ref, win_ref, sink_ref, chunk_index, kvh):
    bias = jnp.concatenate([bias_ref[kvh * GROUP + j] for j in range(GROUP)], axis=1)
    win = win_ref[jnp.minimum(chunk_index, 1)]
    mask = jnp.concatenate([win] * GROUP, axis=1) > 0.5
    sink = jnp.concatenate([jnp.full((1, CHUNK), sink_ref[0, kvh * GROUP + j], F32) for j in range(GROUP)], axis=1)
    return bias, mask, sink


def _attn_probs(qs, kband, bias, mask, sink):
    logits = jnp.where(mask, _dot_nt(kband, qs) + bias, NEG_INF)
    m = jnp.maximum(jnp.max(logits, axis=0, keepdims=True), sink)
    e = jnp.exp(logits - m)
    es = jnp.exp(sink - m)
    inv = 1.0 / (jnp.sum(e, axis=0, keepdims=True) + es)
    return e * inv, es * inv


def _half_mask():
    return lax.broadcasted_iota(jnp.int32, (CHUNK, PAIR), 1) < HEAD_DIM


def _stack_heads(src_ref, rows, kvh, dst_scr, lo):
    for j in range(GROUP):
        h = kvh * GROUP + j
        blk = src_ref[rows, (h // 2) * PAIR:(h // 2 + 1) * PAIR].astype(F32)
        keep = lo if h % 2 == 0 else ~lo
        dst_scr[j * CHUNK:(j + 1) * CHUNK, :] = jnp.where(keep, blk, 0.0).astype(BF16)


def _probs_spec(tm):
    return pl.BlockSpec((tm // CHUNK, N_KV, 2 * CHUNK, GROUP_Q), lambda i: (i, 0, 0, 0))


def _sink_probs_spec(tiles=1):
    return pl.BlockSpec((tiles, 8, GROUP_Q), lambda i: (i, 0, 0))


def _unstack_pairs(stacked, pp, lo):
    return jnp.where(lo, stacked[(2 * pp) * CHUNK:(2 * pp + 1) * CHUNK], stacked[(2 * pp + 1) * CHUNK:(2 * pp + 2) * CHUNK])


def _layer_b_fwd(q, zb, kd, vd, bias, win, sinks, xh1, g1, b1, w_out, g2, b2, tgt):
    t_len = q.shape[0]
    tm = 2 * TM_ATTN

    def body(q_ref, z_ref, kd_ref, vd_ref, bias_ref, win_ref, sink_ref, xh_ref, g1_ref, b1_ref, wout_ref, g2_ref,
             b2_ref, tgt_ref, o_ref, p_ref, ps_ref, dr_ref, loss_ref, dg_ref, db_ref, o_scr, qs_scr):
        i = pl.program_id(0)

        @pl.when(i == 0)
        def _():
            loss_ref[...] = jnp.zeros_like(loss_ref)
            dg_ref[...] = jnp.zeros_like(dg_ref)
            db_ref[...] = jnp.zeros_like(db_ref)

        lo = _half_mask()
        ps_ref[...] = jnp.zeros_like(ps_ref)
        per_part = TM_ATTN // CHUNK
        for part in range(tm // TM_ATTN):
            part_rows = slice(part * TM_ATTN, (part + 1) * TM_ATTN)
            for cp in range(per_part):
                ci = part * per_part + cp
                cg = i * (tm // CHUNK) + ci
                rows = slice(ci * CHUNK, (ci + 1) * CHUNK)
                for kvh in range(N_KV):
                    kband = _band(kd_ref, cg, kvh)
                    vband = _band(vd_ref, cg, kvh)
                    bias_g, mask, sink = _group_tables(bias_ref, win_ref, sink_ref, cg, kvh)
                    _stack_heads(q_ref, rows, kvh, qs_scr, lo)
                    p, p_sink = _attn_probs(qs_scr[...], kband, bias_g, mask, sink)
                    p = p.astype(BF16)
                    p_ref[ci, kvh] = p
                    ps_ref[part, cp * N_KV + kvh:cp * N_KV + kvh + 1, :] = p_sink
                    o_stack = _dot_tn(p, vband)
                    for pp in range(GROUP // 2):
                        pair = kvh * (GROUP // 2) + pp
                        o_scr[rows, pair * PAIR:(pair + 1) * PAIR] = _unstack_pairs(o_stack, pp, lo)
            o = o_scr[part_rows, :]
            o_ref[part_rows, :] = o.astype(BF16)
            sz, _ = _silu_parts(z_ref[part_rows, :].astype(F32))
            y = (o * sz).astype(BF16)
            h1 = xh_ref[part_rows, :] * g1_ref[...] + b1_ref[...]
            r = ALPHA * h1 + _dot(y, wout_ref[...])
            xh2, rstd2 = _ln_fwd(r)
            diff = xh2 * g2_ref[...] + b2_ref[...] - tgt_ref[part_rows, :]
            loss_ref[...] += jnp.sum(diff * diff, axis=0, keepdims=True)
            dh2 = diff * (1.0 / D_MODEL)
            dg_ref[...] += jnp.sum(dh2 * xh2, axis=0, keepdims=True)
            db_ref[...] += jnp.sum(dh2, axis=0, keepdims=True)
            dr_ref[part_rows, :] = _ln_bwd(dh2 * g2_ref[...], xh2, rstd2)

    vec = jax.ShapeDtypeStruct((1, D_MODEL), F32)
    return pl.pallas_call(
        body, name="layer_b_fwd", grid=(t_len // tm,),
        in_specs=[_rows(tm, B_WIDTH), _rows(tm, B_WIDTH), _resident(kd.shape), _resident(vd.shape),
                  _resident(bias.shape), _resident(win.shape), pl.BlockSpec(memory_space=pltpu.SMEM),
                  _rows(tm, D_MODEL), _const(g1.shape), _const(b1.shape), _resident(w_out.shape), _const(g2.shape),
                  _const(b2.shape), _rows(tm, D_MODEL)],
        out_specs=[_rows(tm, B_WIDTH), _probs_spec(tm), _sink_probs_spec(tm // TM_ATTN), _rows(tm, D_MODEL)]
        + [_const((1, D_MODEL))] * 3,
        out_shape=[jax.ShapeDtypeStruct((t_len, B_WIDTH), BF16),
                   jax.ShapeDtypeStruct((t_len // CHUNK, N_KV, 2 * CHUNK, GROUP_Q), BF16),
                   jax.ShapeDtypeStruct((t_len // TM_ATTN, 8, GROUP_Q), F32),
                   jax.ShapeDtypeStruct((t_len, D_MODEL), F32), vec, vec, vec],
        scratch_shapes=[pltpu.VMEM((tm, B_WIDTH), F32), pltpu.VMEM((GROUP_Q, PAIR), BF16)],
        compiler_params=_params(),
    )(q, zb, kd, vd, bias, win, sinks, xh1, g1, b1, w_out, g2, b2, tgt)


def _layer_b_bwd_attn(dr2, zb, o, q, kd, vd, probs, sink_probs, w_out):
    t_len = q.shape[0]
    tm = TM_BWD_ATTN
    n_steps = t_len // tm
    n_chunks = tm // CHUNK
    per_part = TM_ATTN // CHUNK

    def body(dr_ref, z_ref, o_ref, q_ref, kd_ref, vd_ref, p_ref, ps_ref, wout_ref,
             dq_ref, dz_ref, dkd_ref, dvd_ref, ck_ref, cv_ref, gw_ref, dsink_ref, dbias_ref,
             do_scr, qs_scr, dos_scr, gw_acc):
        i = pl.program_id(0)

        @pl.when(i == 0)
        def _():
            gw_acc[...] = jnp.zeros_like(gw_acc)
            dsink_ref[...] = jnp.zeros_like(dsink_ref)
            dbias_ref[...] = jnp.zeros_like(dbias_ref)

        drb = dr_ref[...].astype(BF16)
        per_kvh = 2
        n_blocks = N_KV * per_kvh
        block_cols = B_WIDTH // n_blocks

        def through_gate(b):
            cols = slice(b * block_cols, (b + 1) * block_cols)
            dy = _dot_nt(drb, wout_ref[cols, :])
            sz, dsz = _silu_parts(z_ref[:, cols].astype(F32))
            o_t = o_ref[:, cols].astype(F32)
            dz_ref[:, cols] = (dy * o_t * dsz).astype(BF16)
            do_scr[:, cols] = (dy * sz).astype(BF16)
            return (o_t * sz).astype(BF16)

        def weight_gradient(b, gated):
            cols = slice(b * block_cols, (b + 1) * block_cols)
            gw_acc[cols, :] += _dot_tn(gated, drb)

        gated = {b: through_gate(b) for b in range(per_kvh)}

        lo = _half_mask()
        for kvh in range(N_KV):
            kcols = slice(kvh * PAIR, (kvh + 1) * PAIR)
            dk_bands, dv_bands = [], []
            for ci in range(n_chunks):
                unit = kvh * n_chunks + ci
                if ci < per_kvh and kvh + 1 < N_KV:
                    gated[(kvh + 1) * per_kvh + ci] = through_gate((kvh + 1) * per_kvh + ci)
                if unit in gated:
                    weight_gradient(unit, gated.pop(unit))
                cg = i * n_chunks + ci
                rows = slice(ci * CHUNK, (ci + 1) * CHUNK)
                kband = _band(kd_ref, cg, kvh)
                vband = _band(vd_ref, cg, kvh)
                _stack_heads(q_ref, rows, kvh, qs_scr, lo)
                _stack_heads(do_scr, rows, kvh, dos_scr, lo)
                qs = qs_scr[...]
                dos = dos_scr[...]
                pb = p_ref[ci, kvh]
                p = pb.astype(F32)
                sink_row = (ci % per_part) * N_KV + kvh
                p_sink = ps_ref[ci // per_part, sink_row:sink_row + 1, :]
                dp = _dot_nt(vband, dos)
                delta = jnp.sum(p * dp, axis=0, keepdims=True)
                dlog = p * (dp - delta)
                for j in range(GROUP):
                    dbias_ref[kvh * GROUP + j] += dlog[:, j * CHUNK:(j + 1) * CHUNK]
                dsink_ref[kvh:kvh + 1, :] += -(p_sink * delta)
                ds = dlog.astype(BF16)
                dq_stack = _dot_tn(ds, kband) * SCALE
                for pp in range(GROUP // 2):
                    pair = kvh * (GROUP // 2) + pp
                    dq_ref[rows, pair * PAIR:(pair + 1) * PAIR] = _unstack_pairs(dq_stack, pp, lo).astype(BF16)
                dk_bands.append(_dot(ds, qs))
                dv_bands.append(_dot(pb, dos))
            for bands, out_ref, carry_ref in ((dk_bands, dkd_ref, ck_ref), (dv_bands, dvd_ref, cv_ref)):
                carry_ref[0, :, kcols] = bands[0][0:CHUNK]
                for ci in range(n_chunks):
                    own = bands[ci][CHUNK:2 * CHUNK]
                    if ci + 1 < n_chunks:
                        own = own + bands[ci + 1][0:CHUNK]
                    out_ref[ci * CHUNK:(ci + 1) * CHUNK, kcols] = own

        @pl.when(i == n_steps - 1)
        def _():
            gw_ref[...] = gw_acc[...].astype(BF16)

    carry_spec = pl.BlockSpec((1, CHUNK, 2 * PAIR), lambda i: (i, 0, 0))
    carry_shape = jax.ShapeDtypeStruct((n_steps, CHUNK, 2 * PAIR), F32)
    bias_shape = (N_HEADS, 2 * CHUNK, CHUNK)
    return pl.pallas_call(
        body, name="layer_b_bwd_attn", grid=(n_steps,),
        in_specs=[_rows(tm, D_MODEL), _rows(tm, B_WIDTH), _rows(tm, B_WIDTH), _rows(tm, B_WIDTH),
                  _resident(kd.shape), _resident(vd.shape), _probs_spec(tm), _sink_probs_spec(tm // TM_ATTN),
                  _resident(w_out.shape)],
        out_specs=[_rows(tm, B_WIDTH), _rows(tm, B_WIDTH), _rows(tm, 2 * PAIR), _rows(tm, 2 * PAIR),
                   carry_spec, carry_spec, _const(w_out.shape), _const((N_KV, GROUP_Q)), _const(bias_shape)],
        out_shape=[jax.ShapeDtypeStruct((t_len, B_WIDTH), BF16), jax.ShapeDtypeStruct((t_len, B_WIDTH), BF16),
                   jax.ShapeDtypeStruct((t_len, 2 * PAIR), F32), jax.ShapeDtypeStruct((t_len, 2 * PAIR), F32),
                   carry_shape, carry_shape, jax.ShapeDtypeStruct(w_out.shape, BF16),
                   jax.ShapeDtypeStruct((N_KV, GROUP_Q), F32), jax.ShapeDtypeStruct(bias_shape, F32)],
        scratch_shapes=[pltpu.VMEM((tm, B_WIDTH), BF16), pltpu.VMEM((GROUP_Q, PAIR), BF16),
                        pltpu.VMEM((GROUP_Q, PAIR), BF16), pltpu.VMEM(w_out.shape, F32)],
        compiler_params=_params(),
    )(dr2, zb, o, q, kd, vd, probs, sink_probs, w_out)


def _layer_b_bwd_proj(xh1, rstd1, g1, b1, dr2, dq, dzb, dkd, dvd, carry_k, carry_v, w_in, w_kv):
    t_len = xh1.shape[0]
    tm = TM_MM
    n_steps = t_len // tm
    per_tile = tm // TM_BWD_ATTN
    n_carry = carry_k.shape[0]

    def body(xh_ref, rstd_ref, g_ref, b_ref, dr2_ref, dq_ref, dz_ref, dkd_ref, dvd_ref, *rest):
        carry_refs = rest[:2 * per_tile]
        win_ref, wkv_ref, dr1_ref, dg_ref, db_ref, gwin_ref, gwkv_ref, acc_in, acc_kv = rest[2 * per_tile:]
        i = pl.program_id(0)

        @pl.when(i == 0)
        def _():
            acc_in[...] = jnp.zeros_like(acc_in)
            acc_kv[...] = jnp.zeros_like(acc_kv)
            dg_ref[...] = jnp.zeros_like(dg_ref)
            db_ref[...] = jnp.zeros_like(db_ref)

        lo = lax.broadcasted_iota(jnp.int32, (tm, PAIR), 1) < HEAD_DIM

        def heads_gradient(tile_ref, refs):
            parts = []
            for a in range(per_tile):
                parts.append(tile_ref[a * TM_BWD_ATTN:(a + 1) * TM_BWD_ATTN - CHUNK, :])
                carry = refs[a][0]
                if a == per_tile - 1:
                    carry = jnp.where(i < n_steps - 1, carry, 0.0)
                parts.append(tile_ref[(a + 1) * TM_BWD_ATTN - CHUNK:(a + 1) * TM_BWD_ATTN, :] + carry)
            dup = jnp.concatenate(parts, axis=0)
            return jnp.where(lo, _fold_halves(dup[:, 0:PAIR]), _fold_halves(dup[:, PAIR:2 * PAIR]))

        xh = xh_ref[...]
        h1 = (xh * g_ref[...] + b_ref[...]).astype(BF16)
        dq_t = dq_ref[...]
        dz_t = dz_ref[...]
        dkv = jnp.concatenate([heads_gradient(dkd_ref, carry_refs[:per_tile]),
                               heads_gradient(dvd_ref, carry_refs[per_tile:])], axis=1).astype(BF16)
        dh1 = ALPHA * dr2_ref[...]
        dh1 += _dot_nt(dq_t, win_ref[:, 0:B_WIDTH])
        dh1 += _dot_nt(dz_t, win_ref[:, B_WIDTH:2 * B_WIDTH])
        dh1 += _dot_nt(dkv, wkv_ref[...])
        acc_in[:, 0:B_WIDTH] += _dot_tn(h1, dq_t)
        acc_in[:, B_WIDTH:2 * B_WIDTH] += _dot_tn(h1, dz_t)
        acc_kv[...] += _dot_tn(h1, dkv)
        dg_ref[...] += jnp.sum(dh1 * xh, axis=0, keepdims=True)
        db_ref[...] += jnp.sum(dh1, axis=0, keepdims=True)
        dr1_ref[...] = _ln_bwd(dh1 * g_ref[...], xh, rstd_ref[...])

        @pl.when(i == n_steps - 1)
        def _():
            half_rows = D_MODEL // 2
            shard_cols = 2 * B_WIDTH // N_CHIPS
            for s in range(N_CHIPS):
                for c in range(2):
                    gwin_ref[2 * s + c] = acc_in[c * half_rows:(c + 1) * half_rows,
                                                 s * shard_cols:(s + 1) * shard_cols].astype(BF16)
            gwkv_ref[...] = acc_kv[...].astype(BF16)

    vec = jax.ShapeDtypeStruct((1, D_MODEL), F32)
    gwin_shape = (N_DEV, D_MODEL // 2, 2 * B_WIDTH // N_CHIPS)

    def carry_spec(a):
        return pl.BlockSpec((1, CHUNK, 2 * PAIR), lambda i: (jnp.minimum(per_tile * i + a + 1, n_carry - 1), 0, 0))

    carry_specs = [carry_spec(a) for a in range(per_tile)]
    return pl.pallas_call(
        body, name="layer_b_bwd_proj", grid=(n_steps,),
        in_specs=[_rows(tm, D_MODEL), _rows(tm, 1), _const(g1.shape), _const(b1.shape), _rows(tm, D_MODEL),
                  _rows(tm, B_WIDTH), _rows(tm, B_WIDTH), _rows(tm, 2 * PAIR), _rows(tm, 2 * PAIR)]
        + carry_specs + carry_specs + [_resident(w_in.shape), _resident(w_kv.shape)],
        out_specs=[_rows(tm, D_MODEL), _const((1, D_MODEL)), _const((1, D_MODEL)), _const(gwin_shape),
                   _const(w_kv.shape)],
        out_shape=[jax.ShapeDtypeStruct((t_len, D_MODEL), F32), vec, vec,
                   jax.ShapeDtypeStruct(gwin_shape, BF16), jax.ShapeDtypeStruct(w_kv.shape, BF16)],
        scratch_shapes=[pltpu.VMEM(w_in.shape, F32), pltpu.VMEM(w_kv.shape, F32)],
        compiler_params=_params(),
    )(xh1, rstd1, g1, b1, dr2, dq, dzb, dkd, dvd, *([carry_k] * per_tile), *([carry_v] * per_tile), w_in, w_kv)


def _layer_a_bwd_mix(dr1, u, vh, z, y, rv, w_out, lng, lnb, ws, bsp_t, after):
    t_len = u.shape[0]
    tm = TM_ATTN
    n_steps = t_len // tm

    def body(dr_ref, u_ref, vh_ref, z_ref, y_ref, rv_ref, wout_ref, lng_ref, lnb_ref, ws_ref, bsp_ref, after_ref,
             dp_ref, gw_ref, dws_ref, dbsp_ref, dgs_ref, dbs_ref, dvn_scr, gw_acc):
        i = pl.program_id(0)

        @pl.when(i == 0)
        def _():
            gw_acc[...] = jnp.zeros_like(gw_acc)
            dws_ref[...] = jnp.zeros_like(dws_ref)
            dbsp_ref[...] = jnp.zeros_like(dbsp_ref)
            dgs_ref[...] = jnp.zeros_like(dgs_ref)
            dbs_ref[...] = jnp.zeros_like(dbs_ref)

        drb = dr_ref[...].astype(BF16)

        def group_cols(g):
            return slice(g * A_GROUP_DIM, (g + 1) * A_GROUP_DIM)

        tri = (lax.broadcasted_iota(jnp.int32, (CHUNK, CHUNK), 0)
               >= lax.broadcasted_iota(jnp.int32, (CHUNK, CHUNK), 1))
        lane = lax.broadcasted_iota(jnp.int32, (CHUNK, CHUNK), 1)
        ones = jnp.ones((CHUNK, A_GROUP_DIM), BF16)
        dbsp = jnp.zeros((CHUNK, CHUNK), F32)
        dy_next = _dot_nt(drb, wout_ref[group_cols(0), :])
        for g in range(A_GROUPS):
            wsg = jnp.where(tri, ws_ref[g], 0.0).astype(BF16)
            cols = group_cols(g)
            cols_z = slice(2 * A_WIDTH + g * A_GROUP_DIM, 2 * A_WIDTH + (g + 1) * A_GROUP_DIM)
            dy_g = dy_next
            if g + 1 < A_GROUPS:
                dy_next = _dot_nt(drb, wout_ref[group_cols(g + 1), :])
            gw_acc[cols, :] += _dot_tn(y_ref[:, cols], drb)
            both = jnp.zeros((CHUNK, 2 * CHUNK), F32)
            for ci in range(tm // CHUNK):
                rows = slice(ci * CHUNK, (ci + 1) * CHUNK)
                vn = (vh_ref[rows, cols].astype(F32) * lng_ref[:, cols] + lnb_ref[:, cols]).astype(BF16)
                s = _dot(wsg, vn) + bsp_ref[:, g:g + 1]
                sz, dsz = _silu_parts(z_ref[rows, cols].astype(F32))
                dy = dy_g[rows]
                t = dy * u_ref[rows, cols].astype(F32)
                dp_ref[rows, cols] = (dy * (s * sz)).astype(BF16)
                dp_ref[rows, cols_z] = (t * s * dsz).astype(BF16)
                ds_b = (t * sz).astype(BF16)
                both += _dot_nt(ds_b, jnp.concatenate([vn, ones], axis=0))
                dvn_scr[rows, cols] = _dot_tn(wsg, ds_b)
            dws_ref[g] += jnp.where(tri, both[:, 0:CHUNK], 0.0)
            dbsp = jnp.where(lane == g, both[:, CHUNK:2 * CHUNK], dbsp)
        dbsp_ref[...] += dbsp
        dvn = dvn_scr[...]
        vh_t = vh_ref[...].astype(F32)
        dgs_ref[...] += jnp.sum(dvn * vh_t, axis=0, keepdims=True)
        dbs_ref[...] += jnp.sum(dvn, axis=0, keepdims=True)
        dp_ref[:, A_WIDTH:2 * A_WIDTH] = _ln_bwd(dvn * lng_ref[...], vh_t, rv_ref[...]).astype(BF16)

        @pl.when(i == n_steps - 1)
        def _():
            gw_ref[...] = gw_acc[...].astype(BF16)

    wide = jax.ShapeDtypeStruct((1, A_WIDTH), F32)
    return pl.pallas_call(
        body, name="layer_a_bwd_mix", grid=(n_steps,),
        in_specs=[_rows(tm, D_MODEL), _rows(tm, A_WIDTH), _rows(tm, A_WIDTH), _rows(tm, A_WIDTH), _rows(tm, A_WIDTH),
                  _rows(tm, 1), _resident(w_out.shape), _const(lng.shape), _const(lnb.shape), _const(ws.shape),
                  _const(bsp_t.shape), _const(after.shape)],
        out_specs=[_rows(tm, 3 * A_WIDTH), _const(w_out.shape), _const(ws.shape), _const((CHUNK, CHUNK)),
                   _const((1, A_WIDTH)), _const((1, A_WIDTH))],
        out_shape=[jax.ShapeDtypeStruct((t_len, 3 * A_WIDTH), BF16), jax.ShapeDtypeStruct(w_out.shape, BF16),
                   jax.ShapeDtypeStruct(ws.shape, F32), jax.ShapeDtypeStruct((CHUNK, CHUNK), F32),
                   wide, wide],
        scratch_shapes=[pltpu.VMEM((tm, A_WIDTH), F32), pltpu.VMEM(w_out.shape, F32)],
        compiler_params=_params(),
    )(dr1, u, vh, z, y, rv, w_out, lng, lnb, ws, bsp_t, after)


def _layer_a_bwd_dx(dr1, dp, w_in, after, updates=()):
    t_len = dr1.shape[0]
    tm = TM_MM
    n_steps = t_len // tm
    n_upd = len(updates)

    def body(dr_ref, dp_ref, win_ref, after_ref, *refs):
        upd_in, dx_ref, upd_out = refs[:4 * n_upd], refs[4 * n_upd], refs[4 * n_upd + 1:]
        dx_ref[...] = ALPHA * dr_ref[...] + _dot_nt(dp_ref[...], win_ref[...])
        for k in range(n_upd):
            w_ref, g_ref, m_ref, v_ref = upd_in[4 * k:4 * k + 4]
            g_out, d_ref, nm_ref, nv_ref = upd_out[4 * k:4 * k + 4]
            g_out[...] = g_ref[...]
            _adamw_update(w_ref, g_ref, m_ref, v_ref, d_ref, nm_ref, nv_ref)

    upd_specs, upd_shapes, upd_args = [], [], []
    for w, g, m, v in updates:
        rows, cols = w.shape
        upd_specs.append(pl.BlockSpec((rows // n_steps, cols), lambda i: (i, 0)))
        upd_shapes.append(jax.ShapeDtypeStruct((rows, cols), F32))
        upd_args += [w, g, m, v]
    return pl.pallas_call(
        body, name="layer_a_bwd_dx", grid=(n_steps,),
        in_specs=[_rows(tm, D_MODEL), _rows(tm, 3 * A_WIDTH), _resident(w_in.shape), _const(after.shape)]
        + [s for s in upd_specs for _ in range(4)],
        out_specs=[_rows(tm, D_MODEL)] + [s for s in upd_specs for _ in range(4)],
        out_shape=[jax.ShapeDtypeStruct((t_len, D_MODEL), F32)] + [s for s in upd_shapes for _ in range(4)],
        compiler_params=_params(),
    )(dr1, dp, w_in, after, *upd_args)


def _layer_a_bwd_win(xt, dp, after):
    t_len = xt.shape[1]
    tm = TM_WIN
    n_steps = t_len // tm
    shard_cols = 3 * A_WIDTH // N_CHIPS
    half_rows = D_MODEL // 2

    def body(xt_ref, dp_ref, after_ref, gw_ref, acc):
        i = pl.program_id(1)

        @pl.when(i == 0)
        def _():
            acc[...] = jnp.zeros_like(acc)

        acc[...] += _dot(xt_ref[...], dp_ref[...])

        @pl.when(i == n_steps - 1)
        def _():
            for c in range(2):
                gw_ref[0, c] = acc[c * half_rows:(c + 1) * half_rows, :].astype(BF16)

    return pl.pallas_call(
        body, name="layer_a_bwd_win", grid=(N_CHIPS, n_steps),
        in_specs=[pl.BlockSpec((D_MODEL, tm), lambda j, i: (0, i)),
                  pl.BlockSpec((tm, shard_cols), lambda j, i: (i, j)), _const(after.shape)],
        out_specs=pl.BlockSpec((1, 2, half_rows, shard_cols), lambda j, i: (j, 0, 0, 0)),
        out_shape=jax.ShapeDtypeStruct((N_CHIPS, 2, half_rows, shard_cols), BF16),
        scratch_shapes=[pltpu.VMEM((D_MODEL, shard_cols), F32)],
        compiler_params=_params(("arbitrary", "arbitrary")),
    )(xt, dp, after)


def _bucket_onehot():
    dist = jnp.arange(CHUNK, dtype=jnp.int32)[None, :]
    max_exact = REL_BUCKETS // 2
    df = jnp.maximum(dist, 1).astype(F32)
    large = max_exact + (jnp.log(df / max_exact) / math.log(CHUNK / max_exact)
                         * (REL_BUCKETS - max_exact)).astype(jnp.int32)
    bucket = jnp.where(dist < max_exact, dist, jnp.minimum(large, REL_BUCKETS - 1))
    onehot = bucket == jnp.arange(REL_BUCKETS, dtype=jnp.int32)[:, None]
    return onehot.astype(F32)


def _bias_expand(rel_t, onehot):
    def body(rel_ref, oh_ref, out_ref):
        by_distance = jnp.dot(rel_ref[...], oh_ref[...], preferred_element_type=F32,
                              precision=lax.Precision.HIGHEST)
        for h in range(N_HEADS):
            rows = jnp.broadcast_to(by_distance[h:h + 1, :], (2 * CHUNK, CHUNK))
            out_ref[h] = pltpu.roll(rows, 0, 1, stride=1, stride_axis=0)

    return pl.pallas_call(
        body, name="bias_expand",
        out_shape=jax.ShapeDtypeStruct((N_HEADS, 2 * CHUNK, CHUNK), F32),
    )(rel_t, onehot)


def _bias_reduce(oh_ref, db_ref):
    sublane = lax.broadcasted_iota(jnp.int32, (_SUBLANES, CHUNK), 0)
    rows = []
    for h in range(N_HEADS):
        part = db_ref[h, 0:_SUBLANES, :]
        for a in range(1, 2 * CHUNK // _SUBLANES):
            tile = db_ref[h, a * _SUBLANES:(a + 1) * _SUBLANES, :]
            back = (-a * _SUBLANES) % CHUNK
            part += pltpu.roll(tile, back, 1) if back else tile
        total = jnp.where(sublane == 0, part, 0.0)
        for s in range(1, _SUBLANES):
            total += jnp.where(sublane == s, pltpu.roll(part, CHUNK - s, 1), 0.0)
        rows.append(jnp.sum(total, axis=0, keepdims=True))
    by_distance = jnp.concatenate(rows, axis=0)
    return lax.dot_general(oh_ref[...], by_distance, (((1,), (1,)), ((), ())),
                           preferred_element_type=F32, precision=lax.Precision.HIGHEST)


_SMALL_SHAPES = dict(w_spatial=(A_GROUPS, CHUNK, CHUNK), b_spatial=(A_GROUPS, CHUNK), attn_sinks=(1, N_HEADS),
                     rel_bias=(REL_BUCKETS, N_HEADS), post_ln_g=(2, D_MODEL), post_ln_b=(2, D_MODEL),
                     sgu_ln_g=(1, A_WIDTH), sgu_ln_b=(1, A_WIDTH), loss=(1, 1))
_SMALL_ORDER = tuple(_SMALL_SHAPES)


def _small_rows(name):
    shape = _SMALL_SHAPES[name]
    rows = math.prod(shape[:-1]) if shape[-1] < _LANES else math.prod(shape) // _LANES
    return -(-rows // _SUBLANES) * _SUBLANES


def _small_offset(name):
    return sum(_small_rows(n) for n in _SMALL_ORDER[:_SMALL_ORDER.index(name)])


def _pack_small(dws, dbsp, dsink, dbias, onehot, post_g, post_b, dgs, dbs, loss_vec):
    def body(dws_ref, dbsp_ref, dsink_ref, db_ref, oh_ref, g1_ref, g2_ref, b1_ref, b2_ref, dgs_ref, dbs_ref,
             loss_ref, out_ref):
        out_ref[...] = jnp.zeros_like(out_ref)

        def put_flat(name, refs):
            row = _small_offset(name)
            for ref in refs:
                for k in range(ref.shape[1] // _LANES):
                    out_ref[row:row + 1, :] = ref[:, k * _LANES:(k + 1) * _LANES]
                    row += 1

        row = _small_offset("w_spatial")
        for g in range(A_GROUPS):
            out_ref[row + g * CHUNK:row + (g + 1) * CHUNK, :] = dws_ref[g]
        row = _small_offset("b_spatial")
        out_ref[row:row + A_GROUPS, :] = dbsp_ref[...].T[0:A_GROUPS, :]
        lane = lax.broadcasted_iota(jnp.int32, (1, _LANES), 1)
        sinks = jnp.zeros((1, _LANES), F32)
        for h in range(N_HEADS):
            per_query = dsink_ref[h // GROUP:h // GROUP + 1, (h % GROUP) * CHUNK:(h % GROUP + 1) * CHUNK]
            sinks = jnp.where(lane == h, jnp.sum(per_query, axis=1, keepdims=True), sinks)
        row = _small_offset("attn_sinks")
        out_ref[row:row + 1, :] = sinks
        row = _small_offset("rel_bias")
        out_ref[row:row + REL_BUCKETS, 0:N_HEADS] = _bias_reduce(oh_ref, db_ref)
        put_flat("post_ln_g", [g1_ref, g2_ref])
        put_flat("post_ln_b", [b1_ref, b2_ref])
        put_flat("sgu_ln_g", [dgs_ref])
        put_flat("sgu_ln_b", [dbs_ref])
        row = _small_offset("loss")
        out_ref[row:row + 1, 0:1] = (0.5 / D_MODEL) * jnp.sum(loss_ref[...], axis=1, keepdims=True)

    total_rows = sum(_small_rows(n) for n in _SMALL_ORDER)
    return pl.pallas_call(
        body, name="pack_small",
        out_shape=jax.ShapeDtypeStruct((total_rows, _LANES), F32),
    )(dws, dbsp, dsink, dbias, onehot, *post_g, *post_b, dgs, dbs, loss_vec)


def _place():
    return lax.axis_index("x"), lax.axis_index("y"), lax.axis_index("c")


RELAY_PIECES = 4


def _shard_window(full_ref, shard_shape, col_sharded, s, half, piece=None):
    rows, cols = shard_shape
    if half is None:
        start, size = 0, rows
    elif piece is None:
        start, size = half * (rows // 2), rows // 2
    else:
        size = rows // 2 // RELAY_PIECES
        start = (half * RELAY_PIECES + piece) * size
    if col_sharded:
        return full_ref.at[pl.ds(start, size), pl.ds(s * cols, cols)]
    return full_ref.at[pl.ds(s * rows + start, size), :]


def _other_chips(x, y):
    return [(1 - x, y), (x, 1 - y), (1 - x, 1 - y)]


def _gather_weights(shards, col_sharded, fetch, ln_shard, tokens=None):
    n_w = len(shards)
    fetched = [w for w in range(n_w) if fetch[w]]
    full_shapes = []
    for w, cs in zip(shards, col_sharded):
        r, c = w.shape
        full_shapes.append((r, c * N_CHIPS) if cs else (r * N_CHIPS, c))

    n_tok = 0 if tokens is None else 1
    tok_tile = TM_MM
    n_tiles = 0 if tokens is None else tokens.shape[0] // tok_tile

    def body(*refs):
        refs = list(refs)
        in_refs = [refs.pop(0) for _ in range(n_w)]
        ln_ref = refs.pop(0)
        tok_ref = refs.pop(0) if n_tok else None
        full_refs = [refs.pop(0) for _ in range(n_w)]
        ln_full = refs.pop(0)
        prod_ref = refs.pop(0) if n_tok else None
        raw = [refs.pop(0) for _ in range(n_w)]
        stage = [refs.pop(0) for _ in range(n_w)]
        send_sems, recv_sems, load_sems, local_sems, ln_send, ln_recv = refs[:6]
        tok_buf, prod_buf, tok_sems, prod_sems = refs[6:] if n_tok else (None,) * 4
        x, y, c = _place()
        s_me = 2 * x + y
        chips = _other_chips(x, y)
        pieces = range(RELAY_PIECES)

        def shard_window(w, s, half, piece=None):
            return _shard_window(full_refs[w], shards[w].shape, col_sharded[w], s, half, piece)

        def piece_rows(w, half, piece):
            rows = shards[w].shape[0] // 2 // RELAY_PIECES
            return pl.ds(pl.multiple_of((half * RELAY_PIECES + piece) * rows, rows), rows)

        def ici_copy(w, k, sender_shard, piece):
            idx = (w * 3 + k) * RELAY_PIECES + piece
            return pltpu.make_async_remote_copy(
                src_ref=stage[w].at[piece_rows(w, c, piece), :], dst_ref=shard_window(w, sender_shard, c, piece),
                send_sem=send_sems.at[idx], recv_sem=recv_sems.at[idx],
                device_id=(*chips[k], c), device_id_type=MESH)

        def d2d_copy(w, k, half, piece):
            s_k = 2 * chips[k][0] + chips[k][1]
            win = shard_window(w, s_k, half, piece)
            idx = (3 * n_w + w * 3 + k) * RELAY_PIECES + piece
            return pltpu.make_async_remote_copy(
                src_ref=win, dst_ref=win, send_sem=send_sems.at[idx], recv_sem=recv_sems.at[idx],
                device_id=(x, y, 1 - c), device_id_type=MESH)

        def ln_copy(k, slot):
            return pltpu.make_async_remote_copy(
                src_ref=ln_ref, dst_ref=ln_full.at[slot], send_sem=ln_send.at[k], recv_sem=ln_recv.at[k],
                device_id=(*chips[k], c), device_id_type=MESH)

        loads = []

        def load(w, rows):
            window = (rows, slice(None)) if rows is not None else (slice(None), slice(None))
            cp = pltpu.make_async_copy(in_refs[w].at[window], raw[w].at[window], load_sems.at[len(loads)])
            cp.start()
            loads.append((cp, w, window))

        for half in (c, 1 - c):
            for w in fetched:
                for q in pieces:
                    load(w, piece_rows(w, half, q))
        for w in range(n_w):
            if not fetch[w]:
                load(w, None)

        def to_bf16(k):
            cp, w, window = loads[k]
            cp.wait()
            stage[w][window] = raw[w][window].astype(BF16)

        ln_full[s_me] = ln_ref[...]
        def shard_of(k):
            return 2 * chips[k][0] + chips[k][1]

        relay_from = jnp.where(c == 0, shard_of(0), shard_of(1))
        relay_to = (jnp.where(c == 0, x, 1 - x), jnp.where(c == 0, 1 - y, y), c)

        def relay_copy(w, sender_shard, piece):
            win = shard_window(w, sender_shard, c, piece)
            idx = (w * 3 + 2) * RELAY_PIECES + piece
            return pltpu.make_async_remote_copy(
                src_ref=win, dst_ref=win, send_sem=send_sems.at[idx], recv_sem=recv_sems.at[idx],
                device_id=relay_to, device_id_type=MESH)

        first = [ln_copy(k, s_me) for k in range(3)]
        for cp in first:
            cp.start()
        n_sent = 0
        for w in fetched:
            for q in pieces:
                to_bf16(n_sent)
                n_sent += 1
                for k in range(2):
                    cp = ici_copy(w, k, s_me, q)
                    cp.start()
                    first.append(cp)
        for k in range(n_sent, len(loads)):
            to_bf16(k)
        own = [pltpu.make_async_copy(stage[w], shard_window(w, s_me, None), local_sems.at[w]) for w in range(n_w)]
        for cp in own:
            cp.start()

        def tok_copy(t):
            return pltpu.make_async_copy(tok_ref.at[pl.ds(t * tok_tile, tok_tile), :], tok_buf.at[t % 2],
                                         tok_sems.at[t % 2])

        def prod_copy(t):
            return pltpu.make_async_copy(prod_buf.at[t % 2], prod_ref.at[pl.ds(t * tok_tile, tok_tile), :],
                                         prod_sems.at[t % 2])

        def product_tiles(tiles):
            for t in tiles:
                if t + 1 < n_tiles:
                    tok_copy(t + 1).start()
                tok_copy(t).wait()
                if t >= 2:
                    prod_copy(t - 2).wait()
                prod_buf[t % 2] = _dot(tok_buf[t % 2].astype(BF16), stage[0][...])
                prod_copy(t).start()

        if n_tiles:
            tok_copy(0).start()
        per_wait = -(-n_tiles // RELAY_PIECES)
        passed = []
        for w in fetched:
            for q in pieces:
                if w == fetched[0]:
                    product_tiles(range(q * per_wait, min((q + 1) * per_wait, n_tiles)))
                for k in range(2):
                    ici_copy(w, k, shard_of(k), q).wait_recv()
                relay = relay_copy(w, relay_from, q)
                relay.start()
                passed.append(relay)
                for k in range(2):
                    fwd = d2d_copy(w, k, c, q)
                    fwd.start()
                    passed.append(fwd)
        for w in fetched:
            for q in pieces:
                relay_copy(w, shard_of(2), q).wait_recv()
                fwd = d2d_copy(w, 2, c, q)
                fwd.start()
                passed.append(fwd)
        for w in fetched:
            for k in range(3):
                for q in pieces:
                    d2d_copy(w, k, 1 - c, q).wait_recv()
        for k in range(3):
            ln_copy(k, 2 * chips[k][0] + chips[k][1]).wait_recv()
        for cp in first + passed:
            cp.wait_send()
        for cp in own:
            cp.wait()
        for t in range(max(n_tiles - 2, 0), n_tiles):
            prod_copy(t).wait()

    vmem = pl.BlockSpec(memory_space=pltpu.VMEM)
    hbm = pl.BlockSpec(memory_space=pl.ANY)
    prod_cols = shards[0].shape[1]
    tok_args = [] if tokens is None else [tokens]
    tok_out = [] if tokens is None else [jax.ShapeDtypeStruct((tokens.shape[0], prod_cols), F32)]
    tok_scratch = [] if tokens is None else [
        pltpu.VMEM((2, tok_tile, tokens.shape[1]), F32), pltpu.VMEM((2, tok_tile, prod_cols), F32),
        pltpu.SemaphoreType.DMA((2,)), pltpu.SemaphoreType.DMA((2,))]
    return pl.pallas_call(
        body, name="gather_weights",
        in_specs=[hbm] * n_w + [vmem] + [hbm] * n_tok,
        out_specs=[hbm] * n_w + [vmem] + [hbm] * n_tok,
        out_shape=[jax.ShapeDtypeStruct(s, BF16) for s in full_shapes]
        + [jax.ShapeDtypeStruct((N_CHIPS,) + ln_shard.shape, F32)] + tok_out,
        scratch_shapes=[pltpu.VMEM(w.shape, F32) for w in shards] + [pltpu.VMEM(w.shape, BF16) for w in shards]
        + [pltpu.SemaphoreType.DMA((6 * RELAY_PIECES * n_w,)), pltpu.SemaphoreType.DMA((6 * RELAY_PIECES * n_w,)),
           pltpu.SemaphoreType.DMA((2 * RELAY_PIECES * len(fetched) + n_w - len(fetched),)),
           pltpu.SemaphoreType.DMA((n_w,)), pltpu.SemaphoreType.DMA((3,)), pltpu.SemaphoreType.DMA((3,))]
        + tok_scratch,
        compiler_params=pltpu.CompilerParams(vmem_limit_bytes=VMEM_LIMIT),
    )(*shards, ln_shard, *tok_args)


def _fetch_copy(full_ref, shard_shape, col_sharded, sender_shard, send_sems, recv_sems, idx, chip, c):
    win = _shard_window(full_ref, shard_shape, col_sharded, sender_shard, None)
    return pltpu.make_async_remote_copy(src_ref=win, dst_ref=win, send_sem=send_sems.at[idx],
                                        recv_sem=recv_sems.at[idx], device_id=(*chip, c), device_id_type=MESH)


def _fetch_start(fulls, shard_shapes, col_sharded):
    n = len(fulls)

    def body(*refs):
        full = refs[:n]
        send_sems, recv_sems = refs[n], refs[n + 1]
        token = refs[-1]
        x, y, c = _place()
        for w in range(n):
            for k, chip in enumerate(_other_chips(x, y)):
                _fetch_copy(full[w], shard_shapes[w], col_sharded[w], 2 * x + y, send_sems, recv_sems, w * 3 + k,
                            chip, c).start()
        token[...] = jnp.zeros_like(token)

    outs = pl.pallas_call(
        body, name="fetch_start",
        out_shape=(pltpu.SemaphoreType.DMA((3 * n,)), pltpu.SemaphoreType.DMA((3 * n,)),
                   *[pltpu.HBM(f.shape, f.dtype) for f in fulls], jax.ShapeDtypeStruct((8, 128), F32)),
        in_specs=[_HBM] * n,
        out_specs=(_SEM, _SEM, *([_HBM] * n), pl.BlockSpec(memory_space=pltpu.VMEM)),
        input_output_aliases={i: 2 + i for i in range(n)},
        compiler_params=pltpu.CompilerParams(has_side_effects=pltpu.SideEffectType.DATAFLOW_SIDE_EFFECTING),
    )(*[pltpu.with_memory_space_constraint(f, pltpu.HBM) for f in fulls])
    return dict(send=outs[0], recv=outs[1], full=list(outs[2:2 + n])), outs[-1]


def _fetch_wait(group, shard_shapes, col_sharded, after):
    n = len(group["full"])

    def body(*refs):
        full = refs[:n]
        send_sems, recv_sems = refs[n], refs[n + 1]
        x, y, c = _place()
        for w in range(n):
            for k, chip in enumerate(_other_chips(x, y)):
                _fetch_copy(full[w], shard_shapes[w], col_sharded[w], 2 * x + y, send_sems, recv_sems, w * 3 + k,
                            chip, c).wait_send()
                _fetch_copy(full[w], shard_shapes[w], col_sharded[w], 2 * chip[0] + chip[1], send_sems, recv_sems,
                            w * 3 + k, chip, c).wait_recv()

    outs = pl.pallas_call(
        body, name="fetch_wait", out_shape=tuple(pltpu.HBM(f.shape, f.dtype) for f in group["full"]),
        in_specs=[_HBM] * n + [_SEM, _SEM, pl.BlockSpec(memory_space=pl.ANY)],
        out_specs=tuple([_HBM] * n), input_output_aliases={i: i for i in range(n)},
        compiler_params=pltpu.CompilerParams(has_side_effects=pltpu.SideEffectType.DATAFLOW_SIDE_EFFECTING),
    )(*group["full"], group["send"], group["recv"], after)
    return list(outs)


_HBM = pl.BlockSpec(memory_space=pltpu.HBM)
_SEM = pl.BlockSpec(memory_space=pltpu.SEMAPHORE)
_N_PEER = N_DEV - 1


def _peer(x, y, c, k):
    return (x + (k >> 2)) % 2, (y + ((k >> 1) & 1)) % 2, (c + (k & 1)) % 2


def _exchange_copy(src_ref, land_ref, sliced, send_sems, recv_sems, idx, x, y, c, k):
    px, py, pc = _peer(x, y, c, k)
    src = src_ref.at[4 * px + 2 * py + pc] if sliced else src_ref
    return pltpu.make_async_remote_copy(
        src_ref=src, dst_ref=land_ref.at[4 * x + 2 * y + c],
        send_sem=send_sems.at[idx], recv_sem=recv_sems.at[idx], device_id=(px, py, pc), device_id_type=MESH)


def _exchange_start(tag, arrays, sliced):
    n = len(arrays)
    lands = [lax.empty(a.shape if s else (N_DEV,) + a.shape, a.dtype) for a, s in zip(arrays, sliced)]

    def body(*refs):
        src, land = refs[:n], refs[n:2 * n]
        send_sems, recv_sems = refs[2 * n], refs[2 * n + 1]
        token = refs[-1]
        x, y, c = _place()
        for w in range(n):
            for k in range(1, N_DEV):
                _exchange_copy(src[w], land[w], sliced[w], send_sems, recv_sems, w * _N_PEER + k - 1, x, y, c, k).start()
        token[...] = jnp.zeros_like(token)

    outs = pl.pallas_call(
        body, name="exchange_start_" + tag,
        out_shape=(pltpu.SemaphoreType.DMA((n * _N_PEER,)), pltpu.SemaphoreType.DMA((n * _N_PEER,)),
                   *[pltpu.HBM(a.shape, a.dtype) for a in arrays], *[pltpu.HBM(l.shape, l.dtype) for l in lands],
                   jax.ShapeDtypeStruct((8, 128), F32)),
        in_specs=[_HBM] * (2 * n),
        out_specs=(_SEM, _SEM, *([_HBM] * (2 * n)), pl.BlockSpec(memory_space=pltpu.VMEM)),
        input_output_aliases={i: 2 + i for i in range(2 * n)},
        compiler_params=pltpu.CompilerParams(has_side_effects=pltpu.SideEffectType.DATAFLOW_SIDE_EFFECTING),
    )(*[pltpu.with_memory_space_constraint(a, pltpu.HBM) for a in arrays],
      *[pltpu.with_memory_space_constraint(l, pltpu.HBM) for l in lands])
    return dict(send=outs[0], recv=outs[1], src=list(outs[2:2 + n]), land=list(outs[2 + n:2 + 2 * n]),
                sliced=list(sliced)), outs[-1]


def _exchange_wait(tag, groups, after):
    counts = [len(g["src"]) for g in groups]
    total = sum(counts)

    def body(*refs):
        pos = 0
        x, y, c = _place()
        for g, n in zip(groups, counts):
            src, land = refs[pos:pos + n], refs[pos + n:pos + 2 * n]
            send_sems, recv_sems = refs[pos + 2 * n], refs[pos + 2 * n + 1]
            pos += 2 * n + 2
            for w in range(n):
                for k in range(1, N_DEV):
                    cp = _exchange_copy(src[w], land[w], g["sliced"][w], send_sems, recv_sems,
                                        w * _N_PEER + k - 1, x, y, c, k)
                    cp.wait_send()
                    cp.wait_recv()

    operands, in_specs, aliases, out_shape = [], [], {}, []
    for g in groups:
        for a in g["src"] + g["land"]:
            aliases[len(operands)] = len(out_shape)
            out_shape.append(pltpu.HBM(a.shape, a.dtype))
            operands.append(a)
            in_specs.append(_HBM)
        operands += [g["send"], g["recv"]]
        in_specs += [_SEM, _SEM]
    operands.append(after)
    in_specs.append(pl.BlockSpec(memory_space=pl.ANY))
    outs = pl.pallas_call(
        body, name="exchange_wait_" + tag, out_shape=tuple(out_shape), in_specs=in_specs,
        out_specs=tuple([_HBM] * (2 * total)), input_output_aliases=aliases,
        compiler_params=pltpu.CompilerParams(has_side_effects=pltpu.SideEffectType.DATAFLOW_SIDE_EFFECTING),
    )(*operands)
    srcs, lands, pos = [], [], 0
    for n in counts:
        srcs += list(outs[pos:pos + n])
        lands += list(outs[pos + n:pos + 2 * n])
        pos += 2 * n
    return srcs, lands


def _sum_and_swap(tag, pieces, lands, small=None, small_land=None):
    n_w = len(pieces)
    n_small = 0 if small is None else 1

    def body(*refs):
        g_refs, land_refs = refs[:n_w], refs[n_w:2 * n_w]
        pos = 2 * n_w + 2 * n_small
        out_refs = refs[pos:pos + n_w]
        pos += n_w + n_small
        bufs = refs[pos:pos + n_w]
        load_sems, swap_send, swap_recv = refs[pos + n_w + 2 * n_small:]
        x, y, c = _place()
        me = 4 * x + 2 * y + c

        def slot(k):
            px, py, pc = _peer(x, y, c, k)
            return 4 * px + 2 * py + pc

        def swap_copy(w, half):
            rows = pieces[w].shape[1]
            win = out_refs[w].at[pl.ds(pl.multiple_of(half * rows, rows), rows), :]
            return pltpu.make_async_remote_copy(
                src_ref=win, dst_ref=win, send_sem=swap_send.at[w], recv_sem=swap_recv.at[w],
                device_id=(x, y, 1 - c), device_id_type=MESH)

        loads = []
        for w in range(n_w):
            per_w = [pltpu.make_async_copy(g_refs[w].at[me], bufs[w].at[me], load_sems.at[w * N_DEV])]
            per_w += [pltpu.make_async_copy(land_refs[w].at[slot(k)], bufs[w].at[slot(k)], load_sems.at[w * N_DEV + k])
                      for k in range(1, N_DEV)]
            loads.append(per_w)
        small_loads = []
        if n_small:
            small_ref, small_land_ref = refs[2 * n_w], refs[2 * n_w + 1]
            small_out = refs[2 * n_w + 2 + n_w]
            small_buf, small_sems = refs[pos + n_w], refs[pos + n_w + 1]
            small_loads = [pltpu.make_async_copy(small_land_ref.at[slot(k)], small_buf.at[slot(k)],
                                                 small_sems.at[k - 1]) for k in range(1, N_DEV)]
        for cp in [cp for per_w in loads for cp in per_w] + small_loads:
            cp.start()
        if n_small:
            small_buf[me] = small_ref[...]
        swaps = []
        for w in range(n_w):
            for cp in loads[w]:
                cp.wait()
            rows = pieces[w].shape[1]
            total = bufs[w][0].astype(F32)
            for p in range(1, N_DEV):
                total += bufs[w][p].astype(F32)
            out_refs[w][pl.ds(pl.multiple_of(c * rows, rows), rows), :] = total
            sw = swap_copy(w, c)
            sw.start()
            swaps.append(sw)
        if n_small:
            for cp in small_loads:
                cp.wait()
            total = small_buf[0]
            for p in range(1, N_DEV):
                total += small_buf[p]
            small_out[...] = total
        for w in range(n_w):
            swap_copy(w, 1 - c).wait_recv()
        for sw in swaps:
            sw.wait_send()

    vmem = pl.BlockSpec(memory_space=pltpu.VMEM)
    hbm = pl.BlockSpec(memory_space=pl.ANY)
    small_args = [small, small_land] if n_small else []
    small_shapes = [jax.ShapeDtypeStruct(small.shape, F32)] if n_small else []
    small_scratch = ([pltpu.VMEM((N_DEV,) + small.shape, F32), pltpu.SemaphoreType.DMA((_N_PEER,))]
                     if n_small else [])
    return pl.pallas_call(
        body, name="sum_and_swap_" + tag,
        in_specs=[hbm] * (2 * n_w) + [vmem, hbm] * n_small,
        out_specs=[vmem] * (n_w + n_small),
        out_shape=[jax.ShapeDtypeStruct((2 * p.shape[1], p.shape[2]), F32) for p in pieces] + small_shapes,
        scratch_shapes=[pltpu.VMEM(p.shape, BF16) for p in pieces] + small_scratch
        + [pltpu.SemaphoreType.DMA((n_w * N_DEV,)), pltpu.SemaphoreType.DMA((n_w,)),
           pltpu.SemaphoreType.DMA((n_w,))],
        compiler_params=pltpu.CompilerParams(vmem_limit_bytes=VMEM_LIMIT),
    )(*pieces, *lands, *small_args)


def _adamw_values(w, g_t, m, v):
    c1 = 1.0 - ADAM_B1 ** ADAM_STEP
    c2 = 1.0 - ADAM_B2 ** ADAM_STEP
    nm = ADAM_B1 * m + (1.0 - ADAM_B1) * g_t
    nv = ADAM_B2 * v + (1.0 - ADAM_B2) * (g_t * g_t)
    return -ADAM_LR * ((nm / c1) / (jnp.sqrt(nv / c2) + ADAM_EPS) + ADAM_WD * w), nm, nv


def _adamw_update(w_ref, g_ref, m_ref, v_ref, d_ref, nm_ref, nv_ref):
    d_ref[...], nm_ref[...], nv_ref[...] = _adamw_values(w_ref[...], g_ref[...], m_ref[...], v_ref[...])


def _adamw_small(packed, shard_index, names, weights, moments_m, moments_v):
    n = len(names)
    shapes = [weights[name].shape for name in names]
    flat = [a[name].reshape(-1, a[name].shape[-1]) for name in names for a in (weights, moments_m, moments_v)]

    def body(packed_ref, shard_ref, *refs):
        loss_row = _small_offset("loss")
        refs[-1][...] = packed_ref[loss_row:loss_row + 1, 0:1]
        for k, name in enumerate(names):
            w_ref, m_ref, v_ref = refs[3 * k:3 * k + 3]
            g_ref, d_ref, nm_ref, nv_ref = refs[3 * n + 4 * k:3 * n + 4 * k + 4]
            rows, cols = w_ref.shape
            first = _small_offset(name)
            if cols <= _LANES:
                blocks = [(slice(0, rows), packed_ref[first:first + rows, 0:cols])]
            else:
                per_row = cols // _LANES
                if cols < _SMALL_SHAPES[name][-1]:
                    first = first + shard_ref[0] * per_row
                blocks = [(slice(i, i + 1),
                           jnp.concatenate([packed_ref[pl.ds(first + i * per_row + j, 1), :] for j in range(per_row)],
                                           axis=1)) for i in range(rows)]
            for at, g_t in blocks:
                g_ref[at, :] = g_t
                d_ref[at, :], nm_ref[at, :], nv_ref[at, :] = _adamw_values(w_ref[at, :], g_t, m_ref[at, :],
                                                                           v_ref[at, :])

    vmem = pl.BlockSpec(memory_space=pltpu.VMEM)
    outs = pl.pallas_call(
        body, name="adamw_small",
        in_specs=[vmem, pl.BlockSpec(memory_space=pltpu.SMEM)] + [vmem] * (3 * n),
        out_shape=[jax.ShapeDtypeStruct(flat[3 * k].shape, F32) for k in range(n) for _ in range(4)]
        + [jax.ShapeDtypeStruct((1, 1), F32)],
    )(packed, shard_index.reshape(1).astype(jnp.int32), *flat)
    return [tuple(o.reshape(shapes[k]) for o in outs[4 * k:4 * k + 4]) for k in range(n)], outs[-1].reshape(())


def _adamw(label, w, g, m, v):
    shape = w.shape
    cols = shape[-1]
    rows = w.size // cols
    args = [a.reshape(rows, cols) for a in (w, g, m, v)]

    def body(w_ref, g_ref, m_ref, v_ref, g_out, d_ref, nm_ref, nv_ref):
        g_out[...] = g_ref[...]
        _adamw_update(w_ref, g_ref, m_ref, v_ref, d_ref, nm_ref, nv_ref)

    block_rows = 256 if rows % 256 == 0 and rows > 256 else rows
    spec = pl.BlockSpec((block_rows, cols), lambda i: (i, 0))
    outs = pl.pallas_call(
        body, name="adamw_" + label, grid=(rows // block_rows,),
        in_specs=[spec] * 4, out_specs=[spec] * 4,
        out_shape=[jax.ShapeDtypeStruct((rows, cols), F32)] * 4,
        compiler_params=_params(),
    )(*args)
    return [o.reshape(shape) for o in outs]


def _no_send(tag, arrays, sliced):
    return jnp.zeros((8, 128), F32)


def _local_step(x, tgt, w_in_a, later_weights, first_after, sgu_ln_g, sgu_ln_b, w_spatial, b_spatial,
                attn_sinks, rel_bias, post_ln_g, post_ln_b, send=_no_send, own_product=None):
    bsp_t = b_spatial.T
    g1, b1 = post_ln_g[0:1], post_ln_b[0:1]
    g2, b2 = post_ln_g[1:2], post_ln_b[1:2]
    onehot = _bucket_onehot()
    bias = _bias_expand(rel_bias.T, onehot)
    win = _window_tables()

    if own_product is None:
        xt, u, vh, z, rv, y = _layer_a_fwd(x, w_in_a, None, sgu_ln_g, sgu_ln_b, w_spatial, bsp_t, first_after)
    else:
        chip, p_own = own_product
        xt, u, vh, z, rv, y = lax.switch(
            chip, [functools.partial(_layer_a_fwd, own=own) for own in range(N_CHIPS)],
            x, w_in_a, p_own, sgu_ln_g, sgu_ln_b, w_spatial, bsp_t, first_after)
    w_out_a, w_kv, w_in_b, w_out_b = later_weights(y)
    xh1, rstd1, q, zb, kd, vd = _layer_b_proj(x, y, w_out_a, g1, b1, w_in_b, w_kv)
    o, probs, sink_probs, dr2, loss_vec, dg2, db2 = _layer_b_fwd(q, zb, kd, vd, bias, win, attn_sinks, xh1, g1, b1,
                                                                 w_out_b, g2, b2, tgt)
    dq, dzb, dkd, dvd, carry_k, carry_v, gw_out_b, dsink, dbias = _layer_b_bwd_attn(
        dr2, zb, o, q, kd, vd, probs, sink_probs, w_out_b)
    dr1, dg1, db1, gw_in_b, gw_kv = _layer_b_bwd_proj(xh1, rstd1, g1, b1, dr2, dq, dzb, dkd, dvd, carry_k, carry_v,
                                                      w_in_b, w_kv)
    gw_out_b = gw_out_b.reshape(N_DEV, -1, D_MODEL)
    gw_kv = gw_kv.reshape(N_DEV, -1, 2 * PAIR)
    after = send("b", [gw_out_b, gw_in_b, gw_kv], [True, True, True])
    dp, gw_out_a, dws, dbsp, dgs, dbs = _layer_a_bwd_mix(dr1, u, vh, z, y, rv, w_out_a, sgu_ln_g, sgu_ln_b,
                                                         w_spatial, bsp_t, after)
    gw_out_a = gw_out_a.reshape(N_DEV, -1, D_MODEL)
    small = _pack_small(dws, dbsp, dsink, dbias, onehot, (dg1, dg2), (db1, db2), dgs, dbs, loss_vec)
    after = send("a_out", [gw_out_a, small], [True, False])
    gw_in_a = _layer_a_bwd_win(xt, dp, after).reshape(N_DEV, D_MODEL // 2, -1)
    after = send("a_in", [gw_in_a], [True])
    after, updates = after if isinstance(after, tuple) else (after, ())
    grad_x, *updated = _layer_a_bwd_dx(dr1, dp, w_in_a, after, updates)

    pieces = [gw_in_a, gw_out_a, gw_kv, gw_in_b, gw_out_b]
    return grad_x, pieces, small, updated


def kernel(x, w_in_a, sgu_ln_g, sgu_ln_b, w_spatial, b_spatial, w_out_a, w_kv, w_in_b, attn_sinks, rel_bias, w_out_b, post_ln_g, post_ln_b, loss_target, m_w_in_a, m_sgu_ln_g, m_sgu_ln_b, m_w_spatial, m_b_spatial, m_w_out_a, m_w_kv, m_w_in_b, m_attn_sinks, m_rel_bias, m_w_out_b, m_post_ln_g, m_post_ln_b, v_w_in_a, v_sgu_ln_g, v_sgu_ln_b, v_w_spatial, v_b_spatial, v_w_out_a, v_w_kv, v_w_in_b, v_attn_sinks, v_rel_bias, v_w_out_b, v_post_ln_g, v_post_ln_b):
    weights = dict(w_in_a=w_in_a, sgu_ln_g=sgu_ln_g, sgu_ln_b=sgu_ln_b, w_spatial=w_spatial, b_spatial=b_spatial,
                   w_out_a=w_out_a, w_kv=w_kv, w_in_b=w_in_b, attn_sinks=attn_sinks, rel_bias=rel_bias,
                   w_out_b=w_out_b, post_ln_g=post_ln_g, post_ln_b=post_ln_b)
    moments_m = dict(w_in_a=m_w_in_a, sgu_ln_g=m_sgu_ln_g, sgu_ln_b=m_sgu_ln_b, w_spatial=m_w_spatial,
                     b_spatial=m_b_spatial, w_out_a=m_w_out_a, w_kv=m_w_kv, w_in_b=m_w_in_b,
                     attn_sinks=m_attn_sinks, rel_bias=m_rel_bias, w_out_b=m_w_out_b, post_ln_g=m_post_ln_g,
                     post_ln_b=m_post_ln_b)
    moments_v = dict(w_in_a=v_w_in_a, sgu_ln_g=v_sgu_ln_g, sgu_ln_b=v_sgu_ln_b, w_spatial=v_w_spatial,
                     b_spatial=v_b_spatial, w_out_a=v_w_out_a, w_kv=v_w_kv, w_in_b=v_w_in_b,
                     attn_sinks=v_attn_sinks, rel_bias=v_rel_bias, w_out_b=v_w_out_b, post_ln_g=v_post_ln_g,
                     post_ln_b=v_post_ln_b)
    order = ("w_in_a", "sgu_ln_g", "sgu_ln_b", "w_spatial", "b_spatial", "w_out_a", "w_kv", "w_in_b", "attn_sinks",
             "rel_bias", "w_out_b", "post_ln_g", "post_ln_b")

    shard_index = 2 * lax.axis_index("x") + lax.axis_index("y")
    ln_shard = jnp.concatenate([sgu_ln_g, sgu_ln_b], axis=0)
    shards = [w_in_a[0], w_out_a[0], w_kv, w_in_b[0], w_out_b[0]]
    col_sharded = [True, False, False, True, False]
    full_in_a, *later, ln_full, p_own = _gather_weights(shards, col_sharded, [True, False, False, False, False],
                                                        ln_shard, tokens=x[0])
    ln_full = jnp.transpose(ln_full, (1, 0, 2)).reshape(2, A_WIDTH)
    later_shapes = [s.shape for s in shards[1:]]
    fetch_group, fetch_token = _fetch_start(later, later_shapes, col_sharded[1:])

    def later_weights(y):
        return _fetch_wait(fetch_group, later_shapes, col_sharded[1:], y)

    groups, grads, deltas, new_m, new_v, scalars = {}, {}, {}, {}, {}, {}
    early = ("w_out_b", "w_in_b", "w_kv", "w_out_a")

    def two_dim(a):
        return a.reshape(-1, a.shape[-1])

    def send(tag, arrays, sliced):
        groups[tag], token = _exchange_start(tag, arrays, sliced)
        if tag != "a_in":
            return token
        srcs, lands = _exchange_wait("early", [groups["b"], groups["a_out"]], token)
        *reduced, packed_sum = _sum_and_swap("early", srcs[:4], lands[:4], srcs[4], lands[4])
        updates = [(two_dim(weights[n]), g, two_dim(moments_m[n]), two_dim(moments_v[n]))
                   for n, g in zip(early, reduced)]
        small_names = ("sgu_ln_g", "sgu_ln_b", "w_spatial", "b_spatial", "attn_sinks", "rel_bias", "post_ln_g",
                       "post_ln_b")
        small_updates, scalars["loss"] = _adamw_small(packed_sum, shard_index, small_names, weights, moments_m,
                                                      moments_v)
        for name, (g, d, nm, nv) in zip(small_names, small_updates):
            grads[name], deltas[name], new_m[name], new_v[name] = g, d, nm, nv
        return new_m["b_spatial"].reshape(A_GROUPS, CHUNK), updates

    grad_x, _, _, updated = _local_step(
        x[0], loss_target[0], full_in_a, later_weights, fetch_token, ln_full[0:1], ln_full[1:2], w_spatial[0],
        b_spatial[0], attn_sinks, rel_bias, post_ln_g, post_ln_b, send=send, own_product=(shard_index, p_own))
    for k, name in enumerate(early):
        grads[name], deltas[name], new_m[name], new_v[name] = [
            a.reshape(weights[name].shape) for a in updated[4 * k:4 * k + 4]]

    srcs, lands = _exchange_wait("late", [groups["a_in"]], grad_x)
    (g_in_a,) = _sum_and_swap("late", srcs, lands)
    grads["w_in_a"], deltas["w_in_a"], new_m["w_in_a"], new_v["w_in_a"] = _adamw(
        "w_in_a", w_in_a, g_in_a.reshape(w_in_a.shape), m_w_in_a, v_w_in_a)
    return (scalars["loss"], grad_x[None], *[grads[n] for n in order], *[deltas[n] for n in order],
            *[new_m[n] for n in order], *[new_v[n] for n in order])
```

```python
import math

import jax
import jax.numpy as jnp
from jax import lax
from jax.experimental import pallas as pl
from jax.experimental.pallas import tpu as pltpu

F32 = jnp.float32
BF16 = jnp.bfloat16

D_MODEL = 1024
A_WIDTH = 2048
A_GROUPS = 8
A_GROUP_DIM = 256
CHUNK = 128
N_HEADS = 16
N_KV = 2
HEAD_DIM = 64
PAIR = 2 * HEAD_DIM
B_WIDTH = 1024
REL_BUCKETS = 32
ALPHA = 4.0 ** 0.25
LN_EPS = 1e-5
NEG_INF = -1e30
SCALE = HEAD_DIM ** -0.5

ADAM_LR = 0.001
ADAM_B1 = 0.9
ADAM_B2 = 0.999
ADAM_EPS = 1e-08
ADAM_WD = 0.01
ADAM_STEP = 10

N_DEV = 8
N_CHIPS = 4
MESH = pl.DeviceIdType.MESH
VMEM_LIMIT = 56 * 1024 * 1024

TM_ATTN = 256
TM_BWD_ATTN = 512
TM_MM = 512
TM_WIN = 1024
_LANES = 128
_SUBLANES = 8


def _dot(a, b):
    return jnp.dot(a, b, preferred_element_type=F32)


def _dot_nt(a, b):
    return lax.dot_general(a, b, (((1,), (1,)), ((), ())), preferred_element_type=F32)


def _dot_tn(a, b):
    return lax.dot_general(a, b, (((0,), (0,)), ((), ())), preferred_element_type=F32)


def _ln_fwd(r):
    mu = jnp.mean(r, axis=-1, keepdims=True)
    rc = r - mu
    var = jnp.mean(rc * rc, axis=-1, keepdims=True)
    rstd = lax.rsqrt(var + LN_EPS)
    return rc * rstd, rstd


def _ln_bwd(dxh, xh, rstd):
    m1 = jnp.mean(dxh, axis=-1, keepdims=True)
    m2 = jnp.mean(dxh * xh, axis=-1, keepdims=True)
    return rstd * (dxh - m1 - xh * m2)


def _silu_parts(z):
    sg = jax.nn.sigmoid(z)
    return z * sg, sg * (1.0 + z * (1.0 - sg))


def _dup_halves(blk):
    sw = pltpu.roll(blk, HEAD_DIM, 1)
    lo = lax.broadcasted_iota(jnp.int32, blk.shape, 1) < HEAD_DIM
    return jnp.where(lo, blk, sw), jnp.where(lo, sw, blk)


def _fold_halves(blk):
    return blk + pltpu.roll(blk, HEAD_DIM, 1)


def _resident(shape):
    nd = len(shape)
    return pl.BlockSpec(shape, lambda *_: (0,) * nd, pipeline_mode=pl.Buffered(1))


def _const(shape):
    nd = len(shape)
    return pl.BlockSpec(shape, lambda *_: (0,) * nd)


def _rows(tm, cols):
    return pl.BlockSpec((tm, cols), lambda i: (i, 0))


def _params(sem=("arbitrary",)):
    return pltpu.CompilerParams(dimension_semantics=sem, vmem_limit_bytes=VMEM_LIMIT)


def _spatial_mix(ws_ref, bsp_ref, vn, s_scr, n_chunks):
    tri = (lax.broadcasted_iota(jnp.int32, (CHUNK, CHUNK), 0)
           >= lax.broadcasted_iota(jnp.int32, (CHUNK, CHUNK), 1))
    for g in range(A_GROUPS):
        wsg = jnp.where(tri, ws_ref[g], 0.0).astype(BF16)
        cols = slice(g * A_GROUP_DIM, (g + 1) * A_GROUP_DIM)
        for ci in range(n_chunks):
            rows = slice(ci * CHUNK, (ci + 1) * CHUNK)
            s_scr[rows, cols] = _dot(wsg, vn[rows, cols]) + bsp_ref[:, g:g + 1]


def _layer_a_fwd(x, w_in, own_product, lng, lnb, ws, bsp_t, after):
    t_len = x.shape[0]
    tm = TM_ATTN
    shard_cols = 3 * A_WIDTH // N_CHIPS

    def body(x_ref, win_ref, *rest):
        if own_product is not None:
            chip_ref, pown_ref = rest[:2]
            rest = rest[2:]
        lng_ref, lnb_ref, ws_ref, bsp_ref, after_ref, xt_ref, u_ref, vh_ref, z_ref, rv_ref, y_ref, s_scr = rest

        def tokens():
            x_t = x_ref[...]
            xt_ref[...] = x_t.T.astype(BF16)
            return x_t.astype(BF16)

        def after_projection(u, v, z):
            vh, rv = _ln_fwd(v)
            vn = (vh * lng_ref[...] + lnb_ref[...]).astype(BF16)
            _spatial_mix(ws_ref, bsp_ref, vn, s_scr, tm // CHUNK)
            sz, _ = _silu_parts(z)
            y_ref[...] = (u * s_scr[...] * sz).astype(BF16)
            u_ref[...] = u.astype(BF16)
            vh_ref[...] = vh.astype(BF16)
            z_ref[...] = z.astype(BF16)
            rv_ref[...] = rv

        if own_product is None:
            xb = tokens()
            after_projection(*[_dot(xb, win_ref[:, k * A_WIDTH:(k + 1) * A_WIDTH]) for k in range(3)])
            return
        for own in range(N_CHIPS):
            @pl.when(chip_ref[0] == own)
            def _(own=own):
                xb = tokens()
                p = jnp.concatenate(
                    [pown_ref[...] if s == own else _dot(xb, win_ref[:, s * shard_cols:(s + 1) * shard_cols])
                     for s in range(N_CHIPS)], axis=1)
                after_projection(p[:, 0:A_WIDTH], p[:, A_WIDTH:2 * A_WIDTH], p[:, 2 * A_WIDTH:3 * A_WIDTH])

    wide = jax.ShapeDtypeStruct((t_len, A_WIDTH), BF16)
    product, product_specs = [], []
    if own_product is not None:
        product = [own_product[0].reshape(1).astype(jnp.int32), own_product[1]]
        product_specs = [pl.BlockSpec(memory_space=pltpu.SMEM), _rows(tm, shard_cols)]
    return pl.pallas_call(
        body, name="layer_a_fwd", grid=(t_len // tm,),
        in_specs=[_rows(tm, D_MODEL), _resident(w_in.shape)] + product_specs
        + [_const(lng.shape), _const(lnb.shape), _const(ws.shape), _const(bsp_t.shape), _const(after.shape)],
        out_specs=[pl.BlockSpec((D_MODEL, tm), lambda i: (0, i)), _rows(tm, A_WIDTH), _rows(tm, A_WIDTH),
                   _rows(tm, A_WIDTH), _rows(tm, 1), _rows(tm, A_WIDTH)],
        out_shape=[jax.ShapeDtypeStruct((D_MODEL, t_len), BF16), wide, wide, wide,
                   jax.ShapeDtypeStruct((t_len, 1), F32), wide],
        scratch_shapes=[pltpu.VMEM((tm, A_WIDTH), F32)],
        compiler_params=_params(),
    )(x, w_in, *product, lng, lnb, ws, bsp_t, after)


def _layer_b_proj(x, y, w_out_a, g1, b1, w_in, w_kv):
    t_len = x.shape[0]
    tm = 2 * TM_MM

    def body(x_ref, y_ref, wout_ref, g_ref, b_ref, win_ref, wkv_ref, xh_ref, r1_ref, q_ref, z_ref, kd_ref, vd_ref):
        halves = [slice(k * TM_MM, (k + 1) * TM_MM) for k in range(2)]
        projected = [_dot(y_ref[rows, :], wout_ref[...]) for rows in halves]
        for rows, out_a in zip(halves, projected):
            xh, r1 = _ln_fwd(ALPHA * x_ref[rows, :] + out_a)
            xh_ref[rows, :] = xh
            r1_ref[rows, :] = r1
            h1 = (xh * g_ref[...] + b_ref[...]).astype(BF16)
            q_ref[rows, :] = (_dot(h1, win_ref[:, 0:B_WIDTH]) * SCALE).astype(BF16)
            z_ref[rows, :] = _dot(h1, win_ref[:, B_WIDTH:2 * B_WIDTH]).astype(BF16)
            kv = _dot(h1, wkv_ref[...])
            k0, k1 = _dup_halves(kv[:, 0:PAIR])
            v0, v1 = _dup_halves(kv[:, PAIR:2 * PAIR])
            kd_ref[rows, 0:PAIR] = k0.astype(BF16)
            kd_ref[rows, PAIR:2 * PAIR] = k1.astype(BF16)
            vd_ref[rows, 0:PAIR] = v0.astype(BF16)
            vd_ref[rows, PAIR:2 * PAIR] = v1.astype(BF16)

    return pl.pallas_call(
        body, name="layer_b_proj", grid=(t_len // tm,),
        in_specs=[_rows(tm, D_MODEL), _rows(tm, A_WIDTH), _resident(w_out_a.shape), _const(g1.shape),
                  _const(b1.shape), _resident(w_in.shape), _resident(w_kv.shape)],
        out_specs=[_rows(tm, D_MODEL), _rows(tm, 1), _rows(tm, B_WIDTH), _rows(tm, B_WIDTH), _rows(tm, 2 * PAIR),
                   _rows(tm, 2 * PAIR)],
        out_shape=[jax.ShapeDtypeStruct((t_len, D_MODEL), F32), jax.ShapeDtypeStruct((t_len, 1), F32),
                   jax.ShapeDtypeStruct((t_len, B_WIDTH), BF16), jax.ShapeDtypeStruct((t_len, B_WIDTH), BF16),
                   jax.ShapeDtypeStruct((t_len, 2 * PAIR), BF16), jax.ShapeDtypeStruct((t_len, 2 * PAIR), BF16)],
        compiler_params=_params(),
    )(x, y, w_out_a, g1, b1, w_in, w_kv)


GROUP = N_HEADS // N_KV
GROUP_Q = GROUP * CHUNK


def _window_tables():
    j = jnp.arange(2 * CHUNK, dtype=jnp.int32)[:, None]
    t = jnp.arange(CHUNK, dtype=jnp.int32)[None, :]
    dist = t + CHUNK - j
    inside = (dist >= 0) & (dist < CHUNK)
    return jnp.stack([inside & (j >= CHUNK), inside]).astype(F32)


def _band(ref, chunk_index, kvh):
    prev0 = pl.multiple_of(jnp.maximum(chunk_index - 1, 0) * CHUNK, CHUNK)
    cur0 = pl.multiple_of(chunk_index * CHUNK, CHUNK)
    cols = slice(kvh * PAIR, (kvh + 1) * PAIR)
    return jnp.concatenate([ref[pl.ds(prev0, CHUNK), cols], ref[pl.ds(cur0, CHUNK), cols]], axis=0)


def _group_tables(bias_ref, win_ref, sink_ref, chunk_index, kvh):
    bias = jnp.concatenate([bias_ref[kvh * GROUP + j] for j in range(GROUP)], axis=1)
    win = win_ref[jnp.minimum(chunk_index, 1)]
    mask = jnp.concatenate([win] * GROUP, axis=1) > 0.5
    sink = jnp.concatenate([jnp.full((1, CHUNK), sink_ref[0, kvh * GROUP + j], F32) for j in range(GROUP)], axis=1)
    return bias, mask, sink


def _attn_probs(qs, kband, bias, mask, sink):
    logits = jnp.where(mask, _dot_nt(kband, qs) + bias, NEG_INF)
    m = jnp.maximum(jnp.max(logits, axis=0, keepdims=True), sink)
    e = jnp.exp(logits - m)
    es = jnp.exp(sink - m)
    inv = 1.0 / (jnp.sum(e, axis=0, keepdims=True) + es)
    return e * inv, es * inv


def _half_mask():
    return lax.broadcasted_iota(jnp.int32, (CHUNK, PAIR), 1) < HEAD_DIM


def _stack_heads(src_ref, rows, kvh, dst_scr, lo):
    for j in range(GROUP):
        h = kvh * GROUP + j
        blk = src_ref[rows, (h // 2) * PAIR:(h // 2 + 1) * PAIR].astype(F32)
        keep = lo if h % 2 == 0 else ~lo
        dst_scr[j * CHUNK:(j + 1) * CHUNK, :] = jnp.where(keep, blk, 0.0).astype(BF16)


def _probs_spec(tm):
    return pl.BlockSpec((tm // CHUNK, N_KV, 2 * CHUNK, GROUP_Q), lambda i: (i, 0, 0, 0))


def _sink_probs_spec(tiles=1):
    return pl.BlockSpec((tiles, 8, GROUP_Q), lambda i: (i, 0, 0))


def _unstack_pairs(stacked, pp, lo):
    return jnp.where(lo, stacked[(2 * pp) * CHUNK:(2 * pp + 1) * CHUNK], stacked[(2 * pp + 1) * CHUNK:(2 * pp + 2) * CHUNK])


def _layer_b_fwd(q, zb, kd, vd, bias, win, sinks, xh1, g1, b1, w_out, g2, b2, tgt):
    t_len = q.shape[0]
    tm = 2 * TM_ATTN

    def body(q_ref, z_ref, kd_ref, vd_ref, bias_ref, win_ref, sink_ref, xh_ref, g1_ref, b1_ref, wout_ref, g2_ref,
             b2_ref, tgt_ref, o_ref, p_ref, ps_ref, dr_ref, loss_ref, dg_ref, db_ref, o_scr, qs_scr):
        i = pl.program_id(0)

        @pl.when(i == 0)
        def _():
            loss_ref[...] = jnp.zeros_like(loss_ref)
            dg_ref[...] = jnp.zeros_like(dg_ref)
            db_ref[...] = jnp.zeros_like(db_ref)

        lo = _half_mask()
        ps_ref[...] = jnp.zeros_like(ps_ref)
        per_part = TM_ATTN // CHUNK
        for part in range(tm // TM_ATTN):
            part_rows = slice(part * TM_ATTN, (part + 1) * TM_ATTN)
            for cp in range(per_part):
                ci = part * per_part + cp
                cg = i * (tm // CHUNK) + ci
                rows = slice(ci * CHUNK, (ci + 1) * CHUNK)
                for kvh in range(N_KV):
                    kband = _band(kd_ref, cg, kvh)
                    vband = _band(vd_ref, cg, kvh)
                    bias_g, mask, sink = _group_tables(bias_ref, win_ref, sink_ref, cg, kvh)
                    _stack_heads(q_ref, rows, kvh, qs_scr, lo)
                    p, p_sink = _attn_probs(qs_scr[...], kband, bias_g, mask, sink)
                    p = p.astype(BF16)
                    p_ref[ci, kvh] = p
                    ps_ref[part, cp * N_KV + kvh:cp * N_KV + kvh + 1, :] = p_sink
                    o_stack = _dot_tn(p, vband)
                    for pp in range(GROUP // 2):
                        pair = kvh * (GROUP // 2) + pp
                        o_scr[rows, pair * PAIR:(pair + 1) * PAIR] = _unstack_pairs(o_stack, pp, lo)
            o = o_scr[part_rows, :]
            o_ref[part_rows, :] = o.astype(BF16)
            sz, _ = _silu_parts(z_ref[part_rows, :].astype(F32))
            y = (o * sz).astype(BF16)
            h1 = xh_ref[part_rows, :] * g1_ref[...] + b1_ref[...]
            r = ALPHA * h1 + _dot(y, wout_ref[...])
            xh2, rstd2 = _ln_fwd(r)
            diff = xh2 * g2_ref[...] + b2_ref[...] - tgt_ref[part_rows, :]
            loss_ref[...] += jnp.sum(diff * diff, axis=0, keepdims=True)
            dh2 = diff * (1.0 / D_MODEL)
            dg_ref[...] += jnp.sum(dh2 * xh2, axis=0, keepdims=True)
            db_ref[...] += jnp.sum(dh2, axis=0, keepdims=True)
            dr_ref[part_rows, :] = _ln_bwd(dh2 * g2_ref[...], xh2, rstd2)

    vec = jax.ShapeDtypeStruct((1, D_MODEL), F32)
    return pl.pallas_call(
        body, name="layer_b_fwd", grid=(t_len // tm,),
        in_specs=[_rows(tm, B_WIDTH), _rows(tm, B_WIDTH), _resident(kd.shape), _resident(vd.shape),
                  _resident(bias.shape), _resident(win.shape), pl.BlockSpec(memory_space=pltpu.SMEM),
                  _rows(tm, D_MODEL), _const(g1.shape), _const(b1.shape), _resident(w_out.shape), _const(g2.shape),
                  _const(b2.shape), _rows(tm, D_MODEL)],
        out_specs=[_rows(tm, B_WIDTH), _probs_spec(tm), _sink_probs_spec(tm // TM_ATTN), _rows(tm, D_MODEL)]
        + [_const((1, D_MODEL))] * 3,
        out_shape=[jax.ShapeDtypeStruct((t_len, B_WIDTH), BF16),
                   jax.ShapeDtypeStruct((t_len // CHUNK, N_KV, 2 * CHUNK, GROUP_Q), BF16),
                   jax.ShapeDtypeStruct((t_len // TM_ATTN, 8, GROUP_Q), F32),
                   jax.ShapeDtypeStruct((t_len, D_MODEL), F32), vec, vec, vec],
        scratch_shapes=[pltpu.VMEM((tm, B_WIDTH), F32), pltpu.VMEM((GROUP_Q, PAIR), BF16)],
        compiler_params=_params(),
    )(q, zb, kd, vd, bias, win, sinks, xh1, g1, b1, w_out, g2, b2, tgt)


def _layer_b_bwd_attn(dr2, zb, o, q, kd, vd, probs, sink_probs, w_out):
    t_len = q.shape[0]
    tm = TM_BWD_ATTN
    n_steps = t_len // tm
    n_chunks = tm // CHUNK
    per_part = TM_ATTN // CHUNK

    def body(dr_ref, z_ref, o_ref, q_ref, kd_ref, vd_ref, p_ref, ps_ref, wout_ref,
             dq_ref, dz_ref, dkd_ref, dvd_ref, ck_ref, cv_ref, gw_ref, dsink_ref, dbias_ref,
             do_scr, qs_scr, dos_scr, gw_acc):
        i = pl.program_id(0)

        @pl.when(i == 0)
        def _():
            gw_acc[...] = jnp.zeros_like(gw_acc)
            dsink_ref[...] = jnp.zeros_like(dsink_ref)
            dbias_ref[...] = jnp.zeros_like(dbias_ref)

        drb = dr_ref[...].astype(BF16)
        per_kvh = 2
        n_blocks = N_KV * per_kvh
        block_cols = B_WIDTH // n_blocks

        def through_gate(b):
            cols = slice(b * block_cols, (b + 1) * block_cols)
            dy = _dot_nt(drb, wout_ref[cols, :])
            sz, dsz = _silu_parts(z_ref[:, cols].astype(F32))
            o_t = o_ref[:, cols].astype(F32)
            dz_ref[:, cols] = (dy * o_t * dsz).astype(BF16)
            do_scr[:, cols] = (dy * sz).astype(BF16)
            return (o_t * sz).astype(BF16)

        def weight_gradient(b, gated):
            cols = slice(b * block_cols, (b + 1) * block_cols)
            gw_acc[cols, :] += _dot_tn(gated, drb)

        gated = {b: through_gate(b) for b in range(per_kvh)}

        lo = _half_mask()
        for kvh in range(N_KV):
            kcols = slice(kvh * PAIR, (kvh + 1) * PAIR)
            dk_bands, dv_bands = [], []
            for ci in range(n_chunks):
                unit = kvh * n_chunks + ci
                if ci < per_kvh and kvh + 1 < N_KV:
                    gated[(kvh + 1) * per_kvh + ci] = through_gate((kvh + 1) * per_kvh + ci)
                if unit in gated:
                    weight_gradient(unit, gated.pop(unit))
                cg = i * n_chunks + ci
                rows = slice(ci * CHUNK, (ci + 1) * CHUNK)
                kband = _band(kd_ref, cg, kvh)
                vband = _band(vd_ref, cg, kvh)
                _stack_heads(q_ref, rows, kvh, qs_scr, lo)
                _stack_heads(do_scr, rows, kvh, dos_scr, lo)
                qs = qs_scr[...]
                dos = dos_scr[...]
                pb = p_ref[ci, kvh]
                p = pb.astype(F32)
                sink_row = (ci % per_part) * N_KV + kvh
                p_sink = ps_ref[ci // per_part, sink_row:sink_row + 1, :]
                dp = _dot_nt(vband, dos)
                delta = jnp.sum(p * dp, axis=0, keepdims=True)
                dlog = p * (dp - delta)
                for j in range(GROUP):
                    dbias_ref[kvh * GROUP + j] += dlog[:, j * CHUNK:(j + 1) * CHUNK]
                dsink_ref[kvh:kvh + 1, :] += -(p_sink * delta)
                ds = dlog.astype(BF16)
                dq_stack = _dot_tn(ds, kband) * SCALE
                for pp in range(GROUP // 2):
                    pair = kvh * (GROUP // 2) + pp
                    dq_ref[rows, pair * PAIR:(pair + 1) * PAIR] = _unstack_pairs(dq_stack, pp, lo).astype(BF16)
                dk_bands.append(_dot(ds, qs))
                dv_bands.append(_dot(pb, dos))
            for bands, out_ref, carry_ref in ((dk_bands, dkd_ref, ck_ref), (dv_bands, dvd_ref, cv_ref)):
                carry_ref[0, :, kcols] = bands[0][0:CHUNK]
                for ci in range(n_chunks):
                    own = bands[ci][CHUNK:2 * CHUNK]
                    if ci + 1 < n_chunks:
                        own = own + bands[ci + 1][0:CHUNK]
                    out_ref[ci * CHUNK:(ci + 1) * CHUNK, kcols] = own

        @pl.when(i == n_steps - 1)
        def _():
            gw_ref[...] = gw_acc[...].astype(BF16)

    carry_spec = pl.BlockSpec((1, CHUNK, 2 * PAIR), lambda i: (i, 0, 0))
    carry_shape = jax.ShapeDtypeStruct((n_steps, CHUNK, 2 * PAIR), F32)
    bias_shape = (N_HEADS, 2 * CHUNK, CHUNK)
    return pl.pallas_call(
        body, name="layer_b_bwd_attn", grid=(n_steps,),
        in_specs=[_rows(tm, D_MODEL), _rows(tm, B_WIDTH), _rows(tm, B_WIDTH), _rows(tm, B_WIDTH),
                  _resident(kd.shape), _resident(vd.shape), _probs_spec(tm), _sink_probs_spec(tm // TM_ATTN),
                  _resident(w_out.shape)],
        out_specs=[_rows(tm, B_WIDTH), _rows(tm, B_WIDTH), _rows(tm, 2 * PAIR), _rows(tm, 2 * PAIR),
                   carry_spec, carry_spec, _const(w_out.shape), _const((N_KV, GROUP_Q)), _const(bias_shape)],
        out_shape=[jax.ShapeDtypeStruct((t_len, B_WIDTH), BF16), jax.ShapeDtypeStruct((t_len, B_WIDTH), BF16),
                   jax.ShapeDtypeStruct((t_len, 2 * PAIR), F32), jax.ShapeDtypeStruct((t_len, 2 * PAIR), F32),
                   carry_shape, carry_shape, jax.ShapeDtypeStruct(w_out.shape, BF16),
                   jax.ShapeDtypeStruct((N_KV, GROUP_Q), F32), jax.ShapeDtypeStruct(bias_shape, F32)],
        scratch_shapes=[pltpu.VMEM((tm, B_WIDTH), BF16), pltpu.VMEM((GROUP_Q, PAIR), BF16),
                        pltpu.VMEM((GROUP_Q, PAIR), BF16), pltpu.VMEM(w_out.shape, F32)],
        compiler_params=_params(),
    )(dr2, zb, o, q, kd, vd, probs, sink_probs, w_out)


def _layer_b_bwd_proj(xh1, rstd1, g1, b1, dr2, dq, dzb, dkd, dvd, carry_k, carry_v, w_in, w_kv):
    t_len = xh1.shape[0]
    tm = TM_MM
    n_steps = t_len // tm
    per_tile = tm // TM_BWD_ATTN
    n_carry = carry_k.shape[0]

    def body(xh_ref, rstd_ref, g_ref, b_ref, dr2_ref, dq_ref, dz_ref, dkd_ref, dvd_ref, *rest):
        carry_refs = rest[:2 * per_tile]
        win_ref, wkv_ref, dr1_ref, dg_ref, db_ref, gwin_ref, gwkv_ref, acc_in, acc_kv = rest[2 * per_tile:]
        i = pl.program_id(0)

        @pl.when(i == 0)
        def _():
            acc_in[...] = jnp.zeros_like(acc_in)
            acc_kv[...] = jnp.zeros_like(acc_kv)
            dg_ref[...] = jnp.zeros_like(dg_ref)
            db_ref[...] = jnp.zeros_like(db_ref)

        lo = lax.broadcasted_iota(jnp.int32, (tm, PAIR), 1) < HEAD_DIM

        def heads_gradient(tile_ref, refs):
            parts = []
            for a in range(per_tile):
                parts.append(tile_ref[a * TM_BWD_ATTN:(a + 1) * TM_BWD_ATTN - CHUNK, :])
                carry = refs[a][0]
                if a == per_tile - 1:
                    carry = jnp.where(i < n_steps - 1, carry, 0.0)
                parts.append(tile_ref[(a + 1) * TM_BWD_ATTN - CHUNK:(a + 1) * TM_BWD_ATTN, :] + carry)
            dup = jnp.concatenate(parts, axis=0)
            return jnp.where(lo, _fold_halves(dup[:, 0:PAIR]), _fold_halves(dup[:, PAIR:2 * PAIR]))

        xh = xh_ref[...]
        h1 = (xh * g_ref[...] + b_ref[...]).astype(BF16)
        dq_t = dq_ref[...]
        dz_t = dz_ref[...]
        dkv = jnp.concatenate([heads_gradient(dkd_ref, carry_refs[:per_tile]),
                               heads_gradient(dvd_ref, carry_refs[per_tile:])], axis=1).astype(BF16)
        dh1 = ALPHA * dr2_ref[...]
        dh1 += _dot_nt(dq_t, win_ref[:, 0:B_WIDTH])
        dh1 += _dot_nt(dz_t, win_ref[:, B_WIDTH:2 * B_WIDTH])
        dh1 += _dot_nt(dkv, wkv_ref[...])
        acc_in[:, 0:B_WIDTH] += _dot_tn(h1, dq_t)
        acc_in[:, B_WIDTH:2 * B_WIDTH] += _dot_tn(h1, dz_t)
        acc_kv[...] += _dot_tn(h1, dkv)
        dg_ref[...] += jnp.sum(dh1 * xh, axis=0, keepdims=True)
        db_ref[...] += jnp.sum(dh1, axis=0, keepdims=True)
        dr1_ref[...] = _ln_bwd(dh1 * g_ref[...], xh, rstd_ref[...])

        @pl.when(i == n_steps - 1)
        def _():
            half_rows = D_MODEL // 2
            shard_cols = 2 * B_WIDTH // N_CHIPS
            for s in range(N_CHIPS):
                for c in range(2):
                    gwin_ref[2 * s + c] = acc_in[c * half_rows:(c + 1) * half_rows,
                                                 s * shard_cols:(s + 1) * shard_cols].astype(BF16)
            gwkv_ref[...] = acc_kv[...].astype(BF16)

    vec = jax.ShapeDtypeStruct((1, D_MODEL), F32)
    gwin_shape = (N_DEV, D_MODEL // 2, 2 * B_WIDTH // N_CHIPS)

    def carry_spec(a):
        return pl.BlockSpec((1, CHUNK, 2 * PAIR), lambda i: (jnp.minimum(per_tile * i + a + 1, n_carry - 1), 0, 0))

    carry_specs = [carry_spec(a) for a in range(per_tile)]
    return pl.pallas_call(
        body, name="layer_b_bwd_proj", grid=(n_steps,),
        in_specs=[_rows(tm, D_MODEL), _rows(tm, 1), _const(g1.shape), _const(b1.shape), _rows(tm, D_MODEL),
                  _rows(tm, B_WIDTH), _rows(tm, B_WIDTH), _rows(tm, 2 * PAIR), _rows(tm, 2 * PAIR)]
        + carry_specs + carry_specs + [_resident(w_in.shape), _resident(w_kv.shape)],
        out_specs=[_rows(tm, D_MODEL), _const((1, D_MODEL)), _const((1, D_MODEL)), _const(gwin_shape),
                   _const(w_kv.shape)],
        out_shape=[jax.ShapeDtypeStruct((t_len, D_MODEL), F32), vec, vec,
                   jax.ShapeDtypeStruct(gwin_shape, BF16), jax.ShapeDtypeStruct(w_kv.shape, BF16)],
        scratch_shapes=[pltpu.VMEM(w_in.shape, F32), pltpu.VMEM(w_kv.shape, F32)],
        compiler_params=_params(),
    )(xh1, rstd1, g1, b1, dr2, dq, dzb, dkd, dvd, *([carry_k] * per_tile), *([carry_v] * per_tile), w_in, w_kv)


def _layer_a_bwd_mix(dr1, u, vh, z, y, rv, w_out, lng, lnb, ws, bsp_t, after):
    t_len = u.shape[0]
    tm = TM_ATTN
    n_steps = t_len // tm

    def body(dr_ref, u_ref, vh_ref, z_ref, y_ref, rv_ref, wout_ref, lng_ref, lnb_ref, ws_ref, bsp_ref, after_ref,
             dp_ref, gw_ref, dws_ref, dbsp_ref, dgs_ref, dbs_ref, dvn_scr, gw_acc):
        i = pl.program_id(0)

        @pl.when(i == 0)
        def _():
            gw_acc[...] = jnp.zeros_like(gw_acc)
            dws_ref[...] = jnp.zeros_like(dws_ref)
            dbsp_ref[...] = jnp.zeros_like(dbsp_ref)
            dgs_ref[...] = jnp.zeros_like(dgs_ref)
            dbs_ref[...] = jnp.zeros_like(dbs_ref)

        drb = dr_ref[...].astype(BF16)

        def group_cols(g):
            return slice(g * A_GROUP_DIM, (g + 1) * A_GROUP_DIM)

        tri = (lax.broadcasted_iota(jnp.int32, (CHUNK, CHUNK), 0)
               >= lax.broadcasted_iota(jnp.int32, (CHUNK, CHUNK), 1))
        lane = lax.broadcasted_iota(jnp.int32, (CHUNK, CHUNK), 1)
        ones = jnp.ones((CHUNK, A_GROUP_DIM), BF16)
        dbsp = jnp.zeros((CHUNK, CHUNK), F32)
        dy_next = _dot_nt(drb, wout_ref[group_cols(0), :])
        for g in range(A_GROUPS):
            wsg = jnp.where(tri, ws_ref[g], 0.0).astype(BF16)
            cols = group_cols(g)
            cols_z = slice(2 * A_WIDTH + g * A_GROUP_DIM, 2 * A_WIDTH + (g + 1) * A_GROUP_DIM)
            dy_g = dy_next
            if g + 1 < A_GROUPS:
                dy_next = _dot_nt(drb, wout_ref[group_cols(g + 1), :])
            gw_acc[cols, :] += _dot_tn(y_ref[:, cols], drb)
            both = jnp.zeros((CHUNK, 2 * CHUNK), F32)
            for ci in range(tm // CHUNK):
                rows = slice(ci * CHUNK, (ci + 1) * CHUNK)
                vn = (vh_ref[rows, cols].astype(F32) * lng_ref[:, cols] + lnb_ref[:, cols]).astype(BF16)
                s = _dot(wsg, vn) + bsp_ref[:, g:g + 1]
                sz, dsz = _silu_parts(z_ref[rows, cols].astype(F32))
                dy = dy_g[rows]
                t = dy * u_ref[rows, cols].astype(F32)
                dp_ref[rows, cols] = (dy * (s * sz)).astype(BF16)
                dp_ref[rows, cols_z] = (t * s * dsz).astype(BF16)
                ds_b = (t * sz).astype(BF16)
                both += _dot_nt(ds_b, jnp.concatenate([vn, ones], axis=0))
                dvn_scr[rows, cols] = _dot_tn(wsg, ds_b)
            dws_ref[g] += jnp.where(tri, both[:, 0:CHUNK], 0.0)
            dbsp = jnp.where(lane == g, both[:, CHUNK:2 * CHUNK], dbsp)
        dbsp_ref[...] += dbsp
        dvn = dvn_scr[...]
        vh_t = vh_ref[...].astype(F32)
        dgs_ref[...] += jnp.sum(dvn * vh_t, axis=0, keepdims=True)
        dbs_ref[...] += jnp.sum(dvn, axis=0, keepdims=True)
        dp_ref[:, A_WIDTH:2 * A_WIDTH] = _ln_bwd(dvn * lng_ref[...], vh_t, rv_ref[...]).astype(BF16)

        @pl.when(i == n_steps - 1)
        def _():
            gw_ref[...] = gw_acc[...].astype(BF16)

    wide = jax.ShapeDtypeStruct((1, A_WIDTH), F32)
    return pl.pallas_call(
        body, name="layer_a_bwd_mix", grid=(n_steps,),
        in_specs=[_rows(tm, D_MODEL), _rows(tm, A_WIDTH), _rows(tm, A_WIDTH), _rows(tm, A_WIDTH), _rows(tm, A_WIDTH),
                  _rows(tm, 1), _resident(w_out.shape), _const(lng.shape), _const(lnb.shape), _const(ws.shape),
                  _const(bsp_t.shape), _const(after.shape)],
        out_specs=[_rows(tm, 3 * A_WIDTH), _const(w_out.shape), _const(ws.shape), _const((CHUNK, CHUNK)),
                   _const((1, A_WIDTH)), _const((1, A_WIDTH))],
        out_shape=[jax.ShapeDtypeStruct((t_len, 3 * A_WIDTH), BF16), jax.ShapeDtypeStruct(w_out.shape, BF16),
                   jax.ShapeDtypeStruct(ws.shape, F32), jax.ShapeDtypeStruct((CHUNK, CHUNK), F32),
                   wide, wide],
        scratch_shapes=[pltpu.VMEM((tm, A_WIDTH), F32), pltpu.VMEM(w_out.shape, F32)],
        compiler_params=_params(),
    )(dr1, u, vh, z, y, rv, w_out, lng, lnb, ws, bsp_t, after)


def _layer_a_bwd_dx(dr1, dp, w_in, after, updates=()):
    t_len = dr1.shape[0]
    tm = TM_MM
    n_steps = t_len // tm
    n_upd = len(updates)

    def body(dr_ref, dp_ref, win_ref, after_ref, *refs):
        upd_in, dx_ref, upd_out = refs[:4 * n_upd], refs[4 * n_upd], refs[4 * n_upd + 1:]
        dx_ref[...] = ALPHA * dr_ref[...] + _dot_nt(dp_ref[...], win_ref[...])
        for k in range(n_upd):
            w_ref, g_ref, m_ref, v_ref = upd_in[4 * k:4 * k + 4]
            g_out, d_ref, nm_ref, nv_ref = upd_out[4 * k:4 * k + 4]
            g_out[...] = g_ref[...]
            _adamw_update(w_ref, g_ref, m_ref, v_ref, d_ref, nm_ref, nv_ref)

    upd_specs, upd_shapes, upd_args = [], [], []
    for w, g, m, v in updates:
        rows, cols = w.shape
        upd_specs.append(pl.BlockSpec((rows // n_steps, cols), lambda i: (i, 0)))
        upd_shapes.append(jax.ShapeDtypeStruct((rows, cols), F32))
        upd_args += [w, g, m, v]
    return pl.pallas_call(
        body, name="layer_a_bwd_dx", grid=(n_steps,),
        in_specs=[_rows(tm, D_MODEL), _rows(tm, 3 * A_WIDTH), _resident(w_in.shape), _const(after.shape)]
        + [s for s in upd_specs for _ in range(4)],
        out_specs=[_rows(tm, D_MODEL)] + [s for s in upd_specs for _ in range(4)],
        out_shape=[jax.ShapeDtypeStruct((t_len, D_MODEL), F32)] + [s for s in upd_shapes for _ in range(4)],
        compiler_params=_params(),
    )(dr1, dp, w_in, after, *upd_args)


def _layer_a_bwd_win(xt, dp, after):
    t_len = xt.shape[1]
    tm = TM_WIN
    n_steps = t_len // tm
    shard_cols = 3 * A_WIDTH // N_CHIPS
    half_rows = D_MODEL // 2

    def body(xt_ref, dp_ref, after_ref, gw_ref, acc):
        i = pl.program_id(1)

        @pl.when(i == 0)
        def _():
            acc[...] = jnp.zeros_like(acc)

        acc[...] += _dot(xt_ref[...], dp_ref[...])

        @pl.when(i == n_steps - 1)
        def _():
            for c in range(2):
                gw_ref[0, c] = acc[c * half_rows:(c + 1) * half_rows, :].astype(BF16)

    return pl.pallas_call(
        body, name="layer_a_bwd_win", grid=(N_CHIPS, n_steps),
        in_specs=[pl.BlockSpec((D_MODEL, tm), lambda j, i: (0, i)),
                  pl.BlockSpec((tm, shard_cols), lambda j, i: (i, j)), _const(after.shape)],
        out_specs=pl.BlockSpec((1, 2, half_rows, shard_cols), lambda j, i: (j, 0, 0, 0)),
        out_shape=jax.ShapeDtypeStruct((N_CHIPS, 2, half_rows, shard_cols), BF16),
        scratch_shapes=[pltpu.VMEM((D_MODEL, shard_cols), F32)],
        compiler_params=_params(("arbitrary", "arbitrary")),
    )(xt, dp, after)


def _bucket_onehot():
    dist = jnp.arange(CHUNK, dtype=jnp.int32)[None, :]
    max_exact = REL_BUCKETS // 2
    df = jnp.maximum(dist, 1).astype(F32)
    large = max_exact + (jnp.log(df / max_exact) / math.log(CHUNK / max_exact)
                         * (REL_BUCKETS - max_exact)).astype(jnp.int32)
    bucket = jnp.where(dist < max_exact, dist, jnp.minimum(large, REL_BUCKETS - 1))
    onehot = bucket == jnp.arange(REL_BUCKETS, dtype=jnp.int32)[:, None]
    return onehot.astype(F32)


def _bias_expand(rel_t, onehot):
    def body(rel_ref, oh_ref, out_ref):
        by_distance = jnp.dot(rel_ref[...], oh_ref[...], preferred_element_type=F32,
                              precision=lax.Precision.HIGHEST)
        for h in range(N_HEADS):
            rows = jnp.broadcast_to(by_distance[h:h + 1, :], (2 * CHUNK, CHUNK))
            out_ref[h] = pltpu.roll(rows, 0, 1, stride=1, stride_axis=0)

    return pl.pallas_call(
        body, name="bias_expand",
        out_shape=jax.ShapeDtypeStruct((N_HEADS, 2 * CHUNK, CHUNK), F32),
    )(rel_t, onehot)


def _bias_reduce(oh_ref, db_ref):
    sublane = lax.broadcasted_iota(jnp.int32, (_SUBLANES, CHUNK), 0)
    rows = []
    for h in range(N_HEADS):
        part = db_ref[h, 0:_SUBLANES, :]
        for a in range(1, 2 * CHUNK // _SUBLANES):
            tile = db_ref[h, a * _SUBLANES:(a + 1) * _SUBLANES, :]
            back = (-a * _SUBLANES) % CHUNK
            part += pltpu.roll(tile, back, 1) if back else tile
        total = jnp.where(sublane == 0, part, 0.0)
        for s in range(1, _SUBLANES):
            total += jnp.where(sublane == s, pltpu.roll(part, CHUNK - s, 1), 0.0)
        rows.append(jnp.sum(total, axis=0, keepdims=True))
    by_distance = jnp.concatenate(rows, axis=0)
    return lax.dot_general(oh_ref[...], by_distance, (((1,), (1,)), ((), ())),
                           preferred_element_type=F32, precision=lax.Precision.HIGHEST)


_SMALL_SHAPES = dict(w_spatial=(A_GROUPS, CHUNK, CHUNK), b_spatial=(A_GROUPS, CHUNK), attn_sinks=(1, N_HEADS),
                     rel_bias=(REL_BUCKETS, N_HEADS), post_ln_g=(2, D_MODEL), post_ln_b=(2, D_MODEL),
                     sgu_ln_g=(1, A_WIDTH), sgu_ln_b=(1, A_WIDTH), loss=(1, 1))
_SMALL_ORDER = tuple(_SMALL_SHAPES)


def _small_rows(name):
    shape = _SMALL_SHAPES[name]
    rows = math.prod(shape[:-1]) if shape[-1] < _LANES else math.prod(shape) // _LANES
    return -(-rows // _SUBLANES) * _SUBLANES


def _small_offset(name):
    return sum(_small_rows(n) for n in _SMALL_ORDER[:_SMALL_ORDER.index(name)])


def _pack_small(dws, dbsp, dsink, dbias, onehot, post_g, post_b, dgs, dbs, loss_vec):
    def body(dws_ref, dbsp_ref, dsink_ref, db_ref, oh_ref, g1_ref, g2_ref, b1_ref, b2_ref, dgs_ref, dbs_ref,
             loss_ref, out_ref):
        out_ref[...] = jnp.zeros_like(out_ref)

        def put_flat(name, refs):
            row = _small_offset(name)
            for ref in refs:
                for k in range(ref.shape[1] // _LANES):
                    out_ref[row:row + 1, :] = ref[:, k * _LANES:(k + 1) * _LANES]
                    row += 1

        row = _small_offset("w_spatial")
        for g in range(A_GROUPS):
            out_ref[row + g * CHUNK:row + (g + 1) * CHUNK, :] = dws_ref[g]
        row = _small_offset("b_spatial")
        out_ref[row:row + A_GROUPS, :] = dbsp_ref[...].T[0:A_GROUPS, :]
        lane = lax.broadcasted_iota(jnp.int32, (1, _LANES), 1)
        sinks = jnp.zeros((1, _LANES), F32)
        for h in range(N_HEADS):
            per_query = dsink_ref[h // GROUP:h // GROUP + 1, (h % GROUP) * CHUNK:(h % GROUP + 1) * CHUNK]
            sinks = jnp.where(lane == h, jnp.sum(per_query, axis=1, keepdims=True), sinks)
        row = _small_offset("attn_sinks")
        out_ref[row:row + 1, :] = sinks
        row = _small_offset("rel_bias")
        out_ref[row:row + REL_BUCKETS, 0:N_HEADS] = _bias_reduce(oh_ref, db_ref)
        put_flat("post_ln_g", [g1_ref, g2_ref])
        put_flat("post_ln_b", [b1_ref, b2_ref])
        put_flat("sgu_ln_g", [dgs_ref])
        put_flat("sgu_ln_b", [dbs_ref])
        row = _small_offset("loss")
        out_ref[row:row + 1, 0:1] = (0.5 / D_MODEL) * jnp.sum(loss_ref[...], axis=1, keepdims=True)

    total_rows = sum(_small_rows(n) for n in _SMALL_ORDER)
    return pl.pallas_call(
        body, name="pack_small",
        out_shape=jax.ShapeDtypeStruct((total_rows, _LANES), F32),
    )(dws, dbsp, dsink, dbias, onehot, *post_g, *post_b, dgs, dbs, loss_vec)


def _place():
    return lax.axis_index("x"), lax.axis_index("y"), lax.axis_index("c")


RELAY_PIECES = 4


def _shard_window(full_ref, shard_shape, col_sharded, s, half, piece=None):
    rows, cols = shard_shape
    if half is None:
        start, size = 0, rows
    elif piece is None:
        start, size = half * (rows // 2), rows // 2
    else:
        size = rows // 2 // RELAY_PIECES
        start = (half * RELAY_PIECES + piece) * size
    if col_sharded:
        return full_ref.at[pl.ds(start, size), pl.ds(s * cols, cols)]
    return full_ref.at[pl.ds(s * rows + start, size), :]


def _other_chips(x, y):
    return [(1 - x, y), (x, 1 - y), (1 - x, 1 - y)]


def _gather_weights(shards, col_sharded, fetch, ln_shard, tokens=None):
    n_w = len(shards)
    fetched = [w for w in range(n_w) if fetch[w]]
    full_shapes = []
    for w, cs in zip(shards, col_sharded):
        r, c = w.shape
        full_shapes.append((r, c * N_CHIPS) if cs else (r * N_CHIPS, c))

    n_tok = 0 if tokens is None else 1
    tok_tile = TM_MM
    n_tiles = 0 if tokens is None else tokens.shape[0] // tok_tile

    def body(*refs):
        refs = list(refs)
        in_refs = [refs.pop(0) for _ in range(n_w)]
        ln_ref = refs.pop(0)
        tok_ref = refs.pop(0) if n_tok else None
        full_refs = [refs.pop(0) for _ in range(n_w)]
        ln_full = refs.pop(0)
        prod_ref = refs.pop(0) if n_tok else None
        raw = [refs.pop(0) for _ in range(n_w)]
        stage = [refs.pop(0) for _ in range(n_w)]
        send_sems, recv_sems, load_sems, local_sems, ln_send, ln_recv = refs[:6]
        tok_buf, prod_buf, tok_sems, prod_sems = refs[6:] if n_tok else (None,) * 4
        x, y, c = _place()
        s_me = 2 * x + y
        chips = _other_chips(x, y)
        pieces = range(RELAY_PIECES)

        def shard_window(w, s, half, piece=None):
            return _shard_window(full_refs[w], shards[w].shape, col_sharded[w], s, half, piece)

        def piece_rows(w, half, piece):
            rows = shards[w].shape[0] // 2 // RELAY_PIECES
            return pl.ds(pl.multiple_of((half * RELAY_PIECES + piece) * rows, rows), rows)

        def ici_copy(w, k, sender_shard, piece):
            idx = (w * 3 + k) * RELAY_PIECES + piece
            return pltpu.make_async_remote_copy(
                src_ref=stage[w].at[piece_rows(w, c, piece), :], dst_ref=shard_window(w, sender_shard, c, piece),
                send_sem=send_sems.at[idx], recv_sem=recv_sems.at[idx],
                device_id=(*chips[k], c), device_id_type=MESH)

        def d2d_copy(w, k, half, piece):
            s_k = 2 * chips[k][0] + chips[k][1]
            win = shard_window(w, s_k, half, piece)
            idx = (3 * n_w + w * 3 + k) * RELAY_PIECES + piece
            return pltpu.make_async_remote_copy(
                src_ref=win, dst_ref=win, send_sem=send_sems.at[idx], recv_sem=recv_sems.at[idx],
                device_id=(x, y, 1 - c), device_id_type=MESH)

        def ln_copy(k, slot):
            return pltpu.make_async_remote_copy(
                src_ref=ln_ref, dst_ref=ln_full.at[slot], send_sem=ln_send.at[k], recv_sem=ln_recv.at[k],
                device_id=(*chips[k], c), device_id_type=MESH)

        loads = []

        def load(w, rows):
            window = (rows, slice(None)) if rows is not None else (slice(None), slice(None))
            cp = pltpu.make_async_copy(in_refs[w].at[window], raw[w].at[window], load_sems.at[len(loads)])
            cp.start()
            loads.append((cp, w, window))

        for half in (c, 1 - c):
            for w in fetched:
                for q in pieces:
                    load(w, piece_rows(w, half, q))
        for w in range(n_w):
            if not fetch[w]:
                load(w, None)

        def to_bf16(k):
            cp, w, window = loads[k]
            cp.wait()
            stage[w][window] = raw[w][window].astype(BF16)

        ln_full[s_me] = ln_ref[...]
        def shard_of(k):
            return 2 * chips[k][0] + chips[k][1]

        relay_from = jnp.where(c == 0, shard_of(0), shard_of(1))
        relay_to = (jnp.where(c == 0, x, 1 - x), jnp.where(c == 0, 1 - y, y), c)

        def relay_copy(w, sender_shard, piece):
            win = shard_window(w, sender_shard, c, piece)
            idx = (w * 3 + 2) * RELAY_PIECES + piece
            return pltpu.make_async_remote_copy(
                src_ref=win, dst_ref=win, send_sem=send_sems.at[idx], recv_sem=recv_sems.at[idx],
                device_id=relay_to, device_id_type=MESH)

        first = [ln_copy(k, s_me) for k in range(3)]
        for cp in first:
            cp.start()
        n_sent = 0
        for w in fetched:
            for q in pieces:
                to_bf16(n_sent)
                n_sent += 1
                for k in range(2):
                    cp = ici_copy(w, k, s_me, q)
                    cp.start()
                    first.append(cp)
        for k in range(n_sent, len(loads)):
            to_bf16(k)
        own = [pltpu.make_async_copy(stage[w], shard_window(w, s_me, None), local_sems.at[w]) for w in range(n_w)]
        for cp in own:
            cp.start()

        def tok_copy(t):
            return pltpu.make_async_copy(tok_ref.at[pl.ds(t * tok_tile, tok_tile), :], tok_buf.at[t % 2],
                                         tok_sems.at[t % 2])

        def prod_copy(t):
            return pltpu.make_async_copy(prod_buf.at[t % 2], prod_ref.at[pl.ds(t * tok_tile, tok_tile), :],
                                         prod_sems.at[t % 2])

        def product_tiles(tiles):
            for t in tiles:
                if t + 1 < n_tiles:
                    tok_copy(t + 1).start()
                tok_copy(t).wait()
                if t >= 2:
                    prod_copy(t - 2).wait()
                prod_buf[t % 2] = _dot(tok_buf[t % 2].astype(BF16), stage[0][...])
                prod_copy(t).start()

        if n_tiles:
            tok_copy(0).start()
        share = [4, 1, 1, 1, 3, 3, 3, 0]
        assert len(share) == 2 * RELAY_PIECES
        bounds = [sum(share[:k]) * n_tiles // sum(share) for k in range(len(share) + 1)]
        passed = []
        for w in fetched:
            for q in pieces:
                if w == fetched[0]:
                    product_tiles(range(bounds[q], bounds[q + 1]))
                for k in range(2):
                    ici_copy(w, k, shard_of(k), q).wait_recv()
                relay = relay_copy(w, relay_from, q)
                relay.start()
                passed.append(relay)
                for k in range(2):
                    fwd = d2d_copy(w, k, c, q)
                    fwd.start()
                    passed.append(fwd)
        for w in fetched:
            for q in pieces:
                if w == fetched[0]:
                    product_tiles(range(bounds[RELAY_PIECES + q], bounds[RELAY_PIECES + q + 1]))
                relay_copy(w, shard_of(2), q).wait_recv()
                fwd = d2d_copy(w, 2, c, q)
                fwd.start()
                passed.append(fwd)
        for w in fetched:
            for k in range(3):
                for q in pieces:
                    d2d_copy(w, k, 1 - c, q).wait_recv()
        for k in range(3):
            ln_copy(k, 2 * chips[k][0] + chips[k][1]).wait_recv()
        for cp in first + passed:
            cp.wait_send()
        for cp in own:
            cp.wait()
        for t in range(max(n_tiles - 2, 0), n_tiles):
            prod_copy(t).wait()

    vmem = pl.BlockSpec(memory_space=pltpu.VMEM)
    hbm = pl.BlockSpec(memory_space=pl.ANY)
    prod_cols = shards[0].shape[1]
    tok_args = [] if tokens is None else [tokens]
    tok_out = [] if tokens is None else [jax.ShapeDtypeStruct((tokens.shape[0], prod_cols), F32)]
    tok_scratch = [] if tokens is None else [
        pltpu.VMEM((2, tok_tile, tokens.shape[1]), F32), pltpu.VMEM((2, tok_tile, prod_cols), F32),
        pltpu.SemaphoreType.DMA((2,)), pltpu.SemaphoreType.DMA((2,))]
    return pl.pallas_call(
        body, name="gather_weights",
        in_specs=[hbm] * n_w + [vmem] + [hbm] * n_tok,
        out_specs=[hbm] * n_w + [vmem] + [hbm] * n_tok,
        out_shape=[jax.ShapeDtypeStruct(s, BF16) for s in full_shapes]
        + [jax.ShapeDtypeStruct((N_CHIPS,) + ln_shard.shape, F32)] + tok_out,
        scratch_shapes=[pltpu.VMEM(w.shape, F32) for w in shards] + [pltpu.VMEM(w.shape, BF16) for w in shards]
        + [pltpu.SemaphoreType.DMA((6 * RELAY_PIECES * n_w,)), pltpu.SemaphoreType.DMA((6 * RELAY_PIECES * n_w,)),
           pltpu.SemaphoreType.DMA((2 * RELAY_PIECES * len(fetched) + n_w - len(fetched),)),
           pltpu.SemaphoreType.DMA((n_w,)), pltpu.SemaphoreType.DMA((3,)), pltpu.SemaphoreType.DMA((3,))]
        + tok_scratch,
        compiler_params=pltpu.CompilerParams(vmem_limit_bytes=VMEM_LIMIT),
    )(*shards, ln_shard, *tok_args)


def _fetch_copy(full_ref, shard_shape, col_sharded, sender_shard, send_sems, recv_sems, idx, chip, c):
    win = _shard_window(full_ref, shard_shape, col_sharded, sender_shard, None)
    return pltpu.make_async_remote_copy(src_ref=win, dst_ref=win, send_sem=send_sems.at[idx],
                                        recv_sem=recv_sems.at[idx], device_id=(*chip, c), device_id_type=MESH)


def _fetch_start(fulls, shard_shapes, col_sharded):
    n = len(fulls)

    def body(*refs):
        full = refs[:n]
        send_sems, recv_sems = refs[n], refs[n + 1]
        token = refs[-1]
        x, y, c = _place()
        for w in range(n):
            for k, chip in enumerate(_other_chips(x, y)):
                _fetch_copy(full[w], shard_shapes[w], col_sharded[w], 2 * x + y, send_sems, recv_sems, w * 3 + k,
                            chip, c).start()
        token[...] = jnp.zeros_like(token)

    outs = pl.pallas_call(
        body, name="fetch_start",
        out_shape=(pltpu.SemaphoreType.DMA((3 * n,)), pltpu.SemaphoreType.DMA((3 * n,)),
                   *[pltpu.HBM(f.shape, f.dtype) for f in fulls], jax.ShapeDtypeStruct((8, 128), F32)),
        in_specs=[_HBM] * n,
        out_specs=(_SEM, _SEM, *([_HBM] * n), pl.BlockSpec(memory_space=pltpu.VMEM)),
        input_output_aliases={i: 2 + i for i in range(n)},
        compiler_params=pltpu.CompilerParams(has_side_effects=pltpu.SideEffectType.DATAFLOW_SIDE_EFFECTING),
    )(*[pltpu.with_memory_space_constraint(f, pltpu.HBM) for f in fulls])
    return dict(send=outs[0], recv=outs[1], full=list(outs[2:2 + n])), outs[-1]


def _fetch_wait(group, shard_shapes, col_sharded, after):
    n = len(group["full"])

    def body(*refs):
        full = refs[:n]
        send_sems, recv_sems = refs[n], refs[n + 1]
        x, y, c = _place()
        for w in range(n):
            for k, chip in enumerate(_other_chips(x, y)):
                _fetch_copy(full[w], shard_shapes[w], col_sharded[w], 2 * x + y, send_sems, recv_sems, w * 3 + k,
                            chip, c).wait_send()
                _fetch_copy(full[w], shard_shapes[w], col_sharded[w], 2 * chip[0] + chip[1], send_sems, recv_sems,
                            w * 3 + k, chip, c).wait_recv()

    outs = pl.pallas_call(
        body, name="fetch_wait", out_shape=tuple(pltpu.HBM(f.shape, f.dtype) for f in group["full"]),
        in_specs=[_HBM] * n + [_SEM, _SEM, pl.BlockSpec(memory_space=pl.ANY)],
        out_specs=tuple([_HBM] * n), input_output_aliases={i: i for i in range(n)},
        compiler_params=pltpu.CompilerParams(has_side_effects=pltpu.SideEffectType.DATAFLOW_SIDE_EFFECTING),
    )(*group["full"], group["send"], group["recv"], after)
    return list(outs)


_HBM = pl.BlockSpec(memory_space=pltpu.HBM)
_SEM = pl.BlockSpec(memory_space=pltpu.SEMAPHORE)
_N_PEER = N_DEV - 1


def _peer(x, y, c, k):
    return (x + (k >> 2)) % 2, (y + ((k >> 1) & 1)) % 2, (c + (k & 1)) % 2


def _exchange_copy(src_ref, land_ref, sliced, send_sems, recv_sems, idx, x, y, c, k):
    px, py, pc = _peer(x, y, c, k)
    src = src_ref.at[4 * px + 2 * py + pc] if sliced else src_ref
    return pltpu.make_async_remote_copy(
        src_ref=src, dst_ref=land_ref.at[4 * x + 2 * y + c],
        send_sem=send_sems.at[idx], recv_sem=recv_sems.at[idx], device_id=(px, py, pc), device_id_type=MESH)


def _exchange_start(tag, arrays, sliced):
    n = len(arrays)
    lands = [lax.empty(a.shape if s else (N_DEV,) + a.shape, a.dtype) for a, s in zip(arrays, sliced)]

    def body(*refs):
        src, land = refs[:n], refs[n:2 * n]
        send_sems, recv_sems = refs[2 * n], refs[2 * n + 1]
        token = refs[-1]
        x, y, c = _place()
        for w in range(n):
            for k in range(1, N_DEV):
                _exchange_copy(src[w], land[w], sliced[w], send_sems, recv_sems, w * _N_PEER + k - 1, x, y, c, k).start()
        token[...] = jnp.zeros_like(token)

    outs = pl.pallas_call(
        body, name="exchange_start_" + tag,
        out_shape=(pltpu.SemaphoreType.DMA((n * _N_PEER,)), pltpu.SemaphoreType.DMA((n * _N_PEER,)),
                   *[pltpu.HBM(a.shape, a.dtype) for a in arrays], *[pltpu.HBM(l.shape, l.dtype) for l in lands],
                   jax.ShapeDtypeStruct((8, 128), F32)),
        in_specs=[_HBM] * (2 * n),
        out_specs=(_SEM, _SEM, *([_HBM] * (2 * n)), pl.BlockSpec(memory_space=pltpu.VMEM)),
        input_output_aliases={i: 2 + i for i in range(2 * n)},
        compiler_params=pltpu.CompilerParams(has_side_effects=pltpu.SideEffectType.DATAFLOW_SIDE_EFFECTING),
    )(*[pltpu.with_memory_space_constraint(a, pltpu.HBM) for a in arrays],
      *[pltpu.with_memory_space_constraint(l, pltpu.HBM) for l in lands])
    return dict(send=outs[0], recv=outs[1], src=list(outs[2:2 + n]), land=list(outs[2 + n:2 + 2 * n]),
                sliced=list(sliced)), outs[-1]


def _exchange_wait(tag, groups, after):
    counts = [len(g["src"]) for g in groups]
    total = sum(counts)

    def body(*refs):
        pos = 0
        x, y, c = _place()
        for g, n in zip(groups, counts):
            src, land = refs[pos:pos + n], refs[pos + n:pos + 2 * n]
            send_sems, recv_sems = refs[pos + 2 * n], refs[pos + 2 * n + 1]
            pos += 2 * n + 2
            for w in range(n):
                for k in range(1, N_DEV):
                    cp = _exchange_copy(src[w], land[w], g["sliced"][w], send_sems, recv_sems,
                                        w * _N_PEER + k - 1, x, y, c, k)
                    cp.wait_send()
                    cp.wait_recv()

    operands, in_specs, aliases, out_shape = [], [], {}, []
    for g in groups:
        for a in g["src"] + g["land"]:
            aliases[len(operands)] = len(out_shape)
            out_shape.append(pltpu.HBM(a.shape, a.dtype))
            operands.append(a)
            in_specs.append(_HBM)
        operands += [g["send"], g["recv"]]
        in_specs += [_SEM, _SEM]
    operands.append(after)
    in_specs.append(pl.BlockSpec(memory_space=pl.ANY))
    outs = pl.pallas_call(
        body, name="exchange_wait_" + tag, out_shape=tuple(out_shape), in_specs=in_specs,
        out_specs=tuple([_HBM] * (2 * total)), input_output_aliases=aliases,
        compiler_params=pltpu.CompilerParams(has_side_effects=pltpu.SideEffectType.DATAFLOW_SIDE_EFFECTING),
    )(*operands)
    srcs, lands, pos = [], [], 0
    for n in counts:
        srcs += list(outs[pos:pos + n])
        lands += list(outs[pos + n:pos + 2 * n])
        pos += 2 * n
    return srcs, lands


def _sum_and_swap(tag, pieces, lands, small=None, small_land=None):
    n_w = len(pieces)
    n_small = 0 if small is None else 1

    def body(*refs):
        g_refs, land_refs = refs[:n_w], refs[n_w:2 * n_w]
        pos = 2 * n_w + 2 * n_small
        out_refs = refs[pos:pos + n_w]
        pos += n_w + n_small
        bufs = refs[pos:pos + n_w]
        load_sems, swap_send, swap_recv = refs[pos + n_w + 2 * n_small:]
        x, y, c = _place()
        me = 4 * x + 2 * y + c

        def slot(k):
            px, py, pc = _peer(x, y, c, k)
            return 4 * px + 2 * py + pc

        def swap_copy(w, half):
            rows = pieces[w].shape[1]
            win = out_refs[w].at[pl.ds(pl.multiple_of(half * rows, rows), rows), :]
            return pltpu.make_async_remote_copy(
                src_ref=win, dst_ref=win, send_sem=swap_send.at[w], recv_sem=swap_recv.at[w],
                device_id=(x, y, 1 - c), device_id_type=MESH)

        loads = []
        for w in range(n_w):
            per_w = [pltpu.make_async_copy(g_refs[w].at[me], bufs[w].at[me], load_sems.at[w * N_DEV])]
            per_w += [pltpu.make_async_copy(land_refs[w].at[slot(k)], bufs[w].at[slot(k)], load_sems.at[w * N_DEV + k])
                      for k in range(1, N_DEV)]
            loads.append(per_w)
        small_loads = []
        if n_small:
            small_ref, small_land_ref = refs[2 * n_w], refs[2 * n_w + 1]
            small_out = refs[2 * n_w + 2 + n_w]
            small_buf, small_sems = refs[pos + n_w], refs[pos + n_w + 1]
            small_loads = [pltpu.make_async_copy(small_land_ref.at[slot(k)], small_buf.at[slot(k)],
                                                 small_sems.at[k - 1]) for k in range(1, N_DEV)]
        for cp in [cp for per_w in loads for cp in per_w] + small_loads:
            cp.start()
        if n_small:
            small_buf[me] = small_ref[...]
        swaps = []
        for w in range(n_w):
            for cp in loads[w]:
                cp.wait()
            rows = pieces[w].shape[1]
            total = bufs[w][0].astype(F32)
            for p in range(1, N_DEV):
                total += bufs[w][p].astype(F32)
            out_refs[w][pl.ds(pl.multiple_of(c * rows, rows), rows), :] = total
            sw = swap_copy(w, c)
            sw.start()
            swaps.append(sw)
        if n_small:
            for cp in small_loads:
                cp.wait()
            total = small_buf[0]
            for p in range(1, N_DEV):
                total += small_buf[p]
            small_out[...] = total
        for w in range(n_w):
            swap_copy(w, 1 - c).wait_recv()
        for sw in swaps:
            sw.wait_send()

    vmem = pl.BlockSpec(memory_space=pltpu.VMEM)
    hbm = pl.BlockSpec(memory_space=pl.ANY)
    small_args = [small, small_land] if n_small else []
    small_shapes = [jax.ShapeDtypeStruct(small.shape, F32)] if n_small else []
    small_scratch = ([pltpu.VMEM((N_DEV,) + small.shape, F32), pltpu.SemaphoreType.DMA((_N_PEER,))]
                     if n_small else [])
    return pl.pallas_call(
        body, name="sum_and_swap_" + tag,
        in_specs=[hbm] * (2 * n_w) + [vmem, hbm] * n_small,
        out_specs=[vmem] * (n_w + n_small),
        out_shape=[jax.ShapeDtypeStruct((2 * p.shape[1], p.shape[2]), F32) for p in pieces] + small_shapes,
        scratch_shapes=[pltpu.VMEM(p.shape, BF16) for p in pieces] + small_scratch
        + [pltpu.SemaphoreType.DMA((n_w * N_DEV,)), pltpu.SemaphoreType.DMA((n_w,)),
           pltpu.SemaphoreType.DMA((n_w,))],
        compiler_params=pltpu.CompilerParams(vmem_limit_bytes=VMEM_LIMIT),
    )(*pieces, *lands, *small_args)


def _adamw_values(w, g_t, m, v):
    c1 = 1.0 - ADAM_B1 ** ADAM_STEP
    c2 = 1.0 - ADAM_B2 ** ADAM_STEP
    nm = ADAM_B1 * m + (1.0 - ADAM_B1) * g_t
    nv = ADAM_B2 * v + (1.0 - ADAM_B2) * (g_t * g_t)
    return -ADAM_LR * ((nm / c1) / (jnp.sqrt(nv / c2) + ADAM_EPS) + ADAM_WD * w), nm, nv


def _adamw_update(w_ref, g_ref, m_ref, v_ref, d_ref, nm_ref, nv_ref):
    d_ref[...], nm_ref[...], nv_ref[...] = _adamw_values(w_ref[...], g_ref[...], m_ref[...], v_ref[...])


def _adamw_small(packed, shard_index, names, weights, moments_m, moments_v):
    n = len(names)
    shapes = [weights[name].shape for name in names]
    flat = [a[name].reshape(-1, a[name].shape[-1]) for name in names for a in (weights, moments_m, moments_v)]

    def body(packed_ref, shard_ref, *refs):
        loss_row = _small_offset("loss")
        refs[-1][...] = packed_ref[loss_row:loss_row + 1, 0:1]
        for k, name in enumerate(names):
            w_ref, m_ref, v_ref = refs[3 * k:3 * k + 3]
            g_ref, d_ref, nm_ref, nv_ref = refs[3 * n + 4 * k:3 * n + 4 * k + 4]
            rows, cols = w_ref.shape
            first = _small_offset(name)
            if cols <= _LANES:
                blocks = [(slice(0, rows), packed_ref[first:first + rows, 0:cols])]
            else:
                per_row = cols // _LANES
                if cols < _SMALL_SHAPES[name][-1]:
                    first = first + shard_ref[0] * per_row
                blocks = [(slice(i, i + 1),
                           jnp.concatenate([packed_ref[pl.ds(first + i * per_row + j, 1), :] for j in range(per_row)],
                                           axis=1)) for i in range(rows)]
            for at, g_t in blocks:
                g_ref[at, :] = g_t
                d_ref[at, :], nm_ref[at, :], nv_ref[at, :] = _adamw_values(w_ref[at, :], g_t, m_ref[at, :],
                                                                           v_ref[at, :])

    vmem = pl.BlockSpec(memory_space=pltpu.VMEM)
    outs = pl.pallas_call(
        body, name="adamw_small",
        in_specs=[vmem, pl.BlockSpec(memory_space=pltpu.SMEM)] + [vmem] * (3 * n),
        out_shape=[jax.ShapeDtypeStruct(flat[3 * k].shape, F32) for k in range(n) for _ in range(4)]
        + [jax.ShapeDtypeStruct((1, 1), F32)],
    )(packed, shard_index.reshape(1).astype(jnp.int32), *flat)
    return [tuple(o.reshape(shapes[k]) for o in outs[4 * k:4 * k + 4]) for k in range(n)], outs[-1].reshape(())


def _adamw(label, w, g, m, v):
    shape = w.shape
    cols = shape[-1]
    rows = w.size // cols
    args = [a.reshape(rows, cols) for a in (w, g, m, v)]

    def body(w_ref, g_ref, m_ref, v_ref, g_out, d_ref, nm_ref, nv_ref):
        g_out[...] = g_ref[...]
        _adamw_update(w_ref, g_ref, m_ref, v_ref, d_ref, nm_ref, nv_ref)

    block_rows = 256 if rows % 256 == 0 and rows > 256 else rows
    spec = pl.BlockSpec((block_rows, cols), lambda i: (i, 0))
    outs = pl.pallas_call(
        body, name="adamw_" + label, grid=(rows // block_rows,),
        in_specs=[spec] * 4, out_specs=[spec] * 4,
        out_shape=[jax.ShapeDtypeStruct((rows, cols), F32)] * 4,
        compiler_params=_params(),
    )(*args)
    return [o.reshape(shape) for o in outs]


def _no_send(tag, arrays, sliced):
    return jnp.zeros((8, 128), F32)


def _local_step(x, tgt, w_in_a, later_weights, first_after, sgu_ln_g, sgu_ln_b, w_spatial, b_spatial,
                attn_sinks, rel_bias, post_ln_g, post_ln_b, send=_no_send, own_product=None):
    bsp_t = b_spatial.T
    g1, b1 = post_ln_g[0:1], post_ln_b[0:1]
    g2, b2 = post_ln_g[1:2], post_ln_b[1:2]
    onehot = _bucket_onehot()
    bias = _bias_expand(rel_bias.T, onehot)
    win = _window_tables()

    xt, u, vh, z, rv, y = _layer_a_fwd(x, w_in_a, own_product, sgu_ln_g, sgu_ln_b, w_spatial, bsp_t, first_after)
    w_out_a, w_kv, w_in_b, w_out_b = later_weights(y)
    xh1, rstd1, q, zb, kd, vd = _layer_b_proj(x, y, w_out_a, g1, b1, w_in_b, w_kv)
    o, probs, sink_probs, dr2, loss_vec, dg2, db2 = _layer_b_fwd(q, zb, kd, vd, bias, win, attn_sinks, xh1, g1, b1,
                                                                 w_out_b, g2, b2, tgt)
    dq, dzb, dkd, dvd, carry_k, carry_v, gw_out_b, dsink, dbias = _layer_b_bwd_attn(
        dr2, zb, o, q, kd, vd, probs, sink_probs, w_out_b)
    dr1, dg1, db1, gw_in_b, gw_kv = _layer_b_bwd_proj(xh1, rstd1, g1, b1, dr2, dq, dzb, dkd, dvd, carry_k, carry_v,
                                                      w_in_b, w_kv)
    gw_out_b = gw_out_b.reshape(N_DEV, -1, D_MODEL)
    gw_kv = gw_kv.reshape(N_DEV, -1, 2 * PAIR)
    after = send("b", [gw_out_b, gw_in_b, gw_kv], [True, True, True])
    dp, gw_out_a, dws, dbsp, dgs, dbs = _layer_a_bwd_mix(dr1, u, vh, z, y, rv, w_out_a, sgu_ln_g, sgu_ln_b,
                                                         w_spatial, bsp_t, after)
    gw_out_a = gw_out_a.reshape(N_DEV, -1, D_MODEL)
    small = _pack_small(dws, dbsp, dsink, dbias, onehot, (dg1, dg2), (db1, db2), dgs, dbs, loss_vec)
    after = send("a_out", [gw_out_a, small], [True, False])
    gw_in_a = _layer_a_bwd_win(xt, dp, after).reshape(N_DEV, D_MODEL // 2, -1)
    after = send("a_in", [gw_in_a], [True])
    after, updates = after if isinstance(after, tuple) else (after, ())
    grad_x, *updated = _layer_a_bwd_dx(dr1, dp, w_in_a, after, updates)

    pieces = [gw_in_a, gw_out_a, gw_kv, gw_in_b, gw_out_b]
    return grad_x, pieces, small, updated


def kernel(x, w_in_a, sgu_ln_g, sgu_ln_b, w_spatial, b_spatial, w_out_a, w_kv, w_in_b, attn_sinks, rel_bias, w_out_b, post_ln_g, post_ln_b, loss_target, m_w_in_a, m_sgu_ln_g, m_sgu_ln_b, m_w_spatial, m_b_spatial, m_w_out_a, m_w_kv, m_w_in_b, m_attn_sinks, m_rel_bias, m_w_out_b, m_post_ln_g, m_post_ln_b, v_w_in_a, v_sgu_ln_g, v_sgu_ln_b, v_w_spatial, v_b_spatial, v_w_out_a, v_w_kv, v_w_in_b, v_attn_sinks, v_rel_bias, v_w_out_b, v_post_ln_g, v_post_ln_b):
    weights = dict(w_in_a=w_in_a, sgu_ln_g=sgu_ln_g, sgu_ln_b=sgu_ln_b, w_spatial=w_spatial, b_spatial=b_spatial,
                   w_out_a=w_out_a, w_kv=w_kv, w_in_b=w_in_b, attn_sinks=attn_sinks, rel_bias=rel_bias,
                   w_out_b=w_out_b, post_ln_g=post_ln_g, post_ln_b=post_ln_b)
    moments_m = dict(w_in_a=m_w_in_a, sgu_ln_g=m_sgu_ln_g, sgu_ln_b=m_sgu_ln_b, w_spatial=m_w_spatial,
                     b_spatial=m_b_spatial, w_out_a=m_w_out_a, w_kv=m_w_kv, w_in_b=m_w_in_b,
                     attn_sinks=m_attn_sinks, rel_bias=m_rel_bias, w_out_b=m_w_out_b, post_ln_g=m_post_ln_g,
                     post_ln_b=m_post_ln_b)
    moments_v = dict(w_in_a=v_w_in_a, sgu_ln_g=v_sgu_ln_g, sgu_ln_b=v_sgu_ln_b, w_spatial=v_w_spatial,
                     b_spatial=v_b_spatial, w_out_a=v_w_out_a, w_kv=v_w_kv, w_in_b=v_w_in_b,
                     attn_sinks=v_attn_sinks, rel_bias=v_rel_bias, w_out_b=v_w_out_b, post_ln_g=v_post_ln_g,
                     post_ln_b=v_post_ln_b)
    order = ("w_in_a", "sgu_ln_g", "sgu_ln_b", "w_spatial", "b_spatial", "w_out_a", "w_kv", "w_in_b", "attn_sinks",
             "rel_bias", "w_out_b", "post_ln_g", "post_ln_b")

    shard_index = 2 * lax.axis_index("x") + lax.axis_index("y")
    ln_shard = jnp.concatenate([sgu_ln_g, sgu_ln_b], axis=0)
    shards = [w_in_a[0], w_out_a[0], w_kv, w_in_b[0], w_out_b[0]]
    col_sharded = [True, False, False, True, False]
    full_in_a, *later, ln_full, p_own = _gather_weights(shards, col_sharded, [True, False, False, False, False],
                                                        ln_shard, tokens=x[0])
    ln_full = jnp.transpose(ln_full, (1, 0, 2)).reshape(2, A_WIDTH)
    later_shapes = [s.shape for s in shards[1:]]
    fetch_group, fetch_token = _fetch_start(later, later_shapes, col_sharded[1:])

    def later_weights(y):
        return _fetch_wait(fetch_group, later_shapes, col_sharded[1:], y)

    groups, grads, deltas, new_m, new_v, scalars = {}, {}, {}, {}, {}, {}
    early = ("w_out_b", "w_in_b", "w_kv", "w_out_a")

    def two_dim(a):
        return a.reshape(-1, a.shape[-1])

    def send(tag, arrays, sliced):
        groups[tag], token = _exchange_start(tag, arrays, sliced)
        if tag != "a_in":
            return token
        srcs, lands = _exchange_wait("early", [groups["b"], groups["a_out"]], token)
        *reduced, packed_sum = _sum_and_swap("early", srcs[:4], lands[:4], srcs[4], lands[4])
        updates = [(two_dim(weights[n]), g, two_dim(moments_m[n]), two_dim(moments_v[n]))
                   for n, g in zip(early, reduced)]
        small_names = ("sgu_ln_g", "sgu_ln_b", "w_spatial", "b_spatial", "attn_sinks", "rel_bias", "post_ln_g",
                       "post_ln_b")
        small_updates, scalars["loss"] = _adamw_small(packed_sum, shard_index, small_names, weights, moments_m,
                                                      moments_v)
        for name, (g, d, nm, nv) in zip(small_names, small_updates):
            grads[name], deltas[name], new_m[name], new_v[name] = g, d, nm, nv
        return new_m["b_spatial"].reshape(A_GROUPS, CHUNK), updates

    grad_x, _, _, updated = _local_step(
        x[0], loss_target[0], full_in_a, later_weights, fetch_token, ln_full[0:1], ln_full[1:2], w_spatial[0],
        b_spatial[0], attn_sinks, rel_bias, post_ln_g, post_ln_b, send=send, own_product=(shard_index, p_own))
    for k, name in enumerate(early):
        grads[name], deltas[name], new_m[name], new_v[name] = [
            a.reshape(weights[name].shape) for a in updated[4 * k:4 * k + 4]]

    srcs, lands = _exchange_wait("late", [groups["a_in"]], grad_x)
    (g_in_a,) = _sum_and_swap("late", srcs, lands)
    grads["w_in_a"], deltas["w_in_a"], new_m["w_in_a"], new_v["w_in_a"] = _adamw(
        "w_in_a", w_in_a, g_in_a.reshape(w_in_a.shape), m_w_in_a, v_w_in_a)
    return (scalars["loss"], grad_x[None], *[grads[n] for n in order], *[deltas[n] for n in order],
            *[new_m[n] for n in order], *[new_v[n] for n in order])
```

```python
import math

import jax
import jax.numpy as jnp
from jax import lax
from jax.experimental import pallas as pl
from jax.experimental.pallas import tpu as pltpu

F32 = jnp.float32
BF16 = jnp.bfloat16

D_MODEL = 1024
A_WIDTH = 2048
A_GROUPS = 8
A_GROUP_DIM = 256
CHUNK = 128
N_HEADS = 16
N_KV = 2
HEAD_DIM = 64
PAIR = 2 * HEAD_DIM
B_WIDTH = 1024
REL_BUCKETS = 32
ALPHA = 4.0 ** 0.25
LN_EPS = 1e-5
NEG_INF = -1e30
SCALE = HEAD_DIM ** -0.5

ADAM_LR = 0.001
ADAM_B1 = 0.9
ADAM_B2 = 0.999
ADAM_EPS = 1e-08
ADAM_WD = 0.01
ADAM_STEP = 10

N_DEV = 8
N_CHIPS = 4
MESH = pl.DeviceIdType.MESH
VMEM_LIMIT = 56 * 1024 * 1024

TM_ATTN = 256
TM_BWD_ATTN = 512
TM_MM = 512
TM_WIN = 1024
_LANES = 128
_SUBLANES = 8


def _dot(a, b):
    return jnp.dot(a, b, preferred_element_type=F32)


def _dot_nt(a, b):
    return lax.dot_general(a, b, (((1,), (1,)), ((), ())), preferred_element_type=F32)


def _dot_tn(a, b):
    return lax.dot_general(a, b, (((0,), (0,)), ((), ())), preferred_element_type=F32)


def _ln_fwd(r):
    mu = jnp.mean(r, axis=-1, keepdims=True)
    rc = r - mu
    var = jnp.mean(rc * rc, axis=-1, keepdims=True)
    rstd = lax.rsqrt(var + LN_EPS)
    return rc * rstd, rstd


def _ln_bwd(dxh, xh, rstd):
    m1 = jnp.mean(dxh, axis=-1, keepdims=True)
    m2 = jnp.mean(dxh * xh, axis=-1, keepdims=True)
    return rstd * (dxh - m1 - xh * m2)


def _silu_parts(z):
    sg = jax.nn.sigmoid(z)
    return z * sg, sg * (1.0 + z * (1.0 - sg))


def _dup_halves(blk):
    sw = pltpu.roll(blk, HEAD_DIM, 1)
    lo = lax.broadcasted_iota(jnp.int32, blk.shape, 1) < HEAD_DIM
    return jnp.where(lo, blk, sw), jnp.where(lo, sw, blk)


def _fold_halves(blk):
    return blk + pltpu.roll(blk, HEAD_DIM, 1)


def _resident(shape):
    nd = len(shape)
    return pl.BlockSpec(shape, lambda *_: (0,) * nd, pipeline_mode=pl.Buffered(1))


def _const(shape):
    nd = len(shape)
    return pl.BlockSpec(shape, lambda *_: (0,) * nd)


def _rows(tm, cols):
    return pl.BlockSpec((tm, cols), lambda i: (i, 0))


def _params(sem=("arbitrary",)):
    return pltpu.CompilerParams(dimension_semantics=sem, vmem_limit_bytes=VMEM_LIMIT)


def _spatial_mix(ws_ref, bsp_ref, vn, s_scr, n_chunks):
    tri = (lax.broadcasted_iota(jnp.int32, (CHUNK, CHUNK), 0)
           >= lax.broadcasted_iota(jnp.int32, (CHUNK, CHUNK), 1))
    for g in range(A_GROUPS):
        wsg = jnp.where(tri, ws_ref[g], 0.0).astype(BF16)
        cols = slice(g * A_GROUP_DIM, (g + 1) * A_GROUP_DIM)
        for ci in range(n_chunks):
            rows = slice(ci * CHUNK, (ci + 1) * CHUNK)
            s_scr[rows, cols] = _dot(wsg, vn[rows, cols]) + bsp_ref[:, g:g + 1]


def _layer_a_fwd(x, w_in, own_product, lng, lnb, ws, bsp_t, after):
    t_len = x.shape[0]
    tm = TM_ATTN
    shard_cols = 3 * A_WIDTH // N_CHIPS

    def body(x_ref, win_ref, *rest):
        if own_product is not None:
            chip_ref, pown_ref = rest[:2]
            rest = rest[2:]
        lng_ref, lnb_ref, ws_ref, bsp_ref, after_ref, xt_ref, u_ref, vh_ref, z_ref, rv_ref, y_ref, s_scr = rest

        def tokens():
            x_t = x_ref[...]
            xt_ref[...] = x_t.T.astype(BF16)
            return x_t.astype(BF16)

        def after_projection(u, v, z):
            vh, rv = _ln_fwd(v)
            vn = (vh * lng_ref[...] + lnb_ref[...]).astype(BF16)
            _spatial_mix(ws_ref, bsp_ref, vn, s_scr, tm // CHUNK)
            sz, _ = _silu_parts(z)
            y_ref[...] = (u * s_scr[...] * sz).astype(BF16)
            u_ref[...] = u.astype(BF16)
            vh_ref[...] = vh.astype(BF16)
            z_ref[...] = z.astype(BF16)
            rv_ref[...] = rv

        if own_product is None:
            xb = tokens()
            after_projection(*[_dot(xb, win_ref[:, k * A_WIDTH:(k + 1) * A_WIDTH]) for k in range(3)])
            return
        for own in range(N_CHIPS):
            @pl.when(chip_ref[0] == own)
            def _(own=own):
                xb = tokens()
                p = jnp.concatenate(
                    [pown_ref[...] if s == own else _dot(xb, win_ref[:, s * shard_cols:(s + 1) * shard_cols])
                     for s in range(N_CHIPS)], axis=1)
                after_projection(p[:, 0:A_WIDTH], p[:, A_WIDTH:2 * A_WIDTH], p[:, 2 * A_WIDTH:3 * A_WIDTH])

    wide = jax.ShapeDtypeStruct((t_len, A_WIDTH), BF16)
    product, product_specs = [], []
    if own_product is not None:
        product = [own_product[0].reshape(1).astype(jnp.int32), own_product[1]]
        product_specs = [pl.BlockSpec(memory_space=pltpu.SMEM), _rows(tm, shard_cols)]
    return pl.pallas_call(
        body, name="layer_a_fwd", grid=(t_len // tm,),
        in_specs=[_rows(tm, D_MODEL), _resident(w_in.shape)] + product_specs
        + [_const(lng.shape), _const(lnb.shape), _const(ws.shape), _const(bsp_t.shape), _const(after.shape)],
        out_specs=[pl.BlockSpec((D_MODEL, tm), lambda i: (0, i)), _rows(tm, A_WIDTH), _rows(tm, A_WIDTH),
                   _rows(tm, A_WIDTH), _rows(tm, 1), _rows(tm, A_WIDTH)],
        out_shape=[jax.ShapeDtypeStruct((D_MODEL, t_len), BF16), wide, wide, wide,
                   jax.ShapeDtypeStruct((t_len, 1), F32), wide],
        scratch_shapes=[pltpu.VMEM((tm, A_WIDTH), F32)],
        compiler_params=_params(),
    )(x, w_in, *product, lng, lnb, ws, bsp_t, after)


def _layer_b_proj(x, y, w_out_a, g1, b1, w_in, w_kv):
    t_len = x.shape[0]
    tm = 2 * TM_MM

    def body(x_ref, y_ref, wout_ref, g_ref, b_ref, win_ref, wkv_ref, xh_ref, r1_ref, q_ref, z_ref, kd_ref, vd_ref):
        halves = [slice(k * TM_MM, (k + 1) * TM_MM) for k in range(2)]
        projected = [_dot(y_ref[rows, :], wout_ref[...]) for rows in halves]
        for rows, out_a in zip(halves, projected):
            xh, r1 = _ln_fwd(ALPHA * x_ref[rows, :] + out_a)
            xh_ref[rows, :] = xh
            r1_ref[rows, :] = r1
            h1 = (xh * g_ref[...] + b_ref[...]).astype(BF16)
            q_ref[rows, :] = (_dot(h1, win_ref[:, 0:B_WIDTH]) * SCALE).astype(BF16)
            z_ref[rows, :] = _dot(h1, win_ref[:, B_WIDTH:2 * B_WIDTH]).astype(BF16)
            kv = _dot(h1, wkv_ref[...])
            k0, k1 = _dup_halves(kv[:, 0:PAIR])
            v0, v1 = _dup_halves(kv[:, PAIR:2 * PAIR])
            kd_ref[rows, 0:PAIR] = k0.astype(BF16)
            kd_ref[rows, PAIR:2 * PAIR] = k1.astype(BF16)
            vd_ref[rows, 0:PAIR] = v0.astype(BF16)
            vd_ref[rows, PAIR:2 * PAIR] = v1.astype(BF16)

    return pl.pallas_call(
        body, name="layer_b_proj", grid=(t_len // tm,),
        in_specs=[_rows(tm, D_MODEL), _rows(tm, A_WIDTH), _resident(w_out_a.shape), _const(g1.shape),
                  _const(b1.shape), _resident(w_in.shape), _resident(w_kv.shape)],
        out_specs=[_rows(tm, D_MODEL), _rows(tm, 1), _rows(tm, B_WIDTH), _rows(tm, B_WIDTH), _rows(tm, 2 * PAIR),
                   _rows(tm, 2 * PAIR)],
        out_shape=[jax.ShapeDtypeStruct((t_len, D_MODEL), F32), jax.ShapeDtypeStruct((t_len, 1), F32),
                   jax.ShapeDtypeStruct((t_len, B_WIDTH), BF16), jax.ShapeDtypeStruct((t_len, B_WIDTH), BF16),
                   jax.ShapeDtypeStruct((t_len, 2 * PAIR), BF16), jax.ShapeDtypeStruct((t_len, 2 * PAIR), BF16)],
        compiler_params=_params(),
    )(x, y, w_out_a, g1, b1, w_in, w_kv)


GROUP = N_HEADS // N_KV
GROUP_Q = GROUP * CHUNK


def _window_tables():
    j = jnp.arange(2 * CHUNK, dtype=jnp.int32)[:, None]
    t = jnp.arange(CHUNK, dtype=jnp.int32)[None, :]
    dist = t + CHUNK - j
    inside = (dist >= 0) & (dist < CHUNK)
    return jnp.stack([inside & (j >= CHUNK), inside]).astype(F32)


def _band(ref, chunk_index, kvh):
    prev0 = pl.multiple_of(jnp.maximum(chunk_index - 1, 0) * CHUNK, CHUNK)
    cur0 = pl.multiple_of(chunk_index * CHUNK, CHUNK)
    cols = slice(kvh * PAIR, (kvh + 1) * PAIR)
    return jnp.concatenate([ref[pl.ds(prev0, CHUNK), cols], ref[pl.ds(cur0, CHUNK), cols]], axis=0)


def _group_tables(bias_ref, win_ref, sink_ref, chunk_index, kvh):
    bias = jnp.concatenate([bias_ref[kvh * GROUP + j] for j in range(GROUP)], axis=1)
    win = win_ref[jnp.minimum(chunk_index, 1)]
    mask = jnp.concatenate([win] * GROUP, axis=1) > 0.5
    sink = jnp.concatenate([jnp.full((1, CHUNK), sink_ref[0, kvh * GROUP + j], F32) for j in range(GROUP)], axis=1)
    return bias, mask, sink


def _attn_probs(qs, kband, bias, mask, sink):
    logits = jnp.where(mask, _dot_nt(kband, qs) + bias, NEG_INF)
    m = jnp.maximum(jnp.max(logits, axis=0, keepdims=True), sink)
    e = jnp.exp(logits - m)
    es = jnp.exp(sink - m)
    inv = 1.0 / (jnp.sum(e, axis=0, keepdims=True) + es)
    return e * inv, es * inv


def _half_mask():
    return lax.broadcasted_iota(jnp.int32, (CHUNK, PAIR), 1) < HEAD_DIM


def _stack_heads(src_ref, rows, kvh, dst_scr, lo):
    for j in range(GROUP):
        h = kvh * GROUP + j
        blk = src_ref[rows, (h // 2) * PAIR:(h // 2 + 1) * PAIR].astype(F32)
        keep = lo if h % 2 == 0 else ~lo
        dst_scr[j * CHUNK:(j + 1) * CHUNK, :] = jnp.where(keep, blk, 0.0).astype(BF16)


def _probs_spec(tm):
    return pl.BlockSpec((tm // CHUNK, N_KV, 2 * CHUNK, GROUP_Q), lambda i: (i, 0, 0, 0))


def _sink_probs_spec(tiles=1):
    return pl.BlockSpec((tiles, 8, GROUP_Q), lambda i: (i, 0, 0))


def _unstack_pairs(stacked, pp, lo):
    return jnp.where(lo, stacked[(2 * pp) * CHUNK:(2 * pp + 1) * CHUNK], stacked[(2 * pp + 1) * CHUNK:(2 * pp + 2) * CHUNK])


def _layer_b_fwd(q, zb, kd, vd, bias, win, sinks, xh1, g1, b1, w_out, g2, b2, tgt):
    t_len = q.shape[0]
    tm = 2 * TM_ATTN

    def body(q_ref, z_ref, kd_ref, vd_ref, bias_ref, win_ref, sink_ref, xh_ref, g1_ref, b1_ref, wout_ref, g2_ref,
             b2_ref, tgt_ref, o_ref, p_ref, ps_ref, dr_ref, loss_ref, dg_ref, db_ref, o_scr, qs_scr):
        i = pl.program_id(0)

        @pl.when(i == 0)
        def _():
            loss_ref[...] = jnp.zeros_like(loss_ref)
            dg_ref[...] = jnp.zeros_like(dg_ref)
            db_ref[...] = jnp.zeros_like(db_ref)

        lo = _half_mask()
        ps_ref[...] = jnp.zeros_like(ps_ref)
        per_part = TM_ATTN // CHUNK
        for part in range(tm // TM_ATTN):
            part_rows = slice(part * TM_ATTN, (part + 1) * TM_ATTN)
            for cp in range(per_part):
                ci = part * per_part + cp
                cg = i * (tm // CHUNK) + ci
                rows = slice(ci * CHUNK, (ci + 1) * CHUNK)
                for kvh in range(N_KV):
                    kband = _band(kd_ref, cg, kvh)
                    vband = _band(vd_ref, cg, kvh)
                    bias_g, mask, sink = _group_tables(bias_ref, win_ref, sink_ref, cg, kvh)
                    _stack_heads(q_ref, rows, kvh, qs_scr, lo)
                    p, p_sink = _attn_probs(qs_scr[...], kband, bias_g, mask, sink)
                    p = p.astype(BF16)
                    p_ref[ci, kvh] = p
                    ps_ref[part, cp * N_KV + kvh:cp * N_KV + kvh + 1, :] = p_sink
                    o_stack = _dot_tn(p, vband)
                    for pp in range(GROUP // 2):
                        pair = kvh * (GROUP // 2) + pp
                        o_scr[rows, pair * PAIR:(pair + 1) * PAIR] = _unstack_pairs(o_stack, pp, lo)
            o = o_scr[part_rows, :]
            o_ref[part_rows, :] = o.astype(BF16)
            sz, _ = _silu_parts(z_ref[part_rows, :].astype(F32))
            y = (o * sz).astype(BF16)
            h1 = xh_ref[part_rows, :] * g1_ref[...] + b1_ref[...]
            r = ALPHA * h1 + _dot(y, wout_ref[...])
            xh2, rstd2 = _ln_fwd(r)
            diff = xh2 * g2_ref[...] + b2_ref[...] - tgt_ref[part_rows, :]
            loss_ref[...] += jnp.sum(diff * diff, axis=0, keepdims=True)
            dh2 = diff * (1.0 / D_MODEL)
            dg_ref[...] += jnp.sum(dh2 * xh2, axis=0, keepdims=True)
            db_ref[...] += jnp.sum(dh2, axis=0, keepdims=True)
            dr_ref[part_rows, :] = _ln_bwd(dh2 * g2_ref[...], xh2, rstd2)

    vec = jax.ShapeDtypeStruct((1, D_MODEL), F32)
    return pl.pallas_call(
        body, name="layer_b_fwd", grid=(t_len // tm,),
        in_specs=[_rows(tm, B_WIDTH), _rows(tm, B_WIDTH), _resident(kd.shape), _resident(vd.shape),
                  _resident(bias.shape), _resident(win.shape), pl.BlockSpec(memory_space=pltpu.SMEM),
                  _rows(tm, D_MODEL), _const(g1.shape), _const(b1.shape), _resident(w_out.shape), _const(g2.shape),
                  _const(b2.shape), _rows(tm, D_MODEL)],
        out_specs=[_rows(tm, B_WIDTH), _probs_spec(tm), _sink_probs_spec(tm // TM_ATTN), _rows(tm, D_MODEL)]
        + [_const((1, D_MODEL))] * 3,
        out_shape=[jax.ShapeDtypeStruct((t_len, B_WIDTH), BF16),
                   jax.ShapeDtypeStruct((t_len // CHUNK, N_KV, 2 * CHUNK, GROUP_Q), BF16),
                   jax.ShapeDtypeStruct((t_len // TM_ATTN, 8, GROUP_Q), F32),
                   jax.ShapeDtypeStruct((t_len, D_MODEL), F32), vec, vec, vec],
        scratch_shapes=[pltpu.VMEM((tm, B_WIDTH), F32), pltpu.VMEM((GROUP_Q, PAIR), BF16)],
        compiler_params=_params(),
    )(q, zb, kd, vd, bias, win, sinks, xh1, g1, b1, w_out, g2, b2, tgt)


def _layer_b_bwd_attn(dr2, zb, o, q, kd, vd, probs, sink_probs, w_out):
    t_len = q.shape[0]
    tm = TM_BWD_ATTN
    n_steps = t_len // tm
    n_chunks = tm // CHUNK
    per_part = TM_ATTN // CHUNK

    def body(dr_ref, z_ref, o_ref, q_ref, kd_ref, vd_ref, p_ref, ps_ref, wout_ref,
             dq_ref, dz_ref, dkd_ref, dvd_ref, ck_ref, cv_ref, gw_ref, dsink_ref, dbias_ref,
             do_scr, qs_scr, dos_scr, gw_acc):
        i = pl.program_id(0)

        @pl.when(i == 0)
        def _():
            gw_acc[...] = jnp.zeros_like(gw_acc)
            dsink_ref[...] = jnp.zeros_like(dsink_ref)
            dbias_ref[...] = jnp.zeros_like(dbias_ref)

        drb = dr_ref[...].astype(BF16)
        per_kvh = 2
        n_blocks = N_KV * per_kvh
        block_cols = B_WIDTH // n_blocks

        def through_gate(b):
            cols = slice(b * block_cols, (b + 1) * block_cols)
            dy = _dot_nt(drb, wout_ref[cols, :])
            sz, dsz = _silu_parts(z_ref[:, cols].astype(F32))
            o_t = o_ref[:, cols].astype(F32)
            dz_ref[:, cols] = (dy * o_t * dsz).astype(BF16)
            do_scr[:, cols] = (dy * sz).astype(BF16)
            return (o_t * sz).astype(BF16)

        def weight_gradient(b, gated):
            cols = slice(b * block_cols, (b + 1) * block_cols)
            gw_acc[cols, :] += _dot_tn(gated, drb)

        gated = {b: through_gate(b) for b in range(per_kvh)}

        lo = _half_mask()
        for kvh in range(N_KV):
            kcols = slice(kvh * PAIR, (kvh + 1) * PAIR)
            dk_bands, dv_bands = [], []
            for ci in range(n_chunks):
                unit = kvh * n_chunks + ci
                if ci < per_kvh and kvh + 1 < N_KV:
                    gated[(kvh + 1) * per_kvh + ci] = through_gate((kvh + 1) * per_kvh + ci)
                if unit in gated:
                    weight_gradient(unit, gated.pop(unit))
                cg = i * n_chunks + ci
                rows = slice(ci * CHUNK, (ci + 1) * CHUNK)
                kband = _band(kd_ref, cg, kvh)
                vband = _band(vd_ref, cg, kvh)
                _stack_heads(q_ref, rows, kvh, qs_scr, lo)
                _stack_heads(do_scr, rows, kvh, dos_scr, lo)
                qs = qs_scr[...]
                dos = dos_scr[...]
                pb = p_ref[ci, kvh]
                p = pb.astype(F32)
                sink_row = (ci % per_part) * N_KV + kvh
                p_sink = ps_ref[ci // per_part, sink_row:sink_row + 1, :]
                dp = _dot_nt(vband, dos)
                delta = jnp.sum(p * dp, axis=0, keepdims=True)
                dlog = p * (dp - delta)
                for j in range(GROUP):
                    dbias_ref[kvh * GROUP + j] += dlog[:, j * CHUNK:(j + 1) * CHUNK]
                dsink_ref[kvh:kvh + 1, :] += -(p_sink * delta)
                ds = dlog.astype(BF16)
                dq_stack = _dot_tn(ds, kband) * SCALE
                for pp in range(GROUP // 2):
                    pair = kvh * (GROUP // 2) + pp
                    dq_ref[rows, pair * PAIR:(pair + 1) * PAIR] = _unstack_pairs(dq_stack, pp, lo).astype(BF16)
                dk_bands.append(_dot(ds, qs))
                dv_bands.append(_dot(pb, dos))
            for bands, out_ref, carry_ref in ((dk_bands, dkd_ref, ck_ref), (dv_bands, dvd_ref, cv_ref)):
                carry_ref[0, :, kcols] = bands[0][0:CHUNK]
                for ci in range(n_chunks):
                    own = bands[ci][CHUNK:2 * CHUNK]
                    if ci + 1 < n_chunks:
                        own = own + bands[ci + 1][0:CHUNK]
                    out_ref[ci * CHUNK:(ci + 1) * CHUNK, kcols] = own

        @pl.when(i == n_steps - 1)
        def _():
            gw_ref[...] = gw_acc[...].astype(BF16)

    carry_spec = pl.BlockSpec((1, CHUNK, 2 * PAIR), lambda i: (i, 0, 0))
    carry_shape = jax.ShapeDtypeStruct((n_steps, CHUNK, 2 * PAIR), F32)
    bias_shape = (N_HEADS, 2 * CHUNK, CHUNK)
    return pl.pallas_call(
        body, name="layer_b_bwd_attn", grid=(n_steps,),
        in_specs=[_rows(tm, D_MODEL), _rows(tm, B_WIDTH), _rows(tm, B_WIDTH), _rows(tm, B_WIDTH),
                  _resident(kd.shape), _resident(vd.shape), _probs_spec(tm), _sink_probs_spec(tm // TM_ATTN),
                  _resident(w_out.shape)],
        out_specs=[_rows(tm, B_WIDTH), _rows(tm, B_WIDTH), _rows(tm, 2 * PAIR), _rows(tm, 2 * PAIR),
                   carry_spec, carry_spec, _const(w_out.shape), _const((N_KV, GROUP_Q)), _const(bias_shape)],
        out_shape=[jax.ShapeDtypeStruct((t_len, B_WIDTH), BF16), jax.ShapeDtypeStruct((t_len, B_WIDTH), BF16),
                   jax.ShapeDtypeStruct((t_len, 2 * PAIR), F32), jax.ShapeDtypeStruct((t_len, 2 * PAIR), F32),
                   carry_shape, carry_shape, jax.ShapeDtypeStruct(w_out.shape, BF16),
                   jax.ShapeDtypeStruct((N_KV, GROUP_Q), F32), jax.ShapeDtypeStruct(bias_shape, F32)],
        scratch_shapes=[pltpu.VMEM((tm, B_WIDTH), BF16), pltpu.VMEM((GROUP_Q, PAIR), BF16),
                        pltpu.VMEM((GROUP_Q, PAIR), BF16), pltpu.VMEM(w_out.shape, F32)],
        compiler_params=_params(),
    )(dr2, zb, o, q, kd, vd, probs, sink_probs, w_out)


def _layer_b_bwd_proj(xh1, rstd1, g1, b1, dr2, dq, dzb, dkd, dvd, carry_k, carry_v, w_in, w_kv):
    t_len = xh1.shape[0]
    tm = TM_MM
    n_steps = t_len // tm
    per_tile = tm // TM_BWD_ATTN
    n_carry = carry_k.shape[0]

    def body(xh_ref, rstd_ref, g_ref, b_ref, dr2_ref, dq_ref, dz_ref, dkd_ref, dvd_ref, *rest):
        carry_refs = rest[:2 * per_tile]
        win_ref, wkv_ref, dr1_ref, dg_ref, db_ref, gwin_ref, gwkv_ref, acc_in, acc_kv = rest[2 * per_tile:]
        i = pl.program_id(0)

        @pl.when(i == 0)
        def _():
            acc_in[...] = jnp.zeros_like(acc_in)
            acc_kv[...] = jnp.zeros_like(acc_kv)
            dg_ref[...] = jnp.zeros_like(dg_ref)
            db_ref[...] = jnp.zeros_like(db_ref)

        lo = lax.broadcasted_iota(jnp.int32, (tm, PAIR), 1) < HEAD_DIM

        def heads_gradient(tile_ref, refs):
            parts = []
            for a in range(per_tile):
                parts.append(tile_ref[a * TM_BWD_ATTN:(a + 1) * TM_BWD_ATTN - CHUNK, :])
                carry = refs[a][0]
                if a == per_tile - 1:
                    carry = jnp.where(i < n_steps - 1, carry, 0.0)
                parts.append(tile_ref[(a + 1) * TM_BWD_ATTN - CHUNK:(a + 1) * TM_BWD_ATTN, :] + carry)
            dup = jnp.concatenate(parts, axis=0)
            return jnp.where(lo, _fold_halves(dup[:, 0:PAIR]), _fold_halves(dup[:, PAIR:2 * PAIR]))

        xh = xh_ref[...]
        h1 = (xh * g_ref[...] + b_ref[...]).astype(BF16)
        dq_t = dq_ref[...]
        dz_t = dz_ref[...]
        dkv = jnp.concatenate([heads_gradient(dkd_ref, carry_refs[:per_tile]),
                               heads_gradient(dvd_ref, carry_refs[per_tile:])], axis=1).astype(BF16)
        dh1 = ALPHA * dr2_ref[...]
        dh1 += _dot_nt(dq_t, win_ref[:, 0:B_WIDTH])
        dh1 += _dot_nt(dz_t, win_ref[:, B_WIDTH:2 * B_WIDTH])
        dh1 += _dot_nt(dkv, wkv_ref[...])
        acc_in[:, 0:B_WIDTH] += _dot_tn(h1, dq_t)
        acc_in[:, B_WIDTH:2 * B_WIDTH] += _dot_tn(h1, dz_t)
        acc_kv[...] += _dot_tn(h1, dkv)
        dg_ref[...] += jnp.sum(dh1 * xh, axis=0, keepdims=True)
        db_ref[...] += jnp.sum(dh1, axis=0, keepdims=True)
        dr1_ref[...] = _ln_bwd(dh1 * g_ref[...], xh, rstd_ref[...])

        @pl.when(i == n_steps - 1)
        def _():
            half_rows = D_MODEL // 2
            shard_cols = 2 * B_WIDTH // N_CHIPS
            for s in range(N_CHIPS):
                for c in range(2):
                    gwin_ref[2 * s + c] = acc_in[c * half_rows:(c + 1) * half_rows,
                                                 s * shard_cols:(s + 1) * shard_cols].astype(BF16)
            gwkv_ref[...] = acc_kv[...].astype(BF16)

    vec = jax.ShapeDtypeStruct((1, D_MODEL), F32)
    gwin_shape = (N_DEV, D_MODEL // 2, 2 * B_WIDTH // N_CHIPS)

    def carry_spec(a):
        return pl.BlockSpec((1, CHUNK, 2 * PAIR), lambda i: (jnp.minimum(per_tile * i + a + 1, n_carry - 1), 0, 0))

    carry_specs = [carry_spec(a) for a in range(per_tile)]
    return pl.pallas_call(
        body, name="layer_b_bwd_proj", grid=(n_steps,),
        in_specs=[_rows(tm, D_MODEL), _rows(tm, 1), _const(g1.shape), _const(b1.shape), _rows(tm, D_MODEL),
                  _rows(tm, B_WIDTH), _rows(tm, B_WIDTH), _rows(tm, 2 * PAIR), _rows(tm, 2 * PAIR)]
        + carry_specs + carry_specs + [_resident(w_in.shape), _resident(w_kv.shape)],
        out_specs=[_rows(tm, D_MODEL), _const((1, D_MODEL)), _const((1, D_MODEL)), _const(gwin_shape),
                   _const(w_kv.shape)],
        out_shape=[jax.ShapeDtypeStruct((t_len, D_MODEL), F32), vec, vec,
                   jax.ShapeDtypeStruct(gwin_shape, BF16), jax.ShapeDtypeStruct(w_kv.shape, BF16)],
        scratch_shapes=[pltpu.VMEM(w_in.shape, F32), pltpu.VMEM(w_kv.shape, F32)],
        compiler_params=_params(),
    )(xh1, rstd1, g1, b1, dr2, dq, dzb, dkd, dvd, *([carry_k] * per_tile), *([carry_v] * per_tile), w_in, w_kv)


def _layer_a_bwd_mix(dr1, u, vh, z, y, rv, w_out, lng, lnb, ws, bsp_t, after):
    t_len = u.shape[0]
    tm = TM_ATTN
    n_steps = t_len // tm

    def body(dr_ref, u_ref, vh_ref, z_ref, y_ref, rv_ref, wout_ref, lng_ref, lnb_ref, ws_ref, bsp_ref, after_ref,
             dp_ref, gw_ref, dws_ref, dbsp_ref, dgs_ref, dbs_ref, dvn_scr, gw_acc):
        i = pl.program_id(0)

        @pl.when(i == 0)
        def _():
            gw_acc[...] = jnp.zeros_like(gw_acc)
            dws_ref[...] = jnp.zeros_like(dws_ref)
            dbsp_ref[...] = jnp.zeros_like(dbsp_ref)
            dgs_ref[...] = jnp.zeros_like(dgs_ref)
            dbs_ref[...] = jnp.zeros_like(dbs_ref)

        drb = dr_ref[...].astype(BF16)

        def group_cols(g):
            return slice(g * A_GROUP_DIM, (g + 1) * A_GROUP_DIM)

        tri = (lax.broadcasted_iota(jnp.int32, (CHUNK, CHUNK), 0)
               >= lax.broadcasted_iota(jnp.int32, (CHUNK, CHUNK), 1))
        lane = lax.broadcasted_iota(jnp.int32, (CHUNK, CHUNK), 1)
        ones = jnp.ones((CHUNK, A_GROUP_DIM), BF16)
        dbsp = jnp.zeros((CHUNK, CHUNK), F32)
        dy_next = _dot_nt(drb, wout_ref[group_cols(0), :])
        for g in range(A_GROUPS):
            wsg = jnp.where(tri, ws_ref[g], 0.0).astype(BF16)
            cols = group_cols(g)
            cols_z = slice(2 * A_WIDTH + g * A_GROUP_DIM, 2 * A_WIDTH + (g + 1) * A_GROUP_DIM)
            dy_g = dy_next
            if g + 1 < A_GROUPS:
                dy_next = _dot_nt(drb, wout_ref[group_cols(g + 1), :])
            gw_acc[cols, :] += _dot_tn(y_ref[:, cols], drb)
            both = jnp.zeros((CHUNK, 2 * CHUNK), F32)
            for ci in range(tm // CHUNK):
                rows = slice(ci * CHUNK, (ci + 1) * CHUNK)
                vn = (vh_ref[rows, cols].astype(F32) * lng_ref[:, cols] + lnb_ref[:, cols]).astype(BF16)
                s = _dot(wsg, vn) + bsp_ref[:, g:g + 1]
                sz, dsz = _silu_parts(z_ref[rows, cols].astype(F32))
                dy = dy_g[rows]
                t = dy * u_ref[rows, cols].astype(F32)
                dp_ref[rows, cols] = (dy * (s * sz)).astype(BF16)
                dp_ref[rows, cols_z] = (t * s * dsz).astype(BF16)
                ds_b = (t * sz).astype(BF16)
                both += _dot_nt(ds_b, jnp.concatenate([vn, ones], axis=0))
                dvn_scr[rows, cols] = _dot_tn(wsg, ds_b)
            dws_ref[g] += jnp.where(tri, both[:, 0:CHUNK], 0.0)
            dbsp = jnp.where(lane == g, both[:, CHUNK:2 * CHUNK], dbsp)
        dbsp_ref[...] += dbsp
        dvn = dvn_scr[...]
        vh_t = vh_ref[...].astype(F32)
        dgs_ref[...] += jnp.sum(dvn * vh_t, axis=0, keepdims=True)
        dbs_ref[...] += jnp.sum(dvn, axis=0, keepdims=True)
        dp_ref[:, A_WIDTH:2 * A_WIDTH] = _ln_bwd(dvn * lng_ref[...], vh_t, rv_ref[...]).astype(BF16)

        @pl.when(i == n_steps - 1)
        def _():
            gw_ref[...] = gw_acc[...].astype(BF16)

    wide = jax.ShapeDtypeStruct((1, A_WIDTH), F32)
    return pl.pallas_call(
        body, name="layer_a_bwd_mix", grid=(n_steps,),
        in_specs=[_rows(tm, D_MODEL), _rows(tm, A_WIDTH), _rows(tm, A_WIDTH), _rows(tm, A_WIDTH), _rows(tm, A_WIDTH),
                  _rows(tm, 1), _resident(w_out.shape), _const(lng.shape), _const(lnb.shape), _const(ws.shape),
                  _const(bsp_t.shape), _const(after.shape)],
        out_specs=[_rows(tm, 3 * A_WIDTH), _const(w_out.shape), _const(ws.shape), _const((CHUNK, CHUNK)),
                   _const((1, A_WIDTH)), _const((1, A_WIDTH))],
        out_shape=[jax.ShapeDtypeStruct((t_len, 3 * A_WIDTH), BF16), jax.ShapeDtypeStruct(w_out.shape, BF16),
                   jax.ShapeDtypeStruct(ws.shape, F32), jax.ShapeDtypeStruct((CHUNK, CHUNK), F32),
                   wide, wide],
        scratch_shapes=[pltpu.VMEM((tm, A_WIDTH), F32), pltpu.VMEM(w_out.shape, F32)],
        compiler_params=_params(),
    )(dr1, u, vh, z, y, rv, w_out, lng, lnb, ws, bsp_t, after)


def _layer_a_bwd_dx(dr1, dp, w_in, after, updates=()):
    t_len = dr1.shape[0]
    tm = TM_MM
    n_steps = t_len // tm
    n_upd = len(updates)

    def body(dr_ref, dp_ref, win_ref, after_ref, *refs):
        upd_in, dx_ref, upd_out = refs[:4 * n_upd], refs[4 * n_upd], refs[4 * n_upd + 1:]
        dx_ref[...] = ALPHA * dr_ref[...] + _dot_nt(dp_ref[...], win_ref[...])
        for k in range(n_upd):
            w_ref, g_ref, m_ref, v_ref = upd_in[4 * k:4 * k + 4]
            g_out, d_ref, nm_ref, nv_ref = upd_out[4 * k:4 * k + 4]
            g_out[...] = g_ref[...]
            _adamw_update(w_ref, g_ref, m_ref, v_ref, d_ref, nm_ref, nv_ref)

    upd_specs, upd_shapes, upd_args = [], [], []
    for w, g, m, v in updates:
        rows, cols = w.shape
        upd_specs.append(pl.BlockSpec((rows // n_steps, cols), lambda i: (i, 0)))
        upd_shapes.append(jax.ShapeDtypeStruct((rows, cols), F32))
        upd_args += [w, g, m, v]
    return pl.pallas_call(
        body, name="layer_a_bwd_dx", grid=(n_steps,),
        in_specs=[_rows(tm, D_MODEL), _rows(tm, 3 * A_WIDTH), _resident(w_in.shape), _const(after.shape)]
        + [s for s in upd_specs for _ in range(4)],
        out_specs=[_rows(tm, D_MODEL)] + [s for s in upd_specs for _ in range(4)],
        out_shape=[jax.ShapeDtypeStruct((t_len, D_MODEL), F32)] + [s for s in upd_shapes for _ in range(4)],
        compiler_params=_params(),
    )(dr1, dp, w_in, after, *upd_args)


def _layer_a_bwd_win(xt, dp, after):
    t_len = xt.shape[1]
    tm = TM_WIN
    n_steps = t_len // tm
    shard_cols = 3 * A_WIDTH // N_CHIPS
    half_rows = D_MODEL // 2

    def body(xt_ref, dp_ref, after_ref, gw_ref, acc):
        i = pl.program_id(1)

        @pl.when(i == 0)
        def _():
            acc[...] = jnp.zeros_like(acc)

        acc[...] += _dot(xt_ref[...], dp_ref[...])

        @pl.when(i == n_steps - 1)
        def _():
            for c in range(2):
                gw_ref[0, c] = acc[c * half_rows:(c + 1) * half_rows, :].astype(BF16)

    return pl.pallas_call(
        body, name="layer_a_bwd_win", grid=(N_CHIPS, n_steps),
        in_specs=[pl.BlockSpec((D_MODEL, tm), lambda j, i: (0, i)),
                  pl.BlockSpec((tm, shard_cols), lambda j, i: (i, j)), _const(after.shape)],
        out_specs=pl.BlockSpec((1, 2, half_rows, shard_cols), lambda j, i: (j, 0, 0, 0)),
        out_shape=jax.ShapeDtypeStruct((N_CHIPS, 2, half_rows, shard_cols), BF16),
        scratch_shapes=[pltpu.VMEM((D_MODEL, shard_cols), F32)],
        compiler_params=_params(("arbitrary", "arbitrary")),
    )(xt, dp, after)


def _bucket_onehot():
    dist = jnp.arange(CHUNK, dtype=jnp.int32)[None, :]
    max_exact = REL_BUCKETS // 2
    df = jnp.maximum(dist, 1).astype(F32)
    large = max_exact + (jnp.log(df / max_exact) / math.log(CHUNK / max_exact)
                         * (REL_BUCKETS - max_exact)).astype(jnp.int32)
    bucket = jnp.where(dist < max_exact, dist, jnp.minimum(large, REL_BUCKETS - 1))
    onehot = bucket == jnp.arange(REL_BUCKETS, dtype=jnp.int32)[:, None]
    return onehot.astype(F32)


def _bias_expand(rel_t, onehot):
    def body(rel_ref, oh_ref, out_ref):
        by_distance = jnp.dot(rel_ref[...], oh_ref[...], preferred_element_type=F32,
                              precision=lax.Precision.HIGHEST)
        for h in range(N_HEADS):
            rows = jnp.broadcast_to(by_distance[h:h + 1, :], (2 * CHUNK, CHUNK))
            out_ref[h] = pltpu.roll(rows, 0, 1, stride=1, stride_axis=0)

    return pl.pallas_call(
        body, name="bias_expand",
        out_shape=jax.ShapeDtypeStruct((N_HEADS, 2 * CHUNK, CHUNK), F32),
    )(rel_t, onehot)


def _bias_reduce(oh_ref, db_ref):
    sublane = lax.broadcasted_iota(jnp.int32, (_SUBLANES, CHUNK), 0)
    rows = []
    for h in range(N_HEADS):
        part = db_ref[h, 0:_SUBLANES, :]
        for a in range(1, 2 * CHUNK // _SUBLANES):
            tile = db_ref[h, a * _SUBLANES:(a + 1) * _SUBLANES, :]
            back = (-a * _SUBLANES) % CHUNK
            part += pltpu.roll(tile, back, 1) if back else tile
        total = jnp.where(sublane == 0, part, 0.0)
        for s in range(1, _SUBLANES):
            total += jnp.where(sublane == s, pltpu.roll(part, CHUNK - s, 1), 0.0)
        rows.append(jnp.sum(total, axis=0, keepdims=True))
    by_distance = jnp.concatenate(rows, axis=0)
    return lax.dot_general(oh_ref[...], by_distance, (((1,), (1,)), ((), ())),
                           preferred_element_type=F32, precision=lax.Precision.HIGHEST)


_SMALL_SHAPES = dict(w_spatial=(A_GROUPS, CHUNK, CHUNK), b_spatial=(A_GROUPS, CHUNK), attn_sinks=(1, N_HEADS),
                     rel_bias=(REL_BUCKETS, N_HEADS), post_ln_g=(2, D_MODEL), post_ln_b=(2, D_MODEL),
                     sgu_ln_g=(1, A_WIDTH), sgu_ln_b=(1, A_WIDTH), loss=(1, 1))
_SMALL_ORDER = tuple(_SMALL_SHAPES)


def _small_rows(name):
    shape = _SMALL_SHAPES[name]
    rows = math.prod(shape[:-1]) if shape[-1] < _LANES else math.prod(shape) // _LANES
    return -(-rows // _SUBLANES) * _SUBLANES


def _small_offset(name):
    return sum(_small_rows(n) for n in _SMALL_ORDER[:_SMALL_ORDER.index(name)])


def _pack_small(dws, dbsp, dsink, dbias, onehot, post_g, post_b, dgs, dbs, loss_vec):
    def body(dws_ref, dbsp_ref, dsink_ref, db_ref, oh_ref, g1_ref, g2_ref, b1_ref, b2_ref, dgs_ref, dbs_ref,
             loss_ref, out_ref):
        out_ref[...] = jnp.zeros_like(out_ref)

        def put_flat(name, refs):
            row = _small_offset(name)
            for ref in refs:
                for k in range(ref.shape[1] // _LANES):
                    out_ref[row:row + 1, :] = ref[:, k * _LANES:(k + 1) * _LANES]
                    row += 1

        row = _small_offset("w_spatial")
        for g in range(A_GROUPS):
            out_ref[row + g * CHUNK:row + (g + 1) * CHUNK, :] = dws_ref[g]
        row = _small_offset("b_spatial")
        out_ref[row:row + A_GROUPS, :] = dbsp_ref[...].T[0:A_GROUPS, :]
        lane = lax.broadcasted_iota(jnp.int32, (1, _LANES), 1)
        sinks = jnp.zeros((1, _LANES), F32)
        for h in range(N_HEADS):
            per_query = dsink_ref[h // GROUP:h // GROUP + 1, (h % GROUP) * CHUNK:(h % GROUP + 1) * CHUNK]
            sinks = jnp.where(lane == h, jnp.sum(per_query, axis=1, keepdims=True), sinks)
        row = _small_offset("attn_sinks")
        out_ref[row:row + 1, :] = sinks
        row = _small_offset("rel_bias")
        out_ref[row:row + REL_BUCKETS, 0:N_HEADS] = _bias_reduce(oh_ref, db_ref)
        put_flat("post_ln_g", [g1_ref, g2_ref])
        put_flat("post_ln_b", [b1_ref, b2_ref])
        put_flat("sgu_ln_g", [dgs_ref])
        put_flat("sgu_ln_b", [dbs_ref])
        row = _small_offset("loss")
        out_ref[row:row + 1, 0:1] = (0.5 / D_MODEL) * jnp.sum(loss_ref[...], axis=1, keepdims=True)

    total_rows = sum(_small_rows(n) for n in _SMALL_ORDER)
    return pl.pallas_call(
        body, name="pack_small",
        out_shape=jax.ShapeDtypeStruct((total_rows, _LANES), F32),
    )(dws, dbsp, dsink, dbias, onehot, *post_g, *post_b, dgs, dbs, loss_vec)


def _place():
    return lax.axis_index("x"), lax.axis_index("y"), lax.axis_index("c")


RELAY_PIECES = 4


def _shard_window(full_ref, shard_shape, col_sharded, s, half, piece=None):
    rows, cols = shard_shape
    if half is None:
        start, size = 0, rows
    elif piece is None:
        start, size = half * (rows // 2), rows // 2
    else:
        size = rows // 2 // RELAY_PIECES
        start = (half * RELAY_PIECES + piece) * size
    if col_sharded:
        return full_ref.at[pl.ds(start, size), pl.ds(s * cols, cols)]
    return full_ref.at[pl.ds(s * rows + start, size), :]


def _other_chips(x, y):
    return [(1 - x, y), (x, 1 - y), (1 - x, 1 - y)]


def _gather_weights(shards, col_sharded, fetch, ln_shard, tokens=None):
    n_w = len(shards)
    fetched = [w for w in range(n_w) if fetch[w]]
    full_shapes = []
    for w, cs in zip(shards, col_sharded):
        r, c = w.shape
        full_shapes.append((r, c * N_CHIPS) if cs else (r * N_CHIPS, c))

    n_tok = 0 if tokens is None else 1
    tok_tile = TM_MM
    n_tiles = 0 if tokens is None else tokens.shape[0] // tok_tile

    def body(*refs):
        refs = list(refs)
        in_refs = [refs.pop(0) for _ in range(n_w)]
        ln_ref = refs.pop(0)
        tok_ref = refs.pop(0) if n_tok else None
        full_refs = [refs.pop(0) for _ in range(n_w)]
        ln_full = refs.pop(0)
        prod_ref = refs.pop(0) if n_tok else None
        raw = [refs.pop(0) for _ in range(n_w)]
        stage = [refs.pop(0) for _ in range(n_w)]
        send_sems, recv_sems, load_sems, local_sems, ln_send, ln_recv = refs[:6]
        tok_buf, prod_buf, tok_sems, prod_sems = refs[6:] if n_tok else (None,) * 4
        x, y, c = _place()
        s_me = 2 * x + y
        chips = _other_chips(x, y)
        pieces = range(RELAY_PIECES)

        def shard_window(w, s, half, piece=None):
            return _shard_window(full_refs[w], shards[w].shape, col_sharded[w], s, half, piece)

        def piece_rows(w, half, piece):
            rows = shards[w].shape[0] // 2 // RELAY_PIECES
            return pl.ds(pl.multiple_of((half * RELAY_PIECES + piece) * rows, rows), rows)

        def ici_copy(w, k, sender_shard, piece):
            idx = (w * 3 + k) * RELAY_PIECES + piece
            return pltpu.make_async_remote_copy(
                src_ref=stage[w].at[piece_rows(w, c, piece), :], dst_ref=shard_window(w, sender_shard, c, piece),
                send_sem=send_sems.at[idx], recv_sem=recv_sems.at[idx],
                device_id=(*chips[k], c), device_id_type=MESH)

        def d2d_copy(w, k, half, piece):
            s_k = 2 * chips[k][0] + chips[k][1]
            win = shard_window(w, s_k, half, piece)
            idx = (3 * n_w + w * 3 + k) * RELAY_PIECES + piece
            return pltpu.make_async_remote_copy(
                src_ref=win, dst_ref=win, send_sem=send_sems.at[idx], recv_sem=recv_sems.at[idx],
                device_id=(x, y, 1 - c), device_id_type=MESH)

        def ln_copy(k, slot):
            return pltpu.make_async_remote_copy(
                src_ref=ln_ref, dst_ref=ln_full.at[slot], send_sem=ln_send.at[k], recv_sem=ln_recv.at[k],
                device_id=(*chips[k], c), device_id_type=MESH)

        loads = []

        def load(w, rows):
            window = (rows, slice(None)) if rows is not None else (slice(None), slice(None))
            cp = pltpu.make_async_copy(in_refs[w].at[window], raw[w].at[window], load_sems.at[len(loads)])
            cp.start()
            loads.append((cp, w, window))

        for half in (c, 1 - c):
            for w in fetched:
                for q in pieces:
                    load(w, piece_rows(w, half, q))
        for w in range(n_w):
            if not fetch[w]:
                load(w, None)

        def to_bf16(k):
            cp, w, window = loads[k]
            cp.wait()
            stage[w][window] = raw[w][window].astype(BF16)

        ln_full[s_me] = ln_ref[...]
        def shard_of(k):
            return 2 * chips[k][0] + chips[k][1]

        relay_from = jnp.where(c == 0, shard_of(0), shard_of(1))
        relay_to = (jnp.where(c == 0, x, 1 - x), jnp.where(c == 0, 1 - y, y), c)

        def relay_copy(w, sender_shard, piece):
            win = shard_window(w, sender_shard, c, piece)
            idx = (w * 3 + 2) * RELAY_PIECES + piece
            return pltpu.make_async_remote_copy(
                src_ref=win, dst_ref=win, send_sem=send_sems.at[idx], recv_sem=recv_sems.at[idx],
                device_id=relay_to, device_id_type=MESH)

        first = [ln_copy(k, s_me) for k in range(3)]
        for cp in first:
            cp.start()
        n_sent = 0
        for w in fetched:
            for q in pieces:
                to_bf16(n_sent)
                n_sent += 1
                for k in range(2):
                    cp = ici_copy(w, k, s_me, q)
                    cp.start()
                    first.append(cp)
        for k in range(n_sent, len(loads)):
            to_bf16(k)
        own = [pltpu.make_async_copy(stage[w], shard_window(w, s_me, None), local_sems.at[w]) for w in range(n_w)]
        for cp in own:
            cp.start()

        def tok_copy(t):
            return pltpu.make_async_copy(tok_ref.at[pl.ds(t * tok_tile, tok_tile), :], tok_buf.at[t % 2],
                                         tok_sems.at[t % 2])

        def prod_copy(t):
            return pltpu.make_async_copy(prod_buf.at[t % 2], prod_ref.at[pl.ds(t * tok_tile, tok_tile), :],
                                         prod_sems.at[t % 2])

        def product_tiles(tiles):
            for t in tiles:
                if t + 1 < n_tiles:
                    tok_copy(t + 1).start()
                tok_copy(t).wait()
                if t >= 2:
                    prod_copy(t - 2).wait()
                prod_buf[t % 2] = _dot(tok_buf[t % 2].astype(BF16), stage[0][...])
                prod_copy(t).start()

        if n_tiles:
            tok_copy(0).start()
        share = [4, 3, 3, 3, 3, 0, 0, 0]
        assert len(share) == 2 * RELAY_PIECES
        bounds = [sum(share[:k]) * n_tiles // sum(share) for k in range(len(share) + 1)]
        passed = []
        for w in fetched:
            for q in pieces:
                if w == fetched[0]:
                    product_tiles(range(bounds[q], bounds[q + 1]))
                for k in range(2):
                    ici_copy(w, k, shard_of(k), q).wait_recv()
                relay = relay_copy(w, relay_from, q)
                relay.start()
                passed.append(relay)
                for k in range(2):
                    fwd = d2d_copy(w, k, c, q)
                    fwd.start()
                    passed.append(fwd)
        for w in fetched:
            for q in pieces:
                if w == fetched[0]:
                    product_tiles(range(bounds[RELAY_PIECES + q], bounds[RELAY_PIECES + q + 1]))
                relay_copy(w, shard_of(2), q).wait_recv()
                fwd = d2d_copy(w, 2, c, q)
                fwd.start()
                passed.append(fwd)
        for w in fetched:
            for k in range(3):
                for q in pieces:
                    d2d_copy(w, k, 1 - c, q).wait_recv()
        for k in range(3):
            ln_copy(k, 2 * chips[k][0] + chips[k][1]).wait_recv()
        for cp in first + passed:
            cp.wait_send()
        for cp in own:
            cp.wait()
        for t in range(max(n_tiles - 2, 0), n_tiles):
            prod_copy(t).wait()

    vmem = pl.BlockSpec(memory_space=pltpu.VMEM)
    hbm = pl.BlockSpec(memory_space=pl.ANY)
    prod_cols = shards[0].shape[1]
    tok_args = [] if tokens is None else [tokens]
    tok_out = [] if tokens is None else [jax.ShapeDtypeStruct((tokens.shape[0], prod_cols), F32)]
    tok_scratch = [] if tokens is None else [
        pltpu.VMEM((2, tok_tile, tokens.shape[1]), F32), pltpu.VMEM((2, tok_tile, prod_cols), F32),
        pltpu.SemaphoreType.DMA((2,)), pltpu.SemaphoreType.DMA((2,))]
    return pl.pallas_call(
        body, name="gather_weights",
        in_specs=[hbm] * n_w + [vmem] + [hbm] * n_tok,
        out_specs=[hbm] * n_w + [vmem] + [hbm] * n_tok,
        out_shape=[jax.ShapeDtypeStruct(s, BF16) for s in full_shapes]
        + [jax.ShapeDtypeStruct((N_CHIPS,) + ln_shard.shape, F32)] + tok_out,
        scratch_shapes=[pltpu.VMEM(w.shape, F32) for w in shards] + [pltpu.VMEM(w.shape, BF16) for w in shards]
        + [pltpu.SemaphoreType.DMA((6 * RELAY_PIECES * n_w,)), pltpu.SemaphoreType.DMA((6 * RELAY_PIECES * n_w,)),
           pltpu.SemaphoreType.DMA((2 * RELAY_PIECES * len(fetched) + n_w - len(fetched),)),
           pltpu.SemaphoreType.DMA((n_w,)), pltpu.SemaphoreType.DMA((3,)), pltpu.SemaphoreType.DMA((3,))]
        + tok_scratch,
        compiler_params=pltpu.CompilerParams(vmem_limit_bytes=VMEM_LIMIT),
    )(*shards, ln_shard, *tok_args)


def _fetch_copy(full_ref, shard_shape, col_sharded, sender_shard, send_sems, recv_sems, idx, chip, c):
    win = _shard_window(full_ref, shard_shape, col_sharded, sender_shard, None)
    return pltpu.make_async_remote_copy(src_ref=win, dst_ref=win, send_sem=send_sems.at[idx],
                                        recv_sem=recv_sems.at[idx], device_id=(*chip, c), device_id_type=MESH)


def _fetch_start(fulls, shard_shapes, col_sharded):
    n = len(fulls)

    def body(*refs):
        full = refs[:n]
        send_sems, recv_sems = refs[n], refs[n + 1]
        token = refs[-1]
        x, y, c = _place()
        for w in range(n):
            for k, chip in enumerate(_other_chips(x, y)):
                _fetch_copy(full[w], shard_shapes[w], col_sharded[w], 2 * x + y, send_sems, recv_sems, w * 3 + k,
                            chip, c).start()
        token[...] = jnp.zeros_like(token)

    outs = pl.pallas_call(
        body, name="fetch_start",
        out_shape=(pltpu.SemaphoreType.DMA((3 * n,)), pltpu.SemaphoreType.DMA((3 * n,)),
                   *[pltpu.HBM(f.shape, f.dtype) for f in fulls], jax.ShapeDtypeStruct((8, 128), F32)),
        in_specs=[_HBM] * n,
        out_specs=(_SEM, _SEM, *([_HBM] * n), pl.BlockSpec(memory_space=pltpu.VMEM)),
        input_output_aliases={i: 2 + i for i in range(n)},
        compiler_params=pltpu.CompilerParams(has_side_effects=pltpu.SideEffectType.DATAFLOW_SIDE_EFFECTING),
    )(*[pltpu.with_memory_space_constraint(f, pltpu.HBM) for f in fulls])
    return dict(send=outs[0], recv=outs[1], full=list(outs[2:2 + n])), outs[-1]


def _fetch_wait(group, shard_shapes, col_sharded, after):
    n = len(group["full"])

    def body(*refs):
        full = refs[:n]
        send_sems, recv_sems = refs[n], refs[n + 1]
        x, y, c = _place()
        for w in range(n):
            for k, chip in enumerate(_other_chips(x, y)):
                _fetch_copy(full[w], shard_shapes[w], col_sharded[w], 2 * x + y, send_sems, recv_sems, w * 3 + k,
                            chip, c).wait_send()
                _fetch_copy(full[w], shard_shapes[w], col_sharded[w], 2 * chip[0] + chip[1], send_sems, recv_sems,
                            w * 3 + k, chip, c).wait_recv()

    outs = pl.pallas_call(
        body, name="fetch_wait", out_shape=tuple(pltpu.HBM(f.shape, f.dtype) for f in group["full"]),
        in_specs=[_HBM] * n + [_SEM, _SEM, pl.BlockSpec(memory_space=pl.ANY)],
        out_specs=tuple([_HBM] * n), input_output_aliases={i: i for i in range(n)},
        compiler_params=pltpu.CompilerParams(has_side_effects=pltpu.SideEffectType.DATAFLOW_SIDE_EFFECTING),
    )(*group["full"], group["send"], group["recv"], after)
    return list(outs)


_HBM = pl.BlockSpec(memory_space=pltpu.HBM)
_SEM = pl.BlockSpec(memory_space=pltpu.SEMAPHORE)
_N_PEER = N_DEV - 1


def _peer(x, y, c, k):
    return (x + (k >> 2)) % 2, (y + ((k >> 1) & 1)) % 2, (c + (k & 1)) % 2


def _exchange_copy(src_ref, land_ref, sliced, send_sems, recv_sems, idx, x, y, c, k):
    px, py, pc = _peer(x, y, c, k)
    src = src_ref.at[4 * px + 2 * py + pc] if sliced else src_ref
    return pltpu.make_async_remote_copy(
        src_ref=src, dst_ref=land_ref.at[4 * x + 2 * y + c],
        send_sem=send_sems.at[idx], recv_sem=recv_sems.at[idx], device_id=(px, py, pc), device_id_type=MESH)


def _exchange_start(tag, arrays, sliced):
    n = len(arrays)
    lands = [lax.empty(a.shape if s else (N_DEV,) + a.shape, a.dtype) for a, s in zip(arrays, sliced)]

    def body(*refs):
        src, land = refs[:n], refs[n:2 * n]
        send_sems, recv_sems = refs[2 * n], refs[2 * n + 1]
        token = refs[-1]
        x, y, c = _place()
        for w in range(n):
            for k in range(1, N_DEV):
                _exchange_copy(src[w], land[w], sliced[w], send_sems, recv_sems, w * _N_PEER + k - 1, x, y, c, k).start()
        token[...] = jnp.zeros_like(token)

    outs = pl.pallas_call(
        body, name="exchange_start_" + tag,
        out_shape=(pltpu.SemaphoreType.DMA((n * _N_PEER,)), pltpu.SemaphoreType.DMA((n * _N_PEER,)),
                   *[pltpu.HBM(a.shape, a.dtype) for a in arrays], *[pltpu.HBM(l.shape, l.dtype) for l in lands],
                   jax.ShapeDtypeStruct((8, 128), F32)),
        in_specs=[_HBM] * (2 * n),
        out_specs=(_SEM, _SEM, *([_HBM] * (2 * n)), pl.BlockSpec(memory_space=pltpu.VMEM)),
        input_output_aliases={i: 2 + i for i in range(2 * n)},
        compiler_params=pltpu.CompilerParams(has_side_effects=pltpu.SideEffectType.DATAFLOW_SIDE_EFFECTING),
    )(*[pltpu.with_memory_space_constraint(a, pltpu.HBM) for a in arrays],
      *[pltpu.with_memory_space_constraint(l, pltpu.HBM) for l in lands])
    return dict(send=outs[0], recv=outs[1], src=list(outs[2:2 + n]), land=list(outs[2 + n:2 + 2 * n]),
                sliced=list(sliced)), outs[-1]


def _exchange_wait(tag, groups, after):
    counts = [len(g["src"]) for g in groups]
    total = sum(counts)

    def body(*refs):
        pos = 0
        x, y, c = _place()
        for g, n in zip(groups, counts):
            src, land = refs[pos:pos + n], refs[pos + n:pos + 2 * n]
            send_sems, recv_sems = refs[pos + 2 * n], refs[pos + 2 * n + 1]
            pos += 2 * n + 2
            for w in range(n):
                for k in range(1, N_DEV):
                    cp = _exchange_copy(src[w], land[w], g["sliced"][w], send_sems, recv_sems,
                                        w * _N_PEER + k - 1, x, y, c, k)
                    cp.wait_send()
                    cp.wait_recv()

    operands, in_specs, aliases, out_shape = [], [], {}, []
    for g in groups:
        for a in g["src"] + g["land"]:
            aliases[len(operands)] = len(out_shape)
            out_shape.append(pltpu.HBM(a.shape, a.dtype))
            operands.append(a)
            in_specs.append(_HBM)
        operands += [g["send"], g["recv"]]
        in_specs += [_SEM, _SEM]
    operands.append(after)
    in_specs.append(pl.BlockSpec(memory_space=pl.ANY))
    outs = pl.pallas_call(
        body, name="exchange_wait_" + tag, out_shape=tuple(out_shape), in_specs=in_specs,
        out_specs=tuple([_HBM] * (2 * total)), input_output_aliases=aliases,
        compiler_params=pltpu.CompilerParams(has_side_effects=pltpu.SideEffectType.DATAFLOW_SIDE_EFFECTING),
    )(*operands)
    srcs, lands, pos = [], [], 0
    for n in counts:
        srcs += list(outs[pos:pos + n])
        lands += list(outs[pos + n:pos + 2 * n])
        pos += 2 * n
    return srcs, lands


def _sum_and_swap(tag, pieces, lands, small=None, small_land=None):
    n_w = len(pieces)
    n_small = 0 if small is None else 1

    def body(*refs):
        g_refs, land_refs = refs[:n_w], refs[n_w:2 * n_w]
        pos = 2 * n_w + 2 * n_small
        out_refs = refs[pos:pos + n_w]
        pos += n_w + n_small
        bufs = refs[pos:pos + n_w]
        load_sems, swap_send, swap_recv = refs[pos + n_w + 2 * n_small:]
        x, y, c = _place()
        me = 4 * x + 2 * y + c

        def slot(k):
            px, py, pc = _peer(x, y, c, k)
            return 4 * px + 2 * py + pc

        def swap_copy(w, half):
            rows = pieces[w].shape[1]
            win = out_refs[w].at[pl.ds(pl.multiple_of(half * rows, rows), rows), :]
            return pltpu.make_async_remote_copy(
                src_ref=win, dst_ref=win, send_sem=swap_send.at[w], recv_sem=swap_recv.at[w],
                device_id=(x, y, 1 - c), device_id_type=MESH)

        loads = []
        for w in range(n_w):
            per_w = [pltpu.make_async_copy(g_refs[w].at[me], bufs[w].at[me], load_sems.at[w * N_DEV])]
            per_w += [pltpu.make_async_copy(land_refs[w].at[slot(k)], bufs[w].at[slot(k)], load_sems.at[w * N_DEV + k])
                      for k in range(1, N_DEV)]
            loads.append(per_w)
        small_loads = []
        if n_small:
            small_ref, small_land_ref = refs[2 * n_w], refs[2 * n_w + 1]
            small_out = refs[2 * n_w + 2 + n_w]
            small_buf, small_sems = refs[pos + n_w], refs[pos + n_w + 1]
            small_loads = [pltpu.make_async_copy(small_land_ref.at[slot(k)], small_buf.at[slot(k)],
                                                 small_sems.at[k - 1]) for k in range(1, N_DEV)]
        for cp in [cp for per_w in loads for cp in per_w] + small_loads:
            cp.start()
        if n_small:
            small_buf[me] = small_ref[...]
        swaps = []
        for w in range(n_w):
            for cp in loads[w]:
                cp.wait()
            rows = pieces[w].shape[1]
            total = bufs[w][0].astype(F32)
            for p in range(1, N_DEV):
                total += bufs[w][p].astype(F32)
            out_refs[w][pl.ds(pl.multiple_of(c * rows, rows), rows), :] = total
            sw = swap_copy(w, c)
            sw.start()
            swaps.append(sw)
        if n_small:
            for cp in small_loads:
                cp.wait()
            total = small_buf[0]
            for p in range(1, N_DEV):
                total += small_buf[p]
            small_out[...] = total
        for w in range(n_w):
            swap_copy(w, 1 - c).wait_recv()
        for sw in swaps:
            sw.wait_send()

    vmem = pl.BlockSpec(memory_space=pltpu.VMEM)
    hbm = pl.BlockSpec(memory_space=pl.ANY)
    small_args = [small, small_land] if n_small else []
    small_shapes = [jax.ShapeDtypeStruct(small.shape, F32)] if n_small else []
    small_scratch = ([pltpu.VMEM((N_DEV,) + small.shape, F32), pltpu.SemaphoreType.DMA((_N_PEER,))]
                     if n_small else [])
    return pl.pallas_call(
        body, name="sum_and_swap_" + tag,
        in_specs=[hbm] * (2 * n_w) + [vmem, hbm] * n_small,
        out_specs=[vmem] * (n_w + n_small),
        out_shape=[jax.ShapeDtypeStruct((2 * p.shape[1], p.shape[2]), F32) for p in pieces] + small_shapes,
        scratch_shapes=[pltpu.VMEM(p.shape, BF16) for p in pieces] + small_scratch
        + [pltpu.SemaphoreType.DMA((n_w * N_DEV,)), pltpu.SemaphoreType.DMA((n_w,)),
           pltpu.SemaphoreType.DMA((n_w,))],
        compiler_params=pltpu.CompilerParams(vmem_limit_bytes=VMEM_LIMIT),
    )(*pieces, *lands, *small_args)


def _adamw_values(w, g_t, m, v):
    c1 = 1.0 - ADAM_B1 ** ADAM_STEP
    c2 = 1.0 - ADAM_B2 ** ADAM_STEP
    nm = ADAM_B1 * m + (1.0 - ADAM_B1) * g_t
    nv = ADAM_B2 * v + (1.0 - ADAM_B2) * (g_t * g_t)
    return -ADAM_LR * ((nm / c1) / (jnp.sqrt(nv / c2) + ADAM_EPS) + ADAM_WD * w), nm, nv


def _adamw_update(w_ref, g_ref, m_ref, v_ref, d_ref, nm_ref, nv_ref):
    d_ref[...], nm_ref[...], nv_ref[...] = _adamw_values(w_ref[...], g_ref[...], m_ref[...], v_ref[...])


def _adamw_small(packed, shard_index, names, weights, moments_m, moments_v):
    n = len(names)
    shapes = [weights[name].shape for name in names]
    flat = [a[name].reshape(-1, a[name].shape[-1]) for name in names for a in (weights, moments_m, moments_v)]

    def body(packed_ref, shard_ref, *refs):
        loss_row = _small_offset("loss")
        refs[-1][...] = packed_ref[loss_row:loss_row + 1, 0:1]
        for k, name in enumerate(names):
            w_ref, m_ref, v_ref = refs[3 * k:3 * k + 3]
            g_ref, d_ref, nm_ref, nv_ref = refs[3 * n + 4 * k:3 * n + 4 * k + 4]
            rows, cols = w_ref.shape
            first = _small_offset(name)
            if cols <= _LANES:
                blocks = [(slice(0, rows), packed_ref[first:first + rows, 0:cols])]
            else:
                per_row = cols // _LANES
                if cols < _SMALL_SHAPES[name][-1]:
                    first = first + shard_ref[0] * per_row
                blocks = [(slice(i, i + 1),
                           jnp.concatenate([packed_ref[pl.ds(first + i * per_row + j, 1), :] for j in range(per_row)],
                                           axis=1)) for i in range(rows)]
            for at, g_t in blocks:
                g_ref[at, :] = g_t
                d_ref[at, :], nm_ref[at, :], nv_ref[at, :] = _adamw_values(w_ref[at, :], g_t, m_ref[at, :],
                                                                           v_ref[at, :])

    vmem = pl.BlockSpec(memory_space=pltpu.VMEM)
    outs = pl.pallas_call(
        body, name="adamw_small",
        in_specs=[vmem, pl.BlockSpec(memory_space=pltpu.SMEM)] + [vmem] * (3 * n),
        out_shape=[jax.ShapeDtypeStruct(flat[3 * k].shape, F32) for k in range(n) for _ in range(4)]
        + [jax.ShapeDtypeStruct((1, 1), F32)],
    )(packed, shard_index.reshape(1).astype(jnp.int32), *flat)
    return [tuple(o.reshape(shapes[k]) for o in outs[4 * k:4 * k + 4]) for k in range(n)], outs[-1].reshape(())


def _adamw(label, w, g, m, v):
    shape = w.shape
    cols = shape[-1]
    rows = w.size // cols
    args = [a.reshape(rows, cols) for a in (w, g, m, v)]

    def body(w_ref, g_ref, m_ref, v_ref, g_out, d_ref, nm_ref, nv_ref):
        g_out[...] = g_ref[...]
        _adamw_update(w_ref, g_ref, m_ref, v_ref, d_ref, nm_ref, nv_ref)

    block_rows = 256 if rows % 256 == 0 and rows > 256 else rows
    spec = pl.BlockSpec((block_rows, cols), lambda i: (i, 0))
    outs = pl.pallas_call(
        body, name="adamw_" + label, grid=(rows // block_rows,),
        in_specs=[spec] * 4, out_specs=[spec] * 4,
        out_shape=[jax.ShapeDtypeStruct((rows, cols), F32)] * 4,
        compiler_params=_params(),
    )(*args)
    return [o.reshape(shape) for o in outs]


def _no_send(tag, arrays, sliced):
    return jnp.zeros((8, 128), F32)


def _local_step(x, tgt, w_in_a, later_weights, first_after, sgu_ln_g, sgu_ln_b, w_spatial, b_spatial,
                attn_sinks, rel_bias, post_ln_g, post_ln_b, send=_no_send, own_product=None):
    bsp_t = b_spatial.T
    g1, b1 = post_ln_g[0:1], post_ln_b[0:1]
    g2, b2 = post_ln_g[1:2], post_ln_b[1:2]
    onehot = _bucket_onehot()
    bias = _bias_expand(rel_bias.T, onehot)
    win = _window_tables()

    xt, u, vh, z, rv, y = _layer_a_fwd(x, w_in_a, own_product, sgu_ln_g, sgu_ln_b, w_spatial, bsp_t, first_after)
    w_out_a, w_kv, w_in_b, w_out_b = later_weights(y)
    xh1, rstd1, q, zb, kd, vd = _layer_b_proj(x, y, w_out_a, g1, b1, w_in_b, w_kv)
    o, probs, sink_probs, dr2, loss_vec, dg2, db2 = _layer_b_fwd(q, zb, kd, vd, bias, win, attn_sinks, xh1, g1, b1,
                                                                 w_out_b, g2, b2, tgt)
    dq, dzb, dkd, dvd, carry_k, carry_v, gw_out_b, dsink, dbias = _layer_b_bwd_attn(
        dr2, zb, o, q, kd, vd, probs, sink_probs, w_out_b)
    dr1, dg1, db1, gw_in_b, gw_kv = _layer_b_bwd_proj(xh1, rstd1, g1, b1, dr2, dq, dzb, dkd, dvd, carry_k, carry_v,
                                                      w_in_b, w_kv)
    gw_out_b = gw_out_b.reshape(N_DEV, -1, D_MODEL)
    gw_kv = gw_kv.reshape(N_DEV, -1, 2 * PAIR)
    after = send("b", [gw_out_b, gw_in_b, gw_kv], [True, True, True])
    dp, gw_out_a, dws, dbsp, dgs, dbs = _layer_a_bwd_mix(dr1, u, vh, z, y, rv, w_out_a, sgu_ln_g, sgu_ln_b,
                                                         w_spatial, bsp_t, after)
    gw_out_a = gw_out_a.reshape(N_DEV, -1, D_MODEL)
    small = _pack_small(dws, dbsp, dsink, dbias, onehot, (dg1, dg2), (db1, db2), dgs, dbs, loss_vec)
    after = send("a_out", [gw_out_a, small], [True, False])
    gw_in_a = _layer_a_bwd_win(xt, dp, after).reshape(N_DEV, D_MODEL // 2, -1)
    after = send("a_in", [gw_in_a], [True])
    after, updates = after if isinstance(after, tuple) else (after, ())
    grad_x, *updated = _layer_a_bwd_dx(dr1, dp, w_in_a, after, updates)

    pieces = [gw_in_a, gw_out_a, gw_kv, gw_in_b, gw_out_b]
    return grad_x, pieces, small, updated


def kernel(x, w_in_a, sgu_ln_g, sgu_ln_b, w_spatial, b_spatial, w_out_a, w_kv, w_in_b, attn_sinks, rel_bias, w_out_b, post_ln_g, post_ln_b, loss_target, m_w_in_a, m_sgu_ln_g, m_sgu_ln_b, m_w_spatial, m_b_spatial, m_w_out_a, m_w_kv, m_w_in_b, m_attn_sinks, m_rel_bias, m_w_out_b, m_post_ln_g, m_post_ln_b, v_w_in_a, v_sgu_ln_g, v_sgu_ln_b, v_w_spatial, v_b_spatial, v_w_out_a, v_w_kv, v_w_in_b, v_attn_sinks, v_rel_bias, v_w_out_b, v_post_ln_g, v_post_ln_b):
    weights = dict(w_in_a=w_in_a, sgu_ln_g=sgu_ln_g, sgu_ln_b=sgu_ln_b, w_spatial=w_spatial, b_spatial=b_spatial,
                   w_out_a=w_out_a, w_kv=w_kv, w_in_b=w_in_b, attn_sinks=attn_sinks, rel_bias=rel_bias,
                   w_out_b=w_out_b, post_ln_g=post_ln_g, post_ln_b=post_ln_b)
    moments_m = dict(w_in_a=m_w_in_a, sgu_ln_g=m_sgu_ln_g, sgu_ln_b=m_sgu_ln_b, w_spatial=m_w_spatial,
                     b_spatial=m_b_spatial, w_out_a=m_w_out_a, w_kv=m_w_kv, w_in_b=m_w_in_b,
                     attn_sinks=m_attn_sinks, rel_bias=m_rel_bias, w_out_b=m_w_out_b, post_ln_g=m_post_ln_g,
                     post_ln_b=m_post_ln_b)
    moments_v = dict(w_in_a=v_w_in_a, sgu_ln_g=v_sgu_ln_g, sgu_ln_b=v_sgu_ln_b, w_spatial=v_w_spatial,
                     b_spatial=v_b_spatial, w_out_a=v_w_out_a, w_kv=v_w_kv, w_in_b=v_w_in_b,
                     attn_sinks=v_attn_sinks, rel_bias=v_rel_bias, w_out_b=v_w_out_b, post_ln_g=v_post_ln_g,
                     post_ln_b=v_post_ln_b)
    order = ("w_in_a", "sgu_ln_g", "sgu_ln_b", "w_spatial", "b_spatial", "w_out_a", "w_kv", "w_in_b", "attn_sinks",
             "rel_bias", "w_out_b", "post_ln_g", "post_ln_b")

    shard_index = 2 * lax.axis_index("x") + lax.axis_index("y")
    ln_shard = jnp.concatenate([sgu_ln_g, sgu_ln_b], axis=0)
    shards = [w_in_a[0], w_out_a[0], w_kv, w_in_b[0], w_out_b[0]]
    col_sharded = [True, False, False, True, False]
    full_in_a, *later, ln_full, p_own = _gather_weights(shards, col_sharded, [True, False, False, False, False],
                                                        ln_shard, tokens=x[0])
    ln_full = jnp.transpose(ln_full, (1, 0, 2)).reshape(2, A_WIDTH)
    later_shapes = [s.shape for s in shards[1:]]
    fetch_group, fetch_token = _fetch_start(later, later_shapes, col_sharded[1:])

    def later_weights(y):
        return _fetch_wait(fetch_group, later_shapes, col_sharded[1:], y)

    groups, grads, deltas, new_m, new_v, scalars = {}, {}, {}, {}, {}, {}
    early = ("w_out_b", "w_in_b", "w_kv", "w_out_a")

    def two_dim(a):
        return a.reshape(-1, a.shape[-1])

    def send(tag, arrays, sliced):
        groups[tag], token = _exchange_start(tag, arrays, sliced)
        if tag != "a_in":
            return token
        srcs, lands = _exchange_wait("early", [groups["b"], groups["a_out"]], token)
        *reduced, packed_sum = _sum_and_swap("early", srcs[:4], lands[:4], srcs[4], lands[4])
        updates = [(two_dim(weights[n]), g, two_dim(moments_m[n]), two_dim(moments_v[n]))
                   for n, g in zip(early, reduced)]
        small_names = ("sgu_ln_g", "sgu_ln_b", "w_spatial", "b_spatial", "attn_sinks", "rel_bias", "post_ln_g",
                       "post_ln_b")
        small_updates, scalars["loss"] = _adamw_small(packed_sum, shard_index, small_names, weights, moments_m,
                                                      moments_v)
        for name, (g, d, nm, nv) in zip(small_names, small_updates):
            grads[name], deltas[name], new_m[name], new_v[name] = g, d, nm, nv
        return new_m["b_spatial"].reshape(A_GROUPS, CHUNK), updates

    grad_x, _, _, updated = _local_step(
        x[0], loss_target[0], full_in_a, later_weights, fetch_token, ln_full[0:1], ln_full[1:2], w_spatial[0],
        b_spatial[0], attn_sinks, rel_bias, post_ln_g, post_ln_b, send=send, own_product=(shard_index, p_own))
    for k, name in enumerate(early):
        grads[name], deltas[name], new_m[name], new_v[name] = [
            a.reshape(weights[name].shape) for a in updated[4 * k:4 * k + 4]]

    srcs, lands = _exchange_wait("late", [groups["a_in"]], grad_x)
    (g_in_a,) = _sum_and_swap("late", srcs, lands)
    grads["w_in_a"], deltas["w_in_a"], new_m["w_in_a"], new_v["w_in_a"] = _adamw(
        "w_in_a", w_in_a, g_in_a.reshape(w_in_a.shape), m_w_in_a, v_w_in_a)
    return (scalars["loss"], grad_x[None], *[grads[n] for n in order], *[deltas[n] for n in order],
            *[new_m[n] for n in order], *[new_v[n] for n in order])
```

```python
import functools
import math

import jax
import jax.numpy as jnp
from jax import lax
from jax.experimental import pallas as pl
from jax.experimental.pallas import tpu as pltpu

F32 = jnp.float32
BF16 = jnp.bfloat16

D_MODEL = 1024
A_WIDTH = 2048
A_GROUPS = 8
A_GROUP_DIM = 256
CHUNK = 128
N_HEADS = 16
N_KV = 2
HEAD_DIM = 64
PAIR = 2 * HEAD_DIM
B_WIDTH = 1024
REL_BUCKETS = 32
ALPHA = 4.0 ** 0.25
LN_EPS = 1e-5
NEG_INF = -1e30
SCALE = HEAD_DIM ** -0.5

ADAM_LR = 0.001
ADAM_B1 = 0.9
ADAM_B2 = 0.999
ADAM_EPS = 1e-08
ADAM_WD = 0.01
ADAM_STEP = 10

N_DEV = 8
N_CHIPS = 4
MESH = pl.DeviceIdType.MESH
VMEM_LIMIT = 56 * 1024 * 1024

TM_ATTN = 256
TM_BWD_ATTN = 512
TM_MM = 512
TM_WIN = 1024
_LANES = 128
_SUBLANES = 8


def _dot(a, b):
    return jnp.dot(a, b, preferred_element_type=F32)


def _dot_nt(a, b):
    return lax.dot_general(a, b, (((1,), (1,)), ((), ())), preferred_element_type=F32)


def _dot_tn(a, b):
    return lax.dot_general(a, b, (((0,), (0,)), ((), ())), preferred_element_type=F32)


def _ln_fwd(r):
    mu = jnp.mean(r, axis=-1, keepdims=True)
    rc = r - mu
    var = jnp.mean(rc * rc, axis=-1, keepdims=True)
    rstd = lax.rsqrt(var + LN_EPS)
    return rc * rstd, rstd


def _ln_bwd(dxh, xh, rstd):
    m1 = jnp.mean(dxh, axis=-1, keepdims=True)
    m2 = jnp.mean(dxh * xh, axis=-1, keepdims=True)
    return rstd * (dxh - m1 - xh * m2)


def _silu_parts(z):
    sg = jax.nn.sigmoid(z)
    return z * sg, sg * (1.0 + z * (1.0 - sg))


def _dup_halves(blk):
    sw = pltpu.roll(blk, HEAD_DIM, 1)
    lo = lax.broadcasted_iota(jnp.int32, blk.shape, 1) < HEAD_DIM
    return jnp.where(lo, blk, sw), jnp.where(lo, sw, blk)


def _fold_halves(blk):
    return blk + pltpu.roll(blk, HEAD_DIM, 1)


def _resident(shape):
    nd = len(shape)
    return pl.BlockSpec(shape, lambda *_: (0,) * nd, pipeline_mode=pl.Buffered(1))


def _const(shape):
    nd = len(shape)
    return pl.BlockSpec(shape, lambda *_: (0,) * nd)


def _rows(tm, cols):
    return pl.BlockSpec((tm, cols), lambda i: (i, 0))


def _params(sem=("arbitrary",)):
    return pltpu.CompilerParams(dimension_semantics=sem, vmem_limit_bytes=VMEM_LIMIT)


def _spatial_mix(ws_ref, bsp_ref, vn, s_scr, n_chunks):
    tri = (lax.broadcasted_iota(jnp.int32, (CHUNK, CHUNK), 0)
           >= lax.broadcasted_iota(jnp.int32, (CHUNK, CHUNK), 1))
    for g in range(A_GROUPS):
        wsg = jnp.where(tri, ws_ref[g], 0.0).astype(BF16)
        cols = slice(g * A_GROUP_DIM, (g + 1) * A_GROUP_DIM)
        for ci in range(n_chunks):
            rows = slice(ci * CHUNK, (ci + 1) * CHUNK)
            s_scr[rows, cols] = _dot(wsg, vn[rows, cols]) + bsp_ref[:, g:g + 1]


def _layer_a_fwd(x, w_in, p_own, lng, lnb, ws, bsp_t, after, own=None):
    t_len = x.shape[0]
    tm = TM_ATTN
    shard_cols = 3 * A_WIDTH // N_CHIPS

    def body(x_ref, win_ref, *rest):
        pown_ref = rest[0] if own is not None else None
        (lng_ref, lnb_ref, ws_ref, bsp_ref, after_ref,
         xt_ref, u_ref, vh_ref, z_ref, rv_ref, y_ref, s_scr) = rest[0 if own is None else 1:]
        x_t = x_ref[...]
        xb = x_t.astype(BF16)
        xt_ref[...] = x_t.T.astype(BF16)
        if own is None:
            u = _dot(xb, win_ref[:, 0:A_WIDTH])
            v = _dot(xb, win_ref[:, A_WIDTH:2 * A_WIDTH])
            z = _dot(xb, win_ref[:, 2 * A_WIDTH:3 * A_WIDTH])
        else:
            pieces = {s: pown_ref[...] if s == own else _dot(xb, win_ref[:, s * shard_cols:(s + 1) * shard_cols])
                      for s in (1, 2, 0, 3)}
            p = jnp.concatenate([pieces[s] for s in range(N_CHIPS)], axis=1)
            u, v, z = p[:, 0:A_WIDTH], p[:, A_WIDTH:2 * A_WIDTH], p[:, 2 * A_WIDTH:3 * A_WIDTH]
        vh, rv = _ln_fwd(v)
        vn = (vh * lng_ref[...] + lnb_ref[...]).astype(BF16)
        _spatial_mix(ws_ref, bsp_ref, vn, s_scr, tm // CHUNK)
        sz, _ = _silu_parts(z)
        y_ref[...] = (u * s_scr[...] * sz).astype(BF16)
        u_ref[...] = u.astype(BF16)
        vh_ref[...] = vh.astype(BF16)
        z_ref[...] = z.astype(BF16)
        rv_ref[...] = rv

    wide = jax.ShapeDtypeStruct((t_len, A_WIDTH), BF16)
    product = [] if own is None else [p_own]
    return pl.pallas_call(
        body, name="layer_a_fwd" if own is None else "layer_a_fwd_own%d" % own, grid=(t_len // tm,),
        in_specs=[_rows(tm, D_MODEL), _resident(w_in.shape)] + [_rows(tm, shard_cols)] * len(product)
        + [_const(lng.shape), _const(lnb.shape), _const(ws.shape), _const(bsp_t.shape), _const(after.shape)],
        out_specs=[pl.BlockSpec((D_MODEL, tm), lambda i: (0, i)), _rows(tm, A_WIDTH), _rows(tm, A_WIDTH),
                   _rows(tm, A_WIDTH), _rows(tm, 1), _rows(tm, A_WIDTH)],
        out_shape=[jax.ShapeDtypeStruct((D_MODEL, t_len), BF16), wide, wide, wide,
                   jax.ShapeDtypeStruct((t_len, 1), F32), wide],
        scratch_shapes=[pltpu.VMEM((tm, A_WIDTH), F32)],
        compiler_params=_params(),
    )(x, w_in, *product, lng, lnb, ws, bsp_t, after)


def _layer_b_proj(x, y, w_out_a, g1, b1, w_in, w_kv):
    t_len = x.shape[0]
    tm = 2 * TM_MM

    def body(x_ref, y_ref, wout_ref, g_ref, b_ref, win_ref, wkv_ref, xh_ref, r1_ref, q_ref, z_ref, kd_ref, vd_ref):
        halves = [slice(k * TM_MM, (k + 1) * TM_MM) for k in range(2)]
        projected = [_dot(y_ref[rows, :], wout_ref[...]) for rows in halves]
        for rows, out_a in zip(halves, projected):
            xh, r1 = _ln_fwd(ALPHA * x_ref[rows, :] + out_a)
            xh_ref[rows, :] = xh
            r1_ref[rows, :] = r1
            h1 = (xh * g_ref[...] + b_ref[...]).astype(BF16)
            q_ref[rows, :] = (_dot(h1, win_ref[:, 0:B_WIDTH]) * SCALE).astype(BF16)
            z_ref[rows, :] = _dot(h1, win_ref[:, B_WIDTH:2 * B_WIDTH]).astype(BF16)
            kv = _dot(h1, wkv_ref[...])
            k0, k1 = _dup_halves(kv[:, 0:PAIR])
            v0, v1 = _dup_halves(kv[:, PAIR:2 * PAIR])
            kd_ref[rows, 0:PAIR] = k0.astype(BF16)
            kd_ref[rows, PAIR:2 * PAIR] = k1.astype(BF16)
            vd_ref[rows, 0:PAIR] = v0.astype(BF16)
            vd_ref[rows, PAIR:2 * PAIR] = v1.astype(BF16)

    return pl.pallas_call(
        body, name="layer_b_proj", grid=(t_len // tm,),
        in_specs=[_rows(tm, D_MODEL), _rows(tm, A_WIDTH), _resident(w_out_a.shape), _const(g1.shape),
                  _const(b1.shape), _resident(w_in.shape), _resident(w_kv.shape)],
        out_specs=[_rows(tm, D_MODEL), _rows(tm, 1), _rows(tm, B_WIDTH), _rows(tm, B_WIDTH), _rows(tm, 2 * PAIR),
                   _rows(tm, 2 * PAIR)],
        out_shape=[jax.ShapeDtypeStruct((t_len, D_MODEL), F32), jax.ShapeDtypeStruct((t_len, 1), F32),
                   jax.ShapeDtypeStruct((t_len, B_WIDTH), BF16), jax.ShapeDtypeStruct((t_len, B_WIDTH), BF16),
                   jax.ShapeDtypeStruct((t_len, 2 * PAIR), BF16), jax.ShapeDtypeStruct((t_len, 2 * PAIR), BF16)],
        compiler_params=_params(),
    )(x, y, w_out_a, g1, b1, w_in, w_kv)


GROUP = N_HEADS // N_KV
GROUP_Q = GROUP * CHUNK


def _window_tables():
    j = jnp.arange(2 * CHUNK, dtype=jnp.int32)[:, None]
    t = jnp.arange(CHUNK, dtype=jnp.int32)[None, :]
    dist = t + CHUNK - j
    inside = (dist >= 0) & (dist < CHUNK)
    return jnp.stack([inside & (j >= CHUNK), inside]).astype(F32)


def _band(ref, chunk_index, kvh):
    prev0 = pl.multiple_of(jnp.maximum(chunk_index - 1, 0) * CHUNK, CHUNK)
    cur0 = pl.multiple_of(chunk_index * CHUNK, CHUNK)
    cols = slice(kvh * PAIR, (kvh + 1) * PAIR)
    return jnp.concatenate([ref[pl.ds(prev0, CHUNK), cols], ref[pl.ds(cur0, CHUNK), cols]], axis=0)


def _group_tables(bias_ref, win_ref, sink_ref, chunk_index, kvh):
    bias = jnp.concatenate([bias_ref[kvh * GROUP + j] for j in range(GROUP)], axis=1)
    win = win_ref[jnp.minimum(chunk_index, 1)]
    mask = jnp.concatenate([win] * GROUP, axis=1) > 0.5
    sink = jnp.concatenate([jnp.full((1, CHUNK), sink_ref[0, kvh * GROUP + j], F32) for j in range(GROUP)], axis=1)
    return bias, mask, sink


def _attn_probs(qs, kband, bias, mask, sink):
    logits = jnp.where(mask, _dot_nt(kband, qs) + bias, NEG_INF)
    m = jnp.maximum(jnp.max(logits, axis=0, keepdims=True), sink)
    e = jnp.exp(logits - m)
    es = jnp.exp(sink - m)
    inv = 1.0 / (jnp.sum(e, axis=0, keepdims=True) + es)
    return e * inv, es * inv


def _half_mask():
    return lax.broadcasted_iota(jnp.int32, (CHUNK, PAIR), 1) < HEAD_DIM


def _stack_heads(src_ref, rows, kvh, dst_scr, lo):
    for j in range(GROUP):
        h = kvh * GROUP + j
        blk = src_ref[rows, (h // 2) * PAIR:(h // 2 + 1) * PAIR].astype(F32)
        keep = lo if h % 2 == 0 else ~lo
        dst_scr[j * CHUNK:(j + 1) * CHUNK, :] = jnp.where(keep, blk, 0.0).astype(BF16)


def _probs_spec(tm):
    return pl.BlockSpec((tm // CHUNK, N_KV, 2 * CHUNK, GROUP_Q), lambda i: (i, 0, 0, 0))


def _sink_probs_spec(tiles=1):
    return pl.BlockSpec((tiles, 8, GROUP_Q), lambda i: (i, 0, 0))


def _unstack_pairs(stacked, pp, lo):
    return jnp.where(lo, stacked[(2 * pp) * CHUNK:(2 * pp + 1) * CHUNK], stacked[(2 * pp + 1) * CHUNK:(2 * pp + 2) * CHUNK])


def _layer_b_fwd(q, zb, kd, vd, bias, win, sinks, xh1, g1, b1, w_out, g2, b2, tgt):
    t_len = q.shape[0]
    tm = 2 * TM_ATTN

    def body(q_ref, z_ref, kd_ref, vd_ref, bias_ref, win_ref, sink_ref, xh_ref, g1_ref, b1_ref, wout_ref, g2_ref,
             b2_ref, tgt_ref, o_ref, p_ref, ps_ref, dr_ref, loss_ref, dg_ref, db_ref, o_scr, qs_scr):
        i = pl.program_id(0)

        @pl.when(i == 0)
        def _():
            loss_ref[...] = jnp.zeros_like(loss_ref)
            dg_ref[...] = jnp.zeros_like(dg_ref)
            db_ref[...] = jnp.zeros_like(db_ref)

        lo = _half_mask()
        ps_ref[...] = jnp.zeros_like(ps_ref)
        per_part = TM_ATTN // CHUNK
        for part in range(tm // TM_ATTN):
            part_rows = slice(part * TM_ATTN, (part + 1) * TM_ATTN)
            for cp in range(per_part):
                ci = part * per_part + cp
                cg = i * (tm // CHUNK) + ci
                rows = slice(ci * CHUNK, (ci + 1) * CHUNK)
                for kvh in range(N_KV):
                    kband = _band(kd_ref, cg, kvh)
                    vband = _band(vd_ref, cg, kvh)
                    bias_g, mask, sink = _group_tables(bias_ref, win_ref, sink_ref, cg, kvh)
                    _stack_heads(q_ref, rows, kvh, qs_scr, lo)
                    p, p_sink = _attn_probs(qs_scr[...], kband, bias_g, mask, sink)
                    p = p.astype(BF16)
                    p_ref[ci, kvh] = p
                    ps_ref[part, cp * N_KV + kvh:cp * N_KV + kvh + 1, :] = p_sink
                    o_stack = _dot_tn(p, vband)
                    for pp in range(GROUP // 2):
                        pair = kvh * (GROUP // 2) + pp
                        o_scr[rows, pair * PAIR:(pair + 1) * PAIR] = _unstack_pairs(o_stack, pp, lo)
            o = o_scr[part_rows, :]
            o_ref[part_rows, :] = o.astype(BF16)
            sz, _ = _silu_parts(z_ref[part_rows, :].astype(F32))
            y = (o * sz).astype(BF16)
            h1 = xh_ref[part_rows, :] * g1_ref[...] + b1_ref[...]
            r = ALPHA * h1 + _dot(y, wout_ref[...])
            xh2, rstd2 = _ln_fwd(r)
            diff = xh2 * g2_ref[...] + b2_ref[...] - tgt_ref[part_rows, :]
            loss_ref[...] += jnp.sum(diff * diff, axis=0, keepdims=True)
            dh2 = diff * (1.0 / D_MODEL)
            dg_ref[...] += jnp.sum(dh2 * xh2, axis=0, keepdims=True)
            db_ref[...] += jnp.sum(dh2, axis=0, keepdims=True)
            dr_ref[part_rows, :] = _ln_bwd(dh2 * g2_ref[...], xh2, rstd2)

    vec = jax.ShapeDtypeStruct((1, D_MODEL), F32)
    return pl.pallas_call(
        body, name="layer_b_fwd", grid=(t_len // tm,),
        in_specs=[_rows(tm, B_WIDTH), _rows(tm, B_WIDTH), _resident(kd.shape), _resident(vd.shape),
                  _resident(bias.shape), _resident(win.shape), pl.BlockSpec(memory_space=pltpu.SMEM),
                  _rows(tm, D_MODEL), _const(g1.shape), _const(b1.shape), _resident(w_out.shape), _const(g2.shape),
                  _const(b2.shape), _rows(tm, D_MODEL)],
        out_specs=[_rows(tm, B_WIDTH), _probs_spec(tm), _sink_probs_spec(tm // TM_ATTN), _rows(tm, D_MODEL)]
        + [_const((1, D_MODEL))] * 3,
        out_shape=[jax.ShapeDtypeStruct((t_len, B_WIDTH), BF16),
                   jax.ShapeDtypeStruct((t_len // CHUNK, N_KV, 2 * CHUNK, GROUP_Q), BF16),
                   jax.ShapeDtypeStruct((t_len // TM_ATTN, 8, GROUP_Q), F32),
                   jax.ShapeDtypeStruct((t_len, D_MODEL), F32), vec, vec, vec],
        scratch_shapes=[pltpu.VMEM((tm, B_WIDTH), F32), pltpu.VMEM((GROUP_Q, PAIR), BF16)],
        compiler_params=_params(),
    )(q, zb, kd, vd, bias, win, sinks, xh1, g1, b1, w_out, g2, b2, tgt)


def _layer_b_bwd_attn(dr2, zb, o, q, kd, vd, probs, sink_probs, w_out):
    t_len = q.shape[0]
    tm = TM_BWD_ATTN
    n_steps = t_len // tm
    n_chunks = tm // CHUNK
    per_part = TM_ATTN // CHUNK

    def body(dr_ref, z_ref, o_ref, q_ref, kd_ref, vd_ref, p_ref, ps_ref, wout_ref,
             dq_ref, dz_ref, dkd_ref, dvd_ref, ck_ref, cv_ref, gw_ref, dsink_ref, dbias_ref,
             do_scr, qs_scr, dos_scr, gw_acc):
        i = pl.program_id(0)

        @pl.when(i == 0)
        def _():
            gw_acc[...] = jnp.zeros_like(gw_acc)
            dsink_ref[...] = jnp.zeros_like(dsink_ref)
            dbias_ref[...] = jnp.zeros_like(dbias_ref)

        drb = dr_ref[...].astype(BF16)
        per_kvh = 2
        n_blocks = N_KV * per_kvh
        block_cols = B_WIDTH // n_blocks

        def through_gate(b):
            cols = slice(b * block_cols, (b + 1) * block_cols)
            dy = _dot_nt(drb, wout_ref[cols, :])
            sz, dsz = _silu_parts(z_ref[:, cols].astype(F32))
            o_t = o_ref[:, cols].astype(F32)
            dz_ref[:, cols] = (dy * o_t * dsz).astype(BF16)
            do_scr[:, cols] = (dy * sz).astype(BF16)
            return (o_t * sz).astype(BF16)

        def weight_gradient(b, gated):
            cols = slice(b * block_cols, (b + 1) * block_cols)
            gw_acc[cols, :] += _dot_tn(gated, drb)

        gated = {b: through_gate(b) for b in range(per_kvh)}

        lo = _half_mask()
        for kvh in range(N_KV):
            kcols = slice(kvh * PAIR, (kvh + 1) * PAIR)
            dk_bands, dv_bands = [], []
            for ci in range(n_chunks):
                unit = kvh * n_chunks + ci
                if ci < per_kvh and kvh + 1 < N_KV:
                    gated[(kvh + 1) * per_kvh + ci] = through_gate((kvh + 1) * per_kvh + ci)
                if unit in gated:
                    weight_gradient(unit, gated.pop(unit))
                cg = i * n_chunks + ci
                rows = slice(ci * CHUNK, (ci + 1) * CHUNK)
                kband = _band(kd_ref, cg, kvh)
                vband = _band(vd_ref, cg, kvh)
                _stack_heads(q_ref, rows, kvh, qs_scr, lo)
                _stack_heads(do_scr, rows, kvh, dos_scr, lo)
                qs = qs_scr[...]
                dos = dos_scr[...]
                pb = p_ref[ci, kvh]
                p = pb.astype(F32)
                sink_row = (ci % per_part) * N_KV + kvh
                p_sink = ps_ref[ci // per_part, sink_row:sink_row + 1, :]
                dp = _dot_nt(vband, dos)
                delta = jnp.sum(p * dp, axis=0, keepdims=True)
                dlog = p * (dp - delta)
                for j in range(GROUP):
                    dbias_ref[kvh * GROUP + j] += dlog[:, j * CHUNK:(j + 1) * CHUNK]
                dsink_ref[kvh:kvh + 1, :] += -(p_sink * delta)
                ds = dlog.astype(BF16)
                dq_stack = _dot_tn(ds, kband) * SCALE
                for pp in range(GROUP // 2):
                    pair = kvh * (GROUP // 2) + pp
                    dq_ref[rows, pair * PAIR:(pair + 1) * PAIR] = _unstack_pairs(dq_stack, pp, lo).astype(BF16)
                dk_bands.append(_dot(ds, qs))
                dv_bands.append(_dot(pb, dos))
            for bands, out_ref, carry_ref in ((dk_bands, dkd_ref, ck_ref), (dv_bands, dvd_ref, cv_ref)):
                carry_ref[0, :, kcols] = bands[0][0:CHUNK]
                for ci in range(n_chunks):
                    own = bands[ci][CHUNK:2 * CHUNK]
                    if ci + 1 < n_chunks:
                        own = own + bands[ci + 1][0:CHUNK]
                    out_ref[ci * CHUNK:(ci + 1) * CHUNK, kcols] = own

        @pl.when(i == n_steps - 1)
        def _():
            gw_ref[...] = gw_acc[...].astype(BF16)

    carry_spec = pl.BlockSpec((1, CHUNK, 2 * PAIR), lambda i: (i, 0, 0))
    carry_shape = jax.ShapeDtypeStruct((n_steps, CHUNK, 2 * PAIR), F32)
    bias_shape = (N_HEADS, 2 * CHUNK, CHUNK)
    return pl.pallas_call(
        body, name="layer_b_bwd_attn", grid=(n_steps,),
        in_specs=[_rows(tm, D_MODEL), _rows(tm, B_WIDTH), _rows(tm, B_WIDTH), _rows(tm, B_WIDTH),
                  _resident(kd.shape), _resident(vd.shape), _probs_spec(tm), _sink_probs_spec(tm // TM_ATTN),
                  _resident(w_out.shape)],
        out_specs=[_rows(tm, B_WIDTH), _rows(tm, B_WIDTH), _rows(tm, 2 * PAIR), _rows(tm, 2 * PAIR),
                   carry_spec, carry_spec, _const(w_out.shape), _const((N_KV, GROUP_Q)), _const(bias_shape)],
        out_shape=[jax.ShapeDtypeStruct((t_len, B_WIDTH), BF16), jax.ShapeDtypeStruct((t_len, B_WIDTH), BF16),
                   jax.ShapeDtypeStruct((t_len, 2 * PAIR), F32), jax.ShapeDtypeStruct((t_len, 2 * PAIR), F32),
                   carry_shape, carry_shape, jax.ShapeDtypeStruct(w_out.shape, BF16),
                   jax.ShapeDtypeStruct((N_KV, GROUP_Q), F32), jax.ShapeDtypeStruct(bias_shape, F32)],
        scratch_shapes=[pltpu.VMEM((tm, B_WIDTH), BF16), pltpu.VMEM((GROUP_Q, PAIR), BF16),
                        pltpu.VMEM((GROUP_Q, PAIR), BF16), pltpu.VMEM(w_out.shape, F32)],
        compiler_params=_params(),
    )(dr2, zb, o, q, kd, vd, probs, sink_probs, w_out)


def _layer_b_bwd_proj(xh1, rstd1, g1, b1, dr2, dq, dzb, dkd, dvd, carry_k, carry_v, w_in, w_kv):
    t_len = xh1.shape[0]
    tm = TM_MM
    n_steps = t_len // tm
    per_tile = tm // TM_BWD_ATTN
    n_carry = carry_k.shape[0]

    def body(xh_ref, rstd_ref, g_ref, b_ref, dr2_ref, dq_ref, dz_ref, dkd_ref, dvd_ref, *rest):
        carry_refs = rest[:2 * per_tile]
        win_ref, wkv_ref, dr1_ref, dg_ref, db_ref, gwin_ref, gwkv_ref, acc_in, acc_kv = rest[2 * per_tile:]
        i = pl.program_id(0)

        @pl.when(i == 0)
        def _():
            acc_in[...] = jnp.zeros_like(acc_in)
            acc_kv[...] = jnp.zeros_like(acc_kv)
            dg_ref[...] = jnp.zeros_like(dg_ref)
            db_ref[...] = jnp.zeros_like(db_ref)

        lo = lax.broadcasted_iota(jnp.int32, (tm, PAIR), 1) < HEAD_DIM

        def heads_gradient(tile_ref, refs):
            parts = []
            for a in range(per_tile):
                parts.append(tile_ref[a * TM_BWD_ATTN:(a + 1) * TM_BWD_ATTN - CHUNK, :])
                carry = refs[a][0]
                if a == per_tile - 1:
                    carry = jnp.where(i < n_steps - 1, carry, 0.0)
                parts.append(tile_ref[(a + 1) * TM_BWD_ATTN - CHUNK:(a + 1) * TM_BWD_ATTN, :] + carry)
            dup = jnp.concatenate(parts, axis=0)
            return jnp.where(lo, _fold_halves(dup[:, 0:PAIR]), _fold_halves(dup[:, PAIR:2 * PAIR]))

        xh = xh_ref[...]
        h1 = (xh * g_ref[...] + b_ref[...]).astype(BF16)
        dq_t = dq_ref[...]
        dz_t = dz_ref[...]
        dkv = jnp.concatenate([heads_gradient(dkd_ref, carry_refs[:per_tile]),
                               heads_gradient(dvd_ref, carry_refs[per_tile:])], axis=1).astype(BF16)
        dh1 = ALPHA * dr2_ref[...]
        dh1 += _dot_nt(dq_t, win_ref[:, 0:B_WIDTH])
        dh1 += _dot_nt(dz_t, win_ref[:, B_WIDTH:2 * B_WIDTH])
        dh1 += _dot_nt(dkv, wkv_ref[...])
        acc_in[:, 0:B_WIDTH] += _dot_tn(h1, dq_t)
        acc_in[:, B_WIDTH:2 * B_WIDTH] += _dot_tn(h1, dz_t)
        acc_kv[...] += _dot_tn(h1, dkv)
        dg_ref[...] += jnp.sum(dh1 * xh, axis=0, keepdims=True)
        db_ref[...] += jnp.sum(dh1, axis=0, keepdims=True)
        dr1_ref[...] = _ln_bwd(dh1 * g_ref[...], xh, rstd_ref[...])

        @pl.when(i == n_steps - 1)
        def _():
            half_rows = D_MODEL // 2
            shard_cols = 2 * B_WIDTH // N_CHIPS
            for s in range(N_CHIPS):
                for c in range(2):
                    gwin_ref[2 * s + c] = acc_in[c * half_rows:(c + 1) * half_rows,
                                                 s * shard_cols:(s + 1) * shard_cols].astype(BF16)
            gwkv_ref[...] = acc_kv[...].astype(BF16)

    vec = jax.ShapeDtypeStruct((1, D_MODEL), F32)
    gwin_shape = (N_DEV, D_MODEL // 2, 2 * B_WIDTH // N_CHIPS)

    def carry_spec(a):
        return pl.BlockSpec((1, CHUNK, 2 * PAIR), lambda i: (jnp.minimum(per_tile * i + a + 1, n_carry - 1), 0, 0))

    carry_specs = [carry_spec(a) for a in range(per_tile)]
    return pl.pallas_call(
        body, name="layer_b_bwd_proj", grid=(n_steps,),
        in_specs=[_rows(tm, D_MODEL), _rows(tm, 1), _const(g1.shape), _const(b1.shape), _rows(tm, D_MODEL),
                  _rows(tm, B_WIDTH), _rows(tm, B_WIDTH), _rows(tm, 2 * PAIR), _rows(tm, 2 * PAIR)]
        + carry_specs + carry_specs + [_resident(w_in.shape), _resident(w_kv.shape)],
        out_specs=[_rows(tm, D_MODEL), _const((1, D_MODEL)), _const((1, D_MODEL)), _const(gwin_shape),
                   _const(w_kv.shape)],
        out_shape=[jax.ShapeDtypeStruct((t_len, D_MODEL), F32), vec, vec,
                   jax.ShapeDtypeStruct(gwin_shape, BF16), jax.ShapeDtypeStruct(w_kv.shape, BF16)],
        scratch_shapes=[pltpu.VMEM(w_in.shape, F32), pltpu.VMEM(w_kv.shape, F32)],
        compiler_params=_params(),
    )(xh1, rstd1, g1, b1, dr2, dq, dzb, dkd, dvd, *([carry_k] * per_tile), *([carry_v] * per_tile), w_in, w_kv)


def _layer_a_bwd_mix(dr1, u, vh, z, y, rv, w_out, lng, lnb, ws, bsp_t, after):
    t_len = u.shape[0]
    tm = TM_ATTN
    n_steps = t_len // tm

    def body(dr_ref, u_ref, vh_ref, z_ref, y_ref, rv_ref, wout_ref, lng_ref, lnb_ref, ws_ref, bsp_ref, after_ref,
             dp_ref, gw_ref, dws_ref, dbsp_ref, dgs_ref, dbs_ref, dvn_scr, gw_acc):
        i = pl.program_id(0)

        @pl.when(i == 0)
        def _():
            gw_acc[...] = jnp.zeros_like(gw_acc)
            dws_ref[...] = jnp.zeros_like(dws_ref)
            dbsp_ref[...] = jnp.zeros_like(dbsp_ref)
            dgs_ref[...] = jnp.zeros_like(dgs_ref)
            dbs_ref[...] = jnp.zeros_like(dbs_ref)

        drb = dr_ref[...].astype(BF16)

        def group_cols(g):
            return slice(g * A_GROUP_DIM, (g + 1) * A_GROUP_DIM)

        tri = (lax.broadcasted_iota(jnp.int32, (CHUNK, CHUNK), 0)
               >= lax.broadcasted_iota(jnp.int32, (CHUNK, CHUNK), 1))
        lane = lax.broadcasted_iota(jnp.int32, (CHUNK, CHUNK), 1)
        ones = jnp.ones((CHUNK, A_GROUP_DIM), BF16)
        dbsp = jnp.zeros((CHUNK, CHUNK), F32)
        dy_next = _dot_nt(drb, wout_ref[group_cols(0), :])
        for g in range(A_GROUPS):
            wsg = jnp.where(tri, ws_ref[g], 0.0).astype(BF16)
            cols = group_cols(g)
            cols_z = slice(2 * A_WIDTH + g * A_GROUP_DIM, 2 * A_WIDTH + (g + 1) * A_GROUP_DIM)
            dy_g = dy_next
            if g + 1 < A_GROUPS:
                dy_next = _dot_nt(drb, wout_ref[group_cols(g + 1), :])
            gw_acc[cols, :] += _dot_tn(y_ref[:, cols], drb)
            both = jnp.zeros((CHUNK, 2 * CHUNK), F32)
            for ci in range(tm // CHUNK):
                rows = slice(ci * CHUNK, (ci + 1) * CHUNK)
                vn = (vh_ref[rows, cols].astype(F32) * lng_ref[:, cols] + lnb_ref[:, cols]).astype(BF16)
                s = _dot(wsg, vn) + bsp_ref[:, g:g + 1]
                sz, dsz = _silu_parts(z_ref[rows, cols].astype(F32))
                dy = dy_g[rows]
                t = dy * u_ref[rows, cols].astype(F32)
                dp_ref[rows, cols] = (dy * (s * sz)).astype(BF16)
                dp_ref[rows, cols_z] = (t * s * dsz).astype(BF16)
                ds_b = (t * sz).astype(BF16)
                both += _dot_nt(ds_b, jnp.concatenate([vn, ones], axis=0))
                dvn_scr[rows, cols] = _dot_tn(wsg, ds_b)
            dws_ref[g] += jnp.where(tri, both[:, 0:CHUNK], 0.0)
            dbsp = jnp.where(lane == g, both[:, CHUNK:2 * CHUNK], dbsp)
        dbsp_ref[...] += dbsp
        dvn = dvn_scr[...]
        vh_t = vh_ref[...].astype(F32)
        dgs_ref[...] += jnp.sum(dvn * vh_t, axis=0, keepdims=True)
        dbs_ref[...] += jnp.sum(dvn, axis=0, keepdims=True)
        dp_ref[:, A_WIDTH:2 * A_WIDTH] = _ln_bwd(dvn * lng_ref[...], vh_t, rv_ref[...]).astype(BF16)

        @pl.when(i == n_steps - 1)
        def _():
            gw_ref[...] = gw_acc[...].astype(BF16)

    wide = jax.ShapeDtypeStruct((1, A_WIDTH), F32)
    return pl.pallas_call(
        body, name="layer_a_bwd_mix", grid=(n_steps,),
        in_specs=[_rows(tm, D_MODEL), _rows(tm, A_WIDTH), _rows(tm, A_WIDTH), _rows(tm, A_WIDTH), _rows(tm, A_WIDTH),
                  _rows(tm, 1), _resident(w_out.shape), _const(lng.shape), _const(lnb.shape), _const(ws.shape),
                  _const(bsp_t.shape), _const(after.shape)],
        out_specs=[_rows(tm, 3 * A_WIDTH), _const(w_out.shape), _const(ws.shape), _const((CHUNK, CHUNK)),
                   _const((1, A_WIDTH)), _const((1, A_WIDTH))],
        out_shape=[jax.ShapeDtypeStruct((t_len, 3 * A_WIDTH), BF16), jax.ShapeDtypeStruct(w_out.shape, BF16),
                   jax.ShapeDtypeStruct(ws.shape, F32), jax.ShapeDtypeStruct((CHUNK, CHUNK), F32),
                   wide, wide],
        scratch_shapes=[pltpu.VMEM((tm, A_WIDTH), F32), pltpu.VMEM(w_out.shape, F32)],
        compiler_params=_params(),
    )(dr1, u, vh, z, y, rv, w_out, lng, lnb, ws, bsp_t, after)


def _layer_a_bwd_dx(dr1, dp, w_in, after, updates=()):
    t_len = dr1.shape[0]
    tm = TM_MM
    n_steps = t_len // tm
    n_upd = len(updates)

    def body(dr_ref, dp_ref, win_ref, after_ref, *refs):
        upd_in, dx_ref, upd_out = refs[:4 * n_upd], refs[4 * n_upd], refs[4 * n_upd + 1:]
        dx_ref[...] = ALPHA * dr_ref[...] + _dot_nt(dp_ref[...], win_ref[...])
        for k in range(n_upd):
            w_ref, g_ref, m_ref, v_ref = upd_in[4 * k:4 * k + 4]
            g_out, d_ref, nm_ref, nv_ref = upd_out[4 * k:4 * k + 4]
            g_out[...] = g_ref[...]
            _adamw_update(w_ref, g_ref, m_ref, v_ref, d_ref, nm_ref, nv_ref)

    upd_specs, upd_shapes, upd_args = [], [], []
    for w, g, m, v in updates:
        rows, cols = w.shape
        upd_specs.append(pl.BlockSpec((rows // n_steps, cols), lambda i: (i, 0)))
        upd_shapes.append(jax.ShapeDtypeStruct((rows, cols), F32))
        upd_args += [w, g, m, v]
    return pl.pallas_call(
        body, name="layer_a_bwd_dx", grid=(n_steps,),
        in_specs=[_rows(tm, D_MODEL), _rows(tm, 3 * A_WIDTH), _resident(w_in.shape), _const(after.shape)]
        + [s for s in upd_specs for _ in range(4)],
        out_specs=[_rows(tm, D_MODEL)] + [s for s in upd_specs for _ in range(4)],
        out_shape=[jax.ShapeDtypeStruct((t_len, D_MODEL), F32)] + [s for s in upd_shapes for _ in range(4)],
        compiler_params=_params(),
    )(dr1, dp, w_in, after, *upd_args)


def _layer_a_bwd_win(xt, dp, after):
    t_len = xt.shape[1]
    tm = TM_WIN
    n_steps = t_len // tm
    shard_cols = 3 * A_WIDTH // N_CHIPS
    half_rows = D_MODEL // 2

    def body(xt_ref, dp_ref, after_ref, gw_ref, acc):
        i = pl.program_id(1)

        @pl.when(i == 0)
        def _():
            acc[...] = jnp.zeros_like(acc)

        acc[...] += _dot(xt_ref[...], dp_ref[...])

        @pl.when(i == n_steps - 1)
        def _():
            for c in range(2):
                gw_ref[0, c] = acc[c * half_rows:(c + 1) * half_rows, :].astype(BF16)

    return pl.pallas_call(
        body, name="layer_a_bwd_win", grid=(N_CHIPS, n_steps),
        in_specs=[pl.BlockSpec((D_MODEL, tm), lambda j, i: (0, i)),
                  pl.BlockSpec((tm, shard_cols), lambda j, i: (i, j)), _const(after.shape)],
        out_specs=pl.BlockSpec((1, 2, half_rows, shard_cols), lambda j, i: (j, 0, 0, 0)),
        out_shape=jax.ShapeDtypeStruct((N_CHIPS, 2, half_rows, shard_cols), BF16),
        scratch_shapes=[pltpu.VMEM((D_MODEL, shard_cols), F32)],
        compiler_params=_params(("arbitrary", "arbitrary")),
    )(xt, dp, after)


def _bucket_onehot():
    dist = jnp.arange(CHUNK, dtype=jnp.int32)[None, :]
    max_exact = REL_BUCKETS // 2
    df = jnp.maximum(dist, 1).astype(F32)
    large = max_exact + (jnp.log(df / max_exact) / math.log(CHUNK / max_exact)
                         * (REL_BUCKETS - max_exact)).astype(jnp.int32)
    bucket = jnp.where(dist < max_exact, dist, jnp.minimum(large, REL_BUCKETS - 1))
    onehot = bucket == jnp.arange(REL_BUCKETS, dtype=jnp.int32)[:, None]
    return onehot.astype(F32)


def _bias_expand(rel_t, onehot):
    def body(rel_ref, oh_ref, out_ref):
        by_distance = jnp.dot(rel_ref[...], oh_ref[...], preferred_element_type=F32,
                              precision=lax.Precision.HIGHEST)
        for h in range(N_HEADS):
            rows = jnp.broadcast_to(by_distance[h:h + 1, :], (2 * CHUNK, CHUNK))
            out_ref[h] = pltpu.roll(rows, 0, 1, stride=1, stride_axis=0)

    return pl.pallas_call(
        body, name="bias_expand",
        out_shape=jax.ShapeDtypeStruct((N_HEADS, 2 * CHUNK, CHUNK), F32),
    )(rel_t, onehot)


def _bias_reduce(oh_ref, db_ref):
    sublane = lax.broadcasted_iota(jnp.int32, (_SUBLANES, CHUNK), 0)
    rows = []
    for h in range(N_HEADS):
        part = db_ref[h, 0:_SUBLANES, :]
        for a in range(1, 2 * CHUNK // _SUBLANES):
            tile = db_ref[h, a * _SUBLANES:(a + 1) * _SUBLANES, :]
            back = (-a * _SUBLANES) % CHUNK
            part += pltpu.roll(tile, back, 1) if back else tile
        total = jnp.where(sublane == 0, part, 0.0)
        for s in range(1, _SUBLANES):
            total += jnp.where(sublane == s, pltpu.roll(part, CHUNK - s, 1), 0.0)
        rows.append(jnp.sum(total, axis=0, keepdims=True))
    by_distance = jnp.concatenate(rows, axis=0)
    return lax.dot_general(oh_ref[...], by_distance, (((1,), (1,)), ((), ())),
                           preferred_element_type=F32, precision=lax.Precision.HIGHEST)


_SMALL_SHAPES = dict(w_spatial=(A_GROUPS, CHUNK, CHUNK), b_spatial=(A_GROUPS, CHUNK), attn_sinks=(1, N_HEADS),
                     rel_bias=(REL_BUCKETS, N_HEADS), post_ln_g=(2, D_MODEL), post_ln_b=(2, D_MODEL),
                     sgu_ln_g=(1, A_WIDTH), sgu_ln_b=(1, A_WIDTH), loss=(1, 1))
_SMALL_ORDER = tuple(_SMALL_SHAPES)


def _small_rows(name):
    shape = _SMALL_SHAPES[name]
    rows = math.prod(shape[:-1]) if shape[-1] < _LANES else math.prod(shape) // _LANES
    return -(-rows // _SUBLANES) * _SUBLANES


def _small_offset(name):
    return sum(_small_rows(n) for n in _SMALL_ORDER[:_SMALL_ORDER.index(name)])


def _pack_small(dws, dbsp, dsink, dbias, onehot, post_g, post_b, dgs, dbs, loss_vec):
    def body(dws_ref, dbsp_ref, dsink_ref, db_ref, oh_ref, g1_ref, g2_ref, b1_ref, b2_ref, dgs_ref, dbs_ref,
             loss_ref, out_ref):
        out_ref[...] = jnp.zeros_like(out_ref)

        def put_flat(name, refs):
            row = _small_offset(name)
            for ref in refs:
                for k in range(ref.shape[1] // _LANES):
                    out_ref[row:row + 1, :] = ref[:, k * _LANES:(k + 1) * _LANES]
                    row += 1

        row = _small_offset("w_spatial")
        for g in range(A_GROUPS):
            out_ref[row + g * CHUNK:row + (g + 1) * CHUNK, :] = dws_ref[g]
        row = _small_offset("b_spatial")
        out_ref[row:row + A_GROUPS, :] = dbsp_ref[...].T[0:A_GROUPS, :]
        lane = lax.broadcasted_iota(jnp.int32, (1, _LANES), 1)
        sinks = jnp.zeros((1, _LANES), F32)
        for h in range(N_HEADS):
            per_query = dsink_ref[h // GROUP:h // GROUP + 1, (h % GROUP) * CHUNK:(h % GROUP + 1) * CHUNK]
            sinks = jnp.where(lane == h, jnp.sum(per_query, axis=1, keepdims=True), sinks)
        row = _small_offset("attn_sinks")
        out_ref[row:row + 1, :] = sinks
        row = _small_offset("rel_bias")
        out_ref[row:row + REL_BUCKETS, 0:N_HEADS] = _bias_reduce(oh_ref, db_ref)
        put_flat("post_ln_g", [g1_ref, g2_ref])
        put_flat("post_ln_b", [b1_ref, b2_ref])
        put_flat("sgu_ln_g", [dgs_ref])
        put_flat("sgu_ln_b", [dbs_ref])
        row = _small_offset("loss")
        out_ref[row:row + 1, 0:1] = (0.5 / D_MODEL) * jnp.sum(loss_ref[...], axis=1, keepdims=True)

    total_rows = sum(_small_rows(n) for n in _SMALL_ORDER)
    return pl.pallas_call(
        body, name="pack_small",
        out_shape=jax.ShapeDtypeStruct((total_rows, _LANES), F32),
    )(dws, dbsp, dsink, dbias, onehot, *post_g, *post_b, dgs, dbs, loss_vec)


def _place():
    return lax.axis_index("x"), lax.axis_index("y"), lax.axis_index("c")


RELAY_PIECES = 4


def _shard_window(full_ref, shard_shape, col_sharded, s, half, piece=None):
    rows, cols = shard_shape
    if half is None:
        start, size = 0, rows
    elif piece is None:
        start, size = half * (rows // 2), rows // 2
    else:
        size = rows // 2 // RELAY_PIECES
        start = (half * RELAY_PIECES + piece) * size
    if col_sharded:
        return full_ref.at[pl.ds(start, size), pl.ds(s * cols, cols)]
    return full_ref.at[pl.ds(s * rows + start, size), :]


def _other_chips(x, y):
    return [(1 - x, y), (x, 1 - y), (1 - x, 1 - y)]


def _gather_weights(shards, col_sharded, fetch, ln_shard, tokens=None):
    n_w = len(shards)
    fetched = [w for w in range(n_w) if fetch[w]]
    full_shapes = []
    for w, cs in zip(shards, col_sharded):
        r, c = w.shape
        full_shapes.append((r, c * N_CHIPS) if cs else (r * N_CHIPS, c))

    n_tok = 0 if tokens is None else 1
    tok_tile = TM_MM
    n_tiles = 0 if tokens is None else tokens.shape[0] // tok_tile

    def body(*refs):
        refs = list(refs)
        in_refs = [refs.pop(0) for _ in range(n_w)]
        ln_ref = refs.pop(0)
        tok_ref = refs.pop(0) if n_tok else None
        full_refs = [refs.pop(0) for _ in range(n_w)]
        ln_full = refs.pop(0)
        prod_ref = refs.pop(0) if n_tok else None
        raw = [refs.pop(0) for _ in range(n_w)]
        stage = [refs.pop(0) for _ in range(n_w)]
        send_sems, recv_sems, load_sems, local_sems, ln_send, ln_recv = refs[:6]
        tok_buf, prod_buf, tok_sems, prod_sems = refs[6:] if n_tok else (None,) * 4
        x, y, c = _place()
        s_me = 2 * x + y
        chips = _other_chips(x, y)
        pieces = range(RELAY_PIECES)

        def shard_window(w, s, half, piece=None):
            return _shard_window(full_refs[w], shards[w].shape, col_sharded[w], s, half, piece)

        def piece_rows(w, half, piece):
            rows = shards[w].shape[0] // 2 // RELAY_PIECES
            return pl.ds(pl.multiple_of((half * RELAY_PIECES + piece) * rows, rows), rows)

        def ici_copy(w, k, sender_shard, piece):
            idx = (w * 3 + k) * RELAY_PIECES + piece
            return pltpu.make_async_remote_copy(
                src_ref=stage[w].at[piece_rows(w, c, piece), :], dst_ref=shard_window(w, sender_shard, c, piece),
                send_sem=send_sems.at[idx], recv_sem=recv_sems.at[idx],
                device_id=(*chips[k], c), device_id_type=MESH)

        def d2d_copy(w, k, half, piece):
            s_k = 2 * chips[k][0] + chips[k][1]
            win = shard_window(w, s_k, half, piece)
            idx = (3 * n_w + w * 3 + k) * RELAY_PIECES + piece
            return pltpu.make_async_remote_copy(
                src_ref=win, dst_ref=win, send_sem=send_sems.at[idx], recv_sem=recv_sems.at[idx],
                device_id=(x, y, 1 - c), device_id_type=MESH)

        def ln_copy(k, slot):
            return pltpu.make_async_remote_copy(
                src_ref=ln_ref, dst_ref=ln_full.at[slot], send_sem=ln_send.at[k], recv_sem=ln_recv.at[k],
                device_id=(*chips[k], c), device_id_type=MESH)

        loads = []

        def load(w, rows):
            window = (rows, slice(None)) if rows is not None else (slice(None), slice(None))
            cp = pltpu.make_async_copy(in_refs[w].at[window], raw[w].at[window], load_sems.at[len(loads)])
            cp.start()
            loads.append((cp, w, window))

        for half in (c, 1 - c):
            for w in fetched:
                for q in pieces:
                    load(w, piece_rows(w, half, q))
        for w in range(n_w):
            if not fetch[w]:
                load(w, None)

        def to_bf16(k):
            cp, w, window = loads[k]
            cp.wait()
            stage[w][window] = raw[w][window].astype(BF16)

        ln_full[s_me] = ln_ref[...]
        def shard_of(k):
            return 2 * chips[k][0] + chips[k][1]

        relay_from = jnp.where(c == 0, shard_of(0), shard_of(1))
        relay_to = (jnp.where(c == 0, x, 1 - x), jnp.where(c == 0, 1 - y, y), c)

        def relay_copy(w, sender_shard, piece):
            win = shard_window(w, sender_shard, c, piece)
            idx = (w * 3 + 2) * RELAY_PIECES + piece
            return pltpu.make_async_remote_copy(
                src_ref=win, dst_ref=win, send_sem=send_sems.at[idx], recv_sem=recv_sems.at[idx],
                device_id=relay_to, device_id_type=MESH)

        first = [ln_copy(k, s_me) for k in range(3)]
        for cp in first:
            cp.start()
        n_sent = 0
        for w in fetched:
            for q in pieces:
                to_bf16(n_sent)
                n_sent += 1
                for k in range(2):
                    cp = ici_copy(w, k, s_me, q)
                    cp.start()
                    first.append(cp)
        for k in range(n_sent, len(loads)):
            to_bf16(k)
        own = [pltpu.make_async_copy(stage[w], shard_window(w, s_me, None), local_sems.at[w]) for w in range(n_w)]
        for cp in own:
            cp.start()

        def tok_copy(t):
            return pltpu.make_async_copy(tok_ref.at[pl.ds(t * tok_tile, tok_tile), :], tok_buf.at[t % 2],
                                         tok_sems.at[t % 2])

        def prod_copy(t):
            return pltpu.make_async_copy(prod_buf.at[t % 2], prod_ref.at[pl.ds(t * tok_tile, tok_tile), :],
                                         prod_sems.at[t % 2])

        def product_tiles(tiles):
            for t in tiles:
                if t + 1 < n_tiles:
                    tok_copy(t + 1).start()
                tok_copy(t).wait()
                if t >= 2:
                    prod_copy(t - 2).wait()
                prod_buf[t % 2] = _dot(tok_buf[t % 2].astype(BF16), stage[0][...])
                prod_copy(t).start()

        if n_tiles:
            tok_copy(0).start()
        share = [4, 3, 3, 3, 3, 0, 0, 0]
        assert len(share) == 2 * RELAY_PIECES
        bounds = [sum(share[:k]) * n_tiles // sum(share) for k in range(len(share) + 1)]
        passed = []
        for w in fetched:
            for q in pieces:
                if w == fetched[0]:
                    product_tiles(range(bounds[q], bounds[q + 1]))
                for k in range(2):
                    ici_copy(w, k, shard_of(k), q).wait_recv()
                relay = relay_copy(w, relay_from, q)
                relay.start()
                passed.append(relay)
                for k in range(2):
                    fwd = d2d_copy(w, k, c, q)
                    fwd.start()
                    passed.append(fwd)
        for w in fetched:
            for q in pieces:
                if w == fetched[0]:
                    product_tiles(range(bounds[RELAY_PIECES + q], bounds[RELAY_PIECES + q + 1]))
                relay_copy(w, shard_of(2), q).wait_recv()
                fwd = d2d_copy(w, 2, c, q)
                fwd.start()
                passed.append(fwd)
        for w in fetched:
            for k in range(3):
                for q in pieces:
                    d2d_copy(w, k, 1 - c, q).wait_recv()
        for k in range(3):
            ln_copy(k, 2 * chips[k][0] + chips[k][1]).wait_recv()
        for cp in first + passed:
            cp.wait_send()
        for cp in own:
            cp.wait()
        for t in range(max(n_tiles - 2, 0), n_tiles):
            prod_copy(t).wait()

    vmem = pl.BlockSpec(memory_space=pltpu.VMEM)
    hbm = pl.BlockSpec(memory_space=pl.ANY)
    prod_cols = shards[0].shape[1]
    tok_args = [] if tokens is None else [tokens]
    tok_out = [] if tokens is None else [jax.ShapeDtypeStruct((tokens.shape[0], prod_cols), F32)]
    tok_scratch = [] if tokens is None else [
        pltpu.VMEM((2, tok_tile, tokens.shape[1]), F32), pltpu.VMEM((2, tok_tile, prod_cols), F32),
        pltpu.SemaphoreType.DMA((2,)), pltpu.SemaphoreType.DMA((2,))]
    return pl.pallas_call(
        body, name="gather_weights",
        in_specs=[hbm] * n_w + [vmem] + [hbm] * n_tok,
        out_specs=[hbm] * n_w + [vmem] + [hbm] * n_tok,
        out_shape=[jax.ShapeDtypeStruct(s, BF16) for s in full_shapes]
        + [jax.ShapeDtypeStruct((N_CHIPS,) + ln_shard.shape, F32)] + tok_out,
        scratch_shapes=[pltpu.VMEM(w.shape, F32) for w in shards] + [pltpu.VMEM(w.shape, BF16) for w in shards]
        + [pltpu.SemaphoreType.DMA((6 * RELAY_PIECES * n_w,)), pltpu.SemaphoreType.DMA((6 * RELAY_PIECES * n_w,)),
           pltpu.SemaphoreType.DMA((2 * RELAY_PIECES * len(fetched) + n_w - len(fetched),)),
           pltpu.SemaphoreType.DMA((n_w,)), pltpu.SemaphoreType.DMA((3,)), pltpu.SemaphoreType.DMA((3,))]
        + tok_scratch,
        compiler_params=pltpu.CompilerParams(vmem_limit_bytes=VMEM_LIMIT),
    )(*shards, ln_shard, *tok_args)


def _fetch_copy(full_ref, shard_shape, col_sharded, sender_shard, send_sems, recv_sems, idx, chip, c):
    win = _shard_window(full_ref, shard_shape, col_sharded, sender_shard, None)
    return pltpu.make_async_remote_copy(src_ref=win, dst_ref=win, send_sem=send_sems.at[idx],
                                        recv_sem=recv_sems.at[idx], device_id=(*chip, c), device_id_type=MESH)


def _fetch_start(fulls, shard_shapes, col_sharded):
    n = len(fulls)

    def body(*refs):
        full = refs[:n]
        send_sems, recv_sems = refs[n], refs[n + 1]
        token = refs[-1]
        x, y, c = _place()
        for w in range(n):
            for k, chip in enumerate(_other_chips(x, y)):
                _fetch_copy(full[w], shard_shapes[w], col_sharded[w], 2 * x + y, send_sems, recv_sems, w * 3 + k,
                            chip, c).start()
        token[...] = jnp.zeros_like(token)

    outs = pl.pallas_call(
        body, name="fetch_start",
        out_shape=(pltpu.SemaphoreType.DMA((3 * n,)), pltpu.SemaphoreType.DMA((3 * n,)),
                   *[pltpu.HBM(f.shape, f.dtype) for f in fulls], jax.ShapeDtypeStruct((8, 128), F32)),
        in_specs=[_HBM] * n,
        out_specs=(_SEM, _SEM, *([_HBM] * n), pl.BlockSpec(memory_space=pltpu.VMEM)),
        input_output_aliases={i: 2 + i for i in range(n)},
        compiler_params=pltpu.CompilerParams(has_side_effects=pltpu.SideEffectType.DATAFLOW_SIDE_EFFECTING),
    )(*[pltpu.with_memory_space_constraint(f, pltpu.HBM) for f in fulls])
    return dict(send=outs[0], recv=outs[1], full=list(outs[2:2 + n])), outs[-1]


def _fetch_wait(group, shard_shapes, col_sharded, after):
    n = len(group["full"])

    def body(*refs):
        full = refs[:n]
        send_sems, recv_sems = refs[n], refs[n + 1]
        x, y, c = _place()
        for w in range(n):
            for k, chip in enumerate(_other_chips(x, y)):
                _fetch_copy(full[w], shard_shapes[w], col_sharded[w], 2 * x + y, send_sems, recv_sems, w * 3 + k,
                            chip, c).wait_send()
                _fetch_copy(full[w], shard_shapes[w], col_sharded[w], 2 * chip[0] + chip[1], send_sems, recv_sems,
                            w * 3 + k, chip, c).wait_recv()

    outs = pl.pallas_call(
        body, name="fetch_wait", out_shape=tuple(pltpu.HBM(f.shape, f.dtype) for f in group["full"]),
        in_specs=[_HBM] * n + [_SEM, _SEM, pl.BlockSpec(memory_space=pl.ANY)],
        out_specs=tuple([_HBM] * n), input_output_aliases={i: i for i in range(n)},
        compiler_params=pltpu.CompilerParams(has_side_effects=pltpu.SideEffectType.DATAFLOW_SIDE_EFFECTING),
    )(*group["full"], group["send"], group["recv"], after)
    return list(outs)


_HBM = pl.BlockSpec(memory_space=pltpu.HBM)
_SEM = pl.BlockSpec(memory_space=pltpu.SEMAPHORE)
_N_PEER = N_DEV - 1


def _peer(x, y, c, k):
    return (x + (k >> 2)) % 2, (y + ((k >> 1) & 1)) % 2, (c + (k & 1)) % 2


def _exchange_copy(src_ref, land_ref, sliced, send_sems, recv_sems, idx, x, y, c, k):
    px, py, pc = _peer(x, y, c, k)
    src = src_ref.at[4 * px + 2 * py + pc] if sliced else src_ref
    return pltpu.make_async_remote_copy(
        src_ref=src, dst_ref=land_ref.at[4 * x + 2 * y + c],
        send_sem=send_sems.at[idx], recv_sem=recv_sems.at[idx], device_id=(px, py, pc), device_id_type=MESH)


def _exchange_start(tag, arrays, sliced):
    n = len(arrays)
    lands = [lax.empty(a.shape if s else (N_DEV,) + a.shape, a.dtype) for a, s in zip(arrays, sliced)]

    def body(*refs):
        src, land = refs[:n], refs[n:2 * n]
        send_sems, recv_sems = refs[2 * n], refs[2 * n + 1]
        token = refs[-1]
        x, y, c = _place()
        for w in range(n):
            for k in range(1, N_DEV):
                _exchange_copy(src[w], land[w], sliced[w], send_sems, recv_sems, w * _N_PEER + k - 1, x, y, c, k).start()
        token[...] = jnp.zeros_like(token)

    outs = pl.pallas_call(
        body, name="exchange_start_" + tag,
        out_shape=(pltpu.SemaphoreType.DMA((n * _N_PEER,)), pltpu.SemaphoreType.DMA((n * _N_PEER,)),
                   *[pltpu.HBM(a.shape, a.dtype) for a in arrays], *[pltpu.HBM(l.shape, l.dtype) for l in lands],
                   jax.ShapeDtypeStruct((8, 128), F32)),
        in_specs=[_HBM] * (2 * n),
        out_specs=(_SEM, _SEM, *([_HBM] * (2 * n)), pl.BlockSpec(memory_space=pltpu.VMEM)),
        input_output_aliases={i: 2 + i for i in range(2 * n)},
        compiler_params=pltpu.CompilerParams(has_side_effects=pltpu.SideEffectType.DATAFLOW_SIDE_EFFECTING),
    )(*[pltpu.with_memory_space_constraint(a, pltpu.HBM) for a in arrays],
      *[pltpu.with_memory_space_constraint(l, pltpu.HBM) for l in lands])
    return dict(send=outs[0], recv=outs[1], src=list(outs[2:2 + n]), land=list(outs[2 + n:2 + 2 * n]),
                sliced=list(sliced)), outs[-1]


def _exchange_wait(tag, groups, after):
    counts = [len(g["src"]) for g in groups]
    total = sum(counts)

    def body(*refs):
        pos = 0
        x, y, c = _place()
        for g, n in zip(groups, counts):
            src, land = refs[pos:pos + n], refs[pos + n:pos + 2 * n]
            send_sems, recv_sems = refs[pos + 2 * n], refs[pos + 2 * n + 1]
            pos += 2 * n + 2
            for w in range(n):
                for k in range(1, N_DEV):
                    cp = _exchange_copy(src[w], land[w], g["sliced"][w], send_sems, recv_sems,
                                        w * _N_PEER + k - 1, x, y, c, k)
                    cp.wait_send()
                    cp.wait_recv()

    operands, in_specs, aliases, out_shape = [], [], {}, []
    for g in groups:
        for a in g["src"] + g["land"]:
            aliases[len(operands)] = len(out_shape)
            out_shape.append(pltpu.HBM(a.shape, a.dtype))
            operands.append(a)
            in_specs.append(_HBM)
        operands += [g["send"], g["recv"]]
        in_specs += [_SEM, _SEM]
    operands.append(after)
    in_specs.append(pl.BlockSpec(memory_space=pl.ANY))
    outs = pl.pallas_call(
        body, name="exchange_wait_" + tag, out_shape=tuple(out_shape), in_specs=in_specs,
        out_specs=tuple([_HBM] * (2 * total)), input_output_aliases=aliases,
        compiler_params=pltpu.CompilerParams(has_side_effects=pltpu.SideEffectType.DATAFLOW_SIDE_EFFECTING),
    )(*operands)
    srcs, lands, pos = [], [], 0
    for n in counts:
        srcs += list(outs[pos:pos + n])
        lands += list(outs[pos + n:pos + 2 * n])
        pos += 2 * n
    return srcs, lands


def _sum_and_swap(tag, pieces, lands, small=None, small_land=None):
    n_w = len(pieces)
    n_small = 0 if small is None else 1

    def body(*refs):
        g_refs, land_refs = refs[:n_w], refs[n_w:2 * n_w]
        pos = 2 * n_w + 2 * n_small
        out_refs = refs[pos:pos + n_w]
        pos += n_w + n_small
        bufs = refs[pos:pos + n_w]
        load_sems, swap_send, swap_recv = refs[pos + n_w + 2 * n_small:]
        x, y, c = _place()
        me = 4 * x + 2 * y + c

        def slot(k):
            px, py, pc = _peer(x, y, c, k)
            return 4 * px + 2 * py + pc

        def swap_copy(w, half):
            rows = pieces[w].shape[1]
            win = out_refs[w].at[pl.ds(pl.multiple_of(half * rows, rows), rows), :]
            return pltpu.make_async_remote_copy(
                src_ref=win, dst_ref=win, send_sem=swap_send.at[w], recv_sem=swap_recv.at[w],
                device_id=(x, y, 1 - c), device_id_type=MESH)

        loads = []
        for w in range(n_w):
            per_w = [pltpu.make_async_copy(g_refs[w].at[me], bufs[w].at[me], load_sems.at[w * N_DEV])]
            per_w += [pltpu.make_async_copy(land_refs[w].at[slot(k)], bufs[w].at[slot(k)], load_sems.at[w * N_DEV + k])
                      for k in range(1, N_DEV)]
            loads.append(per_w)
        small_loads = []
        if n_small:
            small_ref, small_land_ref = refs[2 * n_w], refs[2 * n_w + 1]
            small_out = refs[2 * n_w + 2 + n_w]
            small_buf, small_sems = refs[pos + n_w], refs[pos + n_w + 1]
            small_loads = [pltpu.make_async_copy(small_land_ref.at[slot(k)], small_buf.at[slot(k)],
                                                 small_sems.at[k - 1]) for k in range(1, N_DEV)]
        for cp in [cp for per_w in loads for cp in per_w] + small_loads:
            cp.start()
        if n_small:
            small_buf[me] = small_ref[...]
        swaps = []
        for w in range(n_w):
            for cp in loads[w]:
                cp.wait()
            rows = pieces[w].shape[1]
            total = bufs[w][0].astype(F32)
            for p in range(1, N_DEV):
                total += bufs[w][p].astype(F32)
            out_refs[w][pl.ds(pl.multiple_of(c * rows, rows), rows), :] = total
            sw = swap_copy(w, c)
            sw.start()
            swaps.append(sw)
        if n_small:
            for cp in small_loads:
                cp.wait()
            total = small_buf[0]
            for p in range(1, N_DEV):
                total += small_buf[p]
            small_out[...] = total
        for w in range(n_w):
            swap_copy(w, 1 - c).wait_recv()
        for sw in swaps:
            sw.wait_send()

    vmem = pl.BlockSpec(memory_space=pltpu.VMEM)
    hbm = pl.BlockSpec(memory_space=pl.ANY)
    small_args = [small, small_land] if n_small else []
    small_shapes = [jax.ShapeDtypeStruct(small.shape, F32)] if n_small else []
    small_scratch = ([pltpu.VMEM((N_DEV,) + small.shape, F32), pltpu.SemaphoreType.DMA((_N_PEER,))]
                     if n_small else [])
    return pl.pallas_call(
        body, name="sum_and_swap_" + tag,
        in_specs=[hbm] * (2 * n_w) + [vmem, hbm] * n_small,
        out_specs=[vmem] * (n_w + n_small),
        out_shape=[jax.ShapeDtypeStruct((2 * p.shape[1], p.shape[2]), F32) for p in pieces] + small_shapes,
        scratch_shapes=[pltpu.VMEM(p.shape, BF16) for p in pieces] + small_scratch
        + [pltpu.SemaphoreType.DMA((n_w * N_DEV,)), pltpu.SemaphoreType.DMA((n_w,)),
           pltpu.SemaphoreType.DMA((n_w,))],
        compiler_params=pltpu.CompilerParams(vmem_limit_bytes=VMEM_LIMIT),
    )(*pieces, *lands, *small_args)


def _adamw_values(w, g_t, m, v):
    c1 = 1.0 - ADAM_B1 ** ADAM_STEP
    c2 = 1.0 - ADAM_B2 ** ADAM_STEP
    nm = ADAM_B1 * m + (1.0 - ADAM_B1) * g_t
    nv = ADAM_B2 * v + (1.0 - ADAM_B2) * (g_t * g_t)
    return -ADAM_LR * ((nm / c1) / (jnp.sqrt(nv / c2) + ADAM_EPS) + ADAM_WD * w), nm, nv


def _adamw_update(w_ref, g_ref, m_ref, v_ref, d_ref, nm_ref, nv_ref):
    d_ref[...], nm_ref[...], nv_ref[...] = _adamw_values(w_ref[...], g_ref[...], m_ref[...], v_ref[...])


def _adamw_small(packed, shard_index, names, weights, moments_m, moments_v):
    n = len(names)
    shapes = [weights[name].shape for name in names]
    flat = [a[name].reshape(-1, a[name].shape[-1]) for name in names for a in (weights, moments_m, moments_v)]

    def body(packed_ref, shard_ref, *refs):
        loss_row = _small_offset("loss")
        refs[-1][...] = packed_ref[loss_row:loss_row + 1, 0:1]
        for k, name in enumerate(names):
            w_ref, m_ref, v_ref = refs[3 * k:3 * k + 3]
            g_ref, d_ref, nm_ref, nv_ref = refs[3 * n + 4 * k:3 * n + 4 * k + 4]
            rows, cols = w_ref.shape
            first = _small_offset(name)
            if cols <= _LANES:
                blocks = [(slice(0, rows), packed_ref[first:first + rows, 0:cols])]
            else:
                per_row = cols // _LANES
                if cols < _SMALL_SHAPES[name][-1]:
                    first = first + shard_ref[0] * per_row
                blocks = [(slice(i, i + 1),
                           jnp.concatenate([packed_ref[pl.ds(first + i * per_row + j, 1), :] for j in range(per_row)],
                                           axis=1)) for i in range(rows)]
            for at, g_t in blocks:
                g_ref[at, :] = g_t
                d_ref[at, :], nm_ref[at, :], nv_ref[at, :] = _adamw_values(w_ref[at, :], g_t, m_ref[at, :],
                                                                           v_ref[at, :])

    vmem = pl.BlockSpec(memory_space=pltpu.VMEM)
    outs = pl.pallas_call(
        body, name="adamw_small",
        in_specs=[vmem, pl.BlockSpec(memory_space=pltpu.SMEM)] + [vmem] * (3 * n),
        out_shape=[jax.ShapeDtypeStruct(flat[3 * k].shape, F32) for k in range(n) for _ in range(4)]
        + [jax.ShapeDtypeStruct((1, 1), F32)],
    )(packed, shard_index.reshape(1).astype(jnp.int32), *flat)
    return [tuple(o.reshape(shapes[k]) for o in outs[4 * k:4 * k + 4]) for k in range(n)], outs[-1].reshape(())


def _adamw(label, w, g, m, v):
    shape = w.shape
    cols = shape[-1]
    rows = w.size // cols
    args = [a.reshape(rows, cols) for a in (w, g, m, v)]

    def body(w_ref, g_ref, m_ref, v_ref, g_out, d_ref, nm_ref, nv_ref):
        g_out[...] = g_ref[...]
        _adamw_update(w_ref, g_ref, m_ref, v_ref, d_ref, nm_ref, nv_ref)

    block_rows = 256 if rows % 256 == 0 and rows > 256 else rows
    spec = pl.BlockSpec((block_rows, cols), lambda i: (i, 0))
    outs = pl.pallas_call(
        body, name="adamw_" + label, grid=(rows // block_rows,),
        in_specs=[spec] * 4, out_specs=[spec] * 4,
        out_shape=[jax.ShapeDtypeStruct((rows, cols), F32)] * 4,
        compiler_params=_params(),
    )(*args)
    return [o.reshape(shape) for o in outs]


def _no_send(tag, arrays, sliced):
    return jnp.zeros((8, 128), F32)


def _local_step(x, tgt, w_in_a, later_weights, first_after, sgu_ln_g, sgu_ln_b, w_spatial, b_spatial,
                attn_sinks, rel_bias, post_ln_g, post_ln_b, send=_no_send, own_product=None):
    bsp_t = b_spatial.T
    g1, b1 = post_ln_g[0:1], post_ln_b[0:1]
    g2, b2 = post_ln_g[1:2], post_ln_b[1:2]
    onehot = _bucket_onehot()
    bias = _bias_expand(rel_bias.T, onehot)
    win = _window_tables()

    if own_product is None:
        xt, u, vh, z, rv, y = _layer_a_fwd(x, w_in_a, None, sgu_ln_g, sgu_ln_b, w_spatial, bsp_t, first_after)
    else:
        chip, p_own = own_product
        xt, u, vh, z, rv, y = lax.switch(
            chip, [functools.partial(_layer_a_fwd, own=own) for own in range(N_CHIPS)],
            x, w_in_a, p_own, sgu_ln_g, sgu_ln_b, w_spatial, bsp_t, first_after)
    w_out_a, w_kv, w_in_b, w_out_b = later_weights(y)
    xh1, rstd1, q, zb, kd, vd = _layer_b_proj(x, y, w_out_a, g1, b1, w_in_b, w_kv)
    o, probs, sink_probs, dr2, loss_vec, dg2, db2 = _layer_b_fwd(q, zb, kd, vd, bias, win, attn_sinks, xh1, g1, b1,
                                                                 w_out_b, g2, b2, tgt)
    dq, dzb, dkd, dvd, carry_k, carry_v, gw_out_b, dsink, dbias = _layer_b_bwd_attn(
        dr2, zb, o, q, kd, vd, probs, sink_probs, w_out_b)
    dr1, dg1, db1, gw_in_b, gw_kv = _layer_b_bwd_proj(xh1, rstd1, g1, b1, dr2, dq, dzb, dkd, dvd, carry_k, carry_v,
                                                      w_in_b, w_kv)
    gw_out_b = gw_out_b.reshape(N_DEV, -1, D_MODEL)
    gw_kv = gw_kv.reshape(N_DEV, -1, 2 * PAIR)
    after = send("b", [gw_out_b, gw_in_b, gw_kv], [True, True, True])
    dp, gw_out_a, dws, dbsp, dgs, dbs = _layer_a_bwd_mix(dr1, u, vh, z, y, rv, w_out_a, sgu_ln_g, sgu_ln_b,
                                                         w_spatial, bsp_t, after)
    gw_out_a = gw_out_a.reshape(N_DEV, -1, D_MODEL)
    small = _pack_small(dws, dbsp, dsink, dbias, onehot, (dg1, dg2), (db1, db2), dgs, dbs, loss_vec)
    after = send("a_out", [gw_out_a, small], [True, False])
    gw_in_a = _layer_a_bwd_win(xt, dp, after).reshape(N_DEV, D_MODEL // 2, -1)
    after = send("a_in", [gw_in_a], [True])
    after, updates = after if isinstance(after, tuple) else (after, ())
    grad_x, *updated = _layer_a_bwd_dx(dr1, dp, w_in_a, after, updates)

    pieces = [gw_in_a, gw_out_a, gw_kv, gw_in_b, gw_out_b]
    return grad_x, pieces, small, updated


def kernel(x, w_in_a, sgu_ln_g, sgu_ln_b, w_spatial, b_spatial, w_out_a, w_kv, w_in_b, attn_sinks, rel_bias, w_out_b, post_ln_g, post_ln_b, loss_target, m_w_in_a, m_sgu_ln_g, m_sgu_ln_b, m_w_spatial, m_b_spatial, m_w_out_a, m_w_kv, m_w_in_b, m_attn_sinks, m_rel_bias, m_w_out_b, m_post_ln_g, m_post_ln_b, v_w_in_a, v_sgu_ln_g, v_sgu_ln_b, v_w_spatial, v_b_spatial, v_w_out_a, v_w_kv, v_w_in_b, v_attn_sinks, v_rel_bias, v_w_out_b, v_post_ln_g, v_post_ln_b):
    weights = dict(w_in_a=w_in_a, sgu_ln_g=sgu_ln_g, sgu_ln_b=sgu_ln_b, w_spatial=w_spatial, b_spatial=b_spatial,
                   w_out_a=w_out_a, w_kv=w_kv, w_in_b=w_in_b, attn_sinks=attn_sinks, rel_bias=rel_bias,
                   w_out_b=w_out_b, post_ln_g=post_ln_g, post_ln_b=post_ln_b)
    moments_m = dict(w_in_a=m_w_in_a, sgu_ln_g=m_sgu_ln_g, sgu_ln_b=m_sgu_ln_b, w_spatial=m_w_spatial,
                     b_spatial=m_b_spatial, w_out_a=m_w_out_a, w_kv=m_w_kv, w_in_b=m_w_in_b,
                     attn_sinks=m_attn_sinks, rel_bias=m_rel_bias, w_out_b=m_w_out_b, post_ln_g=m_post_ln_g,
                     post_ln_b=m_post_ln_b)
    moments_v = dict(w_in_a=v_w_in_a, sgu_ln_g=v_sgu_ln_g, sgu_ln_b=v_sgu_ln_b, w_spatial=v_w_spatial,
                     b_spatial=v_b_spatial, w_out_a=v_w_out_a, w_kv=v_w_kv, w_in_b=v_w_in_b,
                     attn_sinks=v_attn_sinks, rel_bias=v_rel_bias, w_out_b=v_w_out_b, post_ln_g=v_post_ln_g,
                     post_ln_b=v_post_ln_b)
    order = ("w_in_a", "sgu_ln_g", "sgu_ln_b", "w_spatial", "b_spatial", "w_out_a", "w_kv", "w_in_b", "attn_sinks",
             "rel_bias", "w_out_b", "post_ln_g", "post_ln_b")

    shard_index = 2 * lax.axis_index("x") + lax.axis_index("y")
    ln_shard = jnp.concatenate([sgu_ln_g, sgu_ln_b], axis=0)
    shards = [w_in_a[0], w_out_a[0], w_kv, w_in_b[0], w_out_b[0]]
    col_sharded = [True, False, False, True, False]
    full_in_a, *later, ln_full, p_own = _gather_weights(shards, col_sharded, [True, False, False, False, False],
                                                        ln_shard, tokens=x[0])
    ln_full = jnp.transpose(ln_full, (1, 0, 2)).reshape(2, A_WIDTH)
    later_shapes = [s.shape for s in shards[1:]]
    fetch_group, fetch_token = _fetch_start(later, later_shapes, col_sharded[1:])

    def later_weights(y):
        return _fetch_wait(fetch_group, later_shapes, col_sharded[1:], y)

    groups, grads, deltas, new_m, new_v, scalars = {}, {}, {}, {}, {}, {}
    early = ("w_out_b", "w_in_b", "w_kv", "w_out_a")

    def two_dim(a):
        return a.reshape(-1, a.shape[-1])

    def send(tag, arrays, sliced):
        groups[tag], token = _exchange_start(tag, arrays, sliced)
        if tag != "a_in":
            return token
        srcs, lands = _exchange_wait("early", [groups["b"], groups["a_out"]], token)
        *reduced, packed_sum = _sum_and_swap("early", srcs[:4], lands[:4], srcs[4], lands[4])
        updates = [(two_dim(weights[n]), g, two_dim(moments_m[n]), two_dim(moments_v[n]))
                   for n, g in zip(early, reduced)]
        small_names = ("sgu_ln_g", "sgu_ln_b", "w_spatial", "b_spatial", "attn_sinks", "rel_bias", "post_ln_g",
                       "post_ln_b")
        small_updates, scalars["loss"] = _adamw_small(packed_sum, shard_index, small_names, weights, moments_m,
                                                      moments_v)
        for name, (g, d, nm, nv) in zip(small_names, small_updates):
            grads[name], deltas[name], new_m[name], new_v[name] = g, d, nm, nv
        return new_m["b_spatial"].reshape(A_GROUPS, CHUNK), updates

    grad_x, _, _, updated = _local_step(
        x[0], loss_target[0], full_in_a, later_weights, fetch_token, ln_full[0:1], ln_full[1:2], w_spatial[0],
        b_spatial[0], attn_sinks, rel_bias, post_ln_g, post_ln_b, send=send, own_product=(shard_index, p_own))
    for k, name in enumerate(early):
        grads[name], deltas[name], new_m[name], new_v[name] = [
            a.reshape(weights[name].shape) for a in updated[4 * k:4 * k + 4]]

    srcs, lands = _exchange_wait("late", [groups["a_in"]], grad_x)
    (g_in_a,) = _sum_and_swap("late", srcs, lands)
    grads["w_in_a"], deltas["w_in_a"], new_m["w_in_a"], new_v["w_in_a"] = _adamw(
        "w_in_a", w_in_a, g_in_a.reshape(w_in_a.shape), m_w_in_a, v_w_in_a)
    return (scalars["loss"], grad_x[None], *[grads[n] for n in order], *[deltas[n] for n in order],
            *[new_m[n] for n in order], *[new_v[n] for n in order])
```

```python
import functools
import math

import jax
import jax.numpy as jnp
from jax import lax
from jax.experimental import pallas as pl
from jax.experimental.pallas import tpu as pltpu

F32 = jnp.float32
BF16 = jnp.bfloat16

D_MODEL = 1024
A_WIDTH = 2048
A_GROUPS = 8
A_GROUP_DIM = 256
CHUNK = 128
N_HEADS = 16
N_KV = 2
HEAD_DIM = 64
PAIR = 2 * HEAD_DIM
B_WIDTH = 1024
REL_BUCKETS = 32
ALPHA = 4.0 ** 0.25
LN_EPS = 1e-5
NEG_INF = -1e30
SCALE = HEAD_DIM ** -0.5

ADAM_LR = 0.001
ADAM_B1 = 0.9
ADAM_B2 = 0.999
ADAM_EPS = 1e-08
ADAM_WD = 0.01
ADAM_STEP = 10

N_DEV = 8
N_CHIPS = 4
MESH = pl.DeviceIdType.MESH
VMEM_LIMIT = 56 * 1024 * 1024

TM_ATTN = 256
TM_BWD_ATTN = 512
TM_MM = 512
TM_WIN = 1024
_LANES = 128
_SUBLANES = 8


def _dot(a, b):
    return jnp.dot(a, b, preferred_element_type=F32)


def _dot_nt(a, b):
    return lax.dot_general(a, b, (((1,), (1,)), ((), ())), preferred_element_type=F32)


def _dot_tn(a, b):
    return lax.dot_general(a, b, (((0,), (0,)), ((), ())), preferred_element_type=F32)


def _ln_fwd(r):
    mu = jnp.mean(r, axis=-1, keepdims=True)
    rc = r - mu
    var = jnp.mean(rc * rc, axis=-1, keepdims=True)
    rstd = lax.rsqrt(var + LN_EPS)
    return rc * rstd, rstd


def _ln_bwd(dxh, xh, rstd):
    m1 = jnp.mean(dxh, axis=-1, keepdims=True)
    m2 = jnp.mean(dxh * xh, axis=-1, keepdims=True)
    return rstd * (dxh - m1 - xh * m2)


def _silu_parts(z):
    sg = jax.nn.sigmoid(z)
    return z * sg, sg * (1.0 + z * (1.0 - sg))


def _dup_halves(blk):
    sw = pltpu.roll(blk, HEAD_DIM, 1)
    lo = lax.broadcasted_iota(jnp.int32, blk.shape, 1) < HEAD_DIM
    return jnp.where(lo, blk, sw), jnp.where(lo, sw, blk)


def _fold_halves(blk):
    return blk + pltpu.roll(blk, HEAD_DIM, 1)


def _resident(shape):
    nd = len(shape)
    return pl.BlockSpec(shape, lambda *_: (0,) * nd, pipeline_mode=pl.Buffered(1))


def _const(shape):
    nd = len(shape)
    return pl.BlockSpec(shape, lambda *_: (0,) * nd)


def _rows(tm, cols):
    return pl.BlockSpec((tm, cols), lambda i: (i, 0))


def _params(sem=("arbitrary",)):
    return pltpu.CompilerParams(dimension_semantics=sem, vmem_limit_bytes=VMEM_LIMIT)


def _spatial_mix(ws_ref, bsp_ref, vn, s_scr, n_chunks):
    tri = (lax.broadcasted_iota(jnp.int32, (CHUNK, CHUNK), 0)
           >= lax.broadcasted_iota(jnp.int32, (CHUNK, CHUNK), 1))
    for g in range(A_GROUPS):
        wsg = jnp.where(tri, ws_ref[g], 0.0).astype(BF16)
        cols = slice(g * A_GROUP_DIM, (g + 1) * A_GROUP_DIM)
        for ci in range(n_chunks):
            rows = slice(ci * CHUNK, (ci + 1) * CHUNK)
            s_scr[rows, cols] = _dot(wsg, vn[rows, cols]) + bsp_ref[:, g:g + 1]


def _layer_a_fwd(x, w_in, p_own, lng, lnb, ws, bsp_t, after, own=None):
    t_len = x.shape[0]
    tm = TM_ATTN
    shard_cols = 3 * A_WIDTH // N_CHIPS

    def body(x_ref, win_ref, *rest):
        pown_ref = rest[0] if own is not None else None
        (lng_ref, lnb_ref, ws_ref, bsp_ref, after_ref,
         xt_ref, u_ref, vh_ref, z_ref, rv_ref, y_ref, s_scr) = rest[0 if own is None else 1:]
        x_t = x_ref[...]
        xb = x_t.astype(BF16)
        xt_ref[...] = x_t.T.astype(BF16)
        if own is None:
            u = _dot(xb, win_ref[:, 0:A_WIDTH])
            v = _dot(xb, win_ref[:, A_WIDTH:2 * A_WIDTH])
            z = _dot(xb, win_ref[:, 2 * A_WIDTH:3 * A_WIDTH])
        else:
            pieces = {s: pown_ref[...] if s == own else _dot(xb, win_ref[:, s * shard_cols:(s + 1) * shard_cols])
                      for s in (1, 2, 3, 0)}
            p = jnp.concatenate([pieces[s] for s in range(N_CHIPS)], axis=1)
            u, v, z = p[:, 0:A_WIDTH], p[:, A_WIDTH:2 * A_WIDTH], p[:, 2 * A_WIDTH:3 * A_WIDTH]
        vh, rv = _ln_fwd(v)
        vn = (vh * lng_ref[...] + lnb_ref[...]).astype(BF16)
        _spatial_mix(ws_ref, bsp_ref, vn, s_scr, tm // CHUNK)
        sz, _ = _silu_parts(z)
        y_ref[...] = (u * s_scr[...] * sz).astype(BF16)
        u_ref[...] = u.astype(BF16)
        vh_ref[...] = vh.astype(BF16)
        z_ref[...] = z.astype(BF16)
        rv_ref[...] = rv

    wide = jax.ShapeDtypeStruct((t_len, A_WIDTH), BF16)
    product = [] if own is None else [p_own]
    return pl.pallas_call(
        body, name="layer_a_fwd" if own is None else "layer_a_fwd_own%d" % own, grid=(t_len // tm,),
        in_specs=[_rows(tm, D_MODEL), _resident(w_in.shape)] + [_rows(tm, shard_cols)] * len(product)
        + [_const(lng.shape), _const(lnb.shape), _const(ws.shape), _const(bsp_t.shape), _const(after.shape)],
        out_specs=[pl.BlockSpec((D_MODEL, tm), lambda i: (0, i)), _rows(tm, A_WIDTH), _rows(tm, A_WIDTH),
                   _rows(tm, A_WIDTH), _rows(tm, 1), _rows(tm, A_WIDTH)],
        out_shape=[jax.ShapeDtypeStruct((D_MODEL, t_len), BF16), wide, wide, wide,
                   jax.ShapeDtypeStruct((t_len, 1), F32), wide],
        scratch_shapes=[pltpu.VMEM((tm, A_WIDTH), F32)],
        compiler_params=_params(),
    )(x, w_in, *product, lng, lnb, ws, bsp_t, after)


def _layer_b_proj(x, y, w_out_a, g1, b1, w_in, w_kv):
    t_len = x.shape[0]
    tm = 2 * TM_MM

    def body(x_ref, y_ref, wout_ref, g_ref, b_ref, win_ref, wkv_ref, xh_ref, r1_ref, q_ref, z_ref, kd_ref, vd_ref):
        halves = [slice(k * TM_MM, (k + 1) * TM_MM) for k in range(2)]
        projected = [_dot(y_ref[rows, :], wout_ref[...]) for rows in halves]
        for rows, out_a in zip(halves, projected):
            xh, r1 = _ln_fwd(ALPHA * x_ref[rows, :] + out_a)
            xh_ref[rows, :] = xh
            r1_ref[rows, :] = r1
            h1 = (xh * g_ref[...] + b_ref[...]).astype(BF16)
            q_ref[rows, :] = (_dot(h1, win_ref[:, 0:B_WIDTH]) * SCALE).astype(BF16)
            z_ref[rows, :] = _dot(h1, win_ref[:, B_WIDTH:2 * B_WIDTH]).astype(BF16)
            kv = _dot(h1, wkv_ref[...])
            k0, k1 = _dup_halves(kv[:, 0:PAIR])
            v0, v1 = _dup_halves(kv[:, PAIR:2 * PAIR])
            kd_ref[rows, 0:PAIR] = k0.astype(BF16)
            kd_ref[rows, PAIR:2 * PAIR] = k1.astype(BF16)
            vd_ref[rows, 0:PAIR] = v0.astype(BF16)
            vd_ref[rows, PAIR:2 * PAIR] = v1.astype(BF16)

    return pl.pallas_call(
        body, name="layer_b_proj", grid=(t_len // tm,),
        in_specs=[_rows(tm, D_MODEL), _rows(tm, A_WIDTH), _resident(w_out_a.shape), _const(g1.shape),
                  _const(b1.shape), _resident(w_in.shape), _resident(w_kv.shape)],
        out_specs=[_rows(tm, D_MODEL), _rows(tm, 1), _rows(tm, B_WIDTH), _rows(tm, B_WIDTH), _rows(tm, 2 * PAIR),
                   _rows(tm, 2 * PAIR)],
        out_shape=[jax.ShapeDtypeStruct((t_len, D_MODEL), F32), jax.ShapeDtypeStruct((t_len, 1), F32),
                   jax.ShapeDtypeStruct((t_len, B_WIDTH), BF16), jax.ShapeDtypeStruct((t_len, B_WIDTH), BF16),
                   jax.ShapeDtypeStruct((t_len, 2 * PAIR), BF16), jax.ShapeDtypeStruct((t_len, 2 * PAIR), BF16)],
        compiler_params=_params(),
    )(x, y, w_out_a, g1, b1, w_in, w_kv)


GROUP = N_HEADS // N_KV
GROUP_Q = GROUP * CHUNK


def _window_tables():
    j = jnp.arange(2 * CHUNK, dtype=jnp.int32)[:, None]
    t = jnp.arange(CHUNK, dtype=jnp.int32)[None, :]
    dist = t + CHUNK - j
    inside = (dist >= 0) & (dist < CHUNK)
    return jnp.stack([inside & (j >= CHUNK), inside]).astype(F32)


def _band(ref, chunk_index, kvh):
    prev0 = pl.multiple_of(jnp.maximum(chunk_index - 1, 0) * CHUNK, CHUNK)
    cur0 = pl.multiple_of(chunk_index * CHUNK, CHUNK)
    cols = slice(kvh * PAIR, (kvh + 1) * PAIR)
    return jnp.concatenate([ref[pl.ds(prev0, CHUNK), cols], ref[pl.ds(cur0, CHUNK), cols]], axis=0)


def _group_tables(bias_ref, win_ref, sink_ref, chunk_index, kvh):
    bias = jnp.concatenate([bias_ref[kvh * GROUP + j] for j in range(GROUP)], axis=1)
    win = win_ref[jnp.minimum(chunk_index, 1)]
    mask = jnp.concatenate([win] * GROUP, axis=1) > 0.5
    sink = jnp.concatenate([jnp.full((1, CHUNK), sink_ref[0, kvh * GROUP + j], F32) for j in range(GROUP)], axis=1)
    return bias, mask, sink


def _attn_probs(qs, kband, bias, mask, sink):
    logits = jnp.where(mask, _dot_nt(kband, qs) + bias, NEG_INF)
    m = jnp.maximum(jnp.max(logits, axis=0, keepdims=True), sink)
    e = jnp.exp(logits - m)
    es = jnp.exp(sink - m)
    inv = 1.0 / (jnp.sum(e, axis=0, keepdims=True) + es)
    return e * inv, es * inv


def _half_mask():
    return lax.broadcasted_iota(jnp.int32, (CHUNK, PAIR), 1) < HEAD_DIM


def _stack_heads(src_ref, rows, kvh, dst_scr, lo):
    for j in range(GROUP):
        h = kvh * GROUP + j
        blk = src_ref[rows, (h // 2) * PAIR:(h // 2 + 1) * PAIR].astype(F32)
        keep = lo if h % 2 == 0 else ~lo
        dst_scr[j * CHUNK:(j + 1) * CHUNK, :] = jnp.where(keep, blk, 0.0).astype(BF16)


def _probs_spec(tm):
    return pl.BlockSpec((tm // CHUNK, N_KV, 2 * CHUNK, GROUP_Q), lambda i: (i, 0, 0, 0))


def _sink_probs_spec(tiles=1):
    return pl.BlockSpec((tiles, 8, GROUP_Q), lambda i: (i, 0, 0))


def _unstack_pairs(stacked, pp, lo):
    return jnp.where(lo, stacked[(2 * pp) * CHUNK:(2 * pp + 1) * CHUNK], stacked[(2 * pp + 1) * CHUNK:(2 * pp + 2) * CHUNK])


def _layer_b_fwd(q, zb, kd, vd, bias, win, sinks, xh1, g1, b1, w_out, g2, b2, tgt):
    t_len = q.shape[0]
    tm = 2 * TM_ATTN

    def body(q_ref, z_ref, kd_ref, vd_ref, bias_ref, win_ref, sink_ref, xh_ref, g1_ref, b1_ref, wout_ref, g2_ref,
             b2_ref, tgt_ref, o_ref, p_ref, ps_ref, dr_ref, loss_ref, dg_ref, db_ref, o_scr, qs_scr):
        i = pl.program_id(0)

        @pl.when(i == 0)
        def _():
            loss_ref[...] = jnp.zeros_like(loss_ref)
            dg_ref[...] = jnp.zeros_like(dg_ref)
            db_ref[...] = jnp.zeros_like(db_ref)

        lo = _half_mask()
        ps_ref[...] = jnp.zeros_like(ps_ref)
        per_part = TM_ATTN // CHUNK
        for part in range(tm // TM_ATTN):
            part_rows = slice(part * TM_ATTN, (part + 1) * TM_ATTN)
            for cp in range(per_part):
                ci = part * per_part + cp
                cg = i * (tm // CHUNK) + ci
                rows = slice(ci * CHUNK, (ci + 1) * CHUNK)
                for kvh in range(N_KV):
                    kband = _band(kd_ref, cg, kvh)
                    vband = _band(vd_ref, cg, kvh)
                    bias_g, mask, sink = _group_tables(bias_ref, win_ref, sink_ref, cg, kvh)
                    _stack_heads(q_ref, rows, kvh, qs_scr, lo)
                    p, p_sink = _attn_probs(qs_scr[...], kband, bias_g, mask, sink)
                    p = p.astype(BF16)
                    p_ref[ci, kvh] = p
                    ps_ref[part, cp * N_KV + kvh:cp * N_KV + kvh + 1, :] = p_sink
                    o_stack = _dot_tn(p, vband)
                    for pp in range(GROUP // 2):
                        pair = kvh * (GROUP // 2) + pp
                        o_scr[rows, pair * PAIR:(pair + 1) * PAIR] = _unstack_pairs(o_stack, pp, lo)
            o = o_scr[part_rows, :]
            o_ref[part_rows, :] = o.astype(BF16)
            sz, _ = _silu_parts(z_ref[part_rows, :].astype(F32))
            y = (o * sz).astype(BF16)
            h1 = xh_ref[part_rows, :] * g1_ref[...] + b1_ref[...]
            r = ALPHA * h1 + _dot(y, wout_ref[...])
            xh2, rstd2 = _ln_fwd(r)
            diff = xh2 * g2_ref[...] + b2_ref[...] - tgt_ref[part_rows, :]
            loss_ref[...] += jnp.sum(diff * diff, axis=0, keepdims=True)
            dh2 = diff * (1.0 / D_MODEL)
            dg_ref[...] += jnp.sum(dh2 * xh2, axis=0, keepdims=True)
            db_ref[...] += jnp.sum(dh2, axis=0, keepdims=True)
            dr_ref[part_rows, :] = _ln_bwd(dh2 * g2_ref[...], xh2, rstd2)

    vec = jax.ShapeDtypeStruct((1, D_MODEL), F32)
    return pl.pallas_call(
        body, name="layer_b_fwd", grid=(t_len // tm,),
        in_specs=[_rows(tm, B_WIDTH), _rows(tm, B_WIDTH), _resident(kd.shape), _resident(vd.shape),
                  _resident(bias.shape), _resident(win.shape), pl.BlockSpec(memory_space=pltpu.SMEM),
                  _rows(tm, D_MODEL), _const(g1.shape), _const(b1.shape), _resident(w_out.shape), _const(g2.shape),
                  _const(b2.shape), _rows(tm, D_MODEL)],
        out_specs=[_rows(tm, B_WIDTH), _probs_spec(tm), _sink_probs_spec(tm // TM_ATTN), _rows(tm, D_MODEL)]
        + [_const((1, D_MODEL))] * 3,
        out_shape=[jax.ShapeDtypeStruct((t_len, B_WIDTH), BF16),
                   jax.ShapeDtypeStruct((t_len // CHUNK, N_KV, 2 * CHUNK, GROUP_Q), BF16),
                   jax.ShapeDtypeStruct((t_len // TM_ATTN, 8, GROUP_Q), F32),
                   jax.ShapeDtypeStruct((t_len, D_MODEL), F32), vec, vec, vec],
        scratch_shapes=[pltpu.VMEM((tm, B_WIDTH), F32), pltpu.VMEM((GROUP_Q, PAIR), BF16)],
        compiler_params=_params(),
    )(q, zb, kd, vd, bias, win, sinks, xh1, g1, b1, w_out, g2, b2, tgt)


def _layer_b_bwd_attn(dr2, zb, o, q, kd, vd, probs, sink_probs, w_out):
    t_len = q.shape[0]
    tm = TM_BWD_ATTN
    n_steps = t_len // tm
    n_chunks = tm // CHUNK
    per_part = TM_ATTN // CHUNK

    def body(dr_ref, z_ref, o_ref, q_ref, kd_ref, vd_ref, p_ref, ps_ref, wout_ref,
             dq_ref, dz_ref, dkd_ref, dvd_ref, ck_ref, cv_ref, gw_ref, dsink_ref, dbias_ref,
             do_scr, qs_scr, dos_scr, gw_acc):
        i = pl.program_id(0)

        @pl.when(i == 0)
        def _():
            gw_acc[...] = jnp.zeros_like(gw_acc)
            dsink_ref[...] = jnp.zeros_like(dsink_ref)
            dbias_ref[...] = jnp.zeros_like(dbias_ref)

        drb = dr_ref[...].astype(BF16)
        per_kvh = 2
        n_blocks = N_KV * per_kvh
        block_cols = B_WIDTH // n_blocks

        def through_gate(b):
            cols = slice(b * block_cols, (b + 1) * block_cols)
            dy = _dot_nt(drb, wout_ref[cols, :])
            sz, dsz = _silu_parts(z_ref[:, cols].astype(F32))
            o_t = o_ref[:, cols].astype(F32)
            dz_ref[:, cols] = (dy * o_t * dsz).astype(BF16)
            do_scr[:, cols] = (dy * sz).astype(BF16)
            return (o_t * sz).astype(BF16)

        def weight_gradient(b, gated):
            cols = slice(b * block_cols, (b + 1) * block_cols)
            gw_acc[cols, :] += _dot_tn(gated, drb)

        gated = {b: through_gate(b) for b in range(per_kvh)}

        lo = _half_mask()
        for kvh in range(N_KV):
            kcols = slice(kvh * PAIR, (kvh + 1) * PAIR)
            dk_bands, dv_bands = [], []
            for ci in range(n_chunks):
                unit = kvh * n_chunks + ci
                if ci < per_kvh and kvh + 1 < N_KV:
                    gated[(kvh + 1) * per_kvh + ci] = through_gate((kvh + 1) * per_kvh + ci)
                if unit in gated:
                    weight_gradient(unit, gated.pop(unit))
                cg = i * n_chunks + ci
                rows = slice(ci * CHUNK, (ci + 1) * CHUNK)
                kband = _band(kd_ref, cg, kvh)
                vband = _band(vd_ref, cg, kvh)
                _stack_heads(q_ref, rows, kvh, qs_scr, lo)
                _stack_heads(do_scr, rows, kvh, dos_scr, lo)
                qs = qs_scr[...]
                dos = dos_scr[...]
                pb = p_ref[ci, kvh]
                p = pb.astype(F32)
                sink_row = (ci % per_part) * N_KV + kvh
                p_sink = ps_ref[ci // per_part, sink_row:sink_row + 1, :]
                dp = _dot_nt(vband, dos)
                delta = jnp.sum(p * dp, axis=0, keepdims=True)
                dlog = p * (dp - delta)
                for j in range(GROUP):
                    dbias_ref[kvh * GROUP + j] += dlog[:, j * CHUNK:(j + 1) * CHUNK]
                dsink_ref[kvh:kvh + 1, :] += -(p_sink * delta)
                ds = dlog.astype(BF16)
                dq_stack = _dot_tn(ds, kband) * SCALE
                for pp in range(GROUP // 2):
                    pair = kvh * (GROUP // 2) + pp
                    dq_ref[rows, pair * PAIR:(pair + 1) * PAIR] = _unstack_pairs(dq_stack, pp, lo).astype(BF16)
                dk_bands.append(_dot(ds, qs))
                dv_bands.append(_dot(pb, dos))
            for bands, out_ref, carry_ref in ((dk_bands, dkd_ref, ck_ref), (dv_bands, dvd_ref, cv_ref)):
                carry_ref[0, :, kcols] = bands[0][0:CHUNK]
                for ci in range(n_chunks):
                    own = bands[ci][CHUNK:2 * CHUNK]
                    if ci + 1 < n_chunks:
                        own = own + bands[ci + 1][0:CHUNK]
                    out_ref[ci * CHUNK:(ci + 1) * CHUNK, kcols] = own

        @pl.when(i == n_steps - 1)
        def _():
            gw_ref[...] = gw_acc[...].astype(BF16)

    carry_spec = pl.BlockSpec((1, CHUNK, 2 * PAIR), lambda i: (i, 0, 0))
    carry_shape = jax.ShapeDtypeStruct((n_steps, CHUNK, 2 * PAIR), F32)
    bias_shape = (N_HEADS, 2 * CHUNK, CHUNK)
    return pl.pallas_call(
        body, name="layer_b_bwd_attn", grid=(n_steps,),
        in_specs=[_rows(tm, D_MODEL), _rows(tm, B_WIDTH), _rows(tm, B_WIDTH), _rows(tm, B_WIDTH),
                  _resident(kd.shape), _resident(vd.shape), _probs_spec(tm), _sink_probs_spec(tm // TM_ATTN),
                  _resident(w_out.shape)],
        out_specs=[_rows(tm, B_WIDTH), _rows(tm, B_WIDTH), _rows(tm, 2 * PAIR), _rows(tm, 2 * PAIR),
                   carry_spec, carry_spec, _const(w_out.shape), _const((N_KV, GROUP_Q)), _const(bias_shape)],
        out_shape=[jax.ShapeDtypeStruct((t_len, B_WIDTH), BF16), jax.ShapeDtypeStruct((t_len, B_WIDTH), BF16),
                   jax.ShapeDtypeStruct((t_len, 2 * PAIR), F32), jax.ShapeDtypeStruct((t_len, 2 * PAIR), F32),
                   carry_shape, carry_shape, jax.ShapeDtypeStruct(w_out.shape, BF16),
                   jax.ShapeDtypeStruct((N_KV, GROUP_Q), F32), jax.ShapeDtypeStruct(bias_shape, F32)],
        scratch_shapes=[pltpu.VMEM((tm, B_WIDTH), BF16), pltpu.VMEM((GROUP_Q, PAIR), BF16),
                        pltpu.VMEM((GROUP_Q, PAIR), BF16), pltpu.VMEM(w_out.shape, F32)],
        compiler_params=_params(),
    )(dr2, zb, o, q, kd, vd, probs, sink_probs, w_out)


def _layer_b_bwd_proj(xh1, rstd1, g1, b1, dr2, dq, dzb, dkd, dvd, carry_k, carry_v, w_in, w_kv):
    t_len = xh1.shape[0]
    tm = TM_MM
    n_steps = t_len // tm
    per_tile = tm // TM_BWD_ATTN
    n_carry = carry_k.shape[0]

    def body(xh_ref, rstd_ref, g_ref, b_ref, dr2_ref, dq_ref, dz_ref, dkd_ref, dvd_ref, *rest):
        carry_refs = rest[:2 * per_tile]
        win_ref, wkv_ref, dr1_ref, dg_ref, db_ref, gwin_ref, gwkv_ref, acc_in, acc_kv = rest[2 * per_tile:]
        i = pl.program_id(0)

        @pl.when(i == 0)
        def _():
            acc_in[...] = jnp.zeros_like(acc_in)
            acc_kv[...] = jnp.zeros_like(acc_kv)
            dg_ref[...] = jnp.zeros_like(dg_ref)
            db_ref[...] = jnp.zeros_like(db_ref)

        lo = lax.broadcasted_iota(jnp.int32, (tm, PAIR), 1) < HEAD_DIM

        def heads_gradient(tile_ref, refs):
            parts = []
            for a in range(per_tile):
                parts.append(tile_ref[a * TM_BWD_ATTN:(a + 1) * TM_BWD_ATTN - CHUNK, :])
                carry = refs[a][0]
                if a == per_tile - 1:
                    carry = jnp.where(i < n_steps - 1, carry, 0.0)
                parts.append(tile_ref[(a + 1) * TM_BWD_ATTN - CHUNK:(a + 1) * TM_BWD_ATTN, :] + carry)
            dup = jnp.concatenate(parts, axis=0)
            return jnp.where(lo, _fold_halves(dup[:, 0:PAIR]), _fold_halves(dup[:, PAIR:2 * PAIR]))

        xh = xh_ref[...]
        h1 = (xh * g_ref[...] + b_ref[...]).astype(BF16)
        dq_t = dq_ref[...]
        dz_t = dz_ref[...]
        dkv = jnp.concatenate([heads_gradient(dkd_ref, carry_refs[:per_tile]),
                               heads_gradient(dvd_ref, carry_refs[per_tile:])], axis=1).astype(BF16)
        dh1 = ALPHA * dr2_ref[...]
        dh1 += _dot_nt(dq_t, win_ref[:, 0:B_WIDTH])
        dh1 += _dot_nt(dz_t, win_ref[:, B_WIDTH:2 * B_WIDTH])
        dh1 += _dot_nt(dkv, wkv_ref[...])
        acc_in[:, 0:B_WIDTH] += _dot_tn(h1, dq_t)
        acc_in[:, B_WIDTH:2 * B_WIDTH] += _dot_tn(h1, dz_t)
        acc_kv[...] += _dot_tn(h1, dkv)
        dg_ref[...] += jnp.sum(dh1 * xh, axis=0, keepdims=True)
        db_ref[...] += jnp.sum(dh1, axis=0, keepdims=True)
        dr1_ref[...] = _ln_bwd(dh1 * g_ref[...], xh, rstd_ref[...])

        @pl.when(i == n_steps - 1)
        def _():
            half_rows = D_MODEL // 2
            shard_cols = 2 * B_WIDTH // N_CHIPS
            for s in range(N_CHIPS):
                for c in range(2):
                    gwin_ref[2 * s + c] = acc_in[c * half_rows:(c + 1) * half_rows,
                                                 s * shard_cols:(s + 1) * shard_cols].astype(BF16)
            gwkv_ref[...] = acc_kv[...].astype(BF16)

    vec = jax.ShapeDtypeStruct((1, D_MODEL), F32)
    gwin_shape = (N_DEV, D_MODEL // 2, 2 * B_WIDTH // N_CHIPS)

    def carry_spec(a):
        return pl.BlockSpec((1, CHUNK, 2 * PAIR), lambda i: (jnp.minimum(per_tile * i + a + 1, n_carry - 1), 0, 0))

    carry_specs = [carry_spec(a) for a in range(per_tile)]
    return pl.pallas_call(
        body, name="layer_b_bwd_proj", grid=(n_steps,),
        in_specs=[_rows(tm, D_MODEL), _rows(tm, 1), _const(g1.shape), _const(b1.shape), _rows(tm, D_MODEL),
                  _rows(tm, B_WIDTH), _rows(tm, B_WIDTH), _rows(tm, 2 * PAIR), _rows(tm, 2 * PAIR)]
        + carry_specs + carry_specs + [_resident(w_in.shape), _resident(w_kv.shape)],
        out_specs=[_rows(tm, D_MODEL), _const((1, D_MODEL)), _const((1, D_MODEL)), _const(gwin_shape),
                   _const(w_kv.shape)],
        out_shape=[jax.ShapeDtypeStruct((t_len, D_MODEL), F32), vec, vec,
                   jax.ShapeDtypeStruct(gwin_shape, BF16), jax.ShapeDtypeStruct(w_kv.shape, BF16)],
        scratch_shapes=[pltpu.VMEM(w_in.shape, F32), pltpu.VMEM(w_kv.shape, F32)],
        compiler_params=_params(),
    )(xh1, rstd1, g1, b1, dr2, dq, dzb, dkd, dvd, *([carry_k] * per_tile), *([carry_v] * per_tile), w_in, w_kv)


def _layer_a_bwd_mix(dr1, u, vh, z, y, rv, w_out, lng, lnb, ws, bsp_t, after):
    t_len = u.shape[0]
    tm = TM_ATTN
    n_steps = t_len // tm

    def body(dr_ref, u_ref, vh_ref, z_ref, y_ref, rv_ref, wout_ref, lng_ref, lnb_ref, ws_ref, bsp_ref, after_ref,
             dp_ref, gw_ref, dws_ref, dbsp_ref, dgs_ref, dbs_ref, dvn_scr, gw_acc):
        i = pl.program_id(0)

        @pl.when(i == 0)
        def _():
            gw_acc[...] = jnp.zeros_like(gw_acc)
            dws_ref[...] = jnp.zeros_like(dws_ref)
            dbsp_ref[...] = jnp.zeros_like(dbsp_ref)
            dgs_ref[...] = jnp.zeros_like(dgs_ref)
            dbs_ref[...] = jnp.zeros_like(dbs_ref)

        drb = dr_ref[...].astype(BF16)

        def group_cols(g):
            return slice(g * A_GROUP_DIM, (g + 1) * A_GROUP_DIM)

        tri = (lax.broadcasted_iota(jnp.int32, (CHUNK, CHUNK), 0)
               >= lax.broadcasted_iota(jnp.int32, (CHUNK, CHUNK), 1))
        lane = lax.broadcasted_iota(jnp.int32, (CHUNK, CHUNK), 1)
        ones = jnp.ones((CHUNK, A_GROUP_DIM), BF16)
        dbsp = jnp.zeros((CHUNK, CHUNK), F32)
        dy_next = _dot_nt(drb, wout_ref[group_cols(0), :])
        for g in range(A_GROUPS):
            wsg = jnp.where(tri, ws_ref[g], 0.0).astype(BF16)
            cols = group_cols(g)
            cols_z = slice(2 * A_WIDTH + g * A_GROUP_DIM, 2 * A_WIDTH + (g + 1) * A_GROUP_DIM)
            dy_g = dy_next
            if g + 1 < A_GROUPS:
                dy_next = _dot_nt(drb, wout_ref[group_cols(g + 1), :])
            gw_acc[cols, :] += _dot_tn(y_ref[:, cols], drb)
            both = jnp.zeros((CHUNK, 2 * CHUNK), F32)
            for ci in range(tm // CHUNK):
                rows = slice(ci * CHUNK, (ci + 1) * CHUNK)
                vn = (vh_ref[rows, cols].astype(F32) * lng_ref[:, cols] + lnb_ref[:, cols]).astype(BF16)
                s = _dot(wsg, vn) + bsp_ref[:, g:g + 1]
                sz, dsz = _silu_parts(z_ref[rows, cols].astype(F32))
                dy = dy_g[rows]
                t = dy * u_ref[rows, cols].astype(F32)
                dp_ref[rows, cols] = (dy * (s * sz)).astype(BF16)
                dp_ref[rows, cols_z] = (t * s * dsz).astype(BF16)
                ds_b = (t * sz).astype(BF16)
                both += _dot_nt(ds_b, jnp.concatenate([vn, ones], axis=0))
                dvn_scr[rows, cols] = _dot_tn(wsg, ds_b)
            dws_ref[g] += jnp.where(tri, both[:, 0:CHUNK], 0.0)
            dbsp = jnp.where(lane == g, both[:, CHUNK:2 * CHUNK], dbsp)
        dbsp_ref[...] += dbsp
        dvn = dvn_scr[...]
        vh_t = vh_ref[...].astype(F32)
        dgs_ref[...] += jnp.sum(dvn * vh_t, axis=0, keepdims=True)
        dbs_ref[...] += jnp.sum(dvn, axis=0, keepdims=True)
        dp_ref[:, A_WIDTH:2 * A_WIDTH] = _ln_bwd(dvn * lng_ref[...], vh_t, rv_ref[...]).astype(BF16)

        @pl.when(i == n_steps - 1)
        def _():
            gw_ref[...] = gw_acc[...].astype(BF16)

    wide = jax.ShapeDtypeStruct((1, A_WIDTH), F32)
    return pl.pallas_call(
        body, name="layer_a_bwd_mix", grid=(n_steps,),
        in_specs=[_rows(tm, D_MODEL), _rows(tm, A_WIDTH), _rows(tm, A_WIDTH), _rows(tm, A_WIDTH), _rows(tm, A_WIDTH),
                  _rows(tm, 1), _resident(w_out.shape), _const(lng.shape), _const(lnb.shape), _const(ws.shape),
                  _const(bsp_t.shape), _const(after.shape)],
        out_specs=[_rows(tm, 3 * A_WIDTH), _const(w_out.shape), _const(ws.shape), _const((CHUNK, CHUNK)),
                   _const((1, A_WIDTH)), _const((1, A_WIDTH))],
        out_shape=[jax.ShapeDtypeStruct((t_len, 3 * A_WIDTH), BF16), jax.ShapeDtypeStruct(w_out.shape, BF16),
                   jax.ShapeDtypeStruct(ws.shape, F32), jax.ShapeDtypeStruct((CHUNK, CHUNK), F32),
                   wide, wide],
        scratch_shapes=[pltpu.VMEM((tm, A_WIDTH), F32), pltpu.VMEM(w_out.shape, F32)],
        compiler_params=_params(),
    )(dr1, u, vh, z, y, rv, w_out, lng, lnb, ws, bsp_t, after)


def _layer_a_bwd_dx(dr1, dp, w_in, after, updates=()):
    t_len = dr1.shape[0]
    tm = TM_MM
    n_steps = t_len // tm
    n_upd = len(updates)

    def body(dr_ref, dp_ref, win_ref, after_ref, *refs):
        upd_in, dx_ref, upd_out = refs[:4 * n_upd], refs[4 * n_upd], refs[4 * n_upd + 1:]
        dx_ref[...] = ALPHA * dr_ref[...] + _dot_nt(dp_ref[...], win_ref[...])
        for k in range(n_upd):
            w_ref, g_ref, m_ref, v_ref = upd_in[4 * k:4 * k + 4]
            g_out, d_ref, nm_ref, nv_ref = upd_out[4 * k:4 * k + 4]
            g_out[...] = g_ref[...]
            _adamw_update(w_ref, g_ref, m_ref, v_ref, d_ref, nm_ref, nv_ref)

    upd_specs, upd_shapes, upd_args = [], [], []
    for w, g, m, v in updates:
        rows, cols = w.shape
        upd_specs.append(pl.BlockSpec((rows // n_steps, cols), lambda i: (i, 0)))
        upd_shapes.append(jax.ShapeDtypeStruct((rows, cols), F32))
        upd_args += [w, g, m, v]
    return pl.pallas_call(
        body, name="layer_a_bwd_dx", grid=(n_steps,),
        in_specs=[_rows(tm, D_MODEL), _rows(tm, 3 * A_WIDTH), _resident(w_in.shape), _const(after.shape)]
        + [s for s in upd_specs for _ in range(4)],
        out_specs=[_rows(tm, D_MODEL)] + [s for s in upd_specs for _ in range(4)],
        out_shape=[jax.ShapeDtypeStruct((t_len, D_MODEL), F32)] + [s for s in upd_shapes for _ in range(4)],
        compiler_params=_params(),
    )(dr1, dp, w_in, after, *upd_args)


def _layer_a_bwd_win(xt, dp, after):
    t_len = xt.shape[1]
    tm = TM_WIN
    n_steps = t_len // tm
    shard_cols = 3 * A_WIDTH // N_CHIPS
    half_rows = D_MODEL // 2

    def body(xt_ref, dp_ref, after_ref, gw_ref, acc):
        i = pl.program_id(1)

        @pl.when(i == 0)
        def _():
            acc[...] = jnp.zeros_like(acc)

        acc[...] += _dot(xt_ref[...], dp_ref[...])

        @pl.when(i == n_steps - 1)
        def _():
            for c in range(2):
                gw_ref[0, c] = acc[c * half_rows:(c + 1) * half_rows, :].astype(BF16)

    return pl.pallas_call(
        body, name="layer_a_bwd_win", grid=(N_CHIPS, n_steps),
        in_specs=[pl.BlockSpec((D_MODEL, tm), lambda j, i: (0, i)),
                  pl.BlockSpec((tm, shard_cols), lambda j, i: (i, j)), _const(after.shape)],
        out_specs=pl.BlockSpec((1, 2, half_rows, shard_cols), lambda j, i: (j, 0, 0, 0)),
        out_shape=jax.ShapeDtypeStruct((N_CHIPS, 2, half_rows, shard_cols), BF16),
        scratch_shapes=[pltpu.VMEM((D_MODEL, shard_cols), F32)],
        compiler_params=_params(("arbitrary", "arbitrary")),
    )(xt, dp, after)


def _bucket_onehot():
    dist = jnp.arange(CHUNK, dtype=jnp.int32)[None, :]
    max_exact = REL_BUCKETS // 2
    df = jnp.maximum(dist, 1).astype(F32)
    large = max_exact + (jnp.log(df / max_exact) / math.log(CHUNK / max_exact)
                         * (REL_BUCKETS - max_exact)).astype(jnp.int32)
    bucket = jnp.where(dist < max_exact, dist, jnp.minimum(large, REL_BUCKETS - 1))
    onehot = bucket == jnp.arange(REL_BUCKETS, dtype=jnp.int32)[:, None]
    return onehot.astype(F32)


def _bias_expand(rel_t, onehot):
    def body(rel_ref, oh_ref, out_ref):
        by_distance = jnp.dot(rel_ref[...], oh_ref[...], preferred_element_type=F32,
                              precision=lax.Precision.HIGHEST)
        for h in range(N_HEADS):
            rows = jnp.broadcast_to(by_distance[h:h + 1, :], (2 * CHUNK, CHUNK))
            out_ref[h] = pltpu.roll(rows, 0, 1, stride=1, stride_axis=0)

    return pl.pallas_call(
        body, name="bias_expand",
        out_shape=jax.ShapeDtypeStruct((N_HEADS, 2 * CHUNK, CHUNK), F32),
    )(rel_t, onehot)


def _bias_reduce(oh_ref, db_ref):
    sublane = lax.broadcasted_iota(jnp.int32, (_SUBLANES, CHUNK), 0)
    rows = []
    for h in range(N_HEADS):
        part = db_ref[h, 0:_SUBLANES, :]
        for a in range(1, 2 * CHUNK // _SUBLANES):
            tile = db_ref[h, a * _SUBLANES:(a + 1) * _SUBLANES, :]
            back = (-a * _SUBLANES) % CHUNK
            part += pltpu.roll(tile, back, 1) if back else tile
        total = jnp.where(sublane == 0, part, 0.0)
        for s in range(1, _SUBLANES):
            total += jnp.where(sublane == s, pltpu.roll(part, CHUNK - s, 1), 0.0)
        rows.append(jnp.sum(total, axis=0, keepdims=True))
    by_distance = jnp.concatenate(rows, axis=0)
    return lax.dot_general(by_distance, oh_ref[...], (((1,), (1,)), ((), ())),
                           preferred_element_type=F32, precision=lax.Precision.HIGHEST)


_SMALL_SHAPES = dict(w_spatial=(A_GROUPS, CHUNK, CHUNK), b_spatial=(A_GROUPS, CHUNK), attn_sinks=(1, N_HEADS),
                     rel_bias=(N_HEADS, REL_BUCKETS), post_ln_g=(2, D_MODEL), post_ln_b=(2, D_MODEL),
                     sgu_ln_g=(1, A_WIDTH), sgu_ln_b=(1, A_WIDTH), loss=(1, 1))
_SMALL_ORDER = tuple(_SMALL_SHAPES)


def _small_rows(name):
    shape = _SMALL_SHAPES[name]
    rows = math.prod(shape[:-1]) if shape[-1] < _LANES else math.prod(shape) // _LANES
    return -(-rows // _SUBLANES) * _SUBLANES


def _small_offset(name):
    return sum(_small_rows(n) for n in _SMALL_ORDER[:_SMALL_ORDER.index(name)])


def _pack_small(dws, dbsp, dsink, dbias, onehot, post_g, post_b, dgs, dbs, loss_vec):
    def body(dws_ref, dbsp_ref, dsink_ref, db_ref, oh_ref, g1_ref, g2_ref, b1_ref, b2_ref, dgs_ref, dbs_ref,
             loss_ref, out_ref):
        out_ref[...] = jnp.zeros_like(out_ref)

        def put_flat(name, refs):
            row = _small_offset(name)
            for ref in refs:
                for k in range(ref.shape[1] // _LANES):
                    out_ref[row:row + 1, :] = ref[:, k * _LANES:(k + 1) * _LANES]
                    row += 1

        row = _small_offset("w_spatial")
        for g in range(A_GROUPS):
            out_ref[row + g * CHUNK:row + (g + 1) * CHUNK, :] = dws_ref[g]
        row = _small_offset("b_spatial")
        out_ref[row:row + A_GROUPS, :] = dbsp_ref[...].T[0:A_GROUPS, :]
        lane = lax.broadcasted_iota(jnp.int32, (1, _LANES), 1)
        sinks = jnp.zeros((1, _LANES), F32)
        for h in range(N_HEADS):
            per_query = dsink_ref[h // GROUP:h // GROUP + 1, (h % GROUP) * CHUNK:(h % GROUP + 1) * CHUNK]
            sinks = jnp.where(lane == h, jnp.sum(per_query, axis=1, keepdims=True), sinks)
        row = _small_offset("attn_sinks")
        out_ref[row:row + 1, :] = sinks
        row = _small_offset("rel_bias")
        out_ref[row:row + N_HEADS, 0:REL_BUCKETS] = _bias_reduce(oh_ref, db_ref)
        put_flat("post_ln_g", [g1_ref, g2_ref])
        put_flat("post_ln_b", [b1_ref, b2_ref])
        put_flat("sgu_ln_g", [dgs_ref])
        put_flat("sgu_ln_b", [dbs_ref])
        row = _small_offset("loss")
        out_ref[row:row + 1, 0:1] = (0.5 / D_MODEL) * jnp.sum(loss_ref[...], axis=1, keepdims=True)

    total_rows = sum(_small_rows(n) for n in _SMALL_ORDER)
    return pl.pallas_call(
        body, name="pack_small",
        out_shape=jax.ShapeDtypeStruct((total_rows, _LANES), F32),
    )(dws, dbsp, dsink, dbias, onehot, *post_g, *post_b, dgs, dbs, loss_vec)


def _place():
    return lax.axis_index("x"), lax.axis_index("y"), lax.axis_index("c")


RELAY_PIECES = 4


def _shard_window(full_ref, shard_shape, col_sharded, s, half, piece=None):
    rows, cols = shard_shape
    if half is None:
        start, size = 0, rows
    elif piece is None:
        start, size = half * (rows // 2), rows // 2
    else:
        size = rows // 2 // RELAY_PIECES
        start = (half * RELAY_PIECES + piece) * size
    if col_sharded:
        return full_ref.at[pl.ds(start, size), pl.ds(s * cols, cols)]
    return full_ref.at[pl.ds(s * rows + start, size), :]


def _other_chips(x, y):
    return [(1 - x, y), (x, 1 - y), (1 - x, 1 - y)]


def _gather_weights(shards, col_sharded, fetch, ln_shard, tokens=None):
    n_w = len(shards)
    fetched = [w for w in range(n_w) if fetch[w]]
    full_shapes = []
    for w, cs in zip(shards, col_sharded):
        r, c = w.shape
        full_shapes.append((r, c * N_CHIPS) if cs else (r * N_CHIPS, c))

    n_tok = 0 if tokens is None else 1
    tok_tile = TM_MM
    n_tiles = 0 if tokens is None else tokens.shape[0] // tok_tile

    def body(*refs):
        refs = list(refs)
        in_refs = [refs.pop(0) for _ in range(n_w)]
        ln_ref = refs.pop(0)
        tok_ref = refs.pop(0) if n_tok else None
        full_refs = [refs.pop(0) for _ in range(n_w)]
        ln_full = refs.pop(0)
        prod_ref = refs.pop(0) if n_tok else None
        raw = [refs.pop(0) for _ in range(n_w)]
        stage = [refs.pop(0) for _ in range(n_w)]
        send_sems, recv_sems, load_sems, local_sems, ln_send, ln_recv = refs[:6]
        tok_buf, prod_buf, tok_sems, prod_sems = refs[6:] if n_tok else (None,) * 4
        x, y, c = _place()
        s_me = 2 * x + y
        chips = _other_chips(x, y)
        pieces = range(RELAY_PIECES)

        def shard_window(w, s, half, piece=None):
            return _shard_window(full_refs[w], shards[w].shape, col_sharded[w], s, half, piece)

        def piece_rows(w, half, piece):
            rows = shards[w].shape[0] // 2 // RELAY_PIECES
            return pl.ds(pl.multiple_of((half * RELAY_PIECES + piece) * rows, rows), rows)

        def ici_copy(w, k, sender_shard, piece):
            idx = (w * 3 + k) * RELAY_PIECES + piece
            return pltpu.make_async_remote_copy(
                src_ref=stage[w].at[piece_rows(w, c, piece), :], dst_ref=shard_window(w, sender_shard, c, piece),
                send_sem=send_sems.at[idx], recv_sem=recv_sems.at[idx],
                device_id=(*chips[k], c), device_id_type=MESH)

        def d2d_copy(w, k, half, piece):
            s_k = 2 * chips[k][0] + chips[k][1]
            win = shard_window(w, s_k, half, piece)
            idx = (3 * n_w + w * 3 + k) * RELAY_PIECES + piece
            return pltpu.make_async_remote_copy(
                src_ref=win, dst_ref=win, send_sem=send_sems.at[idx], recv_sem=recv_sems.at[idx],
                device_id=(x, y, 1 - c), device_id_type=MESH)

        def ln_copy(k, slot):
            return pltpu.make_async_remote_copy(
                src_ref=ln_ref, dst_ref=ln_full.at[slot], send_sem=ln_send.at[k], recv_sem=ln_recv.at[k],
                device_id=(*chips[k], c), device_id_type=MESH)

        loads = []

        def load(w, rows):
            window = (rows, slice(None)) if rows is not None else (slice(None), slice(None))
            cp = pltpu.make_async_copy(in_refs[w].at[window], raw[w].at[window], load_sems.at[len(loads)])
            cp.start()
            loads.append((cp, w, window))

        for half in (c, 1 - c):
            for w in fetched:
                for q in pieces:
                    load(w, piece_rows(w, half, q))
        for w in range(n_w):
            if not fetch[w]:
                load(w, None)

        def to_bf16(k):
            cp, w, window = loads[k]
            cp.wait()
            stage[w][window] = raw[w][window].astype(BF16)

        ln_full[s_me] = ln_ref[...]
        def shard_of(k):
            return 2 * chips[k][0] + chips[k][1]

        relay_from = jnp.where(c == 0, shard_of(0), shard_of(1))
        relay_to = (jnp.where(c == 0, x, 1 - x), jnp.where(c == 0, 1 - y, y), c)

        def relay_copy(w, sender_shard, piece):
            win = shard_window(w, sender_shard, c, piece)
            idx = (w * 3 + 2) * RELAY_PIECES + piece
            return pltpu.make_async_remote_copy(
                src_ref=win, dst_ref=win, send_sem=send_sems.at[idx], recv_sem=recv_sems.at[idx],
                device_id=relay_to, device_id_type=MESH)

        first = [ln_copy(k, s_me) for k in range(3)]
        for cp in first:
            cp.start()
        n_sent = 0
        for w in fetched:
            for q in pieces:
                to_bf16(n_sent)
                n_sent += 1
                for k in range(2):
                    cp = ici_copy(w, k, s_me, q)
                    cp.start()
                    first.append(cp)
        for k in range(n_sent, len(loads)):
            to_bf16(k)
        own = [pltpu.make_async_copy(stage[w], shard_window(w, s_me, None), local_sems.at[w]) for w in range(n_w)]
        for cp in own:
            cp.start()

        def tok_copy(t):
            return pltpu.make_async_copy(tok_ref.at[pl.ds(t * tok_tile, tok_tile), :], tok_buf.at[t % 2],
                                         tok_sems.at[t % 2])

        def prod_copy(t):
            return pltpu.make_async_copy(prod_buf.at[t % 2], prod_ref.at[pl.ds(t * tok_tile, tok_tile), :],
                                         prod_sems.at[t % 2])

        def product_tiles(tiles):
            for t in tiles:
                if t + 1 < n_tiles:
                    tok_copy(t + 1).start()
                tok_copy(t).wait()
                if t >= 2:
                    prod_copy(t - 2).wait()
                prod_buf[t % 2] = _dot(tok_buf[t % 2].astype(BF16), stage[0][...])
                prod_copy(t).start()

        if n_tiles:
            tok_copy(0).start()
        share = [4, 3, 3, 3, 3, 0, 0, 0]
        assert len(share) == 2 * RELAY_PIECES
        bounds = [sum(share[:k]) * n_tiles // sum(share) for k in range(len(share) + 1)]
        passed = []
        for w in fetched:
            for q in pieces:
                if w == fetched[0]:
                    product_tiles(range(bounds[q], bounds[q + 1]))
                for k in range(2):
                    ici_copy(w, k, shard_of(k), q).wait_recv()
                relay = relay_copy(w, relay_from, q)
                relay.start()
                passed.append(relay)
                for k in range(2):
                    fwd = d2d_copy(w, k, c, q)
                    fwd.start()
                    passed.append(fwd)
        for w in fetched:
            for q in pieces:
                if w == fetched[0]:
                    product_tiles(range(bounds[RELAY_PIECES + q], bounds[RELAY_PIECES + q + 1]))
                relay_copy(w, shard_of(2), q).wait_recv()
                fwd = d2d_copy(w, 2, c, q)
                fwd.start()
                passed.append(fwd)
        for w in fetched:
            for k in range(3):
                for q in pieces:
                    d2d_copy(w, k, 1 - c, q).wait_recv()
        for k in range(3):
            ln_copy(k, 2 * chips[k][0] + chips[k][1]).wait_recv()
        for cp in first + passed:
            cp.wait_send()
        for cp in own:
            cp.wait()
        for t in range(max(n_tiles - 2, 0), n_tiles):
            prod_copy(t).wait()

    vmem = pl.BlockSpec(memory_space=pltpu.VMEM)
    hbm = pl.BlockSpec(memory_space=pl.ANY)
    prod_cols = shards[0].shape[1]
    tok_args = [] if tokens is None else [tokens]
    tok_out = [] if tokens is None else [jax.ShapeDtypeStruct((tokens.shape[0], prod_cols), F32)]
    tok_scratch = [] if tokens is None else [
        pltpu.VMEM((2, tok_tile, tokens.shape[1]), F32), pltpu.VMEM((2, tok_tile, prod_cols), F32),
        pltpu.SemaphoreType.DMA((2,)), pltpu.SemaphoreType.DMA((2,))]
    return pl.pallas_call(
        body, name="gather_weights",
        in_specs=[hbm] * n_w + [vmem] + [hbm] * n_tok,
        out_specs=[hbm] * n_w + [vmem] + [hbm] * n_tok,
        out_shape=[jax.ShapeDtypeStruct(s, BF16) for s in full_shapes]
        + [jax.ShapeDtypeStruct((N_CHIPS,) + ln_shard.shape, F32)] + tok_out,
        scratch_shapes=[pltpu.VMEM(w.shape, F32) for w in shards] + [pltpu.VMEM(w.shape, BF16) for w in shards]
        + [pltpu.SemaphoreType.DMA((6 * RELAY_PIECES * n_w,)), pltpu.SemaphoreType.DMA((6 * RELAY_PIECES * n_w,)),
           pltpu.SemaphoreType.DMA((2 * RELAY_PIECES * len(fetched) + n_w - len(fetched),)),
           pltpu.SemaphoreType.DMA((n_w,)), pltpu.SemaphoreType.DMA((3,)), pltpu.SemaphoreType.DMA((3,))]
        + tok_scratch,
        compiler_params=pltpu.CompilerParams(vmem_limit_bytes=VMEM_LIMIT),
    )(*shards, ln_shard, *tok_args)


def _fetch_copy(full_ref, shard_shape, col_sharded, sender_shard, send_sems, recv_sems, idx, chip, c):
    win = _shard_window(full_ref, shard_shape, col_sharded, sender_shard, None)
    return pltpu.make_async_remote_copy(src_ref=win, dst_ref=win, send_sem=send_sems.at[idx],
                                        recv_sem=recv_sems.at[idx], device_id=(*chip, c), device_id_type=MESH)


def _fetch_start(fulls, shard_shapes, col_sharded):
    n = len(fulls)

    def body(*refs):
        full = refs[:n]
        send_sems, recv_sems = refs[n], refs[n + 1]
        token = refs[-1]
        x, y, c = _place()
        for w in range(n):
            for k, chip in enumerate(_other_chips(x, y)):
                _fetch_copy(full[w], shard_shapes[w], col_sharded[w], 2 * x + y, send_sems, recv_sems, w * 3 + k,
                            chip, c).start()
        token[...] = jnp.zeros_like(token)

    outs = pl.pallas_call(
        body, name="fetch_start",
        out_shape=(pltpu.SemaphoreType.DMA((3 * n,)), pltpu.SemaphoreType.DMA((3 * n,)),
                   *[pltpu.HBM(f.shape, f.dtype) for f in fulls], jax.ShapeDtypeStruct((8, 128), F32)),
        in_specs=[_HBM] * n,
        out_specs=(_SEM, _SEM, *([_HBM] * n), pl.BlockSpec(memory_space=pltpu.VMEM)),
        input_output_aliases={i: 2 + i for i in range(n)},
        compiler_params=pltpu.CompilerParams(has_side_effects=pltpu.SideEffectType.DATAFLOW_SIDE_EFFECTING),
    )(*[pltpu.with_memory_space_constraint(f, pltpu.HBM) for f in fulls])
    return dict(send=outs[0], recv=outs[1], full=list(outs[2:2 + n])), outs[-1]


def _fetch_wait(group, shard_shapes, col_sharded, after):
    n = len(group["full"])

    def body(*refs):
        full = refs[:n]
        send_sems, recv_sems = refs[n], refs[n + 1]
        x, y, c = _place()
        for w in range(n):
            for k, chip in enumerate(_other_chips(x, y)):
                _fetch_copy(full[w], shard_shapes[w], col_sharded[w], 2 * x + y, send_sems, recv_sems, w * 3 + k,
                            chip, c).wait_send()
                _fetch_copy(full[w], shard_shapes[w], col_sharded[w], 2 * chip[0] + chip[1], send_sems, recv_sems,
                            w * 3 + k, chip, c).wait_recv()

    outs = pl.pallas_call(
        body, name="fetch_wait", out_shape=tuple(pltpu.HBM(f.shape, f.dtype) for f in group["full"]),
        in_specs=[_HBM] * n + [_SEM, _SEM, pl.BlockSpec(memory_space=pl.ANY)],
        out_specs=tuple([_HBM] * n), input_output_aliases={i: i for i in range(n)},
        compiler_params=pltpu.CompilerParams(has_side_effects=pltpu.SideEffectType.DATAFLOW_SIDE_EFFECTING),
    )(*group["full"], group["send"], group["recv"], after)
    return list(outs)


_HBM = pl.BlockSpec(memory_space=pltpu.HBM)
_SEM = pl.BlockSpec(memory_space=pltpu.SEMAPHORE)
_N_PEER = N_DEV - 1


def _peer(x, y, c, k):
    return (x + (k >> 2)) % 2, (y + ((k >> 1) & 1)) % 2, (c + (k & 1)) % 2


def _exchange_copy(src_ref, land_ref, sliced, send_sems, recv_sems, idx, x, y, c, k):
    px, py, pc = _peer(x, y, c, k)
    src = src_ref.at[4 * px + 2 * py + pc] if sliced else src_ref
    return pltpu.make_async_remote_copy(
        src_ref=src, dst_ref=land_ref.at[4 * x + 2 * y + c],
        send_sem=send_sems.at[idx], recv_sem=recv_sems.at[idx], device_id=(px, py, pc), device_id_type=MESH)


def _exchange_start(tag, arrays, sliced):
    n = len(arrays)
    lands = [lax.empty(a.shape if s else (N_DEV,) + a.shape, a.dtype) for a, s in zip(arrays, sliced)]

    def body(*refs):
        src, land = refs[:n], refs[n:2 * n]
        send_sems, recv_sems = refs[2 * n], refs[2 * n + 1]
        token = refs[-1]
        x, y, c = _place()
        for w in range(n):
            for k in range(1, N_DEV):
                _exchange_copy(src[w], land[w], sliced[w], send_sems, recv_sems, w * _N_PEER + k - 1, x, y, c, k).start()
        token[...] = jnp.zeros_like(token)

    outs = pl.pallas_call(
        body, name="exchange_start_" + tag,
        out_shape=(pltpu.SemaphoreType.DMA((n * _N_PEER,)), pltpu.SemaphoreType.DMA((n * _N_PEER,)),
                   *[pltpu.HBM(a.shape, a.dtype) for a in arrays], *[pltpu.HBM(l.shape, l.dtype) for l in lands],
                   jax.ShapeDtypeStruct((8, 128), F32)),
        in_specs=[_HBM] * (2 * n),
        out_specs=(_SEM, _SEM, *([_HBM] * (2 * n)), pl.BlockSpec(memory_space=pltpu.VMEM)),
        input_output_aliases={i: 2 + i for i in range(2 * n)},
        compiler_params=pltpu.CompilerParams(has_side_effects=pltpu.SideEffectType.DATAFLOW_SIDE_EFFECTING),
    )(*[pltpu.with_memory_space_constraint(a, pltpu.HBM) for a in arrays],
      *[pltpu.with_memory_space_constraint(l, pltpu.HBM) for l in lands])
    return dict(send=outs[0], recv=outs[1], src=list(outs[2:2 + n]), land=list(outs[2 + n:2 + 2 * n]),
                sliced=list(sliced)), outs[-1]


def _exchange_wait(tag, groups, after):
    counts = [len(g["src"]) for g in groups]
    total = sum(counts)

    def body(*refs):
        pos = 0
        x, y, c = _place()
        for g, n in zip(groups, counts):
            src, land = refs[pos:pos + n], refs[pos + n:pos + 2 * n]
            send_sems, recv_sems = refs[pos + 2 * n], refs[pos + 2 * n + 1]
            pos += 2 * n + 2
            for w in range(n):
                for k in range(1, N_DEV):
                    cp = _exchange_copy(src[w], land[w], g["sliced"][w], send_sems, recv_sems,
                                        w * _N_PEER + k - 1, x, y, c, k)
                    cp.wait_send()
                    cp.wait_recv()

    operands, in_specs, aliases, out_shape = [], [], {}, []
    for g in groups:
        for a in g["src"] + g["land"]:
            aliases[len(operands)] = len(out_shape)
            out_shape.append(pltpu.HBM(a.shape, a.dtype))
            operands.append(a)
            in_specs.append(_HBM)
        operands += [g["send"], g["recv"]]
        in_specs += [_SEM, _SEM]
    operands.append(after)
    in_specs.append(pl.BlockSpec(memory_space=pl.ANY))
    outs = pl.pallas_call(
        body, name="exchange_wait_" + tag, out_shape=tuple(out_shape), in_specs=in_specs,
        out_specs=tuple([_HBM] * (2 * total)), input_output_aliases=aliases,
        compiler_params=pltpu.CompilerParams(has_side_effects=pltpu.SideEffectType.DATAFLOW_SIDE_EFFECTING),
    )(*operands)
    srcs, lands, pos = [], [], 0
    for n in counts:
        srcs += list(outs[pos:pos + n])
        lands += list(outs[pos + n:pos + 2 * n])
        pos += 2 * n
    return srcs, lands


def _sum_and_swap(tag, pieces, lands, small=None, small_land=None):
    n_w = len(pieces)
    n_small = 0 if small is None else 1

    def body(*refs):
        g_refs, land_refs = refs[:n_w], refs[n_w:2 * n_w]
        pos = 2 * n_w + 2 * n_small
        out_refs = refs[pos:pos + n_w]
        pos += n_w + n_small
        bufs = refs[pos:pos + n_w]
        load_sems, swap_send, swap_recv = refs[pos + n_w + 2 * n_small:]
        x, y, c = _place()
        me = 4 * x + 2 * y + c

        def slot(k):
            px, py, pc = _peer(x, y, c, k)
            return 4 * px + 2 * py + pc

        def swap_copy(w, half):
            rows = pieces[w].shape[1]
            win = out_refs[w].at[pl.ds(pl.multiple_of(half * rows, rows), rows), :]
            return pltpu.make_async_remote_copy(
                src_ref=win, dst_ref=win, send_sem=swap_send.at[w], recv_sem=swap_recv.at[w],
                device_id=(x, y, 1 - c), device_id_type=MESH)

        loads = []
        for w in range(n_w):
            per_w = [pltpu.make_async_copy(g_refs[w].at[me], bufs[w].at[me], load_sems.at[w * N_DEV])]
            per_w += [pltpu.make_async_copy(land_refs[w].at[slot(k)], bufs[w].at[slot(k)], load_sems.at[w * N_DEV + k])
                      for k in range(1, N_DEV)]
            loads.append(per_w)
        small_loads = []
        if n_small:
            small_ref, small_land_ref = refs[2 * n_w], refs[2 * n_w + 1]
            small_out = refs[2 * n_w + 2 + n_w]
            small_buf, small_sems = refs[pos + n_w], refs[pos + n_w + 1]
            small_loads = [pltpu.make_async_copy(small_land_ref.at[slot(k)], small_buf.at[slot(k)],
                                                 small_sems.at[k - 1]) for k in range(1, N_DEV)]
        for cp in [cp for per_w in loads for cp in per_w] + small_loads:
            cp.start()
        if n_small:
            small_buf[me] = small_ref[...]
        swaps = []
        for w in range(n_w):
            for cp in loads[w]:
                cp.wait()
            rows = pieces[w].shape[1]
            total = bufs[w][0].astype(F32)
            for p in range(1, N_DEV):
                total += bufs[w][p].astype(F32)
            out_refs[w][pl.ds(pl.multiple_of(c * rows, rows), rows), :] = total
            sw = swap_copy(w, c)
            sw.start()
            swaps.append(sw)
        if n_small:
            for cp in small_loads:
                cp.wait()
            total = small_buf[0]
            for p in range(1, N_DEV):
                total += small_buf[p]
            small_out[...] = total
        for w in range(n_w):
            swap_copy(w, 1 - c).wait_recv()
        for sw in swaps:
            sw.wait_send()

    vmem = pl.BlockSpec(memory_space=pltpu.VMEM)
    hbm = pl.BlockSpec(memory_space=pl.ANY)
    small_args = [small, small_land] if n_small else []
    small_shapes = [jax.ShapeDtypeStruct(small.shape, F32)] if n_small else []
    small_scratch = ([pltpu.VMEM((N_DEV,) + small.shape, F32), pltpu.SemaphoreType.DMA((_N_PEER,))]
                     if n_small else [])
    return pl.pallas_call(
        body, name="sum_and_swap_" + tag,
        in_specs=[hbm] * (2 * n_w) + [vmem, hbm] * n_small,
        out_specs=[vmem] * (n_w + n_small),
        out_shape=[jax.ShapeDtypeStruct((2 * p.shape[1], p.shape[2]), F32) for p in pieces] + small_shapes,
        scratch_shapes=[pltpu.VMEM(p.shape, BF16) for p in pieces] + small_scratch
        + [pltpu.SemaphoreType.DMA((n_w * N_DEV,)), pltpu.SemaphoreType.DMA((n_w,)),
           pltpu.SemaphoreType.DMA((n_w,))],
        compiler_params=pltpu.CompilerParams(vmem_limit_bytes=VMEM_LIMIT),
    )(*pieces, *lands, *small_args)


def _adamw_values(w, g_t, m, v):
    c1 = 1.0 - ADAM_B1 ** ADAM_STEP
    c2 = 1.0 - ADAM_B2 ** ADAM_STEP
    nm = ADAM_B1 * m + (1.0 - ADAM_B1) * g_t
    nv = ADAM_B2 * v + (1.0 - ADAM_B2) * (g_t * g_t)
    return -ADAM_LR * ((nm / c1) / (jnp.sqrt(nv / c2) + ADAM_EPS) + ADAM_WD * w), nm, nv


def _adamw_update(w_ref, g_ref, m_ref, v_ref, d_ref, nm_ref, nv_ref):
    d_ref[...], nm_ref[...], nv_ref[...] = _adamw_values(w_ref[...], g_ref[...], m_ref[...], v_ref[...])


def _adamw_small(packed, shard_index, names, weights, moments_m, moments_v):
    n = len(names)
    shapes = [weights[name].shape for name in names]
    flat = [a[name].reshape(-1, a[name].shape[-1]) for name in names for a in (weights, moments_m, moments_v)]

    def body(packed_ref, shard_ref, *refs):
        loss_row = _small_offset("loss")
        refs[-1][...] = packed_ref[loss_row:loss_row + 1, 0:1]
        for k, name in enumerate(names):
            w_ref, m_ref, v_ref = refs[3 * k:3 * k + 3]
            g_ref, d_ref, nm_ref, nv_ref = refs[3 * n + 4 * k:3 * n + 4 * k + 4]
            rows, cols = w_ref.shape
            first = _small_offset(name)
            if cols <= _LANES:
                blocks = [(slice(0, rows), packed_ref[first:first + rows, 0:cols])]
            else:
                per_row = cols // _LANES
                if cols < _SMALL_SHAPES[name][-1]:
                    first = first + shard_ref[0] * per_row
                blocks = [(slice(i, i + 1),
                           jnp.concatenate([packed_ref[pl.ds(first + i * per_row + j, 1), :] for j in range(per_row)],
                                           axis=1)) for i in range(rows)]
            for at, g_t in blocks:
                g_ref[at, :] = g_t
                d_ref[at, :], nm_ref[at, :], nv_ref[at, :] = _adamw_values(w_ref[at, :], g_t, m_ref[at, :],
                                                                           v_ref[at, :])

    vmem = pl.BlockSpec(memory_space=pltpu.VMEM)
    outs = pl.pallas_call(
        body, name="adamw_small",
        in_specs=[vmem, pl.BlockSpec(memory_space=pltpu.SMEM)] + [vmem] * (3 * n),
        out_shape=[jax.ShapeDtypeStruct(flat[3 * k].shape, F32) for k in range(n) for _ in range(4)]
        + [jax.ShapeDtypeStruct((1, 1), F32)],
    )(packed, shard_index.reshape(1).astype(jnp.int32), *flat)
    return [tuple(o.reshape(shapes[k]) for o in outs[4 * k:4 * k + 4]) for k in range(n)], outs[-1].reshape(())


def _adamw(label, w, g, m, v):
    shape = w.shape
    cols = shape[-1]
    rows = w.size // cols
    args = [a.reshape(rows, cols) for a in (w, g, m, v)]

    def body(w_ref, g_ref, m_ref, v_ref, g_out, d_ref, nm_ref, nv_ref):
        g_out[...] = g_ref[...]
        _adamw_update(w_ref, g_ref, m_ref, v_ref, d_ref, nm_ref, nv_ref)

    block_rows = 256 if rows % 256 == 0 and rows > 256 else rows
    spec = pl.BlockSpec((block_rows, cols), lambda i: (i, 0))
    outs = pl.pallas_call(
        body, name="adamw_" + label, grid=(rows // block_rows,),
        in_specs=[spec] * 4, out_specs=[spec] * 4,
        out_shape=[jax.ShapeDtypeStruct((rows, cols), F32)] * 4,
        compiler_params=_params(),
    )(*args)
    return [o.reshape(shape) for o in outs]


def _no_send(tag, arrays, sliced):
    return jnp.zeros((8, 128), F32)


def _local_step(x, tgt, w_in_a, later_weights, first_after, sgu_ln_g, sgu_ln_b, w_spatial, b_spatial,
                attn_sinks, rel_bias, post_ln_g, post_ln_b, send=_no_send, own_product=None):
    bsp_t = b_spatial.T
    g1, b1 = post_ln_g[0:1], post_ln_b[0:1]
    g2, b2 = post_ln_g[1:2], post_ln_b[1:2]
    onehot = _bucket_onehot()
    bias = _bias_expand(rel_bias.T, onehot)
    win = _window_tables()

    if own_product is None:
        xt, u, vh, z, rv, y = _layer_a_fwd(x, w_in_a, None, sgu_ln_g, sgu_ln_b, w_spatial, bsp_t, first_after)
    else:
        chip, p_own = own_product
        xt, u, vh, z, rv, y = lax.switch(
            chip, [functools.partial(_layer_a_fwd, own=own) for own in range(N_CHIPS)],
            x, w_in_a, p_own, sgu_ln_g, sgu_ln_b, w_spatial, bsp_t, first_after)
    w_out_a, w_kv, w_in_b, w_out_b = later_weights(y)
    xh1, rstd1, q, zb, kd, vd = _layer_b_proj(x, y, w_out_a, g1, b1, w_in_b, w_kv)
    o, probs, sink_probs, dr2, loss_vec, dg2, db2 = _layer_b_fwd(q, zb, kd, vd, bias, win, attn_sinks, xh1, g1, b1,
                                                                 w_out_b, g2, b2, tgt)
    dq, dzb, dkd, dvd, carry_k, carry_v, gw_out_b, dsink, dbias = _layer_b_bwd_attn(
        dr2, zb, o, q, kd, vd, probs, sink_probs, w_out_b)
    dr1, dg1, db1, gw_in_b, gw_kv = _layer_b_bwd_proj(xh1, rstd1, g1, b1, dr2, dq, dzb, dkd, dvd, carry_k, carry_v,
                                                      w_in_b, w_kv)
    gw_out_b = gw_out_b.reshape(N_DEV, -1, D_MODEL)
    gw_kv = gw_kv.reshape(N_DEV, -1, 2 * PAIR)
    after = send("b", [gw_out_b, gw_in_b, gw_kv], [True, True, True])
    dp, gw_out_a, dws, dbsp, dgs, dbs = _layer_a_bwd_mix(dr1, u, vh, z, y, rv, w_out_a, sgu_ln_g, sgu_ln_b,
                                                         w_spatial, bsp_t, after)
    gw_out_a = gw_out_a.reshape(N_DEV, -1, D_MODEL)
    small = _pack_small(dws, dbsp, dsink, dbias, onehot, (dg1, dg2), (db1, db2), dgs, dbs, loss_vec)
    after = send("a_out", [gw_out_a, small], [True, False])
    gw_in_a = _layer_a_bwd_win(xt, dp, after).reshape(N_DEV, D_MODEL // 2, -1)
    after = send("a_in", [gw_in_a], [True])
    after, updates = after if isinstance(after, tuple) else (after, ())
    grad_x, *updated = _layer_a_bwd_dx(dr1, dp, w_in_a, after, updates)

    pieces = [gw_in_a, gw_out_a, gw_kv, gw_in_b, gw_out_b]
    return grad_x, pieces, small, updated


def kernel(x, w_in_a, sgu_ln_g, sgu_ln_b, w_spatial, b_spatial, w_out_a, w_kv, w_in_b, attn_sinks, rel_bias, w_out_b, post_ln_g, post_ln_b, loss_target, m_w_in_a, m_sgu_ln_g, m_sgu_ln_b, m_w_spatial, m_b_spatial, m_w_out_a, m_w_kv, m_w_in_b, m_attn_sinks, m_rel_bias, m_w_out_b, m_post_ln_g, m_post_ln_b, v_w_in_a, v_sgu_ln_g, v_sgu_ln_b, v_w_spatial, v_b_spatial, v_w_out_a, v_w_kv, v_w_in_b, v_attn_sinks, v_rel_bias, v_w_out_b, v_post_ln_g, v_post_ln_b):
    weights = dict(w_in_a=w_in_a, sgu_ln_g=sgu_ln_g, sgu_ln_b=sgu_ln_b, w_spatial=w_spatial, b_spatial=b_spatial,
                   w_out_a=w_out_a, w_kv=w_kv, w_in_b=w_in_b, attn_sinks=attn_sinks, rel_bias=rel_bias,
                   w_out_b=w_out_b, post_ln_g=post_ln_g, post_ln_b=post_ln_b)
    moments_m = dict(w_in_a=m_w_in_a, sgu_ln_g=m_sgu_ln_g, sgu_ln_b=m_sgu_ln_b, w_spatial=m_w_spatial,
                     b_spatial=m_b_spatial, w_out_a=m_w_out_a, w_kv=m_w_kv, w_in_b=m_w_in_b,
                     attn_sinks=m_attn_sinks, rel_bias=m_rel_bias, w_out_b=m_w_out_b, post_ln_g=m_post_ln_g,
                     post_ln_b=m_post_ln_b)
    moments_v = dict(w_in_a=v_w_in_a, sgu_ln_g=v_sgu_ln_g, sgu_ln_b=v_sgu_ln_b, w_spatial=v_w_spatial,
                     b_spatial=v_b_spatial, w_out_a=v_w_out_a, w_kv=v_w_kv, w_in_b=v_w_in_b,
                     attn_sinks=v_attn_sinks, rel_bias=v_rel_bias, w_out_b=v_w_out_b, post_ln_g=v_post_ln_g,
                     post_ln_b=v_post_ln_b)
    order = ("w_in_a", "sgu_ln_g", "sgu_ln_b", "w_spatial", "b_spatial", "w_out_a", "w_kv", "w_in_b", "attn_sinks",
             "rel_bias", "w_out_b", "post_ln_g", "post_ln_b")

    shard_index = 2 * lax.axis_index("x") + lax.axis_index("y")
    ln_shard = jnp.concatenate([sgu_ln_g, sgu_ln_b], axis=0)
    shards = [w_in_a[0], w_out_a[0], w_kv, w_in_b[0], w_out_b[0]]
    col_sharded = [True, False, False, True, False]
    full_in_a, *later, ln_full, p_own = _gather_weights(shards, col_sharded, [True, False, False, False, False],
                                                        ln_shard, tokens=x[0])
    ln_full = jnp.transpose(ln_full, (1, 0, 2)).reshape(2, A_WIDTH)
    later_shapes = [s.shape for s in shards[1:]]
    fetch_group, fetch_token = _fetch_start(later, later_shapes, col_sharded[1:])

    def later_weights(y):
        return _fetch_wait(fetch_group, later_shapes, col_sharded[1:], y)

    groups, grads, deltas, new_m, new_v, scalars = {}, {}, {}, {}, {}, {}
    early = ("w_out_b", "w_in_b", "w_kv", "w_out_a")

    def two_dim(a):
        return a.reshape(-1, a.shape[-1])

    def send(tag, arrays, sliced):
        groups[tag], token = _exchange_start(tag, arrays, sliced)
        if tag != "a_in":
            return token
        srcs, lands = _exchange_wait("early", [groups["b"], groups["a_out"]], token)
        *reduced, packed_sum = _sum_and_swap("early", srcs[:4], lands[:4], srcs[4], lands[4])
        updates = [(two_dim(weights[n]), g, two_dim(moments_m[n]), two_dim(moments_v[n]))
                   for n, g in zip(early, reduced)]
        small_names = ("sgu_ln_g", "sgu_ln_b", "w_spatial", "b_spatial", "attn_sinks", "rel_bias", "post_ln_g",
                       "post_ln_b")
        def as_packed(arrays):
            return {n: (arrays[n].T if n == "rel_bias" else arrays[n]) for n in small_names}

        small_updates, scalars["loss"] = _adamw_small(packed_sum, shard_index, small_names, as_packed(weights),
                                                      as_packed(moments_m), as_packed(moments_v))
        for name, results in zip(small_names, small_updates):
            grads[name], deltas[name], new_m[name], new_v[name] = [
                a.T if name == "rel_bias" else a for a in results]
        return new_m["b_spatial"].reshape(A_GROUPS, CHUNK), updates

    grad_x, _, _, updated = _local_step(
        x[0], loss_target[0], full_in_a, later_weights, fetch_token, ln_full[0:1], ln_full[1:2], w_spatial[0],
        b_spatial[0], attn_sinks, rel_bias, post_ln_g, post_ln_b, send=send, own_product=(shard_index, p_own))
    for k, name in enumerate(early):
        grads[name], deltas[name], new_m[name], new_v[name] = [
            a.reshape(weights[name].shape) for a in updated[4 * k:4 * k + 4]]

    srcs, lands = _exchange_wait("late", [groups["a_in"]], grad_x)
    (g_in_a,) = _sum_and_swap("late", srcs, lands)
    grads["w_in_a"], deltas["w_in_a"], new_m["w_in_a"], new_v["w_in_a"] = _adamw(
        "w_in_a", w_in_a, g_in_a.reshape(w_in_a.shape), m_w_in_a, v_w_in_a)
    return (scalars["loss"], grad_x[None], *[grads[n] for n in order], *[deltas[n] for n in order],
            *[new_m[n] for n in order], *[new_v[n] for n in order])
```

```python
import functools
import math

import jax
import jax.numpy as jnp
from jax import lax
from jax.experimental import pallas as pl
from jax.experimental.pallas import tpu as pltpu

F32 = jnp.float32
BF16 = jnp.bfloat16

D_MODEL = 1024
A_WIDTH = 2048
A_GROUPS = 8
A_GROUP_DIM = 256
CHUNK = 128
N_HEADS = 16
N_KV = 2
HEAD_DIM = 64
PAIR = 2 * HEAD_DIM
B_WIDTH = 1024
REL_BUCKETS = 32
ALPHA = 4.0 ** 0.25
LN_EPS = 1e-5
NEG_INF = -1e30
SCALE = HEAD_DIM ** -0.5

ADAM_LR = 0.001
ADAM_B1 = 0.9
ADAM_B2 = 0.999
ADAM_EPS = 1e-08
ADAM_WD = 0.01
ADAM_STEP = 10

N_DEV = 8
N_CHIPS = 4
MESH = pl.DeviceIdType.MESH
VMEM_LIMIT = 56 * 1024 * 1024

TM_ATTN = 256
TM_BWD_ATTN = 512
TM_MM = 512
TM_WIN = 1024
_LANES = 128
_SUBLANES = 8


def _dot(a, b):
    return jnp.dot(a, b, preferred_element_type=F32)


def _dot_nt(a, b):
    return lax.dot_general(a, b, (((1,), (1,)), ((), ())), preferred_element_type=F32)


def _dot_tn(a, b):
    return lax.dot_general(a, b, (((0,), (0,)), ((), ())), preferred_element_type=F32)


def _ln_fwd(r):
    mu = jnp.mean(r, axis=-1, keepdims=True)
    rc = r - mu
    var = jnp.mean(rc * rc, axis=-1, keepdims=True)
    rstd = lax.rsqrt(var + LN_EPS)
    return rc * rstd, rstd


def _ln_bwd(dxh, xh, rstd):
    m1 = jnp.mean(dxh, axis=-1, keepdims=True)
    m2 = jnp.mean(dxh * xh, axis=-1, keepdims=True)
    return rstd * (dxh - m1 - xh * m2)


def _silu_parts(z):
    sg = jax.nn.sigmoid(z)
    return z * sg, sg * (1.0 + z * (1.0 - sg))


def _dup_halves(blk):
    sw = pltpu.roll(blk, HEAD_DIM, 1)
    lo = lax.broadcasted_iota(jnp.int32, blk.shape, 1) < HEAD_DIM
    return jnp.where(lo, blk, sw), jnp.where(lo, sw, blk)


def _fold_halves(blk):
    return blk + pltpu.roll(blk, HEAD_DIM, 1)


def _resident(shape):
    nd = len(shape)
    return pl.BlockSpec(shape, lambda *_: (0,) * nd, pipeline_mode=pl.Buffered(1))


def _const(shape):
    nd = len(shape)
    return pl.BlockSpec(shape, lambda *_: (0,) * nd)


def _rows(tm, cols):
    return pl.BlockSpec((tm, cols), lambda i: (i, 0))


def _params(sem=("arbitrary",)):
    return pltpu.CompilerParams(dimension_semantics=sem, vmem_limit_bytes=VMEM_LIMIT)


def _spatial_mix(ws_ref, bsp_ref, vn, s_scr, n_chunks):
    tri = (lax.broadcasted_iota(jnp.int32, (CHUNK, CHUNK), 0)
           >= lax.broadcasted_iota(jnp.int32, (CHUNK, CHUNK), 1))
    for g in range(A_GROUPS):
        wsg = jnp.where(tri, ws_ref[g], 0.0).astype(BF16)
        cols = slice(g * A_GROUP_DIM, (g + 1) * A_GROUP_DIM)
        for ci in range(n_chunks):
            rows = slice(ci * CHUNK, (ci + 1) * CHUNK)
            s_scr[rows, cols] = _dot(wsg, vn[rows, cols]) + bsp_ref[:, g:g + 1]


def _layer_a_fwd(x, w_in, p_own, lng, lnb, ws, bsp_t, after, own=None):
    t_len = x.shape[0]
    tm = TM_ATTN
    shard_cols = 3 * A_WIDTH // N_CHIPS

    def body(x_ref, win_ref, *rest):
        pown_ref = rest[0] if own is not None else None
        (lng_ref, lnb_ref, ws_ref, bsp_ref, after_ref,
         xt_ref, u_ref, vh_ref, z_ref, rv_ref, y_ref, s_scr) = rest[0 if own is None else 1:]
        x_t = x_ref[...]
        xb = x_t.astype(BF16)
        xt_ref[...] = x_t.T.astype(BF16)
        if own is None:
            u = _dot(xb, win_ref[:, 0:A_WIDTH])
            v = _dot(xb, win_ref[:, A_WIDTH:2 * A_WIDTH])
            z = _dot(xb, win_ref[:, 2 * A_WIDTH:3 * A_WIDTH])
        else:
            pieces = {s: pown_ref[...] if s == own else _dot(xb, win_ref[:, s * shard_cols:(s + 1) * shard_cols])
                      for s in (1, 2, 3, 0)}
            p = jnp.concatenate([pieces[s] for s in range(N_CHIPS)], axis=1)
            u, v, z = p[:, 0:A_WIDTH], p[:, A_WIDTH:2 * A_WIDTH], p[:, 2 * A_WIDTH:3 * A_WIDTH]
        vh, rv = _ln_fwd(v)
        vn = (vh * lng_ref[...] + lnb_ref[...]).astype(BF16)
        _spatial_mix(ws_ref, bsp_ref, vn, s_scr, tm // CHUNK)
        sz, _ = _silu_parts(z)
        y_ref[...] = (u * s_scr[...] * sz).astype(BF16)
        u_ref[...] = u.astype(BF16)
        vh_ref[...] = vh.astype(BF16)
        z_ref[...] = z.astype(BF16)
        rv_ref[...] = rv

    wide = jax.ShapeDtypeStruct((t_len, A_WIDTH), BF16)
    product = [] if own is None else [p_own]
    return pl.pallas_call(
        body, name="layer_a_fwd" if own is None else "layer_a_fwd_own%d" % own, grid=(t_len // tm,),
        in_specs=[_rows(tm, D_MODEL), _resident(w_in.shape)] + [_rows(tm, shard_cols)] * len(product)
        + [_const(lng.shape), _const(lnb.shape), _const(ws.shape), _const(bsp_t.shape), _const(after.shape)],
        out_specs=[pl.BlockSpec((D_MODEL, tm), lambda i: (0, i)), _rows(tm, A_WIDTH), _rows(tm, A_WIDTH),
                   _rows(tm, A_WIDTH), _rows(tm, 1), _rows(tm, A_WIDTH)],
        out_shape=[jax.ShapeDtypeStruct((D_MODEL, t_len), BF16), wide, wide, wide,
                   jax.ShapeDtypeStruct((t_len, 1), F32), wide],
        scratch_shapes=[pltpu.VMEM((tm, A_WIDTH), F32)],
        compiler_params=_params(),
    )(x, w_in, *product, lng, lnb, ws, bsp_t, after)


def _layer_b_proj(x, y, w_out_a, g1, b1, w_in, w_kv):
    t_len = x.shape[0]
    tm = 2 * TM_MM

    def body(x_ref, y_ref, wout_ref, g_ref, b_ref, win_ref, wkv_ref, xh_ref, r1_ref, q_ref, z_ref, kd_ref, vd_ref):
        halves = [slice(k * TM_MM, (k + 1) * TM_MM) for k in range(2)]
        projected = [_dot(y_ref[rows, :], wout_ref[...]) for rows in halves]
        for rows, out_a in zip(halves, projected):
            xh, r1 = _ln_fwd(ALPHA * x_ref[rows, :] + out_a)
            xh_ref[rows, :] = xh
            r1_ref[rows, :] = r1
            h1 = (xh * g_ref[...] + b_ref[...]).astype(BF16)
            q_ref[rows, :] = (_dot(h1, win_ref[:, 0:B_WIDTH]) * SCALE).astype(BF16)
            z_ref[rows, :] = _dot(h1, win_ref[:, B_WIDTH:2 * B_WIDTH]).astype(BF16)
            kv = _dot(h1, wkv_ref[...])
            k0, k1 = _dup_halves(kv[:, 0:PAIR])
            v0, v1 = _dup_halves(kv[:, PAIR:2 * PAIR])
            kd_ref[rows, 0:PAIR] = k0.astype(BF16)
            kd_ref[rows, PAIR:2 * PAIR] = k1.astype(BF16)
            vd_ref[rows, 0:PAIR] = v0.astype(BF16)
            vd_ref[rows, PAIR:2 * PAIR] = v1.astype(BF16)

    return pl.pallas_call(
        body, name="layer_b_proj", grid=(t_len // tm,),
        in_specs=[_rows(tm, D_MODEL), _rows(tm, A_WIDTH), _resident(w_out_a.shape), _const(g1.shape),
                  _const(b1.shape), _resident(w_in.shape), _resident(w_kv.shape)],
        out_specs=[_rows(tm, D_MODEL), _rows(tm, 1), _rows(tm, B_WIDTH), _rows(tm, B_WIDTH), _rows(tm, 2 * PAIR),
                   _rows(tm, 2 * PAIR)],
        out_shape=[jax.ShapeDtypeStruct((t_len, D_MODEL), F32), jax.ShapeDtypeStruct((t_len, 1), F32),
                   jax.ShapeDtypeStruct((t_len, B_WIDTH), BF16), jax.ShapeDtypeStruct((t_len, B_WIDTH), BF16),
                   jax.ShapeDtypeStruct((t_len, 2 * PAIR), BF16), jax.ShapeDtypeStruct((t_len, 2 * PAIR), BF16)],
        compiler_params=_params(),
    )(x, y, w_out_a, g1, b1, w_in, w_kv)


GROUP = N_HEADS // N_KV
GROUP_Q = GROUP * CHUNK


def _window_tables():
    j = jnp.arange(2 * CHUNK, dtype=jnp.int32)[:, None]
    t = jnp.arange(CHUNK, dtype=jnp.int32)[None, :]
    dist = t + CHUNK - j
    inside = (dist >= 0) & (dist < CHUNK)
    return jnp.stack([inside & (j >= CHUNK), inside]).astype(F32)


def _band(ref, chunk_index, kvh):
    prev0 = pl.multiple_of(jnp.maximum(chunk_index - 1, 0) * CHUNK, CHUNK)
    cur0 = pl.multiple_of(chunk_index * CHUNK, CHUNK)
    cols = slice(kvh * PAIR, (kvh + 1) * PAIR)
    return jnp.concatenate([ref[pl.ds(prev0, CHUNK), cols], ref[pl.ds(cur0, CHUNK), cols]], axis=0)


def _group_tables(bias_ref, win_ref, sink_ref, chunk_index, kvh):
    bias = jnp.concatenate([bias_ref[kvh * GROUP + j] for j in range(GROUP)], axis=1)
    win = win_ref[jnp.minimum(chunk_index, 1)]
    mask = jnp.concatenate([win] * GROUP, axis=1) > 0.5
    sink = jnp.concatenate([jnp.full((1, CHUNK), sink_ref[0, kvh * GROUP + j], F32) for j in range(GROUP)], axis=1)
    return bias, mask, sink


def _attn_probs(qs, kband, bias, mask, sink):
    logits = jnp.where(mask, _dot_nt(kband, qs) + bias, NEG_INF)
    m = jnp.maximum(jnp.max(logits, axis=0, keepdims=True), sink)
    e = jnp.exp(logits - m)
    es = jnp.exp(sink - m)
    inv = 1.0 / (jnp.sum(e, axis=0, keepdims=True) + es)
    return e * inv, es * inv


def _half_mask():
    return lax.broadcasted_iota(jnp.int32, (CHUNK, PAIR), 1) < HEAD_DIM


def _stack_heads(src_ref, rows, kvh, dst_scr, lo):
    for j in range(GROUP):
        h = kvh * GROUP + j
        blk = src_ref[rows, (h // 2) * PAIR:(h // 2 + 1) * PAIR].astype(F32)
        keep = lo if h % 2 == 0 else ~lo
        dst_scr[j * CHUNK:(j + 1) * CHUNK, :] = jnp.where(keep, blk, 0.0).astype(BF16)


def _probs_spec(tm):
    return pl.BlockSpec((tm // CHUNK, N_KV, 2 * CHUNK, GROUP_Q), lambda i: (i, 0, 0, 0))


def _sink_probs_spec(tiles=1):
    return pl.BlockSpec((tiles, 8, GROUP_Q), lambda i: (i, 0, 0))


def _unstack_pairs(stacked, pp, lo):
    return jnp.where(lo, stacked[(2 * pp) * CHUNK:(2 * pp + 1) * CHUNK], stacked[(2 * pp + 1) * CHUNK:(2 * pp + 2) * CHUNK])


def _layer_b_fwd(q, zb, kd, vd, bias, win, sinks, xh1, g1, b1, w_out, g2, b2, tgt):
    t_len = q.shape[0]
    tm = 2 * TM_ATTN

    def body(q_ref, z_ref, kd_ref, vd_ref, bias_ref, win_ref, sink_ref, xh_ref, g1_ref, b1_ref, wout_ref, g2_ref,
             b2_ref, tgt_ref, o_ref, p_ref, ps_ref, dr_ref, loss_ref, dg_ref, db_ref, o_scr, qs_scr):
        i = pl.program_id(0)

        @pl.when(i == 0)
        def _():
            loss_ref[...] = jnp.zeros_like(loss_ref)
            dg_ref[...] = jnp.zeros_like(dg_ref)
            db_ref[...] = jnp.zeros_like(db_ref)

        lo = _half_mask()
        ps_ref[...] = jnp.zeros_like(ps_ref)
        per_part = TM_ATTN // CHUNK
        for part in range(tm // TM_ATTN):
            part_rows = slice(part * TM_ATTN, (part + 1) * TM_ATTN)
            for cp in range(per_part):
                ci = part * per_part + cp
                cg = i * (tm // CHUNK) + ci
                rows = slice(ci * CHUNK, (ci + 1) * CHUNK)
                for kvh in range(N_KV):
                    kband = _band(kd_ref, cg, kvh)
                    vband = _band(vd_ref, cg, kvh)
                    bias_g, mask, sink = _group_tables(bias_ref, win_ref, sink_ref, cg, kvh)
                    _stack_heads(q_ref, rows, kvh, qs_scr, lo)
                    p, p_sink = _attn_probs(qs_scr[...], kband, bias_g, mask, sink)
                    p = p.astype(BF16)
                    p_ref[ci, kvh] = p
                    ps_ref[part, cp * N_KV + kvh:cp * N_KV + kvh + 1, :] = p_sink
                    o_stack = _dot_tn(p, vband)
                    for pp in range(GROUP // 2):
                        pair = kvh * (GROUP // 2) + pp
                        o_scr[rows, pair * PAIR:(pair + 1) * PAIR] = _unstack_pairs(o_stack, pp, lo)
            o = o_scr[part_rows, :]
            o_ref[part_rows, :] = o.astype(BF16)
            sz, _ = _silu_parts(z_ref[part_rows, :].astype(F32))
            y = (o * sz).astype(BF16)
            h1 = xh_ref[part_rows, :] * g1_ref[...] + b1_ref[...]
            r = ALPHA * h1 + _dot(y, wout_ref[...])
            xh2, rstd2 = _ln_fwd(r)
            diff = xh2 * g2_ref[...] + b2_ref[...] - tgt_ref[part_rows, :]
            loss_ref[...] += jnp.sum(diff * diff, axis=0, keepdims=True)
            dh2 = diff * (1.0 / D_MODEL)
            dg_ref[...] += jnp.sum(dh2 * xh2, axis=0, keepdims=True)
            db_ref[...] += jnp.sum(dh2, axis=0, keepdims=True)
            dr_ref[part_rows, :] = _ln_bwd(dh2 * g2_ref[...], xh2, rstd2)

    vec = jax.ShapeDtypeStruct((1, D_MODEL), F32)
    return pl.pallas_call(
        body, name="layer_b_fwd", grid=(t_len // tm,),
        in_specs=[_rows(tm, B_WIDTH), _rows(tm, B_WIDTH), _resident(kd.shape), _resident(vd.shape),
                  _resident(bias.shape), _resident(win.shape), pl.BlockSpec(memory_space=pltpu.SMEM),
                  _rows(tm, D_MODEL), _const(g1.shape), _const(b1.shape), _resident(w_out.shape), _const(g2.shape),
                  _const(b2.shape), _rows(tm, D_MODEL)],
        out_specs=[_rows(tm, B_WIDTH), _probs_spec(tm), _sink_probs_spec(tm // TM_ATTN), _rows(tm, D_MODEL)]
        + [_const((1, D_MODEL))] * 3,
        out_shape=[jax.ShapeDtypeStruct((t_len, B_WIDTH), BF16),
                   jax.ShapeDtypeStruct((t_len // CHUNK, N_KV, 2 * CHUNK, GROUP_Q), BF16),
                   jax.ShapeDtypeStruct((t_len // TM_ATTN, 8, GROUP_Q), F32),
                   jax.ShapeDtypeStruct((t_len, D_MODEL), F32), vec, vec, vec],
        scratch_shapes=[pltpu.VMEM((tm, B_WIDTH), F32), pltpu.VMEM((GROUP_Q, PAIR), BF16)],
        compiler_params=_params(),
    )(q, zb, kd, vd, bias, win, sinks, xh1, g1, b1, w_out, g2, b2, tgt)


def _layer_b_bwd_attn(dr2, zb, o, q, kd, vd, probs, sink_probs, w_out):
    t_len = q.shape[0]
    tm = TM_BWD_ATTN
    n_steps = t_len // tm
    n_chunks = tm // CHUNK
    per_part = TM_ATTN // CHUNK

    def body(dr_ref, z_ref, o_ref, q_ref, kd_ref, vd_ref, p_ref, ps_ref, wout_ref,
             dq_ref, dz_ref, dkd_ref, dvd_ref, ck_ref, cv_ref, gw_ref, dsink_ref, dbias_ref,
             do_scr, qs_scr, dos_scr, gw_acc):
        i = pl.program_id(0)

        @pl.when(i == 0)
        def _():
            gw_acc[...] = jnp.zeros_like(gw_acc)
            dsink_ref[...] = jnp.zeros_like(dsink_ref)
            dbias_ref[...] = jnp.zeros_like(dbias_ref)

        drb = dr_ref[...].astype(BF16)
        per_kvh = 2
        n_blocks = N_KV * per_kvh
        block_cols = B_WIDTH // n_blocks

        def through_gate(b):
            cols = slice(b * block_cols, (b + 1) * block_cols)
            dy = _dot_nt(drb, wout_ref[cols, :])
            sz, dsz = _silu_parts(z_ref[:, cols].astype(F32))
            o_t = o_ref[:, cols].astype(F32)
            dz_ref[:, cols] = (dy * o_t * dsz).astype(BF16)
            do_scr[:, cols] = (dy * sz).astype(BF16)
            return (o_t * sz).astype(BF16)

        def weight_gradient(b, gated):
            cols = slice(b * block_cols, (b + 1) * block_cols)
            gw_acc[cols, :] += _dot_tn(gated, drb)

        gated = {b: through_gate(b) for b in range(per_kvh)}

        lo = _half_mask()
        for kvh in range(N_KV):
            kcols = slice(kvh * PAIR, (kvh + 1) * PAIR)
            dk_bands, dv_bands = [], []
            for ci in range(n_chunks):
                unit = kvh * n_chunks + ci
                if ci < per_kvh and kvh + 1 < N_KV:
                    gated[(kvh + 1) * per_kvh + ci] = through_gate((kvh + 1) * per_kvh + ci)
                if unit in gated:
                    weight_gradient(unit, gated.pop(unit))
                cg = i * n_chunks + ci
                rows = slice(ci * CHUNK, (ci + 1) * CHUNK)
                kband = _band(kd_ref, cg, kvh)
                vband = _band(vd_ref, cg, kvh)
                _stack_heads(q_ref, rows, kvh, qs_scr, lo)
                _stack_heads(do_scr, rows, kvh, dos_scr, lo)
                qs = qs_scr[...]
                dos = dos_scr[...]
                pb = p_ref[ci, kvh]
                p = pb.astype(F32)
                sink_row = (ci % per_part) * N_KV + kvh
                p_sink = ps_ref[ci // per_part, sink_row:sink_row + 1, :]
                dp = _dot_nt(vband, dos)
                delta = jnp.sum(p * dp, axis=0, keepdims=True)
                dlog = p * (dp - delta)
                for j in range(GROUP):
                    dbias_ref[kvh * GROUP + j] += dlog[:, j * CHUNK:(j + 1) * CHUNK]
                dsink_ref[kvh:kvh + 1, :] += -(p_sink * delta)
                ds = dlog.astype(BF16)
                dq_stack = _dot_tn(ds, kband) * SCALE
                for pp in range(GROUP // 2):
                    pair = kvh * (GROUP // 2) + pp
                    dq_ref[rows, pair * PAIR:(pair + 1) * PAIR] = _unstack_pairs(dq_stack, pp, lo).astype(BF16)
                dk_bands.append(_dot(ds, qs))
                dv_bands.append(_dot(pb, dos))
            for bands, out_ref, carry_ref in ((dk_bands, dkd_ref, ck_ref), (dv_bands, dvd_ref, cv_ref)):
                carry_ref[0, :, kcols] = bands[0][0:CHUNK]
                for ci in range(n_chunks):
                    own = bands[ci][CHUNK:2 * CHUNK]
                    if ci + 1 < n_chunks:
                        own = own + bands[ci + 1][0:CHUNK]
                    out_ref[ci * CHUNK:(ci + 1) * CHUNK, kcols] = own

        @pl.when(i == n_steps - 1)
        def _():
            gw_ref[...] = gw_acc[...].astype(BF16)

    carry_spec = pl.BlockSpec((1, CHUNK, 2 * PAIR), lambda i: (i, 0, 0))
    carry_shape = jax.ShapeDtypeStruct((n_steps, CHUNK, 2 * PAIR), F32)
    bias_shape = (N_HEADS, 2 * CHUNK, CHUNK)
    return pl.pallas_call(
        body, name="layer_b_bwd_attn", grid=(n_steps,),
        in_specs=[_rows(tm, D_MODEL), _rows(tm, B_WIDTH), _rows(tm, B_WIDTH), _rows(tm, B_WIDTH),
                  _resident(kd.shape), _resident(vd.shape), _probs_spec(tm), _sink_probs_spec(tm // TM_ATTN),
                  _resident(w_out.shape)],
        out_specs=[_rows(tm, B_WIDTH), _rows(tm, B_WIDTH), _rows(tm, 2 * PAIR), _rows(tm, 2 * PAIR),
                   carry_spec, carry_spec, _const(w_out.shape), _const((N_KV, GROUP_Q)), _const(bias_shape)],
        out_shape=[jax.ShapeDtypeStruct((t_len, B_WIDTH), BF16), jax.ShapeDtypeStruct((t_len, B_WIDTH), BF16),
                   jax.ShapeDtypeStruct((t_len, 2 * PAIR), F32), jax.ShapeDtypeStruct((t_len, 2 * PAIR), F32),
                   carry_shape, carry_shape, jax.ShapeDtypeStruct(w_out.shape, BF16),
                   jax.ShapeDtypeStruct((N_KV, GROUP_Q), F32), jax.ShapeDtypeStruct(bias_shape, F32)],
        scratch_shapes=[pltpu.VMEM((tm, B_WIDTH), BF16), pltpu.VMEM((GROUP_Q, PAIR), BF16),
                        pltpu.VMEM((GROUP_Q, PAIR), BF16), pltpu.VMEM(w_out.shape, F32)],
        compiler_params=_params(),
    )(dr2, zb, o, q, kd, vd, probs, sink_probs, w_out)


def _layer_b_bwd_proj(xh1, rstd1, g1, b1, dr2, dq, dzb, dkd, dvd, carry_k, carry_v, w_in, w_kv):
    t_len = xh1.shape[0]
    tm = TM_MM
    n_steps = t_len // tm
    per_tile = tm // TM_BWD_ATTN
    n_carry = carry_k.shape[0]

    def body(xh_ref, xhp_ref, rstdp_ref, g_ref, b_ref, dr2_ref, dq_ref, dz_ref, dkd_ref, dvd_ref, *rest):
        carry_refs = rest[:2 * per_tile]
        (win_ref, wkv_ref, dr1_ref, dg_ref, db_ref, gwin_ref, gwkv_ref,
         acc_in, acc_kv, dh1_scr) = rest[2 * per_tile:]
        i = pl.program_id(0)

        @pl.when(i == 0)
        def _():
            acc_in[...] = jnp.zeros_like(acc_in)
            acc_kv[...] = jnp.zeros_like(acc_kv)
            dg_ref[...] = jnp.zeros_like(dg_ref)
            db_ref[...] = jnp.zeros_like(db_ref)
            dh1_scr[...] = jnp.zeros_like(dh1_scr)

        def previous_tile_layer_norm(dh1_prev):
            xh_prev = xhp_ref[...]
            dg_ref[...] += jnp.sum(dh1_prev * xh_prev, axis=0, keepdims=True)
            db_ref[...] += jnp.sum(dh1_prev, axis=0, keepdims=True)
            dr1_ref[...] = _ln_bwd(dh1_prev * g_ref[...], xh_prev, rstdp_ref[...])

        lo = lax.broadcasted_iota(jnp.int32, (tm, PAIR), 1) < HEAD_DIM

        def heads_gradient(tile_ref, refs):
            parts = []
            for a in range(per_tile):
                parts.append(tile_ref[a * TM_BWD_ATTN:(a + 1) * TM_BWD_ATTN - CHUNK, :])
                carry = refs[a][0]
                if a == per_tile - 1:
                    carry = jnp.where(i < n_steps - 1, carry, 0.0)
                parts.append(tile_ref[(a + 1) * TM_BWD_ATTN - CHUNK:(a + 1) * TM_BWD_ATTN, :] + carry)
            dup = jnp.concatenate(parts, axis=0)
            return jnp.where(lo, _fold_halves(dup[:, 0:PAIR]), _fold_halves(dup[:, PAIR:2 * PAIR]))

        @pl.when(i < n_steps)
        def _():
            dh1_prev = dh1_scr[...]
            h1 = (xh_ref[...] * g_ref[...] + b_ref[...]).astype(BF16)
            dq_t = dq_ref[...]
            dz_t = dz_ref[...]
            dkv = jnp.concatenate([heads_gradient(dkd_ref, carry_refs[:per_tile]),
                                   heads_gradient(dvd_ref, carry_refs[per_tile:])], axis=1).astype(BF16)
            dh1 = ALPHA * dr2_ref[...]
            dh1 += _dot_nt(dq_t, win_ref[:, 0:B_WIDTH])
            dh1 += _dot_nt(dz_t, win_ref[:, B_WIDTH:2 * B_WIDTH])
            dh1 += _dot_nt(dkv, wkv_ref[...])
            acc_in[:, 0:B_WIDTH] += _dot_tn(h1, dq_t)
            acc_in[:, B_WIDTH:2 * B_WIDTH] += _dot_tn(h1, dz_t)
            acc_kv[...] += _dot_tn(h1, dkv)
            dh1_scr[...] = dh1
            previous_tile_layer_norm(dh1_prev)

        @pl.when(i == n_steps)
        def _():
            previous_tile_layer_norm(dh1_scr[...])
            half_rows = D_MODEL // 2
            shard_cols = 2 * B_WIDTH // N_CHIPS
            for s in range(N_CHIPS):
                for c in range(2):
                    gwin_ref[2 * s + c] = acc_in[c * half_rows:(c + 1) * half_rows,
                                                 s * shard_cols:(s + 1) * shard_cols].astype(BF16)
            gwkv_ref[...] = acc_kv[...].astype(BF16)

    vec = jax.ShapeDtypeStruct((1, D_MODEL), F32)
    gwin_shape = (N_DEV, D_MODEL // 2, 2 * B_WIDTH // N_CHIPS)

    def carry_spec(a):
        return pl.BlockSpec((1, CHUNK, 2 * PAIR), lambda i: (jnp.minimum(per_tile * i + a + 1, n_carry - 1), 0, 0))

    carry_specs = [carry_spec(a) for a in range(per_tile)]

    def this_tile(cols):
        return pl.BlockSpec((tm, cols), lambda i: (jnp.minimum(i, n_steps - 1), 0))

    def previous_tile(cols):
        return pl.BlockSpec((tm, cols), lambda i: (jnp.maximum(i - 1, 0), 0))

    return pl.pallas_call(
        body, name="layer_b_bwd_proj", grid=(n_steps + 1,),
        in_specs=[this_tile(D_MODEL), previous_tile(D_MODEL), previous_tile(1), _const(g1.shape), _const(b1.shape),
                  this_tile(D_MODEL), this_tile(B_WIDTH), this_tile(B_WIDTH), this_tile(2 * PAIR),
                  this_tile(2 * PAIR)]
        + carry_specs + carry_specs + [_resident(w_in.shape), _resident(w_kv.shape)],
        out_specs=[previous_tile(D_MODEL), _const((1, D_MODEL)), _const((1, D_MODEL)), _const(gwin_shape),
                   _const(w_kv.shape)],
        out_shape=[jax.ShapeDtypeStruct((t_len, D_MODEL), F32), vec, vec,
                   jax.ShapeDtypeStruct(gwin_shape, BF16), jax.ShapeDtypeStruct(w_kv.shape, BF16)],
        scratch_shapes=[pltpu.VMEM(w_in.shape, F32), pltpu.VMEM(w_kv.shape, F32), pltpu.VMEM((tm, D_MODEL), F32)],
        compiler_params=_params(),
    )(xh1, xh1, rstd1, g1, b1, dr2, dq, dzb, dkd, dvd, *([carry_k] * per_tile), *([carry_v] * per_tile), w_in,
      w_kv)


def _layer_a_bwd_mix(dr1, u, vh, z, y, rv, w_out, lng, lnb, ws, bsp_t, after):
    t_len = u.shape[0]
    tm = TM_ATTN
    n_steps = t_len // tm

    def body(dr_ref, u_ref, vh_ref, z_ref, y_ref, rv_ref, wout_ref, lng_ref, lnb_ref, ws_ref, bsp_ref, after_ref,
             dp_ref, gw_ref, dws_ref, dbsp_ref, dgs_ref, dbs_ref, dvn_scr, gw_acc):
        i = pl.program_id(0)

        @pl.when(i == 0)
        def _():
            gw_acc[...] = jnp.zeros_like(gw_acc)
            dws_ref[...] = jnp.zeros_like(dws_ref)
            dbsp_ref[...] = jnp.zeros_like(dbsp_ref)
            dgs_ref[...] = jnp.zeros_like(dgs_ref)
            dbs_ref[...] = jnp.zeros_like(dbs_ref)

        drb = dr_ref[...].astype(BF16)

        def group_cols(g):
            return slice(g * A_GROUP_DIM, (g + 1) * A_GROUP_DIM)

        tri = (lax.broadcasted_iota(jnp.int32, (CHUNK, CHUNK), 0)
               >= lax.broadcasted_iota(jnp.int32, (CHUNK, CHUNK), 1))
        lane = lax.broadcasted_iota(jnp.int32, (CHUNK, CHUNK), 1)
        ones = jnp.ones((CHUNK, A_GROUP_DIM), BF16)
        dbsp = jnp.zeros((CHUNK, CHUNK), F32)
        dy_next = _dot_nt(drb, wout_ref[group_cols(0), :])
        for g in range(A_GROUPS):
            wsg = jnp.where(tri, ws_ref[g], 0.0).astype(BF16)
            cols = group_cols(g)
            cols_z = slice(2 * A_WIDTH + g * A_GROUP_DIM, 2 * A_WIDTH + (g + 1) * A_GROUP_DIM)
            dy_g = dy_next
            if g + 1 < A_GROUPS:
                dy_next = _dot_nt(drb, wout_ref[group_cols(g + 1), :])
            gw_acc[cols, :] += _dot_tn(y_ref[:, cols], drb)
            both = jnp.zeros((CHUNK, 2 * CHUNK), F32)
            for ci in range(tm // CHUNK):
                rows = slice(ci * CHUNK, (ci + 1) * CHUNK)
                vn = (vh_ref[rows, cols].astype(F32) * lng_ref[:, cols] + lnb_ref[:, cols]).astype(BF16)
                s = _dot(wsg, vn) + bsp_ref[:, g:g + 1]
                sz, dsz = _silu_parts(z_ref[rows, cols].astype(F32))
                dy = dy_g[rows]
                t = dy * u_ref[rows, cols].astype(F32)
                dp_ref[rows, cols] = (dy * (s * sz)).astype(BF16)
                dp_ref[rows, cols_z] = (t * s * dsz).astype(BF16)
                ds_b = (t * sz).astype(BF16)
                both += _dot_nt(ds_b, jnp.concatenate([vn, ones], axis=0))
                dvn_scr[rows, cols] = _dot_tn(wsg, ds_b)
            dws_ref[g] += jnp.where(tri, both[:, 0:CHUNK], 0.0)
            dbsp = jnp.where(lane == g, both[:, CHUNK:2 * CHUNK], dbsp)
        dbsp_ref[...] += dbsp
        dvn = dvn_scr[...]
        vh_t = vh_ref[...].astype(F32)
        dgs_ref[...] += jnp.sum(dvn * vh_t, axis=0, keepdims=True)
        dbs_ref[...] += jnp.sum(dvn, axis=0, keepdims=True)
        dp_ref[:, A_WIDTH:2 * A_WIDTH] = _ln_bwd(dvn * lng_ref[...], vh_t, rv_ref[...]).astype(BF16)

        @pl.when(i == n_steps - 1)
        def _():
            gw_ref[...] = gw_acc[...].astype(BF16)

    wide = jax.ShapeDtypeStruct((1, A_WIDTH), F32)
    return pl.pallas_call(
        body, name="layer_a_bwd_mix", grid=(n_steps,),
        in_specs=[_rows(tm, D_MODEL), _rows(tm, A_WIDTH), _rows(tm, A_WIDTH), _rows(tm, A_WIDTH), _rows(tm, A_WIDTH),
                  _rows(tm, 1), _resident(w_out.shape), _const(lng.shape), _const(lnb.shape), _const(ws.shape),
                  _const(bsp_t.shape), _const(after.shape)],
        out_specs=[_rows(tm, 3 * A_WIDTH), _const(w_out.shape), _const(ws.shape), _const((CHUNK, CHUNK)),
                   _const((1, A_WIDTH)), _const((1, A_WIDTH))],
        out_shape=[jax.ShapeDtypeStruct((t_len, 3 * A_WIDTH), BF16), jax.ShapeDtypeStruct(w_out.shape, BF16),
                   jax.ShapeDtypeStruct(ws.shape, F32), jax.ShapeDtypeStruct((CHUNK, CHUNK), F32),
                   wide, wide],
        scratch_shapes=[pltpu.VMEM((tm, A_WIDTH), F32), pltpu.VMEM(w_out.shape, F32)],
        compiler_params=_params(),
    )(dr1, u, vh, z, y, rv, w_out, lng, lnb, ws, bsp_t, after)


def _layer_a_bwd_dx(dr1, dp, w_in, after, updates=()):
    t_len = dr1.shape[0]
    tm = TM_MM
    n_steps = t_len // tm
    n_upd = len(updates)

    def body(dr_ref, dp_ref, win_ref, after_ref, *refs):
        upd_in, dx_ref, upd_out = refs[:4 * n_upd], refs[4 * n_upd], refs[4 * n_upd + 1:]
        dx_ref[...] = ALPHA * dr_ref[...] + _dot_nt(dp_ref[...], win_ref[...])
        for k in range(n_upd):
            w_ref, g_ref, m_ref, v_ref = upd_in[4 * k:4 * k + 4]
            g_out, d_ref, nm_ref, nv_ref = upd_out[4 * k:4 * k + 4]
            g_out[...] = g_ref[...]
            _adamw_update(w_ref, g_ref, m_ref, v_ref, d_ref, nm_ref, nv_ref)

    upd_specs, upd_shapes, upd_args = [], [], []
    for w, g, m, v in updates:
        rows, cols = w.shape
        upd_specs.append(pl.BlockSpec((rows // n_steps, cols), lambda i: (i, 0)))
        upd_shapes.append(jax.ShapeDtypeStruct((rows, cols), F32))
        upd_args += [w, g, m, v]
    return pl.pallas_call(
        body, name="layer_a_bwd_dx", grid=(n_steps,),
        in_specs=[_rows(tm, D_MODEL), _rows(tm, 3 * A_WIDTH), _resident(w_in.shape), _const(after.shape)]
        + [s for s in upd_specs for _ in range(4)],
        out_specs=[_rows(tm, D_MODEL)] + [s for s in upd_specs for _ in range(4)],
        out_shape=[jax.ShapeDtypeStruct((t_len, D_MODEL), F32)] + [s for s in upd_shapes for _ in range(4)],
        compiler_params=_params(),
    )(dr1, dp, w_in, after, *upd_args)


def _layer_a_bwd_win(xt, dp, after):
    t_len = xt.shape[1]
    tm = TM_WIN
    n_steps = t_len // tm
    shard_cols = 3 * A_WIDTH // N_CHIPS
    half_rows = D_MODEL // 2

    def body(xt_ref, dp_ref, after_ref, gw_ref, acc):
        i = pl.program_id(1)

        @pl.when(i == 0)
        def _():
            acc[...] = jnp.zeros_like(acc)

        acc[...] += _dot(xt_ref[...], dp_ref[...])

        @pl.when(i == n_steps - 1)
        def _():
            for c in range(2):
                gw_ref[0, c] = acc[c * half_rows:(c + 1) * half_rows, :].astype(BF16)

    return pl.pallas_call(
        body, name="layer_a_bwd_win", grid=(N_CHIPS, n_steps),
        in_specs=[pl.BlockSpec((D_MODEL, tm), lambda j, i: (0, i)),
                  pl.BlockSpec((tm, shard_cols), lambda j, i: (i, j)), _const(after.shape)],
        out_specs=pl.BlockSpec((1, 2, half_rows, shard_cols), lambda j, i: (j, 0, 0, 0)),
        out_shape=jax.ShapeDtypeStruct((N_CHIPS, 2, half_rows, shard_cols), BF16),
        scratch_shapes=[pltpu.VMEM((D_MODEL, shard_cols), F32)],
        compiler_params=_params(("arbitrary", "arbitrary")),
    )(xt, dp, after)


def _bucket_onehot():
    dist = jnp.arange(CHUNK, dtype=jnp.int32)[None, :]
    max_exact = REL_BUCKETS // 2
    df = jnp.maximum(dist, 1).astype(F32)
    large = max_exact + (jnp.log(df / max_exact) / math.log(CHUNK / max_exact)
                         * (REL_BUCKETS - max_exact)).astype(jnp.int32)
    bucket = jnp.where(dist < max_exact, dist, jnp.minimum(large, REL_BUCKETS - 1))
    onehot = bucket == jnp.arange(REL_BUCKETS, dtype=jnp.int32)[:, None]
    return onehot.astype(F32)


def _bias_expand(rel_t, onehot):
    def body(rel_ref, oh_ref, out_ref):
        by_distance = jnp.dot(rel_ref[...], oh_ref[...], preferred_element_type=F32,
                              precision=lax.Precision.HIGHEST)
        for h in range(N_HEADS):
            rows = jnp.broadcast_to(by_distance[h:h + 1, :], (2 * CHUNK, CHUNK))
            out_ref[h] = pltpu.roll(rows, 0, 1, stride=1, stride_axis=0)

    return pl.pallas_call(
        body, name="bias_expand",
        out_shape=jax.ShapeDtypeStruct((N_HEADS, 2 * CHUNK, CHUNK), F32),
    )(rel_t, onehot)


def _bias_reduce(oh_ref, db_ref):
    sublane = lax.broadcasted_iota(jnp.int32, (_SUBLANES, CHUNK), 0)
    rows = []
    for h in range(N_HEADS):
        part = db_ref[h, 0:_SUBLANES, :]
        for a in range(1, 2 * CHUNK // _SUBLANES):
            tile = db_ref[h, a * _SUBLANES:(a + 1) * _SUBLANES, :]
            back = (-a * _SUBLANES) % CHUNK
            part += pltpu.roll(tile, back, 1) if back else tile
        total = jnp.where(sublane == 0, part, 0.0)
        for s in range(1, _SUBLANES):
            total += jnp.where(sublane == s, pltpu.roll(part, CHUNK - s, 1), 0.0)
        rows.append(jnp.sum(total, axis=0, keepdims=True))
    by_distance = jnp.concatenate(rows, axis=0)
    return lax.dot_general(by_distance, oh_ref[...], (((1,), (1,)), ((), ())),
                           preferred_element_type=F32, precision=lax.Precision.HIGHEST)


_SMALL_SHAPES = dict(w_spatial=(A_GROUPS, CHUNK, CHUNK), b_spatial=(A_GROUPS, CHUNK), attn_sinks=(1, N_HEADS),
                     rel_bias=(N_HEADS, REL_BUCKETS), post_ln_g=(2, D_MODEL), post_ln_b=(2, D_MODEL),
                     sgu_ln_g=(1, A_WIDTH), sgu_ln_b=(1, A_WIDTH), loss=(1, 1))
_SMALL_ORDER = tuple(_SMALL_SHAPES)


def _small_rows(name):
    shape = _SMALL_SHAPES[name]
    rows = math.prod(shape[:-1]) if shape[-1] < _LANES else math.prod(shape) // _LANES
    return -(-rows // _SUBLANES) * _SUBLANES


def _small_offset(name):
    return sum(_small_rows(n) for n in _SMALL_ORDER[:_SMALL_ORDER.index(name)])


def _pack_small(dws, dbsp, dsink, dbias, onehot, post_g, post_b, dgs, dbs, loss_vec):
    def body(dws_ref, dbsp_ref, dsink_ref, db_ref, oh_ref, g1_ref, g2_ref, b1_ref, b2_ref, dgs_ref, dbs_ref,
             loss_ref, out_ref):
        out_ref[...] = jnp.zeros_like(out_ref)

        def put_flat(name, refs):
            row = _small_offset(name)
            for ref in refs:
                for k in range(ref.shape[1] // _LANES):
                    out_ref[row:row + 1, :] = ref[:, k * _LANES:(k + 1) * _LANES]
                    row += 1

        row = _small_offset("w_spatial")
        for g in range(A_GROUPS):
            out_ref[row + g * CHUNK:row + (g + 1) * CHUNK, :] = dws_ref[g]
        row = _small_offset("b_spatial")
        out_ref[row:row + A_GROUPS, :] = dbsp_ref[...].T[0:A_GROUPS, :]
        lane = lax.broadcasted_iota(jnp.int32, (1, _LANES), 1)
        sinks = jnp.zeros((1, _LANES), F32)
        for h in range(N_HEADS):
            per_query = dsink_ref[h // GROUP:h // GROUP + 1, (h % GROUP) * CHUNK:(h % GROUP + 1) * CHUNK]
            sinks = jnp.where(lane == h, jnp.sum(per_query, axis=1, keepdims=True), sinks)
        row = _small_offset("attn_sinks")
        out_ref[row:row + 1, :] = sinks
        row = _small_offset("rel_bias")
        out_ref[row:row + N_HEADS, 0:REL_BUCKETS] = _bias_reduce(oh_ref, db_ref)
        put_flat("post_ln_g", [g1_ref, g2_ref])
        put_flat("post_ln_b", [b1_ref, b2_ref])
        put_flat("sgu_ln_g", [dgs_ref])
        put_flat("sgu_ln_b", [dbs_ref])
        row = _small_offset("loss")
        out_ref[row:row + 1, 0:1] = (0.5 / D_MODEL) * jnp.sum(loss_ref[...], axis=1, keepdims=True)

    total_rows = sum(_small_rows(n) for n in _SMALL_ORDER)
    return pl.pallas_call(
        body, name="pack_small",
        out_shape=jax.ShapeDtypeStruct((total_rows, _LANES), F32),
    )(dws, dbsp, dsink, dbias, onehot, *post_g, *post_b, dgs, dbs, loss_vec)


def _place():
    return lax.axis_index("x"), lax.axis_index("y"), lax.axis_index("c")


RELAY_PIECES = 4


def _shard_window(full_ref, shard_shape, col_sharded, s, half, piece=None):
    rows, cols = shard_shape
    if half is None:
        start, size = 0, rows
    elif piece is None:
        start, size = half * (rows // 2), rows // 2
    else:
        size = rows // 2 // RELAY_PIECES
        start = (half * RELAY_PIECES + piece) * size
    if col_sharded:
        return full_ref.at[pl.ds(start, size), pl.ds(s * cols, cols)]
    return full_ref.at[pl.ds(s * rows + start, size), :]


def _other_chips(x, y):
    return [(1 - x, y), (x, 1 - y), (1 - x, 1 - y)]


def _gather_weights(shards, col_sharded, fetch, ln_shard, tokens=None):
    n_w = len(shards)
    fetched = [w for w in range(n_w) if fetch[w]]
    full_shapes = []
    for w, cs in zip(shards, col_sharded):
        r, c = w.shape
        full_shapes.append((r, c * N_CHIPS) if cs else (r * N_CHIPS, c))

    n_tok = 0 if tokens is None else 1
    tok_tile = TM_MM
    n_tiles = 0 if tokens is None else tokens.shape[0] // tok_tile

    def body(*refs):
        refs = list(refs)
        in_refs = [refs.pop(0) for _ in range(n_w)]
        ln_ref = refs.pop(0)
        tok_ref = refs.pop(0) if n_tok else None
        full_refs = [refs.pop(0) for _ in range(n_w)]
        ln_full = refs.pop(0)
        prod_ref = refs.pop(0) if n_tok else None
        raw = [refs.pop(0) for _ in range(n_w)]
        stage = [refs.pop(0) for _ in range(n_w)]
        send_sems, recv_sems, load_sems, local_sems, ln_send, ln_recv = refs[:6]
        tok_buf, prod_buf, tok_sems, prod_sems = refs[6:] if n_tok else (None,) * 4
        x, y, c = _place()
        s_me = 2 * x + y
        chips = _other_chips(x, y)
        pieces = range(RELAY_PIECES)

        def shard_window(w, s, half, piece=None):
            return _shard_window(full_refs[w], shards[w].shape, col_sharded[w], s, half, piece)

        def piece_rows(w, half, piece):
            rows = shards[w].shape[0] // 2 // RELAY_PIECES
            return pl.ds(pl.multiple_of((half * RELAY_PIECES + piece) * rows, rows), rows)

        def ici_copy(w, k, sender_shard, piece):
            idx = (w * 3 + k) * RELAY_PIECES + piece
            return pltpu.make_async_remote_copy(
                src_ref=stage[w].at[piece_rows(w, c, piece), :], dst_ref=shard_window(w, sender_shard, c, piece),
                send_sem=send_sems.at[idx], recv_sem=recv_sems.at[idx],
                device_id=(*chips[k], c), device_id_type=MESH)

        def d2d_copy(w, k, half, piece):
            s_k = 2 * chips[k][0] + chips[k][1]
            win = shard_window(w, s_k, half, piece)
            idx = (3 * n_w + w * 3 + k) * RELAY_PIECES + piece
            return pltpu.make_async_remote_copy(
                src_ref=win, dst_ref=win, send_sem=send_sems.at[idx], recv_sem=recv_sems.at[idx],
                device_id=(x, y, 1 - c), device_id_type=MESH)

        def ln_copy(k, slot):
            return pltpu.make_async_remote_copy(
                src_ref=ln_ref, dst_ref=ln_full.at[slot], send_sem=ln_send.at[k], recv_sem=ln_recv.at[k],
                device_id=(*chips[k], c), device_id_type=MESH)

        loads = []

        def load(w, rows):
            window = (rows, slice(None)) if rows is not None else (slice(None), slice(None))
            cp = pltpu.make_async_copy(in_refs[w].at[window], raw[w].at[window], load_sems.at[len(loads)])
            cp.start()
            loads.append((cp, w, window))

        for half in (c, 1 - c):
            for w in fetched:
                for q in pieces:
                    load(w, piece_rows(w, half, q))
        for w in range(n_w):
            if not fetch[w]:
                load(w, None)

        def to_bf16(k):
            cp, w, window = loads[k]
            cp.wait()
            stage[w][window] = raw[w][window].astype(BF16)

        ln_full[s_me] = ln_ref[...]
        def shard_of(k):
            return 2 * chips[k][0] + chips[k][1]

        relay_from = jnp.where(c == 0, shard_of(0), shard_of(1))
        relay_to = (jnp.where(c == 0, x, 1 - x), jnp.where(c == 0, 1 - y, y), c)

        def relay_copy(w, sender_shard, piece):
            win = shard_window(w, sender_shard, c, piece)
            idx = (w * 3 + 2) * RELAY_PIECES + piece
            return pltpu.make_async_remote_copy(
                src_ref=win, dst_ref=win, send_sem=send_sems.at[idx], recv_sem=recv_sems.at[idx],
                device_id=relay_to, device_id_type=MESH)

        first = [ln_copy(k, s_me) for k in range(3)]
        for cp in first:
            cp.start()
        n_sent = 0
        for w in fetched:
            for q in pieces:
                to_bf16(n_sent)
                n_sent += 1
                for k in range(2):
                    cp = ici_copy(w, k, s_me, q)
                    cp.start()
                    first.append(cp)
        for k in range(n_sent, len(loads)):
            to_bf16(k)
        own = [pltpu.make_async_copy(stage[w], shard_window(w, s_me, None), local_sems.at[w]) for w in range(n_w)]
        for cp in own:
            cp.start()

        def tok_copy(t):
            return pltpu.make_async_copy(tok_ref.at[pl.ds(t * tok_tile, tok_tile), :], tok_buf.at[t % 2],
                                         tok_sems.at[t % 2])

        def prod_copy(t):
            return pltpu.make_async_copy(prod_buf.at[t % 2], prod_ref.at[pl.ds(t * tok_tile, tok_tile), :],
                                         prod_sems.at[t % 2])

        def product_tiles(tiles):
            for t in tiles:
                if t + 1 < n_tiles:
                    tok_copy(t + 1).start()
                tok_copy(t).wait()
                if t >= 2:
                    prod_copy(t - 2).wait()
                prod_buf[t % 2] = _dot(tok_buf[t % 2].astype(BF16), stage[0][...])
                prod_copy(t).start()

        if n_tiles:
            tok_copy(0).start()
        share = [4, 3, 3, 3, 3, 0, 0, 0]
        assert len(share) == 2 * RELAY_PIECES
        bounds = [sum(share[:k]) * n_tiles // sum(share) for k in range(len(share) + 1)]
        passed = []
        for w in fetched:
            for q in pieces:
                if w == fetched[0]:
                    product_tiles(range(bounds[q], bounds[q + 1]))
                for k in range(2):
                    ici_copy(w, k, shard_of(k), q).wait_recv()
                relay = relay_copy(w, relay_from, q)
                relay.start()
                passed.append(relay)
                for k in range(2):
                    fwd = d2d_copy(w, k, c, q)
                    fwd.start()
                    passed.append(fwd)
        for w in fetched:
            for q in pieces:
                if w == fetched[0]:
                    product_tiles(range(bounds[RELAY_PIECES + q], bounds[RELAY_PIECES + q + 1]))
                relay_copy(w, shard_of(2), q).wait_recv()
                fwd = d2d_copy(w, 2, c, q)
                fwd.start()
                passed.append(fwd)
        for w in fetched:
            for k in range(3):
                for q in pieces:
                    d2d_copy(w, k, 1 - c, q).wait_recv()
        for k in range(3):
            ln_copy(k, 2 * chips[k][0] + chips[k][1]).wait_recv()
        for cp in first + passed:
            cp.wait_send()
        for cp in own:
            cp.wait()
        for t in range(max(n_tiles - 2, 0), n_tiles):
            prod_copy(t).wait()

    vmem = pl.BlockSpec(memory_space=pltpu.VMEM)
    hbm = pl.BlockSpec(memory_space=pl.ANY)
    prod_cols = shards[0].shape[1]
    tok_args = [] if tokens is None else [tokens]
    tok_out = [] if tokens is None else [jax.ShapeDtypeStruct((tokens.shape[0], prod_cols), F32)]
    tok_scratch = [] if tokens is None else [
        pltpu.VMEM((2, tok_tile, tokens.shape[1]), F32), pltpu.VMEM((2, tok_tile, prod_cols), F32),
        pltpu.SemaphoreType.DMA((2,)), pltpu.SemaphoreType.DMA((2,))]
    return pl.pallas_call(
        body, name="gather_weights",
        in_specs=[hbm] * n_w + [vmem] + [hbm] * n_tok,
        out_specs=[hbm] * n_w + [vmem] + [hbm] * n_tok,
        out_shape=[jax.ShapeDtypeStruct(s, BF16) for s in full_shapes]
        + [jax.ShapeDtypeStruct((N_CHIPS,) + ln_shard.shape, F32)] + tok_out,
        scratch_shapes=[pltpu.VMEM(w.shape, F32) for w in shards] + [pltpu.VMEM(w.shape, BF16) for w in shards]
        + [pltpu.SemaphoreType.DMA((6 * RELAY_PIECES * n_w,)), pltpu.SemaphoreType.DMA((6 * RELAY_PIECES * n_w,)),
           pltpu.SemaphoreType.DMA((2 * RELAY_PIECES * len(fetched) + n_w - len(fetched),)),
           pltpu.SemaphoreType.DMA((n_w,)), pltpu.SemaphoreType.DMA((3,)), pltpu.SemaphoreType.DMA((3,))]
        + tok_scratch,
        compiler_params=pltpu.CompilerParams(vmem_limit_bytes=VMEM_LIMIT),
    )(*shards, ln_shard, *tok_args)


def _fetch_copy(full_ref, shard_shape, col_sharded, sender_shard, send_sems, recv_sems, idx, chip, c):
    win = _shard_window(full_ref, shard_shape, col_sharded, sender_shard, None)
    return pltpu.make_async_remote_copy(src_ref=win, dst_ref=win, send_sem=send_sems.at[idx],
                                        recv_sem=recv_sems.at[idx], device_id=(*chip, c), device_id_type=MESH)


def _fetch_start(fulls, shard_shapes, col_sharded):
    n = len(fulls)

    def body(*refs):
        full = refs[:n]
        send_sems, recv_sems = refs[n], refs[n + 1]
        token = refs[-1]
        x, y, c = _place()
        for w in range(n):
            for k, chip in enumerate(_other_chips(x, y)):
                _fetch_copy(full[w], shard_shapes[w], col_sharded[w], 2 * x + y, send_sems, recv_sems, w * 3 + k,
                            chip, c).start()
        token[...] = jnp.zeros_like(token)

    outs = pl.pallas_call(
        body, name="fetch_start",
        out_shape=(pltpu.SemaphoreType.DMA((3 * n,)), pltpu.SemaphoreType.DMA((3 * n,)),
                   *[pltpu.HBM(f.shape, f.dtype) for f in fulls], jax.ShapeDtypeStruct((8, 128), F32)),
        in_specs=[_HBM] * n,
        out_specs=(_SEM, _SEM, *([_HBM] * n), pl.BlockSpec(memory_space=pltpu.VMEM)),
        input_output_aliases={i: 2 + i for i in range(n)},
        compiler_params=pltpu.CompilerParams(has_side_effects=pltpu.SideEffectType.DATAFLOW_SIDE_EFFECTING),
    )(*[pltpu.with_memory_space_constraint(f, pltpu.HBM) for f in fulls])
    return dict(send=outs[0], recv=outs[1], full=list(outs[2:2 + n])), outs[-1]


def _fetch_wait(group, shard_shapes, col_sharded, after):
    n = len(group["full"])

    def body(*refs):
        full = refs[:n]
        send_sems, recv_sems = refs[n], refs[n + 1]
        x, y, c = _place()
        for w in range(n):
            for k, chip in enumerate(_other_chips(x, y)):
                _fetch_copy(full[w], shard_shapes[w], col_sharded[w], 2 * x + y, send_sems, recv_sems, w * 3 + k,
                            chip, c).wait_send()
                _fetch_copy(full[w], shard_shapes[w], col_sharded[w], 2 * chip[0] + chip[1], send_sems, recv_sems,
                            w * 3 + k, chip, c).wait_recv()

    outs = pl.pallas_call(
        body, name="fetch_wait", out_shape=tuple(pltpu.HBM(f.shape, f.dtype) for f in group["full"]),
        in_specs=[_HBM] * n + [_SEM, _SEM, pl.BlockSpec(memory_space=pl.ANY)],
        out_specs=tuple([_HBM] * n), input_output_aliases={i: i for i in range(n)},
        compiler_params=pltpu.CompilerParams(has_side_effects=pltpu.SideEffectType.DATAFLOW_SIDE_EFFECTING),
    )(*group["full"], group["send"], group["recv"], after)
    return list(outs)


_HBM = pl.BlockSpec(memory_space=pltpu.HBM)
_SEM = pl.BlockSpec(memory_space=pltpu.SEMAPHORE)
_N_PEER = N_DEV - 1


def _peer(x, y, c, k):
    return (x + (k >> 2)) % 2, (y + ((k >> 1) & 1)) % 2, (c + (k & 1)) % 2


def _exchange_copy(src_ref, land_ref, sliced, send_sems, recv_sems, idx, x, y, c, k):
    px, py, pc = _peer(x, y, c, k)
    src = src_ref.at[4 * px + 2 * py + pc] if sliced else src_ref
    return pltpu.make_async_remote_copy(
        src_ref=src, dst_ref=land_ref.at[4 * x + 2 * y + c],
        send_sem=send_sems.at[idx], recv_sem=recv_sems.at[idx], device_id=(px, py, pc), device_id_type=MESH)


def _exchange_start(tag, arrays, sliced):
    n = len(arrays)
    lands = [lax.empty(a.shape if s else (N_DEV,) + a.shape, a.dtype) for a, s in zip(arrays, sliced)]

    def body(*refs):
        src, land = refs[:n], refs[n:2 * n]
        send_sems, recv_sems = refs[2 * n], refs[2 * n + 1]
        token = refs[-1]
        x, y, c = _place()
        for w in range(n):
            for k in range(1, N_DEV):
                _exchange_copy(src[w], land[w], sliced[w], send_sems, recv_sems, w * _N_PEER + k - 1, x, y, c, k).start()
        token[...] = jnp.zeros_like(token)

    outs = pl.pallas_call(
        body, name="exchange_start_" + tag,
        out_shape=(pltpu.SemaphoreType.DMA((n * _N_PEER,)), pltpu.SemaphoreType.DMA((n * _N_PEER,)),
                   *[pltpu.HBM(a.shape, a.dtype) for a in arrays], *[pltpu.HBM(l.shape, l.dtype) for l in lands],
                   jax.ShapeDtypeStruct((8, 128), F32)),
        in_specs=[_HBM] * (2 * n),
        out_specs=(_SEM, _SEM, *([_HBM] * (2 * n)), pl.BlockSpec(memory_space=pltpu.VMEM)),
        input_output_aliases={i: 2 + i for i in range(2 * n)},
        compiler_params=pltpu.CompilerParams(has_side_effects=pltpu.SideEffectType.DATAFLOW_SIDE_EFFECTING),
    )(*[pltpu.with_memory_space_constraint(a, pltpu.HBM) for a in arrays],
      *[pltpu.with_memory_space_constraint(l, pltpu.HBM) for l in lands])
    return dict(send=outs[0], recv=outs[1], src=list(outs[2:2 + n]), land=list(outs[2 + n:2 + 2 * n]),
                sliced=list(sliced)), outs[-1]


def _exchange_wait(tag, groups, after):
    counts = [len(g["src"]) for g in groups]
    total = sum(counts)

    def body(*refs):
        pos = 0
        x, y, c = _place()
        for g, n in zip(groups, counts):
            src, land = refs[pos:pos + n], refs[pos + n:pos + 2 * n]
            send_sems, recv_sems = refs[pos + 2 * n], refs[pos + 2 * n + 1]
            pos += 2 * n + 2
            for w in range(n):
                for k in range(1, N_DEV):
                    cp = _exchange_copy(src[w], land[w], g["sliced"][w], send_sems, recv_sems,
                                        w * _N_PEER + k - 1, x, y, c, k)
                    cp.wait_send()
                    cp.wait_recv()

    operands, in_specs, aliases, out_shape = [], [], {}, []
    for g in groups:
        for a in g["src"] + g["land"]:
            aliases[len(operands)] = len(out_shape)
            out_shape.append(pltpu.HBM(a.shape, a.dtype))
            operands.append(a)
            in_specs.append(_HBM)
        operands += [g["send"], g["recv"]]
        in_specs += [_SEM, _SEM]
    operands.append(after)
    in_specs.append(pl.BlockSpec(memory_space=pl.ANY))
    outs = pl.pallas_call(
        body, name="exchange_wait_" + tag, out_shape=tuple(out_shape), in_specs=in_specs,
        out_specs=tuple([_HBM] * (2 * total)), input_output_aliases=aliases,
        compiler_params=pltpu.CompilerParams(has_side_effects=pltpu.SideEffectType.DATAFLOW_SIDE_EFFECTING),
    )(*operands)
    srcs, lands, pos = [], [], 0
    for n in counts:
        srcs += list(outs[pos:pos + n])
        lands += list(outs[pos + n:pos + 2 * n])
        pos += 2 * n
    return srcs, lands


def _sum_and_swap(tag, pieces, lands, small=None, small_land=None):
    n_w = len(pieces)
    n_small = 0 if small is None else 1

    def body(*refs):
        g_refs, land_refs = refs[:n_w], refs[n_w:2 * n_w]
        pos = 2 * n_w + 2 * n_small
        out_refs = refs[pos:pos + n_w]
        pos += n_w + n_small
        bufs = refs[pos:pos + n_w]
        load_sems, swap_send, swap_recv = refs[pos + n_w + 2 * n_small:]
        x, y, c = _place()
        me = 4 * x + 2 * y + c

        def slot(k):
            px, py, pc = _peer(x, y, c, k)
            return 4 * px + 2 * py + pc

        def swap_copy(w, half):
            rows = pieces[w].shape[1]
            win = out_refs[w].at[pl.ds(pl.multiple_of(half * rows, rows), rows), :]
            return pltpu.make_async_remote_copy(
                src_ref=win, dst_ref=win, send_sem=swap_send.at[w], recv_sem=swap_recv.at[w],
                device_id=(x, y, 1 - c), device_id_type=MESH)

        loads = []
        for w in range(n_w):
            per_w = [pltpu.make_async_copy(g_refs[w].at[me], bufs[w].at[me], load_sems.at[w * N_DEV])]
            per_w += [pltpu.make_async_copy(land_refs[w].at[slot(k)], bufs[w].at[slot(k)], load_sems.at[w * N_DEV + k])
                      for k in range(1, N_DEV)]
            loads.append(per_w)
        small_loads = []
        if n_small:
            small_ref, small_land_ref = refs[2 * n_w], refs[2 * n_w + 1]
            small_out = refs[2 * n_w + 2 + n_w]
            small_buf, small_sems = refs[pos + n_w], refs[pos + n_w + 1]
            small_loads = [pltpu.make_async_copy(small_land_ref.at[slot(k)], small_buf.at[slot(k)],
                                                 small_sems.at[k - 1]) for k in range(1, N_DEV)]
        for cp in [cp for per_w in loads for cp in per_w] + small_loads:
            cp.start()
        if n_small:
            small_buf[me] = small_ref[...]
        swaps = []
        for w in range(n_w):
            for cp in loads[w]:
                cp.wait()
            rows = pieces[w].shape[1]
            total = bufs[w][0].astype(F32)
            for p in range(1, N_DEV):
                total += bufs[w][p].astype(F32)
            out_refs[w][pl.ds(pl.multiple_of(c * rows, rows), rows), :] = total
            sw = swap_copy(w, c)
            sw.start()
            swaps.append(sw)
        if n_small:
            for cp in small_loads:
                cp.wait()
            total = small_buf[0]
            for p in range(1, N_DEV):
                total += small_buf[p]
            small_out[...] = total
        for w in range(n_w):
            swap_copy(w, 1 - c).wait_recv()
        for sw in swaps:
            sw.wait_send()

    vmem = pl.BlockSpec(memory_space=pltpu.VMEM)
    hbm = pl.BlockSpec(memory_space=pl.ANY)
    small_args = [small, small_land] if n_small else []
    small_shapes = [jax.ShapeDtypeStruct(small.shape, F32)] if n_small else []
    small_scratch = ([pltpu.VMEM((N_DEV,) + small.shape, F32), pltpu.SemaphoreType.DMA((_N_PEER,))]
                     if n_small else [])
    return pl.pallas_call(
        body, name="sum_and_swap_" + tag,
        in_specs=[hbm] * (2 * n_w) + [vmem, hbm] * n_small,
        out_specs=[vmem] * (n_w + n_small),
        out_shape=[jax.ShapeDtypeStruct((2 * p.shape[1], p.shape[2]), F32) for p in pieces] + small_shapes,
        scratch_shapes=[pltpu.VMEM(p.shape, BF16) for p in pieces] + small_scratch
        + [pltpu.SemaphoreType.DMA((n_w * N_DEV,)), pltpu.SemaphoreType.DMA((n_w,)),
           pltpu.SemaphoreType.DMA((n_w,))],
        compiler_params=pltpu.CompilerParams(vmem_limit_bytes=VMEM_LIMIT),
    )(*pieces, *lands, *small_args)


def _adamw_values(w, g_t, m, v):
    c1 = 1.0 - ADAM_B1 ** ADAM_STEP
    c2 = 1.0 - ADAM_B2 ** ADAM_STEP
    nm = ADAM_B1 * m + (1.0 - ADAM_B1) * g_t
    nv = ADAM_B2 * v + (1.0 - ADAM_B2) * (g_t * g_t)
    return -ADAM_LR * ((nm / c1) / (jnp.sqrt(nv / c2) + ADAM_EPS) + ADAM_WD * w), nm, nv


def _adamw_update(w_ref, g_ref, m_ref, v_ref, d_ref, nm_ref, nv_ref):
    d_ref[...], nm_ref[...], nv_ref[...] = _adamw_values(w_ref[...], g_ref[...], m_ref[...], v_ref[...])


def _adamw_small(packed, shard_index, names, weights, moments_m, moments_v):
    n = len(names)
    shapes = [weights[name].shape for name in names]
    flat = [a[name].reshape(-1, a[name].shape[-1]) for name in names for a in (weights, moments_m, moments_v)]

    def body(packed_ref, shard_ref, *refs):
        loss_row = _small_offset("loss")
        refs[-1][...] = packed_ref[loss_row:loss_row + 1, 0:1]
        for k, name in enumerate(names):
            w_ref, m_ref, v_ref = refs[3 * k:3 * k + 3]
            g_ref, d_ref, nm_ref, nv_ref = refs[3 * n + 4 * k:3 * n + 4 * k + 4]
            rows, cols = w_ref.shape
            first = _small_offset(name)
            if cols <= _LANES:
                blocks = [(slice(0, rows), packed_ref[first:first + rows, 0:cols])]
            else:
                per_row = cols // _LANES
                if cols < _SMALL_SHAPES[name][-1]:
                    first = first + shard_ref[0] * per_row
                blocks = [(slice(i, i + 1),
                           jnp.concatenate([packed_ref[pl.ds(first + i * per_row + j, 1), :] for j in range(per_row)],
                                           axis=1)) for i in range(rows)]
            for at, g_t in blocks:
                g_ref[at, :] = g_t
                d_ref[at, :], nm_ref[at, :], nv_ref[at, :] = _adamw_values(w_ref[at, :], g_t, m_ref[at, :],
                                                                           v_ref[at, :])

    vmem = pl.BlockSpec(memory_space=pltpu.VMEM)
    outs = pl.pallas_call(
        body, name="adamw_small",
        in_specs=[vmem, pl.BlockSpec(memory_space=pltpu.SMEM)] + [vmem] * (3 * n),
        out_shape=[jax.ShapeDtypeStruct(flat[3 * k].shape, F32) for k in range(n) for _ in range(4)]
        + [jax.ShapeDtypeStruct((1, 1), F32)],
    )(packed, shard_index.reshape(1).astype(jnp.int32), *flat)
    return [tuple(o.reshape(shapes[k]) for o in outs[4 * k:4 * k + 4]) for k in range(n)], outs[-1].reshape(())


def _adamw(label, w, g, m, v):
    shape = w.shape
    cols = shape[-1]
    rows = w.size // cols
    args = [a.reshape(rows, cols) for a in (w, g, m, v)]

    def body(w_ref, g_ref, m_ref, v_ref, g_out, d_ref, nm_ref, nv_ref):
        g_out[...] = g_ref[...]
        _adamw_update(w_ref, g_ref, m_ref, v_ref, d_ref, nm_ref, nv_ref)

    block_rows = 256 if rows % 256 == 0 and rows > 256 else rows
    spec = pl.BlockSpec((block_rows, cols), lambda i: (i, 0))
    outs = pl.pallas_call(
        body, name="adamw_" + label, grid=(rows // block_rows,),
        in_specs=[spec] * 4, out_specs=[spec] * 4,
        out_shape=[jax.ShapeDtypeStruct((rows, cols), F32)] * 4,
        compiler_params=_params(),
    )(*args)
    return [o.reshape(shape) for o in outs]


def _no_send(tag, arrays, sliced):
    return jnp.zeros((8, 128), F32)


def _local_step(x, tgt, w_in_a, later_weights, first_after, sgu_ln_g, sgu_ln_b, w_spatial, b_spatial,
                attn_sinks, rel_bias, post_ln_g, post_ln_b, send=_no_send, own_product=None):
    bsp_t = b_spatial.T
    g1, b1 = post_ln_g[0:1], post_ln_b[0:1]
    g2, b2 = post_ln_g[1:2], post_ln_b[1:2]
    onehot = _bucket_onehot()
    bias = _bias_expand(rel_bias.T, onehot)
    win = _window_tables()

    if own_product is None:
        xt, u, vh, z, rv, y = _layer_a_fwd(x, w_in_a, None, sgu_ln_g, sgu_ln_b, w_spatial, bsp_t, first_after)
    else:
        chip, p_own = own_product
        xt, u, vh, z, rv, y = lax.switch(
            chip, [functools.partial(_layer_a_fwd, own=own) for own in range(N_CHIPS)],
            x, w_in_a, p_own, sgu_ln_g, sgu_ln_b, w_spatial, bsp_t, first_after)
    w_out_a, w_kv, w_in_b, w_out_b = later_weights(y)
    xh1, rstd1, q, zb, kd, vd = _layer_b_proj(x, y, w_out_a, g1, b1, w_in_b, w_kv)
    o, probs, sink_probs, dr2, loss_vec, dg2, db2 = _layer_b_fwd(q, zb, kd, vd, bias, win, attn_sinks, xh1, g1, b1,
                                                                 w_out_b, g2, b2, tgt)
    dq, dzb, dkd, dvd, carry_k, carry_v, gw_out_b, dsink, dbias = _layer_b_bwd_attn(
        dr2, zb, o, q, kd, vd, probs, sink_probs, w_out_b)
    dr1, dg1, db1, gw_in_b, gw_kv = _layer_b_bwd_proj(xh1, rstd1, g1, b1, dr2, dq, dzb, dkd, dvd, carry_k, carry_v,
                                                      w_in_b, w_kv)
    gw_out_b = gw_out_b.reshape(N_DEV, -1, D_MODEL)
    gw_kv = gw_kv.reshape(N_DEV, -1, 2 * PAIR)
    after = send("b", [gw_out_b, gw_in_b, gw_kv], [True, True, True])
    dp, gw_out_a, dws, dbsp, dgs, dbs = _layer_a_bwd_mix(dr1, u, vh, z, y, rv, w_out_a, sgu_ln_g, sgu_ln_b,
                                                         w_spatial, bsp_t, after)
    gw_out_a = gw_out_a.reshape(N_DEV, -1, D_MODEL)
    small = _pack_small(dws, dbsp, dsink, dbias, onehot, (dg1, dg2), (db1, db2), dgs, dbs, loss_vec)
    after = send("a_out", [gw_out_a, small], [True, False])
    gw_in_a = _layer_a_bwd_win(xt, dp, after).reshape(N_DEV, D_MODEL // 2, -1)
    after = send("a_in", [gw_in_a], [True])
    after, updates = after if isinstance(after, tuple) else (after, ())
    grad_x, *updated = _layer_a_bwd_dx(dr1, dp, w_in_a, after, updates)

    pieces = [gw_in_a, gw_out_a, gw_kv, gw_in_b, gw_out_b]
    return grad_x, pieces, small, updated


def kernel(x, w_in_a, sgu_ln_g, sgu_ln_b, w_spatial, b_spatial, w_out_a, w_kv, w_in_b, attn_sinks, rel_bias, w_out_b, post_ln_g, post_ln_b, loss_target, m_w_in_a, m_sgu_ln_g, m_sgu_ln_b, m_w_spatial, m_b_spatial, m_w_out_a, m_w_kv, m_w_in_b, m_attn_sinks, m_rel_bias, m_w_out_b, m_post_ln_g, m_post_ln_b, v_w_in_a, v_sgu_ln_g, v_sgu_ln_b, v_w_spatial, v_b_spatial, v_w_out_a, v_w_kv, v_w_in_b, v_attn_sinks, v_rel_bias, v_w_out_b, v_post_ln_g, v_post_ln_b):
    weights = dict(w_in_a=w_in_a, sgu_ln_g=sgu_ln_g, sgu_ln_b=sgu_ln_b, w_spatial=w_spatial, b_spatial=b_spatial,
                   w_out_a=w_out_a, w_kv=w_kv, w_in_b=w_in_b, attn_sinks=attn_sinks, rel_bias=rel_bias,
                   w_out_b=w_out_b, post_ln_g=post_ln_g, post_ln_b=post_ln_b)
    moments_m = dict(w_in_a=m_w_in_a, sgu_ln_g=m_sgu_ln_g, sgu_ln_b=m_sgu_ln_b, w_spatial=m_w_spatial,
                     b_spatial=m_b_spatial, w_out_a=m_w_out_a, w_kv=m_w_kv, w_in_b=m_w_in_b,
                     attn_sinks=m_attn_sinks, rel_bias=m_rel_bias, w_out_b=m_w_out_b, post_ln_g=m_post_ln_g,
                     post_ln_b=m_post_ln_b)
    moments_v = dict(w_in_a=v_w_in_a, sgu_ln_g=v_sgu_ln_g, sgu_ln_b=v_sgu_ln_b, w_spatial=v_w_spatial,
                     b_spatial=v_b_spatial, w_out_a=v_w_out_a, w_kv=v_w_kv, w_in_b=v_w_in_b,
                     attn_sinks=v_attn_sinks, rel_bias=v_rel_bias, w_out_b=v_w_out_b, post_ln_g=v_post_ln_g,
                     post_ln_b=v_post_ln_b)
    order = ("w_in_a", "sgu_ln_g", "sgu_ln_b", "w_spatial", "b_spatial", "w_out_a", "w_kv", "w_in_b", "attn_sinks",
             "rel_bias", "w_out_b", "post_ln_g", "post_ln_b")

    shard_index = 2 * lax.axis_index("x") + lax.axis_index("y")
    ln_shard = jnp.concatenate([sgu_ln_g, sgu_ln_b], axis=0)
    shards = [w_in_a[0], w_out_a[0], w_kv, w_in_b[0], w_out_b[0]]
    col_sharded = [True, False, False, True, False]
    full_in_a, *later, ln_full, p_own = _gather_weights(shards, col_sharded, [True, False, False, False, False],
                                                        ln_shard, tokens=x[0])
    ln_full = jnp.transpose(ln_full, (1, 0, 2)).reshape(2, A_WIDTH)
    later_shapes = [s.shape for s in shards[1:]]
    fetch_group, fetch_token = _fetch_start(later, later_shapes, col_sharded[1:])

    def later_weights(y):
        return _fetch_wait(fetch_group, later_shapes, col_sharded[1:], y)

    groups, grads, deltas, new_m, new_v, scalars = {}, {}, {}, {}, {}, {}
    early = ("w_out_b", "w_in_b", "w_kv", "w_out_a")

    def two_dim(a):
        return a.reshape(-1, a.shape[-1])

    def send(tag, arrays, sliced):
        groups[tag], token = _exchange_start(tag, arrays, sliced)
        if tag != "a_in":
            return token
        srcs, lands = _exchange_wait("early", [groups["b"], groups["a_out"]], token)
        *reduced, packed_sum = _sum_and_swap("early", srcs[:4], lands[:4], srcs[4], lands[4])
        updates = [(two_dim(weights[n]), g, two_dim(moments_m[n]), two_dim(moments_v[n]))
                   for n, g in zip(early, reduced)]
        small_names = ("sgu_ln_g", "sgu_ln_b", "w_spatial", "b_spatial", "attn_sinks", "rel_bias", "post_ln_g",
                       "post_ln_b")
        def as_packed(arrays):
            return {n: (arrays[n].T if n == "rel_bias" else arrays[n]) for n in small_names}

        small_updates, scalars["loss"] = _adamw_small(packed_sum, shard_index, small_names, as_packed(weights),
                                                      as_packed(moments_m), as_packed(moments_v))
        for name, results in zip(small_names, small_updates):
            grads[name], deltas[name], new_m[name], new_v[name] = [
                a.T if name == "rel_bias" else a for a in results]
        return new_m["b_spatial"].reshape(A_GROUPS, CHUNK), updates

    grad_x, _, _, updated = _local_step(
        x[0], loss_target[0], full_in_a, later_weights, fetch_token, ln_full[0:1], ln_full[1:2], w_spatial[0],
        b_spatial[0], attn_sinks, rel_bias, post_ln_g, post_ln_b, send=send, own_product=(shard_index, p_own))
    for k, name in enumerate(early):
        grads[name], deltas[name], new_m[name], new_v[name] = [
            a.reshape(weights[name].shape) for a in updated[4 * k:4 * k + 4]]

    srcs, lands = _exchange_wait("late", [groups["a_in"]], grad_x)
    (g_in_a,) = _sum_and_swap("late", srcs, lands)
    grads["w_in_a"], deltas["w_in_a"], new_m["w_in_a"], new_v["w_in_a"] = _adamw(
        "w_in_a", w_in_a, g_in_a.reshape(w_in_a.shape), m_w_in_a, v_w_in_a)
    return (scalars["loss"], grad_x[None], *[grads[n] for n in order], *[deltas[n] for n in order],
            *[new_m[n] for n in order], *[new_v[n] for n in order])
```

```python
import functools
import math

import jax
import jax.numpy as jnp
from jax import lax
from jax.experimental import pallas as pl
from jax.experimental.pallas import tpu as pltpu

F32 = jnp.float32
BF16 = jnp.bfloat16

D_MODEL = 1024
A_WIDTH = 2048
A_GROUPS = 8
A_GROUP_DIM = 256
CHUNK = 128
N_HEADS = 16
N_KV = 2
HEAD_DIM = 64
PAIR = 2 * HEAD_DIM
B_WIDTH = 1024
REL_BUCKETS = 32
ALPHA = 4.0 ** 0.25
LN_EPS = 1e-5
NEG_INF = -1e30
SCALE = HEAD_DIM ** -0.5

ADAM_LR = 0.001
ADAM_B1 = 0.9
ADAM_B2 = 0.999
ADAM_EPS = 1e-08
ADAM_WD = 0.01
ADAM_STEP = 10

N_DEV = 8
N_CHIPS = 4
MESH = pl.DeviceIdType.MESH
VMEM_LIMIT = 56 * 1024 * 1024

TM_ATTN = 256
TM_BWD_ATTN = 512
TM_MM = 512
TM_WIN = 2048
_LANES = 128
_SUBLANES = 8


def _dot(a, b):
    return jnp.dot(a, b, preferred_element_type=F32)


def _dot_nt(a, b):
    return lax.dot_general(a, b, (((1,), (1,)), ((), ())), preferred_element_type=F32)


def _dot_tn(a, b):
    return lax.dot_general(a, b, (((0,), (0,)), ((), ())), preferred_element_type=F32)


def _ln_fwd(r):
    mu = jnp.mean(r, axis=-1, keepdims=True)
    rc = r - mu
    var = jnp.mean(rc * rc, axis=-1, keepdims=True)
    rstd = lax.rsqrt(var + LN_EPS)
    return rc * rstd, rstd


def _ln_bwd(dxh, xh, rstd):
    m1 = jnp.mean(dxh, axis=-1, keepdims=True)
    m2 = jnp.mean(dxh * xh, axis=-1, keepdims=True)
    return rstd * (dxh - m1 - xh * m2)


def _silu_parts(z):
    sg = jax.nn.sigmoid(z)
    return z * sg, sg * (1.0 + z * (1.0 - sg))


def _dup_halves(blk):
    sw = pltpu.roll(blk, HEAD_DIM, 1)
    lo = lax.broadcasted_iota(jnp.int32, blk.shape, 1) < HEAD_DIM
    return jnp.where(lo, blk, sw), jnp.where(lo, sw, blk)


def _fold_halves(blk):
    return blk + pltpu.roll(blk, HEAD_DIM, 1)


def _resident(shape):
    nd = len(shape)
    return pl.BlockSpec(shape, lambda *_: (0,) * nd, pipeline_mode=pl.Buffered(1))


def _const(shape):
    nd = len(shape)
    return pl.BlockSpec(shape, lambda *_: (0,) * nd)


def _rows(tm, cols):
    return pl.BlockSpec((tm, cols), lambda i: (i, 0))


def _params(sem=("arbitrary",)):
    return pltpu.CompilerParams(dimension_semantics=sem, vmem_limit_bytes=VMEM_LIMIT)


def _spatial_mix(ws_ref, bsp_ref, vn, s_scr, n_chunks):
    tri = (lax.broadcasted_iota(jnp.int32, (CHUNK, CHUNK), 0)
           >= lax.broadcasted_iota(jnp.int32, (CHUNK, CHUNK), 1))
    for g in range(A_GROUPS):
        wsg = jnp.where(tri, ws_ref[g], 0.0).astype(BF16)
        cols = slice(g * A_GROUP_DIM, (g + 1) * A_GROUP_DIM)
        for ci in range(n_chunks):
            rows = slice(ci * CHUNK, (ci + 1) * CHUNK)
            s_scr[rows, cols] = _dot(wsg, vn[rows, cols]) + bsp_ref[:, g:g + 1]


def _layer_a_fwd(x, w_in, p_own, lng, lnb, ws, bsp_t, after, own=None):
    t_len = x.shape[0]
    tm = TM_ATTN
    shard_cols = 3 * A_WIDTH // N_CHIPS

    def body(x_ref, win_ref, *rest):
        pown_ref = rest[0] if own is not None else None
        (lng_ref, lnb_ref, ws_ref, bsp_ref, after_ref,
         xt_ref, u_ref, vh_ref, z_ref, rv_ref, y_ref, s_scr) = rest[0 if own is None else 1:]
        x_t = x_ref[...]
        xb = x_t.astype(BF16)
        xt_ref[...] = x_t.T.astype(BF16)
        if own is None:
            u = _dot(xb, win_ref[:, 0:A_WIDTH])
            v = _dot(xb, win_ref[:, A_WIDTH:2 * A_WIDTH])
            z = _dot(xb, win_ref[:, 2 * A_WIDTH:3 * A_WIDTH])
        else:
            pieces = {s: pown_ref[...] if s == own else _dot(xb, win_ref[:, s * shard_cols:(s + 1) * shard_cols])
                      for s in (1, 2, 3, 0)}
            p = jnp.concatenate([pieces[s] for s in range(N_CHIPS)], axis=1)
            u, v, z = p[:, 0:A_WIDTH], p[:, A_WIDTH:2 * A_WIDTH], p[:, 2 * A_WIDTH:3 * A_WIDTH]
        vh, rv = _ln_fwd(v)
        vn = (vh * lng_ref[...] + lnb_ref[...]).astype(BF16)
        _spatial_mix(ws_ref, bsp_ref, vn, s_scr, tm // CHUNK)
        sz, _ = _silu_parts(z)
        y_ref[...] = (u * s_scr[...] * sz).astype(BF16)
        u_ref[...] = u.astype(BF16)
        vh_ref[...] = vh.astype(BF16)
        z_ref[...] = z.astype(BF16)
        rv_ref[...] = rv

    wide = jax.ShapeDtypeStruct((t_len, A_WIDTH), BF16)
    product = [] if own is None else [p_own]
    return pl.pallas_call(
        body, name="layer_a_fwd" if own is None else "layer_a_fwd_own%d" % own, grid=(t_len // tm,),
        in_specs=[_rows(tm, D_MODEL), _resident(w_in.shape)] + [_rows(tm, shard_cols)] * len(product)
        + [_const(lng.shape), _const(lnb.shape), _const(ws.shape), _const(bsp_t.shape), _const(after.shape)],
        out_specs=[pl.BlockSpec((D_MODEL, tm), lambda i: (0, i)), _rows(tm, A_WIDTH), _rows(tm, A_WIDTH),
                   _rows(tm, A_WIDTH), _rows(tm, 1), _rows(tm, A_WIDTH)],
        out_shape=[jax.ShapeDtypeStruct((D_MODEL, t_len), BF16), wide, wide, wide,
                   jax.ShapeDtypeStruct((t_len, 1), F32), wide],
        scratch_shapes=[pltpu.VMEM((tm, A_WIDTH), F32)],
        compiler_params=_params(),
    )(x, w_in, *product, lng, lnb, ws, bsp_t, after)


def _layer_b_proj(x, y, w_out_a, g1, b1, w_in, w_kv):
    t_len = x.shape[0]
    tm = 2 * TM_MM

    def body(x_ref, y_ref, wout_ref, g_ref, b_ref, win_ref, wkv_ref, xh_ref, r1_ref, q_ref, z_ref, kd_ref, vd_ref):
        halves = [slice(k * TM_MM, (k + 1) * TM_MM) for k in range(2)]
        projected = [_dot(y_ref[rows, :], wout_ref[...]) for rows in halves]
        for rows, out_a in zip(halves, projected):
            xh, r1 = _ln_fwd(ALPHA * x_ref[rows, :] + out_a)
            xh_ref[rows, :] = xh
            r1_ref[rows, :] = r1
            h1 = (xh * g_ref[...] + b_ref[...]).astype(BF16)
            q_ref[rows, :] = (_dot(h1, win_ref[:, 0:B_WIDTH]) * SCALE).astype(BF16)
            z_ref[rows, :] = _dot(h1, win_ref[:, B_WIDTH:2 * B_WIDTH]).astype(BF16)
            kv = _dot(h1, wkv_ref[...])
            k0, k1 = _dup_halves(kv[:, 0:PAIR])
            v0, v1 = _dup_halves(kv[:, PAIR:2 * PAIR])
            kd_ref[rows, 0:PAIR] = k0.astype(BF16)
            kd_ref[rows, PAIR:2 * PAIR] = k1.astype(BF16)
            vd_ref[rows, 0:PAIR] = v0.astype(BF16)
            vd_ref[rows, PAIR:2 * PAIR] = v1.astype(BF16)

    return pl.pallas_call(
        body, name="layer_b_proj", grid=(t_len // tm,),
        in_specs=[_rows(tm, D_MODEL), _rows(tm, A_WIDTH), _resident(w_out_a.shape), _const(g1.shape),
                  _const(b1.shape), _resident(w_in.shape), _resident(w_kv.shape)],
        out_specs=[_rows(tm, D_MODEL), _rows(tm, 1), _rows(tm, B_WIDTH), _rows(tm, B_WIDTH), _rows(tm, 2 * PAIR),
                   _rows(tm, 2 * PAIR)],
        out_shape=[jax.ShapeDtypeStruct((t_len, D_MODEL), F32), jax.ShapeDtypeStruct((t_len, 1), F32),
                   jax.ShapeDtypeStruct((t_len, B_WIDTH), BF16), jax.ShapeDtypeStruct((t_len, B_WIDTH), BF16),
                   jax.ShapeDtypeStruct((t_len, 2 * PAIR), BF16), jax.ShapeDtypeStruct((t_len, 2 * PAIR), BF16)],
        compiler_params=_params(),
    )(x, y, w_out_a, g1, b1, w_in, w_kv)


GROUP = N_HEADS // N_KV
GROUP_Q = GROUP * CHUNK


def _window_tables():
    j = jnp.arange(2 * CHUNK, dtype=jnp.int32)[:, None]
    t = jnp.arange(CHUNK, dtype=jnp.int32)[None, :]
    dist = t + CHUNK - j
    inside = (dist >= 0) & (dist < CHUNK)
    return jnp.stack([inside & (j >= CHUNK), inside]).astype(F32)


def _band(ref, chunk_index, kvh):
    prev0 = pl.multiple_of(jnp.maximum(chunk_index - 1, 0) * CHUNK, CHUNK)
    cur0 = pl.multiple_of(chunk_index * CHUNK, CHUNK)
    cols = slice(kvh * PAIR, (kvh + 1) * PAIR)
    return jnp.concatenate([ref[pl.ds(prev0, CHUNK), cols], ref[pl.ds(cur0, CHUNK), cols]], axis=0)


def _group_tables(bias_ref, win_ref, sink_ref, chunk_index, kvh):
    bias = jnp.concatenate([bias_ref[kvh * GROUP + j] for j in range(GROUP)], axis=1)
    win = win_ref[jnp.minimum(chunk_index, 1)]
    mask = jnp.concatenate([win] * GROUP, axis=1) > 0.5
    sink = jnp.concatenate([jnp.full((1, CHUNK), sink_ref[0, kvh * GROUP + j], F32) for j in range(GROUP)], axis=1)
    return bias, mask, sink


def _attn_probs(qs, kband, bias, mask, sink):
    logits = jnp.where(mask, _dot_nt(kband, qs) + bias, NEG_INF)
    m = jnp.maximum(jnp.max(logits, axis=0, keepdims=True), sink)
    e = jnp.exp(logits - m)
    es = jnp.exp(sink - m)
    inv = 1.0 / (jnp.sum(e, axis=0, keepdims=True) + es)
    return e * inv, es * inv


def _half_mask():
    return lax.broadcasted_iota(jnp.int32, (CHUNK, PAIR), 1) < HEAD_DIM


def _stack_heads(src_ref, rows, kvh, dst_scr, lo):
    for j in range(GROUP):
        h = kvh * GROUP + j
        blk = src_ref[rows, (h // 2) * PAIR:(h // 2 + 1) * PAIR].astype(F32)
        keep = lo if h % 2 == 0 else ~lo
        dst_scr[j * CHUNK:(j + 1) * CHUNK, :] = jnp.where(keep, blk, 0.0).astype(BF16)


def _probs_spec(tm):
    return pl.BlockSpec((tm // CHUNK, N_KV, 2 * CHUNK, GROUP_Q), lambda i: (i, 0, 0, 0))


def _sink_probs_spec(tiles=1):
    return pl.BlockSpec((tiles, 8, GROUP_Q), lambda i: (i, 0, 0))


def _unstack_pairs(stacked, pp, lo):
    return jnp.where(lo, stacked[(2 * pp) * CHUNK:(2 * pp + 1) * CHUNK], stacked[(2 * pp + 1) * CHUNK:(2 * pp + 2) * CHUNK])


def _layer_b_fwd(q, zb, kd, vd, bias, win, sinks, xh1, g1, b1, w_out, g2, b2, tgt):
    t_len = q.shape[0]
    tm = 2 * TM_ATTN

    def body(q_ref, z_ref, kd_ref, vd_ref, bias_ref, win_ref, sink_ref, xh_ref, g1_ref, b1_ref, wout_ref, g2_ref,
             b2_ref, tgt_ref, o_ref, p_ref, ps_ref, dr_ref, loss_ref, dg_ref, db_ref, o_scr, qs_scr):
        i = pl.program_id(0)

        @pl.when(i == 0)
        def _():
            loss_ref[...] = jnp.zeros_like(loss_ref)
            dg_ref[...] = jnp.zeros_like(dg_ref)
            db_ref[...] = jnp.zeros_like(db_ref)

        lo = _half_mask()
        ps_ref[...] = jnp.zeros_like(ps_ref)
        per_part = TM_ATTN // CHUNK
        for part in range(tm // TM_ATTN):
            part_rows = slice(part * TM_ATTN, (part + 1) * TM_ATTN)
            for cp in range(per_part):
                ci = part * per_part + cp
                cg = i * (tm // CHUNK) + ci
                rows = slice(ci * CHUNK, (ci + 1) * CHUNK)
                for kvh in range(N_KV):
                    kband = _band(kd_ref, cg, kvh)
                    vband = _band(vd_ref, cg, kvh)
                    bias_g, mask, sink = _group_tables(bias_ref, win_ref, sink_ref, cg, kvh)
                    _stack_heads(q_ref, rows, kvh, qs_scr, lo)
                    p, p_sink = _attn_probs(qs_scr[...], kband, bias_g, mask, sink)
                    p = p.astype(BF16)
                    p_ref[ci, kvh] = p
                    ps_ref[part, cp * N_KV + kvh:cp * N_KV + kvh + 1, :] = p_sink
                    o_stack = _dot_tn(p, vband)
                    for pp in range(GROUP // 2):
                        pair = kvh * (GROUP // 2) + pp
                        o_scr[rows, pair * PAIR:(pair + 1) * PAIR] = _unstack_pairs(o_stack, pp, lo)
            o = o_scr[part_rows, :]
            o_ref[part_rows, :] = o.astype(BF16)
            sz, _ = _silu_parts(z_ref[part_rows, :].astype(F32))
            y = (o * sz).astype(BF16)
            h1 = xh_ref[part_rows, :] * g1_ref[...] + b1_ref[...]
            r = ALPHA * h1 + _dot(y, wout_ref[...])
            xh2, rstd2 = _ln_fwd(r)
            diff = xh2 * g2_ref[...] + b2_ref[...] - tgt_ref[part_rows, :]
            loss_ref[...] += jnp.sum(diff * diff, axis=0, keepdims=True)
            dh2 = diff * (1.0 / D_MODEL)
            dg_ref[...] += jnp.sum(dh2 * xh2, axis=0, keepdims=True)
            db_ref[...] += jnp.sum(dh2, axis=0, keepdims=True)
            dr_ref[part_rows, :] = _ln_bwd(dh2 * g2_ref[...], xh2, rstd2)

    vec = jax.ShapeDtypeStruct((1, D_MODEL), F32)
    return pl.pallas_call(
        body, name="layer_b_fwd", grid=(t_len // tm,),
        in_specs=[_rows(tm, B_WIDTH), _rows(tm, B_WIDTH), _resident(kd.shape), _resident(vd.shape),
                  _resident(bias.shape), _resident(win.shape), pl.BlockSpec(memory_space=pltpu.SMEM),
                  _rows(tm, D_MODEL), _const(g1.shape), _const(b1.shape), _resident(w_out.shape), _const(g2.shape),
                  _const(b2.shape), _rows(tm, D_MODEL)],
        out_specs=[_rows(tm, B_WIDTH), _probs_spec(tm), _sink_probs_spec(tm // TM_ATTN), _rows(tm, D_MODEL)]
        + [_const((1, D_MODEL))] * 3,
        out_shape=[jax.ShapeDtypeStruct((t_len, B_WIDTH), BF16),
                   jax.ShapeDtypeStruct((t_len // CHUNK, N_KV, 2 * CHUNK, GROUP_Q), BF16),
                   jax.ShapeDtypeStruct((t_len // TM_ATTN, 8, GROUP_Q), F32),
                   jax.ShapeDtypeStruct((t_len, D_MODEL), F32), vec, vec, vec],
        scratch_shapes=[pltpu.VMEM((tm, B_WIDTH), F32), pltpu.VMEM((GROUP_Q, PAIR), BF16)],
        compiler_params=_params(),
    )(q, zb, kd, vd, bias, win, sinks, xh1, g1, b1, w_out, g2, b2, tgt)


def _layer_b_bwd_attn(dr2, zb, o, q, kd, vd, probs, sink_probs, w_out):
    t_len = q.shape[0]
    tm = TM_BWD_ATTN
    n_steps = t_len // tm
    n_chunks = tm // CHUNK
    per_part = TM_ATTN // CHUNK

    def body(dr_ref, z_ref, o_ref, q_ref, kd_ref, vd_ref, p_ref, ps_ref, wout_ref,
             dq_ref, dz_ref, dkd_ref, dvd_ref, ck_ref, cv_ref, gw_ref, dsink_ref, dbias_ref,
             do_scr, qs_scr, dos_scr, gw_acc):
        i = pl.program_id(0)

        @pl.when(i == 0)
        def _():
            gw_acc[...] = jnp.zeros_like(gw_acc)
            dsink_ref[...] = jnp.zeros_like(dsink_ref)
            dbias_ref[...] = jnp.zeros_like(dbias_ref)

        drb = dr_ref[...].astype(BF16)
        per_kvh = 2
        n_blocks = N_KV * per_kvh
        block_cols = B_WIDTH // n_blocks

        def through_gate(b):
            cols = slice(b * block_cols, (b + 1) * block_cols)
            dy = _dot_nt(drb, wout_ref[cols, :])
            sz, dsz = _silu_parts(z_ref[:, cols].astype(F32))
            o_t = o_ref[:, cols].astype(F32)
            dz_ref[:, cols] = (dy * o_t * dsz).astype(BF16)
            do_scr[:, cols] = (dy * sz).astype(BF16)
            return (o_t * sz).astype(BF16)

        def weight_gradient(b, gated):
            cols = slice(b * block_cols, (b + 1) * block_cols)
            gw_acc[cols, :] += _dot_tn(gated, drb)

        gated = {b: through_gate(b) for b in range(per_kvh)}

        lo = _half_mask()
        for kvh in range(N_KV):
            kcols = slice(kvh * PAIR, (kvh + 1) * PAIR)
            dk_bands, dv_bands = [], []
            for ci in range(n_chunks):
                unit = kvh * n_chunks + ci
                if ci < per_kvh and kvh + 1 < N_KV:
                    gated[(kvh + 1) * per_kvh + ci] = through_gate((kvh + 1) * per_kvh + ci)
                if unit in gated:
                    weight_gradient(unit, gated.pop(unit))
                cg = i * n_chunks + ci
                rows = slice(ci * CHUNK, (ci + 1) * CHUNK)
                kband = _band(kd_ref, cg, kvh)
                vband = _band(vd_ref, cg, kvh)
                _stack_heads(q_ref, rows, kvh, qs_scr, lo)
                _stack_heads(do_scr, rows, kvh, dos_scr, lo)
                qs = qs_scr[...]
                dos = dos_scr[...]
                pb = p_ref[ci, kvh]
                p = pb.astype(F32)
                sink_row = (ci % per_part) * N_KV + kvh
                p_sink = ps_ref[ci // per_part, sink_row:sink_row + 1, :]
                dp = _dot_nt(vband, dos)
                delta = jnp.sum(p * dp, axis=0, keepdims=True)
                dlog = p * (dp - delta)
                for j in range(GROUP):
                    dbias_ref[kvh * GROUP + j] += dlog[:, j * CHUNK:(j + 1) * CHUNK]
                dsink_ref[kvh:kvh + 1, :] += -(p_sink * delta)
                ds = dlog.astype(BF16)
                dq_stack = _dot_tn(ds, kband) * SCALE
                for pp in range(GROUP // 2):
                    pair = kvh * (GROUP // 2) + pp
                    dq_ref[rows, pair * PAIR:(pair + 1) * PAIR] = _unstack_pairs(dq_stack, pp, lo).astype(BF16)
                dk_bands.append(_dot(ds, qs))
                dv_bands.append(_dot(pb, dos))
            for bands, out_ref, carry_ref in ((dk_bands, dkd_ref, ck_ref), (dv_bands, dvd_ref, cv_ref)):
                carry_ref[0, :, kcols] = bands[0][0:CHUNK]
                for ci in range(n_chunks):
                    own = bands[ci][CHUNK:2 * CHUNK]
                    if ci + 1 < n_chunks:
                        own = own + bands[ci + 1][0:CHUNK]
                    out_ref[ci * CHUNK:(ci + 1) * CHUNK, kcols] = own

        @pl.when(i == n_steps - 1)
        def _():
            gw_ref[...] = gw_acc[...].astype(BF16)

    carry_spec = pl.BlockSpec((1, CHUNK, 2 * PAIR), lambda i: (i, 0, 0))
    carry_shape = jax.ShapeDtypeStruct((n_steps, CHUNK, 2 * PAIR), F32)
    bias_shape = (N_HEADS, 2 * CHUNK, CHUNK)
    return pl.pallas_call(
        body, name="layer_b_bwd_attn", grid=(n_steps,),
        in_specs=[_rows(tm, D_MODEL), _rows(tm, B_WIDTH), _rows(tm, B_WIDTH), _rows(tm, B_WIDTH),
                  _resident(kd.shape), _resident(vd.shape), _probs_spec(tm), _sink_probs_spec(tm // TM_ATTN),
                  _resident(w_out.shape)],
        out_specs=[_rows(tm, B_WIDTH), _rows(tm, B_WIDTH), _rows(tm, 2 * PAIR), _rows(tm, 2 * PAIR),
                   carry_spec, carry_spec, _const(w_out.shape), _const((N_KV, GROUP_Q)), _const(bias_shape)],
        out_shape=[jax.ShapeDtypeStruct((t_len, B_WIDTH), BF16), jax.ShapeDtypeStruct((t_len, B_WIDTH), BF16),
                   jax.ShapeDtypeStruct((t_len, 2 * PAIR), F32), jax.ShapeDtypeStruct((t_len, 2 * PAIR), F32),
                   carry_shape, carry_shape, jax.ShapeDtypeStruct(w_out.shape, BF16),
                   jax.ShapeDtypeStruct((N_KV, GROUP_Q), F32), jax.ShapeDtypeStruct(bias_shape, F32)],
        scratch_shapes=[pltpu.VMEM((tm, B_WIDTH), BF16), pltpu.VMEM((GROUP_Q, PAIR), BF16),
                        pltpu.VMEM((GROUP_Q, PAIR), BF16), pltpu.VMEM(w_out.shape, F32)],
        compiler_params=_params(),
    )(dr2, zb, o, q, kd, vd, probs, sink_probs, w_out)


def _layer_b_bwd_proj(xh1, rstd1, g1, b1, dr2, dq, dzb, dkd, dvd, carry_k, carry_v, w_in, w_kv):
    t_len = xh1.shape[0]
    tm = TM_MM
    n_steps = t_len // tm
    per_tile = tm // TM_BWD_ATTN
    n_carry = carry_k.shape[0]

    def body(xh_ref, rstd_ref, g_ref, b_ref, dr2_ref, dq_ref, dz_ref, dkd_ref, dvd_ref, *rest):
        carry_refs = rest[:2 * per_tile]
        win_ref, wkv_ref, dr1_ref, dg_ref, db_ref, gwin_ref, gwkv_ref, acc_in, acc_kv = rest[2 * per_tile:]
        i = pl.program_id(0)

        @pl.when(i == 0)
        def _():
            acc_in[...] = jnp.zeros_like(acc_in)
            acc_kv[...] = jnp.zeros_like(acc_kv)
            dg_ref[...] = jnp.zeros_like(dg_ref)
            db_ref[...] = jnp.zeros_like(db_ref)

        lo = lax.broadcasted_iota(jnp.int32, (tm, PAIR), 1) < HEAD_DIM

        def heads_gradient(tile_ref, refs):
            parts = []
            for a in range(per_tile):
                parts.append(tile_ref[a * TM_BWD_ATTN:(a + 1) * TM_BWD_ATTN - CHUNK, :])
                carry = refs[a][0]
                if a == per_tile - 1:
                    carry = jnp.where(i < n_steps - 1, carry, 0.0)
                parts.append(tile_ref[(a + 1) * TM_BWD_ATTN - CHUNK:(a + 1) * TM_BWD_ATTN, :] + carry)
            dup = jnp.concatenate(parts, axis=0)
            return jnp.where(lo, _fold_halves(dup[:, 0:PAIR]), _fold_halves(dup[:, PAIR:2 * PAIR]))

        xh = xh_ref[...]
        h1 = (xh * g_ref[...] + b_ref[...]).astype(BF16)
        dq_t = dq_ref[...]
        dz_t = dz_ref[...]
        dkv = jnp.concatenate([heads_gradient(dkd_ref, carry_refs[:per_tile]),
                               heads_gradient(dvd_ref, carry_refs[per_tile:])], axis=1).astype(BF16)
        dh1 = ALPHA * dr2_ref[...]
        dh1 += _dot_nt(dq_t, win_ref[:, 0:B_WIDTH])
        dh1 += _dot_nt(dz_t, win_ref[:, B_WIDTH:2 * B_WIDTH])
        dh1 += _dot_nt(dkv, wkv_ref[...])
        acc_in[:, 0:B_WIDTH] += _dot_tn(h1, dq_t)
        acc_in[:, B_WIDTH:2 * B_WIDTH] += _dot_tn(h1, dz_t)
        acc_kv[...] += _dot_tn(h1, dkv)
        dg_ref[...] += jnp.sum(dh1 * xh, axis=0, keepdims=True)
        db_ref[...] += jnp.sum(dh1, axis=0, keepdims=True)
        dr1_ref[...] = _ln_bwd(dh1 * g_ref[...], xh, rstd_ref[...])

        @pl.when(i == n_steps - 1)
        def _():
            half_rows = D_MODEL // 2
            shard_cols = 2 * B_WIDTH // N_CHIPS
            for s in range(N_CHIPS):
                for c in range(2):
                    gwin_ref[2 * s + c] = acc_in[c * half_rows:(c + 1) * half_rows,
                                                 s * shard_cols:(s + 1) * shard_cols].astype(BF16)
            gwkv_ref[...] = acc_kv[...].astype(BF16)

    vec = jax.ShapeDtypeStruct((1, D_MODEL), F32)
    gwin_shape = (N_DEV, D_MODEL // 2, 2 * B_WIDTH // N_CHIPS)

    def carry_spec(a):
        return pl.BlockSpec((1, CHUNK, 2 * PAIR), lambda i: (jnp.minimum(per_tile * i + a + 1, n_carry - 1), 0, 0))

    carry_specs = [carry_spec(a) for a in range(per_tile)]
    return pl.pallas_call(
        body, name="layer_b_bwd_proj", grid=(n_steps,),
        in_specs=[_rows(tm, D_MODEL), _rows(tm, 1), _const(g1.shape), _const(b1.shape), _rows(tm, D_MODEL),
                  _rows(tm, B_WIDTH), _rows(tm, B_WIDTH), _rows(tm, 2 * PAIR), _rows(tm, 2 * PAIR)]
        + carry_specs + carry_specs + [_resident(w_in.shape), _resident(w_kv.shape)],
        out_specs=[_rows(tm, D_MODEL), _const((1, D_MODEL)), _const((1, D_MODEL)), _const(gwin_shape),
                   _const(w_kv.shape)],
        out_shape=[jax.ShapeDtypeStruct((t_len, D_MODEL), F32), vec, vec,
                   jax.ShapeDtypeStruct(gwin_shape, BF16), jax.ShapeDtypeStruct(w_kv.shape, BF16)],
        scratch_shapes=[pltpu.VMEM(w_in.shape, F32), pltpu.VMEM(w_kv.shape, F32)],
        compiler_params=_params(),
    )(xh1, rstd1, g1, b1, dr2, dq, dzb, dkd, dvd, *([carry_k] * per_tile), *([carry_v] * per_tile), w_in, w_kv)


def _layer_a_bwd_mix(dr1, u, vh, z, y, rv, w_out, lng, lnb, ws, bsp_t, after):
    t_len = u.shape[0]
    tm = TM_ATTN
    n_steps = t_len // tm

    def body(dr_ref, u_ref, vh_ref, z_ref, y_ref, rv_ref, wout_ref, lng_ref, lnb_ref, ws_ref, bsp_ref, after_ref,
             dp_ref, gw_ref, dws_ref, dbsp_ref, dgs_ref, dbs_ref, dvn_scr, gw_acc):
        i = pl.program_id(0)

        @pl.when(i == 0)
        def _():
            gw_acc[...] = jnp.zeros_like(gw_acc)
            dws_ref[...] = jnp.zeros_like(dws_ref)
            dbsp_ref[...] = jnp.zeros_like(dbsp_ref)
            dgs_ref[...] = jnp.zeros_like(dgs_ref)
            dbs_ref[...] = jnp.zeros_like(dbs_ref)

        drb = dr_ref[...].astype(BF16)

        def group_cols(g):
            return slice(g * A_GROUP_DIM, (g + 1) * A_GROUP_DIM)

        tri = (lax.broadcasted_iota(jnp.int32, (CHUNK, CHUNK), 0)
               >= lax.broadcasted_iota(jnp.int32, (CHUNK, CHUNK), 1))
        lane = lax.broadcasted_iota(jnp.int32, (CHUNK, CHUNK), 1)
        ones = jnp.ones((CHUNK, A_GROUP_DIM), BF16)
        dbsp = jnp.zeros((CHUNK, CHUNK), F32)
        dy_next = _dot_nt(drb, wout_ref[group_cols(0), :])
        for g in range(A_GROUPS):
            wsg = jnp.where(tri, ws_ref[g], 0.0).astype(BF16)
            cols = group_cols(g)
            cols_z = slice(2 * A_WIDTH + g * A_GROUP_DIM, 2 * A_WIDTH + (g + 1) * A_GROUP_DIM)
            dy_g = dy_next
            if g + 1 < A_GROUPS:
                dy_next = _dot_nt(drb, wout_ref[group_cols(g + 1), :])
            gw_acc[cols, :] += _dot_tn(y_ref[:, cols], drb)
            both = jnp.zeros((CHUNK, 2 * CHUNK), F32)
            for ci in range(tm // CHUNK):
                rows = slice(ci * CHUNK, (ci + 1) * CHUNK)
                vn = (vh_ref[rows, cols].astype(F32) * lng_ref[:, cols] + lnb_ref[:, cols]).astype(BF16)
                s = _dot(wsg, vn) + bsp_ref[:, g:g + 1]
                sz, dsz = _silu_parts(z_ref[rows, cols].astype(F32))
                dy = dy_g[rows]
                t = dy * u_ref[rows, cols].astype(F32)
                dp_ref[rows, cols] = (dy * (s * sz)).astype(BF16)
                dp_ref[rows, cols_z] = (t * s * dsz).astype(BF16)
                ds_b = (t * sz).astype(BF16)
                both += _dot_nt(ds_b, jnp.concatenate([vn, ones], axis=0))
                dvn_scr[rows, cols] = _dot_tn(wsg, ds_b)
            dws_ref[g] += jnp.where(tri, both[:, 0:CHUNK], 0.0)
            dbsp = jnp.where(lane == g, both[:, CHUNK:2 * CHUNK], dbsp)
        dbsp_ref[...] += dbsp
        dvn = dvn_scr[...]
        vh_t = vh_ref[...].astype(F32)
        dgs_ref[...] += jnp.sum(dvn * vh_t, axis=0, keepdims=True)
        dbs_ref[...] += jnp.sum(dvn, axis=0, keepdims=True)
        dp_ref[:, A_WIDTH:2 * A_WIDTH] = _ln_bwd(dvn * lng_ref[...], vh_t, rv_ref[...]).astype(BF16)

        @pl.when(i == n_steps - 1)
        def _():
            gw_ref[...] = gw_acc[...].astype(BF16)

    wide = jax.ShapeDtypeStruct((1, A_WIDTH), F32)
    return pl.pallas_call(
        body, name="layer_a_bwd_mix", grid=(n_steps,),
        in_specs=[_rows(tm, D_MODEL), _rows(tm, A_WIDTH), _rows(tm, A_WIDTH), _rows(tm, A_WIDTH), _rows(tm, A_WIDTH),
                  _rows(tm, 1), _resident(w_out.shape), _const(lng.shape), _const(lnb.shape), _const(ws.shape),
                  _const(bsp_t.shape), _const(after.shape)],
        out_specs=[_rows(tm, 3 * A_WIDTH), _const(w_out.shape), _const(ws.shape), _const((CHUNK, CHUNK)),
                   _const((1, A_WIDTH)), _const((1, A_WIDTH))],
        out_shape=[jax.ShapeDtypeStruct((t_len, 3 * A_WIDTH), BF16), jax.ShapeDtypeStruct(w_out.shape, BF16),
                   jax.ShapeDtypeStruct(ws.shape, F32), jax.ShapeDtypeStruct((CHUNK, CHUNK), F32),
                   wide, wide],
        scratch_shapes=[pltpu.VMEM((tm, A_WIDTH), F32), pltpu.VMEM(w_out.shape, F32)],
        compiler_params=_params(),
    )(dr1, u, vh, z, y, rv, w_out, lng, lnb, ws, bsp_t, after)


def _layer_a_bwd_dx(dr1, dp, w_in, after, updates=()):
    t_len = dr1.shape[0]
    tm = TM_MM
    n_steps = t_len // tm
    n_upd = len(updates)

    def body(dr_ref, dp_ref, win_ref, after_ref, *refs):
        upd_in, dx_ref, upd_out = refs[:4 * n_upd], refs[4 * n_upd], refs[4 * n_upd + 1:]
        dx_ref[...] = ALPHA * dr_ref[...] + _dot_nt(dp_ref[...], win_ref[...])
        for k in range(n_upd):
            w_ref, g_ref, m_ref, v_ref = upd_in[4 * k:4 * k + 4]
            g_out, d_ref, nm_ref, nv_ref = upd_out[4 * k:4 * k + 4]
            g_out[...] = g_ref[...]
            _adamw_update(w_ref, g_ref, m_ref, v_ref, d_ref, nm_ref, nv_ref)

    upd_specs, upd_shapes, upd_args = [], [], []
    for w, g, m, v in updates:
        rows, cols = w.shape
        upd_specs.append(pl.BlockSpec((rows // n_steps, cols), lambda i: (i, 0)))
        upd_shapes.append(jax.ShapeDtypeStruct((rows, cols), F32))
        upd_args += [w, g, m, v]
    return pl.pallas_call(
        body, name="layer_a_bwd_dx", grid=(n_steps,),
        in_specs=[_rows(tm, D_MODEL), _rows(tm, 3 * A_WIDTH), _resident(w_in.shape), _const(after.shape)]
        + [s for s in upd_specs for _ in range(4)],
        out_specs=[_rows(tm, D_MODEL)] + [s for s in upd_specs for _ in range(4)],
        out_shape=[jax.ShapeDtypeStruct((t_len, D_MODEL), F32)] + [s for s in upd_shapes for _ in range(4)],
        compiler_params=_params(),
    )(dr1, dp, w_in, after, *upd_args)


def _layer_a_bwd_win(xt, dp, after):
    t_len = xt.shape[1]
    tm = TM_WIN
    n_steps = t_len // tm
    shard_cols = 3 * A_WIDTH // N_CHIPS
    half_rows = D_MODEL // 2

    def body(xt_ref, dp_ref, after_ref, gw_ref, acc):
        i = pl.program_id(1)

        @pl.when(i == 0)
        def _():
            acc[...] = jnp.zeros_like(acc)

        acc[...] += _dot(xt_ref[...], dp_ref[...])

        @pl.when(i == n_steps - 1)
        def _():
            for c in range(2):
                gw_ref[0, c] = acc[c * half_rows:(c + 1) * half_rows, :].astype(BF16)

    return pl.pallas_call(
        body, name="layer_a_bwd_win", grid=(N_CHIPS, n_steps),
        in_specs=[pl.BlockSpec((D_MODEL, tm), lambda j, i: (0, i)),
                  pl.BlockSpec((tm, shard_cols), lambda j, i: (i, j)), _const(after.shape)],
        out_specs=pl.BlockSpec((1, 2, half_rows, shard_cols), lambda j, i: (j, 0, 0, 0)),
        out_shape=jax.ShapeDtypeStruct((N_CHIPS, 2, half_rows, shard_cols), BF16),
        scratch_shapes=[pltpu.VMEM((D_MODEL, shard_cols), F32)],
        compiler_params=_params(("arbitrary", "arbitrary")),
    )(xt, dp, after)


def _bucket_onehot():
    dist = jnp.arange(CHUNK, dtype=jnp.int32)[None, :]
    max_exact = REL_BUCKETS // 2
    df = jnp.maximum(dist, 1).astype(F32)
    large = max_exact + (jnp.log(df / max_exact) / math.log(CHUNK / max_exact)
                         * (REL_BUCKETS - max_exact)).astype(jnp.int32)
    bucket = jnp.where(dist < max_exact, dist, jnp.minimum(large, REL_BUCKETS - 1))
    onehot = bucket == jnp.arange(REL_BUCKETS, dtype=jnp.int32)[:, None]
    return onehot.astype(F32)


def _bias_expand(rel_t, onehot):
    def body(rel_ref, oh_ref, out_ref):
        by_distance = jnp.dot(rel_ref[...], oh_ref[...], preferred_element_type=F32,
                              precision=lax.Precision.HIGHEST)
        for h in range(N_HEADS):
            rows = jnp.broadcast_to(by_distance[h:h + 1, :], (2 * CHUNK, CHUNK))
            out_ref[h] = pltpu.roll(rows, 0, 1, stride=1, stride_axis=0)

    return pl.pallas_call(
        body, name="bias_expand",
        out_shape=jax.ShapeDtypeStruct((N_HEADS, 2 * CHUNK, CHUNK), F32),
    )(rel_t, onehot)


def _bias_reduce(oh_ref, db_ref):
    sublane = lax.broadcasted_iota(jnp.int32, (_SUBLANES, CHUNK), 0)
    rows = []
    for h in range(N_HEADS):
        part = db_ref[h, 0:_SUBLANES, :]
        for a in range(1, 2 * CHUNK // _SUBLANES):
            tile = db_ref[h, a * _SUBLANES:(a + 1) * _SUBLANES, :]
            back = (-a * _SUBLANES) % CHUNK
            part += pltpu.roll(tile, back, 1) if back else tile
        total = jnp.where(sublane == 0, part, 0.0)
        for s in range(1, _SUBLANES):
            total += jnp.where(sublane == s, pltpu.roll(part, CHUNK - s, 1), 0.0)
        rows.append(jnp.sum(total, axis=0, keepdims=True))
    by_distance = jnp.concatenate(rows, axis=0)
    return lax.dot_general(by_distance, oh_ref[...], (((1,), (1,)), ((), ())),
                           preferred_element_type=F32, precision=lax.Precision.HIGHEST)


_SMALL_SHAPES = dict(w_spatial=(A_GROUPS, CHUNK, CHUNK), b_spatial=(A_GROUPS, CHUNK), attn_sinks=(1, N_HEADS),
                     rel_bias=(N_HEADS, REL_BUCKETS), post_ln_g=(2, D_MODEL), post_ln_b=(2, D_MODEL),
                     sgu_ln_g=(1, A_WIDTH), sgu_ln_b=(1, A_WIDTH), loss=(1, 1))
_SMALL_ORDER = tuple(_SMALL_SHAPES)


def _small_rows(name):
    shape = _SMALL_SHAPES[name]
    rows = math.prod(shape[:-1]) if shape[-1] < _LANES else math.prod(shape) // _LANES
    return -(-rows // _SUBLANES) * _SUBLANES


def _small_offset(name):
    return sum(_small_rows(n) for n in _SMALL_ORDER[:_SMALL_ORDER.index(name)])


def _pack_small(dws, dbsp, dsink, dbias, onehot, post_g, post_b, dgs, dbs, loss_vec):
    def body(dws_ref, dbsp_ref, dsink_ref, db_ref, oh_ref, g1_ref, g2_ref, b1_ref, b2_ref, dgs_ref, dbs_ref,
             loss_ref, out_ref):
        out_ref[...] = jnp.zeros_like(out_ref)

        def put_flat(name, refs):
            row = _small_offset(name)
            for ref in refs:
                for k in range(ref.shape[1] // _LANES):
                    out_ref[row:row + 1, :] = ref[:, k * _LANES:(k + 1) * _LANES]
                    row += 1

        row = _small_offset("w_spatial")
        for g in range(A_GROUPS):
            out_ref[row + g * CHUNK:row + (g + 1) * CHUNK, :] = dws_ref[g]
        row = _small_offset("b_spatial")
        out_ref[row:row + A_GROUPS, :] = dbsp_ref[...].T[0:A_GROUPS, :]
        lane = lax.broadcasted_iota(jnp.int32, (1, _LANES), 1)
        sinks = jnp.zeros((1, _LANES), F32)
        for h in range(N_HEADS):
            per_query = dsink_ref[h // GROUP:h // GROUP + 1, (h % GROUP) * CHUNK:(h % GROUP + 1) * CHUNK]
            sinks = jnp.where(lane == h, jnp.sum(per_query, axis=1, keepdims=True), sinks)
        row = _small_offset("attn_sinks")
        out_ref[row:row + 1, :] = sinks
        row = _small_offset("rel_bias")
        out_ref[row:row + N_HEADS, 0:REL_BUCKETS] = _bias_reduce(oh_ref, db_ref)
        put_flat("post_ln_g", [g1_ref, g2_ref])
        put_flat("post_ln_b", [b1_ref, b2_ref])
        put_flat("sgu_ln_g", [dgs_ref])
        put_flat("sgu_ln_b", [dbs_ref])
        row = _small_offset("loss")
        out_ref[row:row + 1, 0:1] = (0.5 / D_MODEL) * jnp.sum(loss_ref[...], axis=1, keepdims=True)

    total_rows = sum(_small_rows(n) for n in _SMALL_ORDER)
    return pl.pallas_call(
        body, name="pack_small",
        out_shape=jax.ShapeDtypeStruct((total_rows, _LANES), F32),
    )(dws, dbsp, dsink, dbias, onehot, *post_g, *post_b, dgs, dbs, loss_vec)


def _place():
    return lax.axis_index("x"), lax.axis_index("y"), lax.axis_index("c")


RELAY_PIECES = 4


def _shard_window(full_ref, shard_shape, col_sharded, s, half, piece=None):
    rows, cols = shard_shape
    if half is None:
        start, size = 0, rows
    elif piece is None:
        start, size = half * (rows // 2), rows // 2
    else:
        size = rows // 2 // RELAY_PIECES
        start = (half * RELAY_PIECES + piece) * size
    if col_sharded:
        return full_ref.at[pl.ds(start, size), pl.ds(s * cols, cols)]
    return full_ref.at[pl.ds(s * rows + start, size), :]


def _other_chips(x, y):
    return [(1 - x, y), (x, 1 - y), (1 - x, 1 - y)]


def _gather_weights(shards, col_sharded, fetch, ln_shard, tokens=None):
    n_w = len(shards)
    fetched = [w for w in range(n_w) if fetch[w]]
    full_shapes = []
    for w, cs in zip(shards, col_sharded):
        r, c = w.shape
        full_shapes.append((r, c * N_CHIPS) if cs else (r * N_CHIPS, c))

    n_tok = 0 if tokens is None else 1
    tok_tile = TM_MM
    n_tiles = 0 if tokens is None else tokens.shape[0] // tok_tile

    def body(*refs):
        refs = list(refs)
        in_refs = [refs.pop(0) for _ in range(n_w)]
        ln_ref = refs.pop(0)
        tok_ref = refs.pop(0) if n_tok else None
        full_refs = [refs.pop(0) for _ in range(n_w)]
        ln_full = refs.pop(0)
        prod_ref = refs.pop(0) if n_tok else None
        raw = [refs.pop(0) for _ in range(n_w)]
        stage = [refs.pop(0) for _ in range(n_w)]
        send_sems, recv_sems, load_sems, local_sems, ln_send, ln_recv = refs[:6]
        tok_buf, prod_buf, tok_sems, prod_sems = refs[6:] if n_tok else (None,) * 4
        x, y, c = _place()
        s_me = 2 * x + y
        chips = _other_chips(x, y)
        pieces = range(RELAY_PIECES)

        def shard_window(w, s, half, piece=None):
            return _shard_window(full_refs[w], shards[w].shape, col_sharded[w], s, half, piece)

        def piece_rows(w, half, piece):
            rows = shards[w].shape[0] // 2 // RELAY_PIECES
            return pl.ds(pl.multiple_of((half * RELAY_PIECES + piece) * rows, rows), rows)

        def ici_copy(w, k, sender_shard, piece):
            idx = (w * 3 + k) * RELAY_PIECES + piece
            return pltpu.make_async_remote_copy(
                src_ref=stage[w].at[piece_rows(w, c, piece), :], dst_ref=shard_window(w, sender_shard, c, piece),
                send_sem=send_sems.at[idx], recv_sem=recv_sems.at[idx],
                device_id=(*chips[k], c), device_id_type=MESH)

        def d2d_copy(w, k, half, piece):
            s_k = 2 * chips[k][0] + chips[k][1]
            win = shard_window(w, s_k, half, piece)
            idx = (3 * n_w + w * 3 + k) * RELAY_PIECES + piece
            return pltpu.make_async_remote_copy(
                src_ref=win, dst_ref=win, send_sem=send_sems.at[idx], recv_sem=recv_sems.at[idx],
                device_id=(x, y, 1 - c), device_id_type=MESH)

        def ln_copy(k, slot):
            return pltpu.make_async_remote_copy(
                src_ref=ln_ref, dst_ref=ln_full.at[slot], send_sem=ln_send.at[k], recv_sem=ln_recv.at[k],
                device_id=(*chips[k], c), device_id_type=MESH)

        loads = []

        def load(w, rows):
            window = (rows, slice(None)) if rows is not None else (slice(None), slice(None))
            cp = pltpu.make_async_copy(in_refs[w].at[window], raw[w].at[window], load_sems.at[len(loads)])
            cp.start()
            loads.append((cp, w, window))

        for half in (c, 1 - c):
            for w in fetched:
                for q in pieces:
                    load(w, piece_rows(w, half, q))
        for w in range(n_w):
            if not fetch[w]:
                load(w, None)

        def to_bf16(k):
            cp, w, window = loads[k]
            cp.wait()
            stage[w][window] = raw[w][window].astype(BF16)

        ln_full[s_me] = ln_ref[...]
        def shard_of(k):
            return 2 * chips[k][0] + chips[k][1]

        relay_from = jnp.where(c == 0, shard_of(0), shard_of(1))
        relay_to = (jnp.where(c == 0, x, 1 - x), jnp.where(c == 0, 1 - y, y), c)

        def relay_copy(w, sender_shard, piece):
            win = shard_window(w, sender_shard, c, piece)
            idx = (w * 3 + 2) * RELAY_PIECES + piece
            return pltpu.make_async_remote_copy(
                src_ref=win, dst_ref=win, send_sem=send_sems.at[idx], recv_sem=recv_sems.at[idx],
                device_id=relay_to, device_id_type=MESH)

        first = [ln_copy(k, s_me) for k in range(3)]
        for cp in first:
            cp.start()
        n_sent = 0
        for w in fetched:
            for q in pieces:
                to_bf16(n_sent)
                n_sent += 1
                for k in range(2):
                    cp = ici_copy(w, k, s_me, q)
                    cp.start()
                    first.append(cp)
        for k in range(n_sent, len(loads)):
            to_bf16(k)
        own = [pltpu.make_async_copy(stage[w], shard_window(w, s_me, None), local_sems.at[w]) for w in range(n_w)]
        for cp in own:
            cp.start()

        def tok_copy(t):
            return pltpu.make_async_copy(tok_ref.at[pl.ds(t * tok_tile, tok_tile), :], tok_buf.at[t % 2],
                                         tok_sems.at[t % 2])

        def prod_copy(t):
            return pltpu.make_async_copy(prod_buf.at[t % 2], prod_ref.at[pl.ds(t * tok_tile, tok_tile), :],
                                         prod_sems.at[t % 2])

        def product_tiles(tiles):
            for t in tiles:
                if t + 1 < n_tiles:
                    tok_copy(t + 1).start()
                tok_copy(t).wait()
                if t >= 2:
                    prod_copy(t - 2).wait()
                prod_buf[t % 2] = _dot(tok_buf[t % 2].astype(BF16), stage[0][...])
                prod_copy(t).start()

        if n_tiles:
            tok_copy(0).start()
        share = [4, 3, 3, 3, 3, 0, 0, 0]
        assert len(share) == 2 * RELAY_PIECES
        bounds = [sum(share[:k]) * n_tiles // sum(share) for k in range(len(share) + 1)]
        passed = []
        for w in fetched:
            for q in pieces:
                if w == fetched[0]:
                    product_tiles(range(bounds[q], bounds[q + 1]))
                for k in range(2):
                    ici_copy(w, k, shard_of(k), q).wait_recv()
                relay = relay_copy(w, relay_from, q)
                relay.start()
                passed.append(relay)
                for k in range(2):
                    fwd = d2d_copy(w, k, c, q)
                    fwd.start()
                    passed.append(fwd)
        for w in fetched:
            for q in pieces:
                if w == fetched[0]:
                    product_tiles(range(bounds[RELAY_PIECES + q], bounds[RELAY_PIECES + q + 1]))
                relay_copy(w, shard_of(2), q).wait_recv()
                fwd = d2d_copy(w, 2, c, q)
                fwd.start()
                passed.append(fwd)
        for w in fetched:
            for k in range(3):
                for q in pieces:
                    d2d_copy(w, k, 1 - c, q).wait_recv()
        for k in range(3):
            ln_copy(k, 2 * chips[k][0] + chips[k][1]).wait_recv()
        for cp in first + passed:
            cp.wait_send()
        for cp in own:
            cp.wait()
        for t in range(max(n_tiles - 2, 0), n_tiles):
            prod_copy(t).wait()

    vmem = pl.BlockSpec(memory_space=pltpu.VMEM)
    hbm = pl.BlockSpec(memory_space=pl.ANY)
    prod_cols = shards[0].shape[1]
    tok_args = [] if tokens is None else [tokens]
    tok_out = [] if tokens is None else [jax.ShapeDtypeStruct((tokens.shape[0], prod_cols), F32)]
    tok_scratch = [] if tokens is None else [
        pltpu.VMEM((2, tok_tile, tokens.shape[1]), F32), pltpu.VMEM((2, tok_tile, prod_cols), F32),
        pltpu.SemaphoreType.DMA((2,)), pltpu.SemaphoreType.DMA((2,))]
    return pl.pallas_call(
        body, name="gather_weights",
        in_specs=[hbm] * n_w + [vmem] + [hbm] * n_tok,
        out_specs=[hbm] * n_w + [vmem] + [hbm] * n_tok,
        out_shape=[jax.ShapeDtypeStruct(s, BF16) for s in full_shapes]
        + [jax.ShapeDtypeStruct((N_CHIPS,) + ln_shard.shape, F32)] + tok_out,
        scratch_shapes=[pltpu.VMEM(w.shape, F32) for w in shards] + [pltpu.VMEM(w.shape, BF16) for w in shards]
        + [pltpu.SemaphoreType.DMA((6 * RELAY_PIECES * n_w,)), pltpu.SemaphoreType.DMA((6 * RELAY_PIECES * n_w,)),
           pltpu.SemaphoreType.DMA((2 * RELAY_PIECES * len(fetched) + n_w - len(fetched),)),
           pltpu.SemaphoreType.DMA((n_w,)), pltpu.SemaphoreType.DMA((3,)), pltpu.SemaphoreType.DMA((3,))]
        + tok_scratch,
        compiler_params=pltpu.CompilerParams(vmem_limit_bytes=VMEM_LIMIT),
    )(*shards, ln_shard, *tok_args)


def _fetch_copy(full_ref, shard_shape, col_sharded, sender_shard, send_sems, recv_sems, idx, chip, c):
    win = _shard_window(full_ref, shard_shape, col_sharded, sender_shard, None)
    return pltpu.make_async_remote_copy(src_ref=win, dst_ref=win, send_sem=send_sems.at[idx],
                                        recv_sem=recv_sems.at[idx], device_id=(*chip, c), device_id_type=MESH)


def _fetch_start(fulls, shard_shapes, col_sharded):
    n = len(fulls)

    def body(*refs):
        full = refs[:n]
        send_sems, recv_sems = refs[n], refs[n + 1]
        token = refs[-1]
        x, y, c = _place()
        for w in range(n):
            for k, chip in enumerate(_other_chips(x, y)):
                _fetch_copy(full[w], shard_shapes[w], col_sharded[w], 2 * x + y, send_sems, recv_sems, w * 3 + k,
                            chip, c).start()
        token[...] = jnp.zeros_like(token)

    outs = pl.pallas_call(
        body, name="fetch_start",
        out_shape=(pltpu.SemaphoreType.DMA((3 * n,)), pltpu.SemaphoreType.DMA((3 * n,)),
                   *[pltpu.HBM(f.shape, f.dtype) for f in fulls], jax.ShapeDtypeStruct((8, 128), F32)),
        in_specs=[_HBM] * n,
        out_specs=(_SEM, _SEM, *([_HBM] * n), pl.BlockSpec(memory_space=pltpu.VMEM)),
        input_output_aliases={i: 2 + i for i in range(n)},
        compiler_params=pltpu.CompilerParams(has_side_effects=pltpu.SideEffectType.DATAFLOW_SIDE_EFFECTING),
    )(*[pltpu.with_memory_space_constraint(f, pltpu.HBM) for f in fulls])
    return dict(send=outs[0], recv=outs[1], full=list(outs[2:2 + n])), outs[-1]


def _fetch_wait(group, shard_shapes, col_sharded, after):
    n = len(group["full"])

    def body(*refs):
        full = refs[:n]
        send_sems, recv_sems = refs[n], refs[n + 1]
        x, y, c = _place()
        for w in range(n):
            for k, chip in enumerate(_other_chips(x, y)):
                _fetch_copy(full[w], shard_shapes[w], col_sharded[w], 2 * x + y, send_sems, recv_sems, w * 3 + k,
                            chip, c).wait_send()
                _fetch_copy(full[w], shard_shapes[w], col_sharded[w], 2 * chip[0] + chip[1], send_sems, recv_sems,
                            w * 3 + k, chip, c).wait_recv()

    outs = pl.pallas_call(
        body, name="fetch_wait", out_shape=tuple(pltpu.HBM(f.shape, f.dtype) for f in group["full"]),
        in_specs=[_HBM] * n + [_SEM, _SEM, pl.BlockSpec(memory_space=pl.ANY)],
        out_specs=tuple([_HBM] * n), input_output_aliases={i: i for i in range(n)},
        compiler_params=pltpu.CompilerParams(has_side_effects=pltpu.SideEffectType.DATAFLOW_SIDE_EFFECTING),
    )(*group["full"], group["send"], group["recv"], after)
    return list(outs)


_HBM = pl.BlockSpec(memory_space=pltpu.HBM)
_SEM = pl.BlockSpec(memory_space=pltpu.SEMAPHORE)
_N_PEER = N_DEV - 1


def _peer(x, y, c, k):
    return (x + (k >> 2)) % 2, (y + ((k >> 1) & 1)) % 2, (c + (k & 1)) % 2


def _exchange_copy(src_ref, land_ref, sliced, send_sems, recv_sems, idx, x, y, c, k):
    px, py, pc = _peer(x, y, c, k)
    src = src_ref.at[4 * px + 2 * py + pc] if sliced else src_ref
    return pltpu.make_async_remote_copy(
        src_ref=src, dst_ref=land_ref.at[4 * x + 2 * y + c],
        send_sem=send_sems.at[idx], recv_sem=recv_sems.at[idx], device_id=(px, py, pc), device_id_type=MESH)


def _exchange_start(tag, arrays, sliced):
    n = len(arrays)
    lands = [lax.empty(a.shape if s else (N_DEV,) + a.shape, a.dtype) for a, s in zip(arrays, sliced)]

    def body(*refs):
        src, land = refs[:n], refs[n:2 * n]
        send_sems, recv_sems = refs[2 * n], refs[2 * n + 1]
        token = refs[-1]
        x, y, c = _place()
        for w in range(n):
            for k in range(1, N_DEV):
                _exchange_copy(src[w], land[w], sliced[w], send_sems, recv_sems, w * _N_PEER + k - 1, x, y, c, k).start()
        token[...] = jnp.zeros_like(token)

    outs = pl.pallas_call(
        body, name="exchange_start_" + tag,
        out_shape=(pltpu.SemaphoreType.DMA((n * _N_PEER,)), pltpu.SemaphoreType.DMA((n * _N_PEER,)),
                   *[pltpu.HBM(a.shape, a.dtype) for a in arrays], *[pltpu.HBM(l.shape, l.dtype) for l in lands],
                   jax.ShapeDtypeStruct((8, 128), F32)),
        in_specs=[_HBM] * (2 * n),
        out_specs=(_SEM, _SEM, *([_HBM] * (2 * n)), pl.BlockSpec(memory_space=pltpu.VMEM)),
        input_output_aliases={i: 2 + i for i in range(2 * n)},
        compiler_params=pltpu.CompilerParams(has_side_effects=pltpu.SideEffectType.DATAFLOW_SIDE_EFFECTING),
    )(*[pltpu.with_memory_space_constraint(a, pltpu.HBM) for a in arrays],
      *[pltpu.with_memory_space_constraint(l, pltpu.HBM) for l in lands])
    return dict(send=outs[0], recv=outs[1], src=list(outs[2:2 + n]), land=list(outs[2 + n:2 + 2 * n]),
                sliced=list(sliced)), outs[-1]


def _exchange_wait(tag, groups, after):
    counts = [len(g["src"]) for g in groups]
    total = sum(counts)

    def body(*refs):
        pos = 0
        x, y, c = _place()
        for g, n in zip(groups, counts):
            src, land = refs[pos:pos + n], refs[pos + n:pos + 2 * n]
            send_sems, recv_sems = refs[pos + 2 * n], refs[pos + 2 * n + 1]
            pos += 2 * n + 2
            for w in range(n):
                for k in range(1, N_DEV):
                    cp = _exchange_copy(src[w], land[w], g["sliced"][w], send_sems, recv_sems,
                                        w * _N_PEER + k - 1, x, y, c, k)
                    cp.wait_send()
                    cp.wait_recv()

    operands, in_specs, aliases, out_shape = [], [], {}, []
    for g in groups:
        for a in g["src"] + g["land"]:
            aliases[len(operands)] = len(out_shape)
            out_shape.append(pltpu.HBM(a.shape, a.dtype))
            operands.append(a)
            in_specs.append(_HBM)
        operands += [g["send"], g["recv"]]
        in_specs += [_SEM, _SEM]
    operands.append(after)
    in_specs.append(pl.BlockSpec(memory_space=pl.ANY))
    outs = pl.pallas_call(
        body, name="exchange_wait_" + tag, out_shape=tuple(out_shape), in_specs=in_specs,
        out_specs=tuple([_HBM] * (2 * total)), input_output_aliases=aliases,
        compiler_params=pltpu.CompilerParams(has_side_effects=pltpu.SideEffectType.DATAFLOW_SIDE_EFFECTING),
    )(*operands)
    srcs, lands, pos = [], [], 0
    for n in counts:
        srcs += list(outs[pos:pos + n])
        lands += list(outs[pos + n:pos + 2 * n])
        pos += 2 * n
    return srcs, lands


def _sum_and_swap(tag, pieces, lands, small=None, small_land=None):
    n_w = len(pieces)
    n_small = 0 if small is None else 1

    def body(*refs):
        g_refs, land_refs = refs[:n_w], refs[n_w:2 * n_w]
        pos = 2 * n_w + 2 * n_small
        out_refs = refs[pos:pos + n_w]
        pos += n_w + n_small
        bufs = refs[pos:pos + n_w]
        load_sems, swap_send, swap_recv = refs[pos + n_w + 2 * n_small:]
        x, y, c = _place()
        me = 4 * x + 2 * y + c

        def slot(k):
            px, py, pc = _peer(x, y, c, k)
            return 4 * px + 2 * py + pc

        def swap_copy(w, half):
            rows = pieces[w].shape[1]
            win = out_refs[w].at[pl.ds(pl.multiple_of(half * rows, rows), rows), :]
            return pltpu.make_async_remote_copy(
                src_ref=win, dst_ref=win, send_sem=swap_send.at[w], recv_sem=swap_recv.at[w],
                device_id=(x, y, 1 - c), device_id_type=MESH)

        loads = []
        for w in range(n_w):
            per_w = [pltpu.make_async_copy(g_refs[w].at[me], bufs[w].at[me], load_sems.at[w * N_DEV])]
            per_w += [pltpu.make_async_copy(land_refs[w].at[slot(k)], bufs[w].at[slot(k)], load_sems.at[w * N_DEV + k])
                      for k in range(1, N_DEV)]
            loads.append(per_w)
        small_loads = []
        if n_small:
            small_ref, small_land_ref = refs[2 * n_w], refs[2 * n_w + 1]
            small_out = refs[2 * n_w + 2 + n_w]
            small_buf, small_sems = refs[pos + n_w], refs[pos + n_w + 1]
            small_loads = [pltpu.make_async_copy(small_land_ref.at[slot(k)], small_buf.at[slot(k)],
                                                 small_sems.at[k - 1]) for k in range(1, N_DEV)]
        for cp in [cp for per_w in loads for cp in per_w] + small_loads:
            cp.start()
        if n_small:
            small_buf[me] = small_ref[...]
        swaps = []
        for w in range(n_w):
            for cp in loads[w]:
                cp.wait()
            rows = pieces[w].shape[1]
            total = bufs[w][0].astype(F32)
            for p in range(1, N_DEV):
                total += bufs[w][p].astype(F32)
            out_refs[w][pl.ds(pl.multiple_of(c * rows, rows), rows), :] = total
            sw = swap_copy(w, c)
            sw.start()
            swaps.append(sw)
        if n_small:
            for cp in small_loads:
                cp.wait()
            total = small_buf[0]
            for p in range(1, N_DEV):
                total += small_buf[p]
            small_out[...] = total
        for w in range(n_w):
            swap_copy(w, 1 - c).wait_recv()
        for sw in swaps:
            sw.wait_send()

    vmem = pl.BlockSpec(memory_space=pltpu.VMEM)
    hbm = pl.BlockSpec(memory_space=pl.ANY)
    small_args = [small, small_land] if n_small else []
    small_shapes = [jax.ShapeDtypeStruct(small.shape, F32)] if n_small else []
    small_scratch = ([pltpu.VMEM((N_DEV,) + small.shape, F32), pltpu.SemaphoreType.DMA((_N_PEER,))]
                     if n_small else [])
    return pl.pallas_call(
        body, name="sum_and_swap_" + tag,
        in_specs=[hbm] * (2 * n_w) + [vmem, hbm] * n_small,
        out_specs=[vmem] * (n_w + n_small),
        out_shape=[jax.ShapeDtypeStruct((2 * p.shape[1], p.shape[2]), F32) for p in pieces] + small_shapes,
        scratch_shapes=[pltpu.VMEM(p.shape, BF16) for p in pieces] + small_scratch
        + [pltpu.SemaphoreType.DMA((n_w * N_DEV,)), pltpu.SemaphoreType.DMA((n_w,)),
           pltpu.SemaphoreType.DMA((n_w,))],
        compiler_params=pltpu.CompilerParams(vmem_limit_bytes=VMEM_LIMIT),
    )(*pieces, *lands, *small_args)


def _adamw_values(w, g_t, m, v):
    c1 = 1.0 - ADAM_B1 ** ADAM_STEP
    c2 = 1.0 - ADAM_B2 ** ADAM_STEP
    nm = ADAM_B1 * m + (1.0 - ADAM_B1) * g_t
    nv = ADAM_B2 * v + (1.0 - ADAM_B2) * (g_t * g_t)
    return -ADAM_LR * ((nm / c1) / (jnp.sqrt(nv / c2) + ADAM_EPS) + ADAM_WD * w), nm, nv


def _adamw_update(w_ref, g_ref, m_ref, v_ref, d_ref, nm_ref, nv_ref):
    d_ref[...], nm_ref[...], nv_ref[...] = _adamw_values(w_ref[...], g_ref[...], m_ref[...], v_ref[...])


def _adamw_small(packed, shard_index, names, weights, moments_m, moments_v):
    n = len(names)
    shapes = [weights[name].shape for name in names]
    flat = [a[name].reshape(-1, a[name].shape[-1]) for name in names for a in (weights, moments_m, moments_v)]

    def body(packed_ref, shard_ref, *refs):
        loss_row = _small_offset("loss")
        refs[-1][...] = packed_ref[loss_row:loss_row + 1, 0:1]
        for k, name in enumerate(names):
            w_ref, m_ref, v_ref = refs[3 * k:3 * k + 3]
            g_ref, d_ref, nm_ref, nv_ref = refs[3 * n + 4 * k:3 * n + 4 * k + 4]
            rows, cols = w_ref.shape
            first = _small_offset(name)
            if cols <= _LANES:
                blocks = [(slice(0, rows), packed_ref[first:first + rows, 0:cols])]
            else:
                per_row = cols // _LANES
                if cols < _SMALL_SHAPES[name][-1]:
                    first = first + shard_ref[0] * per_row
                blocks = [(slice(i, i + 1),
                           jnp.concatenate([packed_ref[pl.ds(first + i * per_row + j, 1), :] for j in range(per_row)],
                                           axis=1)) for i in range(rows)]
            for at, g_t in blocks:
                g_ref[at, :] = g_t
                d_ref[at, :], nm_ref[at, :], nv_ref[at, :] = _adamw_values(w_ref[at, :], g_t, m_ref[at, :],
                                                                           v_ref[at, :])

    vmem = pl.BlockSpec(memory_space=pltpu.VMEM)
    outs = pl.pallas_call(
        body, name="adamw_small",
        in_specs=[vmem, pl.BlockSpec(memory_space=pltpu.SMEM)] + [vmem] * (3 * n),
        out_shape=[jax.ShapeDtypeStruct(flat[3 * k].shape, F32) for k in range(n) for _ in range(4)]
        + [jax.ShapeDtypeStruct((1, 1), F32)],
    )(packed, shard_index.reshape(1).astype(jnp.int32), *flat)
    return [tuple(o.reshape(shapes[k]) for o in outs[4 * k:4 * k + 4]) for k in range(n)], outs[-1].reshape(())


def _adamw(label, w, g, m, v):
    shape = w.shape
    cols = shape[-1]
    rows = w.size // cols
    args = [a.reshape(rows, cols) for a in (w, g, m, v)]

    def body(w_ref, g_ref, m_ref, v_ref, g_out, d_ref, nm_ref, nv_ref):
        g_out[...] = g_ref[...]
        _adamw_update(w_ref, g_ref, m_ref, v_ref, d_ref, nm_ref, nv_ref)

    block_rows = 256 if rows % 256 == 0 and rows > 256 else rows
    spec = pl.BlockSpec((block_rows, cols), lambda i: (i, 0))
    outs = pl.pallas_call(
        body, name="adamw_" + label, grid=(rows // block_rows,),
        in_specs=[spec] * 4, out_specs=[spec] * 4,
        out_shape=[jax.ShapeDtypeStruct((rows, cols), F32)] * 4,
        compiler_params=_params(),
    )(*args)
    return [o.reshape(shape) for o in outs]


def _no_send(tag, arrays, sliced):
    return jnp.zeros((8, 128), F32)


def _local_step(x, tgt, w_in_a, later_weights, first_after, sgu_ln_g, sgu_ln_b, w_spatial, b_spatial,
                attn_sinks, rel_bias, post_ln_g, post_ln_b, send=_no_send, own_product=None):
    bsp_t = b_spatial.T
    g1, b1 = post_ln_g[0:1], post_ln_b[0:1]
    g2, b2 = post_ln_g[1:2], post_ln_b[1:2]
    onehot = _bucket_onehot()
    bias = _bias_expand(rel_bias.T, onehot)
    win = _window_tables()

    if own_product is None:
        xt, u, vh, z, rv, y = _layer_a_fwd(x, w_in_a, None, sgu_ln_g, sgu_ln_b, w_spatial, bsp_t, first_after)
    else:
        chip, p_own = own_product
        xt, u, vh, z, rv, y = lax.switch(
            chip, [functools.partial(_layer_a_fwd, own=own) for own in range(N_CHIPS)],
            x, w_in_a, p_own, sgu_ln_g, sgu_ln_b, w_spatial, bsp_t, first_after)
    w_out_a, w_kv, w_in_b, w_out_b = later_weights(y)
    xh1, rstd1, q, zb, kd, vd = _layer_b_proj(x, y, w_out_a, g1, b1, w_in_b, w_kv)
    o, probs, sink_probs, dr2, loss_vec, dg2, db2 = _layer_b_fwd(q, zb, kd, vd, bias, win, attn_sinks, xh1, g1, b1,
                                                                 w_out_b, g2, b2, tgt)
    dq, dzb, dkd, dvd, carry_k, carry_v, gw_out_b, dsink, dbias = _layer_b_bwd_attn(
        dr2, zb, o, q, kd, vd, probs, sink_probs, w_out_b)
    dr1, dg1, db1, gw_in_b, gw_kv = _layer_b_bwd_proj(xh1, rstd1, g1, b1, dr2, dq, dzb, dkd, dvd, carry_k, carry_v,
                                                      w_in_b, w_kv)
    gw_out_b = gw_out_b.reshape(N_DEV, -1, D_MODEL)
    gw_kv = gw_kv.reshape(N_DEV, -1, 2 * PAIR)
    after = send("b", [gw_out_b, gw_in_b, gw_kv], [True, True, True])
    dp, gw_out_a, dws, dbsp, dgs, dbs = _layer_a_bwd_mix(dr1, u, vh, z, y, rv, w_out_a, sgu_ln_g, sgu_ln_b,
                                                         w_spatial, bsp_t, after)
    gw_out_a = gw_out_a.reshape(N_DEV, -1, D_MODEL)
    small = _pack_small(dws, dbsp, dsink, dbias, onehot, (dg1, dg2), (db1, db2), dgs, dbs, loss_vec)
    after = send("a_out", [gw_out_a, small], [True, False])
    gw_in_a = _layer_a_bwd_win(xt, dp, after).reshape(N_DEV, D_MODEL // 2, -1)
    after = send("a_in", [gw_in_a], [True])
    after, updates = after if isinstance(after, tuple) else (after, ())
    grad_x, *updated = _layer_a_bwd_dx(dr1, dp, w_in_a, after, updates)

    pieces = [gw_in_a, gw_out_a, gw_kv, gw_in_b, gw_out_b]
    return grad_x, pieces, small, updated


def kernel(x, w_in_a, sgu_ln_g, sgu_ln_b, w_spatial, b_spatial, w_out_a, w_kv, w_in_b, attn_sinks, rel_bias, w_out_b, post_ln_g, post_ln_b, loss_target, m_w_in_a, m_sgu_ln_g, m_sgu_ln_b, m_w_spatial, m_b_spatial, m_w_out_a, m_w_kv, m_w_in_b, m_attn_sinks, m_rel_bias, m_w_out_b, m_post_ln_g, m_post_ln_b, v_w_in_a, v_sgu_ln_g, v_sgu_ln_b, v_w_spatial, v_b_spatial, v_w_out_a, v_w_kv, v_w_in_b, v_attn_sinks, v_rel_bias, v_w_out_b, v_post_ln_g, v_post_ln_b):
    weights = dict(w_in_a=w_in_a, sgu_ln_g=sgu_ln_g, sgu_ln_b=sgu_ln_b, w_spatial=w_spatial, b_spatial=b_spatial,
                   w_out_a=w_out_a, w_kv=w_kv, w_in_b=w_in_b, attn_sinks=attn_sinks, rel_bias=rel_bias,
                   w_out_b=w_out_b, post_ln_g=post_ln_g, post_ln_b=post_ln_b)
    moments_m = dict(w_in_a=m_w_in_a, sgu_ln_g=m_sgu_ln_g, sgu_ln_b=m_sgu_ln_b, w_spatial=m_w_spatial,
                     b_spatial=m_b_spatial, w_out_a=m_w_out_a, w_kv=m_w_kv, w_in_b=m_w_in_b,
                     attn_sinks=m_attn_sinks, rel_bias=m_rel_bias, w_out_b=m_w_out_b, post_ln_g=m_post_ln_g,
                     post_ln_b=m_post_ln_b)
    moments_v = dict(w_in_a=v_w_in_a, sgu_ln_g=v_sgu_ln_g, sgu_ln_b=v_sgu_ln_b, w_spatial=v_w_spatial,
                     b_spatial=v_b_spatial, w_out_a=v_w_out_a, w_kv=v_w_kv, w_in_b=v_w_in_b,
                     attn_sinks=v_attn_sinks, rel_bias=v_rel_bias, w_out_b=v_w_out_b, post_ln_g=v_post_ln_g,
                     post_ln_b=v_post_ln_b)
    order = ("w_in_a", "sgu_ln_g", "sgu_ln_b", "w_spatial", "b_spatial", "w_out_a", "w_kv", "w_in_b", "attn_sinks",
             "rel_bias", "w_out_b", "post_ln_g", "post_ln_b")

    shard_index = 2 * lax.axis_index("x") + lax.axis_index("y")
    ln_shard = jnp.concatenate([sgu_ln_g, sgu_ln_b], axis=0)
    shards = [w_in_a[0], w_out_a[0], w_kv, w_in_b[0], w_out_b[0]]
    col_sharded = [True, False, False, True, False]
    full_in_a, *later, ln_full, p_own = _gather_weights(shards, col_sharded, [True, False, False, False, False],
                                                        ln_shard, tokens=x[0])
    ln_full = jnp.transpose(ln_full, (1, 0, 2)).reshape(2, A_WIDTH)
    later_shapes = [s.shape for s in shards[1:]]
    fetch_group, fetch_token = _fetch_start(later, later_shapes, col_sharded[1:])

    def later_weights(y):
        return _fetch_wait(fetch_group, later_shapes, col_sharded[1:], y)

    groups, grads, deltas, new_m, new_v, scalars = {}, {}, {}, {}, {}, {}
    early = ("w_out_b", "w_in_b", "w_kv", "w_out_a")

    def two_dim(a):
        return a.reshape(-1, a.shape[-1])

    def send(tag, arrays, sliced):
        groups[tag], token = _exchange_start(tag, arrays, sliced)
        if tag != "a_in":
            return token
        srcs, lands = _exchange_wait("early", [groups["b"], groups["a_out"]], token)
        *reduced, packed_sum = _sum_and_swap("early", srcs[:4], lands[:4], srcs[4], lands[4])
        updates = [(two_dim(weights[n]), g, two_dim(moments_m[n]), two_dim(moments_v[n]))
                   for n, g in zip(early, reduced)]
        small_names = ("sgu_ln_g", "sgu_ln_b", "w_spatial", "b_spatial", "attn_sinks", "rel_bias", "post_ln_g",
                       "post_ln_b")
        def as_packed(arrays):
            return {n: (arrays[n].T if n == "rel_bias" else arrays[n]) for n in small_names}

        small_updates, scalars["loss"] = _adamw_small(packed_sum, shard_index, small_names, as_packed(weights),
                                                      as_packed(moments_m), as_packed(moments_v))
        for name, results in zip(small_names, small_updates):
            grads[name], deltas[name], new_m[name], new_v[name] = [
                a.T if name == "rel_bias" else a for a in results]
        return new_m["b_spatial"].reshape(A_GROUPS, CHUNK), updates

    grad_x, _, _, updated = _local_step(
        x[0], loss_target[0], full_in_a, later_weights, fetch_token, ln_full[0:1], ln_full[1:2], w_spatial[0],
        b_spatial[0], attn_sinks, rel_bias, post_ln_g, post_ln_b, send=send, own_product=(shard_index, p_own))
    for k, name in enumerate(early):
        grads[name], deltas[name], new_m[name], new_v[name] = [
            a.reshape(weights[name].shape) for a in updated[4 * k:4 * k + 4]]

    srcs, lands = _exchange_wait("late", [groups["a_in"]], grad_x)
    (g_in_a,) = _sum_and_swap("late", srcs, lands)
    grads["w_in_a"], deltas["w_in_a"], new_m["w_in_a"], new_v["w_in_a"] = _adamw(
        "w_in_a", w_in_a, g_in_a.reshape(w_in_a.shape), m_w_in_a, v_w_in_a)
    return (scalars["loss"], grad_x[None], *[grads[n] for n in order], *[deltas[n] for n in order],
            *[new_m[n] for n in order], *[new_v[n] for n in order])
```

```python
import functools
import math

import jax
import jax.numpy as jnp
from jax import lax
from jax.experimental import pallas as pl
from jax.experimental.pallas import tpu as pltpu

F32 = jnp.float32
BF16 = jnp.bfloat16

D_MODEL = 1024
A_WIDTH = 2048
A_GROUPS = 8
A_GROUP_DIM = 256
CHUNK = 128
N_HEADS = 16
N_KV = 2
HEAD_DIM = 64
PAIR = 2 * HEAD_DIM
B_WIDTH = 1024
REL_BUCKETS = 32
ALPHA = 4.0 ** 0.25
LN_EPS = 1e-5
NEG_INF = -1e30
SCALE = HEAD_DIM ** -0.5

ADAM_LR = 0.001
ADAM_B1 = 0.9
ADAM_B2 = 0.999
ADAM_EPS = 1e-08
ADAM_WD = 0.01
ADAM_STEP = 10

N_DEV = 8
N_CHIPS = 4
MESH = pl.DeviceIdType.MESH
VMEM_LIMIT = 56 * 1024 * 1024

TM_ATTN = 256
TM_BWD_ATTN = 512
TM_MM = 512
TM_WIN = 2048
_LANES = 128
_SUBLANES = 8


def _dot(a, b):
    return jnp.dot(a, b, preferred_element_type=F32)


def _dot_nt(a, b):
    return lax.dot_general(a, b, (((1,), (1,)), ((), ())), preferred_element_type=F32)


def _dot_tn(a, b):
    return lax.dot_general(a, b, (((0,), (0,)), ((), ())), preferred_element_type=F32)


def _ln_fwd(r):
    mu = jnp.mean(r, axis=-1, keepdims=True)
    rc = r - mu
    var = jnp.mean(rc * rc, axis=-1, keepdims=True)
    rstd = lax.rsqrt(var + LN_EPS)
    return rc * rstd, rstd


def _ln_bwd(dxh, xh, rstd):
    m1 = jnp.mean(dxh, axis=-1, keepdims=True)
    m2 = jnp.mean(dxh * xh, axis=-1, keepdims=True)
    return rstd * (dxh - m1 - xh * m2)


def _silu_parts(z):
    sg = jax.nn.sigmoid(z)
    return z * sg, sg * (1.0 + z * (1.0 - sg))


def _dup_halves(blk):
    sw = pltpu.roll(blk, HEAD_DIM, 1)
    lo = lax.broadcasted_iota(jnp.int32, blk.shape, 1) < HEAD_DIM
    return jnp.where(lo, blk, sw), jnp.where(lo, sw, blk)


def _fold_halves(blk):
    return blk + pltpu.roll(blk, HEAD_DIM, 1)


def _resident(shape):
    nd = len(shape)
    return pl.BlockSpec(shape, lambda *_: (0,) * nd, pipeline_mode=pl.Buffered(1))


def _const(shape):
    nd = len(shape)
    return pl.BlockSpec(shape, lambda *_: (0,) * nd)


def _rows(tm, cols):
    return pl.BlockSpec((tm, cols), lambda i: (i, 0))


def _params(sem=("arbitrary",)):
    return pltpu.CompilerParams(dimension_semantics=sem, vmem_limit_bytes=VMEM_LIMIT)


def _spatial_mix(ws_ref, bsp_ref, vn, s_scr, n_chunks):
    tri = (lax.broadcasted_iota(jnp.int32, (CHUNK, CHUNK), 0)
           >= lax.broadcasted_iota(jnp.int32, (CHUNK, CHUNK), 1))
    for g in range(A_GROUPS):
        wsg = jnp.where(tri, ws_ref[g], 0.0).astype(BF16)
        cols = slice(g * A_GROUP_DIM, (g + 1) * A_GROUP_DIM)
        for ci in range(n_chunks):
            rows = slice(ci * CHUNK, (ci + 1) * CHUNK)
            s_scr[rows, cols] = _dot(wsg, vn[rows, cols]) + bsp_ref[:, g:g + 1]


def _layer_a_fwd(x, w_in, p_own, lng, lnb, ws, bsp_t, after, own=None):
    t_len = x.shape[0]
    tm = TM_ATTN
    shard_cols = 3 * A_WIDTH // N_CHIPS

    def body(x_ref, win_ref, *rest):
        pown_ref = rest[0] if own is not None else None
        (lng_ref, lnb_ref, ws_ref, bsp_ref, after_ref,
         xt_ref, u_ref, vh_ref, z_ref, rv_ref, y_ref, s_scr) = rest[0 if own is None else 1:]
        x_t = x_ref[...]
        xb = x_t.astype(BF16)
        xt_ref[...] = x_t.T.astype(BF16)
        if own is None:
            u = _dot(xb, win_ref[:, 0:A_WIDTH])
            v = _dot(xb, win_ref[:, A_WIDTH:2 * A_WIDTH])
            z = _dot(xb, win_ref[:, 2 * A_WIDTH:3 * A_WIDTH])
        else:
            pieces = {s: pown_ref[...] if s == own else _dot(xb, win_ref[:, s * shard_cols:(s + 1) * shard_cols])
                      for s in (1, 2, 3, 0)}
            p = jnp.concatenate([pieces[s] for s in range(N_CHIPS)], axis=1)
            u, v, z = p[:, 0:A_WIDTH], p[:, A_WIDTH:2 * A_WIDTH], p[:, 2 * A_WIDTH:3 * A_WIDTH]
        vh, rv = _ln_fwd(v)
        vn = (vh * lng_ref[...] + lnb_ref[...]).astype(BF16)
        _spatial_mix(ws_ref, bsp_ref, vn, s_scr, tm // CHUNK)
        sz, _ = _silu_parts(z)
        y_ref[...] = (u * s_scr[...] * sz).astype(BF16)
        u_ref[...] = u.astype(BF16)
        vh_ref[...] = vh.astype(BF16)
        z_ref[...] = z.astype(BF16)
        rv_ref[...] = rv

    wide = jax.ShapeDtypeStruct((t_len, A_WIDTH), BF16)
    product = [] if own is None else [p_own]
    return pl.pallas_call(
        body, name="layer_a_fwd" if own is None else "layer_a_fwd_own%d" % own, grid=(t_len // tm,),
        in_specs=[_rows(tm, D_MODEL), _resident(w_in.shape)] + [_rows(tm, shard_cols)] * len(product)
        + [_const(lng.shape), _const(lnb.shape), _const(ws.shape), _const(bsp_t.shape), _const(after.shape)],
        out_specs=[pl.BlockSpec((D_MODEL, tm), lambda i: (0, i)), _rows(tm, A_WIDTH), _rows(tm, A_WIDTH),
                   _rows(tm, A_WIDTH), _rows(tm, 1), _rows(tm, A_WIDTH)],
        out_shape=[jax.ShapeDtypeStruct((D_MODEL, t_len), BF16), wide, wide, wide,
                   jax.ShapeDtypeStruct((t_len, 1), F32), wide],
        scratch_shapes=[pltpu.VMEM((tm, A_WIDTH), F32)],
        compiler_params=_params(),
    )(x, w_in, *product, lng, lnb, ws, bsp_t, after)


def _layer_b_proj(x, y, w_out_a, g1, b1, w_in, w_kv):
    t_len = x.shape[0]
    tm = 2 * TM_MM

    def body(x_ref, y_ref, wout_ref, g_ref, b_ref, win_ref, wkv_ref, xh_ref, r1_ref, q_ref, z_ref, kd_ref, vd_ref):
        halves = [slice(k * TM_MM, (k + 1) * TM_MM) for k in range(2)]
        projected = [_dot(y_ref[rows, :], wout_ref[...]) for rows in halves]
        for rows, out_a in zip(halves, projected):
            xh, r1 = _ln_fwd(ALPHA * x_ref[rows, :] + out_a)
            xh_ref[rows, :] = xh
            r1_ref[rows, :] = r1
            h1 = (xh * g_ref[...] + b_ref[...]).astype(BF16)
            q_ref[rows, :] = (_dot(h1, win_ref[:, 0:B_WIDTH]) * SCALE).astype(BF16)
            z_ref[rows, :] = _dot(h1, win_ref[:, B_WIDTH:2 * B_WIDTH]).astype(BF16)
            kv = _dot(h1, wkv_ref[...])
            k0, k1 = _dup_halves(kv[:, 0:PAIR])
            v0, v1 = _dup_halves(kv[:, PAIR:2 * PAIR])
            kd_ref[rows, 0:PAIR] = k0.astype(BF16)
            kd_ref[rows, PAIR:2 * PAIR] = k1.astype(BF16)
            vd_ref[rows, 0:PAIR] = v0.astype(BF16)
            vd_ref[rows, PAIR:2 * PAIR] = v1.astype(BF16)

    return pl.pallas_call(
        body, name="layer_b_proj", grid=(t_len // tm,),
        in_specs=[_rows(tm, D_MODEL), _rows(tm, A_WIDTH), _resident(w_out_a.shape), _const(g1.shape),
                  _const(b1.shape), _resident(w_in.shape), _resident(w_kv.shape)],
        out_specs=[_rows(tm, D_MODEL), _rows(tm, 1), _rows(tm, B_WIDTH), _rows(tm, B_WIDTH), _rows(tm, 2 * PAIR),
                   _rows(tm, 2 * PAIR)],
        out_shape=[jax.ShapeDtypeStruct((t_len, D_MODEL), F32), jax.ShapeDtypeStruct((t_len, 1), F32),
                   jax.ShapeDtypeStruct((t_len, B_WIDTH), BF16), jax.ShapeDtypeStruct((t_len, B_WIDTH), BF16),
                   jax.ShapeDtypeStruct((t_len, 2 * PAIR), BF16), jax.ShapeDtypeStruct((t_len, 2 * PAIR), BF16)],
        compiler_params=_params(),
    )(x, y, w_out_a, g1, b1, w_in, w_kv)


GROUP = N_HEADS // N_KV
GROUP_Q = GROUP * CHUNK


def _window_tables():
    j = jnp.arange(2 * CHUNK, dtype=jnp.int32)[:, None]
    t = jnp.arange(CHUNK, dtype=jnp.int32)[None, :]
    dist = t + CHUNK - j
    inside = (dist >= 0) & (dist < CHUNK)
    return jnp.stack([inside & (j >= CHUNK), inside]).astype(F32)


def _band(ref, chunk_index, kvh):
    prev0 = pl.multiple_of(jnp.maximum(chunk_index - 1, 0) * CHUNK, CHUNK)
    cur0 = pl.multiple_of(chunk_index * CHUNK, CHUNK)
    cols = slice(kvh * PAIR, (kvh + 1) * PAIR)
    return jnp.concatenate([ref[pl.ds(prev0, CHUNK), cols], ref[pl.ds(cur0, CHUNK), cols]], axis=0)


def _group_tables(bias_ref, win_ref, sink_ref, chunk_index, kvh):
    bias = jnp.concatenate([bias_ref[kvh * GROUP + j] for j in range(GROUP)], axis=1)
    win = win_ref[jnp.minimum(chunk_index, 1)]
    mask = jnp.concatenate([win] * GROUP, axis=1) > 0.5
    sink = jnp.concatenate([jnp.full((1, CHUNK), sink_ref[0, kvh * GROUP + j], F32) for j in range(GROUP)], axis=1)
    return bias, mask, sink


def _attn_probs(qs, kband, bias, mask, sink):
    logits = jnp.where(mask, _dot_nt(kband, qs) + bias, NEG_INF)
    m = jnp.maximum(jnp.max(logits, axis=0, keepdims=True), sink)
    e = jnp.exp(logits - m)
    es = jnp.exp(sink - m)
    inv = 1.0 / (jnp.sum(e, axis=0, keepdims=True) + es)
    return e * inv, es * inv


def _half_mask():
    return lax.broadcasted_iota(jnp.int32, (CHUNK, PAIR), 1) < HEAD_DIM


def _stack_heads(src_ref, rows, kvh, dst_scr, lo):
    for j in range(GROUP):
        h = kvh * GROUP + j
        blk = src_ref[rows, (h // 2) * PAIR:(h // 2 + 1) * PAIR].astype(F32)
        keep = lo if h % 2 == 0 else ~lo
        dst_scr[j * CHUNK:(j + 1) * CHUNK, :] = jnp.where(keep, blk, 0.0).astype(BF16)


def _probs_spec(tm):
    return pl.BlockSpec((tm // CHUNK, N_KV, 2 * CHUNK, GROUP_Q), lambda i: (i, 0, 0, 0))


def _sink_probs_spec(tiles=1):
    return pl.BlockSpec((tiles, 8, GROUP_Q), lambda i: (i, 0, 0))


def _unstack_pairs(stacked, pp, lo):
    return jnp.where(lo, stacked[(2 * pp) * CHUNK:(2 * pp + 1) * CHUNK], stacked[(2 * pp + 1) * CHUNK:(2 * pp + 2) * CHUNK])


def _layer_b_fwd(q, zb, kd, vd, bias, win, sinks, xh1, g1, b1, w_out, g2, b2, tgt):
    t_len = q.shape[0]
    tm = 2 * TM_ATTN

    def body(q_ref, z_ref, kd_ref, vd_ref, bias_ref, win_ref, sink_ref, xh_ref, g1_ref, b1_ref, wout_ref, g2_ref,
             b2_ref, tgt_ref, o_ref, p_ref, ps_ref, dr_ref, loss_ref, dg_ref, db_ref, o_scr, qs_scr):
        i = pl.program_id(0)

        @pl.when(i == 0)
        def _():
            loss_ref[...] = jnp.zeros_like(loss_ref)
            dg_ref[...] = jnp.zeros_like(dg_ref)
            db_ref[...] = jnp.zeros_like(db_ref)

        lo = _half_mask()
        ps_ref[...] = jnp.zeros_like(ps_ref)
        per_part = TM_ATTN // CHUNK
        for part in range(tm // TM_ATTN):
            part_rows = slice(part * TM_ATTN, (part + 1) * TM_ATTN)
            for cp in range(per_part):
                ci = part * per_part + cp
                cg = i * (tm // CHUNK) + ci
                rows = slice(ci * CHUNK, (ci + 1) * CHUNK)
                for kvh in range(N_KV):
                    kband = _band(kd_ref, cg, kvh)
                    vband = _band(vd_ref, cg, kvh)
                    bias_g, mask, sink = _group_tables(bias_ref, win_ref, sink_ref, cg, kvh)
                    _stack_heads(q_ref, rows, kvh, qs_scr, lo)
                    p, p_sink = _attn_probs(qs_scr[...], kband, bias_g, mask, sink)
                    p = p.astype(BF16)
                    p_ref[ci, kvh] = p
                    ps_ref[part, cp * N_KV + kvh:cp * N_KV + kvh + 1, :] = p_sink
                    o_stack = _dot_tn(p, vband)
                    for pp in range(GROUP // 2):
                        pair = kvh * (GROUP // 2) + pp
                        o_scr[rows, pair * PAIR:(pair + 1) * PAIR] = _unstack_pairs(o_stack, pp, lo)
            o = o_scr[part_rows, :]
            o_ref[part_rows, :] = o.astype(BF16)
            sz, _ = _silu_parts(z_ref[part_rows, :].astype(F32))
            y = (o * sz).astype(BF16)
            h1 = xh_ref[part_rows, :] * g1_ref[...] + b1_ref[...]
            r = ALPHA * h1 + _dot(y, wout_ref[...])
            xh2, rstd2 = _ln_fwd(r)
            diff = xh2 * g2_ref[...] + b2_ref[...] - tgt_ref[part_rows, :]
            loss_ref[...] += jnp.sum(diff * diff, axis=0, keepdims=True)
            dh2 = diff * (1.0 / D_MODEL)
            dg_ref[...] += jnp.sum(dh2 * xh2, axis=0, keepdims=True)
            db_ref[...] += jnp.sum(dh2, axis=0, keepdims=True)
            dr_ref[part_rows, :] = _ln_bwd(dh2 * g2_ref[...], xh2, rstd2)

    vec = jax.ShapeDtypeStruct((1, D_MODEL), F32)
    return pl.pallas_call(
        body, name="layer_b_fwd", grid=(t_len // tm,),
        in_specs=[_rows(tm, B_WIDTH), _rows(tm, B_WIDTH), _resident(kd.shape), _resident(vd.shape),
                  _resident(bias.shape), _resident(win.shape), pl.BlockSpec(memory_space=pltpu.SMEM),
                  _rows(tm, D_MODEL), _const(g1.shape), _const(b1.shape), _resident(w_out.shape), _const(g2.shape),
                  _const(b2.shape), _rows(tm, D_MODEL)],
        out_specs=[_rows(tm, B_WIDTH), _probs_spec(tm), _sink_probs_spec(tm // TM_ATTN), _rows(tm, D_MODEL)]
        + [_const((1, D_MODEL))] * 3,
        out_shape=[jax.ShapeDtypeStruct((t_len, B_WIDTH), BF16),
                   jax.ShapeDtypeStruct((t_len // CHUNK, N_KV, 2 * CHUNK, GROUP_Q), BF16),
                   jax.ShapeDtypeStruct((t_len // TM_ATTN, 8, GROUP_Q), F32),
                   jax.ShapeDtypeStruct((t_len, D_MODEL), F32), vec, vec, vec],
        scratch_shapes=[pltpu.VMEM((tm, B_WIDTH), F32), pltpu.VMEM((GROUP_Q, PAIR), BF16)],
        compiler_params=_params(),
    )(q, zb, kd, vd, bias, win, sinks, xh1, g1, b1, w_out, g2, b2, tgt)


def _layer_b_bwd_attn(dr2, zb, o, q, kd, vd, probs, sink_probs, w_out):
    t_len = q.shape[0]
    tm = TM_BWD_ATTN
    n_steps = t_len // tm
    n_chunks = tm // CHUNK
    per_part = TM_ATTN // CHUNK

    def body(dr_ref, z_ref, o_ref, q_ref, kd_ref, vd_ref, p_ref, ps_ref, wout_ref,
             dq_ref, dz_ref, dkd_ref, dvd_ref, ck_ref, cv_ref, gw_ref, dsink_ref, dbias_ref,
             do_scr, qs_scr, dos_scr, gw_acc):
        i = pl.program_id(0)

        @pl.when(i == 0)
        def _():
            gw_acc[...] = jnp.zeros_like(gw_acc)
            dsink_ref[...] = jnp.zeros_like(dsink_ref)
            dbias_ref[...] = jnp.zeros_like(dbias_ref)

        drb = dr_ref[...].astype(BF16)
        per_kvh = 2
        n_blocks = N_KV * per_kvh
        block_cols = B_WIDTH // n_blocks

        def through_gate(b):
            cols = slice(b * block_cols, (b + 1) * block_cols)
            dy = _dot_nt(drb, wout_ref[cols, :])
            sz, dsz = _silu_parts(z_ref[:, cols].astype(F32))
            o_t = o_ref[:, cols].astype(F32)
            dz_ref[:, cols] = (dy * o_t * dsz).astype(BF16)
            do_scr[:, cols] = (dy * sz).astype(BF16)
            return (o_t * sz).astype(BF16)

        def weight_gradient(b, gated):
            cols = slice(b * block_cols, (b + 1) * block_cols)
            gw_acc[cols, :] += _dot_tn(gated, drb)

        gated = {b: through_gate(b) for b in range(per_kvh)}

        lo = _half_mask()
        for kvh in range(N_KV):
            kcols = slice(kvh * PAIR, (kvh + 1) * PAIR)
            dk_bands, dv_bands = [], []
            for ci in range(n_chunks):
                unit = kvh * n_chunks + ci
                if ci < per_kvh and kvh + 1 < N_KV:
                    gated[(kvh + 1) * per_kvh + ci] = through_gate((kvh + 1) * per_kvh + ci)
                if unit in gated:
                    weight_gradient(unit, gated.pop(unit))
                cg = i * n_chunks + ci
                rows = slice(ci * CHUNK, (ci + 1) * CHUNK)
                kband = _band(kd_ref, cg, kvh)
                vband = _band(vd_ref, cg, kvh)
                _stack_heads(q_ref, rows, kvh, qs_scr, lo)
                _stack_heads(do_scr, rows, kvh, dos_scr, lo)
                qs = qs_scr[...]
                dos = dos_scr[...]
                pb = p_ref[ci, kvh]
                p = pb.astype(F32)
                sink_row = (ci % per_part) * N_KV + kvh
                p_sink = ps_ref[ci // per_part, sink_row:sink_row + 1, :]
                dp = _dot_nt(vband, dos)
                delta = jnp.sum(p * dp, axis=0, keepdims=True)
                dlog = p * (dp - delta)
                for j in range(GROUP):
                    dbias_ref[kvh * GROUP + j] += dlog[:, j * CHUNK:(j + 1) * CHUNK]
                dsink_ref[kvh:kvh + 1, :] += -(p_sink * delta)
                ds = dlog.astype(BF16)
                dq_stack = _dot_tn(ds, kband) * SCALE
                for pp in range(GROUP // 2):
                    pair = kvh * (GROUP // 2) + pp
                    dq_ref[rows, pair * PAIR:(pair + 1) * PAIR] = _unstack_pairs(dq_stack, pp, lo).astype(BF16)
                dk_bands.append(_dot(ds, qs))
                dv_bands.append(_dot(pb, dos))
            for bands, out_ref, carry_ref in ((dk_bands, dkd_ref, ck_ref), (dv_bands, dvd_ref, cv_ref)):
                carry_ref[0, :, kcols] = bands[0][0:CHUNK]
                for ci in range(n_chunks):
                    own = bands[ci][CHUNK:2 * CHUNK]
                    if ci + 1 < n_chunks:
                        own = own + bands[ci + 1][0:CHUNK]
                    out_ref[ci * CHUNK:(ci + 1) * CHUNK, kcols] = own

        @pl.when(i == n_steps - 1)
        def _():
            gw_ref[...] = gw_acc[...].astype(BF16)

    carry_spec = pl.BlockSpec((1, CHUNK, 2 * PAIR), lambda i: (i, 0, 0))
    carry_shape = jax.ShapeDtypeStruct((n_steps, CHUNK, 2 * PAIR), F32)
    bias_shape = (N_HEADS, 2 * CHUNK, CHUNK)
    return pl.pallas_call(
        body, name="layer_b_bwd_attn", grid=(n_steps,),
        in_specs=[_rows(tm, D_MODEL), _rows(tm, B_WIDTH), _rows(tm, B_WIDTH), _rows(tm, B_WIDTH),
                  _resident(kd.shape), _resident(vd.shape), _probs_spec(tm), _sink_probs_spec(tm // TM_ATTN),
                  _resident(w_out.shape)],
        out_specs=[_rows(tm, B_WIDTH), _rows(tm, B_WIDTH), _rows(tm, 2 * PAIR), _rows(tm, 2 * PAIR),
                   carry_spec, carry_spec, _const(w_out.shape), _const((N_KV, GROUP_Q)), _const(bias_shape)],
        out_shape=[jax.ShapeDtypeStruct((t_len, B_WIDTH), BF16), jax.ShapeDtypeStruct((t_len, B_WIDTH), BF16),
                   jax.ShapeDtypeStruct((t_len, 2 * PAIR), F32), jax.ShapeDtypeStruct((t_len, 2 * PAIR), F32),
                   carry_shape, carry_shape, jax.ShapeDtypeStruct(w_out.shape, BF16),
                   jax.ShapeDtypeStruct((N_KV, GROUP_Q), F32), jax.ShapeDtypeStruct(bias_shape, F32)],
        scratch_shapes=[pltpu.VMEM((tm, B_WIDTH), BF16), pltpu.VMEM((GROUP_Q, PAIR), BF16),
                        pltpu.VMEM((GROUP_Q, PAIR), BF16), pltpu.VMEM(w_out.shape, F32)],
        compiler_params=_params(),
    )(dr2, zb, o, q, kd, vd, probs, sink_probs, w_out)


def _layer_b_bwd_proj(xh1, rstd1, g1, b1, dr2, dq, dzb, dkd, dvd, carry_k, carry_v, w_in, w_kv):
    t_len = xh1.shape[0]
    tm = TM_MM
    n_steps = t_len // tm
    per_tile = tm // TM_BWD_ATTN
    n_carry = carry_k.shape[0]

    def body(xh_ref, rstd_ref, g_ref, b_ref, dr2_ref, dq_ref, dz_ref, dkd_ref, dvd_ref, *rest):
        carry_refs = rest[:2 * per_tile]
        win_ref, wkv_ref, dr1_ref, dg_ref, db_ref, gwin_ref, gwkv_ref, acc_in, acc_kv = rest[2 * per_tile:]
        i = pl.program_id(0)

        @pl.when(i == 0)
        def _():
            acc_in[...] = jnp.zeros_like(acc_in)
            acc_kv[...] = jnp.zeros_like(acc_kv)
            dg_ref[...] = jnp.zeros_like(dg_ref)
            db_ref[...] = jnp.zeros_like(db_ref)

        lo = lax.broadcasted_iota(jnp.int32, (tm, PAIR), 1) < HEAD_DIM

        def heads_gradient(tile_ref, refs):
            parts = []
            for a in range(per_tile):
                parts.append(tile_ref[a * TM_BWD_ATTN:(a + 1) * TM_BWD_ATTN - CHUNK, :])
                carry = refs[a][0]
                if a == per_tile - 1:
                    carry = jnp.where(i < n_steps - 1, carry, 0.0)
                parts.append(tile_ref[(a + 1) * TM_BWD_ATTN - CHUNK:(a + 1) * TM_BWD_ATTN, :] + carry)
            dup = jnp.concatenate(parts, axis=0)
            return jnp.where(lo, _fold_halves(dup[:, 0:PAIR]), _fold_halves(dup[:, PAIR:2 * PAIR]))

        xh = xh_ref[...]
        h1 = (xh * g_ref[...] + b_ref[...]).astype(BF16)
        dq_t = dq_ref[...]
        dz_t = dz_ref[...]
        dkv = jnp.concatenate([heads_gradient(dkd_ref, carry_refs[:per_tile]),
                               heads_gradient(dvd_ref, carry_refs[per_tile:])], axis=1).astype(BF16)
        dh1 = ALPHA * dr2_ref[...]
        dh1 += _dot_nt(dq_t, win_ref[:, 0:B_WIDTH])
        dh1 += _dot_nt(dz_t, win_ref[:, B_WIDTH:2 * B_WIDTH])
        dh1 += _dot_nt(dkv, wkv_ref[...])
        acc_in[:, 0:B_WIDTH] += _dot_tn(h1, dq_t)
        acc_in[:, B_WIDTH:2 * B_WIDTH] += _dot_tn(h1, dz_t)
        acc_kv[...] += _dot_tn(h1, dkv)
        dg_ref[...] += jnp.sum(dh1 * xh, axis=0, keepdims=True)
        db_ref[...] += jnp.sum(dh1, axis=0, keepdims=True)
        dr1_ref[...] = _ln_bwd(dh1 * g_ref[...], xh, rstd_ref[...])

        @pl.when(i == n_steps - 1)
        def _():
            half_rows = D_MODEL // 2
            shard_cols = 2 * B_WIDTH // N_CHIPS
            for s in range(N_CHIPS):
                for c in range(2):
                    gwin_ref[2 * s + c] = acc_in[c * half_rows:(c + 1) * half_rows,
                                                 s * shard_cols:(s + 1) * shard_cols].astype(BF16)
            gwkv_ref[...] = acc_kv[...].astype(BF16)

    vec = jax.ShapeDtypeStruct((1, D_MODEL), F32)
    gwin_shape = (N_DEV, D_MODEL // 2, 2 * B_WIDTH // N_CHIPS)

    def carry_spec(a):
        return pl.BlockSpec((1, CHUNK, 2 * PAIR), lambda i: (jnp.minimum(per_tile * i + a + 1, n_carry - 1), 0, 0))

    carry_specs = [carry_spec(a) for a in range(per_tile)]
    return pl.pallas_call(
        body, name="layer_b_bwd_proj", grid=(n_steps,),
        in_specs=[_rows(tm, D_MODEL), _rows(tm, 1), _const(g1.shape), _const(b1.shape), _rows(tm, D_MODEL),
                  _rows(tm, B_WIDTH), _rows(tm, B_WIDTH), _rows(tm, 2 * PAIR), _rows(tm, 2 * PAIR)]
        + carry_specs + carry_specs + [_resident(w_in.shape), _resident(w_kv.shape)],
        out_specs=[_rows(tm, D_MODEL), _const((1, D_MODEL)), _const((1, D_MODEL)), _const(gwin_shape),
                   _const(w_kv.shape)],
        out_shape=[jax.ShapeDtypeStruct((t_len, D_MODEL), F32), vec, vec,
                   jax.ShapeDtypeStruct(gwin_shape, BF16), jax.ShapeDtypeStruct(w_kv.shape, BF16)],
        scratch_shapes=[pltpu.VMEM(w_in.shape, F32), pltpu.VMEM(w_kv.shape, F32)],
        compiler_params=_params(),
    )(xh1, rstd1, g1, b1, dr2, dq, dzb, dkd, dvd, *([carry_k] * per_tile), *([carry_v] * per_tile), w_in, w_kv)


def _layer_a_bwd_mix(dr1, u, vh, z, y, rv, w_out, lng, lnb, ws, bsp_t, after):
    t_len = u.shape[0]
    tm = TM_ATTN
    n_steps = t_len // tm

    def body(dr_ref, u_ref, vh_ref, z_ref, y_ref, rv_ref, wout_ref, lng_ref, lnb_ref, ws_ref, bsp_ref, after_ref,
             dp_ref, gw_ref, dws_ref, dbsp_ref, dgs_ref, dbs_ref, dvn_scr, gw_acc):
        i = pl.program_id(0)

        @pl.when(i == 0)
        def _():
            gw_acc[...] = jnp.zeros_like(gw_acc)
            dws_ref[...] = jnp.zeros_like(dws_ref)
            dbsp_ref[...] = jnp.zeros_like(dbsp_ref)
            dgs_ref[...] = jnp.zeros_like(dgs_ref)
            dbs_ref[...] = jnp.zeros_like(dbs_ref)

        drb = dr_ref[...].astype(BF16)

        def group_cols(g):
            return slice(g * A_GROUP_DIM, (g + 1) * A_GROUP_DIM)

        tri = (lax.broadcasted_iota(jnp.int32, (CHUNK, CHUNK), 0)
               >= lax.broadcasted_iota(jnp.int32, (CHUNK, CHUNK), 1))
        lane = lax.broadcasted_iota(jnp.int32, (CHUNK, CHUNK), 1)
        ones = jnp.ones((CHUNK, A_GROUP_DIM), BF16)
        dbsp = jnp.zeros((CHUNK, CHUNK), F32)
        dy_next = _dot_nt(drb, wout_ref[group_cols(0), :])
        for g in range(A_GROUPS):
            wsg = jnp.where(tri, ws_ref[g], 0.0).astype(BF16)
            cols = group_cols(g)
            cols_z = slice(2 * A_WIDTH + g * A_GROUP_DIM, 2 * A_WIDTH + (g + 1) * A_GROUP_DIM)
            dy_g = dy_next
            if g + 1 < A_GROUPS:
                dy_next = _dot_nt(drb, wout_ref[group_cols(g + 1), :])
            gw_acc[cols, :] += _dot_tn(y_ref[:, cols], drb)
            both = jnp.zeros((CHUNK, 2 * CHUNK), F32)
            for ci in range(tm // CHUNK):
                rows = slice(ci * CHUNK, (ci + 1) * CHUNK)
                vn = (vh_ref[rows, cols].astype(F32) * lng_ref[:, cols] + lnb_ref[:, cols]).astype(BF16)
                s = _dot(wsg, vn) + bsp_ref[:, g:g + 1]
                sz, dsz = _silu_parts(z_ref[rows, cols].astype(F32))
                dy = dy_g[rows]
                t = dy * u_ref[rows, cols].astype(F32)
                dp_ref[rows, cols] = (dy * (s * sz)).astype(BF16)
                dp_ref[rows, cols_z] = (t * s * dsz).astype(BF16)
                ds_b = (t * sz).astype(BF16)
                both += _dot_nt(ds_b, jnp.concatenate([vn, ones], axis=0))
                dvn_scr[rows, cols] = _dot_tn(wsg, ds_b)
            dws_ref[g] += jnp.where(tri, both[:, 0:CHUNK], 0.0)
            dbsp = jnp.where(lane == g, both[:, CHUNK:2 * CHUNK], dbsp)
        dbsp_ref[...] += dbsp
        dvn = dvn_scr[...]
        vh_t = vh_ref[...].astype(F32)
        dgs_ref[...] += jnp.sum(dvn * vh_t, axis=0, keepdims=True)
        dbs_ref[...] += jnp.sum(dvn, axis=0, keepdims=True)
        dp_ref[:, A_WIDTH:2 * A_WIDTH] = _ln_bwd(dvn * lng_ref[...], vh_t, rv_ref[...]).astype(BF16)

        @pl.when(i == n_steps - 1)
        def _():
            gw_ref[...] = gw_acc[...].astype(BF16)

    wide = jax.ShapeDtypeStruct((1, A_WIDTH), F32)
    return pl.pallas_call(
        body, name="layer_a_bwd_mix", grid=(n_steps,),
        in_specs=[_rows(tm, D_MODEL), _rows(tm, A_WIDTH), _rows(tm, A_WIDTH), _rows(tm, A_WIDTH), _rows(tm, A_WIDTH),
                  _rows(tm, 1), _resident(w_out.shape), _const(lng.shape), _const(lnb.shape), _const(ws.shape),
                  _const(bsp_t.shape), _const(after.shape)],
        out_specs=[_rows(tm, 3 * A_WIDTH), _const(w_out.shape), _const(ws.shape), _const((CHUNK, CHUNK)),
                   _const((1, A_WIDTH)), _const((1, A_WIDTH))],
        out_shape=[jax.ShapeDtypeStruct((t_len, 3 * A_WIDTH), BF16), jax.ShapeDtypeStruct(w_out.shape, BF16),
                   jax.ShapeDtypeStruct(ws.shape, F32), jax.ShapeDtypeStruct((CHUNK, CHUNK), F32),
                   wide, wide],
        scratch_shapes=[pltpu.VMEM((tm, A_WIDTH), F32), pltpu.VMEM(w_out.shape, F32)],
        compiler_params=_params(),
    )(dr1, u, vh, z, y, rv, w_out, lng, lnb, ws, bsp_t, after)


def _layer_a_bwd_dx(dr1, dp, w_in, after, updates=()):
    t_len = dr1.shape[0]
    tm = TM_MM
    n_steps = t_len // tm
    n_upd = len(updates)

    def body(dr_ref, dp_ref, win_ref, after_ref, *refs):
        upd_in, dx_ref, upd_out = refs[:4 * n_upd], refs[4 * n_upd], refs[4 * n_upd + 1:]
        dx_ref[...] = ALPHA * dr_ref[...] + _dot_nt(dp_ref[...], win_ref[...])
        for k in range(n_upd):
            w_ref, g_ref, m_ref, v_ref = upd_in[4 * k:4 * k + 4]
            g_out, d_ref, nm_ref, nv_ref = upd_out[4 * k:4 * k + 4]
            g_out[...] = g_ref[...]
            _adamw_update(w_ref, g_ref, m_ref, v_ref, d_ref, nm_ref, nv_ref)

    upd_specs, upd_shapes, upd_args = [], [], []
    for w, g, m, v in updates:
        rows, cols = w.shape
        upd_specs.append(pl.BlockSpec((rows // n_steps, cols), lambda i: (i, 0)))
        upd_shapes.append(jax.ShapeDtypeStruct((rows, cols), F32))
        upd_args += [w, g, m, v]
    return pl.pallas_call(
        body, name="layer_a_bwd_dx", grid=(n_steps,),
        in_specs=[_rows(tm, D_MODEL), _rows(tm, 3 * A_WIDTH), _resident(w_in.shape), _const(after.shape)]
        + [s for s in upd_specs for _ in range(4)],
        out_specs=[_rows(tm, D_MODEL)] + [s for s in upd_specs for _ in range(4)],
        out_shape=[jax.ShapeDtypeStruct((t_len, D_MODEL), F32)] + [s for s in upd_shapes for _ in range(4)],
        compiler_params=_params(),
    )(dr1, dp, w_in, after, *upd_args)


def _layer_a_bwd_win(xt, dp, after):
    t_len = xt.shape[1]
    tm = TM_WIN
    n_steps = t_len // tm
    shard_cols = 3 * A_WIDTH // N_CHIPS
    half_rows = D_MODEL // 2

    def body(xt_ref, dp_ref, after_ref, gw_ref, acc):
        i = pl.program_id(1)

        @pl.when(i == 0)
        def _():
            acc[...] = jnp.zeros_like(acc)

        acc[...] += _dot(xt_ref[...], dp_ref[...])

        @pl.when(i == n_steps - 1)
        def _():
            for c in range(2):
                gw_ref[0, c] = acc[c * half_rows:(c + 1) * half_rows, :].astype(BF16)

    return pl.pallas_call(
        body, name="layer_a_bwd_win", grid=(N_CHIPS, n_steps),
        in_specs=[pl.BlockSpec((D_MODEL, tm), lambda j, i: (0, i)),
                  pl.BlockSpec((tm, shard_cols), lambda j, i: (i, j)), _const(after.shape)],
        out_specs=pl.BlockSpec((1, 2, half_rows, shard_cols), lambda j, i: (j, 0, 0, 0)),
        out_shape=jax.ShapeDtypeStruct((N_CHIPS, 2, half_rows, shard_cols), BF16),
        scratch_shapes=[pltpu.VMEM((D_MODEL, shard_cols), F32)],
        compiler_params=_params(("arbitrary", "arbitrary")),
    )(xt, dp, after)


def _bucket_onehot():
    dist = jnp.arange(CHUNK, dtype=jnp.int32)[None, :]
    max_exact = REL_BUCKETS // 2
    df = jnp.maximum(dist, 1).astype(F32)
    large = max_exact + (jnp.log(df / max_exact) / math.log(CHUNK / max_exact)
                         * (REL_BUCKETS - max_exact)).astype(jnp.int32)
    bucket = jnp.where(dist < max_exact, dist, jnp.minimum(large, REL_BUCKETS - 1))
    onehot = bucket == jnp.arange(REL_BUCKETS, dtype=jnp.int32)[:, None]
    return onehot.astype(F32)


def _bias_expand(rel_t, onehot):
    def body(rel_ref, oh_ref, out_ref):
        by_distance = jnp.dot(rel_ref[...], oh_ref[...], preferred_element_type=F32,
                              precision=lax.Precision.HIGHEST)
        for h in range(N_HEADS):
            rows = jnp.broadcast_to(by_distance[h:h + 1, :], (2 * CHUNK, CHUNK))
            out_ref[h] = pltpu.roll(rows, 0, 1, stride=1, stride_axis=0)

    return pl.pallas_call(
        body, name="bias_expand",
        out_shape=jax.ShapeDtypeStruct((N_HEADS, 2 * CHUNK, CHUNK), F32),
    )(rel_t, onehot)


def _bias_reduce(oh_ref, db_ref):
    sublane = lax.broadcasted_iota(jnp.int32, (_SUBLANES, CHUNK), 0)
    rows = []
    for h in range(N_HEADS):
        part = db_ref[h, 0:_SUBLANES, :]
        for a in range(1, 2 * CHUNK // _SUBLANES):
            tile = db_ref[h, a * _SUBLANES:(a + 1) * _SUBLANES, :]
            back = (-a * _SUBLANES) % CHUNK
            part += pltpu.roll(tile, back, 1) if back else tile
        total = jnp.where(sublane == 0, part, 0.0)
        for s in range(1, _SUBLANES):
            total += jnp.where(sublane == s, pltpu.roll(part, CHUNK - s, 1), 0.0)
        rows.append(jnp.sum(total, axis=0, keepdims=True))
    by_distance = jnp.concatenate(rows, axis=0)
    return lax.dot_general(by_distance, oh_ref[...], (((1,), (1,)), ((), ())),
                           preferred_element_type=F32, precision=lax.Precision.HIGHEST)


_SMALL_SHAPES = dict(w_spatial=(A_GROUPS, CHUNK, CHUNK), b_spatial=(A_GROUPS, CHUNK), attn_sinks=(1, N_HEADS),
                     rel_bias=(N_HEADS, REL_BUCKETS), post_ln_g=(2, D_MODEL), post_ln_b=(2, D_MODEL),
                     sgu_ln_g=(1, A_WIDTH), sgu_ln_b=(1, A_WIDTH), loss=(1, 1))
_SMALL_ORDER = tuple(_SMALL_SHAPES)


def _small_rows(name):
    shape = _SMALL_SHAPES[name]
    rows = math.prod(shape[:-1]) if shape[-1] < _LANES else math.prod(shape) // _LANES
    return -(-rows // _SUBLANES) * _SUBLANES


def _small_offset(name):
    return sum(_small_rows(n) for n in _SMALL_ORDER[:_SMALL_ORDER.index(name)])


def _pack_small(dws, dbsp, dsink, dbias, onehot, post_g, post_b, dgs, dbs, loss_vec):
    def body(dws_ref, dbsp_ref, dsink_ref, db_ref, oh_ref, g1_ref, g2_ref, b1_ref, b2_ref, dgs_ref, dbs_ref,
             loss_ref, out_ref):
        out_ref[...] = jnp.zeros_like(out_ref)

        def put_flat(name, refs):
            row = _small_offset(name)
            for ref in refs:
                for k in range(ref.shape[1] // _LANES):
                    out_ref[row:row + 1, :] = ref[:, k * _LANES:(k + 1) * _LANES]
                    row += 1

        row = _small_offset("w_spatial")
        for g in range(A_GROUPS):
            out_ref[row + g * CHUNK:row + (g + 1) * CHUNK, :] = dws_ref[g]
        row = _small_offset("b_spatial")
        out_ref[row:row + A_GROUPS, :] = dbsp_ref[...].T[0:A_GROUPS, :]
        lane = lax.broadcasted_iota(jnp.int32, (1, _LANES), 1)
        sinks = jnp.zeros((1, _LANES), F32)
        for h in range(N_HEADS):
            per_query = dsink_ref[h // GROUP:h // GROUP + 1, (h % GROUP) * CHUNK:(h % GROUP + 1) * CHUNK]
            sinks = jnp.where(lane == h, jnp.sum(per_query, axis=1, keepdims=True), sinks)
        row = _small_offset("attn_sinks")
        out_ref[row:row + 1, :] = sinks
        row = _small_offset("rel_bias")
        out_ref[row:row + N_HEADS, 0:REL_BUCKETS] = _bias_reduce(oh_ref, db_ref)
        put_flat("post_ln_g", [g1_ref, g2_ref])
        put_flat("post_ln_b", [b1_ref, b2_ref])
        put_flat("sgu_ln_g", [dgs_ref])
        put_flat("sgu_ln_b", [dbs_ref])
        row = _small_offset("loss")
        out_ref[row:row + 1, 0:1] = (0.5 / D_MODEL) * jnp.sum(loss_ref[...], axis=1, keepdims=True)

    total_rows = sum(_small_rows(n) for n in _SMALL_ORDER)
    return pl.pallas_call(
        body, name="pack_small",
        out_shape=jax.ShapeDtypeStruct((total_rows, _LANES), F32),
    )(dws, dbsp, dsink, dbias, onehot, *post_g, *post_b, dgs, dbs, loss_vec)


def _place():
    return lax.axis_index("x"), lax.axis_index("y"), lax.axis_index("c")


RELAY_PIECES = 4


def _shard_window(full_ref, shard_shape, col_sharded, s, half, piece=None):
    rows, cols = shard_shape
    if half is None:
        start, size = 0, rows
    elif piece is None:
        start, size = half * (rows // 2), rows // 2
    else:
        size = rows // 2 // RELAY_PIECES
        start = (half * RELAY_PIECES + piece) * size
    if col_sharded:
        return full_ref.at[pl.ds(start, size), pl.ds(s * cols, cols)]
    return full_ref.at[pl.ds(s * rows + start, size), :]


def _other_chips(x, y):
    return [(1 - x, y), (x, 1 - y), (1 - x, 1 - y)]


def _gather_weights(shards, col_sharded, fetch, ln_shard, tokens=None):
    n_w = len(shards)
    fetched = [w for w in range(n_w) if fetch[w]]
    full_shapes = []
    for w, cs in zip(shards, col_sharded):
        r, c = w.shape
        full_shapes.append((r, c * N_CHIPS) if cs else (r * N_CHIPS, c))

    n_tok = 0 if tokens is None else 1
    tok_tile = 2 * TM_MM
    n_tiles = 0 if tokens is None else tokens.shape[0] // tok_tile

    def body(*refs):
        refs = list(refs)
        in_refs = [refs.pop(0) for _ in range(n_w)]
        ln_ref = refs.pop(0)
        tok_ref = refs.pop(0) if n_tok else None
        full_refs = [refs.pop(0) for _ in range(n_w)]
        ln_full = refs.pop(0)
        prod_ref = refs.pop(0) if n_tok else None
        raw = [refs.pop(0) for _ in range(n_w)]
        stage = [refs.pop(0) for _ in range(n_w)]
        send_sems, recv_sems, load_sems, local_sems, ln_send, ln_recv = refs[:6]
        tok_buf, prod_buf, tok_sems, prod_sems = refs[6:] if n_tok else (None,) * 4
        x, y, c = _place()
        s_me = 2 * x + y
        chips = _other_chips(x, y)
        pieces = range(RELAY_PIECES)

        def shard_window(w, s, half, piece=None):
            return _shard_window(full_refs[w], shards[w].shape, col_sharded[w], s, half, piece)

        def piece_rows(w, half, piece):
            rows = shards[w].shape[0] // 2 // RELAY_PIECES
            return pl.ds(pl.multiple_of((half * RELAY_PIECES + piece) * rows, rows), rows)

        def ici_copy(w, k, sender_shard, piece):
            idx = (w * 3 + k) * RELAY_PIECES + piece
            return pltpu.make_async_remote_copy(
                src_ref=stage[w].at[piece_rows(w, c, piece), :], dst_ref=shard_window(w, sender_shard, c, piece),
                send_sem=send_sems.at[idx], recv_sem=recv_sems.at[idx],
                device_id=(*chips[k], c), device_id_type=MESH)

        def d2d_copy(w, k, half, piece):
            s_k = 2 * chips[k][0] + chips[k][1]
            win = shard_window(w, s_k, half, piece)
            idx = (3 * n_w + w * 3 + k) * RELAY_PIECES + piece
            return pltpu.make_async_remote_copy(
                src_ref=win, dst_ref=win, send_sem=send_sems.at[idx], recv_sem=recv_sems.at[idx],
                device_id=(x, y, 1 - c), device_id_type=MESH)

        def ln_copy(k, slot):
            return pltpu.make_async_remote_copy(
                src_ref=ln_ref, dst_ref=ln_full.at[slot], send_sem=ln_send.at[k], recv_sem=ln_recv.at[k],
                device_id=(*chips[k], c), device_id_type=MESH)

        loads = []

        def load(w, rows):
            window = (rows, slice(None)) if rows is not None else (slice(None), slice(None))
            cp = pltpu.make_async_copy(in_refs[w].at[window], raw[w].at[window], load_sems.at[len(loads)])
            cp.start()
            loads.append((cp, w, window))

        for half in (c, 1 - c):
            for w in fetched:
                for q in pieces:
                    load(w, piece_rows(w, half, q))
        for w in range(n_w):
            if not fetch[w]:
                load(w, None)

        def to_bf16(k):
            cp, w, window = loads[k]
            cp.wait()
            stage[w][window] = raw[w][window].astype(BF16)

        ln_full[s_me] = ln_ref[...]
        def shard_of(k):
            return 2 * chips[k][0] + chips[k][1]

        relay_from = jnp.where(c == 0, shard_of(0), shard_of(1))
        relay_to = (jnp.where(c == 0, x, 1 - x), jnp.where(c == 0, 1 - y, y), c)

        def relay_copy(w, sender_shard, piece):
            win = shard_window(w, sender_shard, c, piece)
            idx = (w * 3 + 2) * RELAY_PIECES + piece
            return pltpu.make_async_remote_copy(
                src_ref=win, dst_ref=win, send_sem=send_sems.at[idx], recv_sem=recv_sems.at[idx],
                device_id=relay_to, device_id_type=MESH)

        first = [ln_copy(k, s_me) for k in range(3)]
        for cp in first:
            cp.start()
        n_sent = 0
        for w in fetched:
            for q in pieces:
                to_bf16(n_sent)
                n_sent += 1
                for k in range(2):
                    cp = ici_copy(w, k, s_me, q)
                    cp.start()
                    first.append(cp)
        for k in range(n_sent, len(loads)):
            to_bf16(k)
        own = [pltpu.make_async_copy(stage[w], shard_window(w, s_me, None), local_sems.at[w]) for w in range(n_w)]
        for cp in own:
            cp.start()

        def tok_copy(t):
            return pltpu.make_async_copy(tok_ref.at[pl.ds(t * tok_tile, tok_tile), :], tok_buf.at[t % 2],
                                         tok_sems.at[t % 2])

        def prod_copy(t):
            return pltpu.make_async_copy(prod_buf.at[t % 2], prod_ref.at[pl.ds(t * tok_tile, tok_tile), :],
                                         prod_sems.at[t % 2])

        def product_tiles(tiles):
            for t in tiles:
                if t + 1 < n_tiles:
                    tok_copy(t + 1).start()
                tok_copy(t).wait()
                if t >= 2:
                    prod_copy(t - 2).wait()
                prod_buf[t % 2] = _dot(tok_buf[t % 2].astype(BF16), stage[0][...])
                prod_copy(t).start()

        if n_tiles:
            tok_copy(0).start()
        share = [4, 3, 3, 3, 3, 0, 0, 0]
        assert len(share) == 2 * RELAY_PIECES
        bounds = [sum(share[:k]) * n_tiles // sum(share) for k in range(len(share) + 1)]
        passed = []
        for w in fetched:
            for q in pieces:
                if w == fetched[0]:
                    product_tiles(range(bounds[q], bounds[q + 1]))
                for k in range(2):
                    ici_copy(w, k, shard_of(k), q).wait_recv()
                relay = relay_copy(w, relay_from, q)
                relay.start()
                passed.append(relay)
                for k in range(2):
                    fwd = d2d_copy(w, k, c, q)
                    fwd.start()
                    passed.append(fwd)
        for w in fetched:
            for q in pieces:
                if w == fetched[0]:
                    product_tiles(range(bounds[RELAY_PIECES + q], bounds[RELAY_PIECES + q + 1]))
                relay_copy(w, shard_of(2), q).wait_recv()
                fwd = d2d_copy(w, 2, c, q)
                fwd.start()
                passed.append(fwd)
        for w in fetched:
            for k in range(3):
                for q in pieces:
                    d2d_copy(w, k, 1 - c, q).wait_recv()
        for k in range(3):
            ln_copy(k, 2 * chips[k][0] + chips[k][1]).wait_recv()
        for cp in first + passed:
            cp.wait_send()
        for cp in own:
            cp.wait()
        for t in range(max(n_tiles - 2, 0), n_tiles):
            prod_copy(t).wait()

    vmem = pl.BlockSpec(memory_space=pltpu.VMEM)
    hbm = pl.BlockSpec(memory_space=pl.ANY)
    prod_cols = shards[0].shape[1]
    tok_args = [] if tokens is None else [tokens]
    tok_out = [] if tokens is None else [jax.ShapeDtypeStruct((tokens.shape[0], prod_cols), F32)]
    tok_scratch = [] if tokens is None else [
        pltpu.VMEM((2, tok_tile, tokens.shape[1]), F32), pltpu.VMEM((2, tok_tile, prod_cols), F32),
        pltpu.SemaphoreType.DMA((2,)), pltpu.SemaphoreType.DMA((2,))]
    return pl.pallas_call(
        body, name="gather_weights",
        in_specs=[hbm] * n_w + [vmem] + [hbm] * n_tok,
        out_specs=[hbm] * n_w + [vmem] + [hbm] * n_tok,
        out_shape=[jax.ShapeDtypeStruct(s, BF16) for s in full_shapes]
        + [jax.ShapeDtypeStruct((N_CHIPS,) + ln_shard.shape, F32)] + tok_out,
        scratch_shapes=[pltpu.VMEM(w.shape, F32) for w in shards] + [pltpu.VMEM(w.shape, BF16) for w in shards]
        + [pltpu.SemaphoreType.DMA((6 * RELAY_PIECES * n_w,)), pltpu.SemaphoreType.DMA((6 * RELAY_PIECES * n_w,)),
           pltpu.SemaphoreType.DMA((2 * RELAY_PIECES * len(fetched) + n_w - len(fetched),)),
           pltpu.SemaphoreType.DMA((n_w,)), pltpu.SemaphoreType.DMA((3,)), pltpu.SemaphoreType.DMA((3,))]
        + tok_scratch,
        compiler_params=pltpu.CompilerParams(vmem_limit_bytes=VMEM_LIMIT),
    )(*shards, ln_shard, *tok_args)


def _fetch_copy(full_ref, shard_shape, col_sharded, sender_shard, send_sems, recv_sems, idx, chip, c):
    win = _shard_window(full_ref, shard_shape, col_sharded, sender_shard, None)
    return pltpu.make_async_remote_copy(src_ref=win, dst_ref=win, send_sem=send_sems.at[idx],
                                        recv_sem=recv_sems.at[idx], device_id=(*chip, c), device_id_type=MESH)


def _fetch_start(fulls, shard_shapes, col_sharded):
    n = len(fulls)

    def body(*refs):
        full = refs[:n]
        send_sems, recv_sems = refs[n], refs[n + 1]
        token = refs[-1]
        x, y, c = _place()
        for w in range(n):
            for k, chip in enumerate(_other_chips(x, y)):
                _fetch_copy(full[w], shard_shapes[w], col_sharded[w], 2 * x + y, send_sems, recv_sems, w * 3 + k,
                            chip, c).start()
        token[...] = jnp.zeros_like(token)

    outs = pl.pallas_call(
        body, name="fetch_start",
        out_shape=(pltpu.SemaphoreType.DMA((3 * n,)), pltpu.SemaphoreType.DMA((3 * n,)),
                   *[pltpu.HBM(f.shape, f.dtype) for f in fulls], jax.ShapeDtypeStruct((8, 128), F32)),
        in_specs=[_HBM] * n,
        out_specs=(_SEM, _SEM, *([_HBM] * n), pl.BlockSpec(memory_space=pltpu.VMEM)),
        input_output_aliases={i: 2 + i for i in range(n)},
        compiler_params=pltpu.CompilerParams(has_side_effects=pltpu.SideEffectType.DATAFLOW_SIDE_EFFECTING),
    )(*[pltpu.with_memory_space_constraint(f, pltpu.HBM) for f in fulls])
    return dict(send=outs[0], recv=outs[1], full=list(outs[2:2 + n])), outs[-1]


def _fetch_wait(group, shard_shapes, col_sharded, after):
    n = len(group["full"])

    def body(*refs):
        full = refs[:n]
        send_sems, recv_sems = refs[n], refs[n + 1]
        x, y, c = _place()
        for w in range(n):
            for k, chip in enumerate(_other_chips(x, y)):
                _fetch_copy(full[w], shard_shapes[w], col_sharded[w], 2 * x + y, send_sems, recv_sems, w * 3 + k,
                            chip, c).wait_send()
                _fetch_copy(full[w], shard_shapes[w], col_sharded[w], 2 * chip[0] + chip[1], send_sems, recv_sems,
                            w * 3 + k, chip, c).wait_recv()

    outs = pl.pallas_call(
        body, name="fetch_wait", out_shape=tuple(pltpu.HBM(f.shape, f.dtype) for f in group["full"]),
        in_specs=[_HBM] * n + [_SEM, _SEM, pl.BlockSpec(memory_space=pl.ANY)],
        out_specs=tuple([_HBM] * n), input_output_aliases={i: i for i in range(n)},
        compiler_params=pltpu.CompilerParams(has_side_effects=pltpu.SideEffectType.DATAFLOW_SIDE_EFFECTING),
    )(*group["full"], group["send"], group["recv"], after)
    return list(outs)


_HBM = pl.BlockSpec(memory_space=pltpu.HBM)
_SEM = pl.BlockSpec(memory_space=pltpu.SEMAPHORE)
_N_PEER = N_DEV - 1


def _peer(x, y, c, k):
    return (x + (k >> 2)) % 2, (y + ((k >> 1) & 1)) % 2, (c + (k & 1)) % 2


def _exchange_copy(src_ref, land_ref, sliced, send_sems, recv_sems, idx, x, y, c, k):
    px, py, pc = _peer(x, y, c, k)
    src = src_ref.at[4 * px + 2 * py + pc] if sliced else src_ref
    return pltpu.make_async_remote_copy(
        src_ref=src, dst_ref=land_ref.at[4 * x + 2 * y + c],
        send_sem=send_sems.at[idx], recv_sem=recv_sems.at[idx], device_id=(px, py, pc), device_id_type=MESH)


def _exchange_start(tag, arrays, sliced):
    n = len(arrays)
    lands = [lax.empty(a.shape if s else (N_DEV,) + a.shape, a.dtype) for a, s in zip(arrays, sliced)]

    def body(*refs):
        src, land = refs[:n], refs[n:2 * n]
        send_sems, recv_sems = refs[2 * n], refs[2 * n + 1]
        token = refs[-1]
        x, y, c = _place()
        for w in range(n):
            for k in range(1, N_DEV):
                _exchange_copy(src[w], land[w], sliced[w], send_sems, recv_sems, w * _N_PEER + k - 1, x, y, c, k).start()
        token[...] = jnp.zeros_like(token)

    outs = pl.pallas_call(
        body, name="exchange_start_" + tag,
        out_shape=(pltpu.SemaphoreType.DMA((n * _N_PEER,)), pltpu.SemaphoreType.DMA((n * _N_PEER,)),
                   *[pltpu.HBM(a.shape, a.dtype) for a in arrays], *[pltpu.HBM(l.shape, l.dtype) for l in lands],
                   jax.ShapeDtypeStruct((8, 128), F32)),
        in_specs=[_HBM] * (2 * n),
        out_specs=(_SEM, _SEM, *([_HBM] * (2 * n)), pl.BlockSpec(memory_space=pltpu.VMEM)),
        input_output_aliases={i: 2 + i for i in range(2 * n)},
        compiler_params=pltpu.CompilerParams(has_side_effects=pltpu.SideEffectType.DATAFLOW_SIDE_EFFECTING),
    )(*[pltpu.with_memory_space_constraint(a, pltpu.HBM) for a in arrays],
      *[pltpu.with_memory_space_constraint(l, pltpu.HBM) for l in lands])
    return dict(send=outs[0], recv=outs[1], src=list(outs[2:2 + n]), land=list(outs[2 + n:2 + 2 * n]),
                sliced=list(sliced)), outs[-1]


def _exchange_wait(tag, groups, after):
    counts = [len(g["src"]) for g in groups]
    total = sum(counts)

    def body(*refs):
        pos = 0
        x, y, c = _place()
        for g, n in zip(groups, counts):
            src, land = refs[pos:pos + n], refs[pos + n:pos + 2 * n]
            send_sems, recv_sems = refs[pos + 2 * n], refs[pos + 2 * n + 1]
            pos += 2 * n + 2
            for w in range(n):
                for k in range(1, N_DEV):
                    cp = _exchange_copy(src[w], land[w], g["sliced"][w], send_sems, recv_sems,
                                        w * _N_PEER + k - 1, x, y, c, k)
                    cp.wait_send()
                    cp.wait_recv()

    operands, in_specs, aliases, out_shape = [], [], {}, []
    for g in groups:
        for a in g["src"] + g["land"]:
            aliases[len(operands)] = len(out_shape)
            out_shape.append(pltpu.HBM(a.shape, a.dtype))
            operands.append(a)
            in_specs.append(_HBM)
        operands += [g["send"], g["recv"]]
        in_specs += [_SEM, _SEM]
    operands.append(after)
    in_specs.append(pl.BlockSpec(memory_space=pl.ANY))
    outs = pl.pallas_call(
        body, name="exchange_wait_" + tag, out_shape=tuple(out_shape), in_specs=in_specs,
        out_specs=tuple([_HBM] * (2 * total)), input_output_aliases=aliases,
        compiler_params=pltpu.CompilerParams(has_side_effects=pltpu.SideEffectType.DATAFLOW_SIDE_EFFECTING),
    )(*operands)
    srcs, lands, pos = [], [], 0
    for n in counts:
        srcs += list(outs[pos:pos + n])
        lands += list(outs[pos + n:pos + 2 * n])
        pos += 2 * n
    return srcs, lands


def _sum_and_swap(tag, pieces, lands, small=None, small_land=None):
    n_w = len(pieces)
    n_small = 0 if small is None else 1

    def body(*refs):
        g_refs, land_refs = refs[:n_w], refs[n_w:2 * n_w]
        pos = 2 * n_w + 2 * n_small
        out_refs = refs[pos:pos + n_w]
        pos += n_w + n_small
        bufs = refs[pos:pos + n_w]
        load_sems, swap_send, swap_recv = refs[pos + n_w + 2 * n_small:]
        x, y, c = _place()
        me = 4 * x + 2 * y + c

        def slot(k):
            px, py, pc = _peer(x, y, c, k)
            return 4 * px + 2 * py + pc

        def swap_copy(w, half):
            rows = pieces[w].shape[1]
            win = out_refs[w].at[pl.ds(pl.multiple_of(half * rows, rows), rows), :]
            return pltpu.make_async_remote_copy(
                src_ref=win, dst_ref=win, send_sem=swap_send.at[w], recv_sem=swap_recv.at[w],
                device_id=(x, y, 1 - c), device_id_type=MESH)

        loads = []
        for w in range(n_w):
            per_w = [pltpu.make_async_copy(g_refs[w].at[me], bufs[w].at[me], load_sems.at[w * N_DEV])]
            per_w += [pltpu.make_async_copy(land_refs[w].at[slot(k)], bufs[w].at[slot(k)], load_sems.at[w * N_DEV + k])
                      for k in range(1, N_DEV)]
            loads.append(per_w)
        small_loads = []
        if n_small:
            small_ref, small_land_ref = refs[2 * n_w], refs[2 * n_w + 1]
            small_out = refs[2 * n_w + 2 + n_w]
            small_buf, small_sems = refs[pos + n_w], refs[pos + n_w + 1]
            small_loads = [pltpu.make_async_copy(small_land_ref.at[slot(k)], small_buf.at[slot(k)],
                                                 small_sems.at[k - 1]) for k in range(1, N_DEV)]
        for cp in [cp for per_w in loads for cp in per_w] + small_loads:
            cp.start()
        if n_small:
            small_buf[me] = small_ref[...]
        swaps = []
        for w in range(n_w):
            for cp in loads[w]:
                cp.wait()
            rows = pieces[w].shape[1]
            total = bufs[w][0].astype(F32)
            for p in range(1, N_DEV):
                total += bufs[w][p].astype(F32)
            out_refs[w][pl.ds(pl.multiple_of(c * rows, rows), rows), :] = total
            sw = swap_copy(w, c)
            sw.start()
            swaps.append(sw)
        if n_small:
            for cp in small_loads:
                cp.wait()
            total = small_buf[0]
            for p in range(1, N_DEV):
                total += small_buf[p]
            small_out[...] = total
        for w in range(n_w):
            swap_copy(w, 1 - c).wait_recv()
        for sw in swaps:
            sw.wait_send()

    vmem = pl.BlockSpec(memory_space=pltpu.VMEM)
    hbm = pl.BlockSpec(memory_space=pl.ANY)
    small_args = [small, small_land] if n_small else []
    small_shapes = [jax.ShapeDtypeStruct(small.shape, F32)] if n_small else []
    small_scratch = ([pltpu.VMEM((N_DEV,) + small.shape, F32), pltpu.SemaphoreType.DMA((_N_PEER,))]
                     if n_small else [])
    return pl.pallas_call(
        body, name="sum_and_swap_" + tag,
        in_specs=[hbm] * (2 * n_w) + [vmem, hbm] * n_small,
        out_specs=[vmem] * (n_w + n_small),
        out_shape=[jax.ShapeDtypeStruct((2 * p.shape[1], p.shape[2]), F32) for p in pieces] + small_shapes,
        scratch_shapes=[pltpu.VMEM(p.shape, BF16) for p in pieces] + small_scratch
        + [pltpu.SemaphoreType.DMA((n_w * N_DEV,)), pltpu.SemaphoreType.DMA((n_w,)),
           pltpu.SemaphoreType.DMA((n_w,))],
        compiler_params=pltpu.CompilerParams(vmem_limit_bytes=VMEM_LIMIT),
    )(*pieces, *lands, *small_args)


def _adamw_values(w, g_t, m, v):
    c1 = 1.0 - ADAM_B1 ** ADAM_STEP
    c2 = 1.0 - ADAM_B2 ** ADAM_STEP
    nm = ADAM_B1 * m + (1.0 - ADAM_B1) * g_t
    nv = ADAM_B2 * v + (1.0 - ADAM_B2) * (g_t * g_t)
    return -ADAM_LR * ((nm / c1) / (jnp.sqrt(nv / c2) + ADAM_EPS) + ADAM_WD * w), nm, nv


def _adamw_update(w_ref, g_ref, m_ref, v_ref, d_ref, nm_ref, nv_ref):
    d_ref[...], nm_ref[...], nv_ref[...] = _adamw_values(w_ref[...], g_ref[...], m_ref[...], v_ref[...])


def _adamw_small(packed, shard_index, names, weights, moments_m, moments_v):
    n = len(names)
    shapes = [weights[name].shape for name in names]
    flat = [a[name].reshape(-1, a[name].shape[-1]) for name in names for a in (weights, moments_m, moments_v)]

    def body(packed_ref, shard_ref, *refs):
        loss_row = _small_offset("loss")
        refs[-1][...] = packed_ref[loss_row:loss_row + 1, 0:1]
        for k, name in enumerate(names):
            w_ref, m_ref, v_ref = refs[3 * k:3 * k + 3]
            g_ref, d_ref, nm_ref, nv_ref = refs[3 * n + 4 * k:3 * n + 4 * k + 4]
            rows, cols = w_ref.shape
            first = _small_offset(name)
            if cols <= _LANES:
                blocks = [(slice(0, rows), packed_ref[first:first + rows, 0:cols])]
            else:
                per_row = cols // _LANES
                if cols < _SMALL_SHAPES[name][-1]:
                    first = first + shard_ref[0] * per_row
                blocks = [(slice(i, i + 1),
                           jnp.concatenate([packed_ref[pl.ds(first + i * per_row + j, 1), :] for j in range(per_row)],
                                           axis=1)) for i in range(rows)]
            for at, g_t in blocks:
                g_ref[at, :] = g_t
                d_ref[at, :], nm_ref[at, :], nv_ref[at, :] = _adamw_values(w_ref[at, :], g_t, m_ref[at, :],
                                                                           v_ref[at, :])

    vmem = pl.BlockSpec(memory_space=pltpu.VMEM)
    outs = pl.pallas_call(
        body, name="adamw_small",
        in_specs=[vmem, pl.BlockSpec(memory_space=pltpu.SMEM)] + [vmem] * (3 * n),
        out_shape=[jax.ShapeDtypeStruct(flat[3 * k].shape, F32) for k in range(n) for _ in range(4)]
        + [jax.ShapeDtypeStruct((1, 1), F32)],
    )(packed, shard_index.reshape(1).astype(jnp.int32), *flat)
    return [tuple(o.reshape(shapes[k]) for o in outs[4 * k:4 * k + 4]) for k in range(n)], outs[-1].reshape(())


def _adamw(label, w, g, m, v):
    shape = w.shape
    cols = shape[-1]
    rows = w.size // cols
    args = [a.reshape(rows, cols) for a in (w, g, m, v)]

    def body(w_ref, g_ref, m_ref, v_ref, g_out, d_ref, nm_ref, nv_ref):
        g_out[...] = g_ref[...]
        _adamw_update(w_ref, g_ref, m_ref, v_ref, d_ref, nm_ref, nv_ref)

    block_rows = 256 if rows % 256 == 0 and rows > 256 else rows
    spec = pl.BlockSpec((block_rows, cols), lambda i: (i, 0))
    outs = pl.pallas_call(
        body, name="adamw_" + label, grid=(rows // block_rows,),
        in_specs=[spec] * 4, out_specs=[spec] * 4,
        out_shape=[jax.ShapeDtypeStruct((rows, cols), F32)] * 4,
        compiler_params=_params(),
    )(*args)
    return [o.reshape(shape) for o in outs]


def _no_send(tag, arrays, sliced):
    return jnp.zeros((8, 128), F32)


def _local_step(x, tgt, w_in_a, later_weights, first_after, sgu_ln_g, sgu_ln_b, w_spatial, b_spatial,
                attn_sinks, rel_bias, post_ln_g, post_ln_b, send=_no_send, own_product=None):
    bsp_t = b_spatial.T
    g1, b1 = post_ln_g[0:1], post_ln_b[0:1]
    g2, b2 = post_ln_g[1:2], post_ln_b[1:2]
    onehot = _bucket_onehot()
    bias = _bias_expand(rel_bias.T, onehot)
    win = _window_tables()

    if own_product is None:
        xt, u, vh, z, rv, y = _layer_a_fwd(x, w_in_a, None, sgu_ln_g, sgu_ln_b, w_spatial, bsp_t, first_after)
    else:
        chip, p_own = own_product
        xt, u, vh, z, rv, y = lax.switch(
            chip, [functools.partial(_layer_a_fwd, own=own) for own in range(N_CHIPS)],
            x, w_in_a, p_own, sgu_ln_g, sgu_ln_b, w_spatial, bsp_t, first_after)
    w_out_a, w_kv, w_in_b, w_out_b = later_weights(y)
    xh1, rstd1, q, zb, kd, vd = _layer_b_proj(x, y, w_out_a, g1, b1, w_in_b, w_kv)
    o, probs, sink_probs, dr2, loss_vec, dg2, db2 = _layer_b_fwd(q, zb, kd, vd, bias, win, attn_sinks, xh1, g1, b1,
                                                                 w_out_b, g2, b2, tgt)
    dq, dzb, dkd, dvd, carry_k, carry_v, gw_out_b, dsink, dbias = _layer_b_bwd_attn(
        dr2, zb, o, q, kd, vd, probs, sink_probs, w_out_b)
    dr1, dg1, db1, gw_in_b, gw_kv = _layer_b_bwd_proj(xh1, rstd1, g1, b1, dr2, dq, dzb, dkd, dvd, carry_k, carry_v,
                                                      w_in_b, w_kv)
    gw_out_b = gw_out_b.reshape(N_DEV, -1, D_MODEL)
    gw_kv = gw_kv.reshape(N_DEV, -1, 2 * PAIR)
    after = send("b", [gw_out_b, gw_in_b, gw_kv], [True, True, True])
    dp, gw_out_a, dws, dbsp, dgs, dbs = _layer_a_bwd_mix(dr1, u, vh, z, y, rv, w_out_a, sgu_ln_g, sgu_ln_b,
                                                         w_spatial, bsp_t, after)
    gw_out_a = gw_out_a.reshape(N_DEV, -1, D_MODEL)
    small = _pack_small(dws, dbsp, dsink, dbias, onehot, (dg1, dg2), (db1, db2), dgs, dbs, loss_vec)
    after = send("a_out", [gw_out_a, small], [True, False])
    gw_in_a = _layer_a_bwd_win(xt, dp, after).reshape(N_DEV, D_MODEL // 2, -1)
    after = send("a_in", [gw_in_a], [True])
    after, updates = after if isinstance(after, tuple) else (after, ())
    grad_x, *updated = _layer_a_bwd_dx(dr1, dp, w_in_a, after, updates)

    pieces = [gw_in_a, gw_out_a, gw_kv, gw_in_b, gw_out_b]
    return grad_x, pieces, small, updated


def kernel(x, w_in_a, sgu_ln_g, sgu_ln_b, w_spatial, b_spatial, w_out_a, w_kv, w_in_b, attn_sinks, rel_bias, w_out_b, post_ln_g, post_ln_b, loss_target, m_w_in_a, m_sgu_ln_g, m_sgu_ln_b, m_w_spatial, m_b_spatial, m_w_out_a, m_w_kv, m_w_in_b, m_attn_sinks, m_rel_bias, m_w_out_b, m_post_ln_g, m_post_ln_b, v_w_in_a, v_sgu_ln_g, v_sgu_ln_b, v_w_spatial, v_b_spatial, v_w_out_a, v_w_kv, v_w_in_b, v_attn_sinks, v_rel_bias, v_w_out_b, v_post_ln_g, v_post_ln_b):
    weights = dict(w_in_a=w_in_a, sgu_ln_g=sgu_ln_g, sgu_ln_b=sgu_ln_b, w_spatial=w_spatial, b_spatial=b_spatial,
                   w_out_a=w_out_a, w_kv=w_kv, w_in_b=w_in_b, attn_sinks=attn_sinks, rel_bias=rel_bias,
                   w_out_b=w_out_b, post_ln_g=post_ln_g, post_ln_b=post_ln_b)
    moments_m = dict(w_in_a=m_w_in_a, sgu_ln_g=m_sgu_ln_g, sgu_ln_b=m_sgu_ln_b, w_spatial=m_w_spatial,
                     b_spatial=m_b_spatial, w_out_a=m_w_out_a, w_kv=m_w_kv, w_in_b=m_w_in_b,
                     attn_sinks=m_attn_sinks, rel_bias=m_rel_bias, w_out_b=m_w_out_b, post_ln_g=m_post_ln_g,
                     post_ln_b=m_post_ln_b)
    moments_v = dict(w_in_a=v_w_in_a, sgu_ln_g=v_sgu_ln_g, sgu_ln_b=v_sgu_ln_b, w_spatial=v_w_spatial,
                     b_spatial=v_b_spatial, w_out_a=v_w_out_a, w_kv=v_w_kv, w_in_b=v_w_in_b,
                     attn_sinks=v_attn_sinks, rel_bias=v_rel_bias, w_out_b=v_w_out_b, post_ln_g=v_post_ln_g,
                     post_ln_b=v_post_ln_b)
    order = ("w_in_a", "sgu_ln_g", "sgu_ln_b", "w_spatial", "b_spatial", "w_out_a", "w_kv", "w_in_b", "attn_sinks",
             "rel_bias", "w_out_b", "post_ln_g", "post_ln_b")

    shard_index = 2 * lax.axis_index("x") + lax.axis_index("y")
    ln_shard = jnp.concatenate([sgu_ln_g, sgu_ln_b], axis=0)
    shards = [w_in_a[0], w_out_a[0], w_kv, w_in_b[0], w_out_b[0]]
    col_sharded = [True, False, False, True, False]
    full_in_a, *later, ln_full, p_own = _gather_weights(shards, col_sharded, [True, False, False, False, False],
                                                        ln_shard, tokens=x[0])
    ln_full = jnp.transpose(ln_full, (1, 0, 2)).reshape(2, A_WIDTH)
    later_shapes = [s.shape for s in shards[1:]]
    fetch_group, fetch_token = _fetch_start(later, later_shapes, col_sharded[1:])

    def later_weights(y):
        return _fetch_wait(fetch_group, later_shapes, col_sharded[1:], y)

    groups, grads, deltas, new_m, new_v, scalars = {}, {}, {}, {}, {}, {}
    early = ("w_out_b", "w_in_b", "w_kv", "w_out_a")

    def two_dim(a):
        return a.reshape(-1, a.shape[-1])

    def send(tag, arrays, sliced):
        groups[tag], token = _exchange_start(tag, arrays, sliced)
        if tag != "a_in":
            return token
        srcs, lands = _exchange_wait("early", [groups["b"], groups["a_out"]], token)
        *reduced, packed_sum = _sum_and_swap("early", srcs[:4], lands[:4], srcs[4], lands[4])
        updates = [(two_dim(weights[n]), g, two_dim(moments_m[n]), two_dim(moments_v[n]))
                   for n, g in zip(early, reduced)]
        small_names = ("sgu_ln_g", "sgu_ln_b", "w_spatial", "b_spatial", "attn_sinks", "rel_bias", "post_ln_g",
                       "post_ln_b")
        def as_packed(arrays):
            return {n: (arrays[n].T if n == "rel_bias" else arrays[n]) for n in small_names}

        small_updates, scalars["loss"] = _adamw_small(packed_sum, shard_index, small_names, as_packed(weights),
                                                      as_packed(moments_m), as_packed(moments_v))
        for name, results in zip(small_names, small_updates):
            grads[name], deltas[name], new_m[name], new_v[name] = [
                a.T if name == "rel_bias" else a for a in results]
        return new_m["b_spatial"].reshape(A_GROUPS, CHUNK), updates

    grad_x, _, _, updated = _local_step(
        x[0], loss_target[0], full_in_a, later_weights, fetch_token, ln_full[0:1], ln_full[1:2], w_spatial[0],
        b_spatial[0], attn_sinks, rel_bias, post_ln_g, post_ln_b, send=send, own_product=(shard_index, p_own))
    for k, name in enumerate(early):
        grads[name], deltas[name], new_m[name], new_v[name] = [
            a.reshape(weights[name].shape) for a in updated[4 * k:4 * k + 4]]

    srcs, lands = _exchange_wait("late", [groups["a_in"]], grad_x)
    (g_in_a,) = _sum_and_swap("late", srcs, lands)
    grads["w_in_a"], deltas["w_in_a"], new_m["w_in_a"], new_v["w_in_a"] = _adamw(
        "w_in_a", w_in_a, g_in_a.reshape(w_in_a.shape), m_w_in_a, v_w_in_a)
    return (scalars["loss"], grad_x[None], *[grads[n] for n in order], *[deltas[n] for n in order],
            *[new_m[n] for n in order], *[new_v[n] for n in order])
```

```python
import functools
import math

import jax
import jax.numpy as jnp
from jax import lax
from jax.experimental import pallas as pl
from jax.experimental.pallas import tpu as pltpu

F32 = jnp.float32
BF16 = jnp.bfloat16

D_MODEL = 1024
A_WIDTH = 2048
A_GROUPS = 8
A_GROUP_DIM = 256
CHUNK = 128
N_HEADS = 16
N_KV = 2
HEAD_DIM = 64
PAIR = 2 * HEAD_DIM
B_WIDTH = 1024
REL_BUCKETS = 32
ALPHA = 4.0 ** 0.25
LN_EPS = 1e-5
NEG_INF = -1e30
SCALE = HEAD_DIM ** -0.5

ADAM_LR = 0.001
ADAM_B1 = 0.9
ADAM_B2 = 0.999
ADAM_EPS = 1e-08
ADAM_WD = 0.01
ADAM_STEP = 10

N_DEV = 8
N_CHIPS = 4
MESH = pl.DeviceIdType.MESH
VMEM_LIMIT = 56 * 1024 * 1024

TM_ATTN = 256
TM_BWD_ATTN = 512
TM_MM = 512
TM_WIN = 2048
_LANES = 128
_SUBLANES = 8


def _dot(a, b):
    return jnp.dot(a, b, preferred_element_type=F32)


def _dot_nt(a, b):
    return lax.dot_general(a, b, (((1,), (1,)), ((), ())), preferred_element_type=F32)


def _dot_tn(a, b):
    return lax.dot_general(a, b, (((0,), (0,)), ((), ())), preferred_element_type=F32)


def _ln_fwd(r):
    mu = jnp.mean(r, axis=-1, keepdims=True)
    rc = r - mu
    var = jnp.mean(rc * rc, axis=-1, keepdims=True)
    rstd = lax.rsqrt(var + LN_EPS)
    return rc * rstd, rstd


def _ln_bwd(dxh, xh, rstd):
    m1 = jnp.mean(dxh, axis=-1, keepdims=True)
    m2 = jnp.mean(dxh * xh, axis=-1, keepdims=True)
    return rstd * (dxh - m1 - xh * m2)


def _silu_parts(z):
    sg = jax.nn.sigmoid(z)
    return z * sg, sg * (1.0 + z * (1.0 - sg))


def _dup_halves(blk):
    sw = pltpu.roll(blk, HEAD_DIM, 1)
    lo = lax.broadcasted_iota(jnp.int32, blk.shape, 1) < HEAD_DIM
    return jnp.where(lo, blk, sw), jnp.where(lo, sw, blk)


def _fold_halves(blk):
    return blk + pltpu.roll(blk, HEAD_DIM, 1)


def _resident(shape):
    nd = len(shape)
    return pl.BlockSpec(shape, lambda *_: (0,) * nd, pipeline_mode=pl.Buffered(1))


def _const(shape):
    nd = len(shape)
    return pl.BlockSpec(shape, lambda *_: (0,) * nd)


def _rows(tm, cols):
    return pl.BlockSpec((tm, cols), lambda i: (i, 0))


def _params(sem=("arbitrary",)):
    return pltpu.CompilerParams(dimension_semantics=sem, vmem_limit_bytes=VMEM_LIMIT)


def _spatial_mix(ws_ref, bsp_ref, vn, s_scr, n_chunks):
    tri = (lax.broadcasted_iota(jnp.int32, (CHUNK, CHUNK), 0)
           >= lax.broadcasted_iota(jnp.int32, (CHUNK, CHUNK), 1))
    for g in range(A_GROUPS):
        wsg = jnp.where(tri, ws_ref[g], 0.0).astype(BF16)
        cols = slice(g * A_GROUP_DIM, (g + 1) * A_GROUP_DIM)
        for ci in range(n_chunks):
            rows = slice(ci * CHUNK, (ci + 1) * CHUNK)
            s_scr[rows, cols] = _dot(wsg, vn[rows, cols]) + bsp_ref[:, g:g + 1]


def _layer_a_fwd(x, w_in, p_own, lng, lnb, ws, bsp_t, after, own=None):
    t_len = x.shape[0]
    tm = TM_ATTN
    shard_cols = 3 * A_WIDTH // N_CHIPS

    def body(x_ref, win_ref, *rest):
        pown_ref = rest[0] if own is not None else None
        (lng_ref, lnb_ref, ws_ref, bsp_ref, after_ref,
         xt_ref, u_ref, vh_ref, z_ref, rv_ref, y_ref, s_scr) = rest[0 if own is None else 1:]
        x_t = x_ref[...]
        xb = x_t.astype(BF16)
        xt_ref[...] = x_t.T.astype(BF16)
        if own is None:
            u = _dot(xb, win_ref[:, 0:A_WIDTH])
            v = _dot(xb, win_ref[:, A_WIDTH:2 * A_WIDTH])
            z = _dot(xb, win_ref[:, 2 * A_WIDTH:3 * A_WIDTH])
        else:
            pieces = {s: pown_ref[...] if s == own else _dot(xb, win_ref[:, s * shard_cols:(s + 1) * shard_cols])
                      for s in (1, 2, 3, 0)}
            p = jnp.concatenate([pieces[s] for s in range(N_CHIPS)], axis=1)
            u, v, z = p[:, 0:A_WIDTH], p[:, A_WIDTH:2 * A_WIDTH], p[:, 2 * A_WIDTH:3 * A_WIDTH]
        vh, rv = _ln_fwd(v)
        vn = (vh * lng_ref[...] + lnb_ref[...]).astype(BF16)
        _spatial_mix(ws_ref, bsp_ref, vn, s_scr, tm // CHUNK)
        sz, _ = _silu_parts(z)
        y_ref[...] = (u * s_scr[...] * sz).astype(BF16)
        u_ref[...] = u.astype(BF16)
        vh_ref[...] = vh.astype(BF16)
        z_ref[...] = z.astype(BF16)
        rv_ref[...] = rv

    wide = jax.ShapeDtypeStruct((t_len, A_WIDTH), BF16)
    product = [] if own is None else [p_own]
    return pl.pallas_call(
        body, name="layer_a_fwd" if own is None else "layer_a_fwd_own%d" % own, grid=(t_len // tm,),
        in_specs=[_rows(tm, D_MODEL), _resident(w_in.shape)] + [_rows(tm, shard_cols)] * len(product)
        + [_const(lng.shape), _const(lnb.shape), _const(ws.shape), _const(bsp_t.shape), _const(after.shape)],
        out_specs=[pl.BlockSpec((D_MODEL, tm), lambda i: (0, i)), _rows(tm, A_WIDTH), _rows(tm, A_WIDTH),
                   _rows(tm, A_WIDTH), _rows(tm, 1), _rows(tm, A_WIDTH)],
        out_shape=[jax.ShapeDtypeStruct((D_MODEL, t_len), BF16), wide, wide, wide,
                   jax.ShapeDtypeStruct((t_len, 1), F32), wide],
        scratch_shapes=[pltpu.VMEM((tm, A_WIDTH), F32)],
        compiler_params=_params(),
    )(x, w_in, *product, lng, lnb, ws, bsp_t, after)


def _layer_b_proj(x, y, w_out_a, g1, b1, w_in, w_kv):
    t_len = x.shape[0]
    tm = 2 * TM_MM

    def body(x_ref, y_ref, wout_ref, g_ref, b_ref, win_ref, wkv_ref, xh_ref, r1_ref, q_ref, z_ref, kd_ref, vd_ref):
        halves = [slice(k * TM_MM, (k + 1) * TM_MM) for k in range(2)]
        projected = [_dot(y_ref[rows, :], wout_ref[...]) for rows in halves]
        for rows, out_a in zip(halves, projected):
            xh, r1 = _ln_fwd(ALPHA * x_ref[rows, :] + out_a)
            xh_ref[rows, :] = xh
            r1_ref[rows, :] = r1
            h1 = (xh * g_ref[...] + b_ref[...]).astype(BF16)
            q_ref[rows, :] = (_dot(h1, win_ref[:, 0:B_WIDTH]) * SCALE).astype(BF16)
            z_ref[rows, :] = _dot(h1, win_ref[:, B_WIDTH:2 * B_WIDTH]).astype(BF16)
            kv = _dot(h1, wkv_ref[...])
            k0, k1 = _dup_halves(kv[:, 0:PAIR])
            v0, v1 = _dup_halves(kv[:, PAIR:2 * PAIR])
            kd_ref[rows, 0:PAIR] = k0.astype(BF16)
            kd_ref[rows, PAIR:2 * PAIR] = k1.astype(BF16)
            vd_ref[rows, 0:PAIR] = v0.astype(BF16)
            vd_ref[rows, PAIR:2 * PAIR] = v1.astype(BF16)

    return pl.pallas_call(
        body, name="layer_b_proj", grid=(t_len // tm,),
        in_specs=[_rows(tm, D_MODEL), _rows(tm, A_WIDTH), _resident(w_out_a.shape), _const(g1.shape),
                  _const(b1.shape), _resident(w_in.shape), _resident(w_kv.shape)],
        out_specs=[_rows(tm, D_MODEL), _rows(tm, 1), _rows(tm, B_WIDTH), _rows(tm, B_WIDTH), _rows(tm, 2 * PAIR),
                   _rows(tm, 2 * PAIR)],
        out_shape=[jax.ShapeDtypeStruct((t_len, D_MODEL), F32), jax.ShapeDtypeStruct((t_len, 1), F32),
                   jax.ShapeDtypeStruct((t_len, B_WIDTH), BF16), jax.ShapeDtypeStruct((t_len, B_WIDTH), BF16),
                   jax.ShapeDtypeStruct((t_len, 2 * PAIR), BF16), jax.ShapeDtypeStruct((t_len, 2 * PAIR), BF16)],
        compiler_params=_params(),
    )(x, y, w_out_a, g1, b1, w_in, w_kv)


GROUP = N_HEADS // N_KV
GROUP_Q = GROUP * CHUNK


def _window_tables():
    j = jnp.arange(2 * CHUNK, dtype=jnp.int32)[:, None]
    t = jnp.arange(CHUNK, dtype=jnp.int32)[None, :]
    dist = t + CHUNK - j
    inside = (dist >= 0) & (dist < CHUNK)
    return jnp.stack([inside & (j >= CHUNK), inside]).astype(F32)


def _band(ref, chunk_index, kvh):
    prev0 = pl.multiple_of(jnp.maximum(chunk_index - 1, 0) * CHUNK, CHUNK)
    cur0 = pl.multiple_of(chunk_index * CHUNK, CHUNK)
    cols = slice(kvh * PAIR, (kvh + 1) * PAIR)
    return jnp.concatenate([ref[pl.ds(prev0, CHUNK), cols], ref[pl.ds(cur0, CHUNK), cols]], axis=0)


def _group_tables(bias_ref, win_ref, sink_ref, chunk_index, kvh):
    bias = jnp.concatenate([bias_ref[kvh * GROUP + j] for j in range(GROUP)], axis=1)
    win = win_ref[jnp.minimum(chunk_index, 1)]
    mask = jnp.concatenate([win] * GROUP, axis=1) > 0.5
    sink = jnp.concatenate([jnp.full((1, CHUNK), sink_ref[0, kvh * GROUP + j], F32) for j in range(GROUP)], axis=1)
    return bias, mask, sink


def _attn_probs(qs, kband, bias, mask, sink):
    logits = jnp.where(mask, _dot_nt(kband, qs) + bias, NEG_INF)
    m = jnp.maximum(jnp.max(logits, axis=0, keepdims=True), sink)
    e = jnp.exp(logits - m)
    es = jnp.exp(sink - m)
    inv = 1.0 / (jnp.sum(e, axis=0, keepdims=True) + es)
    return e * inv, es * inv


def _half_mask():
    return lax.broadcasted_iota(jnp.int32, (CHUNK, PAIR), 1) < HEAD_DIM


def _stack_heads(src_ref, rows, kvh, dst_scr, lo):
    for j in range(GROUP):
        h = kvh * GROUP + j
        blk = src_ref[rows, (h // 2) * PAIR:(h // 2 + 1) * PAIR].astype(F32)
        keep = lo if h % 2 == 0 else ~lo
        dst_scr[j * CHUNK:(j + 1) * CHUNK, :] = jnp.where(keep, blk, 0.0).astype(BF16)


def _probs_spec(tm):
    return pl.BlockSpec((tm // CHUNK, N_KV, 2 * CHUNK, GROUP_Q), lambda i: (i, 0, 0, 0))


def _sink_probs_spec(tiles=1):
    return pl.BlockSpec((tiles, 8, GROUP_Q), lambda i: (i, 0, 0))


def _unstack_pairs(stacked, pp, lo):
    return jnp.where(lo, stacked[(2 * pp) * CHUNK:(2 * pp + 1) * CHUNK], stacked[(2 * pp + 1) * CHUNK:(2 * pp + 2) * CHUNK])


def _layer_b_fwd(q, zb, kd, vd, bias, win, sinks, xh1, g1, b1, w_out, g2, b2, tgt):
    t_len = q.shape[0]
    tm = 2 * TM_ATTN

    def body(q_ref, z_ref, kd_ref, vd_ref, bias_ref, win_ref, sink_ref, xh_ref, g1_ref, b1_ref, wout_ref, g2_ref,
             b2_ref, tgt_ref, o_ref, p_ref, ps_ref, dr_ref, loss_ref, dg_ref, db_ref, o_scr, qs_scr):
        i = pl.program_id(0)

        @pl.when(i == 0)
        def _():
            loss_ref[...] = jnp.zeros_like(loss_ref)
            dg_ref[...] = jnp.zeros_like(dg_ref)
            db_ref[...] = jnp.zeros_like(db_ref)

        lo = _half_mask()
        ps_ref[...] = jnp.zeros_like(ps_ref)
        per_part = TM_ATTN // CHUNK
        for part in range(tm // TM_ATTN):
            part_rows = slice(part * TM_ATTN, (part + 1) * TM_ATTN)
            for cp in range(per_part):
                ci = part * per_part + cp
                cg = i * (tm // CHUNK) + ci
                rows = slice(ci * CHUNK, (ci + 1) * CHUNK)
                for kvh in range(N_KV):
                    kband = _band(kd_ref, cg, kvh)
                    vband = _band(vd_ref, cg, kvh)
                    bias_g, mask, sink = _group_tables(bias_ref, win_ref, sink_ref, cg, kvh)
                    _stack_heads(q_ref, rows, kvh, qs_scr, lo)
                    p, p_sink = _attn_probs(qs_scr[...], kband, bias_g, mask, sink)
                    p = p.astype(BF16)
                    p_ref[ci, kvh] = p
                    ps_ref[part, cp * N_KV + kvh:cp * N_KV + kvh + 1, :] = p_sink
                    o_stack = _dot_tn(p, vband)
                    for pp in range(GROUP // 2):
                        pair = kvh * (GROUP // 2) + pp
                        o_scr[rows, pair * PAIR:(pair + 1) * PAIR] = _unstack_pairs(o_stack, pp, lo)
            o = o_scr[part_rows, :]
            o_ref[part_rows, :] = o.astype(BF16)
            sz, _ = _silu_parts(z_ref[part_rows, :].astype(F32))
            y = (o * sz).astype(BF16)
            h1 = xh_ref[part_rows, :] * g1_ref[...] + b1_ref[...]
            r = ALPHA * h1 + _dot(y, wout_ref[...])
            xh2, rstd2 = _ln_fwd(r)
            diff = xh2 * g2_ref[...] + b2_ref[...] - tgt_ref[part_rows, :]
            loss_ref[...] += jnp.sum(diff * diff, axis=0, keepdims=True)
            dh2 = diff * (1.0 / D_MODEL)
            dg_ref[...] += jnp.sum(dh2 * xh2, axis=0, keepdims=True)
            db_ref[...] += jnp.sum(dh2, axis=0, keepdims=True)
            dr_ref[part_rows, :] = _ln_bwd(dh2 * g2_ref[...], xh2, rstd2)

    vec = jax.ShapeDtypeStruct((1, D_MODEL), F32)
    return pl.pallas_call(
        body, name="layer_b_fwd", grid=(t_len // tm,),
        in_specs=[_rows(tm, B_WIDTH), _rows(tm, B_WIDTH), _resident(kd.shape), _resident(vd.shape),
                  _resident(bias.shape), _resident(win.shape), pl.BlockSpec(memory_space=pltpu.SMEM),
                  _rows(tm, D_MODEL), _const(g1.shape), _const(b1.shape), _resident(w_out.shape), _const(g2.shape),
                  _const(b2.shape), _rows(tm, D_MODEL)],
        out_specs=[_rows(tm, B_WIDTH), _probs_spec(tm), _sink_probs_spec(tm // TM_ATTN), _rows(tm, D_MODEL)]
        + [_const((1, D_MODEL))] * 3,
        out_shape=[jax.ShapeDtypeStruct((t_len, B_WIDTH), BF16),
                   jax.ShapeDtypeStruct((t_len // CHUNK, N_KV, 2 * CHUNK, GROUP_Q), BF16),
                   jax.ShapeDtypeStruct((t_len // TM_ATTN, 8, GROUP_Q), F32),
                   jax.ShapeDtypeStruct((t_len, D_MODEL), F32), vec, vec, vec],
        scratch_shapes=[pltpu.VMEM((tm, B_WIDTH), F32), pltpu.VMEM((GROUP_Q, PAIR), BF16)],
        compiler_params=_params(),
    )(q, zb, kd, vd, bias, win, sinks, xh1, g1, b1, w_out, g2, b2, tgt)


def _layer_b_bwd_attn(dr2, zb, o, q, kd, vd, probs, sink_probs, w_out):
    t_len = q.shape[0]
    tm = TM_BWD_ATTN
    n_steps = t_len // tm
    n_chunks = tm // CHUNK
    per_part = TM_ATTN // CHUNK

    def body(dr_ref, z_ref, o_ref, q_ref, kd_ref, vd_ref, p_ref, ps_ref, wout_ref,
             dq_ref, dz_ref, dkd_ref, dvd_ref, ck_ref, cv_ref, gw_ref, dsink_ref, dbias_ref,
             do_scr, qs_scr, dos_scr, gw_acc):
        i = pl.program_id(0)

        @pl.when(i == 0)
        def _():
            gw_acc[...] = jnp.zeros_like(gw_acc)
            dsink_ref[...] = jnp.zeros_like(dsink_ref)
            dbias_ref[...] = jnp.zeros_like(dbias_ref)

        drb = dr_ref[...].astype(BF16)
        per_kvh = 2
        n_blocks = N_KV * per_kvh
        block_cols = B_WIDTH // n_blocks

        def through_gate(b):
            cols = slice(b * block_cols, (b + 1) * block_cols)
            dy = _dot_nt(drb, wout_ref[cols, :])
            sz, dsz = _silu_parts(z_ref[:, cols].astype(F32))
            o_t = o_ref[:, cols].astype(F32)
            dz_ref[:, cols] = (dy * o_t * dsz).astype(BF16)
            do_scr[:, cols] = (dy * sz).astype(BF16)
            return (o_t * sz).astype(BF16)

        def weight_gradient(b, gated):
            cols = slice(b * block_cols, (b + 1) * block_cols)
            gw_acc[cols, :] += _dot_tn(gated, drb)

        gated = {b: through_gate(b) for b in range(per_kvh)}

        lo = _half_mask()
        for kvh in range(N_KV):
            kcols = slice(kvh * PAIR, (kvh + 1) * PAIR)
            dk_bands, dv_bands = [], []
            for ci in range(n_chunks):
                unit = kvh * n_chunks + ci
                if ci < per_kvh and kvh + 1 < N_KV:
                    gated[(kvh + 1) * per_kvh + ci] = through_gate((kvh + 1) * per_kvh + ci)
                if unit in gated:
                    weight_gradient(unit, gated.pop(unit))
                cg = i * n_chunks + ci
                rows = slice(ci * CHUNK, (ci + 1) * CHUNK)
                kband = _band(kd_ref, cg, kvh)
                vband = _band(vd_ref, cg, kvh)
                _stack_heads(q_ref, rows, kvh, qs_scr, lo)
                _stack_heads(do_scr, rows, kvh, dos_scr, lo)
                qs = qs_scr[...]
                dos = dos_scr[...]
                pb = p_ref[ci, kvh]
                p = pb.astype(F32)
                sink_row = (ci % per_part) * N_KV + kvh
                p_sink = ps_ref[ci // per_part, sink_row:sink_row + 1, :]
                dp = _dot_nt(vband, dos)
                delta = jnp.sum(p * dp, axis=0, keepdims=True)
                dlog = p * (dp - delta)
                for j in range(GROUP):
                    dbias_ref[kvh * GROUP + j] += dlog[:, j * CHUNK:(j + 1) * CHUNK]
                dsink_ref[kvh:kvh + 1, :] += -(p_sink * delta)
                ds = dlog.astype(BF16)
                dq_stack = _dot_tn(ds, kband) * SCALE
                for pp in range(GROUP // 2):
                    pair = kvh * (GROUP // 2) + pp
                    dq_ref[rows, pair * PAIR:(pair + 1) * PAIR] = _unstack_pairs(dq_stack, pp, lo).astype(BF16)
                dk_bands.append(_dot(ds, qs))
                dv_bands.append(_dot(pb, dos))
            for bands, out_ref, carry_ref in ((dk_bands, dkd_ref, ck_ref), (dv_bands, dvd_ref, cv_ref)):
                carry_ref[0, :, kcols] = bands[0][0:CHUNK]
                for ci in range(n_chunks):
                    own = bands[ci][CHUNK:2 * CHUNK]
                    if ci + 1 < n_chunks:
                        own = own + bands[ci + 1][0:CHUNK]
                    out_ref[ci * CHUNK:(ci + 1) * CHUNK, kcols] = own

        @pl.when(i == n_steps - 1)
        def _():
            gw_ref[...] = gw_acc[...].astype(BF16)

    carry_spec = pl.BlockSpec((1, CHUNK, 2 * PAIR), lambda i: (i, 0, 0))
    carry_shape = jax.ShapeDtypeStruct((n_steps, CHUNK, 2 * PAIR), F32)
    bias_shape = (N_HEADS, 2 * CHUNK, CHUNK)
    return pl.pallas_call(
        body, name="layer_b_bwd_attn", grid=(n_steps,),
        in_specs=[_rows(tm, D_MODEL), _rows(tm, B_WIDTH), _rows(tm, B_WIDTH), _rows(tm, B_WIDTH),
                  _resident(kd.shape), _resident(vd.shape), _probs_spec(tm), _sink_probs_spec(tm // TM_ATTN),
                  _resident(w_out.shape)],
        out_specs=[_rows(tm, B_WIDTH), _rows(tm, B_WIDTH), _rows(tm, 2 * PAIR), _rows(tm, 2 * PAIR),
                   carry_spec, carry_spec, _const(w_out.shape), _const((N_KV, GROUP_Q)), _const(bias_shape)],
        out_shape=[jax.ShapeDtypeStruct((t_len, B_WIDTH), BF16), jax.ShapeDtypeStruct((t_len, B_WIDTH), BF16),
                   jax.ShapeDtypeStruct((t_len, 2 * PAIR), F32), jax.ShapeDtypeStruct((t_len, 2 * PAIR), F32),
                   carry_shape, carry_shape, jax.ShapeDtypeStruct(w_out.shape, BF16),
                   jax.ShapeDtypeStruct((N_KV, GROUP_Q), F32), jax.ShapeDtypeStruct(bias_shape, F32)],
        scratch_shapes=[pltpu.VMEM((tm, B_WIDTH), BF16), pltpu.VMEM((GROUP_Q, PAIR), BF16),
                        pltpu.VMEM((GROUP_Q, PAIR), BF16), pltpu.VMEM(w_out.shape, F32)],
        compiler_params=_params(),
    )(dr2, zb, o, q, kd, vd, probs, sink_probs, w_out)


def _layer_b_bwd_proj(xh1, rstd1, g1, b1, dr2, dq, dzb, dkd, dvd, carry_k, carry_v, w_in, w_kv):
    t_len = xh1.shape[0]
    tm = TM_MM
    n_steps = t_len // tm
    per_tile = tm // TM_BWD_ATTN
    n_carry = carry_k.shape[0]

    def body(xh_ref, rstd_ref, g_ref, b_ref, dr2_ref, dq_ref, dz_ref, dkd_ref, dvd_ref, *rest):
        carry_refs = rest[:2 * per_tile]
        win_ref, wkv_ref, dr1_ref, dg_ref, db_ref, gwin_ref, gwkv_ref, acc_in, acc_kv = rest[2 * per_tile:]
        i = pl.program_id(0)

        @pl.when(i == 0)
        def _():
            acc_in[...] = jnp.zeros_like(acc_in)
            acc_kv[...] = jnp.zeros_like(acc_kv)
            dg_ref[...] = jnp.zeros_like(dg_ref)
            db_ref[...] = jnp.zeros_like(db_ref)

        lo = lax.broadcasted_iota(jnp.int32, (tm, PAIR), 1) < HEAD_DIM

        def heads_gradient(tile_ref, refs):
            parts = []
            for a in range(per_tile):
                parts.append(tile_ref[a * TM_BWD_ATTN:(a + 1) * TM_BWD_ATTN - CHUNK, :])
                carry = refs[a][0]
                if a == per_tile - 1:
                    carry = jnp.where(i < n_steps - 1, carry, 0.0)
                parts.append(tile_ref[(a + 1) * TM_BWD_ATTN - CHUNK:(a + 1) * TM_BWD_ATTN, :] + carry)
            dup = jnp.concatenate(parts, axis=0)
            return jnp.where(lo, _fold_halves(dup[:, 0:PAIR]), _fold_halves(dup[:, PAIR:2 * PAIR]))

        xh = xh_ref[...]
        h1 = (xh * g_ref[...] + b_ref[...]).astype(BF16)
        dq_t = dq_ref[...]
        dz_t = dz_ref[...]
        dkv = jnp.concatenate([heads_gradient(dkd_ref, carry_refs[:per_tile]),
                               heads_gradient(dvd_ref, carry_refs[per_tile:])], axis=1).astype(BF16)
        dh1 = ALPHA * dr2_ref[...]
        dh1 += _dot_nt(dq_t, win_ref[:, 0:B_WIDTH])
        dh1 += _dot_nt(dz_t, win_ref[:, B_WIDTH:2 * B_WIDTH])
        dh1 += _dot_nt(dkv, wkv_ref[...])
        acc_in[:, 0:B_WIDTH] += _dot_tn(h1, dq_t)
        acc_in[:, B_WIDTH:2 * B_WIDTH] += _dot_tn(h1, dz_t)
        acc_kv[...] += _dot_tn(h1, dkv)
        dg_ref[...] += jnp.sum(dh1 * xh, axis=0, keepdims=True)
        db_ref[...] += jnp.sum(dh1, axis=0, keepdims=True)
        dr1_ref[...] = _ln_bwd(dh1 * g_ref[...], xh, rstd_ref[...])

        @pl.when(i == n_steps - 1)
        def _():
            half_rows = D_MODEL // 2
            shard_cols = 2 * B_WIDTH // N_CHIPS
            for s in range(N_CHIPS):
                for c in range(2):
                    gwin_ref[2 * s + c] = acc_in[c * half_rows:(c + 1) * half_rows,
                                                 s * shard_cols:(s + 1) * shard_cols].astype(BF16)
            gwkv_ref[...] = acc_kv[...].astype(BF16)

    vec = jax.ShapeDtypeStruct((1, D_MODEL), F32)
    gwin_shape = (N_DEV, D_MODEL // 2, 2 * B_WIDTH // N_CHIPS)

    def carry_spec(a):
        return pl.BlockSpec((1, CHUNK, 2 * PAIR), lambda i: (jnp.minimum(per_tile * i + a + 1, n_carry - 1), 0, 0))

    carry_specs = [carry_spec(a) for a in range(per_tile)]
    return pl.pallas_call(
        body, name="layer_b_bwd_proj", grid=(n_steps,),
        in_specs=[_rows(tm, D_MODEL), _rows(tm, 1), _const(g1.shape), _const(b1.shape), _rows(tm, D_MODEL),
                  _rows(tm, B_WIDTH), _rows(tm, B_WIDTH), _rows(tm, 2 * PAIR), _rows(tm, 2 * PAIR)]
        + carry_specs + carry_specs + [_resident(w_in.shape), _resident(w_kv.shape)],
        out_specs=[_rows(tm, D_MODEL), _const((1, D_MODEL)), _const((1, D_MODEL)), _const(gwin_shape),
                   _const(w_kv.shape)],
        out_shape=[jax.ShapeDtypeStruct((t_len, D_MODEL), F32), vec, vec,
                   jax.ShapeDtypeStruct(gwin_shape, BF16), jax.ShapeDtypeStruct(w_kv.shape, BF16)],
        scratch_shapes=[pltpu.VMEM(w_in.shape, F32), pltpu.VMEM(w_kv.shape, F32)],
        compiler_params=_params(),
    )(xh1, rstd1, g1, b1, dr2, dq, dzb, dkd, dvd, *([carry_k] * per_tile), *([carry_v] * per_tile), w_in, w_kv)


def _layer_a_bwd_mix(dr1, u, vh, z, y, rv, w_out, lng, lnb, ws, bsp_t, after):
    t_len = u.shape[0]
    tm = TM_ATTN
    n_steps = t_len // tm

    def body(dr_ref, u_ref, vh_ref, z_ref, y_ref, rv_ref, wout_ref, lng_ref, lnb_ref, ws_ref, bsp_ref, after_ref,
             dp_ref, gw_ref, dws_ref, dbsp_ref, dgs_ref, dbs_ref, dvn_scr, gw_acc):
        i = pl.program_id(0)

        @pl.when(i == 0)
        def _():
            gw_acc[...] = jnp.zeros_like(gw_acc)
            dws_ref[...] = jnp.zeros_like(dws_ref)
            dbsp_ref[...] = jnp.zeros_like(dbsp_ref)
            dgs_ref[...] = jnp.zeros_like(dgs_ref)
            dbs_ref[...] = jnp.zeros_like(dbs_ref)

        drb = dr_ref[...].astype(BF16)

        def group_cols(g):
            return slice(g * A_GROUP_DIM, (g + 1) * A_GROUP_DIM)

        tri = (lax.broadcasted_iota(jnp.int32, (CHUNK, CHUNK), 0)
               >= lax.broadcasted_iota(jnp.int32, (CHUNK, CHUNK), 1))
        lane = lax.broadcasted_iota(jnp.int32, (CHUNK, CHUNK), 1)
        ones = jnp.ones((CHUNK, A_GROUP_DIM), BF16)
        dbsp = jnp.zeros((CHUNK, CHUNK), F32)
        dy_next = _dot_nt(drb, wout_ref[group_cols(0), :])
        for g in range(A_GROUPS):
            wsg = jnp.where(tri, ws_ref[g], 0.0).astype(BF16)
            cols = group_cols(g)
            cols_z = slice(2 * A_WIDTH + g * A_GROUP_DIM, 2 * A_WIDTH + (g + 1) * A_GROUP_DIM)
            dy_g = dy_next
            if g + 1 < A_GROUPS:
                dy_next = _dot_nt(drb, wout_ref[group_cols(g + 1), :])
            gw_acc[cols, :] += _dot_tn(y_ref[:, cols], drb)
            both = jnp.zeros((CHUNK, 2 * CHUNK), F32)
            for ci in range(tm // CHUNK):
                rows = slice(ci * CHUNK, (ci + 1) * CHUNK)
                vn = (vh_ref[rows, cols].astype(F32) * lng_ref[:, cols] + lnb_ref[:, cols]).astype(BF16)
                s = _dot(wsg, vn) + bsp_ref[:, g:g + 1]
                sz, dsz = _silu_parts(z_ref[rows, cols].astype(F32))
                dy = dy_g[rows]
                t = dy * u_ref[rows, cols].astype(F32)
                dp_ref[rows, cols] = (dy * (s * sz)).astype(BF16)
                dp_ref[rows, cols_z] = (t * s * dsz).astype(BF16)
                ds_b = (t * sz).astype(BF16)
                both += _dot_nt(ds_b, jnp.concatenate([vn, ones], axis=0))
                dvn_scr[rows, cols] = _dot_tn(wsg, ds_b)
            dws_ref[g] += jnp.where(tri, both[:, 0:CHUNK], 0.0)
            dbsp = jnp.where(lane == g, both[:, CHUNK:2 * CHUNK], dbsp)
        dbsp_ref[...] += dbsp
        dvn = dvn_scr[...]
        vh_t = vh_ref[...].astype(F32)
        dgs_ref[...] += jnp.sum(dvn * vh_t, axis=0, keepdims=True)
        dbs_ref[...] += jnp.sum(dvn, axis=0, keepdims=True)
        dp_ref[:, A_WIDTH:2 * A_WIDTH] = _ln_bwd(dvn * lng_ref[...], vh_t, rv_ref[...]).astype(BF16)

        @pl.when(i == n_steps - 1)
        def _():
            gw_ref[...] = gw_acc[...].astype(BF16)

    wide = jax.ShapeDtypeStruct((1, A_WIDTH), F32)
    return pl.pallas_call(
        body, name="layer_a_bwd_mix", grid=(n_steps,),
        in_specs=[_rows(tm, D_MODEL), _rows(tm, A_WIDTH), _rows(tm, A_WIDTH), _rows(tm, A_WIDTH), _rows(tm, A_WIDTH),
                  _rows(tm, 1), _resident(w_out.shape), _const(lng.shape), _const(lnb.shape), _const(ws.shape),
                  _const(bsp_t.shape), _const(after.shape)],
        out_specs=[_rows(tm, 3 * A_WIDTH), _const(w_out.shape), _const(ws.shape), _const((CHUNK, CHUNK)),
                   _const((1, A_WIDTH)), _const((1, A_WIDTH))],
        out_shape=[jax.ShapeDtypeStruct((t_len, 3 * A_WIDTH), BF16), jax.ShapeDtypeStruct(w_out.shape, BF16),
                   jax.ShapeDtypeStruct(ws.shape, F32), jax.ShapeDtypeStruct((CHUNK, CHUNK), F32),
                   wide, wide],
        scratch_shapes=[pltpu.VMEM((tm, A_WIDTH), F32), pltpu.VMEM(w_out.shape, F32)],
        compiler_params=_params(),
    )(dr1, u, vh, z, y, rv, w_out, lng, lnb, ws, bsp_t, after)


def _layer_a_bwd_dx(dr1, dp, w_in, after, updates=()):
    t_len = dr1.shape[0]
    tm = TM_MM
    n_steps = t_len // tm
    n_upd = len(updates)

    def body(dr_ref, dp_ref, win_ref, after_ref, *refs):
        upd_in, dx_ref, upd_out = refs[:4 * n_upd], refs[4 * n_upd], refs[4 * n_upd + 1:]
        dx_ref[...] = ALPHA * dr_ref[...] + _dot_nt(dp_ref[...], win_ref[...])
        for k in range(n_upd):
            w_ref, g_ref, m_ref, v_ref = upd_in[4 * k:4 * k + 4]
            g_out, d_ref, nm_ref, nv_ref = upd_out[4 * k:4 * k + 4]
            g_out[...] = g_ref[...]
            _adamw_update(w_ref, g_ref, m_ref, v_ref, d_ref, nm_ref, nv_ref)

    upd_specs, upd_shapes, upd_args = [], [], []
    for w, g, m, v in updates:
        rows, cols = w.shape
        upd_specs.append(pl.BlockSpec((rows // n_steps, cols), lambda i: (i, 0)))
        upd_shapes.append(jax.ShapeDtypeStruct((rows, cols), F32))
        upd_args += [w, g, m, v]
    return pl.pallas_call(
        body, name="layer_a_bwd_dx", grid=(n_steps,),
        in_specs=[_rows(tm, D_MODEL), _rows(tm, 3 * A_WIDTH), _resident(w_in.shape), _const(after.shape)]
        + [s for s in upd_specs for _ in range(4)],
        out_specs=[_rows(tm, D_MODEL)] + [s for s in upd_specs for _ in range(4)],
        out_shape=[jax.ShapeDtypeStruct((t_len, D_MODEL), F32)] + [s for s in upd_shapes for _ in range(4)],
        compiler_params=_params(),
    )(dr1, dp, w_in, after, *upd_args)


def _layer_a_bwd_win(xt, dp, after):
    t_len = xt.shape[1]
    tm = TM_WIN
    n_steps = t_len // tm
    shard_cols = 3 * A_WIDTH // N_CHIPS
    half_rows = D_MODEL // 2

    def body(xt_ref, dp_ref, after_ref, gw_ref, acc):
        i = pl.program_id(1)

        @pl.when(i == 0)
        def _():
            acc[...] = jnp.zeros_like(acc)

        acc[...] += _dot(xt_ref[...], dp_ref[...])

        @pl.when(i == n_steps - 1)
        def _():
            for c in range(2):
                gw_ref[0, c] = acc[c * half_rows:(c + 1) * half_rows, :].astype(BF16)

    return pl.pallas_call(
        body, name="layer_a_bwd_win", grid=(N_CHIPS, n_steps),
        in_specs=[pl.BlockSpec((D_MODEL, tm), lambda j, i: (0, i)),
                  pl.BlockSpec((tm, shard_cols), lambda j, i: (i, j)), _const(after.shape)],
        out_specs=pl.BlockSpec((1, 2, half_rows, shard_cols), lambda j, i: (j, 0, 0, 0)),
        out_shape=jax.ShapeDtypeStruct((N_CHIPS, 2, half_rows, shard_cols), BF16),
        scratch_shapes=[pltpu.VMEM((D_MODEL, shard_cols), F32)],
        compiler_params=_params(("arbitrary", "arbitrary")),
    )(xt, dp, after)


def _bucket_onehot():
    dist = jnp.arange(CHUNK, dtype=jnp.int32)[None, :]
    max_exact = REL_BUCKETS // 2
    df = jnp.maximum(dist, 1).astype(F32)
    large = max_exact + (jnp.log(df / max_exact) / math.log(CHUNK / max_exact)
                         * (REL_BUCKETS - max_exact)).astype(jnp.int32)
    bucket = jnp.where(dist < max_exact, dist, jnp.minimum(large, REL_BUCKETS - 1))
    onehot = bucket == jnp.arange(REL_BUCKETS, dtype=jnp.int32)[:, None]
    return onehot.astype(F32)


def _bias_expand(rel_t, onehot):
    def body(rel_ref, oh_ref, out_ref):
        by_distance = jnp.dot(rel_ref[...], oh_ref[...], preferred_element_type=F32,
                              precision=lax.Precision.HIGHEST)
        for h in range(N_HEADS):
            rows = jnp.broadcast_to(by_distance[h:h + 1, :], (2 * CHUNK, CHUNK))
            out_ref[h] = pltpu.roll(rows, 0, 1, stride=1, stride_axis=0)

    return pl.pallas_call(
        body, name="bias_expand",
        out_shape=jax.ShapeDtypeStruct((N_HEADS, 2 * CHUNK, CHUNK), F32),
    )(rel_t, onehot)


def _bias_reduce(oh_ref, db_ref):
    sublane = lax.broadcasted_iota(jnp.int32, (_SUBLANES, CHUNK), 0)
    rows = []
    for h in range(N_HEADS):
        part = db_ref[h, 0:_SUBLANES, :]
        for a in range(1, 2 * CHUNK // _SUBLANES):
            tile = db_ref[h, a * _SUBLANES:(a + 1) * _SUBLANES, :]
            back = (-a * _SUBLANES) % CHUNK
            part += pltpu.roll(tile, back, 1) if back else tile
        total = jnp.where(sublane == 0, part, 0.0)
        for s in range(1, _SUBLANES):
            total += jnp.where(sublane == s, pltpu.roll(part, CHUNK - s, 1), 0.0)
        rows.append(jnp.sum(total, axis=0, keepdims=True))
    by_distance = jnp.concatenate(rows, axis=0)
    return lax.dot_general(by_distance, oh_ref[...], (((1,), (1,)), ((), ())),
                           preferred_element_type=F32, precision=lax.Precision.HIGHEST)


_SMALL_SHAPES = dict(w_spatial=(A_GROUPS, CHUNK, CHUNK), b_spatial=(A_GROUPS, CHUNK), attn_sinks=(1, N_HEADS),
                     rel_bias=(N_HEADS, REL_BUCKETS), post_ln_g=(2, D_MODEL), post_ln_b=(2, D_MODEL),
                     sgu_ln_g=(1, A_WIDTH), sgu_ln_b=(1, A_WIDTH), loss=(1, 1))
_SMALL_ORDER = tuple(_SMALL_SHAPES)


def _small_rows(name):
    shape = _SMALL_SHAPES[name]
    rows = math.prod(shape[:-1]) if shape[-1] < _LANES else math.prod(shape) // _LANES
    return -(-rows // _SUBLANES) * _SUBLANES


def _small_offset(name):
    return sum(_small_rows(n) for n in _SMALL_ORDER[:_SMALL_ORDER.index(name)])


def _pack_small(dws, dbsp, dsink, dbias, onehot, post_g, post_b, dgs, dbs, loss_vec):
    def body(dws_ref, dbsp_ref, dsink_ref, db_ref, oh_ref, g1_ref, g2_ref, b1_ref, b2_ref, dgs_ref, dbs_ref,
             loss_ref, out_ref):
        out_ref[...] = jnp.zeros_like(out_ref)

        def put_flat(name, refs):
            row = _small_offset(name)
            for ref in refs:
                for k in range(ref.shape[1] // _LANES):
                    out_ref[row:row + 1, :] = ref[:, k * _LANES:(k + 1) * _LANES]
                    row += 1

        row = _small_offset("w_spatial")
        for g in range(A_GROUPS):
            out_ref[row + g * CHUNK:row + (g + 1) * CHUNK, :] = dws_ref[g]
        row = _small_offset("b_spatial")
        out_ref[row:row + A_GROUPS, :] = dbsp_ref[...].T[0:A_GROUPS, :]
        lane = lax.broadcasted_iota(jnp.int32, (1, _LANES), 1)
        sinks = jnp.zeros((1, _LANES), F32)
        for h in range(N_HEADS):
            per_query = dsink_ref[h // GROUP:h // GROUP + 1, (h % GROUP) * CHUNK:(h % GROUP + 1) * CHUNK]
            sinks = jnp.where(lane == h, jnp.sum(per_query, axis=1, keepdims=True), sinks)
        row = _small_offset("attn_sinks")
        out_ref[row:row + 1, :] = sinks
        row = _small_offset("rel_bias")
        out_ref[row:row + N_HEADS, 0:REL_BUCKETS] = _bias_reduce(oh_ref, db_ref)
        put_flat("post_ln_g", [g1_ref, g2_ref])
        put_flat("post_ln_b", [b1_ref, b2_ref])
        put_flat("sgu_ln_g", [dgs_ref])
        put_flat("sgu_ln_b", [dbs_ref])
        row = _small_offset("loss")
        out_ref[row:row + 1, 0:1] = (0.5 / D_MODEL) * jnp.sum(loss_ref[...], axis=1, keepdims=True)

    total_rows = sum(_small_rows(n) for n in _SMALL_ORDER)
    return pl.pallas_call(
        body, name="pack_small",
        out_shape=jax.ShapeDtypeStruct((total_rows, _LANES), F32),
    )(dws, dbsp, dsink, dbias, onehot, *post_g, *post_b, dgs, dbs, loss_vec)


def _place():
    return lax.axis_index("x"), lax.axis_index("y"), lax.axis_index("c")


RELAY_PIECES = 4


def _shard_window(full_ref, shard_shape, col_sharded, s, half, piece=None):
    rows, cols = shard_shape
    if half is None:
        start, size = 0, rows
    elif piece is None:
        start, size = half * (rows // 2), rows // 2
    else:
        size = rows // 2 // RELAY_PIECES
        start = (half * RELAY_PIECES + piece) * size
    if col_sharded:
        return full_ref.at[pl.ds(start, size), pl.ds(s * cols, cols)]
    return full_ref.at[pl.ds(s * rows + start, size), :]


def _other_chips(x, y):
    return [(1 - x, y), (x, 1 - y), (1 - x, 1 - y)]


def _gather_weights(shards, col_sharded, fetch, ln_shard, tokens=None):
    n_w = len(shards)
    fetched = [w for w in range(n_w) if fetch[w]]
    full_shapes = []
    for w, cs in zip(shards, col_sharded):
        r, c = w.shape
        full_shapes.append((r, c * N_CHIPS) if cs else (r * N_CHIPS, c))

    n_tok = 0 if tokens is None else 1
    tok_tile = 2 * TM_MM
    n_tiles = 0 if tokens is None else tokens.shape[0] // tok_tile

    def body(*refs):
        refs = list(refs)
        in_refs = [refs.pop(0) for _ in range(n_w)]
        ln_ref = refs.pop(0)
        tok_ref = refs.pop(0) if n_tok else None
        full_refs = [refs.pop(0) for _ in range(n_w)]
        ln_full = refs.pop(0)
        prod_ref = refs.pop(0) if n_tok else None
        raw = [refs.pop(0) for _ in range(n_w)]
        stage = [refs.pop(0) for _ in range(n_w)]
        send_sems, recv_sems, load_sems, local_sems, ln_send, ln_recv = refs[:6]
        tok_buf, prod_buf, tok_sems, prod_sems = refs[6:] if n_tok else (None,) * 4
        x, y, c = _place()
        s_me = 2 * x + y
        chips = _other_chips(x, y)
        pieces = range(RELAY_PIECES)

        def shard_window(w, s, half, piece=None):
            return _shard_window(full_refs[w], shards[w].shape, col_sharded[w], s, half, piece)

        def piece_rows(w, half, piece):
            rows = shards[w].shape[0] // 2 // RELAY_PIECES
            return pl.ds(pl.multiple_of((half * RELAY_PIECES + piece) * rows, rows), rows)

        def ici_copy(w, k, sender_shard, piece):
            idx = (w * 3 + k) * RELAY_PIECES + piece
            return pltpu.make_async_remote_copy(
                src_ref=stage[w].at[piece_rows(w, c, piece), :], dst_ref=shard_window(w, sender_shard, c, piece),
                send_sem=send_sems.at[idx], recv_sem=recv_sems.at[idx],
                device_id=(*chips[k], c), device_id_type=MESH)

        def d2d_copy(w, k, half, piece):
            s_k = 2 * chips[k][0] + chips[k][1]
            win = shard_window(w, s_k, half, piece)
            idx = (3 * n_w + w * 3 + k) * RELAY_PIECES + piece
            return pltpu.make_async_remote_copy(
                src_ref=win, dst_ref=win, send_sem=send_sems.at[idx], recv_sem=recv_sems.at[idx],
                device_id=(x, y, 1 - c), device_id_type=MESH)

        def ln_copy(k, slot):
            return pltpu.make_async_remote_copy(
                src_ref=ln_ref, dst_ref=ln_full.at[slot], send_sem=ln_send.at[k], recv_sem=ln_recv.at[k],
                device_id=(*chips[k], c), device_id_type=MESH)

        loads = []

        def load(w, rows):
            window = (rows, slice(None)) if rows is not None else (slice(None), slice(None))
            cp = pltpu.make_async_copy(in_refs[w].at[window], raw[w].at[window], load_sems.at[len(loads)])
            cp.start()
            loads.append((cp, w, window))

        for half in (c, 1 - c):
            for w in fetched:
                for q in pieces:
                    load(w, piece_rows(w, half, q))
        for w in range(n_w):
            if not fetch[w]:
                load(w, None)

        def to_bf16(k):
            cp, w, window = loads[k]
            cp.wait()
            stage[w][window] = raw[w][window].astype(BF16)

        ln_full[s_me] = ln_ref[...]
        def shard_of(k):
            return 2 * chips[k][0] + chips[k][1]

        relay_from = jnp.where(c == 0, shard_of(0), shard_of(1))
        relay_to = (jnp.where(c == 0, x, 1 - x), jnp.where(c == 0, 1 - y, y), c)

        def relay_copy(w, sender_shard, piece):
            win = shard_window(w, sender_shard, c, piece)
            idx = (w * 3 + 2) * RELAY_PIECES + piece
            return pltpu.make_async_remote_copy(
                src_ref=win, dst_ref=win, send_sem=send_sems.at[idx], recv_sem=recv_sems.at[idx],
                device_id=relay_to, device_id_type=MESH)

        first = [ln_copy(k, s_me) for k in range(3)]
        for cp in first:
            cp.start()
        n_sent = 0
        for w in fetched:
            for q in pieces:
                to_bf16(n_sent)
                n_sent += 1
                for k in range(2):
                    cp = ici_copy(w, k, s_me, q)
                    cp.start()
                    first.append(cp)
        for k in range(n_sent, len(loads)):
            to_bf16(k)
        own = [pltpu.make_async_copy(stage[w], shard_window(w, s_me, None), local_sems.at[w]) for w in range(n_w)]
        for cp in own:
            cp.start()

        def tok_copy(t):
            return pltpu.make_async_copy(tok_ref.at[pl.ds(t * tok_tile, tok_tile), :], tok_buf.at[t % 2],
                                         tok_sems.at[t % 2])

        def prod_copy(t):
            return pltpu.make_async_copy(prod_buf.at[t % 2], prod_ref.at[pl.ds(t * tok_tile, tok_tile), :],
                                         prod_sems.at[t % 2])

        def product_tiles(tiles):
            for t in tiles:
                if t + 1 < n_tiles:
                    tok_copy(t + 1).start()
                tok_copy(t).wait()
                if t >= 2:
                    prod_copy(t - 2).wait()
                prod_buf[t % 2] = _dot(tok_buf[t % 2].astype(BF16), stage[0][...])
                prod_copy(t).start()

        if n_tiles:
            tok_copy(0).start()
        share = [4, 3, 3, 3, 3, 0, 0, 0]
        assert len(share) == 2 * RELAY_PIECES
        bounds = [sum(share[:k]) * n_tiles // sum(share) for k in range(len(share) + 1)]
        passed = []
        for w in fetched:
            for q in pieces:
                if w == fetched[0]:
                    product_tiles(range(bounds[q], bounds[q + 1]))
                for k in range(2):
                    ici_copy(w, k, shard_of(k), q).wait_recv()
                relay = relay_copy(w, relay_from, q)
                relay.start()
                passed.append(relay)
                for k in range(2):
                    fwd = d2d_copy(w, k, c, q)
                    fwd.start()
                    passed.append(fwd)
        for w in fetched:
            for q in pieces:
                if w == fetched[0]:
                    product_tiles(range(bounds[RELAY_PIECES + q], bounds[RELAY_PIECES + q + 1]))
                relay_copy(w, shard_of(2), q).wait_recv()
                fwd = d2d_copy(w, 2, c, q)
                fwd.start()
                passed.append(fwd)
        for w in fetched:
            for k in range(3):
                for q in pieces:
                    d2d_copy(w, k, 1 - c, q).wait_recv()
        for k in range(3):
            ln_copy(k, 2 * chips[k][0] + chips[k][1]).wait_recv()
        for cp in first + passed:
            cp.wait_send()
        for cp in own:
            cp.wait()
        for t in range(max(n_tiles - 2, 0), n_tiles):
            prod_copy(t).wait()

    vmem = pl.BlockSpec(memory_space=pltpu.VMEM)
    hbm = pl.BlockSpec(memory_space=pl.ANY)
    prod_cols = shards[0].shape[1]
    tok_args = [] if tokens is None else [tokens]
    tok_out = [] if tokens is None else [jax.ShapeDtypeStruct((tokens.shape[0], prod_cols), F32)]
    tok_scratch = [] if tokens is None else [
        pltpu.VMEM((2, tok_tile, tokens.shape[1]), F32), pltpu.VMEM((2, tok_tile, prod_cols), F32),
        pltpu.SemaphoreType.DMA((2,)), pltpu.SemaphoreType.DMA((2,))]
    return pl.pallas_call(
        body, name="gather_weights",
        in_specs=[hbm] * n_w + [vmem] + [hbm] * n_tok,
        out_specs=[hbm] * n_w + [vmem] + [hbm] * n_tok,
        out_shape=[jax.ShapeDtypeStruct(s, BF16) for s in full_shapes]
        + [jax.ShapeDtypeStruct((N_CHIPS,) + ln_shard.shape, F32)] + tok_out,
        scratch_shapes=[pltpu.VMEM(w.shape, F32) for w in shards] + [pltpu.VMEM(w.shape, BF16) for w in shards]
        + [pltpu.SemaphoreType.DMA((6 * RELAY_PIECES * n_w,)), pltpu.SemaphoreType.DMA((6 * RELAY_PIECES * n_w,)),
           pltpu.SemaphoreType.DMA((2 * RELAY_PIECES * len(fetched) + n_w - len(fetched),)),
           pltpu.SemaphoreType.DMA((n_w,)), pltpu.SemaphoreType.DMA((3,)), pltpu.SemaphoreType.DMA((3,))]
        + tok_scratch,
        compiler_params=pltpu.CompilerParams(vmem_limit_bytes=VMEM_LIMIT),
    )(*shards, ln_shard, *tok_args)


def _fetch_copy(full_ref, shard_shape, col_sharded, sender_shard, send_sems, recv_sems, idx, chip, c):
    win = _shard_window(full_ref, shard_shape, col_sharded, sender_shard, None)
    return pltpu.make_async_remote_copy(src_ref=win, dst_ref=win, send_sem=send_sems.at[idx],
                                        recv_sem=recv_sems.at[idx], device_id=(*chip, c), device_id_type=MESH)


def _fetch_start(fulls, shard_shapes, col_sharded):
    n = len(fulls)

    def body(*refs):
        full = refs[:n]
        send_sems, recv_sems = refs[n], refs[n + 1]
        token = refs[-1]
        x, y, c = _place()
        for w in range(n):
            for k, chip in enumerate(_other_chips(x, y)):
                _fetch_copy(full[w], shard_shapes[w], col_sharded[w], 2 * x + y, send_sems, recv_sems, w * 3 + k,
                            chip, c).start()
        token[...] = jnp.zeros_like(token)

    outs = pl.pallas_call(
        body, name="fetch_start",
        out_shape=(pltpu.SemaphoreType.DMA((3 * n,)), pltpu.SemaphoreType.DMA((3 * n,)),
                   *[pltpu.HBM(f.shape, f.dtype) for f in fulls], jax.ShapeDtypeStruct((8, 128), F32)),
        in_specs=[_HBM] * n,
        out_specs=(_SEM, _SEM, *([_HBM] * n), pl.BlockSpec(memory_space=pltpu.VMEM)),
        input_output_aliases={i: 2 + i for i in range(n)},
        compiler_params=pltpu.CompilerParams(has_side_effects=pltpu.SideEffectType.DATAFLOW_SIDE_EFFECTING),
    )(*[pltpu.with_memory_space_constraint(f, pltpu.HBM) for f in fulls])
    return dict(send=outs[0], recv=outs[1], full=list(outs[2:2 + n])), outs[-1]


def _fetch_wait(group, shard_shapes, col_sharded, after):
    n = len(group["full"])

    def body(*refs):
        full = refs[:n]
        send_sems, recv_sems = refs[n], refs[n + 1]
        x, y, c = _place()
        for w in range(n):
            for k, chip in enumerate(_other_chips(x, y)):
                _fetch_copy(full[w], shard_shapes[w], col_sharded[w], 2 * x + y, send_sems, recv_sems, w * 3 + k,
                            chip, c).wait_send()
                _fetch_copy(full[w], shard_shapes[w], col_sharded[w], 2 * chip[0] + chip[1], send_sems, recv_sems,
                            w * 3 + k, chip, c).wait_recv()

    outs = pl.pallas_call(
        body, name="fetch_wait", out_shape=tuple(pltpu.HBM(f.shape, f.dtype) for f in group["full"]),
        in_specs=[_HBM] * n + [_SEM, _SEM, pl.BlockSpec(memory_space=pl.ANY)],
        out_specs=tuple([_HBM] * n), input_output_aliases={i: i for i in range(n)},
        compiler_params=pltpu.CompilerParams(has_side_effects=pltpu.SideEffectType.DATAFLOW_SIDE_EFFECTING),
    )(*group["full"], group["send"], group["recv"], after)
    return list(outs)


_HBM = pl.BlockSpec(memory_space=pltpu.HBM)
_SEM = pl.BlockSpec(memory_space=pltpu.SEMAPHORE)
_N_PEER = N_DEV - 1


def _peer(x, y, c, k):
    return (x + (k >> 2)) % 2, (y + ((k >> 1) & 1)) % 2, (c + (k & 1)) % 2


def _exchange_copy(src_ref, land_ref, sliced, send_sems, recv_sems, idx, x, y, c, k):
    px, py, pc = _peer(x, y, c, k)
    src = src_ref.at[4 * px + 2 * py + pc] if sliced else src_ref
    return pltpu.make_async_remote_copy(
        src_ref=src, dst_ref=land_ref.at[4 * x + 2 * y + c],
        send_sem=send_sems.at[idx], recv_sem=recv_sems.at[idx], device_id=(px, py, pc), device_id_type=MESH)


def _exchange_start(tag, arrays, sliced):
    n = len(arrays)
    lands = [lax.empty(a.shape if s else (N_DEV,) + a.shape, a.dtype) for a, s in zip(arrays, sliced)]

    def body(*refs):
        src, land = refs[:n], refs[n:2 * n]
        send_sems, recv_sems = refs[2 * n], refs[2 * n + 1]
        token = refs[-1]
        x, y, c = _place()
        for w in range(n):
            for k in range(1, N_DEV):
                _exchange_copy(src[w], land[w], sliced[w], send_sems, recv_sems, w * _N_PEER + k - 1, x, y, c, k).start()
        token[...] = jnp.zeros_like(token)

    outs = pl.pallas_call(
        body, name="exchange_start_" + tag,
        out_shape=(pltpu.SemaphoreType.DMA((n * _N_PEER,)), pltpu.SemaphoreType.DMA((n * _N_PEER,)),
                   *[pltpu.HBM(a.shape, a.dtype) for a in arrays], *[pltpu.HBM(l.shape, l.dtype) for l in lands],
                   jax.ShapeDtypeStruct((8, 128), F32)),
        in_specs=[_HBM] * (2 * n),
        out_specs=(_SEM, _SEM, *([_HBM] * (2 * n)), pl.BlockSpec(memory_space=pltpu.VMEM)),
        input_output_aliases={i: 2 + i for i in range(2 * n)},
        compiler_params=pltpu.CompilerParams(has_side_effects=pltpu.SideEffectType.DATAFLOW_SIDE_EFFECTING),
    )(*[pltpu.with_memory_space_constraint(a, pltpu.HBM) for a in arrays],
      *[pltpu.with_memory_space_constraint(l, pltpu.HBM) for l in lands])
    return dict(send=outs[0], recv=outs[1], src=list(outs[2:2 + n]), land=list(outs[2 + n:2 + 2 * n]),
                sliced=list(sliced)), outs[-1]


def _exchange_wait(tag, groups, after):
    counts = [len(g["src"]) for g in groups]
    total = sum(counts)

    def body(*refs):
        pos = 0
        x, y, c = _place()
        for g, n in zip(groups, counts):
            src, land = refs[pos:pos + n], refs[pos + n:pos + 2 * n]
            send_sems, recv_sems = refs[pos + 2 * n], refs[pos + 2 * n + 1]
            pos += 2 * n + 2
            for w in range(n):
                for k in range(1, N_DEV):
                    cp = _exchange_copy(src[w], land[w], g["sliced"][w], send_sems, recv_sems,
                                        w * _N_PEER + k - 1, x, y, c, k)
                    cp.wait_send()
                    cp.wait_recv()

    operands, in_specs, aliases, out_shape = [], [], {}, []
    for g in groups:
        for a in g["src"] + g["land"]:
            aliases[len(operands)] = len(out_shape)
            out_shape.append(pltpu.HBM(a.shape, a.dtype))
            operands.append(a)
            in_specs.append(_HBM)
        operands += [g["send"], g["recv"]]
        in_specs += [_SEM, _SEM]
    operands.append(after)
    in_specs.append(pl.BlockSpec(memory_space=pl.ANY))
    outs = pl.pallas_call(
        body, name="exchange_wait_" + tag, out_shape=tuple(out_shape), in_specs=in_specs,
        out_specs=tuple([_HBM] * (2 * total)), input_output_aliases=aliases,
        compiler_params=pltpu.CompilerParams(has_side_effects=pltpu.SideEffectType.DATAFLOW_SIDE_EFFECTING),
    )(*operands)
    srcs, lands, pos = [], [], 0
    for n in counts:
        srcs += list(outs[pos:pos + n])
        lands += list(outs[pos + n:pos + 2 * n])
        pos += 2 * n
    return srcs, lands


def _sum_and_swap(tag, pieces, lands, small=None, small_land=None):
    n_w = len(pieces)
    n_small = 0 if small is None else 1

    def body(*refs):
        g_refs, land_refs = refs[:n_w], refs[n_w:2 * n_w]
        pos = 2 * n_w + 2 * n_small
        out_refs = refs[pos:pos + n_w]
        pos += n_w + n_small
        bufs = refs[pos:pos + n_w]
        load_sems, swap_send, swap_recv = refs[pos + n_w + 2 * n_small:]
        x, y, c = _place()
        me = 4 * x + 2 * y + c

        def slot(k):
            px, py, pc = _peer(x, y, c, k)
            return 4 * px + 2 * py + pc

        def swap_copy(w, half):
            rows = pieces[w].shape[1]
            win = out_refs[w].at[pl.ds(pl.multiple_of(half * rows, rows), rows), :]
            return pltpu.make_async_remote_copy(
                src_ref=win, dst_ref=win, send_sem=swap_send.at[w], recv_sem=swap_recv.at[w],
                device_id=(x, y, 1 - c), device_id_type=MESH)

        loads = []
        for w in range(n_w):
            per_w = [pltpu.make_async_copy(g_refs[w].at[me], bufs[w].at[me], load_sems.at[w * N_DEV])]
            per_w += [pltpu.make_async_copy(land_refs[w].at[slot(k)], bufs[w].at[slot(k)], load_sems.at[w * N_DEV + k])
                      for k in range(1, N_DEV)]
            loads.append(per_w)
        small_loads = []
        if n_small:
            small_ref, small_land_ref = refs[2 * n_w], refs[2 * n_w + 1]
            small_out = refs[2 * n_w + 2 + n_w]
            small_buf, small_sems = refs[pos + n_w], refs[pos + n_w + 1]
            small_loads = [pltpu.make_async_copy(small_land_ref.at[slot(k)], small_buf.at[slot(k)],
                                                 small_sems.at[k - 1]) for k in range(1, N_DEV)]
        for cp in [cp for per_w in loads for cp in per_w] + small_loads:
            cp.start()
        if n_small:
            small_buf[me] = small_ref[...]
        swaps = []
        for w in range(n_w):
            for cp in loads[w]:
                cp.wait()
            rows = pieces[w].shape[1]
            total = bufs[w][0].astype(F32)
            for p in range(1, N_DEV):
                total += bufs[w][p].astype(F32)
            out_refs[w][pl.ds(pl.multiple_of(c * rows, rows), rows), :] = total
            sw = swap_copy(w, c)
            sw.start()
            swaps.append(sw)
        if n_small:
            for cp in small_loads:
                cp.wait()
            total = small_buf[0]
            for p in range(1, N_DEV):
                total += small_buf[p]
            small_out[...] = total
        for w in range(n_w):
            swap_copy(w, 1 - c).wait_recv()
        for sw in swaps:
            sw.wait_send()

    vmem = pl.BlockSpec(memory_space=pltpu.VMEM)
    hbm = pl.BlockSpec(memory_space=pl.ANY)
    small_args = [small, small_land] if n_small else []
    small_shapes = [jax.ShapeDtypeStruct(small.shape, F32)] if n_small else []
    small_scratch = ([pltpu.VMEM((N_DEV,) + small.shape, F32), pltpu.SemaphoreType.DMA((_N_PEER,))]
                     if n_small else [])
    return pl.pallas_call(
        body, name="sum_and_swap_" + tag,
        in_specs=[hbm] * (2 * n_w) + [vmem, hbm] * n_small,
        out_specs=[vmem] * (n_w + n_small),
        out_shape=[jax.ShapeDtypeStruct((2 * p.shape[1], p.shape[2]), F32) for p in pieces] + small_shapes,
        scratch_shapes=[pltpu.VMEM(p.shape, BF16) for p in pieces] + small_scratch
        + [pltpu.SemaphoreType.DMA((n_w * N_DEV,)), pltpu.SemaphoreType.DMA((n_w,)),
           pltpu.SemaphoreType.DMA((n_w,))],
        compiler_params=pltpu.CompilerParams(vmem_limit_bytes=VMEM_LIMIT),
    )(*pieces, *lands, *small_args)


def _adamw_values(w, g_t, m, v):
    c1 = 1.0 - ADAM_B1 ** ADAM_STEP
    c2 = 1.0 - ADAM_B2 ** ADAM_STEP
    nm = ADAM_B1 * m + (1.0 - ADAM_B1) * g_t
    nv = ADAM_B2 * v + (1.0 - ADAM_B2) * (g_t * g_t)
    return -ADAM_LR * ((nm / c1) / (jnp.sqrt(nv / c2) + ADAM_EPS) + ADAM_WD * w), nm, nv


def _adamw_update(w_ref, g_ref, m_ref, v_ref, d_ref, nm_ref, nv_ref):
    d_ref[...], nm_ref[...], nv_ref[...] = _adamw_values(w_ref[...], g_ref[...], m_ref[...], v_ref[...])


def _adamw_small(packed, shard_index, names, weights, moments_m, moments_v):
    n = len(names)
    shapes = [weights[name].shape for name in names]
    flat = [a[name].reshape(-1, a[name].shape[-1]) for name in names for a in (weights, moments_m, moments_v)]

    def body(packed_ref, shard_ref, *refs):
        loss_row = _small_offset("loss")
        refs[-1][...] = packed_ref[loss_row:loss_row + 1, 0:1]
        for k, name in enumerate(names):
            w_ref, m_ref, v_ref = refs[3 * k:3 * k + 3]
            g_ref, d_ref, nm_ref, nv_ref = refs[3 * n + 4 * k:3 * n + 4 * k + 4]
            rows, cols = w_ref.shape
            first = _small_offset(name)
            if cols <= _LANES:
                blocks = [(slice(0, rows), packed_ref[first:first + rows, 0:cols])]
            else:
                per_row = cols // _LANES
                if cols < _SMALL_SHAPES[name][-1]:
                    first = first + shard_ref[0] * per_row
                blocks = [(slice(i, i + 1),
                           jnp.concatenate([packed_ref[pl.ds(first + i * per_row + j, 1), :] for j in range(per_row)],
                                           axis=1)) for i in range(rows)]
            for at, g_t in blocks:
                g_ref[at, :] = g_t
                d_ref[at, :], nm_ref[at, :], nv_ref[at, :] = _adamw_values(w_ref[at, :], g_t, m_ref[at, :],
                                                                           v_ref[at, :])

    vmem = pl.BlockSpec(memory_space=pltpu.VMEM)
    outs = pl.pallas_call(
        body, name="adamw_small",
        in_specs=[vmem, pl.BlockSpec(memory_space=pltpu.SMEM)] + [vmem] * (3 * n),
        out_shape=[jax.ShapeDtypeStruct(flat[3 * k].shape, F32) for k in range(n) for _ in range(4)]
        + [jax.ShapeDtypeStruct((1, 1), F32)],
    )(packed, shard_index.reshape(1).astype(jnp.int32), *flat)
    return [tuple(o.reshape(shapes[k]) for o in outs[4 * k:4 * k + 4]) for k in range(n)], outs[-1].reshape(())


def _adamw(label, w, g, m, v):
    shape = w.shape
    cols = shape[-1]
    rows = w.size // cols
    args = [a.reshape(rows, cols) for a in (w, g, m, v)]

    def body(w_ref, g_ref, m_ref, v_ref, g_out, d_ref, nm_ref, nv_ref):
        g_out[...] = g_ref[...]
        _adamw_update(w_ref, g_ref, m_ref, v_ref, d_ref, nm_ref, nv_ref)

    block_rows = 128 if rows % 128 == 0 and rows > 128 else rows
    spec = pl.BlockSpec((block_rows, cols), lambda i: (i, 0))
    outs = pl.pallas_call(
        body, name="adamw_" + label, grid=(rows // block_rows,),
        in_specs=[spec] * 4, out_specs=[spec] * 4,
        out_shape=[jax.ShapeDtypeStruct((rows, cols), F32)] * 4,
        compiler_params=_params(),
    )(*args)
    return [o.reshape(shape) for o in outs]


def _no_send(tag, arrays, sliced):
    return jnp.zeros((8, 128), F32)


def _local_step(x, tgt, w_in_a, later_weights, first_after, sgu_ln_g, sgu_ln_b, w_spatial, b_spatial,
                attn_sinks, rel_bias, post_ln_g, post_ln_b, send=_no_send, own_product=None):
    bsp_t = b_spatial.T
    g1, b1 = post_ln_g[0:1], post_ln_b[0:1]
    g2, b2 = post_ln_g[1:2], post_ln_b[1:2]
    onehot = _bucket_onehot()
    bias = _bias_expand(rel_bias.T, onehot)
    win = _window_tables()

    if own_product is None:
        xt, u, vh, z, rv, y = _layer_a_fwd(x, w_in_a, None, sgu_ln_g, sgu_ln_b, w_spatial, bsp_t, first_after)
    else:
        chip, p_own = own_product
        xt, u, vh, z, rv, y = lax.switch(
            chip, [functools.partial(_layer_a_fwd, own=own) for own in range(N_CHIPS)],
            x, w_in_a, p_own, sgu_ln_g, sgu_ln_b, w_spatial, bsp_t, first_after)
    w_out_a, w_kv, w_in_b, w_out_b = later_weights(y)
    xh1, rstd1, q, zb, kd, vd = _layer_b_proj(x, y, w_out_a, g1, b1, w_in_b, w_kv)
    o, probs, sink_probs, dr2, loss_vec, dg2, db2 = _layer_b_fwd(q, zb, kd, vd, bias, win, attn_sinks, xh1, g1, b1,
                                                                 w_out_b, g2, b2, tgt)
    dq, dzb, dkd, dvd, carry_k, carry_v, gw_out_b, dsink, dbias = _layer_b_bwd_attn(
        dr2, zb, o, q, kd, vd, probs, sink_probs, w_out_b)
    dr1, dg1, db1, gw_in_b, gw_kv = _layer_b_bwd_proj(xh1, rstd1, g1, b1, dr2, dq, dzb, dkd, dvd, carry_k, carry_v,
                                                      w_in_b, w_kv)
    gw_out_b = gw_out_b.reshape(N_DEV, -1, D_MODEL)
    gw_kv = gw_kv.reshape(N_DEV, -1, 2 * PAIR)
    after = send("b", [gw_out_b, gw_in_b, gw_kv], [True, True, True])
    dp, gw_out_a, dws, dbsp, dgs, dbs = _layer_a_bwd_mix(dr1, u, vh, z, y, rv, w_out_a, sgu_ln_g, sgu_ln_b,
                                                         w_spatial, bsp_t, after)
    gw_out_a = gw_out_a.reshape(N_DEV, -1, D_MODEL)
    small = _pack_small(dws, dbsp, dsink, dbias, onehot, (dg1, dg2), (db1, db2), dgs, dbs, loss_vec)
    after = send("a_out", [gw_out_a, small], [True, False])
    gw_in_a = _layer_a_bwd_win(xt, dp, after).reshape(N_DEV, D_MODEL // 2, -1)
    after = send("a_in", [gw_in_a], [True])
    after, updates = after if isinstance(after, tuple) else (after, ())
    grad_x, *updated = _layer_a_bwd_dx(dr1, dp, w_in_a, after, updates)

    pieces = [gw_in_a, gw_out_a, gw_kv, gw_in_b, gw_out_b]
    return grad_x, pieces, small, updated


def kernel(x, w_in_a, sgu_ln_g, sgu_ln_b, w_spatial, b_spatial, w_out_a, w_kv, w_in_b, attn_sinks, rel_bias, w_out_b, post_ln_g, post_ln_b, loss_target, m_w_in_a, m_sgu_ln_g, m_sgu_ln_b, m_w_spatial, m_b_spatial, m_w_out_a, m_w_kv, m_w_in_b, m_attn_sinks, m_rel_bias, m_w_out_b, m_post_ln_g, m_post_ln_b, v_w_in_a, v_sgu_ln_g, v_sgu_ln_b, v_w_spatial, v_b_spatial, v_w_out_a, v_w_kv, v_w_in_b, v_attn_sinks, v_rel_bias, v_w_out_b, v_post_ln_g, v_post_ln_b):
    weights = dict(w_in_a=w_in_a, sgu_ln_g=sgu_ln_g, sgu_ln_b=sgu_ln_b, w_spatial=w_spatial, b_spatial=b_spatial,
                   w_out_a=w_out_a, w_kv=w_kv, w_in_b=w_in_b, attn_sinks=attn_sinks, rel_bias=rel_bias,
                   w_out_b=w_out_b, post_ln_g=post_ln_g, post_ln_b=post_ln_b)
    moments_m = dict(w_in_a=m_w_in_a, sgu_ln_g=m_sgu_ln_g, sgu_ln_b=m_sgu_ln_b, w_spatial=m_w_spatial,
                     b_spatial=m_b_spatial, w_out_a=m_w_out_a, w_kv=m_w_kv, w_in_b=m_w_in_b,
                     attn_sinks=m_attn_sinks, rel_bias=m_rel_bias, w_out_b=m_w_out_b, post_ln_g=m_post_ln_g,
                     post_ln_b=m_post_ln_b)
    moments_v = dict(w_in_a=v_w_in_a, sgu_ln_g=v_sgu_ln_g, sgu_ln_b=v_sgu_ln_b, w_spatial=v_w_spatial,
                     b_spatial=v_b_spatial, w_out_a=v_w_out_a, w_kv=v_w_kv, w_in_b=v_w_in_b,
                     attn_sinks=v_attn_sinks, rel_bias=v_rel_bias, w_out_b=v_w_out_b, post_ln_g=v_post_ln_g,
                     post_ln_b=v_post_ln_b)
    order = ("w_in_a", "sgu_ln_g", "sgu_ln_b", "w_spatial", "b_spatial", "w_out_a", "w_kv", "w_in_b", "attn_sinks",
             "rel_bias", "w_out_b", "post_ln_g", "post_ln_b")

    shard_index = 2 * lax.axis_index("x") + lax.axis_index("y")
    ln_shard = jnp.concatenate([sgu_ln_g, sgu_ln_b], axis=0)
    shards = [w_in_a[0], w_out_a[0], w_kv, w_in_b[0], w_out_b[0]]
    col_sharded = [True, False, False, True, False]
    full_in_a, *later, ln_full, p_own = _gather_weights(shards, col_sharded, [True, False, False, False, False],
                                                        ln_shard, tokens=x[0])
    ln_full = jnp.transpose(ln_full, (1, 0, 2)).reshape(2, A_WIDTH)
    later_shapes = [s.shape for s in shards[1:]]
    fetch_group, fetch_token = _fetch_start(later, later_shapes, col_sharded[1:])

    def later_weights(y):
        return _fetch_wait(fetch_group, later_shapes, col_sharded[1:], y)

    groups, grads, deltas, new_m, new_v, scalars = {}, {}, {}, {}, {}, {}
    early = ("w_out_b", "w_in_b", "w_kv", "w_out_a")

    def two_dim(a):
        return a.reshape(-1, a.shape[-1])

    def send(tag, arrays, sliced):
        groups[tag], token = _exchange_start(tag, arrays, sliced)
        if tag != "a_in":
            return token
        srcs, lands = _exchange_wait("early", [groups["b"], groups["a_out"]], token)
        *reduced, packed_sum = _sum_and_swap("early", srcs[:4], lands[:4], srcs[4], lands[4])
        updates = [(two_dim(weights[n]), g, two_dim(moments_m[n]), two_dim(moments_v[n]))
                   for n, g in zip(early, reduced)]
        small_names = ("sgu_ln_g", "sgu_ln_b", "w_spatial", "b_spatial", "attn_sinks", "rel_bias", "post_ln_g",
                       "post_ln_b")
        def as_packed(arrays):
            return {n: (arrays[n].T if n == "rel_bias" else arrays[n]) for n in small_names}

        small_updates, scalars["loss"] = _adamw_small(packed_sum, shard_index, small_names, as_packed(weights),
                                                      as_packed(moments_m), as_packed(moments_v))
        for name, results in zip(small_names, small_updates):
            grads[name], deltas[name], new_m[name], new_v[name] = [
                a.T if name == "rel_bias" else a for a in results]
        return new_m["b_spatial"].reshape(A_GROUPS, CHUNK), updates

    grad_x, _, _, updated = _local_step(
        x[0], loss_target[0], full_in_a, later_weights, fetch_token, ln_full[0:1], ln_full[1:2], w_spatial[0],
        b_spatial[0], attn_sinks, rel_bias, post_ln_g, post_ln_b, send=send, own_product=(shard_index, p_own))
    for k, name in enumerate(early):
        grads[name], deltas[name], new_m[name], new_v[name] = [
            a.reshape(weights[name].shape) for a in updated[4 * k:4 * k + 4]]

    srcs, lands = _exchange_wait("late", [groups["a_in"]], grad_x)
    (g_in_a,) = _sum_and_swap("late", srcs, lands)
    grads["w_in_a"], deltas["w_in_a"], new_m["w_in_a"], new_v["w_in_a"] = _adamw(
        "w_in_a", w_in_a, g_in_a.reshape(w_in_a.shape), m_w_in_a, v_w_in_a)
    return (scalars["loss"], grad_x[None], *[grads[n] for n in order], *[deltas[n] for n in order],
            *[new_m[n] for n in order], *[new_v[n] for n in order])
```
